```python
import math
import jax, jax.numpy as jnp
from jax import lax
import numpy as np

D_MODEL = 1024
BATCH = 8
SEQ = 4096
DEPTH = 1

CONV_WIDTH = D_MODEL // 2
CONV_KERNEL = 31
SSM_WIDTH = D_MODEL // 2
SSM_GROUP = 16
SSM_GROUPS = SSM_WIDTH // SSM_GROUP
SSM_STATE = 64
FFN_HIDDEN = ((8 * D_MODEL + 3 * 256 - 1) // (3 * 256)) * 256
IN_COLS = 2 * CONV_WIDTH + SSM_WIDTH + 2 * D_MODEL
N_MOD = 6
EPS = 1e-6
DT_MIN = 1e-3
DT_MAX = 1e-1

kernel_name = "hybrid_conv_s5_gated_block"


def rmsnorm(x, g):
    x32 = x.astype(jnp.float32)
    y = x32 * lax.rsqrt(jnp.mean(x32 * x32, axis=-1, keepdims=True) + EPS)
    return (y * g.astype(jnp.float32)).astype(x.dtype)


def layernorm(x, g, b):
    x32 = x.astype(jnp.float32)
    mu = jnp.mean(x32, axis=-1, keepdims=True)
    var = jnp.mean(jnp.square(x32 - mu), axis=-1, keepdims=True)
    y = (x32 - mu) * lax.rsqrt(var + EPS)
    return (y * g.astype(jnp.float32) + b.astype(jnp.float32)).astype(x.dtype)


def conformer_conv(u_glu, conv_w, conv_b, ln_g, ln_b, w_proj):
    a, gate = jnp.split(u_glu, 2, axis=-1)
    u = a * jax.nn.sigmoid(gate)
    y = lax.conv_general_dilated(
        u, conv_w[:, None, :].astype(u.dtype), window_strides=(1,),
        padding=[(CONV_KERNEL - 1, 0)],
        dimension_numbers=("NWC", "WIO", "NWC"),
        feature_group_count=CONV_WIDTH) + conv_b
    y = jax.nn.silu(layernorm(y, ln_g, ln_b))
    return y @ w_proj


def _scan_combine(e1, e2):
    a1r, a1i, b1r, b1i = e1
    a2r, a2i, b2r, b2i = e2
    ar = a2r * a1r - a2i * a1i
    ai = a2r * a1i + a2i * a1r
    br = a2r * b1r - a2i * b1i + b2r
    bi = a2r * b1i + a2i * b1r + b2i
    return (ar, ai, br, bi)


def s5_mixer(u, a_re, a_im, b_re, b_im, c_re, c_im, d, log_dt, w_glu):
    bsz, seq, _ = u.shape
    f32 = jnp.float32
    u32 = u.astype(f32).reshape(bsz, seq, SSM_GROUPS, SSM_GROUP)
    ar, ai = a_re.astype(f32), a_im.astype(f32)
    dt = jnp.exp(log_dt.astype(f32))[:, None]
    mag = jnp.exp(dt * ar)
    e_re, e_im = mag * jnp.cos(dt * ai), mag * jnp.sin(dt * ai)
    n_re, n_im = e_re - 1.0, e_im
    den = ar * ar + ai * ai
    q_re = (n_re * ar + n_im * ai) / den
    q_im = (n_im * ar - n_re * ai) / den
    br32, bi32 = b_re.astype(f32), b_im.astype(f32)
    bb_re = q_re[..., None] * br32 - q_im[..., None] * bi32
    bb_im = q_re[..., None] * bi32 + q_im[..., None] * br32
    bu_re = jnp.einsum("bsgh,gph->bsgp", u32, bb_re)
    bu_im = jnp.einsum("bsgh,gph->bsgp", u32, bb_im)
    abar_re = jnp.broadcast_to(e_re, bu_re.shape)
    abar_im = jnp.broadcast_to(e_im, bu_re.shape)
    _, _, x_re, x_im = lax.associative_scan(
        _scan_combine, (abar_re, abar_im, bu_re, bu_im), axis=1)
    y = (jnp.einsum("bsgp,ghp->bsgh", x_re, c_re.astype(f32))
         - jnp.einsum("bsgp,ghp->bsgh", x_im, c_im.astype(f32)))
    y = y.reshape(bsz, seq, SSM_WIDTH) + d.astype(f32) * u.astype(f32)
    y = jax.nn.gelu(y).astype(u.dtype)
    za, zb = jnp.split(y @ w_glu, 2, axis=-1)
    return za * jax.nn.sigmoid(zb)


def _fwd_setup_inputs(seed: int = 0) -> dict:
    key = jax.random.key(seed)
    ks = jax.random.split(key, 32)
    f32 = jnp.float32
    L, D, G, P, H = DEPTH, D_MODEL, SSM_GROUPS, SSM_STATE, SSM_GROUP

    def nrm(k, shape, fan_in):
        return jax.random.normal(k, shape, f32) * fan_in ** -0.5

    def gain(k, shape):
        return 1.0 + 0.05 * jax.random.normal(k, shape, f32)

    n_idx = jnp.arange(P, dtype=f32)
    a_re = -0.5 + 0.01 * jax.random.normal(ks[8], (L, G, P), f32)
    a_im = math.pi * n_idx[None, None, :] + 0.01 * jax.random.normal(ks[9], (L, G, P), f32)
    log_dt = jax.random.uniform(ks[10], (L, G), f32, math.log(DT_MIN), math.log(DT_MAX))
    return {
        "x": jax.random.normal(ks[0], (BATCH, SEQ, D), f32),
        "c": jax.random.normal(ks[1], (BATCH, D), f32),
        "w_ada": nrm(ks[2], (L, D, N_MOD * D), D) * 0.5,
        "b_ada": 0.02 * jax.random.normal(ks[3], (L, N_MOD * D), f32),
        "norm1_g": gain(ks[4], (L, D)),
        "w_in": nrm(ks[5], (L, D, IN_COLS), D),
        "conv_w": nrm(ks[6], (L, CONV_KERNEL, CONV_WIDTH), CONV_KERNEL),
        "conv_b": 0.02 * jax.random.normal(ks[7], (L, CONV_WIDTH), f32),
        "conv_ln_g": gain(ks[11], (L, CONV_WIDTH)),
        "conv_ln_b": 0.02 * jax.random.normal(ks[12], (L, CONV_WIDTH), f32),
        "conv_proj": nrm(ks[13], (L, CONV_WIDTH, D), CONV_WIDTH),
        "ssm_a_re": a_re,
        "ssm_a_im": a_im,
        "ssm_b_re": nrm(ks[14], (L, G, P, H), 2 * H),
        "ssm_b_im": nrm(ks[15], (L, G, P, H), 2 * H),
        "ssm_c_re": nrm(ks[16], (L, G, H, P), P),
        "ssm_c_im": nrm(ks[17], (L, G, H, P), P),
        "ssm_d": jax.random.normal(ks[18], (L, SSM_WIDTH), f32),
        "ssm_log_dt": log_dt,
        "ssm_glu": nrm(ks[19], (L, SSM_WIDTH, 2 * D), SSM_WIDTH),
        "w_out": nrm(ks[20], (L, D, D), D),
        "norm2_g": gain(ks[21], (L, D)),
        "w_ffn_in": nrm(ks[22], (L, D, 2 * FFN_HIDDEN), D),
        "w_ffn_out": nrm(ks[23], (L, FFN_HIDDEN, D), FFN_HIDDEN),
        "final_g": gain(ks[24], (D,)),
    }


def _fwd_reference(x, c, w_ada, b_ada, norm1_g, w_in, conv_w, conv_b, conv_ln_g, conv_ln_b,
              conv_proj, ssm_a_re, ssm_a_im, ssm_b_re, ssm_b_im, ssm_c_re, ssm_c_im,
              ssm_d, ssm_log_dt, ssm_glu, w_out, norm2_g, w_ffn_in, w_ffn_out, final_g):
    c_act = jax.nn.silu(c)
    split_pts = [2 * CONV_WIDTH, 2 * CONV_WIDTH + SSM_WIDTH, 2 * CONV_WIDTH + SSM_WIDTH + D_MODEL]
    for l in range(DEPTH):
        mod = (c_act @ w_ada[l] + b_ada[l])[:, None, :]
        sh1, sc1, g1, sh2, sc2, g2 = jnp.split(mod, N_MOD, axis=-1)

        h = rmsnorm(x, norm1_g[l]) * (1.0 + sc1) + sh1
        z = h @ w_in[l]
        u_conv, u_ssm, gl_conv, gl_ssm = jnp.split(z, split_pts, axis=-1)
        y_conv = conformer_conv(u_conv, conv_w[l], conv_b[l], conv_ln_g[l],
                                conv_ln_b[l], conv_proj[l])
        y_ssm = s5_mixer(u_ssm, ssm_a_re[l], ssm_a_im[l], ssm_b_re[l], ssm_b_im[l],
                         ssm_c_re[l], ssm_c_im[l], ssm_d[l], ssm_log_dt[l], ssm_glu[l])
        merged = jax.nn.sigmoid(gl_conv) * y_conv + jax.nn.sigmoid(gl_ssm) * y_ssm
        x = x + g1 * (merged @ w_out[l])

        h = rmsnorm(x, norm2_g[l]) * (1.0 + sc2) + sh2
        f_gate, f_up = jnp.split(h @ w_ffn_in[l], 2, axis=-1)
        x = x + g2 * ((jax.nn.silu(f_gate) * f_up) @ w_ffn_out[l])
    return rmsnorm(x, final_g)


import jax as _jax
import jax.numpy as _jnp

TWIN_FORMAT = 'train_step'
FWD_PARAMS = ['x', 'c', 'w_ada', 'b_ada', 'norm1_g', 'w_in', 'conv_w', 'conv_b', 'conv_ln_g', 'conv_ln_b', 'conv_proj', 'ssm_a_re', 'ssm_a_im', 'ssm_b_re', 'ssm_b_im', 'ssm_c_re', 'ssm_c_im', 'ssm_d', 'ssm_log_dt', 'ssm_glu', 'w_out', 'norm2_g', 'w_ffn_in', 'w_ffn_out', 'final_g']
TWIN_WEIGHTS = ['w_ada', 'b_ada', 'norm1_g', 'w_in', 'conv_w', 'conv_b', 'conv_ln_g', 'conv_ln_b', 'conv_proj', 'ssm_a_re', 'ssm_a_im', 'ssm_b_re', 'ssm_b_im', 'ssm_c_re', 'ssm_c_im', 'ssm_d', 'ssm_log_dt', 'ssm_glu', 'w_out', 'norm2_g', 'w_ffn_in', 'w_ffn_out', 'final_g']
TWIN_DIFF_INPUT = 'x'
TWIN_INPUTS = ['x', 'c', 'w_ada', 'b_ada', 'norm1_g', 'w_in', 'conv_w', 'conv_b', 'conv_ln_g', 'conv_ln_b', 'conv_proj', 'ssm_a_re', 'ssm_a_im', 'ssm_b_re', 'ssm_b_im', 'ssm_c_re', 'ssm_c_im', 'ssm_d', 'ssm_log_dt', 'ssm_glu', 'w_out', 'norm2_g', 'w_ffn_in', 'w_ffn_out', 'final_g', 'loss_target', 'm_w_ada', 'm_b_ada', 'm_norm1_g', 'm_w_in', 'm_conv_w', 'm_conv_b', 'm_conv_ln_g', 'm_conv_ln_b', 'm_conv_proj', 'm_ssm_a_re', 'm_ssm_a_im', 'm_ssm_b_re', 'm_ssm_b_im', 'm_ssm_c_re', 'm_ssm_c_im', 'm_ssm_d', 'm_ssm_log_dt', 'm_ssm_glu', 'm_w_out', 'm_norm2_g', 'm_w_ffn_in', 'm_w_ffn_out', 'm_final_g', 'v_w_ada', 'v_b_ada', 'v_norm1_g', 'v_w_in', 'v_conv_w', 'v_conv_b', 'v_conv_ln_g', 'v_conv_ln_b', 'v_conv_proj', 'v_ssm_a_re', 'v_ssm_a_im', 'v_ssm_b_re', 'v_ssm_b_im', 'v_ssm_c_re', 'v_ssm_c_im', 'v_ssm_d', 'v_ssm_log_dt', 'v_ssm_glu', 'v_w_out', 'v_norm2_g', 'v_w_ffn_in', 'v_w_ffn_out', 'v_final_g']
TWIN_OUTPUTS = ['loss', 'grad_x', 'grad_w_ada', 'grad_b_ada', 'grad_norm1_g', 'grad_w_in', 'grad_conv_w', 'grad_conv_b', 'grad_conv_ln_g', 'grad_conv_ln_b', 'grad_conv_proj', 'grad_ssm_a_re', 'grad_ssm_a_im', 'grad_ssm_b_re', 'grad_ssm_b_im', 'grad_ssm_c_re', 'grad_ssm_c_im', 'grad_ssm_d', 'grad_ssm_log_dt', 'grad_ssm_glu', 'grad_w_out', 'grad_norm2_g', 'grad_w_ffn_in', 'grad_w_ffn_out', 'grad_final_g', 'delta_w_ada', 'delta_b_ada', 'delta_norm1_g', 'delta_w_in', 'delta_conv_w', 'delta_conv_b', 'delta_conv_ln_g', 'delta_conv_ln_b', 'delta_conv_proj', 'delta_ssm_a_re', 'delta_ssm_a_im', 'delta_ssm_b_re', 'delta_ssm_b_im', 'delta_ssm_c_re', 'delta_ssm_c_im', 'delta_ssm_d', 'delta_ssm_log_dt', 'delta_ssm_glu', 'delta_w_out', 'delta_norm2_g', 'delta_w_ffn_in', 'delta_w_ffn_out', 'delta_final_g', 'new_m_w_ada', 'new_m_b_ada', 'new_m_norm1_g', 'new_m_w_in', 'new_m_conv_w', 'new_m_conv_b', 'new_m_conv_ln_g', 'new_m_conv_ln_b', 'new_m_conv_proj', 'new_m_ssm_a_re', 'new_m_ssm_a_im', 'new_m_ssm_b_re', 'new_m_ssm_b_im', 'new_m_ssm_c_re', 'new_m_ssm_c_im', 'new_m_ssm_d', 'new_m_ssm_log_dt', 'new_m_ssm_glu', 'new_m_w_out', 'new_m_norm2_g', 'new_m_w_ffn_in', 'new_m_w_ffn_out', 'new_m_final_g', 'new_v_w_ada', 'new_v_b_ada', 'new_v_norm1_g', 'new_v_w_in', 'new_v_conv_w', 'new_v_conv_b', 'new_v_conv_ln_g', 'new_v_conv_ln_b', 'new_v_conv_proj', 'new_v_ssm_a_re', 'new_v_ssm_a_im', 'new_v_ssm_b_re', 'new_v_ssm_b_im', 'new_v_ssm_c_re', 'new_v_ssm_c_im', 'new_v_ssm_d', 'new_v_ssm_log_dt', 'new_v_ssm_glu', 'new_v_w_out', 'new_v_norm2_g', 'new_v_w_ffn_in', 'new_v_w_ffn_out', 'new_v_final_g']
TWIN_LEAF_KINDS = {'loss': 'loss', 'grad_x': 'grad_x', 'grad_w_ada': 'grad_w', 'grad_b_ada': 'grad_w', 'grad_norm1_g': 'grad_w', 'grad_w_in': 'grad_w', 'grad_conv_w': 'grad_w', 'grad_conv_b': 'grad_w', 'grad_conv_ln_g': 'grad_w', 'grad_conv_ln_b': 'grad_w', 'grad_conv_proj': 'grad_w', 'grad_ssm_a_re': 'grad_w', 'grad_ssm_a_im': 'grad_w', 'grad_ssm_b_re': 'grad_w', 'grad_ssm_b_im': 'grad_w', 'grad_ssm_c_re': 'grad_w', 'grad_ssm_c_im': 'grad_w', 'grad_ssm_d': 'grad_w', 'grad_ssm_log_dt': 'grad_w', 'grad_ssm_glu': 'grad_w', 'grad_w_out': 'grad_w', 'grad_norm2_g': 'grad_w', 'grad_w_ffn_in': 'grad_w', 'grad_w_ffn_out': 'grad_w', 'grad_final_g': 'grad_w', 'delta_w_ada': 'delta_w', 'delta_b_ada': 'delta_w', 'delta_norm1_g': 'delta_w', 'delta_w_in': 'delta_w', 'delta_conv_w': 'delta_w', 'delta_conv_b': 'delta_w', 'delta_conv_ln_g': 'delta_w', 'delta_conv_ln_b': 'delta_w', 'delta_conv_proj': 'delta_w', 'delta_ssm_a_re': 'delta_w', 'delta_ssm_a_im': 'delta_w', 'delta_ssm_b_re': 'delta_w', 'delta_ssm_b_im': 'delta_w', 'delta_ssm_c_re': 'delta_w', 'delta_ssm_c_im': 'delta_w', 'delta_ssm_d': 'delta_w', 'delta_ssm_log_dt': 'delta_w', 'delta_ssm_glu': 'delta_w', 'delta_w_out': 'delta_w', 'delta_norm2_g': 'delta_w', 'delta_w_ffn_in': 'delta_w', 'delta_w_ffn_out': 'delta_w', 'delta_final_g': 'delta_w', 'new_m_w_ada': 'new_m', 'new_m_b_ada': 'new_m', 'new_m_norm1_g': 'new_m', 'new_m_w_in': 'new_m', 'new_m_conv_w': 'new_m', 'new_m_conv_b': 'new_m', 'new_m_conv_ln_g': 'new_m', 'new_m_conv_ln_b': 'new_m', 'new_m_conv_proj': 'new_m', 'new_m_ssm_a_re': 'new_m', 'new_m_ssm_a_im': 'new_m', 'new_m_ssm_b_re': 'new_m', 'new_m_ssm_b_im': 'new_m', 'new_m_ssm_c_re': 'new_m', 'new_m_ssm_c_im': 'new_m', 'new_m_ssm_d': 'new_m', 'new_m_ssm_log_dt': 'new_m', 'new_m_ssm_glu': 'new_m', 'new_m_w_out': 'new_m', 'new_m_norm2_g': 'new_m', 'new_m_w_ffn_in': 'new_m', 'new_m_w_ffn_out': 'new_m', 'new_m_final_g': 'new_m', 'new_v_w_ada': 'new_v', 'new_v_b_ada': 'new_v', 'new_v_norm1_g': 'new_v', 'new_v_w_in': 'new_v', 'new_v_conv_w': 'new_v', 'new_v_conv_b': 'new_v', 'new_v_conv_ln_g': 'new_v', 'new_v_conv_ln_b': 'new_v', 'new_v_conv_proj': 'new_v', 'new_v_ssm_a_re': 'new_v', 'new_v_ssm_a_im': 'new_v', 'new_v_ssm_b_re': 'new_v', 'new_v_ssm_b_im': 'new_v', 'new_v_ssm_c_re': 'new_v', 'new_v_ssm_c_im': 'new_v', 'new_v_ssm_d': 'new_v', 'new_v_ssm_log_dt': 'new_v', 'new_v_ssm_glu': 'new_v', 'new_v_w_out': 'new_v', 'new_v_norm2_g': 'new_v', 'new_v_w_ffn_in': 'new_v', 'new_v_w_ffn_out': 'new_v', 'new_v_final_g': 'new_v'}


def _forward(args):
    return _fwd_reference(*[args[k] for k in FWD_PARAMS])


def _output_shape():
    def fwd():
        inp = _fwd_setup_inputs(0)
        return _fwd_reference(*[inp[k] for k in FWD_PARAMS])
    out = _jax.eval_shape(fwd)
    return out.shape, out.dtype

N_MICROBATCH = 1
ADAM_LR = 0.001
ADAM_B1 = 0.9
ADAM_B2 = 0.999
ADAM_EPS = 1e-08
ADAM_WD = 0.01
ADAM_STEP = 10
PER_EXAMPLE_BATCH_AXIS = {'x': 0, 'c': 0, 'loss_target': 0}
SHARED_INPUTS = []
_WEIGHT_DTYPES = {'w_ada': _jnp.float32, 'b_ada': _jnp.float32, 'norm1_g': _jnp.float32, 'w_in': _jnp.float32, 'conv_w': _jnp.float32, 'conv_b': _jnp.float32, 'conv_ln_g': _jnp.float32, 'conv_ln_b': _jnp.float32, 'conv_proj': _jnp.float32, 'ssm_a_re': _jnp.float32, 'ssm_a_im': _jnp.float32, 'ssm_b_re': _jnp.float32, 'ssm_b_im': _jnp.float32, 'ssm_c_re': _jnp.float32, 'ssm_c_im': _jnp.float32, 'ssm_d': _jnp.float32, 'ssm_log_dt': _jnp.float32, 'ssm_glu': _jnp.float32, 'w_out': _jnp.float32, 'norm2_g': _jnp.float32, 'w_ffn_in': _jnp.float32, 'w_ffn_out': _jnp.float32, 'final_g': _jnp.float32}
MOMENT_SCALE = {'w_ada': 4.985348e-02, 'b_ada': 9.277654e-02, 'norm1_g': 2.419449e-02, 'w_in': 1.355699e-02, 'conv_w': 2.690208e-02, 'conv_b': 5.422635e-02, 'conv_ln_g': 3.194768e-02, 'conv_ln_b': 2.772617e-02, 'conv_proj': 1.835121e-02, 'ssm_a_re': 1.818708e-03, 'ssm_a_im': 1.165223e-03, 'ssm_b_re': 8.627369e-04, 'ssm_b_im': 8.836765e-04, 'ssm_c_re': 1.275196e-03, 'ssm_c_im': 1.216754e-03, 'ssm_d': 1.760183e-02, 'ssm_log_dt': 1.089528e+00, 'ssm_glu': 8.560366e-03, 'w_out': 2.176170e-02, 'norm2_g': 5.438863e-02, 'w_ffn_in': 2.349459e-02, 'w_ffn_out': 3.823195e-02, 'final_g': 3.208262e+01}


def _to_microbatches(a, axis):
    t = _jnp.moveaxis(a, axis, 0)
    t = t.reshape((N_MICROBATCH, t.shape[0] // N_MICROBATCH) + t.shape[1:])
    return _jnp.moveaxis(t, 1, axis + 1)


def setup_inputs(seed: int = 0) -> dict:
    inp = _fwd_setup_inputs(seed)
    key = _jax.random.fold_in(_jax.random.key(seed), 7919)
    shape, _ = _output_shape()
    out = dict(inp)
    out["loss_target"] = _jax.random.normal(_jax.random.fold_in(key, 0), shape, _jnp.float32)
    for i, name in enumerate(TWIN_WEIGHTS):
        w = inp[name].astype(_jnp.float32)
        if MOMENT_SCALE is None:
            s = _jnp.sqrt(_jnp.mean(_jnp.square(w)) + 1e-30)
        else:
            s = MOMENT_SCALE[name]
        km, kv = _jax.random.split(_jax.random.fold_in(key, i + 1))
        out[name] = w
        out["m_" + name] = s * _jax.random.normal(km, w.shape, _jnp.float32)
        out["v_" + name] = (s * s) * _jax.random.uniform(kv, w.shape, _jnp.float32, 0.5, 1.5)
    if N_MICROBATCH > 1:
        for name, axis in PER_EXAMPLE_BATCH_AXIS.items():
            out[name] = _to_microbatches(out[name], axis)
    return {'x': out['x'], 'c': out['c'], 'w_ada': out['w_ada'], 'b_ada': out['b_ada'], 'norm1_g': out['norm1_g'], 'w_in': out['w_in'], 'conv_w': out['conv_w'], 'conv_b': out['conv_b'], 'conv_ln_g': out['conv_ln_g'], 'conv_ln_b': out['conv_ln_b'], 'conv_proj': out['conv_proj'], 'ssm_a_re': out['ssm_a_re'], 'ssm_a_im': out['ssm_a_im'], 'ssm_b_re': out['ssm_b_re'], 'ssm_b_im': out['ssm_b_im'], 'ssm_c_re': out['ssm_c_re'], 'ssm_c_im': out['ssm_c_im'], 'ssm_d': out['ssm_d'], 'ssm_log_dt': out['ssm_log_dt'], 'ssm_glu': out['ssm_glu'], 'w_out': out['w_out'], 'norm2_g': out['norm2_g'], 'w_ffn_in': out['w_ffn_in'], 'w_ffn_out': out['w_ffn_out'], 'final_g': out['final_g'], 'loss_target': out['loss_target'], 'm_w_ada': out['m_w_ada'], 'm_b_ada': out['m_b_ada'], 'm_norm1_g': out['m_norm1_g'], 'm_w_in': out['m_w_in'], 'm_conv_w': out['m_conv_w'], 'm_conv_b': out['m_conv_b'], 'm_conv_ln_g': out['m_conv_ln_g'], 'm_conv_ln_b': out['m_conv_ln_b'], 'm_conv_proj': out['m_conv_proj'], 'm_ssm_a_re': out['m_ssm_a_re'], 'm_ssm_a_im': out['m_ssm_a_im'], 'm_ssm_b_re': out['m_ssm_b_re'], 'm_ssm_b_im': out['m_ssm_b_im'], 'm_ssm_c_re': out['m_ssm_c_re'], 'm_ssm_c_im': out['m_ssm_c_im'], 'm_ssm_d': out['m_ssm_d'], 'm_ssm_log_dt': out['m_ssm_log_dt'], 'm_ssm_glu': out['m_ssm_glu'], 'm_w_out': out['m_w_out'], 'm_norm2_g': out['m_norm2_g'], 'm_w_ffn_in': out['m_w_ffn_in'], 'm_w_ffn_out': out['m_w_ffn_out'], 'm_final_g': out['m_final_g'], 'v_w_ada': out['v_w_ada'], 'v_b_ada': out['v_b_ada'], 'v_norm1_g': out['v_norm1_g'], 'v_w_in': out['v_w_in'], 'v_conv_w': out['v_conv_w'], 'v_conv_b': out['v_conv_b'], 'v_conv_ln_g': out['v_conv_ln_g'], 'v_conv_ln_b': out['v_conv_ln_b'], 'v_conv_proj': out['v_conv_proj'], 'v_ssm_a_re': out['v_ssm_a_re'], 'v_ssm_a_im': out['v_ssm_a_im'], 'v_ssm_b_re': out['v_ssm_b_re'], 'v_ssm_b_im': out['v_ssm_b_im'], 'v_ssm_c_re': out['v_ssm_c_re'], 'v_ssm_c_im': out['v_ssm_c_im'], 'v_ssm_d': out['v_ssm_d'], 'v_ssm_log_dt': out['v_ssm_log_dt'], 'v_ssm_glu': out['v_ssm_glu'], 'v_w_out': out['v_w_out'], 'v_norm2_g': out['v_norm2_g'], 'v_w_ffn_in': out['v_w_ffn_in'], 'v_w_ffn_out': out['v_w_ffn_out'], 'v_final_g': out['v_final_g']}


def _loss(weights, diff, rest, loss_target):
    with _jax.named_scope("forward"):
        args = {**rest, TWIN_DIFF_INPUT: diff, **{k: w.astype(_WEIGHT_DTYPES[k]) for k, w in weights.items()}}
        y = _forward(args)
    with _jax.named_scope("loss_head"):
        err = _jnp.square(y.astype(_jnp.float32) - loss_target)
        return 0.5 * _jnp.sum(_jnp.mean(err, axis=-1)) if err.ndim else 0.5 * err


def _adamw(w, g, m, v):
    m = ADAM_B1 * m + (1.0 - ADAM_B1) * g
    v = ADAM_B2 * v + (1.0 - ADAM_B2) * _jnp.square(g)
    m_hat = m / (1.0 - ADAM_B1 ** ADAM_STEP)
    v_hat = v / (1.0 - ADAM_B2 ** ADAM_STEP)
    delta = -ADAM_LR * (m_hat / (_jnp.sqrt(v_hat) + ADAM_EPS) + ADAM_WD * w)
    return delta, m, v


def reference(x, c, w_ada, b_ada, norm1_g, w_in, conv_w, conv_b, conv_ln_g, conv_ln_b, conv_proj, ssm_a_re, ssm_a_im, ssm_b_re, ssm_b_im, ssm_c_re, ssm_c_im, ssm_d, ssm_log_dt, ssm_glu, w_out, norm2_g, w_ffn_in, w_ffn_out, final_g, loss_target, m_w_ada, m_b_ada, m_norm1_g, m_w_in, m_conv_w, m_conv_b, m_conv_ln_g, m_conv_ln_b, m_conv_proj, m_ssm_a_re, m_ssm_a_im, m_ssm_b_re, m_ssm_b_im, m_ssm_c_re, m_ssm_c_im, m_ssm_d, m_ssm_log_dt, m_ssm_glu, m_w_out, m_norm2_g, m_w_ffn_in, m_w_ffn_out, m_final_g, v_w_ada, v_b_ada, v_norm1_g, v_w_in, v_conv_w, v_conv_b, v_conv_ln_g, v_conv_ln_b, v_conv_proj, v_ssm_a_re, v_ssm_a_im, v_ssm_b_re, v_ssm_b_im, v_ssm_c_re, v_ssm_c_im, v_ssm_d, v_ssm_log_dt, v_ssm_glu, v_w_out, v_norm2_g, v_w_ffn_in, v_w_ffn_out, v_final_g):
    given = dict(x=x, c=c, w_ada=w_ada, b_ada=b_ada, norm1_g=norm1_g, w_in=w_in, conv_w=conv_w, conv_b=conv_b, conv_ln_g=conv_ln_g, conv_ln_b=conv_ln_b, conv_proj=conv_proj, ssm_a_re=ssm_a_re, ssm_a_im=ssm_a_im, ssm_b_re=ssm_b_re, ssm_b_im=ssm_b_im, ssm_c_re=ssm_c_re, ssm_c_im=ssm_c_im, ssm_d=ssm_d, ssm_log_dt=ssm_log_dt, ssm_glu=ssm_glu, w_out=w_out, norm2_g=norm2_g, w_ffn_in=w_ffn_in, w_ffn_out=w_ffn_out, final_g=final_g, loss_target=loss_target, m_w_ada=m_w_ada, m_b_ada=m_b_ada, m_norm1_g=m_norm1_g, m_w_in=m_w_in, m_conv_w=m_conv_w, m_conv_b=m_conv_b, m_conv_ln_g=m_conv_ln_g, m_conv_ln_b=m_conv_ln_b, m_conv_proj=m_conv_proj, m_ssm_a_re=m_ssm_a_re, m_ssm_a_im=m_ssm_a_im, m_ssm_b_re=m_ssm_b_re, m_ssm_b_im=m_ssm_b_im, m_ssm_c_re=m_ssm_c_re, m_ssm_c_im=m_ssm_c_im, m_ssm_d=m_ssm_d, m_ssm_log_dt=m_ssm_log_dt, m_ssm_glu=m_ssm_glu, m_w_out=m_w_out, m_norm2_g=m_norm2_g, m_w_ffn_in=m_w_ffn_in, m_w_ffn_out=m_w_ffn_out, m_final_g=m_final_g, v_w_ada=v_w_ada, v_b_ada=v_b_ada, v_norm1_g=v_norm1_g, v_w_in=v_w_in, v_conv_w=v_conv_w, v_conv_b=v_conv_b, v_conv_ln_g=v_conv_ln_g, v_conv_ln_b=v_conv_ln_b, v_conv_proj=v_conv_proj, v_ssm_a_re=v_ssm_a_re, v_ssm_a_im=v_ssm_a_im, v_ssm_b_re=v_ssm_b_re, v_ssm_b_im=v_ssm_b_im, v_ssm_c_re=v_ssm_c_re, v_ssm_c_im=v_ssm_c_im, v_ssm_d=v_ssm_d, v_ssm_log_dt=v_ssm_log_dt, v_ssm_glu=v_ssm_glu, v_w_out=v_w_out, v_norm2_g=v_norm2_g, v_w_ffn_in=v_w_ffn_in, v_w_ffn_out=v_w_ffn_out, v_final_g=v_final_g)
    weights = {n: given[n] for n in TWIN_WEIGHTS}
    shared = {n: given[n] for n in SHARED_INPUTS}
    per_example = {n: given[n] for n in ['x', 'c']}
    grad_fn = _jax.value_and_grad(_loss, argnums=(0, 1))

    def one_microbatch(ex, loss_target):
        ex = dict(ex)
        diff = ex.pop(TWIN_DIFF_INPUT)
        return grad_fn(weights, diff, {**shared, **ex}, loss_target)

    if N_MICROBATCH == 1:
        loss, (grad_w, grad_x) = one_microbatch(per_example, given["loss_target"])
    else:
        def body(carry, xs):
            loss_sum, grad_sum = carry
            l_k, (gw_k, gx_k) = one_microbatch(xs[0], xs[1])
            with _jax.named_scope("update"):
                return (loss_sum + l_k, _jax.tree.map(_jnp.add, grad_sum, gw_k)), gx_k

        init = (_jnp.zeros((), _jnp.float32), _jax.tree.map(_jnp.zeros_like, weights))
        (loss, grad_w), grad_x = _jax.lax.scan(body, init, (per_example, given["loss_target"]))
    with _jax.named_scope("update"):
        delta_w, new_m, new_v = {}, {}, {}
        for n in TWIN_WEIGHTS:
            delta_w[n], new_m[n], new_v[n] = _adamw(weights[n], grad_w[n], given["m_" + n], given["v_" + n])
    return (loss, grad_x, *[grad_w[n] for n in TWIN_WEIGHTS], *[delta_w[n] for n in TWIN_WEIGHTS],
            *[new_m[n] for n in TWIN_WEIGHTS], *[new_v[n] for n in TWIN_WEIGHTS])
```

```python
import math

import jax
import jax.numpy as jnp
import numpy as np
from jax import lax
from jax.experimental import pallas as pl
from jax.experimental.pallas import tpu as pltpu

F32 = jnp.float32
BF16 = jnp.bfloat16
EPS = 1e-6
D_MODEL = 1024
CW = 512
KW = 31
HALO = 32
G, P, H = 32, 64, 16
NST = G * P
FH = 2816
N_DEV = 8
N_CHIP = 4
VMEM_LIMIT = 56 * 1024 * 1024
LR, B1, B2, AEPS, WD, STEP = 0.001, 0.9, 0.999, 1e-08, 0.01, 10
MESH = pl.DeviceIdType.MESH


def _cp(sem=None):
    return pltpu.CompilerParams(dimension_semantics=sem, vmem_limit_bytes=VMEM_LIMIT)


def _sig(x):
    return jax.nn.sigmoid(x)


def _full(shape):
    return pl.BlockSpec(shape, lambda *_: (0,) * len(shape))


def _colsum8(v):
    t, c = v.shape
    return jnp.sum(v.reshape(t // 8, 8, c), axis=0)


def _matmul(a, b, mode, tm, tn, tk, out_dtype, name):
    if mode == "nn":
        (M, K), N = a.shape, b.shape[1]
    elif mode == "nt":
        (M, K), N = a.shape, b.shape[0]
    else:
        (K, M), N = a.shape, b.shape[1]
    tm, tn, tk = min(tm, M), min(tn, N), min(tk, K)
    assert M % tm == 0 and N % tn == 0 and K % tk == 0, (name, M, N, K, tm, tn, tk)
    nk = K // tk
    if mode == "nn":
        a_spec = pl.BlockSpec((tm, tk), lambda i, j, k: (i, k))
        b_spec = pl.BlockSpec((tk, tn), lambda i, j, k: (k, j))
        dims = (((1,), (0,)), ((), ()))
    elif mode == "nt":
        a_spec = pl.BlockSpec((tm, tk), lambda i, j, k: (i, k))
        b_spec = pl.BlockSpec((tn, tk), lambda i, j, k: (j, k))
        dims = (((1,), (1,)), ((), ()))
    else:
        a_spec = pl.BlockSpec((tk, tm), lambda i, j, k: (k, i))
        b_spec = pl.BlockSpec((tk, tn), lambda i, j, k: (k, j))
        dims = (((0,), (0,)), ((), ()))

    def body(a_ref, b_ref, o_ref, acc_ref):
        k = pl.program_id(2)
        part = lax.dot_general(a_ref[...].astype(BF16), b_ref[...].astype(BF16), dims,
                               preferred_element_type=F32)
        if nk == 1:
            o_ref[...] = part.astype(out_dtype)
        else:
            @pl.when(k == 0)
            def _():
                acc_ref[...] = part

            @pl.when(k > 0)
            def _():
                acc_ref[...] += part

            @pl.when(k == nk - 1)
            def _():
                o_ref[...] = acc_ref[...].astype(out_dtype)

    return pl.pallas_call(
        body, name=name,
        out_shape=jax.ShapeDtypeStruct((M, N), out_dtype),
        grid=(M // tm, N // tn, nk),
        in_specs=[a_spec, b_spec],
        out_specs=pl.BlockSpec((tm, tn), lambda i, j, k: (i, j)),
        scratch_shapes=[pltpu.VMEM((tm, tn) if nk > 1 else (8, 128), F32)],
        compiler_params=_cp(("parallel", "parallel", "arbitrary")),
    )(a, b)


def _row_tile(S):
    return min(512, S)


def _normmod(x, g, sc, sh, name):
    S, D = x.shape
    tm = _row_tile(S)

    def body(x_ref, g_ref, sc_ref, sh_ref, h_ref):
        xv = x_ref[...]
        r = lax.rsqrt(jnp.mean(xv * xv, axis=-1, keepdims=True) + EPS)
        h_ref[...] = (xv * r * (g_ref[...] * (1.0 + sc_ref[...])) + sh_ref[...]).astype(BF16)

    row = pl.BlockSpec((tm, D), lambda i: (i, 0))
    return pl.pallas_call(
        body, name=name, out_shape=jax.ShapeDtypeStruct((S, D), BF16), grid=(S // tm,),
        in_specs=[row, _full((1, D)), _full((1, D)), _full((1, D))], out_specs=row,
        compiler_params=_cp(("parallel",)))(x, g, sc, sh)


def _resid_normmod(x, o, g1, g, sc, sh, name):
    S, D = x.shape
    tm = _row_tile(S)

    def body(x_ref, o_ref, g1_ref, g_ref, sc_ref, sh_ref, x2_ref, h_ref):
        xv = x_ref[...] + g1_ref[...] * o_ref[...]
        x2_ref[...] = xv
        r = lax.rsqrt(jnp.mean(xv * xv, axis=-1, keepdims=True) + EPS)
        h_ref[...] = (xv * r * (g_ref[...] * (1.0 + sc_ref[...])) + sh_ref[...]).astype(BF16)

    row = pl.BlockSpec((tm, D), lambda i: (i, 0))
    par = _full((1, D))
    return pl.pallas_call(
        body, name=name,
        out_shape=(jax.ShapeDtypeStruct((S, D), F32), jax.ShapeDtypeStruct((S, D), BF16)),
        grid=(S // tm,), in_specs=[row, row, par, par, par, par], out_specs=(row, row),
        compiler_params=_cp(("parallel",)))(x, o, g1, g, sc, sh)


def _conv_fwd(z, conv_w, conv_b, ln_g, ln_b):
    S = z.shape[0]
    tm = min(128, S)
    sub = 32
    hb = tm // HALO

    def body(a_ref, g_ref, ha_ref, hg_ref, w_ref, b_ref, lg_ref, lb_ref, yc_ref, s_ref, ug_ref):
        i = pl.program_id(0)
        halo = ha_ref[...] * _sig(hg_ref[...])
        ug_ref[0:HALO, :] = jnp.where(i == 0, 0.0, halo)
        ug_ref[HALO:, :] = a_ref[...] * _sig(g_ref[...])
        for rb in range(tm // sub):
            acc = jnp.zeros((sub, CW), F32) + b_ref[...]
            for k in range(KW):
                off = rb * sub + HALO - (KW - 1) + k
                acc = acc + w_ref[k:k + 1, :] * ug_ref[off:off + sub, :]
            yc_ref[rb * sub:(rb + 1) * sub, :] = acc
            mu = jnp.mean(acc, axis=-1, keepdims=True)
            cen = acc - mu
            rstd = lax.rsqrt(jnp.mean(cen * cen, axis=-1, keepdims=True) + EPS)
            ln = cen * rstd * lg_ref[...] + lb_ref[...]
            s_ref[rb * sub:(rb + 1) * sub, :] = (ln * _sig(ln)).astype(BF16)

    prev = lambda i: (jnp.maximum(i * hb - 1, 0), 0)
    return pl.pallas_call(
        body, name="conv_fwd",
        out_shape=(jax.ShapeDtypeStruct((S, CW), F32), jax.ShapeDtypeStruct((S, CW), BF16)),
        grid=(S // tm,),
        in_specs=[pl.BlockSpec((tm, CW), lambda i: (i, 0)), pl.BlockSpec((tm, CW), lambda i: (i, 1)),
                  pl.BlockSpec((HALO, CW), prev), pl.BlockSpec((HALO, CW), lambda i: (jnp.maximum(i * hb - 1, 0), 1)),
                  _full((KW, CW)), _full((1, CW)), _full((1, CW)), _full((1, CW))],
        out_specs=(pl.BlockSpec((tm, CW), lambda i: (i, 0)), pl.BlockSpec((tm, CW), lambda i: (i, 0))),
        scratch_shapes=[pltpu.VMEM((tm + HALO, CW), F32)],
        compiler_params=_cp(("parallel",)))(z, z, z, z, conv_w, conv_b, ln_g, ln_b)


def _conv_bwd_ln(dsc, yc, ln_g, ln_b):
    S = yc.shape[0]
    tm = _row_tile(S)

    def body(d_ref, yc_ref, lg_ref, lb_ref, dyc_ref, dlg_ref, dlb_ref, dcb_ref):
        i = pl.program_id(0)
        yc_v = yc_ref[...]
        mu = jnp.mean(yc_v, axis=-1, keepdims=True)
        cen = yc_v - mu
        rstd = lax.rsqrt(jnp.mean(cen * cen, axis=-1, keepdims=True) + EPS)
        yn = cen * rstd
        ln = yn * lg_ref[...] + lb_ref[...]
        sl = _sig(ln)
        dln = d_ref[...] * (sl * (1.0 + ln * (1.0 - sl)))
        dyn = dln * lg_ref[...]
        dyc = rstd * (dyn - jnp.mean(dyn, axis=-1, keepdims=True)
                      - yn * jnp.mean(dyn * yn, axis=-1, keepdims=True))
        dyc_ref[...] = dyc

        @pl.when(i == 0)
        def _():
            dlg_ref[...] = jnp.zeros_like(dlg_ref)
            dlb_ref[...] = jnp.zeros_like(dlb_ref)
            dcb_ref[...] = jnp.zeros_like(dcb_ref)

        dlg_ref[...] += _colsum8(dln * yn)
        dlb_ref[...] += _colsum8(dln)
        dcb_ref[...] += _colsum8(dyc)

    row = pl.BlockSpec((tm, CW), lambda i: (i, 0))
    acc = jax.ShapeDtypeStruct((8, CW), F32)
    return pl.pallas_call(
        body, name="conv_bwd_ln",
        out_shape=(jax.ShapeDtypeStruct((S, CW), F32), acc, acc, acc), grid=(S // tm,),
        in_specs=[row, row, _full((1, CW)), _full((1, CW))],
        out_specs=(row, _full((8, CW)), _full((8, CW)), _full((8, CW))),
        compiler_params=_cp(("arbitrary",)))(dsc, yc, ln_g, ln_b)


def _conv_bwd(dyc, z, conv_w):
    S = z.shape[0]
    tm = min(128, S)
    sub = 32
    hb = tm // HALO
    nt = S // tm

    def body(d_ref, dn_ref, a_ref, g_ref, ha_ref, hg_ref, w_ref, dz_ref, dw_ref, ug_ref, dy_ref):
        i = pl.program_id(0)
        halo = ha_ref[...] * _sig(hg_ref[...])
        ug_ref[0:HALO, :] = jnp.where(i == 0, 0.0, halo)
        a = a_ref[...]
        sg = _sig(g_ref[...])
        ug_ref[HALO:, :] = a * sg
        dy_ref[0:tm, :] = d_ref[...]
        dy_ref[tm:, :] = jnp.where(i == nt - 1, 0.0, dn_ref[...])

        @pl.when(i == 0)
        def _():
            dw_ref[...] = jnp.zeros_like(dw_ref)

        for rb in range(tm // sub):
            r0 = rb * sub
            acc = jnp.zeros((sub, CW), F32)
            dyc_b = dy_ref[r0:r0 + sub, :]
            for k in range(KW):
                up = r0 + (KW - 1) - k
                acc = acc + w_ref[k:k + 1, :] * dy_ref[up:up + sub, :]
                off = r0 + HALO - (KW - 1) + k
                dw_ref[k * 8:(k + 1) * 8, :] += _colsum8(dyc_b * ug_ref[off:off + sub, :])
            a_b = a[r0:r0 + sub, :]
            sg_b = sg[r0:r0 + sub, :]
            dz_ref[r0:r0 + sub, 0:CW] = (acc * sg_b).astype(BF16)
            dz_ref[r0:r0 + sub, CW:2 * CW] = (acc * a_b * sg_b * (1.0 - sg_b)).astype(BF16)

    return pl.pallas_call(
        body, name="conv_bwd",
        out_shape=(jax.ShapeDtypeStruct((S, 2 * CW), BF16), jax.ShapeDtypeStruct((KW * 8, CW), F32)),
        grid=(nt,),
        in_specs=[pl.BlockSpec((tm, CW), lambda i: (i, 0)),
                  pl.BlockSpec((HALO, CW), lambda i: (jnp.minimum((i + 1) * hb, nt * hb - 1), 0)),
                  pl.BlockSpec((tm, CW), lambda i: (i, 0)), pl.BlockSpec((tm, CW), lambda i: (i, 1)),
                  pl.BlockSpec((HALO, CW), lambda i: (jnp.maximum(i * hb - 1, 0), 0)),
                  pl.BlockSpec((HALO, CW), lambda i: (jnp.maximum(i * hb - 1, 0), 1)),
                  _full((KW, CW))],
        out_specs=(pl.BlockSpec((tm, 2 * CW), lambda i: (i, 0)), _full((KW * 8, CW))),
        scratch_shapes=[pltpu.VMEM((tm + HALO, CW), F32), pltpu.VMEM((tm + HALO, CW), F32)],
        compiler_params=_cp(("arbitrary",)))(dyc, dyc, z, z, z, z, conv_w)


_GELU_C = math.sqrt(2.0 / math.pi)


def _gelu(x):
    return 0.5 * x * (1.0 + jnp.tanh(_GELU_C * (x + 0.044715 * x * x * x)))


def _gelu_grad(x):
    t = jnp.tanh(_GELU_C * (x + 0.044715 * x * x * x))
    return 0.5 * (1.0 + t) + 0.5 * x * (1.0 - t * t) * (_GELU_C * (1.0 + 3 * 0.044715 * x * x))


_LW = 512


def _ssm_fwd(z, bb, cm, d, tab):
    S = z.shape[0]
    tm = min(256, S)

    def body(u_ref, bb_ref, cm_ref, d_ref, t_ref, x_ref, ys_ref, yg_ref, car_ref):
        i = pl.program_id(0)

        @pl.when(i == 0)
        def _():
            car_ref[...] = jnp.zeros_like(car_ref)

        u = u_ref[...]
        x_ref[...] = jnp.dot(u.astype(BF16), bb_ref[...], preferred_element_type=F32)
        for c in range(NST // _LW):
            lre = pl.ds(c * _LW, _LW)
            lim = pl.ds(NST + c * _LW, _LW)

            def blk(j, car):
                cr, ci = car
                rows = pl.ds(pl.multiple_of(j * 8, 8), 8)
                r = x_ref[rows, lre]
                im = x_ref[rows, lim]
                for lvl, s in enumerate((1, 2, 4)):
                    mr = t_ref[16 * lvl:16 * lvl + 8, lre]
                    mi = t_ref[16 * lvl + 8:16 * lvl + 16, lre]
                    sr = pltpu.roll(r, s, 0)
                    si = pltpu.roll(im, s, 0)
                    r, im = r + (mr * sr - mi * si), im + (mr * si + mi * sr)
                pr = t_ref[48:56, lre]
                pi_ = t_ref[56:64, lre]
                r, im = r + (pr * cr - pi_ * ci), im + (pr * ci + pi_ * cr)
                x_ref[rows, lre] = r
                x_ref[rows, lim] = im
                return (jnp.broadcast_to(r[7:8, :], (8, _LW)), jnp.broadcast_to(im[7:8, :], (8, _LW)))

            cr, ci = lax.fori_loop(0, tm // 8, blk, (car_ref[:, lre], car_ref[:, lim]))
            car_ref[:, lre] = cr
            car_ref[:, lim] = ci
        ys = jnp.dot(x_ref[...].astype(BF16), cm_ref[...], preferred_element_type=F32) + d_ref[...] * u
        ys_ref[...] = ys
        yg_ref[...] = _gelu(ys).astype(BF16)

    return pl.pallas_call(
        body, name="ssm_fwd",
        out_shape=(jax.ShapeDtypeStruct((S, 2 * NST), F32), jax.ShapeDtypeStruct((S, CW), F32),
                   jax.ShapeDtypeStruct((S, CW), BF16)),
        grid=(S // tm,),
        in_specs=[pl.BlockSpec((tm, CW), lambda i: (i, 2)), _full((CW, 2 * NST)), _full((2 * NST, CW)),
                  _full((1, CW)), _full((64, NST))],
        out_specs=(pl.BlockSpec((tm, 2 * NST), lambda i: (i, 0)), pl.BlockSpec((tm, CW), lambda i: (i, 0)),
                   pl.BlockSpec((tm, CW), lambda i: (i, 0))),
        scratch_shapes=[pltpu.VMEM((8, 2 * NST), F32)],
        compiler_params=_cp(("arbitrary",)))(z, bb, cm, d, tab)


def _ssm_bwd(dyg, ys, z, xs, cmt, bbt, d, tab):
    S = z.shape[0]
    tm = min(256, S)
    nt = S // tm

    def body(dyg_ref, ys_ref, u_ref, x_ref, cmt_ref, bbt_ref, d_ref, t_ref,
             lam_ref, du_ref, dys_ref, de_ref, dd_ref, car_ref):
        i = pl.program_id(0)

        @pl.when(i == 0)
        def _():
            car_ref[...] = jnp.zeros_like(car_ref)
            de_ref[...] = jnp.zeros_like(de_ref)
            dd_ref[...] = jnp.zeros_like(dd_ref)

        u = u_ref[...]
        dys = dyg_ref[...] * _gelu_grad(ys_ref[...])
        dys_ref[...] = dys.astype(BF16)
        dd_ref[...] += _colsum8(dys * u)
        lam_ref[...] = jnp.dot(dys.astype(BF16), cmt_ref[...], preferred_element_type=F32)
        row = lax.broadcasted_iota(jnp.int32, (8, _LW), 0)
        for c in range(NST // _LW):
            lre = pl.ds(c * _LW, _LW)
            lim = pl.ds(NST + c * _LW, _LW)

            def blk(jj, car):
                cr, ci, ar, ai = car
                j = tm // 8 - 1 - jj
                rows = pl.ds(pl.multiple_of(j * 8, 8), 8)
                r = lam_ref[rows, lre]
                im = lam_ref[rows, lim]
                for lvl, s in enumerate((1, 2, 4)):
                    mr = t_ref[16 * lvl:16 * lvl + 8, lre]
                    mi = t_ref[16 * lvl + 8:16 * lvl + 16, lre]
                    sr = pltpu.roll(r, 8 - s, 0)
                    si = pltpu.roll(im, 8 - s, 0)
                    r, im = r + (mr * sr - mi * si), im + (mr * si + mi * sr)
                pr = t_ref[48:56, lre]
                pi_ = t_ref[56:64, lre]
                r, im = r + (pr * cr - pi_ * ci), im + (pr * ci + pi_ * cr)
                lam_ref[rows, lre] = r
                lam_ref[rows, lim] = im
                nr = jnp.where(row == 7, cr, pltpu.roll(r, 7, 0))
                ni = jnp.where(row == 7, ci, pltpu.roll(im, 7, 0))
                xr = x_ref[rows, lre]
                xi = x_ref[rows, lim]
                ar = ar + (nr * xr + ni * xi)
                ai = ai + (ni * xr - nr * xi)
                return (jnp.broadcast_to(r[0:1, :], (8, _LW)), jnp.broadcast_to(im[0:1, :], (8, _LW)), ar, ai)

            zero = jnp.zeros((8, _LW), F32)
            cr, ci, ar, ai = lax.fori_loop(0, tm // 8, blk, (car_ref[:, lre], car_ref[:, lim], zero, zero))
            car_ref[:, lre] = cr
            car_ref[:, lim] = ci
            de_ref[0:8, lre] += ar
            de_ref[8:16, lre] += ai
        du = jnp.dot(lam_ref[...].astype(BF16), bbt_ref[...], preferred_element_type=F32) + dys * d_ref[...]
        du_ref[...] = du.astype(BF16)

    rev = lambda i: (nt - 1 - i, 0)
    return pl.pallas_call(
        body, name="ssm_bwd",
        out_shape=(jax.ShapeDtypeStruct((S, 2 * NST), F32), jax.ShapeDtypeStruct((S, CW), BF16),
                   jax.ShapeDtypeStruct((S, CW), BF16), jax.ShapeDtypeStruct((16, NST), F32),
                   jax.ShapeDtypeStruct((8, CW), F32)),
        grid=(nt,),
        in_specs=[pl.BlockSpec((tm, CW), rev), pl.BlockSpec((tm, CW), rev),
                  pl.BlockSpec((tm, CW), lambda i: (nt - 1 - i, 2)), pl.BlockSpec((tm, 2 * NST), rev),
                  _full((CW, 2 * NST)), _full((2 * NST, CW)), _full((1, CW)), _full((64, NST))],
        out_specs=(pl.BlockSpec((tm, 2 * NST), rev), pl.BlockSpec((tm, CW), rev), pl.BlockSpec((tm, CW), rev),
                   _full((16, NST)), _full((8, CW))),
        scratch_shapes=[pltpu.VMEM((8, 2 * NST), F32)],
        compiler_params=_cp(("arbitrary",)))(dyg, ys, z, xs, cmt, bbt, d, tab)


def _ssm_prep(a_re, a_im, b_re, b_im, log_dt):
    dt = jnp.exp(log_dt.reshape(G))[:, None]
    mag = jnp.exp(dt * a_re)
    e_re, e_im = mag * jnp.cos(dt * a_im), mag * jnp.sin(dt * a_im)
    n_re, n_im = e_re - 1.0, e_im
    den = a_re * a_re + a_im * a_im
    q_re = (n_re * a_re + n_im * a_im) / den
    q_im = (n_im * a_re - n_re * a_im) / den
    bb_re = q_re[..., None] * b_re - q_im[..., None] * b_im
    bb_im = q_re[..., None] * b_im + q_im[..., None] * b_re
    return e_re, e_im, bb_re, bb_im


def _scan_tables(e_re, e_im, reverse):
    er = e_re.reshape(1, NST)
    ei = e_im.reshape(1, NST)
    if reverse:
        ei = -ei
    pows = [(er, ei)]
    for _ in range(7):
        pr, pi_ = pows[-1]
        pows.append((pr * er - pi_ * ei, pr * ei + pi_ * er))
    row = jnp.arange(8)[:, None]
    out = []
    for s in (1, 2, 4):
        pr, pi_ = pows[s - 1]
        keep = (row + s <= 7) if reverse else (row >= s)
        out += [jnp.where(keep, pr, 0.0), jnp.where(keep, pi_, 0.0)]
    allr = jnp.concatenate([p[0] for p in pows], 0)
    alli = jnp.concatenate([p[1] for p in pows], 0)
    if reverse:
        allr, alli = allr[::-1], alli[::-1]
    out += [allr, alli]
    return jnp.concatenate(out, 0).astype(F32)


def _block_diag_mats(bb_re, bb_im, c_re, c_im):
    eye = jnp.eye(G, dtype=F32)
    bre = jnp.einsum("gph,gk->ghkp", bb_re, eye).reshape(CW, NST)
    bim = jnp.einsum("gph,gk->ghkp", bb_im, eye).reshape(CW, NST)
    bb = jnp.concatenate([bre, bim], 1)
    cre = jnp.einsum("ghp,gk->gpkh", c_re, eye).reshape(NST, CW)
    cim = jnp.einsum("ghp,gk->gpkh", c_im, eye).reshape(NST, CW)
    cm = jnp.concatenate([cre, -cim], 0)
    return bb, cm


def _diag_blocks(full):
    return jnp.einsum("ghkp,gk->ghp", full.reshape(G, H, G, P), jnp.eye(G, dtype=F32))


def _merge_fwd(z, zz, y_conv):
    S = z.shape[0]
    tm = _row_tile(S)
    D = D_MODEL

    def body(glc_ref, gls_ref, za_ref, zb_ref, yc_ref, m_ref):
        y_ssm = za_ref[...] * _sig(zb_ref[...])
        m_ref[...] = (_sig(glc_ref[...]) * yc_ref[...] + _sig(gls_ref[...]) * y_ssm).astype(BF16)

    return pl.pallas_call(
        body, name="merge_fwd", out_shape=jax.ShapeDtypeStruct((S, D), BF16), grid=(S // tm, 2),
        in_specs=[pl.BlockSpec((tm, CW), lambda i, j: (i, 3 + j)), pl.BlockSpec((tm, CW), lambda i, j: (i, 5 + j)),
                  pl.BlockSpec((tm, CW), lambda i, j: (i, j)), pl.BlockSpec((tm, CW), lambda i, j: (i, 2 + j)),
                  pl.BlockSpec((tm, CW), lambda i, j: (i, j))],
        out_specs=pl.BlockSpec((tm, CW), lambda i, j: (i, j)),
        compiler_params=_cp(("parallel", "parallel")))(z, z, zz, zz, y_conv)


def _merge_bwd(dm, z, zz, y_conv):
    S = z.shape[0]
    tm = min(256, S)
    D = D_MODEL

    def body(dm_ref, glc0_ref, glc1_ref, gls0_ref, gls1_ref, za_ref, zb_ref, yc_ref, dyc_ref, dgl_ref, dzz_ref):
        for half, (glc_ref, gls_ref) in enumerate(((glc0_ref, gls0_ref), (glc1_ref, gls1_ref))):
            lo, hi = half * CW, (half + 1) * CW
            dm_v = dm_ref[:, lo:hi]
            sgc = _sig(glc_ref[...])
            sgs = _sig(gls_ref[...])
            szb = _sig(zb_ref[:, lo:hi])
            za = za_ref[:, lo:hi]
            dyc_ref[:, lo:hi] = (dm_v * sgc).astype(BF16)
            dgl_ref[:, lo:hi] = (dm_v * yc_ref[:, lo:hi] * sgc * (1.0 - sgc)).astype(BF16)
            dys = dm_v * sgs
            dgl_ref[:, D + lo:D + hi] = (dys * (za * szb) * (1.0 - sgs)).astype(BF16)
            dzz_ref[:, lo:hi] = (dys * szb).astype(BF16)
            dzz_ref[:, D + lo:D + hi] = (dys * za * szb * (1.0 - szb)).astype(BF16)

    zb_ = lambda j: pl.BlockSpec((tm, CW), lambda i: (i, j))
    wide = lambda j: pl.BlockSpec((tm, D), lambda i: (i, j))
    return pl.pallas_call(
        body, name="merge_bwd",
        out_shape=(jax.ShapeDtypeStruct((S, D), BF16), jax.ShapeDtypeStruct((S, 2 * D), BF16),
                   jax.ShapeDtypeStruct((S, 2 * D), BF16)),
        grid=(S // tm,),
        in_specs=[wide(0), zb_(3), zb_(4), zb_(5), zb_(6), wide(0), wide(1), wide(0)],
        out_specs=(wide(0), pl.BlockSpec((tm, 2 * D), lambda i: (i, 0)), pl.BlockSpec((tm, 2 * D), lambda i: (i, 0))),
        compiler_params=_cp(("parallel",)))(dm, z, z, z, z, zz, zz, y_conv)


def _ffn_act(f):
    S = f.shape[0]
    tm = _row_tile(S)
    tn = 1408

    def body(g_ref, u_ref, a_ref):
        gv = g_ref[...]
        a_ref[...] = (gv * _sig(gv) * u_ref[...]).astype(BF16)

    return pl.pallas_call(
        body, name="ffn_act", out_shape=jax.ShapeDtypeStruct((S, FH), BF16), grid=(S // tm, FH // tn),
        in_specs=[pl.BlockSpec((tm, tn), lambda i, j: (i, j)), pl.BlockSpec((tm, tn), lambda i, j: (i, j + FH // tn))],
        out_specs=pl.BlockSpec((tm, tn), lambda i, j: (i, j)),
        compiler_params=_cp(("parallel", "parallel")))(f, f)


def _ffn_bwd(f, dact):
    S = f.shape[0]
    tm = _row_tile(S)
    tn = 1408
    nb = FH // tn

    def body(g_ref, u_ref, d_ref, dg_ref, du_ref):
        gv = g_ref[...]
        sg = _sig(gv)
        dv = d_ref[...]
        dg_ref[...] = (dv * u_ref[...] * (sg * (1.0 + gv * (1.0 - sg)))).astype(BF16)
        du_ref[...] = (dv * gv * sg).astype(BF16)

    lo = pl.BlockSpec((tm, tn), lambda i, j: (i, j))
    hi = pl.BlockSpec((tm, tn), lambda i, j: (i, j + nb))
    return pl.pallas_call(
        body, name="ffn_bwd",
        out_shape=(jax.ShapeDtypeStruct((S, FH), BF16), jax.ShapeDtypeStruct((S, FH), BF16)),
        grid=(S // tm, nb), in_specs=[lo, hi, lo], out_specs=(lo, lo),
        compiler_params=_cp(("parallel", "parallel")))(f, f, dact)


def _final(x2, o2, g2, fg, tgt):
    S, D = x2.shape
    tm = _row_tile(S)

    def body(x2_ref, o2_ref, g2_ref, fg_ref, t_ref, dx3_ref, do2_ref, ls_ref, dfg_ref, dg2_ref):
        i = pl.program_id(0)
        o2 = o2_ref[...]
        x3 = x2_ref[...] + g2_ref[...] * o2
        r = lax.rsqrt(jnp.mean(x3 * x3, axis=-1, keepdims=True) + EPS)
        xn = x3 * r
        err = xn * fg_ref[...] - t_ref[...]
        dy = err * (1.0 / D)
        dxn = dy * fg_ref[...]
        dx3 = r * (dxn - xn * jnp.mean(dxn * xn, axis=-1, keepdims=True))
        dx3_ref[...] = dx3
        do2_ref[...] = (dx3 * g2_ref[...]).astype(BF16)

        @pl.when(i == 0)
        def _():
            ls_ref[...] = jnp.zeros_like(ls_ref)
            dfg_ref[...] = jnp.zeros_like(dfg_ref)
            dg2_ref[...] = jnp.zeros_like(dg2_ref)

        e2 = _colsum8(err * err)
        lanes = e2[:, 0:128]
        for q in range(1, D // 128):
            lanes = lanes + e2[:, q * 128:(q + 1) * 128]
        ls_ref[...] += lanes * (0.5 / D)
        dfg_ref[...] += _colsum8(dy * xn)
        dg2_ref[...] += _colsum8(dx3 * o2)

    row = pl.BlockSpec((tm, D), lambda i: (i, 0))
    par = _full((1, D))
    return pl.pallas_call(
        body, name="final_loss",
        out_shape=(jax.ShapeDtypeStruct((S, D), F32), jax.ShapeDtypeStruct((S, D), BF16),
                   jax.ShapeDtypeStruct((8, 128), F32), jax.ShapeDtypeStruct((8, D), F32),
                   jax.ShapeDtypeStruct((8, D), F32)),
        grid=(S // tm,), in_specs=[row, row, par, par, row],
        out_specs=(row, row, _full((8, 128)), _full((8, D)), _full((8, D))),
        compiler_params=_cp(("arbitrary",)))(x2, o2, g2, fg, tgt)


def _normmod_bwd(dh, xin, dres, g, sc, gate, o, name):
    S, D = xin.shape
    tm = _row_tile(S)

    def body(dh_ref, x_ref, dr_ref, g_ref, sc_ref, gate_ref, o_ref, dx_ref, do_ref, dsh_ref, dsc_ref, dg_ref, dgate_ref):
        i = pl.program_id(0)
        xv = x_ref[...]
        r = lax.rsqrt(jnp.mean(xv * xv, axis=-1, keepdims=True) + EPS)
        xn = xv * r
        dh_v = dh_ref[...]
        gv = g_ref[...]
        scale = 1.0 + sc_ref[...]
        dxn = dh_v * (gv * scale)
        dx = dr_ref[...] + r * (dxn - xn * jnp.mean(dxn * xn, axis=-1, keepdims=True))
        dx_ref[...] = dx
        do_ref[...] = (dx * gate_ref[...]).astype(BF16)

        @pl.when(i == 0)
        def _():
            dsh_ref[...] = jnp.zeros_like(dsh_ref)
            dsc_ref[...] = jnp.zeros_like(dsc_ref)
            dg_ref[...] = jnp.zeros_like(dg_ref)
            dgate_ref[...] = jnp.zeros_like(dgate_ref)

        hx = dh_v * xn
        dsh_ref[...] += _colsum8(dh_v)
        dsc_ref[...] += _colsum8(hx) * gv
        dg_ref[...] += _colsum8(hx) * scale
        dgate_ref[...] += _colsum8(dx * o_ref[...])

    row = pl.BlockSpec((tm, D), lambda i: (i, 0))
    par = _full((1, D))
    acc = jax.ShapeDtypeStruct((8, D), F32)
    return pl.pallas_call(
        body, name=name,
        out_shape=(jax.ShapeDtypeStruct((S, D), F32), jax.ShapeDtypeStruct((S, D), BF16), acc, acc, acc, acc),
        grid=(S // tm,), in_specs=[row, row, row, par, par, par, row],
        out_specs=(row, row, _full((8, D)), _full((8, D)), _full((8, D)), _full((8, D))),
        compiler_params=_cp(("arbitrary",)))(dh, xin, dres, g, sc, gate, o)


def _me():
    return lax.axis_index("x"), lax.axis_index("y"), lax.axis_index("c")


def _allgather8(v, name):
    R, C = v.shape

    def body(v_ref, out_ref, send_sems, recv_sems, local_sem):
        x, y, c = _me()
        mine = pltpu.make_async_copy(v_ref, out_ref.at[4 * x + 2 * y + c], local_sem)
        mine.start()
        copies = []
        for k in range(1, N_DEV):
            fx, fy, fc = (k >> 2) & 1, (k >> 1) & 1, k & 1
            peer = (x ^ fx, y ^ fy, c ^ fc)
            copies.append(pltpu.make_async_remote_copy(
                src_ref=v_ref, dst_ref=out_ref.at[4 * x + 2 * y + c],
                send_sem=send_sems.at[k - 1], recv_sem=recv_sems.at[k - 1],
                device_id=peer, device_id_type=MESH))
        for cp in copies:
            cp.start()
        for k in range(1, N_DEV):
            fx, fy, fc = (k >> 2) & 1, (k >> 1) & 1, k & 1
            src_slot = 4 * (x ^ fx) + 2 * (y ^ fy) + (c ^ fc)
            pltpu.make_async_remote_copy(
                src_ref=v_ref, dst_ref=out_ref.at[src_slot],
                send_sem=send_sems.at[k - 1], recv_sem=recv_sems.at[k - 1],
                device_id=(x ^ fx, y ^ fy, c ^ fc), device_id_type=MESH).wait_recv()
        for cp in copies:
            cp.wait_send()
        mine.wait()

    return pl.pallas_call(
        body, name=name, out_shape=jax.ShapeDtypeStruct((N_DEV, R, C), v.dtype),
        in_specs=[pl.BlockSpec(memory_space=pltpu.VMEM)], out_specs=pl.BlockSpec(memory_space=pltpu.VMEM),
        scratch_shapes=[pltpu.SemaphoreType.DMA((N_DEV - 1,)), pltpu.SemaphoreType.DMA((N_DEV - 1,)),
                        pltpu.SemaphoreType.DMA],
        compiler_params=pltpu.CompilerParams(vmem_limit_bytes=VMEM_LIMIT))(v)


def _gather_weights(shards, axes):
    nw = len(shards)
    out_shapes = []
    for s, ax in zip(shards, axes):
        shp = list(s.shape)
        shp[ax] *= N_CHIP
        out_shapes.append(jax.ShapeDtypeStruct(tuple(shp), s.dtype))

    def body(*refs):
        ins, outs = refs[:nw], refs[nw:2 * nw]
        send_sems, recv_sems, local_sems = refs[2 * nw:]
        x, y, c = _me()
        my_chip = 2 * x + y

        def part(w, chip):
            n = ins[w].shape[axes[w]]
            start = pl.multiple_of(chip * n, 8)
            if axes[w] == 0:
                return outs[w].at[pl.ds(start, n), :]
            return outs[w].at[:, pl.ds(start, n)]

        local = [pltpu.make_async_copy(ins[w], part(w, my_chip), local_sems.at[w]) for w in range(nw)]
        for cp in local:
            cp.start()
        sends = []
        for w in range(nw):
            for k in range(1, N_CHIP):
                fx, fy = (k >> 1) & 1, k & 1
                sends.append(pltpu.make_async_remote_copy(
                    src_ref=ins[w], dst_ref=part(w, my_chip),
                    send_sem=send_sems.at[w, k - 1], recv_sem=recv_sems.at[w, k - 1],
                    device_id=(x ^ fx, y ^ fy, c), device_id_type=MESH))
        for cp in sends:
            cp.start()
        for w in range(nw):
            for k in range(1, N_CHIP):
                fx, fy = (k >> 1) & 1, k & 1
                src_chip = 2 * (x ^ fx) + (y ^ fy)
                pltpu.make_async_remote_copy(
                    src_ref=ins[w], dst_ref=part(w, src_chip),
                    send_sem=send_sems.at[w, k - 1], recv_sem=recv_sems.at[w, k - 1],
                    device_id=(x ^ fx, y ^ fy, c), device_id_type=MESH).wait_recv()
        for cp in sends:
            cp.wait_send()
        for cp in local:
            cp.wait()

    hbm = pl.BlockSpec(memory_space=pltpu.HBM)
    return pl.pallas_call(
        body, name="gather_weights", out_shape=tuple(out_shapes),
        in_specs=[hbm] * nw, out_specs=tuple([hbm] * nw),
        scratch_shapes=[pltpu.SemaphoreType.DMA((nw, N_CHIP - 1)), pltpu.SemaphoreType.DMA((nw, N_CHIP - 1)),
                        pltpu.SemaphoreType.DMA((nw,))],
        compiler_params=pltpu.CompilerParams(vmem_limit_bytes=VMEM_LIMIT))(*shards)


def _scatter_grads(grads, axes):
    nw = len(grads)
    out_shapes = []
    for g, ax in zip(grads, axes):
        shp = list(g.shape)
        shp[ax] //= N_CHIP
        out_shapes.append(jax.ShapeDtypeStruct((N_CHIP,) + tuple(shp), g.dtype))

    def body(*refs):
        ins, outs = refs[:nw], refs[nw:2 * nw]
        send_sems, recv_sems, local_sems = refs[2 * nw:]
        x, y, c = _me()
        my_chip = 2 * x + y

        def part(w, chip):
            n = ins[w].shape[axes[w]] // N_CHIP
            start = pl.multiple_of(chip * n, 8)
            if axes[w] == 0:
                return ins[w].at[pl.ds(start, n), :]
            return ins[w].at[:, pl.ds(start, n)]

        local = [pltpu.make_async_copy(part(w, my_chip), outs[w].at[my_chip], local_sems.at[w]) for w in range(nw)]
        for cp in local:
            cp.start()
        sends = []
        for w in range(nw):
            for k in range(1, N_CHIP):
                fx, fy = (k >> 1) & 1, k & 1
                dst_chip = 2 * (x ^ fx) + (y ^ fy)
                sends.append(pltpu.make_async_remote_copy(
                    src_ref=part(w, dst_chip), dst_ref=outs[w].at[my_chip],
                    send_sem=send_sems.at[w, k - 1], recv_sem=recv_sems.at[w, k - 1],
                    device_id=(x ^ fx, y ^ fy, c), device_id_type=MESH))
        for cp in sends:
            cp.start()
        for w in range(nw):
            for k in range(1, N_CHIP):
                fx, fy = (k >> 1) & 1, k & 1
                src_chip = 2 * (x ^ fx) + (y ^ fy)
                pltpu.make_async_remote_copy(
                    src_ref=part(w, my_chip), dst_ref=outs[w].at[src_chip],
                    send_sem=send_sems.at[w, k - 1], recv_sem=recv_sems.at[w, k - 1],
                    device_id=(x ^ fx, y ^ fy, c), device_id_type=MESH).wait_recv()
        for cp in sends:
            cp.wait_send()
        for cp in local:
            cp.wait()

    hbm = pl.BlockSpec(memory_space=pltpu.HBM)
    return pl.pallas_call(
        body, name="scatter_grads", out_shape=tuple(out_shapes),
        in_specs=[hbm] * nw, out_specs=tuple([hbm] * nw),
        scratch_shapes=[pltpu.SemaphoreType.DMA((nw, N_CHIP - 1)), pltpu.SemaphoreType.DMA((nw, N_CHIP - 1)),
                        pltpu.SemaphoreType.DMA((nw,))],
        compiler_params=pltpu.CompilerParams(vmem_limit_bytes=VMEM_LIMIT))(*grads)


def _swap_sibling(arrs):
    nw = len(arrs)

    def body(*refs):
        ins, outs = refs[:nw], refs[nw:2 * nw]
        send_sems, recv_sems = refs[2 * nw:]
        x, y, c = _me()
        copies = [pltpu.make_async_remote_copy(
            src_ref=ins[w], dst_ref=outs[w], send_sem=send_sems.at[w], recv_sem=recv_sems.at[w],
            device_id=(x, y, 1 - c), device_id_type=MESH) for w in range(nw)]
        for cp in copies:
            cp.start()
        for cp in copies:
            cp.wait_recv()
        for cp in copies:
            cp.wait_send()

    hbm = pl.BlockSpec(memory_space=pltpu.HBM)
    return pl.pallas_call(
        body, name="swap_sibling", out_shape=tuple(jax.ShapeDtypeStruct(a.shape, a.dtype) for a in arrs),
        in_specs=[hbm] * nw, out_specs=tuple([hbm] * nw),
        scratch_shapes=[pltpu.SemaphoreType.DMA((nw,)), pltpu.SemaphoreType.DMA((nw,))],
        compiler_params=pltpu.CompilerParams(vmem_limit_bytes=VMEM_LIMIT))(*arrs)


_C1 = 1.0 - B1 ** STEP
_C2 = 1.0 - B2 ** STEP


def _adam_math(w, g, m, v):
    m = B1 * m + (1.0 - B1) * g
    v = B2 * v + (1.0 - B2) * (g * g)
    delta = -LR * ((m / _C1) / (jnp.sqrt(v / _C2) + AEPS) + WD * w)
    return delta, m, v


def _adamw(w, m, v, gparts, name):
    R, C = w.shape
    tr = R if R <= 256 else (128 if R % 128 == 0 else 176)
    assert R % tr == 0, (name, R)
    ng = len(gparts)

    def body(*refs):
        w_ref, m_ref, v_ref = refs[:3]
        g_refs = refs[3:3 + ng]
        g_out, d_out, m_out, v_out = refs[3 + ng:]
        g = None
        for r in g_refs:
            if len(r.shape) == 3:
                s = r[0].astype(F32)
                for q in range(1, r.shape[0]):
                    s = s + r[q].astype(F32)
            else:
                s = r[...].astype(F32)
            g = s if g is None else g + s
        delta, mn, vn = _adam_math(w_ref[...], g, m_ref[...], v_ref[...])
        g_out[...] = g
        d_out[...] = delta
        m_out[...] = mn
        v_out[...] = vn

    blk = pl.BlockSpec((tr, C), lambda i: (i, 0))
    g_specs = [blk if p.ndim == 2 else pl.BlockSpec((p.shape[0], tr, C), lambda i: (0, i, 0)) for p in gparts]
    sds = jax.ShapeDtypeStruct((R, C), F32)
    return pl.pallas_call(
        body, name=name, out_shape=(sds, sds, sds, sds), grid=(R // tr,),
        in_specs=[blk, blk, blk] + g_specs, out_specs=(blk, blk, blk, blk),
        compiler_params=_cp(("parallel",)))(w, m, v, *gparts)


def _mod_shard(c_all, w_ada, b_ada_cols):
    n = w_ada.shape[1]
    tn = 512

    def body(c_ref, w_ref, b_ref, o_ref):
        cv = c_ref[...]
        ca = (cv * _sig(cv)).astype(BF16)
        o_ref[...] = jnp.dot(ca, w_ref[...].astype(BF16), preferred_element_type=F32) + b_ref[...]

    return pl.pallas_call(
        body, name="mod_shard", out_shape=jax.ShapeDtypeStruct((N_DEV, n), F32), grid=(n // tn,),
        in_specs=[_full((N_DEV, D_MODEL)), pl.BlockSpec((D_MODEL, tn), lambda j: (0, j)),
                  pl.BlockSpec((1, tn), lambda j: (0, j))],
        out_specs=pl.BlockSpec((N_DEV, tn), lambda j: (0, j)),
        compiler_params=_cp(("parallel",)))(c_all, w_ada, b_ada_cols)


def _ada_grad(c_all, dmod_cols):
    n = dmod_cols.shape[1]
    tn = 512

    def body(c_ref, d_ref, o_ref):
        cv = c_ref[...]
        ca = cv * _sig(cv)
        o_ref[...] = lax.dot_general(ca, d_ref[...], (((0,), (0,)), ((), ())),
                                     preferred_element_type=F32, precision=lax.Precision.HIGHEST)

    return pl.pallas_call(
        body, name="ada_grad", out_shape=jax.ShapeDtypeStruct((D_MODEL, n), F32), grid=(n // tn,),
        in_specs=[_full((N_DEV, D_MODEL)), pl.BlockSpec((N_DEV, tn), lambda j: (0, j))],
        out_specs=pl.BlockSpec((D_MODEL, tn), lambda j: (0, j)),
        compiler_params=_cp(("parallel",)))(c_all, dmod_cols)


def _device_step(x, mod, W, tgt):
    S = x.shape[0]
    sh1, sc1, g1, sh2, sc2, g2 = [mod[:, i * D_MODEL:(i + 1) * D_MODEL] for i in range(6)]
    e_re, e_im, bb_re, bb_im = _ssm_prep(W["ssm_a_re"], W["ssm_a_im"], W["ssm_b_re"], W["ssm_b_im"], W["ssm_log_dt"])
    bb, cm = _block_diag_mats(bb_re, bb_im, W["ssm_c_re"], W["ssm_c_im"])
    bb16, cm16 = bb.astype(BF16), cm.astype(BF16)
    tab_f = _scan_tables(e_re, e_im, False)
    tab_b = _scan_tables(e_re, e_im, True)

    h1 = _normmod(x, W["norm1_g"], sc1, sh1, "normmod1")
    z = _matmul(h1, W["w_in"], "nn", 512, 512, 1024, F32, "mm_w_in")
    yc, scv = _conv_fwd(z, W["conv_w"], W["conv_b"], W["conv_ln_g"], W["conv_ln_b"])
    y_conv = _matmul(scv, W["conv_proj"], "nn", 512, 1024, 512, F32, "mm_conv_proj")
    xs, ys, yg = _ssm_fwd(z, bb16, cm16, W["ssm_d"], tab_f)
    zz = _matmul(yg, W["ssm_glu"], "nn", 512, 1024, 512, F32, "mm_ssm_glu")
    merged = _merge_fwd(z, zz, y_conv)
    o = _matmul(merged, W["w_out"], "nn", 512, 1024, 1024, F32, "mm_w_out")
    x2, h2 = _resid_normmod(x, o, g1, W["norm2_g"], sc2, sh2, "resid_normmod2")
    f = _matmul(h2, W["w_ffn_in"], "nn", 512, 1408, 1024, F32, "mm_ffn_in")
    act = _ffn_act(f)
    o2 = _matmul(act, W["w_ffn_out"], "nn", 512, 1024, FH, F32, "mm_ffn_out")
    dx3, do2, loss8, dfg8, dg2_8 = _final(x2, o2, g2, W["final_g"], tgt)

    gb = {}
    sm = {}
    dact = _matmul(do2, W["w_ffn_out"], "nt", 512, 1408, 1024, F32, "mm_d_act")
    gb["w_ffn_out"] = _matmul(act, do2, "tn", 1408, 1024, 1024, BF16, "mm_g_ffn_out")
    dfg, dfu = _ffn_bwd(f, dact)
    df = jnp.concatenate([dfg, dfu], axis=1)
    dh2 = _matmul(df, W["w_ffn_in"], "nt", 512, 1024, 1408, F32, "mm_d_h2")
    gb["w_ffn_in"] = _matmul(h2, df, "tn", 1024, 1408, 1024, BF16, "mm_g_ffn_in")
    dx2, do, dsh2, dsc2, dn2, dg1_8 = _normmod_bwd(dh2, x2, dx3, W["norm2_g"], sc2, g1, o, "normmod2_bwd")
    dmerged = _matmul(do, W["w_out"], "nt", 512, 1024, 1024, F32, "mm_d_merged")
    gb["w_out"] = _matmul(merged, do, "tn", 1024, 1024, 1024, BF16, "mm_g_w_out")
    dyconv, dgl, dzz = _merge_bwd(dmerged, z, zz, y_conv)
    dyg = _matmul(dzz, W["ssm_glu"], "nt", 512, 512, 1024, F32, "mm_d_yg")
    gb["ssm_glu"] = _matmul(yg, dzz, "tn", 512, 1024, 1024, BF16, "mm_g_ssm_glu")
    lam, du, dys16, de16, dd8 = _ssm_bwd(dyg, ys, z, xs, cm16.T, bb16.T, W["ssm_d"], tab_b)
    dc_full = _matmul(dys16, xs, "tn", 512, 1024, 1024, F32, "mm_g_ssm_c")
    u = z[:, 2 * CW:3 * CW]
    dbb_full = _matmul(u, lam, "tn", 512, 1024, 1024, F32, "mm_g_ssm_b")
    dsc = _matmul(dyconv, W["conv_proj"], "nt", 512, 512, 1024, F32, "mm_d_sc")
    gb["conv_proj"] = _matmul(scv, dyconv, "tn", 512, 1024, 1024, BF16, "mm_g_conv_proj")
    dyc, dlg8, dlb8, dcb8 = _conv_bwd_ln(dsc, yc, W["conv_ln_g"], W["conv_ln_b"])
    dz_conv, dcw = _conv_bwd(dyc, z, W["conv_w"])
    dz = jnp.concatenate([dz_conv, du, dgl], axis=1)
    dh1 = _matmul(dz, W["w_in"], "nt", 512, 1024, 1792, F32, "mm_d_h1")
    gb["w_in"] = _matmul(h1, dz, "tn", 1024, 512, 1024, BF16, "mm_g_w_in")
    dx, _, dsh1, dsc1, dn1, _ = _normmod_bwd(dh1, x, dx2, W["norm1_g"], sc1, g1, o, "normmod1_bwd")

    s8 = lambda a: jnp.sum(a, axis=0, keepdims=True)
    de = de16.reshape(2, 8, NST).sum(1)
    de_re, de_im = de[0].reshape(G, P), de[1].reshape(G, P)
    dc_re = _diag_blocks(dc_full[:, :NST])
    dc_im = -_diag_blocks(dc_full[:, NST:])
    dbb_re = jnp.swapaxes(_diag_blocks(dbb_full[:, :NST]), 1, 2)
    dbb_im = jnp.swapaxes(_diag_blocks(dbb_full[:, NST:]), 1, 2)
    _, vjp = jax.vjp(_ssm_prep, W["ssm_a_re"], W["ssm_a_im"], W["ssm_b_re"], W["ssm_b_im"], W["ssm_log_dt"])
    sm["ssm_a_re"], sm["ssm_a_im"], sm["ssm_b_re"], sm["ssm_b_im"], sm["ssm_log_dt"] = vjp((de_re, de_im, dbb_re, dbb_im))
    sm["ssm_c_re"], sm["ssm_c_im"] = dc_re, dc_im
    sm["ssm_d"] = s8(dd8)
    sm["norm1_g"], sm["norm2_g"] = s8(dn1), s8(dn2)
    sm["conv_b"], sm["conv_ln_g"], sm["conv_ln_b"] = s8(dcb8), s8(dlg8), s8(dlb8)
    sm["conv_w"] = dcw.reshape(KW, 8, CW).sum(1)
    sm["final_g"] = s8(dfg8)
    dmod = jnp.concatenate([s8(dsh1), s8(dsc1), s8(dg1_8), s8(dsh2), s8(dsc2), s8(dg2_8)], axis=1)
    return loss8, dx, gb, sm, dmod


_BIG = ("w_in", "conv_proj", "ssm_glu", "w_out", "w_ffn_in", "w_ffn_out")
_BIG_AXIS = {"w_in": 1, "conv_proj": 1, "ssm_glu": 1, "w_out": 0, "w_ffn_in": 1, "w_ffn_out": 0}
_SMALL = ("b_ada", "norm1_g", "conv_w", "conv_b", "conv_ln_g", "conv_ln_b", "ssm_a_re", "ssm_a_im", "ssm_b_re",
          "ssm_b_im", "ssm_c_re", "ssm_c_im", "ssm_d", "ssm_log_dt", "norm2_g", "final_g")
_ORDER = ("w_ada", "b_ada", "norm1_g", "w_in", "conv_w", "conv_b", "conv_ln_g", "conv_ln_b", "conv_proj",
          "ssm_a_re", "ssm_a_im", "ssm_b_re", "ssm_b_im", "ssm_c_re", "ssm_c_im", "ssm_d", "ssm_log_dt", "ssm_glu",
          "w_out", "norm2_g", "w_ffn_in", "w_ffn_out", "final_g")
_PACK_COLS = 1024


def _pack_rows(shape):
    return -(-int(np.prod(shape)) // (8 * _PACK_COLS)) * 8


def _pack(arrs):
    parts = []
    for a in arrs:
        flat = a.reshape(-1)
        n = _pack_rows(a.shape)
        parts.append(jnp.pad(flat, (0, n * _PACK_COLS - flat.shape[0])).reshape(n, _PACK_COLS))
    return jnp.concatenate(parts, 0)


def _unpack(packed, shapes):
    out, r = [], 0
    for shp in shapes:
        size = int(np.prod(shp))
        n = _pack_rows(shp)
        out.append(packed[r:r + n].reshape(-1)[:size].reshape(shp))
        r += n
    return out


def kernel(x, c, w_ada, b_ada, norm1_g, w_in, conv_w, conv_b, conv_ln_g, conv_ln_b, conv_proj, ssm_a_re, ssm_a_im, ssm_b_re, ssm_b_im, ssm_c_re, ssm_c_im, ssm_d, ssm_log_dt, ssm_glu, w_out, norm2_g, w_ffn_in, w_ffn_out, final_g, loss_target, m_w_ada, m_b_ada, m_norm1_g, m_w_in, m_conv_w, m_conv_b, m_conv_ln_g, m_conv_ln_b, m_conv_proj, m_ssm_a_re, m_ssm_a_im, m_ssm_b_re, m_ssm_b_im, m_ssm_c_re, m_ssm_c_im, m_ssm_d, m_ssm_log_dt, m_ssm_glu, m_w_out, m_norm2_g, m_w_ffn_in, m_w_ffn_out, m_final_g, v_w_ada, v_b_ada, v_norm1_g, v_w_in, v_conv_w, v_conv_b, v_conv_ln_g, v_conv_ln_b, v_conv_proj, v_ssm_a_re, v_ssm_a_im, v_ssm_b_re, v_ssm_b_im, v_ssm_c_re, v_ssm_c_im, v_ssm_d, v_ssm_log_dt, v_ssm_glu, v_w_out, v_norm2_g, v_w_ffn_in, v_w_ffn_out, v_final_g):
    given = dict(locals())
    mx, my, mc = _me()
    chip = 2 * mx + my
    dev = 4 * mx + 2 * my + mc
    def canon(a):
        return a.reshape(1, -1) if a.ndim <= 2 else a[0]

    wts = {n: canon(given[n]) for n in _ORDER}
    mom = {n: canon(given["m_" + n]) for n in _ORDER}
    var = {n: canon(given["v_" + n]) for n in _ORDER}

    c_all = _allgather8(jnp.broadcast_to(c, (8, D_MODEL)), "gather_c")[:, 0, :]
    n_ada = wts["w_ada"].shape[1]
    b_cols = lax.dynamic_slice(wts["b_ada"], (0, chip * n_ada), (1, n_ada))
    mod_cols = _mod_shard(c_all, wts["w_ada"], b_cols)
    mods = _allgather8(mod_cols, "gather_mod")
    mod = jnp.concatenate([lax.dynamic_index_in_dim(mods[2 * q], dev, 0, keepdims=True) for q in range(N_CHIP)], axis=1)

    full = _gather_weights([wts[n].astype(BF16) for n in _BIG], [_BIG_AXIS[n] for n in _BIG])
    W = dict(wts)
    for n, fw in zip(_BIG, full):
        W[n] = fw
    conv_w_full = _allgather8(jnp.pad(wts["conv_w"], ((0, 1), (0, 0))), "gather_conv_w")
    W["conv_w"] = jnp.concatenate([conv_w_full[2 * q, :KW] for q in range(N_CHIP)], axis=1)

    loss8, dx, gb, sm, dmod = _device_step(x[0], mod, W, loss_target[0])
    loss = lax.psum(jnp.sum(loss8), ("x", "y", "c"))

    recv = _scatter_grads([gb[n] for n in _BIG], [_BIG_AXIS[n] for n in _BIG])
    sib = _swap_sibling(recv)

    small_shapes = [sm[n].shape for n in _SMALL if n != "b_ada"]
    packed = _pack([sm[n] for n in _SMALL if n != "b_ada"] + [dmod])
    allp = _allgather8(packed, "gather_small")

    outs = {}
    for n, r_mine, r_sib in zip(_BIG, recv, sib):
        outs[n] = _adamw(wts[n], mom[n], var[n], [r_mine, r_sib], "adamw_" + n)

    off = sum(_pack_rows(s) for s in small_shapes)
    dmod_all = allp[:, off:off + _pack_rows((6 * D_MODEL,)), :].reshape(N_DEV, -1)[:, :6 * D_MODEL]
    dmod_cols = lax.dynamic_slice(dmod_all, (0, chip * n_ada), (N_DEV, n_ada))
    g_ada = _ada_grad(c_all, dmod_cols)
    outs["w_ada"] = _adamw(wts["w_ada"], mom["w_ada"], var["w_ada"], [g_ada], "adamw_w_ada")

    def small_w(d):
        arrs = []
        for n in _SMALL:
            if n == "b_ada":
                continue
            if n == "conv_w":
                arrs.append(jnp.zeros((KW, CW), F32))
            else:
                arrs.append(d[n])
        arrs.append(d["b_ada"])
        return _pack(arrs)

    res = _adamw(small_w(wts), small_w(mom), small_w(var), [allp], "adamw_small")
    small_out = [_unpack(r, small_shapes + [(1, 6 * D_MODEL)]) for r in res]
    names_small = [n for n in _SMALL if n != "b_ada"] + ["b_ada"]
    for idx, n in enumerate(names_small):
        outs[n] = tuple(small_out[q][idx] for q in range(4))
    g_cw = lax.dynamic_slice(outs["conv_w"][0], (0, chip * (CW // N_CHIP)), (KW, CW // N_CHIP))
    pad = lambda a: jnp.pad(a, ((0, 1), (0, 0)))
    r_cw = _adamw(pad(wts["conv_w"]), pad(mom["conv_w"]), pad(var["conv_w"]), [pad(g_cw)], "adamw_conv_w")
    outs["conv_w"] = tuple(r[:KW] for r in r_cw)

    def shaped(n, a):
        return a.reshape(given[n].shape)

    result = [loss, dx[None]]
    for q in range(4):
        result += [shaped(n, outs[n][q]) for n in _ORDER]
    return tuple(result)
```

```python
import math

import jax
import jax.numpy as jnp
import numpy as np
from jax import lax
from jax.experimental import pallas as pl
from jax.experimental.pallas import tpu as pltpu

F32 = jnp.float32
BF16 = jnp.bfloat16
EPS = 1e-6
D_MODEL = 1024
CW = 512
KW = 31
HALO = 32
G, P, H = 32, 64, 16
NST = G * P
FH = 2816
N_DEV = 8
N_CHIP = 4
VMEM_LIMIT = 56 * 1024 * 1024
LR, B1, B2, AEPS, WD, STEP = 0.001, 0.9, 0.999, 1e-08, 0.01, 10
MESH = pl.DeviceIdType.MESH


def _cp(sem=None):
    return pltpu.CompilerParams(dimension_semantics=sem, vmem_limit_bytes=VMEM_LIMIT)


def _sig(x):
    return jax.nn.sigmoid(x)


def _full(shape):
    return pl.BlockSpec(shape, lambda *_: (0,) * len(shape))


def _colsum8(v):
    t, c = v.shape
    return jnp.sum(v.reshape(t // 8, 8, c), axis=0)


def _matmul(a, b, mode, tm, tn, tk, out_dtype, name):
    if mode == "nn":
        (M, K), N = a.shape, b.shape[1]
    elif mode == "nt":
        (M, K), N = a.shape, b.shape[0]
    else:
        (K, M), N = a.shape, b.shape[1]
    tm, tn, tk = min(tm, M), min(tn, N), min(tk, K)
    assert M % tm == 0 and N % tn == 0 and K % tk == 0, (name, M, N, K, tm, tn, tk)
    nk = K // tk
    if mode == "nn":
        a_spec = pl.BlockSpec((tm, tk), lambda i, j, k: (i, k))
        b_spec = pl.BlockSpec((tk, tn), lambda i, j, k: (k, j))
        dims = (((1,), (0,)), ((), ()))
    elif mode == "nt":
        a_spec = pl.BlockSpec((tm, tk), lambda i, j, k: (i, k))
        b_spec = pl.BlockSpec((tn, tk), lambda i, j, k: (j, k))
        dims = (((1,), (1,)), ((), ()))
    else:
        a_spec = pl.BlockSpec((tk, tm), lambda i, j, k: (k, i))
        b_spec = pl.BlockSpec((tk, tn), lambda i, j, k: (k, j))
        dims = (((0,), (0,)), ((), ()))

    def body(a_ref, b_ref, o_ref, acc_ref):
        k = pl.program_id(2)
        part = lax.dot_general(a_ref[...].astype(BF16), b_ref[...].astype(BF16), dims,
                               preferred_element_type=F32)
        if nk == 1:
            o_ref[...] = part.astype(out_dtype)
        else:
            @pl.when(k == 0)
            def _():
                acc_ref[...] = part

            @pl.when(k > 0)
            def _():
                acc_ref[...] += part

            @pl.when(k == nk - 1)
            def _():
                o_ref[...] = acc_ref[...].astype(out_dtype)

    return pl.pallas_call(
        body, name=name,
        out_shape=jax.ShapeDtypeStruct((M, N), out_dtype),
        grid=(M // tm, N // tn, nk),
        in_specs=[a_spec, b_spec],
        out_specs=pl.BlockSpec((tm, tn), lambda i, j, k: (i, j)),
        scratch_shapes=[pltpu.VMEM((tm, tn) if nk > 1 else (8, 128), F32)],
        compiler_params=_cp(("parallel", "parallel", "arbitrary")),
    )(a, b)


def _row_tile(S):
    return min(512, S)


def _normmod(x, g, sc, sh, name):
    S, D = x.shape
    tm = _row_tile(S)

    def body(x_ref, g_ref, sc_ref, sh_ref, h_ref):
        xv = x_ref[...]
        r = lax.rsqrt(jnp.mean(xv * xv, axis=-1, keepdims=True) + EPS)
        h_ref[...] = (xv * r * (g_ref[...] * (1.0 + sc_ref[...])) + sh_ref[...]).astype(BF16)

    row = pl.BlockSpec((tm, D), lambda i: (i, 0))
    return pl.pallas_call(
        body, name=name, out_shape=jax.ShapeDtypeStruct((S, D), BF16), grid=(S // tm,),
        in_specs=[row, _full((1, D)), _full((1, D)), _full((1, D))], out_specs=row,
        compiler_params=_cp(("parallel",)))(x, g, sc, sh)


def _resid_normmod(x, o, g1, g, sc, sh, name):
    S, D = x.shape
    tm = _row_tile(S)

    def body(x_ref, o_ref, g1_ref, g_ref, sc_ref, sh_ref, x2_ref, h_ref):
        xv = x_ref[...] + g1_ref[...] * o_ref[...]
        x2_ref[...] = xv
        r = lax.rsqrt(jnp.mean(xv * xv, axis=-1, keepdims=True) + EPS)
        h_ref[...] = (xv * r * (g_ref[...] * (1.0 + sc_ref[...])) + sh_ref[...]).astype(BF16)

    row = pl.BlockSpec((tm, D), lambda i: (i, 0))
    par = _full((1, D))
    return pl.pallas_call(
        body, name=name,
        out_shape=(jax.ShapeDtypeStruct((S, D), F32), jax.ShapeDtypeStruct((S, D), BF16)),
        grid=(S // tm,), in_specs=[row, row, par, par, par, par], out_specs=(row, row),
        compiler_params=_cp(("parallel",)))(x, o, g1, g, sc, sh)


def _conv_fwd(z, conv_w, conv_b, ln_g, ln_b):
    S = z.shape[0]
    tm = min(128, S)
    sub = 32
    hb = tm // HALO

    def body(a_ref, g_ref, ha_ref, hg_ref, w_ref, b_ref, lg_ref, lb_ref, yc_ref, s_ref, ug_ref):
        i = pl.program_id(0)
        halo = ha_ref[...] * _sig(hg_ref[...])
        ug_ref[0:HALO, :] = jnp.where(i == 0, 0.0, halo)
        ug_ref[HALO:, :] = a_ref[...] * _sig(g_ref[...])
        for rb in range(tm // sub):
            acc = jnp.zeros((sub, CW), F32) + b_ref[...]
            for k in range(KW):
                off = rb * sub + HALO - (KW - 1) + k
                acc = acc + w_ref[k:k + 1, :] * ug_ref[off:off + sub, :]
            yc_ref[rb * sub:(rb + 1) * sub, :] = acc
            mu = jnp.mean(acc, axis=-1, keepdims=True)
            cen = acc - mu
            rstd = lax.rsqrt(jnp.mean(cen * cen, axis=-1, keepdims=True) + EPS)
            ln = cen * rstd * lg_ref[...] + lb_ref[...]
            s_ref[rb * sub:(rb + 1) * sub, :] = (ln * _sig(ln)).astype(BF16)

    prev = lambda i: (jnp.maximum(i * hb - 1, 0), 0)
    return pl.pallas_call(
        body, name="conv_fwd",
        out_shape=(jax.ShapeDtypeStruct((S, CW), F32), jax.ShapeDtypeStruct((S, CW), BF16)),
        grid=(S // tm,),
        in_specs=[pl.BlockSpec((tm, CW), lambda i: (i, 0)), pl.BlockSpec((tm, CW), lambda i: (i, 1)),
                  pl.BlockSpec((HALO, CW), prev), pl.BlockSpec((HALO, CW), lambda i: (jnp.maximum(i * hb - 1, 0), 1)),
                  _full((KW, CW)), _full((1, CW)), _full((1, CW)), _full((1, CW))],
        out_specs=(pl.BlockSpec((tm, CW), lambda i: (i, 0)), pl.BlockSpec((tm, CW), lambda i: (i, 0))),
        scratch_shapes=[pltpu.VMEM((tm + HALO, CW), F32)],
        compiler_params=_cp(("parallel",)))(z, z, z, z, conv_w, conv_b, ln_g, ln_b)


def _conv_bwd_ln(dsc, yc, ln_g, ln_b):
    S = yc.shape[0]
    tm = _row_tile(S)

    def body(d_ref, yc_ref, lg_ref, lb_ref, dyc_ref, dlg_ref, dlb_ref, dcb_ref):
        i = pl.program_id(0)
        yc_v = yc_ref[...]
        mu = jnp.mean(yc_v, axis=-1, keepdims=True)
        cen = yc_v - mu
        rstd = lax.rsqrt(jnp.mean(cen * cen, axis=-1, keepdims=True) + EPS)
        yn = cen * rstd
        ln = yn * lg_ref[...] + lb_ref[...]
        sl = _sig(ln)
        dln = d_ref[...] * (sl * (1.0 + ln * (1.0 - sl)))
        dyn = dln * lg_ref[...]
        dyc = rstd * (dyn - jnp.mean(dyn, axis=-1, keepdims=True)
                      - yn * jnp.mean(dyn * yn, axis=-1, keepdims=True))
        dyc_ref[...] = dyc

        @pl.when(i == 0)
        def _():
            dlg_ref[...] = jnp.zeros_like(dlg_ref)
            dlb_ref[...] = jnp.zeros_like(dlb_ref)
            dcb_ref[...] = jnp.zeros_like(dcb_ref)

        dlg_ref[...] += _colsum8(dln * yn)
        dlb_ref[...] += _colsum8(dln)
        dcb_ref[...] += _colsum8(dyc)

    row = pl.BlockSpec((tm, CW), lambda i: (i, 0))
    acc = jax.ShapeDtypeStruct((8, CW), F32)
    return pl.pallas_call(
        body, name="conv_bwd_ln",
        out_shape=(jax.ShapeDtypeStruct((S, CW), F32), acc, acc, acc), grid=(S // tm,),
        in_specs=[row, row, _full((1, CW)), _full((1, CW))],
        out_specs=(row, _full((8, CW)), _full((8, CW)), _full((8, CW))),
        compiler_params=_cp(("arbitrary",)))(dsc, yc, ln_g, ln_b)


def _conv_bwd(dyc, z, conv_w):
    S = z.shape[0]
    tm = min(128, S)
    sub = 32
    hb = tm // HALO
    nt = S // tm

    def body(d_ref, dn_ref, a_ref, g_ref, ha_ref, hg_ref, w_ref, dz_ref, dw_ref, ug_ref, dy_ref):
        i = pl.program_id(0)
        halo = ha_ref[...] * _sig(hg_ref[...])
        ug_ref[0:HALO, :] = jnp.where(i == 0, 0.0, halo)
        a = a_ref[...]
        sg = _sig(g_ref[...])
        ug_ref[HALO:, :] = a * sg
        dy_ref[0:tm, :] = d_ref[...]
        dy_ref[tm:, :] = jnp.where(i == nt - 1, 0.0, dn_ref[...])

        @pl.when(i == 0)
        def _():
            dw_ref[...] = jnp.zeros_like(dw_ref)

        for rb in range(tm // sub):
            r0 = rb * sub
            acc = jnp.zeros((sub, CW), F32)
            dyc_b = dy_ref[r0:r0 + sub, :]
            for k in range(KW):
                up = r0 + (KW - 1) - k
                acc = acc + w_ref[k:k + 1, :] * dy_ref[up:up + sub, :]
                off = r0 + HALO - (KW - 1) + k
                dw_ref[k * 8:(k + 1) * 8, :] += _colsum8(dyc_b * ug_ref[off:off + sub, :])
            a_b = a[r0:r0 + sub, :]
            sg_b = sg[r0:r0 + sub, :]
            dz_ref[r0:r0 + sub, 0:CW] = (acc * sg_b).astype(BF16)
            dz_ref[r0:r0 + sub, CW:2 * CW] = (acc * a_b * sg_b * (1.0 - sg_b)).astype(BF16)

    return pl.pallas_call(
        body, name="conv_bwd",
        out_shape=(jax.ShapeDtypeStruct((S, 2 * CW), BF16), jax.ShapeDtypeStruct((KW * 8, CW), F32)),
        grid=(nt,),
        in_specs=[pl.BlockSpec((tm, CW), lambda i: (i, 0)),
                  pl.BlockSpec((HALO, CW), lambda i: (jnp.minimum((i + 1) * hb, nt * hb - 1), 0)),
                  pl.BlockSpec((tm, CW), lambda i: (i, 0)), pl.BlockSpec((tm, CW), lambda i: (i, 1)),
                  pl.BlockSpec((HALO, CW), lambda i: (jnp.maximum(i * hb - 1, 0), 0)),
                  pl.BlockSpec((HALO, CW), lambda i: (jnp.maximum(i * hb - 1, 0), 1)),
                  _full((KW, CW))],
        out_specs=(pl.BlockSpec((tm, 2 * CW), lambda i: (i, 0)), _full((KW * 8, CW))),
        scratch_shapes=[pltpu.VMEM((tm + HALO, CW), F32), pltpu.VMEM((tm + HALO, CW), F32)],
        compiler_params=_cp(("arbitrary",)))(dyc, dyc, z, z, z, z, conv_w)


_GELU_C = math.sqrt(2.0 / math.pi)


def _gelu(x):
    return 0.5 * x * (1.0 + jnp.tanh(_GELU_C * (x + 0.044715 * x * x * x)))


def _gelu_grad(x):
    t = jnp.tanh(_GELU_C * (x + 0.044715 * x * x * x))
    return 0.5 * (1.0 + t) + 0.5 * x * (1.0 - t * t) * (_GELU_C * (1.0 + 3 * 0.044715 * x * x))


_LW = 512


def _ssm_fwd(z, bb, cm, d, tab):
    S = z.shape[0]
    tm = min(256, S)

    def body(u_ref, bb_ref, cm_ref, d_ref, t_ref, x_ref, ys_ref, yg_ref, car_ref):
        i = pl.program_id(0)

        @pl.when(i == 0)
        def _():
            car_ref[...] = jnp.zeros_like(car_ref)

        u = u_ref[...]
        x_ref[...] = jnp.dot(u.astype(BF16), bb_ref[...], preferred_element_type=F32)
        for c in range(NST // _LW):
            lre = pl.ds(c * _LW, _LW)
            lim = pl.ds(NST + c * _LW, _LW)

            def blk(j, car):
                cr, ci = car
                rows = pl.ds(pl.multiple_of(j * 8, 8), 8)
                r = x_ref[rows, lre]
                im = x_ref[rows, lim]
                for lvl, s in enumerate((1, 2, 4)):
                    mr = t_ref[16 * lvl:16 * lvl + 8, lre]
                    mi = t_ref[16 * lvl + 8:16 * lvl + 16, lre]
                    sr = pltpu.roll(r, s, 0)
                    si = pltpu.roll(im, s, 0)
                    r, im = r + (mr * sr - mi * si), im + (mr * si + mi * sr)
                pr = t_ref[48:56, lre]
                pi_ = t_ref[56:64, lre]
                r, im = r + (pr * cr - pi_ * ci), im + (pr * ci + pi_ * cr)
                x_ref[rows, lre] = r
                x_ref[rows, lim] = im
                return (jnp.broadcast_to(r[7:8, :], (8, _LW)), jnp.broadcast_to(im[7:8, :], (8, _LW)))

            cr, ci = lax.fori_loop(0, tm // 8, blk, (car_ref[:, lre], car_ref[:, lim]))
            car_ref[:, lre] = cr
            car_ref[:, lim] = ci
        ys = jnp.dot(x_ref[...].astype(BF16), cm_ref[...], preferred_element_type=F32) + d_ref[...] * u
        ys_ref[...] = ys
        yg_ref[...] = _gelu(ys).astype(BF16)

    return pl.pallas_call(
        body, name="ssm_fwd",
        out_shape=(jax.ShapeDtypeStruct((S, 2 * NST), F32), jax.ShapeDtypeStruct((S, CW), F32),
                   jax.ShapeDtypeStruct((S, CW), BF16)),
        grid=(S // tm,),
        in_specs=[pl.BlockSpec((tm, CW), lambda i: (i, 2)), _full((CW, 2 * NST)), _full((2 * NST, CW)),
                  _full((1, CW)), _full((64, NST))],
        out_specs=(pl.BlockSpec((tm, 2 * NST), lambda i: (i, 0)), pl.BlockSpec((tm, CW), lambda i: (i, 0)),
                   pl.BlockSpec((tm, CW), lambda i: (i, 0))),
        scratch_shapes=[pltpu.VMEM((8, 2 * NST), F32)],
        compiler_params=_cp(("arbitrary",)))(z, bb, cm, d, tab)


def _ssm_bwd(dyg, ys, z, xs, cmt, bbt, d, tab):
    S = z.shape[0]
    tm = min(256, S)
    nt = S // tm

    def body(dyg_ref, ys_ref, u_ref, x_ref, cmt_ref, bbt_ref, d_ref, t_ref,
             lam_ref, du_ref, dys_ref, de_ref, dd_ref, car_ref):
        i = pl.program_id(0)

        @pl.when(i == 0)
        def _():
            car_ref[...] = jnp.zeros_like(car_ref)
            de_ref[...] = jnp.zeros_like(de_ref)
            dd_ref[...] = jnp.zeros_like(dd_ref)

        u = u_ref[...]
        dys = dyg_ref[...] * _gelu_grad(ys_ref[...])
        dys_ref[...] = dys.astype(BF16)
        dd_ref[...] += _colsum8(dys * u)
        lam_ref[...] = jnp.dot(dys.astype(BF16), cmt_ref[...], preferred_element_type=F32)
        row = lax.broadcasted_iota(jnp.int32, (8, _LW), 0)
        for c in range(NST // _LW):
            lre = pl.ds(c * _LW, _LW)
            lim = pl.ds(NST + c * _LW, _LW)

            def blk(jj, car):
                cr, ci, ar, ai = car
                j = tm // 8 - 1 - jj
                rows = pl.ds(pl.multiple_of(j * 8, 8), 8)
                r = lam_ref[rows, lre]
                im = lam_ref[rows, lim]
                for lvl, s in enumerate((1, 2, 4)):
                    mr = t_ref[16 * lvl:16 * lvl + 8, lre]
                    mi = t_ref[16 * lvl + 8:16 * lvl + 16, lre]
                    sr = pltpu.roll(r, 8 - s, 0)
                    si = pltpu.roll(im, 8 - s, 0)
                    r, im = r + (mr * sr - mi * si), im + (mr * si + mi * sr)
                pr = t_ref[48:56, lre]
                pi_ = t_ref[56:64, lre]
                r, im = r + (pr * cr - pi_ * ci), im + (pr * ci + pi_ * cr)
                lam_ref[rows, lre] = r
                lam_ref[rows, lim] = im
                nr = jnp.where(row == 7, cr, pltpu.roll(r, 7, 0))
                ni = jnp.where(row == 7, ci, pltpu.roll(im, 7, 0))
                xr = x_ref[rows, lre]
                xi = x_ref[rows, lim]
                ar = ar + (nr * xr + ni * xi)
                ai = ai + (ni * xr - nr * xi)
                return (jnp.broadcast_to(r[0:1, :], (8, _LW)), jnp.broadcast_to(im[0:1, :], (8, _LW)), ar, ai)

            zero = jnp.zeros((8, _LW), F32)
            cr, ci, ar, ai = lax.fori_loop(0, tm // 8, blk, (car_ref[:, lre], car_ref[:, lim], zero, zero))
            car_ref[:, lre] = cr
            car_ref[:, lim] = ci
            de_ref[0:8, lre] += ar
            de_ref[8:16, lre] += ai
        du = jnp.dot(lam_ref[...].astype(BF16), bbt_ref[...], preferred_element_type=F32) + dys * d_ref[...]
        du_ref[...] = du.astype(BF16)

    rev = lambda i: (nt - 1 - i, 0)
    return pl.pallas_call(
        body, name="ssm_bwd",
        out_shape=(jax.ShapeDtypeStruct((S, 2 * NST), F32), jax.ShapeDtypeStruct((S, CW), BF16),
                   jax.ShapeDtypeStruct((S, CW), BF16), jax.ShapeDtypeStruct((16, NST), F32),
                   jax.ShapeDtypeStruct((8, CW), F32)),
        grid=(nt,),
        in_specs=[pl.BlockSpec((tm, CW), rev), pl.BlockSpec((tm, CW), rev),
                  pl.BlockSpec((tm, CW), lambda i: (nt - 1 - i, 2)), pl.BlockSpec((tm, 2 * NST), rev),
                  _full((CW, 2 * NST)), _full((2 * NST, CW)), _full((1, CW)), _full((64, NST))],
        out_specs=(pl.BlockSpec((tm, 2 * NST), rev), pl.BlockSpec((tm, CW), rev), pl.BlockSpec((tm, CW), rev),
                   _full((16, NST)), _full((8, CW))),
        scratch_shapes=[pltpu.VMEM((8, 2 * NST), F32)],
        compiler_params=_cp(("arbitrary",)))(dyg, ys, z, xs, cmt, bbt, d, tab)


def _ssm_prep(a_re, a_im, b_re, b_im, log_dt):
    dt = jnp.exp(log_dt.reshape(G))[:, None]
    mag = jnp.exp(dt * a_re)
    e_re, e_im = mag * jnp.cos(dt * a_im), mag * jnp.sin(dt * a_im)
    n_re, n_im = e_re - 1.0, e_im
    den = a_re * a_re + a_im * a_im
    q_re = (n_re * a_re + n_im * a_im) / den
    q_im = (n_im * a_re - n_re * a_im) / den
    bb_re = q_re[..., None] * b_re - q_im[..., None] * b_im
    bb_im = q_re[..., None] * b_im + q_im[..., None] * b_re
    return e_re, e_im, bb_re, bb_im


def _scan_tables(e_re, e_im, reverse):
    er = e_re.reshape(1, NST)
    ei = e_im.reshape(1, NST)
    if reverse:
        ei = -ei
    pows = [(er, ei)]
    for _ in range(7):
        pr, pi_ = pows[-1]
        pows.append((pr * er - pi_ * ei, pr * ei + pi_ * er))
    row = jnp.arange(8)[:, None]
    out = []
    for s in (1, 2, 4):
        pr, pi_ = pows[s - 1]
        keep = (row + s <= 7) if reverse else (row >= s)
        out += [jnp.where(keep, pr, 0.0), jnp.where(keep, pi_, 0.0)]
    allr = jnp.concatenate([p[0] for p in pows], 0)
    alli = jnp.concatenate([p[1] for p in pows], 0)
    if reverse:
        allr, alli = allr[::-1], alli[::-1]
    out += [allr, alli]
    return jnp.concatenate(out, 0).astype(F32)


def _block_diag_mats(bb_re, bb_im, c_re, c_im):
    eye = jnp.eye(G, dtype=F32)
    bre = jnp.einsum("gph,gk->ghkp", bb_re, eye).reshape(CW, NST)
    bim = jnp.einsum("gph,gk->ghkp", bb_im, eye).reshape(CW, NST)
    bb = jnp.concatenate([bre, bim], 1)
    cre = jnp.einsum("ghp,gk->gpkh", c_re, eye).reshape(NST, CW)
    cim = jnp.einsum("ghp,gk->gpkh", c_im, eye).reshape(NST, CW)
    cm = jnp.concatenate([cre, -cim], 0)
    return bb, cm


def _diag_blocks(full):
    return jnp.einsum("ghkp,gk->ghp", full.reshape(G, H, G, P), jnp.eye(G, dtype=F32))


def _merge_fwd(z, zz, y_conv):
    S = z.shape[0]
    tm = _row_tile(S)
    D = D_MODEL

    def body(glc_ref, gls_ref, za_ref, zb_ref, yc_ref, m_ref):
        y_ssm = za_ref[...] * _sig(zb_ref[...])
        m_ref[...] = (_sig(glc_ref[...]) * yc_ref[...] + _sig(gls_ref[...]) * y_ssm).astype(BF16)

    return pl.pallas_call(
        body, name="merge_fwd", out_shape=jax.ShapeDtypeStruct((S, D), BF16), grid=(S // tm, 2),
        in_specs=[pl.BlockSpec((tm, CW), lambda i, j: (i, 3 + j)), pl.BlockSpec((tm, CW), lambda i, j: (i, 5 + j)),
                  pl.BlockSpec((tm, CW), lambda i, j: (i, j)), pl.BlockSpec((tm, CW), lambda i, j: (i, 2 + j)),
                  pl.BlockSpec((tm, CW), lambda i, j: (i, j))],
        out_specs=pl.BlockSpec((tm, CW), lambda i, j: (i, j)),
        compiler_params=_cp(("parallel", "parallel")))(z, z, zz, zz, y_conv)


def _merge_bwd(dm, z, zz, y_conv):
    S = z.shape[0]
    tm = min(256, S)
    D = D_MODEL

    def body(dm_ref, glc0_ref, glc1_ref, gls0_ref, gls1_ref, za_ref, zb_ref, yc_ref, dyc_ref, dgl_ref, dzz_ref):
        for half, (glc_ref, gls_ref) in enumerate(((glc0_ref, gls0_ref), (glc1_ref, gls1_ref))):
            lo, hi = half * CW, (half + 1) * CW
            dm_v = dm_ref[:, lo:hi]
            sgc = _sig(glc_ref[...])
            sgs = _sig(gls_ref[...])
            szb = _sig(zb_ref[:, lo:hi])
            za = za_ref[:, lo:hi]
            dyc_ref[:, lo:hi] = (dm_v * sgc).astype(BF16)
            dgl_ref[:, lo:hi] = (dm_v * yc_ref[:, lo:hi] * sgc * (1.0 - sgc)).astype(BF16)
            dys = dm_v * sgs
            dgl_ref[:, D + lo:D + hi] = (dys * (za * szb) * (1.0 - sgs)).astype(BF16)
            dzz_ref[:, lo:hi] = (dys * szb).astype(BF16)
            dzz_ref[:, D + lo:D + hi] = (dys * za * szb * (1.0 - szb)).astype(BF16)

    zb_ = lambda j: pl.BlockSpec((tm, CW), lambda i: (i, j))
    wide = lambda j: pl.BlockSpec((tm, D), lambda i: (i, j))
    return pl.pallas_call(
        body, name="merge_bwd",
        out_shape=(jax.ShapeDtypeStruct((S, D), BF16), jax.ShapeDtypeStruct((S, 2 * D), BF16),
                   jax.ShapeDtypeStruct((S, 2 * D), BF16)),
        grid=(S // tm,),
        in_specs=[wide(0), zb_(3), zb_(4), zb_(5), zb_(6), wide(0), wide(1), wide(0)],
        out_specs=(wide(0), pl.BlockSpec((tm, 2 * D), lambda i: (i, 0)), pl.BlockSpec((tm, 2 * D), lambda i: (i, 0))),
        compiler_params=_cp(("parallel",)))(dm, z, z, z, z, zz, zz, y_conv)


def _ffn_act(f):
    S = f.shape[0]
    tm = _row_tile(S)
    tn = 1408

    def body(g_ref, u_ref, a_ref):
        gv = g_ref[...]
        a_ref[...] = (gv * _sig(gv) * u_ref[...]).astype(BF16)

    return pl.pallas_call(
        body, name="ffn_act", out_shape=jax.ShapeDtypeStruct((S, FH), BF16), grid=(S // tm, FH // tn),
        in_specs=[pl.BlockSpec((tm, tn), lambda i, j: (i, j)), pl.BlockSpec((tm, tn), lambda i, j: (i, j + FH // tn))],
        out_specs=pl.BlockSpec((tm, tn), lambda i, j: (i, j)),
        compiler_params=_cp(("parallel", "parallel")))(f, f)


def _ffn_bwd(f, dact):
    S = f.shape[0]
    tm = _row_tile(S)
    tn = 1408
    nb = FH // tn

    def body(g_ref, u_ref, d_ref, dg_ref, du_ref):
        gv = g_ref[...]
        sg = _sig(gv)
        dv = d_ref[...]
        dg_ref[...] = (dv * u_ref[...] * (sg * (1.0 + gv * (1.0 - sg)))).astype(BF16)
        du_ref[...] = (dv * gv * sg).astype(BF16)

    lo = pl.BlockSpec((tm, tn), lambda i, j: (i, j))
    hi = pl.BlockSpec((tm, tn), lambda i, j: (i, j + nb))
    return pl.pallas_call(
        body, name="ffn_bwd",
        out_shape=(jax.ShapeDtypeStruct((S, FH), BF16), jax.ShapeDtypeStruct((S, FH), BF16)),
        grid=(S // tm, nb), in_specs=[lo, hi, lo], out_specs=(lo, lo),
        compiler_params=_cp(("parallel", "parallel")))(f, f, dact)


def _final(x2, o2, g2, fg, tgt):
    S, D = x2.shape
    tm = _row_tile(S)

    def body(x2_ref, o2_ref, g2_ref, fg_ref, t_ref, dx3_ref, do2_ref, ls_ref, dfg_ref, dg2_ref):
        i = pl.program_id(0)
        o2 = o2_ref[...]
        x3 = x2_ref[...] + g2_ref[...] * o2
        r = lax.rsqrt(jnp.mean(x3 * x3, axis=-1, keepdims=True) + EPS)
        xn = x3 * r
        err = xn * fg_ref[...] - t_ref[...]
        dy = err * (1.0 / D)
        dxn = dy * fg_ref[...]
        dx3 = r * (dxn - xn * jnp.mean(dxn * xn, axis=-1, keepdims=True))
        dx3_ref[...] = dx3
        do2_ref[...] = (dx3 * g2_ref[...]).astype(BF16)

        @pl.when(i == 0)
        def _():
            ls_ref[...] = jnp.zeros_like(ls_ref)
            dfg_ref[...] = jnp.zeros_like(dfg_ref)
            dg2_ref[...] = jnp.zeros_like(dg2_ref)

        e2 = _colsum8(err * err)
        lanes = e2[:, 0:128]
        for q in range(1, D // 128):
            lanes = lanes + e2[:, q * 128:(q + 1) * 128]
        ls_ref[...] += lanes * (0.5 / D)
        dfg_ref[...] += _colsum8(dy * xn)
        dg2_ref[...] += _colsum8(dx3 * o2)

    row = pl.BlockSpec((tm, D), lambda i: (i, 0))
    par = _full((1, D))
    return pl.pallas_call(
        body, name="final_loss",
        out_shape=(jax.ShapeDtypeStruct((S, D), F32), jax.ShapeDtypeStruct((S, D), BF16),
                   jax.ShapeDtypeStruct((8, 128), F32), jax.ShapeDtypeStruct((8, D), F32),
                   jax.ShapeDtypeStruct((8, D), F32)),
        grid=(S // tm,), in_specs=[row, row, par, par, row],
        out_specs=(row, row, _full((8, 128)), _full((8, D)), _full((8, D))),
        compiler_params=_cp(("arbitrary",)))(x2, o2, g2, fg, tgt)


def _normmod_bwd(dh, xin, dres, g, sc, gate, o, name):
    S, D = xin.shape
    tm = _row_tile(S)

    def body(dh_ref, x_ref, dr_ref, g_ref, sc_ref, gate_ref, o_ref, dx_ref, do_ref, dsh_ref, dsc_ref, dg_ref, dgate_ref):
        i = pl.program_id(0)
        xv = x_ref[...]
        r = lax.rsqrt(jnp.mean(xv * xv, axis=-1, keepdims=True) + EPS)
        xn = xv * r
        dh_v = dh_ref[...]
        gv = g_ref[...]
        scale = 1.0 + sc_ref[...]
        dxn = dh_v * (gv * scale)
        dx = dr_ref[...] + r * (dxn - xn * jnp.mean(dxn * xn, axis=-1, keepdims=True))
        dx_ref[...] = dx
        do_ref[...] = (dx * gate_ref[...]).astype(BF16)

        @pl.when(i == 0)
        def _():
            dsh_ref[...] = jnp.zeros_like(dsh_ref)
            dsc_ref[...] = jnp.zeros_like(dsc_ref)
            dg_ref[...] = jnp.zeros_like(dg_ref)
            dgate_ref[...] = jnp.zeros_like(dgate_ref)

        hx = dh_v * xn
        dsh_ref[...] += _colsum8(dh_v)
        dsc_ref[...] += _colsum8(hx) * gv
        dg_ref[...] += _colsum8(hx) * scale
        dgate_ref[...] += _colsum8(dx * o_ref[...])

    row = pl.BlockSpec((tm, D), lambda i: (i, 0))
    par = _full((1, D))
    acc = jax.ShapeDtypeStruct((8, D), F32)
    return pl.pallas_call(
        body, name=name,
        out_shape=(jax.ShapeDtypeStruct((S, D), F32), jax.ShapeDtypeStruct((S, D), BF16), acc, acc, acc, acc),
        grid=(S // tm,), in_specs=[row, row, row, par, par, par, row],
        out_specs=(row, row, _full((8, D)), _full((8, D)), _full((8, D)), _full((8, D))),
        compiler_params=_cp(("arbitrary",)))(dh, xin, dres, g, sc, gate, o)


def _me():
    return lax.axis_index("x"), lax.axis_index("y"), lax.axis_index("c")


def _allgather8(v, name):
    R, C = v.shape

    def body(v_ref, out_ref, send_sems, recv_sems, local_sem):
        x, y, c = _me()
        mine = pltpu.make_async_copy(v_ref, out_ref.at[4 * x + 2 * y + c], local_sem)
        mine.start()
        copies = []
        for k in range(1, N_DEV):
            fx, fy, fc = (k >> 2) & 1, (k >> 1) & 1, k & 1
            peer = (x ^ fx, y ^ fy, c ^ fc)
            copies.append(pltpu.make_async_remote_copy(
                src_ref=v_ref, dst_ref=out_ref.at[4 * x + 2 * y + c],
                send_sem=send_sems.at[k - 1], recv_sem=recv_sems.at[k - 1],
                device_id=peer, device_id_type=MESH))
        for cp in copies:
            cp.start()
        for k in range(1, N_DEV):
            fx, fy, fc = (k >> 2) & 1, (k >> 1) & 1, k & 1
            src_slot = 4 * (x ^ fx) + 2 * (y ^ fy) + (c ^ fc)
            pltpu.make_async_remote_copy(
                src_ref=v_ref, dst_ref=out_ref.at[src_slot],
                send_sem=send_sems.at[k - 1], recv_sem=recv_sems.at[k - 1],
                device_id=(x ^ fx, y ^ fy, c ^ fc), device_id_type=MESH).wait_recv()
        for cp in copies:
            cp.wait_send()
        mine.wait()

    return pl.pallas_call(
        body, name=name, out_shape=jax.ShapeDtypeStruct((N_DEV, R, C), v.dtype),
        in_specs=[pl.BlockSpec(memory_space=pltpu.VMEM)], out_specs=pl.BlockSpec(memory_space=pltpu.VMEM),
        scratch_shapes=[pltpu.SemaphoreType.DMA((N_DEV - 1,)), pltpu.SemaphoreType.DMA((N_DEV - 1,)),
                        pltpu.SemaphoreType.DMA],
        compiler_params=pltpu.CompilerParams(vmem_limit_bytes=VMEM_LIMIT))(v)


def _swap_sibling(arrs):
    nw = len(arrs)

    def body(*refs):
        ins, outs = refs[:nw], refs[nw:2 * nw]
        send_sems, recv_sems = refs[2 * nw:]
        x, y, c = _me()
        copies = [pltpu.make_async_remote_copy(
            src_ref=ins[w], dst_ref=outs[w], send_sem=send_sems.at[w], recv_sem=recv_sems.at[w],
            device_id=(x, y, 1 - c), device_id_type=MESH) for w in range(nw)]
        for cp in copies:
            cp.start()
        for cp in copies:
            cp.wait_recv()
        for cp in copies:
            cp.wait_send()

    hbm = pl.BlockSpec(memory_space=pltpu.HBM)
    return pl.pallas_call(
        body, name="swap_sibling", out_shape=tuple(jax.ShapeDtypeStruct(a.shape, a.dtype) for a in arrs),
        in_specs=[hbm] * nw, out_specs=tuple([hbm] * nw),
        scratch_shapes=[pltpu.SemaphoreType.DMA((nw,)), pltpu.SemaphoreType.DMA((nw,))],
        compiler_params=pltpu.CompilerParams(vmem_limit_bytes=VMEM_LIMIT))(*arrs)


_HBM = pl.BlockSpec(memory_space=pltpu.HBM)
_SEM = pl.BlockSpec(memory_space=pltpu.SEMAPHORE)
_EFFECT = pltpu.SideEffectType.DATAFLOW_SIDE_EFFECTING
_N_PEER = N_CHIP - 1


def _chip_part(ref, axis, n, chip):
    start = pl.multiple_of(chip * n, 8)
    return ref.at[pl.ds(start, n), :] if axis == 0 else ref.at[:, pl.ds(start, n)]


def _gather_copy(k, src_ref, land_ref, send_sems, recv_sems, axis, arriving):
    x, y, c = _me()
    px, py = x ^ ((k >> 1) & 1), y ^ (k & 1)
    chip = 2 * px + py if arriving else 2 * x + y
    return pltpu.make_async_remote_copy(
        src_ref=src_ref, dst_ref=_chip_part(land_ref, axis, src_ref.shape[axis], chip),
        send_sem=send_sems.at[k - 1], recv_sem=recv_sems.at[k - 1], device_id=(px, py, c), device_id_type=MESH)


def _scatter_copy(k, grad_ref, land_ref, send_sems, recv_sems, axis):
    x, y, c = _me()
    px, py = x ^ ((k >> 1) & 1), y ^ (k & 1)
    return pltpu.make_async_remote_copy(
        src_ref=_chip_part(grad_ref, axis, grad_ref.shape[axis] // N_CHIP, 2 * px + py), dst_ref=land_ref.at[k - 1],
        send_sem=send_sems.at[k - 1], recv_sem=recv_sems.at[k - 1], device_id=(px, py, c), device_id_type=MESH)


def _gather_start(shards, lands, axes):
    nw = len(shards)

    def body(*refs):
        srcs, zones = refs[:nw], refs[nw:2 * nw]
        sends, recvs = refs[2 * nw:3 * nw], refs[3 * nw:4 * nw]
        token = refs[-1]
        for w in range(nw):
            for k in range(1, N_CHIP):
                _gather_copy(k, srcs[w], zones[w], sends[w], recvs[w], axes[w], False).start()
        token[...] = jnp.zeros_like(token)

    sem = pltpu.SemaphoreType.DMA((_N_PEER,))
    outs = pl.pallas_call(
        body, name="gather_start",
        out_shape=tuple([sem] * (2 * nw) + [pltpu.HBM(a.shape, a.dtype) for a in list(shards) + list(lands)]
                        + [jax.ShapeDtypeStruct((8, 128), F32)]),
        in_specs=[_HBM] * (2 * nw),
        out_specs=tuple([_SEM] * (2 * nw) + [_HBM] * (2 * nw) + [pl.BlockSpec(memory_space=pltpu.VMEM)]),
        input_output_aliases={i: 2 * nw + i for i in range(2 * nw)},
        compiler_params=pltpu.CompilerParams(has_side_effects=_EFFECT),
    )(*[pltpu.with_memory_space_constraint(a, pltpu.HBM) for a in list(shards) + list(lands)])
    per_weight = [(outs[w], outs[nw + w], outs[2 * nw + w], outs[3 * nw + w]) for w in range(nw)]
    return per_weight, outs[-1]


def _gather_wait(state, axis, after, name):
    send_sems, recv_sems, shard, land = state

    def body(src_ref, land_ref, sends, recvs, after_ref, src_dead, got_ref):
        for k in range(1, N_CHIP):
            _gather_copy(k, src_ref, land_ref, sends, recvs, axis, False).wait_send()
            _gather_copy(k, src_ref, land_ref, sends, recvs, axis, True).wait_recv()

    return pl.pallas_call(
        body, name=name, out_shape=(pltpu.HBM(shard.shape, shard.dtype), pltpu.HBM(land.shape, land.dtype)),
        in_specs=[_HBM, _HBM, _SEM, _SEM, pl.BlockSpec(memory_space=pl.ANY)], out_specs=(_HBM, _HBM),
        input_output_aliases={0: 0, 1: 1},
        compiler_params=pltpu.CompilerParams(has_side_effects=_EFFECT),
    )(shard, land, send_sems, recv_sems, after)[1]


def _scatter_start(grad, axis, name):
    shp = list(grad.shape)
    shp[axis] //= N_CHIP
    land = lax.empty((_N_PEER,) + tuple(shp), grad.dtype)

    def body(grad_ref, land_ref, sends, recvs, grad_thru, land_thru, token):
        for k in range(1, N_CHIP):
            _scatter_copy(k, grad_ref, land_ref, sends, recvs, axis).start()
        token[...] = jnp.zeros_like(token)

    sem = pltpu.SemaphoreType.DMA((_N_PEER,))
    outs = pl.pallas_call(
        body, name=name,
        out_shape=(sem, sem, pltpu.HBM(grad.shape, grad.dtype), pltpu.HBM(land.shape, land.dtype),
                   jax.ShapeDtypeStruct((8, 128), F32)),
        in_specs=[_HBM, _HBM], out_specs=(_SEM, _SEM, _HBM, _HBM, pl.BlockSpec(memory_space=pltpu.VMEM)),
        input_output_aliases={0: 2, 1: 3},
        compiler_params=pltpu.CompilerParams(has_side_effects=_EFFECT),
    )(pltpu.with_memory_space_constraint(grad, pltpu.HBM), pltpu.with_memory_space_constraint(land, pltpu.HBM))
    return outs[:4], outs[4]


def _scatter_wait(state, axis, after, name):
    send_sems, recv_sems, grad, land = state

    def body(grad_ref, land_ref, sends, recvs, after_ref, grad_dead, got_ref):
        for k in range(1, N_CHIP):
            cp = _scatter_copy(k, grad_ref, land_ref, sends, recvs, axis)
            cp.wait_send()
            cp.wait_recv()

    return pl.pallas_call(
        body, name=name, out_shape=(pltpu.HBM(grad.shape, grad.dtype), pltpu.HBM(land.shape, land.dtype)),
        in_specs=[_HBM, _HBM, _SEM, _SEM, pl.BlockSpec(memory_space=pl.ANY)], out_specs=(_HBM, _HBM),
        input_output_aliases={0: 0, 1: 1},
        compiler_params=pltpu.CompilerParams(has_side_effects=_EFFECT),
    )(grad, land, send_sems, recv_sems, after)[1]


_C1 = 1.0 - B1 ** STEP
_C2 = 1.0 - B2 ** STEP


def _adam_math(w, g, m, v):
    m = B1 * m + (1.0 - B1) * g
    v = B2 * v + (1.0 - B2) * (g * g)
    delta = -LR * ((m / _C1) / (jnp.sqrt(v / _C2) + AEPS) + WD * w)
    return delta, m, v


def _adamw(w, m, v, groups, name):
    R, C = w.shape
    tr = R if R <= 256 else (128 if R % 128 == 0 else 176)
    assert R % tr == 0, (name, R)
    gparts = [p for grp in groups for p in grp]
    sizes = [len(grp) for grp in groups]
    ng = len(gparts)

    def body(*refs):
        w_ref, m_ref, v_ref = refs[:3]
        g_refs = list(refs[3:3 + ng])
        g_out, d_out, m_out, v_out = refs[3 + ng:]
        g = None
        for size in sizes:
            s = None
            for r in [g_refs.pop(0) for _ in range(size)]:
                terms = [r[q] for q in range(r.shape[0])] if len(r.shape) == 3 else [r[...]]
                for t in terms:
                    s = t.astype(F32) if s is None else s + t.astype(F32)
            g = s if g is None else g + s
        delta, mn, vn = _adam_math(w_ref[...], g, m_ref[...], v_ref[...])
        g_out[...] = g
        d_out[...] = delta
        m_out[...] = mn
        v_out[...] = vn

    blk = pl.BlockSpec((tr, C), lambda i: (i, 0))
    g_specs = [blk if p.ndim == 2 else pl.BlockSpec((p.shape[0], tr, C), lambda i: (0, i, 0)) for p in gparts]
    sds = jax.ShapeDtypeStruct((R, C), F32)
    return pl.pallas_call(
        body, name=name, out_shape=(sds, sds, sds, sds), grid=(R // tr,),
        in_specs=[blk, blk, blk] + g_specs, out_specs=(blk, blk, blk, blk),
        compiler_params=_cp(("parallel",)))(w, m, v, *gparts)


def _mod_shard(c_all, w_ada, b_ada_cols):
    n = w_ada.shape[1]
    tn = 512

    def body(c_ref, w_ref, b_ref, o_ref):
        cv = c_ref[...]
        ca = (cv * _sig(cv)).astype(BF16)
        o_ref[...] = jnp.dot(ca, w_ref[...].astype(BF16), preferred_element_type=F32) + b_ref[...]

    return pl.pallas_call(
        body, name="mod_shard", out_shape=jax.ShapeDtypeStruct((N_DEV, n), F32), grid=(n // tn,),
        in_specs=[_full((N_DEV, D_MODEL)), pl.BlockSpec((D_MODEL, tn), lambda j: (0, j)),
                  pl.BlockSpec((1, tn), lambda j: (0, j))],
        out_specs=pl.BlockSpec((N_DEV, tn), lambda j: (0, j)),
        compiler_params=_cp(("parallel",)))(c_all, w_ada, b_ada_cols)


def _ada_grad(c_all, dmod_cols):
    n = dmod_cols.shape[1]
    tn = 512

    def body(c_ref, d_ref, o_ref):
        cv = c_ref[...]
        ca = cv * _sig(cv)
        o_ref[...] = lax.dot_general(ca, d_ref[...], (((0,), (0,)), ((), ())),
                                     preferred_element_type=F32, precision=lax.Precision.HIGHEST)

    return pl.pallas_call(
        body, name="ada_grad", out_shape=jax.ShapeDtypeStruct((D_MODEL, n), F32), grid=(n // tn,),
        in_specs=[_full((N_DEV, D_MODEL)), pl.BlockSpec((N_DEV, tn), lambda j: (0, j))],
        out_specs=pl.BlockSpec((D_MODEL, tn), lambda j: (0, j)),
        compiler_params=_cp(("parallel",)))(c_all, dmod_cols)


def _device_step(x, mod, W, tgt, getw, put):
    sh1, sc1, g1, sh2, sc2, g2 = [mod[:, i * D_MODEL:(i + 1) * D_MODEL] for i in range(6)]
    e_re, e_im, bb_re, bb_im = _ssm_prep(W["ssm_a_re"], W["ssm_a_im"], W["ssm_b_re"], W["ssm_b_im"], W["ssm_log_dt"])
    bb, cm = _block_diag_mats(bb_re, bb_im, W["ssm_c_re"], W["ssm_c_im"])
    bb16, cm16 = bb.astype(BF16), cm.astype(BF16)
    tab_f = _scan_tables(e_re, e_im, False)
    tab_b = _scan_tables(e_re, e_im, True)

    h1 = _normmod(x, W["norm1_g"], sc1, sh1, "normmod1")
    w_in = getw("w_in", h1)
    z = _matmul(h1, w_in, "nn", 512, 512, 1024, F32, "mm_w_in")
    yc, scv = _conv_fwd(z, W["conv_w"], W["conv_b"], W["conv_ln_g"], W["conv_ln_b"])
    w_cp = getw("conv_proj", scv)
    y_conv = _matmul(scv, w_cp, "nn", 512, 1024, 512, F32, "mm_conv_proj")
    xs, ys, yg = _ssm_fwd(z, bb16, cm16, W["ssm_d"], tab_f)
    w_glu = getw("ssm_glu", yg)
    zz = _matmul(yg, w_glu, "nn", 512, 1024, 512, F32, "mm_ssm_glu")
    merged = _merge_fwd(z, zz, y_conv)
    w_out = getw("w_out", merged)
    o = _matmul(merged, w_out, "nn", 512, 1024, 1024, F32, "mm_w_out")
    x2, h2 = _resid_normmod(x, o, g1, W["norm2_g"], sc2, sh2, "resid_normmod2")
    w_fi = getw("w_ffn_in", h2)
    f = _matmul(h2, w_fi, "nn", 512, 1408, 1024, F32, "mm_ffn_in")
    act = _ffn_act(f)
    w_fo = getw("w_ffn_out", act)
    o2 = _matmul(act, w_fo, "nn", 512, 1024, FH, F32, "mm_ffn_out")
    dx3, do2, loss8, dfg8, dg2_8 = _final(x2, o2, g2, W["final_g"], tgt)

    sm = {}
    put("w_ffn_out", _matmul(act, do2, "tn", 1408, 1024, 1024, BF16, "mm_g_ffn_out"))
    dact = _matmul(do2, w_fo, "nt", 512, 1408, 1024, F32, "mm_d_act")
    dfg, dfu = _ffn_bwd(f, dact)
    df = jnp.concatenate([dfg, dfu], axis=1)
    put("w_ffn_in", _matmul(h2, df, "tn", 1024, 1408, 1024, BF16, "mm_g_ffn_in"))
    dh2 = _matmul(df, w_fi, "nt", 512, 1024, 1408, F32, "mm_d_h2")
    dx2, do, dsh2, dsc2, dn2, dg1_8 = _normmod_bwd(dh2, x2, dx3, W["norm2_g"], sc2, g1, o, "normmod2_bwd")
    put("w_out", _matmul(merged, do, "tn", 1024, 1024, 1024, BF16, "mm_g_w_out"))
    dmerged = _matmul(do, w_out, "nt", 512, 1024, 1024, F32, "mm_d_merged")
    dyconv, dgl, dzz = _merge_bwd(dmerged, z, zz, y_conv)
    put("ssm_glu", _matmul(yg, dzz, "tn", 512, 1024, 1024, BF16, "mm_g_ssm_glu"))
    put("conv_proj", _matmul(scv, dyconv, "tn", 512, 1024, 1024, BF16, "mm_g_conv_proj"))
    dyg = _matmul(dzz, w_glu, "nt", 512, 512, 1024, F32, "mm_d_yg")
    lam, du, dys16, de16, dd8 = _ssm_bwd(dyg, ys, z, xs, cm16.T, bb16.T, W["ssm_d"], tab_b)
    dc_full = _matmul(dys16, xs, "tn", 512, 1024, 1024, F32, "mm_g_ssm_c")
    u = z[:, 2 * CW:3 * CW]
    dbb_full = _matmul(u, lam, "tn", 512, 1024, 1024, F32, "mm_g_ssm_b")
    dsc = _matmul(dyconv, w_cp, "nt", 512, 512, 1024, F32, "mm_d_sc")
    dyc, dlg8, dlb8, dcb8 = _conv_bwd_ln(dsc, yc, W["conv_ln_g"], W["conv_ln_b"])
    dz_conv, dcw = _conv_bwd(dyc, z, W["conv_w"])
    dz = jnp.concatenate([dz_conv, du, dgl], axis=1)
    put("w_in", _matmul(h1, dz, "tn", 1024, 512, 1024, BF16, "mm_g_w_in"))
    dh1 = _matmul(dz, w_in, "nt", 512, 1024, 1792, F32, "mm_d_h1")
    dx, _, dsh1, dsc1, dn1, _ = _normmod_bwd(dh1, x, dx2, W["norm1_g"], sc1, g1, o, "normmod1_bwd")

    s8 = lambda a: jnp.sum(a, axis=0, keepdims=True)
    de = de16.reshape(2, 8, NST).sum(1)
    de_re, de_im = de[0].reshape(G, P), de[1].reshape(G, P)
    dc_re = _diag_blocks(dc_full[:, :NST])
    dc_im = -_diag_blocks(dc_full[:, NST:])
    dbb_re = jnp.swapaxes(_diag_blocks(dbb_full[:, :NST]), 1, 2)
    dbb_im = jnp.swapaxes(_diag_blocks(dbb_full[:, NST:]), 1, 2)
    _, vjp = jax.vjp(_ssm_prep, W["ssm_a_re"], W["ssm_a_im"], W["ssm_b_re"], W["ssm_b_im"], W["ssm_log_dt"])
    sm["ssm_a_re"], sm["ssm_a_im"], sm["ssm_b_re"], sm["ssm_b_im"], sm["ssm_log_dt"] = vjp((de_re, de_im, dbb_re, dbb_im))
    sm["ssm_c_re"], sm["ssm_c_im"] = dc_re, dc_im
    sm["ssm_d"] = s8(dd8)
    sm["norm1_g"], sm["norm2_g"] = s8(dn1), s8(dn2)
    sm["conv_b"], sm["conv_ln_g"], sm["conv_ln_b"] = s8(dcb8), s8(dlg8), s8(dlb8)
    sm["conv_w"] = dcw.reshape(KW, 8, CW).sum(1)
    sm["final_g"] = s8(dfg8)
    dmod = jnp.concatenate([s8(dsh1), s8(dsc1), s8(dg1_8), s8(dsh2), s8(dsc2), s8(dg2_8)], axis=1)
    return loss8, dx, sm, dmod


_BIG = ("w_in", "conv_proj", "ssm_glu", "w_out", "w_ffn_in", "w_ffn_out")
_BIG_AXIS = {"w_in": 1, "conv_proj": 1, "ssm_glu": 1, "w_out": 0, "w_ffn_in": 1, "w_ffn_out": 0}
_SMALL = ("b_ada", "norm1_g", "conv_w", "conv_b", "conv_ln_g", "conv_ln_b", "ssm_a_re", "ssm_a_im", "ssm_b_re",
          "ssm_b_im", "ssm_c_re", "ssm_c_im", "ssm_d", "ssm_log_dt", "norm2_g", "final_g")
_ORDER = ("w_ada", "b_ada", "norm1_g", "w_in", "conv_w", "conv_b", "conv_ln_g", "conv_ln_b", "conv_proj",
          "ssm_a_re", "ssm_a_im", "ssm_b_re", "ssm_b_im", "ssm_c_re", "ssm_c_im", "ssm_d", "ssm_log_dt", "ssm_glu",
          "w_out", "norm2_g", "w_ffn_in", "w_ffn_out", "final_g")
_PACK_COLS = 1024


def _pack_rows(shape):
    return -(-int(np.prod(shape)) // (8 * _PACK_COLS)) * 8


def _pack(arrs):
    parts = []
    for a in arrs:
        flat = a.reshape(-1)
        n = _pack_rows(a.shape)
        parts.append(jnp.pad(flat, (0, n * _PACK_COLS - flat.shape[0])).reshape(n, _PACK_COLS))
    return jnp.concatenate(parts, 0)


def _unpack(packed, shapes):
    out, r = [], 0
    for shp in shapes:
        size = int(np.prod(shp))
        n = _pack_rows(shp)
        out.append(packed[r:r + n].reshape(-1)[:size].reshape(shp))
        r += n
    return out


def kernel(x, c, w_ada, b_ada, norm1_g, w_in, conv_w, conv_b, conv_ln_g, conv_ln_b, conv_proj, ssm_a_re, ssm_a_im, ssm_b_re, ssm_b_im, ssm_c_re, ssm_c_im, ssm_d, ssm_log_dt, ssm_glu, w_out, norm2_g, w_ffn_in, w_ffn_out, final_g, loss_target, m_w_ada, m_b_ada, m_norm1_g, m_w_in, m_conv_w, m_conv_b, m_conv_ln_g, m_conv_ln_b, m_conv_proj, m_ssm_a_re, m_ssm_a_im, m_ssm_b_re, m_ssm_b_im, m_ssm_c_re, m_ssm_c_im, m_ssm_d, m_ssm_log_dt, m_ssm_glu, m_w_out, m_norm2_g, m_w_ffn_in, m_w_ffn_out, m_final_g, v_w_ada, v_b_ada, v_norm1_g, v_w_in, v_conv_w, v_conv_b, v_conv_ln_g, v_conv_ln_b, v_conv_proj, v_ssm_a_re, v_ssm_a_im, v_ssm_b_re, v_ssm_b_im, v_ssm_c_re, v_ssm_c_im, v_ssm_d, v_ssm_log_dt, v_ssm_glu, v_w_out, v_norm2_g, v_w_ffn_in, v_w_ffn_out, v_final_g):
    given = dict(locals())
    mx, my, mc = _me()
    chip = 2 * mx + my
    dev = 4 * mx + 2 * my + mc
    def canon(a):
        return a.reshape(1, -1) if a.ndim <= 2 else a[0]

    wts = {n: canon(given[n]) for n in _ORDER}
    mom = {n: canon(given["m_" + n]) for n in _ORDER}
    var = {n: canon(given["v_" + n]) for n in _ORDER}

    axes = [_BIG_AXIS[n] for n in _BIG]
    shards = [wts[n].astype(BF16) for n in _BIG]
    lands = []
    for s, ax in zip(shards, axes):
        shp = list(s.shape)
        shp[ax] *= N_CHIP
        idx = (chip * s.shape[0], 0) if ax == 0 else (0, chip * s.shape[1])
        lands.append(lax.dynamic_update_slice(lax.empty(tuple(shp), BF16), s, idx))
    gstate, token = _gather_start(shards, lands, axes)
    gstate = dict(zip(_BIG, gstate))

    c_all = _allgather8(jnp.broadcast_to(c, (8, D_MODEL)) + token[0, 0], "gather_c")[:, 0, :]
    n_ada = wts["w_ada"].shape[1]
    b_cols = lax.dynamic_slice(wts["b_ada"], (0, chip * n_ada), (1, n_ada))
    mod_cols = _mod_shard(c_all, wts["w_ada"], b_cols)
    mods = _allgather8(mod_cols, "gather_mod")
    mod = jnp.concatenate([lax.dynamic_index_in_dim(mods[2 * q], dev, 0, keepdims=True) for q in range(N_CHIP)], axis=1)

    W = {n: wts[n] for n in _ORDER if n not in _BIG}
    conv_w_full = _allgather8(jnp.pad(wts["conv_w"], ((0, 1), (0, 0))), "gather_conv_w")
    W["conv_w"] = jnp.concatenate([conv_w_full[2 * q, :KW] for q in range(N_CHIP)], axis=1)

    def getw(n, after):
        return _gather_wait(gstate[n], _BIG_AXIS[n], after, "gather_wait_" + n)

    sstate, own = {}, {}

    def put(n, g):
        ax = _BIG_AXIS[n]
        k = g.shape[ax] // N_CHIP
        own[n] = lax.dynamic_slice_in_dim(g, chip * k, k, axis=ax)
        sstate[n], _ = _scatter_start(g, ax, "scatter_start_" + n)

    loss8, dx, sm, dmod = _device_step(x[0], mod, W, loss_target[0], getw, put)
    loss = lax.psum(jnp.sum(loss8), ("x", "y", "c"))

    recv = [_scatter_wait(sstate[n], _BIG_AXIS[n], dx, "scatter_wait_" + n) for n in _BIG]
    mine = [a for n, r in zip(_BIG, recv) for a in (own[n], r)]
    sib = _swap_sibling(mine)

    small_shapes = [sm[n].shape for n in _SMALL if n != "b_ada"]
    packed = _pack([sm[n] for n in _SMALL if n != "b_ada"] + [dmod])
    allp = _allgather8(packed, "gather_small")

    outs = {}
    for i, n in enumerate(_BIG):
        outs[n] = _adamw(wts[n], mom[n], var[n], [mine[2 * i:2 * i + 2], sib[2 * i:2 * i + 2]], "adamw_" + n)

    off = sum(_pack_rows(s) for s in small_shapes)
    dmod_all = allp[:, off:off + _pack_rows((6 * D_MODEL,)), :].reshape(N_DEV, -1)[:, :6 * D_MODEL]
    dmod_cols = lax.dynamic_slice(dmod_all, (0, chip * n_ada), (N_DEV, n_ada))
    g_ada = _ada_grad(c_all, dmod_cols)
    outs["w_ada"] = _adamw(wts["w_ada"], mom["w_ada"], var["w_ada"], [[g_ada]], "adamw_w_ada")

    def small_w(d):
        arrs = []
        for n in _SMALL:
            if n == "b_ada":
                continue
            if n == "conv_w":
                arrs.append(jnp.zeros((KW, CW), F32))
            else:
                arrs.append(d[n])
        arrs.append(d["b_ada"])
        return _pack(arrs)

    res = _adamw(small_w(wts), small_w(mom), small_w(var), [[allp]], "adamw_small")
    small_out = [_unpack(r, small_shapes + [(1, 6 * D_MODEL)]) for r in res]
    names_small = [n for n in _SMALL if n != "b_ada"] + ["b_ada"]
    for idx, n in enumerate(names_small):
        outs[n] = tuple(small_out[q][idx] for q in range(4))
    g_cw = lax.dynamic_slice(outs["conv_w"][0], (0, chip * (CW // N_CHIP)), (KW, CW // N_CHIP))
    pad = lambda a: jnp.pad(a, ((0, 1), (0, 0)))
    r_cw = _adamw(pad(wts["conv_w"]), pad(mom["conv_w"]), pad(var["conv_w"]), [[pad(g_cw)]], "adamw_conv_w")
    outs["conv_w"] = tuple(r[:KW] for r in r_cw)

    def shaped(n, a):
        return a.reshape(given[n].shape)

    result = [loss, dx[None]]
    for q in range(4):
        result += [shaped(n, outs[n][q]) for n in _ORDER]
    return tuple(result)
```

```python
import math

import jax
import jax.numpy as jnp
import numpy as np
from jax import lax
from jax.experimental import pallas as pl
from jax.experimental.pallas import tpu as pltpu

F32 = jnp.float32
BF16 = jnp.bfloat16
EPS = 1e-6
D_MODEL = 1024
CW = 512
KW = 31
HALO = 32
G, P, H = 32, 64, 16
NST = G * P
FH = 2816
N_DEV = 8
N_CHIP = 4
VMEM_LIMIT = 56 * 1024 * 1024
LR, B1, B2, AEPS, WD, STEP = 0.001, 0.9, 0.999, 1e-08, 0.01, 10
MESH = pl.DeviceIdType.MESH


def _cp(sem=None):
    return pltpu.CompilerParams(dimension_semantics=sem, vmem_limit_bytes=VMEM_LIMIT)


def _sig(x):
    return jax.nn.sigmoid(x)


def _full(shape):
    return pl.BlockSpec(shape, lambda *_: (0,) * len(shape))


def _colsum8(v):
    t, c = v.shape
    return jnp.sum(v.reshape(t // 8, 8, c), axis=0)


def _matmul(a, b, mode, tm, tn, tk, out_dtype, name):
    if mode == "nn":
        (M, K), N = a.shape, b.shape[1]
    elif mode == "nt":
        (M, K), N = a.shape, b.shape[0]
    else:
        (K, M), N = a.shape, b.shape[1]
    tm, tn, tk = min(tm, M), min(tn, N), min(tk, K)
    assert M % tm == 0 and N % tn == 0 and K % tk == 0, (name, M, N, K, tm, tn, tk)
    nk = K // tk
    if mode == "nn":
        a_spec = pl.BlockSpec((tm, tk), lambda i, j, k: (i, k))
        b_spec = pl.BlockSpec((tk, tn), lambda i, j, k: (k, j))
        dims = (((1,), (0,)), ((), ()))
    elif mode == "nt":
        a_spec = pl.BlockSpec((tm, tk), lambda i, j, k: (i, k))
        b_spec = pl.BlockSpec((tn, tk), lambda i, j, k: (j, k))
        dims = (((1,), (1,)), ((), ()))
    else:
        a_spec = pl.BlockSpec((tk, tm), lambda i, j, k: (k, i))
        b_spec = pl.BlockSpec((tk, tn), lambda i, j, k: (k, j))
        dims = (((0,), (0,)), ((), ()))

    def body(a_ref, b_ref, o_ref, acc_ref):
        k = pl.program_id(2)
        part = lax.dot_general(a_ref[...].astype(BF16), b_ref[...].astype(BF16), dims,
                               preferred_element_type=F32)
        if nk == 1:
            o_ref[...] = part.astype(out_dtype)
        else:
            @pl.when(k == 0)
            def _():
                acc_ref[...] = part

            @pl.when(k > 0)
            def _():
                acc_ref[...] += part

            @pl.when(k == nk - 1)
            def _():
                o_ref[...] = acc_ref[...].astype(out_dtype)

    return pl.pallas_call(
        body, name=name,
        out_shape=jax.ShapeDtypeStruct((M, N), out_dtype),
        grid=(M // tm, N // tn, nk),
        in_specs=[a_spec, b_spec],
        out_specs=pl.BlockSpec((tm, tn), lambda i, j, k: (i, j)),
        scratch_shapes=[pltpu.VMEM((tm, tn) if nk > 1 else (8, 128), F32)],
        compiler_params=_cp(("parallel", "parallel", "arbitrary")),
    )(a, b)


def _row_tile(S):
    return min(512, S)


def _normmod(x, g, sc, sh, name):
    S, D = x.shape
    tm = _row_tile(S)

    def body(x_ref, g_ref, sc_ref, sh_ref, h_ref):
        xv = x_ref[...]
        r = lax.rsqrt(jnp.mean(xv * xv, axis=-1, keepdims=True) + EPS)
        h_ref[...] = (xv * r * (g_ref[...] * (1.0 + sc_ref[...])) + sh_ref[...]).astype(BF16)

    row = pl.BlockSpec((tm, D), lambda i: (i, 0))
    return pl.pallas_call(
        body, name=name, out_shape=jax.ShapeDtypeStruct((S, D), BF16), grid=(S // tm,),
        in_specs=[row, _full((1, D)), _full((1, D)), _full((1, D))], out_specs=row,
        compiler_params=_cp(("parallel",)))(x, g, sc, sh)


def _resid_normmod(x, o, g1, g, sc, sh, name):
    S, D = x.shape
    tm = _row_tile(S)

    def body(x_ref, o_ref, g1_ref, g_ref, sc_ref, sh_ref, x2_ref, h_ref):
        xv = x_ref[...] + g1_ref[...] * o_ref[...]
        x2_ref[...] = xv
        r = lax.rsqrt(jnp.mean(xv * xv, axis=-1, keepdims=True) + EPS)
        h_ref[...] = (xv * r * (g_ref[...] * (1.0 + sc_ref[...])) + sh_ref[...]).astype(BF16)

    row = pl.BlockSpec((tm, D), lambda i: (i, 0))
    par = _full((1, D))
    return pl.pallas_call(
        body, name=name,
        out_shape=(jax.ShapeDtypeStruct((S, D), F32), jax.ShapeDtypeStruct((S, D), BF16)),
        grid=(S // tm,), in_specs=[row, row, par, par, par, par], out_specs=(row, row),
        compiler_params=_cp(("parallel",)))(x, o, g1, g, sc, sh)


def _conv_fwd(z, conv_w, conv_b, ln_g, ln_b):
    S = z.shape[0]
    tm = min(128, S)
    sub = 32
    hb = tm // HALO

    def body(a_ref, g_ref, ha_ref, hg_ref, w_ref, b_ref, lg_ref, lb_ref, yc_ref, s_ref, ug_ref):
        i = pl.program_id(0)
        halo = ha_ref[...] * _sig(hg_ref[...])
        ug_ref[0:HALO, :] = jnp.where(i == 0, 0.0, halo)
        ug_ref[HALO:, :] = a_ref[...] * _sig(g_ref[...])
        for rb in range(tm // sub):
            acc = jnp.zeros((sub, CW), F32) + b_ref[...]
            for k in range(KW):
                off = rb * sub + HALO - (KW - 1) + k
                acc = acc + w_ref[k:k + 1, :] * ug_ref[off:off + sub, :]
            yc_ref[rb * sub:(rb + 1) * sub, :] = acc
            mu = jnp.mean(acc, axis=-1, keepdims=True)
            cen = acc - mu
            rstd = lax.rsqrt(jnp.mean(cen * cen, axis=-1, keepdims=True) + EPS)
            ln = cen * rstd * lg_ref[...] + lb_ref[...]
            s_ref[rb * sub:(rb + 1) * sub, :] = (ln * _sig(ln)).astype(BF16)

    prev = lambda i: (jnp.maximum(i * hb - 1, 0), 0)
    return pl.pallas_call(
        body, name="conv_fwd",
        out_shape=(jax.ShapeDtypeStruct((S, CW), F32), jax.ShapeDtypeStruct((S, CW), BF16)),
        grid=(S // tm,),
        in_specs=[pl.BlockSpec((tm, CW), lambda i: (i, 0)), pl.BlockSpec((tm, CW), lambda i: (i, 1)),
                  pl.BlockSpec((HALO, CW), prev), pl.BlockSpec((HALO, CW), lambda i: (jnp.maximum(i * hb - 1, 0), 1)),
                  _full((KW, CW)), _full((1, CW)), _full((1, CW)), _full((1, CW))],
        out_specs=(pl.BlockSpec((tm, CW), lambda i: (i, 0)), pl.BlockSpec((tm, CW), lambda i: (i, 0))),
        scratch_shapes=[pltpu.VMEM((tm + HALO, CW), F32)],
        compiler_params=_cp(("parallel",)))(z, z, z, z, conv_w, conv_b, ln_g, ln_b)


def _conv_bwd_ln(dsc, yc, ln_g, ln_b):
    S = yc.shape[0]
    tm = _row_tile(S)

    def body(d_ref, yc_ref, lg_ref, lb_ref, dyc_ref, dlg_ref, dlb_ref, dcb_ref):
        i = pl.program_id(0)
        yc_v = yc_ref[...]
        mu = jnp.mean(yc_v, axis=-1, keepdims=True)
        cen = yc_v - mu
        rstd = lax.rsqrt(jnp.mean(cen * cen, axis=-1, keepdims=True) + EPS)
        yn = cen * rstd
        ln = yn * lg_ref[...] + lb_ref[...]
        sl = _sig(ln)
        dln = d_ref[...] * (sl * (1.0 + ln * (1.0 - sl)))
        dyn = dln * lg_ref[...]
        dyc = rstd * (dyn - jnp.mean(dyn, axis=-1, keepdims=True)
                      - yn * jnp.mean(dyn * yn, axis=-1, keepdims=True))
        dyc_ref[...] = dyc

        @pl.when(i == 0)
        def _():
            dlg_ref[...] = jnp.zeros_like(dlg_ref)
            dlb_ref[...] = jnp.zeros_like(dlb_ref)
            dcb_ref[...] = jnp.zeros_like(dcb_ref)

        dlg_ref[...] += _colsum8(dln * yn)
        dlb_ref[...] += _colsum8(dln)
        dcb_ref[...] += _colsum8(dyc)

    row = pl.BlockSpec((tm, CW), lambda i: (i, 0))
    acc = jax.ShapeDtypeStruct((8, CW), F32)
    return pl.pallas_call(
        body, name="conv_bwd_ln",
        out_shape=(jax.ShapeDtypeStruct((S, CW), F32), acc, acc, acc), grid=(S // tm,),
        in_specs=[row, row, _full((1, CW)), _full((1, CW))],
        out_specs=(row, _full((8, CW)), _full((8, CW)), _full((8, CW))),
        compiler_params=_cp(("arbitrary",)))(dsc, yc, ln_g, ln_b)


def _conv_bwd(dyc, z, conv_w):
    S = z.shape[0]
    tm = min(128, S)
    sub = 32
    hb = tm // HALO
    nt = S // tm

    def body(d_ref, dn_ref, a_ref, g_ref, ha_ref, hg_ref, w_ref, dz_ref, dw_ref, ug_ref, dy_ref):
        i = pl.program_id(0)
        halo = ha_ref[...] * _sig(hg_ref[...])
        ug_ref[0:HALO, :] = jnp.where(i == 0, 0.0, halo)
        a = a_ref[...]
        sg = _sig(g_ref[...])
        ug_ref[HALO:, :] = a * sg
        dy_ref[0:tm, :] = d_ref[...]
        dy_ref[tm:, :] = jnp.where(i == nt - 1, 0.0, dn_ref[...])

        @pl.when(i == 0)
        def _():
            dw_ref[...] = jnp.zeros_like(dw_ref)

        for rb in range(tm // sub):
            r0 = rb * sub
            acc = jnp.zeros((sub, CW), F32)
            dyc_b = dy_ref[r0:r0 + sub, :]
            for k in range(KW):
                up = r0 + (KW - 1) - k
                acc = acc + w_ref[k:k + 1, :] * dy_ref[up:up + sub, :]
                off = r0 + HALO - (KW - 1) + k
                dw_ref[k * 8:(k + 1) * 8, :] += _colsum8(dyc_b * ug_ref[off:off + sub, :])
            a_b = a[r0:r0 + sub, :]
            sg_b = sg[r0:r0 + sub, :]
            dz_ref[r0:r0 + sub, 0:CW] = (acc * sg_b).astype(BF16)
            dz_ref[r0:r0 + sub, CW:2 * CW] = (acc * a_b * sg_b * (1.0 - sg_b)).astype(BF16)

    return pl.pallas_call(
        body, name="conv_bwd",
        out_shape=(jax.ShapeDtypeStruct((S, 2 * CW), BF16), jax.ShapeDtypeStruct((KW * 8, CW), F32)),
        grid=(nt,),
        in_specs=[pl.BlockSpec((tm, CW), lambda i: (i, 0)),
                  pl.BlockSpec((HALO, CW), lambda i: (jnp.minimum((i + 1) * hb, nt * hb - 1), 0)),
                  pl.BlockSpec((tm, CW), lambda i: (i, 0)), pl.BlockSpec((tm, CW), lambda i: (i, 1)),
                  pl.BlockSpec((HALO, CW), lambda i: (jnp.maximum(i * hb - 1, 0), 0)),
                  pl.BlockSpec((HALO, CW), lambda i: (jnp.maximum(i * hb - 1, 0), 1)),
                  _full((KW, CW))],
        out_specs=(pl.BlockSpec((tm, 2 * CW), lambda i: (i, 0)), _full((KW * 8, CW))),
        scratch_shapes=[pltpu.VMEM((tm + HALO, CW), F32), pltpu.VMEM((tm + HALO, CW), F32)],
        compiler_params=_cp(("arbitrary",)))(dyc, dyc, z, z, z, z, conv_w)


_GELU_C = math.sqrt(2.0 / math.pi)


def _gelu(x):
    return 0.5 * x * (1.0 + jnp.tanh(_GELU_C * (x + 0.044715 * x * x * x)))


def _gelu_grad(x):
    t = jnp.tanh(_GELU_C * (x + 0.044715 * x * x * x))
    return 0.5 * (1.0 + t) + 0.5 * x * (1.0 - t * t) * (_GELU_C * (1.0 + 3 * 0.044715 * x * x))


_LW = 512


def _ssm_fwd(z, bb, cm, d, tab):
    S = z.shape[0]
    tm = min(256, S)

    def body(u_ref, bb_ref, cm_ref, d_ref, t_ref, x_ref, ys_ref, yg_ref, car_ref):
        i = pl.program_id(0)

        @pl.when(i == 0)
        def _():
            car_ref[...] = jnp.zeros_like(car_ref)

        u = u_ref[...]
        x_ref[...] = jnp.dot(u.astype(BF16), bb_ref[...], preferred_element_type=F32)
        for c in range(NST // _LW):
            lre = pl.ds(c * _LW, _LW)
            lim = pl.ds(NST + c * _LW, _LW)

            def blk(j, car):
                cr, ci = car
                rows = pl.ds(pl.multiple_of(j * 8, 8), 8)
                r = x_ref[rows, lre]
                im = x_ref[rows, lim]
                for lvl, s in enumerate((1, 2, 4)):
                    mr = t_ref[16 * lvl:16 * lvl + 8, lre]
                    mi = t_ref[16 * lvl + 8:16 * lvl + 16, lre]
                    sr = pltpu.roll(r, s, 0)
                    si = pltpu.roll(im, s, 0)
                    r, im = r + (mr * sr - mi * si), im + (mr * si + mi * sr)
                pr = t_ref[48:56, lre]
                pi_ = t_ref[56:64, lre]
                r, im = r + (pr * cr - pi_ * ci), im + (pr * ci + pi_ * cr)
                x_ref[rows, lre] = r
                x_ref[rows, lim] = im
                return (jnp.broadcast_to(r[7:8, :], (8, _LW)), jnp.broadcast_to(im[7:8, :], (8, _LW)))

            cr, ci = lax.fori_loop(0, tm // 8, blk, (car_ref[:, lre], car_ref[:, lim]))
            car_ref[:, lre] = cr
            car_ref[:, lim] = ci
        ys = jnp.dot(x_ref[...].astype(BF16), cm_ref[...], preferred_element_type=F32) + d_ref[...] * u
        ys_ref[...] = ys
        yg_ref[...] = _gelu(ys).astype(BF16)

    return pl.pallas_call(
        body, name="ssm_fwd",
        out_shape=(jax.ShapeDtypeStruct((S, 2 * NST), F32), jax.ShapeDtypeStruct((S, CW), F32),
                   jax.ShapeDtypeStruct((S, CW), BF16)),
        grid=(S // tm,),
        in_specs=[pl.BlockSpec((tm, CW), lambda i: (i, 2)), _full((CW, 2 * NST)), _full((2 * NST, CW)),
                  _full((1, CW)), _full((64, NST))],
        out_specs=(pl.BlockSpec((tm, 2 * NST), lambda i: (i, 0)), pl.BlockSpec((tm, CW), lambda i: (i, 0)),
                   pl.BlockSpec((tm, CW), lambda i: (i, 0))),
        scratch_shapes=[pltpu.VMEM((8, 2 * NST), F32)],
        compiler_params=_cp(("arbitrary",)))(z, bb, cm, d, tab)


def _ssm_bwd(dyg, ys, z, xs, cmt, bbt, d, tab):
    S = z.shape[0]
    tm = min(256, S)
    nt = S // tm

    def body(dyg_ref, ys_ref, u_ref, x_ref, cmt_ref, bbt_ref, d_ref, t_ref,
             lam_ref, du_ref, dys_ref, de_ref, dd_ref, car_ref):
        i = pl.program_id(0)

        @pl.when(i == 0)
        def _():
            car_ref[...] = jnp.zeros_like(car_ref)
            de_ref[...] = jnp.zeros_like(de_ref)
            dd_ref[...] = jnp.zeros_like(dd_ref)

        u = u_ref[...]
        dys = dyg_ref[...] * _gelu_grad(ys_ref[...])
        dys_ref[...] = dys.astype(BF16)
        dd_ref[...] += _colsum8(dys * u)
        lam_ref[...] = jnp.dot(dys.astype(BF16), cmt_ref[...], preferred_element_type=F32)
        row = lax.broadcasted_iota(jnp.int32, (8, _LW), 0)
        for c in range(NST // _LW):
            lre = pl.ds(c * _LW, _LW)
            lim = pl.ds(NST + c * _LW, _LW)

            def blk(jj, car):
                cr, ci, ar, ai = car
                j = tm // 8 - 1 - jj
                rows = pl.ds(pl.multiple_of(j * 8, 8), 8)
                r = lam_ref[rows, lre]
                im = lam_ref[rows, lim]
                for lvl, s in enumerate((1, 2, 4)):
                    mr = t_ref[16 * lvl:16 * lvl + 8, lre]
                    mi = t_ref[16 * lvl + 8:16 * lvl + 16, lre]
                    sr = pltpu.roll(r, 8 - s, 0)
                    si = pltpu.roll(im, 8 - s, 0)
                    r, im = r + (mr * sr - mi * si), im + (mr * si + mi * sr)
                pr = t_ref[48:56, lre]
                pi_ = t_ref[56:64, lre]
                r, im = r + (pr * cr - pi_ * ci), im + (pr * ci + pi_ * cr)
                lam_ref[rows, lre] = r
                lam_ref[rows, lim] = im
                nr = jnp.where(row == 7, cr, pltpu.roll(r, 7, 0))
                ni = jnp.where(row == 7, ci, pltpu.roll(im, 7, 0))
                xr = x_ref[rows, lre]
                xi = x_ref[rows, lim]
                ar = ar + (nr * xr + ni * xi)
                ai = ai + (ni * xr - nr * xi)
                return (jnp.broadcast_to(r[0:1, :], (8, _LW)), jnp.broadcast_to(im[0:1, :], (8, _LW)), ar, ai)

            zero = jnp.zeros((8, _LW), F32)
            cr, ci, ar, ai = lax.fori_loop(0, tm // 8, blk, (car_ref[:, lre], car_ref[:, lim], zero, zero))
            car_ref[:, lre] = cr
            car_ref[:, lim] = ci
            de_ref[0:8, lre] += ar
            de_ref[8:16, lre] += ai
        du = jnp.dot(lam_ref[...].astype(BF16), bbt_ref[...], preferred_element_type=F32) + dys * d_ref[...]
        du_ref[...] = du.astype(BF16)

    rev = lambda i: (nt - 1 - i, 0)
    return pl.pallas_call(
        body, name="ssm_bwd",
        out_shape=(jax.ShapeDtypeStruct((S, 2 * NST), F32), jax.ShapeDtypeStruct((S, CW), BF16),
                   jax.ShapeDtypeStruct((S, CW), BF16), jax.ShapeDtypeStruct((16, NST), F32),
                   jax.ShapeDtypeStruct((8, CW), F32)),
        grid=(nt,),
        in_specs=[pl.BlockSpec((tm, CW), rev), pl.BlockSpec((tm, CW), rev),
                  pl.BlockSpec((tm, CW), lambda i: (nt - 1 - i, 2)), pl.BlockSpec((tm, 2 * NST), rev),
                  _full((CW, 2 * NST)), _full((2 * NST, CW)), _full((1, CW)), _full((64, NST))],
        out_specs=(pl.BlockSpec((tm, 2 * NST), rev), pl.BlockSpec((tm, CW), rev), pl.BlockSpec((tm, CW), rev),
                   _full((16, NST)), _full((8, CW))),
        scratch_shapes=[pltpu.VMEM((8, 2 * NST), F32)],
        compiler_params=_cp(("arbitrary",)))(dyg, ys, z, xs, cmt, bbt, d, tab)


def _ssm_prep(a_re, a_im, b_re, b_im, log_dt):
    dt = jnp.exp(log_dt.reshape(G))[:, None]
    mag = jnp.exp(dt * a_re)
    e_re, e_im = mag * jnp.cos(dt * a_im), mag * jnp.sin(dt * a_im)
    n_re, n_im = e_re - 1.0, e_im
    den = a_re * a_re + a_im * a_im
    q_re = (n_re * a_re + n_im * a_im) / den
    q_im = (n_im * a_re - n_re * a_im) / den
    bb_re = q_re[..., None] * b_re - q_im[..., None] * b_im
    bb_im = q_re[..., None] * b_im + q_im[..., None] * b_re
    return e_re, e_im, bb_re, bb_im


def _scan_tables(e_re, e_im, reverse):
    er = e_re.reshape(1, NST)
    ei = e_im.reshape(1, NST)
    if reverse:
        ei = -ei
    pows = [(er, ei)]
    for _ in range(7):
        pr, pi_ = pows[-1]
        pows.append((pr * er - pi_ * ei, pr * ei + pi_ * er))
    row = jnp.arange(8)[:, None]
    out = []
    for s in (1, 2, 4):
        pr, pi_ = pows[s - 1]
        keep = (row + s <= 7) if reverse else (row >= s)
        out += [jnp.where(keep, pr, 0.0), jnp.where(keep, pi_, 0.0)]
    allr = jnp.concatenate([p[0] for p in pows], 0)
    alli = jnp.concatenate([p[1] for p in pows], 0)
    if reverse:
        allr, alli = allr[::-1], alli[::-1]
    out += [allr, alli]
    return jnp.concatenate(out, 0).astype(F32)


def _block_diag_mats(bb_re, bb_im, c_re, c_im):
    eye = jnp.eye(G, dtype=F32)
    bre = jnp.einsum("gph,gk->ghkp", bb_re, eye).reshape(CW, NST)
    bim = jnp.einsum("gph,gk->ghkp", bb_im, eye).reshape(CW, NST)
    bb = jnp.concatenate([bre, bim], 1)
    cre = jnp.einsum("ghp,gk->gpkh", c_re, eye).reshape(NST, CW)
    cim = jnp.einsum("ghp,gk->gpkh", c_im, eye).reshape(NST, CW)
    cm = jnp.concatenate([cre, -cim], 0)
    return bb, cm


def _diag_blocks(full):
    return jnp.einsum("ghkp,gk->ghp", full.reshape(G, H, G, P), jnp.eye(G, dtype=F32))


def _merge_fwd(z, zz, y_conv):
    S = z.shape[0]
    tm = _row_tile(S)
    D = D_MODEL

    def body(glc_ref, gls_ref, za_ref, zb_ref, yc_ref, m_ref):
        y_ssm = za_ref[...] * _sig(zb_ref[...])
        m_ref[...] = (_sig(glc_ref[...]) * yc_ref[...] + _sig(gls_ref[...]) * y_ssm).astype(BF16)

    return pl.pallas_call(
        body, name="merge_fwd", out_shape=jax.ShapeDtypeStruct((S, D), BF16), grid=(S // tm, 2),
        in_specs=[pl.BlockSpec((tm, CW), lambda i, j: (i, 3 + j)), pl.BlockSpec((tm, CW), lambda i, j: (i, 5 + j)),
                  pl.BlockSpec((tm, CW), lambda i, j: (i, j)), pl.BlockSpec((tm, CW), lambda i, j: (i, 2 + j)),
                  pl.BlockSpec((tm, CW), lambda i, j: (i, j))],
        out_specs=pl.BlockSpec((tm, CW), lambda i, j: (i, j)),
        compiler_params=_cp(("parallel", "parallel")))(z, z, zz, zz, y_conv)


def _merge_bwd(dm, z, zz, y_conv):
    S = z.shape[0]
    tm = min(256, S)
    D = D_MODEL

    def body(dm_ref, glc0_ref, glc1_ref, gls0_ref, gls1_ref, za_ref, zb_ref, yc_ref, dyc_ref, dgl_ref, dzz_ref):
        for half, (glc_ref, gls_ref) in enumerate(((glc0_ref, gls0_ref), (glc1_ref, gls1_ref))):
            lo, hi = half * CW, (half + 1) * CW
            dm_v = dm_ref[:, lo:hi]
            sgc = _sig(glc_ref[...])
            sgs = _sig(gls_ref[...])
            szb = _sig(zb_ref[:, lo:hi])
            za = za_ref[:, lo:hi]
            dyc_ref[:, lo:hi] = (dm_v * sgc).astype(BF16)
            dgl_ref[:, lo:hi] = (dm_v * yc_ref[:, lo:hi] * sgc * (1.0 - sgc)).astype(BF16)
            dys = dm_v * sgs
            dgl_ref[:, D + lo:D + hi] = (dys * (za * szb) * (1.0 - sgs)).astype(BF16)
            dzz_ref[:, lo:hi] = (dys * szb).astype(BF16)
            dzz_ref[:, D + lo:D + hi] = (dys * za * szb * (1.0 - szb)).astype(BF16)

    zb_ = lambda j: pl.BlockSpec((tm, CW), lambda i: (i, j))
    wide = lambda j: pl.BlockSpec((tm, D), lambda i: (i, j))
    return pl.pallas_call(
        body, name="merge_bwd",
        out_shape=(jax.ShapeDtypeStruct((S, D), BF16), jax.ShapeDtypeStruct((S, 2 * D), BF16),
                   jax.ShapeDtypeStruct((S, 2 * D), BF16)),
        grid=(S // tm,),
        in_specs=[wide(0), zb_(3), zb_(4), zb_(5), zb_(6), wide(0), wide(1), wide(0)],
        out_specs=(wide(0), pl.BlockSpec((tm, 2 * D), lambda i: (i, 0)), pl.BlockSpec((tm, 2 * D), lambda i: (i, 0))),
        compiler_params=_cp(("parallel",)))(dm, z, z, z, z, zz, zz, y_conv)


def _ffn_act(f):
    S = f.shape[0]
    tm = _row_tile(S)
    tn = 1408

    def body(g_ref, u_ref, a_ref):
        gv = g_ref[...]
        a_ref[...] = (gv * _sig(gv) * u_ref[...]).astype(BF16)

    return pl.pallas_call(
        body, name="ffn_act", out_shape=jax.ShapeDtypeStruct((S, FH), BF16), grid=(S // tm, FH // tn),
        in_specs=[pl.BlockSpec((tm, tn), lambda i, j: (i, j)), pl.BlockSpec((tm, tn), lambda i, j: (i, j + FH // tn))],
        out_specs=pl.BlockSpec((tm, tn), lambda i, j: (i, j)),
        compiler_params=_cp(("parallel", "parallel")))(f, f)


def _ffn_bwd(f, dact):
    S = f.shape[0]
    tm = _row_tile(S)
    tn = 1408
    nb = FH // tn

    def body(g_ref, u_ref, d_ref, dg_ref, du_ref):
        gv = g_ref[...]
        sg = _sig(gv)
        dv = d_ref[...]
        dg_ref[...] = (dv * u_ref[...] * (sg * (1.0 + gv * (1.0 - sg)))).astype(BF16)
        du_ref[...] = (dv * gv * sg).astype(BF16)

    lo = pl.BlockSpec((tm, tn), lambda i, j: (i, j))
    hi = pl.BlockSpec((tm, tn), lambda i, j: (i, j + nb))
    return pl.pallas_call(
        body, name="ffn_bwd",
        out_shape=(jax.ShapeDtypeStruct((S, FH), BF16), jax.ShapeDtypeStruct((S, FH), BF16)),
        grid=(S // tm, nb), in_specs=[lo, hi, lo], out_specs=(lo, lo),
        compiler_params=_cp(("parallel", "parallel")))(f, f, dact)


def _final(x2, o2, g2, fg, tgt):
    S, D = x2.shape
    tm = _row_tile(S)

    def body(x2_ref, o2_ref, g2_ref, fg_ref, t_ref, dx3_ref, do2_ref, ls_ref, dfg_ref, dg2_ref):
        i = pl.program_id(0)
        o2 = o2_ref[...]
        x3 = x2_ref[...] + g2_ref[...] * o2
        r = lax.rsqrt(jnp.mean(x3 * x3, axis=-1, keepdims=True) + EPS)
        xn = x3 * r
        err = xn * fg_ref[...] - t_ref[...]
        dy = err * (1.0 / D)
        dxn = dy * fg_ref[...]
        dx3 = r * (dxn - xn * jnp.mean(dxn * xn, axis=-1, keepdims=True))
        dx3_ref[...] = dx3
        do2_ref[...] = (dx3 * g2_ref[...]).astype(BF16)

        @pl.when(i == 0)
        def _():
            ls_ref[...] = jnp.zeros_like(ls_ref)
            dfg_ref[...] = jnp.zeros_like(dfg_ref)
            dg2_ref[...] = jnp.zeros_like(dg2_ref)

        e2 = _colsum8(err * err)
        lanes = e2[:, 0:128]
        for q in range(1, D // 128):
            lanes = lanes + e2[:, q * 128:(q + 1) * 128]
        ls_ref[...] += lanes * (0.5 / D)
        dfg_ref[...] += _colsum8(dy * xn)
        dg2_ref[...] += _colsum8(dx3 * o2)

    row = pl.BlockSpec((tm, D), lambda i: (i, 0))
    par = _full((1, D))
    return pl.pallas_call(
        body, name="final_loss",
        out_shape=(jax.ShapeDtypeStruct((S, D), F32), jax.ShapeDtypeStruct((S, D), BF16),
                   jax.ShapeDtypeStruct((8, 128), F32), jax.ShapeDtypeStruct((8, D), F32),
                   jax.ShapeDtypeStruct((8, D), F32)),
        grid=(S // tm,), in_specs=[row, row, par, par, row],
        out_specs=(row, row, _full((8, 128)), _full((8, D)), _full((8, D))),
        compiler_params=_cp(("arbitrary",)))(x2, o2, g2, fg, tgt)


def _normmod_bwd(dh, xin, dres, g, sc, gate, o, name):
    S, D = xin.shape
    tm = _row_tile(S)

    def body(dh_ref, x_ref, dr_ref, g_ref, sc_ref, gate_ref, o_ref, dx_ref, do_ref, dsh_ref, dsc_ref, dg_ref, dgate_ref):
        i = pl.program_id(0)
        xv = x_ref[...]
        r = lax.rsqrt(jnp.mean(xv * xv, axis=-1, keepdims=True) + EPS)
        xn = xv * r
        dh_v = dh_ref[...]
        gv = g_ref[...]
        scale = 1.0 + sc_ref[...]
        dxn = dh_v * (gv * scale)
        dx = dr_ref[...] + r * (dxn - xn * jnp.mean(dxn * xn, axis=-1, keepdims=True))
        dx_ref[...] = dx
        do_ref[...] = (dx * gate_ref[...]).astype(BF16)

        @pl.when(i == 0)
        def _():
            dsh_ref[...] = jnp.zeros_like(dsh_ref)
            dsc_ref[...] = jnp.zeros_like(dsc_ref)
            dg_ref[...] = jnp.zeros_like(dg_ref)
            dgate_ref[...] = jnp.zeros_like(dgate_ref)

        hx = dh_v * xn
        dsh_ref[...] += _colsum8(dh_v)
        dsc_ref[...] += _colsum8(hx) * gv
        dg_ref[...] += _colsum8(hx) * scale
        dgate_ref[...] += _colsum8(dx * o_ref[...])

    row = pl.BlockSpec((tm, D), lambda i: (i, 0))
    par = _full((1, D))
    acc = jax.ShapeDtypeStruct((8, D), F32)
    return pl.pallas_call(
        body, name=name,
        out_shape=(jax.ShapeDtypeStruct((S, D), F32), jax.ShapeDtypeStruct((S, D), BF16), acc, acc, acc, acc),
        grid=(S // tm,), in_specs=[row, row, row, par, par, par, row],
        out_specs=(row, row, _full((8, D)), _full((8, D)), _full((8, D)), _full((8, D))),
        compiler_params=_cp(("arbitrary",)))(dh, xin, dres, g, sc, gate, o)


def _me():
    return lax.axis_index("x"), lax.axis_index("y"), lax.axis_index("c")


def _allgather8(v, name):
    R, C = v.shape

    def body(v_ref, out_ref, send_sems, recv_sems, local_sem):
        x, y, c = _me()
        mine = pltpu.make_async_copy(v_ref, out_ref.at[4 * x + 2 * y + c], local_sem)
        mine.start()
        copies = []
        for k in range(1, N_DEV):
            fx, fy, fc = (k >> 2) & 1, (k >> 1) & 1, k & 1
            peer = (x ^ fx, y ^ fy, c ^ fc)
            copies.append(pltpu.make_async_remote_copy(
                src_ref=v_ref, dst_ref=out_ref.at[4 * x + 2 * y + c],
                send_sem=send_sems.at[k - 1], recv_sem=recv_sems.at[k - 1],
                device_id=peer, device_id_type=MESH))
        for cp in copies:
            cp.start()
        for k in range(1, N_DEV):
            fx, fy, fc = (k >> 2) & 1, (k >> 1) & 1, k & 1
            src_slot = 4 * (x ^ fx) + 2 * (y ^ fy) + (c ^ fc)
            pltpu.make_async_remote_copy(
                src_ref=v_ref, dst_ref=out_ref.at[src_slot],
                send_sem=send_sems.at[k - 1], recv_sem=recv_sems.at[k - 1],
                device_id=(x ^ fx, y ^ fy, c ^ fc), device_id_type=MESH).wait_recv()
        for cp in copies:
            cp.wait_send()
        mine.wait()

    return pl.pallas_call(
        body, name=name, out_shape=jax.ShapeDtypeStruct((N_DEV, R, C), v.dtype),
        in_specs=[pl.BlockSpec(memory_space=pltpu.VMEM)], out_specs=pl.BlockSpec(memory_space=pltpu.VMEM),
        scratch_shapes=[pltpu.SemaphoreType.DMA((N_DEV - 1,)), pltpu.SemaphoreType.DMA((N_DEV - 1,)),
                        pltpu.SemaphoreType.DMA],
        compiler_params=pltpu.CompilerParams(vmem_limit_bytes=VMEM_LIMIT))(v)


def _swap_sibling(arrs):
    nw = len(arrs)

    def body(*refs):
        ins, outs = refs[:nw], refs[nw:2 * nw]
        send_sems, recv_sems = refs[2 * nw:]
        x, y, c = _me()
        copies = [pltpu.make_async_remote_copy(
            src_ref=ins[w], dst_ref=outs[w], send_sem=send_sems.at[w], recv_sem=recv_sems.at[w],
            device_id=(x, y, 1 - c), device_id_type=MESH) for w in range(nw)]
        for cp in copies:
            cp.start()
        for cp in copies:
            cp.wait_recv()
        for cp in copies:
            cp.wait_send()

    hbm = pl.BlockSpec(memory_space=pltpu.HBM)
    return pl.pallas_call(
        body, name="swap_sibling", out_shape=tuple(jax.ShapeDtypeStruct(a.shape, a.dtype) for a in arrs),
        in_specs=[hbm] * nw, out_specs=tuple([hbm] * nw),
        scratch_shapes=[pltpu.SemaphoreType.DMA((nw,)), pltpu.SemaphoreType.DMA((nw,))],
        compiler_params=pltpu.CompilerParams(vmem_limit_bytes=VMEM_LIMIT))(*arrs)


_HBM = pl.BlockSpec(memory_space=pltpu.HBM)
_SEM = pl.BlockSpec(memory_space=pltpu.SEMAPHORE)
_EFFECT = pltpu.SideEffectType.DATAFLOW_SIDE_EFFECTING
_N_PEER = N_CHIP - 1


def _chip_part(ref, axis, n, chip):
    start = pl.multiple_of(chip * n, 8)
    return ref.at[pl.ds(start, n), :] if axis == 0 else ref.at[:, pl.ds(start, n)]


def _gather_copy(k, src_ref, land_ref, send_sems, recv_sems, axis, arriving):
    x, y, c = _me()
    px, py = x ^ ((k >> 1) & 1), y ^ (k & 1)
    chip = 2 * px + py if arriving else 2 * x + y
    return pltpu.make_async_remote_copy(
        src_ref=src_ref, dst_ref=_chip_part(land_ref, axis, src_ref.shape[axis], chip),
        send_sem=send_sems.at[k - 1], recv_sem=recv_sems.at[k - 1], device_id=(px, py, c), device_id_type=MESH)


def _scatter_copy(k, grad_ref, land_ref, send_sems, recv_sems, axis):
    x, y, c = _me()
    px, py = x ^ ((k >> 1) & 1), y ^ (k & 1)
    return pltpu.make_async_remote_copy(
        src_ref=_chip_part(grad_ref, axis, grad_ref.shape[axis] // N_CHIP, 2 * px + py), dst_ref=land_ref.at[k - 1],
        send_sem=send_sems.at[k - 1], recv_sem=recv_sems.at[k - 1], device_id=(px, py, c), device_id_type=MESH)


def _gather_start(shards, lands, axes):
    nw = len(shards)

    def body(*refs):
        srcs, zones = refs[:nw], refs[nw:2 * nw]
        sends, recvs = refs[2 * nw:3 * nw], refs[3 * nw:4 * nw]
        token = refs[-1]
        for w in range(nw):
            for k in range(1, N_CHIP):
                _gather_copy(k, srcs[w], zones[w], sends[w], recvs[w], axes[w], False).start()
        token[...] = jnp.zeros_like(token)

    sem = pltpu.SemaphoreType.DMA((_N_PEER,))
    outs = pl.pallas_call(
        body, name="gather_start",
        out_shape=tuple([sem] * (2 * nw) + [pltpu.HBM(a.shape, a.dtype) for a in list(shards) + list(lands)]
                        + [jax.ShapeDtypeStruct((8, 128), F32)]),
        in_specs=[_HBM] * (2 * nw),
        out_specs=tuple([_SEM] * (2 * nw) + [_HBM] * (2 * nw) + [pl.BlockSpec(memory_space=pltpu.VMEM)]),
        input_output_aliases={i: 2 * nw + i for i in range(2 * nw)},
        compiler_params=pltpu.CompilerParams(has_side_effects=_EFFECT),
    )(*[pltpu.with_memory_space_constraint(a, pltpu.HBM) for a in list(shards) + list(lands)])
    per_weight = [(outs[w], outs[nw + w], outs[2 * nw + w], outs[3 * nw + w]) for w in range(nw)]
    return per_weight, outs[-1]


def _gather_wait(state, axis, after, name):
    send_sems, recv_sems, shard, land = state

    def body(src_ref, land_ref, sends, recvs, after_ref, src_dead, got_ref, local_sem):
        x, y, _ = _me()
        mine = pltpu.make_async_copy(src_ref, _chip_part(land_ref, axis, src_ref.shape[axis], 2 * x + y), local_sem)
        mine.start()
        for k in range(1, N_CHIP):
            _gather_copy(k, src_ref, land_ref, sends, recvs, axis, False).wait_send()
            _gather_copy(k, src_ref, land_ref, sends, recvs, axis, True).wait_recv()
        mine.wait()

    return pl.pallas_call(
        body, name=name, out_shape=(pltpu.HBM(shard.shape, shard.dtype), pltpu.HBM(land.shape, land.dtype)),
        in_specs=[_HBM, _HBM, _SEM, _SEM, pl.BlockSpec(memory_space=pl.ANY)], out_specs=(_HBM, _HBM),
        input_output_aliases={0: 0, 1: 1}, scratch_shapes=[pltpu.SemaphoreType.DMA],
        compiler_params=pltpu.CompilerParams(has_side_effects=_EFFECT),
    )(shard, land, send_sems, recv_sems, after)[1]


def _all8_copy(k, v_ref, land_ref, send_sems, recv_sems, arriving):
    x, y, c = _me()
    px, py, pc = x ^ ((k >> 2) & 1), y ^ ((k >> 1) & 1), c ^ (k & 1)
    slot = 4 * px + 2 * py + pc if arriving else 4 * x + 2 * y + c
    return pltpu.make_async_remote_copy(
        src_ref=v_ref, dst_ref=land_ref.at[slot], send_sem=send_sems.at[k - 1], recv_sem=recv_sems.at[k - 1],
        device_id=(px, py, pc), device_id_type=MESH)


def _all8_start(v, name):
    land = lax.empty((N_DEV,) + v.shape, v.dtype)

    def body(v_ref, land_ref, sends, recvs, v_thru, land_thru, token):
        for k in range(1, N_DEV):
            _all8_copy(k, v_ref, land_ref, sends, recvs, False).start()
        token[...] = jnp.zeros_like(token)

    sem = pltpu.SemaphoreType.DMA((N_DEV - 1,))
    outs = pl.pallas_call(
        body, name=name,
        out_shape=(sem, sem, pltpu.HBM(v.shape, v.dtype), pltpu.HBM(land.shape, land.dtype),
                   jax.ShapeDtypeStruct((8, 128), F32)),
        in_specs=[_HBM, _HBM], out_specs=(_SEM, _SEM, _HBM, _HBM, pl.BlockSpec(memory_space=pltpu.VMEM)),
        input_output_aliases={0: 2, 1: 3},
        compiler_params=pltpu.CompilerParams(has_side_effects=_EFFECT),
    )(pltpu.with_memory_space_constraint(v, pltpu.HBM), pltpu.with_memory_space_constraint(land, pltpu.HBM))
    return outs[:4], outs[4]


def _all8_wait(state, after, name):
    send_sems, recv_sems, v, land = state

    def body(v_ref, land_ref, sends, recvs, after_ref, v_dead, got_ref, local_sem):
        x, y, c = _me()
        mine = pltpu.make_async_copy(v_ref, land_ref.at[4 * x + 2 * y + c], local_sem)
        mine.start()
        for k in range(1, N_DEV):
            _all8_copy(k, v_ref, land_ref, sends, recvs, False).wait_send()
            _all8_copy(k, v_ref, land_ref, sends, recvs, True).wait_recv()
        mine.wait()

    return pl.pallas_call(
        body, name=name, out_shape=(pltpu.HBM(v.shape, v.dtype), pltpu.HBM(land.shape, land.dtype)),
        in_specs=[_HBM, _HBM, _SEM, _SEM, pl.BlockSpec(memory_space=pl.ANY)], out_specs=(_HBM, _HBM),
        input_output_aliases={0: 0, 1: 1}, scratch_shapes=[pltpu.SemaphoreType.DMA],
        compiler_params=pltpu.CompilerParams(has_side_effects=_EFFECT),
    )(v, land, send_sems, recv_sems, after)[1]


def _tie(value, dep):
    return lax.optimization_barrier((value, dep))[0]


def _scatter_start(grad, axis, name):
    shp = list(grad.shape)
    shp[axis] //= N_CHIP
    land = lax.empty((_N_PEER,) + tuple(shp), grad.dtype)

    def body(grad_ref, land_ref, sends, recvs, grad_thru, land_thru, token):
        for k in range(1, N_CHIP):
            _scatter_copy(k, grad_ref, land_ref, sends, recvs, axis).start()
        token[...] = jnp.zeros_like(token)

    sem = pltpu.SemaphoreType.DMA((_N_PEER,))
    outs = pl.pallas_call(
        body, name=name,
        out_shape=(sem, sem, pltpu.HBM(grad.shape, grad.dtype), pltpu.HBM(land.shape, land.dtype),
                   jax.ShapeDtypeStruct((8, 128), F32)),
        in_specs=[_HBM, _HBM], out_specs=(_SEM, _SEM, _HBM, _HBM, pl.BlockSpec(memory_space=pltpu.VMEM)),
        input_output_aliases={0: 2, 1: 3},
        compiler_params=pltpu.CompilerParams(has_side_effects=_EFFECT),
    )(pltpu.with_memory_space_constraint(grad, pltpu.HBM), pltpu.with_memory_space_constraint(land, pltpu.HBM))
    return outs[:4], outs[4]


def _scatter_wait(state, axis, after, name):
    send_sems, recv_sems, grad, land = state

    def body(grad_ref, land_ref, sends, recvs, after_ref, grad_dead, got_ref):
        for k in range(1, N_CHIP):
            cp = _scatter_copy(k, grad_ref, land_ref, sends, recvs, axis)
            cp.wait_send()
            cp.wait_recv()

    return pl.pallas_call(
        body, name=name, out_shape=(pltpu.HBM(grad.shape, grad.dtype), pltpu.HBM(land.shape, land.dtype)),
        in_specs=[_HBM, _HBM, _SEM, _SEM, pl.BlockSpec(memory_space=pl.ANY)], out_specs=(_HBM, _HBM),
        input_output_aliases={0: 0, 1: 1},
        compiler_params=pltpu.CompilerParams(has_side_effects=_EFFECT),
    )(grad, land, send_sems, recv_sems, after)[1]


_C1 = 1.0 - B1 ** STEP
_C2 = 1.0 - B2 ** STEP


def _adam_math(w, g, m, v):
    m = B1 * m + (1.0 - B1) * g
    v = B2 * v + (1.0 - B2) * (g * g)
    delta = -LR * ((m / _C1) / (jnp.sqrt(v / _C2) + AEPS) + WD * w)
    return delta, m, v


def _adamw(w, m, v, groups, name):
    R, C = w.shape
    tr = R if R <= 256 else (128 if R % 128 == 0 else 176)
    assert R % tr == 0, (name, R)
    gparts = [p for grp in groups for p in grp]
    sizes = [len(grp) for grp in groups]
    ng = len(gparts)

    def body(*refs):
        w_ref, m_ref, v_ref = refs[:3]
        g_refs = list(refs[3:3 + ng])
        g_out, d_out, m_out, v_out = refs[3 + ng:]
        g = None
        for size in sizes:
            s = None
            for r in [g_refs.pop(0) for _ in range(size)]:
                terms = [r[q] for q in range(r.shape[0])] if len(r.shape) == 3 else [r[...]]
                for t in terms:
                    s = t.astype(F32) if s is None else s + t.astype(F32)
            g = s if g is None else g + s
        delta, mn, vn = _adam_math(w_ref[...], g, m_ref[...], v_ref[...])
        g_out[...] = g
        d_out[...] = delta
        m_out[...] = mn
        v_out[...] = vn

    blk = pl.BlockSpec((tr, C), lambda i: (i, 0))
    g_specs = [blk if p.ndim == 2 else pl.BlockSpec((p.shape[0], tr, C), lambda i: (0, i, 0)) for p in gparts]
    sds = jax.ShapeDtypeStruct((R, C), F32)
    return pl.pallas_call(
        body, name=name, out_shape=(sds, sds, sds, sds), grid=(R // tr,),
        in_specs=[blk, blk, blk] + g_specs, out_specs=(blk, blk, blk, blk),
        compiler_params=_cp(("parallel",)))(w, m, v, *gparts)


def _mod_shard(c_all, w_ada, b_ada_cols):
    n = w_ada.shape[1]
    tn = 512

    def body(c_ref, w_ref, b_ref, o_ref):
        cv = c_ref[...]
        ca = (cv * _sig(cv)).astype(BF16)
        o_ref[...] = jnp.dot(ca, w_ref[...].astype(BF16), preferred_element_type=F32) + b_ref[...]

    return pl.pallas_call(
        body, name="mod_shard", out_shape=jax.ShapeDtypeStruct((N_DEV, n), F32), grid=(n // tn,),
        in_specs=[_full((N_DEV, D_MODEL)), pl.BlockSpec((D_MODEL, tn), lambda j: (0, j)),
                  pl.BlockSpec((1, tn), lambda j: (0, j))],
        out_specs=pl.BlockSpec((N_DEV, tn), lambda j: (0, j)),
        compiler_params=_cp(("parallel",)))(c_all, w_ada, b_ada_cols)


def _ada_grad(c_all, dmod_cols):
    n = dmod_cols.shape[1]
    tn = 512

    def body(c_ref, d_ref, o_ref):
        cv = c_ref[...]
        ca = cv * _sig(cv)
        o_ref[...] = lax.dot_general(ca, d_ref[...], (((0,), (0,)), ((), ())),
                                     preferred_element_type=F32, precision=lax.Precision.HIGHEST)

    return pl.pallas_call(
        body, name="ada_grad", out_shape=jax.ShapeDtypeStruct((D_MODEL, n), F32), grid=(n // tn,),
        in_specs=[_full((N_DEV, D_MODEL)), pl.BlockSpec((N_DEV, tn), lambda j: (0, j))],
        out_specs=pl.BlockSpec((D_MODEL, tn), lambda j: (0, j)),
        compiler_params=_cp(("parallel",)))(c_all, dmod_cols)


def _device_step(x, mod, W, tgt, getw, put, early):
    sh1, sc1, g1, sh2, sc2, g2 = [mod[:, i * D_MODEL:(i + 1) * D_MODEL] for i in range(6)]
    e_re, e_im, bb_re, bb_im = _ssm_prep(W["ssm_a_re"], W["ssm_a_im"], W["ssm_b_re"], W["ssm_b_im"], W["ssm_log_dt"])
    bb, cm = _block_diag_mats(bb_re, bb_im, W["ssm_c_re"], W["ssm_c_im"])
    bb16, cm16 = bb.astype(BF16), cm.astype(BF16)
    tab_f = _scan_tables(e_re, e_im, False)
    tab_b = _scan_tables(e_re, e_im, True)

    h1 = _normmod(x, W["norm1_g"], sc1, sh1, "normmod1")
    w_in = getw("w_in", h1)
    z = _matmul(h1, w_in, "nn", 512, 512, 1024, F32, "mm_w_in")
    yc, scv = _conv_fwd(z, W["conv_w"], W["conv_b"], W["conv_ln_g"], W["conv_ln_b"])
    w_cp = getw("conv_proj", scv)
    y_conv = _matmul(scv, w_cp, "nn", 512, 1024, 512, F32, "mm_conv_proj")
    xs, ys, yg = _ssm_fwd(z, bb16, cm16, W["ssm_d"], tab_f)
    w_glu = getw("ssm_glu", yg)
    zz = _matmul(yg, w_glu, "nn", 512, 1024, 512, F32, "mm_ssm_glu")
    merged = _merge_fwd(z, zz, y_conv)
    w_out = getw("w_out", merged)
    o = _matmul(merged, w_out, "nn", 512, 1024, 1024, F32, "mm_w_out")
    x2, h2 = _resid_normmod(x, o, g1, W["norm2_g"], sc2, sh2, "resid_normmod2")
    w_fi = getw("w_ffn_in", h2)
    f = _matmul(h2, w_fi, "nn", 512, 1408, 1024, F32, "mm_ffn_in")
    act = _ffn_act(f)
    w_fo = getw("w_ffn_out", act)
    o2 = _matmul(act, w_fo, "nn", 512, 1024, FH, F32, "mm_ffn_out")
    dx3, do2, loss8, dfg8, dg2_8 = _final(x2, o2, g2, W["final_g"], tgt)

    sm = {}
    tok = put("w_ffn_out", _matmul(act, do2, "tn", 1408, 1024, 1024, BF16, "mm_g_ffn_out"))
    dact = _matmul(_tie(do2, tok), w_fo, "nt", 512, 1408, 1024, F32, "mm_d_act")
    dfg, dfu = _ffn_bwd(f, dact)
    df = jnp.concatenate([dfg, dfu], axis=1)
    tok = put("w_ffn_in", _matmul(h2, df, "tn", 1024, 1408, 1024, BF16, "mm_g_ffn_in"))
    dh2 = _matmul(_tie(df, tok), w_fi, "nt", 512, 1024, 1408, F32, "mm_d_h2")
    dx2, do, dsh2, dsc2, dn2, dg1_8 = _normmod_bwd(dh2, x2, dx3, W["norm2_g"], sc2, g1, o, "normmod2_bwd")
    tok = put("w_out", _matmul(merged, do, "tn", 1024, 1024, 1024, BF16, "mm_g_w_out"))
    dmerged = _matmul(_tie(do, tok), w_out, "nt", 512, 1024, 1024, F32, "mm_d_merged")
    dyconv, dgl, dzz = _merge_bwd(dmerged, z, zz, y_conv)
    tok = put("ssm_glu", _matmul(yg, dzz, "tn", 512, 1024, 1024, BF16, "mm_g_ssm_glu"))
    tok2 = put("conv_proj", _matmul(scv, _tie(dyconv, tok), "tn", 512, 1024, 1024, BF16, "mm_g_conv_proj"))
    dyg = _matmul(_tie(dzz, tok2), w_glu, "nt", 512, 512, 1024, F32, "mm_d_yg")
    lam, du, dys16, de16, dd8 = _ssm_bwd(dyg, ys, z, xs, cm16.T, bb16.T, W["ssm_d"], tab_b)
    dc_full = _matmul(dys16, xs, "tn", 512, 1024, 1024, F32, "mm_g_ssm_c")
    u = z[:, 2 * CW:3 * CW]
    dbb_full = _matmul(u, lam, "tn", 512, 1024, 1024, F32, "mm_g_ssm_b")
    dsc = _matmul(dyconv, w_cp, "nt", 512, 512, 1024, F32, "mm_d_sc")
    dyc, dlg8, dlb8, dcb8 = _conv_bwd_ln(dsc, yc, W["conv_ln_g"], W["conv_ln_b"])
    dz_conv, dcw = _conv_bwd(dyc, z, W["conv_w"])

    s8 = lambda a: jnp.sum(a, axis=0, keepdims=True)
    de = de16.reshape(2, 8, NST).sum(1)
    de_re, de_im = de[0].reshape(G, P), de[1].reshape(G, P)
    dc_re = _diag_blocks(dc_full[:, :NST])
    dc_im = -_diag_blocks(dc_full[:, NST:])
    dbb_re = jnp.swapaxes(_diag_blocks(dbb_full[:, :NST]), 1, 2)
    dbb_im = jnp.swapaxes(_diag_blocks(dbb_full[:, NST:]), 1, 2)
    _, vjp = jax.vjp(_ssm_prep, W["ssm_a_re"], W["ssm_a_im"], W["ssm_b_re"], W["ssm_b_im"], W["ssm_log_dt"])
    sm["ssm_a_re"], sm["ssm_a_im"], sm["ssm_b_re"], sm["ssm_b_im"], sm["ssm_log_dt"] = vjp((de_re, de_im, dbb_re, dbb_im))
    sm["ssm_c_re"], sm["ssm_c_im"] = dc_re, dc_im
    sm["ssm_d"] = s8(dd8)
    sm["norm2_g"] = s8(dn2)
    sm["conv_b"], sm["conv_ln_g"], sm["conv_ln_b"] = s8(dcb8), s8(dlg8), s8(dlb8)
    sm["conv_w"] = dcw.reshape(KW, 8, CW).sum(1)
    sm["final_g"] = s8(dfg8)
    tok = early(sm)

    dz = jnp.concatenate([dz_conv, du, dgl], axis=1)
    tok = put("w_in", _matmul(h1, _tie(dz, tok), "tn", 1024, 512, 1024, BF16, "mm_g_w_in"))
    dh1 = _matmul(_tie(dz, tok), w_in, "nt", 512, 1024, 1792, F32, "mm_d_h1")
    dx, _, dsh1, dsc1, dn1, _ = _normmod_bwd(dh1, x, dx2, W["norm1_g"], sc1, g1, o, "normmod1_bwd")
    dmod = jnp.concatenate([s8(dsh1), s8(dsc1), s8(dg1_8), s8(dsh2), s8(dsc2), s8(dg2_8)], axis=1)
    return loss8, dx, s8(dn1), dmod


_BIG = ("w_in", "conv_proj", "ssm_glu", "w_out", "w_ffn_in", "w_ffn_out")
_BIG_AXIS = {"w_in": 1, "conv_proj": 1, "ssm_glu": 1, "w_out": 0, "w_ffn_in": 1, "w_ffn_out": 0}
_EARLY = ("conv_w", "conv_b", "conv_ln_g", "conv_ln_b", "ssm_a_re", "ssm_a_im", "ssm_b_re", "ssm_b_im", "ssm_c_re",
          "ssm_c_im", "ssm_d", "ssm_log_dt", "norm2_g", "final_g")
_LATE = ("norm1_g", "b_ada")
_ORDER = ("w_ada", "b_ada", "norm1_g", "w_in", "conv_w", "conv_b", "conv_ln_g", "conv_ln_b", "conv_proj",
          "ssm_a_re", "ssm_a_im", "ssm_b_re", "ssm_b_im", "ssm_c_re", "ssm_c_im", "ssm_d", "ssm_log_dt", "ssm_glu",
          "w_out", "norm2_g", "w_ffn_in", "w_ffn_out", "final_g")
_PACK_COLS = 1024


def _pack_rows(shape):
    return -(-int(np.prod(shape)) // (8 * _PACK_COLS)) * 8


def _pack(arrs):
    parts = []
    for a in arrs:
        flat = a.reshape(-1)
        n = _pack_rows(a.shape)
        parts.append(jnp.pad(flat, (0, n * _PACK_COLS - flat.shape[0])).reshape(n, _PACK_COLS))
    return jnp.concatenate(parts, 0)


def _unpack(packed, shapes):
    out, r = [], 0
    for shp in shapes:
        size = int(np.prod(shp))
        n = _pack_rows(shp)
        out.append(packed[r:r + n].reshape(-1)[:size].reshape(shp))
        r += n
    return out


def kernel(x, c, w_ada, b_ada, norm1_g, w_in, conv_w, conv_b, conv_ln_g, conv_ln_b, conv_proj, ssm_a_re, ssm_a_im, ssm_b_re, ssm_b_im, ssm_c_re, ssm_c_im, ssm_d, ssm_log_dt, ssm_glu, w_out, norm2_g, w_ffn_in, w_ffn_out, final_g, loss_target, m_w_ada, m_b_ada, m_norm1_g, m_w_in, m_conv_w, m_conv_b, m_conv_ln_g, m_conv_ln_b, m_conv_proj, m_ssm_a_re, m_ssm_a_im, m_ssm_b_re, m_ssm_b_im, m_ssm_c_re, m_ssm_c_im, m_ssm_d, m_ssm_log_dt, m_ssm_glu, m_w_out, m_norm2_g, m_w_ffn_in, m_w_ffn_out, m_final_g, v_w_ada, v_b_ada, v_norm1_g, v_w_in, v_conv_w, v_conv_b, v_conv_ln_g, v_conv_ln_b, v_conv_proj, v_ssm_a_re, v_ssm_a_im, v_ssm_b_re, v_ssm_b_im, v_ssm_c_re, v_ssm_c_im, v_ssm_d, v_ssm_log_dt, v_ssm_glu, v_w_out, v_norm2_g, v_w_ffn_in, v_w_ffn_out, v_final_g):
    given = dict(locals())
    mx, my, mc = _me()
    chip = 2 * mx + my
    dev = 4 * mx + 2 * my + mc
    def canon(a):
        return a.reshape(1, -1) if a.ndim <= 2 else a[0]

    wts = {n: canon(given[n]) for n in _ORDER}
    mom = {n: canon(given["m_" + n]) for n in _ORDER}
    var = {n: canon(given["v_" + n]) for n in _ORDER}

    c_all = _allgather8(jnp.broadcast_to(c, (8, D_MODEL)), "gather_c")[:, 0, :]
    n_ada = wts["w_ada"].shape[1]
    b_cols = lax.dynamic_slice(wts["b_ada"], (0, chip * n_ada), (1, n_ada))
    mod_cols = _mod_shard(c_all, wts["w_ada"], b_cols)
    mods = _allgather8(mod_cols, "gather_mod")
    mod = jnp.concatenate([lax.dynamic_index_in_dim(mods[2 * q], dev, 0, keepdims=True) for q in range(N_CHIP)], axis=1)

    W = {n: wts[n] for n in _ORDER if n not in _BIG}
    conv_w_full = _allgather8(jnp.pad(wts["conv_w"], ((0, 1), (0, 0))), "gather_conv_w")
    W["conv_w"] = jnp.concatenate([conv_w_full[2 * q, :KW] for q in range(N_CHIP)], axis=1)

    axes = [_BIG_AXIS[n] for n in _BIG]
    shards = _tie([wts[n].astype(BF16) for n in _BIG], (mod, W["conv_w"]))
    lands = []
    for s, ax in zip(shards, axes):
        shp = list(s.shape)
        shp[ax] *= N_CHIP
        lands.append(lax.empty(tuple(shp), BF16))
    gstate, token = _gather_start(shards, lands, axes)
    gstate = dict(zip(_BIG, gstate))
    mod = mod + token[0:1, 0:1]

    def getw(n, after):
        return _gather_wait(gstate[n], _BIG_AXIS[n], after, "gather_wait_" + n)

    sstate, own, estate = {}, {}, []

    def put(n, g):
        ax = _BIG_AXIS[n]
        k = g.shape[ax] // N_CHIP
        own[n] = lax.dynamic_slice_in_dim(g, chip * k, k, axis=ax)
        sstate[n], tok = _scatter_start(g, ax, "scatter_start_" + n)
        return tok

    def early(sm):
        state, tok = _all8_start(_pack([sm[n] for n in _EARLY]), "small_start")
        estate.append(state)
        return tok

    loss8, dx, dn1, dmod = _device_step(x[0], mod, W, loss_target[0], getw, put, early)
    loss = lax.psum(jnp.sum(loss8), ("x", "y", "c"))

    late = _allgather8(_pack([dn1, dmod]), "gather_late")

    recv = [_scatter_wait(sstate[n], _BIG_AXIS[n], late, "scatter_wait_" + n) for n in _BIG]
    mine = [a for n, r in zip(_BIG, recv) for a in (own[n], r)]
    sib = _swap_sibling(mine)
    allp = _all8_wait(estate[0], late, "small_wait")

    outs = {}
    for i, n in enumerate(_BIG):
        outs[n] = _adamw(wts[n], mom[n], var[n], [mine[2 * i:2 * i + 2], sib[2 * i:2 * i + 2]], "adamw_" + n)

    r1 = _pack_rows((D_MODEL,))
    dmod_all = late[:, r1:, :].reshape(N_DEV, -1)[:, :6 * D_MODEL]
    dmod_cols = lax.dynamic_slice(dmod_all, (0, chip * n_ada), (N_DEV, n_ada))
    g_ada = _ada_grad(c_all, dmod_cols)
    outs["w_ada"] = _adamw(wts["w_ada"], mom["w_ada"], var["w_ada"], [[g_ada]], "adamw_w_ada")

    def packed_params(d, names):
        return _pack([jnp.zeros((KW, CW), F32) if n == "conv_w" else d[n] for n in names])

    for names, parts, nm in ((_EARLY, allp, "adamw_small"), (_LATE, late, "adamw_late")):
        res = _adamw(packed_params(wts, names), packed_params(mom, names), packed_params(var, names), [[parts]], nm)
        shapes = [(KW, CW) if n == "conv_w" else wts[n].shape for n in names]
        unpacked = [_unpack(r, shapes) for r in res]
        for idx, n in enumerate(names):
            outs[n] = tuple(unpacked[q][idx] for q in range(4))
    g_cw = lax.dynamic_slice(outs["conv_w"][0], (0, chip * (CW // N_CHIP)), (KW, CW // N_CHIP))
    pad = lambda a: jnp.pad(a, ((0, 1), (0, 0)))
    r_cw = _adamw(pad(wts["conv_w"]), pad(mom["conv_w"]), pad(var["conv_w"]), [[pad(g_cw)]], "adamw_conv_w")
    outs["conv_w"] = tuple(r[:KW] for r in r_cw)

    def shaped(n, a):
        return a.reshape(given[n].shape)

    result = [loss, dx[None]]
    for q in range(4):
        result += [shaped(n, outs[n][q]) for n in _ORDER]
    return tuple(result)
```

```python
import math

import jax
import jax.numpy as jnp
import numpy as np
from jax import lax
from jax.experimental import pallas as pl
from jax.experimental.pallas import tpu as pltpu

F32 = jnp.float32
BF16 = jnp.bfloat16
EPS = 1e-6
D_MODEL = 1024
CW = 512
KW = 31
HALO = 32
G, P, H = 32, 64, 16
NST = G * P
FH = 2816
N_DEV = 8
N_CHIP = 4
VMEM_LIMIT = 56 * 1024 * 1024
LR, B1, B2, AEPS, WD, STEP = 0.001, 0.9, 0.999, 1e-08, 0.01, 10
MESH = pl.DeviceIdType.MESH


def _cp(sem=None):
    return pltpu.CompilerParams(dimension_semantics=sem, vmem_limit_bytes=VMEM_LIMIT)


def _sig(x):
    return jax.nn.sigmoid(x)


def _full(shape):
    return pl.BlockSpec(shape, lambda *_: (0,) * len(shape))


def _colsum8(v):
    t, c = v.shape
    return jnp.sum(v.reshape(t // 8, 8, c), axis=0)


def _matmul(a, b, mode, tm, tn, tk, out_dtype, name, after=None):
    if mode == "nn":
        (M, K), N = a.shape, b.shape[1]
    elif mode == "nt":
        (M, K), N = a.shape, b.shape[0]
    else:
        (K, M), N = a.shape, b.shape[1]
    tm, tn, tk = min(tm, M), min(tn, N), min(tk, K)
    assert M % tm == 0 and N % tn == 0 and K % tk == 0, (name, M, N, K, tm, tn, tk)
    nk = K // tk
    if mode == "nn":
        a_spec = pl.BlockSpec((tm, tk), lambda i, j, k: (i, k))
        b_spec = pl.BlockSpec((tk, tn), lambda i, j, k: (k, j))
        dims = (((1,), (0,)), ((), ()))
    elif mode == "nt":
        a_spec = pl.BlockSpec((tm, tk), lambda i, j, k: (i, k))
        b_spec = pl.BlockSpec((tn, tk), lambda i, j, k: (j, k))
        dims = (((1,), (1,)), ((), ()))
    else:
        a_spec = pl.BlockSpec((tk, tm), lambda i, j, k: (k, i))
        b_spec = pl.BlockSpec((tk, tn), lambda i, j, k: (k, j))
        dims = (((0,), (0,)), ((), ()))

    def body(a_ref, b_ref, *rest):
        o_ref, acc_ref = rest[-2:]
        k = pl.program_id(2)
        part = lax.dot_general(a_ref[...].astype(BF16), b_ref[...].astype(BF16), dims,
                               preferred_element_type=F32)
        if nk == 1:
            o_ref[...] = part.astype(out_dtype)
        else:
            @pl.when(k == 0)
            def _():
                acc_ref[...] = part

            @pl.when(k > 0)
            def _():
                acc_ref[...] += part

            @pl.when(k == nk - 1)
            def _():
                o_ref[...] = acc_ref[...].astype(out_dtype)

    return pl.pallas_call(
        body, name=name,
        out_shape=jax.ShapeDtypeStruct((M, N), out_dtype),
        grid=(M // tm, N // tn, nk),
        in_specs=[a_spec, b_spec] + ([] if after is None else [pl.BlockSpec(memory_space=pl.ANY)]),
        out_specs=pl.BlockSpec((tm, tn), lambda i, j, k: (i, j)),
        scratch_shapes=[pltpu.VMEM((tm, tn) if nk > 1 else (8, 128), F32)],
        compiler_params=_cp(("parallel", "parallel", "arbitrary")),
    )(*((a, b) if after is None else (a, b, after)))


def _row_tile(S):
    return min(512, S)


def _normmod(x, g, sc, sh, name):
    S, D = x.shape
    tm = _row_tile(S)

    def body(x_ref, g_ref, sc_ref, sh_ref, h_ref):
        xv = x_ref[...]
        r = lax.rsqrt(jnp.mean(xv * xv, axis=-1, keepdims=True) + EPS)
        h_ref[...] = (xv * r * (g_ref[...] * (1.0 + sc_ref[...])) + sh_ref[...]).astype(BF16)

    row = pl.BlockSpec((tm, D), lambda i: (i, 0))
    return pl.pallas_call(
        body, name=name, out_shape=jax.ShapeDtypeStruct((S, D), BF16), grid=(S // tm,),
        in_specs=[row, _full((1, D)), _full((1, D)), _full((1, D))], out_specs=row,
        compiler_params=_cp(("parallel",)))(x, g, sc, sh)


def _resid_normmod(x, o, g1, g, sc, sh, name):
    S, D = x.shape
    tm = _row_tile(S)

    def body(x_ref, o_ref, g1_ref, g_ref, sc_ref, sh_ref, x2_ref, h_ref):
        xv = x_ref[...] + g1_ref[...] * o_ref[...]
        x2_ref[...] = xv
        r = lax.rsqrt(jnp.mean(xv * xv, axis=-1, keepdims=True) + EPS)
        h_ref[...] = (xv * r * (g_ref[...] * (1.0 + sc_ref[...])) + sh_ref[...]).astype(BF16)

    row = pl.BlockSpec((tm, D), lambda i: (i, 0))
    par = _full((1, D))
    return pl.pallas_call(
        body, name=name,
        out_shape=(jax.ShapeDtypeStruct((S, D), F32), jax.ShapeDtypeStruct((S, D), BF16)),
        grid=(S // tm,), in_specs=[row, row, par, par, par, par], out_specs=(row, row),
        compiler_params=_cp(("parallel",)))(x, o, g1, g, sc, sh)


def _conv_fwd(z, conv_w, conv_b, ln_g, ln_b):
    S = z.shape[0]
    tm = min(128, S)
    sub = 32
    hb = tm // HALO

    def body(a_ref, g_ref, ha_ref, hg_ref, w_ref, b_ref, lg_ref, lb_ref, yc_ref, s_ref, ug_ref):
        i = pl.program_id(0)
        halo = ha_ref[...] * _sig(hg_ref[...])
        ug_ref[0:HALO, :] = jnp.where(i == 0, 0.0, halo)
        ug_ref[HALO:, :] = a_ref[...] * _sig(g_ref[...])
        for rb in range(tm // sub):
            acc = jnp.zeros((sub, CW), F32) + b_ref[...]
            for k in range(KW):
                off = rb * sub + HALO - (KW - 1) + k
                acc = acc + w_ref[k:k + 1, :] * ug_ref[off:off + sub, :]
            yc_ref[rb * sub:(rb + 1) * sub, :] = acc
            mu = jnp.mean(acc, axis=-1, keepdims=True)
            cen = acc - mu
            rstd = lax.rsqrt(jnp.mean(cen * cen, axis=-1, keepdims=True) + EPS)
            ln = cen * rstd * lg_ref[...] + lb_ref[...]
            s_ref[rb * sub:(rb + 1) * sub, :] = (ln * _sig(ln)).astype(BF16)

    prev = lambda i: (jnp.maximum(i * hb - 1, 0), 0)
    return pl.pallas_call(
        body, name="conv_fwd",
        out_shape=(jax.ShapeDtypeStruct((S, CW), F32), jax.ShapeDtypeStruct((S, CW), BF16)),
        grid=(S // tm,),
        in_specs=[pl.BlockSpec((tm, CW), lambda i: (i, 0)), pl.BlockSpec((tm, CW), lambda i: (i, 1)),
                  pl.BlockSpec((HALO, CW), prev), pl.BlockSpec((HALO, CW), lambda i: (jnp.maximum(i * hb - 1, 0), 1)),
                  _full((KW, CW)), _full((1, CW)), _full((1, CW)), _full((1, CW))],
        out_specs=(pl.BlockSpec((tm, CW), lambda i: (i, 0)), pl.BlockSpec((tm, CW), lambda i: (i, 0))),
        scratch_shapes=[pltpu.VMEM((tm + HALO, CW), F32)],
        compiler_params=_cp(("parallel",)))(z, z, z, z, conv_w, conv_b, ln_g, ln_b)


def _conv_bwd_ln(dsc, yc, ln_g, ln_b):
    S = yc.shape[0]
    tm = _row_tile(S)

    def body(d_ref, yc_ref, lg_ref, lb_ref, dyc_ref, dlg_ref, dlb_ref, dcb_ref):
        i = pl.program_id(0)
        yc_v = yc_ref[...]
        mu = jnp.mean(yc_v, axis=-1, keepdims=True)
        cen = yc_v - mu
        rstd = lax.rsqrt(jnp.mean(cen * cen, axis=-1, keepdims=True) + EPS)
        yn = cen * rstd
        ln = yn * lg_ref[...] + lb_ref[...]
        sl = _sig(ln)
        dln = d_ref[...] * (sl * (1.0 + ln * (1.0 - sl)))
        dyn = dln * lg_ref[...]
        dyc = rstd * (dyn - jnp.mean(dyn, axis=-1, keepdims=True)
                      - yn * jnp.mean(dyn * yn, axis=-1, keepdims=True))
        dyc_ref[...] = dyc

        @pl.when(i == 0)
        def _():
            dlg_ref[...] = jnp.zeros_like(dlg_ref)
            dlb_ref[...] = jnp.zeros_like(dlb_ref)
            dcb_ref[...] = jnp.zeros_like(dcb_ref)

        dlg_ref[...] += _colsum8(dln * yn)
        dlb_ref[...] += _colsum8(dln)
        dcb_ref[...] += _colsum8(dyc)

    row = pl.BlockSpec((tm, CW), lambda i: (i, 0))
    acc = jax.ShapeDtypeStruct((8, CW), F32)
    return pl.pallas_call(
        body, name="conv_bwd_ln",
        out_shape=(jax.ShapeDtypeStruct((S, CW), F32), acc, acc, acc), grid=(S // tm,),
        in_specs=[row, row, _full((1, CW)), _full((1, CW))],
        out_specs=(row, _full((8, CW)), _full((8, CW)), _full((8, CW))),
        compiler_params=_cp(("arbitrary",)))(dsc, yc, ln_g, ln_b)


def _conv_bwd(dyc, z, conv_w):
    S = z.shape[0]
    tm = min(128, S)
    sub = 32
    hb = tm // HALO
    nt = S // tm

    def body(d_ref, dn_ref, a_ref, g_ref, ha_ref, hg_ref, w_ref, dz_ref, dw_ref, ug_ref, dy_ref):
        i = pl.program_id(0)
        halo = ha_ref[...] * _sig(hg_ref[...])
        ug_ref[0:HALO, :] = jnp.where(i == 0, 0.0, halo)
        a = a_ref[...]
        sg = _sig(g_ref[...])
        ug_ref[HALO:, :] = a * sg
        dy_ref[0:tm, :] = d_ref[...]
        dy_ref[tm:, :] = jnp.where(i == nt - 1, 0.0, dn_ref[...])

        @pl.when(i == 0)
        def _():
            dw_ref[...] = jnp.zeros_like(dw_ref)

        for rb in range(tm // sub):
            r0 = rb * sub
            acc = jnp.zeros((sub, CW), F32)
            dyc_b = dy_ref[r0:r0 + sub, :]
            for k in range(KW):
                up = r0 + (KW - 1) - k
                acc = acc + w_ref[k:k + 1, :] * dy_ref[up:up + sub, :]
                off = r0 + HALO - (KW - 1) + k
                dw_ref[k * 8:(k + 1) * 8, :] += _colsum8(dyc_b * ug_ref[off:off + sub, :])
            a_b = a[r0:r0 + sub, :]
            sg_b = sg[r0:r0 + sub, :]
            dz_ref[r0:r0 + sub, 0:CW] = (acc * sg_b).astype(BF16)
            dz_ref[r0:r0 + sub, CW:2 * CW] = (acc * a_b * sg_b * (1.0 - sg_b)).astype(BF16)

    return pl.pallas_call(
        body, name="conv_bwd",
        out_shape=(jax.ShapeDtypeStruct((S, 2 * CW), BF16), jax.ShapeDtypeStruct((KW * 8, CW), F32)),
        grid=(nt,),
        in_specs=[pl.BlockSpec((tm, CW), lambda i: (i, 0)),
                  pl.BlockSpec((HALO, CW), lambda i: (jnp.minimum((i + 1) * hb, nt * hb - 1), 0)),
                  pl.BlockSpec((tm, CW), lambda i: (i, 0)), pl.BlockSpec((tm, CW), lambda i: (i, 1)),
                  pl.BlockSpec((HALO, CW), lambda i: (jnp.maximum(i * hb - 1, 0), 0)),
                  pl.BlockSpec((HALO, CW), lambda i: (jnp.maximum(i * hb - 1, 0), 1)),
                  _full((KW, CW))],
        out_specs=(pl.BlockSpec((tm, 2 * CW), lambda i: (i, 0)), _full((KW * 8, CW))),
        scratch_shapes=[pltpu.VMEM((tm + HALO, CW), F32), pltpu.VMEM((tm + HALO, CW), F32)],
        compiler_params=_cp(("arbitrary",)))(dyc, dyc, z, z, z, z, conv_w)


_GELU_C = math.sqrt(2.0 / math.pi)


def _gelu(x):
    return 0.5 * x * (1.0 + jnp.tanh(_GELU_C * (x + 0.044715 * x * x * x)))


def _gelu_grad(x):
    t = jnp.tanh(_GELU_C * (x + 0.044715 * x * x * x))
    return 0.5 * (1.0 + t) + 0.5 * x * (1.0 - t * t) * (_GELU_C * (1.0 + 3 * 0.044715 * x * x))


_LW = 512


def _ssm_fwd(z, bb, cm, d, tab):
    S = z.shape[0]
    tm = min(256, S)

    def body(u_ref, bb_ref, cm_ref, d_ref, t_ref, x_ref, ys_ref, yg_ref, car_ref):
        i = pl.program_id(0)

        @pl.when(i == 0)
        def _():
            car_ref[...] = jnp.zeros_like(car_ref)

        u = u_ref[...]
        x_ref[...] = jnp.dot(u.astype(BF16), bb_ref[...], preferred_element_type=F32)
        for c in range(NST // _LW):
            lre = pl.ds(c * _LW, _LW)
            lim = pl.ds(NST + c * _LW, _LW)

            def blk(j, car):
                cr, ci = car
                rows = pl.ds(pl.multiple_of(j * 8, 8), 8)
                r = x_ref[rows, lre]
                im = x_ref[rows, lim]
                for lvl, s in enumerate((1, 2, 4)):
                    mr = t_ref[16 * lvl:16 * lvl + 8, lre]
                    mi = t_ref[16 * lvl + 8:16 * lvl + 16, lre]
                    sr = pltpu.roll(r, s, 0)
                    si = pltpu.roll(im, s, 0)
                    r, im = r + (mr * sr - mi * si), im + (mr * si + mi * sr)
                pr = t_ref[48:56, lre]
                pi_ = t_ref[56:64, lre]
                r, im = r + (pr * cr - pi_ * ci), im + (pr * ci + pi_ * cr)
                x_ref[rows, lre] = r
                x_ref[rows, lim] = im
                return (jnp.broadcast_to(r[7:8, :], (8, _LW)), jnp.broadcast_to(im[7:8, :], (8, _LW)))

            cr, ci = lax.fori_loop(0, tm // 8, blk, (car_ref[:, lre], car_ref[:, lim]))
            car_ref[:, lre] = cr
            car_ref[:, lim] = ci
        ys = jnp.dot(x_ref[...].astype(BF16), cm_ref[...], preferred_element_type=F32) + d_ref[...] * u
        ys_ref[...] = ys
        yg_ref[...] = _gelu(ys).astype(BF16)

    return pl.pallas_call(
        body, name="ssm_fwd",
        out_shape=(jax.ShapeDtypeStruct((S, 2 * NST), F32), jax.ShapeDtypeStruct((S, CW), F32),
                   jax.ShapeDtypeStruct((S, CW), BF16)),
        grid=(S // tm,),
        in_specs=[pl.BlockSpec((tm, CW), lambda i: (i, 2)), _full((CW, 2 * NST)), _full((2 * NST, CW)),
                  _full((1, CW)), _full((64, NST))],
        out_specs=(pl.BlockSpec((tm, 2 * NST), lambda i: (i, 0)), pl.BlockSpec((tm, CW), lambda i: (i, 0)),
                   pl.BlockSpec((tm, CW), lambda i: (i, 0))),
        scratch_shapes=[pltpu.VMEM((8, 2 * NST), F32)],
        compiler_params=_cp(("arbitrary",)))(z, bb, cm, d, tab)


def _ssm_bwd(dyg, ys, z, xs, cmt, bbt, d, tab):
    S = z.shape[0]
    tm = min(256, S)
    nt = S // tm

    def body(dyg_ref, ys_ref, u_ref, x_ref, cmt_ref, bbt_ref, d_ref, t_ref,
             lam_ref, du_ref, dys_ref, de_ref, dd_ref, car_ref):
        i = pl.program_id(0)

        @pl.when(i == 0)
        def _():
            car_ref[...] = jnp.zeros_like(car_ref)
            de_ref[...] = jnp.zeros_like(de_ref)
            dd_ref[...] = jnp.zeros_like(dd_ref)

        u = u_ref[...]
        dys = dyg_ref[...] * _gelu_grad(ys_ref[...])
        dys_ref[...] = dys.astype(BF16)
        dd_ref[...] += _colsum8(dys * u)
        lam_ref[...] = jnp.dot(dys.astype(BF16), cmt_ref[...], preferred_element_type=F32)
        row = lax.broadcasted_iota(jnp.int32, (8, _LW), 0)
        for c in range(NST // _LW):
            lre = pl.ds(c * _LW, _LW)
            lim = pl.ds(NST + c * _LW, _LW)

            def blk(jj, car):
                cr, ci, ar, ai = car
                j = tm // 8 - 1 - jj
                rows = pl.ds(pl.multiple_of(j * 8, 8), 8)
                r = lam_ref[rows, lre]
                im = lam_ref[rows, lim]
                for lvl, s in enumerate((1, 2, 4)):
                    mr = t_ref[16 * lvl:16 * lvl + 8, lre]
                    mi = t_ref[16 * lvl + 8:16 * lvl + 16, lre]
                    sr = pltpu.roll(r, 8 - s, 0)
                    si = pltpu.roll(im, 8 - s, 0)
                    r, im = r + (mr * sr - mi * si), im + (mr * si + mi * sr)
                pr = t_ref[48:56, lre]
                pi_ = t_ref[56:64, lre]
                r, im = r + (pr * cr - pi_ * ci), im + (pr * ci + pi_ * cr)
                lam_ref[rows, lre] = r
                lam_ref[rows, lim] = im
                nr = jnp.where(row == 7, cr, pltpu.roll(r, 7, 0))
                ni = jnp.where(row == 7, ci, pltpu.roll(im, 7, 0))
                xr = x_ref[rows, lre]
                xi = x_ref[rows, lim]
                ar = ar + (nr * xr + ni * xi)
                ai = ai + (ni * xr - nr * xi)
                return (jnp.broadcast_to(r[0:1, :], (8, _LW)), jnp.broadcast_to(im[0:1, :], (8, _LW)), ar, ai)

            zero = jnp.zeros((8, _LW), F32)
            cr, ci, ar, ai = lax.fori_loop(0, tm // 8, blk, (car_ref[:, lre], car_ref[:, lim], zero, zero))
            car_ref[:, lre] = cr
            car_ref[:, lim] = ci
            de_ref[0:8, lre] += ar
            de_ref[8:16, lre] += ai
        du = jnp.dot(lam_ref[...].astype(BF16), bbt_ref[...], preferred_element_type=F32) + dys * d_ref[...]
        du_ref[...] = du.astype(BF16)

    rev = lambda i: (nt - 1 - i, 0)
    return pl.pallas_call(
        body, name="ssm_bwd",
        out_shape=(jax.ShapeDtypeStruct((S, 2 * NST), F32), jax.ShapeDtypeStruct((S, CW), BF16),
                   jax.ShapeDtypeStruct((S, CW), BF16), jax.ShapeDtypeStruct((16, NST), F32),
                   jax.ShapeDtypeStruct((8, CW), F32)),
        grid=(nt,),
        in_specs=[pl.BlockSpec((tm, CW), rev), pl.BlockSpec((tm, CW), rev),
                  pl.BlockSpec((tm, CW), lambda i: (nt - 1 - i, 2)), pl.BlockSpec((tm, 2 * NST), rev),
                  _full((CW, 2 * NST)), _full((2 * NST, CW)), _full((1, CW)), _full((64, NST))],
        out_specs=(pl.BlockSpec((tm, 2 * NST), rev), pl.BlockSpec((tm, CW), rev), pl.BlockSpec((tm, CW), rev),
                   _full((16, NST)), _full((8, CW))),
        scratch_shapes=[pltpu.VMEM((8, 2 * NST), F32)],
        compiler_params=_cp(("arbitrary",)))(dyg, ys, z, xs, cmt, bbt, d, tab)


def _ssm_prep(a_re, a_im, b_re, b_im, log_dt):
    dt = jnp.exp(log_dt.reshape(G))[:, None]
    mag = jnp.exp(dt * a_re)
    e_re, e_im = mag * jnp.cos(dt * a_im), mag * jnp.sin(dt * a_im)
    n_re, n_im = e_re - 1.0, e_im
    den = a_re * a_re + a_im * a_im
    q_re = (n_re * a_re + n_im * a_im) / den
    q_im = (n_im * a_re - n_re * a_im) / den
    bb_re = q_re[..., None] * b_re - q_im[..., None] * b_im
    bb_im = q_re[..., None] * b_im + q_im[..., None] * b_re
    return e_re, e_im, bb_re, bb_im


def _scan_tables(e_re, e_im, reverse):
    er = e_re.reshape(1, NST)
    ei = e_im.reshape(1, NST)
    if reverse:
        ei = -ei
    pows = [(er, ei)]
    for _ in range(7):
        pr, pi_ = pows[-1]
        pows.append((pr * er - pi_ * ei, pr * ei + pi_ * er))
    row = jnp.arange(8)[:, None]
    out = []
    for s in (1, 2, 4):
        pr, pi_ = pows[s - 1]
        keep = (row + s <= 7) if reverse else (row >= s)
        out += [jnp.where(keep, pr, 0.0), jnp.where(keep, pi_, 0.0)]
    allr = jnp.concatenate([p[0] for p in pows], 0)
    alli = jnp.concatenate([p[1] for p in pows], 0)
    if reverse:
        allr, alli = allr[::-1], alli[::-1]
    out += [allr, alli]
    return jnp.concatenate(out, 0).astype(F32)


def _block_diag_mats(bb_re, bb_im, c_re, c_im):
    eye = jnp.eye(G, dtype=F32)
    bre = jnp.einsum("gph,gk->ghkp", bb_re, eye).reshape(CW, NST)
    bim = jnp.einsum("gph,gk->ghkp", bb_im, eye).reshape(CW, NST)
    bb = jnp.concatenate([bre, bim], 1)
    cre = jnp.einsum("ghp,gk->gpkh", c_re, eye).reshape(NST, CW)
    cim = jnp.einsum("ghp,gk->gpkh", c_im, eye).reshape(NST, CW)
    cm = jnp.concatenate([cre, -cim], 0)
    return bb, cm


def _diag_blocks(full):
    return jnp.einsum("ghkp,gk->ghp", full.reshape(G, H, G, P), jnp.eye(G, dtype=F32))


def _merge_fwd(z, zz, y_conv):
    S = z.shape[0]
    tm = _row_tile(S)
    D = D_MODEL

    def body(glc_ref, gls_ref, za_ref, zb_ref, yc_ref, m_ref):
        y_ssm = za_ref[...] * _sig(zb_ref[...])
        m_ref[...] = (_sig(glc_ref[...]) * yc_ref[...] + _sig(gls_ref[...]) * y_ssm).astype(BF16)

    return pl.pallas_call(
        body, name="merge_fwd", out_shape=jax.ShapeDtypeStruct((S, D), BF16), grid=(S // tm, 2),
        in_specs=[pl.BlockSpec((tm, CW), lambda i, j: (i, 3 + j)), pl.BlockSpec((tm, CW), lambda i, j: (i, 5 + j)),
                  pl.BlockSpec((tm, CW), lambda i, j: (i, j)), pl.BlockSpec((tm, CW), lambda i, j: (i, 2 + j)),
                  pl.BlockSpec((tm, CW), lambda i, j: (i, j))],
        out_specs=pl.BlockSpec((tm, CW), lambda i, j: (i, j)),
        compiler_params=_cp(("parallel", "parallel")))(z, z, zz, zz, y_conv)


def _merge_bwd(dm, z, zz, y_conv):
    S = z.shape[0]
    tm = min(256, S)
    D = D_MODEL

    def body(dm_ref, glc0_ref, glc1_ref, gls0_ref, gls1_ref, za_ref, zb_ref, yc_ref, dyc_ref, dgl_ref, dzz_ref):
        for half, (glc_ref, gls_ref) in enumerate(((glc0_ref, gls0_ref), (glc1_ref, gls1_ref))):
            lo, hi = half * CW, (half + 1) * CW
            dm_v = dm_ref[:, lo:hi]
            sgc = _sig(glc_ref[...])
            sgs = _sig(gls_ref[...])
            szb = _sig(zb_ref[:, lo:hi])
            za = za_ref[:, lo:hi]
            dyc_ref[:, lo:hi] = (dm_v * sgc).astype(BF16)
            dgl_ref[:, lo:hi] = (dm_v * yc_ref[:, lo:hi] * sgc * (1.0 - sgc)).astype(BF16)
            dys = dm_v * sgs
            dgl_ref[:, D + lo:D + hi] = (dys * (za * szb) * (1.0 - sgs)).astype(BF16)
            dzz_ref[:, lo:hi] = (dys * szb).astype(BF16)
            dzz_ref[:, D + lo:D + hi] = (dys * za * szb * (1.0 - szb)).astype(BF16)

    zb_ = lambda j: pl.BlockSpec((tm, CW), lambda i: (i, j))
    wide = lambda j: pl.BlockSpec((tm, D), lambda i: (i, j))
    return pl.pallas_call(
        body, name="merge_bwd",
        out_shape=(jax.ShapeDtypeStruct((S, D), BF16), jax.ShapeDtypeStruct((S, 2 * D), BF16),
                   jax.ShapeDtypeStruct((S, 2 * D), BF16)),
        grid=(S // tm,),
        in_specs=[wide(0), zb_(3), zb_(4), zb_(5), zb_(6), wide(0), wide(1), wide(0)],
        out_specs=(wide(0), pl.BlockSpec((tm, 2 * D), lambda i: (i, 0)), pl.BlockSpec((tm, 2 * D), lambda i: (i, 0))),
        compiler_params=_cp(("parallel",)))(dm, z, z, z, z, zz, zz, y_conv)


def _ffn_act(f):
    S = f.shape[0]
    tm = _row_tile(S)
    tn = 1408

    def body(g_ref, u_ref, a_ref):
        gv = g_ref[...]
        a_ref[...] = (gv * _sig(gv) * u_ref[...]).astype(BF16)

    return pl.pallas_call(
        body, name="ffn_act", out_shape=jax.ShapeDtypeStruct((S, FH), BF16), grid=(S // tm, FH // tn),
        in_specs=[pl.BlockSpec((tm, tn), lambda i, j: (i, j)), pl.BlockSpec((tm, tn), lambda i, j: (i, j + FH // tn))],
        out_specs=pl.BlockSpec((tm, tn), lambda i, j: (i, j)),
        compiler_params=_cp(("parallel", "parallel")))(f, f)


def _ffn_bwd(f, dact):
    S = f.shape[0]
    tm = _row_tile(S)
    tn = 1408
    nb = FH // tn

    def body(g_ref, u_ref, d_ref, dg_ref, du_ref):
        gv = g_ref[...]
        sg = _sig(gv)
        dv = d_ref[...]
        dg_ref[...] = (dv * u_ref[...] * (sg * (1.0 + gv * (1.0 - sg)))).astype(BF16)
        du_ref[...] = (dv * gv * sg).astype(BF16)

    lo = pl.BlockSpec((tm, tn), lambda i, j: (i, j))
    hi = pl.BlockSpec((tm, tn), lambda i, j: (i, j + nb))
    return pl.pallas_call(
        body, name="ffn_bwd",
        out_shape=(jax.ShapeDtypeStruct((S, FH), BF16), jax.ShapeDtypeStruct((S, FH), BF16)),
        grid=(S // tm, nb), in_specs=[lo, hi, lo], out_specs=(lo, lo),
        compiler_params=_cp(("parallel", "parallel")))(f, f, dact)


def _final(x2, o2, g2, fg, tgt):
    S, D = x2.shape
    tm = _row_tile(S)

    def body(x2_ref, o2_ref, g2_ref, fg_ref, t_ref, dx3_ref, do2_ref, ls_ref, dfg_ref, dg2_ref):
        i = pl.program_id(0)
        o2 = o2_ref[...]
        x3 = x2_ref[...] + g2_ref[...] * o2
        r = lax.rsqrt(jnp.mean(x3 * x3, axis=-1, keepdims=True) + EPS)
        xn = x3 * r
        err = xn * fg_ref[...] - t_ref[...]
        dy = err * (1.0 / D)
        dxn = dy * fg_ref[...]
        dx3 = r * (dxn - xn * jnp.mean(dxn * xn, axis=-1, keepdims=True))
        dx3_ref[...] = dx3
        do2_ref[...] = (dx3 * g2_ref[...]).astype(BF16)

        @pl.when(i == 0)
        def _():
            ls_ref[...] = jnp.zeros_like(ls_ref)
            dfg_ref[...] = jnp.zeros_like(dfg_ref)
            dg2_ref[...] = jnp.zeros_like(dg2_ref)

        e2 = _colsum8(err * err)
        lanes = e2[:, 0:128]
        for q in range(1, D // 128):
            lanes = lanes + e2[:, q * 128:(q + 1) * 128]
        ls_ref[...] += lanes * (0.5 / D)
        dfg_ref[...] += _colsum8(dy * xn)
        dg2_ref[...] += _colsum8(dx3 * o2)

    row = pl.BlockSpec((tm, D), lambda i: (i, 0))
    par = _full((1, D))
    return pl.pallas_call(
        body, name="final_loss",
        out_shape=(jax.ShapeDtypeStruct((S, D), F32), jax.ShapeDtypeStruct((S, D), BF16),
                   jax.ShapeDtypeStruct((8, 128), F32), jax.ShapeDtypeStruct((8, D), F32),
                   jax.ShapeDtypeStruct((8, D), F32)),
        grid=(S // tm,), in_specs=[row, row, par, par, row],
        out_specs=(row, row, _full((8, 128)), _full((8, D)), _full((8, D))),
        compiler_params=_cp(("arbitrary",)))(x2, o2, g2, fg, tgt)


def _normmod_bwd(dh, xin, dres, g, sc, gate, o, name):
    S, D = xin.shape
    tm = _row_tile(S)

    def body(dh_ref, x_ref, dr_ref, g_ref, sc_ref, gate_ref, o_ref, dx_ref, do_ref, dsh_ref, dsc_ref, dg_ref, dgate_ref):
        i = pl.program_id(0)
        xv = x_ref[...]
        r = lax.rsqrt(jnp.mean(xv * xv, axis=-1, keepdims=True) + EPS)
        xn = xv * r
        dh_v = dh_ref[...]
        gv = g_ref[...]
        scale = 1.0 + sc_ref[...]
        dxn = dh_v * (gv * scale)
        dx = dr_ref[...] + r * (dxn - xn * jnp.mean(dxn * xn, axis=-1, keepdims=True))
        dx_ref[...] = dx
        do_ref[...] = (dx * gate_ref[...]).astype(BF16)

        @pl.when(i == 0)
        def _():
            dsh_ref[...] = jnp.zeros_like(dsh_ref)
            dsc_ref[...] = jnp.zeros_like(dsc_ref)
            dg_ref[...] = jnp.zeros_like(dg_ref)
            dgate_ref[...] = jnp.zeros_like(dgate_ref)

        hx = dh_v * xn
        dsh_ref[...] += _colsum8(dh_v)
        dsc_ref[...] += _colsum8(hx) * gv
        dg_ref[...] += _colsum8(hx) * scale
        dgate_ref[...] += _colsum8(dx * o_ref[...])

    row = pl.BlockSpec((tm, D), lambda i: (i, 0))
    par = _full((1, D))
    acc = jax.ShapeDtypeStruct((8, D), F32)
    return pl.pallas_call(
        body, name=name,
        out_shape=(jax.ShapeDtypeStruct((S, D), F32), jax.ShapeDtypeStruct((S, D), BF16), acc, acc, acc, acc),
        grid=(S // tm,), in_specs=[row, row, row, par, par, par, row],
        out_specs=(row, row, _full((8, D)), _full((8, D)), _full((8, D)), _full((8, D))),
        compiler_params=_cp(("arbitrary",)))(dh, xin, dres, g, sc, gate, o)


def _me():
    return lax.axis_index("x"), lax.axis_index("y"), lax.axis_index("c")


def _allgather8(v, name):
    R, C = v.shape

    def body(v_ref, out_ref, send_sems, recv_sems, local_sem):
        x, y, c = _me()
        mine = pltpu.make_async_copy(v_ref, out_ref.at[4 * x + 2 * y + c], local_sem)
        mine.start()
        copies = []
        for k in range(1, N_DEV):
            fx, fy, fc = (k >> 2) & 1, (k >> 1) & 1, k & 1
            peer = (x ^ fx, y ^ fy, c ^ fc)
            copies.append(pltpu.make_async_remote_copy(
                src_ref=v_ref, dst_ref=out_ref.at[4 * x + 2 * y + c],
                send_sem=send_sems.at[k - 1], recv_sem=recv_sems.at[k - 1],
                device_id=peer, device_id_type=MESH))
        for cp in copies:
            cp.start()
        for k in range(1, N_DEV):
            fx, fy, fc = (k >> 2) & 1, (k >> 1) & 1, k & 1
            src_slot = 4 * (x ^ fx) + 2 * (y ^ fy) + (c ^ fc)
            pltpu.make_async_remote_copy(
                src_ref=v_ref, dst_ref=out_ref.at[src_slot],
                send_sem=send_sems.at[k - 1], recv_sem=recv_sems.at[k - 1],
                device_id=(x ^ fx, y ^ fy, c ^ fc), device_id_type=MESH).wait_recv()
        for cp in copies:
            cp.wait_send()
        mine.wait()

    return pl.pallas_call(
        body, name=name, out_shape=jax.ShapeDtypeStruct((N_DEV, R, C), v.dtype),
        in_specs=[pl.BlockSpec(memory_space=pltpu.VMEM)], out_specs=pl.BlockSpec(memory_space=pltpu.VMEM),
        scratch_shapes=[pltpu.SemaphoreType.DMA((N_DEV - 1,)), pltpu.SemaphoreType.DMA((N_DEV - 1,)),
                        pltpu.SemaphoreType.DMA],
        compiler_params=pltpu.CompilerParams(vmem_limit_bytes=VMEM_LIMIT))(v)


def _swap_sibling(arrs):
    nw = len(arrs)

    def body(*refs):
        ins, outs = refs[:nw], refs[nw:2 * nw]
        send_sems, recv_sems = refs[2 * nw:]
        x, y, c = _me()
        copies = [pltpu.make_async_remote_copy(
            src_ref=ins[w], dst_ref=outs[w], send_sem=send_sems.at[w], recv_sem=recv_sems.at[w],
            device_id=(x, y, 1 - c), device_id_type=MESH) for w in range(nw)]
        for cp in copies:
            cp.start()
        for cp in copies:
            cp.wait_recv()
        for cp in copies:
            cp.wait_send()

    hbm = pl.BlockSpec(memory_space=pltpu.HBM)
    return pl.pallas_call(
        body, name="swap_sibling", out_shape=tuple(jax.ShapeDtypeStruct(a.shape, a.dtype) for a in arrs),
        in_specs=[hbm] * nw, out_specs=tuple([hbm] * nw),
        scratch_shapes=[pltpu.SemaphoreType.DMA((nw,)), pltpu.SemaphoreType.DMA((nw,))],
        compiler_params=pltpu.CompilerParams(vmem_limit_bytes=VMEM_LIMIT))(*arrs)


_HBM = pl.BlockSpec(memory_space=pltpu.HBM)
_SEM = pl.BlockSpec(memory_space=pltpu.SEMAPHORE)
_EFFECT = pltpu.SideEffectType.DATAFLOW_SIDE_EFFECTING
_N_PEER = N_CHIP - 1


def _chip_part(ref, axis, n, chip):
    start = pl.multiple_of(chip * n, 8)
    return ref.at[pl.ds(start, n), :] if axis == 0 else ref.at[:, pl.ds(start, n)]


def _gather_copy(k, src_ref, land_ref, send_sems, recv_sems, axis, arriving):
    x, y, c = _me()
    px, py = x ^ ((k >> 1) & 1), y ^ (k & 1)
    chip = 2 * px + py if arriving else 2 * x + y
    return pltpu.make_async_remote_copy(
        src_ref=src_ref, dst_ref=_chip_part(land_ref, axis, src_ref.shape[axis], chip),
        send_sem=send_sems.at[k - 1], recv_sem=recv_sems.at[k - 1], device_id=(px, py, c), device_id_type=MESH)


def _scatter_copy(k, grad_ref, land_ref, send_sems, recv_sems, axis):
    x, y, c = _me()
    px, py = x ^ ((k >> 1) & 1), y ^ (k & 1)
    return pltpu.make_async_remote_copy(
        src_ref=_chip_part(grad_ref, axis, grad_ref.shape[axis] // N_CHIP, 2 * px + py), dst_ref=land_ref.at[k - 1],
        send_sem=send_sems.at[k - 1], recv_sem=recv_sems.at[k - 1], device_id=(px, py, c), device_id_type=MESH)


def _own_copy(src_ref, land_ref, sends, axis):
    x, y, _ = _me()
    return pltpu.make_async_copy(src_ref, _chip_part(land_ref, axis, src_ref.shape[axis], 2 * x + y),
                                 sends.at[_N_PEER])


def _gather_start(shards, lands, axes, after):
    nw = len(shards)

    def body(*refs):
        srcs, zones = refs[:nw], refs[nw:2 * nw]
        sends, recvs = refs[2 * nw + 1:3 * nw + 1], refs[3 * nw + 1:4 * nw + 1]
        token = refs[-1]
        for w in range(nw):
            for k in range(1, N_CHIP):
                _gather_copy(k, srcs[w], zones[w], sends[w], recvs[w], axes[w], False).start()
        for w in range(nw):
            _own_copy(srcs[w], zones[w], sends[w], axes[w]).start()
        token[...] = jnp.zeros_like(token)

    outs = pl.pallas_call(
        body, name="gather_start",
        out_shape=tuple([pltpu.SemaphoreType.DMA((_N_PEER + 1,))] * nw + [pltpu.SemaphoreType.DMA((_N_PEER,))] * nw
                        + [pltpu.HBM(a.shape, a.dtype) for a in list(shards) + list(lands)]
                        + [jax.ShapeDtypeStruct((8, 128), F32)]),
        in_specs=[_HBM] * (2 * nw) + [pl.BlockSpec(memory_space=pl.ANY)],
        out_specs=tuple([_SEM] * (2 * nw) + [_HBM] * (2 * nw) + [pl.BlockSpec(memory_space=pltpu.VMEM)]),
        input_output_aliases={i: 2 * nw + i for i in range(2 * nw)},
        compiler_params=pltpu.CompilerParams(has_side_effects=_EFFECT),
    )(*([pltpu.with_memory_space_constraint(a, pltpu.HBM) for a in list(shards) + list(lands)] + [after]))
    per_weight = [(outs[w], outs[nw + w], outs[2 * nw + w], outs[3 * nw + w]) for w in range(nw)]
    return per_weight, outs[-1]


def _gather_wait(state, axis, after, name):
    send_sems, recv_sems, shard, land = state

    def body(src_ref, land_ref, sends, recvs, after_ref, src_dead, got_ref):
        for k in range(1, N_CHIP):
            _gather_copy(k, src_ref, land_ref, sends, recvs, axis, False).wait_send()
            _gather_copy(k, src_ref, land_ref, sends, recvs, axis, True).wait_recv()
        _own_copy(src_ref, land_ref, sends, axis).wait()

    return pl.pallas_call(
        body, name=name, out_shape=(pltpu.HBM(shard.shape, shard.dtype), pltpu.HBM(land.shape, land.dtype)),
        in_specs=[_HBM, _HBM, _SEM, _SEM, pl.BlockSpec(memory_space=pl.ANY)], out_specs=(_HBM, _HBM),
        input_output_aliases={0: 0, 1: 1},
        compiler_params=pltpu.CompilerParams(has_side_effects=_EFFECT),
    )(shard, land, send_sems, recv_sems, after)[1]


def _all8_copy(k, v_ref, land_ref, send_sems, recv_sems, arriving):
    x, y, c = _me()
    px, py, pc = x ^ ((k >> 2) & 1), y ^ ((k >> 1) & 1), c ^ (k & 1)
    slot = 4 * px + 2 * py + pc if arriving else 4 * x + 2 * y + c
    return pltpu.make_async_remote_copy(
        src_ref=v_ref, dst_ref=land_ref.at[slot], send_sem=send_sems.at[k - 1], recv_sem=recv_sems.at[k - 1],
        device_id=(px, py, pc), device_id_type=MESH)


def _all8_own(v_ref, land_ref, send_sems):
    x, y, c = _me()
    return pltpu.make_async_copy(v_ref, land_ref.at[4 * x + 2 * y + c], send_sems.at[N_DEV - 1])


def _all8_start(v, name):
    land = lax.empty((N_DEV,) + v.shape, v.dtype)

    def body(v_ref, land_ref, sends, recvs, v_thru, land_thru, token):
        for k in range(1, N_DEV):
            _all8_copy(k, v_ref, land_ref, sends, recvs, False).start()
        _all8_own(v_ref, land_ref, sends).start()
        token[...] = jnp.zeros_like(token)

    outs = pl.pallas_call(
        body, name=name,
        out_shape=(pltpu.SemaphoreType.DMA((N_DEV,)), pltpu.SemaphoreType.DMA((N_DEV - 1,)),
                   pltpu.HBM(v.shape, v.dtype), pltpu.HBM(land.shape, land.dtype),
                   jax.ShapeDtypeStruct((8, 128), F32)),
        in_specs=[_HBM, _HBM], out_specs=(_SEM, _SEM, _HBM, _HBM, pl.BlockSpec(memory_space=pltpu.VMEM)),
        input_output_aliases={0: 2, 1: 3},
        compiler_params=pltpu.CompilerParams(has_side_effects=_EFFECT),
    )(pltpu.with_memory_space_constraint(v, pltpu.HBM), pltpu.with_memory_space_constraint(land, pltpu.HBM))
    return outs[:4], outs[4]


def _all8_wait(state, after, name):
    send_sems, recv_sems, v, land = state

    def body(v_ref, land_ref, sends, recvs, after_ref, v_dead, got_ref):
        for k in range(1, N_DEV):
            _all8_copy(k, v_ref, land_ref, sends, recvs, False).wait_send()
            _all8_copy(k, v_ref, land_ref, sends, recvs, True).wait_recv()
        _all8_own(v_ref, land_ref, sends).wait()

    return pl.pallas_call(
        body, name=name, out_shape=(pltpu.HBM(v.shape, v.dtype), pltpu.HBM(land.shape, land.dtype)),
        in_specs=[_HBM, _HBM, _SEM, _SEM, pl.BlockSpec(memory_space=pl.ANY)], out_specs=(_HBM, _HBM),
        input_output_aliases={0: 0, 1: 1},
        compiler_params=pltpu.CompilerParams(has_side_effects=_EFFECT),
    )(v, land, send_sems, recv_sems, after)[1]


def _scatter_start(grad, axis, name):
    shp = list(grad.shape)
    shp[axis] //= N_CHIP
    land = lax.empty((_N_PEER,) + tuple(shp), grad.dtype)

    def body(grad_ref, land_ref, sends, recvs, grad_thru, land_thru, token):
        for k in range(1, N_CHIP):
            _scatter_copy(k, grad_ref, land_ref, sends, recvs, axis).start()
        token[...] = jnp.zeros_like(token)

    sem = pltpu.SemaphoreType.DMA((_N_PEER,))
    outs = pl.pallas_call(
        body, name=name,
        out_shape=(sem, sem, pltpu.HBM(grad.shape, grad.dtype), pltpu.HBM(land.shape, land.dtype),
                   jax.ShapeDtypeStruct((8, 128), F32)),
        in_specs=[_HBM, _HBM], out_specs=(_SEM, _SEM, _HBM, _HBM, pl.BlockSpec(memory_space=pltpu.VMEM)),
        input_output_aliases={0: 2, 1: 3},
        compiler_params=pltpu.CompilerParams(has_side_effects=_EFFECT),
    )(pltpu.with_memory_space_constraint(grad, pltpu.HBM), pltpu.with_memory_space_constraint(land, pltpu.HBM))
    return outs[:4], outs[4]


def _scatter_wait(state, axis, after, name):
    send_sems, recv_sems, grad, land = state

    def body(grad_ref, land_ref, sends, recvs, after_ref, grad_dead, got_ref):
        for k in range(1, N_CHIP):
            cp = _scatter_copy(k, grad_ref, land_ref, sends, recvs, axis)
            cp.wait_send()
            cp.wait_recv()

    return pl.pallas_call(
        body, name=name, out_shape=(pltpu.HBM(grad.shape, grad.dtype), pltpu.HBM(land.shape, land.dtype)),
        in_specs=[_HBM, _HBM, _SEM, _SEM, pl.BlockSpec(memory_space=pl.ANY)], out_specs=(_HBM, _HBM),
        input_output_aliases={0: 0, 1: 1},
        compiler_params=pltpu.CompilerParams(has_side_effects=_EFFECT),
    )(grad, land, send_sems, recv_sems, after)[1]


_C1 = 1.0 - B1 ** STEP
_C2 = 1.0 - B2 ** STEP


def _adam_math(w, g, m, v):
    m = B1 * m + (1.0 - B1) * g
    v = B2 * v + (1.0 - B2) * (g * g)
    delta = -LR * ((m / _C1) / (jnp.sqrt(v / _C2) + AEPS) + WD * w)
    return delta, m, v


def _adamw(w, m, v, groups, name):
    R, C = w.shape
    tr = R if R <= 256 else (128 if R % 128 == 0 else 176)
    assert R % tr == 0, (name, R)
    gparts = [p for grp in groups for p in grp]
    sizes = [len(grp) for grp in groups]
    ng = len(gparts)

    def body(*refs):
        w_ref, m_ref, v_ref = refs[:3]
        g_refs = list(refs[3:3 + ng])
        g_out, d_out, m_out, v_out = refs[3 + ng:]
        g = None
        for size in sizes:
            s = None
            for r in [g_refs.pop(0) for _ in range(size)]:
                terms = [r[q] for q in range(r.shape[0])] if len(r.shape) == 3 else [r[...]]
                for t in terms:
                    s = t.astype(F32) if s is None else s + t.astype(F32)
            g = s if g is None else g + s
        delta, mn, vn = _adam_math(w_ref[...], g, m_ref[...], v_ref[...])
        g_out[...] = g
        d_out[...] = delta
        m_out[...] = mn
        v_out[...] = vn

    blk = pl.BlockSpec((tr, C), lambda i: (i, 0))
    g_specs = [blk if p.ndim == 2 else pl.BlockSpec((p.shape[0], tr, C), lambda i: (0, i, 0)) for p in gparts]
    sds = jax.ShapeDtypeStruct((R, C), F32)
    return pl.pallas_call(
        body, name=name, out_shape=(sds, sds, sds, sds), grid=(R // tr,),
        in_specs=[blk, blk, blk] + g_specs, out_specs=(blk, blk, blk, blk),
        compiler_params=_cp(("parallel",)))(w, m, v, *gparts)


def _mod_shard(c_all, w_ada, b_ada_cols):
    n = w_ada.shape[1]
    tn = 512

    def body(c_ref, w_ref, b_ref, o_ref):
        cv = c_ref[...]
        ca = (cv * _sig(cv)).astype(BF16)
        o_ref[...] = jnp.dot(ca, w_ref[...].astype(BF16), preferred_element_type=F32) + b_ref[...]

    return pl.pallas_call(
        body, name="mod_shard", out_shape=jax.ShapeDtypeStruct((N_DEV, n), F32), grid=(n // tn,),
        in_specs=[_full((N_DEV, D_MODEL)), pl.BlockSpec((D_MODEL, tn), lambda j: (0, j)),
                  pl.BlockSpec((1, tn), lambda j: (0, j))],
        out_specs=pl.BlockSpec((N_DEV, tn), lambda j: (0, j)),
        compiler_params=_cp(("parallel",)))(c_all, w_ada, b_ada_cols)


def _ada_grad(c_all, dmod_cols):
    n = dmod_cols.shape[1]
    tn = 512

    def body(c_ref, d_ref, o_ref):
        cv = c_ref[...]
        ca = cv * _sig(cv)
        o_ref[...] = lax.dot_general(ca, d_ref[...], (((0,), (0,)), ((), ())),
                                     preferred_element_type=F32, precision=lax.Precision.HIGHEST)

    return pl.pallas_call(
        body, name="ada_grad", out_shape=jax.ShapeDtypeStruct((D_MODEL, n), F32), grid=(n // tn,),
        in_specs=[_full((N_DEV, D_MODEL)), pl.BlockSpec((N_DEV, tn), lambda j: (0, j))],
        out_specs=pl.BlockSpec((D_MODEL, tn), lambda j: (0, j)),
        compiler_params=_cp(("parallel",)))(c_all, dmod_cols)


def _device_step(x, mod, W, tgt, getw, put, early):
    sh1, sc1, g1, sh2, sc2, g2 = [mod[:, i * D_MODEL:(i + 1) * D_MODEL] for i in range(6)]
    e_re, e_im, bb_re, bb_im = _ssm_prep(W["ssm_a_re"], W["ssm_a_im"], W["ssm_b_re"], W["ssm_b_im"], W["ssm_log_dt"])
    bb, cm = _block_diag_mats(bb_re, bb_im, W["ssm_c_re"], W["ssm_c_im"])
    bb16, cm16 = bb.astype(BF16), cm.astype(BF16)
    tab_f = _scan_tables(e_re, e_im, False)
    tab_b = _scan_tables(e_re, e_im, True)

    h1 = _normmod(x, W["norm1_g"], sc1, sh1, "normmod1")
    w_in = getw("w_in", h1)
    z = _matmul(h1, w_in, "nn", 512, 512, 1024, F32, "mm_w_in")
    yc, scv = _conv_fwd(z, W["conv_w"], W["conv_b"], W["conv_ln_g"], W["conv_ln_b"])
    w_cp = getw("conv_proj", scv)
    y_conv = _matmul(scv, w_cp, "nn", 512, 1024, 512, F32, "mm_conv_proj")
    xs, ys, yg = _ssm_fwd(z, bb16, cm16, W["ssm_d"], tab_f)
    w_glu = getw("ssm_glu", yg)
    zz = _matmul(yg, w_glu, "nn", 512, 1024, 512, F32, "mm_ssm_glu")
    merged = _merge_fwd(z, zz, y_conv)
    w_out = getw("w_out", merged)
    o = _matmul(merged, w_out, "nn", 512, 1024, 1024, F32, "mm_w_out")
    x2, h2 = _resid_normmod(x, o, g1, W["norm2_g"], sc2, sh2, "resid_normmod2")
    w_fi = getw("w_ffn_in", h2)
    f = _matmul(h2, w_fi, "nn", 512, 1408, 1024, F32, "mm_ffn_in")
    act = _ffn_act(f)
    w_fo = getw("w_ffn_out", act)
    o2 = _matmul(act, w_fo, "nn", 512, 1024, FH, F32, "mm_ffn_out")
    dx3, do2, loss8, dfg8, dg2_8 = _final(x2, o2, g2, W["final_g"], tgt)

    sm = {}
    tok = put("w_ffn_out", _matmul(act, do2, "tn", 1408, 1024, 1024, BF16, "mm_g_ffn_out"))
    dact = _matmul(do2, w_fo, "nt", 512, 1408, 1024, F32, "mm_d_act", after=tok)
    dfg, dfu = _ffn_bwd(f, dact)
    df = jnp.concatenate([dfg, dfu], axis=1)
    tok = put("w_ffn_in", _matmul(h2, df, "tn", 1024, 1408, 1024, BF16, "mm_g_ffn_in"))
    dh2 = _matmul(df, w_fi, "nt", 512, 1024, 1408, F32, "mm_d_h2", after=tok)
    dx2, do, dsh2, dsc2, dn2, dg1_8 = _normmod_bwd(dh2, x2, dx3, W["norm2_g"], sc2, g1, o, "normmod2_bwd")
    tok = put("w_out", _matmul(merged, do, "tn", 1024, 1024, 1024, BF16, "mm_g_w_out"))
    dmerged = _matmul(do, w_out, "nt", 512, 1024, 1024, F32, "mm_d_merged", after=tok)
    dyconv, dgl, dzz = _merge_bwd(dmerged, z, zz, y_conv)
    tok = put("ssm_glu", _matmul(yg, dzz, "tn", 512, 1024, 1024, BF16, "mm_g_ssm_glu"))
    tok = put("conv_proj", _matmul(scv, dyconv, "tn", 512, 1024, 1024, BF16, "mm_g_conv_proj", after=tok))
    dyg = _matmul(dzz, w_glu, "nt", 512, 512, 1024, F32, "mm_d_yg", after=tok)
    lam, du, dys16, de16, dd8 = _ssm_bwd(dyg, ys, z, xs, cm16.T, bb16.T, W["ssm_d"], tab_b)
    dc_full = _matmul(dys16, xs, "tn", 512, 1024, 1024, F32, "mm_g_ssm_c")
    u = z[:, 2 * CW:3 * CW]
    dbb_full = _matmul(u, lam, "tn", 512, 1024, 1024, F32, "mm_g_ssm_b")
    dsc = _matmul(dyconv, w_cp, "nt", 512, 512, 1024, F32, "mm_d_sc")
    dyc, dlg8, dlb8, dcb8 = _conv_bwd_ln(dsc, yc, W["conv_ln_g"], W["conv_ln_b"])
    dz_conv, dcw = _conv_bwd(dyc, z, W["conv_w"])

    s8 = lambda a: jnp.sum(a, axis=0, keepdims=True)
    de = de16.reshape(2, 8, NST).sum(1)
    de_re, de_im = de[0].reshape(G, P), de[1].reshape(G, P)
    dc_re = _diag_blocks(dc_full[:, :NST])
    dc_im = -_diag_blocks(dc_full[:, NST:])
    dbb_re = jnp.swapaxes(_diag_blocks(dbb_full[:, :NST]), 1, 2)
    dbb_im = jnp.swapaxes(_diag_blocks(dbb_full[:, NST:]), 1, 2)
    _, vjp = jax.vjp(_ssm_prep, W["ssm_a_re"], W["ssm_a_im"], W["ssm_b_re"], W["ssm_b_im"], W["ssm_log_dt"])
    sm["ssm_a_re"], sm["ssm_a_im"], sm["ssm_b_re"], sm["ssm_b_im"], sm["ssm_log_dt"] = vjp((de_re, de_im, dbb_re, dbb_im))
    sm["ssm_c_re"], sm["ssm_c_im"] = dc_re, dc_im
    sm["ssm_d"] = s8(dd8)
    sm["norm2_g"] = s8(dn2)
    sm["conv_b"], sm["conv_ln_g"], sm["conv_ln_b"] = s8(dcb8), s8(dlg8), s8(dlb8)
    sm["conv_w"] = dcw.reshape(KW, 8, CW).sum(1)
    sm["final_g"] = s8(dfg8)
    tok = early(sm)

    dz = jnp.concatenate([dz_conv, du, dgl], axis=1)
    tok = put("w_in", _matmul(h1, dz, "tn", 1024, 512, 1024, BF16, "mm_g_w_in", after=tok))
    dh1 = _matmul(dz, w_in, "nt", 512, 1024, 1792, F32, "mm_d_h1", after=tok)
    dx, _, dsh1, dsc1, dn1, _ = _normmod_bwd(dh1, x, dx2, W["norm1_g"], sc1, g1, o, "normmod1_bwd")
    dmod = jnp.concatenate([s8(dsh1), s8(dsc1), s8(dg1_8), s8(dsh2), s8(dsc2), s8(dg2_8)], axis=1)
    return loss8, dx, s8(dn1), dmod


_BIG = ("w_in", "conv_proj", "ssm_glu", "w_out", "w_ffn_in", "w_ffn_out")
_BIG_AXIS = {"w_in": 1, "conv_proj": 1, "ssm_glu": 1, "w_out": 0, "w_ffn_in": 1, "w_ffn_out": 0}
_EARLY = ("conv_w", "conv_b", "conv_ln_g", "conv_ln_b", "ssm_a_re", "ssm_a_im", "ssm_b_re", "ssm_b_im", "ssm_c_re",
          "ssm_c_im", "ssm_d", "ssm_log_dt", "norm2_g", "final_g")
_LATE = ("norm1_g", "b_ada")
_ORDER = ("w_ada", "b_ada", "norm1_g", "w_in", "conv_w", "conv_b", "conv_ln_g", "conv_ln_b", "conv_proj",
          "ssm_a_re", "ssm_a_im", "ssm_b_re", "ssm_b_im", "ssm_c_re", "ssm_c_im", "ssm_d", "ssm_log_dt", "ssm_glu",
          "w_out", "norm2_g", "w_ffn_in", "w_ffn_out", "final_g")
_PACK_COLS = 1024


def _pack_rows(shape):
    return -(-int(np.prod(shape)) // (8 * _PACK_COLS)) * 8


def _pack(arrs):
    parts = []
    for a in arrs:
        flat = a.reshape(-1)
        n = _pack_rows(a.shape)
        parts.append(jnp.pad(flat, (0, n * _PACK_COLS - flat.shape[0])).reshape(n, _PACK_COLS))
    return jnp.concatenate(parts, 0)


def _unpack(packed, shapes):
    out, r = [], 0
    for shp in shapes:
        size = int(np.prod(shp))
        n = _pack_rows(shp)
        out.append(packed[r:r + n].reshape(-1)[:size].reshape(shp))
        r += n
    return out


def kernel(x, c, w_ada, b_ada, norm1_g, w_in, conv_w, conv_b, conv_ln_g, conv_ln_b, conv_proj, ssm_a_re, ssm_a_im, ssm_b_re, ssm_b_im, ssm_c_re, ssm_c_im, ssm_d, ssm_log_dt, ssm_glu, w_out, norm2_g, w_ffn_in, w_ffn_out, final_g, loss_target, m_w_ada, m_b_ada, m_norm1_g, m_w_in, m_conv_w, m_conv_b, m_conv_ln_g, m_conv_ln_b, m_conv_proj, m_ssm_a_re, m_ssm_a_im, m_ssm_b_re, m_ssm_b_im, m_ssm_c_re, m_ssm_c_im, m_ssm_d, m_ssm_log_dt, m_ssm_glu, m_w_out, m_norm2_g, m_w_ffn_in, m_w_ffn_out, m_final_g, v_w_ada, v_b_ada, v_norm1_g, v_w_in, v_conv_w, v_conv_b, v_conv_ln_g, v_conv_ln_b, v_conv_proj, v_ssm_a_re, v_ssm_a_im, v_ssm_b_re, v_ssm_b_im, v_ssm_c_re, v_ssm_c_im, v_ssm_d, v_ssm_log_dt, v_ssm_glu, v_w_out, v_norm2_g, v_w_ffn_in, v_w_ffn_out, v_final_g):
    given = dict(locals())
    mx, my, mc = _me()
    chip = 2 * mx + my
    dev = 4 * mx + 2 * my + mc
    def canon(a):
        return a.reshape(1, -1) if a.ndim <= 2 else a[0]

    wts = {n: canon(given[n]) for n in _ORDER}
    mom = {n: canon(given["m_" + n]) for n in _ORDER}
    var = {n: canon(given["v_" + n]) for n in _ORDER}

    c_all = _allgather8(jnp.broadcast_to(c, (8, D_MODEL)), "gather_c")[:, 0, :]
    n_ada = wts["w_ada"].shape[1]
    b_cols = lax.dynamic_slice(wts["b_ada"], (0, chip * n_ada), (1, n_ada))
    mod_cols = _mod_shard(c_all, wts["w_ada"], b_cols)
    mods = _allgather8(mod_cols, "gather_mod")
    mod = jnp.concatenate([lax.dynamic_index_in_dim(mods[2 * q], dev, 0, keepdims=True) for q in range(N_CHIP)], axis=1)

    W = {n: wts[n] for n in _ORDER if n not in _BIG}
    conv_w_full = _allgather8(jnp.pad(wts["conv_w"], ((0, 1), (0, 0))), "gather_conv_w")
    W["conv_w"] = jnp.concatenate([conv_w_full[2 * q, :KW] for q in range(N_CHIP)], axis=1)

    axes = [_BIG_AXIS[n] for n in _BIG]
    shards = [wts[n].astype(BF16) for n in _BIG]
    lands = []
    for s, ax in zip(shards, axes):
        shp = list(s.shape)
        shp[ax] *= N_CHIP
        lands.append(lax.empty(tuple(shp), BF16))
    gstate, token = _gather_start(shards, lands, axes, mod + W["conv_w"][0:1, 0:1])
    gstate = dict(zip(_BIG, gstate))
    mod = mod + token[0:1, 0:1]

    def getw(n, after):
        return _gather_wait(gstate[n], _BIG_AXIS[n], after, "gather_wait_" + n)

    sstate, own, estate = {}, {}, []

    def put(n, g):
        ax = _BIG_AXIS[n]
        k = g.shape[ax] // N_CHIP
        own[n] = lax.dynamic_slice_in_dim(g, chip * k, k, axis=ax)
        sstate[n], tok = _scatter_start(g, ax, "scatter_start_" + n)
        return tok

    def early(sm):
        state, tok = _all8_start(_pack([sm[n] for n in _EARLY]), "small_start")
        estate.append(state)
        return tok

    loss8, dx, dn1, dmod = _device_step(x[0], mod, W, loss_target[0], getw, put, early)
    loss = lax.psum(jnp.sum(loss8), ("x", "y", "c"))

    late = _allgather8(_pack([dn1, dmod]), "gather_late")

    recv = [_scatter_wait(sstate[n], _BIG_AXIS[n], late, "scatter_wait_" + n) for n in _BIG]
    mine = [a for n, r in zip(_BIG, recv) for a in (own[n], r)]
    sib = _swap_sibling(mine)
    allp = _all8_wait(estate[0], late, "small_wait")

    outs = {}
    for i, n in enumerate(_BIG):
        outs[n] = _adamw(wts[n], mom[n], var[n], [mine[2 * i:2 * i + 2], sib[2 * i:2 * i + 2]], "adamw_" + n)

    r1 = _pack_rows((D_MODEL,))
    dmod_all = late[:, r1:, :].reshape(N_DEV, -1)[:, :6 * D_MODEL]
    dmod_cols = lax.dynamic_slice(dmod_all, (0, chip * n_ada), (N_DEV, n_ada))
    g_ada = _ada_grad(c_all, dmod_cols)
    outs["w_ada"] = _adamw(wts["w_ada"], mom["w_ada"], var["w_ada"], [[g_ada]], "adamw_w_ada")

    def packed_params(d, names):
        return _pack([jnp.zeros((KW, CW), F32) if n == "conv_w" else d[n] for n in names])

    for names, parts, nm in ((_EARLY, allp, "adamw_small"), (_LATE, late, "adamw_late")):
        res = _adamw(packed_params(wts, names), packed_params(mom, names), packed_params(var, names), [[parts]], nm)
        shapes = [(KW, CW) if n == "conv_w" else wts[n].shape for n in names]
        unpacked = [_unpack(r, shapes) for r in res]
        for idx, n in enumerate(names):
            outs[n] = tuple(unpacked[q][idx] for q in range(4))
    g_cw = lax.dynamic_slice(outs["conv_w"][0], (0, chip * (CW // N_CHIP)), (KW, CW // N_CHIP))
    pad = lambda a: jnp.pad(a, ((0, 1), (0, 0)))
    r_cw = _adamw(pad(wts["conv_w"]), pad(mom["conv_w"]), pad(var["conv_w"]), [[pad(g_cw)]], "adamw_conv_w")
    outs["conv_w"] = tuple(r[:KW] for r in r_cw)

    def shaped(n, a):
        return a.reshape(given[n].shape)

    result = [loss, dx[None]]
    for q in range(4):
        result += [shaped(n, outs[n][q]) for n in _ORDER]
    return tuple(result)
```

```python
import math

import jax
import jax.numpy as jnp
import numpy as np
from jax import lax
from jax.experimental import pallas as pl
from jax.experimental.pallas import tpu as pltpu

F32 = jnp.float32
BF16 = jnp.bfloat16
EPS = 1e-6
D_MODEL = 1024
CW = 512
KW = 31
HALO = 32
G, P, H = 32, 64, 16
NST = G * P
FH = 2816
N_DEV = 8
N_CHIP = 4
VMEM_LIMIT = 56 * 1024 * 1024
LR, B1, B2, AEPS, WD, STEP = 0.001, 0.9, 0.999, 1e-08, 0.01, 10
MESH = pl.DeviceIdType.MESH


def _cp(sem=None):
    return pltpu.CompilerParams(dimension_semantics=sem, vmem_limit_bytes=VMEM_LIMIT)


def _sig(x):
    return jax.nn.sigmoid(x)


def _full(shape):
    return pl.BlockSpec(shape, lambda *_: (0,) * len(shape))


def _colsum8(v):
    t, c = v.shape
    return jnp.sum(v.reshape(t // 8, 8, c), axis=0)


def _matmul(a, b, mode, tm, tn, tk, out_dtype, name, after=None, n_outer=False, m_cols=None):
    m0 = 0
    if mode == "nn":
        (M, K), N = a.shape, b.shape[1]
    elif mode == "nt":
        (M, K), N = a.shape, b.shape[0]
    else:
        (K, M), N = a.shape, b.shape[1]
        if m_cols is not None:
            m0, M = m_cols
    tm, tn, tk = min(tm, M), min(tn, N), min(tk, K)
    assert M % tm == 0 and N % tn == 0 and K % tk == 0 and m0 % tm == 0, (name, M, N, K, tm, tn, tk)
    nk = K // tk
    mb = m0 // tm

    def ij(fn):
        return (lambda j, i, k: fn(i, j, k)) if n_outer else fn

    if mode == "nn":
        a_spec = pl.BlockSpec((tm, tk), ij(lambda i, j, k: (i, k)))
        b_spec = pl.BlockSpec((tk, tn), ij(lambda i, j, k: (k, j)))
        dims = (((1,), (0,)), ((), ()))
    elif mode == "nt":
        a_spec = pl.BlockSpec((tm, tk), ij(lambda i, j, k: (i, k)))
        b_spec = pl.BlockSpec((tn, tk), ij(lambda i, j, k: (j, k)))
        dims = (((1,), (1,)), ((), ()))
    else:
        a_spec = pl.BlockSpec((tk, tm), ij(lambda i, j, k: (k, i + mb)))
        b_spec = pl.BlockSpec((tk, tn), ij(lambda i, j, k: (k, j)))
        dims = (((0,), (0,)), ((), ()))

    def body(a_ref, b_ref, *rest):
        o_ref, acc_ref = rest[-2:]
        k = pl.program_id(2)
        part = lax.dot_general(a_ref[...].astype(BF16), b_ref[...].astype(BF16), dims,
                               preferred_element_type=F32)
        if nk == 1:
            o_ref[...] = part.astype(out_dtype)
        else:
            @pl.when(k == 0)
            def _():
                acc_ref[...] = part

            @pl.when(k > 0)
            def _():
                acc_ref[...] += part

            @pl.when(k == nk - 1)
            def _():
                o_ref[...] = acc_ref[...].astype(out_dtype)

    return pl.pallas_call(
        body, name=name,
        out_shape=jax.ShapeDtypeStruct((M, N), out_dtype),
        grid=(N // tn, M // tm, nk) if n_outer else (M // tm, N // tn, nk),
        in_specs=[a_spec, b_spec] + ([] if after is None else [pl.BlockSpec(memory_space=pl.ANY)]),
        out_specs=pl.BlockSpec((tm, tn), ij(lambda i, j, k: (i, j))),
        scratch_shapes=[pltpu.VMEM((tm, tn) if nk > 1 else (8, 128), F32)],
        compiler_params=_cp(("parallel", "parallel", "arbitrary")),
    )(*((a, b) if after is None else (a, b, after)))


def _row_tile(S):
    return min(512, S)


def _normmod(x, g, sc, sh, name):
    S, D = x.shape
    tm = _row_tile(S)

    def body(x_ref, g_ref, sc_ref, sh_ref, h_ref):
        xv = x_ref[...]
        r = lax.rsqrt(jnp.mean(xv * xv, axis=-1, keepdims=True) + EPS)
        h_ref[...] = (xv * r * (g_ref[...] * (1.0 + sc_ref[...])) + sh_ref[...]).astype(BF16)

    row = pl.BlockSpec((tm, D), lambda i: (i, 0))
    return pl.pallas_call(
        body, name=name, out_shape=jax.ShapeDtypeStruct((S, D), BF16), grid=(S // tm,),
        in_specs=[row, _full((1, D)), _full((1, D)), _full((1, D))], out_specs=row,
        compiler_params=_cp(("parallel",)))(x, g, sc, sh)


def _resid_normmod(x, o, g1, g, sc, sh, name):
    S, D = x.shape
    tm = _row_tile(S)

    def body(x_ref, o_ref, g1_ref, g_ref, sc_ref, sh_ref, x2_ref, h_ref):
        xv = x_ref[...] + g1_ref[...] * o_ref[...]
        x2_ref[...] = xv
        r = lax.rsqrt(jnp.mean(xv * xv, axis=-1, keepdims=True) + EPS)
        h_ref[...] = (xv * r * (g_ref[...] * (1.0 + sc_ref[...])) + sh_ref[...]).astype(BF16)

    row = pl.BlockSpec((tm, D), lambda i: (i, 0))
    par = _full((1, D))
    return pl.pallas_call(
        body, name=name,
        out_shape=(jax.ShapeDtypeStruct((S, D), F32), jax.ShapeDtypeStruct((S, D), BF16)),
        grid=(S // tm,), in_specs=[row, row, par, par, par, par], out_specs=(row, row),
        compiler_params=_cp(("parallel",)))(x, o, g1, g, sc, sh)


def _conv_fwd(z, conv_w, conv_b, ln_g, ln_b):
    S = z.shape[0]
    tm = min(128, S)
    sub = 32
    hb = tm // HALO

    def body(a_ref, g_ref, ha_ref, hg_ref, w_ref, b_ref, lg_ref, lb_ref, yc_ref, s_ref, ug_ref):
        i = pl.program_id(0)
        halo = ha_ref[...] * _sig(hg_ref[...])
        ug_ref[0:HALO, :] = jnp.where(i == 0, 0.0, halo)
        ug_ref[HALO:, :] = a_ref[...] * _sig(g_ref[...])
        for rb in range(tm // sub):
            acc = jnp.zeros((sub, CW), F32) + b_ref[...]
            for k in range(KW):
                off = rb * sub + HALO - (KW - 1) + k
                acc = acc + w_ref[k:k + 1, :] * ug_ref[off:off + sub, :]
            yc_ref[rb * sub:(rb + 1) * sub, :] = acc
            mu = jnp.mean(acc, axis=-1, keepdims=True)
            cen = acc - mu
            rstd = lax.rsqrt(jnp.mean(cen * cen, axis=-1, keepdims=True) + EPS)
            ln = cen * rstd * lg_ref[...] + lb_ref[...]
            s_ref[rb * sub:(rb + 1) * sub, :] = (ln * _sig(ln)).astype(BF16)

    prev = lambda i: (jnp.maximum(i * hb - 1, 0), 0)
    return pl.pallas_call(
        body, name="conv_fwd",
        out_shape=(jax.ShapeDtypeStruct((S, CW), F32), jax.ShapeDtypeStruct((S, CW), BF16)),
        grid=(S // tm,),
        in_specs=[pl.BlockSpec((tm, CW), lambda i: (i, 0)), pl.BlockSpec((tm, CW), lambda i: (i, 1)),
                  pl.BlockSpec((HALO, CW), prev), pl.BlockSpec((HALO, CW), lambda i: (jnp.maximum(i * hb - 1, 0), 1)),
                  _full((KW, CW)), _full((1, CW)), _full((1, CW)), _full((1, CW))],
        out_specs=(pl.BlockSpec((tm, CW), lambda i: (i, 0)), pl.BlockSpec((tm, CW), lambda i: (i, 0))),
        scratch_shapes=[pltpu.VMEM((tm + HALO, CW), F32)],
        compiler_params=_cp(("parallel",)))(z, z, z, z, conv_w, conv_b, ln_g, ln_b)


def _conv_bwd_ln(dsc, yc, ln_g, ln_b):
    S = yc.shape[0]
    tm = _row_tile(S)

    def body(d_ref, yc_ref, lg_ref, lb_ref, dyc_ref, dlg_ref, dlb_ref, dcb_ref):
        i = pl.program_id(0)
        yc_v = yc_ref[...]
        mu = jnp.mean(yc_v, axis=-1, keepdims=True)
        cen = yc_v - mu
        rstd = lax.rsqrt(jnp.mean(cen * cen, axis=-1, keepdims=True) + EPS)
        yn = cen * rstd
        ln = yn * lg_ref[...] + lb_ref[...]
        sl = _sig(ln)
        dln = d_ref[...] * (sl * (1.0 + ln * (1.0 - sl)))
        dyn = dln * lg_ref[...]
        dyc = rstd * (dyn - jnp.mean(dyn, axis=-1, keepdims=True)
                      - yn * jnp.mean(dyn * yn, axis=-1, keepdims=True))
        dyc_ref[...] = dyc

        @pl.when(i == 0)
        def _():
            dlg_ref[...] = jnp.zeros_like(dlg_ref)
            dlb_ref[...] = jnp.zeros_like(dlb_ref)
            dcb_ref[...] = jnp.zeros_like(dcb_ref)

        dlg_ref[...] += _colsum8(dln * yn)
        dlb_ref[...] += _colsum8(dln)
        dcb_ref[...] += _colsum8(dyc)

    row = pl.BlockSpec((tm, CW), lambda i: (i, 0))
    acc = jax.ShapeDtypeStruct((8, CW), F32)
    return pl.pallas_call(
        body, name="conv_bwd_ln",
        out_shape=(jax.ShapeDtypeStruct((S, CW), F32), acc, acc, acc), grid=(S // tm,),
        in_specs=[row, row, _full((1, CW)), _full((1, CW))],
        out_specs=(row, _full((8, CW)), _full((8, CW)), _full((8, CW))),
        compiler_params=_cp(("arbitrary",)))(dsc, yc, ln_g, ln_b)


def _conv_bwd(dyc, z, conv_w):
    S = z.shape[0]
    tm = min(128, S)
    sub = 32
    hb = tm // HALO
    nt = S // tm

    def body(d_ref, dn_ref, a_ref, g_ref, ha_ref, hg_ref, w_ref, dz_ref, dw_ref, ug_ref, dy_ref):
        i = pl.program_id(0)
        halo = ha_ref[...] * _sig(hg_ref[...])
        ug_ref[0:HALO, :] = jnp.where(i == 0, 0.0, halo)
        a = a_ref[...]
        sg = _sig(g_ref[...])
        ug_ref[HALO:, :] = a * sg
        dy_ref[0:tm, :] = d_ref[...]
        dy_ref[tm:, :] = jnp.where(i == nt - 1, 0.0, dn_ref[...])

        @pl.when(i == 0)
        def _():
            dw_ref[...] = jnp.zeros_like(dw_ref)

        for rb in range(tm // sub):
            r0 = rb * sub
            acc = jnp.zeros((sub, CW), F32)
            dyc_b = dy_ref[r0:r0 + sub, :]
            for k in range(KW):
                up = r0 + (KW - 1) - k
                acc = acc + w_ref[k:k + 1, :] * dy_ref[up:up + sub, :]
                off = r0 + HALO - (KW - 1) + k
                dw_ref[k * 8:(k + 1) * 8, :] += _colsum8(dyc_b * ug_ref[off:off + sub, :])
            a_b = a[r0:r0 + sub, :]
            sg_b = sg[r0:r0 + sub, :]
            dz_ref[r0:r0 + sub, 0:CW] = (acc * sg_b).astype(BF16)
            dz_ref[r0:r0 + sub, CW:2 * CW] = (acc * a_b * sg_b * (1.0 - sg_b)).astype(BF16)

    return pl.pallas_call(
        body, name="conv_bwd",
        out_shape=(jax.ShapeDtypeStruct((S, 2 * CW), BF16), jax.ShapeDtypeStruct((KW * 8, CW), F32)),
        grid=(nt,),
        in_specs=[pl.BlockSpec((tm, CW), lambda i: (i, 0)),
                  pl.BlockSpec((HALO, CW), lambda i: (jnp.minimum((i + 1) * hb, nt * hb - 1), 0)),
                  pl.BlockSpec((tm, CW), lambda i: (i, 0)), pl.BlockSpec((tm, CW), lambda i: (i, 1)),
                  pl.BlockSpec((HALO, CW), lambda i: (jnp.maximum(i * hb - 1, 0), 0)),
                  pl.BlockSpec((HALO, CW), lambda i: (jnp.maximum(i * hb - 1, 0), 1)),
                  _full((KW, CW))],
        out_specs=(pl.BlockSpec((tm, 2 * CW), lambda i: (i, 0)), _full((KW * 8, CW))),
        scratch_shapes=[pltpu.VMEM((tm + HALO, CW), F32), pltpu.VMEM((tm + HALO, CW), F32)],
        compiler_params=_cp(("arbitrary",)))(dyc, dyc, z, z, z, z, conv_w)


_GELU_C = math.sqrt(2.0 / math.pi)


def _gelu(x):
    return 0.5 * x * (1.0 + jnp.tanh(_GELU_C * (x + 0.044715 * x * x * x)))


def _gelu_grad(x):
    t = jnp.tanh(_GELU_C * (x + 0.044715 * x * x * x))
    return 0.5 * (1.0 + t) + 0.5 * x * (1.0 - t * t) * (_GELU_C * (1.0 + 3 * 0.044715 * x * x))


_LW = 512


def _ssm_fwd(z, bb, cm, d, tab):
    S = z.shape[0]
    tm = min(256, S)

    def body(u_ref, bb_ref, cm_ref, d_ref, t_ref, x_ref, ys_ref, yg_ref, car_ref):
        i = pl.program_id(0)

        @pl.when(i == 0)
        def _():
            car_ref[...] = jnp.zeros_like(car_ref)

        u = u_ref[...]
        x_ref[...] = jnp.dot(u.astype(BF16), bb_ref[...], preferred_element_type=F32)
        for c in range(NST // _LW):
            lre = pl.ds(c * _LW, _LW)
            lim = pl.ds(NST + c * _LW, _LW)

            def blk(j, car):
                cr, ci = car
                rows = pl.ds(pl.multiple_of(j * 8, 8), 8)
                r = x_ref[rows, lre]
                im = x_ref[rows, lim]
                for lvl, s in enumerate((1, 2, 4)):
                    mr = t_ref[16 * lvl:16 * lvl + 8, lre]
                    mi = t_ref[16 * lvl + 8:16 * lvl + 16, lre]
                    sr = pltpu.roll(r, s, 0)
                    si = pltpu.roll(im, s, 0)
                    r, im = r + (mr * sr - mi * si), im + (mr * si + mi * sr)
                pr = t_ref[48:56, lre]
                pi_ = t_ref[56:64, lre]
                r, im = r + (pr * cr - pi_ * ci), im + (pr * ci + pi_ * cr)
                x_ref[rows, lre] = r
                x_ref[rows, lim] = im
                return (jnp.broadcast_to(r[7:8, :], (8, _LW)), jnp.broadcast_to(im[7:8, :], (8, _LW)))

            cr, ci = lax.fori_loop(0, tm // 8, blk, (car_ref[:, lre], car_ref[:, lim]))
            car_ref[:, lre] = cr
            car_ref[:, lim] = ci
        ys = jnp.dot(x_ref[...].astype(BF16), cm_ref[...], preferred_element_type=F32) + d_ref[...] * u
        ys_ref[...] = ys
        yg_ref[...] = _gelu(ys).astype(BF16)

    return pl.pallas_call(
        body, name="ssm_fwd",
        out_shape=(jax.ShapeDtypeStruct((S, 2 * NST), F32), jax.ShapeDtypeStruct((S, CW), F32),
                   jax.ShapeDtypeStruct((S, CW), BF16)),
        grid=(S // tm,),
        in_specs=[pl.BlockSpec((tm, CW), lambda i: (i, 2)), _full((CW, 2 * NST)), _full((2 * NST, CW)),
                  _full((1, CW)), _full((64, NST))],
        out_specs=(pl.BlockSpec((tm, 2 * NST), lambda i: (i, 0)), pl.BlockSpec((tm, CW), lambda i: (i, 0)),
                   pl.BlockSpec((tm, CW), lambda i: (i, 0))),
        scratch_shapes=[pltpu.VMEM((8, 2 * NST), F32)],
        compiler_params=_cp(("arbitrary",)))(z, bb, cm, d, tab)


def _ssm_bwd(dyg, ys, z, xs, cmt, bbt, d, tab):
    S = z.shape[0]
    tm = min(256, S)
    nt = S // tm

    def body(dyg_ref, ys_ref, u_ref, x_ref, cmt_ref, bbt_ref, d_ref, t_ref,
             lam_ref, du_ref, dys_ref, de_ref, dd_ref, car_ref):
        i = pl.program_id(0)

        @pl.when(i == 0)
        def _():
            car_ref[...] = jnp.zeros_like(car_ref)
            de_ref[...] = jnp.zeros_like(de_ref)
            dd_ref[...] = jnp.zeros_like(dd_ref)

        u = u_ref[...]
        dys = dyg_ref[...] * _gelu_grad(ys_ref[...])
        dys_ref[...] = dys.astype(BF16)
        dd_ref[...] += _colsum8(dys * u)
        lam_ref[...] = jnp.dot(dys.astype(BF16), cmt_ref[...], preferred_element_type=F32)
        row = lax.broadcasted_iota(jnp.int32, (8, _LW), 0)
        for c in range(NST // _LW):
            lre = pl.ds(c * _LW, _LW)
            lim = pl.ds(NST + c * _LW, _LW)

            def blk(jj, car):
                cr, ci, ar, ai = car
                j = tm // 8 - 1 - jj
                rows = pl.ds(pl.multiple_of(j * 8, 8), 8)
                r = lam_ref[rows, lre]
                im = lam_ref[rows, lim]
                for lvl, s in enumerate((1, 2, 4)):
                    mr = t_ref[16 * lvl:16 * lvl + 8, lre]
                    mi = t_ref[16 * lvl + 8:16 * lvl + 16, lre]
                    sr = pltpu.roll(r, 8 - s, 0)
                    si = pltpu.roll(im, 8 - s, 0)
                    r, im = r + (mr * sr - mi * si), im + (mr * si + mi * sr)
                pr = t_ref[48:56, lre]
                pi_ = t_ref[56:64, lre]
                r, im = r + (pr * cr - pi_ * ci), im + (pr * ci + pi_ * cr)
                lam_ref[rows, lre] = r
                lam_ref[rows, lim] = im
                nr = jnp.where(row == 7, cr, pltpu.roll(r, 7, 0))
                ni = jnp.where(row == 7, ci, pltpu.roll(im, 7, 0))
                xr = x_ref[rows, lre]
                xi = x_ref[rows, lim]
                ar = ar + (nr * xr + ni * xi)
                ai = ai + (ni * xr - nr * xi)
                return (jnp.broadcast_to(r[0:1, :], (8, _LW)), jnp.broadcast_to(im[0:1, :], (8, _LW)), ar, ai)

            zero = jnp.zeros((8, _LW), F32)
            cr, ci, ar, ai = lax.fori_loop(0, tm // 8, blk, (car_ref[:, lre], car_ref[:, lim], zero, zero))
            car_ref[:, lre] = cr
            car_ref[:, lim] = ci
            de_ref[0:8, lre] += ar
            de_ref[8:16, lre] += ai
        du = jnp.dot(lam_ref[...].astype(BF16), bbt_ref[...], preferred_element_type=F32) + dys * d_ref[...]
        du_ref[...] = du.astype(BF16)

    rev = lambda i: (nt - 1 - i, 0)
    return pl.pallas_call(
        body, name="ssm_bwd",
        out_shape=(jax.ShapeDtypeStruct((S, 2 * NST), F32), jax.ShapeDtypeStruct((S, CW), BF16),
                   jax.ShapeDtypeStruct((S, CW), BF16), jax.ShapeDtypeStruct((16, NST), F32),
                   jax.ShapeDtypeStruct((8, CW), F32)),
        grid=(nt,),
        in_specs=[pl.BlockSpec((tm, CW), rev), pl.BlockSpec((tm, CW), rev),
                  pl.BlockSpec((tm, CW), lambda i: (nt - 1 - i, 2)), pl.BlockSpec((tm, 2 * NST), rev),
                  _full((CW, 2 * NST)), _full((2 * NST, CW)), _full((1, CW)), _full((64, NST))],
        out_specs=(pl.BlockSpec((tm, 2 * NST), rev), pl.BlockSpec((tm, CW), rev), pl.BlockSpec((tm, CW), rev),
                   _full((16, NST)), _full((8, CW))),
        scratch_shapes=[pltpu.VMEM((8, 2 * NST), F32)],
        compiler_params=_cp(("arbitrary",)))(dyg, ys, z, xs, cmt, bbt, d, tab)


def _ssm_prep(a_re, a_im, b_re, b_im, log_dt):
    dt = jnp.exp(log_dt.reshape(G))[:, None]
    mag = jnp.exp(dt * a_re)
    e_re, e_im = mag * jnp.cos(dt * a_im), mag * jnp.sin(dt * a_im)
    n_re, n_im = e_re - 1.0, e_im
    den = a_re * a_re + a_im * a_im
    q_re = (n_re * a_re + n_im * a_im) / den
    q_im = (n_im * a_re - n_re * a_im) / den
    bb_re = q_re[..., None] * b_re - q_im[..., None] * b_im
    bb_im = q_re[..., None] * b_im + q_im[..., None] * b_re
    return e_re, e_im, bb_re, bb_im


def _scan_tables(e_re, e_im, reverse):
    er = e_re.reshape(1, NST)
    ei = e_im.reshape(1, NST)
    if reverse:
        ei = -ei
    pows = [(er, ei)]
    for _ in range(7):
        pr, pi_ = pows[-1]
        pows.append((pr * er - pi_ * ei, pr * ei + pi_ * er))
    row = jnp.arange(8)[:, None]
    out = []
    for s in (1, 2, 4):
        pr, pi_ = pows[s - 1]
        keep = (row + s <= 7) if reverse else (row >= s)
        out += [jnp.where(keep, pr, 0.0), jnp.where(keep, pi_, 0.0)]
    allr = jnp.concatenate([p[0] for p in pows], 0)
    alli = jnp.concatenate([p[1] for p in pows], 0)
    if reverse:
        allr, alli = allr[::-1], alli[::-1]
    out += [allr, alli]
    return jnp.concatenate(out, 0).astype(F32)


def _block_diag_mats(bb_re, bb_im, c_re, c_im):
    eye = jnp.eye(G, dtype=F32)
    bre = jnp.einsum("gph,gk->ghkp", bb_re, eye).reshape(CW, NST)
    bim = jnp.einsum("gph,gk->ghkp", bb_im, eye).reshape(CW, NST)
    bb = jnp.concatenate([bre, bim], 1)
    cre = jnp.einsum("ghp,gk->gpkh", c_re, eye).reshape(NST, CW)
    cim = jnp.einsum("ghp,gk->gpkh", c_im, eye).reshape(NST, CW)
    cm = jnp.concatenate([cre, -cim], 0)
    return bb, cm


def _diag_blocks(full):
    return jnp.einsum("ghkp,gk->ghp", full.reshape(G, H, G, P), jnp.eye(G, dtype=F32))


def _merge_fwd(z, zz, y_conv):
    S = z.shape[0]
    tm = _row_tile(S)
    D = D_MODEL

    def body(glc_ref, gls_ref, za_ref, zb_ref, yc_ref, m_ref):
        y_ssm = za_ref[...] * _sig(zb_ref[...])
        m_ref[...] = (_sig(glc_ref[...]) * yc_ref[...] + _sig(gls_ref[...]) * y_ssm).astype(BF16)

    return pl.pallas_call(
        body, name="merge_fwd", out_shape=jax.ShapeDtypeStruct((S, D), BF16), grid=(S // tm, 2),
        in_specs=[pl.BlockSpec((tm, CW), lambda i, j: (i, 3 + j)), pl.BlockSpec((tm, CW), lambda i, j: (i, 5 + j)),
                  pl.BlockSpec((tm, CW), lambda i, j: (i, j)), pl.BlockSpec((tm, CW), lambda i, j: (i, 2 + j)),
                  pl.BlockSpec((tm, CW), lambda i, j: (i, j))],
        out_specs=pl.BlockSpec((tm, CW), lambda i, j: (i, j)),
        compiler_params=_cp(("parallel", "parallel")))(z, z, zz, zz, y_conv)


def _merge_bwd(dm, z, zz, y_conv):
    S = z.shape[0]
    tm = min(256, S)
    D = D_MODEL

    def body(dm_ref, glc0_ref, glc1_ref, gls0_ref, gls1_ref, za_ref, zb_ref, yc_ref, dyc_ref, dgl_ref, dzz_ref):
        for half, (glc_ref, gls_ref) in enumerate(((glc0_ref, gls0_ref), (glc1_ref, gls1_ref))):
            lo, hi = half * CW, (half + 1) * CW
            dm_v = dm_ref[:, lo:hi]
            sgc = _sig(glc_ref[...])
            sgs = _sig(gls_ref[...])
            szb = _sig(zb_ref[:, lo:hi])
            za = za_ref[:, lo:hi]
            dyc_ref[:, lo:hi] = (dm_v * sgc).astype(BF16)
            dgl_ref[:, lo:hi] = (dm_v * yc_ref[:, lo:hi] * sgc * (1.0 - sgc)).astype(BF16)
            dys = dm_v * sgs
            dgl_ref[:, D + lo:D + hi] = (dys * (za * szb) * (1.0 - sgs)).astype(BF16)
            dzz_ref[:, lo:hi] = (dys * szb).astype(BF16)
            dzz_ref[:, D + lo:D + hi] = (dys * za * szb * (1.0 - szb)).astype(BF16)

    zb_ = lambda j: pl.BlockSpec((tm, CW), lambda i: (i, j))
    wide = lambda j: pl.BlockSpec((tm, D), lambda i: (i, j))
    return pl.pallas_call(
        body, name="merge_bwd",
        out_shape=(jax.ShapeDtypeStruct((S, D), BF16), jax.ShapeDtypeStruct((S, 2 * D), BF16),
                   jax.ShapeDtypeStruct((S, 2 * D), BF16)),
        grid=(S // tm,),
        in_specs=[wide(0), zb_(3), zb_(4), zb_(5), zb_(6), wide(0), wide(1), wide(0)],
        out_specs=(wide(0), pl.BlockSpec((tm, 2 * D), lambda i: (i, 0)), pl.BlockSpec((tm, 2 * D), lambda i: (i, 0))),
        compiler_params=_cp(("parallel",)))(dm, z, z, z, z, zz, zz, y_conv)


def _ffn_act(f):
    S = f.shape[0]
    tm = _row_tile(S)
    tn = 1408

    def body(g_ref, u_ref, a_ref):
        gv = g_ref[...]
        a_ref[...] = (gv * _sig(gv) * u_ref[...]).astype(BF16)

    return pl.pallas_call(
        body, name="ffn_act", out_shape=jax.ShapeDtypeStruct((S, FH), BF16), grid=(S // tm, FH // tn),
        in_specs=[pl.BlockSpec((tm, tn), lambda i, j: (i, j)), pl.BlockSpec((tm, tn), lambda i, j: (i, j + FH // tn))],
        out_specs=pl.BlockSpec((tm, tn), lambda i, j: (i, j)),
        compiler_params=_cp(("parallel", "parallel")))(f, f)


def _ffn_bwd(f, dact):
    S = f.shape[0]
    tm = _row_tile(S)
    tn = 1408
    nb = FH // tn

    def body(g_ref, u_ref, d_ref, dg_ref, du_ref):
        gv = g_ref[...]
        sg = _sig(gv)
        dv = d_ref[...]
        dg_ref[...] = (dv * u_ref[...] * (sg * (1.0 + gv * (1.0 - sg)))).astype(BF16)
        du_ref[...] = (dv * gv * sg).astype(BF16)

    lo = pl.BlockSpec((tm, tn), lambda i, j: (i, j))
    hi = pl.BlockSpec((tm, tn), lambda i, j: (i, j + nb))
    return pl.pallas_call(
        body, name="ffn_bwd",
        out_shape=(jax.ShapeDtypeStruct((S, FH), BF16), jax.ShapeDtypeStruct((S, FH), BF16)),
        grid=(S // tm, nb), in_specs=[lo, hi, lo], out_specs=(lo, lo),
        compiler_params=_cp(("parallel", "parallel")))(f, f, dact)


def _final(x2, o2, g2, fg, tgt):
    S, D = x2.shape
    tm = _row_tile(S)

    def body(x2_ref, o2_ref, g2_ref, fg_ref, t_ref, dx3_ref, do2_ref, ls_ref, dfg_ref, dg2_ref):
        i = pl.program_id(0)
        o2 = o2_ref[...]
        x3 = x2_ref[...] + g2_ref[...] * o2
        r = lax.rsqrt(jnp.mean(x3 * x3, axis=-1, keepdims=True) + EPS)
        xn = x3 * r
        err = xn * fg_ref[...] - t_ref[...]
        dy = err * (1.0 / D)
        dxn = dy * fg_ref[...]
        dx3 = r * (dxn - xn * jnp.mean(dxn * xn, axis=-1, keepdims=True))
        dx3_ref[...] = dx3
        do2_ref[...] = (dx3 * g2_ref[...]).astype(BF16)

        @pl.when(i == 0)
        def _():
            ls_ref[...] = jnp.zeros_like(ls_ref)
            dfg_ref[...] = jnp.zeros_like(dfg_ref)
            dg2_ref[...] = jnp.zeros_like(dg2_ref)

        e2 = _colsum8(err * err)
        lanes = e2[:, 0:128]
        for q in range(1, D // 128):
            lanes = lanes + e2[:, q * 128:(q + 1) * 128]
        ls_ref[...] += lanes * (0.5 / D)
        dfg_ref[...] += _colsum8(dy * xn)
        dg2_ref[...] += _colsum8(dx3 * o2)

    row = pl.BlockSpec((tm, D), lambda i: (i, 0))
    par = _full((1, D))
    return pl.pallas_call(
        body, name="final_loss",
        out_shape=(jax.ShapeDtypeStruct((S, D), F32), jax.ShapeDtypeStruct((S, D), BF16),
                   jax.ShapeDtypeStruct((8, 128), F32), jax.ShapeDtypeStruct((8, D), F32),
                   jax.ShapeDtypeStruct((8, D), F32)),
        grid=(S // tm,), in_specs=[row, row, par, par, row],
        out_specs=(row, row, _full((8, 128)), _full((8, D)), _full((8, D))),
        compiler_params=_cp(("arbitrary",)))(x2, o2, g2, fg, tgt)


def _normmod_bwd(dh, xin, dres, g, sc, gate, o, name):
    S, D = xin.shape
    tm = _row_tile(S)

    def body(dh_ref, x_ref, dr_ref, g_ref, sc_ref, gate_ref, o_ref, dx_ref, do_ref, dsh_ref, dsc_ref, dg_ref, dgate_ref):
        i = pl.program_id(0)
        xv = x_ref[...]
        r = lax.rsqrt(jnp.mean(xv * xv, axis=-1, keepdims=True) + EPS)
        xn = xv * r
        dh_v = dh_ref[...]
        gv = g_ref[...]
        scale = 1.0 + sc_ref[...]
        dxn = dh_v * (gv * scale)
        dx = dr_ref[...] + r * (dxn - xn * jnp.mean(dxn * xn, axis=-1, keepdims=True))
        dx_ref[...] = dx
        do_ref[...] = (dx * gate_ref[...]).astype(BF16)

        @pl.when(i == 0)
        def _():
            dsh_ref[...] = jnp.zeros_like(dsh_ref)
            dsc_ref[...] = jnp.zeros_like(dsc_ref)
            dg_ref[...] = jnp.zeros_like(dg_ref)
            dgate_ref[...] = jnp.zeros_like(dgate_ref)

        hx = dh_v * xn
        dsh_ref[...] += _colsum8(dh_v)
        dsc_ref[...] += _colsum8(hx) * gv
        dg_ref[...] += _colsum8(hx) * scale
        dgate_ref[...] += _colsum8(dx * o_ref[...])

    row = pl.BlockSpec((tm, D), lambda i: (i, 0))
    par = _full((1, D))
    acc = jax.ShapeDtypeStruct((8, D), F32)
    return pl.pallas_call(
        body, name=name,
        out_shape=(jax.ShapeDtypeStruct((S, D), F32), jax.ShapeDtypeStruct((S, D), BF16), acc, acc, acc, acc),
        grid=(S // tm,), in_specs=[row, row, row, par, par, par, row],
        out_specs=(row, row, _full((8, D)), _full((8, D)), _full((8, D)), _full((8, D))),
        compiler_params=_cp(("arbitrary",)))(dh, xin, dres, g, sc, gate, o)


def _me():
    return lax.axis_index("x"), lax.axis_index("y"), lax.axis_index("c")


def _allgather8(v, name):
    R, C = v.shape

    def body(v_ref, out_ref, send_sems, recv_sems, local_sem):
        x, y, c = _me()
        mine = pltpu.make_async_copy(v_ref, out_ref.at[4 * x + 2 * y + c], local_sem)
        mine.start()
        copies = []
        for k in range(1, N_DEV):
            fx, fy, fc = (k >> 2) & 1, (k >> 1) & 1, k & 1
            peer = (x ^ fx, y ^ fy, c ^ fc)
            copies.append(pltpu.make_async_remote_copy(
                src_ref=v_ref, dst_ref=out_ref.at[4 * x + 2 * y + c],
                send_sem=send_sems.at[k - 1], recv_sem=recv_sems.at[k - 1],
                device_id=peer, device_id_type=MESH))
        for cp in copies:
            cp.start()
        for k in range(1, N_DEV):
            fx, fy, fc = (k >> 2) & 1, (k >> 1) & 1, k & 1
            src_slot = 4 * (x ^ fx) + 2 * (y ^ fy) + (c ^ fc)
            pltpu.make_async_remote_copy(
                src_ref=v_ref, dst_ref=out_ref.at[src_slot],
                send_sem=send_sems.at[k - 1], recv_sem=recv_sems.at[k - 1],
                device_id=(x ^ fx, y ^ fy, c ^ fc), device_id_type=MESH).wait_recv()
        for cp in copies:
            cp.wait_send()
        mine.wait()

    return pl.pallas_call(
        body, name=name, out_shape=jax.ShapeDtypeStruct((N_DEV, R, C), v.dtype),
        in_specs=[pl.BlockSpec(memory_space=pltpu.VMEM)], out_specs=pl.BlockSpec(memory_space=pltpu.VMEM),
        scratch_shapes=[pltpu.SemaphoreType.DMA((N_DEV - 1,)), pltpu.SemaphoreType.DMA((N_DEV - 1,)),
                        pltpu.SemaphoreType.DMA],
        compiler_params=pltpu.CompilerParams(vmem_limit_bytes=VMEM_LIMIT))(v)


def _swap_sibling(arrs):
    nw = len(arrs)

    def body(*refs):
        ins, outs = refs[:nw], refs[nw:2 * nw]
        send_sems, recv_sems = refs[2 * nw:]
        x, y, c = _me()
        copies = [pltpu.make_async_remote_copy(
            src_ref=ins[w], dst_ref=outs[w], send_sem=send_sems.at[w], recv_sem=recv_sems.at[w],
            device_id=(x, y, 1 - c), device_id_type=MESH) for w in range(nw)]
        for cp in copies:
            cp.start()
        for cp in copies:
            cp.wait_recv()
        for cp in copies:
            cp.wait_send()

    hbm = pl.BlockSpec(memory_space=pltpu.HBM)
    return pl.pallas_call(
        body, name="swap_sibling", out_shape=tuple(jax.ShapeDtypeStruct(a.shape, a.dtype) for a in arrs),
        in_specs=[hbm] * nw, out_specs=tuple([hbm] * nw),
        scratch_shapes=[pltpu.SemaphoreType.DMA((nw,)), pltpu.SemaphoreType.DMA((nw,))],
        compiler_params=pltpu.CompilerParams(vmem_limit_bytes=VMEM_LIMIT))(*arrs)


_HBM = pl.BlockSpec(memory_space=pltpu.HBM)
_SEM = pl.BlockSpec(memory_space=pltpu.SEMAPHORE)
_EFFECT = pltpu.SideEffectType.DATAFLOW_SIDE_EFFECTING
_N_PEER = N_CHIP - 1


def _chip_part(ref, axis, n, chip):
    start = pl.multiple_of(chip * n, 8)
    return ref.at[pl.ds(start, n), :] if axis == 0 else ref.at[:, pl.ds(start, n)]


def _gather_copy(k, src_ref, land_ref, send_sems, recv_sems, axis, arriving):
    x, y, c = _me()
    px, py = x ^ ((k >> 1) & 1), y ^ (k & 1)
    chip = 2 * px + py if arriving else 2 * x + y
    return pltpu.make_async_remote_copy(
        src_ref=src_ref, dst_ref=_chip_part(land_ref, axis, src_ref.shape[axis], chip),
        send_sem=send_sems.at[k - 1], recv_sem=recv_sems.at[k - 1], device_id=(px, py, c), device_id_type=MESH)


def _scatter_copy(k, grad_ref, land_ref, send_sems, recv_sems, axis):
    x, y, c = _me()
    px, py = x ^ ((k >> 1) & 1), y ^ (k & 1)
    return pltpu.make_async_remote_copy(
        src_ref=_chip_part(grad_ref, axis, grad_ref.shape[axis] // N_CHIP, 2 * px + py), dst_ref=land_ref.at[k - 1],
        send_sem=send_sems.at[k - 1], recv_sem=recv_sems.at[k - 1], device_id=(px, py, c), device_id_type=MESH)


def _own_copy(src_ref, land_ref, sends, axis):
    x, y, _ = _me()
    return pltpu.make_async_copy(src_ref, _chip_part(land_ref, axis, src_ref.shape[axis], 2 * x + y),
                                 sends.at[_N_PEER])


def _gather_start(shards, lands, axes, after):
    nw = len(shards)

    def body(*refs):
        srcs, zones = refs[:nw], refs[nw:2 * nw]
        sends, recvs = refs[2 * nw + 1:3 * nw + 1], refs[3 * nw + 1:4 * nw + 1]
        token = refs[-1]
        for w in range(nw):
            for k in range(1, N_CHIP):
                _gather_copy(k, srcs[w], zones[w], sends[w], recvs[w], axes[w], False).start()
        for w in range(nw):
            _own_copy(srcs[w], zones[w], sends[w], axes[w]).start()
        token[...] = jnp.zeros_like(token)

    outs = pl.pallas_call(
        body, name="gather_start",
        out_shape=tuple([pltpu.SemaphoreType.DMA((_N_PEER + 1,))] * nw + [pltpu.SemaphoreType.DMA((_N_PEER,))] * nw
                        + [pltpu.HBM(a.shape, a.dtype) for a in list(shards) + list(lands)]
                        + [jax.ShapeDtypeStruct((8, 128), F32)]),
        in_specs=[_HBM] * (2 * nw) + [pl.BlockSpec(memory_space=pl.ANY)],
        out_specs=tuple([_SEM] * (2 * nw) + [_HBM] * (2 * nw) + [pl.BlockSpec(memory_space=pltpu.VMEM)]),
        input_output_aliases={i: 2 * nw + i for i in range(2 * nw)},
        compiler_params=pltpu.CompilerParams(has_side_effects=_EFFECT),
    )(*([pltpu.with_memory_space_constraint(a, pltpu.HBM) for a in list(shards) + list(lands)] + [after]))
    per_weight = [(outs[w], outs[nw + w], outs[2 * nw + w], outs[3 * nw + w]) for w in range(nw)]
    return per_weight, outs[-1]


def _gather_wait(state, axis, after, name):
    send_sems, recv_sems, shard, land = state

    after = list(after) if isinstance(after, (list, tuple)) else [after]

    def body(src_ref, land_ref, sends, recvs, *rest):
        for k in range(1, N_CHIP):
            _gather_copy(k, src_ref, land_ref, sends, recvs, axis, False).wait_send()
            _gather_copy(k, src_ref, land_ref, sends, recvs, axis, True).wait_recv()
        _own_copy(src_ref, land_ref, sends, axis).wait()

    return pl.pallas_call(
        body, name=name, out_shape=(pltpu.HBM(shard.shape, shard.dtype), pltpu.HBM(land.shape, land.dtype)),
        in_specs=[_HBM, _HBM, _SEM, _SEM] + [pl.BlockSpec(memory_space=pl.ANY)] * len(after), out_specs=(_HBM, _HBM),
        input_output_aliases={0: 0, 1: 1},
        compiler_params=pltpu.CompilerParams(has_side_effects=_EFFECT),
    )(shard, land, send_sems, recv_sems, *after)[1]


def _all8_copy(k, v_ref, land_ref, send_sems, recv_sems, arriving):
    x, y, c = _me()
    px, py, pc = x ^ ((k >> 2) & 1), y ^ ((k >> 1) & 1), c ^ (k & 1)
    slot = 4 * px + 2 * py + pc if arriving else 4 * x + 2 * y + c
    return pltpu.make_async_remote_copy(
        src_ref=v_ref, dst_ref=land_ref.at[slot], send_sem=send_sems.at[k - 1], recv_sem=recv_sems.at[k - 1],
        device_id=(px, py, pc), device_id_type=MESH)


def _all8_own(v_ref, land_ref, send_sems):
    x, y, c = _me()
    return pltpu.make_async_copy(v_ref, land_ref.at[4 * x + 2 * y + c], send_sems.at[N_DEV - 1])


def _all8_start(v, name):
    land = lax.empty((N_DEV,) + v.shape, v.dtype)

    def body(v_ref, land_ref, sends, recvs, v_thru, land_thru, token):
        for k in range(1, N_DEV):
            _all8_copy(k, v_ref, land_ref, sends, recvs, False).start()
        _all8_own(v_ref, land_ref, sends).start()
        token[...] = jnp.zeros_like(token)

    outs = pl.pallas_call(
        body, name=name,
        out_shape=(pltpu.SemaphoreType.DMA((N_DEV,)), pltpu.SemaphoreType.DMA((N_DEV - 1,)),
                   pltpu.HBM(v.shape, v.dtype), pltpu.HBM(land.shape, land.dtype),
                   jax.ShapeDtypeStruct((8, 128), F32)),
        in_specs=[_HBM, _HBM], out_specs=(_SEM, _SEM, _HBM, _HBM, pl.BlockSpec(memory_space=pltpu.VMEM)),
        input_output_aliases={0: 2, 1: 3},
        compiler_params=pltpu.CompilerParams(has_side_effects=_EFFECT),
    )(pltpu.with_memory_space_constraint(v, pltpu.HBM), pltpu.with_memory_space_constraint(land, pltpu.HBM))
    return outs[:4], outs[4]


def _all8_wait(state, after, name):
    send_sems, recv_sems, v, land = state

    def body(v_ref, land_ref, sends, recvs, after_ref, v_dead, got_ref):
        for k in range(1, N_DEV):
            _all8_copy(k, v_ref, land_ref, sends, recvs, False).wait_send()
            _all8_copy(k, v_ref, land_ref, sends, recvs, True).wait_recv()
        _all8_own(v_ref, land_ref, sends).wait()

    return pl.pallas_call(
        body, name=name, out_shape=(pltpu.HBM(v.shape, v.dtype), pltpu.HBM(land.shape, land.dtype)),
        in_specs=[_HBM, _HBM, _SEM, _SEM, pl.BlockSpec(memory_space=pl.ANY)], out_specs=(_HBM, _HBM),
        input_output_aliases={0: 0, 1: 1},
        compiler_params=pltpu.CompilerParams(has_side_effects=_EFFECT),
    )(v, land, send_sems, recv_sems, after)[1]


def _swap_copy(w, src_ref, land_ref, send_sems, recv_sems):
    x, y, c = _me()
    return pltpu.make_async_remote_copy(src_ref=src_ref, dst_ref=land_ref, send_sem=send_sems.at[w],
                                        recv_sem=recv_sems.at[w], device_id=(x, y, 1 - c), device_id_type=MESH)


def _swap_start(arrs, after, name):
    nw = len(arrs)
    lands = [lax.empty(a.shape, a.dtype) for a in arrs]

    def body(*refs):
        srcs, zones = refs[:nw], refs[nw:2 * nw]
        sends, recvs = refs[2 * nw + 1], refs[2 * nw + 2]
        for w in range(nw):
            _swap_copy(w, srcs[w], zones[w], sends, recvs).start()
        refs[-1][...] = jnp.zeros_like(refs[-1])

    sem = pltpu.SemaphoreType.DMA((nw,))
    outs = pl.pallas_call(
        body, name=name,
        out_shape=tuple([sem, sem] + [pltpu.HBM(a.shape, a.dtype) for a in list(arrs) + lands]
                        + [jax.ShapeDtypeStruct((8, 128), F32)]),
        in_specs=[_HBM] * (2 * nw) + [pl.BlockSpec(memory_space=pl.ANY)],
        out_specs=tuple([_SEM, _SEM] + [_HBM] * (2 * nw) + [pl.BlockSpec(memory_space=pltpu.VMEM)]),
        input_output_aliases={i: 2 + i for i in range(2 * nw)},
        compiler_params=pltpu.CompilerParams(has_side_effects=_EFFECT),
    )(*([pltpu.with_memory_space_constraint(a, pltpu.HBM) for a in list(arrs) + lands] + [after]))
    return (outs[0], outs[1], outs[2:2 + nw], outs[2 + nw:2 + 2 * nw]), outs[-1]


def _swap_wait(state, after, name):
    send_sems, recv_sems, arrs, lands = state
    nw = len(arrs)

    def body(*refs):
        srcs, zones = refs[:nw], refs[nw:2 * nw]
        sends, recvs = refs[2 * nw], refs[2 * nw + 1]
        for w in range(nw):
            cp = _swap_copy(w, srcs[w], zones[w], sends, recvs)
            cp.wait_send()
            cp.wait_recv()

    outs = pl.pallas_call(
        body, name=name, out_shape=tuple(pltpu.HBM(a.shape, a.dtype) for a in list(arrs) + list(lands)),
        in_specs=[_HBM] * (2 * nw) + [_SEM, _SEM, pl.BlockSpec(memory_space=pl.ANY)],
        out_specs=tuple([_HBM] * (2 * nw)),
        input_output_aliases={i: i for i in range(2 * nw)},
        compiler_params=pltpu.CompilerParams(has_side_effects=_EFFECT),
    )(*arrs, *lands, send_sems, recv_sems, after)
    return list(outs[:nw]), list(outs[nw:])


def _scatter_start(grad, axis, name):
    shp = list(grad.shape)
    shp[axis] //= N_CHIP
    land = lax.empty((_N_PEER,) + tuple(shp), grad.dtype)

    def body(grad_ref, land_ref, sends, recvs, grad_thru, land_thru, token):
        for k in range(1, N_CHIP):
            _scatter_copy(k, grad_ref, land_ref, sends, recvs, axis).start()
        token[...] = jnp.zeros_like(token)

    sem = pltpu.SemaphoreType.DMA((_N_PEER,))
    outs = pl.pallas_call(
        body, name=name,
        out_shape=(sem, sem, pltpu.HBM(grad.shape, grad.dtype), pltpu.HBM(land.shape, land.dtype),
                   jax.ShapeDtypeStruct((8, 128), F32)),
        in_specs=[_HBM, _HBM], out_specs=(_SEM, _SEM, _HBM, _HBM, pl.BlockSpec(memory_space=pltpu.VMEM)),
        input_output_aliases={0: 2, 1: 3},
        compiler_params=pltpu.CompilerParams(has_side_effects=_EFFECT),
    )(pltpu.with_memory_space_constraint(grad, pltpu.HBM), pltpu.with_memory_space_constraint(land, pltpu.HBM))
    return outs[:4], outs[4]


def _scatter_wait(state, axis, after, name):
    send_sems, recv_sems, grad, land = state

    def body(grad_ref, land_ref, sends, recvs, after_ref, grad_dead, got_ref):
        for k in range(1, N_CHIP):
            cp = _scatter_copy(k, grad_ref, land_ref, sends, recvs, axis)
            cp.wait_send()
            cp.wait_recv()

    return pl.pallas_call(
        body, name=name, out_shape=(pltpu.HBM(grad.shape, grad.dtype), pltpu.HBM(land.shape, land.dtype)),
        in_specs=[_HBM, _HBM, _SEM, _SEM, pl.BlockSpec(memory_space=pl.ANY)], out_specs=(_HBM, _HBM),
        input_output_aliases={0: 0, 1: 1},
        compiler_params=pltpu.CompilerParams(has_side_effects=_EFFECT),
    )(grad, land, send_sems, recv_sems, after)[1]


_C1 = 1.0 - B1 ** STEP
_C2 = 1.0 - B2 ** STEP


def _adam_math(w, g, m, v):
    m = B1 * m + (1.0 - B1) * g
    v = B2 * v + (1.0 - B2) * (g * g)
    delta = -LR * ((m / _C1) / (jnp.sqrt(v / _C2) + AEPS) + WD * w)
    return delta, m, v


def _adamw(w, m, v, groups, name):
    R, C = w.shape
    tr = R if R <= 256 else (128 if R % 128 == 0 else 176)
    assert R % tr == 0, (name, R)
    gparts = [p for grp in groups for p in grp]
    sizes = [len(grp) for grp in groups]
    ng = len(gparts)

    def body(*refs):
        w_ref, m_ref, v_ref = refs[:3]
        g_refs = list(refs[3:3 + ng])
        g_out, d_out, m_out, v_out = refs[3 + ng:]
        g = None
        for size in sizes:
            s = None
            for r in [g_refs.pop(0) for _ in range(size)]:
                terms = [r[q] for q in range(r.shape[0])] if len(r.shape) == 3 else [r[...]]
                for t in terms:
                    s = t.astype(F32) if s is None else s + t.astype(F32)
            g = s if g is None else g + s
        delta, mn, vn = _adam_math(w_ref[...], g, m_ref[...], v_ref[...])
        g_out[...] = g
        d_out[...] = delta
        m_out[...] = mn
        v_out[...] = vn

    blk = pl.BlockSpec((tr, C), lambda i: (i, 0))
    g_specs = [blk if p.ndim == 2 else pl.BlockSpec((p.shape[0], tr, C), lambda i: (0, i, 0)) for p in gparts]
    sds = jax.ShapeDtypeStruct((R, C), F32)
    return pl.pallas_call(
        body, name=name, out_shape=(sds, sds, sds, sds), grid=(R // tr,),
        in_specs=[blk, blk, blk] + g_specs, out_specs=(blk, blk, blk, blk),
        compiler_params=_cp(("parallel",)))(w, m, v, *gparts)


def _mod_shard(c_all, w_ada, b_ada_cols):
    n = w_ada.shape[1]
    tn = 512

    def body(c_ref, w_ref, b_ref, o_ref):
        cv = c_ref[...]
        ca = (cv * _sig(cv)).astype(BF16)
        o_ref[...] = jnp.dot(ca, w_ref[...].astype(BF16), preferred_element_type=F32) + b_ref[...]

    return pl.pallas_call(
        body, name="mod_shard", out_shape=jax.ShapeDtypeStruct((N_DEV, n), F32), grid=(n // tn,),
        in_specs=[_full((N_DEV, D_MODEL)), pl.BlockSpec((D_MODEL, tn), lambda j: (0, j)),
                  pl.BlockSpec((1, tn), lambda j: (0, j))],
        out_specs=pl.BlockSpec((N_DEV, tn), lambda j: (0, j)),
        compiler_params=_cp(("parallel",)))(c_all, w_ada, b_ada_cols)


def _ada_grad(c_all, dmod_cols):
    n = dmod_cols.shape[1]
    tn = 512

    def body(c_ref, d_ref, o_ref):
        cv = c_ref[...]
        ca = cv * _sig(cv)
        o_ref[...] = lax.dot_general(ca, d_ref[...], (((0,), (0,)), ((), ())),
                                     preferred_element_type=F32, precision=lax.Precision.HIGHEST)

    return pl.pallas_call(
        body, name="ada_grad", out_shape=jax.ShapeDtypeStruct((D_MODEL, n), F32), grid=(n // tn,),
        in_specs=[_full((N_DEV, D_MODEL)), pl.BlockSpec((N_DEV, tn), lambda j: (0, j))],
        out_specs=pl.BlockSpec((D_MODEL, tn), lambda j: (0, j)),
        compiler_params=_cp(("parallel",)))(c_all, dmod_cols)


def _device_step(x, mod, W, tgt, getw, put, early):
    sh1, sc1, g1, sh2, sc2, g2 = [mod[:, i * D_MODEL:(i + 1) * D_MODEL] for i in range(6)]
    e_re, e_im, bb_re, bb_im = _ssm_prep(W["ssm_a_re"], W["ssm_a_im"], W["ssm_b_re"], W["ssm_b_im"], W["ssm_log_dt"])
    bb, cm = _block_diag_mats(bb_re, bb_im, W["ssm_c_re"], W["ssm_c_im"])
    bb16, cm16 = bb.astype(BF16), cm.astype(BF16)
    bbt16, cmt16 = bb16.T, cm16.T
    tab_f = _scan_tables(e_re, e_im, False)
    tab_b = _scan_tables(e_re, e_im, True)

    h1 = _normmod(x, W["norm1_g"], sc1, sh1, "normmod1")
    w_in = getw("w_in", [h1, bb16, cm16, bbt16, cmt16, tab_f, tab_b])
    z = _matmul(h1, w_in, "nn", 512, 896, 1024, F32, "mm_w_in", n_outer=True)
    yc, scv = _conv_fwd(z, W["conv_w"], W["conv_b"], W["conv_ln_g"], W["conv_ln_b"])
    w_cp = getw("conv_proj", scv)
    y_conv = _matmul(scv, w_cp, "nn", 512, 1024, 512, F32, "mm_conv_proj")
    xs, ys, yg = _ssm_fwd(z, bb16, cm16, W["ssm_d"], tab_f)
    w_glu = getw("ssm_glu", yg)
    zz = _matmul(yg, w_glu, "nn", 512, 1024, 512, F32, "mm_ssm_glu", n_outer=True)
    merged = _merge_fwd(z, zz, y_conv)
    w_out = getw("w_out", merged)
    o = _matmul(merged, w_out, "nn", 1024, 1024, 1024, F32, "mm_w_out")
    x2, h2 = _resid_normmod(x, o, g1, W["norm2_g"], sc2, sh2, "resid_normmod2")
    w_fi = getw("w_ffn_in", h2)
    f = _matmul(h2, w_fi, "nn", 512, 1408, 1024, F32, "mm_ffn_in", n_outer=True)
    act = _ffn_act(f)
    w_fo = getw("w_ffn_out", act)
    o2 = _matmul(act, w_fo, "nn", 512, 1024, FH, F32, "mm_ffn_out")
    dx3, do2, loss8, dfg8, dg2_8 = _final(x2, o2, g2, W["final_g"], tgt)

    sm = {}
    tok = put("w_ffn_out", _matmul(act, do2, "tn", 1408, 1024, 1024, BF16, "mm_g_ffn_out"))
    dact = _matmul(do2, w_fo, "nt", 512, 1408, 1024, F32, "mm_d_act", after=tok, n_outer=True)
    dfg, dfu = _ffn_bwd(f, dact)
    df = jnp.concatenate([dfg, dfu], axis=1)
    tok = put("w_ffn_in", _matmul(h2, df, "tn", 1024, 1408, 1024, BF16, "mm_g_ffn_in"))
    dh2 = _matmul(df, w_fi, "nt", 1024, 1024, 1408, F32, "mm_d_h2", after=tok)
    dx2, do, dsh2, dsc2, dn2, dg1_8 = _normmod_bwd(dh2, x2, dx3, W["norm2_g"], sc2, g1, o, "normmod2_bwd")
    tok = put("w_out", _matmul(merged, do, "tn", 1024, 1024, 1024, BF16, "mm_g_w_out"))
    dmerged = _matmul(do, w_out, "nt", 1024, 1024, 1024, F32, "mm_d_merged", after=tok)
    dyconv, dgl, dzz = _merge_bwd(dmerged, z, zz, y_conv)
    tok = put("ssm_glu", _matmul(yg, dzz, "tn", 512, 1024, 1024, BF16, "mm_g_ssm_glu"))
    tok = put("conv_proj", _matmul(scv, dyconv, "tn", 512, 1024, 1024, BF16, "mm_g_conv_proj", after=tok))
    dyg = _matmul(dzz, w_glu, "nt", 512, 512, 2048, F32, "mm_d_yg", after=tok)
    lam, du, dys16, de16, dd8 = _ssm_bwd(dyg, ys, z, xs, cmt16, bbt16, W["ssm_d"], tab_b)
    dc_full = _matmul(dys16, xs, "tn", 512, 1024, 1024, F32, "mm_g_ssm_c")
    dbb_full = _matmul(z, lam, "tn", 512, 1024, 1024, F32, "mm_g_ssm_b", m_cols=(2 * CW, CW))
    dsc = _matmul(dyconv, w_cp, "nt", 512, 512, 1024, F32, "mm_d_sc")
    dyc, dlg8, dlb8, dcb8 = _conv_bwd_ln(dsc, yc, W["conv_ln_g"], W["conv_ln_b"])
    dz_conv, dcw = _conv_bwd(dyc, z, W["conv_w"])

    s8 = lambda a: jnp.sum(a, axis=0, keepdims=True)
    de = de16.reshape(2, 8, NST).sum(1)
    de_re, de_im = de[0].reshape(G, P), de[1].reshape(G, P)
    dc_re = _diag_blocks(dc_full[:, :NST])
    dc_im = -_diag_blocks(dc_full[:, NST:])
    dbb_re = jnp.swapaxes(_diag_blocks(dbb_full[:, :NST]), 1, 2)
    dbb_im = jnp.swapaxes(_diag_blocks(dbb_full[:, NST:]), 1, 2)
    _, vjp = jax.vjp(_ssm_prep, W["ssm_a_re"], W["ssm_a_im"], W["ssm_b_re"], W["ssm_b_im"], W["ssm_log_dt"])
    sm["ssm_a_re"], sm["ssm_a_im"], sm["ssm_b_re"], sm["ssm_b_im"], sm["ssm_log_dt"] = vjp((de_re, de_im, dbb_re, dbb_im))
    sm["ssm_c_re"], sm["ssm_c_im"] = dc_re, dc_im
    sm["ssm_d"] = s8(dd8)
    sm["norm2_g"] = s8(dn2)
    sm["conv_b"], sm["conv_ln_g"], sm["conv_ln_b"] = s8(dcb8), s8(dlg8), s8(dlb8)
    sm["conv_w"] = dcw.reshape(KW, 8, CW).sum(1)
    sm["final_g"] = s8(dfg8)
    tok = early(sm)

    dz = jnp.concatenate([dz_conv, du, dgl], axis=1)
    tok = put("w_in", _matmul(h1, dz, "tn", 1024, 896, 1024, BF16, "mm_g_w_in", after=tok))
    dh1 = _matmul(dz, w_in, "nt", 1024, 1024, 1792, F32, "mm_d_h1", after=tok)
    dx, _, dsh1, dsc1, dn1, _ = _normmod_bwd(dh1, x, dx2, W["norm1_g"], sc1, g1, o, "normmod1_bwd")
    dmod = jnp.concatenate([s8(dsh1), s8(dsc1), s8(dg1_8), s8(dsh2), s8(dsc2), s8(dg2_8)], axis=1)
    return loss8, dx, s8(dn1), dmod


_BIG = ("w_in", "conv_proj", "ssm_glu", "w_out", "w_ffn_in", "w_ffn_out")
_BIG_AXIS = {"w_in": 1, "conv_proj": 1, "ssm_glu": 1, "w_out": 0, "w_ffn_in": 1, "w_ffn_out": 0}
_EARLY = ("conv_w", "conv_b", "conv_ln_g", "conv_ln_b", "ssm_a_re", "ssm_a_im", "ssm_b_re", "ssm_b_im", "ssm_c_re",
          "ssm_c_im", "ssm_d", "ssm_log_dt", "norm2_g", "final_g")
_LATE = ("norm1_g", "b_ada")
_ORDER = ("w_ada", "b_ada", "norm1_g", "w_in", "conv_w", "conv_b", "conv_ln_g", "conv_ln_b", "conv_proj",
          "ssm_a_re", "ssm_a_im", "ssm_b_re", "ssm_b_im", "ssm_c_re", "ssm_c_im", "ssm_d", "ssm_log_dt", "ssm_glu",
          "w_out", "norm2_g", "w_ffn_in", "w_ffn_out", "final_g")
_PACK_COLS = 1024


def _pack_rows(shape):
    return -(-int(np.prod(shape)) // (8 * _PACK_COLS)) * 8


def _pack(arrs):
    parts = []
    for a in arrs:
        flat = a.reshape(-1)
        n = _pack_rows(a.shape)
        parts.append(jnp.pad(flat, (0, n * _PACK_COLS - flat.shape[0])).reshape(n, _PACK_COLS))
    return jnp.concatenate(parts, 0)


def _unpack(packed, shapes):
    out, r = [], 0
    for shp in shapes:
        size = int(np.prod(shp))
        n = _pack_rows(shp)
        out.append(packed[r:r + n].reshape(-1)[:size].reshape(shp))
        r += n
    return out


def kernel(x, c, w_ada, b_ada, norm1_g, w_in, conv_w, conv_b, conv_ln_g, conv_ln_b, conv_proj, ssm_a_re, ssm_a_im, ssm_b_re, ssm_b_im, ssm_c_re, ssm_c_im, ssm_d, ssm_log_dt, ssm_glu, w_out, norm2_g, w_ffn_in, w_ffn_out, final_g, loss_target, m_w_ada, m_b_ada, m_norm1_g, m_w_in, m_conv_w, m_conv_b, m_conv_ln_g, m_conv_ln_b, m_conv_proj, m_ssm_a_re, m_ssm_a_im, m_ssm_b_re, m_ssm_b_im, m_ssm_c_re, m_ssm_c_im, m_ssm_d, m_ssm_log_dt, m_ssm_glu, m_w_out, m_norm2_g, m_w_ffn_in, m_w_ffn_out, m_final_g, v_w_ada, v_b_ada, v_norm1_g, v_w_in, v_conv_w, v_conv_b, v_conv_ln_g, v_conv_ln_b, v_conv_proj, v_ssm_a_re, v_ssm_a_im, v_ssm_b_re, v_ssm_b_im, v_ssm_c_re, v_ssm_c_im, v_ssm_d, v_ssm_log_dt, v_ssm_glu, v_w_out, v_norm2_g, v_w_ffn_in, v_w_ffn_out, v_final_g):
    given = dict(locals())
    mx, my, mc = _me()
    chip = 2 * mx + my
    dev = 4 * mx + 2 * my + mc
    def canon(a):
        return a.reshape(1, -1) if a.ndim <= 2 else a[0]

    wts = {n: canon(given[n]) for n in _ORDER}
    mom = {n: canon(given["m_" + n]) for n in _ORDER}
    var = {n: canon(given["v_" + n]) for n in _ORDER}

    c_all = _allgather8(jnp.broadcast_to(c, (8, D_MODEL)), "gather_c")[:, 0, :]
    n_ada = wts["w_ada"].shape[1]
    b_cols = lax.dynamic_slice(wts["b_ada"], (0, chip * n_ada), (1, n_ada))
    mod_cols = _mod_shard(c_all, wts["w_ada"], b_cols)
    mods = _allgather8(mod_cols, "gather_mod")
    mod = jnp.concatenate([lax.dynamic_index_in_dim(mods[2 * q], dev, 0, keepdims=True) for q in range(N_CHIP)], axis=1)

    W = {n: wts[n] for n in _ORDER if n not in _BIG}
    conv_w_full = _allgather8(jnp.pad(wts["conv_w"], ((0, 1), (0, 0))), "gather_conv_w")
    W["conv_w"] = jnp.concatenate([conv_w_full[2 * q, :KW] for q in range(N_CHIP)], axis=1)

    axes = [_BIG_AXIS[n] for n in _BIG]
    shards = [wts[n].astype(BF16) for n in _BIG]
    lands = []
    for s, ax in zip(shards, axes):
        shp = list(s.shape)
        shp[ax] *= N_CHIP
        lands.append(lax.empty(tuple(shp), BF16))
    gstate, token = _gather_start(shards, lands, axes, mod + W["conv_w"][0:1, 0:1])
    gstate = dict(zip(_BIG, gstate))
    mod = mod + token[0:1, 0:1]

    def getw(n, after):
        return _gather_wait(gstate[n], _BIG_AXIS[n], after, "gather_wait_" + n)

    sstate, own, estate = {}, {}, []

    def put(n, g):
        ax = _BIG_AXIS[n]
        k = g.shape[ax] // N_CHIP
        own[n] = lax.dynamic_slice_in_dim(g, chip * k, k, axis=ax)
        sstate[n], tok = _scatter_start(g, ax, "scatter_start_" + n)
        return tok

    first5 = [n for n in _BIG if n != "w_in"]

    def early(sm):
        state, tok = _all8_start(_pack([sm[n] for n in _EARLY]), "small_start")
        estate.append(state)
        recv5 = [_scatter_wait(sstate[n], _BIG_AXIS[n], tok, "scatter_wait_" + n) for n in first5]
        held = [a for n, r in zip(first5, recv5) for a in (own[n], r)]
        state, tok = _swap_start(held, tok, "swap_start")
        estate.append(state)
        return tok

    loss8, dx, dn1, dmod = _device_step(x[0], mod, W, loss_target[0], getw, put, early)
    loss = lax.psum(jnp.sum(loss8), ("x", "y", "c"))

    late = _allgather8(_pack([dn1, dmod]), "gather_late")

    held_in = [own["w_in"], _scatter_wait(sstate["w_in"], _BIG_AXIS["w_in"], late, "scatter_wait_w_in")]
    sib_in = _swap_sibling(held_in)
    held5, sib5 = _swap_wait(estate[1], late, "swap_wait")
    allp = _all8_wait(estate[0], late, "small_wait")

    outs = {"w_in": _adamw(wts["w_in"], mom["w_in"], var["w_in"], [held_in, sib_in], "adamw_w_in")}
    for i, n in enumerate(first5):
        outs[n] = _adamw(wts[n], mom[n], var[n], [held5[2 * i:2 * i + 2], sib5[2 * i:2 * i + 2]], "adamw_" + n)

    r1 = _pack_rows((D_MODEL,))
    dmod_all = late[:, r1:, :].reshape(N_DEV, -1)[:, :6 * D_MODEL]
    dmod_cols = lax.dynamic_slice(dmod_all, (0, chip * n_ada), (N_DEV, n_ada))
    g_ada = _ada_grad(c_all, dmod_cols)
    outs["w_ada"] = _adamw(wts["w_ada"], mom["w_ada"], var["w_ada"], [[g_ada]], "adamw_w_ada")

    def packed_params(d, names):
        return _pack([jnp.zeros((KW, CW), F32) if n == "conv_w" else d[n] for n in names])

    for names, parts, nm in ((_EARLY, allp, "adamw_small"), (_LATE, late, "adamw_late")):
        res = _adamw(packed_params(wts, names), packed_params(mom, names), packed_params(var, names), [[parts]], nm)
        shapes = [(KW, CW) if n == "conv_w" else wts[n].shape for n in names]
        unpacked = [_unpack(r, shapes) for r in res]
        for idx, n in enumerate(names):
            outs[n] = tuple(unpacked[q][idx] for q in range(4))
    g_cw = lax.dynamic_slice(outs["conv_w"][0], (0, chip * (CW // N_CHIP)), (KW, CW // N_CHIP))
    pad = lambda a: jnp.pad(a, ((0, 1), (0, 0)))
    r_cw = _adamw(pad(wts["conv_w"]), pad(mom["conv_w"]), pad(var["conv_w"]), [[pad(g_cw)]], "adamw_conv_w")
    outs["conv_w"] = tuple(r[:KW] for r in r_cw)

    def shaped(n, a):
        return a.reshape(given[n].shape)

    result = [loss, dx[None]]
    for q in range(4):
        result += [shaped(n, outs[n][q]) for n in _ORDER]
    return tuple(result)
```

```python
import math

import jax
import jax.numpy as jnp
import numpy as np
from jax import lax
from jax.experimental import pallas as pl
from jax.experimental.pallas import tpu as pltpu

F32 = jnp.float32
BF16 = jnp.bfloat16
EPS = 1e-6
D_MODEL = 1024
CW = 512
KW = 31
HALO = 32
G, P, H = 32, 64, 16
NST = G * P
FH = 2816
N_DEV = 8
N_CHIP = 4
VMEM_LIMIT = 56 * 1024 * 1024
LR, B1, B2, AEPS, WD, STEP = 0.001, 0.9, 0.999, 1e-08, 0.01, 10
MESH = pl.DeviceIdType.MESH


def _cp(sem=None):
    return pltpu.CompilerParams(dimension_semantics=sem, vmem_limit_bytes=VMEM_LIMIT)


def _sig(x):
    return jax.nn.sigmoid(x)


def _full(shape):
    return pl.BlockSpec(shape, lambda *_: (0,) * len(shape))


def _colsum8(v):
    t, c = v.shape
    return jnp.sum(v.reshape(t // 8, 8, c), axis=0)


def _matmul(a, b, mode, tm, tn, tk, out_dtype, name, after=None, n_outer=False, m_cols=None):
    m0 = 0
    if mode == "nn":
        (M, K), N = a.shape, b.shape[1]
    elif mode == "nt":
        (M, K), N = a.shape, b.shape[0]
    else:
        (K, M), N = a.shape, b.shape[1]
        if m_cols is not None:
            m0, M = m_cols
    tm, tn, tk = min(tm, M), min(tn, N), min(tk, K)
    assert M % tm == 0 and N % tn == 0 and K % tk == 0 and m0 % tm == 0, (name, M, N, K, tm, tn, tk)
    nk = K // tk
    mb = m0 // tm

    def ij(fn):
        return (lambda j, i, k: fn(i, j, k)) if n_outer else fn

    if mode == "nn":
        a_spec = pl.BlockSpec((tm, tk), ij(lambda i, j, k: (i, k)))
        b_spec = pl.BlockSpec((tk, tn), ij(lambda i, j, k: (k, j)))
        dims = (((1,), (0,)), ((), ()))
    elif mode == "nt":
        a_spec = pl.BlockSpec((tm, tk), ij(lambda i, j, k: (i, k)))
        b_spec = pl.BlockSpec((tn, tk), ij(lambda i, j, k: (j, k)))
        dims = (((1,), (1,)), ((), ()))
    else:
        a_spec = pl.BlockSpec((tk, tm), ij(lambda i, j, k: (k, i + mb)))
        b_spec = pl.BlockSpec((tk, tn), ij(lambda i, j, k: (k, j)))
        dims = (((0,), (0,)), ((), ()))

    def body(a_ref, b_ref, *rest):
        o_ref, acc_ref = rest[-2:]
        k = pl.program_id(2)
        part = lax.dot_general(a_ref[...].astype(BF16), b_ref[...].astype(BF16), dims,
                               preferred_element_type=F32)
        if nk == 1:
            o_ref[...] = part.astype(out_dtype)
        else:
            @pl.when(k == 0)
            def _():
                acc_ref[...] = part

            @pl.when(k > 0)
            def _():
                acc_ref[...] += part

            @pl.when(k == nk - 1)
            def _():
                o_ref[...] = acc_ref[...].astype(out_dtype)

    return pl.pallas_call(
        body, name=name,
        out_shape=jax.ShapeDtypeStruct((M, N), out_dtype),
        grid=(N // tn, M // tm, nk) if n_outer else (M // tm, N // tn, nk),
        in_specs=[a_spec, b_spec] + ([] if after is None else [pl.BlockSpec(memory_space=pl.ANY)]),
        out_specs=pl.BlockSpec((tm, tn), ij(lambda i, j, k: (i, j))),
        scratch_shapes=[pltpu.VMEM((tm, tn) if nk > 1 else (8, 128), F32)],
        compiler_params=_cp(("parallel", "parallel", "arbitrary")),
    )(*((a, b) if after is None else (a, b, after)))


def _row_tile(S):
    return min(512, S)


def _in_proj(x, g, sc, sh, w_in):
    S, D = x.shape
    N = w_in.shape[1]
    tm = min(256, S)

    def body(x_ref, g_ref, sc_ref, sh_ref, w_ref, h_ref, z_ref):
        xv = x_ref[...]
        r = lax.rsqrt(jnp.mean(xv * xv, axis=-1, keepdims=True) + EPS)
        h = (xv * r * (g_ref[...] * (1.0 + sc_ref[...])) + sh_ref[...]).astype(BF16)
        h_ref[...] = h
        z_ref[...] = jnp.dot(h, w_ref[...], preferred_element_type=F32)

    row = pl.BlockSpec((tm, D), lambda i: (i, 0))
    par = _full((1, D))
    return pl.pallas_call(
        body, name="in_proj",
        out_shape=(jax.ShapeDtypeStruct((S, D), BF16), jax.ShapeDtypeStruct((S, N), F32)), grid=(S // tm,),
        in_specs=[row, par, par, par, _full((D, N))], out_specs=(row, pl.BlockSpec((tm, N), lambda i: (i, 0))),
        compiler_params=_cp(("parallel",)))(x, g, sc, sh, w_in)


def _conv_fwd(z, conv_w, conv_b, ln_g, ln_b):
    S = z.shape[0]
    tm = min(128, S)
    sub = 32
    hb = tm // HALO

    def body(a_ref, g_ref, ha_ref, hg_ref, w_ref, b_ref, lg_ref, lb_ref, yc_ref, s_ref, ug_ref):
        i = pl.program_id(0)
        halo = ha_ref[...] * _sig(hg_ref[...])
        ug_ref[0:HALO, :] = jnp.where(i == 0, 0.0, halo)
        ug_ref[HALO:, :] = a_ref[...] * _sig(g_ref[...])
        for rb in range(tm // sub):
            acc = jnp.zeros((sub, CW), F32) + b_ref[...]
            for k in range(KW):
                off = rb * sub + HALO - (KW - 1) + k
                acc = acc + w_ref[k:k + 1, :] * ug_ref[off:off + sub, :]
            yc_ref[rb * sub:(rb + 1) * sub, :] = acc
            mu = jnp.mean(acc, axis=-1, keepdims=True)
            cen = acc - mu
            rstd = lax.rsqrt(jnp.mean(cen * cen, axis=-1, keepdims=True) + EPS)
            ln = cen * rstd * lg_ref[...] + lb_ref[...]
            s_ref[rb * sub:(rb + 1) * sub, :] = (ln * _sig(ln)).astype(BF16)

    prev = lambda i: (jnp.maximum(i * hb - 1, 0), 0)
    return pl.pallas_call(
        body, name="conv_fwd",
        out_shape=(jax.ShapeDtypeStruct((S, CW), F32), jax.ShapeDtypeStruct((S, CW), BF16)),
        grid=(S // tm,),
        in_specs=[pl.BlockSpec((tm, CW), lambda i: (i, 0)), pl.BlockSpec((tm, CW), lambda i: (i, 1)),
                  pl.BlockSpec((HALO, CW), prev), pl.BlockSpec((HALO, CW), lambda i: (jnp.maximum(i * hb - 1, 0), 1)),
                  _full((KW, CW)), _full((1, CW)), _full((1, CW)), _full((1, CW))],
        out_specs=(pl.BlockSpec((tm, CW), lambda i: (i, 0)), pl.BlockSpec((tm, CW), lambda i: (i, 0))),
        scratch_shapes=[pltpu.VMEM((tm + HALO, CW), F32)],
        compiler_params=_cp(("parallel",)))(z, z, z, z, conv_w, conv_b, ln_g, ln_b)


def _conv_bwd_ln(dyconv, w_cp, yc, ln_g, ln_b):
    S = yc.shape[0]
    tm = _row_tile(S)

    def body(dy_ref, w_ref, yc_ref, lg_ref, lb_ref, dyc_ref, dlg_ref, dlb_ref, dcb_ref):
        i = pl.program_id(0)
        dsc = lax.dot_general(dy_ref[...], w_ref[...], (((1,), (1,)), ((), ())), preferred_element_type=F32)
        yc_v = yc_ref[...]
        mu = jnp.mean(yc_v, axis=-1, keepdims=True)
        cen = yc_v - mu
        rstd = lax.rsqrt(jnp.mean(cen * cen, axis=-1, keepdims=True) + EPS)
        yn = cen * rstd
        ln = yn * lg_ref[...] + lb_ref[...]
        sl = _sig(ln)
        dln = dsc * (sl * (1.0 + ln * (1.0 - sl)))
        dyn = dln * lg_ref[...]
        dyc = rstd * (dyn - jnp.mean(dyn, axis=-1, keepdims=True)
                      - yn * jnp.mean(dyn * yn, axis=-1, keepdims=True))
        dyc_ref[...] = dyc

        @pl.when(i == 0)
        def _():
            dlg_ref[...] = jnp.zeros_like(dlg_ref)
            dlb_ref[...] = jnp.zeros_like(dlb_ref)
            dcb_ref[...] = jnp.zeros_like(dcb_ref)

        dlg_ref[...] += _colsum8(dln * yn)
        dlb_ref[...] += _colsum8(dln)
        dcb_ref[...] += _colsum8(dyc)

    row = pl.BlockSpec((tm, CW), lambda i: (i, 0))
    acc = jax.ShapeDtypeStruct((8, CW), F32)
    return pl.pallas_call(
        body, name="conv_bwd_ln",
        out_shape=(jax.ShapeDtypeStruct((S, CW), F32), acc, acc, acc), grid=(S // tm,),
        in_specs=[pl.BlockSpec((tm, D_MODEL), lambda i: (i, 0)), _full((CW, D_MODEL)), row, _full((1, CW)),
                  _full((1, CW))],
        out_specs=(row, _full((8, CW)), _full((8, CW)), _full((8, CW))),
        compiler_params=_cp(("arbitrary",)))(dyconv, w_cp, yc, ln_g, ln_b)


def _conv_bwd(dyc, z, conv_w):
    S = z.shape[0]
    tm = min(128, S)
    sub = 32
    hb = tm // HALO
    nt = S // tm

    def body(d_ref, dn_ref, a_ref, g_ref, ha_ref, hg_ref, w_ref, dz_ref, dw_ref, ug_ref, dy_ref):
        i = pl.program_id(0)
        halo = ha_ref[...] * _sig(hg_ref[...])
        ug_ref[0:HALO, :] = jnp.where(i == 0, 0.0, halo)
        a = a_ref[...]
        sg = _sig(g_ref[...])
        ug_ref[HALO:, :] = a * sg
        dy_ref[0:tm, :] = d_ref[...]
        dy_ref[tm:, :] = jnp.where(i == nt - 1, 0.0, dn_ref[...])

        @pl.when(i == 0)
        def _():
            dw_ref[...] = jnp.zeros_like(dw_ref)

        for rb in range(tm // sub):
            r0 = rb * sub
            acc = jnp.zeros((sub, CW), F32)
            dyc_b = dy_ref[r0:r0 + sub, :]
            for k in range(KW):
                up = r0 + (KW - 1) - k
                acc = acc + w_ref[k:k + 1, :] * dy_ref[up:up + sub, :]
                off = r0 + HALO - (KW - 1) + k
                dw_ref[k * 8:(k + 1) * 8, :] += _colsum8(dyc_b * ug_ref[off:off + sub, :])
            a_b = a[r0:r0 + sub, :]
            sg_b = sg[r0:r0 + sub, :]
            dz_ref[r0:r0 + sub, 0:CW] = (acc * sg_b).astype(BF16)
            dz_ref[r0:r0 + sub, CW:2 * CW] = (acc * a_b * sg_b * (1.0 - sg_b)).astype(BF16)

    return pl.pallas_call(
        body, name="conv_bwd",
        out_shape=(jax.ShapeDtypeStruct((S, 2 * CW), BF16), jax.ShapeDtypeStruct((KW * 8, CW), F32)),
        grid=(nt,),
        in_specs=[pl.BlockSpec((tm, CW), lambda i: (i, 0)),
                  pl.BlockSpec((HALO, CW), lambda i: (jnp.minimum((i + 1) * hb, nt * hb - 1), 0)),
                  pl.BlockSpec((tm, CW), lambda i: (i, 0)), pl.BlockSpec((tm, CW), lambda i: (i, 1)),
                  pl.BlockSpec((HALO, CW), lambda i: (jnp.maximum(i * hb - 1, 0), 0)),
                  pl.BlockSpec((HALO, CW), lambda i: (jnp.maximum(i * hb - 1, 0), 1)),
                  _full((KW, CW))],
        out_specs=(pl.BlockSpec((tm, 2 * CW), lambda i: (i, 0)), _full((KW * 8, CW))),
        scratch_shapes=[pltpu.VMEM((tm + HALO, CW), F32), pltpu.VMEM((tm + HALO, CW), F32)],
        compiler_params=_cp(("arbitrary",)))(dyc, dyc, z, z, z, z, conv_w)


_GELU_C = math.sqrt(2.0 / math.pi)


def _gelu(x):
    return 0.5 * x * (1.0 + jnp.tanh(_GELU_C * (x + 0.044715 * x * x * x)))


def _gelu_grad(x):
    t = jnp.tanh(_GELU_C * (x + 0.044715 * x * x * x))
    return 0.5 * (1.0 + t) + 0.5 * x * (1.0 - t * t) * (_GELU_C * (1.0 + 3 * 0.044715 * x * x))


_LW = 512


def _ssm_fwd(z, bb, cm, d, tab):
    S = z.shape[0]
    tm = min(256, S)

    def body(u_ref, bb_ref, cm_ref, d_ref, t_ref, x_ref, ys_ref, yg_ref, car_ref):
        i = pl.program_id(0)

        @pl.when(i == 0)
        def _():
            car_ref[...] = jnp.zeros_like(car_ref)

        u = u_ref[...]
        x_ref[...] = jnp.dot(u.astype(BF16), bb_ref[...], preferred_element_type=F32)
        for c in range(NST // _LW):
            lre = pl.ds(c * _LW, _LW)
            lim = pl.ds(NST + c * _LW, _LW)

            def blk(j, car):
                cr, ci = car
                rows = pl.ds(pl.multiple_of(j * 8, 8), 8)
                r = x_ref[rows, lre]
                im = x_ref[rows, lim]
                for lvl, s in enumerate((1, 2, 4)):
                    mr = t_ref[16 * lvl:16 * lvl + 8, lre]
                    mi = t_ref[16 * lvl + 8:16 * lvl + 16, lre]
                    sr = pltpu.roll(r, s, 0)
                    si = pltpu.roll(im, s, 0)
                    r, im = r + (mr * sr - mi * si), im + (mr * si + mi * sr)
                pr = t_ref[48:56, lre]
                pi_ = t_ref[56:64, lre]
                r, im = r + (pr * cr - pi_ * ci), im + (pr * ci + pi_ * cr)
                x_ref[rows, lre] = r
                x_ref[rows, lim] = im
                return (jnp.broadcast_to(r[7:8, :], (8, _LW)), jnp.broadcast_to(im[7:8, :], (8, _LW)))

            cr, ci = lax.fori_loop(0, tm // 8, blk, (car_ref[:, lre], car_ref[:, lim]))
            car_ref[:, lre] = cr
            car_ref[:, lim] = ci
        ys = jnp.dot(x_ref[...].astype(BF16), cm_ref[...], preferred_element_type=F32) + d_ref[...] * u
        ys_ref[...] = ys
        yg_ref[...] = _gelu(ys).astype(BF16)

    return pl.pallas_call(
        body, name="ssm_fwd",
        out_shape=(jax.ShapeDtypeStruct((S, 2 * NST), F32), jax.ShapeDtypeStruct((S, CW), F32),
                   jax.ShapeDtypeStruct((S, CW), BF16)),
        grid=(S // tm,),
        in_specs=[pl.BlockSpec((tm, CW), lambda i: (i, 2)), _full((CW, 2 * NST)), _full((2 * NST, CW)),
                  _full((1, CW)), _full((64, NST))],
        out_specs=(pl.BlockSpec((tm, 2 * NST), lambda i: (i, 0)), pl.BlockSpec((tm, CW), lambda i: (i, 0)),
                   pl.BlockSpec((tm, CW), lambda i: (i, 0))),
        scratch_shapes=[pltpu.VMEM((8, 2 * NST), F32)],
        compiler_params=_cp(("arbitrary",)))(z, bb, cm, d, tab)


def _ssm_bwd(dzz, w_glu, ys, z, xs, cmt, bbt, d, tab, after):
    S = z.shape[0]
    tm = min(256, S)
    nt = S // tm

    def body(dzz_ref, wglu_ref, ys_ref, u_ref, x_ref, cmt_ref, bbt_ref, d_ref, t_ref, after_ref,
             lam_ref, du_ref, dys_ref, de_ref, dd_ref, car_ref):
        i = pl.program_id(0)

        @pl.when(i == 0)
        def _():
            car_ref[...] = jnp.zeros_like(car_ref)
            de_ref[...] = jnp.zeros_like(de_ref)
            dd_ref[...] = jnp.zeros_like(dd_ref)

        u = u_ref[...]
        dyg = lax.dot_general(dzz_ref[...], wglu_ref[...], (((1,), (1,)), ((), ())), preferred_element_type=F32)
        dys = dyg * _gelu_grad(ys_ref[...])
        dys_ref[...] = dys.astype(BF16)
        dd_ref[...] += _colsum8(dys * u)
        lam_ref[...] = jnp.dot(dys.astype(BF16), cmt_ref[...], preferred_element_type=F32)
        row = lax.broadcasted_iota(jnp.int32, (8, _LW), 0)
        for c in range(NST // _LW):
            lre = pl.ds(c * _LW, _LW)
            lim = pl.ds(NST + c * _LW, _LW)

            def blk(jj, car):
                cr, ci, ar, ai = car
                j = tm // 8 - 1 - jj
                rows = pl.ds(pl.multiple_of(j * 8, 8), 8)
                r = lam_ref[rows, lre]
                im = lam_ref[rows, lim]
                for lvl, s in enumerate((1, 2, 4)):
                    mr = t_ref[16 * lvl:16 * lvl + 8, lre]
                    mi = t_ref[16 * lvl + 8:16 * lvl + 16, lre]
                    sr = pltpu.roll(r, 8 - s, 0)
                    si = pltpu.roll(im, 8 - s, 0)
                    r, im = r + (mr * sr - mi * si), im + (mr * si + mi * sr)
                pr = t_ref[48:56, lre]
                pi_ = t_ref[56:64, lre]
                r, im = r + (pr * cr - pi_ * ci), im + (pr * ci + pi_ * cr)
                lam_ref[rows, lre] = r
                lam_ref[rows, lim] = im
                nr = jnp.where(row == 7, cr, pltpu.roll(r, 7, 0))
                ni = jnp.where(row == 7, ci, pltpu.roll(im, 7, 0))
                xr = x_ref[rows, lre]
                xi = x_ref[rows, lim]
                ar = ar + (nr * xr + ni * xi)
                ai = ai + (ni * xr - nr * xi)
                return (jnp.broadcast_to(r[0:1, :], (8, _LW)), jnp.broadcast_to(im[0:1, :], (8, _LW)), ar, ai)

            zero = jnp.zeros((8, _LW), F32)
            cr, ci, ar, ai = lax.fori_loop(0, tm // 8, blk, (car_ref[:, lre], car_ref[:, lim], zero, zero))
            car_ref[:, lre] = cr
            car_ref[:, lim] = ci
            de_ref[0:8, lre] += ar
            de_ref[8:16, lre] += ai
        du = jnp.dot(lam_ref[...].astype(BF16), bbt_ref[...], preferred_element_type=F32) + dys * d_ref[...]
        du_ref[...] = du.astype(BF16)

    rev = lambda i: (nt - 1 - i, 0)
    return pl.pallas_call(
        body, name="ssm_bwd",
        out_shape=(jax.ShapeDtypeStruct((S, 2 * NST), F32), jax.ShapeDtypeStruct((S, CW), BF16),
                   jax.ShapeDtypeStruct((S, CW), BF16), jax.ShapeDtypeStruct((16, NST), F32),
                   jax.ShapeDtypeStruct((8, CW), F32)),
        grid=(nt,),
        in_specs=[pl.BlockSpec((tm, 2 * D_MODEL), rev), _full((CW, 2 * D_MODEL)), pl.BlockSpec((tm, CW), rev),
                  pl.BlockSpec((tm, CW), lambda i: (nt - 1 - i, 2)), pl.BlockSpec((tm, 2 * NST), rev),
                  _full((CW, 2 * NST)), _full((2 * NST, CW)), _full((1, CW)), _full((64, NST)),
                  pl.BlockSpec(memory_space=pl.ANY)],
        out_specs=(pl.BlockSpec((tm, 2 * NST), rev), pl.BlockSpec((tm, CW), rev), pl.BlockSpec((tm, CW), rev),
                   _full((16, NST)), _full((8, CW))),
        scratch_shapes=[pltpu.VMEM((8, 2 * NST), F32)],
        compiler_params=_cp(("arbitrary",)))(dzz, w_glu, ys, z, xs, cmt, bbt, d, tab, after)


def _ssm_prep(a_re, a_im, b_re, b_im, log_dt):
    dt = jnp.exp(log_dt.reshape(G))[:, None]
    mag = jnp.exp(dt * a_re)
    e_re, e_im = mag * jnp.cos(dt * a_im), mag * jnp.sin(dt * a_im)
    n_re, n_im = e_re - 1.0, e_im
    den = a_re * a_re + a_im * a_im
    q_re = (n_re * a_re + n_im * a_im) / den
    q_im = (n_im * a_re - n_re * a_im) / den
    bb_re = q_re[..., None] * b_re - q_im[..., None] * b_im
    bb_im = q_re[..., None] * b_im + q_im[..., None] * b_re
    return e_re, e_im, bb_re, bb_im


def _scan_tables(e_re, e_im, reverse):
    er = e_re.reshape(1, NST)
    ei = e_im.reshape(1, NST)
    if reverse:
        ei = -ei
    pows = [(er, ei)]
    for _ in range(7):
        pr, pi_ = pows[-1]
        pows.append((pr * er - pi_ * ei, pr * ei + pi_ * er))
    row = jnp.arange(8)[:, None]
    out = []
    for s in (1, 2, 4):
        pr, pi_ = pows[s - 1]
        keep = (row + s <= 7) if reverse else (row >= s)
        out += [jnp.where(keep, pr, 0.0), jnp.where(keep, pi_, 0.0)]
    allr = jnp.concatenate([p[0] for p in pows], 0)
    alli = jnp.concatenate([p[1] for p in pows], 0)
    if reverse:
        allr, alli = allr[::-1], alli[::-1]
    out += [allr, alli]
    return jnp.concatenate(out, 0).astype(F32)


def _block_diag_mats(bb_re, bb_im, c_re, c_im):
    eye = jnp.eye(G, dtype=F32)
    bre = jnp.einsum("gph,gk->ghkp", bb_re, eye).reshape(CW, NST)
    bim = jnp.einsum("gph,gk->ghkp", bb_im, eye).reshape(CW, NST)
    bb = jnp.concatenate([bre, bim], 1)
    cre = jnp.einsum("ghp,gk->gpkh", c_re, eye).reshape(NST, CW)
    cim = jnp.einsum("ghp,gk->gpkh", c_im, eye).reshape(NST, CW)
    cm = jnp.concatenate([cre, -cim], 0)
    return bb, cm


def _diag_blocks(full):
    return jnp.einsum("ghkp,gk->ghp", full.reshape(G, H, G, P), jnp.eye(G, dtype=F32))


def _mix_fwd(scv, yg, z, x, w_cp, w_glu, w_out, g1, n2g, sc2, sh2):
    S = z.shape[0]
    tm = min(256, S)
    D = D_MODEL

    def body(s_ref, yg_ref, glc0_ref, glc1_ref, gls0_ref, gls1_ref, x_ref, wcp_ref, wglu_ref, wout_ref,
             g1_ref, n2_ref, sc_ref, sh_ref, yc_ref, zz_ref, m_ref, o_ref, x2_ref, h2_ref):
        y_conv = jnp.dot(s_ref[...], wcp_ref[...], preferred_element_type=F32)
        zz = jnp.dot(yg_ref[...], wglu_ref[...], preferred_element_type=F32)
        yc_ref[...] = y_conv.astype(BF16)
        zz_ref[...] = zz.astype(BF16)
        for half, (glc_ref, gls_ref) in enumerate(((glc0_ref, gls0_ref), (glc1_ref, gls1_ref))):
            lo, hi = half * CW, (half + 1) * CW
            y_ssm = zz[:, lo:hi] * _sig(zz[:, D + lo:D + hi])
            m_ref[:, lo:hi] = (_sig(glc_ref[...]) * y_conv[:, lo:hi] + _sig(gls_ref[...]) * y_ssm).astype(BF16)
        o = jnp.dot(m_ref[...], wout_ref[...], preferred_element_type=F32)
        o_ref[...] = o.astype(BF16)
        xv = x_ref[...] + g1_ref[...] * o
        x2_ref[...] = xv
        r = lax.rsqrt(jnp.mean(xv * xv, axis=-1, keepdims=True) + EPS)
        h2_ref[...] = (xv * r * (n2_ref[...] * (1.0 + sc_ref[...])) + sh_ref[...]).astype(BF16)

    zb_ = lambda j: pl.BlockSpec((tm, CW), lambda i: (i, j))
    row = lambda w: pl.BlockSpec((tm, w), lambda i: (i, 0))
    par = _full((1, D))
    bf = lambda w: jax.ShapeDtypeStruct((S, w), BF16)
    return pl.pallas_call(
        body, name="mix_fwd",
        out_shape=(bf(D), bf(2 * D), bf(D), bf(D), jax.ShapeDtypeStruct((S, D), F32), bf(D)),
        grid=(S // tm,),
        in_specs=[row(CW), row(CW), zb_(3), zb_(4), zb_(5), zb_(6), row(D), _full((CW, D)), _full((CW, 2 * D)),
                  _full((D, D)), par, par, par, par],
        out_specs=(row(D), row(2 * D), row(D), row(D), row(D), row(D)),
        compiler_params=_cp(("parallel",)))(scv, yg, z, z, z, z, x, w_cp, w_glu, w_out, g1, n2g, sc2, sh2)


def _mix_bwd(do, w_out, z, zz, y_conv, after):
    S = z.shape[0]
    tm = min(256, S)
    D = D_MODEL

    def body(do_ref, w_ref, glc0_ref, glc1_ref, gls0_ref, gls1_ref, za_ref, zb_ref, yc_ref, after_ref,
             dyc_ref, dgl_ref, dzz_ref):
        dm = lax.dot_general(do_ref[...], w_ref[...], (((1,), (1,)), ((), ())), preferred_element_type=F32)
        for half, (glc_ref, gls_ref) in enumerate(((glc0_ref, gls0_ref), (glc1_ref, gls1_ref))):
            lo, hi = half * CW, (half + 1) * CW
            dm_v = dm[:, lo:hi]
            sgc = _sig(glc_ref[...])
            sgs = _sig(gls_ref[...])
            szb = _sig(zb_ref[:, lo:hi].astype(F32))
            za = za_ref[:, lo:hi].astype(F32)
            dyc_ref[:, lo:hi] = (dm_v * sgc).astype(BF16)
            dgl_ref[:, lo:hi] = (dm_v * yc_ref[:, lo:hi].astype(F32) * sgc * (1.0 - sgc)).astype(BF16)
            dys = dm_v * sgs
            dgl_ref[:, D + lo:D + hi] = (dys * (za * szb) * (1.0 - sgs)).astype(BF16)
            dzz_ref[:, lo:hi] = (dys * szb).astype(BF16)
            dzz_ref[:, D + lo:D + hi] = (dys * za * szb * (1.0 - szb)).astype(BF16)

    zb_ = lambda j: pl.BlockSpec((tm, CW), lambda i: (i, j))
    wide = lambda j: pl.BlockSpec((tm, D), lambda i: (i, j))
    return pl.pallas_call(
        body, name="mix_bwd",
        out_shape=(jax.ShapeDtypeStruct((S, D), BF16), jax.ShapeDtypeStruct((S, 2 * D), BF16),
                   jax.ShapeDtypeStruct((S, 2 * D), BF16)),
        grid=(S // tm,),
        in_specs=[wide(0), _full((D, D)), zb_(3), zb_(4), zb_(5), zb_(6), wide(0), wide(1), wide(0),
                  pl.BlockSpec(memory_space=pl.ANY)],
        out_specs=(wide(0), pl.BlockSpec((tm, 2 * D), lambda i: (i, 0)), pl.BlockSpec((tm, 2 * D), lambda i: (i, 0))),
        compiler_params=_cp(("parallel",)))(do, w_out, z, z, z, z, zz, zz, y_conv, after)


_FC = 1408


def _ffn_in_act(h2, w_fi):
    S, D = h2.shape
    tm = min(256, S)

    def body(h_ref, w_ref, f_ref, a_ref):
        hv = h_ref[...]
        for c in range(FH // _FC):
            lo, hi = c * _FC, (c + 1) * _FC
            g = jnp.dot(hv, w_ref[:, lo:hi], preferred_element_type=F32)
            u = jnp.dot(hv, w_ref[:, FH + lo:FH + hi], preferred_element_type=F32)
            f_ref[:, lo:hi] = g.astype(BF16)
            f_ref[:, FH + lo:FH + hi] = u.astype(BF16)
            a_ref[:, lo:hi] = (g * _sig(g) * u).astype(BF16)

    return pl.pallas_call(
        body, name="ffn_in_act",
        out_shape=(jax.ShapeDtypeStruct((S, 2 * FH), BF16), jax.ShapeDtypeStruct((S, FH), BF16)),
        grid=(S // tm,),
        in_specs=[pl.BlockSpec((tm, D), lambda i: (i, 0)), _full((D, 2 * FH))],
        out_specs=(pl.BlockSpec((tm, 2 * FH), lambda i: (i, 0)), pl.BlockSpec((tm, FH), lambda i: (i, 0))),
        compiler_params=_cp(("parallel",)))(h2, w_fi)


def _ffn_bwd(do2, w_fo, f, after):
    S, D = do2.shape
    tm = min(256, S)

    def body(d_ref, w_ref, f_ref, after_ref, df_ref):
        dv = d_ref[...]
        for c in range(FH // _FC):
            lo, hi = c * _FC, (c + 1) * _FC
            dact = lax.dot_general(dv, w_ref[lo:hi, :], (((1,), (1,)), ((), ())), preferred_element_type=F32)
            g = f_ref[:, lo:hi].astype(F32)
            u = f_ref[:, FH + lo:FH + hi].astype(F32)
            sg = _sig(g)
            df_ref[:, lo:hi] = (dact * u * (sg * (1.0 + g * (1.0 - sg)))).astype(BF16)
            df_ref[:, FH + lo:FH + hi] = (dact * g * sg).astype(BF16)

    return pl.pallas_call(
        body, name="ffn_bwd", out_shape=jax.ShapeDtypeStruct((S, 2 * FH), BF16), grid=(S // tm,),
        in_specs=[pl.BlockSpec((tm, D), lambda i: (i, 0)), _full((FH, D)),
                  pl.BlockSpec((tm, 2 * FH), lambda i: (i, 0)), pl.BlockSpec(memory_space=pl.ANY)],
        out_specs=pl.BlockSpec((tm, 2 * FH), lambda i: (i, 0)),
        compiler_params=_cp(("parallel",)))(do2, w_fo, f, after)


def _ffn_out_final(x2, act, w_fo, g2, fg, tgt):
    S, D = x2.shape
    tm = min(256, S)

    def body(x2_ref, a_ref, w_ref, g2_ref, fg_ref, t_ref, dx3_ref, do2_ref, ls_ref, dfg_ref, dg2_ref):
        i = pl.program_id(0)
        o2 = jnp.dot(a_ref[...], w_ref[...], preferred_element_type=F32)
        x3 = x2_ref[...] + g2_ref[...] * o2
        r = lax.rsqrt(jnp.mean(x3 * x3, axis=-1, keepdims=True) + EPS)
        xn = x3 * r
        err = xn * fg_ref[...] - t_ref[...]
        dy = err * (1.0 / D)
        dxn = dy * fg_ref[...]
        dx3 = r * (dxn - xn * jnp.mean(dxn * xn, axis=-1, keepdims=True))
        dx3_ref[...] = dx3
        do2_ref[...] = (dx3 * g2_ref[...]).astype(BF16)

        @pl.when(i == 0)
        def _():
            ls_ref[...] = jnp.zeros_like(ls_ref)
            dfg_ref[...] = jnp.zeros_like(dfg_ref)
            dg2_ref[...] = jnp.zeros_like(dg2_ref)

        e2 = _colsum8(err * err)
        lanes = e2[:, 0:128]
        for q in range(1, D // 128):
            lanes = lanes + e2[:, q * 128:(q + 1) * 128]
        ls_ref[...] += lanes * (0.5 / D)
        dfg_ref[...] += _colsum8(dy * xn)
        dg2_ref[...] += _colsum8(dx3 * o2)

    row = pl.BlockSpec((tm, D), lambda i: (i, 0))
    par = _full((1, D))
    return pl.pallas_call(
        body, name="final_loss",
        out_shape=(jax.ShapeDtypeStruct((S, D), F32), jax.ShapeDtypeStruct((S, D), BF16),
                   jax.ShapeDtypeStruct((8, 128), F32), jax.ShapeDtypeStruct((8, D), F32),
                   jax.ShapeDtypeStruct((8, D), F32)),
        grid=(S // tm,), in_specs=[row, pl.BlockSpec((tm, FH), lambda i: (i, 0)), _full((FH, D)), par, par, row],
        out_specs=(row, row, _full((8, 128)), _full((8, D)), _full((8, D))),
        compiler_params=_cp(("arbitrary",)))(x2, act, w_fo, g2, fg, tgt)


def _normmod_bwd(dsrc, w, xin, dres, g, sc, gate, o, after, name):
    S, D = xin.shape
    K = dsrc.shape[1]
    tm = min(256, S)

    def body(ds_ref, w_ref, x_ref, dr_ref, g_ref, sc_ref, gate_ref, o_ref, after_ref,
             dx_ref, do_ref, dsh_ref, dsc_ref, dg_ref, dgate_ref):
        i = pl.program_id(0)
        xv = x_ref[...]
        r = lax.rsqrt(jnp.mean(xv * xv, axis=-1, keepdims=True) + EPS)
        xn = xv * r
        dh_v = lax.dot_general(ds_ref[...], w_ref[...], (((1,), (1,)), ((), ())), preferred_element_type=F32)
        gv = g_ref[...]
        scale = 1.0 + sc_ref[...]
        dxn = dh_v * (gv * scale)
        dx = dr_ref[...] + r * (dxn - xn * jnp.mean(dxn * xn, axis=-1, keepdims=True))
        dx_ref[...] = dx
        do_ref[...] = (dx * gate_ref[...]).astype(BF16)

        @pl.when(i == 0)
        def _():
            dsh_ref[...] = jnp.zeros_like(dsh_ref)
            dsc_ref[...] = jnp.zeros_like(dsc_ref)
            dg_ref[...] = jnp.zeros_like(dg_ref)
            dgate_ref[...] = jnp.zeros_like(dgate_ref)

        hx = dh_v * xn
        dsh_ref[...] += _colsum8(dh_v)
        dsc_ref[...] += _colsum8(hx) * gv
        dg_ref[...] += _colsum8(hx) * scale
        dgate_ref[...] += _colsum8(dx * o_ref[...])

    row = pl.BlockSpec((tm, D), lambda i: (i, 0))
    par = _full((1, D))
    acc = jax.ShapeDtypeStruct((8, D), F32)
    return pl.pallas_call(
        body, name=name,
        out_shape=(jax.ShapeDtypeStruct((S, D), F32), jax.ShapeDtypeStruct((S, D), BF16), acc, acc, acc, acc),
        grid=(S // tm,),
        in_specs=[pl.BlockSpec((tm, K), lambda i: (i, 0)), _full((D, K)), row, row, par, par, par, row,
                  pl.BlockSpec(memory_space=pl.ANY)],
        out_specs=(row, row, _full((8, D)), _full((8, D)), _full((8, D)), _full((8, D))),
        compiler_params=_cp(("arbitrary",)))(dsrc, w, xin, dres, g, sc, gate, o, after)


def _me():
    return lax.axis_index("x"), lax.axis_index("y"), lax.axis_index("c")


def _allgather8(v, name):
    R, C = v.shape

    def body(v_ref, out_ref, send_sems, recv_sems, local_sem):
        x, y, c = _me()
        mine = pltpu.make_async_copy(v_ref, out_ref.at[4 * x + 2 * y + c], local_sem)
        mine.start()
        copies = []
        for k in range(1, N_DEV):
            fx, fy, fc = (k >> 2) & 1, (k >> 1) & 1, k & 1
            peer = (x ^ fx, y ^ fy, c ^ fc)
            copies.append(pltpu.make_async_remote_copy(
                src_ref=v_ref, dst_ref=out_ref.at[4 * x + 2 * y + c],
                send_sem=send_sems.at[k - 1], recv_sem=recv_sems.at[k - 1],
                device_id=peer, device_id_type=MESH))
        for cp in copies:
            cp.start()
        for k in range(1, N_DEV):
            fx, fy, fc = (k >> 2) & 1, (k >> 1) & 1, k & 1
            src_slot = 4 * (x ^ fx) + 2 * (y ^ fy) + (c ^ fc)
            pltpu.make_async_remote_copy(
                src_ref=v_ref, dst_ref=out_ref.at[src_slot],
                send_sem=send_sems.at[k - 1], recv_sem=recv_sems.at[k - 1],
                device_id=(x ^ fx, y ^ fy, c ^ fc), device_id_type=MESH).wait_recv()
        for cp in copies:
            cp.wait_send()
        mine.wait()

    return pl.pallas_call(
        body, name=name, out_shape=jax.ShapeDtypeStruct((N_DEV, R, C), v.dtype),
        in_specs=[pl.BlockSpec(memory_space=pltpu.VMEM)], out_specs=pl.BlockSpec(memory_space=pltpu.VMEM),
        scratch_shapes=[pltpu.SemaphoreType.DMA((N_DEV - 1,)), pltpu.SemaphoreType.DMA((N_DEV - 1,)),
                        pltpu.SemaphoreType.DMA],
        compiler_params=pltpu.CompilerParams(vmem_limit_bytes=VMEM_LIMIT))(v)


def _swap_sibling(arrs):
    nw = len(arrs)

    def body(*refs):
        ins, outs = refs[:nw], refs[nw:2 * nw]
        send_sems, recv_sems = refs[2 * nw:]
        x, y, c = _me()
        copies = [pltpu.make_async_remote_copy(
            src_ref=ins[w], dst_ref=outs[w], send_sem=send_sems.at[w], recv_sem=recv_sems.at[w],
            device_id=(x, y, 1 - c), device_id_type=MESH) for w in range(nw)]
        for cp in copies:
            cp.start()
        for cp in copies:
            cp.wait_recv()
        for cp in copies:
            cp.wait_send()

    hbm = pl.BlockSpec(memory_space=pltpu.HBM)
    return pl.pallas_call(
        body, name="swap_sibling", out_shape=tuple(jax.ShapeDtypeStruct(a.shape, a.dtype) for a in arrs),
        in_specs=[hbm] * nw, out_specs=tuple([hbm] * nw),
        scratch_shapes=[pltpu.SemaphoreType.DMA((nw,)), pltpu.SemaphoreType.DMA((nw,))],
        compiler_params=pltpu.CompilerParams(vmem_limit_bytes=VMEM_LIMIT))(*arrs)


_HBM = pl.BlockSpec(memory_space=pltpu.HBM)
_SEM = pl.BlockSpec(memory_space=pltpu.SEMAPHORE)
_EFFECT = pltpu.SideEffectType.DATAFLOW_SIDE_EFFECTING
_N_PEER = N_CHIP - 1


def _chip_part(ref, axis, n, chip):
    start = pl.multiple_of(chip * n, 8)
    return ref.at[pl.ds(start, n), :] if axis == 0 else ref.at[:, pl.ds(start, n)]


def _gather_copy(k, src_ref, land_ref, send_sems, recv_sems, axis, arriving):
    x, y, c = _me()
    px, py = x ^ ((k >> 1) & 1), y ^ (k & 1)
    chip = 2 * px + py if arriving else 2 * x + y
    return pltpu.make_async_remote_copy(
        src_ref=src_ref, dst_ref=_chip_part(land_ref, axis, src_ref.shape[axis], chip),
        send_sem=send_sems.at[k - 1], recv_sem=recv_sems.at[k - 1], device_id=(px, py, c), device_id_type=MESH)


def _scatter_copy(k, grad_ref, land_ref, send_sems, recv_sems, axis):
    x, y, c = _me()
    px, py = x ^ ((k >> 1) & 1), y ^ (k & 1)
    return pltpu.make_async_remote_copy(
        src_ref=_chip_part(grad_ref, axis, grad_ref.shape[axis] // N_CHIP, 2 * px + py), dst_ref=land_ref.at[k - 1],
        send_sem=send_sems.at[k - 1], recv_sem=recv_sems.at[k - 1], device_id=(px, py, c), device_id_type=MESH)


def _own_copy(src_ref, land_ref, sends, axis):
    x, y, _ = _me()
    return pltpu.make_async_copy(src_ref, _chip_part(land_ref, axis, src_ref.shape[axis], 2 * x + y),
                                 sends.at[_N_PEER])


def _gather_start(shards, lands, axes, after):
    nw = len(shards)

    def body(*refs):
        srcs, zones = refs[:nw], refs[nw:2 * nw]
        sends, recvs = refs[2 * nw + 1:3 * nw + 1], refs[3 * nw + 1:4 * nw + 1]
        token = refs[-1]
        for w in range(nw):
            for k in range(1, N_CHIP):
                _gather_copy(k, srcs[w], zones[w], sends[w], recvs[w], axes[w], False).start()
        for w in range(nw):
            _own_copy(srcs[w], zones[w], sends[w], axes[w]).start()
        token[...] = jnp.zeros_like(token)

    outs = pl.pallas_call(
        body, name="gather_start",
        out_shape=tuple([pltpu.SemaphoreType.DMA((_N_PEER + 1,))] * nw + [pltpu.SemaphoreType.DMA((_N_PEER,))] * nw
                        + [pltpu.HBM(a.shape, a.dtype) for a in list(shards) + list(lands)]
                        + [jax.ShapeDtypeStruct((8, 128), F32)]),
        in_specs=[_HBM] * (2 * nw) + [pl.BlockSpec(memory_space=pl.ANY)],
        out_specs=tuple([_SEM] * (2 * nw) + [_HBM] * (2 * nw) + [pl.BlockSpec(memory_space=pltpu.VMEM)]),
        input_output_aliases={i: 2 * nw + i for i in range(2 * nw)},
        compiler_params=pltpu.CompilerParams(has_side_effects=_EFFECT),
    )(*([pltpu.with_memory_space_constraint(a, pltpu.HBM) for a in list(shards) + list(lands)] + [after]))
    per_weight = [(outs[w], outs[nw + w], outs[2 * nw + w], outs[3 * nw + w]) for w in range(nw)]
    return per_weight, outs[-1]


def _gather_wait(state, axis, after, name):
    send_sems, recv_sems, shard, land = state

    after = list(after) if isinstance(after, (list, tuple)) else [after]

    def body(src_ref, land_ref, sends, recvs, *rest):
        for k in range(1, N_CHIP):
            _gather_copy(k, src_ref, land_ref, sends, recvs, axis, False).wait_send()
            _gather_copy(k, src_ref, land_ref, sends, recvs, axis, True).wait_recv()
        _own_copy(src_ref, land_ref, sends, axis).wait()

    return pl.pallas_call(
        body, name=name, out_shape=(pltpu.HBM(shard.shape, shard.dtype), pltpu.HBM(land.shape, land.dtype)),
        in_specs=[_HBM, _HBM, _SEM, _SEM] + [pl.BlockSpec(memory_space=pl.ANY)] * len(after), out_specs=(_HBM, _HBM),
        input_output_aliases={0: 0, 1: 1},
        compiler_params=pltpu.CompilerParams(has_side_effects=_EFFECT),
    )(shard, land, send_sems, recv_sems, *after)[1]


def _all8_copy(k, v_ref, land_ref, send_sems, recv_sems, arriving):
    x, y, c = _me()
    px, py, pc = x ^ ((k >> 2) & 1), y ^ ((k >> 1) & 1), c ^ (k & 1)
    slot = 4 * px + 2 * py + pc if arriving else 4 * x + 2 * y + c
    return pltpu.make_async_remote_copy(
        src_ref=v_ref, dst_ref=land_ref.at[slot], send_sem=send_sems.at[k - 1], recv_sem=recv_sems.at[k - 1],
        device_id=(px, py, pc), device_id_type=MESH)


def _all8_own(v_ref, land_ref, send_sems):
    x, y, c = _me()
    return pltpu.make_async_copy(v_ref, land_ref.at[4 * x + 2 * y + c], send_sems.at[N_DEV - 1])


def _all8_start(v, name):
    land = lax.empty((N_DEV,) + v.shape, v.dtype)

    def body(v_ref, land_ref, sends, recvs, v_thru, land_thru, token):
        for k in range(1, N_DEV):
            _all8_copy(k, v_ref, land_ref, sends, recvs, False).start()
        _all8_own(v_ref, land_ref, sends).start()
        token[...] = jnp.zeros_like(token)

    outs = pl.pallas_call(
        body, name=name,
        out_shape=(pltpu.SemaphoreType.DMA((N_DEV,)), pltpu.SemaphoreType.DMA((N_DEV - 1,)),
                   pltpu.HBM(v.shape, v.dtype), pltpu.HBM(land.shape, land.dtype),
                   jax.ShapeDtypeStruct((8, 128), F32)),
        in_specs=[_HBM, _HBM], out_specs=(_SEM, _SEM, _HBM, _HBM, pl.BlockSpec(memory_space=pltpu.VMEM)),
        input_output_aliases={0: 2, 1: 3},
        compiler_params=pltpu.CompilerParams(has_side_effects=_EFFECT),
    )(pltpu.with_memory_space_constraint(v, pltpu.HBM), pltpu.with_memory_space_constraint(land, pltpu.HBM))
    return outs[:4], outs[4]


def _all8_wait(state, after, name):
    send_sems, recv_sems, v, land = state

    def body(v_ref, land_ref, sends, recvs, after_ref, v_dead, got_ref):
        for k in range(1, N_DEV):
            _all8_copy(k, v_ref, land_ref, sends, recvs, False).wait_send()
            _all8_copy(k, v_ref, land_ref, sends, recvs, True).wait_recv()
        _all8_own(v_ref, land_ref, sends).wait()

    return pl.pallas_call(
        body, name=name, out_shape=(pltpu.HBM(v.shape, v.dtype), pltpu.HBM(land.shape, land.dtype)),
        in_specs=[_HBM, _HBM, _SEM, _SEM, pl.BlockSpec(memory_space=pl.ANY)], out_specs=(_HBM, _HBM),
        input_output_aliases={0: 0, 1: 1},
        compiler_params=pltpu.CompilerParams(has_side_effects=_EFFECT),
    )(v, land, send_sems, recv_sems, after)[1]


def _swap_copy(w, src_ref, land_ref, send_sems, recv_sems):
    x, y, c = _me()
    return pltpu.make_async_remote_copy(src_ref=src_ref, dst_ref=land_ref, send_sem=send_sems.at[w],
                                        recv_sem=recv_sems.at[w], device_id=(x, y, 1 - c), device_id_type=MESH)


def _swap_start(arrs, after, name):
    nw = len(arrs)
    lands = [lax.empty(a.shape, a.dtype) for a in arrs]

    def body(*refs):
        srcs, zones = refs[:nw], refs[nw:2 * nw]
        sends, recvs = refs[2 * nw + 1], refs[2 * nw + 2]
        for w in range(nw):
            _swap_copy(w, srcs[w], zones[w], sends, recvs).start()
        refs[-1][...] = jnp.zeros_like(refs[-1])

    sem = pltpu.SemaphoreType.DMA((nw,))
    outs = pl.pallas_call(
        body, name=name,
        out_shape=tuple([sem, sem] + [pltpu.HBM(a.shape, a.dtype) for a in list(arrs) + lands]
                        + [jax.ShapeDtypeStruct((8, 128), F32)]),
        in_specs=[_HBM] * (2 * nw) + [pl.BlockSpec(memory_space=pl.ANY)],
        out_specs=tuple([_SEM, _SEM] + [_HBM] * (2 * nw) + [pl.BlockSpec(memory_space=pltpu.VMEM)]),
        input_output_aliases={i: 2 + i for i in range(2 * nw)},
        compiler_params=pltpu.CompilerParams(has_side_effects=_EFFECT),
    )(*([pltpu.with_memory_space_constraint(a, pltpu.HBM) for a in list(arrs) + lands] + [after]))
    return (outs[0], outs[1], outs[2:2 + nw], outs[2 + nw:2 + 2 * nw]), outs[-1]


def _swap_wait(state, after, name):
    send_sems, recv_sems, arrs, lands = state
    nw = len(arrs)

    def body(*refs):
        srcs, zones = refs[:nw], refs[nw:2 * nw]
        sends, recvs = refs[2 * nw], refs[2 * nw + 1]
        for w in range(nw):
            cp = _swap_copy(w, srcs[w], zones[w], sends, recvs)
            cp.wait_send()
            cp.wait_recv()

    outs = pl.pallas_call(
        body, name=name, out_shape=tuple(pltpu.HBM(a.shape, a.dtype) for a in list(arrs) + list(lands)),
        in_specs=[_HBM] * (2 * nw) + [_SEM, _SEM, pl.BlockSpec(memory_space=pl.ANY)],
        out_specs=tuple([_HBM] * (2 * nw)),
        input_output_aliases={i: i for i in range(2 * nw)},
        compiler_params=pltpu.CompilerParams(has_side_effects=_EFFECT),
    )(*arrs, *lands, send_sems, recv_sems, after)
    return list(outs[:nw]), list(outs[nw:])


def _scatter_start(grad, axis, name):
    shp = list(grad.shape)
    shp[axis] //= N_CHIP
    land = lax.empty((_N_PEER,) + tuple(shp), grad.dtype)

    def body(grad_ref, land_ref, sends, recvs, grad_thru, land_thru, token):
        for k in range(1, N_CHIP):
            _scatter_copy(k, grad_ref, land_ref, sends, recvs, axis).start()
        token[...] = jnp.zeros_like(token)

    sem = pltpu.SemaphoreType.DMA((_N_PEER,))
    outs = pl.pallas_call(
        body, name=name,
        out_shape=(sem, sem, pltpu.HBM(grad.shape, grad.dtype), pltpu.HBM(land.shape, land.dtype),
                   jax.ShapeDtypeStruct((8, 128), F32)),
        in_specs=[_HBM, _HBM], out_specs=(_SEM, _SEM, _HBM, _HBM, pl.BlockSpec(memory_space=pltpu.VMEM)),
        input_output_aliases={0: 2, 1: 3},
        compiler_params=pltpu.CompilerParams(has_side_effects=_EFFECT),
    )(pltpu.with_memory_space_constraint(grad, pltpu.HBM), pltpu.with_memory_space_constraint(land, pltpu.HBM))
    return outs[:4], outs[4]


def _scatter_wait(state, axis, after, name):
    send_sems, recv_sems, grad, land = state

    def body(grad_ref, land_ref, sends, recvs, after_ref, grad_dead, got_ref):
        for k in range(1, N_CHIP):
            cp = _scatter_copy(k, grad_ref, land_ref, sends, recvs, axis)
            cp.wait_send()
            cp.wait_recv()

    return pl.pallas_call(
        body, name=name, out_shape=(pltpu.HBM(grad.shape, grad.dtype), pltpu.HBM(land.shape, land.dtype)),
        in_specs=[_HBM, _HBM, _SEM, _SEM, pl.BlockSpec(memory_space=pl.ANY)], out_specs=(_HBM, _HBM),
        input_output_aliases={0: 0, 1: 1},
        compiler_params=pltpu.CompilerParams(has_side_effects=_EFFECT),
    )(grad, land, send_sems, recv_sems, after)[1]


_C1 = 1.0 - B1 ** STEP
_C2 = 1.0 - B2 ** STEP


def _adam_math(w, g, m, v):
    m = B1 * m + (1.0 - B1) * g
    v = B2 * v + (1.0 - B2) * (g * g)
    delta = -LR * ((m / _C1) / (jnp.sqrt(v / _C2) + AEPS) + WD * w)
    return delta, m, v


def _adamw(w, m, v, groups, name):
    R, C = w.shape
    tr = R if R <= 256 else (128 if R % 128 == 0 else 176)
    assert R % tr == 0, (name, R)
    gparts = [p for grp in groups for p in grp]
    sizes = [len(grp) for grp in groups]
    ng = len(gparts)

    def body(*refs):
        w_ref, m_ref, v_ref = refs[:3]
        g_refs = list(refs[3:3 + ng])
        g_out, d_out, m_out, v_out = refs[3 + ng:]
        g = None
        for size in sizes:
            s = None
            for r in [g_refs.pop(0) for _ in range(size)]:
                terms = [r[q] for q in range(r.shape[0])] if len(r.shape) == 3 else [r[...]]
                for t in terms:
                    s = t.astype(F32) if s is None else s + t.astype(F32)
            g = s if g is None else g + s
        delta, mn, vn = _adam_math(w_ref[...], g, m_ref[...], v_ref[...])
        g_out[...] = g
        d_out[...] = delta
        m_out[...] = mn
        v_out[...] = vn

    blk = pl.BlockSpec((tr, C), lambda i: (i, 0))
    g_specs = [blk if p.ndim == 2 else pl.BlockSpec((p.shape[0], tr, C), lambda i: (0, i, 0)) for p in gparts]
    sds = jax.ShapeDtypeStruct((R, C), F32)
    return pl.pallas_call(
        body, name=name, out_shape=(sds, sds, sds, sds), grid=(R // tr,),
        in_specs=[blk, blk, blk] + g_specs, out_specs=(blk, blk, blk, blk),
        compiler_params=_cp(("parallel",)))(w, m, v, *gparts)


def _mod_shard(c_all, w_ada, b_ada_cols):
    n = w_ada.shape[1]
    tn = 512

    def body(c_ref, w_ref, b_ref, o_ref):
        cv = c_ref[...]
        ca = (cv * _sig(cv)).astype(BF16)
        o_ref[...] = jnp.dot(ca, w_ref[...].astype(BF16), preferred_element_type=F32) + b_ref[...]

    return pl.pallas_call(
        body, name="mod_shard", out_shape=jax.ShapeDtypeStruct((N_DEV, n), F32), grid=(n // tn,),
        in_specs=[_full((N_DEV, D_MODEL)), pl.BlockSpec((D_MODEL, tn), lambda j: (0, j)),
                  pl.BlockSpec((1, tn), lambda j: (0, j))],
        out_specs=pl.BlockSpec((N_DEV, tn), lambda j: (0, j)),
        compiler_params=_cp(("parallel",)))(c_all, w_ada, b_ada_cols)


def _ada_grad(c_all, dmod_cols):
    n = dmod_cols.shape[1]
    tn = 512

    def body(c_ref, d_ref, o_ref):
        cv = c_ref[...]
        ca = cv * _sig(cv)
        o_ref[...] = lax.dot_general(ca, d_ref[...], (((0,), (0,)), ((), ())),
                                     preferred_element_type=F32, precision=lax.Precision.HIGHEST)

    return pl.pallas_call(
        body, name="ada_grad", out_shape=jax.ShapeDtypeStruct((D_MODEL, n), F32), grid=(n // tn,),
        in_specs=[_full((N_DEV, D_MODEL)), pl.BlockSpec((N_DEV, tn), lambda j: (0, j))],
        out_specs=pl.BlockSpec((D_MODEL, tn), lambda j: (0, j)),
        compiler_params=_cp(("parallel",)))(c_all, dmod_cols)


def _device_step(x, mod, W, tgt, getw, put, early):
    sh1, sc1, g1, sh2, sc2, g2 = [mod[:, i * D_MODEL:(i + 1) * D_MODEL] for i in range(6)]
    e_re, e_im, bb_re, bb_im = _ssm_prep(W["ssm_a_re"], W["ssm_a_im"], W["ssm_b_re"], W["ssm_b_im"], W["ssm_log_dt"])
    bb, cm = _block_diag_mats(bb_re, bb_im, W["ssm_c_re"], W["ssm_c_im"])
    bb16, cm16 = bb.astype(BF16), cm.astype(BF16)
    bbt16, cmt16 = bb16.T, cm16.T
    tab_f = _scan_tables(e_re, e_im, False)
    tab_b = _scan_tables(e_re, e_im, True)

    w_in = getw("w_in", [mod, bb16, cm16, bbt16, cmt16, tab_f, tab_b])
    h1, z = _in_proj(x, W["norm1_g"], sc1, sh1, w_in)
    yc, scv = _conv_fwd(z, W["conv_w"], W["conv_b"], W["conv_ln_g"], W["conv_ln_b"])
    xs, ys, yg = _ssm_fwd(z, bb16, cm16, W["ssm_d"], tab_f)
    w_cp, w_glu, w_out = getw("conv_proj", scv), getw("ssm_glu", yg), getw("w_out", yg)
    y_conv, zz, merged, o, x2, h2 = _mix_fwd(scv, yg, z, x, w_cp, w_glu, w_out, g1, W["norm2_g"], sc2, sh2)
    w_fi = getw("w_ffn_in", h2)
    f, act = _ffn_in_act(h2, w_fi)
    w_fo = getw("w_ffn_out", act)
    dx3, do2, loss8, dfg8, dg2_8 = _ffn_out_final(x2, act, w_fo, g2, W["final_g"], tgt)

    sm = {}
    tok = put("w_ffn_out", _matmul(act, do2, "tn", 1408, 1024, 1024, BF16, "mm_g_ffn_out"))
    df = _ffn_bwd(do2, w_fo, f, tok)
    tok = put("w_ffn_in", _matmul(h2, df, "tn", 1024, 1408, 1024, BF16, "mm_g_ffn_in"))
    dx2, do, dsh2, dsc2, dn2, dg1_8 = _normmod_bwd(df, w_fi, x2, dx3, W["norm2_g"], sc2, g1, o, tok, "d_h2_normmod2_bwd")
    tok = put("w_out", _matmul(merged, do, "tn", 1024, 1024, 1024, BF16, "mm_g_w_out"))
    dyconv, dgl, dzz = _mix_bwd(do, w_out, z, zz, y_conv, tok)
    tok = put("ssm_glu", _matmul(yg, dzz, "tn", 512, 1024, 1024, BF16, "mm_g_ssm_glu"))
    tok = put("conv_proj", _matmul(scv, dyconv, "tn", 512, 1024, 1024, BF16, "mm_g_conv_proj", after=tok))
    lam, du, dys16, de16, dd8 = _ssm_bwd(dzz, w_glu, ys, z, xs, cmt16, bbt16, W["ssm_d"], tab_b, tok)
    dc_full = _matmul(dys16, xs, "tn", 512, 1024, 1024, F32, "mm_g_ssm_c")
    dbb_full = _matmul(z, lam, "tn", 512, 1024, 1024, F32, "mm_g_ssm_b", m_cols=(2 * CW, CW))
    dyc, dlg8, dlb8, dcb8 = _conv_bwd_ln(dyconv, w_cp, yc, W["conv_ln_g"], W["conv_ln_b"])
    dz_conv, dcw = _conv_bwd(dyc, z, W["conv_w"])

    s8 = lambda a: jnp.sum(a, axis=0, keepdims=True)
    de = de16.reshape(2, 8, NST).sum(1)
    de_re, de_im = de[0].reshape(G, P), de[1].reshape(G, P)
    dc_re = _diag_blocks(dc_full[:, :NST])
    dc_im = -_diag_blocks(dc_full[:, NST:])
    dbb_re = jnp.swapaxes(_diag_blocks(dbb_full[:, :NST]), 1, 2)
    dbb_im = jnp.swapaxes(_diag_blocks(dbb_full[:, NST:]), 1, 2)
    _, vjp = jax.vjp(_ssm_prep, W["ssm_a_re"], W["ssm_a_im"], W["ssm_b_re"], W["ssm_b_im"], W["ssm_log_dt"])
    sm["ssm_a_re"], sm["ssm_a_im"], sm["ssm_b_re"], sm["ssm_b_im"], sm["ssm_log_dt"] = vjp((de_re, de_im, dbb_re, dbb_im))
    sm["ssm_c_re"], sm["ssm_c_im"] = dc_re, dc_im
    sm["ssm_d"] = s8(dd8)
    sm["norm2_g"] = s8(dn2)
    sm["conv_b"], sm["conv_ln_g"], sm["conv_ln_b"] = s8(dcb8), s8(dlg8), s8(dlb8)
    sm["conv_w"] = dcw.reshape(KW, 8, CW).sum(1)
    sm["final_g"] = s8(dfg8)
    tok = early(sm)

    dz = jnp.concatenate([dz_conv, du, dgl], axis=1)
    tok = put("w_in", _matmul(h1, dz, "tn", 1024, 896, 1024, BF16, "mm_g_w_in", after=tok))
    dx, _, dsh1, dsc1, dn1, _ = _normmod_bwd(dz, w_in, x, dx2, W["norm1_g"], sc1, g1, o, tok, "d_h1_normmod1_bwd")
    dmod = jnp.concatenate([s8(dsh1), s8(dsc1), s8(dg1_8), s8(dsh2), s8(dsc2), s8(dg2_8)], axis=1)
    return loss8, dx, s8(dn1), dmod


_BIG = ("w_in", "conv_proj", "ssm_glu", "w_out", "w_ffn_in", "w_ffn_out")
_BIG_AXIS = {"w_in": 1, "conv_proj": 1, "ssm_glu": 1, "w_out": 0, "w_ffn_in": 1, "w_ffn_out": 0}
_EARLY = ("conv_w", "conv_b", "conv_ln_g", "conv_ln_b", "ssm_a_re", "ssm_a_im", "ssm_b_re", "ssm_b_im", "ssm_c_re",
          "ssm_c_im", "ssm_d", "ssm_log_dt", "norm2_g", "final_g")
_LATE = ("norm1_g", "b_ada")
_ORDER = ("w_ada", "b_ada", "norm1_g", "w_in", "conv_w", "conv_b", "conv_ln_g", "conv_ln_b", "conv_proj",
          "ssm_a_re", "ssm_a_im", "ssm_b_re", "ssm_b_im", "ssm_c_re", "ssm_c_im", "ssm_d", "ssm_log_dt", "ssm_glu",
          "w_out", "norm2_g", "w_ffn_in", "w_ffn_out", "final_g")
_PACK_COLS = 1024


def _pack_rows(shape):
    return -(-int(np.prod(shape)) // (8 * _PACK_COLS)) * 8


def _pack(arrs):
    parts = []
    for a in arrs:
        flat = a.reshape(-1)
        n = _pack_rows(a.shape)
        parts.append(jnp.pad(flat, (0, n * _PACK_COLS - flat.shape[0])).reshape(n, _PACK_COLS))
    return jnp.concatenate(parts, 0)


def _unpack(packed, shapes):
    out, r = [], 0
    for shp in shapes:
        size = int(np.prod(shp))
        n = _pack_rows(shp)
        out.append(packed[r:r + n].reshape(-1)[:size].reshape(shp))
        r += n
    return out


def kernel(x, c, w_ada, b_ada, norm1_g, w_in, conv_w, conv_b, conv_ln_g, conv_ln_b, conv_proj, ssm_a_re, ssm_a_im, ssm_b_re, ssm_b_im, ssm_c_re, ssm_c_im, ssm_d, ssm_log_dt, ssm_glu, w_out, norm2_g, w_ffn_in, w_ffn_out, final_g, loss_target, m_w_ada, m_b_ada, m_norm1_g, m_w_in, m_conv_w, m_conv_b, m_conv_ln_g, m_conv_ln_b, m_conv_proj, m_ssm_a_re, m_ssm_a_im, m_ssm_b_re, m_ssm_b_im, m_ssm_c_re, m_ssm_c_im, m_ssm_d, m_ssm_log_dt, m_ssm_glu, m_w_out, m_norm2_g, m_w_ffn_in, m_w_ffn_out, m_final_g, v_w_ada, v_b_ada, v_norm1_g, v_w_in, v_conv_w, v_conv_b, v_conv_ln_g, v_conv_ln_b, v_conv_proj, v_ssm_a_re, v_ssm_a_im, v_ssm_b_re, v_ssm_b_im, v_ssm_c_re, v_ssm_c_im, v_ssm_d, v_ssm_log_dt, v_ssm_glu, v_w_out, v_norm2_g, v_w_ffn_in, v_w_ffn_out, v_final_g):
    given = dict(locals())
    mx, my, mc = _me()
    chip = 2 * mx + my
    dev = 4 * mx + 2 * my + mc
    def canon(a):
        return a.reshape(1, -1) if a.ndim <= 2 else a[0]

    wts = {n: canon(given[n]) for n in _ORDER}
    mom = {n: canon(given["m_" + n]) for n in _ORDER}
    var = {n: canon(given["v_" + n]) for n in _ORDER}

    c_all = _allgather8(jnp.broadcast_to(c, (8, D_MODEL)), "gather_c")[:, 0, :]
    n_ada = wts["w_ada"].shape[1]
    b_cols = lax.dynamic_slice(wts["b_ada"], (0, chip * n_ada), (1, n_ada))
    mod_cols = _mod_shard(c_all, wts["w_ada"], b_cols)
    mods = _allgather8(mod_cols, "gather_mod")
    mod = jnp.concatenate([lax.dynamic_index_in_dim(mods[2 * q], dev, 0, keepdims=True) for q in range(N_CHIP)], axis=1)

    W = {n: wts[n] for n in _ORDER if n not in _BIG}
    conv_w_full = _allgather8(jnp.pad(wts["conv_w"], ((0, 1), (0, 0))), "gather_conv_w")
    W["conv_w"] = jnp.concatenate([conv_w_full[2 * q, :KW] for q in range(N_CHIP)], axis=1)

    axes = [_BIG_AXIS[n] for n in _BIG]
    shards = [wts[n].astype(BF16) for n in _BIG]
    lands = []
    for s, ax in zip(shards, axes):
        shp = list(s.shape)
        shp[ax] *= N_CHIP
        lands.append(lax.empty(tuple(shp), BF16))
    gstate, token = _gather_start(shards, lands, axes, mod + W["conv_w"][0:1, 0:1])
    gstate = dict(zip(_BIG, gstate))
    mod = mod + token[0:1, 0:1]

    def getw(n, after):
        return _gather_wait(gstate[n], _BIG_AXIS[n], after, "gather_wait_" + n)

    sstate, own, estate = {}, {}, []

    def put(n, g):
        ax = _BIG_AXIS[n]
        k = g.shape[ax] // N_CHIP
        own[n] = lax.dynamic_slice_in_dim(g, chip * k, k, axis=ax)
        sstate[n], tok = _scatter_start(g, ax, "scatter_start_" + n)
        return tok

    first5 = [n for n in _BIG if n != "w_in"]

    def early(sm):
        state, tok = _all8_start(_pack([sm[n] for n in _EARLY]), "small_start")
        estate.append(state)
        recv5 = [_scatter_wait(sstate[n], _BIG_AXIS[n], tok, "scatter_wait_" + n) for n in first5]
        held = [a for n, r in zip(first5, recv5) for a in (own[n], r)]
        state, tok = _swap_start(held, tok, "swap_start")
        estate.append(state)
        return tok

    loss8, dx, dn1, dmod = _device_step(x[0], mod, W, loss_target[0], getw, put, early)
    loss = lax.psum(jnp.sum(loss8), ("x", "y", "c"))

    late = _allgather8(_pack([dn1, dmod]), "gather_late")

    held_in = [own["w_in"], _scatter_wait(sstate["w_in"], _BIG_AXIS["w_in"], late, "scatter_wait_w_in")]
    sib_in = _swap_sibling(held_in)
    held5, sib5 = _swap_wait(estate[1], late, "swap_wait")
    allp = _all8_wait(estate[0], late, "small_wait")

    outs = {"w_in": _adamw(wts["w_in"], mom["w_in"], var["w_in"], [held_in, sib_in], "adamw_w_in")}
    for i, n in enumerate(first5):
        outs[n] = _adamw(wts[n], mom[n], var[n], [held5[2 * i:2 * i + 2], sib5[2 * i:2 * i + 2]], "adamw_" + n)

    r1 = _pack_rows((D_MODEL,))
    dmod_all = late[:, r1:, :].reshape(N_DEV, -1)[:, :6 * D_MODEL]
    dmod_cols = lax.dynamic_slice(dmod_all, (0, chip * n_ada), (N_DEV, n_ada))
    g_ada = _ada_grad(c_all, dmod_cols)
    outs["w_ada"] = _adamw(wts["w_ada"], mom["w_ada"], var["w_ada"], [[g_ada]], "adamw_w_ada")

    def packed_params(d, names):
        return _pack([jnp.zeros((KW, CW), F32) if n == "conv_w" else d[n] for n in names])

    for names, parts, nm in ((_EARLY, allp, "adamw_small"), (_LATE, late, "adamw_late")):
        res = _adamw(packed_params(wts, names), packed_params(mom, names), packed_params(var, names), [[parts]], nm)
        shapes = [(KW, CW) if n == "conv_w" else wts[n].shape for n in names]
        unpacked = [_unpack(r, shapes) for r in res]
        for idx, n in enumerate(names):
            outs[n] = tuple(unpacked[q][idx] for q in range(4))
    g_cw = lax.dynamic_slice(outs["conv_w"][0], (0, chip * (CW // N_CHIP)), (KW, CW // N_CHIP))
    pad = lambda a: jnp.pad(a, ((0, 1), (0, 0)))
    r_cw = _adamw(pad(wts["conv_w"]), pad(mom["conv_w"]), pad(var["conv_w"]), [[pad(g_cw)]], "adamw_conv_w")
    outs["conv_w"] = tuple(r[:KW] for r in r_cw)

    def shaped(n, a):
        return a.reshape(given[n].shape)

    result = [loss, dx[None]]
    for q in range(4):
        result += [shaped(n, outs[n][q]) for n in _ORDER]
    return tuple(result)
```

```python
import math

import jax
import jax.numpy as jnp
import numpy as np
from jax import lax
from jax.experimental import pallas as pl
from jax.experimental.pallas import tpu as pltpu

F32 = jnp.float32
BF16 = jnp.bfloat16
EPS = 1e-6
D_MODEL = 1024
CW = 512
KW = 31
HALO = 32
G, P, H = 32, 64, 16
NST = G * P
FH = 2816
N_DEV = 8
N_CHIP = 4
VMEM_LIMIT = 56 * 1024 * 1024
LR, B1, B2, AEPS, WD, STEP = 0.001, 0.9, 0.999, 1e-08, 0.01, 10
MESH = pl.DeviceIdType.MESH


def _cp(sem=None):
    return pltpu.CompilerParams(dimension_semantics=sem, vmem_limit_bytes=VMEM_LIMIT)


def _sig(x):
    return jax.nn.sigmoid(x)


def _full(shape):
    return pl.BlockSpec(shape, lambda *_: (0,) * len(shape))


def _colsum8(v):
    t, c = v.shape
    return jnp.sum(v.reshape(t // 8, 8, c), axis=0)


def _matmul(a, b, mode, tm, tn, tk, out_dtype, name, after=None, n_outer=False, m_cols=None):
    m0 = 0
    if mode == "nn":
        (M, K), N = a.shape, b.shape[1]
    elif mode == "nt":
        (M, K), N = a.shape, b.shape[0]
    else:
        (K, M), N = a.shape, b.shape[1]
        if m_cols is not None:
            m0, M = m_cols
    tm, tn, tk = min(tm, M), min(tn, N), min(tk, K)
    assert M % tm == 0 and N % tn == 0 and K % tk == 0 and m0 % tm == 0, (name, M, N, K, tm, tn, tk)
    nk = K // tk
    mb = m0 // tm

    def ij(fn):
        return (lambda j, i, k: fn(i, j, k)) if n_outer else fn

    if mode == "nn":
        a_spec = pl.BlockSpec((tm, tk), ij(lambda i, j, k: (i, k)))
        b_spec = pl.BlockSpec((tk, tn), ij(lambda i, j, k: (k, j)))
        dims = (((1,), (0,)), ((), ()))
    elif mode == "nt":
        a_spec = pl.BlockSpec((tm, tk), ij(lambda i, j, k: (i, k)))
        b_spec = pl.BlockSpec((tn, tk), ij(lambda i, j, k: (j, k)))
        dims = (((1,), (1,)), ((), ()))
    else:
        a_spec = pl.BlockSpec((tk, tm), ij(lambda i, j, k: (k, i + mb)))
        b_spec = pl.BlockSpec((tk, tn), ij(lambda i, j, k: (k, j)))
        dims = (((0,), (0,)), ((), ()))

    def body(a_ref, b_ref, *rest):
        o_ref, acc_ref = rest[-2:]
        k = pl.program_id(2)
        part = lax.dot_general(a_ref[...].astype(BF16), b_ref[...].astype(BF16), dims,
                               preferred_element_type=F32)
        if nk == 1:
            o_ref[...] = part.astype(out_dtype)
        else:
            @pl.when(k == 0)
            def _():
                acc_ref[...] = part

            @pl.when(k > 0)
            def _():
                acc_ref[...] += part

            @pl.when(k == nk - 1)
            def _():
                o_ref[...] = acc_ref[...].astype(out_dtype)

    return pl.pallas_call(
        body, name=name,
        out_shape=jax.ShapeDtypeStruct((M, N), out_dtype),
        grid=(N // tn, M // tm, nk) if n_outer else (M // tm, N // tn, nk),
        in_specs=[a_spec, b_spec] + ([] if after is None else [pl.BlockSpec(memory_space=pl.ANY)]),
        out_specs=pl.BlockSpec((tm, tn), ij(lambda i, j, k: (i, j))),
        scratch_shapes=[pltpu.VMEM((tm, tn) if nk > 1 else (8, 128), F32)],
        compiler_params=_cp(("parallel", "parallel", "arbitrary")),
    )(*((a, b) if after is None else (a, b, after)))


def _row_tile(S):
    return min(512, S)


def _in_proj(x, g, sc, sh, w_in):
    S, D = x.shape
    N = w_in.shape[1]
    tm = min(256, S)

    def body(x_ref, g_ref, sc_ref, sh_ref, w_ref, h_ref, z_ref):
        xv = x_ref[...]
        r = lax.rsqrt(jnp.mean(xv * xv, axis=-1, keepdims=True) + EPS)
        h = (xv * r * (g_ref[...] * (1.0 + sc_ref[...])) + sh_ref[...]).astype(BF16)
        h_ref[...] = h
        z_ref[...] = jnp.dot(h, w_ref[...], preferred_element_type=F32)

    row = pl.BlockSpec((tm, D), lambda i: (i, 0))
    par = _full((1, D))
    return pl.pallas_call(
        body, name="in_proj",
        out_shape=(jax.ShapeDtypeStruct((S, D), BF16), jax.ShapeDtypeStruct((S, N), F32)), grid=(S // tm,),
        in_specs=[row, par, par, par, _full((D, N))], out_specs=(row, pl.BlockSpec((tm, N), lambda i: (i, 0))),
        compiler_params=_cp(("parallel",)))(x, g, sc, sh, w_in)


def _conv_fwd(z, conv_w, conv_b, ln_g, ln_b):
    S = z.shape[0]
    tm = min(128, S)
    sub = 32
    hb = tm // HALO

    def body(a_ref, g_ref, ha_ref, hg_ref, w_ref, b_ref, lg_ref, lb_ref, yc_ref, s_ref, ug_ref):
        i = pl.program_id(0)
        halo = ha_ref[...] * _sig(hg_ref[...])
        ug_ref[0:HALO, :] = jnp.where(i == 0, 0.0, halo)
        ug_ref[HALO:, :] = a_ref[...] * _sig(g_ref[...])
        for rb in range(tm // sub):
            acc = jnp.zeros((sub, CW), F32) + b_ref[...]
            for k in range(KW):
                off = rb * sub + HALO - (KW - 1) + k
                acc = acc + w_ref[k:k + 1, :] * ug_ref[off:off + sub, :]
            yc_ref[rb * sub:(rb + 1) * sub, :] = acc
            mu = jnp.mean(acc, axis=-1, keepdims=True)
            cen = acc - mu
            rstd = lax.rsqrt(jnp.mean(cen * cen, axis=-1, keepdims=True) + EPS)
            ln = cen * rstd * lg_ref[...] + lb_ref[...]
            s_ref[rb * sub:(rb + 1) * sub, :] = (ln * _sig(ln)).astype(BF16)

    prev = lambda i: (jnp.maximum(i * hb - 1, 0), 0)
    return pl.pallas_call(
        body, name="conv_fwd",
        out_shape=(jax.ShapeDtypeStruct((S, CW), F32), jax.ShapeDtypeStruct((S, CW), BF16)),
        grid=(S // tm,),
        in_specs=[pl.BlockSpec((tm, CW), lambda i: (i, 0)), pl.BlockSpec((tm, CW), lambda i: (i, 1)),
                  pl.BlockSpec((HALO, CW), prev), pl.BlockSpec((HALO, CW), lambda i: (jnp.maximum(i * hb - 1, 0), 1)),
                  _full((KW, CW)), _full((1, CW)), _full((1, CW)), _full((1, CW))],
        out_specs=(pl.BlockSpec((tm, CW), lambda i: (i, 0)), pl.BlockSpec((tm, CW), lambda i: (i, 0))),
        scratch_shapes=[pltpu.VMEM((tm + HALO, CW), F32)],
        compiler_params=_cp(("parallel",)))(z, z, z, z, conv_w, conv_b, ln_g, ln_b)


def _conv_bwd_ln(dyconv, w_cp, yc, ln_g, ln_b):
    S = yc.shape[0]
    tm = _row_tile(S)

    def body(dy_ref, w_ref, yc_ref, lg_ref, lb_ref, dyc_ref, dlg_ref, dlb_ref, dcb_ref):
        i = pl.program_id(0)
        dsc = lax.dot_general(dy_ref[...], w_ref[...], (((1,), (1,)), ((), ())), preferred_element_type=F32)
        yc_v = yc_ref[...]
        mu = jnp.mean(yc_v, axis=-1, keepdims=True)
        cen = yc_v - mu
        rstd = lax.rsqrt(jnp.mean(cen * cen, axis=-1, keepdims=True) + EPS)
        yn = cen * rstd
        ln = yn * lg_ref[...] + lb_ref[...]
        sl = _sig(ln)
        dln = dsc * (sl * (1.0 + ln * (1.0 - sl)))
        dyn = dln * lg_ref[...]
        dyc = rstd * (dyn - jnp.mean(dyn, axis=-1, keepdims=True)
                      - yn * jnp.mean(dyn * yn, axis=-1, keepdims=True))
        dyc_ref[...] = dyc

        @pl.when(i == 0)
        def _():
            dlg_ref[...] = jnp.zeros_like(dlg_ref)
            dlb_ref[...] = jnp.zeros_like(dlb_ref)
            dcb_ref[...] = jnp.zeros_like(dcb_ref)

        dlg_ref[...] += _colsum8(dln * yn)
        dlb_ref[...] += _colsum8(dln)
        dcb_ref[...] += _colsum8(dyc)

    row = pl.BlockSpec((tm, CW), lambda i: (i, 0))
    acc = jax.ShapeDtypeStruct((8, CW), F32)
    return pl.pallas_call(
        body, name="conv_bwd_ln",
        out_shape=(jax.ShapeDtypeStruct((S, CW), F32), acc, acc, acc), grid=(S // tm,),
        in_specs=[pl.BlockSpec((tm, D_MODEL), lambda i: (i, 0)), _full((CW, D_MODEL)), row, _full((1, CW)),
                  _full((1, CW))],
        out_specs=(row, _full((8, CW)), _full((8, CW)), _full((8, CW))),
        compiler_params=_cp(("arbitrary",)))(dyconv, w_cp, yc, ln_g, ln_b)


def _conv_bwd(dyc, z, conv_w):
    S = z.shape[0]
    tm = min(128, S)
    sub = 32
    hb = tm // HALO
    nt = S // tm

    def body(d_ref, dn_ref, a_ref, g_ref, ha_ref, hg_ref, w_ref, dz_ref, dw_ref, ug_ref, dy_ref):
        i = pl.program_id(0)
        halo = ha_ref[...] * _sig(hg_ref[...])
        ug_ref[0:HALO, :] = jnp.where(i == 0, 0.0, halo)
        a = a_ref[...]
        sg = _sig(g_ref[...])
        ug_ref[HALO:, :] = a * sg
        dy_ref[0:tm, :] = d_ref[...]
        dy_ref[tm:, :] = jnp.where(i == nt - 1, 0.0, dn_ref[...])

        @pl.when(i == 0)
        def _():
            dw_ref[...] = jnp.zeros_like(dw_ref)

        for rb in range(tm // sub):
            r0 = rb * sub
            acc = jnp.zeros((sub, CW), F32)
            dyc_b = dy_ref[r0:r0 + sub, :]
            for k in range(KW):
                up = r0 + (KW - 1) - k
                acc = acc + w_ref[k:k + 1, :] * dy_ref[up:up + sub, :]
                off = r0 + HALO - (KW - 1) + k
                dw_ref[k * 8:(k + 1) * 8, :] += _colsum8(dyc_b * ug_ref[off:off + sub, :])
            a_b = a[r0:r0 + sub, :]
            sg_b = sg[r0:r0 + sub, :]
            dz_ref[r0:r0 + sub, 0:CW] = (acc * sg_b).astype(BF16)
            dz_ref[r0:r0 + sub, CW:2 * CW] = (acc * a_b * sg_b * (1.0 - sg_b)).astype(BF16)

    return pl.pallas_call(
        body, name="conv_bwd",
        out_shape=(jax.ShapeDtypeStruct((S, 2 * CW), BF16), jax.ShapeDtypeStruct((KW * 8, CW), F32)),
        grid=(nt,),
        in_specs=[pl.BlockSpec((tm, CW), lambda i: (i, 0)),
                  pl.BlockSpec((HALO, CW), lambda i: (jnp.minimum((i + 1) * hb, nt * hb - 1), 0)),
                  pl.BlockSpec((tm, CW), lambda i: (i, 0)), pl.BlockSpec((tm, CW), lambda i: (i, 1)),
                  pl.BlockSpec((HALO, CW), lambda i: (jnp.maximum(i * hb - 1, 0), 0)),
                  pl.BlockSpec((HALO, CW), lambda i: (jnp.maximum(i * hb - 1, 0), 1)),
                  _full((KW, CW))],
        out_specs=(pl.BlockSpec((tm, 2 * CW), lambda i: (i, 0)), _full((KW * 8, CW))),
        scratch_shapes=[pltpu.VMEM((tm + HALO, CW), F32), pltpu.VMEM((tm + HALO, CW), F32)],
        compiler_params=_cp(("arbitrary",)))(dyc, dyc, z, z, z, z, conv_w)


_GELU_C = math.sqrt(2.0 / math.pi)


def _gelu(x):
    return 0.5 * x * (1.0 + jnp.tanh(_GELU_C * (x + 0.044715 * x * x * x)))


def _gelu_grad(x):
    t = jnp.tanh(_GELU_C * (x + 0.044715 * x * x * x))
    return 0.5 * (1.0 + t) + 0.5 * x * (1.0 - t * t) * (_GELU_C * (1.0 + 3 * 0.044715 * x * x))


_LW = 512


def _ssm_fwd(z, bb, cm, d, tab):
    S = z.shape[0]
    tm = min(256, S)

    def body(u_ref, bb_ref, cm_ref, d_ref, t_ref, x_ref, ys_ref, yg_ref, car_ref):
        i = pl.program_id(0)

        @pl.when(i == 0)
        def _():
            car_ref[...] = jnp.zeros_like(car_ref)

        u = u_ref[...]
        x_ref[...] = jnp.dot(u.astype(BF16), bb_ref[...], preferred_element_type=F32)
        for c in range(NST // _LW):
            lre = pl.ds(c * _LW, _LW)
            lim = pl.ds(NST + c * _LW, _LW)

            def blk(j, car):
                cr, ci = car
                rows = pl.ds(pl.multiple_of(j * 8, 8), 8)
                r = x_ref[rows, lre]
                im = x_ref[rows, lim]
                for lvl, s in enumerate((1, 2, 4)):
                    mr = t_ref[16 * lvl:16 * lvl + 8, lre]
                    mi = t_ref[16 * lvl + 8:16 * lvl + 16, lre]
                    sr = pltpu.roll(r, s, 0)
                    si = pltpu.roll(im, s, 0)
                    r, im = r + (mr * sr - mi * si), im + (mr * si + mi * sr)
                pr = t_ref[48:56, lre]
                pi_ = t_ref[56:64, lre]
                r, im = r + (pr * cr - pi_ * ci), im + (pr * ci + pi_ * cr)
                x_ref[rows, lre] = r
                x_ref[rows, lim] = im
                return (jnp.broadcast_to(r[7:8, :], (8, _LW)), jnp.broadcast_to(im[7:8, :], (8, _LW)))

            cr, ci = lax.fori_loop(0, tm // 8, blk, (car_ref[:, lre], car_ref[:, lim]))
            car_ref[:, lre] = cr
            car_ref[:, lim] = ci
        ys = jnp.dot(x_ref[...].astype(BF16), cm_ref[...], preferred_element_type=F32) + d_ref[...] * u
        ys_ref[...] = ys
        yg_ref[...] = _gelu(ys).astype(BF16)

    return pl.pallas_call(
        body, name="ssm_fwd",
        out_shape=(jax.ShapeDtypeStruct((S, 2 * NST), F32), jax.ShapeDtypeStruct((S, CW), F32),
                   jax.ShapeDtypeStruct((S, CW), BF16)),
        grid=(S // tm,),
        in_specs=[pl.BlockSpec((tm, CW), lambda i: (i, 2)), _full((CW, 2 * NST)), _full((2 * NST, CW)),
                  _full((1, CW)), _full((64, NST))],
        out_specs=(pl.BlockSpec((tm, 2 * NST), lambda i: (i, 0)), pl.BlockSpec((tm, CW), lambda i: (i, 0)),
                   pl.BlockSpec((tm, CW), lambda i: (i, 0))),
        scratch_shapes=[pltpu.VMEM((8, 2 * NST), F32)],
        compiler_params=_cp(("arbitrary",)))(z, bb, cm, d, tab)


def _ssm_bwd(dzz, w_glu, ys, z, xs, cmt, bbt, d, tab, after):
    S = z.shape[0]
    tm = min(256, S)
    nt = S // tm
    tdims = (((0,), (0,)), ((), ()))

    def body(dzz_ref, wglu_ref, ys_ref, u_ref, x_ref, cmt_ref, bbt_ref, d_ref, t_ref, after_ref,
             du_ref, de_ref, dd_ref, dc_hbm, dbb_hbm, car_ref, lam_ref, dc_ref, dbb_ref):
        i = pl.program_id(0)

        @pl.when(i == 0)
        def _():
            car_ref[...] = jnp.zeros_like(car_ref)
            de_ref[...] = jnp.zeros_like(de_ref)
            dd_ref[...] = jnp.zeros_like(dd_ref)
            dc_ref[...] = jnp.zeros_like(dc_ref)
            dbb_ref[...] = jnp.zeros_like(dbb_ref)

        u = u_ref[...]
        dyg = lax.dot_general(dzz_ref[...], wglu_ref[...], (((1,), (1,)), ((), ())), preferred_element_type=F32)
        dys = dyg * _gelu_grad(ys_ref[...])
        dys16 = dys.astype(BF16)
        dd_ref[...] += _colsum8(dys * u)
        dc_ref[...] += lax.dot_general(dys16, x_ref[...].astype(BF16), tdims, preferred_element_type=F32)
        lam_ref[...] = jnp.dot(dys16, cmt_ref[...], preferred_element_type=F32)
        row = lax.broadcasted_iota(jnp.int32, (8, _LW), 0)
        for c in range(NST // _LW):
            lre = pl.ds(c * _LW, _LW)
            lim = pl.ds(NST + c * _LW, _LW)

            def blk(jj, car):
                cr, ci, ar, ai = car
                j = tm // 8 - 1 - jj
                rows = pl.ds(pl.multiple_of(j * 8, 8), 8)
                r = lam_ref[rows, lre]
                im = lam_ref[rows, lim]
                for lvl, s in enumerate((1, 2, 4)):
                    mr = t_ref[16 * lvl:16 * lvl + 8, lre]
                    mi = t_ref[16 * lvl + 8:16 * lvl + 16, lre]
                    sr = pltpu.roll(r, 8 - s, 0)
                    si = pltpu.roll(im, 8 - s, 0)
                    r, im = r + (mr * sr - mi * si), im + (mr * si + mi * sr)
                pr = t_ref[48:56, lre]
                pi_ = t_ref[56:64, lre]
                r, im = r + (pr * cr - pi_ * ci), im + (pr * ci + pi_ * cr)
                lam_ref[rows, lre] = r
                lam_ref[rows, lim] = im
                nr = jnp.where(row == 7, cr, pltpu.roll(r, 7, 0))
                ni = jnp.where(row == 7, ci, pltpu.roll(im, 7, 0))
                xr = x_ref[rows, lre]
                xi = x_ref[rows, lim]
                ar = ar + (nr * xr + ni * xi)
                ai = ai + (ni * xr - nr * xi)
                return (jnp.broadcast_to(r[0:1, :], (8, _LW)), jnp.broadcast_to(im[0:1, :], (8, _LW)), ar, ai)

            zero = jnp.zeros((8, _LW), F32)
            cr, ci, ar, ai = lax.fori_loop(0, tm // 8, blk, (car_ref[:, lre], car_ref[:, lim], zero, zero))
            car_ref[:, lre] = cr
            car_ref[:, lim] = ci
            de_ref[0:8, lre] += ar
            de_ref[8:16, lre] += ai
        lam16 = lam_ref[...].astype(BF16)
        dbb_ref[...] += lax.dot_general(u.astype(BF16), lam16, tdims, preferred_element_type=F32)
        du = jnp.dot(lam16, bbt_ref[...], preferred_element_type=F32) + dys * d_ref[...]
        du_ref[...] = du.astype(BF16)

        @pl.when(i == nt - 1)
        def _():
            pltpu.sync_copy(dc_ref, dc_hbm)
            pltpu.sync_copy(dbb_ref, dbb_hbm)

    rev = lambda i: (nt - 1 - i, 0)
    once = lambda shape: pl.BlockSpec(shape, lambda *_: (0,) * len(shape), pipeline_mode=pl.Buffered(1))
    cross = jax.ShapeDtypeStruct((CW, 2 * NST), F32)
    return pl.pallas_call(
        body, name="ssm_bwd",
        out_shape=(jax.ShapeDtypeStruct((S, CW), BF16), jax.ShapeDtypeStruct((16, NST), F32),
                   jax.ShapeDtypeStruct((8, CW), F32), cross, cross),
        grid=(nt,),
        in_specs=[pl.BlockSpec((tm, 2 * D_MODEL), rev), once((CW, 2 * D_MODEL)), pl.BlockSpec((tm, CW), rev),
                  pl.BlockSpec((tm, CW), lambda i: (nt - 1 - i, 2)), pl.BlockSpec((tm, 2 * NST), rev),
                  once((CW, 2 * NST)), once((2 * NST, CW)), _full((1, CW)), once((64, NST)),
                  pl.BlockSpec(memory_space=pl.ANY)],
        out_specs=(pl.BlockSpec((tm, CW), rev), _full((16, NST)), _full((8, CW)),
                   pl.BlockSpec(memory_space=pl.ANY), pl.BlockSpec(memory_space=pl.ANY)),
        scratch_shapes=[pltpu.VMEM((8, 2 * NST), F32), pltpu.VMEM((tm, 2 * NST), F32),
                        pltpu.VMEM((CW, 2 * NST), F32), pltpu.VMEM((CW, 2 * NST), F32)],
        compiler_params=_cp(("arbitrary",)))(dzz, w_glu, ys, z, xs, cmt, bbt, d, tab, after)


def _ssm_prep(a_re, a_im, b_re, b_im, log_dt):
    dt = jnp.exp(log_dt.reshape(G))[:, None]
    mag = jnp.exp(dt * a_re)
    e_re, e_im = mag * jnp.cos(dt * a_im), mag * jnp.sin(dt * a_im)
    n_re, n_im = e_re - 1.0, e_im
    den = a_re * a_re + a_im * a_im
    q_re = (n_re * a_re + n_im * a_im) / den
    q_im = (n_im * a_re - n_re * a_im) / den
    bb_re = q_re[..., None] * b_re - q_im[..., None] * b_im
    bb_im = q_re[..., None] * b_im + q_im[..., None] * b_re
    return e_re, e_im, bb_re, bb_im


def _scan_tables(e_re, e_im, reverse):
    er = e_re.reshape(1, NST)
    ei = e_im.reshape(1, NST)
    if reverse:
        ei = -ei
    pows = [(er, ei)]
    for _ in range(7):
        pr, pi_ = pows[-1]
        pows.append((pr * er - pi_ * ei, pr * ei + pi_ * er))
    row = jnp.arange(8)[:, None]
    out = []
    for s in (1, 2, 4):
        pr, pi_ = pows[s - 1]
        keep = (row + s <= 7) if reverse else (row >= s)
        out += [jnp.where(keep, pr, 0.0), jnp.where(keep, pi_, 0.0)]
    allr = jnp.concatenate([p[0] for p in pows], 0)
    alli = jnp.concatenate([p[1] for p in pows], 0)
    if reverse:
        allr, alli = allr[::-1], alli[::-1]
    out += [allr, alli]
    return jnp.concatenate(out, 0).astype(F32)


def _block_diag_mats(bb_re, bb_im, c_re, c_im):
    eye = jnp.eye(G, dtype=F32)
    bre = jnp.einsum("gph,gk->ghkp", bb_re, eye).reshape(CW, NST)
    bim = jnp.einsum("gph,gk->ghkp", bb_im, eye).reshape(CW, NST)
    bb = jnp.concatenate([bre, bim], 1)
    cre = jnp.einsum("ghp,gk->gpkh", c_re, eye).reshape(NST, CW)
    cim = jnp.einsum("ghp,gk->gpkh", c_im, eye).reshape(NST, CW)
    cm = jnp.concatenate([cre, -cim], 0)
    return bb, cm


def _diag_blocks(full):
    return jnp.stack([full[H * g:H * (g + 1), P * g:P * (g + 1)] for g in range(G)])


def _mix_fwd(scv, yg, z, x, w_cp, w_glu, w_out, g1, n2g, sc2, sh2):
    S = z.shape[0]
    tm = min(256, S)
    D = D_MODEL

    def body(s_ref, yg_ref, glc0_ref, glc1_ref, gls0_ref, gls1_ref, x_ref, wcp_ref, wglu_ref, wout_ref,
             g1_ref, n2_ref, sc_ref, sh_ref, yc_ref, zz_ref, m_ref, o_ref, x2_ref, h2_ref):
        y_conv = jnp.dot(s_ref[...], wcp_ref[...], preferred_element_type=F32)
        zz = jnp.dot(yg_ref[...], wglu_ref[...], preferred_element_type=F32)
        yc_ref[...] = y_conv.astype(BF16)
        zz_ref[...] = zz.astype(BF16)
        for half, (glc_ref, gls_ref) in enumerate(((glc0_ref, gls0_ref), (glc1_ref, gls1_ref))):
            lo, hi = half * CW, (half + 1) * CW
            y_ssm = zz[:, lo:hi] * _sig(zz[:, D + lo:D + hi])
            m_ref[:, lo:hi] = (_sig(glc_ref[...]) * y_conv[:, lo:hi] + _sig(gls_ref[...]) * y_ssm).astype(BF16)
        o = jnp.dot(m_ref[...], wout_ref[...], preferred_element_type=F32)
        o_ref[...] = o.astype(BF16)
        xv = x_ref[...] + g1_ref[...] * o
        x2_ref[...] = xv
        r = lax.rsqrt(jnp.mean(xv * xv, axis=-1, keepdims=True) + EPS)
        h2_ref[...] = (xv * r * (n2_ref[...] * (1.0 + sc_ref[...])) + sh_ref[...]).astype(BF16)

    zb_ = lambda j: pl.BlockSpec((tm, CW), lambda i: (i, j))
    row = lambda w: pl.BlockSpec((tm, w), lambda i: (i, 0))
    par = _full((1, D))
    bf = lambda w: jax.ShapeDtypeStruct((S, w), BF16)
    return pl.pallas_call(
        body, name="mix_fwd",
        out_shape=(bf(D), bf(2 * D), bf(D), bf(D), jax.ShapeDtypeStruct((S, D), F32), bf(D)),
        grid=(S // tm,),
        in_specs=[row(CW), row(CW), zb_(3), zb_(4), zb_(5), zb_(6), row(D), _full((CW, D)), _full((CW, 2 * D)),
                  _full((D, D)), par, par, par, par],
        out_specs=(row(D), row(2 * D), row(D), row(D), row(D), row(D)),
        compiler_params=_cp(("parallel",)))(scv, yg, z, z, z, z, x, w_cp, w_glu, w_out, g1, n2g, sc2, sh2)


def _mix_bwd(do, w_out, z, zz, y_conv, after):
    S = z.shape[0]
    tm = min(256, S)
    D = D_MODEL

    def body(do_ref, w_ref, glc0_ref, glc1_ref, gls0_ref, gls1_ref, za_ref, zb_ref, yc_ref, after_ref,
             dyc_ref, dgl_ref, dzz_ref):
        dm = lax.dot_general(do_ref[...], w_ref[...], (((1,), (1,)), ((), ())), preferred_element_type=F32)
        for half, (glc_ref, gls_ref) in enumerate(((glc0_ref, gls0_ref), (glc1_ref, gls1_ref))):
            lo, hi = half * CW, (half + 1) * CW
            dm_v = dm[:, lo:hi]
            sgc = _sig(glc_ref[...])
            sgs = _sig(gls_ref[...])
            szb = _sig(zb_ref[:, lo:hi].astype(F32))
            za = za_ref[:, lo:hi].astype(F32)
            dyc_ref[:, lo:hi] = (dm_v * sgc).astype(BF16)
            dgl_ref[:, lo:hi] = (dm_v * yc_ref[:, lo:hi].astype(F32) * sgc * (1.0 - sgc)).astype(BF16)
            dys = dm_v * sgs
            dgl_ref[:, D + lo:D + hi] = (dys * (za * szb) * (1.0 - sgs)).astype(BF16)
            dzz_ref[:, lo:hi] = (dys * szb).astype(BF16)
            dzz_ref[:, D + lo:D + hi] = (dys * za * szb * (1.0 - szb)).astype(BF16)

    zb_ = lambda j: pl.BlockSpec((tm, CW), lambda i: (i, j))
    wide = lambda j: pl.BlockSpec((tm, D), lambda i: (i, j))
    return pl.pallas_call(
        body, name="mix_bwd",
        out_shape=(jax.ShapeDtypeStruct((S, D), BF16), jax.ShapeDtypeStruct((S, 2 * D), BF16),
                   jax.ShapeDtypeStruct((S, 2 * D), BF16)),
        grid=(S // tm,),
        in_specs=[wide(0), _full((D, D)), zb_(3), zb_(4), zb_(5), zb_(6), wide(0), wide(1), wide(0),
                  pl.BlockSpec(memory_space=pl.ANY)],
        out_specs=(wide(0), pl.BlockSpec((tm, 2 * D), lambda i: (i, 0)), pl.BlockSpec((tm, 2 * D), lambda i: (i, 0))),
        compiler_params=_cp(("parallel",)))(do, w_out, z, z, z, z, zz, zz, y_conv, after)


_FC = 1408


def _ffn_in_act(h2, w_fi):
    S, D = h2.shape
    tm = min(256, S)

    def body(h_ref, w_ref, f_ref, a_ref):
        hv = h_ref[...]
        for c in range(FH // _FC):
            lo, hi = c * _FC, (c + 1) * _FC
            g = jnp.dot(hv, w_ref[:, lo:hi], preferred_element_type=F32)
            u = jnp.dot(hv, w_ref[:, FH + lo:FH + hi], preferred_element_type=F32)
            f_ref[:, lo:hi] = g.astype(BF16)
            f_ref[:, FH + lo:FH + hi] = u.astype(BF16)
            a_ref[:, lo:hi] = (g * _sig(g) * u).astype(BF16)

    return pl.pallas_call(
        body, name="ffn_in_act",
        out_shape=(jax.ShapeDtypeStruct((S, 2 * FH), BF16), jax.ShapeDtypeStruct((S, FH), BF16)),
        grid=(S // tm,),
        in_specs=[pl.BlockSpec((tm, D), lambda i: (i, 0)), _full((D, 2 * FH))],
        out_specs=(pl.BlockSpec((tm, 2 * FH), lambda i: (i, 0)), pl.BlockSpec((tm, FH), lambda i: (i, 0))),
        compiler_params=_cp(("parallel",)))(h2, w_fi)


def _ffn_bwd(do2, w_fo, f, after):
    S, D = do2.shape
    tm = min(256, S)

    def body(d_ref, w_ref, f_ref, after_ref, df_ref):
        dv = d_ref[...]
        for c in range(FH // _FC):
            lo, hi = c * _FC, (c + 1) * _FC
            dact = lax.dot_general(dv, w_ref[lo:hi, :], (((1,), (1,)), ((), ())), preferred_element_type=F32)
            g = f_ref[:, lo:hi].astype(F32)
            u = f_ref[:, FH + lo:FH + hi].astype(F32)
            sg = _sig(g)
            df_ref[:, lo:hi] = (dact * u * (sg * (1.0 + g * (1.0 - sg)))).astype(BF16)
            df_ref[:, FH + lo:FH + hi] = (dact * g * sg).astype(BF16)

    return pl.pallas_call(
        body, name="ffn_bwd", out_shape=jax.ShapeDtypeStruct((S, 2 * FH), BF16), grid=(S // tm,),
        in_specs=[pl.BlockSpec((tm, D), lambda i: (i, 0)), _full((FH, D)),
                  pl.BlockSpec((tm, 2 * FH), lambda i: (i, 0)), pl.BlockSpec(memory_space=pl.ANY)],
        out_specs=pl.BlockSpec((tm, 2 * FH), lambda i: (i, 0)),
        compiler_params=_cp(("parallel",)))(do2, w_fo, f, after)


def _ffn_out_final(x2, act, w_fo, g2, fg, tgt):
    S, D = x2.shape
    tm = min(256, S)

    def body(x2_ref, a_ref, w_ref, g2_ref, fg_ref, t_ref, dx3_ref, do2_ref, ls_ref, dfg_ref, dg2_ref):
        i = pl.program_id(0)
        o2 = jnp.dot(a_ref[...], w_ref[...], preferred_element_type=F32)
        x3 = x2_ref[...] + g2_ref[...] * o2
        r = lax.rsqrt(jnp.mean(x3 * x3, axis=-1, keepdims=True) + EPS)
        xn = x3 * r
        err = xn * fg_ref[...] - t_ref[...]
        dy = err * (1.0 / D)
        dxn = dy * fg_ref[...]
        dx3 = r * (dxn - xn * jnp.mean(dxn * xn, axis=-1, keepdims=True))
        dx3_ref[...] = dx3
        do2_ref[...] = (dx3 * g2_ref[...]).astype(BF16)

        @pl.when(i == 0)
        def _():
            ls_ref[...] = jnp.zeros_like(ls_ref)
            dfg_ref[...] = jnp.zeros_like(dfg_ref)
            dg2_ref[...] = jnp.zeros_like(dg2_ref)

        e2 = _colsum8(err * err)
        lanes = e2[:, 0:128]
        for q in range(1, D // 128):
            lanes = lanes + e2[:, q * 128:(q + 1) * 128]
        ls_ref[...] += lanes * (0.5 / D)
        dfg_ref[...] += _colsum8(dy * xn)
        dg2_ref[...] += _colsum8(dx3 * o2)

    row = pl.BlockSpec((tm, D), lambda i: (i, 0))
    par = _full((1, D))
    return pl.pallas_call(
        body, name="final_loss",
        out_shape=(jax.ShapeDtypeStruct((S, D), F32), jax.ShapeDtypeStruct((S, D), BF16),
                   jax.ShapeDtypeStruct((8, 128), F32), jax.ShapeDtypeStruct((8, D), F32),
                   jax.ShapeDtypeStruct((8, D), F32)),
        grid=(S // tm,), in_specs=[row, pl.BlockSpec((tm, FH), lambda i: (i, 0)), _full((FH, D)), par, par, row],
        out_specs=(row, row, _full((8, 128)), _full((8, D)), _full((8, D))),
        compiler_params=_cp(("arbitrary",)))(x2, act, w_fo, g2, fg, tgt)


def _normmod_bwd(dsrc, w, xin, dres, g, sc, gate, o, after, name):
    S, D = xin.shape
    K = dsrc.shape[1]
    tm = min(256, S)

    def body(ds_ref, w_ref, x_ref, dr_ref, g_ref, sc_ref, gate_ref, o_ref, after_ref,
             dx_ref, do_ref, dsh_ref, dsc_ref, dg_ref, dgate_ref):
        i = pl.program_id(0)
        xv = x_ref[...]
        r = lax.rsqrt(jnp.mean(xv * xv, axis=-1, keepdims=True) + EPS)
        xn = xv * r
        dh_v = lax.dot_general(ds_ref[...], w_ref[...], (((1,), (1,)), ((), ())), preferred_element_type=F32)
        gv = g_ref[...]
        scale = 1.0 + sc_ref[...]
        dxn = dh_v * (gv * scale)
        dx = dr_ref[...] + r * (dxn - xn * jnp.mean(dxn * xn, axis=-1, keepdims=True))
        dx_ref[...] = dx
        do_ref[...] = (dx * gate_ref[...]).astype(BF16)

        @pl.when(i == 0)
        def _():
            dsh_ref[...] = jnp.zeros_like(dsh_ref)
            dsc_ref[...] = jnp.zeros_like(dsc_ref)
            dg_ref[...] = jnp.zeros_like(dg_ref)
            dgate_ref[...] = jnp.zeros_like(dgate_ref)

        hx = dh_v * xn
        dsh_ref[...] += _colsum8(dh_v)
        dsc_ref[...] += _colsum8(hx) * gv
        dg_ref[...] += _colsum8(hx) * scale
        dgate_ref[...] += _colsum8(dx * o_ref[...])

    row = pl.BlockSpec((tm, D), lambda i: (i, 0))
    par = _full((1, D))
    acc = jax.ShapeDtypeStruct((8, D), F32)
    return pl.pallas_call(
        body, name=name,
        out_shape=(jax.ShapeDtypeStruct((S, D), F32), jax.ShapeDtypeStruct((S, D), BF16), acc, acc, acc, acc),
        grid=(S // tm,),
        in_specs=[pl.BlockSpec((tm, K), lambda i: (i, 0)), _full((D, K)), row, row, par, par, par, row,
                  pl.BlockSpec(memory_space=pl.ANY)],
        out_specs=(row, row, _full((8, D)), _full((8, D)), _full((8, D)), _full((8, D))),
        compiler_params=_cp(("arbitrary",)))(dsrc, w, xin, dres, g, sc, gate, o, after)


def _me():
    return lax.axis_index("x"), lax.axis_index("y"), lax.axis_index("c")


def _allgather8(v, name, after=()):
    R, C = v.shape
    after = list(after)

    def body(v_ref, *rest):
        out_ref, send_sems, recv_sems, local_sem = rest[len(after):]
        x, y, c = _me()
        mine = pltpu.make_async_copy(v_ref, out_ref.at[4 * x + 2 * y + c], local_sem)
        mine.start()
        copies = []
        for k in range(1, N_DEV):
            fx, fy, fc = (k >> 2) & 1, (k >> 1) & 1, k & 1
            peer = (x ^ fx, y ^ fy, c ^ fc)
            copies.append(pltpu.make_async_remote_copy(
                src_ref=v_ref, dst_ref=out_ref.at[4 * x + 2 * y + c],
                send_sem=send_sems.at[k - 1], recv_sem=recv_sems.at[k - 1],
                device_id=peer, device_id_type=MESH))
        for cp in copies:
            cp.start()
        for k in range(1, N_DEV):
            fx, fy, fc = (k >> 2) & 1, (k >> 1) & 1, k & 1
            src_slot = 4 * (x ^ fx) + 2 * (y ^ fy) + (c ^ fc)
            pltpu.make_async_remote_copy(
                src_ref=v_ref, dst_ref=out_ref.at[src_slot],
                send_sem=send_sems.at[k - 1], recv_sem=recv_sems.at[k - 1],
                device_id=(x ^ fx, y ^ fy, c ^ fc), device_id_type=MESH).wait_recv()
        for cp in copies:
            cp.wait_send()
        mine.wait()

    return pl.pallas_call(
        body, name=name, out_shape=jax.ShapeDtypeStruct((N_DEV, R, C), v.dtype),
        in_specs=[pl.BlockSpec(memory_space=pltpu.VMEM)] + [pl.BlockSpec(memory_space=pl.ANY)] * len(after),
        out_specs=pl.BlockSpec(memory_space=pltpu.VMEM),
        scratch_shapes=[pltpu.SemaphoreType.DMA((N_DEV - 1,)), pltpu.SemaphoreType.DMA((N_DEV - 1,)),
                        pltpu.SemaphoreType.DMA],
        compiler_params=pltpu.CompilerParams(vmem_limit_bytes=VMEM_LIMIT))(v, *after)


def _swap_sibling(arrs):
    nw = len(arrs)

    def body(*refs):
        ins, outs = refs[:nw], refs[nw:2 * nw]
        send_sems, recv_sems = refs[2 * nw:]
        x, y, c = _me()
        copies = [pltpu.make_async_remote_copy(
            src_ref=ins[w], dst_ref=outs[w], send_sem=send_sems.at[w], recv_sem=recv_sems.at[w],
            device_id=(x, y, 1 - c), device_id_type=MESH) for w in range(nw)]
        for cp in copies:
            cp.start()
        for cp in copies:
            cp.wait_recv()
        for cp in copies:
            cp.wait_send()

    hbm = pl.BlockSpec(memory_space=pltpu.HBM)
    return pl.pallas_call(
        body, name="swap_sibling", out_shape=tuple(jax.ShapeDtypeStruct(a.shape, a.dtype) for a in arrs),
        in_specs=[hbm] * nw, out_specs=tuple([hbm] * nw),
        scratch_shapes=[pltpu.SemaphoreType.DMA((nw,)), pltpu.SemaphoreType.DMA((nw,))],
        compiler_params=pltpu.CompilerParams(vmem_limit_bytes=VMEM_LIMIT))(*arrs)


_HBM = pl.BlockSpec(memory_space=pltpu.HBM)
_SEM = pl.BlockSpec(memory_space=pltpu.SEMAPHORE)
_EFFECT = pltpu.SideEffectType.DATAFLOW_SIDE_EFFECTING
_N_PEER = N_CHIP - 1


def _chip_part(ref, axis, n, chip):
    start = pl.multiple_of(chip * n, 8)
    return ref.at[pl.ds(start, n), :] if axis == 0 else ref.at[:, pl.ds(start, n)]


def _gather_copy(k, src_ref, land_ref, send_sems, recv_sems, axis, arriving):
    x, y, c = _me()
    px, py = x ^ ((k >> 1) & 1), y ^ (k & 1)
    chip = 2 * px + py if arriving else 2 * x + y
    return pltpu.make_async_remote_copy(
        src_ref=src_ref, dst_ref=_chip_part(land_ref, axis, src_ref.shape[axis], chip),
        send_sem=send_sems.at[k - 1], recv_sem=recv_sems.at[k - 1], device_id=(px, py, c), device_id_type=MESH)


def _scatter_copy(k, grad_ref, land_ref, send_sems, recv_sems, axis):
    x, y, c = _me()
    px, py = x ^ ((k >> 1) & 1), y ^ (k & 1)
    return pltpu.make_async_remote_copy(
        src_ref=_chip_part(grad_ref, axis, grad_ref.shape[axis] // N_CHIP, 2 * px + py), dst_ref=land_ref.at[k - 1],
        send_sem=send_sems.at[k - 1], recv_sem=recv_sems.at[k - 1], device_id=(px, py, c), device_id_type=MESH)


def _own_copy(src_ref, land_ref, sends, axis):
    x, y, _ = _me()
    return pltpu.make_async_copy(src_ref, _chip_part(land_ref, axis, src_ref.shape[axis], 2 * x + y),
                                 sends.at[_N_PEER])


def _gather_start(shards, lands, axes, after):
    nw = len(shards)

    def body(*refs):
        srcs, zones = refs[:nw], refs[nw:2 * nw]
        sends, recvs = refs[2 * nw + 1:3 * nw + 1], refs[3 * nw + 1:4 * nw + 1]
        token = refs[-1]
        for w in range(nw):
            for k in range(1, N_CHIP):
                _gather_copy(k, srcs[w], zones[w], sends[w], recvs[w], axes[w], False).start()
        for w in range(nw):
            _own_copy(srcs[w], zones[w], sends[w], axes[w]).start()
        token[...] = jnp.zeros_like(token)

    outs = pl.pallas_call(
        body, name="gather_start",
        out_shape=tuple([pltpu.SemaphoreType.DMA((_N_PEER + 1,))] * nw + [pltpu.SemaphoreType.DMA((_N_PEER,))] * nw
                        + [pltpu.HBM(a.shape, a.dtype) for a in list(shards) + list(lands)]
                        + [jax.ShapeDtypeStruct((8, 128), F32)]),
        in_specs=[_HBM] * (2 * nw) + [pl.BlockSpec(memory_space=pl.ANY)],
        out_specs=tuple([_SEM] * (2 * nw) + [_HBM] * (2 * nw) + [pl.BlockSpec(memory_space=pltpu.VMEM)]),
        input_output_aliases={i: 2 * nw + i for i in range(2 * nw)},
        compiler_params=pltpu.CompilerParams(has_side_effects=_EFFECT),
    )(*([pltpu.with_memory_space_constraint(a, pltpu.HBM) for a in list(shards) + list(lands)] + [after]))
    per_weight = [(outs[w], outs[nw + w], outs[2 * nw + w], outs[3 * nw + w]) for w in range(nw)]
    return per_weight, outs[-1]


def _gather_wait(state, axis, after, name):
    send_sems, recv_sems, shard, land = state

    after = list(after) if isinstance(after, (list, tuple)) else [after]

    def body(src_ref, land_ref, sends, recvs, *rest):
        for k in range(1, N_CHIP):
            _gather_copy(k, src_ref, land_ref, sends, recvs, axis, False).wait_send()
            _gather_copy(k, src_ref, land_ref, sends, recvs, axis, True).wait_recv()
        _own_copy(src_ref, land_ref, sends, axis).wait()

    return pl.pallas_call(
        body, name=name, out_shape=(pltpu.HBM(shard.shape, shard.dtype), pltpu.HBM(land.shape, land.dtype)),
        in_specs=[_HBM, _HBM, _SEM, _SEM] + [pl.BlockSpec(memory_space=pl.ANY)] * len(after), out_specs=(_HBM, _HBM),
        input_output_aliases={0: 0, 1: 1},
        compiler_params=pltpu.CompilerParams(has_side_effects=_EFFECT),
    )(shard, land, send_sems, recv_sems, *after)[1]


def _all8_copy(k, v_ref, land_ref, send_sems, recv_sems, arriving):
    x, y, c = _me()
    px, py, pc = x ^ ((k >> 2) & 1), y ^ ((k >> 1) & 1), c ^ (k & 1)
    slot = 4 * px + 2 * py + pc if arriving else 4 * x + 2 * y + c
    return pltpu.make_async_remote_copy(
        src_ref=v_ref, dst_ref=land_ref.at[slot], send_sem=send_sems.at[k - 1], recv_sem=recv_sems.at[k - 1],
        device_id=(px, py, pc), device_id_type=MESH)


def _all8_own(v_ref, land_ref, send_sems):
    x, y, c = _me()
    return pltpu.make_async_copy(v_ref, land_ref.at[4 * x + 2 * y + c], send_sems.at[N_DEV - 1])


def _all8_start(v, name):
    land = lax.empty((N_DEV,) + v.shape, v.dtype)

    def body(v_ref, land_ref, sends, recvs, v_thru, land_thru, token):
        for k in range(1, N_DEV):
            _all8_copy(k, v_ref, land_ref, sends, recvs, False).start()
        _all8_own(v_ref, land_ref, sends).start()
        token[...] = jnp.zeros_like(token)

    outs = pl.pallas_call(
        body, name=name,
        out_shape=(pltpu.SemaphoreType.DMA((N_DEV,)), pltpu.SemaphoreType.DMA((N_DEV - 1,)),
                   pltpu.HBM(v.shape, v.dtype), pltpu.HBM(land.shape, land.dtype),
                   jax.ShapeDtypeStruct((8, 128), F32)),
        in_specs=[_HBM, _HBM], out_specs=(_SEM, _SEM, _HBM, _HBM, pl.BlockSpec(memory_space=pltpu.VMEM)),
        input_output_aliases={0: 2, 1: 3},
        compiler_params=pltpu.CompilerParams(has_side_effects=_EFFECT),
    )(pltpu.with_memory_space_constraint(v, pltpu.HBM), pltpu.with_memory_space_constraint(land, pltpu.HBM))
    return outs[:4], outs[4]


def _all8_wait(state, after, name):
    send_sems, recv_sems, v, land = state

    def body(v_ref, land_ref, sends, recvs, after_ref, v_dead, got_ref):
        for k in range(1, N_DEV):
            _all8_copy(k, v_ref, land_ref, sends, recvs, False).wait_send()
            _all8_copy(k, v_ref, land_ref, sends, recvs, True).wait_recv()
        _all8_own(v_ref, land_ref, sends).wait()

    return pl.pallas_call(
        body, name=name, out_shape=(pltpu.HBM(v.shape, v.dtype), pltpu.HBM(land.shape, land.dtype)),
        in_specs=[_HBM, _HBM, _SEM, _SEM, pl.BlockSpec(memory_space=pl.ANY)], out_specs=(_HBM, _HBM),
        input_output_aliases={0: 0, 1: 1},
        compiler_params=pltpu.CompilerParams(has_side_effects=_EFFECT),
    )(v, land, send_sems, recv_sems, after)[1]


def _swap_copy(w, src_ref, land_ref, send_sems, recv_sems):
    x, y, c = _me()
    return pltpu.make_async_remote_copy(src_ref=src_ref, dst_ref=land_ref, send_sem=send_sems.at[w],
                                        recv_sem=recv_sems.at[w], device_id=(x, y, 1 - c), device_id_type=MESH)


def _swap_start(arrs, after, name):
    nw = len(arrs)
    lands = [lax.empty(a.shape, a.dtype) for a in arrs]

    def body(*refs):
        srcs, zones = refs[:nw], refs[nw:2 * nw]
        sends, recvs = refs[2 * nw + 1], refs[2 * nw + 2]
        for w in range(nw):
            _swap_copy(w, srcs[w], zones[w], sends, recvs).start()
        refs[-1][...] = jnp.zeros_like(refs[-1])

    sem = pltpu.SemaphoreType.DMA((nw,))
    outs = pl.pallas_call(
        body, name=name,
        out_shape=tuple([sem, sem] + [pltpu.HBM(a.shape, a.dtype) for a in list(arrs) + lands]
                        + [jax.ShapeDtypeStruct((8, 128), F32)]),
        in_specs=[_HBM] * (2 * nw) + [pl.BlockSpec(memory_space=pl.ANY)],
        out_specs=tuple([_SEM, _SEM] + [_HBM] * (2 * nw) + [pl.BlockSpec(memory_space=pltpu.VMEM)]),
        input_output_aliases={i: 2 + i for i in range(2 * nw)},
        compiler_params=pltpu.CompilerParams(has_side_effects=_EFFECT),
    )(*([pltpu.with_memory_space_constraint(a, pltpu.HBM) for a in list(arrs) + lands] + [after]))
    return (outs[0], outs[1], outs[2:2 + nw], outs[2 + nw:2 + 2 * nw]), outs[-1]


def _swap_wait(state, after, name):
    send_sems, recv_sems, arrs, lands = state
    nw = len(arrs)

    def body(*refs):
        srcs, zones = refs[:nw], refs[nw:2 * nw]
        sends, recvs = refs[2 * nw], refs[2 * nw + 1]
        for w in range(nw):
            cp = _swap_copy(w, srcs[w], zones[w], sends, recvs)
            cp.wait_send()
            cp.wait_recv()

    outs = pl.pallas_call(
        body, name=name, out_shape=tuple(pltpu.HBM(a.shape, a.dtype) for a in list(arrs) + list(lands)),
        in_specs=[_HBM] * (2 * nw) + [_SEM, _SEM, pl.BlockSpec(memory_space=pl.ANY)],
        out_specs=tuple([_HBM] * (2 * nw)),
        input_output_aliases={i: i for i in range(2 * nw)},
        compiler_params=pltpu.CompilerParams(has_side_effects=_EFFECT),
    )(*arrs, *lands, send_sems, recv_sems, after)
    return list(outs[:nw]), list(outs[nw:])


def _scatter_start(grad, axis, name):
    shp = list(grad.shape)
    shp[axis] //= N_CHIP
    land = lax.empty((_N_PEER,) + tuple(shp), grad.dtype)

    def body(grad_ref, land_ref, sends, recvs, grad_thru, land_thru, token):
        for k in range(1, N_CHIP):
            _scatter_copy(k, grad_ref, land_ref, sends, recvs, axis).start()
        token[...] = jnp.zeros_like(token)

    sem = pltpu.SemaphoreType.DMA((_N_PEER,))
    outs = pl.pallas_call(
        body, name=name,
        out_shape=(sem, sem, pltpu.HBM(grad.shape, grad.dtype), pltpu.HBM(land.shape, land.dtype),
                   jax.ShapeDtypeStruct((8, 128), F32)),
        in_specs=[_HBM, _HBM], out_specs=(_SEM, _SEM, _HBM, _HBM, pl.BlockSpec(memory_space=pltpu.VMEM)),
        input_output_aliases={0: 2, 1: 3},
        compiler_params=pltpu.CompilerParams(has_side_effects=_EFFECT),
    )(pltpu.with_memory_space_constraint(grad, pltpu.HBM), pltpu.with_memory_space_constraint(land, pltpu.HBM))
    return outs[:4], outs[4]


def _scatter_wait(state, axis, after, name):
    send_sems, recv_sems, grad, land = state

    def body(grad_ref, land_ref, sends, recvs, after_ref, grad_dead, got_ref):
        for k in range(1, N_CHIP):
            cp = _scatter_copy(k, grad_ref, land_ref, sends, recvs, axis)
            cp.wait_send()
            cp.wait_recv()

    return pl.pallas_call(
        body, name=name, out_shape=(pltpu.HBM(grad.shape, grad.dtype), pltpu.HBM(land.shape, land.dtype)),
        in_specs=[_HBM, _HBM, _SEM, _SEM, pl.BlockSpec(memory_space=pl.ANY)], out_specs=(_HBM, _HBM),
        input_output_aliases={0: 0, 1: 1},
        compiler_params=pltpu.CompilerParams(has_side_effects=_EFFECT),
    )(grad, land, send_sems, recv_sems, after)[1]


_C1 = 1.0 - B1 ** STEP
_C2 = 1.0 - B2 ** STEP


def _adam_math(w, g, m, v):
    m = B1 * m + (1.0 - B1) * g
    v = B2 * v + (1.0 - B2) * (g * g)
    delta = -LR * ((m / _C1) / (jnp.sqrt(v / _C2) + AEPS) + WD * w)
    return delta, m, v


def _adamw(w, m, v, groups, name):
    R, C = w.shape
    tr = R if R <= 256 else (128 if R % 128 == 0 else 176)
    assert R % tr == 0, (name, R)
    gparts = [p for grp in groups for p in grp]
    sizes = [len(grp) for grp in groups]
    ng = len(gparts)

    def body(*refs):
        w_ref, m_ref, v_ref = refs[:3]
        g_refs = list(refs[3:3 + ng])
        g_out, d_out, m_out, v_out = refs[3 + ng:]
        g = None
        for size in sizes:
            s = None
            for r in [g_refs.pop(0) for _ in range(size)]:
                terms = [r[q] for q in range(r.shape[0])] if len(r.shape) == 3 else [r[...]]
                for t in terms:
                    s = t.astype(F32) if s is None else s + t.astype(F32)
            g = s if g is None else g + s
        delta, mn, vn = _adam_math(w_ref[...], g, m_ref[...], v_ref[...])
        g_out[...] = g
        d_out[...] = delta
        m_out[...] = mn
        v_out[...] = vn

    blk = pl.BlockSpec((tr, C), lambda i: (i, 0))
    g_specs = [blk if p.ndim == 2 else pl.BlockSpec((p.shape[0], tr, C), lambda i: (0, i, 0)) for p in gparts]
    sds = jax.ShapeDtypeStruct((R, C), F32)
    return pl.pallas_call(
        body, name=name, out_shape=(sds, sds, sds, sds), grid=(R // tr,),
        in_specs=[blk, blk, blk] + g_specs, out_specs=(blk, blk, blk, blk),
        compiler_params=_cp(("parallel",)))(w, m, v, *gparts)


def _mod_shard(c_all, w_ada, b_ada_cols):
    n = w_ada.shape[1]
    tn = 512

    def body(c_ref, w_ref, b_ref, o_ref):
        cv = c_ref[...]
        ca = (cv * _sig(cv)).astype(BF16)
        o_ref[...] = jnp.dot(ca, w_ref[...].astype(BF16), preferred_element_type=F32) + b_ref[...]

    return pl.pallas_call(
        body, name="mod_shard", out_shape=jax.ShapeDtypeStruct((N_DEV, n), F32), grid=(n // tn,),
        in_specs=[_full((N_DEV, D_MODEL)), pl.BlockSpec((D_MODEL, tn), lambda j: (0, j)),
                  pl.BlockSpec((1, tn), lambda j: (0, j))],
        out_specs=pl.BlockSpec((N_DEV, tn), lambda j: (0, j)),
        compiler_params=_cp(("parallel",)))(c_all, w_ada, b_ada_cols)


def _ada_grad(c_all, dmod_cols):
    n = dmod_cols.shape[1]
    tn = 512

    def body(c_ref, d_ref, o_ref):
        cv = c_ref[...]
        ca = cv * _sig(cv)
        o_ref[...] = lax.dot_general(ca, d_ref[...], (((0,), (0,)), ((), ())),
                                     preferred_element_type=F32, precision=lax.Precision.HIGHEST)

    return pl.pallas_call(
        body, name="ada_grad", out_shape=jax.ShapeDtypeStruct((D_MODEL, n), F32), grid=(n // tn,),
        in_specs=[_full((N_DEV, D_MODEL)), pl.BlockSpec((N_DEV, tn), lambda j: (0, j))],
        out_specs=pl.BlockSpec((D_MODEL, tn), lambda j: (0, j)),
        compiler_params=_cp(("parallel",)))(c_all, dmod_cols)


def _device_step(x, mod, W, tgt, getw, put, early):
    sh1, sc1, g1, sh2, sc2, g2 = [mod[:, i * D_MODEL:(i + 1) * D_MODEL] for i in range(6)]
    e_re, e_im, bb_re, bb_im = _ssm_prep(W["ssm_a_re"], W["ssm_a_im"], W["ssm_b_re"], W["ssm_b_im"], W["ssm_log_dt"])
    bb, cm = _block_diag_mats(bb_re, bb_im, W["ssm_c_re"], W["ssm_c_im"])
    bb16, cm16 = bb.astype(BF16), cm.astype(BF16)
    bbt16, cmt16 = bb16.T, cm16.T
    tab_f = _scan_tables(e_re, e_im, False)
    tab_b = _scan_tables(e_re, e_im, True)

    w_in = getw("w_in", [mod, bb16, cm16, bbt16, cmt16, tab_f, tab_b])
    h1, z = _in_proj(x, W["norm1_g"], sc1, sh1, w_in)
    yc, scv = _conv_fwd(z, W["conv_w"], W["conv_b"], W["conv_ln_g"], W["conv_ln_b"])
    xs, ys, yg = _ssm_fwd(z, bb16, cm16, W["ssm_d"], tab_f)
    w_cp, w_glu, w_out = getw("conv_proj", scv), getw("ssm_glu", yg), getw("w_out", yg)
    y_conv, zz, merged, o, x2, h2 = _mix_fwd(scv, yg, z, x, w_cp, w_glu, w_out, g1, W["norm2_g"], sc2, sh2)
    w_fi = getw("w_ffn_in", h2)
    f, act = _ffn_in_act(h2, w_fi)
    w_fo = getw("w_ffn_out", act)
    dx3, do2, loss8, dfg8, dg2_8 = _ffn_out_final(x2, act, w_fo, g2, W["final_g"], tgt)

    sm = {}
    tok = put("w_ffn_out", _matmul(act, do2, "tn", 1408, 1024, 2048, BF16, "mm_g_ffn_out"))
    df = _ffn_bwd(do2, w_fo, f, tok)
    tok = put("w_ffn_in", _matmul(h2, df, "tn", 1024, 1408, 2048, BF16, "mm_g_ffn_in"))
    dx2, do, dsh2, dsc2, dn2, dg1_8 = _normmod_bwd(df, w_fi, x2, dx3, W["norm2_g"], sc2, g1, o, tok, "d_h2_normmod2_bwd")
    tok = put("w_out", _matmul(merged, do, "tn", 1024, 1024, 4096, BF16, "mm_g_w_out"))
    dyconv, dgl, dzz = _mix_bwd(do, w_out, z, zz, y_conv, tok)
    tok = put("ssm_glu", _matmul(yg, dzz, "tn", 512, 1024, 4096, BF16, "mm_g_ssm_glu"))
    tok = put("conv_proj", _matmul(scv, dyconv, "tn", 512, 1024, 4096, BF16, "mm_g_conv_proj", after=tok))
    du, de16, dd8, dc_full, dbb_full = _ssm_bwd(dzz, w_glu, ys, z, xs, cmt16, bbt16, W["ssm_d"], tab_b, tok)
    dyc, dlg8, dlb8, dcb8 = _conv_bwd_ln(dyconv, w_cp, yc, W["conv_ln_g"], W["conv_ln_b"])
    dz_conv, dcw = _conv_bwd(dyc, z, W["conv_w"])

    s8 = lambda a: jnp.sum(a, axis=0, keepdims=True)
    de = de16.reshape(2, 8, NST).sum(1)
    de_re, de_im = de[0].reshape(G, P), de[1].reshape(G, P)
    dc_re = _diag_blocks(dc_full[:, :NST])
    dc_im = -_diag_blocks(dc_full[:, NST:])
    dbb_re = jnp.swapaxes(_diag_blocks(dbb_full[:, :NST]), 1, 2)
    dbb_im = jnp.swapaxes(_diag_blocks(dbb_full[:, NST:]), 1, 2)
    _, vjp = jax.vjp(_ssm_prep, W["ssm_a_re"], W["ssm_a_im"], W["ssm_b_re"], W["ssm_b_im"], W["ssm_log_dt"])
    sm["ssm_a_re"], sm["ssm_a_im"], sm["ssm_b_re"], sm["ssm_b_im"], sm["ssm_log_dt"] = vjp((de_re, de_im, dbb_re, dbb_im))
    sm["ssm_c_re"], sm["ssm_c_im"] = dc_re, dc_im
    sm["ssm_d"] = s8(dd8)
    sm["norm2_g"] = s8(dn2)
    sm["conv_b"], sm["conv_ln_g"], sm["conv_ln_b"] = s8(dcb8), s8(dlg8), s8(dlb8)
    sm["conv_w"] = dcw.reshape(KW, 8, CW).sum(1)
    sm["final_g"] = s8(dfg8)
    tok = early(sm)

    dz = jnp.concatenate([dz_conv, du, dgl], axis=1)
    tok = put("w_in", _matmul(h1, dz, "tn", 1024, 896, 4096, BF16, "mm_g_w_in", after=tok))
    dx, _, dsh1, dsc1, dn1, _ = _normmod_bwd(dz, w_in, x, dx2, W["norm1_g"], sc1, g1, o, tok, "d_h1_normmod1_bwd")
    dmod = jnp.concatenate([s8(dsh1), s8(dsc1), s8(dg1_8), s8(dsh2), s8(dsc2), s8(dg2_8)], axis=1)
    return loss8, dx, s8(dn1), dmod


_BIG = ("w_in", "conv_proj", "ssm_glu", "w_out", "w_ffn_in", "w_ffn_out")
_BIG_AXIS = {"w_in": 1, "conv_proj": 1, "ssm_glu": 1, "w_out": 0, "w_ffn_in": 1, "w_ffn_out": 0}
_EARLY = ("conv_w", "conv_b", "conv_ln_g", "conv_ln_b", "ssm_a_re", "ssm_a_im", "ssm_b_re", "ssm_b_im", "ssm_c_re",
          "ssm_c_im", "ssm_d", "ssm_log_dt", "norm2_g", "final_g")
_LATE = ("norm1_g", "b_ada")
_ORDER = ("w_ada", "b_ada", "norm1_g", "w_in", "conv_w", "conv_b", "conv_ln_g", "conv_ln_b", "conv_proj",
          "ssm_a_re", "ssm_a_im", "ssm_b_re", "ssm_b_im", "ssm_c_re", "ssm_c_im", "ssm_d", "ssm_log_dt", "ssm_glu",
          "w_out", "norm2_g", "w_ffn_in", "w_ffn_out", "final_g")
_PACK_COLS = 1024


def _pack_rows(shape):
    return -(-int(np.prod(shape)) // (8 * _PACK_COLS)) * 8


def _pack(arrs):
    parts = []
    for a in arrs:
        flat = a.reshape(-1)
        n = _pack_rows(a.shape)
        parts.append(jnp.pad(flat, (0, n * _PACK_COLS - flat.shape[0])).reshape(n, _PACK_COLS))
    return jnp.concatenate(parts, 0)


def _unpack(packed, shapes):
    out, r = [], 0
    for shp in shapes:
        size = int(np.prod(shp))
        n = _pack_rows(shp)
        out.append(packed[r:r + n].reshape(-1)[:size].reshape(shp))
        r += n
    return out


def kernel(x, c, w_ada, b_ada, norm1_g, w_in, conv_w, conv_b, conv_ln_g, conv_ln_b, conv_proj, ssm_a_re, ssm_a_im, ssm_b_re, ssm_b_im, ssm_c_re, ssm_c_im, ssm_d, ssm_log_dt, ssm_glu, w_out, norm2_g, w_ffn_in, w_ffn_out, final_g, loss_target, m_w_ada, m_b_ada, m_norm1_g, m_w_in, m_conv_w, m_conv_b, m_conv_ln_g, m_conv_ln_b, m_conv_proj, m_ssm_a_re, m_ssm_a_im, m_ssm_b_re, m_ssm_b_im, m_ssm_c_re, m_ssm_c_im, m_ssm_d, m_ssm_log_dt, m_ssm_glu, m_w_out, m_norm2_g, m_w_ffn_in, m_w_ffn_out, m_final_g, v_w_ada, v_b_ada, v_norm1_g, v_w_in, v_conv_w, v_conv_b, v_conv_ln_g, v_conv_ln_b, v_conv_proj, v_ssm_a_re, v_ssm_a_im, v_ssm_b_re, v_ssm_b_im, v_ssm_c_re, v_ssm_c_im, v_ssm_d, v_ssm_log_dt, v_ssm_glu, v_w_out, v_norm2_g, v_w_ffn_in, v_w_ffn_out, v_final_g):
    given = dict(locals())
    mx, my, mc = _me()
    chip = 2 * mx + my
    dev = 4 * mx + 2 * my + mc
    def canon(a):
        return a.reshape(1, -1) if a.ndim <= 2 else a[0]

    wts = {n: canon(given[n]) for n in _ORDER}
    mom = {n: canon(given["m_" + n]) for n in _ORDER}
    var = {n: canon(given["v_" + n]) for n in _ORDER}

    c_all = _allgather8(jnp.broadcast_to(c, (8, D_MODEL)), "gather_c")[:, 0, :]
    n_ada = wts["w_ada"].shape[1]
    b_cols = lax.dynamic_slice(wts["b_ada"], (0, chip * n_ada), (1, n_ada))
    mod_cols = _mod_shard(c_all, wts["w_ada"], b_cols)
    mods = _allgather8(mod_cols, "gather_mod")
    mod = jnp.concatenate([lax.dynamic_index_in_dim(mods[2 * q], dev, 0, keepdims=True) for q in range(N_CHIP)], axis=1)

    W = {n: wts[n] for n in _ORDER if n not in _BIG}
    conv_w_full = _allgather8(jnp.pad(wts["conv_w"], ((0, 1), (0, 0))), "gather_conv_w")
    W["conv_w"] = jnp.concatenate([conv_w_full[2 * q, :KW] for q in range(N_CHIP)], axis=1)

    axes = [_BIG_AXIS[n] for n in _BIG]
    shards = [wts[n].astype(BF16) for n in _BIG]
    lands = []
    for s, ax in zip(shards, axes):
        shp = list(s.shape)
        shp[ax] *= N_CHIP
        lands.append(lax.empty(tuple(shp), BF16))
    gstate, token = _gather_start(shards, lands, axes, mod + W["conv_w"][0:1, 0:1])
    gstate = dict(zip(_BIG, gstate))
    mod = mod + token[0:1, 0:1]

    def getw(n, after):
        return _gather_wait(gstate[n], _BIG_AXIS[n], after, "gather_wait_" + n)

    sstate, own, estate = {}, {}, []

    def put(n, g):
        ax = _BIG_AXIS[n]
        k = g.shape[ax] // N_CHIP
        own[n] = lax.dynamic_slice_in_dim(g, chip * k, k, axis=ax)
        sstate[n], tok = _scatter_start(g, ax, "scatter_start_" + n)
        return tok

    first5 = [n for n in _BIG if n != "w_in"]

    def early(sm):
        state, tok = _all8_start(_pack([sm[n] for n in _EARLY]), "small_start")
        estate.append(state)
        recv5 = [_scatter_wait(sstate[n], _BIG_AXIS[n], tok, "scatter_wait_" + n) for n in first5]
        held = [a for n, r in zip(first5, recv5) for a in (own[n], r)]
        state, tok = _swap_start(held, tok, "swap_start")
        estate.append(state)
        return tok

    loss8, dx, dn1, dmod = _device_step(x[0], mod, W, loss_target[0], getw, put, early)
    loss = lax.psum(jnp.sum(loss8), ("x", "y", "c"))

    held5, sib5 = _swap_wait(estate[1], dx, "swap_wait")
    outs = {}
    for i, n in enumerate(first5):
        outs[n] = _adamw(wts[n], mom[n], var[n], [held5[2 * i:2 * i + 2], sib5[2 * i:2 * i + 2]], "adamw_" + n)
    allp = _all8_wait(estate[0], dx, "small_wait")

    late = _allgather8(_pack([dn1, dmod]), "gather_late", after=[outs[n][1] for n in first5])
    held_in = [own["w_in"], _scatter_wait(sstate["w_in"], _BIG_AXIS["w_in"], late, "scatter_wait_w_in")]
    sib_in = _swap_sibling(held_in)
    outs["w_in"] = _adamw(wts["w_in"], mom["w_in"], var["w_in"], [held_in, sib_in], "adamw_w_in")

    r1 = _pack_rows((D_MODEL,))
    dmod_all = late[:, r1:, :].reshape(N_DEV, -1)[:, :6 * D_MODEL]
    dmod_cols = lax.dynamic_slice(dmod_all, (0, chip * n_ada), (N_DEV, n_ada))
    g_ada = _ada_grad(c_all, dmod_cols)
    outs["w_ada"] = _adamw(wts["w_ada"], mom["w_ada"], var["w_ada"], [[g_ada]], "adamw_w_ada")

    def packed_params(d, names):
        return _pack([jnp.zeros((KW, CW), F32) if n == "conv_w" else d[n] for n in names])

    for names, parts, nm in ((_EARLY, allp, "adamw_small"), (_LATE, late, "adamw_late")):
        res = _adamw(packed_params(wts, names), packed_params(mom, names), packed_params(var, names), [[parts]], nm)
        shapes = [(KW, CW) if n == "conv_w" else wts[n].shape for n in names]
        unpacked = [_unpack(r, shapes) for r in res]
        for idx, n in enumerate(names):
            outs[n] = tuple(unpacked[q][idx] for q in range(4))
    g_cw = lax.dynamic_slice(outs["conv_w"][0], (0, chip * (CW // N_CHIP)), (KW, CW // N_CHIP))
    pad = lambda a: jnp.pad(a, ((0, 1), (0, 0)))
    r_cw = _adamw(pad(wts["conv_w"]), pad(mom["conv_w"]), pad(var["conv_w"]), [[pad(g_cw)]], "adamw_conv_w")
    outs["conv_w"] = tuple(r[:KW] for r in r_cw)

    def shaped(n, a):
        return a.reshape(given[n].shape)

    result = [loss, dx[None]]
    for q in range(4):
        result += [shaped(n, outs[n][q]) for n in _ORDER]
    return tuple(result)
```

```python
import math

import jax
import jax.numpy as jnp
import numpy as np
from jax import lax
from jax.experimental import pallas as pl
from jax.experimental.pallas import tpu as pltpu

F32 = jnp.float32
BF16 = jnp.bfloat16
EPS = 1e-6
D_MODEL = 1024
CW = 512
KW = 31
HALO = 32
G, P, H = 32, 64, 16
NST = G * P
FH = 2816
N_DEV = 8
N_CHIP = 4
VMEM_LIMIT = 56 * 1024 * 1024
LR, B1, B2, AEPS, WD, STEP = 0.001, 0.9, 0.999, 1e-08, 0.01, 10
MESH = pl.DeviceIdType.MESH


def _cp(sem=None):
    return pltpu.CompilerParams(dimension_semantics=sem, vmem_limit_bytes=VMEM_LIMIT)


def _sig(x):
    return jax.nn.sigmoid(x)


def _full(shape):
    return pl.BlockSpec(shape, lambda *_: (0,) * len(shape))


def _colsum8(v):
    t, c = v.shape
    return jnp.sum(v.reshape(t // 8, 8, c), axis=0)


def _matmul(a, b, mode, tm, tn, tk, out_dtype, name, after=None, n_outer=False, m_cols=None):
    m0 = 0
    if mode == "nn":
        (M, K), N = a.shape, b.shape[1]
    elif mode == "nt":
        (M, K), N = a.shape, b.shape[0]
    else:
        (K, M), N = a.shape, b.shape[1]
        if m_cols is not None:
            m0, M = m_cols
    tm, tn, tk = min(tm, M), min(tn, N), min(tk, K)
    assert M % tm == 0 and N % tn == 0 and K % tk == 0 and m0 % tm == 0, (name, M, N, K, tm, tn, tk)
    nk = K // tk
    mb = m0 // tm

    def ij(fn):
        return (lambda j, i, k: fn(i, j, k)) if n_outer else fn

    if mode == "nn":
        a_spec = pl.BlockSpec((tm, tk), ij(lambda i, j, k: (i, k)))
        b_spec = pl.BlockSpec((tk, tn), ij(lambda i, j, k: (k, j)))
        dims = (((1,), (0,)), ((), ()))
    elif mode == "nt":
        a_spec = pl.BlockSpec((tm, tk), ij(lambda i, j, k: (i, k)))
        b_spec = pl.BlockSpec((tn, tk), ij(lambda i, j, k: (j, k)))
        dims = (((1,), (1,)), ((), ()))
    else:
        a_spec = pl.BlockSpec((tk, tm), ij(lambda i, j, k: (k, i + mb)))
        b_spec = pl.BlockSpec((tk, tn), ij(lambda i, j, k: (k, j)))
        dims = (((0,), (0,)), ((), ()))

    def body(a_ref, b_ref, *rest):
        o_ref, acc_ref = rest[-2:]
        k = pl.program_id(2)
        part = lax.dot_general(a_ref[...].astype(BF16), b_ref[...].astype(BF16), dims,
                               preferred_element_type=F32)
        if nk == 1:
            o_ref[...] = part.astype(out_dtype)
        else:
            @pl.when(k == 0)
            def _():
                acc_ref[...] = part

            @pl.when(k > 0)
            def _():
                acc_ref[...] += part

            @pl.when(k == nk - 1)
            def _():
                o_ref[...] = acc_ref[...].astype(out_dtype)

    return pl.pallas_call(
        body, name=name,
        out_shape=jax.ShapeDtypeStruct((M, N), out_dtype),
        grid=(N // tn, M // tm, nk) if n_outer else (M // tm, N // tn, nk),
        in_specs=[a_spec, b_spec] + ([] if after is None else [pl.BlockSpec(memory_space=pl.ANY)]),
        out_specs=pl.BlockSpec((tm, tn), ij(lambda i, j, k: (i, j))),
        scratch_shapes=[pltpu.VMEM((tm, tn) if nk > 1 else (8, 128), F32)],
        compiler_params=_cp(("parallel", "parallel", "arbitrary")),
    )(*((a, b) if after is None else (a, b, after)))


def _row_tile(S):
    return min(512, S)


def _in_proj(x, g, sc, sh, w_in):
    S, D = x.shape
    N = w_in.shape[1]
    tm = min(256, S)

    def body(x_ref, g_ref, sc_ref, sh_ref, w_ref, h_ref, z_ref):
        xv = x_ref[...]
        r = lax.rsqrt(jnp.mean(xv * xv, axis=-1, keepdims=True) + EPS)
        h = (xv * r * (g_ref[...] * (1.0 + sc_ref[...])) + sh_ref[...]).astype(BF16)
        h_ref[...] = h
        z_ref[...] = jnp.dot(h, w_ref[...], preferred_element_type=F32)

    row = pl.BlockSpec((tm, D), lambda i: (i, 0))
    par = _full((1, D))
    return pl.pallas_call(
        body, name="in_proj",
        out_shape=(jax.ShapeDtypeStruct((S, D), BF16), jax.ShapeDtypeStruct((S, N), F32)), grid=(S // tm,),
        in_specs=[row, par, par, par, _full((D, N))], out_specs=(row, pl.BlockSpec((tm, N), lambda i: (i, 0))),
        compiler_params=_cp(("parallel",)))(x, g, sc, sh, w_in)


def _fill_shifted(buf_ref, sh_ref):
    n = buf_ref.shape[0] - 8
    for s in range(1, 8):
        sh_ref[s, 0:n, :] = buf_ref[s:s + n, :]


def _window(buf_ref, sh_ref, off, n):
    s = off % 8
    return buf_ref[off:off + n, :] if s == 0 else sh_ref[s, off - s:off - s + n, :]


def _conv_fwd(z, conv_w, conv_b, ln_g, ln_b):
    S = z.shape[0]
    tm = min(128, S)
    sub = 32
    hb = tm // HALO

    def body(a_ref, g_ref, ha_ref, hg_ref, w_ref, b_ref, lg_ref, lb_ref, yc_ref, s_ref, ug_ref, sh_ref):
        i = pl.program_id(0)
        halo = ha_ref[...] * _sig(hg_ref[...])
        ug_ref[0:HALO, :] = jnp.where(i == 0, 0.0, halo)
        ug_ref[HALO:, :] = a_ref[...] * _sig(g_ref[...])
        _fill_shifted(ug_ref, sh_ref)
        for rb in range(tm // sub):
            acc = jnp.zeros((sub, CW), F32) + b_ref[...]
            for k in range(KW):
                off = rb * sub + HALO - (KW - 1) + k
                acc = acc + w_ref[k:k + 1, :] * _window(ug_ref, sh_ref, off, sub)
            yc_ref[rb * sub:(rb + 1) * sub, :] = acc
            mu = jnp.mean(acc, axis=-1, keepdims=True)
            cen = acc - mu
            rstd = lax.rsqrt(jnp.mean(cen * cen, axis=-1, keepdims=True) + EPS)
            ln = cen * rstd * lg_ref[...] + lb_ref[...]
            s_ref[rb * sub:(rb + 1) * sub, :] = (ln * _sig(ln)).astype(BF16)

    prev = lambda i: (jnp.maximum(i * hb - 1, 0), 0)
    return pl.pallas_call(
        body, name="conv_fwd",
        out_shape=(jax.ShapeDtypeStruct((S, CW), F32), jax.ShapeDtypeStruct((S, CW), BF16)),
        grid=(S // tm,),
        in_specs=[pl.BlockSpec((tm, CW), lambda i: (i, 0)), pl.BlockSpec((tm, CW), lambda i: (i, 1)),
                  pl.BlockSpec((HALO, CW), prev), pl.BlockSpec((HALO, CW), lambda i: (jnp.maximum(i * hb - 1, 0), 1)),
                  _full((KW, CW)), _full((1, CW)), _full((1, CW)), _full((1, CW))],
        out_specs=(pl.BlockSpec((tm, CW), lambda i: (i, 0)), pl.BlockSpec((tm, CW), lambda i: (i, 0))),
        scratch_shapes=[pltpu.VMEM((tm + HALO, CW), F32), pltpu.VMEM((8, tm + HALO, CW), F32)],
        compiler_params=_cp(("parallel",)))(z, z, z, z, conv_w, conv_b, ln_g, ln_b)


def _conv_bwd_ln(dyconv, w_cp, yc, ln_g, ln_b):
    S = yc.shape[0]
    tm = _row_tile(S)

    def body(dy_ref, w_ref, yc_ref, lg_ref, lb_ref, dyc_ref, dlg_ref, dlb_ref, dcb_ref):
        i = pl.program_id(0)
        dsc = lax.dot_general(dy_ref[...], w_ref[...], (((1,), (1,)), ((), ())), preferred_element_type=F32)
        yc_v = yc_ref[...]
        mu = jnp.mean(yc_v, axis=-1, keepdims=True)
        cen = yc_v - mu
        rstd = lax.rsqrt(jnp.mean(cen * cen, axis=-1, keepdims=True) + EPS)
        yn = cen * rstd
        ln = yn * lg_ref[...] + lb_ref[...]
        sl = _sig(ln)
        dln = dsc * (sl * (1.0 + ln * (1.0 - sl)))
        dyn = dln * lg_ref[...]
        dyc = rstd * (dyn - jnp.mean(dyn, axis=-1, keepdims=True)
                      - yn * jnp.mean(dyn * yn, axis=-1, keepdims=True))
        dyc_ref[...] = dyc

        @pl.when(i == 0)
        def _():
            dlg_ref[...] = jnp.zeros_like(dlg_ref)
            dlb_ref[...] = jnp.zeros_like(dlb_ref)
            dcb_ref[...] = jnp.zeros_like(dcb_ref)

        dlg_ref[...] += _colsum8(dln * yn)
        dlb_ref[...] += _colsum8(dln)
        dcb_ref[...] += _colsum8(dyc)

    row = pl.BlockSpec((tm, CW), lambda i: (i, 0))
    acc = jax.ShapeDtypeStruct((8, CW), F32)
    return pl.pallas_call(
        body, name="conv_bwd_ln",
        out_shape=(jax.ShapeDtypeStruct((S, CW), F32), acc, acc, acc), grid=(S // tm,),
        in_specs=[pl.BlockSpec((tm, D_MODEL), lambda i: (i, 0)), _full((CW, D_MODEL)), row, _full((1, CW)),
                  _full((1, CW))],
        out_specs=(row, _full((8, CW)), _full((8, CW)), _full((8, CW))),
        compiler_params=_cp(("arbitrary",)))(dyconv, w_cp, yc, ln_g, ln_b)


def _conv_bwd(dyc, z, conv_w):
    S = z.shape[0]
    tm = min(128, S)
    sub = 32
    hb = tm // HALO
    nt = S // tm

    def body(d_ref, dn_ref, a_ref, g_ref, ha_ref, hg_ref, w_ref, dz_ref, dw_ref, ug_ref, dy_ref, ugs_ref, dys_ref):
        i = pl.program_id(0)
        halo = ha_ref[...] * _sig(hg_ref[...])
        ug_ref[0:HALO, :] = jnp.where(i == 0, 0.0, halo)
        a = a_ref[...]
        sg = _sig(g_ref[...])
        ug_ref[HALO:, :] = a * sg
        dy_ref[0:tm, :] = d_ref[...]
        dy_ref[tm:, :] = jnp.where(i == nt - 1, 0.0, dn_ref[...])
        _fill_shifted(ug_ref, ugs_ref)
        _fill_shifted(dy_ref, dys_ref)

        @pl.when(i == 0)
        def _():
            dw_ref[...] = jnp.zeros_like(dw_ref)

        for rb in range(tm // sub):
            r0 = rb * sub
            acc = jnp.zeros((sub, CW), F32)
            dyc_b = dy_ref[r0:r0 + sub, :]
            for k in range(KW):
                up = r0 + (KW - 1) - k
                acc = acc + w_ref[k:k + 1, :] * _window(dy_ref, dys_ref, up, sub)
                off = r0 + HALO - (KW - 1) + k
                dw_ref[k * 8:(k + 1) * 8, :] += _colsum8(dyc_b * _window(ug_ref, ugs_ref, off, sub))
            a_b = a[r0:r0 + sub, :]
            sg_b = sg[r0:r0 + sub, :]
            dz_ref[r0:r0 + sub, 0:CW] = (acc * sg_b).astype(BF16)
            dz_ref[r0:r0 + sub, CW:2 * CW] = (acc * a_b * sg_b * (1.0 - sg_b)).astype(BF16)

    return pl.pallas_call(
        body, name="conv_bwd",
        out_shape=(jax.ShapeDtypeStruct((S, 2 * CW), BF16), jax.ShapeDtypeStruct((KW * 8, CW), F32)),
        grid=(nt,),
        in_specs=[pl.BlockSpec((tm, CW), lambda i: (i, 0)),
                  pl.BlockSpec((HALO, CW), lambda i: (jnp.minimum((i + 1) * hb, nt * hb - 1), 0)),
                  pl.BlockSpec((tm, CW), lambda i: (i, 0)), pl.BlockSpec((tm, CW), lambda i: (i, 1)),
                  pl.BlockSpec((HALO, CW), lambda i: (jnp.maximum(i * hb - 1, 0), 0)),
                  pl.BlockSpec((HALO, CW), lambda i: (jnp.maximum(i * hb - 1, 0), 1)),
                  _full((KW, CW))],
        out_specs=(pl.BlockSpec((tm, 2 * CW), lambda i: (i, 0)), _full((KW * 8, CW))),
        scratch_shapes=[pltpu.VMEM((tm + HALO, CW), F32), pltpu.VMEM((tm + HALO, CW), F32),
                        pltpu.VMEM((8, tm + HALO, CW), F32), pltpu.VMEM((8, tm + HALO, CW), F32)],
        compiler_params=_cp(("arbitrary",)))(dyc, dyc, z, z, z, z, conv_w)


_GELU_C = math.sqrt(2.0 / math.pi)


def _gelu(x):
    return 0.5 * x * (1.0 + jnp.tanh(_GELU_C * (x + 0.044715 * x * x * x)))


def _gelu_grad(x):
    t = jnp.tanh(_GELU_C * (x + 0.044715 * x * x * x))
    return 0.5 * (1.0 + t) + 0.5 * x * (1.0 - t * t) * (_GELU_C * (1.0 + 3 * 0.044715 * x * x))


_NCL = 4
_UC = CW // _NCL
_LW = NST // _NCL
_CS = 2 * _LW


def _ssm_fwd(z, bb, cm, d, tab):
    S = z.shape[0]
    tm = min(256, S)

    def body(u_ref, bb_ref, cm_ref, d_ref, t_ref, x_ref, ys_ref, yg_ref, car_ref):
        i = pl.program_id(0)

        @pl.when(i == 0)
        def _():
            car_ref[...] = jnp.zeros_like(car_ref)

        u = u_ref[...]
        u16 = u.astype(BF16)
        for c in range(_NCL):
            lre = pl.ds(c * _CS, _LW)
            lim = pl.ds(c * _CS + _LW, _LW)
            tl = pl.ds(c * _LW, _LW)
            x_ref[:, c * _CS:(c + 1) * _CS] = jnp.dot(u16[:, c * _UC:(c + 1) * _UC], bb_ref[c],
                                                      preferred_element_type=F32)

            def blk(j, car):
                cr, ci = car
                rows = pl.ds(pl.multiple_of(j * 8, 8), 8)
                r = x_ref[rows, lre]
                im = x_ref[rows, lim]
                for lvl, s in enumerate((1, 2, 4)):
                    mr = t_ref[16 * lvl:16 * lvl + 8, tl]
                    mi = t_ref[16 * lvl + 8:16 * lvl + 16, tl]
                    sr = pltpu.roll(r, s, 0)
                    si = pltpu.roll(im, s, 0)
                    r, im = r + (mr * sr - mi * si), im + (mr * si + mi * sr)
                pr = t_ref[48:56, tl]
                pi_ = t_ref[56:64, tl]
                r, im = r + (pr * cr - pi_ * ci), im + (pr * ci + pi_ * cr)
                x_ref[rows, lre] = r
                x_ref[rows, lim] = im
                return (jnp.broadcast_to(r[7:8, :], (8, _LW)), jnp.broadcast_to(im[7:8, :], (8, _LW)))

            cr, ci = lax.fori_loop(0, tm // 8, blk, (car_ref[:, lre], car_ref[:, lim]))
            car_ref[:, lre] = cr
            car_ref[:, lim] = ci
            cols = slice(c * _UC, (c + 1) * _UC)
            ys = jnp.dot(x_ref[:, c * _CS:(c + 1) * _CS].astype(BF16), cm_ref[c], preferred_element_type=F32)
            ys = ys + d_ref[:, cols] * u[:, cols]
            ys_ref[:, cols] = ys
            yg_ref[:, cols] = _gelu(ys).astype(BF16)

    return pl.pallas_call(
        body, name="ssm_fwd",
        out_shape=(jax.ShapeDtypeStruct((S, 2 * NST), F32), jax.ShapeDtypeStruct((S, CW), F32),
                   jax.ShapeDtypeStruct((S, CW), BF16)),
        grid=(S // tm,),
        in_specs=[pl.BlockSpec((tm, CW), lambda i: (i, 2)), _full((_NCL, _UC, _CS)), _full((_NCL, _CS, _UC)),
                  _full((1, CW)), _full((64, NST))],
        out_specs=(pl.BlockSpec((tm, 2 * NST), lambda i: (i, 0)), pl.BlockSpec((tm, CW), lambda i: (i, 0)),
                   pl.BlockSpec((tm, CW), lambda i: (i, 0))),
        scratch_shapes=[pltpu.VMEM((8, 2 * NST), F32)],
        compiler_params=_cp(("arbitrary",)))(z, bb, cm, d, tab)


def _ssm_bwd(dzz, w_glu, ys, z, xs, cmt, bbt, d, tab, after):
    S = z.shape[0]
    tm = min(256, S)
    nt = S // tm
    tdims = (((0,), (0,)), ((), ()))

    def body(dzz_ref, wglu_ref, ys_ref, u_ref, x_ref, cmt_ref, bbt_ref, d_ref, t_ref, after_ref,
             du_ref, de_ref, dd_ref, dc_hbm, dbb_hbm, car_ref, lam_ref, dc_ref, dbb_ref):
        i = pl.program_id(0)

        @pl.when(i == 0)
        def _():
            car_ref[...] = jnp.zeros_like(car_ref)
            de_ref[...] = jnp.zeros_like(de_ref)
            dd_ref[...] = jnp.zeros_like(dd_ref)
            dc_ref[...] = jnp.zeros_like(dc_ref)
            dbb_ref[...] = jnp.zeros_like(dbb_ref)

        u = u_ref[...]
        u16 = u.astype(BF16)
        dyg = lax.dot_general(dzz_ref[...], wglu_ref[...], (((1,), (1,)), ((), ())), preferred_element_type=F32)
        dys = dyg * _gelu_grad(ys_ref[...])
        dys16 = dys.astype(BF16)
        dd_ref[...] += _colsum8(dys * u)
        row = lax.broadcasted_iota(jnp.int32, (8, _LW), 0)
        for c in range(_NCL):
            lre = pl.ds(c * _CS, _LW)
            lim = pl.ds(c * _CS + _LW, _LW)
            tl = pl.ds(c * _LW, _LW)
            cols = slice(c * _UC, (c + 1) * _UC)
            span = slice(c * _CS, (c + 1) * _CS)
            dc_ref[cols, :] += lax.dot_general(dys16[:, cols], x_ref[:, span].astype(BF16), tdims,
                                               preferred_element_type=F32)
            lam_ref[...] = jnp.dot(dys16[:, cols], cmt_ref[c], preferred_element_type=F32)

            def blk(jj, car):
                cr, ci, ar, ai = car
                j = tm // 8 - 1 - jj
                rows = pl.ds(pl.multiple_of(j * 8, 8), 8)
                r = lam_ref[rows, 0:_LW]
                im = lam_ref[rows, _LW:_CS]
                for lvl, s in enumerate((1, 2, 4)):
                    mr = t_ref[16 * lvl:16 * lvl + 8, tl]
                    mi = t_ref[16 * lvl + 8:16 * lvl + 16, tl]
                    sr = pltpu.roll(r, 8 - s, 0)
                    si = pltpu.roll(im, 8 - s, 0)
                    r, im = r + (mr * sr - mi * si), im + (mr * si + mi * sr)
                pr = t_ref[48:56, tl]
                pi_ = t_ref[56:64, tl]
                r, im = r + (pr * cr - pi_ * ci), im + (pr * ci + pi_ * cr)
                lam_ref[rows, 0:_LW] = r
                lam_ref[rows, _LW:_CS] = im
                nr = jnp.where(row == 7, cr, pltpu.roll(r, 7, 0))
                ni = jnp.where(row == 7, ci, pltpu.roll(im, 7, 0))
                xr = x_ref[rows, lre]
                xi = x_ref[rows, lim]
                ar = ar + (nr * xr + ni * xi)
                ai = ai + (ni * xr - nr * xi)
                return (jnp.broadcast_to(r[0:1, :], (8, _LW)), jnp.broadcast_to(im[0:1, :], (8, _LW)), ar, ai)

            zero = jnp.zeros((8, _LW), F32)
            cr, ci, ar, ai = lax.fori_loop(0, tm // 8, blk, (car_ref[:, lre], car_ref[:, lim], zero, zero))
            car_ref[:, lre] = cr
            car_ref[:, lim] = ci
            de_ref[0:8, tl] += ar
            de_ref[8:16, tl] += ai
            lam16 = lam_ref[...].astype(BF16)
            dbb_ref[cols, :] += lax.dot_general(u16[:, cols], lam16, tdims, preferred_element_type=F32)
            du = jnp.dot(lam16, bbt_ref[c], preferred_element_type=F32) + dys[:, cols] * d_ref[:, cols]
            du_ref[:, cols] = du.astype(BF16)

        @pl.when(i == nt - 1)
        def _():
            pltpu.sync_copy(dc_ref, dc_hbm)
            pltpu.sync_copy(dbb_ref, dbb_hbm)

    rev = lambda i: (nt - 1 - i, 0)
    once = lambda shape: pl.BlockSpec(shape, lambda *_: (0,) * len(shape), pipeline_mode=pl.Buffered(1))
    cross = jax.ShapeDtypeStruct((CW, _CS), F32)
    return pl.pallas_call(
        body, name="ssm_bwd",
        out_shape=(jax.ShapeDtypeStruct((S, CW), BF16), jax.ShapeDtypeStruct((16, NST), F32),
                   jax.ShapeDtypeStruct((8, CW), F32), cross, cross),
        grid=(nt,),
        in_specs=[pl.BlockSpec((tm, 2 * D_MODEL), rev), once((CW, 2 * D_MODEL)), pl.BlockSpec((tm, CW), rev),
                  pl.BlockSpec((tm, CW), lambda i: (nt - 1 - i, 2)), pl.BlockSpec((tm, 2 * NST), rev),
                  once((_NCL, _UC, _CS)), once((_NCL, _CS, _UC)), _full((1, CW)), once((64, NST)),
                  pl.BlockSpec(memory_space=pl.ANY)],
        out_specs=(pl.BlockSpec((tm, CW), rev), _full((16, NST)), _full((8, CW)),
                   pl.BlockSpec(memory_space=pl.ANY), pl.BlockSpec(memory_space=pl.ANY)),
        scratch_shapes=[pltpu.VMEM((8, 2 * NST), F32), pltpu.VMEM((tm, _CS), F32),
                        pltpu.VMEM((CW, _CS), F32), pltpu.VMEM((CW, _CS), F32)],
        compiler_params=_cp(("arbitrary",)))(dzz, w_glu, ys, z, xs, cmt, bbt, d, tab, after)


def _ssm_prep(a_re, a_im, b_re, b_im, log_dt):
    dt = jnp.exp(log_dt.reshape(G))[:, None]
    mag = jnp.exp(dt * a_re)
    e_re, e_im = mag * jnp.cos(dt * a_im), mag * jnp.sin(dt * a_im)
    n_re, n_im = e_re - 1.0, e_im
    den = a_re * a_re + a_im * a_im
    q_re = (n_re * a_re + n_im * a_im) / den
    q_im = (n_im * a_re - n_re * a_im) / den
    bb_re = q_re[..., None] * b_re - q_im[..., None] * b_im
    bb_im = q_re[..., None] * b_im + q_im[..., None] * b_re
    return e_re, e_im, bb_re, bb_im


def _scan_tables(e_re, e_im, reverse):
    er = e_re.reshape(1, NST)
    ei = e_im.reshape(1, NST)
    if reverse:
        ei = -ei
    pows = [(er, ei)]
    for _ in range(7):
        pr, pi_ = pows[-1]
        pows.append((pr * er - pi_ * ei, pr * ei + pi_ * er))
    row = jnp.arange(8)[:, None]
    out = []
    for s in (1, 2, 4):
        pr, pi_ = pows[s - 1]
        keep = (row + s <= 7) if reverse else (row >= s)
        out += [jnp.where(keep, pr, 0.0), jnp.where(keep, pi_, 0.0)]
    allr = jnp.concatenate([p[0] for p in pows], 0)
    alli = jnp.concatenate([p[1] for p in pows], 0)
    if reverse:
        allr, alli = allr[::-1], alli[::-1]
    out += [allr, alli]
    return jnp.concatenate(out, 0).astype(F32)


def _block_diag_mats(bb_re, bb_im, c_re, c_im):
    gc = G // _NCL
    eye = jnp.eye(gc, dtype=F32)
    bre = jnp.einsum("cjph,jk->cjhkp", bb_re.reshape(_NCL, gc, P, H), eye).reshape(_NCL, _UC, _LW)
    bim = jnp.einsum("cjph,jk->cjhkp", bb_im.reshape(_NCL, gc, P, H), eye).reshape(_NCL, _UC, _LW)
    bb = jnp.concatenate([bre, bim], 2)
    cre = jnp.einsum("cjhp,jk->cjpkh", c_re.reshape(_NCL, gc, H, P), eye).reshape(_NCL, _LW, _UC)
    cim = jnp.einsum("cjhp,jk->cjpkh", c_im.reshape(_NCL, gc, H, P), eye).reshape(_NCL, _LW, _UC)
    cm = jnp.concatenate([cre, -cim], 1)
    return bb, cm


def _diag_blocks(cross, imag):
    gc = G // _NCL
    off = _LW if imag else 0
    return jnp.stack([cross[H * g:H * (g + 1), off + P * (g % gc):off + P * (g % gc + 1)] for g in range(G)])


def _mix_fwd(scv, yg, z, x, w_cp, w_glu, w_out, g1, n2g, sc2, sh2):
    S = z.shape[0]
    tm = min(256, S)
    D = D_MODEL

    def body(s_ref, yg_ref, glc0_ref, glc1_ref, gls0_ref, gls1_ref, x_ref, wcp_ref, wglu_ref, wout_ref,
             g1_ref, n2_ref, sc_ref, sh_ref, yc_ref, zz_ref, m_ref, o_ref, x2_ref, h2_ref):
        y_conv = jnp.dot(s_ref[...], wcp_ref[...], preferred_element_type=F32)
        zz = jnp.dot(yg_ref[...], wglu_ref[...], preferred_element_type=F32)
        yc_ref[...] = y_conv.astype(BF16)
        zz_ref[...] = zz.astype(BF16)
        for half, (glc_ref, gls_ref) in enumerate(((glc0_ref, gls0_ref), (glc1_ref, gls1_ref))):
            lo, hi = half * CW, (half + 1) * CW
            y_ssm = zz[:, lo:hi] * _sig(zz[:, D + lo:D + hi])
            m_ref[:, lo:hi] = (_sig(glc_ref[...]) * y_conv[:, lo:hi] + _sig(gls_ref[...]) * y_ssm).astype(BF16)
        o = jnp.dot(m_ref[...], wout_ref[...], preferred_element_type=F32)
        o_ref[...] = o.astype(BF16)
        xv = x_ref[...] + g1_ref[...] * o
        x2_ref[...] = xv
        r = lax.rsqrt(jnp.mean(xv * xv, axis=-1, keepdims=True) + EPS)
        h2_ref[...] = (xv * r * (n2_ref[...] * (1.0 + sc_ref[...])) + sh_ref[...]).astype(BF16)

    zb_ = lambda j: pl.BlockSpec((tm, CW), lambda i: (i, j))
    row = lambda w: pl.BlockSpec((tm, w), lambda i: (i, 0))
    par = _full((1, D))
    bf = lambda w: jax.ShapeDtypeStruct((S, w), BF16)
    return pl.pallas_call(
        body, name="mix_fwd",
        out_shape=(bf(D), bf(2 * D), bf(D), bf(D), jax.ShapeDtypeStruct((S, D), F32), bf(D)),
        grid=(S // tm,),
        in_specs=[row(CW), row(CW), zb_(3), zb_(4), zb_(5), zb_(6), row(D), _full((CW, D)), _full((CW, 2 * D)),
                  _full((D, D)), par, par, par, par],
        out_specs=(row(D), row(2 * D), row(D), row(D), row(D), row(D)),
        compiler_params=_cp(("parallel",)))(scv, yg, z, z, z, z, x, w_cp, w_glu, w_out, g1, n2g, sc2, sh2)


def _mix_bwd(do, w_out, z, zz, y_conv, after):
    S = z.shape[0]
    tm = min(256, S)
    D = D_MODEL

    def body(do_ref, w_ref, glc0_ref, glc1_ref, gls0_ref, gls1_ref, za_ref, zb_ref, yc_ref, after_ref,
             dyc_ref, dgl_ref, dzz_ref):
        dm = lax.dot_general(do_ref[...], w_ref[...], (((1,), (1,)), ((), ())), preferred_element_type=F32)
        for half, (glc_ref, gls_ref) in enumerate(((glc0_ref, gls0_ref), (glc1_ref, gls1_ref))):
            lo, hi = half * CW, (half + 1) * CW
            dm_v = dm[:, lo:hi]
            sgc = _sig(glc_ref[...])
            sgs = _sig(gls_ref[...])
            szb = _sig(zb_ref[:, lo:hi].astype(F32))
            za = za_ref[:, lo:hi].astype(F32)
            dyc_ref[:, lo:hi] = (dm_v * sgc).astype(BF16)
            dgl_ref[:, lo:hi] = (dm_v * yc_ref[:, lo:hi].astype(F32) * sgc * (1.0 - sgc)).astype(BF16)
            dys = dm_v * sgs
            dgl_ref[:, D + lo:D + hi] = (dys * (za * szb) * (1.0 - sgs)).astype(BF16)
            dzz_ref[:, lo:hi] = (dys * szb).astype(BF16)
            dzz_ref[:, D + lo:D + hi] = (dys * za * szb * (1.0 - szb)).astype(BF16)

    zb_ = lambda j: pl.BlockSpec((tm, CW), lambda i: (i, j))
    wide = lambda j: pl.BlockSpec((tm, D), lambda i: (i, j))
    return pl.pallas_call(
        body, name="mix_bwd",
        out_shape=(jax.ShapeDtypeStruct((S, D), BF16), jax.ShapeDtypeStruct((S, 2 * D), BF16),
                   jax.ShapeDtypeStruct((S, 2 * D), BF16)),
        grid=(S // tm,),
        in_specs=[wide(0), _full((D, D)), zb_(3), zb_(4), zb_(5), zb_(6), wide(0), wide(1), wide(0),
                  pl.BlockSpec(memory_space=pl.ANY)],
        out_specs=(wide(0), pl.BlockSpec((tm, 2 * D), lambda i: (i, 0)), pl.BlockSpec((tm, 2 * D), lambda i: (i, 0))),
        compiler_params=_cp(("parallel",)))(do, w_out, z, z, z, z, zz, zz, y_conv, after)


_FC = 1408


def _ffn_in_act(h2, w_fi):
    S, D = h2.shape
    tm = min(256, S)

    def body(h_ref, w_ref, f_ref, a_ref):
        hv = h_ref[...]
        for c in range(FH // _FC):
            lo, hi = c * _FC, (c + 1) * _FC
            g = jnp.dot(hv, w_ref[:, lo:hi], preferred_element_type=F32)
            u = jnp.dot(hv, w_ref[:, FH + lo:FH + hi], preferred_element_type=F32)
            f_ref[:, lo:hi] = g.astype(BF16)
            f_ref[:, FH + lo:FH + hi] = u.astype(BF16)
            a_ref[:, lo:hi] = (g * _sig(g) * u).astype(BF16)

    return pl.pallas_call(
        body, name="ffn_in_act",
        out_shape=(jax.ShapeDtypeStruct((S, 2 * FH), BF16), jax.ShapeDtypeStruct((S, FH), BF16)),
        grid=(S // tm,),
        in_specs=[pl.BlockSpec((tm, D), lambda i: (i, 0)), _full((D, 2 * FH))],
        out_specs=(pl.BlockSpec((tm, 2 * FH), lambda i: (i, 0)), pl.BlockSpec((tm, FH), lambda i: (i, 0))),
        compiler_params=_cp(("parallel",)))(h2, w_fi)


def _ffn_bwd(do2, w_fo, f, after):
    S, D = do2.shape
    tm = min(256, S)

    def body(d_ref, w_ref, f_ref, after_ref, df_ref):
        dv = d_ref[...]
        for c in range(FH // _FC):
            lo, hi = c * _FC, (c + 1) * _FC
            dact = lax.dot_general(dv, w_ref[lo:hi, :], (((1,), (1,)), ((), ())), preferred_element_type=F32)
            g = f_ref[:, lo:hi].astype(F32)
            u = f_ref[:, FH + lo:FH + hi].astype(F32)
            sg = _sig(g)
            df_ref[:, lo:hi] = (dact * u * (sg * (1.0 + g * (1.0 - sg)))).astype(BF16)
            df_ref[:, FH + lo:FH + hi] = (dact * g * sg).astype(BF16)

    return pl.pallas_call(
        body, name="ffn_bwd", out_shape=jax.ShapeDtypeStruct((S, 2 * FH), BF16), grid=(S // tm,),
        in_specs=[pl.BlockSpec((tm, D), lambda i: (i, 0)), _full((FH, D)),
                  pl.BlockSpec((tm, 2 * FH), lambda i: (i, 0)), pl.BlockSpec(memory_space=pl.ANY)],
        out_specs=pl.BlockSpec((tm, 2 * FH), lambda i: (i, 0)),
        compiler_params=_cp(("parallel",)))(do2, w_fo, f, after)


def _ffn_out_final(x2, act, w_fo, g2, fg, tgt):
    S, D = x2.shape
    tm = min(256, S)

    def body(x2_ref, a_ref, w_ref, g2_ref, fg_ref, t_ref, dx3_ref, do2_ref, ls_ref, dfg_ref, dg2_ref):
        i = pl.program_id(0)
        o2 = jnp.dot(a_ref[...], w_ref[...], preferred_element_type=F32)
        x3 = x2_ref[...] + g2_ref[...] * o2
        r = lax.rsqrt(jnp.mean(x3 * x3, axis=-1, keepdims=True) + EPS)
        xn = x3 * r
        err = xn * fg_ref[...] - t_ref[...]
        dy = err * (1.0 / D)
        dxn = dy * fg_ref[...]
        dx3 = r * (dxn - xn * jnp.mean(dxn * xn, axis=-1, keepdims=True))
        dx3_ref[...] = dx3
        do2_ref[...] = (dx3 * g2_ref[...]).astype(BF16)

        @pl.when(i == 0)
        def _():
            ls_ref[...] = jnp.zeros_like(ls_ref)
            dfg_ref[...] = jnp.zeros_like(dfg_ref)
            dg2_ref[...] = jnp.zeros_like(dg2_ref)

        e2 = _colsum8(err * err)
        lanes = e2[:, 0:128]
        for q in range(1, D // 128):
            lanes = lanes + e2[:, q * 128:(q + 1) * 128]
        ls_ref[...] += lanes * (0.5 / D)
        dfg_ref[...] += _colsum8(dy * xn)
        dg2_ref[...] += _colsum8(dx3 * o2)

    row = pl.BlockSpec((tm, D), lambda i: (i, 0))
    par = _full((1, D))
    return pl.pallas_call(
        body, name="final_loss",
        out_shape=(jax.ShapeDtypeStruct((S, D), F32), jax.ShapeDtypeStruct((S, D), BF16),
                   jax.ShapeDtypeStruct((8, 128), F32), jax.ShapeDtypeStruct((8, D), F32),
                   jax.ShapeDtypeStruct((8, D), F32)),
        grid=(S // tm,), in_specs=[row, pl.BlockSpec((tm, FH), lambda i: (i, 0)), _full((FH, D)), par, par, row],
        out_specs=(row, row, _full((8, 128)), _full((8, D)), _full((8, D))),
        compiler_params=_cp(("arbitrary",)))(x2, act, w_fo, g2, fg, tgt)


def _normmod_bwd(dsrc, w, xin, dres, g, sc, gate, o, after, name):
    S, D = xin.shape
    K = dsrc.shape[1]
    tm = min(256, S)

    def body(ds_ref, w_ref, x_ref, dr_ref, g_ref, sc_ref, gate_ref, o_ref, after_ref,
             dx_ref, do_ref, dsh_ref, dsc_ref, dg_ref, dgate_ref):
        i = pl.program_id(0)
        xv = x_ref[...]
        r = lax.rsqrt(jnp.mean(xv * xv, axis=-1, keepdims=True) + EPS)
        xn = xv * r
        dh_v = lax.dot_general(ds_ref[...], w_ref[...], (((1,), (1,)), ((), ())), preferred_element_type=F32)
        gv = g_ref[...]
        scale = 1.0 + sc_ref[...]
        dxn = dh_v * (gv * scale)
        dx = dr_ref[...] + r * (dxn - xn * jnp.mean(dxn * xn, axis=-1, keepdims=True))
        dx_ref[...] = dx
        do_ref[...] = (dx * gate_ref[...]).astype(BF16)

        @pl.when(i == 0)
        def _():
            dsh_ref[...] = jnp.zeros_like(dsh_ref)
            dsc_ref[...] = jnp.zeros_like(dsc_ref)
            dg_ref[...] = jnp.zeros_like(dg_ref)
            dgate_ref[...] = jnp.zeros_like(dgate_ref)

        hx = dh_v * xn
        dsh_ref[...] += _colsum8(dh_v)
        dsc_ref[...] += _colsum8(hx) * gv
        dg_ref[...] += _colsum8(hx) * scale
        dgate_ref[...] += _colsum8(dx * o_ref[...])

    row = pl.BlockSpec((tm, D), lambda i: (i, 0))
    par = _full((1, D))
    acc = jax.ShapeDtypeStruct((8, D), F32)
    return pl.pallas_call(
        body, name=name,
        out_shape=(jax.ShapeDtypeStruct((S, D), F32), jax.ShapeDtypeStruct((S, D), BF16), acc, acc, acc, acc),
        grid=(S // tm,),
        in_specs=[pl.BlockSpec((tm, K), lambda i: (i, 0)), _full((D, K)), row, row, par, par, par, row,
                  pl.BlockSpec(memory_space=pl.ANY)],
        out_specs=(row, row, _full((8, D)), _full((8, D)), _full((8, D)), _full((8, D))),
        compiler_params=_cp(("arbitrary",)))(dsrc, w, xin, dres, g, sc, gate, o, after)


def _me():
    return lax.axis_index("x"), lax.axis_index("y"), lax.axis_index("c")


def _allgather8(v, name, after=()):
    R, C = v.shape
    after = list(after)

    def body(v_ref, *rest):
        out_ref, send_sems, recv_sems, local_sem = rest[len(after):]
        x, y, c = _me()
        mine = pltpu.make_async_copy(v_ref, out_ref.at[4 * x + 2 * y + c], local_sem)
        mine.start()
        copies = []
        for k in range(1, N_DEV):
            fx, fy, fc = (k >> 2) & 1, (k >> 1) & 1, k & 1
            peer = (x ^ fx, y ^ fy, c ^ fc)
            copies.append(pltpu.make_async_remote_copy(
                src_ref=v_ref, dst_ref=out_ref.at[4 * x + 2 * y + c],
                send_sem=send_sems.at[k - 1], recv_sem=recv_sems.at[k - 1],
                device_id=peer, device_id_type=MESH))
        for cp in copies:
            cp.start()
        for k in range(1, N_DEV):
            fx, fy, fc = (k >> 2) & 1, (k >> 1) & 1, k & 1
            src_slot = 4 * (x ^ fx) + 2 * (y ^ fy) + (c ^ fc)
            pltpu.make_async_remote_copy(
                src_ref=v_ref, dst_ref=out_ref.at[src_slot],
                send_sem=send_sems.at[k - 1], recv_sem=recv_sems.at[k - 1],
                device_id=(x ^ fx, y ^ fy, c ^ fc), device_id_type=MESH).wait_recv()
        for cp in copies:
            cp.wait_send()
        mine.wait()

    return pl.pallas_call(
        body, name=name, out_shape=jax.ShapeDtypeStruct((N_DEV, R, C), v.dtype),
        in_specs=[pl.BlockSpec(memory_space=pltpu.VMEM)] + [pl.BlockSpec(memory_space=pl.ANY)] * len(after),
        out_specs=pl.BlockSpec(memory_space=pltpu.VMEM),
        scratch_shapes=[pltpu.SemaphoreType.DMA((N_DEV - 1,)), pltpu.SemaphoreType.DMA((N_DEV - 1,)),
                        pltpu.SemaphoreType.DMA],
        compiler_params=pltpu.CompilerParams(vmem_limit_bytes=VMEM_LIMIT))(v, *after)


def _swap_sibling(arrs):
    nw = len(arrs)

    def body(*refs):
        ins, outs = refs[:nw], refs[nw:2 * nw]
        send_sems, recv_sems = refs[2 * nw:]
        x, y, c = _me()
        copies = [pltpu.make_async_remote_copy(
            src_ref=ins[w], dst_ref=outs[w], send_sem=send_sems.at[w], recv_sem=recv_sems.at[w],
            device_id=(x, y, 1 - c), device_id_type=MESH) for w in range(nw)]
        for cp in copies:
            cp.start()
        for cp in copies:
            cp.wait_recv()
        for cp in copies:
            cp.wait_send()

    hbm = pl.BlockSpec(memory_space=pltpu.HBM)
    return pl.pallas_call(
        body, name="swap_sibling", out_shape=tuple(jax.ShapeDtypeStruct(a.shape, a.dtype) for a in arrs),
        in_specs=[hbm] * nw, out_specs=tuple([hbm] * nw),
        scratch_shapes=[pltpu.SemaphoreType.DMA((nw,)), pltpu.SemaphoreType.DMA((nw,))],
        compiler_params=pltpu.CompilerParams(vmem_limit_bytes=VMEM_LIMIT))(*arrs)


_HBM = pl.BlockSpec(memory_space=pltpu.HBM)
_SEM = pl.BlockSpec(memory_space=pltpu.SEMAPHORE)
_EFFECT = pltpu.SideEffectType.DATAFLOW_SIDE_EFFECTING
_N_PEER = N_CHIP - 1


def _chip_part(ref, axis, n, chip):
    start = pl.multiple_of(chip * n, 8)
    return ref.at[pl.ds(start, n), :] if axis == 0 else ref.at[:, pl.ds(start, n)]


def _gather_copy(k, src_ref, land_ref, send_sems, recv_sems, axis, arriving):
    x, y, c = _me()
    px, py = x ^ ((k >> 1) & 1), y ^ (k & 1)
    chip = 2 * px + py if arriving else 2 * x + y
    return pltpu.make_async_remote_copy(
        src_ref=src_ref, dst_ref=_chip_part(land_ref, axis, src_ref.shape[axis], chip),
        send_sem=send_sems.at[k - 1], recv_sem=recv_sems.at[k - 1], device_id=(px, py, c), device_id_type=MESH)


def _scatter_copy(k, grad_ref, land_ref, send_sems, recv_sems, axis):
    x, y, c = _me()
    px, py = x ^ ((k >> 1) & 1), y ^ (k & 1)
    return pltpu.make_async_remote_copy(
        src_ref=_chip_part(grad_ref, axis, grad_ref.shape[axis] // N_CHIP, 2 * px + py), dst_ref=land_ref.at[k - 1],
        send_sem=send_sems.at[k - 1], recv_sem=recv_sems.at[k - 1], device_id=(px, py, c), device_id_type=MESH)


def _own_copy(src_ref, land_ref, sends, axis):
    x, y, _ = _me()
    return pltpu.make_async_copy(src_ref, _chip_part(land_ref, axis, src_ref.shape[axis], 2 * x + y),
                                 sends.at[_N_PEER])


def _gather_start(shards, lands, axes, after):
    nw = len(shards)

    def body(*refs):
        srcs, zones = refs[:nw], refs[nw:2 * nw]
        sends, recvs = refs[2 * nw + 1:3 * nw + 1], refs[3 * nw + 1:4 * nw + 1]
        token = refs[-1]
        for w in range(nw):
            for k in range(1, N_CHIP):
                _gather_copy(k, srcs[w], zones[w], sends[w], recvs[w], axes[w], False).start()
        for w in range(nw):
            _own_copy(srcs[w], zones[w], sends[w], axes[w]).start()
        token[...] = jnp.zeros_like(token)

    outs = pl.pallas_call(
        body, name="gather_start",
        out_shape=tuple([pltpu.SemaphoreType.DMA((_N_PEER + 1,))] * nw + [pltpu.SemaphoreType.DMA((_N_PEER,))] * nw
                        + [pltpu.HBM(a.shape, a.dtype) for a in list(shards) + list(lands)]
                        + [jax.ShapeDtypeStruct((8, 128), F32)]),
        in_specs=[_HBM] * (2 * nw) + [pl.BlockSpec(memory_space=pl.ANY)],
        out_specs=tuple([_SEM] * (2 * nw) + [_HBM] * (2 * nw) + [pl.BlockSpec(memory_space=pltpu.VMEM)]),
        input_output_aliases={i: 2 * nw + i for i in range(2 * nw)},
        compiler_params=pltpu.CompilerParams(has_side_effects=_EFFECT),
    )(*([pltpu.with_memory_space_constraint(a, pltpu.HBM) for a in list(shards) + list(lands)] + [after]))
    per_weight = [(outs[w], outs[nw + w], outs[2 * nw + w], outs[3 * nw + w]) for w in range(nw)]
    return per_weight, outs[-1]


def _gather_wait(state, axis, after, name):
    send_sems, recv_sems, shard, land = state

    after = list(after) if isinstance(after, (list, tuple)) else [after]

    def body(src_ref, land_ref, sends, recvs, *rest):
        for k in range(1, N_CHIP):
            _gather_copy(k, src_ref, land_ref, sends, recvs, axis, False).wait_send()
            _gather_copy(k, src_ref, land_ref, sends, recvs, axis, True).wait_recv()
        _own_copy(src_ref, land_ref, sends, axis).wait()

    return pl.pallas_call(
        body, name=name, out_shape=(pltpu.HBM(shard.shape, shard.dtype), pltpu.HBM(land.shape, land.dtype)),
        in_specs=[_HBM, _HBM, _SEM, _SEM] + [pl.BlockSpec(memory_space=pl.ANY)] * len(after), out_specs=(_HBM, _HBM),
        input_output_aliases={0: 0, 1: 1},
        compiler_params=pltpu.CompilerParams(has_side_effects=_EFFECT),
    )(shard, land, send_sems, recv_sems, *after)[1]


def _all8_copy(k, v_ref, land_ref, send_sems, recv_sems, arriving):
    x, y, c = _me()
    px, py, pc = x ^ ((k >> 2) & 1), y ^ ((k >> 1) & 1), c ^ (k & 1)
    slot = 4 * px + 2 * py + pc if arriving else 4 * x + 2 * y + c
    return pltpu.make_async_remote_copy(
        src_ref=v_ref, dst_ref=land_ref.at[slot], send_sem=send_sems.at[k - 1], recv_sem=recv_sems.at[k - 1],
        device_id=(px, py, pc), device_id_type=MESH)


def _all8_own(v_ref, land_ref, send_sems):
    x, y, c = _me()
    return pltpu.make_async_copy(v_ref, land_ref.at[4 * x + 2 * y + c], send_sems.at[N_DEV - 1])


def _all8_start(v, name):
    land = lax.empty((N_DEV,) + v.shape, v.dtype)

    def body(v_ref, land_ref, sends, recvs, v_thru, land_thru, token):
        for k in range(1, N_DEV):
            _all8_copy(k, v_ref, land_ref, sends, recvs, False).start()
        _all8_own(v_ref, land_ref, sends).start()
        token[...] = jnp.zeros_like(token)

    outs = pl.pallas_call(
        body, name=name,
        out_shape=(pltpu.SemaphoreType.DMA((N_DEV,)), pltpu.SemaphoreType.DMA((N_DEV - 1,)),
                   pltpu.HBM(v.shape, v.dtype), pltpu.HBM(land.shape, land.dtype),
                   jax.ShapeDtypeStruct((8, 128), F32)),
        in_specs=[_HBM, _HBM], out_specs=(_SEM, _SEM, _HBM, _HBM, pl.BlockSpec(memory_space=pltpu.VMEM)),
        input_output_aliases={0: 2, 1: 3},
        compiler_params=pltpu.CompilerParams(has_side_effects=_EFFECT),
    )(pltpu.with_memory_space_constraint(v, pltpu.HBM), pltpu.with_memory_space_constraint(land, pltpu.HBM))
    return outs[:4], outs[4]


def _all8_wait(state, after, name):
    send_sems, recv_sems, v, land = state

    def body(v_ref, land_ref, sends, recvs, after_ref, v_dead, got_ref):
        for k in range(1, N_DEV):
            _all8_copy(k, v_ref, land_ref, sends, recvs, False).wait_send()
            _all8_copy(k, v_ref, land_ref, sends, recvs, True).wait_recv()
        _all8_own(v_ref, land_ref, sends).wait()

    return pl.pallas_call(
        body, name=name, out_shape=(pltpu.HBM(v.shape, v.dtype), pltpu.HBM(land.shape, land.dtype)),
        in_specs=[_HBM, _HBM, _SEM, _SEM, pl.BlockSpec(memory_space=pl.ANY)], out_specs=(_HBM, _HBM),
        input_output_aliases={0: 0, 1: 1},
        compiler_params=pltpu.CompilerParams(has_side_effects=_EFFECT),
    )(v, land, send_sems, recv_sems, after)[1]


def _swap_copy(w, src_ref, land_ref, send_sems, recv_sems):
    x, y, c = _me()
    return pltpu.make_async_remote_copy(src_ref=src_ref, dst_ref=land_ref, send_sem=send_sems.at[w],
                                        recv_sem=recv_sems.at[w], device_id=(x, y, 1 - c), device_id_type=MESH)


def _swap_start(arrs, after, name):
    nw = len(arrs)
    lands = [lax.empty(a.shape, a.dtype) for a in arrs]

    def body(*refs):
        srcs, zones = refs[:nw], refs[nw:2 * nw]
        sends, recvs = refs[2 * nw + 1], refs[2 * nw + 2]
        for w in range(nw):
            _swap_copy(w, srcs[w], zones[w], sends, recvs).start()
        refs[-1][...] = jnp.zeros_like(refs[-1])

    sem = pltpu.SemaphoreType.DMA((nw,))
    outs = pl.pallas_call(
        body, name=name,
        out_shape=tuple([sem, sem] + [pltpu.HBM(a.shape, a.dtype) for a in list(arrs) + lands]
                        + [jax.ShapeDtypeStruct((8, 128), F32)]),
        in_specs=[_HBM] * (2 * nw) + [pl.BlockSpec(memory_space=pl.ANY)],
        out_specs=tuple([_SEM, _SEM] + [_HBM] * (2 * nw) + [pl.BlockSpec(memory_space=pltpu.VMEM)]),
        input_output_aliases={i: 2 + i for i in range(2 * nw)},
        compiler_params=pltpu.CompilerParams(has_side_effects=_EFFECT),
    )(*([pltpu.with_memory_space_constraint(a, pltpu.HBM) for a in list(arrs) + lands] + [after]))
    return (outs[0], outs[1], outs[2:2 + nw], outs[2 + nw:2 + 2 * nw]), outs[-1]


def _swap_wait(state, after, name):
    send_sems, recv_sems, arrs, lands = state
    nw = len(arrs)

    def body(*refs):
        srcs, zones = refs[:nw], refs[nw:2 * nw]
        sends, recvs = refs[2 * nw], refs[2 * nw + 1]
        for w in range(nw):
            cp = _swap_copy(w, srcs[w], zones[w], sends, recvs)
            cp.wait_send()
            cp.wait_recv()

    outs = pl.pallas_call(
        body, name=name, out_shape=tuple(pltpu.HBM(a.shape, a.dtype) for a in list(arrs) + list(lands)),
        in_specs=[_HBM] * (2 * nw) + [_SEM, _SEM, pl.BlockSpec(memory_space=pl.ANY)],
        out_specs=tuple([_HBM] * (2 * nw)),
        input_output_aliases={i: i for i in range(2 * nw)},
        compiler_params=pltpu.CompilerParams(has_side_effects=_EFFECT),
    )(*arrs, *lands, send_sems, recv_sems, after)
    return list(outs[:nw]), list(outs[nw:])


def _scatter_start(grad, axis, name):
    shp = list(grad.shape)
    shp[axis] //= N_CHIP
    land = lax.empty((_N_PEER,) + tuple(shp), grad.dtype)

    def body(grad_ref, land_ref, sends, recvs, grad_thru, land_thru, token):
        for k in range(1, N_CHIP):
            _scatter_copy(k, grad_ref, land_ref, sends, recvs, axis).start()
        token[...] = jnp.zeros_like(token)

    sem = pltpu.SemaphoreType.DMA((_N_PEER,))
    outs = pl.pallas_call(
        body, name=name,
        out_shape=(sem, sem, pltpu.HBM(grad.shape, grad.dtype), pltpu.HBM(land.shape, land.dtype),
                   jax.ShapeDtypeStruct((8, 128), F32)),
        in_specs=[_HBM, _HBM], out_specs=(_SEM, _SEM, _HBM, _HBM, pl.BlockSpec(memory_space=pltpu.VMEM)),
        input_output_aliases={0: 2, 1: 3},
        compiler_params=pltpu.CompilerParams(has_side_effects=_EFFECT),
    )(pltpu.with_memory_space_constraint(grad, pltpu.HBM), pltpu.with_memory_space_constraint(land, pltpu.HBM))
    return outs[:4], outs[4]


def _scatter_wait(state, axis, after, name):
    send_sems, recv_sems, grad, land = state

    def body(grad_ref, land_ref, sends, recvs, after_ref, grad_dead, got_ref):
        for k in range(1, N_CHIP):
            cp = _scatter_copy(k, grad_ref, land_ref, sends, recvs, axis)
            cp.wait_send()
            cp.wait_recv()

    return pl.pallas_call(
        body, name=name, out_shape=(pltpu.HBM(grad.shape, grad.dtype), pltpu.HBM(land.shape, land.dtype)),
        in_specs=[_HBM, _HBM, _SEM, _SEM, pl.BlockSpec(memory_space=pl.ANY)], out_specs=(_HBM, _HBM),
        input_output_aliases={0: 0, 1: 1},
        compiler_params=pltpu.CompilerParams(has_side_effects=_EFFECT),
    )(grad, land, send_sems, recv_sems, after)[1]


_C1 = 1.0 - B1 ** STEP
_C2 = 1.0 - B2 ** STEP


def _adam_math(w, g, m, v):
    m = B1 * m + (1.0 - B1) * g
    v = B2 * v + (1.0 - B2) * (g * g)
    delta = -LR * ((m / _C1) / (jnp.sqrt(v / _C2) + AEPS) + WD * w)
    return delta, m, v


def _adamw(w, m, v, groups, name):
    R, C = w.shape
    tr = R if R <= 256 else (128 if R % 128 == 0 else 176)
    assert R % tr == 0, (name, R)
    gparts = [p for grp in groups for p in grp]
    sizes = [len(grp) for grp in groups]
    ng = len(gparts)

    def body(*refs):
        w_ref, m_ref, v_ref = refs[:3]
        g_refs = list(refs[3:3 + ng])
        g_out, d_out, m_out, v_out = refs[3 + ng:]
        g = None
        for size in sizes:
            s = None
            for r in [g_refs.pop(0) for _ in range(size)]:
                terms = [r[q] for q in range(r.shape[0])] if len(r.shape) == 3 else [r[...]]
                for t in terms:
                    s = t.astype(F32) if s is None else s + t.astype(F32)
            g = s if g is None else g + s
        delta, mn, vn = _adam_math(w_ref[...], g, m_ref[...], v_ref[...])
        g_out[...] = g
        d_out[...] = delta
        m_out[...] = mn
        v_out[...] = vn

    blk = pl.BlockSpec((tr, C), lambda i: (i, 0))
    g_specs = [blk if p.ndim == 2 else pl.BlockSpec((p.shape[0], tr, C), lambda i: (0, i, 0)) for p in gparts]
    sds = jax.ShapeDtypeStruct((R, C), F32)
    return pl.pallas_call(
        body, name=name, out_shape=(sds, sds, sds, sds), grid=(R // tr,),
        in_specs=[blk, blk, blk] + g_specs, out_specs=(blk, blk, blk, blk),
        compiler_params=_cp(("parallel",)))(w, m, v, *gparts)


def _mod_shard(c_all, w_ada, b_ada_cols):
    n = w_ada.shape[1]
    tn = 512

    def body(c_ref, w_ref, b_ref, o_ref):
        cv = c_ref[...]
        ca = (cv * _sig(cv)).astype(BF16)
        o_ref[...] = jnp.dot(ca, w_ref[...].astype(BF16), preferred_element_type=F32) + b_ref[...]

    return pl.pallas_call(
        body, name="mod_shard", out_shape=jax.ShapeDtypeStruct((N_DEV, n), F32), grid=(n // tn,),
        in_specs=[_full((N_DEV, D_MODEL)), pl.BlockSpec((D_MODEL, tn), lambda j: (0, j)),
                  pl.BlockSpec((1, tn), lambda j: (0, j))],
        out_specs=pl.BlockSpec((N_DEV, tn), lambda j: (0, j)),
        compiler_params=_cp(("parallel",)))(c_all, w_ada, b_ada_cols)


def _ada_grad(c_all, dmod_cols):
    n = dmod_cols.shape[1]
    tn = 512

    def body(c_ref, d_ref, o_ref):
        cv = c_ref[...]
        ca = cv * _sig(cv)
        o_ref[...] = lax.dot_general(ca, d_ref[...], (((0,), (0,)), ((), ())),
                                     preferred_element_type=F32, precision=lax.Precision.HIGHEST)

    return pl.pallas_call(
        body, name="ada_grad", out_shape=jax.ShapeDtypeStruct((D_MODEL, n), F32), grid=(n // tn,),
        in_specs=[_full((N_DEV, D_MODEL)), pl.BlockSpec((N_DEV, tn), lambda j: (0, j))],
        out_specs=pl.BlockSpec((D_MODEL, tn), lambda j: (0, j)),
        compiler_params=_cp(("parallel",)))(c_all, dmod_cols)


def _device_step(x, mod, W, tgt, getw, put, early):
    sh1, sc1, g1, sh2, sc2, g2 = [mod[:, i * D_MODEL:(i + 1) * D_MODEL] for i in range(6)]
    e_re, e_im, bb_re, bb_im = _ssm_prep(W["ssm_a_re"], W["ssm_a_im"], W["ssm_b_re"], W["ssm_b_im"], W["ssm_log_dt"])
    bb, cm = _block_diag_mats(bb_re, bb_im, W["ssm_c_re"], W["ssm_c_im"])
    bb16, cm16 = bb.astype(BF16), cm.astype(BF16)
    bbt16, cmt16 = jnp.swapaxes(bb16, 1, 2), jnp.swapaxes(cm16, 1, 2)
    tab_f = _scan_tables(e_re, e_im, False)
    tab_b = _scan_tables(e_re, e_im, True)

    w_in = getw("w_in", [mod, bb16, cm16, bbt16, cmt16, tab_f, tab_b])
    h1, z = _in_proj(x, W["norm1_g"], sc1, sh1, w_in)
    yc, scv = _conv_fwd(z, W["conv_w"], W["conv_b"], W["conv_ln_g"], W["conv_ln_b"])
    xs, ys, yg = _ssm_fwd(z, bb16, cm16, W["ssm_d"], tab_f)
    w_cp, w_glu, w_out = getw("conv_proj", scv), getw("ssm_glu", yg), getw("w_out", yg)
    y_conv, zz, merged, o, x2, h2 = _mix_fwd(scv, yg, z, x, w_cp, w_glu, w_out, g1, W["norm2_g"], sc2, sh2)
    w_fi = getw("w_ffn_in", h2)
    f, act = _ffn_in_act(h2, w_fi)
    w_fo = getw("w_ffn_out", act)
    dx3, do2, loss8, dfg8, dg2_8 = _ffn_out_final(x2, act, w_fo, g2, W["final_g"], tgt)

    sm = {}
    tok = put("w_ffn_out", _matmul(act, do2, "tn", 1408, 1024, 2048, BF16, "mm_g_ffn_out"))
    df = _ffn_bwd(do2, w_fo, f, tok)
    tok = put("w_ffn_in", _matmul(h2, df, "tn", 1024, 1408, 2048, BF16, "mm_g_ffn_in"))
    dx2, do, dsh2, dsc2, dn2, dg1_8 = _normmod_bwd(df, w_fi, x2, dx3, W["norm2_g"], sc2, g1, o, tok, "d_h2_normmod2_bwd")
    tok = put("w_out", _matmul(merged, do, "tn", 1024, 1024, 4096, BF16, "mm_g_w_out"))
    dyconv, dgl, dzz = _mix_bwd(do, w_out, z, zz, y_conv, tok)
    tok = put("ssm_glu", _matmul(yg, dzz, "tn", 512, 1024, 4096, BF16, "mm_g_ssm_glu"))
    tok = put("conv_proj", _matmul(scv, dyconv, "tn", 512, 1024, 4096, BF16, "mm_g_conv_proj", after=tok))
    du, de16, dd8, dc_full, dbb_full = _ssm_bwd(dzz, w_glu, ys, z, xs, cmt16, bbt16, W["ssm_d"], tab_b, tok)
    dyc, dlg8, dlb8, dcb8 = _conv_bwd_ln(dyconv, w_cp, yc, W["conv_ln_g"], W["conv_ln_b"])
    dz_conv, dcw = _conv_bwd(dyc, z, W["conv_w"])

    s8 = lambda a: jnp.sum(a, axis=0, keepdims=True)
    de = de16.reshape(2, 8, NST).sum(1)
    de_re, de_im = de[0].reshape(G, P), de[1].reshape(G, P)
    dc_re = _diag_blocks(dc_full, False)
    dc_im = -_diag_blocks(dc_full, True)
    dbb_re = jnp.swapaxes(_diag_blocks(dbb_full, False), 1, 2)
    dbb_im = jnp.swapaxes(_diag_blocks(dbb_full, True), 1, 2)
    _, vjp = jax.vjp(_ssm_prep, W["ssm_a_re"], W["ssm_a_im"], W["ssm_b_re"], W["ssm_b_im"], W["ssm_log_dt"])
    sm["ssm_a_re"], sm["ssm_a_im"], sm["ssm_b_re"], sm["ssm_b_im"], sm["ssm_log_dt"] = vjp((de_re, de_im, dbb_re, dbb_im))
    sm["ssm_c_re"], sm["ssm_c_im"] = dc_re, dc_im
    sm["ssm_d"] = s8(dd8)
    sm["norm2_g"] = s8(dn2)
    sm["conv_b"], sm["conv_ln_g"], sm["conv_ln_b"] = s8(dcb8), s8(dlg8), s8(dlb8)
    sm["conv_w"] = dcw.reshape(KW, 8, CW).sum(1)
    sm["final_g"] = s8(dfg8)
    tok = early(sm)

    dz = jnp.concatenate([dz_conv, du, dgl], axis=1)
    tok = put("w_in", _matmul(h1, dz, "tn", 1024, 896, 4096, BF16, "mm_g_w_in", after=tok))
    dx, _, dsh1, dsc1, dn1, _ = _normmod_bwd(dz, w_in, x, dx2, W["norm1_g"], sc1, g1, o, tok, "d_h1_normmod1_bwd")
    dmod = jnp.concatenate([s8(dsh1), s8(dsc1), s8(dg1_8), s8(dsh2), s8(dsc2), s8(dg2_8)], axis=1)
    return loss8, dx, s8(dn1), dmod


_BIG = ("w_in", "conv_proj", "ssm_glu", "w_out", "w_ffn_in", "w_ffn_out")
_BIG_AXIS = {"w_in": 1, "conv_proj": 1, "ssm_glu": 1, "w_out": 0, "w_ffn_in": 1, "w_ffn_out": 0}
_EARLY = ("conv_w", "conv_b", "conv_ln_g", "conv_ln_b", "ssm_a_re", "ssm_a_im", "ssm_b_re", "ssm_b_im", "ssm_c_re",
          "ssm_c_im", "ssm_d", "ssm_log_dt", "norm2_g", "final_g")
_LATE = ("norm1_g", "b_ada")
_ORDER = ("w_ada", "b_ada", "norm1_g", "w_in", "conv_w", "conv_b", "conv_ln_g", "conv_ln_b", "conv_proj",
          "ssm_a_re", "ssm_a_im", "ssm_b_re", "ssm_b_im", "ssm_c_re", "ssm_c_im", "ssm_d", "ssm_log_dt", "ssm_glu",
          "w_out", "norm2_g", "w_ffn_in", "w_ffn_out", "final_g")
_PACK_COLS = 1024


def _pack_rows(shape):
    return -(-int(np.prod(shape)) // (8 * _PACK_COLS)) * 8


def _pack(arrs):
    parts = []
    for a in arrs:
        flat = a.reshape(-1)
        n = _pack_rows(a.shape)
        parts.append(jnp.pad(flat, (0, n * _PACK_COLS - flat.shape[0])).reshape(n, _PACK_COLS))
    return jnp.concatenate(parts, 0)


def _unpack(packed, shapes):
    out, r = [], 0
    for shp in shapes:
        size = int(np.prod(shp))
        n = _pack_rows(shp)
        out.append(packed[r:r + n].reshape(-1)[:size].reshape(shp))
        r += n
    return out


def kernel(x, c, w_ada, b_ada, norm1_g, w_in, conv_w, conv_b, conv_ln_g, conv_ln_b, conv_proj, ssm_a_re, ssm_a_im, ssm_b_re, ssm_b_im, ssm_c_re, ssm_c_im, ssm_d, ssm_log_dt, ssm_glu, w_out, norm2_g, w_ffn_in, w_ffn_out, final_g, loss_target, m_w_ada, m_b_ada, m_norm1_g, m_w_in, m_conv_w, m_conv_b, m_conv_ln_g, m_conv_ln_b, m_conv_proj, m_ssm_a_re, m_ssm_a_im, m_ssm_b_re, m_ssm_b_im, m_ssm_c_re, m_ssm_c_im, m_ssm_d, m_ssm_log_dt, m_ssm_glu, m_w_out, m_norm2_g, m_w_ffn_in, m_w_ffn_out, m_final_g, v_w_ada, v_b_ada, v_norm1_g, v_w_in, v_conv_w, v_conv_b, v_conv_ln_g, v_conv_ln_b, v_conv_proj, v_ssm_a_re, v_ssm_a_im, v_ssm_b_re, v_ssm_b_im, v_ssm_c_re, v_ssm_c_im, v_ssm_d, v_ssm_log_dt, v_ssm_glu, v_w_out, v_norm2_g, v_w_ffn_in, v_w_ffn_out, v_final_g):
    given = dict(locals())
    mx, my, mc = _me()
    chip = 2 * mx + my
    dev = 4 * mx + 2 * my + mc
    def canon(a):
        return a.reshape(1, -1) if a.ndim <= 2 else a[0]

    wts = {n: canon(given[n]) for n in _ORDER}
    mom = {n: canon(given["m_" + n]) for n in _ORDER}
    var = {n: canon(given["v_" + n]) for n in _ORDER}

    c_all = _allgather8(jnp.broadcast_to(c, (8, D_MODEL)), "gather_c")[:, 0, :]
    n_ada = wts["w_ada"].shape[1]
    b_cols = lax.dynamic_slice(wts["b_ada"], (0, chip * n_ada), (1, n_ada))
    mod_cols = _mod_shard(c_all, wts["w_ada"], b_cols)
    mods = _allgather8(mod_cols, "gather_mod")
    mod = jnp.concatenate([lax.dynamic_index_in_dim(mods[2 * q], dev, 0, keepdims=True) for q in range(N_CHIP)], axis=1)

    W = {n: wts[n] for n in _ORDER if n not in _BIG}
    conv_w_full = _allgather8(jnp.pad(wts["conv_w"], ((0, 1), (0, 0))), "gather_conv_w")
    W["conv_w"] = jnp.concatenate([conv_w_full[2 * q, :KW] for q in range(N_CHIP)], axis=1)

    axes = [_BIG_AXIS[n] for n in _BIG]
    shards = [wts[n].astype(BF16) for n in _BIG]
    lands = []
    for s, ax in zip(shards, axes):
        shp = list(s.shape)
        shp[ax] *= N_CHIP
        lands.append(lax.empty(tuple(shp), BF16))
    gstate, token = _gather_start(shards, lands, axes, mod + W["conv_w"][0:1, 0:1])
    gstate = dict(zip(_BIG, gstate))
    mod = mod + token[0:1, 0:1]

    def getw(n, after):
        return _gather_wait(gstate[n], _BIG_AXIS[n], after, "gather_wait_" + n)

    sstate, own, estate = {}, {}, []

    def put(n, g):
        ax = _BIG_AXIS[n]
        k = g.shape[ax] // N_CHIP
        own[n] = lax.dynamic_slice_in_dim(g, chip * k, k, axis=ax)
        sstate[n], tok = _scatter_start(g, ax, "scatter_start_" + n)
        return tok

    first5 = [n for n in _BIG if n != "w_in"]

    def early(sm):
        state, tok = _all8_start(_pack([sm[n] for n in _EARLY]), "small_start")
        estate.append(state)
        recv5 = [_scatter_wait(sstate[n], _BIG_AXIS[n], tok, "scatter_wait_" + n) for n in first5]
        held = [a for n, r in zip(first5, recv5) for a in (own[n], r)]
        state, tok = _swap_start(held, tok, "swap_start")
        estate.append(state)
        return tok

    loss8, dx, dn1, dmod = _device_step(x[0], mod, W, loss_target[0], getw, put, early)

    held5, sib5 = _swap_wait(estate[1], dx, "swap_wait")
    outs = {}
    for i, n in enumerate(first5):
        outs[n] = _adamw(wts[n], mom[n], var[n], [held5[2 * i:2 * i + 2], sib5[2 * i:2 * i + 2]], "adamw_" + n)
    allp = _all8_wait(estate[0], dx, "small_wait")

    late = _allgather8(_pack([dn1, dmod, loss8]), "gather_late", after=[outs[n][1] for n in first5])
    n_late = _pack_rows((D_MODEL,)) + _pack_rows((6 * D_MODEL,))
    loss = jnp.sum(late[:, n_late:, :])
    late = late[:, :n_late, :]
    held_in = [own["w_in"], _scatter_wait(sstate["w_in"], _BIG_AXIS["w_in"], late, "scatter_wait_w_in")]
    sib_in = _swap_sibling(held_in)
    outs["w_in"] = _adamw(wts["w_in"], mom["w_in"], var["w_in"], [held_in, sib_in], "adamw_w_in")

    r1 = _pack_rows((D_MODEL,))
    dmod_all = late[:, r1:, :].reshape(N_DEV, -1)[:, :6 * D_MODEL]
    dmod_cols = lax.dynamic_slice(dmod_all, (0, chip * n_ada), (N_DEV, n_ada))
    g_ada = _ada_grad(c_all, dmod_cols)
    outs["w_ada"] = _adamw(wts["w_ada"], mom["w_ada"], var["w_ada"], [[g_ada]], "adamw_w_ada")

    def packed_params(d, names):
        return _pack([jnp.zeros((KW, CW), F32) if n == "conv_w" else d[n] for n in names])

    for names, parts, nm in ((_EARLY, allp, "adamw_small"), (_LATE, late, "adamw_late")):
        res = _adamw(packed_params(wts, names), packed_params(mom, names), packed_params(var, names), [[parts]], nm)
        shapes = [(KW, CW) if n == "conv_w" else wts[n].shape for n in names]
        unpacked = [_unpack(r, shapes) for r in res]
        for idx, n in enumerate(names):
            outs[n] = tuple(unpacked[q][idx] for q in range(4))
    g_cw = lax.dynamic_slice(outs["conv_w"][0], (0, chip * (CW // N_CHIP)), (KW, CW // N_CHIP))
    pad = lambda a: jnp.pad(a, ((0, 1), (0, 0)))
    r_cw = _adamw(pad(wts["conv_w"]), pad(mom["conv_w"]), pad(var["conv_w"]), [[pad(g_cw)]], "adamw_conv_w")
    outs["conv_w"] = tuple(r[:KW] for r in r_cw)

    def shaped(n, a):
        return a.reshape(given[n].shape)

    result = [loss, dx[None]]
    for q in range(4):
        result += [shaped(n, outs[n][q]) for n in _ORDER]
    return tuple(result)
```

```python
import math

import jax
import jax.numpy as jnp
import numpy as np
from jax import lax
from jax.experimental import pallas as pl
from jax.experimental.pallas import tpu as pltpu

F32 = jnp.float32
BF16 = jnp.bfloat16
EPS = 1e-6
D_MODEL = 1024
CW = 512
KW = 31
HALO = 32
G, P, H = 32, 64, 16
NST = G * P
FH = 2816
N_DEV = 8
N_CHIP = 4
VMEM_LIMIT = 56 * 1024 * 1024
LR, B1, B2, AEPS, WD, STEP = 0.001, 0.9, 0.999, 1e-08, 0.01, 10
MESH = pl.DeviceIdType.MESH


def _cp(sem=None):
    return pltpu.CompilerParams(dimension_semantics=sem, vmem_limit_bytes=VMEM_LIMIT)


def _sig(x):
    return jax.nn.sigmoid(x)


def _full(shape):
    return pl.BlockSpec(shape, lambda *_: (0,) * len(shape))


def _colsum8(v):
    t, c = v.shape
    return jnp.sum(v.reshape(t // 8, 8, c), axis=0)


def _matmul(a, b, mode, tm, tn, tk, out_dtype, name, after=None, n_outer=False, m_cols=None):
    m0 = 0
    b_parts = list(b) if isinstance(b, (list, tuple)) else [b]
    if mode == "nn":
        (M, K), N = a.shape, b.shape[1]
    elif mode == "nt":
        (M, K), N = a.shape, b.shape[0]
    else:
        (K, M), N = a.shape, sum(p.shape[1] for p in b_parts)
        if m_cols is not None:
            m0, M = m_cols
    tm, tn, tk = min(tm, M), min(tn, N), min(tk, K)
    assert M % tm == 0 and N % tn == 0 and K % tk == 0 and m0 % tm == 0, (name, M, N, K, tm, tn, tk)
    assert len(b_parts) == 1 or (mode == "tn" and all(p.shape[1] % tn == 0 for p in b_parts)), name
    nk = K // tk
    mb = m0 // tm
    counts = [p.shape[1] // tn for p in b_parts] if mode == "tn" else [N // tn]
    starts = [sum(counts[:p]) for p in range(len(counts))]

    def ij(fn):
        return (lambda j, i, k: fn(i, j, k)) if n_outer else fn

    if mode == "nn":
        a_spec = pl.BlockSpec((tm, tk), ij(lambda i, j, k: (i, k)))
        b_spec = pl.BlockSpec((tk, tn), ij(lambda i, j, k: (k, j)))
        dims = (((1,), (0,)), ((), ()))
    elif mode == "nt":
        a_spec = pl.BlockSpec((tm, tk), ij(lambda i, j, k: (i, k)))
        b_spec = pl.BlockSpec((tn, tk), ij(lambda i, j, k: (j, k)))
        dims = (((1,), (1,)), ((), ()))
    else:
        a_spec = pl.BlockSpec((tk, tm), ij(lambda i, j, k: (k, i + mb)))
        dims = (((0,), (0,)), ((), ()))
    if mode == "tn":
        b_specs = [pl.BlockSpec((tk, tn), ij(lambda i, j, k, s=s, n=n: (k, jnp.clip(j - s, 0, n - 1))))
                   for s, n in zip(starts, counts)]
    else:
        b_specs = [b_spec]
    nb = len(b_parts)

    def body(a_ref, *rest):
        b_refs = rest[:nb]
        o_ref, acc_ref = rest[-2:]
        j = pl.program_id(0 if n_outer else 1)
        k = pl.program_id(2)

        def compute(b_ref):
            part = lax.dot_general(a_ref[...].astype(BF16), b_ref[...].astype(BF16), dims,
                                   preferred_element_type=F32)
            if nk == 1:
                o_ref[...] = part.astype(out_dtype)
            else:
                @pl.when(k == 0)
                def _():
                    acc_ref[...] = part

                @pl.when(k > 0)
                def _():
                    acc_ref[...] += part

                @pl.when(k == nk - 1)
                def _():
                    o_ref[...] = acc_ref[...].astype(out_dtype)

        if nb == 1:
            compute(b_refs[0])
        else:
            for p in range(nb):
                pl.when(jnp.logical_and(j >= starts[p], j < starts[p] + counts[p]))(
                    lambda b_ref=b_refs[p]: compute(b_ref))

    return pl.pallas_call(
        body, name=name,
        out_shape=jax.ShapeDtypeStruct((M, N), out_dtype),
        grid=(N // tn, M // tm, nk) if n_outer else (M // tm, N // tn, nk),
        in_specs=[a_spec] + b_specs + ([] if after is None else [pl.BlockSpec(memory_space=pl.ANY)]),
        out_specs=pl.BlockSpec((tm, tn), ij(lambda i, j, k: (i, j))),
        scratch_shapes=[pltpu.VMEM((tm, tn) if nk > 1 else (8, 128), F32)],
        compiler_params=_cp(("parallel", "parallel", "arbitrary")),
    )(*([a] + b_parts + ([] if after is None else [after])))


def _row_tile(S):
    return min(512, S)


def _in_proj(x, g, sc, sh, w_in):
    S, D = x.shape
    N = w_in.shape[1]
    tm = min(256, S)

    def body(x_ref, g_ref, sc_ref, sh_ref, w_ref, h_ref, z_ref):
        xv = x_ref[...]
        r = lax.rsqrt(jnp.mean(xv * xv, axis=-1, keepdims=True) + EPS)
        h = (xv * r * (g_ref[...] * (1.0 + sc_ref[...])) + sh_ref[...]).astype(BF16)
        h_ref[...] = h
        z_ref[...] = jnp.dot(h, w_ref[...], preferred_element_type=F32)

    row = pl.BlockSpec((tm, D), lambda i: (i, 0))
    par = _full((1, D))
    return pl.pallas_call(
        body, name="in_proj",
        out_shape=(jax.ShapeDtypeStruct((S, D), BF16), jax.ShapeDtypeStruct((S, N), F32)), grid=(S // tm,),
        in_specs=[row, par, par, par, _full((D, N))], out_specs=(row, pl.BlockSpec((tm, N), lambda i: (i, 0))),
        compiler_params=_cp(("parallel",)))(x, g, sc, sh, w_in)


def _fill_shifted(buf_ref, sh_ref):
    n = buf_ref.shape[0] - 8
    for s in range(1, 8):
        sh_ref[s, 0:n, :] = buf_ref[s:s + n, :]


def _window(buf_ref, sh_ref, off, n):
    s = off % 8
    return buf_ref[off:off + n, :] if s == 0 else sh_ref[s, off - s:off - s + n, :]


def _conv_fwd(z, conv_w, conv_b, ln_g, ln_b):
    S = z.shape[0]
    tm = min(128, S)
    sub = 32
    hb = tm // HALO

    def body(a_ref, g_ref, ha_ref, hg_ref, w_ref, b_ref, lg_ref, lb_ref, yc_ref, s_ref, ug_ref, sh_ref):
        i = pl.program_id(0)
        halo = ha_ref[...] * _sig(hg_ref[...])
        ug_ref[0:HALO, :] = jnp.where(i == 0, 0.0, halo)
        ug_ref[HALO:, :] = a_ref[...] * _sig(g_ref[...])
        _fill_shifted(ug_ref, sh_ref)
        for rb in range(tm // sub):
            acc = jnp.zeros((sub, CW), F32) + b_ref[...]
            for k in range(KW):
                off = rb * sub + HALO - (KW - 1) + k
                acc = acc + w_ref[k:k + 1, :] * _window(ug_ref, sh_ref, off, sub)
            yc_ref[rb * sub:(rb + 1) * sub, :] = acc
            mu = jnp.mean(acc, axis=-1, keepdims=True)
            cen = acc - mu
            rstd = lax.rsqrt(jnp.mean(cen * cen, axis=-1, keepdims=True) + EPS)
            ln = cen * rstd * lg_ref[...] + lb_ref[...]
            s_ref[rb * sub:(rb + 1) * sub, :] = (ln * _sig(ln)).astype(BF16)

    prev = lambda i: (jnp.maximum(i * hb - 1, 0), 0)
    return pl.pallas_call(
        body, name="conv_fwd",
        out_shape=(jax.ShapeDtypeStruct((S, CW), F32), jax.ShapeDtypeStruct((S, CW), BF16)),
        grid=(S // tm,),
        in_specs=[pl.BlockSpec((tm, CW), lambda i: (i, 0)), pl.BlockSpec((tm, CW), lambda i: (i, 1)),
                  pl.BlockSpec((HALO, CW), prev), pl.BlockSpec((HALO, CW), lambda i: (jnp.maximum(i * hb - 1, 0), 1)),
                  _full((KW, CW)), _full((1, CW)), _full((1, CW)), _full((1, CW))],
        out_specs=(pl.BlockSpec((tm, CW), lambda i: (i, 0)), pl.BlockSpec((tm, CW), lambda i: (i, 0))),
        scratch_shapes=[pltpu.VMEM((tm + HALO, CW), F32), pltpu.VMEM((8, tm + HALO, CW), F32)],
        compiler_params=_cp(("parallel",)))(z, z, z, z, conv_w, conv_b, ln_g, ln_b)


def _conv_bwd_ln(dyconv, w_cp, yc, ln_g, ln_b):
    S = yc.shape[0]
    tm = _row_tile(S)

    def body(dy_ref, w_ref, yc_ref, lg_ref, lb_ref, dyc_ref, dlg_ref, dlb_ref, dcb_ref):
        i = pl.program_id(0)
        dsc = lax.dot_general(dy_ref[...], w_ref[...], (((1,), (1,)), ((), ())), preferred_element_type=F32)
        yc_v = yc_ref[...]
        mu = jnp.mean(yc_v, axis=-1, keepdims=True)
        cen = yc_v - mu
        rstd = lax.rsqrt(jnp.mean(cen * cen, axis=-1, keepdims=True) + EPS)
        yn = cen * rstd
        ln = yn * lg_ref[...] + lb_ref[...]
        sl = _sig(ln)
        dln = dsc * (sl * (1.0 + ln * (1.0 - sl)))
        dyn = dln * lg_ref[...]
        dyc = rstd * (dyn - jnp.mean(dyn, axis=-1, keepdims=True)
                      - yn * jnp.mean(dyn * yn, axis=-1, keepdims=True))
        dyc_ref[...] = dyc

        @pl.when(i == 0)
        def _():
            dlg_ref[...] = jnp.zeros_like(dlg_ref)
            dlb_ref[...] = jnp.zeros_like(dlb_ref)
            dcb_ref[...] = jnp.zeros_like(dcb_ref)

        dlg_ref[...] += _colsum8(dln * yn)
        dlb_ref[...] += _colsum8(dln)
        dcb_ref[...] += _colsum8(dyc)

    row = pl.BlockSpec((tm, CW), lambda i: (i, 0))
    acc = jax.ShapeDtypeStruct((8, CW), F32)
    return pl.pallas_call(
        body, name="conv_bwd_ln",
        out_shape=(jax.ShapeDtypeStruct((S, CW), F32), acc, acc, acc), grid=(S // tm,),
        in_specs=[pl.BlockSpec((tm, D_MODEL), lambda i: (i, 0)), _full((CW, D_MODEL)), row, _full((1, CW)),
                  _full((1, CW))],
        out_specs=(row, _full((8, CW)), _full((8, CW)), _full((8, CW))),
        compiler_params=_cp(("arbitrary",)))(dyconv, w_cp, yc, ln_g, ln_b)


def _conv_bwd(dyc, z, conv_w):
    S = z.shape[0]
    tm = min(128, S)
    sub = 32
    hb = tm // HALO
    nt = S // tm

    def body(d_ref, dn_ref, a_ref, g_ref, ha_ref, hg_ref, w_ref, dz_ref, dw_ref, ug_ref, dy_ref, ugs_ref, dys_ref):
        i = pl.program_id(0)
        halo = ha_ref[...] * _sig(hg_ref[...])
        ug_ref[0:HALO, :] = jnp.where(i == 0, 0.0, halo)
        a = a_ref[...]
        sg = _sig(g_ref[...])
        ug_ref[HALO:, :] = a * sg
        dy_ref[0:tm, :] = d_ref[...]
        dy_ref[tm:, :] = jnp.where(i == nt - 1, 0.0, dn_ref[...])
        _fill_shifted(ug_ref, ugs_ref)
        _fill_shifted(dy_ref, dys_ref)

        @pl.when(i == 0)
        def _():
            dw_ref[...] = jnp.zeros_like(dw_ref)

        for rb in range(tm // sub):
            r0 = rb * sub
            acc = jnp.zeros((sub, CW), F32)
            dyc_b = dy_ref[r0:r0 + sub, :]
            for k in range(KW):
                up = r0 + (KW - 1) - k
                acc = acc + w_ref[k:k + 1, :] * _window(dy_ref, dys_ref, up, sub)
                off = r0 + HALO - (KW - 1) + k
                dw_ref[k * 8:(k + 1) * 8, :] += _colsum8(dyc_b * _window(ug_ref, ugs_ref, off, sub))
            a_b = a[r0:r0 + sub, :]
            sg_b = sg[r0:r0 + sub, :]
            dz_ref[r0:r0 + sub, 0:CW] = (acc * sg_b).astype(BF16)
            dz_ref[r0:r0 + sub, CW:2 * CW] = (acc * a_b * sg_b * (1.0 - sg_b)).astype(BF16)

    return pl.pallas_call(
        body, name="conv_bwd",
        out_shape=(jax.ShapeDtypeStruct((S, 2 * CW), BF16), jax.ShapeDtypeStruct((KW * 8, CW), F32)),
        grid=(nt,),
        in_specs=[pl.BlockSpec((tm, CW), lambda i: (i, 0)),
                  pl.BlockSpec((HALO, CW), lambda i: (jnp.minimum((i + 1) * hb, nt * hb - 1), 0)),
                  pl.BlockSpec((tm, CW), lambda i: (i, 0)), pl.BlockSpec((tm, CW), lambda i: (i, 1)),
                  pl.BlockSpec((HALO, CW), lambda i: (jnp.maximum(i * hb - 1, 0), 0)),
                  pl.BlockSpec((HALO, CW), lambda i: (jnp.maximum(i * hb - 1, 0), 1)),
                  _full((KW, CW))],
        out_specs=(pl.BlockSpec((tm, 2 * CW), lambda i: (i, 0)), _full((KW * 8, CW))),
        scratch_shapes=[pltpu.VMEM((tm + HALO, CW), F32), pltpu.VMEM((tm + HALO, CW), F32),
                        pltpu.VMEM((8, tm + HALO, CW), F32), pltpu.VMEM((8, tm + HALO, CW), F32)],
        compiler_params=_cp(("arbitrary",)))(dyc, dyc, z, z, z, z, conv_w)


_GELU_C = math.sqrt(2.0 / math.pi)


def _gelu(x):
    return 0.5 * x * (1.0 + jnp.tanh(_GELU_C * (x + 0.044715 * x * x * x)))


def _gelu_grad(x):
    t = jnp.tanh(_GELU_C * (x + 0.044715 * x * x * x))
    return 0.5 * (1.0 + t) + 0.5 * x * (1.0 - t * t) * (_GELU_C * (1.0 + 3 * 0.044715 * x * x))


_NCL = 4
_UC = CW // _NCL
_LW = NST // _NCL
_CS = 2 * _LW


def _ssm_fwd(z, bb, cm, d, tab):
    S = z.shape[0]
    tm = min(256, S)

    def body(u_ref, bb_ref, cm_ref, d_ref, t_ref, x_ref, ys_ref, yg_ref, car_ref):
        i = pl.program_id(0)

        @pl.when(i == 0)
        def _():
            car_ref[...] = jnp.zeros_like(car_ref)

        u = u_ref[...]
        u16 = u.astype(BF16)
        for c in range(_NCL):
            lre = pl.ds(c * _CS, _LW)
            lim = pl.ds(c * _CS + _LW, _LW)
            tl = pl.ds(c * _LW, _LW)
            x_ref[:, c * _CS:(c + 1) * _CS] = jnp.dot(u16[:, c * _UC:(c + 1) * _UC], bb_ref[c],
                                                      preferred_element_type=F32)

            def blk(j, car):
                cr, ci = car
                rows = pl.ds(pl.multiple_of(j * 8, 8), 8)
                r = x_ref[rows, lre]
                im = x_ref[rows, lim]
                for lvl, s in enumerate((1, 2, 4)):
                    mr = t_ref[16 * lvl:16 * lvl + 8, tl]
                    mi = t_ref[16 * lvl + 8:16 * lvl + 16, tl]
                    sr = pltpu.roll(r, s, 0)
                    si = pltpu.roll(im, s, 0)
                    r, im = r + (mr * sr - mi * si), im + (mr * si + mi * sr)
                pr = t_ref[48:56, tl]
                pi_ = t_ref[56:64, tl]
                r, im = r + (pr * cr - pi_ * ci), im + (pr * ci + pi_ * cr)
                x_ref[rows, lre] = r
                x_ref[rows, lim] = im
                return (jnp.broadcast_to(r[7:8, :], (8, _LW)), jnp.broadcast_to(im[7:8, :], (8, _LW)))

            cr, ci = lax.fori_loop(0, tm // 8, blk, (car_ref[:, lre], car_ref[:, lim]))
            car_ref[:, lre] = cr
            car_ref[:, lim] = ci
            cols = slice(c * _UC, (c + 1) * _UC)
            ys = jnp.dot(x_ref[:, c * _CS:(c + 1) * _CS].astype(BF16), cm_ref[c], preferred_element_type=F32)
            ys = ys + d_ref[:, cols] * u[:, cols]
            ys_ref[:, cols] = ys
            yg_ref[:, cols] = _gelu(ys).astype(BF16)

    return pl.pallas_call(
        body, name="ssm_fwd",
        out_shape=(jax.ShapeDtypeStruct((S, 2 * NST), F32), jax.ShapeDtypeStruct((S, CW), F32),
                   jax.ShapeDtypeStruct((S, CW), BF16)),
        grid=(S // tm,),
        in_specs=[pl.BlockSpec((tm, CW), lambda i: (i, 2)), _full((_NCL, _UC, _CS)), _full((_NCL, _CS, _UC)),
                  _full((1, CW)), _full((64, NST))],
        out_specs=(pl.BlockSpec((tm, 2 * NST), lambda i: (i, 0)), pl.BlockSpec((tm, CW), lambda i: (i, 0)),
                   pl.BlockSpec((tm, CW), lambda i: (i, 0))),
        scratch_shapes=[pltpu.VMEM((8, 2 * NST), F32)],
        compiler_params=_cp(("arbitrary",)))(z, bb, cm, d, tab)


def _ssm_bwd(dzz, w_glu, ys, z, xs, cmt, bbt, d, tab, after):
    S = z.shape[0]
    tm = min(256, S)
    nt = S // tm
    tdims = (((0,), (0,)), ((), ()))

    def body(dzz_ref, wglu_ref, ys_ref, u_ref, x_ref, cmt_ref, bbt_ref, d_ref, t_ref, after_ref,
             du_ref, de_ref, dd_ref, dc_hbm, dbb_hbm, car_ref, lam_ref, dc_ref, dbb_ref):
        i = pl.program_id(0)

        @pl.when(i == 0)
        def _():
            car_ref[...] = jnp.zeros_like(car_ref)
            de_ref[...] = jnp.zeros_like(de_ref)
            dd_ref[...] = jnp.zeros_like(dd_ref)
            dc_ref[...] = jnp.zeros_like(dc_ref)
            dbb_ref[...] = jnp.zeros_like(dbb_ref)

        u = u_ref[...]
        u16 = u.astype(BF16)
        dyg = lax.dot_general(dzz_ref[...], wglu_ref[...], (((1,), (1,)), ((), ())), preferred_element_type=F32)
        dys = dyg * _gelu_grad(ys_ref[...])
        dys16 = dys.astype(BF16)
        dd_ref[...] += _colsum8(dys * u)
        row = lax.broadcasted_iota(jnp.int32, (8, _LW), 0)
        for c in range(_NCL):
            lre = pl.ds(c * _CS, _LW)
            lim = pl.ds(c * _CS + _LW, _LW)
            tl = pl.ds(c * _LW, _LW)
            cols = slice(c * _UC, (c + 1) * _UC)
            span = slice(c * _CS, (c + 1) * _CS)
            dc_ref[cols, :] += lax.dot_general(dys16[:, cols], x_ref[:, span].astype(BF16), tdims,
                                               preferred_element_type=F32)
            lam_ref[...] = jnp.dot(dys16[:, cols], cmt_ref[c], preferred_element_type=F32)

            def blk(jj, car):
                cr, ci, ar, ai = car
                j = tm // 8 - 1 - jj
                rows = pl.ds(pl.multiple_of(j * 8, 8), 8)
                r = lam_ref[rows, 0:_LW]
                im = lam_ref[rows, _LW:_CS]
                for lvl, s in enumerate((1, 2, 4)):
                    mr = t_ref[16 * lvl:16 * lvl + 8, tl]
                    mi = t_ref[16 * lvl + 8:16 * lvl + 16, tl]
                    sr = pltpu.roll(r, 8 - s, 0)
                    si = pltpu.roll(im, 8 - s, 0)
                    r, im = r + (mr * sr - mi * si), im + (mr * si + mi * sr)
                pr = t_ref[48:56, tl]
                pi_ = t_ref[56:64, tl]
                r, im = r + (pr * cr - pi_ * ci), im + (pr * ci + pi_ * cr)
                lam_ref[rows, 0:_LW] = r
                lam_ref[rows, _LW:_CS] = im
                nr = jnp.where(row == 7, cr, pltpu.roll(r, 7, 0))
                ni = jnp.where(row == 7, ci, pltpu.roll(im, 7, 0))
                xr = x_ref[rows, lre]
                xi = x_ref[rows, lim]
                ar = ar + (nr * xr + ni * xi)
                ai = ai + (ni * xr - nr * xi)
                return (jnp.broadcast_to(r[0:1, :], (8, _LW)), jnp.broadcast_to(im[0:1, :], (8, _LW)), ar, ai)

            zero = jnp.zeros((8, _LW), F32)
            cr, ci, ar, ai = lax.fori_loop(0, tm // 8, blk, (car_ref[:, lre], car_ref[:, lim], zero, zero))
            car_ref[:, lre] = cr
            car_ref[:, lim] = ci
            de_ref[0:8, tl] += ar
            de_ref[8:16, tl] += ai
            lam16 = lam_ref[...].astype(BF16)
            dbb_ref[cols, :] += lax.dot_general(u16[:, cols], lam16, tdims, preferred_element_type=F32)
            du = jnp.dot(lam16, bbt_ref[c], preferred_element_type=F32) + dys[:, cols] * d_ref[:, cols]
            du_ref[:, cols] = du.astype(BF16)

        @pl.when(i == nt - 1)
        def _():
            pltpu.sync_copy(dc_ref, dc_hbm)
            pltpu.sync_copy(dbb_ref, dbb_hbm)

    rev = lambda i: (nt - 1 - i, 0)
    once = lambda shape: pl.BlockSpec(shape, lambda *_: (0,) * len(shape), pipeline_mode=pl.Buffered(1))
    cross = jax.ShapeDtypeStruct((CW, _CS), F32)
    return pl.pallas_call(
        body, name="ssm_bwd",
        out_shape=(jax.ShapeDtypeStruct((S, CW), BF16), jax.ShapeDtypeStruct((16, NST), F32),
                   jax.ShapeDtypeStruct((8, CW), F32), cross, cross),
        grid=(nt,),
        in_specs=[pl.BlockSpec((tm, 2 * D_MODEL), rev), once((CW, 2 * D_MODEL)), pl.BlockSpec((tm, CW), rev),
                  pl.BlockSpec((tm, CW), lambda i: (nt - 1 - i, 2)), pl.BlockSpec((tm, 2 * NST), rev),
                  once((_NCL, _UC, _CS)), once((_NCL, _CS, _UC)), _full((1, CW)), once((64, NST)),
                  pl.BlockSpec(memory_space=pl.ANY)],
        out_specs=(pl.BlockSpec((tm, CW), rev), _full((16, NST)), _full((8, CW)),
                   pl.BlockSpec(memory_space=pl.ANY), pl.BlockSpec(memory_space=pl.ANY)),
        scratch_shapes=[pltpu.VMEM((8, 2 * NST), F32), pltpu.VMEM((tm, _CS), F32),
                        pltpu.VMEM((CW, _CS), F32), pltpu.VMEM((CW, _CS), F32)],
        compiler_params=_cp(("arbitrary",)))(dzz, w_glu, ys, z, xs, cmt, bbt, d, tab, after)


def _ssm_prep(a_re, a_im, b_re, b_im, log_dt):
    dt = jnp.exp(log_dt.reshape(G))[:, None]
    mag = jnp.exp(dt * a_re)
    e_re, e_im = mag * jnp.cos(dt * a_im), mag * jnp.sin(dt * a_im)
    n_re, n_im = e_re - 1.0, e_im
    den = a_re * a_re + a_im * a_im
    q_re = (n_re * a_re + n_im * a_im) / den
    q_im = (n_im * a_re - n_re * a_im) / den
    bb_re = q_re[..., None] * b_re - q_im[..., None] * b_im
    bb_im = q_re[..., None] * b_im + q_im[..., None] * b_re
    return e_re, e_im, bb_re, bb_im


def _scan_tables(e_re, e_im, reverse):
    er = e_re.reshape(1, NST)
    ei = e_im.reshape(1, NST)
    if reverse:
        ei = -ei
    pows = [(er, ei)]
    for _ in range(7):
        pr, pi_ = pows[-1]
        pows.append((pr * er - pi_ * ei, pr * ei + pi_ * er))
    row = jnp.arange(8)[:, None]
    out = []
    for s in (1, 2, 4):
        pr, pi_ = pows[s - 1]
        keep = (row + s <= 7) if reverse else (row >= s)
        out += [jnp.where(keep, pr, 0.0), jnp.where(keep, pi_, 0.0)]
    allr = jnp.concatenate([p[0] for p in pows], 0)
    alli = jnp.concatenate([p[1] for p in pows], 0)
    if reverse:
        allr, alli = allr[::-1], alli[::-1]
    out += [allr, alli]
    return jnp.concatenate(out, 0).astype(F32)


def _block_diag_mats(bb_re, bb_im, c_re, c_im):
    gc = G // _NCL
    eye = jnp.eye(gc, dtype=F32)
    bre = jnp.einsum("cjph,jk->cjhkp", bb_re.reshape(_NCL, gc, P, H), eye).reshape(_NCL, _UC, _LW)
    bim = jnp.einsum("cjph,jk->cjhkp", bb_im.reshape(_NCL, gc, P, H), eye).reshape(_NCL, _UC, _LW)
    bb = jnp.concatenate([bre, bim], 2)
    cre = jnp.einsum("cjhp,jk->cjpkh", c_re.reshape(_NCL, gc, H, P), eye).reshape(_NCL, _LW, _UC)
    cim = jnp.einsum("cjhp,jk->cjpkh", c_im.reshape(_NCL, gc, H, P), eye).reshape(_NCL, _LW, _UC)
    cm = jnp.concatenate([cre, -cim], 1)
    return bb, cm


def _diag_blocks(cross, imag):
    gc = G // _NCL
    off = _LW if imag else 0
    return jnp.stack([cross[H * g:H * (g + 1), off + P * (g % gc):off + P * (g % gc + 1)] for g in range(G)])


def _mix_fwd(scv, yg, z, x, w_cp, w_glu, w_out, g1, n2g, sc2, sh2):
    S = z.shape[0]
    tm = min(256, S)
    D = D_MODEL

    def body(s_ref, yg_ref, glc0_ref, glc1_ref, gls0_ref, gls1_ref, x_ref, wcp_ref, wglu_ref, wout_ref,
             g1_ref, n2_ref, sc_ref, sh_ref, yc_ref, zz_ref, m_ref, o_ref, x2_ref, h2_ref):
        y_conv = jnp.dot(s_ref[...], wcp_ref[...], preferred_element_type=F32)
        zz = jnp.dot(yg_ref[...], wglu_ref[...], preferred_element_type=F32)
        yc_ref[...] = y_conv.astype(BF16)
        zz_ref[...] = zz.astype(BF16)
        for half, (glc_ref, gls_ref) in enumerate(((glc0_ref, gls0_ref), (glc1_ref, gls1_ref))):
            lo, hi = half * CW, (half + 1) * CW
            y_ssm = zz[:, lo:hi] * _sig(zz[:, D + lo:D + hi])
            m_ref[:, lo:hi] = (_sig(glc_ref[...]) * y_conv[:, lo:hi] + _sig(gls_ref[...]) * y_ssm).astype(BF16)
        o = jnp.dot(m_ref[...], wout_ref[...], preferred_element_type=F32)
        o_ref[...] = o.astype(BF16)
        xv = x_ref[...] + g1_ref[...] * o
        x2_ref[...] = xv
        r = lax.rsqrt(jnp.mean(xv * xv, axis=-1, keepdims=True) + EPS)
        h2_ref[...] = (xv * r * (n2_ref[...] * (1.0 + sc_ref[...])) + sh_ref[...]).astype(BF16)

    zb_ = lambda j: pl.BlockSpec((tm, CW), lambda i: (i, j))
    row = lambda w: pl.BlockSpec((tm, w), lambda i: (i, 0))
    par = _full((1, D))
    bf = lambda w: jax.ShapeDtypeStruct((S, w), BF16)
    return pl.pallas_call(
        body, name="mix_fwd",
        out_shape=(bf(D), bf(2 * D), bf(D), bf(D), jax.ShapeDtypeStruct((S, D), F32), bf(D)),
        grid=(S // tm,),
        in_specs=[row(CW), row(CW), zb_(3), zb_(4), zb_(5), zb_(6), row(D), _full((CW, D)), _full((CW, 2 * D)),
                  _full((D, D)), par, par, par, par],
        out_specs=(row(D), row(2 * D), row(D), row(D), row(D), row(D)),
        compiler_params=_cp(("parallel",)))(scv, yg, z, z, z, z, x, w_cp, w_glu, w_out, g1, n2g, sc2, sh2)


def _mix_bwd(do, w_out, z, zz, y_conv, after):
    S = z.shape[0]
    tm = min(256, S)
    D = D_MODEL

    def body(do_ref, w_ref, glc0_ref, glc1_ref, gls0_ref, gls1_ref, za_ref, zb_ref, yc_ref, after_ref,
             dyc_ref, dgl_ref, dzz_ref):
        dm = lax.dot_general(do_ref[...], w_ref[...], (((1,), (1,)), ((), ())), preferred_element_type=F32)
        for half, (glc_ref, gls_ref) in enumerate(((glc0_ref, gls0_ref), (glc1_ref, gls1_ref))):
            lo, hi = half * CW, (half + 1) * CW
            dm_v = dm[:, lo:hi]
            sgc = _sig(glc_ref[...])
            sgs = _sig(gls_ref[...])
            szb = _sig(zb_ref[:, lo:hi].astype(F32))
            za = za_ref[:, lo:hi].astype(F32)
            dyc_ref[:, lo:hi] = (dm_v * sgc).astype(BF16)
            dgl_ref[:, lo:hi] = (dm_v * yc_ref[:, lo:hi].astype(F32) * sgc * (1.0 - sgc)).astype(BF16)
            dys = dm_v * sgs
            dgl_ref[:, D + lo:D + hi] = (dys * (za * szb) * (1.0 - sgs)).astype(BF16)
            dzz_ref[:, lo:hi] = (dys * szb).astype(BF16)
            dzz_ref[:, D + lo:D + hi] = (dys * za * szb * (1.0 - szb)).astype(BF16)

    zb_ = lambda j: pl.BlockSpec((tm, CW), lambda i: (i, j))
    wide = lambda j: pl.BlockSpec((tm, D), lambda i: (i, j))
    return pl.pallas_call(
        body, name="mix_bwd",
        out_shape=(jax.ShapeDtypeStruct((S, D), BF16), jax.ShapeDtypeStruct((S, 2 * D), BF16),
                   jax.ShapeDtypeStruct((S, 2 * D), BF16)),
        grid=(S // tm,),
        in_specs=[wide(0), _full((D, D)), zb_(3), zb_(4), zb_(5), zb_(6), wide(0), wide(1), wide(0),
                  pl.BlockSpec(memory_space=pl.ANY)],
        out_specs=(wide(0), pl.BlockSpec((tm, 2 * D), lambda i: (i, 0)), pl.BlockSpec((tm, 2 * D), lambda i: (i, 0))),
        compiler_params=_cp(("parallel",)))(do, w_out, z, z, z, z, zz, zz, y_conv, after)


_FC = 1408


def _ffn_in_act(h2, w_fi):
    S, D = h2.shape
    tm = min(256, S)

    def body(h_ref, w_ref, f_ref, a_ref):
        hv = h_ref[...]
        for c in range(FH // _FC):
            lo, hi = c * _FC, (c + 1) * _FC
            g = jnp.dot(hv, w_ref[:, lo:hi], preferred_element_type=F32)
            u = jnp.dot(hv, w_ref[:, FH + lo:FH + hi], preferred_element_type=F32)
            f_ref[:, lo:hi] = g.astype(BF16)
            f_ref[:, FH + lo:FH + hi] = u.astype(BF16)
            a_ref[:, lo:hi] = (g * _sig(g) * u).astype(BF16)

    return pl.pallas_call(
        body, name="ffn_in_act",
        out_shape=(jax.ShapeDtypeStruct((S, 2 * FH), BF16), jax.ShapeDtypeStruct((S, FH), BF16)),
        grid=(S // tm,),
        in_specs=[pl.BlockSpec((tm, D), lambda i: (i, 0)), _full((D, 2 * FH))],
        out_specs=(pl.BlockSpec((tm, 2 * FH), lambda i: (i, 0)), pl.BlockSpec((tm, FH), lambda i: (i, 0))),
        compiler_params=_cp(("parallel",)))(h2, w_fi)


def _ffn_bwd(do2, w_fo, f, after):
    S, D = do2.shape
    tm = min(256, S)

    def body(d_ref, w_ref, f_ref, after_ref, df_ref):
        dv = d_ref[...]
        for c in range(FH // _FC):
            lo, hi = c * _FC, (c + 1) * _FC
            dact = lax.dot_general(dv, w_ref[lo:hi, :], (((1,), (1,)), ((), ())), preferred_element_type=F32)
            g = f_ref[:, lo:hi].astype(F32)
            u = f_ref[:, FH + lo:FH + hi].astype(F32)
            sg = _sig(g)
            df_ref[:, lo:hi] = (dact * u * (sg * (1.0 + g * (1.0 - sg)))).astype(BF16)
            df_ref[:, FH + lo:FH + hi] = (dact * g * sg).astype(BF16)

    return pl.pallas_call(
        body, name="ffn_bwd", out_shape=jax.ShapeDtypeStruct((S, 2 * FH), BF16), grid=(S // tm,),
        in_specs=[pl.BlockSpec((tm, D), lambda i: (i, 0)), _full((FH, D)),
                  pl.BlockSpec((tm, 2 * FH), lambda i: (i, 0)), pl.BlockSpec(memory_space=pl.ANY)],
        out_specs=pl.BlockSpec((tm, 2 * FH), lambda i: (i, 0)),
        compiler_params=_cp(("parallel",)))(do2, w_fo, f, after)


def _ffn_out_final(x2, act, w_fo, g2, fg, tgt):
    S, D = x2.shape
    tm = min(256, S)

    def body(x2_ref, a_ref, w_ref, g2_ref, fg_ref, t_ref, dx3_ref, do2_ref, ls_ref, dfg_ref, dg2_ref):
        i = pl.program_id(0)
        o2 = jnp.dot(a_ref[...], w_ref[...], preferred_element_type=F32)
        x3 = x2_ref[...] + g2_ref[...] * o2
        r = lax.rsqrt(jnp.mean(x3 * x3, axis=-1, keepdims=True) + EPS)
        xn = x3 * r
        err = xn * fg_ref[...] - t_ref[...]
        dy = err * (1.0 / D)
        dxn = dy * fg_ref[...]
        dx3 = r * (dxn - xn * jnp.mean(dxn * xn, axis=-1, keepdims=True))
        dx3_ref[...] = dx3
        do2_ref[...] = (dx3 * g2_ref[...]).astype(BF16)

        @pl.when(i == 0)
        def _():
            ls_ref[...] = jnp.zeros_like(ls_ref)
            dfg_ref[...] = jnp.zeros_like(dfg_ref)
            dg2_ref[...] = jnp.zeros_like(dg2_ref)

        e2 = _colsum8(err * err)
        lanes = e2[:, 0:128]
        for q in range(1, D // 128):
            lanes = lanes + e2[:, q * 128:(q + 1) * 128]
        ls_ref[...] += lanes * (0.5 / D)
        dfg_ref[...] += _colsum8(dy * xn)
        dg2_ref[...] += _colsum8(dx3 * o2)

    row = pl.BlockSpec((tm, D), lambda i: (i, 0))
    par = _full((1, D))
    return pl.pallas_call(
        body, name="final_loss",
        out_shape=(jax.ShapeDtypeStruct((S, D), F32), jax.ShapeDtypeStruct((S, D), BF16),
                   jax.ShapeDtypeStruct((8, 128), F32), jax.ShapeDtypeStruct((8, D), F32),
                   jax.ShapeDtypeStruct((8, D), F32)),
        grid=(S // tm,), in_specs=[row, pl.BlockSpec((tm, FH), lambda i: (i, 0)), _full((FH, D)), par, par, row],
        out_specs=(row, row, _full((8, 128)), _full((8, D)), _full((8, D))),
        compiler_params=_cp(("arbitrary",)))(x2, act, w_fo, g2, fg, tgt)


def _normmod_bwd(dsrc, w, xin, dres, g, sc, gate, o, after, name):
    S, D = xin.shape
    parts = list(dsrc) if isinstance(dsrc, (list, tuple)) else [dsrc]
    widths = [p.shape[1] for p in parts]
    K = sum(widths)
    tm = min(256, S)
    npart = len(parts)

    def body(*refs):
        ds_refs = refs[:npart]
        w_ref, x_ref, dr_ref, g_ref, sc_ref, gate_ref, o_ref, after_ref = refs[npart:npart + 8]
        dx_ref, do_ref, dsh_ref, dsc_ref, dg_ref, dgate_ref = refs[npart + 8:]
        i = pl.program_id(0)
        xv = x_ref[...]
        r = lax.rsqrt(jnp.mean(xv * xv, axis=-1, keepdims=True) + EPS)
        xn = xv * r
        dh_v, col = None, 0
        for ds_ref, wd in zip(ds_refs, widths):
            t = lax.dot_general(ds_ref[...], w_ref[:, col:col + wd], (((1,), (1,)), ((), ())),
                                preferred_element_type=F32)
            dh_v = t if dh_v is None else dh_v + t
            col += wd
        gv = g_ref[...]
        scale = 1.0 + sc_ref[...]
        dxn = dh_v * (gv * scale)
        dx = dr_ref[...] + r * (dxn - xn * jnp.mean(dxn * xn, axis=-1, keepdims=True))
        dx_ref[...] = dx
        do_ref[...] = (dx * gate_ref[...]).astype(BF16)

        @pl.when(i == 0)
        def _():
            dsh_ref[...] = jnp.zeros_like(dsh_ref)
            dsc_ref[...] = jnp.zeros_like(dsc_ref)
            dg_ref[...] = jnp.zeros_like(dg_ref)
            dgate_ref[...] = jnp.zeros_like(dgate_ref)

        hx = dh_v * xn
        dsh_ref[...] += _colsum8(dh_v)
        dsc_ref[...] += _colsum8(hx) * gv
        dg_ref[...] += _colsum8(hx) * scale
        dgate_ref[...] += _colsum8(dx * o_ref[...])

    row = pl.BlockSpec((tm, D), lambda i: (i, 0))
    par = _full((1, D))
    acc = jax.ShapeDtypeStruct((8, D), F32)
    return pl.pallas_call(
        body, name=name,
        out_shape=(jax.ShapeDtypeStruct((S, D), F32), jax.ShapeDtypeStruct((S, D), BF16), acc, acc, acc, acc),
        grid=(S // tm,),
        in_specs=[pl.BlockSpec((tm, wd), lambda i: (i, 0)) for wd in widths]
        + [_full((D, K)), row, row, par, par, par, row, pl.BlockSpec(memory_space=pl.ANY)],
        out_specs=(row, row, _full((8, D)), _full((8, D)), _full((8, D)), _full((8, D))),
        compiler_params=_cp(("arbitrary",)))(*parts, w, xin, dres, g, sc, gate, o, after)


def _me():
    return lax.axis_index("x"), lax.axis_index("y"), lax.axis_index("c")


def _allgather8(v, name, after=()):
    R, C = v.shape
    after = list(after)

    def body(v_ref, *rest):
        out_ref, send_sems, recv_sems, local_sem = rest[len(after):]
        x, y, c = _me()
        mine = pltpu.make_async_copy(v_ref, out_ref.at[4 * x + 2 * y + c], local_sem)
        mine.start()
        copies = []
        for k in range(1, N_DEV):
            fx, fy, fc = (k >> 2) & 1, (k >> 1) & 1, k & 1
            peer = (x ^ fx, y ^ fy, c ^ fc)
            copies.append(pltpu.make_async_remote_copy(
                src_ref=v_ref, dst_ref=out_ref.at[4 * x + 2 * y + c],
                send_sem=send_sems.at[k - 1], recv_sem=recv_sems.at[k - 1],
                device_id=peer, device_id_type=MESH))
        for cp in copies:
            cp.start()
        for k in range(1, N_DEV):
            fx, fy, fc = (k >> 2) & 1, (k >> 1) & 1, k & 1
            src_slot = 4 * (x ^ fx) + 2 * (y ^ fy) + (c ^ fc)
            pltpu.make_async_remote_copy(
                src_ref=v_ref, dst_ref=out_ref.at[src_slot],
                send_sem=send_sems.at[k - 1], recv_sem=recv_sems.at[k - 1],
                device_id=(x ^ fx, y ^ fy, c ^ fc), device_id_type=MESH).wait_recv()
        for cp in copies:
            cp.wait_send()
        mine.wait()

    return pl.pallas_call(
        body, name=name, out_shape=jax.ShapeDtypeStruct((N_DEV, R, C), v.dtype),
        in_specs=[pl.BlockSpec(memory_space=pltpu.VMEM)] + [pl.BlockSpec(memory_space=pl.ANY)] * len(after),
        out_specs=pl.BlockSpec(memory_space=pltpu.VMEM),
        scratch_shapes=[pltpu.SemaphoreType.DMA((N_DEV - 1,)), pltpu.SemaphoreType.DMA((N_DEV - 1,)),
                        pltpu.SemaphoreType.DMA],
        compiler_params=pltpu.CompilerParams(vmem_limit_bytes=VMEM_LIMIT))(v, *after)


def _swap_sibling(arrs):
    nw = len(arrs)

    def body(*refs):
        ins, outs = refs[:nw], refs[nw:2 * nw]
        send_sems, recv_sems = refs[2 * nw:]
        x, y, c = _me()
        copies = [pltpu.make_async_remote_copy(
            src_ref=ins[w], dst_ref=outs[w], send_sem=send_sems.at[w], recv_sem=recv_sems.at[w],
            device_id=(x, y, 1 - c), device_id_type=MESH) for w in range(nw)]
        for cp in copies:
            cp.start()
        for cp in copies:
            cp.wait_recv()
        for cp in copies:
            cp.wait_send()

    hbm = pl.BlockSpec(memory_space=pltpu.HBM)
    return pl.pallas_call(
        body, name="swap_sibling", out_shape=tuple(jax.ShapeDtypeStruct(a.shape, a.dtype) for a in arrs),
        in_specs=[hbm] * nw, out_specs=tuple([hbm] * nw),
        scratch_shapes=[pltpu.SemaphoreType.DMA((nw,)), pltpu.SemaphoreType.DMA((nw,))],
        compiler_params=pltpu.CompilerParams(vmem_limit_bytes=VMEM_LIMIT))(*arrs)


_HBM = pl.BlockSpec(memory_space=pltpu.HBM)
_SEM = pl.BlockSpec(memory_space=pltpu.SEMAPHORE)
_EFFECT = pltpu.SideEffectType.DATAFLOW_SIDE_EFFECTING
_N_PEER = N_CHIP - 1


def _chip_part(ref, axis, n, chip):
    start = pl.multiple_of(chip * n, 8)
    return ref.at[pl.ds(start, n), :] if axis == 0 else ref.at[:, pl.ds(start, n)]


def _gather_copy(k, src_ref, land_ref, send_sems, recv_sems, axis, arriving):
    x, y, c = _me()
    px, py = x ^ ((k >> 1) & 1), y ^ (k & 1)
    chip = 2 * px + py if arriving else 2 * x + y
    return pltpu.make_async_remote_copy(
        src_ref=src_ref, dst_ref=_chip_part(land_ref, axis, src_ref.shape[axis], chip),
        send_sem=send_sems.at[k - 1], recv_sem=recv_sems.at[k - 1], device_id=(px, py, c), device_id_type=MESH)


def _scatter_copy(k, grad_ref, land_ref, send_sems, recv_sems, axis):
    x, y, c = _me()
    px, py = x ^ ((k >> 1) & 1), y ^ (k & 1)
    return pltpu.make_async_remote_copy(
        src_ref=_chip_part(grad_ref, axis, grad_ref.shape[axis] // N_CHIP, 2 * px + py), dst_ref=land_ref.at[k - 1],
        send_sem=send_sems.at[k - 1], recv_sem=recv_sems.at[k - 1], device_id=(px, py, c), device_id_type=MESH)


def _own_copy(src_ref, land_ref, sends, axis):
    x, y, _ = _me()
    return pltpu.make_async_copy(src_ref, _chip_part(land_ref, axis, src_ref.shape[axis], 2 * x + y),
                                 sends.at[_N_PEER])


def _gather_start(shards, axes, after, name):
    nw = len(shards)
    lands = []
    for s, ax in zip(shards, axes):
        shp = list(s.shape)
        shp[ax] *= N_CHIP
        lands.append(lax.empty(tuple(shp), s.dtype))

    def body(*refs):
        srcs, zones = refs[:nw], refs[nw:2 * nw]
        sends, recvs = refs[2 * nw + 1:3 * nw + 1], refs[3 * nw + 1:4 * nw + 1]
        token = refs[-1]
        for w in range(nw):
            for k in range(1, N_CHIP):
                _gather_copy(k, srcs[w], zones[w], sends[w], recvs[w], axes[w], False).start()
        for w in range(nw):
            _own_copy(srcs[w], zones[w], sends[w], axes[w]).start()
        token[...] = jnp.zeros_like(token)

    outs = pl.pallas_call(
        body, name=name,
        out_shape=tuple([pltpu.SemaphoreType.DMA((_N_PEER + 1,))] * nw + [pltpu.SemaphoreType.DMA((_N_PEER,))] * nw
                        + [pltpu.HBM(a.shape, a.dtype) for a in list(shards) + list(lands)]
                        + [jax.ShapeDtypeStruct((8, 128), F32)]),
        in_specs=[_HBM] * (2 * nw) + [pl.BlockSpec(memory_space=pl.ANY)],
        out_specs=tuple([_SEM] * (2 * nw) + [_HBM] * (2 * nw) + [pl.BlockSpec(memory_space=pltpu.VMEM)]),
        input_output_aliases={i: 2 * nw + i for i in range(2 * nw)},
        compiler_params=pltpu.CompilerParams(has_side_effects=_EFFECT),
    )(*([pltpu.with_memory_space_constraint(a, pltpu.HBM) for a in list(shards) + list(lands)] + [after]))
    per_weight = [(outs[w], outs[nw + w], outs[2 * nw + w], outs[3 * nw + w]) for w in range(nw)]
    return per_weight, outs[-1]


def _gather_wait(state, axis, after, name):
    send_sems, recv_sems, shard, land = state

    after = list(after) if isinstance(after, (list, tuple)) else [after]

    def body(src_ref, land_ref, sends, recvs, *rest):
        for k in range(1, N_CHIP):
            _gather_copy(k, src_ref, land_ref, sends, recvs, axis, False).wait_send()
            _gather_copy(k, src_ref, land_ref, sends, recvs, axis, True).wait_recv()
        _own_copy(src_ref, land_ref, sends, axis).wait()

    return pl.pallas_call(
        body, name=name, out_shape=(pltpu.HBM(shard.shape, shard.dtype), pltpu.HBM(land.shape, land.dtype)),
        in_specs=[_HBM, _HBM, _SEM, _SEM] + [pl.BlockSpec(memory_space=pl.ANY)] * len(after), out_specs=(_HBM, _HBM),
        input_output_aliases={0: 0, 1: 1},
        compiler_params=pltpu.CompilerParams(has_side_effects=_EFFECT),
    )(shard, land, send_sems, recv_sems, *after)[1]


def _all8_copy(k, v_ref, land_ref, send_sems, recv_sems, arriving):
    x, y, c = _me()
    px, py, pc = x ^ ((k >> 2) & 1), y ^ ((k >> 1) & 1), c ^ (k & 1)
    slot = 4 * px + 2 * py + pc if arriving else 4 * x + 2 * y + c
    return pltpu.make_async_remote_copy(
        src_ref=v_ref, dst_ref=land_ref.at[slot], send_sem=send_sems.at[k - 1], recv_sem=recv_sems.at[k - 1],
        device_id=(px, py, pc), device_id_type=MESH)


def _all8_own(v_ref, land_ref, send_sems):
    x, y, c = _me()
    return pltpu.make_async_copy(v_ref, land_ref.at[4 * x + 2 * y + c], send_sems.at[N_DEV - 1])


def _all8_start(v, name):
    land = lax.empty((N_DEV,) + v.shape, v.dtype)

    def body(v_ref, land_ref, sends, recvs, v_thru, land_thru, token):
        for k in range(1, N_DEV):
            _all8_copy(k, v_ref, land_ref, sends, recvs, False).start()
        _all8_own(v_ref, land_ref, sends).start()
        token[...] = jnp.zeros_like(token)

    outs = pl.pallas_call(
        body, name=name,
        out_shape=(pltpu.SemaphoreType.DMA((N_DEV,)), pltpu.SemaphoreType.DMA((N_DEV - 1,)),
                   pltpu.HBM(v.shape, v.dtype), pltpu.HBM(land.shape, land.dtype),
                   jax.ShapeDtypeStruct((8, 128), F32)),
        in_specs=[_HBM, _HBM], out_specs=(_SEM, _SEM, _HBM, _HBM, pl.BlockSpec(memory_space=pltpu.VMEM)),
        input_output_aliases={0: 2, 1: 3},
        compiler_params=pltpu.CompilerParams(has_side_effects=_EFFECT),
    )(pltpu.with_memory_space_constraint(v, pltpu.HBM), pltpu.with_memory_space_constraint(land, pltpu.HBM))
    return outs[:4], outs[4]


def _all8_wait(state, after, name):
    send_sems, recv_sems, v, land = state

    def body(v_ref, land_ref, sends, recvs, after_ref, v_dead, got_ref):
        for k in range(1, N_DEV):
            _all8_copy(k, v_ref, land_ref, sends, recvs, False).wait_send()
            _all8_copy(k, v_ref, land_ref, sends, recvs, True).wait_recv()
        _all8_own(v_ref, land_ref, sends).wait()

    return pl.pallas_call(
        body, name=name, out_shape=(pltpu.HBM(v.shape, v.dtype), pltpu.HBM(land.shape, land.dtype)),
        in_specs=[_HBM, _HBM, _SEM, _SEM, pl.BlockSpec(memory_space=pl.ANY)], out_specs=(_HBM, _HBM),
        input_output_aliases={0: 0, 1: 1},
        compiler_params=pltpu.CompilerParams(has_side_effects=_EFFECT),
    )(v, land, send_sems, recv_sems, after)[1]


def _swap_copy(w, src_ref, land_ref, send_sems, recv_sems):
    x, y, c = _me()
    return pltpu.make_async_remote_copy(src_ref=src_ref, dst_ref=land_ref, send_sem=send_sems.at[w],
                                        recv_sem=recv_sems.at[w], device_id=(x, y, 1 - c), device_id_type=MESH)


def _swap_start(arrs, after, name):
    nw = len(arrs)
    lands = [lax.empty(a.shape, a.dtype) for a in arrs]

    def body(*refs):
        srcs, zones = refs[:nw], refs[nw:2 * nw]
        sends, recvs = refs[2 * nw + 1], refs[2 * nw + 2]
        for w in range(nw):
            _swap_copy(w, srcs[w], zones[w], sends, recvs).start()
        refs[-1][...] = jnp.zeros_like(refs[-1])

    sem = pltpu.SemaphoreType.DMA((nw,))
    outs = pl.pallas_call(
        body, name=name,
        out_shape=tuple([sem, sem] + [pltpu.HBM(a.shape, a.dtype) for a in list(arrs) + lands]
                        + [jax.ShapeDtypeStruct((8, 128), F32)]),
        in_specs=[_HBM] * (2 * nw) + [pl.BlockSpec(memory_space=pl.ANY)],
        out_specs=tuple([_SEM, _SEM] + [_HBM] * (2 * nw) + [pl.BlockSpec(memory_space=pltpu.VMEM)]),
        input_output_aliases={i: 2 + i for i in range(2 * nw)},
        compiler_params=pltpu.CompilerParams(has_side_effects=_EFFECT),
    )(*([pltpu.with_memory_space_constraint(a, pltpu.HBM) for a in list(arrs) + lands] + [after]))
    return (outs[0], outs[1], outs[2:2 + nw], outs[2 + nw:2 + 2 * nw]), outs[-1]


def _swap_wait(state, after, name):
    send_sems, recv_sems, arrs, lands = state
    nw = len(arrs)

    def body(*refs):
        srcs, zones = refs[:nw], refs[nw:2 * nw]
        sends, recvs = refs[2 * nw], refs[2 * nw + 1]
        for w in range(nw):
            cp = _swap_copy(w, srcs[w], zones[w], sends, recvs)
            cp.wait_send()
            cp.wait_recv()

    outs = pl.pallas_call(
        body, name=name, out_shape=tuple(pltpu.HBM(a.shape, a.dtype) for a in list(arrs) + list(lands)),
        in_specs=[_HBM] * (2 * nw) + [_SEM, _SEM, pl.BlockSpec(memory_space=pl.ANY)],
        out_specs=tuple([_HBM] * (2 * nw)),
        input_output_aliases={i: i for i in range(2 * nw)},
        compiler_params=pltpu.CompilerParams(has_side_effects=_EFFECT),
    )(*arrs, *lands, send_sems, recv_sems, after)
    return list(outs[:nw]), list(outs[nw:])


def _scatter_start(grad, axis, name):
    shp = list(grad.shape)
    shp[axis] //= N_CHIP
    land = lax.empty((_N_PEER,) + tuple(shp), grad.dtype)

    def body(grad_ref, land_ref, sends, recvs, grad_thru, land_thru, token):
        for k in range(1, N_CHIP):
            _scatter_copy(k, grad_ref, land_ref, sends, recvs, axis).start()
        token[...] = jnp.zeros_like(token)

    sem = pltpu.SemaphoreType.DMA((_N_PEER,))
    outs = pl.pallas_call(
        body, name=name,
        out_shape=(sem, sem, pltpu.HBM(grad.shape, grad.dtype), pltpu.HBM(land.shape, land.dtype),
                   jax.ShapeDtypeStruct((8, 128), F32)),
        in_specs=[_HBM, _HBM], out_specs=(_SEM, _SEM, _HBM, _HBM, pl.BlockSpec(memory_space=pltpu.VMEM)),
        input_output_aliases={0: 2, 1: 3},
        compiler_params=pltpu.CompilerParams(has_side_effects=_EFFECT),
    )(pltpu.with_memory_space_constraint(grad, pltpu.HBM), pltpu.with_memory_space_constraint(land, pltpu.HBM))
    return outs[:4], outs[4]


def _scatter_wait(state, axis, after, name):
    send_sems, recv_sems, grad, land = state

    def body(grad_ref, land_ref, sends, recvs, after_ref, grad_dead, got_ref):
        for k in range(1, N_CHIP):
            cp = _scatter_copy(k, grad_ref, land_ref, sends, recvs, axis)
            cp.wait_send()
            cp.wait_recv()

    return pl.pallas_call(
        body, name=name, out_shape=(pltpu.HBM(grad.shape, grad.dtype), pltpu.HBM(land.shape, land.dtype)),
        in_specs=[_HBM, _HBM, _SEM, _SEM, pl.BlockSpec(memory_space=pl.ANY)], out_specs=(_HBM, _HBM),
        input_output_aliases={0: 0, 1: 1},
        compiler_params=pltpu.CompilerParams(has_side_effects=_EFFECT),
    )(grad, land, send_sems, recv_sems, after)[1]


_C1 = 1.0 - B1 ** STEP
_C2 = 1.0 - B2 ** STEP


def _adam_math(w, g, m, v):
    m = B1 * m + (1.0 - B1) * g
    v = B2 * v + (1.0 - B2) * (g * g)
    delta = -LR * ((m / _C1) / (jnp.sqrt(v / _C2) + AEPS) + WD * w)
    return delta, m, v


def _adamw(w, m, v, groups, name):
    R, C = w.shape
    tr = R if R <= 256 else (128 if R % 128 == 0 else 176)
    assert R % tr == 0, (name, R)
    gparts = [p for grp in groups for p in grp]
    sizes = [len(grp) for grp in groups]
    ng = len(gparts)

    def body(*refs):
        w_ref, m_ref, v_ref = refs[:3]
        g_refs = list(refs[3:3 + ng])
        g_out, d_out, m_out, v_out = refs[3 + ng:]
        g = None
        for size in sizes:
            s = None
            for r in [g_refs.pop(0) for _ in range(size)]:
                terms = [r[q] for q in range(r.shape[0])] if len(r.shape) == 3 else [r[...]]
                for t in terms:
                    s = t.astype(F32) if s is None else s + t.astype(F32)
            g = s if g is None else g + s
        delta, mn, vn = _adam_math(w_ref[...], g, m_ref[...], v_ref[...])
        g_out[...] = g
        d_out[...] = delta
        m_out[...] = mn
        v_out[...] = vn

    blk = pl.BlockSpec((tr, C), lambda i: (i, 0))
    g_specs = [blk if p.ndim == 2 else pl.BlockSpec((p.shape[0], tr, C), lambda i: (0, i, 0)) for p in gparts]
    sds = jax.ShapeDtypeStruct((R, C), F32)
    return pl.pallas_call(
        body, name=name, out_shape=(sds, sds, sds, sds), grid=(R // tr,),
        in_specs=[blk, blk, blk] + g_specs, out_specs=(blk, blk, blk, blk),
        compiler_params=_cp(("parallel",)))(w, m, v, *gparts)


def _mod_shard(c_all, w_ada, b_ada_cols):
    n = w_ada.shape[1]
    tn = 512

    def body(c_ref, w_ref, b_ref, o_ref):
        cv = c_ref[...]
        ca = (cv * _sig(cv)).astype(BF16)
        o_ref[...] = jnp.dot(ca, w_ref[...].astype(BF16), preferred_element_type=F32) + b_ref[...]

    return pl.pallas_call(
        body, name="mod_shard", out_shape=jax.ShapeDtypeStruct((N_DEV, n), F32), grid=(n // tn,),
        in_specs=[_full((N_DEV, D_MODEL)), pl.BlockSpec((D_MODEL, tn), lambda j: (0, j)),
                  pl.BlockSpec((1, tn), lambda j: (0, j))],
        out_specs=pl.BlockSpec((N_DEV, tn), lambda j: (0, j)),
        compiler_params=_cp(("parallel",)))(c_all, w_ada, b_ada_cols)


def _ada_grad(c_all, dmod_cols):
    n = dmod_cols.shape[1]
    tn = 512

    def body(c_ref, d_ref, o_ref):
        cv = c_ref[...]
        ca = cv * _sig(cv)
        o_ref[...] = lax.dot_general(ca, d_ref[...], (((0,), (0,)), ((), ())),
                                     preferred_element_type=F32, precision=lax.Precision.HIGHEST)

    return pl.pallas_call(
        body, name="ada_grad", out_shape=jax.ShapeDtypeStruct((D_MODEL, n), F32), grid=(n // tn,),
        in_specs=[_full((N_DEV, D_MODEL)), pl.BlockSpec((N_DEV, tn), lambda j: (0, j))],
        out_specs=pl.BlockSpec((D_MODEL, tn), lambda j: (0, j)),
        compiler_params=_cp(("parallel",)))(c_all, dmod_cols)


def _ssm_tables(W):
    e_re, e_im, bb_re, bb_im = _ssm_prep(W["ssm_a_re"], W["ssm_a_im"], W["ssm_b_re"], W["ssm_b_im"], W["ssm_log_dt"])
    bb, cm = _block_diag_mats(bb_re, bb_im, W["ssm_c_re"], W["ssm_c_im"])
    bb16, cm16 = bb.astype(BF16), cm.astype(BF16)
    return (bb16, cm16, jnp.swapaxes(bb16, 1, 2), jnp.swapaxes(cm16, 1, 2),
            _scan_tables(e_re, e_im, False), _scan_tables(e_re, e_im, True))


def _device_step(x, mod, W, tables, tgt, getw, put, early):
    sh1, sc1, g1, sh2, sc2, g2 = [mod[:, i * D_MODEL:(i + 1) * D_MODEL] for i in range(6)]
    bb16, cm16, bbt16, cmt16, tab_f, tab_b = tables

    w_in = getw("w_in", mod)
    h1, z = _in_proj(x, W["norm1_g"], sc1, sh1, w_in)
    yc, scv = _conv_fwd(z, W["conv_w"], W["conv_b"], W["conv_ln_g"], W["conv_ln_b"])
    xs, ys, yg = _ssm_fwd(z, bb16, cm16, W["ssm_d"], tab_f)
    w_cp, w_glu, w_out = getw("conv_proj", scv), getw("ssm_glu", yg), getw("w_out", yg)
    y_conv, zz, merged, o, x2, h2 = _mix_fwd(scv, yg, z, x, w_cp, w_glu, w_out, g1, W["norm2_g"], sc2, sh2)
    w_fi = getw("w_ffn_in", h2)
    f, act = _ffn_in_act(h2, w_fi)
    w_fo = getw("w_ffn_out", act)
    dx3, do2, loss8, dfg8, dg2_8 = _ffn_out_final(x2, act, w_fo, g2, W["final_g"], tgt)

    sm = {}
    tok = put("w_ffn_out", _matmul(act, do2, "tn", 1408, 1024, 2048, BF16, "mm_g_ffn_out"))
    df = _ffn_bwd(do2, w_fo, f, tok)
    tok = put("w_ffn_in", _matmul(h2, df, "tn", 1024, 1408, 2048, BF16, "mm_g_ffn_in"))
    dx2, do, dsh2, dsc2, dn2, dg1_8 = _normmod_bwd(df, w_fi, x2, dx3, W["norm2_g"], sc2, g1, o, tok, "d_h2_normmod2_bwd")
    tok = put("w_out", _matmul(merged, do, "tn", 1024, 1024, 4096, BF16, "mm_g_w_out"))
    dyconv, dgl, dzz = _mix_bwd(do, w_out, z, zz, y_conv, tok)
    tok = put("ssm_glu", _matmul(yg, dzz, "tn", 512, 1024, 4096, BF16, "mm_g_ssm_glu"))
    tok = put("conv_proj", _matmul(scv, dyconv, "tn", 512, 1024, 4096, BF16, "mm_g_conv_proj", after=tok))
    du, de16, dd8, dc_full, dbb_full = _ssm_bwd(dzz, w_glu, ys, z, xs, cmt16, bbt16, W["ssm_d"], tab_b, tok)
    dyc, dlg8, dlb8, dcb8 = _conv_bwd_ln(dyconv, w_cp, yc, W["conv_ln_g"], W["conv_ln_b"])
    dz_conv, dcw = _conv_bwd(dyc, z, W["conv_w"])

    s8 = lambda a: jnp.sum(a, axis=0, keepdims=True)
    de = de16.reshape(2, 8, NST).sum(1)
    de_re, de_im = de[0].reshape(G, P), de[1].reshape(G, P)
    dc_re = _diag_blocks(dc_full, False)
    dc_im = -_diag_blocks(dc_full, True)
    dbb_re = jnp.swapaxes(_diag_blocks(dbb_full, False), 1, 2)
    dbb_im = jnp.swapaxes(_diag_blocks(dbb_full, True), 1, 2)
    _, vjp = jax.vjp(_ssm_prep, W["ssm_a_re"], W["ssm_a_im"], W["ssm_b_re"], W["ssm_b_im"], W["ssm_log_dt"])
    sm["ssm_a_re"], sm["ssm_a_im"], sm["ssm_b_re"], sm["ssm_b_im"], sm["ssm_log_dt"] = vjp((de_re, de_im, dbb_re, dbb_im))
    sm["ssm_c_re"], sm["ssm_c_im"] = dc_re, dc_im
    sm["ssm_d"] = s8(dd8)
    sm["norm2_g"] = s8(dn2)
    sm["conv_b"], sm["conv_ln_g"], sm["conv_ln_b"] = s8(dcb8), s8(dlg8), s8(dlb8)
    sm["conv_w"] = dcw.reshape(KW, 8, CW).sum(1)
    sm["final_g"] = s8(dfg8)
    tok = early(sm)

    dz = [dz_conv, du, dgl]
    tok = put("w_in", _matmul(h1, dz, "tn", 1024, 512, 4096, BF16, "mm_g_w_in", after=tok))
    dx, _, dsh1, dsc1, dn1, _ = _normmod_bwd(dz, w_in, x, dx2, W["norm1_g"], sc1, g1, o, tok, "d_h1_normmod1_bwd")
    dmod = jnp.concatenate([s8(dsh1), s8(dsc1), s8(dg1_8), s8(dsh2), s8(dsc2), s8(dg2_8)], axis=1)
    return loss8, dx, s8(dn1), dmod


_BIG = ("w_in", "conv_proj", "ssm_glu", "w_out", "w_ffn_in", "w_ffn_out")
_BIG_AXIS = {"w_in": 1, "conv_proj": 1, "ssm_glu": 1, "w_out": 0, "w_ffn_in": 1, "w_ffn_out": 0}
_EARLY = ("conv_w", "conv_b", "conv_ln_g", "conv_ln_b", "ssm_a_re", "ssm_a_im", "ssm_b_re", "ssm_b_im", "ssm_c_re",
          "ssm_c_im", "ssm_d", "ssm_log_dt", "norm2_g", "final_g")
_LATE = ("norm1_g", "b_ada")
_ORDER = ("w_ada", "b_ada", "norm1_g", "w_in", "conv_w", "conv_b", "conv_ln_g", "conv_ln_b", "conv_proj",
          "ssm_a_re", "ssm_a_im", "ssm_b_re", "ssm_b_im", "ssm_c_re", "ssm_c_im", "ssm_d", "ssm_log_dt", "ssm_glu",
          "w_out", "norm2_g", "w_ffn_in", "w_ffn_out", "final_g")
_PACK_COLS = 1024


def _pack_rows(shape):
    return -(-int(np.prod(shape)) // (8 * _PACK_COLS)) * 8


def _pack(arrs):
    parts = []
    for a in arrs:
        flat = a.reshape(-1)
        n = _pack_rows(a.shape)
        parts.append(jnp.pad(flat, (0, n * _PACK_COLS - flat.shape[0])).reshape(n, _PACK_COLS))
    return jnp.concatenate(parts, 0)


def _unpack(packed, shapes):
    out, r = [], 0
    for shp in shapes:
        size = int(np.prod(shp))
        n = _pack_rows(shp)
        out.append(packed[r:r + n].reshape(-1)[:size].reshape(shp))
        r += n
    return out


def kernel(x, c, w_ada, b_ada, norm1_g, w_in, conv_w, conv_b, conv_ln_g, conv_ln_b, conv_proj, ssm_a_re, ssm_a_im, ssm_b_re, ssm_b_im, ssm_c_re, ssm_c_im, ssm_d, ssm_log_dt, ssm_glu, w_out, norm2_g, w_ffn_in, w_ffn_out, final_g, loss_target, m_w_ada, m_b_ada, m_norm1_g, m_w_in, m_conv_w, m_conv_b, m_conv_ln_g, m_conv_ln_b, m_conv_proj, m_ssm_a_re, m_ssm_a_im, m_ssm_b_re, m_ssm_b_im, m_ssm_c_re, m_ssm_c_im, m_ssm_d, m_ssm_log_dt, m_ssm_glu, m_w_out, m_norm2_g, m_w_ffn_in, m_w_ffn_out, m_final_g, v_w_ada, v_b_ada, v_norm1_g, v_w_in, v_conv_w, v_conv_b, v_conv_ln_g, v_conv_ln_b, v_conv_proj, v_ssm_a_re, v_ssm_a_im, v_ssm_b_re, v_ssm_b_im, v_ssm_c_re, v_ssm_c_im, v_ssm_d, v_ssm_log_dt, v_ssm_glu, v_w_out, v_norm2_g, v_w_ffn_in, v_w_ffn_out, v_final_g):
    given = dict(locals())
    mx, my, mc = _me()
    chip = 2 * mx + my
    dev = 4 * mx + 2 * my + mc
    def canon(a):
        return a.reshape(1, -1) if a.ndim <= 2 else a[0]

    wts = {n: canon(given[n]) for n in _ORDER}
    mom = {n: canon(given["m_" + n]) for n in _ORDER}
    var = {n: canon(given["v_" + n]) for n in _ORDER}

    W = {n: wts[n] for n in _ORDER if n not in _BIG}
    gstate, token = _gather_start([wts["w_in"].astype(BF16)], [_BIG_AXIS["w_in"]], c, "gather_start_w_in")
    gstate = {"w_in": gstate[0]}
    W["ssm_log_dt"] = wts["ssm_log_dt"] + token[0:1, 0:1]
    W["ssm_c_re"] = wts["ssm_c_re"] + token[0, 0]
    tables = _ssm_tables(W)

    c_all = _allgather8(jnp.broadcast_to(c, (8, D_MODEL)), "gather_c", after=tables)[:, 0, :]
    n_ada = wts["w_ada"].shape[1]
    b_cols = lax.dynamic_slice(wts["b_ada"], (0, chip * n_ada), (1, n_ada))
    mod_cols = _mod_shard(c_all, wts["w_ada"], b_cols)
    mods = _allgather8(mod_cols, "gather_mod")
    mod = jnp.concatenate([lax.dynamic_index_in_dim(mods[2 * q], dev, 0, keepdims=True) for q in range(N_CHIP)], axis=1)
    conv_w_full = _allgather8(jnp.pad(wts["conv_w"], ((0, 1), (0, 0))), "gather_conv_w", after=[c_all])
    W["conv_w"] = jnp.concatenate([conv_w_full[2 * q, :KW] for q in range(N_CHIP)], axis=1)

    rest = [n for n in _BIG if n != "w_in"]
    rstate, token = _gather_start([wts[n].astype(BF16) for n in rest], [_BIG_AXIS[n] for n in rest],
                                  mod + W["conv_w"][0:1, 0:1], "gather_start_rest")
    gstate.update(zip(rest, rstate))
    mod = mod + token[0:1, 0:1]

    def getw(n, after):
        return _gather_wait(gstate[n], _BIG_AXIS[n], after, "gather_wait_" + n)

    sstate, own, estate = {}, {}, []

    def put(n, g):
        ax = _BIG_AXIS[n]
        k = g.shape[ax] // N_CHIP
        own[n] = lax.dynamic_slice_in_dim(g, chip * k, k, axis=ax)
        sstate[n], tok = _scatter_start(g, ax, "scatter_start_" + n)
        return tok

    first5 = [n for n in _BIG if n != "w_in"]

    def early(sm):
        state, tok = _all8_start(_pack([sm[n] for n in _EARLY]), "small_start")
        estate.append(state)
        recv5 = [_scatter_wait(sstate[n], _BIG_AXIS[n], tok, "scatter_wait_" + n) for n in first5]
        held = [a for n, r in zip(first5, recv5) for a in (own[n], r)]
        state, tok = _swap_start(held, tok, "swap_start")
        estate.append(state)
        return tok

    loss8, dx, dn1, dmod = _device_step(x[0], mod, W, tables, loss_target[0], getw, put, early)

    held5, sib5 = _swap_wait(estate[1], dx, "swap_wait")
    outs = {}
    for i, n in enumerate(first5):
        outs[n] = _adamw(wts[n], mom[n], var[n], [held5[2 * i:2 * i + 2], sib5[2 * i:2 * i + 2]], "adamw_" + n)
    allp = _all8_wait(estate[0], dx, "small_wait")

    late = _allgather8(_pack([dn1, dmod, loss8]), "gather_late", after=[outs[n][1] for n in first5])
    n_late = _pack_rows((D_MODEL,)) + _pack_rows((6 * D_MODEL,))
    loss = jnp.sum(late[:, n_late:, :])
    late = late[:, :n_late, :]
    held_in = [own["w_in"], _scatter_wait(sstate["w_in"], _BIG_AXIS["w_in"], late, "scatter_wait_w_in")]
    sib_in = _swap_sibling(held_in)
    outs["w_in"] = _adamw(wts["w_in"], mom["w_in"], var["w_in"], [held_in, sib_in], "adamw_w_in")

    r1 = _pack_rows((D_MODEL,))
    dmod_all = late[:, r1:, :].reshape(N_DEV, -1)[:, :6 * D_MODEL]
    dmod_cols = lax.dynamic_slice(dmod_all, (0, chip * n_ada), (N_DEV, n_ada))
    g_ada = _ada_grad(c_all, dmod_cols)
    outs["w_ada"] = _adamw(wts["w_ada"], mom["w_ada"], var["w_ada"], [[g_ada]], "adamw_w_ada")

    def packed_params(d, names):
        return _pack([jnp.zeros((KW, CW), F32) if n == "conv_w" else d[n] for n in names])

    for names, parts, nm in ((_EARLY, allp, "adamw_small"), (_LATE, late, "adamw_late")):
        res = _adamw(packed_params(wts, names), packed_params(mom, names), packed_params(var, names), [[parts]], nm)
        shapes = [(KW, CW) if n == "conv_w" else wts[n].shape for n in names]
        unpacked = [_unpack(r, shapes) for r in res]
        for idx, n in enumerate(names):
            outs[n] = tuple(unpacked[q][idx] for q in range(4))
    g_cw = lax.dynamic_slice(outs["conv_w"][0], (0, chip * (CW // N_CHIP)), (KW, CW // N_CHIP))
    pad = lambda a: jnp.pad(a, ((0, 1), (0, 0)))
    r_cw = _adamw(pad(wts["conv_w"]), pad(mom["conv_w"]), pad(var["conv_w"]), [[pad(g_cw)]], "adamw_conv_w")
    outs["conv_w"] = tuple(r[:KW] for r in r_cw)

    def shaped(n, a):
        return a.reshape(given[n].shape)

    result = [loss, dx[None]]
    for q in range(4):
        result += [shaped(n, outs[n][q]) for n in _ORDER]
    return tuple(result)
```

```python
import math

import jax
import jax.numpy as jnp
import numpy as np
from jax import lax
from jax.experimental import pallas as pl
from jax.experimental.pallas import tpu as pltpu

F32 = jnp.float32
BF16 = jnp.bfloat16
EPS = 1e-6
D_MODEL = 1024
CW = 512
KW = 31
HALO = 32
G, P, H = 32, 64, 16
NST = G * P
FH = 2816
N_DEV = 8
N_CHIP = 4
VMEM_LIMIT = 56 * 1024 * 1024
LR, B1, B2, AEPS, WD, STEP = 0.001, 0.9, 0.999, 1e-08, 0.01, 10
MESH = pl.DeviceIdType.MESH


def _cp(sem=None):
    return pltpu.CompilerParams(dimension_semantics=sem, vmem_limit_bytes=VMEM_LIMIT)


def _sig(x):
    return jax.nn.sigmoid(x)


def _full(shape):
    return pl.BlockSpec(shape, lambda *_: (0,) * len(shape))


def _colsum8(v):
    t, c = v.shape
    return jnp.sum(v.reshape(t // 8, 8, c), axis=0)


def _matmul(a, b, mode, tm, tn, tk, out_dtype, name, after=None, n_outer=False, m_cols=None):
    m0 = 0
    b_parts = list(b) if isinstance(b, (list, tuple)) else [b]
    if mode == "nn":
        (M, K), N = a.shape, b.shape[1]
    elif mode == "nt":
        (M, K), N = a.shape, b.shape[0]
    else:
        (K, M), N = a.shape, sum(p.shape[1] for p in b_parts)
        if m_cols is not None:
            m0, M = m_cols
    tm, tn, tk = min(tm, M), min(tn, N), min(tk, K)
    assert M % tm == 0 and N % tn == 0 and K % tk == 0 and m0 % tm == 0, (name, M, N, K, tm, tn, tk)
    assert len(b_parts) == 1 or (mode == "tn" and all(p.shape[1] % tn == 0 for p in b_parts)), name
    nk = K // tk
    mb = m0 // tm
    counts = [p.shape[1] // tn for p in b_parts] if mode == "tn" else [N // tn]
    starts = [sum(counts[:p]) for p in range(len(counts))]

    def ij(fn):
        return (lambda j, i, k: fn(i, j, k)) if n_outer else fn

    if mode == "nn":
        a_spec = pl.BlockSpec((tm, tk), ij(lambda i, j, k: (i, k)))
        b_spec = pl.BlockSpec((tk, tn), ij(lambda i, j, k: (k, j)))
        dims = (((1,), (0,)), ((), ()))
    elif mode == "nt":
        a_spec = pl.BlockSpec((tm, tk), ij(lambda i, j, k: (i, k)))
        b_spec = pl.BlockSpec((tn, tk), ij(lambda i, j, k: (j, k)))
        dims = (((1,), (1,)), ((), ()))
    else:
        a_spec = pl.BlockSpec((tk, tm), ij(lambda i, j, k: (k, i + mb)))
        dims = (((0,), (0,)), ((), ()))
    if mode == "tn":
        b_specs = [pl.BlockSpec((tk, tn), ij(lambda i, j, k, s=s, n=n: (k, jnp.clip(j - s, 0, n - 1))))
                   for s, n in zip(starts, counts)]
    else:
        b_specs = [b_spec]
    nb = len(b_parts)

    def body(a_ref, *rest):
        b_refs = rest[:nb]
        o_ref, acc_ref = rest[-2:]
        j = pl.program_id(0 if n_outer else 1)
        k = pl.program_id(2)

        def compute(b_ref):
            part = lax.dot_general(a_ref[...].astype(BF16), b_ref[...].astype(BF16), dims,
                                   preferred_element_type=F32)
            if nk == 1:
                o_ref[...] = part.astype(out_dtype)
            else:
                @pl.when(k == 0)
                def _():
                    acc_ref[...] = part

                @pl.when(k > 0)
                def _():
                    acc_ref[...] += part

                @pl.when(k == nk - 1)
                def _():
                    o_ref[...] = acc_ref[...].astype(out_dtype)

        if nb == 1:
            compute(b_refs[0])
        else:
            for p in range(nb):
                pl.when(jnp.logical_and(j >= starts[p], j < starts[p] + counts[p]))(
                    lambda b_ref=b_refs[p]: compute(b_ref))

    return pl.pallas_call(
        body, name=name,
        out_shape=jax.ShapeDtypeStruct((M, N), out_dtype),
        grid=(N // tn, M // tm, nk) if n_outer else (M // tm, N // tn, nk),
        in_specs=[a_spec] + b_specs + ([] if after is None else [pl.BlockSpec(memory_space=pl.ANY)]),
        out_specs=pl.BlockSpec((tm, tn), ij(lambda i, j, k: (i, j))),
        scratch_shapes=[pltpu.VMEM((tm, tn) if nk > 1 else (8, 128), F32)],
        compiler_params=_cp(("parallel", "parallel", "arbitrary")),
    )(*([a] + b_parts + ([] if after is None else [after])))


def _row_tile(S):
    return min(512, S)


def _in_proj(x, g, sc, sh, w_in):
    S, D = x.shape
    N = w_in.shape[1]
    tm = min(256, S)

    def body(x_ref, g_ref, sc_ref, sh_ref, w_ref, h_ref, z_ref):
        xv = x_ref[...]
        r = lax.rsqrt(jnp.mean(xv * xv, axis=-1, keepdims=True) + EPS)
        h = (xv * r * (g_ref[...] * (1.0 + sc_ref[...])) + sh_ref[...]).astype(BF16)
        h_ref[...] = h
        z_ref[...] = jnp.dot(h, w_ref[...], preferred_element_type=F32).astype(BF16)

    row = pl.BlockSpec((tm, D), lambda i: (i, 0))
    par = _full((1, D))
    return pl.pallas_call(
        body, name="in_proj",
        out_shape=(jax.ShapeDtypeStruct((S, D), BF16), jax.ShapeDtypeStruct((S, N), BF16)), grid=(S // tm,),
        in_specs=[row, par, par, par, _full((D, N))], out_specs=(row, pl.BlockSpec((tm, N), lambda i: (i, 0))),
        compiler_params=_cp(("parallel",)))(x, g, sc, sh, w_in)


def _fill_shifted(buf_ref, sh_ref):
    n = buf_ref.shape[0] - 8
    for s in range(1, 8):
        sh_ref[s, 0:n, :] = buf_ref[s:s + n, :]


def _window(buf_ref, sh_ref, off, n):
    s = off % 8
    return buf_ref[off:off + n, :] if s == 0 else sh_ref[s, off - s:off - s + n, :]


def _conv_fwd(z, conv_w, conv_b, ln_g, ln_b):
    S = z.shape[0]
    tm = min(128, S)
    sub = 32
    hb = tm // HALO

    def body(a_ref, g_ref, ha_ref, hg_ref, w_ref, b_ref, lg_ref, lb_ref, yc_ref, s_ref, ug_ref, sh_ref):
        i = pl.program_id(0)
        halo = ha_ref[...].astype(F32) * _sig(hg_ref[...].astype(F32))
        ug_ref[0:HALO, :] = jnp.where(i == 0, 0.0, halo)
        ug_ref[HALO:, :] = a_ref[...].astype(F32) * _sig(g_ref[...].astype(F32))
        _fill_shifted(ug_ref, sh_ref)
        for rb in range(tm // sub):
            acc = jnp.zeros((sub, CW), F32) + b_ref[...]
            for k in range(KW):
                off = rb * sub + HALO - (KW - 1) + k
                acc = acc + w_ref[k:k + 1, :] * _window(ug_ref, sh_ref, off, sub)
            yc_ref[rb * sub:(rb + 1) * sub, :] = acc
            mu = jnp.mean(acc, axis=-1, keepdims=True)
            cen = acc - mu
            rstd = lax.rsqrt(jnp.mean(cen * cen, axis=-1, keepdims=True) + EPS)
            ln = cen * rstd * lg_ref[...] + lb_ref[...]
            s_ref[rb * sub:(rb + 1) * sub, :] = (ln * _sig(ln)).astype(BF16)

    prev = lambda i: (jnp.maximum(i * hb - 1, 0), 0)
    return pl.pallas_call(
        body, name="conv_fwd",
        out_shape=(jax.ShapeDtypeStruct((S, CW), F32), jax.ShapeDtypeStruct((S, CW), BF16)),
        grid=(S // tm,),
        in_specs=[pl.BlockSpec((tm, CW), lambda i: (i, 0)), pl.BlockSpec((tm, CW), lambda i: (i, 1)),
                  pl.BlockSpec((HALO, CW), prev), pl.BlockSpec((HALO, CW), lambda i: (jnp.maximum(i * hb - 1, 0), 1)),
                  _full((KW, CW)), _full((1, CW)), _full((1, CW)), _full((1, CW))],
        out_specs=(pl.BlockSpec((tm, CW), lambda i: (i, 0)), pl.BlockSpec((tm, CW), lambda i: (i, 0))),
        scratch_shapes=[pltpu.VMEM((tm + HALO, CW), F32), pltpu.VMEM((8, tm + HALO, CW), F32)],
        compiler_params=_cp(("parallel",)))(z, z, z, z, conv_w, conv_b, ln_g, ln_b)


def _conv_bwd_ln(dyconv, w_cp, yc, ln_g, ln_b):
    S = yc.shape[0]
    tm = _row_tile(S)

    def body(dy_ref, w_ref, yc_ref, lg_ref, lb_ref, dyc_ref, dlg_ref, dlb_ref, dcb_ref):
        i = pl.program_id(0)
        dsc = lax.dot_general(dy_ref[...], w_ref[...], (((1,), (1,)), ((), ())), preferred_element_type=F32)
        yc_v = yc_ref[...]
        mu = jnp.mean(yc_v, axis=-1, keepdims=True)
        cen = yc_v - mu
        rstd = lax.rsqrt(jnp.mean(cen * cen, axis=-1, keepdims=True) + EPS)
        yn = cen * rstd
        ln = yn * lg_ref[...] + lb_ref[...]
        sl = _sig(ln)
        dln = dsc * (sl * (1.0 + ln * (1.0 - sl)))
        dyn = dln * lg_ref[...]
        dyc = rstd * (dyn - jnp.mean(dyn, axis=-1, keepdims=True)
                      - yn * jnp.mean(dyn * yn, axis=-1, keepdims=True))
        dyc_ref[...] = dyc

        @pl.when(i == 0)
        def _():
            dlg_ref[...] = jnp.zeros_like(dlg_ref)
            dlb_ref[...] = jnp.zeros_like(dlb_ref)
            dcb_ref[...] = jnp.zeros_like(dcb_ref)

        dlg_ref[...] += _colsum8(dln * yn)
        dlb_ref[...] += _colsum8(dln)
        dcb_ref[...] += _colsum8(dyc)

    row = pl.BlockSpec((tm, CW), lambda i: (i, 0))
    acc = jax.ShapeDtypeStruct((8, CW), F32)
    return pl.pallas_call(
        body, name="conv_bwd_ln",
        out_shape=(jax.ShapeDtypeStruct((S, CW), F32), acc, acc, acc), grid=(S // tm,),
        in_specs=[pl.BlockSpec((tm, D_MODEL), lambda i: (i, 0)), _full((CW, D_MODEL)), row, _full((1, CW)),
                  _full((1, CW))],
        out_specs=(row, _full((8, CW)), _full((8, CW)), _full((8, CW))),
        compiler_params=_cp(("arbitrary",)))(dyconv, w_cp, yc, ln_g, ln_b)


def _conv_bwd(dyc, z, conv_w):
    S = z.shape[0]
    tm = min(128, S)
    sub = 32
    hb = tm // HALO
    nt = S // tm

    def body(d_ref, dn_ref, a_ref, g_ref, ha_ref, hg_ref, w_ref, dz_ref, dw_ref, ug_ref, dy_ref, ugs_ref, dys_ref):
        i = pl.program_id(0)
        halo = ha_ref[...].astype(F32) * _sig(hg_ref[...].astype(F32))
        ug_ref[0:HALO, :] = jnp.where(i == 0, 0.0, halo)
        a = a_ref[...].astype(F32)
        sg = _sig(g_ref[...].astype(F32))
        ug_ref[HALO:, :] = a * sg
        dy_ref[0:tm, :] = d_ref[...]
        dy_ref[tm:, :] = jnp.where(i == nt - 1, 0.0, dn_ref[...])
        _fill_shifted(ug_ref, ugs_ref)
        _fill_shifted(dy_ref, dys_ref)

        @pl.when(i == 0)
        def _():
            dw_ref[...] = jnp.zeros_like(dw_ref)

        for rb in range(tm // sub):
            r0 = rb * sub
            acc = jnp.zeros((sub, CW), F32)
            dyc_b = dy_ref[r0:r0 + sub, :]
            for k in range(KW):
                up = r0 + (KW - 1) - k
                acc = acc + w_ref[k:k + 1, :] * _window(dy_ref, dys_ref, up, sub)
                off = r0 + HALO - (KW - 1) + k
                dw_ref[k * 8:(k + 1) * 8, :] += _colsum8(dyc_b * _window(ug_ref, ugs_ref, off, sub))
            a_b = a[r0:r0 + sub, :]
            sg_b = sg[r0:r0 + sub, :]
            dz_ref[r0:r0 + sub, 0:CW] = (acc * sg_b).astype(BF16)
            dz_ref[r0:r0 + sub, CW:2 * CW] = (acc * a_b * sg_b * (1.0 - sg_b)).astype(BF16)

    return pl.pallas_call(
        body, name="conv_bwd",
        out_shape=(jax.ShapeDtypeStruct((S, 2 * CW), BF16), jax.ShapeDtypeStruct((KW * 8, CW), F32)),
        grid=(nt,),
        in_specs=[pl.BlockSpec((tm, CW), lambda i: (i, 0)),
                  pl.BlockSpec((HALO, CW), lambda i: (jnp.minimum((i + 1) * hb, nt * hb - 1), 0)),
                  pl.BlockSpec((tm, CW), lambda i: (i, 0)), pl.BlockSpec((tm, CW), lambda i: (i, 1)),
                  pl.BlockSpec((HALO, CW), lambda i: (jnp.maximum(i * hb - 1, 0), 0)),
                  pl.BlockSpec((HALO, CW), lambda i: (jnp.maximum(i * hb - 1, 0), 1)),
                  _full((KW, CW))],
        out_specs=(pl.BlockSpec((tm, 2 * CW), lambda i: (i, 0)), _full((KW * 8, CW))),
        scratch_shapes=[pltpu.VMEM((tm + HALO, CW), F32), pltpu.VMEM((tm + HALO, CW), F32),
                        pltpu.VMEM((8, tm + HALO, CW), F32), pltpu.VMEM((8, tm + HALO, CW), F32)],
        compiler_params=_cp(("arbitrary",)))(dyc, dyc, z, z, z, z, conv_w)


_GELU_C = math.sqrt(2.0 / math.pi)


def _gelu(x):
    return 0.5 * x * (1.0 + jnp.tanh(_GELU_C * (x + 0.044715 * x * x * x)))


def _gelu_grad(x):
    t = jnp.tanh(_GELU_C * (x + 0.044715 * x * x * x))
    return 0.5 * (1.0 + t) + 0.5 * x * (1.0 - t * t) * (_GELU_C * (1.0 + 3 * 0.044715 * x * x))


_NCL = 4
_UC = CW // _NCL
_LW = NST // _NCL
_CS = 2 * _LW


def _ssm_fwd(z, bb, cm, d, tab):
    S = z.shape[0]
    tm = min(256, S)

    def body(u_ref, bb_ref, cm_ref, d_ref, t_ref, x_ref, ys_ref, yg_ref, car_ref):
        i = pl.program_id(0)

        @pl.when(i == 0)
        def _():
            car_ref[...] = jnp.zeros_like(car_ref)

        u16 = u_ref[...]
        u = u16.astype(F32)
        for c in range(_NCL):
            lre = pl.ds(c * _CS, _LW)
            lim = pl.ds(c * _CS + _LW, _LW)
            tl = pl.ds(c * _LW, _LW)
            x_ref[:, c * _CS:(c + 1) * _CS] = jnp.dot(u16[:, c * _UC:(c + 1) * _UC], bb_ref[c],
                                                      preferred_element_type=F32)

            def blk(j, car):
                cr, ci = car
                rows = pl.ds(pl.multiple_of(j * 8, 8), 8)
                r = x_ref[rows, lre]
                im = x_ref[rows, lim]
                for lvl, s in enumerate((1, 2, 4)):
                    mr = t_ref[16 * lvl:16 * lvl + 8, tl]
                    mi = t_ref[16 * lvl + 8:16 * lvl + 16, tl]
                    sr = pltpu.roll(r, s, 0)
                    si = pltpu.roll(im, s, 0)
                    r, im = r + (mr * sr - mi * si), im + (mr * si + mi * sr)
                pr = t_ref[48:56, tl]
                pi_ = t_ref[56:64, tl]
                r, im = r + (pr * cr - pi_ * ci), im + (pr * ci + pi_ * cr)
                x_ref[rows, lre] = r
                x_ref[rows, lim] = im
                return (jnp.broadcast_to(r[7:8, :], (8, _LW)), jnp.broadcast_to(im[7:8, :], (8, _LW)))

            cr, ci = lax.fori_loop(0, tm // 8, blk, (car_ref[:, lre], car_ref[:, lim]))
            car_ref[:, lre] = cr
            car_ref[:, lim] = ci
            cols = slice(c * _UC, (c + 1) * _UC)
            ys = jnp.dot(x_ref[:, c * _CS:(c + 1) * _CS].astype(BF16), cm_ref[c], preferred_element_type=F32)
            ys = ys + d_ref[:, cols] * u[:, cols]
            ys_ref[:, cols] = ys
            yg_ref[:, cols] = _gelu(ys).astype(BF16)

    return pl.pallas_call(
        body, name="ssm_fwd",
        out_shape=(jax.ShapeDtypeStruct((S, 2 * NST), F32), jax.ShapeDtypeStruct((S, CW), F32),
                   jax.ShapeDtypeStruct((S, CW), BF16)),
        grid=(S // tm,),
        in_specs=[pl.BlockSpec((tm, CW), lambda i: (i, 2)), _full((_NCL, _UC, _CS)), _full((_NCL, _CS, _UC)),
                  _full((1, CW)), _full((64, NST))],
        out_specs=(pl.BlockSpec((tm, 2 * NST), lambda i: (i, 0)), pl.BlockSpec((tm, CW), lambda i: (i, 0)),
                   pl.BlockSpec((tm, CW), lambda i: (i, 0))),
        scratch_shapes=[pltpu.VMEM((8, 2 * NST), F32)],
        compiler_params=_cp(("arbitrary",)))(z, bb, cm, d, tab)


def _ssm_bwd(dzz, w_glu, ys, z, xs, cmt, bbt, d, tab, after):
    S = z.shape[0]
    tm = min(256, S)
    nt = S // tm
    tdims = (((0,), (0,)), ((), ()))

    def body(dzz_ref, wglu_ref, ys_ref, u_ref, x_ref, cmt_ref, bbt_ref, d_ref, t_ref, after_ref,
             du_ref, de_ref, dd_ref, dc_hbm, dbb_hbm, car_ref, lam_ref, dc_ref, dbb_ref):
        i = pl.program_id(0)

        @pl.when(i == 0)
        def _():
            car_ref[...] = jnp.zeros_like(car_ref)
            de_ref[...] = jnp.zeros_like(de_ref)
            dd_ref[...] = jnp.zeros_like(dd_ref)
            dc_ref[...] = jnp.zeros_like(dc_ref)
            dbb_ref[...] = jnp.zeros_like(dbb_ref)

        u16 = u_ref[...]
        u = u16.astype(F32)
        dyg = lax.dot_general(dzz_ref[...], wglu_ref[...], (((1,), (1,)), ((), ())), preferred_element_type=F32)
        dys = dyg * _gelu_grad(ys_ref[...])
        dys16 = dys.astype(BF16)
        dd_ref[...] += _colsum8(dys * u)
        row = lax.broadcasted_iota(jnp.int32, (8, _LW), 0)
        for c in range(_NCL):
            lre = pl.ds(c * _CS, _LW)
            lim = pl.ds(c * _CS + _LW, _LW)
            tl = pl.ds(c * _LW, _LW)
            cols = slice(c * _UC, (c + 1) * _UC)
            span = slice(c * _CS, (c + 1) * _CS)
            dc_ref[cols, :] += lax.dot_general(dys16[:, cols], x_ref[:, span].astype(BF16), tdims,
                                               preferred_element_type=F32)
            lam_ref[...] = jnp.dot(dys16[:, cols], cmt_ref[c], preferred_element_type=F32)

            def blk(jj, car):
                cr, ci, ar, ai = car
                j = tm // 8 - 1 - jj
                rows = pl.ds(pl.multiple_of(j * 8, 8), 8)
                r = lam_ref[rows, 0:_LW]
                im = lam_ref[rows, _LW:_CS]
                for lvl, s in enumerate((1, 2, 4)):
                    mr = t_ref[16 * lvl:16 * lvl + 8, tl]
                    mi = t_ref[16 * lvl + 8:16 * lvl + 16, tl]
                    sr = pltpu.roll(r, 8 - s, 0)
                    si = pltpu.roll(im, 8 - s, 0)
                    r, im = r + (mr * sr - mi * si), im + (mr * si + mi * sr)
                pr = t_ref[48:56, tl]
                pi_ = t_ref[56:64, tl]
                r, im = r + (pr * cr - pi_ * ci), im + (pr * ci + pi_ * cr)
                lam_ref[rows, 0:_LW] = r
                lam_ref[rows, _LW:_CS] = im
                nr = jnp.where(row == 7, cr, pltpu.roll(r, 7, 0))
                ni = jnp.where(row == 7, ci, pltpu.roll(im, 7, 0))
                xr = x_ref[rows, lre]
                xi = x_ref[rows, lim]
                ar = ar + (nr * xr + ni * xi)
                ai = ai + (ni * xr - nr * xi)
                return (jnp.broadcast_to(r[0:1, :], (8, _LW)), jnp.broadcast_to(im[0:1, :], (8, _LW)), ar, ai)

            zero = jnp.zeros((8, _LW), F32)
            cr, ci, ar, ai = lax.fori_loop(0, tm // 8, blk, (car_ref[:, lre], car_ref[:, lim], zero, zero))
            car_ref[:, lre] = cr
            car_ref[:, lim] = ci
            de_ref[0:8, tl] += ar
            de_ref[8:16, tl] += ai
            lam16 = lam_ref[...].astype(BF16)
            dbb_ref[cols, :] += lax.dot_general(u16[:, cols], lam16, tdims, preferred_element_type=F32)
            du = jnp.dot(lam16, bbt_ref[c], preferred_element_type=F32) + dys[:, cols] * d_ref[:, cols]
            du_ref[:, cols] = du.astype(BF16)

        @pl.when(i == nt - 1)
        def _():
            pltpu.sync_copy(dc_ref, dc_hbm)
            pltpu.sync_copy(dbb_ref, dbb_hbm)

    rev = lambda i: (nt - 1 - i, 0)
    once = lambda shape: pl.BlockSpec(shape, lambda *_: (0,) * len(shape), pipeline_mode=pl.Buffered(1))
    cross = jax.ShapeDtypeStruct((CW, _CS), F32)
    return pl.pallas_call(
        body, name="ssm_bwd",
        out_shape=(jax.ShapeDtypeStruct((S, CW), BF16), jax.ShapeDtypeStruct((16, NST), F32),
                   jax.ShapeDtypeStruct((8, CW), F32), cross, cross),
        grid=(nt,),
        in_specs=[pl.BlockSpec((tm, 2 * D_MODEL), rev), once((CW, 2 * D_MODEL)), pl.BlockSpec((tm, CW), rev),
                  pl.BlockSpec((tm, CW), lambda i: (nt - 1 - i, 2)), pl.BlockSpec((tm, 2 * NST), rev),
                  once((_NCL, _UC, _CS)), once((_NCL, _CS, _UC)), _full((1, CW)), once((64, NST)),
                  pl.BlockSpec(memory_space=pl.ANY)],
        out_specs=(pl.BlockSpec((tm, CW), rev), _full((16, NST)), _full((8, CW)),
                   pl.BlockSpec(memory_space=pl.ANY), pl.BlockSpec(memory_space=pl.ANY)),
        scratch_shapes=[pltpu.VMEM((8, 2 * NST), F32), pltpu.VMEM((tm, _CS), F32),
                        pltpu.VMEM((CW, _CS), F32), pltpu.VMEM((CW, _CS), F32)],
        compiler_params=_cp(("arbitrary",)))(dzz, w_glu, ys, z, xs, cmt, bbt, d, tab, after)


def _ssm_prep(a_re, a_im, b_re, b_im, log_dt):
    dt = jnp.exp(log_dt.reshape(G))[:, None]
    mag = jnp.exp(dt * a_re)
    e_re, e_im = mag * jnp.cos(dt * a_im), mag * jnp.sin(dt * a_im)
    n_re, n_im = e_re - 1.0, e_im
    den = a_re * a_re + a_im * a_im
    q_re = (n_re * a_re + n_im * a_im) / den
    q_im = (n_im * a_re - n_re * a_im) / den
    bb_re = q_re[..., None] * b_re - q_im[..., None] * b_im
    bb_im = q_re[..., None] * b_im + q_im[..., None] * b_re
    return e_re, e_im, bb_re, bb_im


def _scan_tables(e_re, e_im, reverse):
    er = e_re.reshape(1, NST)
    ei = e_im.reshape(1, NST)
    if reverse:
        ei = -ei
    pows = [(er, ei)]
    for _ in range(7):
        pr, pi_ = pows[-1]
        pows.append((pr * er - pi_ * ei, pr * ei + pi_ * er))
    row = jnp.arange(8)[:, None]
    out = []
    for s in (1, 2, 4):
        pr, pi_ = pows[s - 1]
        keep = (row + s <= 7) if reverse else (row >= s)
        out += [jnp.where(keep, pr, 0.0), jnp.where(keep, pi_, 0.0)]
    allr = jnp.concatenate([p[0] for p in pows], 0)
    alli = jnp.concatenate([p[1] for p in pows], 0)
    if reverse:
        allr, alli = allr[::-1], alli[::-1]
    out += [allr, alli]
    return jnp.concatenate(out, 0).astype(F32)


def _block_diag_mats(bb_re, bb_im, c_re, c_im):
    gc = G // _NCL
    eye = jnp.eye(gc, dtype=F32)
    bre = jnp.einsum("cjph,jk->cjhkp", bb_re.reshape(_NCL, gc, P, H), eye).reshape(_NCL, _UC, _LW)
    bim = jnp.einsum("cjph,jk->cjhkp", bb_im.reshape(_NCL, gc, P, H), eye).reshape(_NCL, _UC, _LW)
    bb = jnp.concatenate([bre, bim], 2)
    cre = jnp.einsum("cjhp,jk->cjpkh", c_re.reshape(_NCL, gc, H, P), eye).reshape(_NCL, _LW, _UC)
    cim = jnp.einsum("cjhp,jk->cjpkh", c_im.reshape(_NCL, gc, H, P), eye).reshape(_NCL, _LW, _UC)
    cm = jnp.concatenate([cre, -cim], 1)
    return bb, cm


def _diag_blocks(cross, imag):
    gc = G // _NCL
    off = _LW if imag else 0
    return jnp.stack([cross[H * g:H * (g + 1), off + P * (g % gc):off + P * (g % gc + 1)] for g in range(G)])


def _mix_fwd(scv, yg, z, x, w_cp, w_glu, w_out, g1, n2g, sc2, sh2):
    S = z.shape[0]
    tm = min(256, S)
    D = D_MODEL

    def body(s_ref, yg_ref, glc0_ref, glc1_ref, gls0_ref, gls1_ref, x_ref, wcp_ref, wglu_ref, wout_ref,
             g1_ref, n2_ref, sc_ref, sh_ref, yc_ref, zz_ref, m_ref, o_ref, x2_ref, h2_ref):
        y_conv = jnp.dot(s_ref[...], wcp_ref[...], preferred_element_type=F32)
        zz = jnp.dot(yg_ref[...], wglu_ref[...], preferred_element_type=F32)
        yc_ref[...] = y_conv.astype(BF16)
        zz_ref[...] = zz.astype(BF16)
        for half, (glc_ref, gls_ref) in enumerate(((glc0_ref, gls0_ref), (glc1_ref, gls1_ref))):
            lo, hi = half * CW, (half + 1) * CW
            y_ssm = zz[:, lo:hi] * _sig(zz[:, D + lo:D + hi])
            m_ref[:, lo:hi] = (_sig(glc_ref[...].astype(F32)) * y_conv[:, lo:hi]
                               + _sig(gls_ref[...].astype(F32)) * y_ssm).astype(BF16)
        o = jnp.dot(m_ref[...], wout_ref[...], preferred_element_type=F32)
        o_ref[...] = o.astype(BF16)
        xv = x_ref[...] + g1_ref[...] * o
        x2_ref[...] = xv
        r = lax.rsqrt(jnp.mean(xv * xv, axis=-1, keepdims=True) + EPS)
        h2_ref[...] = (xv * r * (n2_ref[...] * (1.0 + sc_ref[...])) + sh_ref[...]).astype(BF16)

    zb_ = lambda j: pl.BlockSpec((tm, CW), lambda i: (i, j))
    row = lambda w: pl.BlockSpec((tm, w), lambda i: (i, 0))
    par = _full((1, D))
    bf = lambda w: jax.ShapeDtypeStruct((S, w), BF16)
    return pl.pallas_call(
        body, name="mix_fwd",
        out_shape=(bf(D), bf(2 * D), bf(D), bf(D), jax.ShapeDtypeStruct((S, D), F32), bf(D)),
        grid=(S // tm,),
        in_specs=[row(CW), row(CW), zb_(3), zb_(4), zb_(5), zb_(6), row(D), _full((CW, D)), _full((CW, 2 * D)),
                  _full((D, D)), par, par, par, par],
        out_specs=(row(D), row(2 * D), row(D), row(D), row(D), row(D)),
        compiler_params=_cp(("parallel",)))(scv, yg, z, z, z, z, x, w_cp, w_glu, w_out, g1, n2g, sc2, sh2)


def _mix_bwd(do, w_out, z, zz, y_conv, after):
    S = z.shape[0]
    tm = min(256, S)
    D = D_MODEL

    def body(do_ref, w_ref, glc0_ref, glc1_ref, gls0_ref, gls1_ref, za_ref, zb_ref, yc_ref, after_ref,
             dyc_ref, dgl_ref, dzz_ref):
        dm = lax.dot_general(do_ref[...], w_ref[...], (((1,), (1,)), ((), ())), preferred_element_type=F32)
        for half, (glc_ref, gls_ref) in enumerate(((glc0_ref, gls0_ref), (glc1_ref, gls1_ref))):
            lo, hi = half * CW, (half + 1) * CW
            dm_v = dm[:, lo:hi]
            sgc = _sig(glc_ref[...].astype(F32))
            sgs = _sig(gls_ref[...].astype(F32))
            szb = _sig(zb_ref[:, lo:hi].astype(F32))
            za = za_ref[:, lo:hi].astype(F32)
            dyc_ref[:, lo:hi] = (dm_v * sgc).astype(BF16)
            dgl_ref[:, lo:hi] = (dm_v * yc_ref[:, lo:hi].astype(F32) * sgc * (1.0 - sgc)).astype(BF16)
            dys = dm_v * sgs
            dgl_ref[:, D + lo:D + hi] = (dys * (za * szb) * (1.0 - sgs)).astype(BF16)
            dzz_ref[:, lo:hi] = (dys * szb).astype(BF16)
            dzz_ref[:, D + lo:D + hi] = (dys * za * szb * (1.0 - szb)).astype(BF16)

    zb_ = lambda j: pl.BlockSpec((tm, CW), lambda i: (i, j))
    wide = lambda j: pl.BlockSpec((tm, D), lambda i: (i, j))
    return pl.pallas_call(
        body, name="mix_bwd",
        out_shape=(jax.ShapeDtypeStruct((S, D), BF16), jax.ShapeDtypeStruct((S, 2 * D), BF16),
                   jax.ShapeDtypeStruct((S, 2 * D), BF16)),
        grid=(S // tm,),
        in_specs=[wide(0), _full((D, D)), zb_(3), zb_(4), zb_(5), zb_(6), wide(0), wide(1), wide(0),
                  pl.BlockSpec(memory_space=pl.ANY)],
        out_specs=(wide(0), pl.BlockSpec((tm, 2 * D), lambda i: (i, 0)), pl.BlockSpec((tm, 2 * D), lambda i: (i, 0))),
        compiler_params=_cp(("parallel",)))(do, w_out, z, z, z, z, zz, zz, y_conv, after)


_FC = 1408


def _ffn_in_act(h2, w_fi):
    S, D = h2.shape
    tm = min(256, S)

    def body(h_ref, w_ref, f_ref, a_ref):
        hv = h_ref[...]
        for c in range(FH // _FC):
            lo, hi = c * _FC, (c + 1) * _FC
            g = jnp.dot(hv, w_ref[:, lo:hi], preferred_element_type=F32)
            u = jnp.dot(hv, w_ref[:, FH + lo:FH + hi], preferred_element_type=F32)
            f_ref[:, lo:hi] = g.astype(BF16)
            f_ref[:, FH + lo:FH + hi] = u.astype(BF16)
            a_ref[:, lo:hi] = (g * _sig(g) * u).astype(BF16)

    return pl.pallas_call(
        body, name="ffn_in_act",
        out_shape=(jax.ShapeDtypeStruct((S, 2 * FH), BF16), jax.ShapeDtypeStruct((S, FH), BF16)),
        grid=(S // tm,),
        in_specs=[pl.BlockSpec((tm, D), lambda i: (i, 0)), _full((D, 2 * FH))],
        out_specs=(pl.BlockSpec((tm, 2 * FH), lambda i: (i, 0)), pl.BlockSpec((tm, FH), lambda i: (i, 0))),
        compiler_params=_cp(("parallel",)))(h2, w_fi)


def _ffn_bwd(do2, w_fo, f, after):
    S, D = do2.shape
    tm = min(256, S)

    def body(d_ref, w_ref, f_ref, after_ref, df_ref):
        dv = d_ref[...]
        for c in range(FH // _FC):
            lo, hi = c * _FC, (c + 1) * _FC
            dact = lax.dot_general(dv, w_ref[lo:hi, :], (((1,), (1,)), ((), ())), preferred_element_type=F32)
            g = f_ref[:, lo:hi].astype(F32)
            u = f_ref[:, FH + lo:FH + hi].astype(F32)
            sg = _sig(g)
            df_ref[:, lo:hi] = (dact * u * (sg * (1.0 + g * (1.0 - sg)))).astype(BF16)
            df_ref[:, FH + lo:FH + hi] = (dact * g * sg).astype(BF16)

    return pl.pallas_call(
        body, name="ffn_bwd", out_shape=jax.ShapeDtypeStruct((S, 2 * FH), BF16), grid=(S // tm,),
        in_specs=[pl.BlockSpec((tm, D), lambda i: (i, 0)), _full((FH, D)),
                  pl.BlockSpec((tm, 2 * FH), lambda i: (i, 0)), pl.BlockSpec(memory_space=pl.ANY)],
        out_specs=pl.BlockSpec((tm, 2 * FH), lambda i: (i, 0)),
        compiler_params=_cp(("parallel",)))(do2, w_fo, f, after)


def _ffn_out_final(x2, act, w_fo, g2, fg, tgt):
    S, D = x2.shape
    tm = min(256, S)

    def body(x2_ref, a_ref, w_ref, g2_ref, fg_ref, t_ref, dx3_ref, do2_ref, ls_ref, dfg_ref, dg2_ref):
        i = pl.program_id(0)
        o2 = jnp.dot(a_ref[...], w_ref[...], preferred_element_type=F32)
        x3 = x2_ref[...] + g2_ref[...] * o2
        r = lax.rsqrt(jnp.mean(x3 * x3, axis=-1, keepdims=True) + EPS)
        xn = x3 * r
        err = xn * fg_ref[...] - t_ref[...]
        dy = err * (1.0 / D)
        dxn = dy * fg_ref[...]
        dx3 = r * (dxn - xn * jnp.mean(dxn * xn, axis=-1, keepdims=True))
        dx3_ref[...] = dx3
        do2_ref[...] = (dx3 * g2_ref[...]).astype(BF16)

        @pl.when(i == 0)
        def _():
            ls_ref[...] = jnp.zeros_like(ls_ref)
            dfg_ref[...] = jnp.zeros_like(dfg_ref)
            dg2_ref[...] = jnp.zeros_like(dg2_ref)

        e2 = _colsum8(err * err)
        lanes = e2[:, 0:128]
        for q in range(1, D // 128):
            lanes = lanes + e2[:, q * 128:(q + 1) * 128]
        ls_ref[...] += lanes * (0.5 / D)
        dfg_ref[...] += _colsum8(dy * xn)
        dg2_ref[...] += _colsum8(dx3 * o2)

    row = pl.BlockSpec((tm, D), lambda i: (i, 0))
    par = _full((1, D))
    return pl.pallas_call(
        body, name="final_loss",
        out_shape=(jax.ShapeDtypeStruct((S, D), F32), jax.ShapeDtypeStruct((S, D), BF16),
                   jax.ShapeDtypeStruct((8, 128), F32), jax.ShapeDtypeStruct((8, D), F32),
                   jax.ShapeDtypeStruct((8, D), F32)),
        grid=(S // tm,), in_specs=[row, pl.BlockSpec((tm, FH), lambda i: (i, 0)), _full((FH, D)), par, par, row],
        out_specs=(row, row, _full((8, 128)), _full((8, D)), _full((8, D))),
        compiler_params=_cp(("arbitrary",)))(x2, act, w_fo, g2, fg, tgt)


def _normmod_bwd(dsrc, w, xin, dres, g, sc, gate, o, after, name):
    S, D = xin.shape
    parts = list(dsrc) if isinstance(dsrc, (list, tuple)) else [dsrc]
    widths = [p.shape[1] for p in parts]
    K = sum(widths)
    tm = min(256, S)
    npart = len(parts)

    def body(*refs):
        ds_refs = refs[:npart]
        w_ref, x_ref, dr_ref, g_ref, sc_ref, gate_ref, o_ref, after_ref = refs[npart:npart + 8]
        dx_ref, do_ref, dsh_ref, dsc_ref, dg_ref, dgate_ref = refs[npart + 8:]
        i = pl.program_id(0)
        xv = x_ref[...]
        r = lax.rsqrt(jnp.mean(xv * xv, axis=-1, keepdims=True) + EPS)
        xn = xv * r
        dh_v, col = None, 0
        for ds_ref, wd in zip(ds_refs, widths):
            t = lax.dot_general(ds_ref[...], w_ref[:, col:col + wd], (((1,), (1,)), ((), ())),
                                preferred_element_type=F32)
            dh_v = t if dh_v is None else dh_v + t
            col += wd
        gv = g_ref[...]
        scale = 1.0 + sc_ref[...]
        dxn = dh_v * (gv * scale)
        dx = dr_ref[...] + r * (dxn - xn * jnp.mean(dxn * xn, axis=-1, keepdims=True))
        dx_ref[...] = dx
        do_ref[...] = (dx * gate_ref[...]).astype(BF16)

        @pl.when(i == 0)
        def _():
            dsh_ref[...] = jnp.zeros_like(dsh_ref)
            dsc_ref[...] = jnp.zeros_like(dsc_ref)
            dg_ref[...] = jnp.zeros_like(dg_ref)
            dgate_ref[...] = jnp.zeros_like(dgate_ref)

        hx = dh_v * xn
        dsh_ref[...] += _colsum8(dh_v)
        dsc_ref[...] += _colsum8(hx) * gv
        dg_ref[...] += _colsum8(hx) * scale
        dgate_ref[...] += _colsum8(dx * o_ref[...])

    row = pl.BlockSpec((tm, D), lambda i: (i, 0))
    par = _full((1, D))
    acc = jax.ShapeDtypeStruct((8, D), F32)
    return pl.pallas_call(
        body, name=name,
        out_shape=(jax.ShapeDtypeStruct((S, D), F32), jax.ShapeDtypeStruct((S, D), BF16), acc, acc, acc, acc),
        grid=(S // tm,),
        in_specs=[pl.BlockSpec((tm, wd), lambda i: (i, 0)) for wd in widths]
        + [_full((D, K)), row, row, par, par, par, row, pl.BlockSpec(memory_space=pl.ANY)],
        out_specs=(row, row, _full((8, D)), _full((8, D)), _full((8, D)), _full((8, D))),
        compiler_params=_cp(("arbitrary",)))(*parts, w, xin, dres, g, sc, gate, o, after)


def _me():
    return lax.axis_index("x"), lax.axis_index("y"), lax.axis_index("c")


def _allgather8(v, name, after=()):
    R, C = v.shape
    after = list(after)

    def body(v_ref, *rest):
        out_ref, send_sems, recv_sems, local_sem = rest[len(after):]
        x, y, c = _me()
        mine = pltpu.make_async_copy(v_ref, out_ref.at[4 * x + 2 * y + c], local_sem)
        mine.start()
        copies = []
        for k in range(1, N_DEV):
            fx, fy, fc = (k >> 2) & 1, (k >> 1) & 1, k & 1
            peer = (x ^ fx, y ^ fy, c ^ fc)
            copies.append(pltpu.make_async_remote_copy(
                src_ref=v_ref, dst_ref=out_ref.at[4 * x + 2 * y + c],
                send_sem=send_sems.at[k - 1], recv_sem=recv_sems.at[k - 1],
                device_id=peer, device_id_type=MESH))
        for cp in copies:
            cp.start()
        for k in range(1, N_DEV):
            fx, fy, fc = (k >> 2) & 1, (k >> 1) & 1, k & 1
            src_slot = 4 * (x ^ fx) + 2 * (y ^ fy) + (c ^ fc)
            pltpu.make_async_remote_copy(
                src_ref=v_ref, dst_ref=out_ref.at[src_slot],
                send_sem=send_sems.at[k - 1], recv_sem=recv_sems.at[k - 1],
                device_id=(x ^ fx, y ^ fy, c ^ fc), device_id_type=MESH).wait_recv()
        for cp in copies:
            cp.wait_send()
        mine.wait()

    return pl.pallas_call(
        body, name=name, out_shape=jax.ShapeDtypeStruct((N_DEV, R, C), v.dtype),
        in_specs=[pl.BlockSpec(memory_space=pltpu.VMEM)] + [pl.BlockSpec(memory_space=pl.ANY)] * len(after),
        out_specs=pl.BlockSpec(memory_space=pltpu.VMEM),
        scratch_shapes=[pltpu.SemaphoreType.DMA((N_DEV - 1,)), pltpu.SemaphoreType.DMA((N_DEV - 1,)),
                        pltpu.SemaphoreType.DMA],
        compiler_params=pltpu.CompilerParams(vmem_limit_bytes=VMEM_LIMIT))(v, *after)


def _swap_sibling(arrs):
    nw = len(arrs)

    def body(*refs):
        ins, outs = refs[:nw], refs[nw:2 * nw]
        send_sems, recv_sems = refs[2 * nw:]
        x, y, c = _me()
        copies = [pltpu.make_async_remote_copy(
            src_ref=ins[w], dst_ref=outs[w], send_sem=send_sems.at[w], recv_sem=recv_sems.at[w],
            device_id=(x, y, 1 - c), device_id_type=MESH) for w in range(nw)]
        for cp in copies:
            cp.start()
        for cp in copies:
            cp.wait_recv()
        for cp in copies:
            cp.wait_send()

    hbm = pl.BlockSpec(memory_space=pltpu.HBM)
    return pl.pallas_call(
        body, name="swap_sibling", out_shape=tuple(jax.ShapeDtypeStruct(a.shape, a.dtype) for a in arrs),
        in_specs=[hbm] * nw, out_specs=tuple([hbm] * nw),
        scratch_shapes=[pltpu.SemaphoreType.DMA((nw,)), pltpu.SemaphoreType.DMA((nw,))],
        compiler_params=pltpu.CompilerParams(vmem_limit_bytes=VMEM_LIMIT))(*arrs)


_HBM = pl.BlockSpec(memory_space=pltpu.HBM)
_SEM = pl.BlockSpec(memory_space=pltpu.SEMAPHORE)
_EFFECT = pltpu.SideEffectType.DATAFLOW_SIDE_EFFECTING
_N_PEER = N_CHIP - 1


def _chip_part(ref, axis, n, chip):
    start = pl.multiple_of(chip * n, 8)
    return ref.at[pl.ds(start, n), :] if axis == 0 else ref.at[:, pl.ds(start, n)]


def _gather_copy(k, src_ref, land_ref, send_sems, recv_sems, axis, arriving):
    x, y, c = _me()
    px, py = x ^ ((k >> 1) & 1), y ^ (k & 1)
    chip = 2 * px + py if arriving else 2 * x + y
    return pltpu.make_async_remote_copy(
        src_ref=src_ref, dst_ref=_chip_part(land_ref, axis, src_ref.shape[axis], chip),
        send_sem=send_sems.at[k - 1], recv_sem=recv_sems.at[k - 1], device_id=(px, py, c), device_id_type=MESH)


def _scatter_copy(k, grad_ref, land_ref, send_sems, recv_sems, axis):
    x, y, c = _me()
    px, py = x ^ ((k >> 1) & 1), y ^ (k & 1)
    return pltpu.make_async_remote_copy(
        src_ref=_chip_part(grad_ref, axis, grad_ref.shape[axis] // N_CHIP, 2 * px + py), dst_ref=land_ref.at[k - 1],
        send_sem=send_sems.at[k - 1], recv_sem=recv_sems.at[k - 1], device_id=(px, py, c), device_id_type=MESH)


def _own_copy(src_ref, land_ref, sends, axis):
    x, y, _ = _me()
    return pltpu.make_async_copy(src_ref, _chip_part(land_ref, axis, src_ref.shape[axis], 2 * x + y),
                                 sends.at[_N_PEER])


def _gather_start(shards, axes, after, name):
    nw = len(shards)
    lands = []
    for s, ax in zip(shards, axes):
        shp = list(s.shape)
        shp[ax] *= N_CHIP
        lands.append(lax.empty(tuple(shp), s.dtype))

    def body(*refs):
        srcs, zones = refs[:nw], refs[nw:2 * nw]
        sends, recvs = refs[2 * nw + 1:3 * nw + 1], refs[3 * nw + 1:4 * nw + 1]
        token = refs[-1]
        for w in range(nw):
            for k in range(1, N_CHIP):
                _gather_copy(k, srcs[w], zones[w], sends[w], recvs[w], axes[w], False).start()
        for w in range(nw):
            _own_copy(srcs[w], zones[w], sends[w], axes[w]).start()
        token[...] = jnp.zeros_like(token)

    outs = pl.pallas_call(
        body, name=name,
        out_shape=tuple([pltpu.SemaphoreType.DMA((_N_PEER + 1,))] * nw + [pltpu.SemaphoreType.DMA((_N_PEER,))] * nw
                        + [pltpu.HBM(a.shape, a.dtype) for a in list(shards) + list(lands)]
                        + [jax.ShapeDtypeStruct((8, 128), F32)]),
        in_specs=[_HBM] * (2 * nw) + [pl.BlockSpec(memory_space=pl.ANY)],
        out_specs=tuple([_SEM] * (2 * nw) + [_HBM] * (2 * nw) + [pl.BlockSpec(memory_space=pltpu.VMEM)]),
        input_output_aliases={i: 2 * nw + i for i in range(2 * nw)},
        compiler_params=pltpu.CompilerParams(has_side_effects=_EFFECT),
    )(*([pltpu.with_memory_space_constraint(a, pltpu.HBM) for a in list(shards) + list(lands)] + [after]))
    per_weight = [(outs[w], outs[nw + w], outs[2 * nw + w], outs[3 * nw + w]) for w in range(nw)]
    return per_weight, outs[-1]


def _gather_wait(state, axis, after, name):
    send_sems, recv_sems, shard, land = state

    after = list(after) if isinstance(after, (list, tuple)) else [after]

    def body(src_ref, land_ref, sends, recvs, *rest):
        for k in range(1, N_CHIP):
            _gather_copy(k, src_ref, land_ref, sends, recvs, axis, False).wait_send()
            _gather_copy(k, src_ref, land_ref, sends, recvs, axis, True).wait_recv()
        _own_copy(src_ref, land_ref, sends, axis).wait()

    return pl.pallas_call(
        body, name=name, out_shape=(pltpu.HBM(shard.shape, shard.dtype), pltpu.HBM(land.shape, land.dtype)),
        in_specs=[_HBM, _HBM, _SEM, _SEM] + [pl.BlockSpec(memory_space=pl.ANY)] * len(after), out_specs=(_HBM, _HBM),
        input_output_aliases={0: 0, 1: 1},
        compiler_params=pltpu.CompilerParams(has_side_effects=_EFFECT),
    )(shard, land, send_sems, recv_sems, *after)[1]


def _all8_copy(k, v_ref, land_ref, send_sems, recv_sems, arriving):
    x, y, c = _me()
    px, py, pc = x ^ ((k >> 2) & 1), y ^ ((k >> 1) & 1), c ^ (k & 1)
    slot = 4 * px + 2 * py + pc if arriving else 4 * x + 2 * y + c
    return pltpu.make_async_remote_copy(
        src_ref=v_ref, dst_ref=land_ref.at[slot], send_sem=send_sems.at[k - 1], recv_sem=recv_sems.at[k - 1],
        device_id=(px, py, pc), device_id_type=MESH)


def _all8_own(v_ref, land_ref, send_sems):
    x, y, c = _me()
    return pltpu.make_async_copy(v_ref, land_ref.at[4 * x + 2 * y + c], send_sems.at[N_DEV - 1])


def _all8_start(v, name):
    land = lax.empty((N_DEV,) + v.shape, v.dtype)

    def body(v_ref, land_ref, sends, recvs, v_thru, land_thru, token):
        for k in range(1, N_DEV):
            _all8_copy(k, v_ref, land_ref, sends, recvs, False).start()
        _all8_own(v_ref, land_ref, sends).start()
        token[...] = jnp.zeros_like(token)

    outs = pl.pallas_call(
        body, name=name,
        out_shape=(pltpu.SemaphoreType.DMA((N_DEV,)), pltpu.SemaphoreType.DMA((N_DEV - 1,)),
                   pltpu.HBM(v.shape, v.dtype), pltpu.HBM(land.shape, land.dtype),
                   jax.ShapeDtypeStruct((8, 128), F32)),
        in_specs=[_HBM, _HBM], out_specs=(_SEM, _SEM, _HBM, _HBM, pl.BlockSpec(memory_space=pltpu.VMEM)),
        input_output_aliases={0: 2, 1: 3},
        compiler_params=pltpu.CompilerParams(has_side_effects=_EFFECT),
    )(pltpu.with_memory_space_constraint(v, pltpu.HBM), pltpu.with_memory_space_constraint(land, pltpu.HBM))
    return outs[:4], outs[4]


def _all8_wait(state, after, name):
    send_sems, recv_sems, v, land = state

    def body(v_ref, land_ref, sends, recvs, after_ref, v_dead, got_ref):
        for k in range(1, N_DEV):
            _all8_copy(k, v_ref, land_ref, sends, recvs, False).wait_send()
            _all8_copy(k, v_ref, land_ref, sends, recvs, True).wait_recv()
        _all8_own(v_ref, land_ref, sends).wait()

    return pl.pallas_call(
        body, name=name, out_shape=(pltpu.HBM(v.shape, v.dtype), pltpu.HBM(land.shape, land.dtype)),
        in_specs=[_HBM, _HBM, _SEM, _SEM, pl.BlockSpec(memory_space=pl.ANY)], out_specs=(_HBM, _HBM),
        input_output_aliases={0: 0, 1: 1},
        compiler_params=pltpu.CompilerParams(has_side_effects=_EFFECT),
    )(v, land, send_sems, recv_sems, after)[1]


def _swap_copy(w, src_ref, land_ref, send_sems, recv_sems):
    x, y, c = _me()
    return pltpu.make_async_remote_copy(src_ref=src_ref, dst_ref=land_ref, send_sem=send_sems.at[w],
                                        recv_sem=recv_sems.at[w], device_id=(x, y, 1 - c), device_id_type=MESH)


def _swap_start(arrs, after, name):
    nw = len(arrs)
    lands = [lax.empty(a.shape, a.dtype) for a in arrs]

    def body(*refs):
        srcs, zones = refs[:nw], refs[nw:2 * nw]
        sends, recvs = refs[2 * nw + 1], refs[2 * nw + 2]
        for w in range(nw):
            _swap_copy(w, srcs[w], zones[w], sends, recvs).start()
        refs[-1][...] = jnp.zeros_like(refs[-1])

    sem = pltpu.SemaphoreType.DMA((nw,))
    outs = pl.pallas_call(
        body, name=name,
        out_shape=tuple([sem, sem] + [pltpu.HBM(a.shape, a.dtype) for a in list(arrs) + lands]
                        + [jax.ShapeDtypeStruct((8, 128), F32)]),
        in_specs=[_HBM] * (2 * nw) + [pl.BlockSpec(memory_space=pl.ANY)],
        out_specs=tuple([_SEM, _SEM] + [_HBM] * (2 * nw) + [pl.BlockSpec(memory_space=pltpu.VMEM)]),
        input_output_aliases={i: 2 + i for i in range(2 * nw)},
        compiler_params=pltpu.CompilerParams(has_side_effects=_EFFECT),
    )(*([pltpu.with_memory_space_constraint(a, pltpu.HBM) for a in list(arrs) + lands] + [after]))
    return (outs[0], outs[1], outs[2:2 + nw], outs[2 + nw:2 + 2 * nw]), outs[-1]


def _swap_wait(state, after, name):
    send_sems, recv_sems, arrs, lands = state
    nw = len(arrs)

    def body(*refs):
        srcs, zones = refs[:nw], refs[nw:2 * nw]
        sends, recvs = refs[2 * nw], refs[2 * nw + 1]
        for w in range(nw):
            cp = _swap_copy(w, srcs[w], zones[w], sends, recvs)
            cp.wait_send()
            cp.wait_recv()

    outs = pl.pallas_call(
        body, name=name, out_shape=tuple(pltpu.HBM(a.shape, a.dtype) for a in list(arrs) + list(lands)),
        in_specs=[_HBM] * (2 * nw) + [_SEM, _SEM, pl.BlockSpec(memory_space=pl.ANY)],
        out_specs=tuple([_HBM] * (2 * nw)),
        input_output_aliases={i: i for i in range(2 * nw)},
        compiler_params=pltpu.CompilerParams(has_side_effects=_EFFECT),
    )(*arrs, *lands, send_sems, recv_sems, after)
    return list(outs[:nw]), list(outs[nw:])


def _scatter_start(grad, axis, name):
    shp = list(grad.shape)
    shp[axis] //= N_CHIP
    land = lax.empty((_N_PEER,) + tuple(shp), grad.dtype)

    def body(grad_ref, land_ref, sends, recvs, grad_thru, land_thru, token):
        for k in range(1, N_CHIP):
            _scatter_copy(k, grad_ref, land_ref, sends, recvs, axis).start()
        token[...] = jnp.zeros_like(token)

    sem = pltpu.SemaphoreType.DMA((_N_PEER,))
    outs = pl.pallas_call(
        body, name=name,
        out_shape=(sem, sem, pltpu.HBM(grad.shape, grad.dtype), pltpu.HBM(land.shape, land.dtype),
                   jax.ShapeDtypeStruct((8, 128), F32)),
        in_specs=[_HBM, _HBM], out_specs=(_SEM, _SEM, _HBM, _HBM, pl.BlockSpec(memory_space=pltpu.VMEM)),
        input_output_aliases={0: 2, 1: 3},
        compiler_params=pltpu.CompilerParams(has_side_effects=_EFFECT),
    )(pltpu.with_memory_space_constraint(grad, pltpu.HBM), pltpu.with_memory_space_constraint(land, pltpu.HBM))
    return outs[:4], outs[4]


def _scatter_wait(state, axis, after, name):
    send_sems, recv_sems, grad, land = state

    def body(grad_ref, land_ref, sends, recvs, after_ref, grad_dead, got_ref):
        for k in range(1, N_CHIP):
            cp = _scatter_copy(k, grad_ref, land_ref, sends, recvs, axis)
            cp.wait_send()
            cp.wait_recv()

    return pl.pallas_call(
        body, name=name, out_shape=(pltpu.HBM(grad.shape, grad.dtype), pltpu.HBM(land.shape, land.dtype)),
        in_specs=[_HBM, _HBM, _SEM, _SEM, pl.BlockSpec(memory_space=pl.ANY)], out_specs=(_HBM, _HBM),
        input_output_aliases={0: 0, 1: 1},
        compiler_params=pltpu.CompilerParams(has_side_effects=_EFFECT),
    )(grad, land, send_sems, recv_sems, after)[1]


_C1 = 1.0 - B1 ** STEP
_C2 = 1.0 - B2 ** STEP


def _adam_math(w, g, m, v):
    m = B1 * m + (1.0 - B1) * g
    v = B2 * v + (1.0 - B2) * (g * g)
    delta = -LR * ((m / _C1) / (jnp.sqrt(v / _C2) + AEPS) + WD * w)
    return delta, m, v


def _adamw(w, m, v, groups, name):
    R, C = w.shape
    tr = R if R <= 256 else (128 if R % 128 == 0 else 176)
    assert R % tr == 0, (name, R)
    gparts = [p for grp in groups for p in grp]
    sizes = [len(grp) for grp in groups]
    ng = len(gparts)

    def body(*refs):
        w_ref, m_ref, v_ref = refs[:3]
        g_refs = list(refs[3:3 + ng])
        g_out, d_out, m_out, v_out = refs[3 + ng:]
        g = None
        for size in sizes:
            s = None
            for r in [g_refs.pop(0) for _ in range(size)]:
                terms = [r[q] for q in range(r.shape[0])] if len(r.shape) == 3 else [r[...]]
                for t in terms:
                    s = t.astype(F32) if s is None else s + t.astype(F32)
            g = s if g is None else g + s
        delta, mn, vn = _adam_math(w_ref[...], g, m_ref[...], v_ref[...])
        g_out[...] = g
        d_out[...] = delta
        m_out[...] = mn
        v_out[...] = vn

    blk = pl.BlockSpec((tr, C), lambda i: (i, 0))
    g_specs = [blk if p.ndim == 2 else pl.BlockSpec((p.shape[0], tr, C), lambda i: (0, i, 0)) for p in gparts]
    sds = jax.ShapeDtypeStruct((R, C), F32)
    return pl.pallas_call(
        body, name=name, out_shape=(sds, sds, sds, sds), grid=(R // tr,),
        in_specs=[blk, blk, blk] + g_specs, out_specs=(blk, blk, blk, blk),
        compiler_params=_cp(("parallel",)))(w, m, v, *gparts)


def _mod_shard(c_all, w_ada, b_ada_cols):
    n = w_ada.shape[1]
    tn = 512

    def body(c_ref, w_ref, b_ref, o_ref):
        cv = c_ref[...]
        ca = (cv * _sig(cv)).astype(BF16)
        o_ref[...] = jnp.dot(ca, w_ref[...].astype(BF16), preferred_element_type=F32) + b_ref[...]

    return pl.pallas_call(
        body, name="mod_shard", out_shape=jax.ShapeDtypeStruct((N_DEV, n), F32), grid=(n // tn,),
        in_specs=[_full((N_DEV, D_MODEL)), pl.BlockSpec((D_MODEL, tn), lambda j: (0, j)),
                  pl.BlockSpec((1, tn), lambda j: (0, j))],
        out_specs=pl.BlockSpec((N_DEV, tn), lambda j: (0, j)),
        compiler_params=_cp(("parallel",)))(c_all, w_ada, b_ada_cols)


def _ada_grad(c_all, dmod_cols):
    n = dmod_cols.shape[1]
    tn = 512

    def body(c_ref, d_ref, o_ref):
        cv = c_ref[...]
        ca = cv * _sig(cv)
        o_ref[...] = lax.dot_general(ca, d_ref[...], (((0,), (0,)), ((), ())),
                                     preferred_element_type=F32, precision=lax.Precision.HIGHEST)

    return pl.pallas_call(
        body, name="ada_grad", out_shape=jax.ShapeDtypeStruct((D_MODEL, n), F32), grid=(n // tn,),
        in_specs=[_full((N_DEV, D_MODEL)), pl.BlockSpec((N_DEV, tn), lambda j: (0, j))],
        out_specs=pl.BlockSpec((D_MODEL, tn), lambda j: (0, j)),
        compiler_params=_cp(("parallel",)))(c_all, dmod_cols)


def _ssm_tables(W):
    e_re, e_im, bb_re, bb_im = _ssm_prep(W["ssm_a_re"], W["ssm_a_im"], W["ssm_b_re"], W["ssm_b_im"], W["ssm_log_dt"])
    bb, cm = _block_diag_mats(bb_re, bb_im, W["ssm_c_re"], W["ssm_c_im"])
    bb16, cm16 = bb.astype(BF16), cm.astype(BF16)
    return (bb16, cm16, jnp.swapaxes(bb16, 1, 2), jnp.swapaxes(cm16, 1, 2),
            _scan_tables(e_re, e_im, False), _scan_tables(e_re, e_im, True))


def _device_step(x, mod, W, tables, tgt, getw, put, early):
    sh1, sc1, g1, sh2, sc2, g2 = [mod[:, i * D_MODEL:(i + 1) * D_MODEL] for i in range(6)]
    bb16, cm16, bbt16, cmt16, tab_f, tab_b = tables

    w_in = getw("w_in", [mod, *tables])
    h1, z = _in_proj(x, W["norm1_g"], sc1, sh1, w_in)
    yc, scv = _conv_fwd(z, W["conv_w"], W["conv_b"], W["conv_ln_g"], W["conv_ln_b"])
    xs, ys, yg = _ssm_fwd(z, bb16, cm16, W["ssm_d"], tab_f)
    w_cp, w_glu, w_out = getw("conv_proj", scv), getw("ssm_glu", yg), getw("w_out", yg)
    y_conv, zz, merged, o, x2, h2 = _mix_fwd(scv, yg, z, x, w_cp, w_glu, w_out, g1, W["norm2_g"], sc2, sh2)
    w_fi = getw("w_ffn_in", h2)
    f, act = _ffn_in_act(h2, w_fi)
    w_fo = getw("w_ffn_out", act)
    dx3, do2, loss8, dfg8, dg2_8 = _ffn_out_final(x2, act, w_fo, g2, W["final_g"], tgt)

    sm = {}
    tok = put("w_ffn_out", _matmul(act, do2, "tn", 1408, 1024, 2048, BF16, "mm_g_ffn_out"))
    df = _ffn_bwd(do2, w_fo, f, tok)
    tok = put("w_ffn_in", _matmul(h2, df, "tn", 1024, 1408, 2048, BF16, "mm_g_ffn_in"))
    dx2, do, dsh2, dsc2, dn2, dg1_8 = _normmod_bwd(df, w_fi, x2, dx3, W["norm2_g"], sc2, g1, o, tok, "d_h2_normmod2_bwd")
    tok = put("w_out", _matmul(merged, do, "tn", 1024, 1024, 4096, BF16, "mm_g_w_out"))
    dyconv, dgl, dzz = _mix_bwd(do, w_out, z, zz, y_conv, tok)
    tok = put("ssm_glu", _matmul(yg, dzz, "tn", 512, 1024, 4096, BF16, "mm_g_ssm_glu"))
    tok = put("conv_proj", _matmul(scv, dyconv, "tn", 512, 1024, 4096, BF16, "mm_g_conv_proj", after=tok))
    du, de16, dd8, dc_full, dbb_full = _ssm_bwd(dzz, w_glu, ys, z, xs, cmt16, bbt16, W["ssm_d"], tab_b, tok)
    dyc, dlg8, dlb8, dcb8 = _conv_bwd_ln(dyconv, w_cp, yc, W["conv_ln_g"], W["conv_ln_b"])
    dz_conv, dcw = _conv_bwd(dyc, z, W["conv_w"])

    s8 = lambda a: jnp.sum(a, axis=0, keepdims=True)
    de = de16.reshape(2, 8, NST).sum(1)
    de_re, de_im = de[0].reshape(G, P), de[1].reshape(G, P)
    dc_re = _diag_blocks(dc_full, False)
    dc_im = -_diag_blocks(dc_full, True)
    dbb_re = jnp.swapaxes(_diag_blocks(dbb_full, False), 1, 2)
    dbb_im = jnp.swapaxes(_diag_blocks(dbb_full, True), 1, 2)
    _, vjp = jax.vjp(_ssm_prep, W["ssm_a_re"], W["ssm_a_im"], W["ssm_b_re"], W["ssm_b_im"], W["ssm_log_dt"])
    sm["ssm_a_re"], sm["ssm_a_im"], sm["ssm_b_re"], sm["ssm_b_im"], sm["ssm_log_dt"] = vjp((de_re, de_im, dbb_re, dbb_im))
    sm["ssm_c_re"], sm["ssm_c_im"] = dc_re, dc_im
    sm["ssm_d"] = s8(dd8)
    sm["norm2_g"] = s8(dn2)
    sm["conv_b"], sm["conv_ln_g"], sm["conv_ln_b"] = s8(dcb8), s8(dlg8), s8(dlb8)
    sm["conv_w"] = dcw.reshape(KW, 8, CW).sum(1)
    sm["final_g"] = s8(dfg8)
    tok = early(sm)

    dz = [dz_conv, du, dgl]
    tok = put("w_in", _matmul(h1, dz, "tn", 1024, 512, 4096, BF16, "mm_g_w_in", after=tok))
    dx, _, dsh1, dsc1, dn1, _ = _normmod_bwd(dz, w_in, x, dx2, W["norm1_g"], sc1, g1, o, tok, "d_h1_normmod1_bwd")
    dmod = jnp.concatenate([s8(dsh1), s8(dsc1), s8(dg1_8), s8(dsh2), s8(dsc2), s8(dg2_8)], axis=1)
    return loss8, dx, s8(dn1), dmod


_BIG = ("w_in", "conv_proj", "ssm_glu", "w_out", "w_ffn_in", "w_ffn_out")
_BIG_AXIS = {"w_in": 1, "conv_proj": 1, "ssm_glu": 1, "w_out": 0, "w_ffn_in": 1, "w_ffn_out": 0}
_EARLY = ("conv_w", "conv_b", "conv_ln_g", "conv_ln_b", "ssm_a_re", "ssm_a_im", "ssm_b_re", "ssm_b_im", "ssm_c_re",
          "ssm_c_im", "ssm_d", "ssm_log_dt", "norm2_g", "final_g")
_LATE = ("norm1_g", "b_ada")
_ORDER = ("w_ada", "b_ada", "norm1_g", "w_in", "conv_w", "conv_b", "conv_ln_g", "conv_ln_b", "conv_proj",
          "ssm_a_re", "ssm_a_im", "ssm_b_re", "ssm_b_im", "ssm_c_re", "ssm_c_im", "ssm_d", "ssm_log_dt", "ssm_glu",
          "w_out", "norm2_g", "w_ffn_in", "w_ffn_out", "final_g")
_PACK_COLS = 1024


def _pack_rows(shape):
    return -(-int(np.prod(shape)) // (8 * _PACK_COLS)) * 8


def _pack(arrs):
    parts = []
    for a in arrs:
        flat = a.reshape(-1)
        n = _pack_rows(a.shape)
        parts.append(jnp.pad(flat, (0, n * _PACK_COLS - flat.shape[0])).reshape(n, _PACK_COLS))
    return jnp.concatenate(parts, 0)


def _unpack(packed, shapes):
    out, r = [], 0
    for shp in shapes:
        size = int(np.prod(shp))
        n = _pack_rows(shp)
        out.append(packed[r:r + n].reshape(-1)[:size].reshape(shp))
        r += n
    return out


def kernel(x, c, w_ada, b_ada, norm1_g, w_in, conv_w, conv_b, conv_ln_g, conv_ln_b, conv_proj, ssm_a_re, ssm_a_im, ssm_b_re, ssm_b_im, ssm_c_re, ssm_c_im, ssm_d, ssm_log_dt, ssm_glu, w_out, norm2_g, w_ffn_in, w_ffn_out, final_g, loss_target, m_w_ada, m_b_ada, m_norm1_g, m_w_in, m_conv_w, m_conv_b, m_conv_ln_g, m_conv_ln_b, m_conv_proj, m_ssm_a_re, m_ssm_a_im, m_ssm_b_re, m_ssm_b_im, m_ssm_c_re, m_ssm_c_im, m_ssm_d, m_ssm_log_dt, m_ssm_glu, m_w_out, m_norm2_g, m_w_ffn_in, m_w_ffn_out, m_final_g, v_w_ada, v_b_ada, v_norm1_g, v_w_in, v_conv_w, v_conv_b, v_conv_ln_g, v_conv_ln_b, v_conv_proj, v_ssm_a_re, v_ssm_a_im, v_ssm_b_re, v_ssm_b_im, v_ssm_c_re, v_ssm_c_im, v_ssm_d, v_ssm_log_dt, v_ssm_glu, v_w_out, v_norm2_g, v_w_ffn_in, v_w_ffn_out, v_final_g):
    given = dict(locals())
    mx, my, mc = _me()
    chip = 2 * mx + my
    dev = 4 * mx + 2 * my + mc
    def canon(a):
        return a.reshape(1, -1) if a.ndim <= 2 else a[0]

    wts = {n: canon(given[n]) for n in _ORDER}
    mom = {n: canon(given["m_" + n]) for n in _ORDER}
    var = {n: canon(given["v_" + n]) for n in _ORDER}

    c_all = _allgather8(jnp.broadcast_to(c, (8, D_MODEL)), "gather_c")[:, 0, :]
    n_ada = wts["w_ada"].shape[1]
    b_cols = lax.dynamic_slice(wts["b_ada"], (0, chip * n_ada), (1, n_ada))
    mod_cols = _mod_shard(c_all, wts["w_ada"], b_cols)
    mods = _allgather8(mod_cols, "gather_mod")
    mod = jnp.concatenate([lax.dynamic_index_in_dim(mods[2 * q], dev, 0, keepdims=True) for q in range(N_CHIP)], axis=1)
    W = {n: wts[n] for n in _ORDER if n not in _BIG}
    conv_w_full = _allgather8(jnp.pad(wts["conv_w"], ((0, 1), (0, 0))), "gather_conv_w", after=[c_all])
    W["conv_w"] = jnp.concatenate([conv_w_full[2 * q, :KW] for q in range(N_CHIP)], axis=1)

    gstate, token = _gather_start([wts[n].astype(BF16) for n in _BIG], [_BIG_AXIS[n] for n in _BIG],
                                  mod + W["conv_w"][0:1, 0:1], "gather_start")
    gstate = dict(zip(_BIG, gstate))
    mod = mod + token[0:1, 0:1]
    W["ssm_log_dt"] = wts["ssm_log_dt"] + token[0:1, 0:1]
    W["ssm_c_re"] = wts["ssm_c_re"] + token[0, 0]
    tables = _ssm_tables(W)

    def getw(n, after):
        return _gather_wait(gstate[n], _BIG_AXIS[n], after, "gather_wait_" + n)

    sstate, own, estate = {}, {}, []

    def put(n, g):
        ax = _BIG_AXIS[n]
        k = g.shape[ax] // N_CHIP
        own[n] = lax.dynamic_slice_in_dim(g, chip * k, k, axis=ax)
        sstate[n], tok = _scatter_start(g, ax, "scatter_start_" + n)
        return tok

    first5 = [n for n in _BIG if n != "w_in"]

    def early(sm):
        state, tok = _all8_start(_pack([sm[n] for n in _EARLY]), "small_start")
        estate.append(state)
        recv5 = [_scatter_wait(sstate[n], _BIG_AXIS[n], tok, "scatter_wait_" + n) for n in first5]
        held = [a for n, r in zip(first5, recv5) for a in (own[n], r)]
        state, tok = _swap_start(held, tok, "swap_start")
        estate.append(state)
        return tok

    loss8, dx, dn1, dmod = _device_step(x[0], mod, W, tables, loss_target[0], getw, put, early)

    held5, sib5 = _swap_wait(estate[1], dx, "swap_wait")
    outs = {}
    for i, n in enumerate(first5):
        outs[n] = _adamw(wts[n], mom[n], var[n], [held5[2 * i:2 * i + 2], sib5[2 * i:2 * i + 2]], "adamw_" + n)
    allp = _all8_wait(estate[0], dx, "small_wait")

    late = _allgather8(_pack([dn1, dmod, loss8]), "gather_late", after=[outs[n][1] for n in first5])
    n_late = _pack_rows((D_MODEL,)) + _pack_rows((6 * D_MODEL,))
    loss = jnp.sum(late[:, n_late:, :])
    late = late[:, :n_late, :]
    held_in = [own["w_in"], _scatter_wait(sstate["w_in"], _BIG_AXIS["w_in"], late, "scatter_wait_w_in")]
    sib_in = _swap_sibling(held_in)
    outs["w_in"] = _adamw(wts["w_in"], mom["w_in"], var["w_in"], [held_in, sib_in], "adamw_w_in")

    r1 = _pack_rows((D_MODEL,))
    dmod_all = late[:, r1:, :].reshape(N_DEV, -1)[:, :6 * D_MODEL]
    dmod_cols = lax.dynamic_slice(dmod_all, (0, chip * n_ada), (N_DEV, n_ada))
    g_ada = _ada_grad(c_all, dmod_cols)
    outs["w_ada"] = _adamw(wts["w_ada"], mom["w_ada"], var["w_ada"], [[g_ada]], "adamw_w_ada")

    def packed_params(d, names):
        return _pack([jnp.zeros((KW, CW), F32) if n == "conv_w" else d[n] for n in names])

    for names, parts, nm in ((_EARLY, allp, "adamw_small"), (_LATE, late, "adamw_late")):
        res = _adamw(packed_params(wts, names), packed_params(mom, names), packed_params(var, names), [[parts]], nm)
        shapes = [(KW, CW) if n == "conv_w" else wts[n].shape for n in names]
        unpacked = [_unpack(r, shapes) for r in res]
        for idx, n in enumerate(names):
            outs[n] = tuple(unpacked[q][idx] for q in range(4))
    g_cw = lax.dynamic_slice(outs["conv_w"][0], (0, chip * (CW // N_CHIP)), (KW, CW // N_CHIP))
    pad = lambda a: jnp.pad(a, ((0, 1), (0, 0)))
    r_cw = _adamw(pad(wts["conv_w"]), pad(mom["conv_w"]), pad(var["conv_w"]), [[pad(g_cw)]], "adamw_conv_w")
    outs["conv_w"] = tuple(r[:KW] for r in r_cw)

    def shaped(n, a):
        return a.reshape(given[n].shape)

    result = [loss, dx[None]]
    for q in range(4):
        result += [shaped(n, outs[n][q]) for n in _ORDER]
    return tuple(result)
```

```python
import math

import jax
import jax.numpy as jnp
import numpy as np
from jax import lax
from jax.experimental import pallas as pl
from jax.experimental.pallas import tpu as pltpu

F32 = jnp.float32
BF16 = jnp.bfloat16
EPS = 1e-6
D_MODEL = 1024
CW = 512
KW = 31
HALO = 32
G, P, H = 32, 64, 16
NST = G * P
FH = 2816
N_DEV = 8
N_CHIP = 4
VMEM_LIMIT = 56 * 1024 * 1024
LR, B1, B2, AEPS, WD, STEP = 0.001, 0.9, 0.999, 1e-08, 0.01, 10
MESH = pl.DeviceIdType.MESH


def _cp(sem=None):
    return pltpu.CompilerParams(dimension_semantics=sem, vmem_limit_bytes=VMEM_LIMIT)


def _sig(x):
    return jax.nn.sigmoid(x)


def _full(shape):
    return pl.BlockSpec(shape, lambda *_: (0,) * len(shape))


def _colsum8(v):
    t, c = v.shape
    return jnp.sum(v.reshape(t // 8, 8, c), axis=0)


def _matmul(a, b, mode, tm, tn, tk, out_dtype, name, after=None, n_outer=False, m_cols=None):
    m0 = 0
    b_parts = list(b) if isinstance(b, (list, tuple)) else [b]
    if mode == "nn":
        (M, K), N = a.shape, b.shape[1]
    elif mode == "nt":
        (M, K), N = a.shape, b.shape[0]
    else:
        (K, M), N = a.shape, sum(p.shape[1] for p in b_parts)
        if m_cols is not None:
            m0, M = m_cols
    tm, tn, tk = min(tm, M), min(tn, N), min(tk, K)
    assert M % tm == 0 and N % tn == 0 and K % tk == 0 and m0 % tm == 0, (name, M, N, K, tm, tn, tk)
    assert len(b_parts) == 1 or (mode == "tn" and all(p.shape[1] % tn == 0 for p in b_parts)), name
    nk = K // tk
    mb = m0 // tm
    counts = [p.shape[1] // tn for p in b_parts] if mode == "tn" else [N // tn]
    starts = [sum(counts[:p]) for p in range(len(counts))]

    def ij(fn):
        return (lambda j, i, k: fn(i, j, k)) if n_outer else fn

    if mode == "nn":
        a_spec = pl.BlockSpec((tm, tk), ij(lambda i, j, k: (i, k)))
        b_spec = pl.BlockSpec((tk, tn), ij(lambda i, j, k: (k, j)))
        dims = (((1,), (0,)), ((), ()))
    elif mode == "nt":
        a_spec = pl.BlockSpec((tm, tk), ij(lambda i, j, k: (i, k)))
        b_spec = pl.BlockSpec((tn, tk), ij(lambda i, j, k: (j, k)))
        dims = (((1,), (1,)), ((), ()))
    else:
        a_spec = pl.BlockSpec((tk, tm), ij(lambda i, j, k: (k, i + mb)))
        dims = (((0,), (0,)), ((), ()))
    if mode == "tn":
        b_specs = [pl.BlockSpec((tk, tn), ij(lambda i, j, k, s=s, n=n: (k, jnp.clip(j - s, 0, n - 1))))
                   for s, n in zip(starts, counts)]
    else:
        b_specs = [b_spec]
    nb = len(b_parts)

    def body(a_ref, *rest):
        b_refs = rest[:nb]
        o_ref, acc_ref = rest[-2:]
        j = pl.program_id(0 if n_outer else 1)
        k = pl.program_id(2)

        def compute(b_ref):
            part = lax.dot_general(a_ref[...].astype(BF16), b_ref[...].astype(BF16), dims,
                                   preferred_element_type=F32)
            if nk == 1:
                o_ref[...] = part.astype(out_dtype)
            else:
                @pl.when(k == 0)
                def _():
                    acc_ref[...] = part

                @pl.when(k > 0)
                def _():
                    acc_ref[...] += part

                @pl.when(k == nk - 1)
                def _():
                    o_ref[...] = acc_ref[...].astype(out_dtype)

        if nb == 1:
            compute(b_refs[0])
        else:
            for p in range(nb):
                pl.when(jnp.logical_and(j >= starts[p], j < starts[p] + counts[p]))(
                    lambda b_ref=b_refs[p]: compute(b_ref))

    return pl.pallas_call(
        body, name=name,
        out_shape=jax.ShapeDtypeStruct((M, N), out_dtype),
        grid=(N // tn, M // tm, nk) if n_outer else (M // tm, N // tn, nk),
        in_specs=[a_spec] + b_specs + ([] if after is None else [pl.BlockSpec(memory_space=pl.ANY)]),
        out_specs=pl.BlockSpec((tm, tn), ij(lambda i, j, k: (i, j))),
        scratch_shapes=[pltpu.VMEM((tm, tn) if nk > 1 else (8, 128), F32)],
        compiler_params=_cp(("parallel", "parallel", "arbitrary")),
    )(*([a] + b_parts + ([] if after is None else [after])))


def _row_tile(S):
    return min(512, S)


def _in_proj(x, g, sc, sh, w_in):
    S, D = x.shape
    N = w_in.shape[1]
    tm = min(256, S)

    def body(x_ref, g_ref, sc_ref, sh_ref, w_ref, h_ref, z_ref):
        xv = x_ref[...]
        r = lax.rsqrt(jnp.mean(xv * xv, axis=-1, keepdims=True) + EPS)
        h = (xv * r * (g_ref[...] * (1.0 + sc_ref[...])) + sh_ref[...]).astype(BF16)
        h_ref[...] = h
        z_ref[...] = jnp.dot(h, w_ref[...], preferred_element_type=F32).astype(BF16)

    row = pl.BlockSpec((tm, D), lambda i: (i, 0))
    par = _full((1, D))
    return pl.pallas_call(
        body, name="in_proj",
        out_shape=(jax.ShapeDtypeStruct((S, D), BF16), jax.ShapeDtypeStruct((S, N), BF16)), grid=(S // tm,),
        in_specs=[row, par, par, par, _full((D, N))], out_specs=(row, pl.BlockSpec((tm, N), lambda i: (i, 0))),
        compiler_params=_cp(("parallel",)))(x, g, sc, sh, w_in)


def _fill_shifted(buf_ref, sh_ref):
    n = buf_ref.shape[0] - 8
    for s in range(1, 8):
        sh_ref[s, 0:n, :] = buf_ref[s:s + n, :]


def _window(buf_ref, sh_ref, off, n):
    s = off % 8
    return buf_ref[off:off + n, :] if s == 0 else sh_ref[s, off - s:off - s + n, :]


def _conv_fwd(z, conv_w, conv_b, ln_g, ln_b):
    S = z.shape[0]
    tm = min(128, S)
    sub = 32
    hb = tm // HALO

    def body(a_ref, g_ref, ha_ref, hg_ref, w_ref, b_ref, lg_ref, lb_ref, yc_ref, s_ref, ug_ref, sh_ref):
        i = pl.program_id(0)
        halo = ha_ref[...].astype(F32) * _sig(hg_ref[...].astype(F32))
        ug_ref[0:HALO, :] = jnp.where(i == 0, 0.0, halo)
        ug_ref[HALO:, :] = a_ref[...].astype(F32) * _sig(g_ref[...].astype(F32))
        _fill_shifted(ug_ref, sh_ref)
        for rb in range(tm // sub):
            acc = jnp.zeros((sub, CW), F32) + b_ref[...]
            for k in range(KW):
                off = rb * sub + HALO - (KW - 1) + k
                acc = acc + w_ref[k:k + 1, :] * _window(ug_ref, sh_ref, off, sub)
            yc_ref[rb * sub:(rb + 1) * sub, :] = acc
            mu = jnp.mean(acc, axis=-1, keepdims=True)
            cen = acc - mu
            rstd = lax.rsqrt(jnp.mean(cen * cen, axis=-1, keepdims=True) + EPS)
            ln = cen * rstd * lg_ref[...] + lb_ref[...]
            s_ref[rb * sub:(rb + 1) * sub, :] = (ln * _sig(ln)).astype(BF16)

    prev = lambda i: (jnp.maximum(i * hb - 1, 0), 0)
    return pl.pallas_call(
        body, name="conv_fwd",
        out_shape=(jax.ShapeDtypeStruct((S, CW), F32), jax.ShapeDtypeStruct((S, CW), BF16)),
        grid=(S // tm,),
        in_specs=[pl.BlockSpec((tm, CW), lambda i: (i, 0)), pl.BlockSpec((tm, CW), lambda i: (i, 1)),
                  pl.BlockSpec((HALO, CW), prev), pl.BlockSpec((HALO, CW), lambda i: (jnp.maximum(i * hb - 1, 0), 1)),
                  _full((KW, CW)), _full((1, CW)), _full((1, CW)), _full((1, CW))],
        out_specs=(pl.BlockSpec((tm, CW), lambda i: (i, 0)), pl.BlockSpec((tm, CW), lambda i: (i, 0))),
        scratch_shapes=[pltpu.VMEM((tm + HALO, CW), F32), pltpu.VMEM((8, tm + HALO, CW), F32)],
        compiler_params=_cp(("parallel",)))(z, z, z, z, conv_w, conv_b, ln_g, ln_b)


def _conv_bwd_ln(dyconv, w_cp, yc, ln_g, ln_b):
    S = yc.shape[0]
    tm = _row_tile(S)

    def body(dy_ref, w_ref, yc_ref, lg_ref, lb_ref, dyc_ref, dlg_ref, dlb_ref, dcb_ref):
        i = pl.program_id(0)
        dsc = lax.dot_general(dy_ref[...], w_ref[...], (((1,), (1,)), ((), ())), preferred_element_type=F32)
        yc_v = yc_ref[...]
        mu = jnp.mean(yc_v, axis=-1, keepdims=True)
        cen = yc_v - mu
        rstd = lax.rsqrt(jnp.mean(cen * cen, axis=-1, keepdims=True) + EPS)
        yn = cen * rstd
        ln = yn * lg_ref[...] + lb_ref[...]
        sl = _sig(ln)
        dln = dsc * (sl * (1.0 + ln * (1.0 - sl)))
        dyn = dln * lg_ref[...]
        dyc = rstd * (dyn - jnp.mean(dyn, axis=-1, keepdims=True)
                      - yn * jnp.mean(dyn * yn, axis=-1, keepdims=True))
        dyc_ref[...] = dyc

        @pl.when(i == 0)
        def _():
            dlg_ref[...] = jnp.zeros_like(dlg_ref)
            dlb_ref[...] = jnp.zeros_like(dlb_ref)
            dcb_ref[...] = jnp.zeros_like(dcb_ref)

        dlg_ref[...] += _colsum8(dln * yn)
        dlb_ref[...] += _colsum8(dln)
        dcb_ref[...] += _colsum8(dyc)

    row = pl.BlockSpec((tm, CW), lambda i: (i, 0))
    acc = jax.ShapeDtypeStruct((8, CW), F32)
    return pl.pallas_call(
        body, name="conv_bwd_ln",
        out_shape=(jax.ShapeDtypeStruct((S, CW), F32), acc, acc, acc), grid=(S // tm,),
        in_specs=[pl.BlockSpec((tm, D_MODEL), lambda i: (i, 0)), _full((CW, D_MODEL)), row, _full((1, CW)),
                  _full((1, CW))],
        out_specs=(row, _full((8, CW)), _full((8, CW)), _full((8, CW))),
        compiler_params=_cp(("arbitrary",)))(dyconv, w_cp, yc, ln_g, ln_b)


def _conv_bwd(dyc, z, conv_w):
    S = z.shape[0]
    tm = min(128, S)
    sub = 32
    hb = tm // HALO
    nt = S // tm

    def body(d_ref, dn_ref, a_ref, g_ref, ha_ref, hg_ref, w_ref, dz_ref, dw_ref, ug_ref, dy_ref, ugs_ref, dys_ref):
        i = pl.program_id(0)
        halo = ha_ref[...].astype(F32) * _sig(hg_ref[...].astype(F32))
        ug_ref[0:HALO, :] = jnp.where(i == 0, 0.0, halo)
        a = a_ref[...].astype(F32)
        sg = _sig(g_ref[...].astype(F32))
        ug_ref[HALO:, :] = a * sg
        dy_ref[0:tm, :] = d_ref[...]
        dy_ref[tm:, :] = jnp.where(i == nt - 1, 0.0, dn_ref[...])
        _fill_shifted(ug_ref, ugs_ref)
        _fill_shifted(dy_ref, dys_ref)

        @pl.when(i == 0)
        def _():
            dw_ref[...] = jnp.zeros_like(dw_ref)

        for rb in range(tm // sub):
            r0 = rb * sub
            acc = jnp.zeros((sub, CW), F32)
            dyc_b = dy_ref[r0:r0 + sub, :]
            for k in range(KW):
                up = r0 + (KW - 1) - k
                acc = acc + w_ref[k:k + 1, :] * _window(dy_ref, dys_ref, up, sub)
                off = r0 + HALO - (KW - 1) + k
                dw_ref[k * 8:(k + 1) * 8, :] += _colsum8(dyc_b * _window(ug_ref, ugs_ref, off, sub))
            a_b = a[r0:r0 + sub, :]
            sg_b = sg[r0:r0 + sub, :]
            dz_ref[r0:r0 + sub, 0:CW] = (acc * sg_b).astype(BF16)
            dz_ref[r0:r0 + sub, CW:2 * CW] = (acc * a_b * sg_b * (1.0 - sg_b)).astype(BF16)

    return pl.pallas_call(
        body, name="conv_bwd",
        out_shape=(jax.ShapeDtypeStruct((S, 2 * CW), BF16), jax.ShapeDtypeStruct((KW * 8, CW), F32)),
        grid=(nt,),
        in_specs=[pl.BlockSpec((tm, CW), lambda i: (i, 0)),
                  pl.BlockSpec((HALO, CW), lambda i: (jnp.minimum((i + 1) * hb, nt * hb - 1), 0)),
                  pl.BlockSpec((tm, CW), lambda i: (i, 0)), pl.BlockSpec((tm, CW), lambda i: (i, 1)),
                  pl.BlockSpec((HALO, CW), lambda i: (jnp.maximum(i * hb - 1, 0), 0)),
                  pl.BlockSpec((HALO, CW), lambda i: (jnp.maximum(i * hb - 1, 0), 1)),
                  _full((KW, CW))],
        out_specs=(pl.BlockSpec((tm, 2 * CW), lambda i: (i, 0)), _full((KW * 8, CW))),
        scratch_shapes=[pltpu.VMEM((tm + HALO, CW), F32), pltpu.VMEM((tm + HALO, CW), F32),
                        pltpu.VMEM((8, tm + HALO, CW), F32), pltpu.VMEM((8, tm + HALO, CW), F32)],
        compiler_params=_cp(("arbitrary",)))(dyc, dyc, z, z, z, z, conv_w)


_GELU_C = math.sqrt(2.0 / math.pi)


def _gelu(x):
    return 0.5 * x * (1.0 + jnp.tanh(_GELU_C * (x + 0.044715 * x * x * x)))


def _gelu_grad(x):
    t = jnp.tanh(_GELU_C * (x + 0.044715 * x * x * x))
    return 0.5 * (1.0 + t) + 0.5 * x * (1.0 - t * t) * (_GELU_C * (1.0 + 3 * 0.044715 * x * x))


_NCL = 4
_UC = CW // _NCL
_LW = NST // _NCL
_CS = 2 * _LW


def _ssm_fwd(z, bb, cm, d, tab):
    S = z.shape[0]
    tm = min(256, S)

    def body(u_ref, bb_ref, cm_ref, d_ref, t_ref, x_ref, ys_ref, yg_ref, car_ref):
        i = pl.program_id(0)

        @pl.when(i == 0)
        def _():
            car_ref[...] = jnp.zeros_like(car_ref)

        u16 = u_ref[...]
        u = u16.astype(F32)
        for c in range(_NCL):
            lre = pl.ds(c * _CS, _LW)
            lim = pl.ds(c * _CS + _LW, _LW)
            tl = pl.ds(c * _LW, _LW)
            x_ref[:, c * _CS:(c + 1) * _CS] = jnp.dot(u16[:, c * _UC:(c + 1) * _UC], bb_ref[c],
                                                      preferred_element_type=F32)

            def blk(j, car):
                cr, ci = car
                rows = pl.ds(pl.multiple_of(j * 8, 8), 8)
                r = x_ref[rows, lre]
                im = x_ref[rows, lim]
                for lvl, s in enumerate((1, 2, 4)):
                    mr = t_ref[16 * lvl:16 * lvl + 8, tl]
                    mi = t_ref[16 * lvl + 8:16 * lvl + 16, tl]
                    sr = pltpu.roll(r, s, 0)
                    si = pltpu.roll(im, s, 0)
                    r, im = r + (mr * sr - mi * si), im + (mr * si + mi * sr)
                pr = t_ref[48:56, tl]
                pi_ = t_ref[56:64, tl]
                r, im = r + (pr * cr - pi_ * ci), im + (pr * ci + pi_ * cr)
                x_ref[rows, lre] = r
                x_ref[rows, lim] = im
                return (jnp.broadcast_to(r[7:8, :], (8, _LW)), jnp.broadcast_to(im[7:8, :], (8, _LW)))

            cr, ci = lax.fori_loop(0, tm // 8, blk, (car_ref[:, lre], car_ref[:, lim]))
            car_ref[:, lre] = cr
            car_ref[:, lim] = ci
            cols = slice(c * _UC, (c + 1) * _UC)
            ys = jnp.dot(x_ref[:, c * _CS:(c + 1) * _CS].astype(BF16), cm_ref[c], preferred_element_type=F32)
            ys = ys + d_ref[:, cols] * u[:, cols]
            ys_ref[:, cols] = ys
            yg_ref[:, cols] = _gelu(ys).astype(BF16)

    return pl.pallas_call(
        body, name="ssm_fwd",
        out_shape=(jax.ShapeDtypeStruct((S, 2 * NST), F32), jax.ShapeDtypeStruct((S, CW), F32),
                   jax.ShapeDtypeStruct((S, CW), BF16)),
        grid=(S // tm,),
        in_specs=[pl.BlockSpec((tm, CW), lambda i: (i, 2)), _full((_NCL, _UC, _CS)), _full((_NCL, _CS, _UC)),
                  _full((1, CW)), _full((64, NST))],
        out_specs=(pl.BlockSpec((tm, 2 * NST), lambda i: (i, 0)), pl.BlockSpec((tm, CW), lambda i: (i, 0)),
                   pl.BlockSpec((tm, CW), lambda i: (i, 0))),
        scratch_shapes=[pltpu.VMEM((8, 2 * NST), F32)],
        compiler_params=_cp(("arbitrary",)))(z, bb, cm, d, tab)


def _ssm_bwd(dzz, w_glu, ys, z, xs, cmt, bbt, d, tab, after):
    S = z.shape[0]
    tm = min(256, S)
    nt = S // tm
    tdims = (((0,), (0,)), ((), ()))

    def body(dzz_ref, wglu_ref, ys_ref, u_ref, x_ref, cmt_ref, bbt_ref, d_ref, t_ref, after_ref,
             du_ref, de_ref, dd_ref, dc_hbm, dbb_hbm, car_ref, lam_ref, dc_ref, dbb_ref):
        i = pl.program_id(0)

        @pl.when(i == 0)
        def _():
            car_ref[...] = jnp.zeros_like(car_ref)
            de_ref[...] = jnp.zeros_like(de_ref)
            dd_ref[...] = jnp.zeros_like(dd_ref)
            dc_ref[...] = jnp.zeros_like(dc_ref)
            dbb_ref[...] = jnp.zeros_like(dbb_ref)

        u16 = u_ref[...]
        u = u16.astype(F32)
        dyg = lax.dot_general(dzz_ref[...], wglu_ref[...], (((1,), (1,)), ((), ())), preferred_element_type=F32)
        dys = dyg * _gelu_grad(ys_ref[...])
        dys16 = dys.astype(BF16)
        dd_ref[...] += _colsum8(dys * u)
        row = lax.broadcasted_iota(jnp.int32, (8, _LW), 0)
        for c in range(_NCL):
            lre = pl.ds(c * _CS, _LW)
            lim = pl.ds(c * _CS + _LW, _LW)
            tl = pl.ds(c * _LW, _LW)
            cols = slice(c * _UC, (c + 1) * _UC)
            span = slice(c * _CS, (c + 1) * _CS)
            dc_ref[cols, :] += lax.dot_general(dys16[:, cols], x_ref[:, span].astype(BF16), tdims,
                                               preferred_element_type=F32)
            lam_ref[...] = jnp.dot(dys16[:, cols], cmt_ref[c], preferred_element_type=F32)

            def blk(jj, car):
                cr, ci, ar, ai = car
                j = tm // 8 - 1 - jj
                rows = pl.ds(pl.multiple_of(j * 8, 8), 8)
                r = lam_ref[rows, 0:_LW]
                im = lam_ref[rows, _LW:_CS]
                for lvl, s in enumerate((1, 2, 4)):
                    mr = t_ref[16 * lvl:16 * lvl + 8, tl]
                    mi = t_ref[16 * lvl + 8:16 * lvl + 16, tl]
                    sr = pltpu.roll(r, 8 - s, 0)
                    si = pltpu.roll(im, 8 - s, 0)
                    r, im = r + (mr * sr - mi * si), im + (mr * si + mi * sr)
                pr = t_ref[48:56, tl]
                pi_ = t_ref[56:64, tl]
                r, im = r + (pr * cr - pi_ * ci), im + (pr * ci + pi_ * cr)
                lam_ref[rows, 0:_LW] = r
                lam_ref[rows, _LW:_CS] = im
                nr = jnp.where(row == 7, cr, pltpu.roll(r, 7, 0))
                ni = jnp.where(row == 7, ci, pltpu.roll(im, 7, 0))
                xr = x_ref[rows, lre]
                xi = x_ref[rows, lim]
                ar = ar + (nr * xr + ni * xi)
                ai = ai + (ni * xr - nr * xi)
                return (jnp.broadcast_to(r[0:1, :], (8, _LW)), jnp.broadcast_to(im[0:1, :], (8, _LW)), ar, ai)

            zero = jnp.zeros((8, _LW), F32)
            cr, ci, ar, ai = lax.fori_loop(0, tm // 8, blk, (car_ref[:, lre], car_ref[:, lim], zero, zero))
            car_ref[:, lre] = cr
            car_ref[:, lim] = ci
            de_ref[0:8, tl] += ar
            de_ref[8:16, tl] += ai
            lam16 = lam_ref[...].astype(BF16)
            dbb_ref[cols, :] += lax.dot_general(u16[:, cols], lam16, tdims, preferred_element_type=F32)
            du = jnp.dot(lam16, bbt_ref[c], preferred_element_type=F32) + dys[:, cols] * d_ref[:, cols]
            du_ref[:, cols] = du.astype(BF16)

        @pl.when(i == nt - 1)
        def _():
            pltpu.sync_copy(dc_ref, dc_hbm)
            pltpu.sync_copy(dbb_ref, dbb_hbm)

    rev = lambda i: (nt - 1 - i, 0)
    once = lambda shape: pl.BlockSpec(shape, lambda *_: (0,) * len(shape), pipeline_mode=pl.Buffered(1))
    cross = jax.ShapeDtypeStruct((CW, _CS), F32)
    return pl.pallas_call(
        body, name="ssm_bwd",
        out_shape=(jax.ShapeDtypeStruct((S, CW), BF16), jax.ShapeDtypeStruct((16, NST), F32),
                   jax.ShapeDtypeStruct((8, CW), F32), cross, cross),
        grid=(nt,),
        in_specs=[pl.BlockSpec((tm, 2 * D_MODEL), rev), once((CW, 2 * D_MODEL)), pl.BlockSpec((tm, CW), rev),
                  pl.BlockSpec((tm, CW), lambda i: (nt - 1 - i, 2)), pl.BlockSpec((tm, 2 * NST), rev),
                  once((_NCL, _UC, _CS)), once((_NCL, _CS, _UC)), _full((1, CW)), once((64, NST)),
                  pl.BlockSpec(memory_space=pl.ANY)],
        out_specs=(pl.BlockSpec((tm, CW), rev), _full((16, NST)), _full((8, CW)),
                   pl.BlockSpec(memory_space=pl.ANY), pl.BlockSpec(memory_space=pl.ANY)),
        scratch_shapes=[pltpu.VMEM((8, 2 * NST), F32), pltpu.VMEM((tm, _CS), F32),
                        pltpu.VMEM((CW, _CS), F32), pltpu.VMEM((CW, _CS), F32)],
        compiler_params=_cp(("arbitrary",)))(dzz, w_glu, ys, z, xs, cmt, bbt, d, tab, after)


def _ssm_prep(a_re, a_im, b_re, b_im, log_dt):
    dt = jnp.exp(log_dt.reshape(G))[:, None]
    mag = jnp.exp(dt * a_re)
    e_re, e_im = mag * jnp.cos(dt * a_im), mag * jnp.sin(dt * a_im)
    n_re, n_im = e_re - 1.0, e_im
    den = a_re * a_re + a_im * a_im
    q_re = (n_re * a_re + n_im * a_im) / den
    q_im = (n_im * a_re - n_re * a_im) / den
    bb_re = q_re[..., None] * b_re - q_im[..., None] * b_im
    bb_im = q_re[..., None] * b_im + q_im[..., None] * b_re
    return e_re, e_im, bb_re, bb_im


def _scan_tables(e_re, e_im, reverse):
    er = e_re.reshape(1, NST)
    ei = e_im.reshape(1, NST)
    if reverse:
        ei = -ei
    pows = [(er, ei)]
    for _ in range(7):
        pr, pi_ = pows[-1]
        pows.append((pr * er - pi_ * ei, pr * ei + pi_ * er))
    row = jnp.arange(8)[:, None]
    out = []
    for s in (1, 2, 4):
        pr, pi_ = pows[s - 1]
        keep = (row + s <= 7) if reverse else (row >= s)
        out += [jnp.where(keep, pr, 0.0), jnp.where(keep, pi_, 0.0)]
    allr = jnp.concatenate([p[0] for p in pows], 0)
    alli = jnp.concatenate([p[1] for p in pows], 0)
    if reverse:
        allr, alli = allr[::-1], alli[::-1]
    out += [allr, alli]
    return jnp.concatenate(out, 0).astype(F32)


def _block_diag_mats(bb_re, bb_im, c_re, c_im):
    gc = G // _NCL
    eye = jnp.eye(gc, dtype=F32)
    bre = jnp.einsum("cjph,jk->cjhkp", bb_re.reshape(_NCL, gc, P, H), eye).reshape(_NCL, _UC, _LW)
    bim = jnp.einsum("cjph,jk->cjhkp", bb_im.reshape(_NCL, gc, P, H), eye).reshape(_NCL, _UC, _LW)
    bb = jnp.concatenate([bre, bim], 2)
    cre = jnp.einsum("cjhp,jk->cjpkh", c_re.reshape(_NCL, gc, H, P), eye).reshape(_NCL, _LW, _UC)
    cim = jnp.einsum("cjhp,jk->cjpkh", c_im.reshape(_NCL, gc, H, P), eye).reshape(_NCL, _LW, _UC)
    cm = jnp.concatenate([cre, -cim], 1)
    return bb, cm


def _diag_blocks(cross, imag):
    gc = G // _NCL
    off = _LW if imag else 0
    return jnp.stack([cross[H * g:H * (g + 1), off + P * (g % gc):off + P * (g % gc + 1)] for g in range(G)])


def _mix_fwd(scv, yg, z, x, w_cp, w_glu, w_out, g1, n2g, sc2, sh2):
    S = z.shape[0]
    tm = min(256, S)
    D = D_MODEL

    def body(s_ref, yg_ref, glc0_ref, glc1_ref, gls0_ref, gls1_ref, x_ref, wcp_ref, wglu_ref, wout_ref,
             g1_ref, n2_ref, sc_ref, sh_ref, yc_ref, zz_ref, m_ref, o_ref, x2_ref, h2_ref):
        y_conv = jnp.dot(s_ref[...], wcp_ref[...], preferred_element_type=F32)
        zz = jnp.dot(yg_ref[...], wglu_ref[...], preferred_element_type=F32)
        yc_ref[...] = y_conv.astype(BF16)
        zz_ref[...] = zz.astype(BF16)
        for half, (glc_ref, gls_ref) in enumerate(((glc0_ref, gls0_ref), (glc1_ref, gls1_ref))):
            lo, hi = half * CW, (half + 1) * CW
            y_ssm = zz[:, lo:hi] * _sig(zz[:, D + lo:D + hi])
            m_ref[:, lo:hi] = (_sig(glc_ref[...].astype(F32)) * y_conv[:, lo:hi]
                               + _sig(gls_ref[...].astype(F32)) * y_ssm).astype(BF16)
        o = jnp.dot(m_ref[...], wout_ref[...], preferred_element_type=F32)
        o_ref[...] = o.astype(BF16)
        xv = x_ref[...] + g1_ref[...] * o
        x2_ref[...] = xv
        r = lax.rsqrt(jnp.mean(xv * xv, axis=-1, keepdims=True) + EPS)
        h2_ref[...] = (xv * r * (n2_ref[...] * (1.0 + sc_ref[...])) + sh_ref[...]).astype(BF16)

    zb_ = lambda j: pl.BlockSpec((tm, CW), lambda i: (i, j))
    row = lambda w: pl.BlockSpec((tm, w), lambda i: (i, 0))
    par = _full((1, D))
    bf = lambda w: jax.ShapeDtypeStruct((S, w), BF16)
    return pl.pallas_call(
        body, name="mix_fwd",
        out_shape=(bf(D), bf(2 * D), bf(D), bf(D), jax.ShapeDtypeStruct((S, D), F32), bf(D)),
        grid=(S // tm,),
        in_specs=[row(CW), row(CW), zb_(3), zb_(4), zb_(5), zb_(6), row(D), _full((CW, D)), _full((CW, 2 * D)),
                  _full((D, D)), par, par, par, par],
        out_specs=(row(D), row(2 * D), row(D), row(D), row(D), row(D)),
        compiler_params=_cp(("parallel",)))(scv, yg, z, z, z, z, x, w_cp, w_glu, w_out, g1, n2g, sc2, sh2)


def _mix_bwd(do, w_out, z, zz, y_conv, after):
    S = z.shape[0]
    tm = min(256, S)
    D = D_MODEL

    def body(do_ref, w_ref, glc0_ref, glc1_ref, gls0_ref, gls1_ref, za_ref, zb_ref, yc_ref, after_ref,
             dyc_ref, dgl_ref, dzz_ref):
        dm = lax.dot_general(do_ref[...], w_ref[...], (((1,), (1,)), ((), ())), preferred_element_type=F32)
        for half, (glc_ref, gls_ref) in enumerate(((glc0_ref, gls0_ref), (glc1_ref, gls1_ref))):
            lo, hi = half * CW, (half + 1) * CW
            dm_v = dm[:, lo:hi]
            sgc = _sig(glc_ref[...].astype(F32))
            sgs = _sig(gls_ref[...].astype(F32))
            szb = _sig(zb_ref[:, lo:hi].astype(F32))
            za = za_ref[:, lo:hi].astype(F32)
            dyc_ref[:, lo:hi] = (dm_v * sgc).astype(BF16)
            dgl_ref[:, lo:hi] = (dm_v * yc_ref[:, lo:hi].astype(F32) * sgc * (1.0 - sgc)).astype(BF16)
            dys = dm_v * sgs
            dgl_ref[:, D + lo:D + hi] = (dys * (za * szb) * (1.0 - sgs)).astype(BF16)
            dzz_ref[:, lo:hi] = (dys * szb).astype(BF16)
            dzz_ref[:, D + lo:D + hi] = (dys * za * szb * (1.0 - szb)).astype(BF16)

    zb_ = lambda j: pl.BlockSpec((tm, CW), lambda i: (i, j))
    wide = lambda j: pl.BlockSpec((tm, D), lambda i: (i, j))
    return pl.pallas_call(
        body, name="mix_bwd",
        out_shape=(jax.ShapeDtypeStruct((S, D), BF16), jax.ShapeDtypeStruct((S, 2 * D), BF16),
                   jax.ShapeDtypeStruct((S, 2 * D), BF16)),
        grid=(S // tm,),
        in_specs=[wide(0), _full((D, D)), zb_(3), zb_(4), zb_(5), zb_(6), wide(0), wide(1), wide(0),
                  pl.BlockSpec(memory_space=pl.ANY)],
        out_specs=(wide(0), pl.BlockSpec((tm, 2 * D), lambda i: (i, 0)), pl.BlockSpec((tm, 2 * D), lambda i: (i, 0))),
        compiler_params=_cp(("parallel",)))(do, w_out, z, z, z, z, zz, zz, y_conv, after)


_FC = 1408


def _ffn_in_act(h2, w_fi):
    S, D = h2.shape
    tm = min(256, S)

    def body(h_ref, w_ref, f_ref, a_ref):
        hv = h_ref[...]
        for c in range(FH // _FC):
            lo, hi = c * _FC, (c + 1) * _FC
            g = jnp.dot(hv, w_ref[:, lo:hi], preferred_element_type=F32)
            u = jnp.dot(hv, w_ref[:, FH + lo:FH + hi], preferred_element_type=F32)
            f_ref[:, lo:hi] = g.astype(BF16)
            f_ref[:, FH + lo:FH + hi] = u.astype(BF16)
            a_ref[:, lo:hi] = (g * _sig(g) * u).astype(BF16)

    return pl.pallas_call(
        body, name="ffn_in_act",
        out_shape=(jax.ShapeDtypeStruct((S, 2 * FH), BF16), jax.ShapeDtypeStruct((S, FH), BF16)),
        grid=(S // tm,),
        in_specs=[pl.BlockSpec((tm, D), lambda i: (i, 0)), _full((D, 2 * FH))],
        out_specs=(pl.BlockSpec((tm, 2 * FH), lambda i: (i, 0)), pl.BlockSpec((tm, FH), lambda i: (i, 0))),
        compiler_params=_cp(("parallel",)))(h2, w_fi)


def _ffn_bwd(do2, w_fo, f, after):
    S, D = do2.shape
    tm = min(256, S)

    def body(d_ref, w_ref, f_ref, after_ref, df_ref):
        dv = d_ref[...]
        for c in range(FH // _FC):
            lo, hi = c * _FC, (c + 1) * _FC
            dact = lax.dot_general(dv, w_ref[lo:hi, :], (((1,), (1,)), ((), ())), preferred_element_type=F32)
            g = f_ref[:, lo:hi].astype(F32)
            u = f_ref[:, FH + lo:FH + hi].astype(F32)
            sg = _sig(g)
            df_ref[:, lo:hi] = (dact * u * (sg * (1.0 + g * (1.0 - sg)))).astype(BF16)
            df_ref[:, FH + lo:FH + hi] = (dact * g * sg).astype(BF16)

    return pl.pallas_call(
        body, name="ffn_bwd", out_shape=jax.ShapeDtypeStruct((S, 2 * FH), BF16), grid=(S // tm,),
        in_specs=[pl.BlockSpec((tm, D), lambda i: (i, 0)), _full((FH, D)),
                  pl.BlockSpec((tm, 2 * FH), lambda i: (i, 0)), pl.BlockSpec(memory_space=pl.ANY)],
        out_specs=pl.BlockSpec((tm, 2 * FH), lambda i: (i, 0)),
        compiler_params=_cp(("parallel",)))(do2, w_fo, f, after)


def _ffn_out_final(x2, act, w_fo, g2, fg, tgt):
    S, D = x2.shape
    tm = min(256, S)

    def body(x2_ref, a_ref, w_ref, g2_ref, fg_ref, t_ref, dx3_ref, do2_ref, ls_ref, dfg_ref, dg2_ref):
        i = pl.program_id(0)
        o2 = jnp.dot(a_ref[...], w_ref[...], preferred_element_type=F32)
        x3 = x2_ref[...] + g2_ref[...] * o2
        r = lax.rsqrt(jnp.mean(x3 * x3, axis=-1, keepdims=True) + EPS)
        xn = x3 * r
        err = xn * fg_ref[...] - t_ref[...]
        dy = err * (1.0 / D)
        dxn = dy * fg_ref[...]
        dx3 = r * (dxn - xn * jnp.mean(dxn * xn, axis=-1, keepdims=True))
        dx3_ref[...] = dx3
        do2_ref[...] = (dx3 * g2_ref[...]).astype(BF16)

        @pl.when(i == 0)
        def _():
            ls_ref[...] = jnp.zeros_like(ls_ref)
            dfg_ref[...] = jnp.zeros_like(dfg_ref)
            dg2_ref[...] = jnp.zeros_like(dg2_ref)

        e2 = _colsum8(err * err)
        lanes = e2[:, 0:128]
        for q in range(1, D // 128):
            lanes = lanes + e2[:, q * 128:(q + 1) * 128]
        ls_ref[...] += lanes * (0.5 / D)
        dfg_ref[...] += _colsum8(dy * xn)
        dg2_ref[...] += _colsum8(dx3 * o2)

    row = pl.BlockSpec((tm, D), lambda i: (i, 0))
    par = _full((1, D))
    return pl.pallas_call(
        body, name="final_loss",
        out_shape=(jax.ShapeDtypeStruct((S, D), F32), jax.ShapeDtypeStruct((S, D), BF16),
                   jax.ShapeDtypeStruct((8, 128), F32), jax.ShapeDtypeStruct((8, D), F32),
                   jax.ShapeDtypeStruct((8, D), F32)),
        grid=(S // tm,), in_specs=[row, pl.BlockSpec((tm, FH), lambda i: (i, 0)), _full((FH, D)), par, par, row],
        out_specs=(row, row, _full((8, 128)), _full((8, D)), _full((8, D))),
        compiler_params=_cp(("arbitrary",)))(x2, act, w_fo, g2, fg, tgt)


def _normmod_bwd(dsrc, w, xin, dres, g, sc, gate, o, after, name):
    S, D = xin.shape
    parts = list(dsrc) if isinstance(dsrc, (list, tuple)) else [dsrc]
    widths = [p.shape[1] for p in parts]
    K = sum(widths)
    tm = min(256, S)
    npart = len(parts)

    def body(*refs):
        ds_refs = refs[:npart]
        w_ref, x_ref, dr_ref, g_ref, sc_ref, gate_ref, o_ref, after_ref = refs[npart:npart + 8]
        dx_ref, do_ref, dsh_ref, dsc_ref, dg_ref, dgate_ref = refs[npart + 8:]
        i = pl.program_id(0)
        xv = x_ref[...]
        r = lax.rsqrt(jnp.mean(xv * xv, axis=-1, keepdims=True) + EPS)
        xn = xv * r
        dh_v, col = None, 0
        for ds_ref, wd in zip(ds_refs, widths):
            t = lax.dot_general(ds_ref[...], w_ref[:, col:col + wd], (((1,), (1,)), ((), ())),
                                preferred_element_type=F32)
            dh_v = t if dh_v is None else dh_v + t
            col += wd
        gv = g_ref[...]
        scale = 1.0 + sc_ref[...]
        dxn = dh_v * (gv * scale)
        dx = dr_ref[...] + r * (dxn - xn * jnp.mean(dxn * xn, axis=-1, keepdims=True))
        dx_ref[...] = dx
        do_ref[...] = (dx * gate_ref[...]).astype(BF16)

        @pl.when(i == 0)
        def _():
            dsh_ref[...] = jnp.zeros_like(dsh_ref)
            dsc_ref[...] = jnp.zeros_like(dsc_ref)
            dg_ref[...] = jnp.zeros_like(dg_ref)
            dgate_ref[...] = jnp.zeros_like(dgate_ref)

        hx = dh_v * xn
        dsh_ref[...] += _colsum8(dh_v)
        dsc_ref[...] += _colsum8(hx) * gv
        dg_ref[...] += _colsum8(hx) * scale
        dgate_ref[...] += _colsum8(dx * o_ref[...])

    row = pl.BlockSpec((tm, D), lambda i: (i, 0))
    par = _full((1, D))
    acc = jax.ShapeDtypeStruct((8, D), F32)
    return pl.pallas_call(
        body, name=name,
        out_shape=(jax.ShapeDtypeStruct((S, D), F32), jax.ShapeDtypeStruct((S, D), BF16), acc, acc, acc, acc),
        grid=(S // tm,),
        in_specs=[pl.BlockSpec((tm, wd), lambda i: (i, 0)) for wd in widths]
        + [_full((D, K)), row, row, par, par, par, row, pl.BlockSpec(memory_space=pl.ANY)],
        out_specs=(row, row, _full((8, D)), _full((8, D)), _full((8, D)), _full((8, D))),
        compiler_params=_cp(("arbitrary",)))(*parts, w, xin, dres, g, sc, gate, o, after)


def _me():
    return lax.axis_index("x"), lax.axis_index("y"), lax.axis_index("c")


def _allgather8(v, name, after=()):
    R, C = v.shape
    after = list(after)

    def body(v_ref, *rest):
        out_ref, send_sems, recv_sems, local_sem = rest[len(after):]
        x, y, c = _me()
        mine = pltpu.make_async_copy(v_ref, out_ref.at[4 * x + 2 * y + c], local_sem)
        mine.start()
        copies = []
        for k in range(1, N_DEV):
            fx, fy, fc = (k >> 2) & 1, (k >> 1) & 1, k & 1
            peer = (x ^ fx, y ^ fy, c ^ fc)
            copies.append(pltpu.make_async_remote_copy(
                src_ref=v_ref, dst_ref=out_ref.at[4 * x + 2 * y + c],
                send_sem=send_sems.at[k - 1], recv_sem=recv_sems.at[k - 1],
                device_id=peer, device_id_type=MESH))
        for cp in copies:
            cp.start()
        for k in range(1, N_DEV):
            fx, fy, fc = (k >> 2) & 1, (k >> 1) & 1, k & 1
            src_slot = 4 * (x ^ fx) + 2 * (y ^ fy) + (c ^ fc)
            pltpu.make_async_remote_copy(
                src_ref=v_ref, dst_ref=out_ref.at[src_slot],
                send_sem=send_sems.at[k - 1], recv_sem=recv_sems.at[k - 1],
                device_id=(x ^ fx, y ^ fy, c ^ fc), device_id_type=MESH).wait_recv()
        for cp in copies:
            cp.wait_send()
        mine.wait()

    return pl.pallas_call(
        body, name=name, out_shape=jax.ShapeDtypeStruct((N_DEV, R, C), v.dtype),
        in_specs=[pl.BlockSpec(memory_space=pltpu.VMEM)] + [pl.BlockSpec(memory_space=pl.ANY)] * len(after),
        out_specs=pl.BlockSpec(memory_space=pltpu.VMEM),
        scratch_shapes=[pltpu.SemaphoreType.DMA((N_DEV - 1,)), pltpu.SemaphoreType.DMA((N_DEV - 1,)),
                        pltpu.SemaphoreType.DMA],
        compiler_params=pltpu.CompilerParams(vmem_limit_bytes=VMEM_LIMIT))(v, *after)


def _swap_sibling(arrs):
    nw = len(arrs)

    def body(*refs):
        ins, outs = refs[:nw], refs[nw:2 * nw]
        send_sems, recv_sems = refs[2 * nw:]
        x, y, c = _me()
        copies = [pltpu.make_async_remote_copy(
            src_ref=ins[w], dst_ref=outs[w], send_sem=send_sems.at[w], recv_sem=recv_sems.at[w],
            device_id=(x, y, 1 - c), device_id_type=MESH) for w in range(nw)]
        for cp in copies:
            cp.start()
        for cp in copies:
            cp.wait_recv()
        for cp in copies:
            cp.wait_send()

    hbm = pl.BlockSpec(memory_space=pltpu.HBM)
    return pl.pallas_call(
        body, name="swap_sibling", out_shape=tuple(jax.ShapeDtypeStruct(a.shape, a.dtype) for a in arrs),
        in_specs=[hbm] * nw, out_specs=tuple([hbm] * nw),
        scratch_shapes=[pltpu.SemaphoreType.DMA((nw,)), pltpu.SemaphoreType.DMA((nw,))],
        compiler_params=pltpu.CompilerParams(vmem_limit_bytes=VMEM_LIMIT))(*arrs)


_HBM = pl.BlockSpec(memory_space=pltpu.HBM)
_SEM = pl.BlockSpec(memory_space=pltpu.SEMAPHORE)
_EFFECT = pltpu.SideEffectType.DATAFLOW_SIDE_EFFECTING
_N_PEER = N_CHIP - 1


def _chip_part(ref, axis, n, chip):
    start = pl.multiple_of(chip * n, 8)
    return ref.at[pl.ds(start, n), :] if axis == 0 else ref.at[:, pl.ds(start, n)]


def _gather_copy(k, src_ref, land_ref, send_sems, recv_sems, axis, arriving):
    x, y, c = _me()
    px, py = x ^ ((k >> 1) & 1), y ^ (k & 1)
    chip = 2 * px + py if arriving else 2 * x + y
    return pltpu.make_async_remote_copy(
        src_ref=src_ref, dst_ref=_chip_part(land_ref, axis, src_ref.shape[axis], chip),
        send_sem=send_sems.at[k - 1], recv_sem=recv_sems.at[k - 1], device_id=(px, py, c), device_id_type=MESH)


def _scatter_copy(k, grad_ref, land_ref, send_sems, recv_sems, axis):
    x, y, c = _me()
    px, py = x ^ ((k >> 1) & 1), y ^ (k & 1)
    return pltpu.make_async_remote_copy(
        src_ref=_chip_part(grad_ref, axis, grad_ref.shape[axis] // N_CHIP, 2 * px + py), dst_ref=land_ref.at[k - 1],
        send_sem=send_sems.at[k - 1], recv_sem=recv_sems.at[k - 1], device_id=(px, py, c), device_id_type=MESH)


def _own_copy(src_ref, land_ref, sends, axis):
    x, y, _ = _me()
    return pltpu.make_async_copy(src_ref, _chip_part(land_ref, axis, src_ref.shape[axis], 2 * x + y),
                                 sends.at[_N_PEER])


def _gather_start(shards, axes, after, name):
    nw = len(shards)
    lands = []
    for s, ax in zip(shards, axes):
        shp = list(s.shape)
        shp[ax] *= N_CHIP
        lands.append(lax.empty(tuple(shp), s.dtype))

    def body(*refs):
        srcs, zones = refs[:nw], refs[nw:2 * nw]
        sends, recvs = refs[2 * nw + 1:3 * nw + 1], refs[3 * nw + 1:4 * nw + 1]
        token = refs[-1]
        for w in range(nw):
            for k in range(1, N_CHIP):
                _gather_copy(k, srcs[w], zones[w], sends[w], recvs[w], axes[w], False).start()
        for w in range(nw):
            _own_copy(srcs[w], zones[w], sends[w], axes[w]).start()
        token[...] = jnp.zeros_like(token)

    outs = pl.pallas_call(
        body, name=name,
        out_shape=tuple([pltpu.SemaphoreType.DMA((_N_PEER + 1,))] * nw + [pltpu.SemaphoreType.DMA((_N_PEER,))] * nw
                        + [pltpu.HBM(a.shape, a.dtype) for a in list(shards) + list(lands)]
                        + [jax.ShapeDtypeStruct((8, 128), F32)]),
        in_specs=[_HBM] * (2 * nw) + [pl.BlockSpec(memory_space=pl.ANY)],
        out_specs=tuple([_SEM] * (2 * nw) + [_HBM] * (2 * nw) + [pl.BlockSpec(memory_space=pltpu.VMEM)]),
        input_output_aliases={i: 2 * nw + i for i in range(2 * nw)},
        compiler_params=pltpu.CompilerParams(has_side_effects=_EFFECT),
    )(*([pltpu.with_memory_space_constraint(a, pltpu.HBM) for a in list(shards) + list(lands)] + [after]))
    per_weight = [(outs[w], outs[nw + w], outs[2 * nw + w], outs[3 * nw + w]) for w in range(nw)]
    return per_weight, outs[-1]


def _gather_wait(state, axis, after, name):
    send_sems, recv_sems, shard, land = state

    after = list(after) if isinstance(after, (list, tuple)) else [after]

    def body(src_ref, land_ref, sends, recvs, *rest):
        for k in range(1, N_CHIP):
            _gather_copy(k, src_ref, land_ref, sends, recvs, axis, False).wait_send()
            _gather_copy(k, src_ref, land_ref, sends, recvs, axis, True).wait_recv()
        _own_copy(src_ref, land_ref, sends, axis).wait()

    return pl.pallas_call(
        body, name=name, out_shape=(pltpu.HBM(shard.shape, shard.dtype), pltpu.HBM(land.shape, land.dtype)),
        in_specs=[_HBM, _HBM, _SEM, _SEM] + [pl.BlockSpec(memory_space=pl.ANY)] * len(after), out_specs=(_HBM, _HBM),
        input_output_aliases={0: 0, 1: 1},
        compiler_params=pltpu.CompilerParams(has_side_effects=_EFFECT),
    )(shard, land, send_sems, recv_sems, *after)[1]


def _all8_copy(k, v_ref, land_ref, send_sems, recv_sems, arriving):
    x, y, c = _me()
    px, py, pc = x ^ ((k >> 2) & 1), y ^ ((k >> 1) & 1), c ^ (k & 1)
    slot = 4 * px + 2 * py + pc if arriving else 4 * x + 2 * y + c
    return pltpu.make_async_remote_copy(
        src_ref=v_ref, dst_ref=land_ref.at[slot], send_sem=send_sems.at[k - 1], recv_sem=recv_sems.at[k - 1],
        device_id=(px, py, pc), device_id_type=MESH)


def _all8_own(v_ref, land_ref, send_sems):
    x, y, c = _me()
    return pltpu.make_async_copy(v_ref, land_ref.at[4 * x + 2 * y + c], send_sems.at[N_DEV - 1])


def _all8_start(v, name):
    land = lax.empty((N_DEV,) + v.shape, v.dtype)

    def body(v_ref, land_ref, sends, recvs, v_thru, land_thru, token):
        for k in range(1, N_DEV):
            _all8_copy(k, v_ref, land_ref, sends, recvs, False).start()
        _all8_own(v_ref, land_ref, sends).start()
        token[...] = jnp.zeros_like(token)

    outs = pl.pallas_call(
        body, name=name,
        out_shape=(pltpu.SemaphoreType.DMA((N_DEV,)), pltpu.SemaphoreType.DMA((N_DEV - 1,)),
                   pltpu.HBM(v.shape, v.dtype), pltpu.HBM(land.shape, land.dtype),
                   jax.ShapeDtypeStruct((8, 128), F32)),
        in_specs=[_HBM, _HBM], out_specs=(_SEM, _SEM, _HBM, _HBM, pl.BlockSpec(memory_space=pltpu.VMEM)),
        input_output_aliases={0: 2, 1: 3},
        compiler_params=pltpu.CompilerParams(has_side_effects=_EFFECT),
    )(pltpu.with_memory_space_constraint(v, pltpu.HBM), pltpu.with_memory_space_constraint(land, pltpu.HBM))
    return outs[:4], outs[4]


def _all8_wait(state, after, name):
    send_sems, recv_sems, v, land = state

    def body(v_ref, land_ref, sends, recvs, after_ref, v_dead, got_ref):
        for k in range(1, N_DEV):
            _all8_copy(k, v_ref, land_ref, sends, recvs, False).wait_send()
            _all8_copy(k, v_ref, land_ref, sends, recvs, True).wait_recv()
        _all8_own(v_ref, land_ref, sends).wait()

    return pl.pallas_call(
        body, name=name, out_shape=(pltpu.HBM(v.shape, v.dtype), pltpu.HBM(land.shape, land.dtype)),
        in_specs=[_HBM, _HBM, _SEM, _SEM, pl.BlockSpec(memory_space=pl.ANY)], out_specs=(_HBM, _HBM),
        input_output_aliases={0: 0, 1: 1},
        compiler_params=pltpu.CompilerParams(has_side_effects=_EFFECT),
    )(v, land, send_sems, recv_sems, after)[1]


def _swap_copy(w, src_ref, land_ref, send_sems, recv_sems):
    x, y, c = _me()
    return pltpu.make_async_remote_copy(src_ref=src_ref, dst_ref=land_ref, send_sem=send_sems.at[w],
                                        recv_sem=recv_sems.at[w], device_id=(x, y, 1 - c), device_id_type=MESH)


def _swap_start(arrs, after, name):
    nw = len(arrs)
    lands = [lax.empty(a.shape, a.dtype) for a in arrs]

    def body(*refs):
        srcs, zones = refs[:nw], refs[nw:2 * nw]
        sends, recvs = refs[2 * nw + 1], refs[2 * nw + 2]
        for w in range(nw):
            _swap_copy(w, srcs[w], zones[w], sends, recvs).start()
        refs[-1][...] = jnp.zeros_like(refs[-1])

    sem = pltpu.SemaphoreType.DMA((nw,))
    outs = pl.pallas_call(
        body, name=name,
        out_shape=tuple([sem, sem] + [pltpu.HBM(a.shape, a.dtype) for a in list(arrs) + lands]
                        + [jax.ShapeDtypeStruct((8, 128), F32)]),
        in_specs=[_HBM] * (2 * nw) + [pl.BlockSpec(memory_space=pl.ANY)],
        out_specs=tuple([_SEM, _SEM] + [_HBM] * (2 * nw) + [pl.BlockSpec(memory_space=pltpu.VMEM)]),
        input_output_aliases={i: 2 + i for i in range(2 * nw)},
        compiler_params=pltpu.CompilerParams(has_side_effects=_EFFECT),
    )(*([pltpu.with_memory_space_constraint(a, pltpu.HBM) for a in list(arrs) + lands] + [after]))
    return (outs[0], outs[1], outs[2:2 + nw], outs[2 + nw:2 + 2 * nw]), outs[-1]


def _swap_wait(state, after, name):
    send_sems, recv_sems, arrs, lands = state
    nw = len(arrs)

    def body(*refs):
        srcs, zones = refs[:nw], refs[nw:2 * nw]
        sends, recvs = refs[2 * nw], refs[2 * nw + 1]
        for w in range(nw):
            cp = _swap_copy(w, srcs[w], zones[w], sends, recvs)
            cp.wait_send()
            cp.wait_recv()

    outs = pl.pallas_call(
        body, name=name, out_shape=tuple(pltpu.HBM(a.shape, a.dtype) for a in list(arrs) + list(lands)),
        in_specs=[_HBM] * (2 * nw) + [_SEM, _SEM, pl.BlockSpec(memory_space=pl.ANY)],
        out_specs=tuple([_HBM] * (2 * nw)),
        input_output_aliases={i: i for i in range(2 * nw)},
        compiler_params=pltpu.CompilerParams(has_side_effects=_EFFECT),
    )(*arrs, *lands, send_sems, recv_sems, after)
    return list(outs[:nw]), list(outs[nw:])


def _scatter_start(grad, axis, name):
    shp = list(grad.shape)
    shp[axis] //= N_CHIP
    land = lax.empty((_N_PEER,) + tuple(shp), grad.dtype)

    def body(grad_ref, land_ref, sends, recvs, grad_thru, land_thru, token):
        for k in range(1, N_CHIP):
            _scatter_copy(k, grad_ref, land_ref, sends, recvs, axis).start()
        token[...] = jnp.zeros_like(token)

    sem = pltpu.SemaphoreType.DMA((_N_PEER,))
    outs = pl.pallas_call(
        body, name=name,
        out_shape=(sem, sem, pltpu.HBM(grad.shape, grad.dtype), pltpu.HBM(land.shape, land.dtype),
                   jax.ShapeDtypeStruct((8, 128), F32)),
        in_specs=[_HBM, _HBM], out_specs=(_SEM, _SEM, _HBM, _HBM, pl.BlockSpec(memory_space=pltpu.VMEM)),
        input_output_aliases={0: 2, 1: 3},
        compiler_params=pltpu.CompilerParams(has_side_effects=_EFFECT),
    )(pltpu.with_memory_space_constraint(grad, pltpu.HBM), pltpu.with_memory_space_constraint(land, pltpu.HBM))
    return outs[:4], outs[4]


def _scatter_wait(state, axis, after, name):
    send_sems, recv_sems, grad, land = state

    def body(grad_ref, land_ref, sends, recvs, after_ref, grad_dead, got_ref):
        for k in range(1, N_CHIP):
            cp = _scatter_copy(k, grad_ref, land_ref, sends, recvs, axis)
            cp.wait_send()
            cp.wait_recv()

    return pl.pallas_call(
        body, name=name, out_shape=(pltpu.HBM(grad.shape, grad.dtype), pltpu.HBM(land.shape, land.dtype)),
        in_specs=[_HBM, _HBM, _SEM, _SEM, pl.BlockSpec(memory_space=pl.ANY)], out_specs=(_HBM, _HBM),
        input_output_aliases={0: 0, 1: 1},
        compiler_params=pltpu.CompilerParams(has_side_effects=_EFFECT),
    )(grad, land, send_sems, recv_sems, after)[1]


_C1 = 1.0 - B1 ** STEP
_C2 = 1.0 - B2 ** STEP


def _adam_math(w, g, m, v):
    m = B1 * m + (1.0 - B1) * g
    v = B2 * v + (1.0 - B2) * (g * g)
    delta = -LR * ((m / _C1) / (jnp.sqrt(v / _C2) + AEPS) + WD * w)
    return delta, m, v


def _adamw(w, m, v, groups, name):
    R, C = w.shape
    tr = R if R <= 256 else (128 if R % 128 == 0 else 176)
    assert R % tr == 0, (name, R)
    gparts = [p for grp in groups for p in grp]
    sizes = [len(grp) for grp in groups]
    ng = len(gparts)

    def body(*refs):
        w_ref, m_ref, v_ref = refs[:3]
        g_refs = list(refs[3:3 + ng])
        g_out, d_out, m_out, v_out = refs[3 + ng:]
        g = None
        for size in sizes:
            s = None
            for r in [g_refs.pop(0) for _ in range(size)]:
                terms = [r[q] for q in range(r.shape[0])] if len(r.shape) == 3 else [r[...]]
                for t in terms:
                    s = t.astype(F32) if s is None else s + t.astype(F32)
            g = s if g is None else g + s
        delta, mn, vn = _adam_math(w_ref[...], g, m_ref[...], v_ref[...])
        g_out[...] = g
        d_out[...] = delta
        m_out[...] = mn
        v_out[...] = vn

    blk = pl.BlockSpec((tr, C), lambda i: (i, 0))
    g_specs = [blk if p.ndim == 2 else pl.BlockSpec((p.shape[0], tr, C), lambda i: (0, i, 0)) for p in gparts]
    sds = jax.ShapeDtypeStruct((R, C), F32)
    return pl.pallas_call(
        body, name=name, out_shape=(sds, sds, sds, sds), grid=(R // tr,),
        in_specs=[blk, blk, blk] + g_specs, out_specs=(blk, blk, blk, blk),
        compiler_params=_cp(("parallel",)))(w, m, v, *gparts)


def _adamw_small(stack, names, wts, mom, var, sum_only, name):
    items, row = [], 0
    for n in names:
        shape = (KW, CW) if n == "conv_w" else wts[n].shape
        size = int(np.prod(shape))
        vec = len(shape) == 2 and shape[0] == 1 and n not in sum_only
        view = shape if vec else (-(-size // _PACK_COLS), _PACK_COLS)
        items.append((n, row, size, vec, view))
        row += _pack_rows(shape)
    upd = [it for it in items if it[0] not in sum_only]
    operands = [stack]
    for n, _, _, _, view in upd:
        operands += [d[n].reshape(view) for d in (wts, mom, var)]

    def grad(stack_ref, r0, nrows, ncols):
        g = stack_ref[0, r0:r0 + nrows, 0:ncols]
        for q in range(1, N_DEV):
            g = g + stack_ref[q, r0:r0 + nrows, 0:ncols]
        return g

    def body(*refs):
        stack_ref, ins, outs = refs[0], refs[1:1 + 3 * len(upd)], refs[1 + 3 * len(upd):]
        o = 0
        for idx, (n, r0, size, vec, view) in enumerate(upd):
            w_ref, m_ref, v_ref = ins[3 * idx:3 * idx + 3]
            g_out, d_out, m_out, v_out = outs[o:o + 4]
            o += 4
            if vec:
                pieces = [(j, j * _PACK_COLS, min((j + 1) * _PACK_COLS, size)) for j in range(-(-size // _PACK_COLS))]
            else:
                pieces = [(None, 0, _PACK_COLS)]
            for j, lo, hi in pieces:
                if vec:
                    g = grad(stack_ref, r0 + j, 1, hi - lo)
                    sl = (slice(None), slice(lo, hi))
                else:
                    g = grad(stack_ref, r0, view[0], _PACK_COLS)
                    sl = (slice(None), slice(None))
                delta, mn, vn = _adam_math(w_ref[sl], g, m_ref[sl], v_ref[sl])
                g_out[sl] = g
                d_out[sl] = delta
                m_out[sl] = mn
                v_out[sl] = vn
        for n, r0, size, vec, view in items:
            if n in sum_only:
                outs[o][...] = grad(stack_ref, r0, view[0], _PACK_COLS)
                o += 1

    out_shape = []
    for n, _, _, _, view in upd:
        out_shape += [jax.ShapeDtypeStruct(view, F32)] * 4
    out_shape += [jax.ShapeDtypeStruct(view, F32) for n, _, _, _, view in items if n in sum_only]
    vm = pl.BlockSpec(memory_space=pltpu.VMEM)
    res = pl.pallas_call(
        body, name=name, out_shape=tuple(out_shape), in_specs=[vm] * len(operands),
        out_specs=tuple([vm] * len(out_shape)),
        compiler_params=pltpu.CompilerParams(vmem_limit_bytes=VMEM_LIMIT))(*operands)
    updated = {n: tuple(r.reshape(wts[n].shape) for r in res[4 * i:4 * i + 4]) for i, (n, *_) in enumerate(upd)}
    sums = dict(zip([it[0] for it in items if it[0] in sum_only], res[4 * len(upd):]))
    return updated, sums


def _mod_shard(c_all, w_ada, b_ada_cols):
    n = w_ada.shape[1]
    tn = 512

    def body(c_ref, w_ref, b_ref, o_ref):
        cv = c_ref[...]
        ca = (cv * _sig(cv)).astype(BF16)
        o_ref[...] = jnp.dot(ca, w_ref[...].astype(BF16), preferred_element_type=F32) + b_ref[...]

    return pl.pallas_call(
        body, name="mod_shard", out_shape=jax.ShapeDtypeStruct((N_DEV, n), F32), grid=(n // tn,),
        in_specs=[_full((N_DEV, D_MODEL)), pl.BlockSpec((D_MODEL, tn), lambda j: (0, j)),
                  pl.BlockSpec((1, tn), lambda j: (0, j))],
        out_specs=pl.BlockSpec((N_DEV, tn), lambda j: (0, j)),
        compiler_params=_cp(("parallel",)))(c_all, w_ada, b_ada_cols)


def _ada_grad(c_all, dmod_cols):
    n = dmod_cols.shape[1]
    tn = 512

    def body(c_ref, d_ref, o_ref):
        cv = c_ref[...]
        ca = cv * _sig(cv)
        o_ref[...] = lax.dot_general(ca, d_ref[...], (((0,), (0,)), ((), ())),
                                     preferred_element_type=F32, precision=lax.Precision.HIGHEST)

    return pl.pallas_call(
        body, name="ada_grad", out_shape=jax.ShapeDtypeStruct((D_MODEL, n), F32), grid=(n // tn,),
        in_specs=[_full((N_DEV, D_MODEL)), pl.BlockSpec((N_DEV, tn), lambda j: (0, j))],
        out_specs=pl.BlockSpec((D_MODEL, tn), lambda j: (0, j)),
        compiler_params=_cp(("parallel",)))(c_all, dmod_cols)


def _ssm_tables(W):
    e_re, e_im, bb_re, bb_im = _ssm_prep(W["ssm_a_re"], W["ssm_a_im"], W["ssm_b_re"], W["ssm_b_im"], W["ssm_log_dt"])
    bb, cm = _block_diag_mats(bb_re, bb_im, W["ssm_c_re"], W["ssm_c_im"])
    bb16, cm16 = bb.astype(BF16), cm.astype(BF16)
    return (bb16, cm16, jnp.swapaxes(bb16, 1, 2), jnp.swapaxes(cm16, 1, 2),
            _scan_tables(e_re, e_im, False), _scan_tables(e_re, e_im, True))


def _device_step(x, mod, W, tables, tgt, getw, put, early):
    sh1, sc1, g1, sh2, sc2, g2 = [mod[:, i * D_MODEL:(i + 1) * D_MODEL] for i in range(6)]
    bb16, cm16, bbt16, cmt16, tab_f, tab_b = tables

    w_in = getw("w_in", [mod, *tables])
    h1, z = _in_proj(x, W["norm1_g"], sc1, sh1, w_in)
    yc, scv = _conv_fwd(z, W["conv_w"], W["conv_b"], W["conv_ln_g"], W["conv_ln_b"])
    xs, ys, yg = _ssm_fwd(z, bb16, cm16, W["ssm_d"], tab_f)
    w_cp, w_glu, w_out = getw("conv_proj", scv), getw("ssm_glu", yg), getw("w_out", yg)
    y_conv, zz, merged, o, x2, h2 = _mix_fwd(scv, yg, z, x, w_cp, w_glu, w_out, g1, W["norm2_g"], sc2, sh2)
    w_fi = getw("w_ffn_in", h2)
    f, act = _ffn_in_act(h2, w_fi)
    w_fo = getw("w_ffn_out", act)
    dx3, do2, loss8, dfg8, dg2_8 = _ffn_out_final(x2, act, w_fo, g2, W["final_g"], tgt)

    sm = {}
    tok = put("w_ffn_out", _matmul(act, do2, "tn", 1408, 1024, 2048, BF16, "mm_g_ffn_out"))
    df = _ffn_bwd(do2, w_fo, f, tok)
    tok = put("w_ffn_in", _matmul(h2, df, "tn", 1024, 1408, 2048, BF16, "mm_g_ffn_in"))
    dx2, do, dsh2, dsc2, dn2, dg1_8 = _normmod_bwd(df, w_fi, x2, dx3, W["norm2_g"], sc2, g1, o, tok, "d_h2_normmod2_bwd")
    tok = put("w_out", _matmul(merged, do, "tn", 1024, 1024, 4096, BF16, "mm_g_w_out"))
    dyconv, dgl, dzz = _mix_bwd(do, w_out, z, zz, y_conv, tok)
    tok = put("ssm_glu", _matmul(yg, dzz, "tn", 512, 1024, 4096, BF16, "mm_g_ssm_glu"))
    tok = put("conv_proj", _matmul(scv, dyconv, "tn", 512, 1024, 4096, BF16, "mm_g_conv_proj", after=tok))
    du, de16, dd8, dc_full, dbb_full = _ssm_bwd(dzz, w_glu, ys, z, xs, cmt16, bbt16, W["ssm_d"], tab_b, tok)
    dyc, dlg8, dlb8, dcb8 = _conv_bwd_ln(dyconv, w_cp, yc, W["conv_ln_g"], W["conv_ln_b"])
    dz_conv, dcw = _conv_bwd(dyc, z, W["conv_w"])

    s8 = lambda a: jnp.sum(a, axis=0, keepdims=True)
    de = de16.reshape(2, 8, NST).sum(1)
    de_re, de_im = de[0].reshape(G, P), de[1].reshape(G, P)
    dc_re = _diag_blocks(dc_full, False)
    dc_im = -_diag_blocks(dc_full, True)
    dbb_re = jnp.swapaxes(_diag_blocks(dbb_full, False), 1, 2)
    dbb_im = jnp.swapaxes(_diag_blocks(dbb_full, True), 1, 2)
    _, vjp = jax.vjp(_ssm_prep, W["ssm_a_re"], W["ssm_a_im"], W["ssm_b_re"], W["ssm_b_im"], W["ssm_log_dt"])
    sm["ssm_a_re"], sm["ssm_a_im"], sm["ssm_b_re"], sm["ssm_b_im"], sm["ssm_log_dt"] = vjp((de_re, de_im, dbb_re, dbb_im))
    sm["ssm_c_re"], sm["ssm_c_im"] = dc_re, dc_im
    sm["ssm_d"] = s8(dd8)
    sm["norm2_g"] = s8(dn2)
    sm["conv_b"], sm["conv_ln_g"], sm["conv_ln_b"] = s8(dcb8), s8(dlg8), s8(dlb8)
    sm["conv_w"] = dcw.reshape(KW, 8, CW).sum(1)
    sm["final_g"] = s8(dfg8)
    tok = early(sm)

    dz = [dz_conv, du, dgl]
    tok = put("w_in", _matmul(h1, dz, "tn", 1024, 512, 4096, BF16, "mm_g_w_in", after=tok))
    dx, _, dsh1, dsc1, dn1, _ = _normmod_bwd(dz, w_in, x, dx2, W["norm1_g"], sc1, g1, o, tok, "d_h1_normmod1_bwd")
    dmod = jnp.concatenate([s8(dsh1), s8(dsc1), s8(dg1_8), s8(dsh2), s8(dsc2), s8(dg2_8)], axis=1)
    return loss8, dx, s8(dn1), dmod


_BIG = ("w_in", "conv_proj", "ssm_glu", "w_out", "w_ffn_in", "w_ffn_out")
_BIG_AXIS = {"w_in": 1, "conv_proj": 1, "ssm_glu": 1, "w_out": 0, "w_ffn_in": 1, "w_ffn_out": 0}
_EARLY = ("conv_w", "conv_b", "conv_ln_g", "conv_ln_b", "ssm_a_re", "ssm_a_im", "ssm_b_re", "ssm_b_im", "ssm_c_re",
          "ssm_c_im", "ssm_d", "ssm_log_dt", "norm2_g", "final_g")
_LATE = ("norm1_g", "b_ada")
_ORDER = ("w_ada", "b_ada", "norm1_g", "w_in", "conv_w", "conv_b", "conv_ln_g", "conv_ln_b", "conv_proj",
          "ssm_a_re", "ssm_a_im", "ssm_b_re", "ssm_b_im", "ssm_c_re", "ssm_c_im", "ssm_d", "ssm_log_dt", "ssm_glu",
          "w_out", "norm2_g", "w_ffn_in", "w_ffn_out", "final_g")
_PACK_COLS = 1024


def _pack_rows(shape):
    return -(-int(np.prod(shape)) // (8 * _PACK_COLS)) * 8


def _pack(arrs):
    parts = []
    for a in arrs:
        flat = a.reshape(-1)
        n = _pack_rows(a.shape)
        parts.append(jnp.pad(flat, (0, n * _PACK_COLS - flat.shape[0])).reshape(n, _PACK_COLS))
    return jnp.concatenate(parts, 0)


def kernel(x, c, w_ada, b_ada, norm1_g, w_in, conv_w, conv_b, conv_ln_g, conv_ln_b, conv_proj, ssm_a_re, ssm_a_im, ssm_b_re, ssm_b_im, ssm_c_re, ssm_c_im, ssm_d, ssm_log_dt, ssm_glu, w_out, norm2_g, w_ffn_in, w_ffn_out, final_g, loss_target, m_w_ada, m_b_ada, m_norm1_g, m_w_in, m_conv_w, m_conv_b, m_conv_ln_g, m_conv_ln_b, m_conv_proj, m_ssm_a_re, m_ssm_a_im, m_ssm_b_re, m_ssm_b_im, m_ssm_c_re, m_ssm_c_im, m_ssm_d, m_ssm_log_dt, m_ssm_glu, m_w_out, m_norm2_g, m_w_ffn_in, m_w_ffn_out, m_final_g, v_w_ada, v_b_ada, v_norm1_g, v_w_in, v_conv_w, v_conv_b, v_conv_ln_g, v_conv_ln_b, v_conv_proj, v_ssm_a_re, v_ssm_a_im, v_ssm_b_re, v_ssm_b_im, v_ssm_c_re, v_ssm_c_im, v_ssm_d, v_ssm_log_dt, v_ssm_glu, v_w_out, v_norm2_g, v_w_ffn_in, v_w_ffn_out, v_final_g):
    given = dict(locals())
    mx, my, mc = _me()
    chip = 2 * mx + my
    dev = 4 * mx + 2 * my + mc
    def canon(a):
        return a.reshape(1, -1) if a.ndim <= 2 else a[0]

    wts = {n: canon(given[n]) for n in _ORDER}
    mom = {n: canon(given["m_" + n]) for n in _ORDER}
    var = {n: canon(given["v_" + n]) for n in _ORDER}

    c_all = _allgather8(jnp.broadcast_to(c, (8, D_MODEL)), "gather_c")[:, 0, :]
    n_ada = wts["w_ada"].shape[1]
    b_cols = lax.dynamic_slice(wts["b_ada"], (0, chip * n_ada), (1, n_ada))
    mod_cols = _mod_shard(c_all, wts["w_ada"], b_cols)
    mods = _allgather8(mod_cols, "gather_mod")
    mod = jnp.concatenate([lax.dynamic_index_in_dim(mods[2 * q], dev, 0, keepdims=True) for q in range(N_CHIP)], axis=1)
    W = {n: wts[n] for n in _ORDER if n not in _BIG}
    conv_w_full = _allgather8(jnp.pad(wts["conv_w"], ((0, 1), (0, 0))), "gather_conv_w", after=[c_all])
    W["conv_w"] = jnp.concatenate([conv_w_full[2 * q, :KW] for q in range(N_CHIP)], axis=1)

    state_in, token = _gather_start([wts["w_in"].astype(BF16)], [_BIG_AXIS["w_in"]],
                                    mod + W["conv_w"][0:1, 0:1], "gather_start_w_in")
    W["ssm_log_dt"] = wts["ssm_log_dt"] + token[0:1, 0:1]
    W["ssm_c_re"] = wts["ssm_c_re"] + token[0, 0]
    tables = _ssm_tables(W)
    w_in_full = _gather_wait(state_in[0], _BIG_AXIS["w_in"], [mod, *tables], "gather_wait_w_in")
    rest = [n for n in _BIG if n != "w_in"]
    gstate, token = _gather_start([wts[n].astype(BF16) for n in rest], [_BIG_AXIS[n] for n in rest], w_in_full,
                                  "gather_start_rest")
    gstate = dict(zip(rest, gstate))
    mod = mod + token[0:1, 0:1]

    def getw(n, after):
        if n == "w_in":
            return w_in_full
        return _gather_wait(gstate[n], _BIG_AXIS[n], after, "gather_wait_" + n)

    sstate, own, estate = {}, {}, []

    def put(n, g):
        ax = _BIG_AXIS[n]
        k = g.shape[ax] // N_CHIP
        own[n] = lax.dynamic_slice_in_dim(g, chip * k, k, axis=ax)
        sstate[n], tok = _scatter_start(g, ax, "scatter_start_" + n)
        return tok

    first5 = [n for n in _BIG if n != "w_in"]

    def early(sm):
        state, tok = _all8_start(_pack([sm[n] for n in _EARLY]), "small_start")
        estate.append(state)
        recv5 = [_scatter_wait(sstate[n], _BIG_AXIS[n], tok, "scatter_wait_" + n) for n in first5]
        held = [a for n, r in zip(first5, recv5) for a in (own[n], r)]
        state, tok = _swap_start(held, tok, "swap_start")
        estate.append(state)
        return tok

    loss8, dx, dn1, dmod = _device_step(x[0], mod, W, tables, loss_target[0], getw, put, early)

    held5, sib5 = _swap_wait(estate[1], dx, "swap_wait")
    outs = {}
    for i, n in enumerate(first5):
        outs[n] = _adamw(wts[n], mom[n], var[n], [held5[2 * i:2 * i + 2], sib5[2 * i:2 * i + 2]], "adamw_" + n)
    allp = _all8_wait(estate[0], dx, "small_wait")

    late = _allgather8(_pack([dn1, dmod, loss8]), "gather_late", after=[outs[n][1] for n in first5])
    n_late = _pack_rows((D_MODEL,)) + _pack_rows((6 * D_MODEL,))
    loss = jnp.sum(late[:, n_late:, :])
    late = late[:, :n_late, :]
    held_in = [own["w_in"], _scatter_wait(sstate["w_in"], _BIG_AXIS["w_in"], late, "scatter_wait_w_in")]
    sib_in = _swap_sibling(held_in)
    outs["w_in"] = _adamw(wts["w_in"], mom["w_in"], var["w_in"], [held_in, sib_in], "adamw_w_in")

    r1 = _pack_rows((D_MODEL,))
    dmod_all = late[:, r1:, :].reshape(N_DEV, -1)[:, :6 * D_MODEL]
    dmod_cols = lax.dynamic_slice(dmod_all, (0, chip * n_ada), (N_DEV, n_ada))
    g_ada = _ada_grad(c_all, dmod_cols)
    outs["w_ada"] = _adamw(wts["w_ada"], mom["w_ada"], var["w_ada"], [[g_ada]], "adamw_w_ada")

    upd, sums = _adamw_small(allp, _EARLY, wts, mom, var, ("conv_w",), "adamw_small")
    outs.update(upd)
    upd, _ = _adamw_small(late, _LATE, wts, mom, var, (), "adamw_late")
    outs.update(upd)
    g_cw_full = sums["conv_w"].reshape(-1)[:KW * CW].reshape(KW, CW)
    g_cw = lax.dynamic_slice(g_cw_full, (0, chip * (CW // N_CHIP)), (KW, CW // N_CHIP))
    pad = lambda a: jnp.pad(a, ((0, 1), (0, 0)))
    r_cw = _adamw(pad(wts["conv_w"]), pad(mom["conv_w"]), pad(var["conv_w"]), [[pad(g_cw)]], "adamw_conv_w")
    outs["conv_w"] = tuple(r[:KW] for r in r_cw)

    def shaped(n, a):
        return a.reshape(given[n].shape)

    result = [loss, dx[None]]
    for q in range(4):
        result += [shaped(n, outs[n][q]) for n in _ORDER]
    return tuple(result)
```

```python
import math

import jax
import jax.numpy as jnp
import numpy as np
from jax import lax
from jax.experimental import pallas as pl
from jax.experimental.pallas import tpu as pltpu

F32 = jnp.float32
BF16 = jnp.bfloat16
EPS = 1e-6
D_MODEL = 1024
CW = 512
KW = 31
HALO = 32
G, P, H = 32, 64, 16
NST = G * P
FH = 2816
N_DEV = 8
N_CHIP = 4
VMEM_LIMIT = 56 * 1024 * 1024
LR, B1, B2, AEPS, WD, STEP = 0.001, 0.9, 0.999, 1e-08, 0.01, 10
MESH = pl.DeviceIdType.MESH


def _cp(sem=None):
    return pltpu.CompilerParams(dimension_semantics=sem, vmem_limit_bytes=VMEM_LIMIT)


def _sig(x):
    return jax.nn.sigmoid(x)


def _full(shape):
    return pl.BlockSpec(shape, lambda *_: (0,) * len(shape))


def _resident(shape):
    return pl.BlockSpec(shape, lambda *_: (0,) * len(shape), pipeline_mode=pl.Buffered(1))


def _colsum8(v):
    t, c = v.shape
    return jnp.sum(v.reshape(t // 8, 8, c), axis=0)


def _matmul(a, b, mode, tm, tn, tk, out_dtype, name, after=None, n_outer=False, m_cols=None):
    m0 = 0
    b_parts = list(b) if isinstance(b, (list, tuple)) else [b]
    if mode == "nn":
        (M, K), N = a.shape, b.shape[1]
    elif mode == "nt":
        (M, K), N = a.shape, b.shape[0]
    else:
        (K, M), N = a.shape, sum(p.shape[1] for p in b_parts)
        if m_cols is not None:
            m0, M = m_cols
    tm, tn, tk = min(tm, M), min(tn, N), min(tk, K)
    assert M % tm == 0 and N % tn == 0 and K % tk == 0 and m0 % tm == 0, (name, M, N, K, tm, tn, tk)
    assert len(b_parts) == 1 or (mode == "tn" and all(p.shape[1] % tn == 0 for p in b_parts)), name
    nk = K // tk
    mb = m0 // tm
    counts = [p.shape[1] // tn for p in b_parts] if mode == "tn" else [N // tn]
    starts = [sum(counts[:p]) for p in range(len(counts))]

    def ij(fn):
        return (lambda j, i, k: fn(i, j, k)) if n_outer else fn

    if mode == "nn":
        a_spec = pl.BlockSpec((tm, tk), ij(lambda i, j, k: (i, k)))
        b_spec = pl.BlockSpec((tk, tn), ij(lambda i, j, k: (k, j)))
        dims = (((1,), (0,)), ((), ()))
    elif mode == "nt":
        a_spec = pl.BlockSpec((tm, tk), ij(lambda i, j, k: (i, k)))
        b_spec = pl.BlockSpec((tn, tk), ij(lambda i, j, k: (j, k)))
        dims = (((1,), (1,)), ((), ()))
    else:
        a_spec = pl.BlockSpec((tk, tm), ij(lambda i, j, k: (k, i + mb)))
        dims = (((0,), (0,)), ((), ()))
    if mode == "tn":
        b_specs = [pl.BlockSpec((tk, tn), ij(lambda i, j, k, s=s, n=n: (k, jnp.clip(j - s, 0, n - 1))))
                   for s, n in zip(starts, counts)]
    else:
        b_specs = [b_spec]
    nb = len(b_parts)

    def body(a_ref, *rest):
        b_refs = rest[:nb]
        o_ref, acc_ref = rest[-2:]
        j = pl.program_id(0 if n_outer else 1)
        k = pl.program_id(2)

        def compute(b_ref):
            part = lax.dot_general(a_ref[...].astype(BF16), b_ref[...].astype(BF16), dims,
                                   preferred_element_type=F32)
            if nk == 1:
                o_ref[...] = part.astype(out_dtype)
            else:
                @pl.when(k == 0)
                def _():
                    acc_ref[...] = part

                @pl.when(k > 0)
                def _():
                    acc_ref[...] += part

                @pl.when(k == nk - 1)
                def _():
                    o_ref[...] = acc_ref[...].astype(out_dtype)

        if nb == 1:
            compute(b_refs[0])
        else:
            for p in range(nb):
                pl.when(jnp.logical_and(j >= starts[p], j < starts[p] + counts[p]))(
                    lambda b_ref=b_refs[p]: compute(b_ref))

    return pl.pallas_call(
        body, name=name,
        out_shape=jax.ShapeDtypeStruct((M, N), out_dtype),
        grid=(N // tn, M // tm, nk) if n_outer else (M // tm, N // tn, nk),
        in_specs=[a_spec] + b_specs + ([] if after is None else [pl.BlockSpec(memory_space=pl.ANY)]),
        out_specs=pl.BlockSpec((tm, tn), ij(lambda i, j, k: (i, j))),
        scratch_shapes=[pltpu.VMEM((tm, tn) if nk > 1 else (8, 128), F32)],
        compiler_params=_cp(("parallel", "parallel", "arbitrary")),
    )(*([a] + b_parts + ([] if after is None else [after])))


def _row_tile(S):
    return min(512, S)


def _in_proj(x, g, sc, sh, w_in):
    S, D = x.shape
    N = w_in.shape[1]
    tm = min(512, S)

    def body(x_ref, g_ref, sc_ref, sh_ref, w_ref, h_ref, z_ref):
        xv = x_ref[...]
        r = lax.rsqrt(jnp.mean(xv * xv, axis=-1, keepdims=True) + EPS)
        h = (xv * r * (g_ref[...] * (1.0 + sc_ref[...])) + sh_ref[...]).astype(BF16)
        h_ref[...] = h
        z_ref[...] = jnp.dot(h, w_ref[...], preferred_element_type=F32).astype(BF16)

    row = pl.BlockSpec((tm, D), lambda i: (i, 0))
    par = _full((1, D))
    return pl.pallas_call(
        body, name="in_proj",
        out_shape=(jax.ShapeDtypeStruct((S, D), BF16), jax.ShapeDtypeStruct((S, N), BF16)), grid=(S // tm,),
        in_specs=[row, par, par, par, _resident((D, N))], out_specs=(row, pl.BlockSpec((tm, N), lambda i: (i, 0))),
        compiler_params=_cp(("parallel",)))(x, g, sc, sh, w_in)


def _fill_shifted(buf_ref, sh_ref):
    n = buf_ref.shape[0] - 8
    for s in range(1, 8):
        sh_ref[s, 0:n, :] = buf_ref[s:s + n, :]


def _window(buf_ref, sh_ref, off, n):
    s = off % 8
    return buf_ref[off:off + n, :] if s == 0 else sh_ref[s, off - s:off - s + n, :]


def _conv_fwd(z, conv_w, conv_b, ln_g, ln_b):
    S = z.shape[0]
    tm = min(128, S)
    sub = 32
    hb = tm // HALO

    def body(a_ref, g_ref, ha_ref, hg_ref, w_ref, b_ref, lg_ref, lb_ref, yc_ref, s_ref, ug_ref, sh_ref):
        i = pl.program_id(0)
        halo = ha_ref[...].astype(F32) * _sig(hg_ref[...].astype(F32))
        ug_ref[0:HALO, :] = jnp.where(i == 0, 0.0, halo)
        ug_ref[HALO:, :] = a_ref[...].astype(F32) * _sig(g_ref[...].astype(F32))
        _fill_shifted(ug_ref, sh_ref)
        for rb in range(tm // sub):
            acc = jnp.zeros((sub, CW), F32) + b_ref[...]
            for k in range(KW):
                off = rb * sub + HALO - (KW - 1) + k
                acc = acc + w_ref[k:k + 1, :] * _window(ug_ref, sh_ref, off, sub)
            yc_ref[rb * sub:(rb + 1) * sub, :] = acc
            mu = jnp.mean(acc, axis=-1, keepdims=True)
            cen = acc - mu
            rstd = lax.rsqrt(jnp.mean(cen * cen, axis=-1, keepdims=True) + EPS)
            ln = cen * rstd * lg_ref[...] + lb_ref[...]
            s_ref[rb * sub:(rb + 1) * sub, :] = (ln * _sig(ln)).astype(BF16)

    prev = lambda i: (jnp.maximum(i * hb - 1, 0), 0)
    return pl.pallas_call(
        body, name="conv_fwd",
        out_shape=(jax.ShapeDtypeStruct((S, CW), F32), jax.ShapeDtypeStruct((S, CW), BF16)),
        grid=(S // tm,),
        in_specs=[pl.BlockSpec((tm, CW), lambda i: (i, 0)), pl.BlockSpec((tm, CW), lambda i: (i, 1)),
                  pl.BlockSpec((HALO, CW), prev), pl.BlockSpec((HALO, CW), lambda i: (jnp.maximum(i * hb - 1, 0), 1)),
                  _full((KW, CW)), _full((1, CW)), _full((1, CW)), _full((1, CW))],
        out_specs=(pl.BlockSpec((tm, CW), lambda i: (i, 0)), pl.BlockSpec((tm, CW), lambda i: (i, 0))),
        scratch_shapes=[pltpu.VMEM((tm + HALO, CW), F32), pltpu.VMEM((8, tm + HALO, CW), F32)],
        compiler_params=_cp(("parallel",)))(z, z, z, z, conv_w, conv_b, ln_g, ln_b)


def _conv_bwd_ln(dyconv, w_cp, yc, ln_g, ln_b):
    S = yc.shape[0]
    tm = _row_tile(S)

    def body(dy_ref, w_ref, yc_ref, lg_ref, lb_ref, dyc_ref, dlg_ref, dlb_ref, dcb_ref):
        i = pl.program_id(0)
        dsc = lax.dot_general(dy_ref[...], w_ref[...], (((1,), (1,)), ((), ())), preferred_element_type=F32)
        yc_v = yc_ref[...]
        mu = jnp.mean(yc_v, axis=-1, keepdims=True)
        cen = yc_v - mu
        rstd = lax.rsqrt(jnp.mean(cen * cen, axis=-1, keepdims=True) + EPS)
        yn = cen * rstd
        ln = yn * lg_ref[...] + lb_ref[...]
        sl = _sig(ln)
        dln = dsc * (sl * (1.0 + ln * (1.0 - sl)))
        dyn = dln * lg_ref[...]
        dyc = rstd * (dyn - jnp.mean(dyn, axis=-1, keepdims=True)
                      - yn * jnp.mean(dyn * yn, axis=-1, keepdims=True))
        dyc_ref[...] = dyc

        @pl.when(i == 0)
        def _():
            dlg_ref[...] = jnp.zeros_like(dlg_ref)
            dlb_ref[...] = jnp.zeros_like(dlb_ref)
            dcb_ref[...] = jnp.zeros_like(dcb_ref)

        dlg_ref[...] += _colsum8(dln * yn)
        dlb_ref[...] += _colsum8(dln)
        dcb_ref[...] += _colsum8(dyc)

    row = pl.BlockSpec((tm, CW), lambda i: (i, 0))
    acc = jax.ShapeDtypeStruct((8, CW), F32)
    return pl.pallas_call(
        body, name="conv_bwd_ln",
        out_shape=(jax.ShapeDtypeStruct((S, CW), F32), acc, acc, acc), grid=(S // tm,),
        in_specs=[pl.BlockSpec((tm, D_MODEL), lambda i: (i, 0)), _full((CW, D_MODEL)), row, _full((1, CW)),
                  _full((1, CW))],
        out_specs=(row, _full((8, CW)), _full((8, CW)), _full((8, CW))),
        compiler_params=_cp(("arbitrary",)))(dyconv, w_cp, yc, ln_g, ln_b)


def _conv_bwd(dyc, z, conv_w):
    S = z.shape[0]
    tm = min(128, S)
    sub = 32
    hb = tm // HALO
    nt = S // tm

    def body(d_ref, dn_ref, a_ref, g_ref, ha_ref, hg_ref, w_ref, dz_ref, dw_ref, ug_ref, dy_ref, ugs_ref, dys_ref):
        i = pl.program_id(0)
        halo = ha_ref[...].astype(F32) * _sig(hg_ref[...].astype(F32))
        ug_ref[0:HALO, :] = jnp.where(i == 0, 0.0, halo)
        a = a_ref[...].astype(F32)
        sg = _sig(g_ref[...].astype(F32))
        ug_ref[HALO:, :] = a * sg
        dy_ref[0:tm, :] = d_ref[...]
        dy_ref[tm:, :] = jnp.where(i == nt - 1, 0.0, dn_ref[...])
        _fill_shifted(ug_ref, ugs_ref)
        _fill_shifted(dy_ref, dys_ref)

        @pl.when(i == 0)
        def _():
            dw_ref[...] = jnp.zeros_like(dw_ref)

        for rb in range(tm // sub):
            r0 = rb * sub
            acc = jnp.zeros((sub, CW), F32)
            dyc_b = dy_ref[r0:r0 + sub, :]
            for k in range(KW):
                up = r0 + (KW - 1) - k
                acc = acc + w_ref[k:k + 1, :] * _window(dy_ref, dys_ref, up, sub)
                off = r0 + HALO - (KW - 1) + k
                dw_ref[k * 8:(k + 1) * 8, :] += _colsum8(dyc_b * _window(ug_ref, ugs_ref, off, sub))
            a_b = a[r0:r0 + sub, :]
            sg_b = sg[r0:r0 + sub, :]
            dz_ref[r0:r0 + sub, 0:CW] = (acc * sg_b).astype(BF16)
            dz_ref[r0:r0 + sub, CW:2 * CW] = (acc * a_b * sg_b * (1.0 - sg_b)).astype(BF16)

    return pl.pallas_call(
        body, name="conv_bwd",
        out_shape=(jax.ShapeDtypeStruct((S, 2 * CW), BF16), jax.ShapeDtypeStruct((KW * 8, CW), F32)),
        grid=(nt,),
        in_specs=[pl.BlockSpec((tm, CW), lambda i: (i, 0)),
                  pl.BlockSpec((HALO, CW), lambda i: (jnp.minimum((i + 1) * hb, nt * hb - 1), 0)),
                  pl.BlockSpec((tm, CW), lambda i: (i, 0)), pl.BlockSpec((tm, CW), lambda i: (i, 1)),
                  pl.BlockSpec((HALO, CW), lambda i: (jnp.maximum(i * hb - 1, 0), 0)),
                  pl.BlockSpec((HALO, CW), lambda i: (jnp.maximum(i * hb - 1, 0), 1)),
                  _full((KW, CW))],
        out_specs=(pl.BlockSpec((tm, 2 * CW), lambda i: (i, 0)), _full((KW * 8, CW))),
        scratch_shapes=[pltpu.VMEM((tm + HALO, CW), F32), pltpu.VMEM((tm + HALO, CW), F32),
                        pltpu.VMEM((8, tm + HALO, CW), F32), pltpu.VMEM((8, tm + HALO, CW), F32)],
        compiler_params=_cp(("arbitrary",)))(dyc, dyc, z, z, z, z, conv_w)


_GELU_C = math.sqrt(2.0 / math.pi)


def _gelu(x):
    return 0.5 * x * (1.0 + jnp.tanh(_GELU_C * (x + 0.044715 * x * x * x)))


def _gelu_grad(x):
    t = jnp.tanh(_GELU_C * (x + 0.044715 * x * x * x))
    return 0.5 * (1.0 + t) + 0.5 * x * (1.0 - t * t) * (_GELU_C * (1.0 + 3 * 0.044715 * x * x))


_NCL = 4
_UC = CW // _NCL
_LW = NST // _NCL
_CS = 2 * _LW


def _ssm_fwd(z, bb, cm, d, tab):
    S = z.shape[0]
    tm = min(512, S)

    def body(u_ref, bb_ref, cm_ref, d_ref, t_ref, x_ref, ys_ref, yg_ref, car_ref):
        i = pl.program_id(0)

        @pl.when(i == 0)
        def _():
            car_ref[...] = jnp.zeros_like(car_ref)

        u16 = u_ref[...]
        u = u16.astype(F32)
        for c in range(_NCL):
            lre = pl.ds(c * _CS, _LW)
            lim = pl.ds(c * _CS + _LW, _LW)
            tl = pl.ds(c * _LW, _LW)
            x_ref[:, c * _CS:(c + 1) * _CS] = jnp.dot(u16[:, c * _UC:(c + 1) * _UC], bb_ref[c],
                                                      preferred_element_type=F32)

            def blk(j, car):
                cr, ci = car
                rows = pl.ds(pl.multiple_of(j * 8, 8), 8)
                r = x_ref[rows, lre]
                im = x_ref[rows, lim]
                for lvl, s in enumerate((1, 2, 4)):
                    mr = t_ref[16 * lvl:16 * lvl + 8, tl]
                    mi = t_ref[16 * lvl + 8:16 * lvl + 16, tl]
                    sr = pltpu.roll(r, s, 0)
                    si = pltpu.roll(im, s, 0)
                    r, im = r + (mr * sr - mi * si), im + (mr * si + mi * sr)
                pr = t_ref[48:56, tl]
                pi_ = t_ref[56:64, tl]
                r, im = r + (pr * cr - pi_ * ci), im + (pr * ci + pi_ * cr)
                x_ref[rows, lre] = r
                x_ref[rows, lim] = im
                return (jnp.broadcast_to(r[7:8, :], (8, _LW)), jnp.broadcast_to(im[7:8, :], (8, _LW)))

            cr, ci = lax.fori_loop(0, tm // 8, blk, (car_ref[:, lre], car_ref[:, lim]))
            car_ref[:, lre] = cr
            car_ref[:, lim] = ci
            cols = slice(c * _UC, (c + 1) * _UC)
            ys = jnp.dot(x_ref[:, c * _CS:(c + 1) * _CS].astype(BF16), cm_ref[c], preferred_element_type=F32)
            ys = ys + d_ref[:, cols] * u[:, cols]
            ys_ref[:, cols] = ys
            yg_ref[:, cols] = _gelu(ys).astype(BF16)

    return pl.pallas_call(
        body, name="ssm_fwd",
        out_shape=(jax.ShapeDtypeStruct((S, 2 * NST), F32), jax.ShapeDtypeStruct((S, CW), F32),
                   jax.ShapeDtypeStruct((S, CW), BF16)),
        grid=(S // tm,),
        in_specs=[pl.BlockSpec((tm, CW), lambda i: (i, 2)), _full((_NCL, _UC, _CS)), _full((_NCL, _CS, _UC)),
                  _full((1, CW)), _full((64, NST))],
        out_specs=(pl.BlockSpec((tm, 2 * NST), lambda i: (i, 0)), pl.BlockSpec((tm, CW), lambda i: (i, 0)),
                   pl.BlockSpec((tm, CW), lambda i: (i, 0))),
        scratch_shapes=[pltpu.VMEM((8, 2 * NST), F32)],
        compiler_params=_cp(("arbitrary",)))(z, bb, cm, d, tab)


def _ssm_bwd(dzz, w_glu, ys, z, xs, cmt, bbt, d, tab, after):
    S = z.shape[0]
    tm = min(512, S)
    nt = S // tm
    tdims = (((0,), (0,)), ((), ()))

    def body(dzz_ref, wglu_ref, ys_ref, u_ref, x_ref, cmt_ref, bbt_ref, d_ref, t_ref, after_ref,
             du_ref, de_ref, dd_ref, dc_hbm, dbb_hbm, car_ref, lam_ref, dc_ref, dbb_ref):
        i = pl.program_id(0)

        @pl.when(i == 0)
        def _():
            car_ref[...] = jnp.zeros_like(car_ref)
            de_ref[...] = jnp.zeros_like(de_ref)
            dd_ref[...] = jnp.zeros_like(dd_ref)
            dc_ref[...] = jnp.zeros_like(dc_ref)
            dbb_ref[...] = jnp.zeros_like(dbb_ref)

        u16 = u_ref[...]
        u = u16.astype(F32)
        dyg = lax.dot_general(dzz_ref[...], wglu_ref[...], (((1,), (1,)), ((), ())), preferred_element_type=F32)
        dys = dyg * _gelu_grad(ys_ref[...])
        dys16 = dys.astype(BF16)
        dd_ref[...] += _colsum8(dys * u)
        row = lax.broadcasted_iota(jnp.int32, (8, _LW), 0)
        for c in range(_NCL):
            lre = pl.ds(c * _CS, _LW)
            lim = pl.ds(c * _CS + _LW, _LW)
            tl = pl.ds(c * _LW, _LW)
            cols = slice(c * _UC, (c + 1) * _UC)
            span = slice(c * _CS, (c + 1) * _CS)
            dc_ref[cols, :] += lax.dot_general(dys16[:, cols], x_ref[:, span].astype(BF16), tdims,
                                               preferred_element_type=F32)
            lam_ref[...] = jnp.dot(dys16[:, cols], cmt_ref[c], preferred_element_type=F32)

            def blk(jj, car):
                cr, ci, ar, ai = car
                j = tm // 8 - 1 - jj
                rows = pl.ds(pl.multiple_of(j * 8, 8), 8)
                r = lam_ref[rows, 0:_LW]
                im = lam_ref[rows, _LW:_CS]
                for lvl, s in enumerate((1, 2, 4)):
                    mr = t_ref[16 * lvl:16 * lvl + 8, tl]
                    mi = t_ref[16 * lvl + 8:16 * lvl + 16, tl]
                    sr = pltpu.roll(r, 8 - s, 0)
                    si = pltpu.roll(im, 8 - s, 0)
                    r, im = r + (mr * sr - mi * si), im + (mr * si + mi * sr)
                pr = t_ref[48:56, tl]
                pi_ = t_ref[56:64, tl]
                r, im = r + (pr * cr - pi_ * ci), im + (pr * ci + pi_ * cr)
                lam_ref[rows, 0:_LW] = r
                lam_ref[rows, _LW:_CS] = im
                nr = jnp.where(row == 7, cr, pltpu.roll(r, 7, 0))
                ni = jnp.where(row == 7, ci, pltpu.roll(im, 7, 0))
                xr = x_ref[rows, lre]
                xi = x_ref[rows, lim]
                ar = ar + (nr * xr + ni * xi)
                ai = ai + (ni * xr - nr * xi)
                return (jnp.broadcast_to(r[0:1, :], (8, _LW)), jnp.broadcast_to(im[0:1, :], (8, _LW)), ar, ai)

            zero = jnp.zeros((8, _LW), F32)
            cr, ci, ar, ai = lax.fori_loop(0, tm // 8, blk, (car_ref[:, lre], car_ref[:, lim], zero, zero))
            car_ref[:, lre] = cr
            car_ref[:, lim] = ci
            de_ref[0:8, tl] += ar
            de_ref[8:16, tl] += ai
            lam16 = lam_ref[...].astype(BF16)
            dbb_ref[cols, :] += lax.dot_general(u16[:, cols], lam16, tdims, preferred_element_type=F32)
            du = jnp.dot(lam16, bbt_ref[c], preferred_element_type=F32) + dys[:, cols] * d_ref[:, cols]
            du_ref[:, cols] = du.astype(BF16)

        @pl.when(i == nt - 1)
        def _():
            pltpu.sync_copy(dc_ref, dc_hbm)
            pltpu.sync_copy(dbb_ref, dbb_hbm)

    rev = lambda i: (nt - 1 - i, 0)
    once = lambda shape: pl.BlockSpec(shape, lambda *_: (0,) * len(shape), pipeline_mode=pl.Buffered(1))
    cross = jax.ShapeDtypeStruct((CW, _CS), F32)
    return pl.pallas_call(
        body, name="ssm_bwd",
        out_shape=(jax.ShapeDtypeStruct((S, CW), BF16), jax.ShapeDtypeStruct((16, NST), F32),
                   jax.ShapeDtypeStruct((8, CW), F32), cross, cross),
        grid=(nt,),
        in_specs=[pl.BlockSpec((tm, 2 * D_MODEL), rev), once((CW, 2 * D_MODEL)), pl.BlockSpec((tm, CW), rev),
                  pl.BlockSpec((tm, CW), lambda i: (nt - 1 - i, 2)), pl.BlockSpec((tm, 2 * NST), rev),
                  once((_NCL, _UC, _CS)), once((_NCL, _CS, _UC)), _full((1, CW)), once((64, NST)),
                  pl.BlockSpec(memory_space=pl.ANY)],
        out_specs=(pl.BlockSpec((tm, CW), rev), _full((16, NST)), _full((8, CW)),
                   pl.BlockSpec(memory_space=pl.ANY), pl.BlockSpec(memory_space=pl.ANY)),
        scratch_shapes=[pltpu.VMEM((8, 2 * NST), F32), pltpu.VMEM((tm, _CS), F32),
                        pltpu.VMEM((CW, _CS), F32), pltpu.VMEM((CW, _CS), F32)],
        compiler_params=_cp(("arbitrary",)))(dzz, w_glu, ys, z, xs, cmt, bbt, d, tab, after)


def _ssm_prep(a_re, a_im, b_re, b_im, log_dt):
    dt = jnp.exp(log_dt.reshape(G))[:, None]
    mag = jnp.exp(dt * a_re)
    e_re, e_im = mag * jnp.cos(dt * a_im), mag * jnp.sin(dt * a_im)
    n_re, n_im = e_re - 1.0, e_im
    den = a_re * a_re + a_im * a_im
    q_re = (n_re * a_re + n_im * a_im) / den
    q_im = (n_im * a_re - n_re * a_im) / den
    bb_re = q_re[..., None] * b_re - q_im[..., None] * b_im
    bb_im = q_re[..., None] * b_im + q_im[..., None] * b_re
    return e_re, e_im, bb_re, bb_im


def _scan_tables(e_re, e_im, reverse):
    er = e_re.reshape(1, NST)
    ei = e_im.reshape(1, NST)
    if reverse:
        ei = -ei
    pows = [(er, ei)]
    for _ in range(7):
        pr, pi_ = pows[-1]
        pows.append((pr * er - pi_ * ei, pr * ei + pi_ * er))
    row = jnp.arange(8)[:, None]
    out = []
    for s in (1, 2, 4):
        pr, pi_ = pows[s - 1]
        keep = (row + s <= 7) if reverse else (row >= s)
        out += [jnp.where(keep, pr, 0.0), jnp.where(keep, pi_, 0.0)]
    allr = jnp.concatenate([p[0] for p in pows], 0)
    alli = jnp.concatenate([p[1] for p in pows], 0)
    if reverse:
        allr, alli = allr[::-1], alli[::-1]
    out += [allr, alli]
    return jnp.concatenate(out, 0).astype(F32)


def _block_diag_mats(bb_re, bb_im, c_re, c_im):
    gc = G // _NCL
    eye = jnp.eye(gc, dtype=F32)
    bre = jnp.einsum("cjph,jk->cjhkp", bb_re.reshape(_NCL, gc, P, H), eye).reshape(_NCL, _UC, _LW)
    bim = jnp.einsum("cjph,jk->cjhkp", bb_im.reshape(_NCL, gc, P, H), eye).reshape(_NCL, _UC, _LW)
    bb = jnp.concatenate([bre, bim], 2)
    cre = jnp.einsum("cjhp,jk->cjpkh", c_re.reshape(_NCL, gc, H, P), eye).reshape(_NCL, _LW, _UC)
    cim = jnp.einsum("cjhp,jk->cjpkh", c_im.reshape(_NCL, gc, H, P), eye).reshape(_NCL, _LW, _UC)
    cm = jnp.concatenate([cre, -cim], 1)
    return bb, cm


def _diag_blocks(cross, imag):
    gc = G // _NCL
    off = _LW if imag else 0
    return jnp.stack([cross[H * g:H * (g + 1), off + P * (g % gc):off + P * (g % gc + 1)] for g in range(G)])


def _mix_fwd(scv, yg, z, x, w_cp, w_glu, w_out, g1, n2g, sc2, sh2):
    S = z.shape[0]
    tm = min(512, S)
    D = D_MODEL

    def body(s_ref, yg_ref, glc0_ref, glc1_ref, gls0_ref, gls1_ref, x_ref, wcp_ref, wglu_ref, wout_ref,
             g1_ref, n2_ref, sc_ref, sh_ref, yc_ref, zz_ref, m_ref, o_ref, x2_ref, h2_ref):
        y_conv = jnp.dot(s_ref[...], wcp_ref[...], preferred_element_type=F32)
        zz = jnp.dot(yg_ref[...], wglu_ref[...], preferred_element_type=F32)
        yc_ref[...] = y_conv.astype(BF16)
        zz_ref[...] = zz.astype(BF16)
        for half, (glc_ref, gls_ref) in enumerate(((glc0_ref, gls0_ref), (glc1_ref, gls1_ref))):
            lo, hi = half * CW, (half + 1) * CW
            y_ssm = zz[:, lo:hi] * _sig(zz[:, D + lo:D + hi])
            m_ref[:, lo:hi] = (_sig(glc_ref[...].astype(F32)) * y_conv[:, lo:hi]
                               + _sig(gls_ref[...].astype(F32)) * y_ssm).astype(BF16)
        o = jnp.dot(m_ref[...], wout_ref[...], preferred_element_type=F32)
        o_ref[...] = o.astype(BF16)
        xv = x_ref[...] + g1_ref[...] * o
        x2_ref[...] = xv
        r = lax.rsqrt(jnp.mean(xv * xv, axis=-1, keepdims=True) + EPS)
        h2_ref[...] = (xv * r * (n2_ref[...] * (1.0 + sc_ref[...])) + sh_ref[...]).astype(BF16)

    zb_ = lambda j: pl.BlockSpec((tm, CW), lambda i: (i, j))
    row = lambda w: pl.BlockSpec((tm, w), lambda i: (i, 0))
    par = _full((1, D))
    bf = lambda w: jax.ShapeDtypeStruct((S, w), BF16)
    return pl.pallas_call(
        body, name="mix_fwd",
        out_shape=(bf(D), bf(2 * D), bf(D), bf(D), jax.ShapeDtypeStruct((S, D), F32), bf(D)),
        grid=(S // tm,),
        in_specs=[row(CW), row(CW), zb_(3), zb_(4), zb_(5), zb_(6), row(D), _resident((CW, D)),
                  _resident((CW, 2 * D)), _resident((D, D)), par, par, par, par],
        out_specs=(row(D), row(2 * D), row(D), row(D), row(D), row(D)),
        compiler_params=_cp(("parallel",)))(scv, yg, z, z, z, z, x, w_cp, w_glu, w_out, g1, n2g, sc2, sh2)


def _mix_bwd(do, w_out, z, zz, y_conv, after):
    S = z.shape[0]
    tm = min(512, S)
    D = D_MODEL

    def body(do_ref, w_ref, glc0_ref, glc1_ref, gls0_ref, gls1_ref, za_ref, zb_ref, yc_ref, after_ref,
             dyc_ref, dgl_ref, dzz_ref):
        dm = lax.dot_general(do_ref[...], w_ref[...], (((1,), (1,)), ((), ())), preferred_element_type=F32)
        for half, (glc_ref, gls_ref) in enumerate(((glc0_ref, gls0_ref), (glc1_ref, gls1_ref))):
            lo, hi = half * CW, (half + 1) * CW
            dm_v = dm[:, lo:hi]
            sgc = _sig(glc_ref[...].astype(F32))
            sgs = _sig(gls_ref[...].astype(F32))
            szb = _sig(zb_ref[:, lo:hi].astype(F32))
            za = za_ref[:, lo:hi].astype(F32)
            dyc_ref[:, lo:hi] = (dm_v * sgc).astype(BF16)
            dgl_ref[:, lo:hi] = (dm_v * yc_ref[:, lo:hi].astype(F32) * sgc * (1.0 - sgc)).astype(BF16)
            dys = dm_v * sgs
            dgl_ref[:, D + lo:D + hi] = (dys * (za * szb) * (1.0 - sgs)).astype(BF16)
            dzz_ref[:, lo:hi] = (dys * szb).astype(BF16)
            dzz_ref[:, D + lo:D + hi] = (dys * za * szb * (1.0 - szb)).astype(BF16)

    zb_ = lambda j: pl.BlockSpec((tm, CW), lambda i: (i, j))
    wide = lambda j: pl.BlockSpec((tm, D), lambda i: (i, j))
    return pl.pallas_call(
        body, name="mix_bwd",
        out_shape=(jax.ShapeDtypeStruct((S, D), BF16), jax.ShapeDtypeStruct((S, 2 * D), BF16),
                   jax.ShapeDtypeStruct((S, 2 * D), BF16)),
        grid=(S // tm,),
        in_specs=[wide(0), _resident((D, D)), zb_(3), zb_(4), zb_(5), zb_(6), wide(0), wide(1), wide(0),
                  pl.BlockSpec(memory_space=pl.ANY)],
        out_specs=(wide(0), pl.BlockSpec((tm, 2 * D), lambda i: (i, 0)), pl.BlockSpec((tm, 2 * D), lambda i: (i, 0))),
        compiler_params=_cp(("parallel",)))(do, w_out, z, z, z, z, zz, zz, y_conv, after)


_FC = 1408


def _ffn_in_act(h2, w_fi):
    S, D = h2.shape
    tm = min(512, S)

    def body(h_ref, w_ref, f_ref, a_ref):
        hv = h_ref[...]
        for c in range(FH // _FC):
            lo, hi = c * _FC, (c + 1) * _FC
            g = jnp.dot(hv, w_ref[:, lo:hi], preferred_element_type=F32)
            u = jnp.dot(hv, w_ref[:, FH + lo:FH + hi], preferred_element_type=F32)
            f_ref[:, lo:hi] = g.astype(BF16)
            f_ref[:, FH + lo:FH + hi] = u.astype(BF16)
            a_ref[:, lo:hi] = (g * _sig(g) * u).astype(BF16)

    return pl.pallas_call(
        body, name="ffn_in_act",
        out_shape=(jax.ShapeDtypeStruct((S, 2 * FH), BF16), jax.ShapeDtypeStruct((S, FH), BF16)),
        grid=(S // tm,),
        in_specs=[pl.BlockSpec((tm, D), lambda i: (i, 0)), _resident((D, 2 * FH))],
        out_specs=(pl.BlockSpec((tm, 2 * FH), lambda i: (i, 0)), pl.BlockSpec((tm, FH), lambda i: (i, 0))),
        compiler_params=_cp(("parallel",)))(h2, w_fi)


def _ffn_bwd(do2, w_fo, f, after):
    S, D = do2.shape
    tm = min(512, S)

    def body(d_ref, w_ref, f_ref, after_ref, df_ref):
        dv = d_ref[...]
        for c in range(FH // _FC):
            lo, hi = c * _FC, (c + 1) * _FC
            dact = lax.dot_general(dv, w_ref[lo:hi, :], (((1,), (1,)), ((), ())), preferred_element_type=F32)
            g = f_ref[:, lo:hi].astype(F32)
            u = f_ref[:, FH + lo:FH + hi].astype(F32)
            sg = _sig(g)
            df_ref[:, lo:hi] = (dact * u * (sg * (1.0 + g * (1.0 - sg)))).astype(BF16)
            df_ref[:, FH + lo:FH + hi] = (dact * g * sg).astype(BF16)

    return pl.pallas_call(
        body, name="ffn_bwd", out_shape=jax.ShapeDtypeStruct((S, 2 * FH), BF16), grid=(S // tm,),
        in_specs=[pl.BlockSpec((tm, D), lambda i: (i, 0)), _resident((FH, D)),
                  pl.BlockSpec((tm, 2 * FH), lambda i: (i, 0)), pl.BlockSpec(memory_space=pl.ANY)],
        out_specs=pl.BlockSpec((tm, 2 * FH), lambda i: (i, 0)),
        compiler_params=_cp(("parallel",)))(do2, w_fo, f, after)


def _ffn_out_final(x2, act, w_fo, g2, fg, tgt):
    S, D = x2.shape
    tm = min(512, S)

    def body(x2_ref, a_ref, w_ref, g2_ref, fg_ref, t_ref, dx3_ref, do2_ref, ls_ref, dfg_ref, dg2_ref):
        i = pl.program_id(0)

        @pl.when(i == 0)
        def _():
            ls_ref[...] = jnp.zeros_like(ls_ref)
            dfg_ref[...] = jnp.zeros_like(dfg_ref)
            dg2_ref[...] = jnp.zeros_like(dg2_ref)

        o2 = jnp.dot(a_ref[...], w_ref[...], preferred_element_type=F32)
        x3 = x2_ref[...] + g2_ref[...] * o2
        r = lax.rsqrt(jnp.mean(x3 * x3, axis=-1, keepdims=True) + EPS)
        xn = x3 * r
        err = xn * fg_ref[...] - t_ref[...]
        dy = err * (1.0 / D)
        dxn = dy * fg_ref[...]
        dx3 = r * (dxn - xn * jnp.mean(dxn * xn, axis=-1, keepdims=True))
        dx3_ref[...] = dx3
        do2_ref[...] = (dx3 * g2_ref[...]).astype(BF16)
        e2 = _colsum8(err * err)
        lanes = e2[:, 0:128]
        for q in range(1, D // 128):
            lanes = lanes + e2[:, q * 128:(q + 1) * 128]
        ls_ref[...] += lanes * (0.5 / D)
        dfg_ref[...] += _colsum8(dy * xn)
        dg2_ref[...] += _colsum8(dx3 * o2)

    row = pl.BlockSpec((tm, D), lambda i: (i, 0))
    par = _full((1, D))
    return pl.pallas_call(
        body, name="final_loss",
        out_shape=(jax.ShapeDtypeStruct((S, D), F32), jax.ShapeDtypeStruct((S, D), BF16),
                   jax.ShapeDtypeStruct((8, 128), F32), jax.ShapeDtypeStruct((8, D), F32),
                   jax.ShapeDtypeStruct((8, D), F32)),
        grid=(S // tm,), in_specs=[row, pl.BlockSpec((tm, FH), lambda i: (i, 0)), _resident((FH, D)), par, par, row],
        out_specs=(row, row, _full((8, 128)), _full((8, D)), _full((8, D))),
        compiler_params=_cp(("arbitrary",)))(x2, act, w_fo, g2, fg, tgt)


def _normmod_bwd(dsrc, w, xin, dres, g, sc, gate, o, after, name):
    S, D = xin.shape
    parts = list(dsrc) if isinstance(dsrc, (list, tuple)) else [dsrc]
    widths = [p.shape[1] for p in parts]
    K = sum(widths)
    tm = min(512, S)
    npart = len(parts)

    def body(*refs):
        ds_refs = refs[:npart]
        w_ref, x_ref, dr_ref, g_ref, sc_ref, gate_ref, o_ref, after_ref = refs[npart:npart + 8]
        dx_ref, do_ref, dsh_ref, dsc_ref, dg_ref, dgate_ref = refs[npart + 8:]
        i = pl.program_id(0)

        @pl.when(i == 0)
        def _():
            dsh_ref[...] = jnp.zeros_like(dsh_ref)
            dsc_ref[...] = jnp.zeros_like(dsc_ref)
            dg_ref[...] = jnp.zeros_like(dg_ref)
            dgate_ref[...] = jnp.zeros_like(dgate_ref)

        gv = g_ref[...]
        scale = 1.0 + sc_ref[...]
        xv = x_ref[...]
        r = lax.rsqrt(jnp.mean(xv * xv, axis=-1, keepdims=True) + EPS)
        xn = xv * r
        dh_v, col = None, 0
        for ds_ref, wd in zip(ds_refs, widths):
            t = lax.dot_general(ds_ref[...], w_ref[:, col:col + wd], (((1,), (1,)), ((), ())),
                                preferred_element_type=F32)
            dh_v = t if dh_v is None else dh_v + t
            col += wd
        dxn = dh_v * (gv * scale)
        dx = dr_ref[...] + r * (dxn - xn * jnp.mean(dxn * xn, axis=-1, keepdims=True))
        dx_ref[...] = dx
        do_ref[...] = (dx * gate_ref[...]).astype(BF16)
        hx = dh_v * xn
        dsh_ref[...] += _colsum8(dh_v)
        dsc_ref[...] += _colsum8(hx) * gv
        dg_ref[...] += _colsum8(hx) * scale
        dgate_ref[...] += _colsum8(dx * o_ref[...])

    row = pl.BlockSpec((tm, D), lambda i: (i, 0))
    par = _full((1, D))
    acc = jax.ShapeDtypeStruct((8, D), F32)
    return pl.pallas_call(
        body, name=name,
        out_shape=(jax.ShapeDtypeStruct((S, D), F32), jax.ShapeDtypeStruct((S, D), BF16), acc, acc, acc, acc),
        grid=(S // tm,),
        in_specs=[pl.BlockSpec((tm, wd), lambda i: (i, 0)) for wd in widths]
        + [_resident((D, K)), row, row, par, par, par, row, pl.BlockSpec(memory_space=pl.ANY)],
        out_specs=(row, row, _full((8, D)), _full((8, D)), _full((8, D)), _full((8, D))),
        compiler_params=_cp(("arbitrary",)))(*parts, w, xin, dres, g, sc, gate, o, after)


def _me():
    return lax.axis_index("x"), lax.axis_index("y"), lax.axis_index("c")


def _allgather8(v, name, after=()):
    R, C = v.shape
    after = list(after)

    def body(v_ref, *rest):
        out_ref, send_sems, recv_sems, local_sem = rest[len(after):]
        x, y, c = _me()
        mine = pltpu.make_async_copy(v_ref, out_ref.at[4 * x + 2 * y + c], local_sem)
        mine.start()
        copies = []
        for k in range(1, N_DEV):
            fx, fy, fc = (k >> 2) & 1, (k >> 1) & 1, k & 1
            peer = (x ^ fx, y ^ fy, c ^ fc)
            copies.append(pltpu.make_async_remote_copy(
                src_ref=v_ref, dst_ref=out_ref.at[4 * x + 2 * y + c],
                send_sem=send_sems.at[k - 1], recv_sem=recv_sems.at[k - 1],
                device_id=peer, device_id_type=MESH))
        for cp in copies:
            cp.start()
        for k in range(1, N_DEV):
            fx, fy, fc = (k >> 2) & 1, (k >> 1) & 1, k & 1
            src_slot = 4 * (x ^ fx) + 2 * (y ^ fy) + (c ^ fc)
            pltpu.make_async_remote_copy(
                src_ref=v_ref, dst_ref=out_ref.at[src_slot],
                send_sem=send_sems.at[k - 1], recv_sem=recv_sems.at[k - 1],
                device_id=(x ^ fx, y ^ fy, c ^ fc), device_id_type=MESH).wait_recv()
        for cp in copies:
            cp.wait_send()
        mine.wait()

    return pl.pallas_call(
        body, name=name, out_shape=jax.ShapeDtypeStruct((N_DEV, R, C), v.dtype),
        in_specs=[pl.BlockSpec(memory_space=pltpu.VMEM)] + [pl.BlockSpec(memory_space=pl.ANY)] * len(after),
        out_specs=pl.BlockSpec(memory_space=pltpu.VMEM),
        scratch_shapes=[pltpu.SemaphoreType.DMA((N_DEV - 1,)), pltpu.SemaphoreType.DMA((N_DEV - 1,)),
                        pltpu.SemaphoreType.DMA],
        compiler_params=pltpu.CompilerParams(vmem_limit_bytes=VMEM_LIMIT))(v, *after)


def _swap_sibling(arrs):
    nw = len(arrs)

    def body(*refs):
        ins, outs = refs[:nw], refs[nw:2 * nw]
        send_sems, recv_sems = refs[2 * nw:]
        x, y, c = _me()
        copies = [pltpu.make_async_remote_copy(
            src_ref=ins[w], dst_ref=outs[w], send_sem=send_sems.at[w], recv_sem=recv_sems.at[w],
            device_id=(x, y, 1 - c), device_id_type=MESH) for w in range(nw)]
        for cp in copies:
            cp.start()
        for cp in copies:
            cp.wait_recv()
        for cp in copies:
            cp.wait_send()

    hbm = pl.BlockSpec(memory_space=pltpu.HBM)
    return pl.pallas_call(
        body, name="swap_sibling", out_shape=tuple(jax.ShapeDtypeStruct(a.shape, a.dtype) for a in arrs),
        in_specs=[hbm] * nw, out_specs=tuple([hbm] * nw),
        scratch_shapes=[pltpu.SemaphoreType.DMA((nw,)), pltpu.SemaphoreType.DMA((nw,))],
        compiler_params=pltpu.CompilerParams(vmem_limit_bytes=VMEM_LIMIT))(*arrs)


_HBM = pl.BlockSpec(memory_space=pltpu.HBM)
_SEM = pl.BlockSpec(memory_space=pltpu.SEMAPHORE)
_EFFECT = pltpu.SideEffectType.DATAFLOW_SIDE_EFFECTING
_N_PEER = N_CHIP - 1


def _chip_part(ref, axis, n, chip):
    start = pl.multiple_of(chip * n, 8)
    return ref.at[pl.ds(start, n), :] if axis == 0 else ref.at[:, pl.ds(start, n)]


def _gather_copy(k, src_ref, land_ref, send_sems, recv_sems, axis, arriving):
    x, y, c = _me()
    px, py = x ^ ((k >> 1) & 1), y ^ (k & 1)
    chip = 2 * px + py if arriving else 2 * x + y
    return pltpu.make_async_remote_copy(
        src_ref=src_ref, dst_ref=_chip_part(land_ref, axis, src_ref.shape[axis], chip),
        send_sem=send_sems.at[k - 1], recv_sem=recv_sems.at[k - 1], device_id=(px, py, c), device_id_type=MESH)


def _scatter_copy(k, grad_ref, land_ref, send_sems, recv_sems, axis):
    x, y, c = _me()
    px, py = x ^ ((k >> 1) & 1), y ^ (k & 1)
    return pltpu.make_async_remote_copy(
        src_ref=_chip_part(grad_ref, axis, grad_ref.shape[axis] // N_CHIP, 2 * px + py), dst_ref=land_ref.at[k - 1],
        send_sem=send_sems.at[k - 1], recv_sem=recv_sems.at[k - 1], device_id=(px, py, c), device_id_type=MESH)


def _own_copy(src_ref, land_ref, sends, axis):
    x, y, _ = _me()
    return pltpu.make_async_copy(src_ref, _chip_part(land_ref, axis, src_ref.shape[axis], 2 * x + y),
                                 sends.at[_N_PEER])


def _gather_start(shards, axes, after, name):
    nw = len(shards)
    lands = []
    for s, ax in zip(shards, axes):
        shp = list(s.shape)
        shp[ax] *= N_CHIP
        lands.append(lax.empty(tuple(shp), s.dtype))

    def body(*refs):
        srcs, zones = refs[:nw], refs[nw:2 * nw]
        sends, recvs = refs[2 * nw + 1:3 * nw + 1], refs[3 * nw + 1:4 * nw + 1]
        token = refs[-1]
        for w in range(nw):
            for k in range(1, N_CHIP):
                _gather_copy(k, srcs[w], zones[w], sends[w], recvs[w], axes[w], False).start()
        for w in range(nw):
            _own_copy(srcs[w], zones[w], sends[w], axes[w]).start()
        token[...] = jnp.zeros_like(token)

    outs = pl.pallas_call(
        body, name=name,
        out_shape=tuple([pltpu.SemaphoreType.DMA((_N_PEER + 1,))] * nw + [pltpu.SemaphoreType.DMA((_N_PEER,))] * nw
                        + [pltpu.HBM(a.shape, a.dtype) for a in list(shards) + list(lands)]
                        + [jax.ShapeDtypeStruct((8, 128), F32)]),
        in_specs=[_HBM] * (2 * nw) + [pl.BlockSpec(memory_space=pl.ANY)],
        out_specs=tuple([_SEM] * (2 * nw) + [_HBM] * (2 * nw) + [pl.BlockSpec(memory_space=pltpu.VMEM)]),
        input_output_aliases={i: 2 * nw + i for i in range(2 * nw)},
        compiler_params=pltpu.CompilerParams(has_side_effects=_EFFECT),
    )(*([pltpu.with_memory_space_constraint(a, pltpu.HBM) for a in list(shards) + list(lands)] + [after]))
    per_weight = [(outs[w], outs[nw + w], outs[2 * nw + w], outs[3 * nw + w]) for w in range(nw)]
    return per_weight, outs[-1]


def _gather_wait(state, axis, after, name):
    send_sems, recv_sems, shard, land = state

    after = list(after) if isinstance(after, (list, tuple)) else [after]

    def body(src_ref, land_ref, sends, recvs, *rest):
        for k in range(1, N_CHIP):
            _gather_copy(k, src_ref, land_ref, sends, recvs, axis, False).wait_send()
            _gather_copy(k, src_ref, land_ref, sends, recvs, axis, True).wait_recv()
        _own_copy(src_ref, land_ref, sends, axis).wait()

    return pl.pallas_call(
        body, name=name, out_shape=(pltpu.HBM(shard.shape, shard.dtype), pltpu.HBM(land.shape, land.dtype)),
        in_specs=[_HBM, _HBM, _SEM, _SEM] + [pl.BlockSpec(memory_space=pl.ANY)] * len(after), out_specs=(_HBM, _HBM),
        input_output_aliases={0: 0, 1: 1},
        compiler_params=pltpu.CompilerParams(has_side_effects=_EFFECT),
    )(shard, land, send_sems, recv_sems, *after)[1]


def _all8_copy(k, v_ref, land_ref, send_sems, recv_sems, arriving):
    x, y, c = _me()
    px, py, pc = x ^ ((k >> 2) & 1), y ^ ((k >> 1) & 1), c ^ (k & 1)
    slot = 4 * px + 2 * py + pc if arriving else 4 * x + 2 * y + c
    return pltpu.make_async_remote_copy(
        src_ref=v_ref, dst_ref=land_ref.at[slot], send_sem=send_sems.at[k - 1], recv_sem=recv_sems.at[k - 1],
        device_id=(px, py, pc), device_id_type=MESH)


def _all8_own(v_ref, land_ref, send_sems):
    x, y, c = _me()
    return pltpu.make_async_copy(v_ref, land_ref.at[4 * x + 2 * y + c], send_sems.at[N_DEV - 1])


def _all8_start(v, name):
    land = lax.empty((N_DEV,) + v.shape, v.dtype)

    def body(v_ref, land_ref, sends, recvs, v_thru, land_thru, token):
        for k in range(1, N_DEV):
            _all8_copy(k, v_ref, land_ref, sends, recvs, False).start()
        _all8_own(v_ref, land_ref, sends).start()
        token[...] = jnp.zeros_like(token)

    outs = pl.pallas_call(
        body, name=name,
        out_shape=(pltpu.SemaphoreType.DMA((N_DEV,)), pltpu.SemaphoreType.DMA((N_DEV - 1,)),
                   pltpu.HBM(v.shape, v.dtype), pltpu.HBM(land.shape, land.dtype),
                   jax.ShapeDtypeStruct((8, 128), F32)),
        in_specs=[_HBM, _HBM], out_specs=(_SEM, _SEM, _HBM, _HBM, pl.BlockSpec(memory_space=pltpu.VMEM)),
        input_output_aliases={0: 2, 1: 3},
        compiler_params=pltpu.CompilerParams(has_side_effects=_EFFECT),
    )(pltpu.with_memory_space_constraint(v, pltpu.HBM), pltpu.with_memory_space_constraint(land, pltpu.HBM))
    return outs[:4], outs[4]


def _all8_wait(state, after, name):
    send_sems, recv_sems, v, land = state

    def body(v_ref, land_ref, sends, recvs, after_ref, v_dead, got_ref):
        for k in range(1, N_DEV):
            _all8_copy(k, v_ref, land_ref, sends, recvs, False).wait_send()
            _all8_copy(k, v_ref, land_ref, sends, recvs, True).wait_recv()
        _all8_own(v_ref, land_ref, sends).wait()

    return pl.pallas_call(
        body, name=name, out_shape=(pltpu.HBM(v.shape, v.dtype), pltpu.HBM(land.shape, land.dtype)),
        in_specs=[_HBM, _HBM, _SEM, _SEM, pl.BlockSpec(memory_space=pl.ANY)], out_specs=(_HBM, _HBM),
        input_output_aliases={0: 0, 1: 1},
        compiler_params=pltpu.CompilerParams(has_side_effects=_EFFECT),
    )(v, land, send_sems, recv_sems, after)[1]


def _swap_copy(w, src_ref, land_ref, send_sems, recv_sems):
    x, y, c = _me()
    return pltpu.make_async_remote_copy(src_ref=src_ref, dst_ref=land_ref, send_sem=send_sems.at[w],
                                        recv_sem=recv_sems.at[w], device_id=(x, y, 1 - c), device_id_type=MESH)


def _swap_start(arrs, after, name):
    nw = len(arrs)
    lands = [lax.empty(a.shape, a.dtype) for a in arrs]

    def body(*refs):
        srcs, zones = refs[:nw], refs[nw:2 * nw]
        sends, recvs = refs[2 * nw + 1], refs[2 * nw + 2]
        for w in range(nw):
            _swap_copy(w, srcs[w], zones[w], sends, recvs).start()
        refs[-1][...] = jnp.zeros_like(refs[-1])

    sem = pltpu.SemaphoreType.DMA((nw,))
    outs = pl.pallas_call(
        body, name=name,
        out_shape=tuple([sem, sem] + [pltpu.HBM(a.shape, a.dtype) for a in list(arrs) + lands]
                        + [jax.ShapeDtypeStruct((8, 128), F32)]),
        in_specs=[_HBM] * (2 * nw) + [pl.BlockSpec(memory_space=pl.ANY)],
        out_specs=tuple([_SEM, _SEM] + [_HBM] * (2 * nw) + [pl.BlockSpec(memory_space=pltpu.VMEM)]),
        input_output_aliases={i: 2 + i for i in range(2 * nw)},
        compiler_params=pltpu.CompilerParams(has_side_effects=_EFFECT),
    )(*([pltpu.with_memory_space_constraint(a, pltpu.HBM) for a in list(arrs) + lands] + [after]))
    return (outs[0], outs[1], outs[2:2 + nw], outs[2 + nw:2 + 2 * nw]), outs[-1]


def _swap_wait(state, after, name):
    send_sems, recv_sems, arrs, lands = state
    nw = len(arrs)

    def body(*refs):
        srcs, zones = refs[:nw], refs[nw:2 * nw]
        sends, recvs = refs[2 * nw], refs[2 * nw + 1]
        for w in range(nw):
            cp = _swap_copy(w, srcs[w], zones[w], sends, recvs)
            cp.wait_send()
            cp.wait_recv()

    outs = pl.pallas_call(
        body, name=name, out_shape=tuple(pltpu.HBM(a.shape, a.dtype) for a in list(arrs) + list(lands)),
        in_specs=[_HBM] * (2 * nw) + [_SEM, _SEM, pl.BlockSpec(memory_space=pl.ANY)],
        out_specs=tuple([_HBM] * (2 * nw)),
        input_output_aliases={i: i for i in range(2 * nw)},
        compiler_params=pltpu.CompilerParams(has_side_effects=_EFFECT),
    )(*arrs, *lands, send_sems, recv_sems, after)
    return list(outs[:nw]), list(outs[nw:])


def _scatter_start(grad, axis, name):
    shp = list(grad.shape)
    shp[axis] //= N_CHIP
    land = lax.empty((_N_PEER,) + tuple(shp), grad.dtype)

    def body(grad_ref, land_ref, sends, recvs, grad_thru, land_thru, token):
        for k in range(1, N_CHIP):
            _scatter_copy(k, grad_ref, land_ref, sends, recvs, axis).start()
        token[...] = jnp.zeros_like(token)

    sem = pltpu.SemaphoreType.DMA((_N_PEER,))
    outs = pl.pallas_call(
        body, name=name,
        out_shape=(sem, sem, pltpu.HBM(grad.shape, grad.dtype), pltpu.HBM(land.shape, land.dtype),
                   jax.ShapeDtypeStruct((8, 128), F32)),
        in_specs=[_HBM, _HBM], out_specs=(_SEM, _SEM, _HBM, _HBM, pl.BlockSpec(memory_space=pltpu.VMEM)),
        input_output_aliases={0: 2, 1: 3},
        compiler_params=pltpu.CompilerParams(has_side_effects=_EFFECT),
    )(pltpu.with_memory_space_constraint(grad, pltpu.HBM), pltpu.with_memory_space_constraint(land, pltpu.HBM))
    return outs[:4], outs[4]


def _scatter_wait(state, axis, after, name):
    send_sems, recv_sems, grad, land = state

    def body(grad_ref, land_ref, sends, recvs, after_ref, grad_dead, got_ref):
        for k in range(1, N_CHIP):
            cp = _scatter_copy(k, grad_ref, land_ref, sends, recvs, axis)
            cp.wait_send()
            cp.wait_recv()

    return pl.pallas_call(
        body, name=name, out_shape=(pltpu.HBM(grad.shape, grad.dtype), pltpu.HBM(land.shape, land.dtype)),
        in_specs=[_HBM, _HBM, _SEM, _SEM, pl.BlockSpec(memory_space=pl.ANY)], out_specs=(_HBM, _HBM),
        input_output_aliases={0: 0, 1: 1},
        compiler_params=pltpu.CompilerParams(has_side_effects=_EFFECT),
    )(grad, land, send_sems, recv_sems, after)[1]


_C1 = 1.0 - B1 ** STEP
_C2 = 1.0 - B2 ** STEP


def _adam_math(w, g, m, v):
    m = B1 * m + (1.0 - B1) * g
    v = B2 * v + (1.0 - B2) * (g * g)
    delta = -LR * ((m / _C1) / (jnp.sqrt(v / _C2) + AEPS) + WD * w)
    return delta, m, v


def _adamw(w, m, v, groups, name):
    R, C = w.shape
    tr = R if R <= 256 else (128 if R % 128 == 0 else 176)
    assert R % tr == 0, (name, R)
    gparts = [p for grp in groups for p in grp]
    sizes = [len(grp) for grp in groups]
    ng = len(gparts)

    def body(*refs):
        w_ref, m_ref, v_ref = refs[:3]
        g_refs = list(refs[3:3 + ng])
        g_out, d_out, m_out, v_out = refs[3 + ng:]
        g = None
        for size in sizes:
            s = None
            for r in [g_refs.pop(0) for _ in range(size)]:
                terms = [r[q] for q in range(r.shape[0])] if len(r.shape) == 3 else [r[...]]
                for t in terms:
                    s = t.astype(F32) if s is None else s + t.astype(F32)
            g = s if g is None else g + s
        delta, mn, vn = _adam_math(w_ref[...], g, m_ref[...], v_ref[...])
        g_out[...] = g
        d_out[...] = delta
        m_out[...] = mn
        v_out[...] = vn

    blk = pl.BlockSpec((tr, C), lambda i: (i, 0))
    g_specs = [blk if p.ndim == 2 else pl.BlockSpec((p.shape[0], tr, C), lambda i: (0, i, 0)) for p in gparts]
    sds = jax.ShapeDtypeStruct((R, C), F32)
    return pl.pallas_call(
        body, name=name, out_shape=(sds, sds, sds, sds), grid=(R // tr,),
        in_specs=[blk, blk, blk] + g_specs, out_specs=(blk, blk, blk, blk),
        compiler_params=_cp(("parallel",)))(w, m, v, *gparts)


def _adamw_small(stack, names, wts, mom, var, sum_only, name):
    items, row = [], 0
    for n in names:
        shape = (KW, CW) if n == "conv_w" else wts[n].shape
        size = int(np.prod(shape))
        vec = len(shape) == 2 and shape[0] == 1 and n not in sum_only
        view = shape if vec else (-(-size // _PACK_COLS), _PACK_COLS)
        items.append((n, row, size, vec, view))
        row += _pack_rows(shape)
    upd = [it for it in items if it[0] not in sum_only]
    operands = [stack]
    for n, _, _, _, view in upd:
        operands += [d[n].reshape(view) for d in (wts, mom, var)]

    def grad(stack_ref, r0, nrows, ncols):
        g = stack_ref[0, r0:r0 + nrows, 0:ncols]
        for q in range(1, N_DEV):
            g = g + stack_ref[q, r0:r0 + nrows, 0:ncols]
        return g

    def body(*refs):
        stack_ref, ins, outs = refs[0], refs[1:1 + 3 * len(upd)], refs[1 + 3 * len(upd):]
        o = 0
        for idx, (n, r0, size, vec, view) in enumerate(upd):
            w_ref, m_ref, v_ref = ins[3 * idx:3 * idx + 3]
            g_out, d_out, m_out, v_out = outs[o:o + 4]
            o += 4
            if vec:
                pieces = [(j, j * _PACK_COLS, min((j + 1) * _PACK_COLS, size)) for j in range(-(-size // _PACK_COLS))]
            else:
                pieces = [(None, 0, _PACK_COLS)]
            for j, lo, hi in pieces:
                if vec:
                    g = grad(stack_ref, r0 + j, 1, hi - lo)
                    sl = (slice(None), slice(lo, hi))
                else:
                    g = grad(stack_ref, r0, view[0], _PACK_COLS)
                    sl = (slice(None), slice(None))
                delta, mn, vn = _adam_math(w_ref[sl], g, m_ref[sl], v_ref[sl])
                g_out[sl] = g
                d_out[sl] = delta
                m_out[sl] = mn
                v_out[sl] = vn
        for n, r0, size, vec, view in items:
            if n in sum_only:
                outs[o][...] = grad(stack_ref, r0, view[0], _PACK_COLS)
                o += 1

    out_shape = []
    for n, _, _, _, view in upd:
        out_shape += [jax.ShapeDtypeStruct(view, F32)] * 4
    out_shape += [jax.ShapeDtypeStruct(view, F32) for n, _, _, _, view in items if n in sum_only]
    vm = pl.BlockSpec(memory_space=pltpu.VMEM)
    res = pl.pallas_call(
        body, name=name, out_shape=tuple(out_shape), in_specs=[vm] * len(operands),
        out_specs=tuple([vm] * len(out_shape)),
        compiler_params=pltpu.CompilerParams(vmem_limit_bytes=VMEM_LIMIT))(*operands)
    updated = {n: tuple(r.reshape(wts[n].shape) for r in res[4 * i:4 * i + 4]) for i, (n, *_) in enumerate(upd)}
    sums = dict(zip([it[0] for it in items if it[0] in sum_only], res[4 * len(upd):]))
    return updated, sums


def _mod_shard(c_all, w_ada, b_ada_cols):
    n = w_ada.shape[1]
    tn = 512

    def body(c_ref, w_ref, b_ref, o_ref):
        cv = c_ref[...]
        ca = (cv * _sig(cv)).astype(BF16)
        o_ref[...] = jnp.dot(ca, w_ref[...].astype(BF16), preferred_element_type=F32) + b_ref[...]

    return pl.pallas_call(
        body, name="mod_shard", out_shape=jax.ShapeDtypeStruct((N_DEV, n), F32), grid=(n // tn,),
        in_specs=[_full((N_DEV, D_MODEL)), pl.BlockSpec((D_MODEL, tn), lambda j: (0, j)),
                  pl.BlockSpec((1, tn), lambda j: (0, j))],
        out_specs=pl.BlockSpec((N_DEV, tn), lambda j: (0, j)),
        compiler_params=_cp(("parallel",)))(c_all, w_ada, b_ada_cols)


def _ada_grad(c_all, dmod_cols):
    n = dmod_cols.shape[1]
    tn = 512

    def body(c_ref, d_ref, o_ref):
        cv = c_ref[...]
        ca = cv * _sig(cv)
        o_ref[...] = lax.dot_general(ca, d_ref[...], (((0,), (0,)), ((), ())),
                                     preferred_element_type=F32, precision=lax.Precision.HIGHEST)

    return pl.pallas_call(
        body, name="ada_grad", out_shape=jax.ShapeDtypeStruct((D_MODEL, n), F32), grid=(n // tn,),
        in_specs=[_full((N_DEV, D_MODEL)), pl.BlockSpec((N_DEV, tn), lambda j: (0, j))],
        out_specs=pl.BlockSpec((D_MODEL, tn), lambda j: (0, j)),
        compiler_params=_cp(("parallel",)))(c_all, dmod_cols)


def _ssm_tables(W):
    e_re, e_im, bb_re, bb_im = _ssm_prep(W["ssm_a_re"], W["ssm_a_im"], W["ssm_b_re"], W["ssm_b_im"], W["ssm_log_dt"])
    bb, cm = _block_diag_mats(bb_re, bb_im, W["ssm_c_re"], W["ssm_c_im"])
    bb16, cm16 = bb.astype(BF16), cm.astype(BF16)
    return (bb16, cm16, jnp.swapaxes(bb16, 1, 2), jnp.swapaxes(cm16, 1, 2),
            _scan_tables(e_re, e_im, False), _scan_tables(e_re, e_im, True))


def _device_step(x, mod, W, tables, tgt, getw, put, early):
    sh1, sc1, g1, sh2, sc2, g2 = [mod[:, i * D_MODEL:(i + 1) * D_MODEL] for i in range(6)]
    bb16, cm16, bbt16, cmt16, tab_f, tab_b = tables

    w_in = getw("w_in", [mod, *tables])
    h1, z = _in_proj(x, W["norm1_g"], sc1, sh1, w_in)
    yc, scv = _conv_fwd(z, W["conv_w"], W["conv_b"], W["conv_ln_g"], W["conv_ln_b"])
    xs, ys, yg = _ssm_fwd(z, bb16, cm16, W["ssm_d"], tab_f)
    w_cp, w_glu, w_out = getw("conv_proj", scv), getw("ssm_glu", yg), getw("w_out", yg)
    y_conv, zz, merged, o, x2, h2 = _mix_fwd(scv, yg, z, x, w_cp, w_glu, w_out, g1, W["norm2_g"], sc2, sh2)
    w_fi = getw("w_ffn_in", h2)
    f, act = _ffn_in_act(h2, w_fi)
    w_fo = getw("w_ffn_out", act)
    dx3, do2, loss8, dfg8, dg2_8 = _ffn_out_final(x2, act, w_fo, g2, W["final_g"], tgt)

    sm = {}
    tok = put("w_ffn_out", _matmul(act, do2, "tn", 1408, 1024, 2048, BF16, "mm_g_ffn_out"))
    df = _ffn_bwd(do2, w_fo, f, tok)
    tok = put("w_ffn_in", _matmul(h2, df, "tn", 1024, 1408, 2048, BF16, "mm_g_ffn_in"))
    dx2, do, dsh2, dsc2, dn2, dg1_8 = _normmod_bwd(df, w_fi, x2, dx3, W["norm2_g"], sc2, g1, o, tok, "d_h2_normmod2_bwd")
    tok = put("w_out", _matmul(merged, do, "tn", 1024, 1024, 4096, BF16, "mm_g_w_out"))
    dyconv, dgl, dzz = _mix_bwd(do, w_out, z, zz, y_conv, tok)
    tok = put("ssm_glu", _matmul(yg, dzz, "tn", 512, 1024, 4096, BF16, "mm_g_ssm_glu"))
    tok = put("conv_proj", _matmul(scv, dyconv, "tn", 512, 1024, 4096, BF16, "mm_g_conv_proj", after=tok))
    du, de16, dd8, dc_full, dbb_full = _ssm_bwd(dzz, w_glu, ys, z, xs, cmt16, bbt16, W["ssm_d"], tab_b, tok)
    dyc, dlg8, dlb8, dcb8 = _conv_bwd_ln(dyconv, w_cp, yc, W["conv_ln_g"], W["conv_ln_b"])
    dz_conv, dcw = _conv_bwd(dyc, z, W["conv_w"])

    s8 = lambda a: jnp.sum(a, axis=0, keepdims=True)
    de = de16.reshape(2, 8, NST).sum(1)
    de_re, de_im = de[0].reshape(G, P), de[1].reshape(G, P)
    dc_re = _diag_blocks(dc_full, False)
    dc_im = -_diag_blocks(dc_full, True)
    dbb_re = jnp.swapaxes(_diag_blocks(dbb_full, False), 1, 2)
    dbb_im = jnp.swapaxes(_diag_blocks(dbb_full, True), 1, 2)
    _, vjp = jax.vjp(_ssm_prep, W["ssm_a_re"], W["ssm_a_im"], W["ssm_b_re"], W["ssm_b_im"], W["ssm_log_dt"])
    sm["ssm_a_re"], sm["ssm_a_im"], sm["ssm_b_re"], sm["ssm_b_im"], sm["ssm_log_dt"] = vjp((de_re, de_im, dbb_re, dbb_im))
    sm["ssm_c_re"], sm["ssm_c_im"] = dc_re, dc_im
    sm["ssm_d"] = s8(dd8)
    sm["norm2_g"] = s8(dn2)
    sm["conv_b"], sm["conv_ln_g"], sm["conv_ln_b"] = s8(dcb8), s8(dlg8), s8(dlb8)
    sm["conv_w"] = dcw.reshape(KW, 8, CW).sum(1)
    sm["final_g"] = s8(dfg8)
    tok = early(sm)

    dz = [dz_conv, du, dgl]
    tok = put("w_in", _matmul(h1, dz, "tn", 1024, 512, 4096, BF16, "mm_g_w_in", after=tok))
    dx, _, dsh1, dsc1, dn1, _ = _normmod_bwd(dz, w_in, x, dx2, W["norm1_g"], sc1, g1, o, tok, "d_h1_normmod1_bwd")
    dmod = jnp.concatenate([s8(dsh1), s8(dsc1), s8(dg1_8), s8(dsh2), s8(dsc2), s8(dg2_8)], axis=1)
    return loss8, dx, s8(dn1), dmod


_BIG = ("w_in", "conv_proj", "ssm_glu", "w_out", "w_ffn_in", "w_ffn_out")
_BIG_AXIS = {"w_in": 1, "conv_proj": 1, "ssm_glu": 1, "w_out": 0, "w_ffn_in": 1, "w_ffn_out": 0}
_EARLY = ("conv_w", "conv_b", "conv_ln_g", "conv_ln_b", "ssm_a_re", "ssm_a_im", "ssm_b_re", "ssm_b_im", "ssm_c_re",
          "ssm_c_im", "ssm_d", "ssm_log_dt", "norm2_g", "final_g")
_LATE = ("norm1_g", "b_ada")
_ORDER = ("w_ada", "b_ada", "norm1_g", "w_in", "conv_w", "conv_b", "conv_ln_g", "conv_ln_b", "conv_proj",
          "ssm_a_re", "ssm_a_im", "ssm_b_re", "ssm_b_im", "ssm_c_re", "ssm_c_im", "ssm_d", "ssm_log_dt", "ssm_glu",
          "w_out", "norm2_g", "w_ffn_in", "w_ffn_out", "final_g")
_PACK_COLS = 1024


def _pack_rows(shape):
    return -(-int(np.prod(shape)) // (8 * _PACK_COLS)) * 8


def _pack(arrs):
    parts = []
    for a in arrs:
        flat = a.reshape(-1)
        n = _pack_rows(a.shape)
        parts.append(jnp.pad(flat, (0, n * _PACK_COLS - flat.shape[0])).reshape(n, _PACK_COLS))
    return jnp.concatenate(parts, 0)


def kernel(x, c, w_ada, b_ada, norm1_g, w_in, conv_w, conv_b, conv_ln_g, conv_ln_b, conv_proj, ssm_a_re, ssm_a_im, ssm_b_re, ssm_b_im, ssm_c_re, ssm_c_im, ssm_d, ssm_log_dt, ssm_glu, w_out, norm2_g, w_ffn_in, w_ffn_out, final_g, loss_target, m_w_ada, m_b_ada, m_norm1_g, m_w_in, m_conv_w, m_conv_b, m_conv_ln_g, m_conv_ln_b, m_conv_proj, m_ssm_a_re, m_ssm_a_im, m_ssm_b_re, m_ssm_b_im, m_ssm_c_re, m_ssm_c_im, m_ssm_d, m_ssm_log_dt, m_ssm_glu, m_w_out, m_norm2_g, m_w_ffn_in, m_w_ffn_out, m_final_g, v_w_ada, v_b_ada, v_norm1_g, v_w_in, v_conv_w, v_conv_b, v_conv_ln_g, v_conv_ln_b, v_conv_proj, v_ssm_a_re, v_ssm_a_im, v_ssm_b_re, v_ssm_b_im, v_ssm_c_re, v_ssm_c_im, v_ssm_d, v_ssm_log_dt, v_ssm_glu, v_w_out, v_norm2_g, v_w_ffn_in, v_w_ffn_out, v_final_g):
    given = dict(locals())
    mx, my, mc = _me()
    chip = 2 * mx + my
    dev = 4 * mx + 2 * my + mc
    def canon(a):
        return a.reshape(1, -1) if a.ndim <= 2 else a[0]

    wts = {n: canon(given[n]) for n in _ORDER}
    mom = {n: canon(given["m_" + n]) for n in _ORDER}
    var = {n: canon(given["v_" + n]) for n in _ORDER}

    c_all = _allgather8(jnp.broadcast_to(c, (8, D_MODEL)), "gather_c")[:, 0, :]
    n_ada = wts["w_ada"].shape[1]
    b_cols = lax.dynamic_slice(wts["b_ada"], (0, chip * n_ada), (1, n_ada))
    mod_cols = _mod_shard(c_all, wts["w_ada"], b_cols)
    mods = _allgather8(mod_cols, "gather_mod")
    mod = jnp.concatenate([lax.dynamic_index_in_dim(mods[2 * q], dev, 0, keepdims=True) for q in range(N_CHIP)], axis=1)
    W = {n: wts[n] for n in _ORDER if n not in _BIG}
    conv_w_full = _allgather8(jnp.pad(wts["conv_w"], ((0, 1), (0, 0))), "gather_conv_w", after=[c_all])
    W["conv_w"] = jnp.concatenate([conv_w_full[2 * q, :KW] for q in range(N_CHIP)], axis=1)

    state_in, token = _gather_start([wts["w_in"].astype(BF16)], [_BIG_AXIS["w_in"]],
                                    mod + W["conv_w"][0:1, 0:1], "gather_start_w_in")
    W["ssm_log_dt"] = wts["ssm_log_dt"] + token[0:1, 0:1]
    W["ssm_c_re"] = wts["ssm_c_re"] + token[0, 0]
    tables = _ssm_tables(W)
    w_in_full = _gather_wait(state_in[0], _BIG_AXIS["w_in"], [mod, *tables], "gather_wait_w_in")
    rest = [n for n in _BIG if n != "w_in"]
    gstate, token = _gather_start([wts[n].astype(BF16) for n in rest], [_BIG_AXIS[n] for n in rest], w_in_full,
                                  "gather_start_rest")
    gstate = dict(zip(rest, gstate))
    mod = mod + token[0:1, 0:1]

    def getw(n, after):
        if n == "w_in":
            return w_in_full
        return _gather_wait(gstate[n], _BIG_AXIS[n], after, "gather_wait_" + n)

    sstate, own, estate = {}, {}, []

    def put(n, g):
        ax = _BIG_AXIS[n]
        k = g.shape[ax] // N_CHIP
        own[n] = lax.dynamic_slice_in_dim(g, chip * k, k, axis=ax)
        sstate[n], tok = _scatter_start(g, ax, "scatter_start_" + n)
        return tok

    first5 = [n for n in _BIG if n != "w_in"]

    def early(sm):
        state, tok = _all8_start(_pack([sm[n] for n in _EARLY]), "small_start")
        estate.append(state)
        recv5 = [_scatter_wait(sstate[n], _BIG_AXIS[n], tok, "scatter_wait_" + n) for n in first5]
        held = [a for n, r in zip(first5, recv5) for a in (own[n], r)]
        state, tok = _swap_start(held, tok, "swap_start")
        estate.append(state)
        return tok

    loss8, dx, dn1, dmod = _device_step(x[0], mod, W, tables, loss_target[0], getw, put, early)

    held5, sib5 = _swap_wait(estate[1], dx, "swap_wait")
    outs = {}
    for i, n in enumerate(first5):
        outs[n] = _adamw(wts[n], mom[n], var[n], [held5[2 * i:2 * i + 2], sib5[2 * i:2 * i + 2]], "adamw_" + n)
    allp = _all8_wait(estate[0], dx, "small_wait")

    late = _allgather8(_pack([dn1, dmod, loss8]), "gather_late", after=[outs[n][1] for n in first5])
    n_late = _pack_rows((D_MODEL,)) + _pack_rows((6 * D_MODEL,))
    loss = jnp.sum(late[:, n_late:, :])
    late = late[:, :n_late, :]
    held_in = [own["w_in"], _scatter_wait(sstate["w_in"], _BIG_AXIS["w_in"], late, "scatter_wait_w_in")]
    sib_in = _swap_sibling(held_in)
    outs["w_in"] = _adamw(wts["w_in"], mom["w_in"], var["w_in"], [held_in, sib_in], "adamw_w_in")

    r1 = _pack_rows((D_MODEL,))
    dmod_all = late[:, r1:, :].reshape(N_DEV, -1)[:, :6 * D_MODEL]
    dmod_cols = lax.dynamic_slice(dmod_all, (0, chip * n_ada), (N_DEV, n_ada))
    g_ada = _ada_grad(c_all, dmod_cols)
    outs["w_ada"] = _adamw(wts["w_ada"], mom["w_ada"], var["w_ada"], [[g_ada]], "adamw_w_ada")

    upd, sums = _adamw_small(allp, _EARLY, wts, mom, var, ("conv_w",), "adamw_small")
    outs.update(upd)
    upd, _ = _adamw_small(late, _LATE, wts, mom, var, (), "adamw_late")
    outs.update(upd)
    g_cw_full = sums["conv_w"].reshape(-1)[:KW * CW].reshape(KW, CW)
    g_cw = lax.dynamic_slice(g_cw_full, (0, chip * (CW // N_CHIP)), (KW, CW // N_CHIP))
    pad = lambda a: jnp.pad(a, ((0, 1), (0, 0)))
    r_cw = _adamw(pad(wts["conv_w"]), pad(mom["conv_w"]), pad(var["conv_w"]), [[pad(g_cw)]], "adamw_conv_w")
    outs["conv_w"] = tuple(r[:KW] for r in r_cw)

    def shaped(n, a):
        return a.reshape(given[n].shape)

    result = [loss, dx[None]]
    for q in range(4):
        result += [shaped(n, outs[n][q]) for n in _ORDER]
    return tuple(result)
```

```python
import math

import jax
import jax.numpy as jnp
import numpy as np
from jax import lax
from jax.experimental import pallas as pl
from jax.experimental.pallas import tpu as pltpu

F32 = jnp.float32
BF16 = jnp.bfloat16
EPS = 1e-6
D_MODEL = 1024
CW = 512
KW = 31
HALO = 32
G, P, H = 32, 64, 16
NST = G * P
FH = 2816
N_DEV = 8
N_CHIP = 4
VMEM_LIMIT = 56 * 1024 * 1024
LR, B1, B2, AEPS, WD, STEP = 0.001, 0.9, 0.999, 1e-08, 0.01, 10
MESH = pl.DeviceIdType.MESH


def _cp(sem=None):
    return pltpu.CompilerParams(dimension_semantics=sem, vmem_limit_bytes=VMEM_LIMIT)


def _sig(x):
    return jax.nn.sigmoid(x)


def _full(shape):
    return pl.BlockSpec(shape, lambda *_: (0,) * len(shape))


def _resident(shape):
    return pl.BlockSpec(shape, lambda *_: (0,) * len(shape), pipeline_mode=pl.Buffered(1))


def _colsum8(v):
    t, c = v.shape
    return jnp.sum(v.reshape(t // 8, 8, c), axis=0)


def _matmul(a, b, mode, tm, tn, tk, out_dtype, name, after=None, n_outer=False, m_cols=None):
    m0 = 0
    b_parts = list(b) if isinstance(b, (list, tuple)) else [b]
    if mode == "nn":
        (M, K), N = a.shape, b.shape[1]
    elif mode == "nt":
        (M, K), N = a.shape, b.shape[0]
    else:
        (K, M), N = a.shape, sum(p.shape[1] for p in b_parts)
        if m_cols is not None:
            m0, M = m_cols
    tm, tn, tk = min(tm, M), min(tn, N), min(tk, K)
    assert M % tm == 0 and N % tn == 0 and K % tk == 0 and m0 % tm == 0, (name, M, N, K, tm, tn, tk)
    assert len(b_parts) == 1 or (mode == "tn" and all(p.shape[1] % tn == 0 for p in b_parts)), name
    nk = K // tk
    mb = m0 // tm
    counts = [p.shape[1] // tn for p in b_parts] if mode == "tn" else [N // tn]
    starts = [sum(counts[:p]) for p in range(len(counts))]

    def ij(fn):
        return (lambda j, i, k: fn(i, j, k)) if n_outer else fn

    if mode == "nn":
        a_spec = pl.BlockSpec((tm, tk), ij(lambda i, j, k: (i, k)))
        b_spec = pl.BlockSpec((tk, tn), ij(lambda i, j, k: (k, j)))
        dims = (((1,), (0,)), ((), ()))
    elif mode == "nt":
        a_spec = pl.BlockSpec((tm, tk), ij(lambda i, j, k: (i, k)))
        b_spec = pl.BlockSpec((tn, tk), ij(lambda i, j, k: (j, k)))
        dims = (((1,), (1,)), ((), ()))
    else:
        a_spec = pl.BlockSpec((tk, tm), ij(lambda i, j, k: (k, i + mb)))
        dims = (((0,), (0,)), ((), ()))
    if mode == "tn":
        b_specs = [pl.BlockSpec((tk, tn), ij(lambda i, j, k, s=s, n=n: (k, jnp.clip(j - s, 0, n - 1))))
                   for s, n in zip(starts, counts)]
    else:
        b_specs = [b_spec]
    nb = len(b_parts)

    def body(a_ref, *rest):
        b_refs = rest[:nb]
        o_ref, acc_ref = rest[-2:]
        j = pl.program_id(0 if n_outer else 1)
        k = pl.program_id(2)

        def compute(b_ref):
            part = lax.dot_general(a_ref[...].astype(BF16), b_ref[...].astype(BF16), dims,
                                   preferred_element_type=F32)
            if nk == 1:
                o_ref[...] = part.astype(out_dtype)
            else:
                @pl.when(k == 0)
                def _():
                    acc_ref[...] = part

                @pl.when(k > 0)
                def _():
                    acc_ref[...] += part

                @pl.when(k == nk - 1)
                def _():
                    o_ref[...] = acc_ref[...].astype(out_dtype)

        if nb == 1:
            compute(b_refs[0])
        else:
            for p in range(nb):
                pl.when(jnp.logical_and(j >= starts[p], j < starts[p] + counts[p]))(
                    lambda b_ref=b_refs[p]: compute(b_ref))

    return pl.pallas_call(
        body, name=name,
        out_shape=jax.ShapeDtypeStruct((M, N), out_dtype),
        grid=(N // tn, M // tm, nk) if n_outer else (M // tm, N // tn, nk),
        in_specs=[a_spec] + b_specs + ([] if after is None else [pl.BlockSpec(memory_space=pl.ANY)]),
        out_specs=pl.BlockSpec((tm, tn), ij(lambda i, j, k: (i, j))),
        scratch_shapes=[pltpu.VMEM((tm, tn) if nk > 1 else (8, 128), F32)],
        compiler_params=_cp(("parallel", "parallel", "arbitrary")),
    )(*([a] + b_parts + ([] if after is None else [after])))


def _row_tile(S):
    return min(512, S)


def _in_proj(x, g, sc, sh, w_in):
    S, D = x.shape
    N = w_in.shape[1]
    tm = min(512, S)

    def body(x_ref, g_ref, sc_ref, sh_ref, w_ref, h_ref, z_ref):
        xv = x_ref[...]
        r = lax.rsqrt(jnp.mean(xv * xv, axis=-1, keepdims=True) + EPS)
        h = (xv * r * (g_ref[...] * (1.0 + sc_ref[...])) + sh_ref[...]).astype(BF16)
        h_ref[...] = h
        z_ref[...] = jnp.dot(h, w_ref[...], preferred_element_type=F32).astype(BF16)

    row = pl.BlockSpec((tm, D), lambda i: (i, 0))
    par = _full((1, D))
    return pl.pallas_call(
        body, name="in_proj",
        out_shape=(jax.ShapeDtypeStruct((S, D), BF16), jax.ShapeDtypeStruct((S, N), BF16)), grid=(S // tm,),
        in_specs=[row, par, par, par, _resident((D, N))], out_specs=(row, pl.BlockSpec((tm, N), lambda i: (i, 0))),
        compiler_params=_cp(("parallel",)))(x, g, sc, sh, w_in)


def _fill_shifted(buf_ref, sh_ref):
    n = buf_ref.shape[0] - 8
    for s in range(1, 8):
        sh_ref[s, 0:n, :] = buf_ref[s:s + n, :]


def _window(buf_ref, sh_ref, off, n):
    s = off % 8
    return buf_ref[off:off + n, :] if s == 0 else sh_ref[s, off - s:off - s + n, :]


def _conv_fwd(z, conv_w, conv_b, ln_g, ln_b):
    S = z.shape[0]
    tm = min(128, S)
    sub = 32
    hb = tm // HALO

    def body(a_ref, g_ref, ha_ref, hg_ref, w_ref, b_ref, lg_ref, lb_ref, yc_ref, s_ref, ug_ref, sh_ref):
        i = pl.program_id(0)
        halo = ha_ref[...].astype(F32) * _sig(hg_ref[...].astype(F32))
        ug_ref[0:HALO, :] = jnp.where(i == 0, 0.0, halo)
        ug_ref[HALO:, :] = a_ref[...].astype(F32) * _sig(g_ref[...].astype(F32))
        _fill_shifted(ug_ref, sh_ref)
        for rb in range(tm // sub):
            acc = jnp.zeros((sub, CW), F32) + b_ref[...]
            for k in range(KW):
                off = rb * sub + HALO - (KW - 1) + k
                acc = acc + w_ref[k:k + 1, :] * _window(ug_ref, sh_ref, off, sub)
            yc_ref[rb * sub:(rb + 1) * sub, :] = acc
            mu = jnp.mean(acc, axis=-1, keepdims=True)
            cen = acc - mu
            rstd = lax.rsqrt(jnp.mean(cen * cen, axis=-1, keepdims=True) + EPS)
            ln = cen * rstd * lg_ref[...] + lb_ref[...]
            s_ref[rb * sub:(rb + 1) * sub, :] = (ln * _sig(ln)).astype(BF16)

    prev = lambda i: (jnp.maximum(i * hb - 1, 0), 0)
    return pl.pallas_call(
        body, name="conv_fwd",
        out_shape=(jax.ShapeDtypeStruct((S, CW), F32), jax.ShapeDtypeStruct((S, CW), BF16)),
        grid=(S // tm,),
        in_specs=[pl.BlockSpec((tm, CW), lambda i: (i, 0)), pl.BlockSpec((tm, CW), lambda i: (i, 1)),
                  pl.BlockSpec((HALO, CW), prev), pl.BlockSpec((HALO, CW), lambda i: (jnp.maximum(i * hb - 1, 0), 1)),
                  _full((KW, CW)), _full((1, CW)), _full((1, CW)), _full((1, CW))],
        out_specs=(pl.BlockSpec((tm, CW), lambda i: (i, 0)), pl.BlockSpec((tm, CW), lambda i: (i, 0))),
        scratch_shapes=[pltpu.VMEM((tm + HALO, CW), F32), pltpu.VMEM((8, tm + HALO, CW), F32)],
        compiler_params=_cp(("parallel",)))(z, z, z, z, conv_w, conv_b, ln_g, ln_b)


def _conv_bwd_ln(dyconv, w_cp, yc, ln_g, ln_b):
    S = yc.shape[0]
    tm = _row_tile(S)

    def body(dy_ref, w_ref, yc_ref, lg_ref, lb_ref, dyc_ref, dlg_ref, dlb_ref, dcb_ref):
        i = pl.program_id(0)
        dsc = lax.dot_general(dy_ref[...], w_ref[...], (((1,), (1,)), ((), ())), preferred_element_type=F32)
        yc_v = yc_ref[...]
        mu = jnp.mean(yc_v, axis=-1, keepdims=True)
        cen = yc_v - mu
        rstd = lax.rsqrt(jnp.mean(cen * cen, axis=-1, keepdims=True) + EPS)
        yn = cen * rstd
        ln = yn * lg_ref[...] + lb_ref[...]
        sl = _sig(ln)
        dln = dsc * (sl * (1.0 + ln * (1.0 - sl)))
        dyn = dln * lg_ref[...]
        dyc = rstd * (dyn - jnp.mean(dyn, axis=-1, keepdims=True)
                      - yn * jnp.mean(dyn * yn, axis=-1, keepdims=True))
        dyc_ref[...] = dyc

        @pl.when(i == 0)
        def _():
            dlg_ref[...] = jnp.zeros_like(dlg_ref)
            dlb_ref[...] = jnp.zeros_like(dlb_ref)
            dcb_ref[...] = jnp.zeros_like(dcb_ref)

        dlg_ref[...] += _colsum8(dln * yn)
        dlb_ref[...] += _colsum8(dln)
        dcb_ref[...] += _colsum8(dyc)

    row = pl.BlockSpec((tm, CW), lambda i: (i, 0))
    acc = jax.ShapeDtypeStruct((8, CW), F32)
    return pl.pallas_call(
        body, name="conv_bwd_ln",
        out_shape=(jax.ShapeDtypeStruct((S, CW), F32), acc, acc, acc), grid=(S // tm,),
        in_specs=[pl.BlockSpec((tm, D_MODEL), lambda i: (i, 0)), _full((CW, D_MODEL)), row, _full((1, CW)),
                  _full((1, CW))],
        out_specs=(row, _full((8, CW)), _full((8, CW)), _full((8, CW))),
        compiler_params=_cp(("arbitrary",)))(dyconv, w_cp, yc, ln_g, ln_b)


def _conv_bwd(dyc, z, conv_w):
    S = z.shape[0]
    tm = min(128, S)
    sub = 32
    hb = tm // HALO
    nt = S // tm

    def body(d_ref, dn_ref, a_ref, g_ref, ha_ref, hg_ref, w_ref, dz_ref, dw_ref, ug_ref, dy_ref, ugs_ref, dys_ref):
        i = pl.program_id(0)
        halo = ha_ref[...].astype(F32) * _sig(hg_ref[...].astype(F32))
        ug_ref[0:HALO, :] = jnp.where(i == 0, 0.0, halo)
        a = a_ref[...].astype(F32)
        sg = _sig(g_ref[...].astype(F32))
        ug_ref[HALO:, :] = a * sg
        dy_ref[0:tm, :] = d_ref[...]
        dy_ref[tm:, :] = jnp.where(i == nt - 1, 0.0, dn_ref[...])
        _fill_shifted(ug_ref, ugs_ref)
        _fill_shifted(dy_ref, dys_ref)

        @pl.when(i == 0)
        def _():
            dw_ref[...] = jnp.zeros_like(dw_ref)

        for rb in range(tm // sub):
            r0 = rb * sub
            acc = jnp.zeros((sub, CW), F32)
            dyc_b = dy_ref[r0:r0 + sub, :]
            for k in range(KW):
                up = r0 + (KW - 1) - k
                acc = acc + w_ref[k:k + 1, :] * _window(dy_ref, dys_ref, up, sub)
                off = r0 + HALO - (KW - 1) + k
                dw_ref[k * 8:(k + 1) * 8, :] += _colsum8(dyc_b * _window(ug_ref, ugs_ref, off, sub))
            a_b = a[r0:r0 + sub, :]
            sg_b = sg[r0:r0 + sub, :]
            dz_ref[r0:r0 + sub, 0:CW] = (acc * sg_b).astype(BF16)
            dz_ref[r0:r0 + sub, CW:2 * CW] = (acc * a_b * sg_b * (1.0 - sg_b)).astype(BF16)

    return pl.pallas_call(
        body, name="conv_bwd",
        out_shape=(jax.ShapeDtypeStruct((S, 2 * CW), BF16), jax.ShapeDtypeStruct((KW * 8, CW), F32)),
        grid=(nt,),
        in_specs=[pl.BlockSpec((tm, CW), lambda i: (i, 0)),
                  pl.BlockSpec((HALO, CW), lambda i: (jnp.minimum((i + 1) * hb, nt * hb - 1), 0)),
                  pl.BlockSpec((tm, CW), lambda i: (i, 0)), pl.BlockSpec((tm, CW), lambda i: (i, 1)),
                  pl.BlockSpec((HALO, CW), lambda i: (jnp.maximum(i * hb - 1, 0), 0)),
                  pl.BlockSpec((HALO, CW), lambda i: (jnp.maximum(i * hb - 1, 0), 1)),
                  _full((KW, CW))],
        out_specs=(pl.BlockSpec((tm, 2 * CW), lambda i: (i, 0)), _full((KW * 8, CW))),
        scratch_shapes=[pltpu.VMEM((tm + HALO, CW), F32), pltpu.VMEM((tm + HALO, CW), F32),
                        pltpu.VMEM((8, tm + HALO, CW), F32), pltpu.VMEM((8, tm + HALO, CW), F32)],
        compiler_params=_cp(("arbitrary",)))(dyc, dyc, z, z, z, z, conv_w)


_GELU_C = math.sqrt(2.0 / math.pi)


def _gelu(x):
    return 0.5 * x * (1.0 + jnp.tanh(_GELU_C * (x + 0.044715 * x * x * x)))


def _gelu_grad(x):
    t = jnp.tanh(_GELU_C * (x + 0.044715 * x * x * x))
    return 0.5 * (1.0 + t) + 0.5 * x * (1.0 - t * t) * (_GELU_C * (1.0 + 3 * 0.044715 * x * x))


_NCL = 4
_UC = CW // _NCL
_LW = NST // _NCL
_CS = 2 * _LW


def _ssm_fwd(z, bb, cm, d, tab):
    S = z.shape[0]
    tm = min(512, S)

    def body(u_ref, bb_ref, cm_ref, d_ref, t_ref, x_ref, ys_ref, yg_ref, car_ref):
        i = pl.program_id(0)

        @pl.when(i == 0)
        def _():
            car_ref[...] = jnp.zeros_like(car_ref)

        u16 = u_ref[...]
        u = u16.astype(F32)
        for c in range(_NCL):
            lre = pl.ds(c * _CS, _LW)
            lim = pl.ds(c * _CS + _LW, _LW)
            tl = pl.ds(c * _LW, _LW)
            x_ref[:, c * _CS:(c + 1) * _CS] = jnp.dot(u16[:, c * _UC:(c + 1) * _UC], bb_ref[c],
                                                      preferred_element_type=F32)

            def blk(j, car):
                cr, ci = car
                rows = pl.ds(pl.multiple_of(j * 8, 8), 8)
                r = x_ref[rows, lre]
                im = x_ref[rows, lim]
                for lvl, s in enumerate((1, 2, 4)):
                    mr = t_ref[16 * lvl:16 * lvl + 8, tl]
                    mi = t_ref[16 * lvl + 8:16 * lvl + 16, tl]
                    sr = pltpu.roll(r, s, 0)
                    si = pltpu.roll(im, s, 0)
                    r, im = r + (mr * sr - mi * si), im + (mr * si + mi * sr)
                pr = t_ref[48:56, tl]
                pi_ = t_ref[56:64, tl]
                r, im = r + (pr * cr - pi_ * ci), im + (pr * ci + pi_ * cr)
                x_ref[rows, lre] = r
                x_ref[rows, lim] = im
                return (jnp.broadcast_to(r[7:8, :], (8, _LW)), jnp.broadcast_to(im[7:8, :], (8, _LW)))

            cr, ci = lax.fori_loop(0, tm // 8, blk, (car_ref[:, lre], car_ref[:, lim]))
            car_ref[:, lre] = cr
            car_ref[:, lim] = ci
            cols = slice(c * _UC, (c + 1) * _UC)
            ys = jnp.dot(x_ref[:, c * _CS:(c + 1) * _CS].astype(BF16), cm_ref[c], preferred_element_type=F32)
            ys = ys + d_ref[:, cols] * u[:, cols]
            ys_ref[:, cols] = ys
            yg_ref[:, cols] = _gelu(ys).astype(BF16)

    return pl.pallas_call(
        body, name="ssm_fwd",
        out_shape=(jax.ShapeDtypeStruct((S, 2 * NST), F32), jax.ShapeDtypeStruct((S, CW), F32),
                   jax.ShapeDtypeStruct((S, CW), BF16)),
        grid=(S // tm,),
        in_specs=[pl.BlockSpec((tm, CW), lambda i: (i, 2)), _full((_NCL, _UC, _CS)), _full((_NCL, _CS, _UC)),
                  _full((1, CW)), _full((64, NST))],
        out_specs=(pl.BlockSpec((tm, 2 * NST), lambda i: (i, 0)), pl.BlockSpec((tm, CW), lambda i: (i, 0)),
                   pl.BlockSpec((tm, CW), lambda i: (i, 0))),
        scratch_shapes=[pltpu.VMEM((8, 2 * NST), F32)],
        compiler_params=_cp(("arbitrary",)))(z, bb, cm, d, tab)


def _ssm_bwd(dzz, w_glu, ys, z, xs, cmt, bbt, d, tab, after):
    S = z.shape[0]
    tm = min(512, S)
    nt = S // tm
    tdims = (((0,), (0,)), ((), ()))

    def body(dzz_ref, wglu_ref, ys_ref, u_ref, x_ref, cmt_ref, bbt_ref, d_ref, t_ref, after_ref,
             du_ref, de_ref, dd_ref, dc_hbm, dbb_hbm, car_ref, lam_ref, dc_ref, dbb_ref):
        i = pl.program_id(0)

        @pl.when(i == 0)
        def _():
            car_ref[...] = jnp.zeros_like(car_ref)
            de_ref[...] = jnp.zeros_like(de_ref)
            dd_ref[...] = jnp.zeros_like(dd_ref)
            dc_ref[...] = jnp.zeros_like(dc_ref)
            dbb_ref[...] = jnp.zeros_like(dbb_ref)

        u16 = u_ref[...]
        u = u16.astype(F32)
        dyg = lax.dot_general(dzz_ref[...], wglu_ref[...], (((1,), (1,)), ((), ())), preferred_element_type=F32)
        dys = dyg * _gelu_grad(ys_ref[...])
        dys16 = dys.astype(BF16)
        dd_ref[...] += _colsum8(dys * u)
        row = lax.broadcasted_iota(jnp.int32, (8, _LW), 0)
        for c in range(_NCL):
            lre = pl.ds(c * _CS, _LW)
            lim = pl.ds(c * _CS + _LW, _LW)
            tl = pl.ds(c * _LW, _LW)
            cols = slice(c * _UC, (c + 1) * _UC)
            span = slice(c * _CS, (c + 1) * _CS)
            dc_ref[cols, :] += lax.dot_general(dys16[:, cols], x_ref[:, span].astype(BF16), tdims,
                                               preferred_element_type=F32)
            lam_ref[...] = jnp.dot(dys16[:, cols], cmt_ref[c], preferred_element_type=F32)

            def blk(jj, car):
                cr, ci, ar, ai = car
                j = tm // 8 - 1 - jj
                rows = pl.ds(pl.multiple_of(j * 8, 8), 8)
                r = lam_ref[rows, 0:_LW]
                im = lam_ref[rows, _LW:_CS]
                for lvl, s in enumerate((1, 2, 4)):
                    mr = t_ref[16 * lvl:16 * lvl + 8, tl]
                    mi = t_ref[16 * lvl + 8:16 * lvl + 16, tl]
                    sr = pltpu.roll(r, 8 - s, 0)
                    si = pltpu.roll(im, 8 - s, 0)
                    r, im = r + (mr * sr - mi * si), im + (mr * si + mi * sr)
                pr = t_ref[48:56, tl]
                pi_ = t_ref[56:64, tl]
                r, im = r + (pr * cr - pi_ * ci), im + (pr * ci + pi_ * cr)
                lam_ref[rows, 0:_LW] = r
                lam_ref[rows, _LW:_CS] = im
                nr = jnp.where(row == 7, cr, pltpu.roll(r, 7, 0))
                ni = jnp.where(row == 7, ci, pltpu.roll(im, 7, 0))
                xr = x_ref[rows, lre]
                xi = x_ref[rows, lim]
                ar = ar + (nr * xr + ni * xi)
                ai = ai + (ni * xr - nr * xi)
                return (jnp.broadcast_to(r[0:1, :], (8, _LW)), jnp.broadcast_to(im[0:1, :], (8, _LW)), ar, ai)

            zero = jnp.zeros((8, _LW), F32)
            cr, ci, ar, ai = lax.fori_loop(0, tm // 8, blk, (car_ref[:, lre], car_ref[:, lim], zero, zero))
            car_ref[:, lre] = cr
            car_ref[:, lim] = ci
            de_ref[0:8, tl] += ar
            de_ref[8:16, tl] += ai
            lam16 = lam_ref[...].astype(BF16)
            dbb_ref[cols, :] += lax.dot_general(u16[:, cols], lam16, tdims, preferred_element_type=F32)
            du = jnp.dot(lam16, bbt_ref[c], preferred_element_type=F32) + dys[:, cols] * d_ref[:, cols]
            du_ref[:, cols] = du.astype(BF16)

        @pl.when(i == nt - 1)
        def _():
            pltpu.sync_copy(dc_ref, dc_hbm)
            pltpu.sync_copy(dbb_ref, dbb_hbm)

    rev = lambda i: (nt - 1 - i, 0)
    once = lambda shape: pl.BlockSpec(shape, lambda *_: (0,) * len(shape), pipeline_mode=pl.Buffered(1))
    cross = jax.ShapeDtypeStruct((CW, _CS), F32)
    return pl.pallas_call(
        body, name="ssm_bwd",
        out_shape=(jax.ShapeDtypeStruct((S, CW), BF16), jax.ShapeDtypeStruct((16, NST), F32),
                   jax.ShapeDtypeStruct((8, CW), F32), cross, cross),
        grid=(nt,),
        in_specs=[pl.BlockSpec((tm, 2 * D_MODEL), rev), once((CW, 2 * D_MODEL)), pl.BlockSpec((tm, CW), rev),
                  pl.BlockSpec((tm, CW), lambda i: (nt - 1 - i, 2)), pl.BlockSpec((tm, 2 * NST), rev),
                  once((_NCL, _UC, _CS)), once((_NCL, _CS, _UC)), _full((1, CW)), once((64, NST)),
                  pl.BlockSpec(memory_space=pl.ANY)],
        out_specs=(pl.BlockSpec((tm, CW), rev), _full((16, NST)), _full((8, CW)),
                   pl.BlockSpec(memory_space=pl.ANY), pl.BlockSpec(memory_space=pl.ANY)),
        scratch_shapes=[pltpu.VMEM((8, 2 * NST), F32), pltpu.VMEM((tm, _CS), F32),
                        pltpu.VMEM((CW, _CS), F32), pltpu.VMEM((CW, _CS), F32)],
        compiler_params=_cp(("arbitrary",)))(dzz, w_glu, ys, z, xs, cmt, bbt, d, tab, after)


def _ssm_prep(a_re, a_im, b_re, b_im, log_dt):
    dt = jnp.exp(log_dt.reshape(G))[:, None]
    mag = jnp.exp(dt * a_re)
    e_re, e_im = mag * jnp.cos(dt * a_im), mag * jnp.sin(dt * a_im)
    n_re, n_im = e_re - 1.0, e_im
    den = a_re * a_re + a_im * a_im
    q_re = (n_re * a_re + n_im * a_im) / den
    q_im = (n_im * a_re - n_re * a_im) / den
    bb_re = q_re[..., None] * b_re - q_im[..., None] * b_im
    bb_im = q_re[..., None] * b_im + q_im[..., None] * b_re
    return e_re, e_im, bb_re, bb_im


def _scan_tables(e_re, e_im, reverse):
    er = e_re.reshape(1, NST)
    ei = e_im.reshape(1, NST)
    if reverse:
        ei = -ei
    pows = [(er, ei)]
    for _ in range(7):
        pr, pi_ = pows[-1]
        pows.append((pr * er - pi_ * ei, pr * ei + pi_ * er))
    row = jnp.arange(8)[:, None]
    out = []
    for s in (1, 2, 4):
        pr, pi_ = pows[s - 1]
        keep = (row + s <= 7) if reverse else (row >= s)
        out += [jnp.where(keep, pr, 0.0), jnp.where(keep, pi_, 0.0)]
    allr = jnp.concatenate([p[0] for p in pows], 0)
    alli = jnp.concatenate([p[1] for p in pows], 0)
    if reverse:
        allr, alli = allr[::-1], alli[::-1]
    out += [allr, alli]
    return jnp.concatenate(out, 0).astype(F32)


def _block_diag_mats(bb_re, bb_im, c_re, c_im):
    gc = G // _NCL
    eye = jnp.eye(gc, dtype=F32)
    bre = jnp.einsum("cjph,jk->cjhkp", bb_re.reshape(_NCL, gc, P, H), eye).reshape(_NCL, _UC, _LW)
    bim = jnp.einsum("cjph,jk->cjhkp", bb_im.reshape(_NCL, gc, P, H), eye).reshape(_NCL, _UC, _LW)
    bb = jnp.concatenate([bre, bim], 2)
    cre = jnp.einsum("cjhp,jk->cjpkh", c_re.reshape(_NCL, gc, H, P), eye).reshape(_NCL, _LW, _UC)
    cim = jnp.einsum("cjhp,jk->cjpkh", c_im.reshape(_NCL, gc, H, P), eye).reshape(_NCL, _LW, _UC)
    cm = jnp.concatenate([cre, -cim], 1)
    return bb, cm


def _diag_blocks(cross, imag):
    gc = G // _NCL
    off = _LW if imag else 0
    return jnp.stack([cross[H * g:H * (g + 1), off + P * (g % gc):off + P * (g % gc + 1)] for g in range(G)])


def _mix_fwd(scv, yg, z, x, w_cp, w_glu, w_out, g1, n2g, sc2, sh2):
    S = z.shape[0]
    tm = min(512, S)
    D = D_MODEL

    def body(s_ref, yg_ref, glc0_ref, glc1_ref, gls0_ref, gls1_ref, x_ref, wcp_ref, wglu_ref, wout_ref,
             g1_ref, n2_ref, sc_ref, sh_ref, yc_ref, zz_ref, m_ref, o_ref, x2_ref, h2_ref):
        y_conv = jnp.dot(s_ref[...], wcp_ref[...], preferred_element_type=F32)
        zz = jnp.dot(yg_ref[...], wglu_ref[...], preferred_element_type=F32)
        yc_ref[...] = y_conv.astype(BF16)
        zz_ref[...] = zz.astype(BF16)
        for half, (glc_ref, gls_ref) in enumerate(((glc0_ref, gls0_ref), (glc1_ref, gls1_ref))):
            lo, hi = half * CW, (half + 1) * CW
            y_ssm = zz[:, lo:hi] * _sig(zz[:, D + lo:D + hi])
            m_ref[:, lo:hi] = (_sig(glc_ref[...].astype(F32)) * y_conv[:, lo:hi]
                               + _sig(gls_ref[...].astype(F32)) * y_ssm).astype(BF16)
        o = jnp.dot(m_ref[...], wout_ref[...], preferred_element_type=F32)
        o_ref[...] = o.astype(BF16)
        xv = x_ref[...] + g1_ref[...] * o
        x2_ref[...] = xv
        r = lax.rsqrt(jnp.mean(xv * xv, axis=-1, keepdims=True) + EPS)
        h2_ref[...] = (xv * r * (n2_ref[...] * (1.0 + sc_ref[...])) + sh_ref[...]).astype(BF16)

    zb_ = lambda j: pl.BlockSpec((tm, CW), lambda i: (i, j))
    row = lambda w: pl.BlockSpec((tm, w), lambda i: (i, 0))
    par = _full((1, D))
    bf = lambda w: jax.ShapeDtypeStruct((S, w), BF16)
    return pl.pallas_call(
        body, name="mix_fwd",
        out_shape=(bf(D), bf(2 * D), bf(D), bf(D), jax.ShapeDtypeStruct((S, D), F32), bf(D)),
        grid=(S // tm,),
        in_specs=[row(CW), row(CW), zb_(3), zb_(4), zb_(5), zb_(6), row(D), _resident((CW, D)),
                  _resident((CW, 2 * D)), _resident((D, D)), par, par, par, par],
        out_specs=(row(D), row(2 * D), row(D), row(D), row(D), row(D)),
        compiler_params=_cp(("parallel",)))(scv, yg, z, z, z, z, x, w_cp, w_glu, w_out, g1, n2g, sc2, sh2)


def _mix_bwd(do, w_out, z, zz, y_conv, after):
    S = z.shape[0]
    tm = min(512, S)
    D = D_MODEL

    def body(do_ref, w_ref, glc0_ref, glc1_ref, gls0_ref, gls1_ref, za_ref, zb_ref, yc_ref, after_ref,
             dyc_ref, dgl_ref, dzz_ref):
        dm = lax.dot_general(do_ref[...], w_ref[...], (((1,), (1,)), ((), ())), preferred_element_type=F32)
        for half, (glc_ref, gls_ref) in enumerate(((glc0_ref, gls0_ref), (glc1_ref, gls1_ref))):
            lo, hi = half * CW, (half + 1) * CW
            dm_v = dm[:, lo:hi]
            sgc = _sig(glc_ref[...].astype(F32))
            sgs = _sig(gls_ref[...].astype(F32))
            szb = _sig(zb_ref[:, lo:hi].astype(F32))
            za = za_ref[:, lo:hi].astype(F32)
            dyc_ref[:, lo:hi] = (dm_v * sgc).astype(BF16)
            dgl_ref[:, lo:hi] = (dm_v * yc_ref[:, lo:hi].astype(F32) * sgc * (1.0 - sgc)).astype(BF16)
            dys = dm_v * sgs
            dgl_ref[:, D + lo:D + hi] = (dys * (za * szb) * (1.0 - sgs)).astype(BF16)
            dzz_ref[:, lo:hi] = (dys * szb).astype(BF16)
            dzz_ref[:, D + lo:D + hi] = (dys * za * szb * (1.0 - szb)).astype(BF16)

    zb_ = lambda j: pl.BlockSpec((tm, CW), lambda i: (i, j))
    wide = lambda j: pl.BlockSpec((tm, D), lambda i: (i, j))
    return pl.pallas_call(
        body, name="mix_bwd",
        out_shape=(jax.ShapeDtypeStruct((S, D), BF16), jax.ShapeDtypeStruct((S, 2 * D), BF16),
                   jax.ShapeDtypeStruct((S, 2 * D), BF16)),
        grid=(S // tm,),
        in_specs=[wide(0), _resident((D, D)), zb_(3), zb_(4), zb_(5), zb_(6), wide(0), wide(1), wide(0),
                  pl.BlockSpec(memory_space=pl.ANY)],
        out_specs=(wide(0), pl.BlockSpec((tm, 2 * D), lambda i: (i, 0)), pl.BlockSpec((tm, 2 * D), lambda i: (i, 0))),
        compiler_params=_cp(("parallel",)))(do, w_out, z, z, z, z, zz, zz, y_conv, after)


_FC = 1408


def _ffn_in_act(h2, w_fi):
    S, D = h2.shape
    tm = min(512, S)

    def body(h_ref, w_ref, f_ref, a_ref):
        hv = h_ref[...]
        for c in range(FH // _FC):
            lo, hi = c * _FC, (c + 1) * _FC
            g = jnp.dot(hv, w_ref[:, lo:hi], preferred_element_type=F32)
            u = jnp.dot(hv, w_ref[:, FH + lo:FH + hi], preferred_element_type=F32)
            f_ref[:, lo:hi] = g.astype(BF16)
            f_ref[:, FH + lo:FH + hi] = u.astype(BF16)
            a_ref[:, lo:hi] = (g * _sig(g) * u).astype(BF16)

    return pl.pallas_call(
        body, name="ffn_in_act",
        out_shape=(jax.ShapeDtypeStruct((S, 2 * FH), BF16), jax.ShapeDtypeStruct((S, FH), BF16)),
        grid=(S // tm,),
        in_specs=[pl.BlockSpec((tm, D), lambda i: (i, 0)), _resident((D, 2 * FH))],
        out_specs=(pl.BlockSpec((tm, 2 * FH), lambda i: (i, 0)), pl.BlockSpec((tm, FH), lambda i: (i, 0))),
        compiler_params=_cp(("parallel",)))(h2, w_fi)


def _ffn_bwd(do2, w_fo, f, after):
    S, D = do2.shape
    tm = min(512, S)

    def body(d_ref, w_ref, f_ref, after_ref, df_ref):
        dv = d_ref[...]
        for c in range(FH // _FC):
            lo, hi = c * _FC, (c + 1) * _FC
            dact = lax.dot_general(dv, w_ref[lo:hi, :], (((1,), (1,)), ((), ())), preferred_element_type=F32)
            g = f_ref[:, lo:hi].astype(F32)
            u = f_ref[:, FH + lo:FH + hi].astype(F32)
            sg = _sig(g)
            df_ref[:, lo:hi] = (dact * u * (sg * (1.0 + g * (1.0 - sg)))).astype(BF16)
            df_ref[:, FH + lo:FH + hi] = (dact * g * sg).astype(BF16)

    return pl.pallas_call(
        body, name="ffn_bwd", out_shape=jax.ShapeDtypeStruct((S, 2 * FH), BF16), grid=(S // tm,),
        in_specs=[pl.BlockSpec((tm, D), lambda i: (i, 0)), _resident((FH, D)),
                  pl.BlockSpec((tm, 2 * FH), lambda i: (i, 0)), pl.BlockSpec(memory_space=pl.ANY)],
        out_specs=pl.BlockSpec((tm, 2 * FH), lambda i: (i, 0)),
        compiler_params=_cp(("parallel",)))(do2, w_fo, f, after)


def _ffn_out_final(x2, act, w_fo, g2, fg, tgt):
    S, D = x2.shape
    tm = min(512, S)

    def body(x2_ref, a_ref, w_ref, g2_ref, fg_ref, t_ref, dx3_ref, do2_ref, ls_ref, dfg_ref, dg2_ref):
        i = pl.program_id(0)

        @pl.when(i == 0)
        def _():
            ls_ref[...] = jnp.zeros_like(ls_ref)
            dfg_ref[...] = jnp.zeros_like(dfg_ref)
            dg2_ref[...] = jnp.zeros_like(dg2_ref)

        o2 = jnp.dot(a_ref[...], w_ref[...], preferred_element_type=F32)
        x3 = x2_ref[...] + g2_ref[...] * o2
        r = lax.rsqrt(jnp.mean(x3 * x3, axis=-1, keepdims=True) + EPS)
        xn = x3 * r
        err = xn * fg_ref[...] - t_ref[...]
        dy = err * (1.0 / D)
        dxn = dy * fg_ref[...]
        dx3 = r * (dxn - xn * jnp.mean(dxn * xn, axis=-1, keepdims=True))
        dx3_ref[...] = dx3
        do2_ref[...] = (dx3 * g2_ref[...]).astype(BF16)
        e2 = _colsum8(err * err)
        lanes = e2[:, 0:128]
        for q in range(1, D // 128):
            lanes = lanes + e2[:, q * 128:(q + 1) * 128]
        ls_ref[...] += lanes * (0.5 / D)
        dfg_ref[...] += _colsum8(dy * xn)
        dg2_ref[...] += _colsum8(dx3 * o2)

    row = pl.BlockSpec((tm, D), lambda i: (i, 0))
    par = _full((1, D))
    return pl.pallas_call(
        body, name="final_loss",
        out_shape=(jax.ShapeDtypeStruct((S, D), F32), jax.ShapeDtypeStruct((S, D), BF16),
                   jax.ShapeDtypeStruct((8, 128), F32), jax.ShapeDtypeStruct((8, D), F32),
                   jax.ShapeDtypeStruct((8, D), F32)),
        grid=(S // tm,), in_specs=[row, pl.BlockSpec((tm, FH), lambda i: (i, 0)), _resident((FH, D)), par, par, row],
        out_specs=(row, row, _full((8, 128)), _full((8, D)), _full((8, D))),
        compiler_params=_cp(("arbitrary",)))(x2, act, w_fo, g2, fg, tgt)


def _normmod_bwd(dsrc, w, xin, dres, g, sc, gate, o, after, name):
    S, D = xin.shape
    parts = list(dsrc) if isinstance(dsrc, (list, tuple)) else [dsrc]
    widths = [p.shape[1] for p in parts]
    K = sum(widths)
    tm = min(512, S)
    npart = len(parts)

    def body(*refs):
        ds_refs = refs[:npart]
        w_ref, x_ref, dr_ref, g_ref, sc_ref, gate_ref, o_ref, after_ref = refs[npart:npart + 8]
        dx_ref, do_ref, dsh_ref, dsc_ref, dg_ref, dgate_ref = refs[npart + 8:]
        i = pl.program_id(0)

        @pl.when(i == 0)
        def _():
            dsh_ref[...] = jnp.zeros_like(dsh_ref)
            dsc_ref[...] = jnp.zeros_like(dsc_ref)
            dg_ref[...] = jnp.zeros_like(dg_ref)
            dgate_ref[...] = jnp.zeros_like(dgate_ref)

        gv = g_ref[...]
        scale = 1.0 + sc_ref[...]
        xv = x_ref[...]
        r = lax.rsqrt(jnp.mean(xv * xv, axis=-1, keepdims=True) + EPS)
        xn = xv * r
        dh_v, col = None, 0
        for ds_ref, wd in zip(ds_refs, widths):
            t = lax.dot_general(ds_ref[...], w_ref[:, col:col + wd], (((1,), (1,)), ((), ())),
                                preferred_element_type=F32)
            dh_v = t if dh_v is None else dh_v + t
            col += wd
        dxn = dh_v * (gv * scale)
        dx = dr_ref[...] + r * (dxn - xn * jnp.mean(dxn * xn, axis=-1, keepdims=True))
        dx_ref[...] = dx
        do_ref[...] = (dx * gate_ref[...]).astype(BF16)
        hx = dh_v * xn
        dsh_ref[...] += _colsum8(dh_v)
        dsc_ref[...] += _colsum8(hx) * gv
        dg_ref[...] += _colsum8(hx) * scale
        dgate_ref[...] += _colsum8(dx * o_ref[...])

    row = pl.BlockSpec((tm, D), lambda i: (i, 0))
    par = _full((1, D))
    acc = jax.ShapeDtypeStruct((8, D), F32)
    return pl.pallas_call(
        body, name=name,
        out_shape=(jax.ShapeDtypeStruct((S, D), F32), jax.ShapeDtypeStruct((S, D), BF16), acc, acc, acc, acc),
        grid=(S // tm,),
        in_specs=[pl.BlockSpec((tm, wd), lambda i: (i, 0)) for wd in widths]
        + [_resident((D, K)), row, row, par, par, par, row, pl.BlockSpec(memory_space=pl.ANY)],
        out_specs=(row, row, _full((8, D)), _full((8, D)), _full((8, D)), _full((8, D))),
        compiler_params=_cp(("arbitrary",)))(*parts, w, xin, dres, g, sc, gate, o, after)


def _me():
    return lax.axis_index("x"), lax.axis_index("y"), lax.axis_index("c")


def _allgather8(v, name, after=()):
    R, C = v.shape
    after = list(after)

    def body(v_ref, *rest):
        out_ref, send_sems, recv_sems, local_sem = rest[len(after):]
        x, y, c = _me()
        mine = pltpu.make_async_copy(v_ref, out_ref.at[4 * x + 2 * y + c], local_sem)
        mine.start()
        copies = []
        for k in range(1, N_DEV):
            fx, fy, fc = (k >> 2) & 1, (k >> 1) & 1, k & 1
            peer = (x ^ fx, y ^ fy, c ^ fc)
            copies.append(pltpu.make_async_remote_copy(
                src_ref=v_ref, dst_ref=out_ref.at[4 * x + 2 * y + c],
                send_sem=send_sems.at[k - 1], recv_sem=recv_sems.at[k - 1],
                device_id=peer, device_id_type=MESH))
        for cp in copies:
            cp.start()
        for k in range(1, N_DEV):
            fx, fy, fc = (k >> 2) & 1, (k >> 1) & 1, k & 1
            src_slot = 4 * (x ^ fx) + 2 * (y ^ fy) + (c ^ fc)
            pltpu.make_async_remote_copy(
                src_ref=v_ref, dst_ref=out_ref.at[src_slot],
                send_sem=send_sems.at[k - 1], recv_sem=recv_sems.at[k - 1],
                device_id=(x ^ fx, y ^ fy, c ^ fc), device_id_type=MESH).wait_recv()
        for cp in copies:
            cp.wait_send()
        mine.wait()

    return pl.pallas_call(
        body, name=name, out_shape=jax.ShapeDtypeStruct((N_DEV, R, C), v.dtype),
        in_specs=[pl.BlockSpec(memory_space=pltpu.VMEM)] + [pl.BlockSpec(memory_space=pl.ANY)] * len(after),
        out_specs=pl.BlockSpec(memory_space=pltpu.VMEM),
        scratch_shapes=[pltpu.SemaphoreType.DMA((N_DEV - 1,)), pltpu.SemaphoreType.DMA((N_DEV - 1,)),
                        pltpu.SemaphoreType.DMA],
        compiler_params=pltpu.CompilerParams(vmem_limit_bytes=VMEM_LIMIT))(v, *after)


_HBM = pl.BlockSpec(memory_space=pltpu.HBM)
_SEM = pl.BlockSpec(memory_space=pltpu.SEMAPHORE)
_EFFECT = pltpu.SideEffectType.DATAFLOW_SIDE_EFFECTING
_N_PEER = N_CHIP - 1


def _chip_part(ref, axis, n, chip):
    start = pl.multiple_of(chip * n, 8)
    return ref.at[pl.ds(start, n), :] if axis == 0 else ref.at[:, pl.ds(start, n)]


def _gather_copy(k, src_ref, land_ref, send_sems, recv_sems, axis, arriving):
    x, y, c = _me()
    px, py = x ^ ((k >> 1) & 1), y ^ (k & 1)
    chip = 2 * px + py if arriving else 2 * x + y
    return pltpu.make_async_remote_copy(
        src_ref=src_ref, dst_ref=_chip_part(land_ref, axis, src_ref.shape[axis], chip),
        send_sem=send_sems.at[k - 1], recv_sem=recv_sems.at[k - 1], device_id=(px, py, c), device_id_type=MESH)


def _scatter_copy(k, grad_ref, land_ref, send_sems, recv_sems, axis):
    x, y, c = _me()
    px, py = x ^ ((k >> 1) & 1), y ^ (k & 1)
    return pltpu.make_async_remote_copy(
        src_ref=_chip_part(grad_ref, axis, grad_ref.shape[axis] // N_CHIP, 2 * px + py), dst_ref=land_ref.at[k],
        send_sem=send_sems.at[k - 1], recv_sem=recv_sems.at[k - 1], device_id=(px, py, c), device_id_type=MESH)


def _scatter_own(grad_ref, land_ref, send_sems, axis):
    x, y, _ = _me()
    return pltpu.make_async_copy(_chip_part(grad_ref, axis, grad_ref.shape[axis] // N_CHIP, 2 * x + y),
                                 land_ref.at[0], send_sems.at[_N_PEER])


def _own_copy(src_ref, land_ref, sends, axis):
    x, y, _ = _me()
    return pltpu.make_async_copy(src_ref, _chip_part(land_ref, axis, src_ref.shape[axis], 2 * x + y),
                                 sends.at[_N_PEER])


def _gather_start(shards, axes, after, name):
    nw = len(shards)
    lands = []
    for s, ax in zip(shards, axes):
        shp = list(s.shape)
        shp[ax] *= N_CHIP
        lands.append(lax.empty(tuple(shp), s.dtype))

    def body(*refs):
        srcs, zones = refs[:nw], refs[nw:2 * nw]
        sends, recvs = refs[2 * nw + 1:3 * nw + 1], refs[3 * nw + 1:4 * nw + 1]
        token = refs[-1]
        for w in range(nw):
            for k in range(1, N_CHIP):
                _gather_copy(k, srcs[w], zones[w], sends[w], recvs[w], axes[w], False).start()
        for w in range(nw):
            _own_copy(srcs[w], zones[w], sends[w], axes[w]).start()
        token[...] = jnp.zeros_like(token)

    outs = pl.pallas_call(
        body, name=name,
        out_shape=tuple([pltpu.SemaphoreType.DMA((_N_PEER + 1,))] * nw + [pltpu.SemaphoreType.DMA((_N_PEER,))] * nw
                        + [pltpu.HBM(a.shape, a.dtype) for a in list(shards) + list(lands)]
                        + [jax.ShapeDtypeStruct((8, 128), F32)]),
        in_specs=[_HBM] * (2 * nw) + [pl.BlockSpec(memory_space=pl.ANY)],
        out_specs=tuple([_SEM] * (2 * nw) + [_HBM] * (2 * nw) + [pl.BlockSpec(memory_space=pltpu.VMEM)]),
        input_output_aliases={i: 2 * nw + i for i in range(2 * nw)},
        compiler_params=pltpu.CompilerParams(has_side_effects=_EFFECT),
    )(*([pltpu.with_memory_space_constraint(a, pltpu.HBM) for a in list(shards) + list(lands)] + [after]))
    per_weight = [(outs[w], outs[nw + w], outs[2 * nw + w], outs[3 * nw + w]) for w in range(nw)]
    return per_weight, outs[-1]


def _gather_wait(state, axis, after, name):
    send_sems, recv_sems, shard, land = state

    after = list(after) if isinstance(after, (list, tuple)) else [after]

    def body(src_ref, land_ref, sends, recvs, *rest):
        for k in range(1, N_CHIP):
            _gather_copy(k, src_ref, land_ref, sends, recvs, axis, False).wait_send()
            _gather_copy(k, src_ref, land_ref, sends, recvs, axis, True).wait_recv()
        _own_copy(src_ref, land_ref, sends, axis).wait()

    return pl.pallas_call(
        body, name=name, out_shape=(pltpu.HBM(shard.shape, shard.dtype), pltpu.HBM(land.shape, land.dtype)),
        in_specs=[_HBM, _HBM, _SEM, _SEM] + [pl.BlockSpec(memory_space=pl.ANY)] * len(after), out_specs=(_HBM, _HBM),
        input_output_aliases={0: 0, 1: 1},
        compiler_params=pltpu.CompilerParams(has_side_effects=_EFFECT),
    )(shard, land, send_sems, recv_sems, *after)[1]


def _all8_copy(k, v_ref, land_ref, send_sems, recv_sems, arriving):
    x, y, c = _me()
    px, py, pc = x ^ ((k >> 2) & 1), y ^ ((k >> 1) & 1), c ^ (k & 1)
    slot = 4 * px + 2 * py + pc if arriving else 4 * x + 2 * y + c
    return pltpu.make_async_remote_copy(
        src_ref=v_ref, dst_ref=land_ref.at[slot], send_sem=send_sems.at[k - 1], recv_sem=recv_sems.at[k - 1],
        device_id=(px, py, pc), device_id_type=MESH)


def _all8_own(v_ref, land_ref, send_sems):
    x, y, c = _me()
    return pltpu.make_async_copy(v_ref, land_ref.at[4 * x + 2 * y + c], send_sems.at[N_DEV - 1])


def _all8_start(v, name):
    land = lax.empty((N_DEV,) + v.shape, v.dtype)

    def body(v_ref, land_ref, sends, recvs, v_thru, land_thru, token):
        for k in range(1, N_DEV):
            _all8_copy(k, v_ref, land_ref, sends, recvs, False).start()
        _all8_own(v_ref, land_ref, sends).start()
        token[...] = jnp.zeros_like(token)

    outs = pl.pallas_call(
        body, name=name,
        out_shape=(pltpu.SemaphoreType.DMA((N_DEV,)), pltpu.SemaphoreType.DMA((N_DEV - 1,)),
                   pltpu.HBM(v.shape, v.dtype), pltpu.HBM(land.shape, land.dtype),
                   jax.ShapeDtypeStruct((8, 128), F32)),
        in_specs=[_HBM, _HBM], out_specs=(_SEM, _SEM, _HBM, _HBM, pl.BlockSpec(memory_space=pltpu.VMEM)),
        input_output_aliases={0: 2, 1: 3},
        compiler_params=pltpu.CompilerParams(has_side_effects=_EFFECT),
    )(pltpu.with_memory_space_constraint(v, pltpu.HBM), pltpu.with_memory_space_constraint(land, pltpu.HBM))
    return outs[:4], outs[4]


def _all8_wait(state, after, name):
    send_sems, recv_sems, v, land = state

    def body(v_ref, land_ref, sends, recvs, after_ref, v_dead, got_ref):
        for k in range(1, N_DEV):
            _all8_copy(k, v_ref, land_ref, sends, recvs, False).wait_send()
            _all8_copy(k, v_ref, land_ref, sends, recvs, True).wait_recv()
        _all8_own(v_ref, land_ref, sends).wait()

    return pl.pallas_call(
        body, name=name, out_shape=(pltpu.HBM(v.shape, v.dtype), pltpu.HBM(land.shape, land.dtype)),
        in_specs=[_HBM, _HBM, _SEM, _SEM, pl.BlockSpec(memory_space=pl.ANY)], out_specs=(_HBM, _HBM),
        input_output_aliases={0: 0, 1: 1},
        compiler_params=pltpu.CompilerParams(has_side_effects=_EFFECT),
    )(v, land, send_sems, recv_sems, after)[1]


def _swap_copy(w, src_ref, land_ref, send_sems, recv_sems):
    x, y, c = _me()
    return pltpu.make_async_remote_copy(src_ref=src_ref, dst_ref=land_ref, send_sem=send_sems.at[w],
                                        recv_sem=recv_sems.at[w], device_id=(x, y, 1 - c), device_id_type=MESH)


def _swap_start(arrs, after, name):
    nw = len(arrs)
    lands = [lax.empty(a.shape, a.dtype) for a in arrs]

    def body(*refs):
        srcs, zones = refs[:nw], refs[nw:2 * nw]
        sends, recvs = refs[2 * nw + 1], refs[2 * nw + 2]
        for w in range(nw):
            _swap_copy(w, srcs[w], zones[w], sends, recvs).start()
        refs[-1][...] = jnp.zeros_like(refs[-1])

    sem = pltpu.SemaphoreType.DMA((nw,))
    outs = pl.pallas_call(
        body, name=name,
        out_shape=tuple([sem, sem] + [pltpu.HBM(a.shape, a.dtype) for a in list(arrs) + lands]
                        + [jax.ShapeDtypeStruct((8, 128), F32)]),
        in_specs=[_HBM] * (2 * nw) + [pl.BlockSpec(memory_space=pl.ANY)],
        out_specs=tuple([_SEM, _SEM] + [_HBM] * (2 * nw) + [pl.BlockSpec(memory_space=pltpu.VMEM)]),
        input_output_aliases={i: 2 + i for i in range(2 * nw)},
        compiler_params=pltpu.CompilerParams(has_side_effects=_EFFECT),
    )(*([pltpu.with_memory_space_constraint(a, pltpu.HBM) for a in list(arrs) + lands] + [after]))
    return (outs[0], outs[1], outs[2:2 + nw], outs[2 + nw:2 + 2 * nw]), outs[-1]


def _swap_wait(state, after, name):
    send_sems, recv_sems, arrs, lands = state
    nw = len(arrs)

    def body(*refs):
        srcs, zones = refs[:nw], refs[nw:2 * nw]
        sends, recvs = refs[2 * nw], refs[2 * nw + 1]
        for w in range(nw):
            cp = _swap_copy(w, srcs[w], zones[w], sends, recvs)
            cp.wait_send()
            cp.wait_recv()

    outs = pl.pallas_call(
        body, name=name, out_shape=tuple(pltpu.HBM(a.shape, a.dtype) for a in list(arrs) + list(lands)),
        in_specs=[_HBM] * (2 * nw) + [_SEM, _SEM, pl.BlockSpec(memory_space=pl.ANY)],
        out_specs=tuple([_HBM] * (2 * nw)),
        input_output_aliases={i: i for i in range(2 * nw)},
        compiler_params=pltpu.CompilerParams(has_side_effects=_EFFECT),
    )(*arrs, *lands, send_sems, recv_sems, after)
    return list(outs[:nw]), list(outs[nw:])


def _scatter_start(grad, axis, name):
    shp = list(grad.shape)
    shp[axis] //= N_CHIP
    land = lax.empty((N_CHIP,) + tuple(shp), grad.dtype)

    def body(grad_ref, land_ref, sends, recvs, grad_thru, land_thru, token):
        for k in range(1, N_CHIP):
            _scatter_copy(k, grad_ref, land_ref, sends, recvs, axis).start()
        _scatter_own(grad_ref, land_ref, sends, axis).start()
        token[...] = jnp.zeros_like(token)

    outs = pl.pallas_call(
        body, name=name,
        out_shape=(pltpu.SemaphoreType.DMA((_N_PEER + 1,)), pltpu.SemaphoreType.DMA((_N_PEER,)),
                   pltpu.HBM(grad.shape, grad.dtype), pltpu.HBM(land.shape, land.dtype),
                   jax.ShapeDtypeStruct((8, 128), F32)),
        in_specs=[_HBM, _HBM], out_specs=(_SEM, _SEM, _HBM, _HBM, pl.BlockSpec(memory_space=pltpu.VMEM)),
        input_output_aliases={0: 2, 1: 3},
        compiler_params=pltpu.CompilerParams(has_side_effects=_EFFECT),
    )(pltpu.with_memory_space_constraint(grad, pltpu.HBM), pltpu.with_memory_space_constraint(land, pltpu.HBM))
    return outs[:4], outs[4]


def _scatter_wait(state, axis, after, name):
    send_sems, recv_sems, grad, land = state

    def body(grad_ref, land_ref, sends, recvs, after_ref, grad_dead, got_ref):
        for k in range(1, N_CHIP):
            cp = _scatter_copy(k, grad_ref, land_ref, sends, recvs, axis)
            cp.wait_send()
            cp.wait_recv()
        _scatter_own(grad_ref, land_ref, sends, axis).wait()

    return pl.pallas_call(
        body, name=name, out_shape=(pltpu.HBM(grad.shape, grad.dtype), pltpu.HBM(land.shape, land.dtype)),
        in_specs=[_HBM, _HBM, _SEM, _SEM, pl.BlockSpec(memory_space=pl.ANY)], out_specs=(_HBM, _HBM),
        input_output_aliases={0: 0, 1: 1},
        compiler_params=pltpu.CompilerParams(has_side_effects=_EFFECT),
    )(grad, land, send_sems, recv_sems, after)[1]


_C1 = 1.0 - B1 ** STEP
_C2 = 1.0 - B2 ** STEP


def _adam_math(w, g, m, v):
    m = B1 * m + (1.0 - B1) * g
    v = B2 * v + (1.0 - B2) * (g * g)
    delta = -LR * ((m / _C1) / (jnp.sqrt(v / _C2) + AEPS) + WD * w)
    return delta, m, v


def _adamw(w, m, v, groups, name):
    R, C = w.shape
    tr = R if R <= 256 else (128 if R % 128 == 0 else 176)
    assert R % tr == 0, (name, R)
    gparts = [p for grp in groups for p in grp]
    sizes = [len(grp) for grp in groups]
    ng = len(gparts)

    def body(*refs):
        w_ref, m_ref, v_ref = refs[:3]
        g_refs = list(refs[3:3 + ng])
        g_out, d_out, m_out, v_out = refs[3 + ng:]
        g = None
        for size in sizes:
            s = None
            for r in [g_refs.pop(0) for _ in range(size)]:
                terms = [r[q] for q in range(r.shape[0])] if len(r.shape) == 3 else [r[...]]
                for t in terms:
                    s = t.astype(F32) if s is None else s + t.astype(F32)
            g = s if g is None else g + s
        delta, mn, vn = _adam_math(w_ref[...], g, m_ref[...], v_ref[...])
        g_out[...] = g
        d_out[...] = delta
        m_out[...] = mn
        v_out[...] = vn

    blk = pl.BlockSpec((tr, C), lambda i: (i, 0))
    g_specs = [blk if p.ndim == 2 else pl.BlockSpec((p.shape[0], tr, C), lambda i: (0, i, 0)) for p in gparts]
    sds = jax.ShapeDtypeStruct((R, C), F32)
    return pl.pallas_call(
        body, name=name, out_shape=(sds, sds, sds, sds), grid=(R // tr,),
        in_specs=[blk, blk, blk] + g_specs, out_specs=(blk, blk, blk, blk),
        compiler_params=_cp(("parallel",)))(w, m, v, *gparts)


def _adamw_small(stack, names, wts, mom, var, sum_only, name):
    items, row = [], 0
    for n in names:
        shape = (KW, CW) if n == "conv_w" else wts[n].shape
        size = int(np.prod(shape))
        vec = len(shape) == 2 and shape[0] == 1 and n not in sum_only
        view = shape if vec else (-(-size // _PACK_COLS), _PACK_COLS)
        items.append((n, row, size, vec, view))
        row += _pack_rows(shape)
    upd = [it for it in items if it[0] not in sum_only]
    operands = [stack]
    for n, _, _, _, view in upd:
        operands += [d[n].reshape(view) for d in (wts, mom, var)]

    def grad(stack_ref, r0, nrows, ncols):
        g = stack_ref[0, r0:r0 + nrows, 0:ncols]
        for q in range(1, N_DEV):
            g = g + stack_ref[q, r0:r0 + nrows, 0:ncols]
        return g

    def body(*refs):
        stack_ref, ins, outs = refs[0], refs[1:1 + 3 * len(upd)], refs[1 + 3 * len(upd):]
        o = 0
        for idx, (n, r0, size, vec, view) in enumerate(upd):
            w_ref, m_ref, v_ref = ins[3 * idx:3 * idx + 3]
            g_out, d_out, m_out, v_out = outs[o:o + 4]
            o += 4
            if vec:
                pieces = [(j, j * _PACK_COLS, min((j + 1) * _PACK_COLS, size)) for j in range(-(-size // _PACK_COLS))]
            else:
                pieces = [(None, 0, _PACK_COLS)]
            for j, lo, hi in pieces:
                if vec:
                    g = grad(stack_ref, r0 + j, 1, hi - lo)
                    sl = (slice(None), slice(lo, hi))
                else:
                    g = grad(stack_ref, r0, view[0], _PACK_COLS)
                    sl = (slice(None), slice(None))
                delta, mn, vn = _adam_math(w_ref[sl], g, m_ref[sl], v_ref[sl])
                g_out[sl] = g
                d_out[sl] = delta
                m_out[sl] = mn
                v_out[sl] = vn
        for n, r0, size, vec, view in items:
            if n in sum_only:
                outs[o][...] = grad(stack_ref, r0, view[0], _PACK_COLS)
                o += 1

    out_shape = []
    for n, _, _, _, view in upd:
        out_shape += [jax.ShapeDtypeStruct(view, F32)] * 4
    out_shape += [jax.ShapeDtypeStruct(view, F32) for n, _, _, _, view in items if n in sum_only]
    vm = pl.BlockSpec(memory_space=pltpu.VMEM)
    res = pl.pallas_call(
        body, name=name, out_shape=tuple(out_shape), in_specs=[vm] * len(operands),
        out_specs=tuple([vm] * len(out_shape)),
        compiler_params=pltpu.CompilerParams(vmem_limit_bytes=VMEM_LIMIT))(*operands)
    updated = {n: tuple(r.reshape(wts[n].shape) for r in res[4 * i:4 * i + 4]) for i, (n, *_) in enumerate(upd)}
    sums = dict(zip([it[0] for it in items if it[0] in sum_only], res[4 * len(upd):]))
    return updated, sums


def _mod_shard(c_all, w_ada, b_ada_cols):
    n = w_ada.shape[1]
    tn = 512

    def body(c_ref, w_ref, b_ref, o_ref):
        cv = c_ref[...]
        ca = (cv * _sig(cv)).astype(BF16)
        o_ref[...] = jnp.dot(ca, w_ref[...].astype(BF16), preferred_element_type=F32) + b_ref[...]

    return pl.pallas_call(
        body, name="mod_shard", out_shape=jax.ShapeDtypeStruct((N_DEV, n), F32), grid=(n // tn,),
        in_specs=[_full((N_DEV, D_MODEL)), pl.BlockSpec((D_MODEL, tn), lambda j: (0, j)),
                  pl.BlockSpec((1, tn), lambda j: (0, j))],
        out_specs=pl.BlockSpec((N_DEV, tn), lambda j: (0, j)),
        compiler_params=_cp(("parallel",)))(c_all, w_ada, b_ada_cols)


def _ada_grad(c_all, dmod_cols, after):
    n = dmod_cols.shape[1]
    tn = 512

    def body(c_ref, d_ref, after_ref, o_ref):
        cv = c_ref[...]
        ca = cv * _sig(cv)
        o_ref[...] = lax.dot_general(ca, d_ref[...], (((0,), (0,)), ((), ())),
                                     preferred_element_type=F32, precision=lax.Precision.HIGHEST)

    return pl.pallas_call(
        body, name="ada_grad", out_shape=jax.ShapeDtypeStruct((D_MODEL, n), F32), grid=(n // tn,),
        in_specs=[_full((N_DEV, D_MODEL)), pl.BlockSpec((N_DEV, tn), lambda j: (0, j)),
                  pl.BlockSpec(memory_space=pl.ANY)],
        out_specs=pl.BlockSpec((D_MODEL, tn), lambda j: (0, j)),
        compiler_params=_cp(("parallel",)))(c_all, dmod_cols, after)


def _ssm_tables(W):
    e_re, e_im, bb_re, bb_im = _ssm_prep(W["ssm_a_re"], W["ssm_a_im"], W["ssm_b_re"], W["ssm_b_im"], W["ssm_log_dt"])
    bb, cm = _block_diag_mats(bb_re, bb_im, W["ssm_c_re"], W["ssm_c_im"])
    bb16, cm16 = bb.astype(BF16), cm.astype(BF16)
    return (bb16, cm16, jnp.swapaxes(bb16, 1, 2), jnp.swapaxes(cm16, 1, 2),
            _scan_tables(e_re, e_im, False), _scan_tables(e_re, e_im, True))


def _device_step(x, mod, W, tables, tgt, getw, put, early):
    sh1, sc1, g1, sh2, sc2, g2 = [mod[:, i * D_MODEL:(i + 1) * D_MODEL] for i in range(6)]
    bb16, cm16, bbt16, cmt16, tab_f, tab_b = tables

    w_in = getw("w_in", [mod, *tables])
    h1, z = _in_proj(x, W["norm1_g"], sc1, sh1, w_in)
    yc, scv = _conv_fwd(z, W["conv_w"], W["conv_b"], W["conv_ln_g"], W["conv_ln_b"])
    xs, ys, yg = _ssm_fwd(z, bb16, cm16, W["ssm_d"], tab_f)
    w_cp, w_glu, w_out = getw("conv_proj", scv), getw("ssm_glu", yg), getw("w_out", yg)
    y_conv, zz, merged, o, x2, h2 = _mix_fwd(scv, yg, z, x, w_cp, w_glu, w_out, g1, W["norm2_g"], sc2, sh2)
    w_fi = getw("w_ffn_in", h2)
    f, act = _ffn_in_act(h2, w_fi)
    w_fo = getw("w_ffn_out", act)
    dx3, do2, loss8, dfg8, dg2_8 = _ffn_out_final(x2, act, w_fo, g2, W["final_g"], tgt)

    sm = {}
    tok = put("w_ffn_out", _matmul(act, do2, "tn", 1408, 1024, 2048, BF16, "mm_g_ffn_out"))
    df = _ffn_bwd(do2, w_fo, f, tok)
    tok = put("w_ffn_in", _matmul(h2, df, "tn", 1024, 1408, 2048, BF16, "mm_g_ffn_in"))
    dx2, do, dsh2, dsc2, dn2, dg1_8 = _normmod_bwd(df, w_fi, x2, dx3, W["norm2_g"], sc2, g1, o, tok, "d_h2_normmod2_bwd")
    tok = put("w_out", _matmul(merged, do, "tn", 1024, 1024, 4096, BF16, "mm_g_w_out"))
    dyconv, dgl, dzz = _mix_bwd(do, w_out, z, zz, y_conv, tok)
    tok = put("ssm_glu", _matmul(yg, dzz, "tn", 512, 1024, 4096, BF16, "mm_g_ssm_glu"))
    tok = put("conv_proj", _matmul(scv, dyconv, "tn", 512, 1024, 4096, BF16, "mm_g_conv_proj", after=tok))
    du, de16, dd8, dc_full, dbb_full = _ssm_bwd(dzz, w_glu, ys, z, xs, cmt16, bbt16, W["ssm_d"], tab_b, tok)
    dyc, dlg8, dlb8, dcb8 = _conv_bwd_ln(dyconv, w_cp, yc, W["conv_ln_g"], W["conv_ln_b"])
    dz_conv, dcw = _conv_bwd(dyc, z, W["conv_w"])

    s8 = lambda a: jnp.sum(a, axis=0, keepdims=True)
    de = de16.reshape(2, 8, NST).sum(1)
    de_re, de_im = de[0].reshape(G, P), de[1].reshape(G, P)
    dc_re = _diag_blocks(dc_full, False)
    dc_im = -_diag_blocks(dc_full, True)
    dbb_re = jnp.swapaxes(_diag_blocks(dbb_full, False), 1, 2)
    dbb_im = jnp.swapaxes(_diag_blocks(dbb_full, True), 1, 2)
    _, vjp = jax.vjp(_ssm_prep, W["ssm_a_re"], W["ssm_a_im"], W["ssm_b_re"], W["ssm_b_im"], W["ssm_log_dt"])
    sm["ssm_a_re"], sm["ssm_a_im"], sm["ssm_b_re"], sm["ssm_b_im"], sm["ssm_log_dt"] = vjp((de_re, de_im, dbb_re, dbb_im))
    sm["ssm_c_re"], sm["ssm_c_im"] = dc_re, dc_im
    sm["ssm_d"] = s8(dd8)
    sm["norm2_g"] = s8(dn2)
    sm["conv_b"], sm["conv_ln_g"], sm["conv_ln_b"] = s8(dcb8), s8(dlg8), s8(dlb8)
    sm["conv_w"] = dcw.reshape(KW, 8, CW).sum(1)
    sm["final_g"] = s8(dfg8)
    tok = early(sm)

    dz = [dz_conv, du, dgl]
    tok = put("w_in", _matmul(h1, dz, "tn", 1024, 512, 4096, BF16, "mm_g_w_in", after=tok))
    dx, _, dsh1, dsc1, dn1, _ = _normmod_bwd(dz, w_in, x, dx2, W["norm1_g"], sc1, g1, o, tok, "d_h1_normmod1_bwd")
    dmod = jnp.concatenate([s8(dsh1), s8(dsc1), s8(dg1_8), s8(dsh2), s8(dsc2), s8(dg2_8)], axis=1)
    return loss8, dx, s8(dn1), dmod


_BIG = ("w_in", "conv_proj", "ssm_glu", "w_out", "w_ffn_in", "w_ffn_out")
_BIG_AXIS = {"w_in": 1, "conv_proj": 1, "ssm_glu": 1, "w_out": 0, "w_ffn_in": 1, "w_ffn_out": 0}
_EARLY = ("conv_w", "conv_b", "conv_ln_g", "conv_ln_b", "ssm_a_re", "ssm_a_im", "ssm_b_re", "ssm_b_im", "ssm_c_re",
          "ssm_c_im", "ssm_d", "ssm_log_dt", "norm2_g", "final_g")
_LATE = ("norm1_g", "b_ada")
_ORDER = ("w_ada", "b_ada", "norm1_g", "w_in", "conv_w", "conv_b", "conv_ln_g", "conv_ln_b", "conv_proj",
          "ssm_a_re", "ssm_a_im", "ssm_b_re", "ssm_b_im", "ssm_c_re", "ssm_c_im", "ssm_d", "ssm_log_dt", "ssm_glu",
          "w_out", "norm2_g", "w_ffn_in", "w_ffn_out", "final_g")
_PACK_COLS = 1024


def _pack_rows(shape):
    return -(-int(np.prod(shape)) // (8 * _PACK_COLS)) * 8


def _pack(arrs):
    parts = []
    for a in arrs:
        flat = a.reshape(-1)
        n = _pack_rows(a.shape)
        parts.append(jnp.pad(flat, (0, n * _PACK_COLS - flat.shape[0])).reshape(n, _PACK_COLS))
    return jnp.concatenate(parts, 0)


def kernel(x, c, w_ada, b_ada, norm1_g, w_in, conv_w, conv_b, conv_ln_g, conv_ln_b, conv_proj, ssm_a_re, ssm_a_im, ssm_b_re, ssm_b_im, ssm_c_re, ssm_c_im, ssm_d, ssm_log_dt, ssm_glu, w_out, norm2_g, w_ffn_in, w_ffn_out, final_g, loss_target, m_w_ada, m_b_ada, m_norm1_g, m_w_in, m_conv_w, m_conv_b, m_conv_ln_g, m_conv_ln_b, m_conv_proj, m_ssm_a_re, m_ssm_a_im, m_ssm_b_re, m_ssm_b_im, m_ssm_c_re, m_ssm_c_im, m_ssm_d, m_ssm_log_dt, m_ssm_glu, m_w_out, m_norm2_g, m_w_ffn_in, m_w_ffn_out, m_final_g, v_w_ada, v_b_ada, v_norm1_g, v_w_in, v_conv_w, v_conv_b, v_conv_ln_g, v_conv_ln_b, v_conv_proj, v_ssm_a_re, v_ssm_a_im, v_ssm_b_re, v_ssm_b_im, v_ssm_c_re, v_ssm_c_im, v_ssm_d, v_ssm_log_dt, v_ssm_glu, v_w_out, v_norm2_g, v_w_ffn_in, v_w_ffn_out, v_final_g):
    given = dict(locals())
    mx, my, mc = _me()
    chip = 2 * mx + my
    dev = 4 * mx + 2 * my + mc
    def canon(a):
        return a.reshape(1, -1) if a.ndim <= 2 else a[0]

    wts = {n: canon(given[n]) for n in _ORDER}
    mom = {n: canon(given["m_" + n]) for n in _ORDER}
    var = {n: canon(given["v_" + n]) for n in _ORDER}

    c_all = _allgather8(jnp.broadcast_to(c, (8, D_MODEL)), "gather_c")[:, 0, :]
    n_ada = wts["w_ada"].shape[1]
    b_cols = lax.dynamic_slice(wts["b_ada"], (0, chip * n_ada), (1, n_ada))
    mod_cols = _mod_shard(c_all, wts["w_ada"], b_cols)
    mods = _allgather8(mod_cols, "gather_mod")
    mod = jnp.concatenate([lax.dynamic_index_in_dim(mods[2 * q], dev, 0, keepdims=True) for q in range(N_CHIP)], axis=1)
    W = {n: wts[n] for n in _ORDER if n not in _BIG}
    conv_w_full = _allgather8(jnp.pad(wts["conv_w"], ((0, 1), (0, 0))), "gather_conv_w", after=[c_all])
    W["conv_w"] = jnp.concatenate([conv_w_full[2 * q, :KW] for q in range(N_CHIP)], axis=1)

    state_in, token = _gather_start([wts["w_in"].astype(BF16)], [_BIG_AXIS["w_in"]],
                                    mod + W["conv_w"][0:1, 0:1], "gather_start_w_in")
    W["ssm_log_dt"] = wts["ssm_log_dt"] + token[0:1, 0:1]
    W["ssm_c_re"] = wts["ssm_c_re"] + token[0, 0]
    tables = _ssm_tables(W)
    w_in_full = _gather_wait(state_in[0], _BIG_AXIS["w_in"], [mod, *tables], "gather_wait_w_in")
    rest = [n for n in _BIG if n != "w_in"]
    gstate, token = _gather_start([wts[n].astype(BF16) for n in rest], [_BIG_AXIS[n] for n in rest], w_in_full,
                                  "gather_start_rest")
    gstate = dict(zip(rest, gstate))
    mod = mod + token[0:1, 0:1]

    def getw(n, after):
        if n == "w_in":
            return w_in_full
        return _gather_wait(gstate[n], _BIG_AXIS[n], after, "gather_wait_" + n)

    sstate, estate = {}, []

    def put(n, g):
        sstate[n], tok = _scatter_start(g, _BIG_AXIS[n], "scatter_start_" + n)
        return tok

    first5 = [n for n in _BIG if n != "w_in"]

    def early(sm):
        state, tok = _all8_start(_pack([sm[n] for n in _EARLY]), "small_start")
        estate.append(state)
        held = [_scatter_wait(sstate[n], _BIG_AXIS[n], tok, "scatter_wait_" + n) for n in first5]
        state, tok = _swap_start(held, tok, "swap_start")
        estate.append(state)
        return tok

    loss8, dx, dn1, dmod = _device_step(x[0], mod, W, tables, loss_target[0], getw, put, early)

    held5, sib5 = _swap_wait(estate[1], dx, "swap_wait")
    outs = {}
    for i, n in enumerate(first5):
        outs[n] = _adamw(wts[n], mom[n], var[n], [[held5[i]], [sib5[i]]], "adamw_" + n)
    allp = _all8_wait(estate[0], dx, "small_wait")
    upd, sums = _adamw_small(allp, _EARLY, wts, mom, var, ("conv_w",), "adamw_small")
    outs.update(upd)

    late = _allgather8(_pack([dn1, dmod, loss8]), "gather_late", after=[outs[n][1] for n in first5])
    n_late = _pack_rows((D_MODEL,)) + _pack_rows((6 * D_MODEL,))
    loss = jnp.sum(late[:, n_late:, :])
    late = late[:, :n_late, :]
    held_in = _scatter_wait(sstate["w_in"], _BIG_AXIS["w_in"], late, "scatter_wait_w_in")
    state_in, tok = _swap_start([held_in], late, "swap_start_w_in")

    r1 = _pack_rows((D_MODEL,))
    dmod_all = late[:, r1:, :].reshape(N_DEV, -1)[:, :6 * D_MODEL]
    dmod_cols = lax.dynamic_slice(dmod_all, (0, chip * n_ada), (N_DEV, n_ada))
    g_ada = _ada_grad(c_all, dmod_cols, tok)
    outs["w_ada"] = _adamw(wts["w_ada"], mom["w_ada"], var["w_ada"], [[g_ada]], "adamw_w_ada")
    upd, _ = _adamw_small(late, _LATE, wts, mom, var, (), "adamw_late")
    outs.update(upd)
    held_in, sib_in = _swap_wait(state_in, outs["w_ada"][1], "swap_wait_w_in")
    outs["w_in"] = _adamw(wts["w_in"], mom["w_in"], var["w_in"], [held_in, sib_in], "adamw_w_in")
    g_cw_full = sums["conv_w"].reshape(-1)[:KW * CW].reshape(KW, CW)
    g_cw = lax.dynamic_slice(g_cw_full, (0, chip * (CW // N_CHIP)), (KW, CW // N_CHIP))
    pad = lambda a: jnp.pad(a, ((0, 1), (0, 0)))
    r_cw = _adamw(pad(wts["conv_w"]), pad(mom["conv_w"]), pad(var["conv_w"]), [[pad(g_cw)]], "adamw_conv_w")
    outs["conv_w"] = tuple(r[:KW] for r in r_cw)

    def shaped(n, a):
        return a.reshape(given[n].shape)

    result = [loss, dx[None]]
    for q in range(4):
        result += [shaped(n, outs[n][q]) for n in _ORDER]
    return tuple(result)
```

```python
import math

import jax
import jax.numpy as jnp
import numpy as np
from jax import lax
from jax.experimental import pallas as pl
from jax.experimental.pallas import tpu as pltpu

F32 = jnp.float32
BF16 = jnp.bfloat16
EPS = 1e-6
D_MODEL = 1024
CW = 512
KW = 31
HALO = 32
G, P, H = 32, 64, 16
NST = G * P
FH = 2816
N_DEV = 8
N_CHIP = 4
VMEM_LIMIT = 56 * 1024 * 1024
LR, B1, B2, AEPS, WD, STEP = 0.001, 0.9, 0.999, 1e-08, 0.01, 10
MESH = pl.DeviceIdType.MESH


def _cp(sem=None):
    return pltpu.CompilerParams(dimension_semantics=sem, vmem_limit_bytes=VMEM_LIMIT)


def _sig(x):
    return jax.nn.sigmoid(x)


def _full(shape):
    return pl.BlockSpec(shape, lambda *_: (0,) * len(shape))


def _resident(shape):
    return pl.BlockSpec(shape, lambda *_: (0,) * len(shape), pipeline_mode=pl.Buffered(1))


def _colsum8(v):
    t, c = v.shape
    return jnp.sum(v.reshape(t // 8, 8, c), axis=0)


def _matmul(a, b, mode, tm, tn, tk, out_dtype, name, after=None, n_outer=False, m_cols=None):
    m0 = 0
    b_parts = list(b) if isinstance(b, (list, tuple)) else [b]
    if mode == "nn":
        (M, K), N = a.shape, b.shape[1]
    elif mode == "nt":
        (M, K), N = a.shape, b.shape[0]
    else:
        (K, M), N = a.shape, sum(p.shape[1] for p in b_parts)
        if m_cols is not None:
            m0, M = m_cols
    tm, tn, tk = min(tm, M), min(tn, N), min(tk, K)
    assert M % tm == 0 and N % tn == 0 and K % tk == 0 and m0 % tm == 0, (name, M, N, K, tm, tn, tk)
    assert len(b_parts) == 1 or (mode == "tn" and all(p.shape[1] % tn == 0 for p in b_parts)), name
    nk = K // tk
    mb = m0 // tm
    counts = [p.shape[1] // tn for p in b_parts] if mode == "tn" else [N // tn]
    starts = [sum(counts[:p]) for p in range(len(counts))]

    def ij(fn):
        return (lambda j, i, k: fn(i, j, k)) if n_outer else fn

    if mode == "nn":
        a_spec = pl.BlockSpec((tm, tk), ij(lambda i, j, k: (i, k)))
        b_spec = pl.BlockSpec((tk, tn), ij(lambda i, j, k: (k, j)))
        dims = (((1,), (0,)), ((), ()))
    elif mode == "nt":
        a_spec = pl.BlockSpec((tm, tk), ij(lambda i, j, k: (i, k)))
        b_spec = pl.BlockSpec((tn, tk), ij(lambda i, j, k: (j, k)))
        dims = (((1,), (1,)), ((), ()))
    else:
        a_spec = pl.BlockSpec((tk, tm), ij(lambda i, j, k: (k, i + mb)))
        dims = (((0,), (0,)), ((), ()))
    if mode == "tn":
        b_specs = [pl.BlockSpec((tk, tn), ij(lambda i, j, k, s=s, n=n: (k, jnp.clip(j - s, 0, n - 1))))
                   for s, n in zip(starts, counts)]
    else:
        b_specs = [b_spec]
    nb = len(b_parts)

    def body(a_ref, *rest):
        b_refs = rest[:nb]
        o_ref, acc_ref = rest[-2:]
        j = pl.program_id(0 if n_outer else 1)
        k = pl.program_id(2)

        def compute(b_ref):
            part = lax.dot_general(a_ref[...].astype(BF16), b_ref[...].astype(BF16), dims,
                                   preferred_element_type=F32)
            if nk == 1:
                o_ref[...] = part.astype(out_dtype)
            else:
                @pl.when(k == 0)
                def _():
                    acc_ref[...] = part

                @pl.when(k > 0)
                def _():
                    acc_ref[...] += part

                @pl.when(k == nk - 1)
                def _():
                    o_ref[...] = acc_ref[...].astype(out_dtype)

        if nb == 1:
            compute(b_refs[0])
        else:
            for p in range(nb):
                pl.when(jnp.logical_and(j >= starts[p], j < starts[p] + counts[p]))(
                    lambda b_ref=b_refs[p]: compute(b_ref))

    return pl.pallas_call(
        body, name=name,
        out_shape=jax.ShapeDtypeStruct((M, N), out_dtype),
        grid=(N // tn, M // tm, nk) if n_outer else (M // tm, N // tn, nk),
        in_specs=[a_spec] + b_specs + ([] if after is None else [pl.BlockSpec(memory_space=pl.ANY)]),
        out_specs=pl.BlockSpec((tm, tn), ij(lambda i, j, k: (i, j))),
        scratch_shapes=[pltpu.VMEM((tm, tn) if nk > 1 else (8, 128), F32)],
        compiler_params=_cp(("parallel", "parallel", "arbitrary")),
    )(*([a] + b_parts + ([] if after is None else [after])))


def _row_tile(S):
    return min(512, S)


def _in_proj(x, g, sc, sh, w_in):
    S, D = x.shape
    N = w_in.shape[1]
    tm = min(512, S)

    def body(x_ref, g_ref, sc_ref, sh_ref, w_ref, h_ref, z_ref):
        xv = x_ref[...]
        r = lax.rsqrt(jnp.mean(xv * xv, axis=-1, keepdims=True) + EPS)
        h = (xv * r * (g_ref[...] * (1.0 + sc_ref[...])) + sh_ref[...]).astype(BF16)
        h_ref[...] = h
        z_ref[...] = jnp.dot(h, w_ref[...], preferred_element_type=F32).astype(BF16)

    row = pl.BlockSpec((tm, D), lambda i: (i, 0))
    par = _full((1, D))
    return pl.pallas_call(
        body, name="in_proj",
        out_shape=(jax.ShapeDtypeStruct((S, D), BF16), jax.ShapeDtypeStruct((S, N), BF16)), grid=(S // tm,),
        in_specs=[row, par, par, par, _resident((D, N))], out_specs=(row, pl.BlockSpec((tm, N), lambda i: (i, 0))),
        compiler_params=_cp(("parallel",)))(x, g, sc, sh, w_in)


def _fill_shifted(buf_ref, sh_ref):
    n = buf_ref.shape[0] - 8
    for s in range(1, 8):
        sh_ref[s, 0:n, :] = buf_ref[s:s + n, :]


def _window(buf_ref, sh_ref, off, n):
    s = off % 8
    return buf_ref[off:off + n, :] if s == 0 else sh_ref[s, off - s:off - s + n, :]


def _conv_fwd(z, conv_w, conv_b, ln_g, ln_b):
    S = z.shape[0]
    tm = min(128, S)
    sub = 32
    hb = tm // HALO

    def body(a_ref, g_ref, ha_ref, hg_ref, w_ref, b_ref, lg_ref, lb_ref, yc_ref, s_ref, ug_ref, sh_ref):
        i = pl.program_id(0)
        halo = ha_ref[...].astype(F32) * _sig(hg_ref[...].astype(F32))
        ug_ref[0:HALO, :] = jnp.where(i == 0, 0.0, halo)
        ug_ref[HALO:, :] = a_ref[...].astype(F32) * _sig(g_ref[...].astype(F32))
        _fill_shifted(ug_ref, sh_ref)
        for rb in range(tm // sub):
            acc = jnp.zeros((sub, CW), F32) + b_ref[...]
            for k in range(KW):
                off = rb * sub + HALO - (KW - 1) + k
                acc = acc + w_ref[k:k + 1, :] * _window(ug_ref, sh_ref, off, sub)
            yc_ref[rb * sub:(rb + 1) * sub, :] = acc
            mu = jnp.mean(acc, axis=-1, keepdims=True)
            cen = acc - mu
            rstd = lax.rsqrt(jnp.mean(cen * cen, axis=-1, keepdims=True) + EPS)
            ln = cen * rstd * lg_ref[...] + lb_ref[...]
            s_ref[rb * sub:(rb + 1) * sub, :] = (ln * _sig(ln)).astype(BF16)

    prev = lambda i: (jnp.maximum(i * hb - 1, 0), 0)
    return pl.pallas_call(
        body, name="conv_fwd",
        out_shape=(jax.ShapeDtypeStruct((S, CW), F32), jax.ShapeDtypeStruct((S, CW), BF16)),
        grid=(S // tm,),
        in_specs=[pl.BlockSpec((tm, CW), lambda i: (i, 0)), pl.BlockSpec((tm, CW), lambda i: (i, 1)),
                  pl.BlockSpec((HALO, CW), prev), pl.BlockSpec((HALO, CW), lambda i: (jnp.maximum(i * hb - 1, 0), 1)),
                  _full((KW, CW)), _full((1, CW)), _full((1, CW)), _full((1, CW))],
        out_specs=(pl.BlockSpec((tm, CW), lambda i: (i, 0)), pl.BlockSpec((tm, CW), lambda i: (i, 0))),
        scratch_shapes=[pltpu.VMEM((tm + HALO, CW), F32), pltpu.VMEM((8, tm + HALO, CW), F32)],
        compiler_params=_cp(("parallel",)))(z, z, z, z, conv_w, conv_b, ln_g, ln_b)


def _conv_bwd_ln(dyconv, w_cp, yc, ln_g, ln_b):
    S = yc.shape[0]
    tm = _row_tile(S)

    def body(dy_ref, w_ref, yc_ref, lg_ref, lb_ref, dyc_ref, dlg_ref, dlb_ref, dcb_ref):
        i = pl.program_id(0)
        dsc = lax.dot_general(dy_ref[...], w_ref[...], (((1,), (1,)), ((), ())), preferred_element_type=F32)
        yc_v = yc_ref[...]
        mu = jnp.mean(yc_v, axis=-1, keepdims=True)
        cen = yc_v - mu
        rstd = lax.rsqrt(jnp.mean(cen * cen, axis=-1, keepdims=True) + EPS)
        yn = cen * rstd
        ln = yn * lg_ref[...] + lb_ref[...]
        sl = _sig(ln)
        dln = dsc * (sl * (1.0 + ln * (1.0 - sl)))
        dyn = dln * lg_ref[...]
        dyc = rstd * (dyn - jnp.mean(dyn, axis=-1, keepdims=True)
                      - yn * jnp.mean(dyn * yn, axis=-1, keepdims=True))
        dyc_ref[...] = dyc

        @pl.when(i == 0)
        def _():
            dlg_ref[...] = jnp.zeros_like(dlg_ref)
            dlb_ref[...] = jnp.zeros_like(dlb_ref)
            dcb_ref[...] = jnp.zeros_like(dcb_ref)

        dlg_ref[...] += _colsum8(dln * yn)
        dlb_ref[...] += _colsum8(dln)
        dcb_ref[...] += _colsum8(dyc)

    row = pl.BlockSpec((tm, CW), lambda i: (i, 0))
    acc = jax.ShapeDtypeStruct((8, CW), F32)
    return pl.pallas_call(
        body, name="conv_bwd_ln",
        out_shape=(jax.ShapeDtypeStruct((S, CW), F32), acc, acc, acc), grid=(S // tm,),
        in_specs=[pl.BlockSpec((tm, D_MODEL), lambda i: (i, 0)), _full((CW, D_MODEL)), row, _full((1, CW)),
                  _full((1, CW))],
        out_specs=(row, _full((8, CW)), _full((8, CW)), _full((8, CW))),
        compiler_params=_cp(("arbitrary",)))(dyconv, w_cp, yc, ln_g, ln_b)


def _conv_bwd(dyc, z, conv_w):
    S = z.shape[0]
    tm = min(128, S)
    sub = 32
    hb = tm // HALO
    nt = S // tm

    def body(d_ref, dn_ref, a_ref, g_ref, ha_ref, hg_ref, w_ref, dz_ref, dw_ref, ug_ref, dy_ref, ugs_ref, dys_ref):
        i = pl.program_id(0)
        halo = ha_ref[...].astype(F32) * _sig(hg_ref[...].astype(F32))
        ug_ref[0:HALO, :] = jnp.where(i == 0, 0.0, halo)
        a = a_ref[...].astype(F32)
        sg = _sig(g_ref[...].astype(F32))
        ug_ref[HALO:, :] = a * sg
        dy_ref[0:tm, :] = d_ref[...]
        dy_ref[tm:, :] = jnp.where(i == nt - 1, 0.0, dn_ref[...])
        _fill_shifted(ug_ref, ugs_ref)
        _fill_shifted(dy_ref, dys_ref)

        @pl.when(i == 0)
        def _():
            dw_ref[...] = jnp.zeros_like(dw_ref)

        for rb in range(tm // sub):
            r0 = rb * sub
            acc = jnp.zeros((sub, CW), F32)
            dyc_b = dy_ref[r0:r0 + sub, :]
            for k in range(KW):
                up = r0 + (KW - 1) - k
                acc = acc + w_ref[k:k + 1, :] * _window(dy_ref, dys_ref, up, sub)
                off = r0 + HALO - (KW - 1) + k
                dw_ref[k * 8:(k + 1) * 8, :] += _colsum8(dyc_b * _window(ug_ref, ugs_ref, off, sub))
            a_b = a[r0:r0 + sub, :]
            sg_b = sg[r0:r0 + sub, :]
            dz_ref[r0:r0 + sub, 0:CW] = (acc * sg_b).astype(BF16)
            dz_ref[r0:r0 + sub, CW:2 * CW] = (acc * a_b * sg_b * (1.0 - sg_b)).astype(BF16)

    return pl.pallas_call(
        body, name="conv_bwd",
        out_shape=(jax.ShapeDtypeStruct((S, 2 * CW), BF16), jax.ShapeDtypeStruct((KW * 8, CW), F32)),
        grid=(nt,),
        in_specs=[pl.BlockSpec((tm, CW), lambda i: (i, 0)),
                  pl.BlockSpec((HALO, CW), lambda i: (jnp.minimum((i + 1) * hb, nt * hb - 1), 0)),
                  pl.BlockSpec((tm, CW), lambda i: (i, 0)), pl.BlockSpec((tm, CW), lambda i: (i, 1)),
                  pl.BlockSpec((HALO, CW), lambda i: (jnp.maximum(i * hb - 1, 0), 0)),
                  pl.BlockSpec((HALO, CW), lambda i: (jnp.maximum(i * hb - 1, 0), 1)),
                  _full((KW, CW))],
        out_specs=(pl.BlockSpec((tm, 2 * CW), lambda i: (i, 0)), _full((KW * 8, CW))),
        scratch_shapes=[pltpu.VMEM((tm + HALO, CW), F32), pltpu.VMEM((tm + HALO, CW), F32),
                        pltpu.VMEM((8, tm + HALO, CW), F32), pltpu.VMEM((8, tm + HALO, CW), F32)],
        compiler_params=_cp(("arbitrary",)))(dyc, dyc, z, z, z, z, conv_w)


_GELU_C = math.sqrt(2.0 / math.pi)


def _gelu(x):
    return 0.5 * x * (1.0 + jnp.tanh(_GELU_C * (x + 0.044715 * x * x * x)))


def _gelu_grad(x):
    t = jnp.tanh(_GELU_C * (x + 0.044715 * x * x * x))
    return 0.5 * (1.0 + t) + 0.5 * x * (1.0 - t * t) * (_GELU_C * (1.0 + 3 * 0.044715 * x * x))


_NCL = 4
_UC = CW // _NCL
_LW = NST // _NCL
_CS = 2 * _LW


def _ssm_fwd(z, bb, cm, d, tab):
    S = z.shape[0]
    tm = min(512, S)

    def body(u_ref, bb_ref, cm_ref, d_ref, t_ref, x_ref, ys_ref, yg_ref, car_ref):
        i = pl.program_id(0)

        @pl.when(i == 0)
        def _():
            car_ref[...] = jnp.zeros_like(car_ref)

        u16 = u_ref[...]
        u = u16.astype(F32)
        for c in range(_NCL):
            lre = pl.ds(c * _CS, _LW)
            lim = pl.ds(c * _CS + _LW, _LW)
            tl = pl.ds(c * _LW, _LW)
            x_ref[:, c * _CS:(c + 1) * _CS] = jnp.dot(u16[:, c * _UC:(c + 1) * _UC], bb_ref[c],
                                                      preferred_element_type=F32)

            def blk(j, car):
                cr, ci = car
                rows = pl.ds(pl.multiple_of(j * 8, 8), 8)
                r = x_ref[rows, lre]
                im = x_ref[rows, lim]
                for lvl, s in enumerate((1, 2, 4)):
                    mr = t_ref[16 * lvl:16 * lvl + 8, tl]
                    mi = t_ref[16 * lvl + 8:16 * lvl + 16, tl]
                    sr = pltpu.roll(r, s, 0)
                    si = pltpu.roll(im, s, 0)
                    r, im = r + (mr * sr - mi * si), im + (mr * si + mi * sr)
                pr = t_ref[48:56, tl]
                pi_ = t_ref[56:64, tl]
                r, im = r + (pr * cr - pi_ * ci), im + (pr * ci + pi_ * cr)
                x_ref[rows, lre] = r
                x_ref[rows, lim] = im
                return (jnp.broadcast_to(r[7:8, :], (8, _LW)), jnp.broadcast_to(im[7:8, :], (8, _LW)))

            cr, ci = lax.fori_loop(0, tm // 8, blk, (car_ref[:, lre], car_ref[:, lim]))
            car_ref[:, lre] = cr
            car_ref[:, lim] = ci
            cols = slice(c * _UC, (c + 1) * _UC)
            ys = jnp.dot(x_ref[:, c * _CS:(c + 1) * _CS].astype(BF16), cm_ref[c], preferred_element_type=F32)
            ys = ys + d_ref[:, cols] * u[:, cols]
            ys_ref[:, cols] = ys
            yg_ref[:, cols] = _gelu(ys).astype(BF16)

    return pl.pallas_call(
        body, name="ssm_fwd",
        out_shape=(jax.ShapeDtypeStruct((S, 2 * NST), F32), jax.ShapeDtypeStruct((S, CW), F32),
                   jax.ShapeDtypeStruct((S, CW), BF16)),
        grid=(S // tm,),
        in_specs=[pl.BlockSpec((tm, CW), lambda i: (i, 2)), _full((_NCL, _UC, _CS)), _full((_NCL, _CS, _UC)),
                  _full((1, CW)), _full((64, NST))],
        out_specs=(pl.BlockSpec((tm, 2 * NST), lambda i: (i, 0)), pl.BlockSpec((tm, CW), lambda i: (i, 0)),
                   pl.BlockSpec((tm, CW), lambda i: (i, 0))),
        scratch_shapes=[pltpu.VMEM((8, 2 * NST), F32)],
        compiler_params=_cp(("arbitrary",)))(z, bb, cm, d, tab)


def _ssm_bwd(dzz, w_glu, ys, z, xs, cmt, bbt, d, tab, after):
    S = z.shape[0]
    tm = min(512, S)
    nt = S // tm
    tdims = (((0,), (0,)), ((), ()))

    def body(dzz_ref, wglu_ref, ys_ref, u_ref, x_ref, cmt_ref, bbt_ref, d_ref, t_ref, after_ref,
             du_ref, de_ref, dd_ref, dc_hbm, dbb_hbm, car_ref, lam_ref, dc_ref, dbb_ref):
        i = pl.program_id(0)

        @pl.when(i == 0)
        def _():
            car_ref[...] = jnp.zeros_like(car_ref)
            de_ref[...] = jnp.zeros_like(de_ref)
            dd_ref[...] = jnp.zeros_like(dd_ref)
            dc_ref[...] = jnp.zeros_like(dc_ref)
            dbb_ref[...] = jnp.zeros_like(dbb_ref)

        u16 = u_ref[...]
        u = u16.astype(F32)
        dyg = lax.dot_general(dzz_ref[...], wglu_ref[...], (((1,), (1,)), ((), ())), preferred_element_type=F32)
        dys = dyg * _gelu_grad(ys_ref[...])
        dys16 = dys.astype(BF16)
        dd_ref[...] += _colsum8(dys * u)
        row = lax.broadcasted_iota(jnp.int32, (8, _LW), 0)
        for c in range(_NCL):
            lre = pl.ds(c * _CS, _LW)
            lim = pl.ds(c * _CS + _LW, _LW)
            tl = pl.ds(c * _LW, _LW)
            cols = slice(c * _UC, (c + 1) * _UC)
            span = slice(c * _CS, (c + 1) * _CS)
            dc_ref[cols, :] += lax.dot_general(dys16[:, cols], x_ref[:, span].astype(BF16), tdims,
                                               preferred_element_type=F32)
            lam_ref[...] = jnp.dot(dys16[:, cols], cmt_ref[c], preferred_element_type=F32)

            def blk(jj, car):
                cr, ci, ar, ai = car
                j = tm // 8 - 1 - jj
                rows = pl.ds(pl.multiple_of(j * 8, 8), 8)
                r = lam_ref[rows, 0:_LW]
                im = lam_ref[rows, _LW:_CS]
                for lvl, s in enumerate((1, 2, 4)):
                    mr = t_ref[16 * lvl:16 * lvl + 8, tl]
                    mi = t_ref[16 * lvl + 8:16 * lvl + 16, tl]
                    sr = pltpu.roll(r, 8 - s, 0)
                    si = pltpu.roll(im, 8 - s, 0)
                    r, im = r + (mr * sr - mi * si), im + (mr * si + mi * sr)
                pr = t_ref[48:56, tl]
                pi_ = t_ref[56:64, tl]
                r, im = r + (pr * cr - pi_ * ci), im + (pr * ci + pi_ * cr)
                lam_ref[rows, 0:_LW] = r
                lam_ref[rows, _LW:_CS] = im
                nr = jnp.where(row == 7, cr, pltpu.roll(r, 7, 0))
                ni = jnp.where(row == 7, ci, pltpu.roll(im, 7, 0))
                xr = x_ref[rows, lre]
                xi = x_ref[rows, lim]
                ar = ar + (nr * xr + ni * xi)
                ai = ai + (ni * xr - nr * xi)
                return (jnp.broadcast_to(r[0:1, :], (8, _LW)), jnp.broadcast_to(im[0:1, :], (8, _LW)), ar, ai)

            zero = jnp.zeros((8, _LW), F32)
            cr, ci, ar, ai = lax.fori_loop(0, tm // 8, blk, (car_ref[:, lre], car_ref[:, lim], zero, zero))
            car_ref[:, lre] = cr
            car_ref[:, lim] = ci
            de_ref[0:8, tl] += ar
            de_ref[8:16, tl] += ai
            lam16 = lam_ref[...].astype(BF16)
            dbb_ref[cols, :] += lax.dot_general(u16[:, cols], lam16, tdims, preferred_element_type=F32)
            du = jnp.dot(lam16, bbt_ref[c], preferred_element_type=F32) + dys[:, cols] * d_ref[:, cols]
            du_ref[:, cols] = du.astype(BF16)

        @pl.when(i == nt - 1)
        def _():
            pltpu.sync_copy(dc_ref, dc_hbm)
            pltpu.sync_copy(dbb_ref, dbb_hbm)

    rev = lambda i: (nt - 1 - i, 0)
    once = lambda shape: pl.BlockSpec(shape, lambda *_: (0,) * len(shape), pipeline_mode=pl.Buffered(1))
    cross = jax.ShapeDtypeStruct((CW, _CS), F32)
    return pl.pallas_call(
        body, name="ssm_bwd",
        out_shape=(jax.ShapeDtypeStruct((S, CW), BF16), jax.ShapeDtypeStruct((16, NST), F32),
                   jax.ShapeDtypeStruct((8, CW), F32), cross, cross),
        grid=(nt,),
        in_specs=[pl.BlockSpec((tm, 2 * D_MODEL), rev), once((CW, 2 * D_MODEL)), pl.BlockSpec((tm, CW), rev),
                  pl.BlockSpec((tm, CW), lambda i: (nt - 1 - i, 2)), pl.BlockSpec((tm, 2 * NST), rev),
                  once((_NCL, _UC, _CS)), once((_NCL, _CS, _UC)), _full((1, CW)), once((64, NST)),
                  pl.BlockSpec(memory_space=pl.ANY)],
        out_specs=(pl.BlockSpec((tm, CW), rev), _full((16, NST)), _full((8, CW)),
                   pl.BlockSpec(memory_space=pl.ANY), pl.BlockSpec(memory_space=pl.ANY)),
        scratch_shapes=[pltpu.VMEM((8, 2 * NST), F32), pltpu.VMEM((tm, _CS), F32),
                        pltpu.VMEM((CW, _CS), F32), pltpu.VMEM((CW, _CS), F32)],
        compiler_params=_cp(("arbitrary",)))(dzz, w_glu, ys, z, xs, cmt, bbt, d, tab, after)


def _ssm_prep(a_re, a_im, b_re, b_im, log_dt):
    dt = jnp.exp(log_dt.reshape(G))[:, None]
    mag = jnp.exp(dt * a_re)
    e_re, e_im = mag * jnp.cos(dt * a_im), mag * jnp.sin(dt * a_im)
    n_re, n_im = e_re - 1.0, e_im
    den = a_re * a_re + a_im * a_im
    q_re = (n_re * a_re + n_im * a_im) / den
    q_im = (n_im * a_re - n_re * a_im) / den
    bb_re = q_re[..., None] * b_re - q_im[..., None] * b_im
    bb_im = q_re[..., None] * b_im + q_im[..., None] * b_re
    return e_re, e_im, bb_re, bb_im


def _scan_tables(e_re, e_im, reverse):
    er = e_re.reshape(1, NST)
    ei = e_im.reshape(1, NST)
    if reverse:
        ei = -ei
    pows = [(er, ei)]
    for _ in range(7):
        pr, pi_ = pows[-1]
        pows.append((pr * er - pi_ * ei, pr * ei + pi_ * er))
    row = jnp.arange(8)[:, None]
    out = []
    for s in (1, 2, 4):
        pr, pi_ = pows[s - 1]
        keep = (row + s <= 7) if reverse else (row >= s)
        out += [jnp.where(keep, pr, 0.0), jnp.where(keep, pi_, 0.0)]
    allr = jnp.concatenate([p[0] for p in pows], 0)
    alli = jnp.concatenate([p[1] for p in pows], 0)
    if reverse:
        allr, alli = allr[::-1], alli[::-1]
    out += [allr, alli]
    return jnp.concatenate(out, 0).astype(F32)


def _block_diag_mats(bb_re, bb_im, c_re, c_im):
    gc = G // _NCL
    eye = jnp.eye(gc, dtype=F32)
    bre = jnp.einsum("cjph,jk->cjhkp", bb_re.reshape(_NCL, gc, P, H), eye).reshape(_NCL, _UC, _LW)
    bim = jnp.einsum("cjph,jk->cjhkp", bb_im.reshape(_NCL, gc, P, H), eye).reshape(_NCL, _UC, _LW)
    bb = jnp.concatenate([bre, bim], 2)
    cre = jnp.einsum("cjhp,jk->cjpkh", c_re.reshape(_NCL, gc, H, P), eye).reshape(_NCL, _LW, _UC)
    cim = jnp.einsum("cjhp,jk->cjpkh", c_im.reshape(_NCL, gc, H, P), eye).reshape(_NCL, _LW, _UC)
    cm = jnp.concatenate([cre, -cim], 1)
    return bb, cm


def _diag_blocks(cross, imag):
    gc = G // _NCL
    off = _LW if imag else 0
    return jnp.stack([cross[H * g:H * (g + 1), off + P * (g % gc):off + P * (g % gc + 1)] for g in range(G)])


def _mix_fwd(scv, yg, z, x, w_cp, w_glu, w_out, g1, n2g, sc2, sh2):
    S = z.shape[0]
    tm = min(512, S)
    D = D_MODEL

    def body(s_ref, yg_ref, glc0_ref, glc1_ref, gls0_ref, gls1_ref, x_ref, wcp_ref, wglu_ref, wout_ref,
             g1_ref, n2_ref, sc_ref, sh_ref, yc_ref, zz_ref, m_ref, o_ref, x2_ref, h2_ref):
        y_conv = jnp.dot(s_ref[...], wcp_ref[...], preferred_element_type=F32)
        zz = jnp.dot(yg_ref[...], wglu_ref[...], preferred_element_type=F32)
        yc_ref[...] = y_conv.astype(BF16)
        zz_ref[...] = zz.astype(BF16)
        for half, (glc_ref, gls_ref) in enumerate(((glc0_ref, gls0_ref), (glc1_ref, gls1_ref))):
            lo, hi = half * CW, (half + 1) * CW
            y_ssm = zz[:, lo:hi] * _sig(zz[:, D + lo:D + hi])
            m_ref[:, lo:hi] = (_sig(glc_ref[...].astype(F32)) * y_conv[:, lo:hi]
                               + _sig(gls_ref[...].astype(F32)) * y_ssm).astype(BF16)
        o = jnp.dot(m_ref[...], wout_ref[...], preferred_element_type=F32)
        o_ref[...] = o.astype(BF16)
        xv = x_ref[...] + g1_ref[...] * o
        x2_ref[...] = xv
        r = lax.rsqrt(jnp.mean(xv * xv, axis=-1, keepdims=True) + EPS)
        h2_ref[...] = (xv * r * (n2_ref[...] * (1.0 + sc_ref[...])) + sh_ref[...]).astype(BF16)

    zb_ = lambda j: pl.BlockSpec((tm, CW), lambda i: (i, j))
    row = lambda w: pl.BlockSpec((tm, w), lambda i: (i, 0))
    par = _full((1, D))
    bf = lambda w: jax.ShapeDtypeStruct((S, w), BF16)
    return pl.pallas_call(
        body, name="mix_fwd",
        out_shape=(bf(D), bf(2 * D), bf(D), bf(D), jax.ShapeDtypeStruct((S, D), F32), bf(D)),
        grid=(S // tm,),
        in_specs=[row(CW), row(CW), zb_(3), zb_(4), zb_(5), zb_(6), row(D), _resident((CW, D)),
                  _resident((CW, 2 * D)), _resident((D, D)), par, par, par, par],
        out_specs=(row(D), row(2 * D), row(D), row(D), row(D), row(D)),
        compiler_params=_cp(("parallel",)))(scv, yg, z, z, z, z, x, w_cp, w_glu, w_out, g1, n2g, sc2, sh2)


def _mix_bwd(do, w_out, z, zz, y_conv, after):
    S = z.shape[0]
    tm = min(512, S)
    D = D_MODEL

    def body(do_ref, w_ref, glc0_ref, glc1_ref, gls0_ref, gls1_ref, za_ref, zb_ref, yc_ref, after_ref,
             dyc_ref, dgl_ref, dzz_ref):
        dm = lax.dot_general(do_ref[...], w_ref[...], (((1,), (1,)), ((), ())), preferred_element_type=F32)
        for half, (glc_ref, gls_ref) in enumerate(((glc0_ref, gls0_ref), (glc1_ref, gls1_ref))):
            lo, hi = half * CW, (half + 1) * CW
            dm_v = dm[:, lo:hi]
            sgc = _sig(glc_ref[...].astype(F32))
            sgs = _sig(gls_ref[...].astype(F32))
            szb = _sig(zb_ref[:, lo:hi].astype(F32))
            za = za_ref[:, lo:hi].astype(F32)
            dyc_ref[:, lo:hi] = (dm_v * sgc).astype(BF16)
            dgl_ref[:, lo:hi] = (dm_v * yc_ref[:, lo:hi].astype(F32) * sgc * (1.0 - sgc)).astype(BF16)
            dys = dm_v * sgs
            dgl_ref[:, D + lo:D + hi] = (dys * (za * szb) * (1.0 - sgs)).astype(BF16)
            dzz_ref[:, lo:hi] = (dys * szb).astype(BF16)
            dzz_ref[:, D + lo:D + hi] = (dys * za * szb * (1.0 - szb)).astype(BF16)

    zb_ = lambda j: pl.BlockSpec((tm, CW), lambda i: (i, j))
    wide = lambda j: pl.BlockSpec((tm, D), lambda i: (i, j))
    return pl.pallas_call(
        body, name="mix_bwd",
        out_shape=(jax.ShapeDtypeStruct((S, D), BF16), jax.ShapeDtypeStruct((S, 2 * D), BF16),
                   jax.ShapeDtypeStruct((S, 2 * D), BF16)),
        grid=(S // tm,),
        in_specs=[wide(0), _resident((D, D)), zb_(3), zb_(4), zb_(5), zb_(6), wide(0), wide(1), wide(0),
                  pl.BlockSpec(memory_space=pl.ANY)],
        out_specs=(wide(0), pl.BlockSpec((tm, 2 * D), lambda i: (i, 0)), pl.BlockSpec((tm, 2 * D), lambda i: (i, 0))),
        compiler_params=_cp(("parallel",)))(do, w_out, z, z, z, z, zz, zz, y_conv, after)


_FC = 1408


def _ffn_in_act(h2, w_fi):
    S, D = h2.shape
    tm = min(512, S)

    def body(h_ref, w_ref, f_ref, a_ref):
        hv = h_ref[...]
        for c in range(FH // _FC):
            lo, hi = c * _FC, (c + 1) * _FC
            g = jnp.dot(hv, w_ref[:, lo:hi], preferred_element_type=F32)
            u = jnp.dot(hv, w_ref[:, FH + lo:FH + hi], preferred_element_type=F32)
            f_ref[:, lo:hi] = g.astype(BF16)
            f_ref[:, FH + lo:FH + hi] = u.astype(BF16)
            a_ref[:, lo:hi] = (g * _sig(g) * u).astype(BF16)

    return pl.pallas_call(
        body, name="ffn_in_act",
        out_shape=(jax.ShapeDtypeStruct((S, 2 * FH), BF16), jax.ShapeDtypeStruct((S, FH), BF16)),
        grid=(S // tm,),
        in_specs=[pl.BlockSpec((tm, D), lambda i: (i, 0)), _resident((D, 2 * FH))],
        out_specs=(pl.BlockSpec((tm, 2 * FH), lambda i: (i, 0)), pl.BlockSpec((tm, FH), lambda i: (i, 0))),
        compiler_params=_cp(("parallel",)))(h2, w_fi)


def _ffn_bwd(do2, w_fo, f, after):
    S, D = do2.shape
    tm = min(512, S)

    def body(d_ref, w_ref, f_ref, after_ref, df_ref):
        dv = d_ref[...]
        for c in range(FH // _FC):
            lo, hi = c * _FC, (c + 1) * _FC
            dact = lax.dot_general(dv, w_ref[lo:hi, :], (((1,), (1,)), ((), ())), preferred_element_type=F32)
            g = f_ref[:, lo:hi].astype(F32)
            u = f_ref[:, FH + lo:FH + hi].astype(F32)
            sg = _sig(g)
            df_ref[:, lo:hi] = (dact * u * (sg * (1.0 + g * (1.0 - sg)))).astype(BF16)
            df_ref[:, FH + lo:FH + hi] = (dact * g * sg).astype(BF16)

    return pl.pallas_call(
        body, name="ffn_bwd", out_shape=jax.ShapeDtypeStruct((S, 2 * FH), BF16), grid=(S // tm,),
        in_specs=[pl.BlockSpec((tm, D), lambda i: (i, 0)), _resident((FH, D)),
                  pl.BlockSpec((tm, 2 * FH), lambda i: (i, 0)), pl.BlockSpec(memory_space=pl.ANY)],
        out_specs=pl.BlockSpec((tm, 2 * FH), lambda i: (i, 0)),
        compiler_params=_cp(("parallel",)))(do2, w_fo, f, after)


def _ffn_out_final(x2, act, w_fo, g2, fg, tgt):
    S, D = x2.shape
    tm = min(512, S)

    def body(x2_ref, a_ref, w_ref, g2_ref, fg_ref, t_ref, dx3_ref, do2_ref, ls_ref, dfg_ref, dg2_ref):
        i = pl.program_id(0)

        @pl.when(i == 0)
        def _():
            ls_ref[...] = jnp.zeros_like(ls_ref)
            dfg_ref[...] = jnp.zeros_like(dfg_ref)
            dg2_ref[...] = jnp.zeros_like(dg2_ref)

        o2 = jnp.dot(a_ref[...], w_ref[...], preferred_element_type=F32)
        x3 = x2_ref[...] + g2_ref[...] * o2
        r = lax.rsqrt(jnp.mean(x3 * x3, axis=-1, keepdims=True) + EPS)
        xn = x3 * r
        err = xn * fg_ref[...] - t_ref[...]
        dy = err * (1.0 / D)
        dxn = dy * fg_ref[...]
        dx3 = r * (dxn - xn * jnp.mean(dxn * xn, axis=-1, keepdims=True))
        dx3_ref[...] = dx3
        do2_ref[...] = (dx3 * g2_ref[...]).astype(BF16)
        e2 = _colsum8(err * err)
        lanes = e2[:, 0:128]
        for q in range(1, D // 128):
            lanes = lanes + e2[:, q * 128:(q + 1) * 128]
        ls_ref[...] += lanes * (0.5 / D)
        dfg_ref[...] += _colsum8(dy * xn)
        dg2_ref[...] += _colsum8(dx3 * o2)

    row = pl.BlockSpec((tm, D), lambda i: (i, 0))
    par = _full((1, D))
    return pl.pallas_call(
        body, name="final_loss",
        out_shape=(jax.ShapeDtypeStruct((S, D), F32), jax.ShapeDtypeStruct((S, D), BF16),
                   jax.ShapeDtypeStruct((8, 128), F32), jax.ShapeDtypeStruct((8, D), F32),
                   jax.ShapeDtypeStruct((8, D), F32)),
        grid=(S // tm,), in_specs=[row, pl.BlockSpec((tm, FH), lambda i: (i, 0)), _resident((FH, D)), par, par, row],
        out_specs=(row, row, _full((8, 128)), _full((8, D)), _full((8, D))),
        compiler_params=_cp(("arbitrary",)))(x2, act, w_fo, g2, fg, tgt)


def _normmod_bwd(dsrc, w, xin, dres, g, sc, gate, o, after, name):
    S, D = xin.shape
    parts = list(dsrc) if isinstance(dsrc, (list, tuple)) else [dsrc]
    widths = [p.shape[1] for p in parts]
    K = sum(widths)
    tm = min(512, S)
    npart = len(parts)

    def body(*refs):
        ds_refs = refs[:npart]
        w_ref, x_ref, dr_ref, g_ref, sc_ref, gate_ref, o_ref, after_ref = refs[npart:npart + 8]
        dx_ref, do_ref, dsh_ref, dsc_ref, dg_ref, dgate_ref = refs[npart + 8:]
        i = pl.program_id(0)

        @pl.when(i == 0)
        def _():
            dsh_ref[...] = jnp.zeros_like(dsh_ref)
            dsc_ref[...] = jnp.zeros_like(dsc_ref)
            dg_ref[...] = jnp.zeros_like(dg_ref)
            dgate_ref[...] = jnp.zeros_like(dgate_ref)

        gv = g_ref[...]
        scale = 1.0 + sc_ref[...]
        xv = x_ref[...]
        r = lax.rsqrt(jnp.mean(xv * xv, axis=-1, keepdims=True) + EPS)
        xn = xv * r
        dh_v, col = None, 0
        for ds_ref, wd in zip(ds_refs, widths):
            t = lax.dot_general(ds_ref[...], w_ref[:, col:col + wd], (((1,), (1,)), ((), ())),
                                preferred_element_type=F32)
            dh_v = t if dh_v is None else dh_v + t
            col += wd
        dxn = dh_v * (gv * scale)
        dx = dr_ref[...] + r * (dxn - xn * jnp.mean(dxn * xn, axis=-1, keepdims=True))
        dx_ref[...] = dx
        do_ref[...] = (dx * gate_ref[...]).astype(BF16)
        hx = dh_v * xn
        dsh_ref[...] += _colsum8(dh_v)
        dsc_ref[...] += _colsum8(hx) * gv
        dg_ref[...] += _colsum8(hx) * scale
        dgate_ref[...] += _colsum8(dx * o_ref[...])

    row = pl.BlockSpec((tm, D), lambda i: (i, 0))
    par = _full((1, D))
    acc = jax.ShapeDtypeStruct((8, D), F32)
    return pl.pallas_call(
        body, name=name,
        out_shape=(jax.ShapeDtypeStruct((S, D), F32), jax.ShapeDtypeStruct((S, D), BF16), acc, acc, acc, acc),
        grid=(S // tm,),
        in_specs=[pl.BlockSpec((tm, wd), lambda i: (i, 0)) for wd in widths]
        + [_resident((D, K)), row, row, par, par, par, row, pl.BlockSpec(memory_space=pl.ANY)],
        out_specs=(row, row, _full((8, D)), _full((8, D)), _full((8, D)), _full((8, D))),
        compiler_params=_cp(("arbitrary",)))(*parts, w, xin, dres, g, sc, gate, o, after)


def _me():
    return lax.axis_index("x"), lax.axis_index("y"), lax.axis_index("c")


def _allgather8(v, name, after=()):
    R, C = v.shape
    after = list(after)

    def body(v_ref, *rest):
        out_ref, send_sems, recv_sems, local_sem = rest[len(after):]
        x, y, c = _me()
        mine = pltpu.make_async_copy(v_ref, out_ref.at[4 * x + 2 * y + c], local_sem)
        mine.start()
        copies = []
        for k in range(1, N_DEV):
            fx, fy, fc = (k >> 2) & 1, (k >> 1) & 1, k & 1
            peer = (x ^ fx, y ^ fy, c ^ fc)
            copies.append(pltpu.make_async_remote_copy(
                src_ref=v_ref, dst_ref=out_ref.at[4 * x + 2 * y + c],
                send_sem=send_sems.at[k - 1], recv_sem=recv_sems.at[k - 1],
                device_id=peer, device_id_type=MESH))
        for cp in copies:
            cp.start()
        for k in range(1, N_DEV):
            fx, fy, fc = (k >> 2) & 1, (k >> 1) & 1, k & 1
            src_slot = 4 * (x ^ fx) + 2 * (y ^ fy) + (c ^ fc)
            pltpu.make_async_remote_copy(
                src_ref=v_ref, dst_ref=out_ref.at[src_slot],
                send_sem=send_sems.at[k - 1], recv_sem=recv_sems.at[k - 1],
                device_id=(x ^ fx, y ^ fy, c ^ fc), device_id_type=MESH).wait_recv()
        for cp in copies:
            cp.wait_send()
        mine.wait()

    return pl.pallas_call(
        body, name=name, out_shape=jax.ShapeDtypeStruct((N_DEV, R, C), v.dtype),
        in_specs=[pl.BlockSpec(memory_space=pltpu.VMEM)] + [pl.BlockSpec(memory_space=pl.ANY)] * len(after),
        out_specs=pl.BlockSpec(memory_space=pltpu.VMEM),
        scratch_shapes=[pltpu.SemaphoreType.DMA((N_DEV - 1,)), pltpu.SemaphoreType.DMA((N_DEV - 1,)),
                        pltpu.SemaphoreType.DMA],
        compiler_params=pltpu.CompilerParams(vmem_limit_bytes=VMEM_LIMIT))(v, *after)


_HBM = pl.BlockSpec(memory_space=pltpu.HBM)
_SEM = pl.BlockSpec(memory_space=pltpu.SEMAPHORE)
_EFFECT = pltpu.SideEffectType.DATAFLOW_SIDE_EFFECTING
_N_PEER = N_CHIP - 1


def _chip_part(ref, axis, n, chip):
    start = pl.multiple_of(chip * n, 8)
    return ref.at[pl.ds(start, n), :] if axis == 0 else ref.at[:, pl.ds(start, n)]


def _gather_copy(k, src_ref, land_ref, send_sems, recv_sems, axis, arriving):
    x, y, c = _me()
    px, py = x ^ ((k >> 1) & 1), y ^ (k & 1)
    chip = 2 * px + py if arriving else 2 * x + y
    return pltpu.make_async_remote_copy(
        src_ref=src_ref, dst_ref=_chip_part(land_ref, axis, src_ref.shape[axis], chip),
        send_sem=send_sems.at[k - 1], recv_sem=recv_sems.at[k - 1], device_id=(px, py, c), device_id_type=MESH)


def _scatter_copy(k, grad_ref, land_ref, send_sems, recv_sems, axis):
    x, y, c = _me()
    px, py = x ^ ((k >> 1) & 1), y ^ (k & 1)
    return pltpu.make_async_remote_copy(
        src_ref=_chip_part(grad_ref, axis, grad_ref.shape[axis] // N_CHIP, 2 * px + py), dst_ref=land_ref.at[k],
        send_sem=send_sems.at[k - 1], recv_sem=recv_sems.at[k - 1], device_id=(px, py, c), device_id_type=MESH)


def _scatter_own(grad_ref, land_ref, send_sems, axis):
    x, y, _ = _me()
    return pltpu.make_async_copy(_chip_part(grad_ref, axis, grad_ref.shape[axis] // N_CHIP, 2 * x + y),
                                 land_ref.at[0], send_sems.at[_N_PEER])


def _own_copy(src_ref, land_ref, sends, axis):
    x, y, _ = _me()
    return pltpu.make_async_copy(src_ref, _chip_part(land_ref, axis, src_ref.shape[axis], 2 * x + y),
                                 sends.at[_N_PEER])


def _gather_start(shards, axes, after, name):
    nw = len(shards)
    lands = []
    for s, ax in zip(shards, axes):
        shp = list(s.shape)
        shp[ax] *= N_CHIP
        lands.append(lax.empty(tuple(shp), s.dtype))

    def body(*refs):
        srcs, zones = refs[:nw], refs[nw:2 * nw]
        sends, recvs = refs[2 * nw + 1:3 * nw + 1], refs[3 * nw + 1:4 * nw + 1]
        token = refs[-1]
        for w in range(nw):
            for k in range(1, N_CHIP):
                _gather_copy(k, srcs[w], zones[w], sends[w], recvs[w], axes[w], False).start()
        for w in range(nw):
            _own_copy(srcs[w], zones[w], sends[w], axes[w]).start()
        token[...] = jnp.zeros_like(token)

    outs = pl.pallas_call(
        body, name=name,
        out_shape=tuple([pltpu.SemaphoreType.DMA((_N_PEER + 1,))] * nw + [pltpu.SemaphoreType.DMA((_N_PEER,))] * nw
                        + [pltpu.HBM(a.shape, a.dtype) for a in list(shards) + list(lands)]
                        + [jax.ShapeDtypeStruct((8, 128), F32)]),
        in_specs=[_HBM] * (2 * nw) + [pl.BlockSpec(memory_space=pl.ANY)],
        out_specs=tuple([_SEM] * (2 * nw) + [_HBM] * (2 * nw) + [pl.BlockSpec(memory_space=pltpu.VMEM)]),
        input_output_aliases={i: 2 * nw + i for i in range(2 * nw)},
        compiler_params=pltpu.CompilerParams(has_side_effects=_EFFECT),
    )(*([pltpu.with_memory_space_constraint(a, pltpu.HBM) for a in list(shards) + list(lands)] + [after]))
    per_weight = [(outs[w], outs[nw + w], outs[2 * nw + w], outs[3 * nw + w]) for w in range(nw)]
    return per_weight, outs[-1]


def _gather_wait(state, axis, after, name):
    send_sems, recv_sems, shard, land = state

    after = list(after) if isinstance(after, (list, tuple)) else [after]

    def body(src_ref, land_ref, sends, recvs, *rest):
        for k in range(1, N_CHIP):
            _gather_copy(k, src_ref, land_ref, sends, recvs, axis, False).wait_send()
            _gather_copy(k, src_ref, land_ref, sends, recvs, axis, True).wait_recv()
        _own_copy(src_ref, land_ref, sends, axis).wait()

    return pl.pallas_call(
        body, name=name, out_shape=(pltpu.HBM(shard.shape, shard.dtype), pltpu.HBM(land.shape, land.dtype)),
        in_specs=[_HBM, _HBM, _SEM, _SEM] + [pl.BlockSpec(memory_space=pl.ANY)] * len(after), out_specs=(_HBM, _HBM),
        input_output_aliases={0: 0, 1: 1},
        compiler_params=pltpu.CompilerParams(has_side_effects=_EFFECT),
    )(shard, land, send_sems, recv_sems, *after)[1]


def _half_rows(ref, c):
    k2 = ref.shape[0] // 2
    return pl.ds(pl.multiple_of(c * k2, 8), k2)


def _half_copy(k, shard_ref, land_ref, send_sems, recv_sems, arriving):
    x, y, c = _me()
    px, py = x ^ ((k >> 1) & 1), y ^ (k & 1)
    n = shard_ref.shape[1]
    chip = 2 * px + py if arriving else 2 * x + y
    return pltpu.make_async_remote_copy(
        src_ref=shard_ref.at[_half_rows(shard_ref, c), :],
        dst_ref=land_ref.at[_half_rows(land_ref, c), pl.ds(pl.multiple_of(chip * n, 128), n)],
        send_sem=send_sems.at[k - 1], recv_sem=recv_sems.at[k - 1], device_id=(px, py, c), device_id_type=MESH)


def _half_own(shard_ref, land_ref, send_sems):
    x, y, c = _me()
    n = shard_ref.shape[1]
    return pltpu.make_async_copy(
        shard_ref.at[_half_rows(shard_ref, c), :],
        land_ref.at[_half_rows(land_ref, c), pl.ds(pl.multiple_of((2 * x + y) * n, 128), n)], send_sems.at[_N_PEER])


def _half_gather_start(shard, after, name):
    K, n = shard.shape
    land = lax.empty((K, N_CHIP * n), shard.dtype)

    def body(shard_ref, land_ref, after_ref, sends, recvs, shard_thru, land_thru, token):
        for k in range(1, N_CHIP):
            _half_copy(k, shard_ref, land_ref, sends, recvs, False).start()
        _half_own(shard_ref, land_ref, sends).start()
        token[...] = jnp.zeros_like(token)

    outs = pl.pallas_call(
        body, name=name,
        out_shape=(pltpu.SemaphoreType.DMA((_N_PEER + 1,)), pltpu.SemaphoreType.DMA((_N_PEER,)),
                   pltpu.HBM(shard.shape, shard.dtype), pltpu.HBM(land.shape, land.dtype),
                   jax.ShapeDtypeStruct((8, 128), F32)),
        in_specs=[_HBM, _HBM, pl.BlockSpec(memory_space=pl.ANY)],
        out_specs=(_SEM, _SEM, _HBM, _HBM, pl.BlockSpec(memory_space=pltpu.VMEM)),
        input_output_aliases={0: 2, 1: 3},
        compiler_params=pltpu.CompilerParams(has_side_effects=_EFFECT),
    )(pltpu.with_memory_space_constraint(shard, pltpu.HBM), pltpu.with_memory_space_constraint(land, pltpu.HBM), after)
    return outs[:4], outs[4]


def _half_gather_wait(state, after, name):
    send_sems, recv_sems, shard, land = state
    after = list(after)

    def body(shard_ref, land_ref, sends, recvs, *rest):
        for k in range(1, N_CHIP):
            _half_copy(k, shard_ref, land_ref, sends, recvs, False).wait_send()
            _half_copy(k, shard_ref, land_ref, sends, recvs, True).wait_recv()
        _half_own(shard_ref, land_ref, sends).wait()

    return pl.pallas_call(
        body, name=name, out_shape=(pltpu.HBM(shard.shape, shard.dtype), pltpu.HBM(land.shape, land.dtype)),
        in_specs=[_HBM, _HBM, _SEM, _SEM] + [pl.BlockSpec(memory_space=pl.ANY)] * len(after), out_specs=(_HBM, _HBM),
        input_output_aliases={0: 0, 1: 1},
        compiler_params=pltpu.CompilerParams(has_side_effects=_EFFECT),
    )(shard, land, send_sems, recv_sems, *after)[1]


def _half_swap_copy(land_ref, send_sem, recv_sem, arriving):
    x, y, c = _me()
    rows = _half_rows(land_ref, 1 - c if arriving else c)
    return pltpu.make_async_remote_copy(src_ref=land_ref.at[rows, :], dst_ref=land_ref.at[rows, :], send_sem=send_sem,
                                        recv_sem=recv_sem, device_id=(x, y, 1 - c), device_id_type=MESH)


def _half_swap_start(land, name):
    def body(land_ref, send, recv, land_thru, token):
        _half_swap_copy(land_ref, send.at[0], recv.at[0], False).start()
        token[...] = jnp.zeros_like(token)

    sem = pltpu.SemaphoreType.DMA((1,))
    outs = pl.pallas_call(
        body, name=name,
        out_shape=(sem, sem, pltpu.HBM(land.shape, land.dtype), jax.ShapeDtypeStruct((8, 128), F32)),
        in_specs=[_HBM], out_specs=(_SEM, _SEM, _HBM, pl.BlockSpec(memory_space=pltpu.VMEM)),
        input_output_aliases={0: 2},
        compiler_params=pltpu.CompilerParams(has_side_effects=_EFFECT),
    )(pltpu.with_memory_space_constraint(land, pltpu.HBM))
    return outs[:3], outs[3]


def _half_swap_wait(state, after, name):
    send, recv, land = state

    def body(land_ref, send_ref, recv_ref, after_ref, got_ref):
        _half_swap_copy(land_ref, send_ref.at[0], recv_ref.at[0], False).wait_send()
        _half_swap_copy(land_ref, send_ref.at[0], recv_ref.at[0], True).wait_recv()

    return pl.pallas_call(
        body, name=name, out_shape=pltpu.HBM(land.shape, land.dtype),
        in_specs=[_HBM, _SEM, _SEM, pl.BlockSpec(memory_space=pl.ANY)], out_specs=_HBM,
        input_output_aliases={0: 0},
        compiler_params=pltpu.CompilerParams(has_side_effects=_EFFECT),
    )(land, send, recv, after)


def _all8_copy(k, v_ref, land_ref, send_sems, recv_sems, arriving):
    x, y, c = _me()
    px, py, pc = x ^ ((k >> 2) & 1), y ^ ((k >> 1) & 1), c ^ (k & 1)
    slot = 4 * px + 2 * py + pc if arriving else 4 * x + 2 * y + c
    return pltpu.make_async_remote_copy(
        src_ref=v_ref, dst_ref=land_ref.at[slot], send_sem=send_sems.at[k - 1], recv_sem=recv_sems.at[k - 1],
        device_id=(px, py, pc), device_id_type=MESH)


def _all8_own(v_ref, land_ref, send_sems):
    x, y, c = _me()
    return pltpu.make_async_copy(v_ref, land_ref.at[4 * x + 2 * y + c], send_sems.at[N_DEV - 1])


def _all8_start(v, name):
    land = lax.empty((N_DEV,) + v.shape, v.dtype)

    def body(v_ref, land_ref, sends, recvs, v_thru, land_thru, token):
        for k in range(1, N_DEV):
            _all8_copy(k, v_ref, land_ref, sends, recvs, False).start()
        _all8_own(v_ref, land_ref, sends).start()
        token[...] = jnp.zeros_like(token)

    outs = pl.pallas_call(
        body, name=name,
        out_shape=(pltpu.SemaphoreType.DMA((N_DEV,)), pltpu.SemaphoreType.DMA((N_DEV - 1,)),
                   pltpu.HBM(v.shape, v.dtype), pltpu.HBM(land.shape, land.dtype),
                   jax.ShapeDtypeStruct((8, 128), F32)),
        in_specs=[_HBM, _HBM], out_specs=(_SEM, _SEM, _HBM, _HBM, pl.BlockSpec(memory_space=pltpu.VMEM)),
        input_output_aliases={0: 2, 1: 3},
        compiler_params=pltpu.CompilerParams(has_side_effects=_EFFECT),
    )(pltpu.with_memory_space_constraint(v, pltpu.HBM), pltpu.with_memory_space_constraint(land, pltpu.HBM))
    return outs[:4], outs[4]


def _all8_wait(state, after, name):
    send_sems, recv_sems, v, land = state

    def body(v_ref, land_ref, sends, recvs, after_ref, v_dead, got_ref):
        for k in range(1, N_DEV):
            _all8_copy(k, v_ref, land_ref, sends, recvs, False).wait_send()
            _all8_copy(k, v_ref, land_ref, sends, recvs, True).wait_recv()
        _all8_own(v_ref, land_ref, sends).wait()

    return pl.pallas_call(
        body, name=name, out_shape=(pltpu.HBM(v.shape, v.dtype), pltpu.HBM(land.shape, land.dtype)),
        in_specs=[_HBM, _HBM, _SEM, _SEM, pl.BlockSpec(memory_space=pl.ANY)], out_specs=(_HBM, _HBM),
        input_output_aliases={0: 0, 1: 1},
        compiler_params=pltpu.CompilerParams(has_side_effects=_EFFECT),
    )(v, land, send_sems, recv_sems, after)[1]


def _swap_copy(w, src_ref, land_ref, send_sems, recv_sems):
    x, y, c = _me()
    return pltpu.make_async_remote_copy(src_ref=src_ref, dst_ref=land_ref, send_sem=send_sems.at[w],
                                        recv_sem=recv_sems.at[w], device_id=(x, y, 1 - c), device_id_type=MESH)


def _swap_start(arrs, after, name):
    nw = len(arrs)
    lands = [lax.empty(a.shape, a.dtype) for a in arrs]

    def body(*refs):
        srcs, zones = refs[:nw], refs[nw:2 * nw]
        sends, recvs = refs[2 * nw + 1], refs[2 * nw + 2]
        for w in range(nw):
            _swap_copy(w, srcs[w], zones[w], sends, recvs).start()
        refs[-1][...] = jnp.zeros_like(refs[-1])

    sem = pltpu.SemaphoreType.DMA((nw,))
    outs = pl.pallas_call(
        body, name=name,
        out_shape=tuple([sem, sem] + [pltpu.HBM(a.shape, a.dtype) for a in list(arrs) + lands]
                        + [jax.ShapeDtypeStruct((8, 128), F32)]),
        in_specs=[_HBM] * (2 * nw) + [pl.BlockSpec(memory_space=pl.ANY)],
        out_specs=tuple([_SEM, _SEM] + [_HBM] * (2 * nw) + [pl.BlockSpec(memory_space=pltpu.VMEM)]),
        input_output_aliases={i: 2 + i for i in range(2 * nw)},
        compiler_params=pltpu.CompilerParams(has_side_effects=_EFFECT),
    )(*([pltpu.with_memory_space_constraint(a, pltpu.HBM) for a in list(arrs) + lands] + [after]))
    return (outs[0], outs[1], outs[2:2 + nw], outs[2 + nw:2 + 2 * nw]), outs[-1]


def _swap_wait(state, after, name):
    send_sems, recv_sems, arrs, lands = state
    nw = len(arrs)

    def body(*refs):
        srcs, zones = refs[:nw], refs[nw:2 * nw]
        sends, recvs = refs[2 * nw], refs[2 * nw + 1]
        for w in range(nw):
            cp = _swap_copy(w, srcs[w], zones[w], sends, recvs)
            cp.wait_send()
            cp.wait_recv()

    outs = pl.pallas_call(
        body, name=name, out_shape=tuple(pltpu.HBM(a.shape, a.dtype) for a in list(arrs) + list(lands)),
        in_specs=[_HBM] * (2 * nw) + [_SEM, _SEM, pl.BlockSpec(memory_space=pl.ANY)],
        out_specs=tuple([_HBM] * (2 * nw)),
        input_output_aliases={i: i for i in range(2 * nw)},
        compiler_params=pltpu.CompilerParams(has_side_effects=_EFFECT),
    )(*arrs, *lands, send_sems, recv_sems, after)
    return list(outs[:nw]), list(outs[nw:])


def _scatter_start(grad, axis, name):
    shp = list(grad.shape)
    shp[axis] //= N_CHIP
    land = lax.empty((N_CHIP,) + tuple(shp), grad.dtype)

    def body(grad_ref, land_ref, sends, recvs, grad_thru, land_thru, token):
        for k in range(1, N_CHIP):
            _scatter_copy(k, grad_ref, land_ref, sends, recvs, axis).start()
        _scatter_own(grad_ref, land_ref, sends, axis).start()
        token[...] = jnp.zeros_like(token)

    outs = pl.pallas_call(
        body, name=name,
        out_shape=(pltpu.SemaphoreType.DMA((_N_PEER + 1,)), pltpu.SemaphoreType.DMA((_N_PEER,)),
                   pltpu.HBM(grad.shape, grad.dtype), pltpu.HBM(land.shape, land.dtype),
                   jax.ShapeDtypeStruct((8, 128), F32)),
        in_specs=[_HBM, _HBM], out_specs=(_SEM, _SEM, _HBM, _HBM, pl.BlockSpec(memory_space=pltpu.VMEM)),
        input_output_aliases={0: 2, 1: 3},
        compiler_params=pltpu.CompilerParams(has_side_effects=_EFFECT),
    )(pltpu.with_memory_space_constraint(grad, pltpu.HBM), pltpu.with_memory_space_constraint(land, pltpu.HBM))
    return outs[:4], outs[4]


def _scatter_wait(state, axis, after, name):
    send_sems, recv_sems, grad, land = state

    def body(grad_ref, land_ref, sends, recvs, after_ref, grad_dead, got_ref):
        for k in range(1, N_CHIP):
            cp = _scatter_copy(k, grad_ref, land_ref, sends, recvs, axis)
            cp.wait_send()
            cp.wait_recv()
        _scatter_own(grad_ref, land_ref, sends, axis).wait()

    return pl.pallas_call(
        body, name=name, out_shape=(pltpu.HBM(grad.shape, grad.dtype), pltpu.HBM(land.shape, land.dtype)),
        in_specs=[_HBM, _HBM, _SEM, _SEM, pl.BlockSpec(memory_space=pl.ANY)], out_specs=(_HBM, _HBM),
        input_output_aliases={0: 0, 1: 1},
        compiler_params=pltpu.CompilerParams(has_side_effects=_EFFECT),
    )(grad, land, send_sems, recv_sems, after)[1]


_C1 = 1.0 - B1 ** STEP
_C2 = 1.0 - B2 ** STEP


def _adam_math(w, g, m, v):
    m = B1 * m + (1.0 - B1) * g
    v = B2 * v + (1.0 - B2) * (g * g)
    delta = -LR * ((m / _C1) / (jnp.sqrt(v / _C2) + AEPS) + WD * w)
    return delta, m, v


def _adamw(w, m, v, groups, name):
    R, C = w.shape
    tr = R if R <= 256 else (128 if R % 128 == 0 else 176)
    assert R % tr == 0, (name, R)
    gparts = [p for grp in groups for p in grp]
    sizes = [len(grp) for grp in groups]
    ng = len(gparts)

    def body(*refs):
        w_ref, m_ref, v_ref = refs[:3]
        g_refs = list(refs[3:3 + ng])
        g_out, d_out, m_out, v_out = refs[3 + ng:]
        g = None
        for size in sizes:
            s = None
            for r in [g_refs.pop(0) for _ in range(size)]:
                terms = [r[q] for q in range(r.shape[0])] if len(r.shape) == 3 else [r[...]]
                for t in terms:
                    s = t.astype(F32) if s is None else s + t.astype(F32)
            g = s if g is None else g + s
        delta, mn, vn = _adam_math(w_ref[...], g, m_ref[...], v_ref[...])
        g_out[...] = g
        d_out[...] = delta
        m_out[...] = mn
        v_out[...] = vn

    blk = pl.BlockSpec((tr, C), lambda i: (i, 0))
    g_specs = [blk if p.ndim == 2 else pl.BlockSpec((p.shape[0], tr, C), lambda i: (0, i, 0)) for p in gparts]
    sds = jax.ShapeDtypeStruct((R, C), F32)
    return pl.pallas_call(
        body, name=name, out_shape=(sds, sds, sds, sds), grid=(R // tr,),
        in_specs=[blk, blk, blk] + g_specs, out_specs=(blk, blk, blk, blk),
        compiler_params=_cp(("parallel",)))(w, m, v, *gparts)


def _adamw_small(stack, names, wts, mom, var, sum_only, name):
    items, row = [], 0
    for n in names:
        shape = (KW, CW) if n == "conv_w" else wts[n].shape
        size = int(np.prod(shape))
        vec = len(shape) == 2 and shape[0] == 1 and n not in sum_only
        view = shape if vec else (-(-size // _PACK_COLS), _PACK_COLS)
        items.append((n, row, size, vec, view))
        row += _pack_rows(shape)
    upd = [it for it in items if it[0] not in sum_only]
    operands = [stack]
    for n, _, _, _, view in upd:
        operands += [d[n].reshape(view) for d in (wts, mom, var)]

    def grad(stack_ref, r0, nrows, ncols):
        g = stack_ref[0, r0:r0 + nrows, 0:ncols]
        for q in range(1, N_DEV):
            g = g + stack_ref[q, r0:r0 + nrows, 0:ncols]
        return g

    def body(*refs):
        stack_ref, ins, outs = refs[0], refs[1:1 + 3 * len(upd)], refs[1 + 3 * len(upd):]
        o = 0
        for idx, (n, r0, size, vec, view) in enumerate(upd):
            w_ref, m_ref, v_ref = ins[3 * idx:3 * idx + 3]
            g_out, d_out, m_out, v_out = outs[o:o + 4]
            o += 4
            if vec:
                pieces = [(j, j * _PACK_COLS, min((j + 1) * _PACK_COLS, size)) for j in range(-(-size // _PACK_COLS))]
            else:
                pieces = [(None, 0, _PACK_COLS)]
            for j, lo, hi in pieces:
                if vec:
                    g = grad(stack_ref, r0 + j, 1, hi - lo)
                    sl = (slice(None), slice(lo, hi))
                else:
                    g = grad(stack_ref, r0, view[0], _PACK_COLS)
                    sl = (slice(None), slice(None))
                delta, mn, vn = _adam_math(w_ref[sl], g, m_ref[sl], v_ref[sl])
                g_out[sl] = g
                d_out[sl] = delta
                m_out[sl] = mn
                v_out[sl] = vn
        for n, r0, size, vec, view in items:
            if n in sum_only:
                outs[o][...] = grad(stack_ref, r0, view[0], _PACK_COLS)
                o += 1

    out_shape = []
    for n, _, _, _, view in upd:
        out_shape += [jax.ShapeDtypeStruct(view, F32)] * 4
    out_shape += [jax.ShapeDtypeStruct(view, F32) for n, _, _, _, view in items if n in sum_only]
    vm = pl.BlockSpec(memory_space=pltpu.VMEM)
    res = pl.pallas_call(
        body, name=name, out_shape=tuple(out_shape), in_specs=[vm] * len(operands),
        out_specs=tuple([vm] * len(out_shape)),
        compiler_params=pltpu.CompilerParams(vmem_limit_bytes=VMEM_LIMIT))(*operands)
    updated = {n: tuple(r.reshape(wts[n].shape) for r in res[4 * i:4 * i + 4]) for i, (n, *_) in enumerate(upd)}
    sums = dict(zip([it[0] for it in items if it[0] in sum_only], res[4 * len(upd):]))
    return updated, sums


def _mod_shard(c_all, w_ada, b_ada_cols):
    n = w_ada.shape[1]
    tn = 512

    def body(c_ref, w_ref, b_ref, o_ref):
        cv = c_ref[...]
        ca = (cv * _sig(cv)).astype(BF16)
        o_ref[...] = jnp.dot(ca, w_ref[...].astype(BF16), preferred_element_type=F32) + b_ref[...]

    return pl.pallas_call(
        body, name="mod_shard", out_shape=jax.ShapeDtypeStruct((N_DEV, n), F32), grid=(n // tn,),
        in_specs=[_full((N_DEV, D_MODEL)), pl.BlockSpec((D_MODEL, tn), lambda j: (0, j)),
                  pl.BlockSpec((1, tn), lambda j: (0, j))],
        out_specs=pl.BlockSpec((N_DEV, tn), lambda j: (0, j)),
        compiler_params=_cp(("parallel",)))(c_all, w_ada, b_ada_cols)


def _ada_grad(c_all, dmod_cols, after):
    n = dmod_cols.shape[1]
    tn = 512

    def body(c_ref, d_ref, after_ref, o_ref):
        cv = c_ref[...]
        ca = cv * _sig(cv)
        o_ref[...] = lax.dot_general(ca, d_ref[...], (((0,), (0,)), ((), ())),
                                     preferred_element_type=F32, precision=lax.Precision.HIGHEST)

    return pl.pallas_call(
        body, name="ada_grad", out_shape=jax.ShapeDtypeStruct((D_MODEL, n), F32), grid=(n // tn,),
        in_specs=[_full((N_DEV, D_MODEL)), pl.BlockSpec((N_DEV, tn), lambda j: (0, j)),
                  pl.BlockSpec(memory_space=pl.ANY)],
        out_specs=pl.BlockSpec((D_MODEL, tn), lambda j: (0, j)),
        compiler_params=_cp(("parallel",)))(c_all, dmod_cols, after)


def _ssm_tables(W):
    e_re, e_im, bb_re, bb_im = _ssm_prep(W["ssm_a_re"], W["ssm_a_im"], W["ssm_b_re"], W["ssm_b_im"], W["ssm_log_dt"])
    bb, cm = _block_diag_mats(bb_re, bb_im, W["ssm_c_re"], W["ssm_c_im"])
    bb16, cm16 = bb.astype(BF16), cm.astype(BF16)
    return (bb16, cm16, jnp.swapaxes(bb16, 1, 2), jnp.swapaxes(cm16, 1, 2),
            _scan_tables(e_re, e_im, False), _scan_tables(e_re, e_im, True))


def _device_step(x, mod, W, tables, tgt, getw, put, early):
    sh1, sc1, g1, sh2, sc2, g2 = [mod[:, i * D_MODEL:(i + 1) * D_MODEL] for i in range(6)]
    bb16, cm16, bbt16, cmt16, tab_f, tab_b = tables

    w_in = getw("w_in", [mod, *tables])
    h1, z = _in_proj(x, W["norm1_g"], sc1, sh1, w_in)
    yc, scv = _conv_fwd(z, W["conv_w"], W["conv_b"], W["conv_ln_g"], W["conv_ln_b"])
    xs, ys, yg = _ssm_fwd(z, bb16, cm16, W["ssm_d"], tab_f)
    w_cp, w_glu, w_out = getw("conv_proj", scv), getw("ssm_glu", yg), getw("w_out", yg)
    y_conv, zz, merged, o, x2, h2 = _mix_fwd(scv, yg, z, x, w_cp, w_glu, w_out, g1, W["norm2_g"], sc2, sh2)
    w_fi = getw("w_ffn_in", h2)
    f, act = _ffn_in_act(h2, w_fi)
    w_fo = getw("w_ffn_out", act)
    dx3, do2, loss8, dfg8, dg2_8 = _ffn_out_final(x2, act, w_fo, g2, W["final_g"], tgt)

    sm = {}
    tok = put("w_ffn_out", _matmul(act, do2, "tn", 1408, 1024, 2048, BF16, "mm_g_ffn_out"))
    df = _ffn_bwd(do2, w_fo, f, tok)
    tok = put("w_ffn_in", _matmul(h2, df, "tn", 1024, 1408, 2048, BF16, "mm_g_ffn_in"))
    dx2, do, dsh2, dsc2, dn2, dg1_8 = _normmod_bwd(df, w_fi, x2, dx3, W["norm2_g"], sc2, g1, o, tok, "d_h2_normmod2_bwd")
    tok = put("w_out", _matmul(merged, do, "tn", 1024, 1024, 4096, BF16, "mm_g_w_out"))
    dyconv, dgl, dzz = _mix_bwd(do, w_out, z, zz, y_conv, tok)
    tok = put("ssm_glu", _matmul(yg, dzz, "tn", 512, 1024, 4096, BF16, "mm_g_ssm_glu"))
    tok = put("conv_proj", _matmul(scv, dyconv, "tn", 512, 1024, 4096, BF16, "mm_g_conv_proj", after=tok))
    du, de16, dd8, dc_full, dbb_full = _ssm_bwd(dzz, w_glu, ys, z, xs, cmt16, bbt16, W["ssm_d"], tab_b, tok)
    dyc, dlg8, dlb8, dcb8 = _conv_bwd_ln(dyconv, w_cp, yc, W["conv_ln_g"], W["conv_ln_b"])
    dz_conv, dcw = _conv_bwd(dyc, z, W["conv_w"])

    s8 = lambda a: jnp.sum(a, axis=0, keepdims=True)
    de = de16.reshape(2, 8, NST).sum(1)
    de_re, de_im = de[0].reshape(G, P), de[1].reshape(G, P)
    dc_re = _diag_blocks(dc_full, False)
    dc_im = -_diag_blocks(dc_full, True)
    dbb_re = jnp.swapaxes(_diag_blocks(dbb_full, False), 1, 2)
    dbb_im = jnp.swapaxes(_diag_blocks(dbb_full, True), 1, 2)
    _, vjp = jax.vjp(_ssm_prep, W["ssm_a_re"], W["ssm_a_im"], W["ssm_b_re"], W["ssm_b_im"], W["ssm_log_dt"])
    sm["ssm_a_re"], sm["ssm_a_im"], sm["ssm_b_re"], sm["ssm_b_im"], sm["ssm_log_dt"] = vjp((de_re, de_im, dbb_re, dbb_im))
    sm["ssm_c_re"], sm["ssm_c_im"] = dc_re, dc_im
    sm["ssm_d"] = s8(dd8)
    sm["norm2_g"] = s8(dn2)
    sm["conv_b"], sm["conv_ln_g"], sm["conv_ln_b"] = s8(dcb8), s8(dlg8), s8(dlb8)
    sm["conv_w"] = dcw.reshape(KW, 8, CW).sum(1)
    sm["final_g"] = s8(dfg8)
    tok = early(sm)

    dz = [dz_conv, du, dgl]
    tok = put("w_in", _matmul(h1, dz, "tn", 1024, 512, 4096, BF16, "mm_g_w_in", after=tok))
    dx, _, dsh1, dsc1, dn1, _ = _normmod_bwd(dz, w_in, x, dx2, W["norm1_g"], sc1, g1, o, tok, "d_h1_normmod1_bwd")
    dmod = jnp.concatenate([s8(dsh1), s8(dsc1), s8(dg1_8), s8(dsh2), s8(dsc2), s8(dg2_8)], axis=1)
    return loss8, dx, s8(dn1), dmod


_BIG = ("w_in", "conv_proj", "ssm_glu", "w_out", "w_ffn_in", "w_ffn_out")
_BIG_AXIS = {"w_in": 1, "conv_proj": 1, "ssm_glu": 1, "w_out": 0, "w_ffn_in": 1, "w_ffn_out": 0}
_EARLY = ("conv_w", "conv_b", "conv_ln_g", "conv_ln_b", "ssm_a_re", "ssm_a_im", "ssm_b_re", "ssm_b_im", "ssm_c_re",
          "ssm_c_im", "ssm_d", "ssm_log_dt", "norm2_g", "final_g")
_LATE = ("norm1_g", "b_ada")
_ORDER = ("w_ada", "b_ada", "norm1_g", "w_in", "conv_w", "conv_b", "conv_ln_g", "conv_ln_b", "conv_proj",
          "ssm_a_re", "ssm_a_im", "ssm_b_re", "ssm_b_im", "ssm_c_re", "ssm_c_im", "ssm_d", "ssm_log_dt", "ssm_glu",
          "w_out", "norm2_g", "w_ffn_in", "w_ffn_out", "final_g")
_PACK_COLS = 1024


def _pack_rows(shape):
    return -(-int(np.prod(shape)) // (8 * _PACK_COLS)) * 8


def _pack(arrs):
    parts = []
    for a in arrs:
        flat = a.reshape(-1)
        n = _pack_rows(a.shape)
        parts.append(jnp.pad(flat, (0, n * _PACK_COLS - flat.shape[0])).reshape(n, _PACK_COLS))
    return jnp.concatenate(parts, 0)


def kernel(x, c, w_ada, b_ada, norm1_g, w_in, conv_w, conv_b, conv_ln_g, conv_ln_b, conv_proj, ssm_a_re, ssm_a_im, ssm_b_re, ssm_b_im, ssm_c_re, ssm_c_im, ssm_d, ssm_log_dt, ssm_glu, w_out, norm2_g, w_ffn_in, w_ffn_out, final_g, loss_target, m_w_ada, m_b_ada, m_norm1_g, m_w_in, m_conv_w, m_conv_b, m_conv_ln_g, m_conv_ln_b, m_conv_proj, m_ssm_a_re, m_ssm_a_im, m_ssm_b_re, m_ssm_b_im, m_ssm_c_re, m_ssm_c_im, m_ssm_d, m_ssm_log_dt, m_ssm_glu, m_w_out, m_norm2_g, m_w_ffn_in, m_w_ffn_out, m_final_g, v_w_ada, v_b_ada, v_norm1_g, v_w_in, v_conv_w, v_conv_b, v_conv_ln_g, v_conv_ln_b, v_conv_proj, v_ssm_a_re, v_ssm_a_im, v_ssm_b_re, v_ssm_b_im, v_ssm_c_re, v_ssm_c_im, v_ssm_d, v_ssm_log_dt, v_ssm_glu, v_w_out, v_norm2_g, v_w_ffn_in, v_w_ffn_out, v_final_g):
    given = dict(locals())
    mx, my, mc = _me()
    chip = 2 * mx + my
    dev = 4 * mx + 2 * my + mc
    def canon(a):
        return a.reshape(1, -1) if a.ndim <= 2 else a[0]

    wts = {n: canon(given[n]) for n in _ORDER}
    mom = {n: canon(given["m_" + n]) for n in _ORDER}
    var = {n: canon(given["v_" + n]) for n in _ORDER}

    c_all = _allgather8(jnp.broadcast_to(c, (8, D_MODEL)), "gather_c")[:, 0, :]
    n_ada = wts["w_ada"].shape[1]
    b_cols = lax.dynamic_slice(wts["b_ada"], (0, chip * n_ada), (1, n_ada))
    mod_cols = _mod_shard(c_all, wts["w_ada"], b_cols)
    mods = _allgather8(mod_cols, "gather_mod")
    mod = jnp.concatenate([lax.dynamic_index_in_dim(mods[2 * q], dev, 0, keepdims=True) for q in range(N_CHIP)], axis=1)
    W = {n: wts[n] for n in _ORDER if n not in _BIG}
    conv_w_full = _allgather8(jnp.pad(wts["conv_w"], ((0, 1), (0, 0))), "gather_conv_w", after=[c_all])
    W["conv_w"] = jnp.concatenate([conv_w_full[2 * q, :KW] for q in range(N_CHIP)], axis=1)

    state_in, token = _half_gather_start(wts["w_in"].astype(BF16), mod + W["conv_w"][0:1, 0:1], "gather_start_w_in")
    W["ssm_log_dt"] = wts["ssm_log_dt"] + token[0:1, 0:1]
    W["ssm_c_re"] = wts["ssm_c_re"] + token[0, 0]
    tables = _ssm_tables(W)
    halves = _half_gather_wait(state_in, [mod, *tables], "gather_wait_w_in")
    state_in, token = _half_swap_start(halves, "gather_swap_start_w_in")
    w_in_full = _half_swap_wait(state_in, token, "gather_swap_wait_w_in")
    rest = [n for n in _BIG if n != "w_in"]
    gstate, token = _gather_start([wts[n].astype(BF16) for n in rest], [_BIG_AXIS[n] for n in rest], w_in_full,
                                  "gather_start_rest")
    gstate = dict(zip(rest, gstate))
    mod = mod + token[0:1, 0:1]

    def getw(n, after):
        if n == "w_in":
            return w_in_full
        return _gather_wait(gstate[n], _BIG_AXIS[n], after, "gather_wait_" + n)

    sstate, estate = {}, []

    def put(n, g):
        sstate[n], tok = _scatter_start(g, _BIG_AXIS[n], "scatter_start_" + n)
        return tok

    first5 = [n for n in _BIG if n != "w_in"]

    def early(sm):
        state, tok = _all8_start(_pack([sm[n] for n in _EARLY]), "small_start")
        estate.append(state)
        held = [_scatter_wait(sstate[n], _BIG_AXIS[n], tok, "scatter_wait_" + n) for n in first5]
        state, tok = _swap_start(held, tok, "swap_start")
        estate.append(state)
        return tok

    loss8, dx, dn1, dmod = _device_step(x[0], mod, W, tables, loss_target[0], getw, put, early)

    held5, sib5 = _swap_wait(estate[1], dx, "swap_wait")
    outs = {}
    for i, n in enumerate(first5):
        outs[n] = _adamw(wts[n], mom[n], var[n], [[held5[i]], [sib5[i]]], "adamw_" + n)
    allp = _all8_wait(estate[0], dx, "small_wait")
    upd, sums = _adamw_small(allp, _EARLY, wts, mom, var, ("conv_w",), "adamw_small")
    outs.update(upd)

    late = _allgather8(_pack([dn1, dmod, loss8]), "gather_late", after=[outs[n][1] for n in first5])
    n_late = _pack_rows((D_MODEL,)) + _pack_rows((6 * D_MODEL,))
    loss = jnp.sum(late[:, n_late:, :])
    late = late[:, :n_late, :]
    held_in = _scatter_wait(sstate["w_in"], _BIG_AXIS["w_in"], late, "scatter_wait_w_in")
    state_in, tok = _swap_start([held_in], late, "swap_start_w_in")

    r1 = _pack_rows((D_MODEL,))
    dmod_all = late[:, r1:, :].reshape(N_DEV, -1)[:, :6 * D_MODEL]
    dmod_cols = lax.dynamic_slice(dmod_all, (0, chip * n_ada), (N_DEV, n_ada))
    g_ada = _ada_grad(c_all, dmod_cols, tok)
    outs["w_ada"] = _adamw(wts["w_ada"], mom["w_ada"], var["w_ada"], [[g_ada]], "adamw_w_ada")
    upd, _ = _adamw_small(late, _LATE, wts, mom, var, (), "adamw_late")
    outs.update(upd)
    held_in, sib_in = _swap_wait(state_in, outs["w_ada"][1], "swap_wait_w_in")
    outs["w_in"] = _adamw(wts["w_in"], mom["w_in"], var["w_in"], [held_in, sib_in], "adamw_w_in")
    g_cw_full = sums["conv_w"].reshape(-1)[:KW * CW].reshape(KW, CW)
    g_cw = lax.dynamic_slice(g_cw_full, (0, chip * (CW // N_CHIP)), (KW, CW // N_CHIP))
    pad = lambda a: jnp.pad(a, ((0, 1), (0, 0)))
    r_cw = _adamw(pad(wts["conv_w"]), pad(mom["conv_w"]), pad(var["conv_w"]), [[pad(g_cw)]], "adamw_conv_w")
    outs["conv_w"] = tuple(r[:KW] for r in r_cw)

    def shaped(n, a):
        return a.reshape(given[n].shape)

    result = [loss, dx[None]]
    for q in range(4):
        result += [shaped(n, outs[n][q]) for n in _ORDER]
    return tuple(result)
```

```python
import math

import jax
import jax.numpy as jnp
import numpy as np
from jax import lax
from jax.experimental import pallas as pl
from jax.experimental.pallas import tpu as pltpu

F32 = jnp.float32
BF16 = jnp.bfloat16
EPS = 1e-6
D_MODEL = 1024
CW = 512
KW = 31
HALO = 32
G, P, H = 32, 64, 16
NST = G * P
FH = 2816
N_DEV = 8
N_CHIP = 4
VMEM_LIMIT = 56 * 1024 * 1024
LR, B1, B2, AEPS, WD, STEP = 0.001, 0.9, 0.999, 1e-08, 0.01, 10
MESH = pl.DeviceIdType.MESH


def _cp(sem=None):
    return pltpu.CompilerParams(dimension_semantics=sem, vmem_limit_bytes=VMEM_LIMIT)


def _sig(x):
    return jax.nn.sigmoid(x)


def _full(shape):
    return pl.BlockSpec(shape, lambda *_: (0,) * len(shape))


def _resident(shape):
    return pl.BlockSpec(shape, lambda *_: (0,) * len(shape), pipeline_mode=pl.Buffered(1))


def _colsum8(v):
    t, c = v.shape
    return jnp.sum(v.reshape(t // 8, 8, c), axis=0)


def _matmul(a, b, mode, tm, tn, tk, out_dtype, name, after=None, n_outer=False, m_cols=None):
    m0 = 0
    b_parts = list(b) if isinstance(b, (list, tuple)) else [b]
    if mode == "nn":
        (M, K), N = a.shape, b.shape[1]
    elif mode == "nt":
        (M, K), N = a.shape, b.shape[0]
    else:
        (K, M), N = a.shape, sum(p.shape[1] for p in b_parts)
        if m_cols is not None:
            m0, M = m_cols
    tm, tn, tk = min(tm, M), min(tn, N), min(tk, K)
    assert M % tm == 0 and N % tn == 0 and K % tk == 0 and m0 % tm == 0, (name, M, N, K, tm, tn, tk)
    assert len(b_parts) == 1 or (mode == "tn" and all(p.shape[1] % tn == 0 for p in b_parts)), name
    nk = K // tk
    mb = m0 // tm
    counts = [p.shape[1] // tn for p in b_parts] if mode == "tn" else [N // tn]
    starts = [sum(counts[:p]) for p in range(len(counts))]

    def ij(fn):
        return (lambda j, i, k: fn(i, j, k)) if n_outer else fn

    if mode == "nn":
        a_spec = pl.BlockSpec((tm, tk), ij(lambda i, j, k: (i, k)))
        b_spec = pl.BlockSpec((tk, tn), ij(lambda i, j, k: (k, j)))
        dims = (((1,), (0,)), ((), ()))
    elif mode == "nt":
        a_spec = pl.BlockSpec((tm, tk), ij(lambda i, j, k: (i, k)))
        b_spec = pl.BlockSpec((tn, tk), ij(lambda i, j, k: (j, k)))
        dims = (((1,), (1,)), ((), ()))
    else:
        a_spec = pl.BlockSpec((tk, tm), ij(lambda i, j, k: (k, i + mb)))
        dims = (((0,), (0,)), ((), ()))
    if mode == "tn":
        b_specs = [pl.BlockSpec((tk, tn), ij(lambda i, j, k, s=s, n=n: (k, jnp.clip(j - s, 0, n - 1))))
                   for s, n in zip(starts, counts)]
    else:
        b_specs = [b_spec]
    nb = len(b_parts)

    def body(a_ref, *rest):
        b_refs = rest[:nb]
        o_ref, acc_ref = rest[-2:]
        j = pl.program_id(0 if n_outer else 1)
        k = pl.program_id(2)

        def compute(b_ref):
            part = lax.dot_general(a_ref[...].astype(BF16), b_ref[...].astype(BF16), dims,
                                   preferred_element_type=F32)
            if nk == 1:
                o_ref[...] = part.astype(out_dtype)
            else:
                @pl.when(k == 0)
                def _():
                    acc_ref[...] = part

                @pl.when(k > 0)
                def _():
                    acc_ref[...] += part

                @pl.when(k == nk - 1)
                def _():
                    o_ref[...] = acc_ref[...].astype(out_dtype)

        if nb == 1:
            compute(b_refs[0])
        else:
            for p in range(nb):
                pl.when(jnp.logical_and(j >= starts[p], j < starts[p] + counts[p]))(
                    lambda b_ref=b_refs[p]: compute(b_ref))

    return pl.pallas_call(
        body, name=name,
        out_shape=jax.ShapeDtypeStruct((M, N), out_dtype),
        grid=(N // tn, M // tm, nk) if n_outer else (M // tm, N // tn, nk),
        in_specs=[a_spec] + b_specs + ([] if after is None else [pl.BlockSpec(memory_space=pl.ANY)]),
        out_specs=pl.BlockSpec((tm, tn), ij(lambda i, j, k: (i, j))),
        scratch_shapes=[pltpu.VMEM((tm, tn) if nk > 1 else (8, 128), F32)],
        compiler_params=_cp(("parallel", "parallel", "arbitrary")),
    )(*([a] + b_parts + ([] if after is None else [after])))


def _row_tile(S):
    return min(512, S)


def _in_proj(x, g, sc, sh, w_in):
    S, D = x.shape
    N = w_in.shape[1]
    tm = min(512, S)

    def body(x_ref, g_ref, sc_ref, sh_ref, w_ref, h_ref, z_ref):
        xv = x_ref[...]
        r = lax.rsqrt(jnp.mean(xv * xv, axis=-1, keepdims=True) + EPS)
        h = (xv * r * (g_ref[...] * (1.0 + sc_ref[...])) + sh_ref[...]).astype(BF16)
        h_ref[...] = h
        z_ref[...] = jnp.dot(h, w_ref[...], preferred_element_type=F32).astype(BF16)

    row = pl.BlockSpec((tm, D), lambda i: (i, 0))
    par = _full((1, D))
    return pl.pallas_call(
        body, name="in_proj",
        out_shape=(jax.ShapeDtypeStruct((S, D), BF16), jax.ShapeDtypeStruct((S, N), BF16)), grid=(S // tm,),
        in_specs=[row, par, par, par, _resident((D, N))], out_specs=(row, pl.BlockSpec((tm, N), lambda i: (i, 0))),
        compiler_params=_cp(("parallel",)))(x, g, sc, sh, w_in)


def _fill_shifted(buf_ref, sh_ref):
    n = buf_ref.shape[0] - 8
    for s in range(1, 8):
        sh_ref[s, 0:n, :] = buf_ref[s:s + n, :]


def _window(buf_ref, sh_ref, off, n):
    s = off % 8
    return buf_ref[off:off + n, :] if s == 0 else sh_ref[s, off - s:off - s + n, :]


def _conv_fwd(z, conv_w, conv_b, ln_g, ln_b):
    S = z.shape[0]
    tm = min(128, S)
    sub = 32
    hb = tm // HALO

    def body(a_ref, g_ref, ha_ref, hg_ref, w_ref, b_ref, lg_ref, lb_ref, yc_ref, s_ref, ug_ref, sh_ref):
        i = pl.program_id(0)
        halo = ha_ref[...].astype(F32) * _sig(hg_ref[...].astype(F32))
        ug_ref[0:HALO, :] = jnp.where(i == 0, 0.0, halo)
        ug_ref[HALO:, :] = a_ref[...].astype(F32) * _sig(g_ref[...].astype(F32))
        _fill_shifted(ug_ref, sh_ref)
        for rb in range(tm // sub):
            acc = jnp.zeros((sub, CW), F32) + b_ref[...]
            for k in range(KW):
                off = rb * sub + HALO - (KW - 1) + k
                acc = acc + w_ref[k:k + 1, :] * _window(ug_ref, sh_ref, off, sub)
            yc_ref[rb * sub:(rb + 1) * sub, :] = acc
            mu = jnp.mean(acc, axis=-1, keepdims=True)
            cen = acc - mu
            rstd = lax.rsqrt(jnp.mean(cen * cen, axis=-1, keepdims=True) + EPS)
            ln = cen * rstd * lg_ref[...] + lb_ref[...]
            s_ref[rb * sub:(rb + 1) * sub, :] = (ln * _sig(ln)).astype(BF16)

    prev = lambda i: (jnp.maximum(i * hb - 1, 0), 0)
    return pl.pallas_call(
        body, name="conv_fwd",
        out_shape=(jax.ShapeDtypeStruct((S, CW), F32), jax.ShapeDtypeStruct((S, CW), BF16)),
        grid=(S // tm,),
        in_specs=[pl.BlockSpec((tm, CW), lambda i: (i, 0)), pl.BlockSpec((tm, CW), lambda i: (i, 1)),
                  pl.BlockSpec((HALO, CW), prev), pl.BlockSpec((HALO, CW), lambda i: (jnp.maximum(i * hb - 1, 0), 1)),
                  _full((KW, CW)), _full((1, CW)), _full((1, CW)), _full((1, CW))],
        out_specs=(pl.BlockSpec((tm, CW), lambda i: (i, 0)), pl.BlockSpec((tm, CW), lambda i: (i, 0))),
        scratch_shapes=[pltpu.VMEM((tm + HALO, CW), F32), pltpu.VMEM((8, tm + HALO, CW), F32)],
        compiler_params=_cp(("parallel",)))(z, z, z, z, conv_w, conv_b, ln_g, ln_b)


def _conv_bwd_ln(dyconv, w_cp, yc, ln_g, ln_b):
    S = yc.shape[0]
    tm = _row_tile(S)

    def body(dy_ref, w_ref, yc_ref, lg_ref, lb_ref, dyc_ref, dlg_ref, dlb_ref, dcb_ref):
        i = pl.program_id(0)
        dsc = lax.dot_general(dy_ref[...], w_ref[...], (((1,), (1,)), ((), ())), preferred_element_type=F32)
        yc_v = yc_ref[...]
        mu = jnp.mean(yc_v, axis=-1, keepdims=True)
        cen = yc_v - mu
        rstd = lax.rsqrt(jnp.mean(cen * cen, axis=-1, keepdims=True) + EPS)
        yn = cen * rstd
        ln = yn * lg_ref[...] + lb_ref[...]
        sl = _sig(ln)
        dln = dsc * (sl * (1.0 + ln * (1.0 - sl)))
        dyn = dln * lg_ref[...]
        dyc = rstd * (dyn - jnp.mean(dyn, axis=-1, keepdims=True)
                      - yn * jnp.mean(dyn * yn, axis=-1, keepdims=True))
        dyc_ref[...] = dyc

        @pl.when(i == 0)
        def _():
            dlg_ref[...] = jnp.zeros_like(dlg_ref)
            dlb_ref[...] = jnp.zeros_like(dlb_ref)
            dcb_ref[...] = jnp.zeros_like(dcb_ref)

        dlg_ref[...] += _colsum8(dln * yn)
        dlb_ref[...] += _colsum8(dln)
        dcb_ref[...] += _colsum8(dyc)

    row = pl.BlockSpec((tm, CW), lambda i: (i, 0))
    acc = jax.ShapeDtypeStruct((8, CW), F32)
    return pl.pallas_call(
        body, name="conv_bwd_ln",
        out_shape=(jax.ShapeDtypeStruct((S, CW), F32), acc, acc, acc), grid=(S // tm,),
        in_specs=[pl.BlockSpec((tm, D_MODEL), lambda i: (i, 0)), _full((CW, D_MODEL)), row, _full((1, CW)),
                  _full((1, CW))],
        out_specs=(row, _full((8, CW)), _full((8, CW)), _full((8, CW))),
        compiler_params=_cp(("arbitrary",)))(dyconv, w_cp, yc, ln_g, ln_b)


def _conv_bwd(dyc, z, conv_w):
    S = z.shape[0]
    tm = min(128, S)
    sub = 32
    hb = tm // HALO
    nt = S // tm

    def body(d_ref, dn_ref, a_ref, g_ref, ha_ref, hg_ref, w_ref, dz_ref, dw_ref, ug_ref, dy_ref, ugs_ref, dys_ref):
        i = pl.program_id(0)
        halo = ha_ref[...].astype(F32) * _sig(hg_ref[...].astype(F32))
        ug_ref[0:HALO, :] = jnp.where(i == 0, 0.0, halo)
        a = a_ref[...].astype(F32)
        sg = _sig(g_ref[...].astype(F32))
        ug_ref[HALO:, :] = a * sg
        dy_ref[0:tm, :] = d_ref[...]
        dy_ref[tm:, :] = jnp.where(i == nt - 1, 0.0, dn_ref[...])
        _fill_shifted(ug_ref, ugs_ref)
        _fill_shifted(dy_ref, dys_ref)

        @pl.when(i == 0)
        def _():
            dw_ref[...] = jnp.zeros_like(dw_ref)

        for rb in range(tm // sub):
            r0 = rb * sub
            acc = jnp.zeros((sub, CW), F32)
            dyc_b = dy_ref[r0:r0 + sub, :]
            for k in range(KW):
                up = r0 + (KW - 1) - k
                acc = acc + w_ref[k:k + 1, :] * _window(dy_ref, dys_ref, up, sub)
                off = r0 + HALO - (KW - 1) + k
                dw_ref[k * 8:(k + 1) * 8, :] += _colsum8(dyc_b * _window(ug_ref, ugs_ref, off, sub))
            a_b = a[r0:r0 + sub, :]
            sg_b = sg[r0:r0 + sub, :]
            dz_ref[r0:r0 + sub, 0:CW] = (acc * sg_b).astype(BF16)
            dz_ref[r0:r0 + sub, CW:2 * CW] = (acc * a_b * sg_b * (1.0 - sg_b)).astype(BF16)

    return pl.pallas_call(
        body, name="conv_bwd",
        out_shape=(jax.ShapeDtypeStruct((S, 2 * CW), BF16), jax.ShapeDtypeStruct((KW * 8, CW), F32)),
        grid=(nt,),
        in_specs=[pl.BlockSpec((tm, CW), lambda i: (i, 0)),
                  pl.BlockSpec((HALO, CW), lambda i: (jnp.minimum((i + 1) * hb, nt * hb - 1), 0)),
                  pl.BlockSpec((tm, CW), lambda i: (i, 0)), pl.BlockSpec((tm, CW), lambda i: (i, 1)),
                  pl.BlockSpec((HALO, CW), lambda i: (jnp.maximum(i * hb - 1, 0), 0)),
                  pl.BlockSpec((HALO, CW), lambda i: (jnp.maximum(i * hb - 1, 0), 1)),
                  _full((KW, CW))],
        out_specs=(pl.BlockSpec((tm, 2 * CW), lambda i: (i, 0)), _full((KW * 8, CW))),
        scratch_shapes=[pltpu.VMEM((tm + HALO, CW), F32), pltpu.VMEM((tm + HALO, CW), F32),
                        pltpu.VMEM((8, tm + HALO, CW), F32), pltpu.VMEM((8, tm + HALO, CW), F32)],
        compiler_params=_cp(("arbitrary",)))(dyc, dyc, z, z, z, z, conv_w)


_GELU_C = math.sqrt(2.0 / math.pi)


def _gelu(x):
    return 0.5 * x * (1.0 + jnp.tanh(_GELU_C * (x + 0.044715 * x * x * x)))


def _gelu_grad(x):
    t = jnp.tanh(_GELU_C * (x + 0.044715 * x * x * x))
    return 0.5 * (1.0 + t) + 0.5 * x * (1.0 - t * t) * (_GELU_C * (1.0 + 3 * 0.044715 * x * x))


_NCL = 4
_UC = CW // _NCL
_LW = NST // _NCL
_CS = 2 * _LW


def _ssm_fwd(z, bb, cm, d, tab):
    S = z.shape[0]
    tm = min(512, S)

    def body(u_ref, bb_ref, cm_ref, d_ref, t_ref, x_ref, ys_ref, yg_ref, car_ref):
        i = pl.program_id(0)

        @pl.when(i == 0)
        def _():
            car_ref[...] = jnp.zeros_like(car_ref)

        u16 = u_ref[...]
        u = u16.astype(F32)
        for c in range(_NCL):
            lre = pl.ds(c * _CS, _LW)
            lim = pl.ds(c * _CS + _LW, _LW)
            tl = pl.ds(c * _LW, _LW)
            x_ref[:, c * _CS:(c + 1) * _CS] = jnp.dot(u16[:, c * _UC:(c + 1) * _UC], bb_ref[c],
                                                      preferred_element_type=F32)

            def blk(j, car):
                cr, ci = car
                rows = pl.ds(pl.multiple_of(j * 8, 8), 8)
                r = x_ref[rows, lre]
                im = x_ref[rows, lim]
                for lvl, s in enumerate((1, 2, 4)):
                    mr = t_ref[16 * lvl:16 * lvl + 8, tl]
                    mi = t_ref[16 * lvl + 8:16 * lvl + 16, tl]
                    sr = pltpu.roll(r, s, 0)
                    si = pltpu.roll(im, s, 0)
                    r, im = r + (mr * sr - mi * si), im + (mr * si + mi * sr)
                pr = t_ref[48:56, tl]
                pi_ = t_ref[56:64, tl]
                r, im = r + (pr * cr - pi_ * ci), im + (pr * ci + pi_ * cr)
                x_ref[rows, lre] = r
                x_ref[rows, lim] = im
                return (jnp.broadcast_to(r[7:8, :], (8, _LW)), jnp.broadcast_to(im[7:8, :], (8, _LW)))

            cr, ci = lax.fori_loop(0, tm // 8, blk, (car_ref[:, lre], car_ref[:, lim]))
            car_ref[:, lre] = cr
            car_ref[:, lim] = ci
            cols = slice(c * _UC, (c + 1) * _UC)
            ys = jnp.dot(x_ref[:, c * _CS:(c + 1) * _CS].astype(BF16), cm_ref[c], preferred_element_type=F32)
            ys = ys + d_ref[:, cols] * u[:, cols]
            ys_ref[:, cols] = ys
            yg_ref[:, cols] = _gelu(ys).astype(BF16)

    return pl.pallas_call(
        body, name="ssm_fwd",
        out_shape=(jax.ShapeDtypeStruct((S, 2 * NST), F32), jax.ShapeDtypeStruct((S, CW), F32),
                   jax.ShapeDtypeStruct((S, CW), BF16)),
        grid=(S // tm,),
        in_specs=[pl.BlockSpec((tm, CW), lambda i: (i, 2)), _full((_NCL, _UC, _CS)), _full((_NCL, _CS, _UC)),
                  _full((1, CW)), _full((64, NST))],
        out_specs=(pl.BlockSpec((tm, 2 * NST), lambda i: (i, 0)), pl.BlockSpec((tm, CW), lambda i: (i, 0)),
                   pl.BlockSpec((tm, CW), lambda i: (i, 0))),
        scratch_shapes=[pltpu.VMEM((8, 2 * NST), F32)],
        compiler_params=_cp(("arbitrary",)))(z, bb, cm, d, tab)


def _ssm_bwd(dzz, w_glu, ys, z, xs, cmt, bbt, d, tab, after):
    S = z.shape[0]
    tm = min(512, S)
    nt = S // tm
    tdims = (((0,), (0,)), ((), ()))

    def body(dzz_ref, wglu_ref, ys_ref, u_ref, x_ref, cmt_ref, bbt_ref, d_ref, t_ref, after_ref,
             du_ref, de_ref, dd_ref, dc_hbm, dbb_hbm, car_ref, lam_ref, dc_ref, dbb_ref):
        i = pl.program_id(0)

        @pl.when(i == 0)
        def _():
            car_ref[...] = jnp.zeros_like(car_ref)
            de_ref[...] = jnp.zeros_like(de_ref)
            dd_ref[...] = jnp.zeros_like(dd_ref)
            dc_ref[...] = jnp.zeros_like(dc_ref)
            dbb_ref[...] = jnp.zeros_like(dbb_ref)

        u16 = u_ref[...]
        u = u16.astype(F32)
        dyg = lax.dot_general(dzz_ref[...], wglu_ref[...], (((1,), (1,)), ((), ())), preferred_element_type=F32)
        dys = dyg * _gelu_grad(ys_ref[...])
        dys16 = dys.astype(BF16)
        dd_ref[...] += _colsum8(dys * u)
        row = lax.broadcasted_iota(jnp.int32, (8, _LW), 0)
        for c in range(_NCL):
            lre = pl.ds(c * _CS, _LW)
            lim = pl.ds(c * _CS + _LW, _LW)
            tl = pl.ds(c * _LW, _LW)
            cols = slice(c * _UC, (c + 1) * _UC)
            span = slice(c * _CS, (c + 1) * _CS)
            dc_ref[cols, :] += lax.dot_general(dys16[:, cols], x_ref[:, span].astype(BF16), tdims,
                                               preferred_element_type=F32)
            lam_ref[...] = jnp.dot(dys16[:, cols], cmt_ref[c], preferred_element_type=F32)

            def blk(jj, car):
                cr, ci, ar, ai = car
                j = tm // 8 - 1 - jj
                rows = pl.ds(pl.multiple_of(j * 8, 8), 8)
                r = lam_ref[rows, 0:_LW]
                im = lam_ref[rows, _LW:_CS]
                for lvl, s in enumerate((1, 2, 4)):
                    mr = t_ref[16 * lvl:16 * lvl + 8, tl]
                    mi = t_ref[16 * lvl + 8:16 * lvl + 16, tl]
                    sr = pltpu.roll(r, 8 - s, 0)
                    si = pltpu.roll(im, 8 - s, 0)
                    r, im = r + (mr * sr - mi * si), im + (mr * si + mi * sr)
                pr = t_ref[48:56, tl]
                pi_ = t_ref[56:64, tl]
                r, im = r + (pr * cr - pi_ * ci), im + (pr * ci + pi_ * cr)
                lam_ref[rows, 0:_LW] = r
                lam_ref[rows, _LW:_CS] = im
                nr = jnp.where(row == 7, cr, pltpu.roll(r, 7, 0))
                ni = jnp.where(row == 7, ci, pltpu.roll(im, 7, 0))
                xr = x_ref[rows, lre]
                xi = x_ref[rows, lim]
                ar = ar + (nr * xr + ni * xi)
                ai = ai + (ni * xr - nr * xi)
                return (jnp.broadcast_to(r[0:1, :], (8, _LW)), jnp.broadcast_to(im[0:1, :], (8, _LW)), ar, ai)

            zero = jnp.zeros((8, _LW), F32)
            cr, ci, ar, ai = lax.fori_loop(0, tm // 8, blk, (car_ref[:, lre], car_ref[:, lim], zero, zero))
            car_ref[:, lre] = cr
            car_ref[:, lim] = ci
            de_ref[0:8, tl] += ar
            de_ref[8:16, tl] += ai
            lam16 = lam_ref[...].astype(BF16)
            dbb_ref[cols, :] += lax.dot_general(u16[:, cols], lam16, tdims, preferred_element_type=F32)
            du = jnp.dot(lam16, bbt_ref[c], preferred_element_type=F32) + dys[:, cols] * d_ref[:, cols]
            du_ref[:, cols] = du.astype(BF16)

        @pl.when(i == nt - 1)
        def _():
            pltpu.sync_copy(dc_ref, dc_hbm)
            pltpu.sync_copy(dbb_ref, dbb_hbm)

    rev = lambda i: (nt - 1 - i, 0)
    once = lambda shape: pl.BlockSpec(shape, lambda *_: (0,) * len(shape), pipeline_mode=pl.Buffered(1))
    cross = jax.ShapeDtypeStruct((CW, _CS), F32)
    return pl.pallas_call(
        body, name="ssm_bwd",
        out_shape=(jax.ShapeDtypeStruct((S, CW), BF16), jax.ShapeDtypeStruct((16, NST), F32),
                   jax.ShapeDtypeStruct((8, CW), F32), cross, cross),
        grid=(nt,),
        in_specs=[pl.BlockSpec((tm, 2 * D_MODEL), rev), once((CW, 2 * D_MODEL)), pl.BlockSpec((tm, CW), rev),
                  pl.BlockSpec((tm, CW), lambda i: (nt - 1 - i, 2)), pl.BlockSpec((tm, 2 * NST), rev),
                  once((_NCL, _UC, _CS)), once((_NCL, _CS, _UC)), _full((1, CW)), once((64, NST)),
                  pl.BlockSpec(memory_space=pl.ANY)],
        out_specs=(pl.BlockSpec((tm, CW), rev), _full((16, NST)), _full((8, CW)),
                   pl.BlockSpec(memory_space=pl.ANY), pl.BlockSpec(memory_space=pl.ANY)),
        scratch_shapes=[pltpu.VMEM((8, 2 * NST), F32), pltpu.VMEM((tm, _CS), F32),
                        pltpu.VMEM((CW, _CS), F32), pltpu.VMEM((CW, _CS), F32)],
        compiler_params=_cp(("arbitrary",)))(dzz, w_glu, ys, z, xs, cmt, bbt, d, tab, after)


def _ssm_prep(a_re, a_im, b_re, b_im, log_dt):
    dt = jnp.exp(log_dt.reshape(G))[:, None]
    mag = jnp.exp(dt * a_re)
    e_re, e_im = mag * jnp.cos(dt * a_im), mag * jnp.sin(dt * a_im)
    n_re, n_im = e_re - 1.0, e_im
    den = a_re * a_re + a_im * a_im
    q_re = (n_re * a_re + n_im * a_im) / den
    q_im = (n_im * a_re - n_re * a_im) / den
    bb_re = q_re[..., None] * b_re - q_im[..., None] * b_im
    bb_im = q_re[..., None] * b_im + q_im[..., None] * b_re
    return e_re, e_im, bb_re, bb_im


def _scan_tables(e_re, e_im, reverse):
    er = e_re.reshape(1, NST)
    ei = e_im.reshape(1, NST)
    if reverse:
        ei = -ei
    pows = [(er, ei)]
    for _ in range(7):
        pr, pi_ = pows[-1]
        pows.append((pr * er - pi_ * ei, pr * ei + pi_ * er))
    row = jnp.arange(8)[:, None]
    out = []
    for s in (1, 2, 4):
        pr, pi_ = pows[s - 1]
        keep = (row + s <= 7) if reverse else (row >= s)
        out += [jnp.where(keep, pr, 0.0), jnp.where(keep, pi_, 0.0)]
    allr = jnp.concatenate([p[0] for p in pows], 0)
    alli = jnp.concatenate([p[1] for p in pows], 0)
    if reverse:
        allr, alli = allr[::-1], alli[::-1]
    out += [allr, alli]
    return jnp.concatenate(out, 0).astype(F32)


def _block_diag_mats(bb_re, bb_im, c_re, c_im):
    gc = G // _NCL
    eye = jnp.eye(gc, dtype=F32)
    bre = jnp.einsum("cjph,jk->cjhkp", bb_re.reshape(_NCL, gc, P, H), eye).reshape(_NCL, _UC, _LW)
    bim = jnp.einsum("cjph,jk->cjhkp", bb_im.reshape(_NCL, gc, P, H), eye).reshape(_NCL, _UC, _LW)
    bb = jnp.concatenate([bre, bim], 2)
    cre = jnp.einsum("cjhp,jk->cjpkh", c_re.reshape(_NCL, gc, H, P), eye).reshape(_NCL, _LW, _UC)
    cim = jnp.einsum("cjhp,jk->cjpkh", c_im.reshape(_NCL, gc, H, P), eye).reshape(_NCL, _LW, _UC)
    cm = jnp.concatenate([cre, -cim], 1)
    return bb, cm


def _diag_blocks(cross, imag):
    gc = G // _NCL
    off = _LW if imag else 0
    return jnp.stack([cross[H * g:H * (g + 1), off + P * (g % gc):off + P * (g % gc + 1)] for g in range(G)])


def _mix_fwd(scv, yg, z, x, w_cp, w_glu, w_out, g1, n2g, sc2, sh2):
    S = z.shape[0]
    tm = min(512, S)
    D = D_MODEL

    def body(s_ref, yg_ref, glc0_ref, glc1_ref, gls0_ref, gls1_ref, x_ref, wcp_ref, wglu_ref, wout_ref,
             g1_ref, n2_ref, sc_ref, sh_ref, yc_ref, zz_ref, m_ref, o_ref, x2_ref, h2_ref):
        y_conv = jnp.dot(s_ref[...], wcp_ref[...], preferred_element_type=F32)
        zz = jnp.dot(yg_ref[...], wglu_ref[...], preferred_element_type=F32)
        yc_ref[...] = y_conv.astype(BF16)
        zz_ref[...] = zz.astype(BF16)
        for half, (glc_ref, gls_ref) in enumerate(((glc0_ref, gls0_ref), (glc1_ref, gls1_ref))):
            lo, hi = half * CW, (half + 1) * CW
            y_ssm = zz[:, lo:hi] * _sig(zz[:, D + lo:D + hi])
            m_ref[:, lo:hi] = (_sig(glc_ref[...].astype(F32)) * y_conv[:, lo:hi]
                               + _sig(gls_ref[...].astype(F32)) * y_ssm).astype(BF16)
        o = jnp.dot(m_ref[...], wout_ref[...], preferred_element_type=F32)
        o_ref[...] = o.astype(BF16)
        xv = x_ref[...] + g1_ref[...] * o
        x2_ref[...] = xv
        r = lax.rsqrt(jnp.mean(xv * xv, axis=-1, keepdims=True) + EPS)
        h2_ref[...] = (xv * r * (n2_ref[...] * (1.0 + sc_ref[...])) + sh_ref[...]).astype(BF16)

    zb_ = lambda j: pl.BlockSpec((tm, CW), lambda i: (i, j))
    row = lambda w: pl.BlockSpec((tm, w), lambda i: (i, 0))
    par = _full((1, D))
    bf = lambda w: jax.ShapeDtypeStruct((S, w), BF16)
    return pl.pallas_call(
        body, name="mix_fwd",
        out_shape=(bf(D), bf(2 * D), bf(D), bf(D), jax.ShapeDtypeStruct((S, D), F32), bf(D)),
        grid=(S // tm,),
        in_specs=[row(CW), row(CW), zb_(3), zb_(4), zb_(5), zb_(6), row(D), _resident((CW, D)),
                  _resident((CW, 2 * D)), _resident((D, D)), par, par, par, par],
        out_specs=(row(D), row(2 * D), row(D), row(D), row(D), row(D)),
        compiler_params=_cp(("parallel",)))(scv, yg, z, z, z, z, x, w_cp, w_glu, w_out, g1, n2g, sc2, sh2)


def _mix_bwd(do, w_out, z, zz, y_conv, after):
    S = z.shape[0]
    tm = min(512, S)
    D = D_MODEL

    def body(do_ref, w_ref, glc0_ref, glc1_ref, gls0_ref, gls1_ref, za_ref, zb_ref, yc_ref, after_ref,
             dyc_ref, dgl_ref, dzz_ref):
        dm = lax.dot_general(do_ref[...], w_ref[...], (((1,), (1,)), ((), ())), preferred_element_type=F32)
        for half, (glc_ref, gls_ref) in enumerate(((glc0_ref, gls0_ref), (glc1_ref, gls1_ref))):
            lo, hi = half * CW, (half + 1) * CW
            dm_v = dm[:, lo:hi]
            sgc = _sig(glc_ref[...].astype(F32))
            sgs = _sig(gls_ref[...].astype(F32))
            szb = _sig(zb_ref[:, lo:hi].astype(F32))
            za = za_ref[:, lo:hi].astype(F32)
            dyc_ref[:, lo:hi] = (dm_v * sgc).astype(BF16)
            dgl_ref[:, lo:hi] = (dm_v * yc_ref[:, lo:hi].astype(F32) * sgc * (1.0 - sgc)).astype(BF16)
            dys = dm_v * sgs
            dgl_ref[:, D + lo:D + hi] = (dys * (za * szb) * (1.0 - sgs)).astype(BF16)
            dzz_ref[:, lo:hi] = (dys * szb).astype(BF16)
            dzz_ref[:, D + lo:D + hi] = (dys * za * szb * (1.0 - szb)).astype(BF16)

    zb_ = lambda j: pl.BlockSpec((tm, CW), lambda i: (i, j))
    wide = lambda j: pl.BlockSpec((tm, D), lambda i: (i, j))
    return pl.pallas_call(
        body, name="mix_bwd",
        out_shape=(jax.ShapeDtypeStruct((S, D), BF16), jax.ShapeDtypeStruct((S, 2 * D), BF16),
                   jax.ShapeDtypeStruct((S, 2 * D), BF16)),
        grid=(S // tm,),
        in_specs=[wide(0), _resident((D, D)), zb_(3), zb_(4), zb_(5), zb_(6), wide(0), wide(1), wide(0),
                  pl.BlockSpec(memory_space=pl.ANY)],
        out_specs=(wide(0), pl.BlockSpec((tm, 2 * D), lambda i: (i, 0)), pl.BlockSpec((tm, 2 * D), lambda i: (i, 0))),
        compiler_params=_cp(("parallel",)))(do, w_out, z, z, z, z, zz, zz, y_conv, after)


_FC = 1408


def _ffn_in_act(h2, w_fi):
    S, D = h2.shape
    tm = min(512, S)

    def body(h_ref, w_ref, f_ref, a_ref):
        hv = h_ref[...]
        for c in range(FH // _FC):
            lo, hi = c * _FC, (c + 1) * _FC
            g = jnp.dot(hv, w_ref[:, lo:hi], preferred_element_type=F32)
            u = jnp.dot(hv, w_ref[:, FH + lo:FH + hi], preferred_element_type=F32)
            f_ref[:, lo:hi] = g.astype(BF16)
            f_ref[:, FH + lo:FH + hi] = u.astype(BF16)
            a_ref[:, lo:hi] = (g * _sig(g) * u).astype(BF16)

    return pl.pallas_call(
        body, name="ffn_in_act",
        out_shape=(jax.ShapeDtypeStruct((S, 2 * FH), BF16), jax.ShapeDtypeStruct((S, FH), BF16)),
        grid=(S // tm,),
        in_specs=[pl.BlockSpec((tm, D), lambda i: (i, 0)), _resident((D, 2 * FH))],
        out_specs=(pl.BlockSpec((tm, 2 * FH), lambda i: (i, 0)), pl.BlockSpec((tm, FH), lambda i: (i, 0))),
        compiler_params=_cp(("parallel",)))(h2, w_fi)


def _ffn_bwd(do2, w_fo, f, after):
    S, D = do2.shape
    tm = min(512, S)

    def body(d_ref, w_ref, f_ref, after_ref, df_ref):
        dv = d_ref[...]
        for c in range(FH // _FC):
            lo, hi = c * _FC, (c + 1) * _FC
            dact = lax.dot_general(dv, w_ref[lo:hi, :], (((1,), (1,)), ((), ())), preferred_element_type=F32)
            g = f_ref[:, lo:hi].astype(F32)
            u = f_ref[:, FH + lo:FH + hi].astype(F32)
            sg = _sig(g)
            df_ref[:, lo:hi] = (dact * u * (sg * (1.0 + g * (1.0 - sg)))).astype(BF16)
            df_ref[:, FH + lo:FH + hi] = (dact * g * sg).astype(BF16)

    return pl.pallas_call(
        body, name="ffn_bwd", out_shape=jax.ShapeDtypeStruct((S, 2 * FH), BF16), grid=(S // tm,),
        in_specs=[pl.BlockSpec((tm, D), lambda i: (i, 0)), _resident((FH, D)),
                  pl.BlockSpec((tm, 2 * FH), lambda i: (i, 0)), pl.BlockSpec(memory_space=pl.ANY)],
        out_specs=pl.BlockSpec((tm, 2 * FH), lambda i: (i, 0)),
        compiler_params=_cp(("parallel",)))(do2, w_fo, f, after)


def _ffn_out_final(x2, act, w_fo, g2, fg, tgt):
    S, D = x2.shape
    tm = min(512, S)

    def body(x2_ref, a_ref, w_ref, g2_ref, fg_ref, t_ref, dx3_ref, do2_ref, ls_ref, dfg_ref, dg2_ref):
        i = pl.program_id(0)

        @pl.when(i == 0)
        def _():
            ls_ref[...] = jnp.zeros_like(ls_ref)
            dfg_ref[...] = jnp.zeros_like(dfg_ref)
            dg2_ref[...] = jnp.zeros_like(dg2_ref)

        o2 = jnp.dot(a_ref[...], w_ref[...], preferred_element_type=F32)
        x3 = x2_ref[...] + g2_ref[...] * o2
        r = lax.rsqrt(jnp.mean(x3 * x3, axis=-1, keepdims=True) + EPS)
        xn = x3 * r
        err = xn * fg_ref[...] - t_ref[...]
        dy = err * (1.0 / D)
        dxn = dy * fg_ref[...]
        dx3 = r * (dxn - xn * jnp.mean(dxn * xn, axis=-1, keepdims=True))
        dx3_ref[...] = dx3
        do2_ref[...] = (dx3 * g2_ref[...]).astype(BF16)
        e2 = _colsum8(err * err)
        lanes = e2[:, 0:128]
        for q in range(1, D // 128):
            lanes = lanes + e2[:, q * 128:(q + 1) * 128]
        ls_ref[...] += lanes * (0.5 / D)
        dfg_ref[...] += _colsum8(dy * xn)
        dg2_ref[...] += _colsum8(dx3 * o2)

    row = pl.BlockSpec((tm, D), lambda i: (i, 0))
    par = _full((1, D))
    return pl.pallas_call(
        body, name="final_loss",
        out_shape=(jax.ShapeDtypeStruct((S, D), F32), jax.ShapeDtypeStruct((S, D), BF16),
                   jax.ShapeDtypeStruct((8, 128), F32), jax.ShapeDtypeStruct((8, D), F32),
                   jax.ShapeDtypeStruct((8, D), F32)),
        grid=(S // tm,), in_specs=[row, pl.BlockSpec((tm, FH), lambda i: (i, 0)), _resident((FH, D)), par, par, row],
        out_specs=(row, row, _full((8, 128)), _full((8, D)), _full((8, D))),
        compiler_params=_cp(("arbitrary",)))(x2, act, w_fo, g2, fg, tgt)


def _normmod_bwd(dsrc, w, xin, dres, g, sc, gate, o, after, name):
    S, D = xin.shape
    parts = list(dsrc) if isinstance(dsrc, (list, tuple)) else [dsrc]
    widths = [p.shape[1] for p in parts]
    K = sum(widths)
    tm = min(512, S)
    npart = len(parts)

    def body(*refs):
        ds_refs = refs[:npart]
        w_ref, x_ref, dr_ref, g_ref, sc_ref, gate_ref, o_ref, after_ref = refs[npart:npart + 8]
        dx_ref, do_ref, dsh_ref, dsc_ref, dg_ref, dgate_ref = refs[npart + 8:]
        i = pl.program_id(0)

        @pl.when(i == 0)
        def _():
            dsh_ref[...] = jnp.zeros_like(dsh_ref)
            dsc_ref[...] = jnp.zeros_like(dsc_ref)
            dg_ref[...] = jnp.zeros_like(dg_ref)
            dgate_ref[...] = jnp.zeros_like(dgate_ref)

        gv = g_ref[...]
        scale = 1.0 + sc_ref[...]
        xv = x_ref[...]
        r = lax.rsqrt(jnp.mean(xv * xv, axis=-1, keepdims=True) + EPS)
        xn = xv * r
        dh_v, col = None, 0
        for ds_ref, wd in zip(ds_refs, widths):
            t = lax.dot_general(ds_ref[...], w_ref[:, col:col + wd], (((1,), (1,)), ((), ())),
                                preferred_element_type=F32)
            dh_v = t if dh_v is None else dh_v + t
            col += wd
        dxn = dh_v * (gv * scale)
        dx = dr_ref[...] + r * (dxn - xn * jnp.mean(dxn * xn, axis=-1, keepdims=True))
        dx_ref[...] = dx
        do_ref[...] = (dx * gate_ref[...]).astype(BF16)
        hx = dh_v * xn
        dsh_ref[...] += _colsum8(dh_v)
        dsc_ref[...] += _colsum8(hx) * gv
        dg_ref[...] += _colsum8(hx) * scale
        dgate_ref[...] += _colsum8(dx * o_ref[...])

    row = pl.BlockSpec((tm, D), lambda i: (i, 0))
    par = _full((1, D))
    acc = jax.ShapeDtypeStruct((8, D), F32)
    return pl.pallas_call(
        body, name=name,
        out_shape=(jax.ShapeDtypeStruct((S, D), F32), jax.ShapeDtypeStruct((S, D), BF16), acc, acc, acc, acc),
        grid=(S // tm,),
        in_specs=[pl.BlockSpec((tm, wd), lambda i: (i, 0)) for wd in widths]
        + [_resident((D, K)), row, row, par, par, par, row, pl.BlockSpec(memory_space=pl.ANY)],
        out_specs=(row, row, _full((8, D)), _full((8, D)), _full((8, D)), _full((8, D))),
        compiler_params=_cp(("arbitrary",)))(*parts, w, xin, dres, g, sc, gate, o, after)


def _me():
    return lax.axis_index("x"), lax.axis_index("y"), lax.axis_index("c")


def _allgather8(v, name, after=()):
    R, C = v.shape
    after = list(after)

    def body(v_ref, *rest):
        out_ref, send_sems, recv_sems, local_sem = rest[len(after):]
        x, y, c = _me()
        mine = pltpu.make_async_copy(v_ref, out_ref.at[4 * x + 2 * y + c], local_sem)
        mine.start()
        copies = []
        for k in range(1, N_DEV):
            fx, fy, fc = (k >> 2) & 1, (k >> 1) & 1, k & 1
            peer = (x ^ fx, y ^ fy, c ^ fc)
            copies.append(pltpu.make_async_remote_copy(
                src_ref=v_ref, dst_ref=out_ref.at[4 * x + 2 * y + c],
                send_sem=send_sems.at[k - 1], recv_sem=recv_sems.at[k - 1],
                device_id=peer, device_id_type=MESH))
        for cp in copies:
            cp.start()
        for k in range(1, N_DEV):
            fx, fy, fc = (k >> 2) & 1, (k >> 1) & 1, k & 1
            src_slot = 4 * (x ^ fx) + 2 * (y ^ fy) + (c ^ fc)
            pltpu.make_async_remote_copy(
                src_ref=v_ref, dst_ref=out_ref.at[src_slot],
                send_sem=send_sems.at[k - 1], recv_sem=recv_sems.at[k - 1],
                device_id=(x ^ fx, y ^ fy, c ^ fc), device_id_type=MESH).wait_recv()
        for cp in copies:
            cp.wait_send()
        mine.wait()

    return pl.pallas_call(
        body, name=name, out_shape=jax.ShapeDtypeStruct((N_DEV, R, C), v.dtype),
        in_specs=[pl.BlockSpec(memory_space=pltpu.VMEM)] + [pl.BlockSpec(memory_space=pl.ANY)] * len(after),
        out_specs=pl.BlockSpec(memory_space=pltpu.VMEM),
        scratch_shapes=[pltpu.SemaphoreType.DMA((N_DEV - 1,)), pltpu.SemaphoreType.DMA((N_DEV - 1,)),
                        pltpu.SemaphoreType.DMA],
        compiler_params=pltpu.CompilerParams(vmem_limit_bytes=VMEM_LIMIT))(v, *after)


_HBM = pl.BlockSpec(memory_space=pltpu.HBM)
_SEM = pl.BlockSpec(memory_space=pltpu.SEMAPHORE)
_EFFECT = pltpu.SideEffectType.DATAFLOW_SIDE_EFFECTING
_N_PEER = N_CHIP - 1


def _chip_part(ref, axis, n, chip):
    start = pl.multiple_of(chip * n, 8)
    return ref.at[pl.ds(start, n), :] if axis == 0 else ref.at[:, pl.ds(start, n)]


def _gather_copy(k, src_ref, land_ref, send_sems, recv_sems, axis, arriving):
    x, y, c = _me()
    px, py = x ^ ((k >> 1) & 1), y ^ (k & 1)
    chip = 2 * px + py if arriving else 2 * x + y
    return pltpu.make_async_remote_copy(
        src_ref=src_ref, dst_ref=_chip_part(land_ref, axis, src_ref.shape[axis], chip),
        send_sem=send_sems.at[k - 1], recv_sem=recv_sems.at[k - 1], device_id=(px, py, c), device_id_type=MESH)


def _scatter_copy(k, grad_ref, land_ref, send_sems, recv_sems, axis):
    x, y, c = _me()
    px, py = x ^ ((k >> 1) & 1), y ^ (k & 1)
    return pltpu.make_async_remote_copy(
        src_ref=_chip_part(grad_ref, axis, grad_ref.shape[axis] // N_CHIP, 2 * px + py), dst_ref=land_ref.at[k],
        send_sem=send_sems.at[k - 1], recv_sem=recv_sems.at[k - 1], device_id=(px, py, c), device_id_type=MESH)


def _scatter_own(grad_ref, land_ref, send_sems, axis):
    x, y, _ = _me()
    return pltpu.make_async_copy(_chip_part(grad_ref, axis, grad_ref.shape[axis] // N_CHIP, 2 * x + y),
                                 land_ref.at[0], send_sems.at[_N_PEER])


def _own_copy(src_ref, land_ref, sends, axis):
    x, y, _ = _me()
    return pltpu.make_async_copy(src_ref, _chip_part(land_ref, axis, src_ref.shape[axis], 2 * x + y),
                                 sends.at[_N_PEER])


def _gather_start(shards, axes, after, name):
    nw = len(shards)
    lands = []
    for s, ax in zip(shards, axes):
        shp = list(s.shape)
        shp[ax] *= N_CHIP
        lands.append(lax.empty(tuple(shp), s.dtype))

    def body(*refs):
        srcs, zones = refs[:nw], refs[nw:2 * nw]
        sends, recvs = refs[2 * nw + 1:3 * nw + 1], refs[3 * nw + 1:4 * nw + 1]
        token = refs[-1]
        for w in range(nw):
            for k in range(1, N_CHIP):
                _gather_copy(k, srcs[w], zones[w], sends[w], recvs[w], axes[w], False).start()
        for w in range(nw):
            _own_copy(srcs[w], zones[w], sends[w], axes[w]).start()
        token[...] = jnp.zeros_like(token)

    outs = pl.pallas_call(
        body, name=name,
        out_shape=tuple([pltpu.SemaphoreType.DMA((_N_PEER + 1,))] * nw + [pltpu.SemaphoreType.DMA((_N_PEER,))] * nw
                        + [pltpu.HBM(a.shape, a.dtype) for a in list(shards) + list(lands)]
                        + [jax.ShapeDtypeStruct((8, 128), F32)]),
        in_specs=[_HBM] * (2 * nw) + [pl.BlockSpec(memory_space=pl.ANY)],
        out_specs=tuple([_SEM] * (2 * nw) + [_HBM] * (2 * nw) + [pl.BlockSpec(memory_space=pltpu.VMEM)]),
        input_output_aliases={i: 2 * nw + i for i in range(2 * nw)},
        compiler_params=pltpu.CompilerParams(has_side_effects=_EFFECT),
    )(*([pltpu.with_memory_space_constraint(a, pltpu.HBM) for a in list(shards) + list(lands)] + [after]))
    per_weight = [(outs[w], outs[nw + w], outs[2 * nw + w], outs[3 * nw + w]) for w in range(nw)]
    return per_weight, outs[-1]


def _gather_wait(state, axis, after, name):
    send_sems, recv_sems, shard, land = state

    after = list(after) if isinstance(after, (list, tuple)) else [after]

    def body(src_ref, land_ref, sends, recvs, *rest):
        for k in range(1, N_CHIP):
            _gather_copy(k, src_ref, land_ref, sends, recvs, axis, False).wait_send()
            _gather_copy(k, src_ref, land_ref, sends, recvs, axis, True).wait_recv()
        _own_copy(src_ref, land_ref, sends, axis).wait()

    return pl.pallas_call(
        body, name=name, out_shape=(pltpu.HBM(shard.shape, shard.dtype), pltpu.HBM(land.shape, land.dtype)),
        in_specs=[_HBM, _HBM, _SEM, _SEM] + [pl.BlockSpec(memory_space=pl.ANY)] * len(after), out_specs=(_HBM, _HBM),
        input_output_aliases={0: 0, 1: 1},
        compiler_params=pltpu.CompilerParams(has_side_effects=_EFFECT),
    )(shard, land, send_sems, recv_sems, *after)[1]


def _half_rows(ref, c):
    k2 = ref.shape[0] // 2
    return pl.ds(pl.multiple_of(c * k2, 8), k2)


def _half_copy(k, shard_ref, land_ref, send_sems, recv_sems, arriving):
    x, y, c = _me()
    px, py = x ^ ((k >> 1) & 1), y ^ (k & 1)
    n = shard_ref.shape[1]
    chip = 2 * px + py if arriving else 2 * x + y
    return pltpu.make_async_remote_copy(
        src_ref=shard_ref.at[_half_rows(shard_ref, c), :],
        dst_ref=land_ref.at[_half_rows(land_ref, c), pl.ds(pl.multiple_of(chip * n, 128), n)],
        send_sem=send_sems.at[k - 1], recv_sem=recv_sems.at[k - 1], device_id=(px, py, c), device_id_type=MESH)


def _half_own(shard_ref, land_ref, send_sems):
    x, y, c = _me()
    n = shard_ref.shape[1]
    return pltpu.make_async_copy(
        shard_ref.at[_half_rows(shard_ref, c), :],
        land_ref.at[_half_rows(land_ref, c), pl.ds(pl.multiple_of((2 * x + y) * n, 128), n)], send_sems.at[_N_PEER])


def _half_gather_start(shard, after, name):
    K, n = shard.shape
    land = lax.empty((K, N_CHIP * n), shard.dtype)

    def body(shard_ref, land_ref, after_ref, sends, recvs, shard_thru, land_thru, token):
        for k in range(1, N_CHIP):
            _half_copy(k, shard_ref, land_ref, sends, recvs, False).start()
        _half_own(shard_ref, land_ref, sends).start()
        token[...] = jnp.zeros_like(token)

    outs = pl.pallas_call(
        body, name=name,
        out_shape=(pltpu.SemaphoreType.DMA((_N_PEER + 1,)), pltpu.SemaphoreType.DMA((_N_PEER,)),
                   pltpu.HBM(shard.shape, shard.dtype), pltpu.HBM(land.shape, land.dtype),
                   jax.ShapeDtypeStruct((8, 128), F32)),
        in_specs=[_HBM, _HBM, pl.BlockSpec(memory_space=pl.ANY)],
        out_specs=(_SEM, _SEM, _HBM, _HBM, pl.BlockSpec(memory_space=pltpu.VMEM)),
        input_output_aliases={0: 2, 1: 3},
        compiler_params=pltpu.CompilerParams(has_side_effects=_EFFECT),
    )(pltpu.with_memory_space_constraint(shard, pltpu.HBM), pltpu.with_memory_space_constraint(land, pltpu.HBM), after)
    return outs[:4], outs[4]


def _half_gather_wait(state, after, name):
    send_sems, recv_sems, shard, land = state
    after = list(after)

    def body(shard_ref, land_ref, sends, recvs, *rest):
        for k in range(1, N_CHIP):
            _half_copy(k, shard_ref, land_ref, sends, recvs, False).wait_send()
            _half_copy(k, shard_ref, land_ref, sends, recvs, True).wait_recv()
        _half_own(shard_ref, land_ref, sends).wait()

    return pl.pallas_call(
        body, name=name, out_shape=(pltpu.HBM(shard.shape, shard.dtype), pltpu.HBM(land.shape, land.dtype)),
        in_specs=[_HBM, _HBM, _SEM, _SEM] + [pl.BlockSpec(memory_space=pl.ANY)] * len(after), out_specs=(_HBM, _HBM),
        input_output_aliases={0: 0, 1: 1},
        compiler_params=pltpu.CompilerParams(has_side_effects=_EFFECT),
    )(shard, land, send_sems, recv_sems, *after)[1]


def _half_swap_copy(land_ref, send_sem, recv_sem, arriving):
    x, y, c = _me()
    rows = _half_rows(land_ref, 1 - c if arriving else c)
    return pltpu.make_async_remote_copy(src_ref=land_ref.at[rows, :], dst_ref=land_ref.at[rows, :], send_sem=send_sem,
                                        recv_sem=recv_sem, device_id=(x, y, 1 - c), device_id_type=MESH)


def _half_swap_start(land, name):
    def body(land_ref, send, recv, land_thru, token):
        _half_swap_copy(land_ref, send.at[0], recv.at[0], False).start()
        token[...] = jnp.zeros_like(token)

    sem = pltpu.SemaphoreType.DMA((1,))
    outs = pl.pallas_call(
        body, name=name,
        out_shape=(sem, sem, pltpu.HBM(land.shape, land.dtype), jax.ShapeDtypeStruct((8, 128), F32)),
        in_specs=[_HBM], out_specs=(_SEM, _SEM, _HBM, pl.BlockSpec(memory_space=pltpu.VMEM)),
        input_output_aliases={0: 2},
        compiler_params=pltpu.CompilerParams(has_side_effects=_EFFECT),
    )(pltpu.with_memory_space_constraint(land, pltpu.HBM))
    return outs[:3], outs[3]


def _half_swap_wait(state, after, name):
    send, recv, land = state

    def body(land_ref, send_ref, recv_ref, after_ref, got_ref):
        _half_swap_copy(land_ref, send_ref.at[0], recv_ref.at[0], False).wait_send()
        _half_swap_copy(land_ref, send_ref.at[0], recv_ref.at[0], True).wait_recv()

    return pl.pallas_call(
        body, name=name, out_shape=pltpu.HBM(land.shape, land.dtype),
        in_specs=[_HBM, _SEM, _SEM, pl.BlockSpec(memory_space=pl.ANY)], out_specs=_HBM,
        input_output_aliases={0: 0},
        compiler_params=pltpu.CompilerParams(has_side_effects=_EFFECT),
    )(land, send, recv, after)


def _all8_copy(k, v_ref, land_ref, send_sems, recv_sems, arriving):
    x, y, c = _me()
    px, py, pc = x ^ ((k >> 2) & 1), y ^ ((k >> 1) & 1), c ^ (k & 1)
    slot = 4 * px + 2 * py + pc if arriving else 4 * x + 2 * y + c
    return pltpu.make_async_remote_copy(
        src_ref=v_ref, dst_ref=land_ref.at[slot], send_sem=send_sems.at[k - 1], recv_sem=recv_sems.at[k - 1],
        device_id=(px, py, pc), device_id_type=MESH)


def _all8_own(v_ref, land_ref, send_sems):
    x, y, c = _me()
    return pltpu.make_async_copy(v_ref, land_ref.at[4 * x + 2 * y + c], send_sems.at[N_DEV - 1])


def _all8_start(v, name):
    land = lax.empty((N_DEV,) + v.shape, v.dtype)

    def body(v_ref, land_ref, sends, recvs, v_thru, land_thru, token):
        for k in range(1, N_DEV):
            _all8_copy(k, v_ref, land_ref, sends, recvs, False).start()
        _all8_own(v_ref, land_ref, sends).start()
        token[...] = jnp.zeros_like(token)

    outs = pl.pallas_call(
        body, name=name,
        out_shape=(pltpu.SemaphoreType.DMA((N_DEV,)), pltpu.SemaphoreType.DMA((N_DEV - 1,)),
                   pltpu.HBM(v.shape, v.dtype), pltpu.HBM(land.shape, land.dtype),
                   jax.ShapeDtypeStruct((8, 128), F32)),
        in_specs=[_HBM, _HBM], out_specs=(_SEM, _SEM, _HBM, _HBM, pl.BlockSpec(memory_space=pltpu.VMEM)),
        input_output_aliases={0: 2, 1: 3},
        compiler_params=pltpu.CompilerParams(has_side_effects=_EFFECT),
    )(pltpu.with_memory_space_constraint(v, pltpu.HBM), pltpu.with_memory_space_constraint(land, pltpu.HBM))
    return outs[:4], outs[4]


def _all8_wait(state, after, name):
    send_sems, recv_sems, v, land = state

    def body(v_ref, land_ref, sends, recvs, after_ref, v_dead, got_ref):
        for k in range(1, N_DEV):
            _all8_copy(k, v_ref, land_ref, sends, recvs, False).wait_send()
            _all8_copy(k, v_ref, land_ref, sends, recvs, True).wait_recv()
        _all8_own(v_ref, land_ref, sends).wait()

    return pl.pallas_call(
        body, name=name, out_shape=(pltpu.HBM(v.shape, v.dtype), pltpu.HBM(land.shape, land.dtype)),
        in_specs=[_HBM, _HBM, _SEM, _SEM, pl.BlockSpec(memory_space=pl.ANY)], out_specs=(_HBM, _HBM),
        input_output_aliases={0: 0, 1: 1},
        compiler_params=pltpu.CompilerParams(has_side_effects=_EFFECT),
    )(v, land, send_sems, recv_sems, after)[1]


def _swap_copy(w, src_ref, land_ref, send_sems, recv_sems):
    x, y, c = _me()
    return pltpu.make_async_remote_copy(src_ref=src_ref, dst_ref=land_ref, send_sem=send_sems.at[w],
                                        recv_sem=recv_sems.at[w], device_id=(x, y, 1 - c), device_id_type=MESH)


def _swap_start(arrs, after, name):
    nw = len(arrs)
    lands = [lax.empty(a.shape, a.dtype) for a in arrs]

    def body(*refs):
        srcs, zones = refs[:nw], refs[nw:2 * nw]
        sends, recvs = refs[2 * nw + 1], refs[2 * nw + 2]
        for w in range(nw):
            _swap_copy(w, srcs[w], zones[w], sends, recvs).start()
        refs[-1][...] = jnp.zeros_like(refs[-1])

    sem = pltpu.SemaphoreType.DMA((nw,))
    outs = pl.pallas_call(
        body, name=name,
        out_shape=tuple([sem, sem] + [pltpu.HBM(a.shape, a.dtype) for a in list(arrs) + lands]
                        + [jax.ShapeDtypeStruct((8, 128), F32)]),
        in_specs=[_HBM] * (2 * nw) + [pl.BlockSpec(memory_space=pl.ANY)],
        out_specs=tuple([_SEM, _SEM] + [_HBM] * (2 * nw) + [pl.BlockSpec(memory_space=pltpu.VMEM)]),
        input_output_aliases={i: 2 + i for i in range(2 * nw)},
        compiler_params=pltpu.CompilerParams(has_side_effects=_EFFECT),
    )(*([pltpu.with_memory_space_constraint(a, pltpu.HBM) for a in list(arrs) + lands] + [after]))
    return (outs[0], outs[1], outs[2:2 + nw], outs[2 + nw:2 + 2 * nw]), outs[-1]


def _swap_wait(state, after, name):
    send_sems, recv_sems, arrs, lands = state
    nw = len(arrs)

    def body(*refs):
        srcs, zones = refs[:nw], refs[nw:2 * nw]
        sends, recvs = refs[2 * nw], refs[2 * nw + 1]
        for w in range(nw):
            cp = _swap_copy(w, srcs[w], zones[w], sends, recvs)
            cp.wait_send()
            cp.wait_recv()

    outs = pl.pallas_call(
        body, name=name, out_shape=tuple(pltpu.HBM(a.shape, a.dtype) for a in list(arrs) + list(lands)),
        in_specs=[_HBM] * (2 * nw) + [_SEM, _SEM, pl.BlockSpec(memory_space=pl.ANY)],
        out_specs=tuple([_HBM] * (2 * nw)),
        input_output_aliases={i: i for i in range(2 * nw)},
        compiler_params=pltpu.CompilerParams(has_side_effects=_EFFECT),
    )(*arrs, *lands, send_sems, recv_sems, after)
    return list(outs[:nw]), list(outs[nw:])


def _scatter_start(grad, axis, name):
    shp = list(grad.shape)
    shp[axis] //= N_CHIP
    land = lax.empty((N_CHIP,) + tuple(shp), grad.dtype)

    def body(grad_ref, land_ref, sends, recvs, grad_thru, land_thru, token):
        for k in range(1, N_CHIP):
            _scatter_copy(k, grad_ref, land_ref, sends, recvs, axis).start()
        _scatter_own(grad_ref, land_ref, sends, axis).start()
        token[...] = jnp.zeros_like(token)

    outs = pl.pallas_call(
        body, name=name,
        out_shape=(pltpu.SemaphoreType.DMA((_N_PEER + 1,)), pltpu.SemaphoreType.DMA((_N_PEER,)),
                   pltpu.HBM(grad.shape, grad.dtype), pltpu.HBM(land.shape, land.dtype),
                   jax.ShapeDtypeStruct((8, 128), F32)),
        in_specs=[_HBM, _HBM], out_specs=(_SEM, _SEM, _HBM, _HBM, pl.BlockSpec(memory_space=pltpu.VMEM)),
        input_output_aliases={0: 2, 1: 3},
        compiler_params=pltpu.CompilerParams(has_side_effects=_EFFECT),
    )(pltpu.with_memory_space_constraint(grad, pltpu.HBM), pltpu.with_memory_space_constraint(land, pltpu.HBM))
    return outs[:4], outs[4]


def _scatter_wait(state, axis, after, name):
    send_sems, recv_sems, grad, land = state

    def body(grad_ref, land_ref, sends, recvs, after_ref, grad_dead, got_ref):
        for k in range(1, N_CHIP):
            cp = _scatter_copy(k, grad_ref, land_ref, sends, recvs, axis)
            cp.wait_send()
            cp.wait_recv()
        _scatter_own(grad_ref, land_ref, sends, axis).wait()

    return pl.pallas_call(
        body, name=name, out_shape=(pltpu.HBM(grad.shape, grad.dtype), pltpu.HBM(land.shape, land.dtype)),
        in_specs=[_HBM, _HBM, _SEM, _SEM, pl.BlockSpec(memory_space=pl.ANY)], out_specs=(_HBM, _HBM),
        input_output_aliases={0: 0, 1: 1},
        compiler_params=pltpu.CompilerParams(has_side_effects=_EFFECT),
    )(grad, land, send_sems, recv_sems, after)[1]


_C1 = 1.0 - B1 ** STEP
_C2 = 1.0 - B2 ** STEP


def _adam_math(w, g, m, v):
    m = B1 * m + (1.0 - B1) * g
    v = B2 * v + (1.0 - B2) * (g * g)
    delta = -LR * ((m / _C1) / (jnp.sqrt(v / _C2) + AEPS) + WD * w)
    return delta, m, v


def _adamw(w, m, v, groups, name):
    R, C = w.shape
    tr = R if R <= 256 else (128 if R % 128 == 0 else 176)
    assert R % tr == 0, (name, R)
    gparts = [p for grp in groups for p in grp]
    sizes = [len(grp) for grp in groups]
    ng = len(gparts)

    def body(*refs):
        w_ref, m_ref, v_ref = refs[:3]
        g_refs = list(refs[3:3 + ng])
        g_out, d_out, m_out, v_out = refs[3 + ng:]
        g = None
        for size in sizes:
            s = None
            for r in [g_refs.pop(0) for _ in range(size)]:
                terms = [r[q] for q in range(r.shape[0])] if len(r.shape) == 3 else [r[...]]
                for t in terms:
                    s = t.astype(F32) if s is None else s + t.astype(F32)
            g = s if g is None else g + s
        delta, mn, vn = _adam_math(w_ref[...], g, m_ref[...], v_ref[...])
        g_out[...] = g
        d_out[...] = delta
        m_out[...] = mn
        v_out[...] = vn

    blk = pl.BlockSpec((tr, C), lambda i: (i, 0))
    g_specs = [blk if p.ndim == 2 else pl.BlockSpec((p.shape[0], tr, C), lambda i: (0, i, 0)) for p in gparts]
    sds = jax.ShapeDtypeStruct((R, C), F32)
    return pl.pallas_call(
        body, name=name, out_shape=(sds, sds, sds, sds), grid=(R // tr,),
        in_specs=[blk, blk, blk] + g_specs, out_specs=(blk, blk, blk, blk),
        compiler_params=_cp(("parallel",)))(w, m, v, *gparts)


def _adamw_small(stack, names, wts, mom, var, sum_only, name):
    items, row = [], 0
    for n in names:
        shape = (KW, CW) if n == "conv_w" else wts[n].shape
        size = int(np.prod(shape))
        vec = len(shape) == 2 and shape[0] == 1 and n not in sum_only
        view = shape if vec else (-(-size // _PACK_COLS), _PACK_COLS)
        items.append((n, row, size, vec, view))
        row += _pack_rows(shape)
    upd = [it for it in items if it[0] not in sum_only]
    operands = [stack]
    for n, _, _, _, view in upd:
        operands += [d[n].reshape(view) for d in (wts, mom, var)]

    def grad(stack_ref, r0, nrows, ncols):
        g = stack_ref[0, r0:r0 + nrows, 0:ncols]
        for q in range(1, N_DEV):
            g = g + stack_ref[q, r0:r0 + nrows, 0:ncols]
        return g

    def body(*refs):
        stack_ref, ins, outs = refs[0], refs[1:1 + 3 * len(upd)], refs[1 + 3 * len(upd):]
        o = 0
        for idx, (n, r0, size, vec, view) in enumerate(upd):
            w_ref, m_ref, v_ref = ins[3 * idx:3 * idx + 3]
            g_out, d_out, m_out, v_out = outs[o:o + 4]
            o += 4
            if vec:
                pieces = [(j, j * _PACK_COLS, min((j + 1) * _PACK_COLS, size)) for j in range(-(-size // _PACK_COLS))]
            else:
                pieces = [(None, 0, _PACK_COLS)]
            for j, lo, hi in pieces:
                if vec:
                    g = grad(stack_ref, r0 + j, 1, hi - lo)
                    sl = (slice(None), slice(lo, hi))
                else:
                    g = grad(stack_ref, r0, view[0], _PACK_COLS)
                    sl = (slice(None), slice(None))
                delta, mn, vn = _adam_math(w_ref[sl], g, m_ref[sl], v_ref[sl])
                g_out[sl] = g
                d_out[sl] = delta
                m_out[sl] = mn
                v_out[sl] = vn
        for n, r0, size, vec, view in items:
            if n in sum_only:
                outs[o][...] = grad(stack_ref, r0, view[0], _PACK_COLS)
                o += 1

    out_shape = []
    for n, _, _, _, view in upd:
        out_shape += [jax.ShapeDtypeStruct(view, F32)] * 4
    out_shape += [jax.ShapeDtypeStruct(view, F32) for n, _, _, _, view in items if n in sum_only]
    vm = pl.BlockSpec(memory_space=pltpu.VMEM)
    res = pl.pallas_call(
        body, name=name, out_shape=tuple(out_shape), in_specs=[vm] * len(operands),
        out_specs=tuple([vm] * len(out_shape)),
        compiler_params=pltpu.CompilerParams(vmem_limit_bytes=VMEM_LIMIT))(*operands)
    updated = {n: tuple(r.reshape(wts[n].shape) for r in res[4 * i:4 * i + 4]) for i, (n, *_) in enumerate(upd)}
    sums = dict(zip([it[0] for it in items if it[0] in sum_only], res[4 * len(upd):]))
    return updated, sums


def _mod_shard(c_all, w_ada, b_ada_cols):
    n = w_ada.shape[1]
    tn = 512

    def body(c_ref, w_ref, b_ref, o_ref):
        cv = c_ref[...]
        ca = (cv * _sig(cv)).astype(BF16)
        o_ref[...] = jnp.dot(ca, w_ref[...].astype(BF16), preferred_element_type=F32) + b_ref[...]

    return pl.pallas_call(
        body, name="mod_shard", out_shape=jax.ShapeDtypeStruct((N_DEV, n), F32), grid=(n // tn,),
        in_specs=[_full((N_DEV, D_MODEL)), pl.BlockSpec((D_MODEL, tn), lambda j: (0, j)),
                  pl.BlockSpec((1, tn), lambda j: (0, j))],
        out_specs=pl.BlockSpec((N_DEV, tn), lambda j: (0, j)),
        compiler_params=_cp(("parallel",)))(c_all, w_ada, b_ada_cols)


def _ada_grad(c_all, dmod_cols, after):
    n = dmod_cols.shape[1]
    tn = 512

    def body(c_ref, d_ref, after_ref, o_ref):
        cv = c_ref[...]
        ca = cv * _sig(cv)
        o_ref[...] = lax.dot_general(ca, d_ref[...], (((0,), (0,)), ((), ())),
                                     preferred_element_type=F32, precision=lax.Precision.HIGHEST)

    return pl.pallas_call(
        body, name="ada_grad", out_shape=jax.ShapeDtypeStruct((D_MODEL, n), F32), grid=(n // tn,),
        in_specs=[_full((N_DEV, D_MODEL)), pl.BlockSpec((N_DEV, tn), lambda j: (0, j)),
                  pl.BlockSpec(memory_space=pl.ANY)],
        out_specs=pl.BlockSpec((D_MODEL, tn), lambda j: (0, j)),
        compiler_params=_cp(("parallel",)))(c_all, dmod_cols, after)


def _ssm_tables(W):
    e_re, e_im, bb_re, bb_im = _ssm_prep(W["ssm_a_re"], W["ssm_a_im"], W["ssm_b_re"], W["ssm_b_im"], W["ssm_log_dt"])
    bb, cm = _block_diag_mats(bb_re, bb_im, W["ssm_c_re"], W["ssm_c_im"])
    bb16, cm16 = bb.astype(BF16), cm.astype(BF16)
    return (bb16, cm16, jnp.swapaxes(bb16, 1, 2), jnp.swapaxes(cm16, 1, 2),
            _scan_tables(e_re, e_im, False), _scan_tables(e_re, e_im, True))


def _device_step(x, mod, W, tables, tgt, getw, put, early):
    sh1, sc1, g1, sh2, sc2, g2 = [mod[:, i * D_MODEL:(i + 1) * D_MODEL] for i in range(6)]
    bb16, cm16, bbt16, cmt16, tab_f, tab_b = tables

    w_in = getw("w_in", [mod, *tables])
    h1, z = _in_proj(x, W["norm1_g"], sc1, sh1, w_in)
    yc, scv = _conv_fwd(z, W["conv_w"], W["conv_b"], W["conv_ln_g"], W["conv_ln_b"])
    xs, ys, yg = _ssm_fwd(z, bb16, cm16, W["ssm_d"], tab_f)
    w_cp, w_glu, w_out = getw("conv_proj", scv), getw("ssm_glu", yg), getw("w_out", yg)
    y_conv, zz, merged, o, x2, h2 = _mix_fwd(scv, yg, z, x, w_cp, w_glu, w_out, g1, W["norm2_g"], sc2, sh2)
    w_fi = getw("w_ffn_in", h2)
    f, act = _ffn_in_act(h2, w_fi)
    w_fo = getw("w_ffn_out", act)
    dx3, do2, loss8, dfg8, dg2_8 = _ffn_out_final(x2, act, w_fo, g2, W["final_g"], tgt)

    sm = {}
    tok = put("w_ffn_out", _matmul(act, do2, "tn", 1408, 1024, 2048, BF16, "mm_g_ffn_out"))
    df = _ffn_bwd(do2, w_fo, f, tok)
    tok = put("w_ffn_in", _matmul(h2, df, "tn", 1024, 1408, 2048, BF16, "mm_g_ffn_in"))
    dx2, do, dsh2, dsc2, dn2, dg1_8 = _normmod_bwd(df, w_fi, x2, dx3, W["norm2_g"], sc2, g1, o, tok, "d_h2_normmod2_bwd")
    tok = put("w_out", _matmul(merged, do, "tn", 1024, 1024, 4096, BF16, "mm_g_w_out"))
    dyconv, dgl, dzz = _mix_bwd(do, w_out, z, zz, y_conv, tok)
    tok = put("ssm_glu", _matmul(yg, dzz, "tn", 512, 1024, 4096, BF16, "mm_g_ssm_glu"))
    tok = put("conv_proj", _matmul(scv, dyconv, "tn", 512, 1024, 4096, BF16, "mm_g_conv_proj", after=tok))
    du, de16, dd8, dc_full, dbb_full = _ssm_bwd(dzz, w_glu, ys, z, xs, cmt16, bbt16, W["ssm_d"], tab_b, tok)
    dyc, dlg8, dlb8, dcb8 = _conv_bwd_ln(dyconv, w_cp, yc, W["conv_ln_g"], W["conv_ln_b"])
    dz_conv, dcw = _conv_bwd(dyc, z, W["conv_w"])

    s8 = lambda a: jnp.sum(a, axis=0, keepdims=True)
    de = de16.reshape(2, 8, NST).sum(1)
    de_re, de_im = de[0].reshape(G, P), de[1].reshape(G, P)
    dc_re = _diag_blocks(dc_full, False)
    dc_im = -_diag_blocks(dc_full, True)
    dbb_re = jnp.swapaxes(_diag_blocks(dbb_full, False), 1, 2)
    dbb_im = jnp.swapaxes(_diag_blocks(dbb_full, True), 1, 2)
    _, vjp = jax.vjp(_ssm_prep, W["ssm_a_re"], W["ssm_a_im"], W["ssm_b_re"], W["ssm_b_im"], W["ssm_log_dt"])
    sm["ssm_a_re"], sm["ssm_a_im"], sm["ssm_b_re"], sm["ssm_b_im"], sm["ssm_log_dt"] = vjp((de_re, de_im, dbb_re, dbb_im))
    sm["ssm_c_re"], sm["ssm_c_im"] = dc_re, dc_im
    sm["ssm_d"] = s8(dd8)
    sm["norm2_g"] = s8(dn2)
    sm["conv_b"], sm["conv_ln_g"], sm["conv_ln_b"] = s8(dcb8), s8(dlg8), s8(dlb8)
    sm["conv_w"] = dcw.reshape(KW, 8, CW).sum(1)
    sm["final_g"] = s8(dfg8)
    tok = early(sm)

    dz = [dz_conv, du, dgl]
    tok = put("w_in", _matmul(h1, dz, "tn", 1024, 512, 4096, BF16, "mm_g_w_in", after=tok))
    dx, _, dsh1, dsc1, dn1, _ = _normmod_bwd(dz, w_in, x, dx2, W["norm1_g"], sc1, g1, o, tok, "d_h1_normmod1_bwd")
    dmod = jnp.concatenate([s8(dsh1), s8(dsc1), s8(dg1_8), s8(dsh2), s8(dsc2), s8(dg2_8)], axis=1)
    return loss8, dx, s8(dn1), dmod


_BIG = ("w_in", "conv_proj", "ssm_glu", "w_out", "w_ffn_in", "w_ffn_out")
_BIG_AXIS = {"w_in": 1, "conv_proj": 1, "ssm_glu": 1, "w_out": 0, "w_ffn_in": 1, "w_ffn_out": 0}
_EARLY = ("conv_w", "conv_b", "conv_ln_g", "conv_ln_b", "ssm_a_re", "ssm_a_im", "ssm_b_re", "ssm_b_im", "ssm_c_re",
          "ssm_c_im", "ssm_d", "ssm_log_dt", "norm2_g", "final_g")
_LATE = ("norm1_g", "b_ada")
_ORDER = ("w_ada", "b_ada", "norm1_g", "w_in", "conv_w", "conv_b", "conv_ln_g", "conv_ln_b", "conv_proj",
          "ssm_a_re", "ssm_a_im", "ssm_b_re", "ssm_b_im", "ssm_c_re", "ssm_c_im", "ssm_d", "ssm_log_dt", "ssm_glu",
          "w_out", "norm2_g", "w_ffn_in", "w_ffn_out", "final_g")
_PACK_COLS = 1024


def _pack_rows(shape):
    return -(-int(np.prod(shape)) // (8 * _PACK_COLS)) * 8


def _pack(arrs):
    parts = []
    for a in arrs:
        flat = a.reshape(-1)
        n = _pack_rows(a.shape)
        parts.append(jnp.pad(flat, (0, n * _PACK_COLS - flat.shape[0])).reshape(n, _PACK_COLS))
    return jnp.concatenate(parts, 0)


def kernel(x, c, w_ada, b_ada, norm1_g, w_in, conv_w, conv_b, conv_ln_g, conv_ln_b, conv_proj, ssm_a_re, ssm_a_im, ssm_b_re, ssm_b_im, ssm_c_re, ssm_c_im, ssm_d, ssm_log_dt, ssm_glu, w_out, norm2_g, w_ffn_in, w_ffn_out, final_g, loss_target, m_w_ada, m_b_ada, m_norm1_g, m_w_in, m_conv_w, m_conv_b, m_conv_ln_g, m_conv_ln_b, m_conv_proj, m_ssm_a_re, m_ssm_a_im, m_ssm_b_re, m_ssm_b_im, m_ssm_c_re, m_ssm_c_im, m_ssm_d, m_ssm_log_dt, m_ssm_glu, m_w_out, m_norm2_g, m_w_ffn_in, m_w_ffn_out, m_final_g, v_w_ada, v_b_ada, v_norm1_g, v_w_in, v_conv_w, v_conv_b, v_conv_ln_g, v_conv_ln_b, v_conv_proj, v_ssm_a_re, v_ssm_a_im, v_ssm_b_re, v_ssm_b_im, v_ssm_c_re, v_ssm_c_im, v_ssm_d, v_ssm_log_dt, v_ssm_glu, v_w_out, v_norm2_g, v_w_ffn_in, v_w_ffn_out, v_final_g):
    given = dict(locals())
    mx, my, mc = _me()
    chip = 2 * mx + my
    dev = 4 * mx + 2 * my + mc
    def canon(a):
        return a.reshape(1, -1) if a.ndim <= 2 else a[0]

    wts = {n: canon(given[n]) for n in _ORDER}
    mom = {n: canon(given["m_" + n]) for n in _ORDER}
    var = {n: canon(given["v_" + n]) for n in _ORDER}

    W = {n: wts[n] for n in _ORDER if n not in _BIG}
    rest = [n for n in _BIG if n != "w_in"]
    rest_shards = [wts[n].astype(BF16) for n in rest]
    state_in, token = _half_gather_start(wts["w_in"].astype(BF16), c, "gather_start_w_in")
    W["ssm_log_dt"] = wts["ssm_log_dt"] + token[0:1, 0:1]
    W["ssm_c_re"] = wts["ssm_c_re"] + token[0, 0]
    tables = _ssm_tables(W)

    c_all = _allgather8(jnp.broadcast_to(c, (8, D_MODEL)), "gather_c", after=[*tables, *rest_shards])[:, 0, :]
    n_ada = wts["w_ada"].shape[1]
    b_cols = lax.dynamic_slice(wts["b_ada"], (0, chip * n_ada), (1, n_ada))
    mod_cols = _mod_shard(c_all, wts["w_ada"], b_cols)
    mods = _allgather8(mod_cols, "gather_mod")
    mod = jnp.concatenate([lax.dynamic_index_in_dim(mods[2 * q], dev, 0, keepdims=True) for q in range(N_CHIP)], axis=1)
    conv_w_full = _allgather8(jnp.pad(wts["conv_w"], ((0, 1), (0, 0))), "gather_conv_w", after=[c_all])
    W["conv_w"] = jnp.concatenate([conv_w_full[2 * q, :KW] for q in range(N_CHIP)], axis=1)

    halves = _half_gather_wait(state_in, [mod, W["conv_w"]], "gather_wait_w_in")
    state_in, token = _half_swap_start(halves, "gather_swap_start_w_in")
    w_in_full = _half_swap_wait(state_in, token, "gather_swap_wait_w_in")
    gstate, token = _gather_start(rest_shards, [_BIG_AXIS[n] for n in rest], w_in_full, "gather_start_rest")
    gstate = dict(zip(rest, gstate))
    mod = mod + token[0:1, 0:1]

    def getw(n, after):
        if n == "w_in":
            return w_in_full
        return _gather_wait(gstate[n], _BIG_AXIS[n], after, "gather_wait_" + n)

    sstate, estate = {}, []

    def put(n, g):
        sstate[n], tok = _scatter_start(g, _BIG_AXIS[n], "scatter_start_" + n)
        return tok

    first5 = [n for n in _BIG if n != "w_in"]

    def early(sm):
        state, tok = _all8_start(_pack([sm[n] for n in _EARLY]), "small_start")
        estate.append(state)
        held = [_scatter_wait(sstate[n], _BIG_AXIS[n], tok, "scatter_wait_" + n) for n in first5]
        state, tok = _swap_start(held, tok, "swap_start")
        estate.append(state)
        return tok

    loss8, dx, dn1, dmod = _device_step(x[0], mod, W, tables, loss_target[0], getw, put, early)

    held5, sib5 = _swap_wait(estate[1], dx, "swap_wait")
    outs = {}
    for i, n in enumerate(first5):
        outs[n] = _adamw(wts[n], mom[n], var[n], [[held5[i]], [sib5[i]]], "adamw_" + n)
    allp = _all8_wait(estate[0], dx, "small_wait")
    upd, sums = _adamw_small(allp, _EARLY, wts, mom, var, ("conv_w",), "adamw_small")
    outs.update(upd)

    late = _allgather8(_pack([dn1, dmod, loss8]), "gather_late", after=[outs[n][1] for n in first5])
    n_late = _pack_rows((D_MODEL,)) + _pack_rows((6 * D_MODEL,))
    loss = jnp.sum(late[:, n_late:, :])
    late = late[:, :n_late, :]
    held_in = _scatter_wait(sstate["w_in"], _BIG_AXIS["w_in"], late, "scatter_wait_w_in")
    state_in, tok = _swap_start([held_in], late, "swap_start_w_in")

    r1 = _pack_rows((D_MODEL,))
    dmod_all = late[:, r1:, :].reshape(N_DEV, -1)[:, :6 * D_MODEL]
    dmod_cols = lax.dynamic_slice(dmod_all, (0, chip * n_ada), (N_DEV, n_ada))
    g_ada = _ada_grad(c_all, dmod_cols, tok)
    outs["w_ada"] = _adamw(wts["w_ada"], mom["w_ada"], var["w_ada"], [[g_ada]], "adamw_w_ada")
    upd, _ = _adamw_small(late, _LATE, wts, mom, var, (), "adamw_late")
    outs.update(upd)
    held_in, sib_in = _swap_wait(state_in, outs["w_ada"][1], "swap_wait_w_in")
    outs["w_in"] = _adamw(wts["w_in"], mom["w_in"], var["w_in"], [held_in, sib_in], "adamw_w_in")
    g_cw_full = sums["conv_w"].reshape(-1)[:KW * CW].reshape(KW, CW)
    g_cw = lax.dynamic_slice(g_cw_full, (0, chip * (CW // N_CHIP)), (KW, CW // N_CHIP))
    pad = lambda a: jnp.pad(a, ((0, 1), (0, 0)))
    r_cw = _adamw(pad(wts["conv_w"]), pad(mom["conv_w"]), pad(var["conv_w"]), [[pad(g_cw)]], "adamw_conv_w")
    outs["conv_w"] = tuple(r[:KW] for r in r_cw)

    def shaped(n, a):
        return a.reshape(given[n].shape)

    result = [loss, dx[None]]
    for q in range(4):
        result += [shaped(n, outs[n][q]) for n in _ORDER]
    return tuple(result)
```

```python
import math

import jax
import jax.numpy as jnp
import numpy as np
from jax import lax
from jax.experimental import pallas as pl
from jax.experimental.pallas import tpu as pltpu

F32 = jnp.float32
BF16 = jnp.bfloat16
EPS = 1e-6
D_MODEL = 1024
CW = 512
KW = 31
HALO = 32
G, P, H = 32, 64, 16
NST = G * P
FH = 2816
N_DEV = 8
N_CHIP = 4
VMEM_LIMIT = 56 * 1024 * 1024
LR, B1, B2, AEPS, WD, STEP = 0.001, 0.9, 0.999, 1e-08, 0.01, 10
MESH = pl.DeviceIdType.MESH


def _cp(sem=None):
    return pltpu.CompilerParams(dimension_semantics=sem, vmem_limit_bytes=VMEM_LIMIT)


def _sig(x):
    return jax.nn.sigmoid(x)


def _full(shape):
    return pl.BlockSpec(shape, lambda *_: (0,) * len(shape))


def _resident(shape):
    return pl.BlockSpec(shape, lambda *_: (0,) * len(shape), pipeline_mode=pl.Buffered(1))


def _colsum8(v):
    t, c = v.shape
    return jnp.sum(v.reshape(t // 8, 8, c), axis=0)


def _matmul(a, b, mode, tm, tn, tk, out_dtype, name, after=None, n_outer=False, m_cols=None):
    m0 = 0
    b_parts = list(b) if isinstance(b, (list, tuple)) else [b]
    if mode == "nn":
        (M, K), N = a.shape, b.shape[1]
    elif mode == "nt":
        (M, K), N = a.shape, b.shape[0]
    else:
        (K, M), N = a.shape, sum(p.shape[1] for p in b_parts)
        if m_cols is not None:
            m0, M = m_cols
    tm, tn, tk = min(tm, M), min(tn, N), min(tk, K)
    assert M % tm == 0 and N % tn == 0 and K % tk == 0 and m0 % tm == 0, (name, M, N, K, tm, tn, tk)
    assert len(b_parts) == 1 or (mode == "tn" and all(p.shape[1] % tn == 0 for p in b_parts)), name
    nk = K // tk
    mb = m0 // tm
    counts = [p.shape[1] // tn for p in b_parts] if mode == "tn" else [N // tn]
    starts = [sum(counts[:p]) for p in range(len(counts))]

    def ij(fn):
        return (lambda j, i, k: fn(i, j, k)) if n_outer else fn

    if mode == "nn":
        a_spec = pl.BlockSpec((tm, tk), ij(lambda i, j, k: (i, k)))
        b_spec = pl.BlockSpec((tk, tn), ij(lambda i, j, k: (k, j)))
        dims = (((1,), (0,)), ((), ()))
    elif mode == "nt":
        a_spec = pl.BlockSpec((tm, tk), ij(lambda i, j, k: (i, k)))
        b_spec = pl.BlockSpec((tn, tk), ij(lambda i, j, k: (j, k)))
        dims = (((1,), (1,)), ((), ()))
    else:
        a_spec = pl.BlockSpec((tk, tm), ij(lambda i, j, k: (k, i + mb)))
        dims = (((0,), (0,)), ((), ()))
    if mode == "tn":
        b_specs = [pl.BlockSpec((tk, tn), ij(lambda i, j, k, s=s, n=n: (k, jnp.clip(j - s, 0, n - 1))))
                   for s, n in zip(starts, counts)]
    else:
        b_specs = [b_spec]
    nb = len(b_parts)

    def body(a_ref, *rest):
        b_refs = rest[:nb]
        o_ref, acc_ref = rest[-2:]
        j = pl.program_id(0 if n_outer else 1)
        k = pl.program_id(2)

        def compute(b_ref):
            part = lax.dot_general(a_ref[...].astype(BF16), b_ref[...].astype(BF16), dims,
                                   preferred_element_type=F32)
            if nk == 1:
                o_ref[...] = part.astype(out_dtype)
            else:
                @pl.when(k == 0)
                def _():
                    acc_ref[...] = part

                @pl.when(k > 0)
                def _():
                    acc_ref[...] += part

                @pl.when(k == nk - 1)
                def _():
                    o_ref[...] = acc_ref[...].astype(out_dtype)

        if nb == 1:
            compute(b_refs[0])
        else:
            for p in range(nb):
                pl.when(jnp.logical_and(j >= starts[p], j < starts[p] + counts[p]))(
                    lambda b_ref=b_refs[p]: compute(b_ref))

    return pl.pallas_call(
        body, name=name,
        out_shape=jax.ShapeDtypeStruct((M, N), out_dtype),
        grid=(N // tn, M // tm, nk) if n_outer else (M // tm, N // tn, nk),
        in_specs=[a_spec] + b_specs + ([] if after is None else [pl.BlockSpec(memory_space=pl.ANY)]),
        out_specs=pl.BlockSpec((tm, tn), ij(lambda i, j, k: (i, j))),
        scratch_shapes=[pltpu.VMEM((tm, tn) if nk > 1 else (8, 128), F32)],
        compiler_params=_cp(("parallel", "parallel", "arbitrary")),
    )(*([a] + b_parts + ([] if after is None else [after])))


def _row_tile(S):
    return min(512, S)


def _in_proj(x, g, sc, sh, w_in):
    S, D = x.shape
    N = w_in.shape[1]
    tm = min(512, S)

    def body(x_ref, g_ref, sc_ref, sh_ref, w_ref, h_ref, z_ref):
        xv = x_ref[...]
        r = lax.rsqrt(jnp.mean(xv * xv, axis=-1, keepdims=True) + EPS)
        h = (xv * r * (g_ref[...] * (1.0 + sc_ref[...])) + sh_ref[...]).astype(BF16)
        h_ref[...] = h
        z_ref[...] = jnp.dot(h, w_ref[...], preferred_element_type=F32).astype(BF16)

    row = pl.BlockSpec((tm, D), lambda i: (i, 0))
    par = _full((1, D))
    return pl.pallas_call(
        body, name="in_proj",
        out_shape=(jax.ShapeDtypeStruct((S, D), BF16), jax.ShapeDtypeStruct((S, N), BF16)), grid=(S // tm,),
        in_specs=[row, par, par, par, _resident((D, N))], out_specs=(row, pl.BlockSpec((tm, N), lambda i: (i, 0))),
        compiler_params=_cp(("parallel",)))(x, g, sc, sh, w_in)


def _fill_shifted(buf_ref, sh_ref):
    n = buf_ref.shape[0] - 8
    for s in range(1, 8):
        sh_ref[s, 0:n, :] = buf_ref[s:s + n, :]


def _window(buf_ref, sh_ref, off, n):
    s = off % 8
    return buf_ref[off:off + n, :] if s == 0 else sh_ref[s, off - s:off - s + n, :]


def _conv_fwd(z, conv_w, conv_b, ln_g, ln_b):
    S = z.shape[0]
    tm = min(128, S)
    sub = 32
    hb = tm // HALO

    def body(a_ref, g_ref, ha_ref, hg_ref, w_ref, b_ref, lg_ref, lb_ref, yc_ref, s_ref, ug_ref, sh_ref):
        i = pl.program_id(0)
        halo = ha_ref[...].astype(F32) * _sig(hg_ref[...].astype(F32))
        ug_ref[0:HALO, :] = jnp.where(i == 0, 0.0, halo)
        ug_ref[HALO:, :] = a_ref[...].astype(F32) * _sig(g_ref[...].astype(F32))
        _fill_shifted(ug_ref, sh_ref)
        for rb in range(tm // sub):
            acc = jnp.zeros((sub, CW), F32) + b_ref[...]
            for k in range(KW):
                off = rb * sub + HALO - (KW - 1) + k
                acc = acc + w_ref[k:k + 1, :] * _window(ug_ref, sh_ref, off, sub)
            yc_ref[rb * sub:(rb + 1) * sub, :] = acc
            mu = jnp.mean(acc, axis=-1, keepdims=True)
            cen = acc - mu
            rstd = lax.rsqrt(jnp.mean(cen * cen, axis=-1, keepdims=True) + EPS)
            ln = cen * rstd * lg_ref[...] + lb_ref[...]
            s_ref[rb * sub:(rb + 1) * sub, :] = (ln * _sig(ln)).astype(BF16)

    prev = lambda i: (jnp.maximum(i * hb - 1, 0), 0)
    return pl.pallas_call(
        body, name="conv_fwd",
        out_shape=(jax.ShapeDtypeStruct((S, CW), F32), jax.ShapeDtypeStruct((S, CW), BF16)),
        grid=(S // tm,),
        in_specs=[pl.BlockSpec((tm, CW), lambda i: (i, 0)), pl.BlockSpec((tm, CW), lambda i: (i, 1)),
                  pl.BlockSpec((HALO, CW), prev), pl.BlockSpec((HALO, CW), lambda i: (jnp.maximum(i * hb - 1, 0), 1)),
                  _full((KW, CW)), _full((1, CW)), _full((1, CW)), _full((1, CW))],
        out_specs=(pl.BlockSpec((tm, CW), lambda i: (i, 0)), pl.BlockSpec((tm, CW), lambda i: (i, 0))),
        scratch_shapes=[pltpu.VMEM((tm + HALO, CW), F32), pltpu.VMEM((8, tm + HALO, CW), F32)],
        compiler_params=_cp(("parallel",)))(z, z, z, z, conv_w, conv_b, ln_g, ln_b)


def _conv_bwd_ln(dyconv, w_cp, yc, ln_g, ln_b):
    S = yc.shape[0]
    tm = _row_tile(S)

    def body(dy_ref, w_ref, yc_ref, lg_ref, lb_ref, dyc_ref, dlg_ref, dlb_ref, dcb_ref):
        i = pl.program_id(0)
        dsc = lax.dot_general(dy_ref[...], w_ref[...], (((1,), (1,)), ((), ())), preferred_element_type=F32)
        yc_v = yc_ref[...]
        mu = jnp.mean(yc_v, axis=-1, keepdims=True)
        cen = yc_v - mu
        rstd = lax.rsqrt(jnp.mean(cen * cen, axis=-1, keepdims=True) + EPS)
        yn = cen * rstd
        ln = yn * lg_ref[...] + lb_ref[...]
        sl = _sig(ln)
        dln = dsc * (sl * (1.0 + ln * (1.0 - sl)))
        dyn = dln * lg_ref[...]
        dyc = rstd * (dyn - jnp.mean(dyn, axis=-1, keepdims=True)
                      - yn * jnp.mean(dyn * yn, axis=-1, keepdims=True))
        dyc_ref[...] = dyc

        @pl.when(i == 0)
        def _():
            dlg_ref[...] = jnp.zeros_like(dlg_ref)
            dlb_ref[...] = jnp.zeros_like(dlb_ref)
            dcb_ref[...] = jnp.zeros_like(dcb_ref)

        dlg_ref[...] += _colsum8(dln * yn)
        dlb_ref[...] += _colsum8(dln)
        dcb_ref[...] += _colsum8(dyc)

    row = pl.BlockSpec((tm, CW), lambda i: (i, 0))
    acc = jax.ShapeDtypeStruct((8, CW), F32)
    return pl.pallas_call(
        body, name="conv_bwd_ln",
        out_shape=(jax.ShapeDtypeStruct((S, CW), F32), acc, acc, acc), grid=(S // tm,),
        in_specs=[pl.BlockSpec((tm, D_MODEL), lambda i: (i, 0)), _full((CW, D_MODEL)), row, _full((1, CW)),
                  _full((1, CW))],
        out_specs=(row, _full((8, CW)), _full((8, CW)), _full((8, CW))),
        compiler_params=_cp(("arbitrary",)))(dyconv, w_cp, yc, ln_g, ln_b)


def _conv_bwd(dyc, z, conv_w):
    S = z.shape[0]
    tm = min(128, S)
    sub = 32
    hb = tm // HALO
    nt = S // tm

    def body(d_ref, dn_ref, a_ref, g_ref, ha_ref, hg_ref, w_ref, dz_ref, dw_ref, ug_ref, dy_ref, ugs_ref, dys_ref):
        i = pl.program_id(0)
        halo = ha_ref[...].astype(F32) * _sig(hg_ref[...].astype(F32))
        ug_ref[0:HALO, :] = jnp.where(i == 0, 0.0, halo)
        a = a_ref[...].astype(F32)
        sg = _sig(g_ref[...].astype(F32))
        ug_ref[HALO:, :] = a * sg
        dy_ref[0:tm, :] = d_ref[...]
        dy_ref[tm:, :] = jnp.where(i == nt - 1, 0.0, dn_ref[...])
        _fill_shifted(ug_ref, ugs_ref)
        _fill_shifted(dy_ref, dys_ref)

        @pl.when(i == 0)
        def _():
            dw_ref[...] = jnp.zeros_like(dw_ref)

        for rb in range(tm // sub):
            r0 = rb * sub
            acc = jnp.zeros((sub, CW), F32)
            dyc_b = dy_ref[r0:r0 + sub, :]
            for k in range(KW):
                up = r0 + (KW - 1) - k
                acc = acc + w_ref[k:k + 1, :] * _window(dy_ref, dys_ref, up, sub)
                off = r0 + HALO - (KW - 1) + k
                dw_ref[k * 8:(k + 1) * 8, :] += _colsum8(dyc_b * _window(ug_ref, ugs_ref, off, sub))
            a_b = a[r0:r0 + sub, :]
            sg_b = sg[r0:r0 + sub, :]
            dz_ref[r0:r0 + sub, 0:CW] = (acc * sg_b).astype(BF16)
            dz_ref[r0:r0 + sub, CW:2 * CW] = (acc * a_b * sg_b * (1.0 - sg_b)).astype(BF16)

    return pl.pallas_call(
        body, name="conv_bwd",
        out_shape=(jax.ShapeDtypeStruct((S, 2 * CW), BF16), jax.ShapeDtypeStruct((KW * 8, CW), F32)),
        grid=(nt,),
        in_specs=[pl.BlockSpec((tm, CW), lambda i: (i, 0)),
                  pl.BlockSpec((HALO, CW), lambda i: (jnp.minimum((i + 1) * hb, nt * hb - 1), 0)),
                  pl.BlockSpec((tm, CW), lambda i: (i, 0)), pl.BlockSpec((tm, CW), lambda i: (i, 1)),
                  pl.BlockSpec((HALO, CW), lambda i: (jnp.maximum(i * hb - 1, 0), 0)),
                  pl.BlockSpec((HALO, CW), lambda i: (jnp.maximum(i * hb - 1, 0), 1)),
                  _full((KW, CW))],
        out_specs=(pl.BlockSpec((tm, 2 * CW), lambda i: (i, 0)), _full((KW * 8, CW))),
        scratch_shapes=[pltpu.VMEM((tm + HALO, CW), F32), pltpu.VMEM((tm + HALO, CW), F32),
                        pltpu.VMEM((8, tm + HALO, CW), F32), pltpu.VMEM((8, tm + HALO, CW), F32)],
        compiler_params=_cp(("arbitrary",)))(dyc, dyc, z, z, z, z, conv_w)


_GELU_C = math.sqrt(2.0 / math.pi)


def _gelu(x):
    return 0.5 * x * (1.0 + jnp.tanh(_GELU_C * (x + 0.044715 * x * x * x)))


def _gelu_grad(x):
    t = jnp.tanh(_GELU_C * (x + 0.044715 * x * x * x))
    return 0.5 * (1.0 + t) + 0.5 * x * (1.0 - t * t) * (_GELU_C * (1.0 + 3 * 0.044715 * x * x))


_NCL = 4
_UC = CW // _NCL
_LW = NST // _NCL
_CS = 2 * _LW


def _ssm_fwd(z, bb, cm, d, tab):
    S = z.shape[0]
    tm = min(512, S)

    def body(u_ref, bb_ref, cm_ref, d_ref, t_ref, x_ref, ys_ref, yg_ref, car_ref):
        i = pl.program_id(0)

        @pl.when(i == 0)
        def _():
            car_ref[...] = jnp.zeros_like(car_ref)

        u16 = u_ref[...]
        u = u16.astype(F32)
        for c in range(_NCL):
            lre = pl.ds(c * _CS, _LW)
            lim = pl.ds(c * _CS + _LW, _LW)
            tl = pl.ds(c * _LW, _LW)
            x_ref[:, c * _CS:(c + 1) * _CS] = jnp.dot(u16[:, c * _UC:(c + 1) * _UC], bb_ref[c],
                                                      preferred_element_type=F32)

            def blk(j, car):
                cr, ci = car
                rows = pl.ds(pl.multiple_of(j * 8, 8), 8)
                r = x_ref[rows, lre]
                im = x_ref[rows, lim]
                for lvl, s in enumerate((1, 2, 4)):
                    mr = t_ref[16 * lvl:16 * lvl + 8, tl]
                    mi = t_ref[16 * lvl + 8:16 * lvl + 16, tl]
                    sr = pltpu.roll(r, s, 0)
                    si = pltpu.roll(im, s, 0)
                    r, im = r + (mr * sr - mi * si), im + (mr * si + mi * sr)
                pr = t_ref[48:56, tl]
                pi_ = t_ref[56:64, tl]
                r, im = r + (pr * cr - pi_ * ci), im + (pr * ci + pi_ * cr)
                x_ref[rows, lre] = r
                x_ref[rows, lim] = im
                return (jnp.broadcast_to(r[7:8, :], (8, _LW)), jnp.broadcast_to(im[7:8, :], (8, _LW)))

            cr, ci = lax.fori_loop(0, tm // 8, blk, (car_ref[:, lre], car_ref[:, lim]))
            car_ref[:, lre] = cr
            car_ref[:, lim] = ci
            cols = slice(c * _UC, (c + 1) * _UC)
            ys = jnp.dot(x_ref[:, c * _CS:(c + 1) * _CS].astype(BF16), cm_ref[c], preferred_element_type=F32)
            ys = ys + d_ref[:, cols] * u[:, cols]
            ys_ref[:, cols] = ys
            yg_ref[:, cols] = _gelu(ys).astype(BF16)

    return pl.pallas_call(
        body, name="ssm_fwd",
        out_shape=(jax.ShapeDtypeStruct((S, 2 * NST), F32), jax.ShapeDtypeStruct((S, CW), F32),
                   jax.ShapeDtypeStruct((S, CW), BF16)),
        grid=(S // tm,),
        in_specs=[pl.BlockSpec((tm, CW), lambda i: (i, 2)), _full((_NCL, _UC, _CS)), _full((_NCL, _CS, _UC)),
                  _full((1, CW)), _full((64, NST))],
        out_specs=(pl.BlockSpec((tm, 2 * NST), lambda i: (i, 0)), pl.BlockSpec((tm, CW), lambda i: (i, 0)),
                   pl.BlockSpec((tm, CW), lambda i: (i, 0))),
        scratch_shapes=[pltpu.VMEM((8, 2 * NST), F32)],
        compiler_params=_cp(("arbitrary",)))(z, bb, cm, d, tab)


def _ssm_bwd(dzz, w_glu, ys, z, xs, cmt, bbt, d, tab, after):
    S = z.shape[0]
    tm = min(512, S)
    nt = S // tm
    tdims = (((0,), (0,)), ((), ()))

    def body(dzz_ref, wglu_ref, ys_ref, u_ref, x_ref, cmt_ref, bbt_ref, d_ref, t_ref, after_ref,
             du_ref, de_ref, dd_ref, dc_hbm, dbb_hbm, car_ref, lam_ref, dc_ref, dbb_ref):
        i = pl.program_id(0)

        @pl.when(i == 0)
        def _():
            car_ref[...] = jnp.zeros_like(car_ref)
            de_ref[...] = jnp.zeros_like(de_ref)
            dd_ref[...] = jnp.zeros_like(dd_ref)
            dc_ref[...] = jnp.zeros_like(dc_ref)
            dbb_ref[...] = jnp.zeros_like(dbb_ref)

        u16 = u_ref[...]
        u = u16.astype(F32)
        dyg = lax.dot_general(dzz_ref[...], wglu_ref[...], (((1,), (1,)), ((), ())), preferred_element_type=F32)
        dys = dyg * _gelu_grad(ys_ref[...])
        dys16 = dys.astype(BF16)
        dd_ref[...] += _colsum8(dys * u)
        row = lax.broadcasted_iota(jnp.int32, (8, _LW), 0)
        for c in range(_NCL):
            lre = pl.ds(c * _CS, _LW)
            lim = pl.ds(c * _CS + _LW, _LW)
            tl = pl.ds(c * _LW, _LW)
            cols = slice(c * _UC, (c + 1) * _UC)
            span = slice(c * _CS, (c + 1) * _CS)
            dc_ref[cols, :] += lax.dot_general(dys16[:, cols], x_ref[:, span].astype(BF16), tdims,
                                               preferred_element_type=F32)
            lam_ref[...] = jnp.dot(dys16[:, cols], cmt_ref[c], preferred_element_type=F32)

            def blk(jj, car):
                cr, ci, ar, ai = car
                j = tm // 8 - 1 - jj
                rows = pl.ds(pl.multiple_of(j * 8, 8), 8)
                r = lam_ref[rows, 0:_LW]
                im = lam_ref[rows, _LW:_CS]
                for lvl, s in enumerate((1, 2, 4)):
                    mr = t_ref[16 * lvl:16 * lvl + 8, tl]
                    mi = t_ref[16 * lvl + 8:16 * lvl + 16, tl]
                    sr = pltpu.roll(r, 8 - s, 0)
                    si = pltpu.roll(im, 8 - s, 0)
                    r, im = r + (mr * sr - mi * si), im + (mr * si + mi * sr)
                pr = t_ref[48:56, tl]
                pi_ = t_ref[56:64, tl]
                r, im = r + (pr * cr - pi_ * ci), im + (pr * ci + pi_ * cr)
                lam_ref[rows, 0:_LW] = r
                lam_ref[rows, _LW:_CS] = im
                nr = jnp.where(row == 7, cr, pltpu.roll(r, 7, 0))
                ni = jnp.where(row == 7, ci, pltpu.roll(im, 7, 0))
                xr = x_ref[rows, lre]
                xi = x_ref[rows, lim]
                ar = ar + (nr * xr + ni * xi)
                ai = ai + (ni * xr - nr * xi)
                return (jnp.broadcast_to(r[0:1, :], (8, _LW)), jnp.broadcast_to(im[0:1, :], (8, _LW)), ar, ai)

            zero = jnp.zeros((8, _LW), F32)
            cr, ci, ar, ai = lax.fori_loop(0, tm // 8, blk, (car_ref[:, lre], car_ref[:, lim], zero, zero))
            car_ref[:, lre] = cr
            car_ref[:, lim] = ci
            de_ref[0:8, tl] += ar
            de_ref[8:16, tl] += ai
            lam16 = lam_ref[...].astype(BF16)
            dbb_ref[cols, :] += lax.dot_general(u16[:, cols], lam16, tdims, preferred_element_type=F32)
            du = jnp.dot(lam16, bbt_ref[c], preferred_element_type=F32) + dys[:, cols] * d_ref[:, cols]
            du_ref[:, cols] = du.astype(BF16)

        @pl.when(i == nt - 1)
        def _():
            pltpu.sync_copy(dc_ref, dc_hbm)
            pltpu.sync_copy(dbb_ref, dbb_hbm)

    rev = lambda i: (nt - 1 - i, 0)
    once = lambda shape: pl.BlockSpec(shape, lambda *_: (0,) * len(shape), pipeline_mode=pl.Buffered(1))
    cross = jax.ShapeDtypeStruct((CW, _CS), F32)
    return pl.pallas_call(
        body, name="ssm_bwd",
        out_shape=(jax.ShapeDtypeStruct((S, CW), BF16), jax.ShapeDtypeStruct((16, NST), F32),
                   jax.ShapeDtypeStruct((8, CW), F32), cross, cross),
        grid=(nt,),
        in_specs=[pl.BlockSpec((tm, 2 * D_MODEL), rev), once((CW, 2 * D_MODEL)), pl.BlockSpec((tm, CW), rev),
                  pl.BlockSpec((tm, CW), lambda i: (nt - 1 - i, 2)), pl.BlockSpec((tm, 2 * NST), rev),
                  once((_NCL, _UC, _CS)), once((_NCL, _CS, _UC)), _full((1, CW)), once((64, NST)),
                  pl.BlockSpec(memory_space=pl.ANY)],
        out_specs=(pl.BlockSpec((tm, CW), rev), _full((16, NST)), _full((8, CW)),
                   pl.BlockSpec(memory_space=pl.ANY), pl.BlockSpec(memory_space=pl.ANY)),
        scratch_shapes=[pltpu.VMEM((8, 2 * NST), F32), pltpu.VMEM((tm, _CS), F32),
                        pltpu.VMEM((CW, _CS), F32), pltpu.VMEM((CW, _CS), F32)],
        compiler_params=_cp(("arbitrary",)))(dzz, w_glu, ys, z, xs, cmt, bbt, d, tab, after)


def _ssm_prep(a_re, a_im, b_re, b_im, log_dt):
    dt = jnp.exp(log_dt.reshape(G))[:, None]
    mag = jnp.exp(dt * a_re)
    e_re, e_im = mag * jnp.cos(dt * a_im), mag * jnp.sin(dt * a_im)
    n_re, n_im = e_re - 1.0, e_im
    den = a_re * a_re + a_im * a_im
    q_re = (n_re * a_re + n_im * a_im) / den
    q_im = (n_im * a_re - n_re * a_im) / den
    bb_re = q_re[..., None] * b_re - q_im[..., None] * b_im
    bb_im = q_re[..., None] * b_im + q_im[..., None] * b_re
    return e_re, e_im, bb_re, bb_im


def _scan_tables(e_re, e_im, reverse):
    er = e_re.reshape(1, NST)
    ei = e_im.reshape(1, NST)
    if reverse:
        ei = -ei
    pows = [(er, ei)]
    for _ in range(7):
        pr, pi_ = pows[-1]
        pows.append((pr * er - pi_ * ei, pr * ei + pi_ * er))
    row = jnp.arange(8)[:, None]
    out = []
    for s in (1, 2, 4):
        pr, pi_ = pows[s - 1]
        keep = (row + s <= 7) if reverse else (row >= s)
        out += [jnp.where(keep, pr, 0.0), jnp.where(keep, pi_, 0.0)]
    allr = jnp.concatenate([p[0] for p in pows], 0)
    alli = jnp.concatenate([p[1] for p in pows], 0)
    if reverse:
        allr, alli = allr[::-1], alli[::-1]
    out += [allr, alli]
    return jnp.concatenate(out, 0).astype(F32)


def _block_diag_mats(bb_re, bb_im, c_re, c_im):
    gc = G // _NCL
    eye = jnp.eye(gc, dtype=F32)
    bre = jnp.einsum("cjph,jk->cjhkp", bb_re.reshape(_NCL, gc, P, H), eye).reshape(_NCL, _UC, _LW)
    bim = jnp.einsum("cjph,jk->cjhkp", bb_im.reshape(_NCL, gc, P, H), eye).reshape(_NCL, _UC, _LW)
    bb = jnp.concatenate([bre, bim], 2)
    cre = jnp.einsum("cjhp,jk->cjpkh", c_re.reshape(_NCL, gc, H, P), eye).reshape(_NCL, _LW, _UC)
    cim = jnp.einsum("cjhp,jk->cjpkh", c_im.reshape(_NCL, gc, H, P), eye).reshape(_NCL, _LW, _UC)
    cm = jnp.concatenate([cre, -cim], 1)
    return bb, cm


def _diag_blocks(cross, imag):
    gc = G // _NCL
    off = _LW if imag else 0
    return jnp.stack([cross[H * g:H * (g + 1), off + P * (g % gc):off + P * (g % gc + 1)] for g in range(G)])


def _mix_fwd(scv, yg, z, x, w_cp, w_glu, w_out, g1, n2g, sc2, sh2):
    S = z.shape[0]
    tm = min(512, S)
    D = D_MODEL

    def body(s_ref, yg_ref, glc0_ref, glc1_ref, gls0_ref, gls1_ref, x_ref, wcp_ref, wglu_ref, wout_ref,
             g1_ref, n2_ref, sc_ref, sh_ref, yc_ref, zz_ref, m_ref, o_ref, x2_ref, h2_ref):
        y_conv = jnp.dot(s_ref[...], wcp_ref[...], preferred_element_type=F32)
        zz = jnp.dot(yg_ref[...], wglu_ref[...], preferred_element_type=F32)
        yc_ref[...] = y_conv.astype(BF16)
        zz_ref[...] = zz.astype(BF16)
        for half, (glc_ref, gls_ref) in enumerate(((glc0_ref, gls0_ref), (glc1_ref, gls1_ref))):
            lo, hi = half * CW, (half + 1) * CW
            y_ssm = zz[:, lo:hi] * _sig(zz[:, D + lo:D + hi])
            m_ref[:, lo:hi] = (_sig(glc_ref[...].astype(F32)) * y_conv[:, lo:hi]
                               + _sig(gls_ref[...].astype(F32)) * y_ssm).astype(BF16)
        o = jnp.dot(m_ref[...], wout_ref[...], preferred_element_type=F32)
        o_ref[...] = o.astype(BF16)
        xv = x_ref[...] + g1_ref[...] * o
        x2_ref[...] = xv
        r = lax.rsqrt(jnp.mean(xv * xv, axis=-1, keepdims=True) + EPS)
        h2_ref[...] = (xv * r * (n2_ref[...] * (1.0 + sc_ref[...])) + sh_ref[...]).astype(BF16)

    zb_ = lambda j: pl.BlockSpec((tm, CW), lambda i: (i, j))
    row = lambda w: pl.BlockSpec((tm, w), lambda i: (i, 0))
    par = _full((1, D))
    bf = lambda w: jax.ShapeDtypeStruct((S, w), BF16)
    return pl.pallas_call(
        body, name="mix_fwd",
        out_shape=(bf(D), bf(2 * D), bf(D), bf(D), jax.ShapeDtypeStruct((S, D), F32), bf(D)),
        grid=(S // tm,),
        in_specs=[row(CW), row(CW), zb_(3), zb_(4), zb_(5), zb_(6), row(D), _resident((CW, D)),
                  _resident((CW, 2 * D)), _resident((D, D)), par, par, par, par],
        out_specs=(row(D), row(2 * D), row(D), row(D), row(D), row(D)),
        compiler_params=_cp(("parallel",)))(scv, yg, z, z, z, z, x, w_cp, w_glu, w_out, g1, n2g, sc2, sh2)


def _mix_bwd(do, w_out, z, zz, y_conv, after):
    S = z.shape[0]
    tm = min(512, S)
    D = D_MODEL

    def body(do_ref, w_ref, glc0_ref, glc1_ref, gls0_ref, gls1_ref, za_ref, zb_ref, yc_ref, after_ref,
             dyc_ref, dgl_ref, dzz_ref):
        dm = lax.dot_general(do_ref[...], w_ref[...], (((1,), (1,)), ((), ())), preferred_element_type=F32)
        for half, (glc_ref, gls_ref) in enumerate(((glc0_ref, gls0_ref), (glc1_ref, gls1_ref))):
            lo, hi = half * CW, (half + 1) * CW
            dm_v = dm[:, lo:hi]
            sgc = _sig(glc_ref[...].astype(F32))
            sgs = _sig(gls_ref[...].astype(F32))
            szb = _sig(zb_ref[:, lo:hi].astype(F32))
            za = za_ref[:, lo:hi].astype(F32)
            dyc_ref[:, lo:hi] = (dm_v * sgc).astype(BF16)
            dgl_ref[:, lo:hi] = (dm_v * yc_ref[:, lo:hi].astype(F32) * sgc * (1.0 - sgc)).astype(BF16)
            dys = dm_v * sgs
            dgl_ref[:, D + lo:D + hi] = (dys * (za * szb) * (1.0 - sgs)).astype(BF16)
            dzz_ref[:, lo:hi] = (dys * szb).astype(BF16)
            dzz_ref[:, D + lo:D + hi] = (dys * za * szb * (1.0 - szb)).astype(BF16)

    zb_ = lambda j: pl.BlockSpec((tm, CW), lambda i: (i, j))
    wide = lambda j: pl.BlockSpec((tm, D), lambda i: (i, j))
    return pl.pallas_call(
        body, name="mix_bwd",
        out_shape=(jax.ShapeDtypeStruct((S, D), BF16), jax.ShapeDtypeStruct((S, 2 * D), BF16),
                   jax.ShapeDtypeStruct((S, 2 * D), BF16)),
        grid=(S // tm,),
        in_specs=[wide(0), _resident((D, D)), zb_(3), zb_(4), zb_(5), zb_(6), wide(0), wide(1), wide(0),
                  pl.BlockSpec(memory_space=pl.ANY)],
        out_specs=(wide(0), pl.BlockSpec((tm, 2 * D), lambda i: (i, 0)), pl.BlockSpec((tm, 2 * D), lambda i: (i, 0))),
        compiler_params=_cp(("parallel",)))(do, w_out, z, z, z, z, zz, zz, y_conv, after)


_FC = 1408


def _ffn_in_act(h2, w_fi):
    S, D = h2.shape
    tm = min(512, S)

    def body(h_ref, w_ref, f_ref, a_ref):
        hv = h_ref[...]
        for c in range(FH // _FC):
            lo, hi = c * _FC, (c + 1) * _FC
            g = jnp.dot(hv, w_ref[:, lo:hi], preferred_element_type=F32)
            u = jnp.dot(hv, w_ref[:, FH + lo:FH + hi], preferred_element_type=F32)
            f_ref[:, lo:hi] = g.astype(BF16)
            f_ref[:, FH + lo:FH + hi] = u.astype(BF16)
            a_ref[:, lo:hi] = (g * _sig(g) * u).astype(BF16)

    return pl.pallas_call(
        body, name="ffn_in_act",
        out_shape=(jax.ShapeDtypeStruct((S, 2 * FH), BF16), jax.ShapeDtypeStruct((S, FH), BF16)),
        grid=(S // tm,),
        in_specs=[pl.BlockSpec((tm, D), lambda i: (i, 0)), _resident((D, 2 * FH))],
        out_specs=(pl.BlockSpec((tm, 2 * FH), lambda i: (i, 0)), pl.BlockSpec((tm, FH), lambda i: (i, 0))),
        compiler_params=_cp(("parallel",)))(h2, w_fi)


def _ffn_bwd(do2, w_fo, f, after):
    S, D = do2.shape
    tm = min(512, S)

    def body(d_ref, w_ref, f_ref, after_ref, df_ref):
        dv = d_ref[...]
        for c in range(FH // _FC):
            lo, hi = c * _FC, (c + 1) * _FC
            dact = lax.dot_general(dv, w_ref[lo:hi, :], (((1,), (1,)), ((), ())), preferred_element_type=F32)
            g = f_ref[:, lo:hi].astype(F32)
            u = f_ref[:, FH + lo:FH + hi].astype(F32)
            sg = _sig(g)
            df_ref[:, lo:hi] = (dact * u * (sg * (1.0 + g * (1.0 - sg)))).astype(BF16)
            df_ref[:, FH + lo:FH + hi] = (dact * g * sg).astype(BF16)

    return pl.pallas_call(
        body, name="ffn_bwd", out_shape=jax.ShapeDtypeStruct((S, 2 * FH), BF16), grid=(S // tm,),
        in_specs=[pl.BlockSpec((tm, D), lambda i: (i, 0)), _resident((FH, D)),
                  pl.BlockSpec((tm, 2 * FH), lambda i: (i, 0)), pl.BlockSpec(memory_space=pl.ANY)],
        out_specs=pl.BlockSpec((tm, 2 * FH), lambda i: (i, 0)),
        compiler_params=_cp(("parallel",)))(do2, w_fo, f, after)


def _ffn_out_final(x2, act, w_fo, g2, fg, tgt):
    S, D = x2.shape
    tm = min(512, S)

    def body(x2_ref, a_ref, w_ref, g2_ref, fg_ref, t_ref, dx3_ref, do2_ref, ls_ref, dfg_ref, dg2_ref):
        i = pl.program_id(0)

        @pl.when(i == 0)
        def _():
            ls_ref[...] = jnp.zeros_like(ls_ref)
            dfg_ref[...] = jnp.zeros_like(dfg_ref)
            dg2_ref[...] = jnp.zeros_like(dg2_ref)

        o2 = jnp.dot(a_ref[...], w_ref[...], preferred_element_type=F32)
        x3 = x2_ref[...] + g2_ref[...] * o2
        r = lax.rsqrt(jnp.mean(x3 * x3, axis=-1, keepdims=True) + EPS)
        xn = x3 * r
        err = xn * fg_ref[...] - t_ref[...]
        dy = err * (1.0 / D)
        dxn = dy * fg_ref[...]
        dx3 = r * (dxn - xn * jnp.mean(dxn * xn, axis=-1, keepdims=True))
        dx3_ref[...] = dx3
        do2_ref[...] = (dx3 * g2_ref[...]).astype(BF16)
        e2 = _colsum8(err * err)
        lanes = e2[:, 0:128]
        for q in range(1, D // 128):
            lanes = lanes + e2[:, q * 128:(q + 1) * 128]
        ls_ref[...] += lanes * (0.5 / D)
        dfg_ref[...] += _colsum8(dy * xn)
        dg2_ref[...] += _colsum8(dx3 * o2)

    row = pl.BlockSpec((tm, D), lambda i: (i, 0))
    par = _full((1, D))
    return pl.pallas_call(
        body, name="final_loss",
        out_shape=(jax.ShapeDtypeStruct((S, D), F32), jax.ShapeDtypeStruct((S, D), BF16),
                   jax.ShapeDtypeStruct((8, 128), F32), jax.ShapeDtypeStruct((8, D), F32),
                   jax.ShapeDtypeStruct((8, D), F32)),
        grid=(S // tm,), in_specs=[row, pl.BlockSpec((tm, FH), lambda i: (i, 0)), _resident((FH, D)), par, par, row],
        out_specs=(row, row, _full((8, 128)), _full((8, D)), _full((8, D))),
        compiler_params=_cp(("arbitrary",)))(x2, act, w_fo, g2, fg, tgt)


def _normmod_bwd(dsrc, w, xin, dres, g, sc, gate, o, after, name):
    S, D = xin.shape
    parts = list(dsrc) if isinstance(dsrc, (list, tuple)) else [dsrc]
    widths = [p.shape[1] for p in parts]
    K = sum(widths)
    tm = min(512, S)
    npart = len(parts)

    def body(*refs):
        ds_refs = refs[:npart]
        w_ref, x_ref, dr_ref, g_ref, sc_ref, gate_ref, o_ref, after_ref = refs[npart:npart + 8]
        dx_ref, do_ref, dsh_ref, dsc_ref, dg_ref, dgate_ref = refs[npart + 8:]
        i = pl.program_id(0)

        @pl.when(i == 0)
        def _():
            dsh_ref[...] = jnp.zeros_like(dsh_ref)
            dsc_ref[...] = jnp.zeros_like(dsc_ref)
            dg_ref[...] = jnp.zeros_like(dg_ref)
            dgate_ref[...] = jnp.zeros_like(dgate_ref)

        gv = g_ref[...]
        scale = 1.0 + sc_ref[...]
        xv = x_ref[...]
        r = lax.rsqrt(jnp.mean(xv * xv, axis=-1, keepdims=True) + EPS)
        xn = xv * r
        dh_v, col = None, 0
        for ds_ref, wd in zip(ds_refs, widths):
            t = lax.dot_general(ds_ref[...], w_ref[:, col:col + wd], (((1,), (1,)), ((), ())),
                                preferred_element_type=F32)
            dh_v = t if dh_v is None else dh_v + t
            col += wd
        dxn = dh_v * (gv * scale)
        dx = dr_ref[...] + r * (dxn - xn * jnp.mean(dxn * xn, axis=-1, keepdims=True))
        dx_ref[...] = dx
        do_ref[...] = (dx * gate_ref[...]).astype(BF16)
        hx = dh_v * xn
        dsh_ref[...] += _colsum8(dh_v)
        dsc_ref[...] += _colsum8(hx) * gv
        dg_ref[...] += _colsum8(hx) * scale
        dgate_ref[...] += _colsum8(dx * o_ref[...])

    row = pl.BlockSpec((tm, D), lambda i: (i, 0))
    par = _full((1, D))
    acc = jax.ShapeDtypeStruct((8, D), F32)
    return pl.pallas_call(
        body, name=name,
        out_shape=(jax.ShapeDtypeStruct((S, D), F32), jax.ShapeDtypeStruct((S, D), BF16), acc, acc, acc, acc),
        grid=(S // tm,),
        in_specs=[pl.BlockSpec((tm, wd), lambda i: (i, 0)) for wd in widths]
        + [_resident((D, K)), row, row, par, par, par, row, pl.BlockSpec(memory_space=pl.ANY)],
        out_specs=(row, row, _full((8, D)), _full((8, D)), _full((8, D)), _full((8, D))),
        compiler_params=_cp(("arbitrary",)))(*parts, w, xin, dres, g, sc, gate, o, after)


def _me():
    return lax.axis_index("x"), lax.axis_index("y"), lax.axis_index("c")


def _allgather8(v, name, after=()):
    R, C = v.shape
    after = list(after)

    def body(v_ref, *rest):
        out_ref, send_sems, recv_sems, local_sem = rest[len(after):]
        x, y, c = _me()
        mine = pltpu.make_async_copy(v_ref, out_ref.at[4 * x + 2 * y + c], local_sem)
        mine.start()
        copies = []
        for k in range(1, N_DEV):
            fx, fy, fc = (k >> 2) & 1, (k >> 1) & 1, k & 1
            peer = (x ^ fx, y ^ fy, c ^ fc)
            copies.append(pltpu.make_async_remote_copy(
                src_ref=v_ref, dst_ref=out_ref.at[4 * x + 2 * y + c],
                send_sem=send_sems.at[k - 1], recv_sem=recv_sems.at[k - 1],
                device_id=peer, device_id_type=MESH))
        for cp in copies:
            cp.start()
        for k in range(1, N_DEV):
            fx, fy, fc = (k >> 2) & 1, (k >> 1) & 1, k & 1
            src_slot = 4 * (x ^ fx) + 2 * (y ^ fy) + (c ^ fc)
            pltpu.make_async_remote_copy(
                src_ref=v_ref, dst_ref=out_ref.at[src_slot],
                send_sem=send_sems.at[k - 1], recv_sem=recv_sems.at[k - 1],
                device_id=(x ^ fx, y ^ fy, c ^ fc), device_id_type=MESH).wait_recv()
        for cp in copies:
            cp.wait_send()
        mine.wait()

    return pl.pallas_call(
        body, name=name, out_shape=jax.ShapeDtypeStruct((N_DEV, R, C), v.dtype),
        in_specs=[pl.BlockSpec(memory_space=pltpu.VMEM)] + [pl.BlockSpec(memory_space=pl.ANY)] * len(after),
        out_specs=pl.BlockSpec(memory_space=pltpu.VMEM),
        scratch_shapes=[pltpu.SemaphoreType.DMA((N_DEV - 1,)), pltpu.SemaphoreType.DMA((N_DEV - 1,)),
                        pltpu.SemaphoreType.DMA],
        compiler_params=pltpu.CompilerParams(vmem_limit_bytes=VMEM_LIMIT))(v, *after)


_HBM = pl.BlockSpec(memory_space=pltpu.HBM)
_SEM = pl.BlockSpec(memory_space=pltpu.SEMAPHORE)
_EFFECT = pltpu.SideEffectType.DATAFLOW_SIDE_EFFECTING
_N_PEER = N_CHIP - 1


def _chip_part(ref, axis, n, chip):
    start = pl.multiple_of(chip * n, 8)
    return ref.at[pl.ds(start, n), :] if axis == 0 else ref.at[:, pl.ds(start, n)]


def _gather_copy(k, src_ref, land_ref, send_sems, recv_sems, axis, arriving):
    x, y, c = _me()
    px, py = x ^ ((k >> 1) & 1), y ^ (k & 1)
    chip = 2 * px + py if arriving else 2 * x + y
    return pltpu.make_async_remote_copy(
        src_ref=src_ref, dst_ref=_chip_part(land_ref, axis, src_ref.shape[axis], chip),
        send_sem=send_sems.at[k - 1], recv_sem=recv_sems.at[k - 1], device_id=(px, py, c), device_id_type=MESH)


def _scatter_copy(k, grad_ref, land_ref, send_sems, recv_sems, axis):
    x, y, c = _me()
    px, py = x ^ ((k >> 1) & 1), y ^ (k & 1)
    return pltpu.make_async_remote_copy(
        src_ref=_chip_part(grad_ref, axis, grad_ref.shape[axis] // N_CHIP, 2 * px + py), dst_ref=land_ref.at[k],
        send_sem=send_sems.at[k - 1], recv_sem=recv_sems.at[k - 1], device_id=(px, py, c), device_id_type=MESH)


def _scatter_own(grad_ref, land_ref, send_sems, axis):
    x, y, _ = _me()
    return pltpu.make_async_copy(_chip_part(grad_ref, axis, grad_ref.shape[axis] // N_CHIP, 2 * x + y),
                                 land_ref.at[0], send_sems.at[_N_PEER])


def _own_copy(src_ref, land_ref, sends, axis):
    x, y, _ = _me()
    return pltpu.make_async_copy(src_ref, _chip_part(land_ref, axis, src_ref.shape[axis], 2 * x + y),
                                 sends.at[_N_PEER])


def _gather_start(shards, axes, after, name):
    nw = len(shards)
    lands = []
    for s, ax in zip(shards, axes):
        shp = list(s.shape)
        shp[ax] *= N_CHIP
        lands.append(lax.empty(tuple(shp), s.dtype))

    def body(*refs):
        srcs, zones = refs[:nw], refs[nw:2 * nw]
        sends, recvs = refs[2 * nw + 1:3 * nw + 1], refs[3 * nw + 1:4 * nw + 1]
        token = refs[-1]
        for w in range(nw):
            for k in range(1, N_CHIP):
                _gather_copy(k, srcs[w], zones[w], sends[w], recvs[w], axes[w], False).start()
        for w in range(nw):
            _own_copy(srcs[w], zones[w], sends[w], axes[w]).start()
        token[...] = jnp.zeros_like(token)

    outs = pl.pallas_call(
        body, name=name,
        out_shape=tuple([pltpu.SemaphoreType.DMA((_N_PEER + 1,))] * nw + [pltpu.SemaphoreType.DMA((_N_PEER,))] * nw
                        + [pltpu.HBM(a.shape, a.dtype) for a in list(shards) + list(lands)]
                        + [jax.ShapeDtypeStruct((8, 128), F32)]),
        in_specs=[_HBM] * (2 * nw) + [pl.BlockSpec(memory_space=pl.ANY)],
        out_specs=tuple([_SEM] * (2 * nw) + [_HBM] * (2 * nw) + [pl.BlockSpec(memory_space=pltpu.VMEM)]),
        input_output_aliases={i: 2 * nw + i for i in range(2 * nw)},
        compiler_params=pltpu.CompilerParams(has_side_effects=_EFFECT),
    )(*([pltpu.with_memory_space_constraint(a, pltpu.HBM) for a in list(shards) + list(lands)] + [after]))
    per_weight = [(outs[w], outs[nw + w], outs[2 * nw + w], outs[3 * nw + w]) for w in range(nw)]
    return per_weight, outs[-1]


def _gather_wait(state, axis, after, name):
    send_sems, recv_sems, shard, land = state

    after = list(after) if isinstance(after, (list, tuple)) else [after]

    def body(src_ref, land_ref, sends, recvs, *rest):
        for k in range(1, N_CHIP):
            _gather_copy(k, src_ref, land_ref, sends, recvs, axis, False).wait_send()
            _gather_copy(k, src_ref, land_ref, sends, recvs, axis, True).wait_recv()
        _own_copy(src_ref, land_ref, sends, axis).wait()

    return pl.pallas_call(
        body, name=name, out_shape=(pltpu.HBM(shard.shape, shard.dtype), pltpu.HBM(land.shape, land.dtype)),
        in_specs=[_HBM, _HBM, _SEM, _SEM] + [pl.BlockSpec(memory_space=pl.ANY)] * len(after), out_specs=(_HBM, _HBM),
        input_output_aliases={0: 0, 1: 1},
        compiler_params=pltpu.CompilerParams(has_side_effects=_EFFECT),
    )(shard, land, send_sems, recv_sems, *after)[1]


def _half_rows(ref, c):
    k2 = ref.shape[0] // 2
    return pl.ds(pl.multiple_of(c * k2, 8), k2)


def _half_copy(k, shard_ref, land_ref, send_sems, recv_sems, arriving):
    x, y, c = _me()
    px, py = x ^ ((k >> 1) & 1), y ^ (k & 1)
    n = shard_ref.shape[1]
    chip = 2 * px + py if arriving else 2 * x + y
    return pltpu.make_async_remote_copy(
        src_ref=shard_ref.at[_half_rows(shard_ref, c), :],
        dst_ref=land_ref.at[_half_rows(land_ref, c), pl.ds(pl.multiple_of(chip * n, 128), n)],
        send_sem=send_sems.at[k - 1], recv_sem=recv_sems.at[k - 1], device_id=(px, py, c), device_id_type=MESH)


def _half_own(shard_ref, land_ref, send_sems):
    x, y, c = _me()
    n = shard_ref.shape[1]
    return pltpu.make_async_copy(
        shard_ref.at[_half_rows(shard_ref, c), :],
        land_ref.at[_half_rows(land_ref, c), pl.ds(pl.multiple_of((2 * x + y) * n, 128), n)], send_sems.at[_N_PEER])


def _half_gather_start(shard, after, name):
    K, n = shard.shape
    land = lax.empty((K, N_CHIP * n), shard.dtype)

    def body(shard_ref, land_ref, after_ref, sends, recvs, shard_thru, land_thru, token):
        for k in range(1, N_CHIP):
            _half_copy(k, shard_ref, land_ref, sends, recvs, False).start()
        _half_own(shard_ref, land_ref, sends).start()
        token[...] = jnp.zeros_like(token)

    outs = pl.pallas_call(
        body, name=name,
        out_shape=(pltpu.SemaphoreType.DMA((_N_PEER + 1,)), pltpu.SemaphoreType.DMA((_N_PEER,)),
                   pltpu.HBM(shard.shape, shard.dtype), pltpu.HBM(land.shape, land.dtype),
                   jax.ShapeDtypeStruct((8, 128), F32)),
        in_specs=[_HBM, _HBM, pl.BlockSpec(memory_space=pl.ANY)],
        out_specs=(_SEM, _SEM, _HBM, _HBM, pl.BlockSpec(memory_space=pltpu.VMEM)),
        input_output_aliases={0: 2, 1: 3},
        compiler_params=pltpu.CompilerParams(has_side_effects=_EFFECT),
    )(pltpu.with_memory_space_constraint(shard, pltpu.HBM), pltpu.with_memory_space_constraint(land, pltpu.HBM), after)
    return outs[:4], outs[4]


def _half_gather_wait(state, after, name):
    send_sems, recv_sems, shard, land = state
    after = list(after)

    def body(shard_ref, land_ref, sends, recvs, *rest):
        for k in range(1, N_CHIP):
            _half_copy(k, shard_ref, land_ref, sends, recvs, False).wait_send()
            _half_copy(k, shard_ref, land_ref, sends, recvs, True).wait_recv()
        _half_own(shard_ref, land_ref, sends).wait()

    return pl.pallas_call(
        body, name=name, out_shape=(pltpu.HBM(shard.shape, shard.dtype), pltpu.HBM(land.shape, land.dtype)),
        in_specs=[_HBM, _HBM, _SEM, _SEM] + [pl.BlockSpec(memory_space=pl.ANY)] * len(after), out_specs=(_HBM, _HBM),
        input_output_aliases={0: 0, 1: 1},
        compiler_params=pltpu.CompilerParams(has_side_effects=_EFFECT),
    )(shard, land, send_sems, recv_sems, *after)[1]


def _half_swap_copy(land_ref, send_sem, recv_sem, arriving):
    x, y, c = _me()
    rows = _half_rows(land_ref, 1 - c if arriving else c)
    return pltpu.make_async_remote_copy(src_ref=land_ref.at[rows, :], dst_ref=land_ref.at[rows, :], send_sem=send_sem,
                                        recv_sem=recv_sem, device_id=(x, y, 1 - c), device_id_type=MESH)


def _half_swap_start(land, name):
    def body(land_ref, send, recv, land_thru, token):
        _half_swap_copy(land_ref, send.at[0], recv.at[0], False).start()
        token[...] = jnp.zeros_like(token)

    sem = pltpu.SemaphoreType.DMA((1,))
    outs = pl.pallas_call(
        body, name=name,
        out_shape=(sem, sem, pltpu.HBM(land.shape, land.dtype), jax.ShapeDtypeStruct((8, 128), F32)),
        in_specs=[_HBM], out_specs=(_SEM, _SEM, _HBM, pl.BlockSpec(memory_space=pltpu.VMEM)),
        input_output_aliases={0: 2},
        compiler_params=pltpu.CompilerParams(has_side_effects=_EFFECT),
    )(pltpu.with_memory_space_constraint(land, pltpu.HBM))
    return outs[:3], outs[3]


def _half_swap_wait(state, after, name):
    send, recv, land = state

    def body(land_ref, send_ref, recv_ref, after_ref, got_ref):
        _half_swap_copy(land_ref, send_ref.at[0], recv_ref.at[0], False).wait_send()
        _half_swap_copy(land_ref, send_ref.at[0], recv_ref.at[0], True).wait_recv()

    return pl.pallas_call(
        body, name=name, out_shape=pltpu.HBM(land.shape, land.dtype),
        in_specs=[_HBM, _SEM, _SEM, pl.BlockSpec(memory_space=pl.ANY)], out_specs=_HBM,
        input_output_aliases={0: 0},
        compiler_params=pltpu.CompilerParams(has_side_effects=_EFFECT),
    )(land, send, recv, after)


def _all8_copy(k, v_ref, land_ref, send_sems, recv_sems, arriving):
    x, y, c = _me()
    px, py, pc = x ^ ((k >> 2) & 1), y ^ ((k >> 1) & 1), c ^ (k & 1)
    slot = 4 * px + 2 * py + pc if arriving else 4 * x + 2 * y + c
    return pltpu.make_async_remote_copy(
        src_ref=v_ref, dst_ref=land_ref.at[slot], send_sem=send_sems.at[k - 1], recv_sem=recv_sems.at[k - 1],
        device_id=(px, py, pc), device_id_type=MESH)


def _all8_own(v_ref, land_ref, send_sems):
    x, y, c = _me()
    return pltpu.make_async_copy(v_ref, land_ref.at[4 * x + 2 * y + c], send_sems.at[N_DEV - 1])


def _all8_start(v, name):
    land = lax.empty((N_DEV,) + v.shape, v.dtype)

    def body(v_ref, land_ref, sends, recvs, v_thru, land_thru, token):
        for k in range(1, N_DEV):
            _all8_copy(k, v_ref, land_ref, sends, recvs, False).start()
        _all8_own(v_ref, land_ref, sends).start()
        token[...] = jnp.zeros_like(token)

    outs = pl.pallas_call(
        body, name=name,
        out_shape=(pltpu.SemaphoreType.DMA((N_DEV,)), pltpu.SemaphoreType.DMA((N_DEV - 1,)),
                   pltpu.HBM(v.shape, v.dtype), pltpu.HBM(land.shape, land.dtype),
                   jax.ShapeDtypeStruct((8, 128), F32)),
        in_specs=[_HBM, _HBM], out_specs=(_SEM, _SEM, _HBM, _HBM, pl.BlockSpec(memory_space=pltpu.VMEM)),
        input_output_aliases={0: 2, 1: 3},
        compiler_params=pltpu.CompilerParams(has_side_effects=_EFFECT),
    )(pltpu.with_memory_space_constraint(v, pltpu.HBM), pltpu.with_memory_space_constraint(land, pltpu.HBM))
    return outs[:4], outs[4]


def _all8_wait(state, after, name):
    send_sems, recv_sems, v, land = state

    def body(v_ref, land_ref, sends, recvs, after_ref, v_dead, got_ref):
        for k in range(1, N_DEV):
            _all8_copy(k, v_ref, land_ref, sends, recvs, False).wait_send()
            _all8_copy(k, v_ref, land_ref, sends, recvs, True).wait_recv()
        _all8_own(v_ref, land_ref, sends).wait()

    return pl.pallas_call(
        body, name=name, out_shape=(pltpu.HBM(v.shape, v.dtype), pltpu.HBM(land.shape, land.dtype)),
        in_specs=[_HBM, _HBM, _SEM, _SEM, pl.BlockSpec(memory_space=pl.ANY)], out_specs=(_HBM, _HBM),
        input_output_aliases={0: 0, 1: 1},
        compiler_params=pltpu.CompilerParams(has_side_effects=_EFFECT),
    )(v, land, send_sems, recv_sems, after)[1]


def _swap_copy(w, src_ref, land_ref, send_sems, recv_sems):
    x, y, c = _me()
    return pltpu.make_async_remote_copy(src_ref=src_ref, dst_ref=land_ref, send_sem=send_sems.at[w],
                                        recv_sem=recv_sems.at[w], device_id=(x, y, 1 - c), device_id_type=MESH)


def _swap_start(arrs, after, name):
    nw = len(arrs)
    lands = [lax.empty(a.shape, a.dtype) for a in arrs]

    def body(*refs):
        srcs, zones = refs[:nw], refs[nw:2 * nw]
        sends, recvs = refs[2 * nw + 1], refs[2 * nw + 2]
        for w in range(nw):
            _swap_copy(w, srcs[w], zones[w], sends, recvs).start()
        refs[-1][...] = jnp.zeros_like(refs[-1])

    sem = pltpu.SemaphoreType.DMA((nw,))
    outs = pl.pallas_call(
        body, name=name,
        out_shape=tuple([sem, sem] + [pltpu.HBM(a.shape, a.dtype) for a in list(arrs) + lands]
                        + [jax.ShapeDtypeStruct((8, 128), F32)]),
        in_specs=[_HBM] * (2 * nw) + [pl.BlockSpec(memory_space=pl.ANY)],
        out_specs=tuple([_SEM, _SEM] + [_HBM] * (2 * nw) + [pl.BlockSpec(memory_space=pltpu.VMEM)]),
        input_output_aliases={i: 2 + i for i in range(2 * nw)},
        compiler_params=pltpu.CompilerParams(has_side_effects=_EFFECT),
    )(*([pltpu.with_memory_space_constraint(a, pltpu.HBM) for a in list(arrs) + lands] + [after]))
    return (outs[0], outs[1], outs[2:2 + nw], outs[2 + nw:2 + 2 * nw]), outs[-1]


def _swap_wait(state, after, name):
    send_sems, recv_sems, arrs, lands = state
    nw = len(arrs)

    def body(*refs):
        srcs, zones = refs[:nw], refs[nw:2 * nw]
        sends, recvs = refs[2 * nw], refs[2 * nw + 1]
        for w in range(nw):
            cp = _swap_copy(w, srcs[w], zones[w], sends, recvs)
            cp.wait_send()
            cp.wait_recv()

    outs = pl.pallas_call(
        body, name=name, out_shape=tuple(pltpu.HBM(a.shape, a.dtype) for a in list(arrs) + list(lands)),
        in_specs=[_HBM] * (2 * nw) + [_SEM, _SEM, pl.BlockSpec(memory_space=pl.ANY)],
        out_specs=tuple([_HBM] * (2 * nw)),
        input_output_aliases={i: i for i in range(2 * nw)},
        compiler_params=pltpu.CompilerParams(has_side_effects=_EFFECT),
    )(*arrs, *lands, send_sems, recv_sems, after)
    return list(outs[:nw]), list(outs[nw:])


def _scatter_start(grad, axis, name):
    shp = list(grad.shape)
    shp[axis] //= N_CHIP
    land = lax.empty((N_CHIP,) + tuple(shp), grad.dtype)

    def body(grad_ref, land_ref, sends, recvs, grad_thru, land_thru, token):
        for k in range(1, N_CHIP):
            _scatter_copy(k, grad_ref, land_ref, sends, recvs, axis).start()
        _scatter_own(grad_ref, land_ref, sends, axis).start()
        token[...] = jnp.zeros_like(token)

    outs = pl.pallas_call(
        body, name=name,
        out_shape=(pltpu.SemaphoreType.DMA((_N_PEER + 1,)), pltpu.SemaphoreType.DMA((_N_PEER,)),
                   pltpu.HBM(grad.shape, grad.dtype), pltpu.HBM(land.shape, land.dtype),
                   jax.ShapeDtypeStruct((8, 128), F32)),
        in_specs=[_HBM, _HBM], out_specs=(_SEM, _SEM, _HBM, _HBM, pl.BlockSpec(memory_space=pltpu.VMEM)),
        input_output_aliases={0: 2, 1: 3},
        compiler_params=pltpu.CompilerParams(has_side_effects=_EFFECT),
    )(pltpu.with_memory_space_constraint(grad, pltpu.HBM), pltpu.with_memory_space_constraint(land, pltpu.HBM))
    return outs[:4], outs[4]


def _scatter_wait(state, axis, after, name):
    send_sems, recv_sems, grad, land = state

    def body(grad_ref, land_ref, sends, recvs, after_ref, grad_dead, got_ref):
        for k in range(1, N_CHIP):
            cp = _scatter_copy(k, grad_ref, land_ref, sends, recvs, axis)
            cp.wait_send()
            cp.wait_recv()
        _scatter_own(grad_ref, land_ref, sends, axis).wait()

    return pl.pallas_call(
        body, name=name, out_shape=(pltpu.HBM(grad.shape, grad.dtype), pltpu.HBM(land.shape, land.dtype)),
        in_specs=[_HBM, _HBM, _SEM, _SEM, pl.BlockSpec(memory_space=pl.ANY)], out_specs=(_HBM, _HBM),
        input_output_aliases={0: 0, 1: 1},
        compiler_params=pltpu.CompilerParams(has_side_effects=_EFFECT),
    )(grad, land, send_sems, recv_sems, after)[1]


_C1 = 1.0 - B1 ** STEP
_C2 = 1.0 - B2 ** STEP


def _adam_math(w, g, m, v):
    m = B1 * m + (1.0 - B1) * g
    v = B2 * v + (1.0 - B2) * (g * g)
    delta = -LR * ((m / _C1) / (jnp.sqrt(v / _C2) + AEPS) + WD * w)
    return delta, m, v


def _adamw(w, m, v, groups, name):
    R, C = w.shape
    tr = R if R <= 256 else (128 if R % 128 == 0 else 176)
    assert R % tr == 0, (name, R)
    gparts = [p for grp in groups for p in grp]
    sizes = [len(grp) for grp in groups]
    ng = len(gparts)

    def body(*refs):
        w_ref, m_ref, v_ref = refs[:3]
        g_refs = list(refs[3:3 + ng])
        g_out, d_out, m_out, v_out = refs[3 + ng:]
        g = None
        for size in sizes:
            s = None
            for r in [g_refs.pop(0) for _ in range(size)]:
                terms = [r[q] for q in range(r.shape[0])] if len(r.shape) == 3 else [r[...]]
                for t in terms:
                    s = t.astype(F32) if s is None else s + t.astype(F32)
            g = s if g is None else g + s
        delta, mn, vn = _adam_math(w_ref[...], g, m_ref[...], v_ref[...])
        g_out[...] = g
        d_out[...] = delta
        m_out[...] = mn
        v_out[...] = vn

    blk = pl.BlockSpec((tr, C), lambda i: (i, 0))
    g_specs = [blk if p.ndim == 2 else pl.BlockSpec((p.shape[0], tr, C), lambda i: (0, i, 0)) for p in gparts]
    sds = jax.ShapeDtypeStruct((R, C), F32)
    return pl.pallas_call(
        body, name=name, out_shape=(sds, sds, sds, sds), grid=(R // tr,),
        in_specs=[blk, blk, blk] + g_specs, out_specs=(blk, blk, blk, blk),
        compiler_params=_cp(("parallel",)))(w, m, v, *gparts)


def _adamw_small(stack, names, wts, mom, var, sum_only, name):
    items, row = [], 0
    for n in names:
        shape = (KW, CW) if n == "conv_w" else wts[n].shape
        size = int(np.prod(shape))
        vec = len(shape) == 2 and shape[0] == 1 and n not in sum_only
        view = shape if vec else (-(-size // _PACK_COLS), _PACK_COLS)
        items.append((n, row, size, vec, view))
        row += _pack_rows(shape)
    upd = [it for it in items if it[0] not in sum_only]
    operands = [stack]
    for n, _, _, _, view in upd:
        operands += [d[n].reshape(view) for d in (wts, mom, var)]

    def grad(stack_ref, r0, nrows, ncols):
        g = stack_ref[0, r0:r0 + nrows, 0:ncols]
        for q in range(1, N_DEV):
            g = g + stack_ref[q, r0:r0 + nrows, 0:ncols]
        return g

    def body(*refs):
        stack_ref, ins, outs = refs[0], refs[1:1 + 3 * len(upd)], refs[1 + 3 * len(upd):]
        o = 0
        for idx, (n, r0, size, vec, view) in enumerate(upd):
            w_ref, m_ref, v_ref = ins[3 * idx:3 * idx + 3]
            g_out, d_out, m_out, v_out = outs[o:o + 4]
            o += 4
            if vec:
                pieces = [(j, j * _PACK_COLS, min((j + 1) * _PACK_COLS, size)) for j in range(-(-size // _PACK_COLS))]
            else:
                pieces = [(None, 0, _PACK_COLS)]
            for j, lo, hi in pieces:
                if vec:
                    g = grad(stack_ref, r0 + j, 1, hi - lo)
                    sl = (slice(None), slice(lo, hi))
                else:
                    g = grad(stack_ref, r0, view[0], _PACK_COLS)
                    sl = (slice(None), slice(None))
                delta, mn, vn = _adam_math(w_ref[sl], g, m_ref[sl], v_ref[sl])
                g_out[sl] = g
                d_out[sl] = delta
                m_out[sl] = mn
                v_out[sl] = vn
        for n, r0, size, vec, view in items:
            if n in sum_only:
                outs[o][...] = grad(stack_ref, r0, view[0], _PACK_COLS)
                o += 1

    out_shape = []
    for n, _, _, _, view in upd:
        out_shape += [jax.ShapeDtypeStruct(view, F32)] * 4
    out_shape += [jax.ShapeDtypeStruct(view, F32) for n, _, _, _, view in items if n in sum_only]
    vm = pl.BlockSpec(memory_space=pltpu.VMEM)
    res = pl.pallas_call(
        body, name=name, out_shape=tuple(out_shape), in_specs=[vm] * len(operands),
        out_specs=tuple([vm] * len(out_shape)),
        compiler_params=pltpu.CompilerParams(vmem_limit_bytes=VMEM_LIMIT))(*operands)
    updated = {n: tuple(r.reshape(wts[n].shape) for r in res[4 * i:4 * i + 4]) for i, (n, *_) in enumerate(upd)}
    sums = dict(zip([it[0] for it in items if it[0] in sum_only], res[4 * len(upd):]))
    return updated, sums


def _mod_shard(c_all, w_ada, b_ada_cols):
    n = w_ada.shape[1]
    tn = 512

    def body(c_ref, w_ref, b_ref, o_ref):
        cv = c_ref[...]
        ca = (cv * _sig(cv)).astype(BF16)
        o_ref[...] = jnp.dot(ca, w_ref[...].astype(BF16), preferred_element_type=F32) + b_ref[...]

    return pl.pallas_call(
        body, name="mod_shard", out_shape=jax.ShapeDtypeStruct((N_DEV, n), F32), grid=(n // tn,),
        in_specs=[_full((N_DEV, D_MODEL)), pl.BlockSpec((D_MODEL, tn), lambda j: (0, j)),
                  pl.BlockSpec((1, tn), lambda j: (0, j))],
        out_specs=pl.BlockSpec((N_DEV, tn), lambda j: (0, j)),
        compiler_params=_cp(("parallel",)))(c_all, w_ada, b_ada_cols)


def _ada_grad(c_all, dmod_cols, after):
    n = dmod_cols.shape[1]
    tn = 512

    def body(c_ref, d_ref, after_ref, o_ref):
        cv = c_ref[...]
        ca = cv * _sig(cv)
        o_ref[...] = lax.dot_general(ca, d_ref[...], (((0,), (0,)), ((), ())),
                                     preferred_element_type=F32, precision=lax.Precision.HIGHEST)

    return pl.pallas_call(
        body, name="ada_grad", out_shape=jax.ShapeDtypeStruct((D_MODEL, n), F32), grid=(n // tn,),
        in_specs=[_full((N_DEV, D_MODEL)), pl.BlockSpec((N_DEV, tn), lambda j: (0, j)),
                  pl.BlockSpec(memory_space=pl.ANY)],
        out_specs=pl.BlockSpec((D_MODEL, tn), lambda j: (0, j)),
        compiler_params=_cp(("parallel",)))(c_all, dmod_cols, after)


def _ssm_tables(W):
    e_re, e_im, bb_re, bb_im = _ssm_prep(W["ssm_a_re"], W["ssm_a_im"], W["ssm_b_re"], W["ssm_b_im"], W["ssm_log_dt"])
    bb, cm = _block_diag_mats(bb_re, bb_im, W["ssm_c_re"], W["ssm_c_im"])
    bb16, cm16 = bb.astype(BF16), cm.astype(BF16)
    return (bb16, cm16, jnp.swapaxes(bb16, 1, 2), jnp.swapaxes(cm16, 1, 2),
            _scan_tables(e_re, e_im, False), _scan_tables(e_re, e_im, True))


def _device_step(x, mod, W, tables, tgt, getw, put, early):
    sh1, sc1, g1, sh2, sc2, g2 = [mod[:, i * D_MODEL:(i + 1) * D_MODEL] for i in range(6)]
    bb16, cm16, bbt16, cmt16, tab_f, tab_b = tables

    w_in = getw("w_in", [mod, *tables])
    h1, z = _in_proj(x, W["norm1_g"], sc1, sh1, w_in)
    yc, scv = _conv_fwd(z, W["conv_w"], W["conv_b"], W["conv_ln_g"], W["conv_ln_b"])
    xs, ys, yg = _ssm_fwd(z, bb16, cm16, W["ssm_d"], tab_f)
    w_cp, w_glu, w_out = getw("conv_proj", scv), getw("ssm_glu", yg), getw("w_out", yg)
    y_conv, zz, merged, o, x2, h2 = _mix_fwd(scv, yg, z, x, w_cp, w_glu, w_out, g1, W["norm2_g"], sc2, sh2)
    w_fi = getw("w_ffn_in", h2)
    f, act = _ffn_in_act(h2, w_fi)
    w_fo = getw("w_ffn_out", act)
    dx3, do2, loss8, dfg8, dg2_8 = _ffn_out_final(x2, act, w_fo, g2, W["final_g"], tgt)

    sm = {}
    tok = put("w_ffn_out", _matmul(act, do2, "tn", 1408, 1024, 2048, BF16, "mm_g_ffn_out"))
    df = _ffn_bwd(do2, w_fo, f, tok)
    tok = put("w_ffn_in", _matmul(h2, df, "tn", 1024, 1408, 2048, BF16, "mm_g_ffn_in"))
    dx2, do, dsh2, dsc2, dn2, dg1_8 = _normmod_bwd(df, w_fi, x2, dx3, W["norm2_g"], sc2, g1, o, tok, "d_h2_normmod2_bwd")
    tok = put("w_out", _matmul(merged, do, "tn", 1024, 1024, 4096, BF16, "mm_g_w_out"))
    dyconv, dgl, dzz = _mix_bwd(do, w_out, z, zz, y_conv, tok)
    tok = put("ssm_glu", _matmul(yg, dzz, "tn", 512, 1024, 4096, BF16, "mm_g_ssm_glu"))
    tok = put("conv_proj", _matmul(scv, dyconv, "tn", 512, 1024, 4096, BF16, "mm_g_conv_proj", after=tok))
    du, de16, dd8, dc_full, dbb_full = _ssm_bwd(dzz, w_glu, ys, z, xs, cmt16, bbt16, W["ssm_d"], tab_b, tok)
    dyc, dlg8, dlb8, dcb8 = _conv_bwd_ln(dyconv, w_cp, yc, W["conv_ln_g"], W["conv_ln_b"])
    dz_conv, dcw = _conv_bwd(dyc, z, W["conv_w"])

    s8 = lambda a: jnp.sum(a, axis=0, keepdims=True)
    de = de16.reshape(2, 8, NST).sum(1)
    de_re, de_im = de[0].reshape(G, P), de[1].reshape(G, P)
    dc_re = _diag_blocks(dc_full, False)
    dc_im = -_diag_blocks(dc_full, True)
    dbb_re = jnp.swapaxes(_diag_blocks(dbb_full, False), 1, 2)
    dbb_im = jnp.swapaxes(_diag_blocks(dbb_full, True), 1, 2)
    _, vjp = jax.vjp(_ssm_prep, W["ssm_a_re"], W["ssm_a_im"], W["ssm_b_re"], W["ssm_b_im"], W["ssm_log_dt"])
    sm["ssm_a_re"], sm["ssm_a_im"], sm["ssm_b_re"], sm["ssm_b_im"], sm["ssm_log_dt"] = vjp((de_re, de_im, dbb_re, dbb_im))
    sm["ssm_c_re"], sm["ssm_c_im"] = dc_re, dc_im
    sm["ssm_d"] = s8(dd8)
    sm["norm2_g"] = s8(dn2)
    sm["conv_b"], sm["conv_ln_g"], sm["conv_ln_b"] = s8(dcb8), s8(dlg8), s8(dlb8)
    sm["conv_w"] = dcw.reshape(KW, 8, CW).sum(1)
    sm["final_g"] = s8(dfg8)
    tok = early(sm)

    dz = [dz_conv, du, dgl]
    tok = put("w_in", _matmul(h1, dz, "tn", 1024, 512, 4096, BF16, "mm_g_w_in", after=tok))
    dx, _, dsh1, dsc1, dn1, _ = _normmod_bwd(dz, w_in, x, dx2, W["norm1_g"], sc1, g1, o, tok, "d_h1_normmod1_bwd")
    dmod = jnp.concatenate([s8(dsh1), s8(dsc1), s8(dg1_8), s8(dsh2), s8(dsc2), s8(dg2_8)], axis=1)
    return loss8, dx, s8(dn1), dmod


_BIG = ("w_in", "conv_proj", "ssm_glu", "w_out", "w_ffn_in", "w_ffn_out")
_BIG_AXIS = {"w_in": 1, "conv_proj": 1, "ssm_glu": 1, "w_out": 0, "w_ffn_in": 1, "w_ffn_out": 0}
_EARLY = ("conv_w", "conv_b", "conv_ln_g", "conv_ln_b", "ssm_a_re", "ssm_a_im", "ssm_b_re", "ssm_b_im", "ssm_c_re",
          "ssm_c_im", "ssm_d", "ssm_log_dt", "norm2_g", "final_g")
_LATE = ("norm1_g", "b_ada")
_ORDER = ("w_ada", "b_ada", "norm1_g", "w_in", "conv_w", "conv_b", "conv_ln_g", "conv_ln_b", "conv_proj",
          "ssm_a_re", "ssm_a_im", "ssm_b_re", "ssm_b_im", "ssm_c_re", "ssm_c_im", "ssm_d", "ssm_log_dt", "ssm_glu",
          "w_out", "norm2_g", "w_ffn_in", "w_ffn_out", "final_g")
_PACK_COLS = 1024


def _pack_rows(shape):
    return -(-int(np.prod(shape)) // (8 * _PACK_COLS)) * 8


def _pack(arrs):
    parts = []
    for a in arrs:
        flat = a.reshape(-1)
        n = _pack_rows(a.shape)
        parts.append(jnp.pad(flat, (0, n * _PACK_COLS - flat.shape[0])).reshape(n, _PACK_COLS))
    return jnp.concatenate(parts, 0)


def kernel(x, c, w_ada, b_ada, norm1_g, w_in, conv_w, conv_b, conv_ln_g, conv_ln_b, conv_proj, ssm_a_re, ssm_a_im, ssm_b_re, ssm_b_im, ssm_c_re, ssm_c_im, ssm_d, ssm_log_dt, ssm_glu, w_out, norm2_g, w_ffn_in, w_ffn_out, final_g, loss_target, m_w_ada, m_b_ada, m_norm1_g, m_w_in, m_conv_w, m_conv_b, m_conv_ln_g, m_conv_ln_b, m_conv_proj, m_ssm_a_re, m_ssm_a_im, m_ssm_b_re, m_ssm_b_im, m_ssm_c_re, m_ssm_c_im, m_ssm_d, m_ssm_log_dt, m_ssm_glu, m_w_out, m_norm2_g, m_w_ffn_in, m_w_ffn_out, m_final_g, v_w_ada, v_b_ada, v_norm1_g, v_w_in, v_conv_w, v_conv_b, v_conv_ln_g, v_conv_ln_b, v_conv_proj, v_ssm_a_re, v_ssm_a_im, v_ssm_b_re, v_ssm_b_im, v_ssm_c_re, v_ssm_c_im, v_ssm_d, v_ssm_log_dt, v_ssm_glu, v_w_out, v_norm2_g, v_w_ffn_in, v_w_ffn_out, v_final_g):
    given = dict(locals())
    mx, my, mc = _me()
    chip = 2 * mx + my
    dev = 4 * mx + 2 * my + mc
    def canon(a):
        return a.reshape(1, -1) if a.ndim <= 2 else a[0]

    wts = {n: canon(given[n]) for n in _ORDER}
    mom = {n: canon(given["m_" + n]) for n in _ORDER}
    var = {n: canon(given["v_" + n]) for n in _ORDER}

    W = {n: wts[n] for n in _ORDER if n not in _BIG}
    rest = [n for n in _BIG if n != "w_in"]
    rest_shards = [wts[n].astype(BF16) for n in rest]
    state_in, token = _half_gather_start(wts["w_in"].astype(BF16), c, "gather_start_w_in")
    W["ssm_log_dt"] = wts["ssm_log_dt"] + token[0:1, 0:1]
    W["ssm_c_re"] = wts["ssm_c_re"] + token[0, 0]
    tables = _ssm_tables(W)

    c_all = _allgather8(jnp.broadcast_to(c, (8, D_MODEL)), "gather_c", after=[*tables, *rest_shards])[:, 0, :]
    n_ada = wts["w_ada"].shape[1]
    b_cols = lax.dynamic_slice(wts["b_ada"], (0, chip * n_ada), (1, n_ada))
    mod_cols = _mod_shard(c_all, wts["w_ada"], b_cols)
    halves = _half_gather_wait(state_in, [mod_cols], "gather_wait_w_in")
    state_in, token = _half_swap_start(halves, "gather_swap_start_w_in")
    mods = _allgather8(mod_cols, "gather_mod", after=[token])
    mod = jnp.concatenate([lax.dynamic_index_in_dim(mods[2 * q], dev, 0, keepdims=True) for q in range(N_CHIP)], axis=1)
    conv_w_full = _allgather8(jnp.pad(wts["conv_w"], ((0, 1), (0, 0))), "gather_conv_w", after=[token])
    W["conv_w"] = jnp.concatenate([conv_w_full[2 * q, :KW] for q in range(N_CHIP)], axis=1)
    w_in_full = _half_swap_wait(state_in, mod + W["conv_w"][0:1, 0:1], "gather_swap_wait_w_in")
    gstate, token = _gather_start(rest_shards, [_BIG_AXIS[n] for n in rest], w_in_full, "gather_start_rest")
    gstate = dict(zip(rest, gstate))
    mod = mod + token[0:1, 0:1]

    def getw(n, after):
        if n == "w_in":
            return w_in_full
        return _gather_wait(gstate[n], _BIG_AXIS[n], after, "gather_wait_" + n)

    sstate, estate = {}, []

    def put(n, g):
        sstate[n], tok = _scatter_start(g, _BIG_AXIS[n], "scatter_start_" + n)
        return tok

    first5 = [n for n in _BIG if n != "w_in"]

    def early(sm):
        state, tok = _all8_start(_pack([sm[n] for n in _EARLY]), "small_start")
        estate.append(state)
        held = [_scatter_wait(sstate[n], _BIG_AXIS[n], tok, "scatter_wait_" + n) for n in first5]
        state, tok = _swap_start(held, tok, "swap_start")
        estate.append(state)
        return tok

    loss8, dx, dn1, dmod = _device_step(x[0], mod, W, tables, loss_target[0], getw, put, early)

    held5, sib5 = _swap_wait(estate[1], dx, "swap_wait")
    outs = {}
    for i, n in enumerate(first5):
        outs[n] = _adamw(wts[n], mom[n], var[n], [[held5[i]], [sib5[i]]], "adamw_" + n)
    allp = _all8_wait(estate[0], dx, "small_wait")
    upd, sums = _adamw_small(allp, _EARLY, wts, mom, var, ("conv_w",), "adamw_small")
    outs.update(upd)

    late = _allgather8(_pack([dn1, dmod, loss8]), "gather_late", after=[outs[n][1] for n in first5])
    n_late = _pack_rows((D_MODEL,)) + _pack_rows((6 * D_MODEL,))
    loss = jnp.sum(late[:, n_late:, :])
    late = late[:, :n_late, :]
    held_in = _scatter_wait(sstate["w_in"], _BIG_AXIS["w_in"], late, "scatter_wait_w_in")
    state_in, tok = _swap_start([held_in], late, "swap_start_w_in")

    r1 = _pack_rows((D_MODEL,))
    dmod_all = late[:, r1:, :].reshape(N_DEV, -1)[:, :6 * D_MODEL]
    dmod_cols = lax.dynamic_slice(dmod_all, (0, chip * n_ada), (N_DEV, n_ada))
    g_ada = _ada_grad(c_all, dmod_cols, tok)
    outs["w_ada"] = _adamw(wts["w_ada"], mom["w_ada"], var["w_ada"], [[g_ada]], "adamw_w_ada")
    upd, _ = _adamw_small(late, _LATE, wts, mom, var, (), "adamw_late")
    outs.update(upd)
    held_in, sib_in = _swap_wait(state_in, outs["w_ada"][1], "swap_wait_w_in")
    outs["w_in"] = _adamw(wts["w_in"], mom["w_in"], var["w_in"], [held_in, sib_in], "adamw_w_in")
    g_cw_full = sums["conv_w"].reshape(-1)[:KW * CW].reshape(KW, CW)
    g_cw = lax.dynamic_slice(g_cw_full, (0, chip * (CW // N_CHIP)), (KW, CW // N_CHIP))
    pad = lambda a: jnp.pad(a, ((0, 1), (0, 0)))
    r_cw = _adamw(pad(wts["conv_w"]), pad(mom["conv_w"]), pad(var["conv_w"]), [[pad(g_cw)]], "adamw_conv_w")
    outs["conv_w"] = tuple(r[:KW] for r in r_cw)

    def shaped(n, a):
        return a.reshape(given[n].shape)

    result = [loss, dx[None]]
    for q in range(4):
        result += [shaped(n, outs[n][q]) for n in _ORDER]
    return tuple(result)
```

```python
import math

import jax
import jax.numpy as jnp
import numpy as np
from jax import lax
from jax.experimental import pallas as pl
from jax.experimental.pallas import tpu as pltpu

F32 = jnp.float32
BF16 = jnp.bfloat16
EPS = 1e-6
D_MODEL = 1024
CW = 512
KW = 31
HALO = 32
G, P, H = 32, 64, 16
NST = G * P
FH = 2816
N_DEV = 8
N_CHIP = 4
VMEM_LIMIT = 56 * 1024 * 1024
LR, B1, B2, AEPS, WD, STEP = 0.001, 0.9, 0.999, 1e-08, 0.01, 10
MESH = pl.DeviceIdType.MESH


def _cp(sem=None):
    return pltpu.CompilerParams(dimension_semantics=sem, vmem_limit_bytes=VMEM_LIMIT)


def _sig(x):
    return jax.nn.sigmoid(x)


def _full(shape):
    return pl.BlockSpec(shape, lambda *_: (0,) * len(shape))


def _resident(shape):
    return pl.BlockSpec(shape, lambda *_: (0,) * len(shape), pipeline_mode=pl.Buffered(1))


def _colsum8(v):
    t, c = v.shape
    return jnp.sum(v.reshape(t // 8, 8, c), axis=0)


def _matmul(a, b, mode, tm, tn, tk, out_dtype, name, after=None, n_outer=False, m_cols=None):
    m0 = 0
    b_parts = list(b) if isinstance(b, (list, tuple)) else [b]
    if mode == "nn":
        (M, K), N = a.shape, b.shape[1]
    elif mode == "nt":
        (M, K), N = a.shape, b.shape[0]
    else:
        (K, M), N = a.shape, sum(p.shape[1] for p in b_parts)
        if m_cols is not None:
            m0, M = m_cols
    tm, tn, tk = min(tm, M), min(tn, N), min(tk, K)
    assert M % tm == 0 and N % tn == 0 and K % tk == 0 and m0 % tm == 0, (name, M, N, K, tm, tn, tk)
    assert len(b_parts) == 1 or (mode == "tn" and all(p.shape[1] % tn == 0 for p in b_parts)), name
    nk = K // tk
    mb = m0 // tm
    counts = [p.shape[1] // tn for p in b_parts] if mode == "tn" else [N // tn]
    starts = [sum(counts[:p]) for p in range(len(counts))]

    def ij(fn):
        return (lambda j, i, k: fn(i, j, k)) if n_outer else fn

    if mode == "nn":
        a_spec = pl.BlockSpec((tm, tk), ij(lambda i, j, k: (i, k)))
        b_spec = pl.BlockSpec((tk, tn), ij(lambda i, j, k: (k, j)))
        dims = (((1,), (0,)), ((), ()))
    elif mode == "nt":
        a_spec = pl.BlockSpec((tm, tk), ij(lambda i, j, k: (i, k)))
        b_spec = pl.BlockSpec((tn, tk), ij(lambda i, j, k: (j, k)))
        dims = (((1,), (1,)), ((), ()))
    else:
        a_spec = pl.BlockSpec((tk, tm), ij(lambda i, j, k: (k, i + mb)))
        dims = (((0,), (0,)), ((), ()))
    if mode == "tn":
        b_specs = [pl.BlockSpec((tk, tn), ij(lambda i, j, k, s=s, n=n: (k, jnp.clip(j - s, 0, n - 1))))
                   for s, n in zip(starts, counts)]
    else:
        b_specs = [b_spec]
    nb = len(b_parts)

    def body(a_ref, *rest):
        b_refs = rest[:nb]
        o_ref, acc_ref = rest[-2:]
        j = pl.program_id(0 if n_outer else 1)
        k = pl.program_id(2)

        def compute(b_ref):
            part = lax.dot_general(a_ref[...].astype(BF16), b_ref[...].astype(BF16), dims,
                                   preferred_element_type=F32)
            if nk == 1:
                o_ref[...] = part.astype(out_dtype)
            else:
                @pl.when(k == 0)
                def _():
                    acc_ref[...] = part

                @pl.when(k > 0)
                def _():
                    acc_ref[...] += part

                @pl.when(k == nk - 1)
                def _():
                    o_ref[...] = acc_ref[...].astype(out_dtype)

        if nb == 1:
            compute(b_refs[0])
        else:
            for p in range(nb):
                pl.when(jnp.logical_and(j >= starts[p], j < starts[p] + counts[p]))(
                    lambda b_ref=b_refs[p]: compute(b_ref))

    return pl.pallas_call(
        body, name=name,
        out_shape=jax.ShapeDtypeStruct((M, N), out_dtype),
        grid=(N // tn, M // tm, nk) if n_outer else (M // tm, N // tn, nk),
        in_specs=[a_spec] + b_specs + ([] if after is None else [pl.BlockSpec(memory_space=pl.ANY)]),
        out_specs=pl.BlockSpec((tm, tn), ij(lambda i, j, k: (i, j))),
        scratch_shapes=[pltpu.VMEM((tm, tn) if nk > 1 else (8, 128), F32)],
        compiler_params=_cp(("parallel", "parallel", "arbitrary")),
    )(*([a] + b_parts + ([] if after is None else [after])))


def _row_tile(S):
    return min(512, S)


def _in_proj(x, g, sc, sh, w_in):
    S, D = x.shape
    N = w_in.shape[1]
    tm = min(512, S)

    def body(x_ref, g_ref, sc_ref, sh_ref, w_ref, h_ref, z_ref):
        xv = x_ref[...]
        r = lax.rsqrt(jnp.mean(xv * xv, axis=-1, keepdims=True) + EPS)
        h = (xv * r * (g_ref[...] * (1.0 + sc_ref[...])) + sh_ref[...]).astype(BF16)
        h_ref[...] = h
        z_ref[...] = jnp.dot(h, w_ref[...], preferred_element_type=F32).astype(BF16)

    row = pl.BlockSpec((tm, D), lambda i: (i, 0))
    par = _full((1, D))
    return pl.pallas_call(
        body, name="in_proj",
        out_shape=(jax.ShapeDtypeStruct((S, D), BF16), jax.ShapeDtypeStruct((S, N), BF16)), grid=(S // tm,),
        in_specs=[row, par, par, par, _resident((D, N))], out_specs=(row, pl.BlockSpec((tm, N), lambda i: (i, 0))),
        compiler_params=_cp(("parallel",)))(x, g, sc, sh, w_in)


def _fill_shifted(buf_ref, sh_ref):
    n = buf_ref.shape[0] - 8
    for s in range(1, 8):
        sh_ref[s, 0:n, :] = buf_ref[s:s + n, :]


def _window(buf_ref, sh_ref, off, n):
    s = off % 8
    return buf_ref[off:off + n, :] if s == 0 else sh_ref[s, off - s:off - s + n, :]


def _conv_fwd(z, conv_w, conv_b, ln_g, ln_b):
    S = z.shape[0]
    tm = min(128, S)
    sub = 32
    hb = tm // HALO

    def body(a_ref, g_ref, ha_ref, hg_ref, w_ref, b_ref, lg_ref, lb_ref, yc_ref, s_ref, ug_ref, sh_ref):
        i = pl.program_id(0)
        halo = ha_ref[...].astype(F32) * _sig(hg_ref[...].astype(F32))
        ug_ref[0:HALO, :] = jnp.where(i == 0, 0.0, halo)
        ug_ref[HALO:, :] = a_ref[...].astype(F32) * _sig(g_ref[...].astype(F32))
        _fill_shifted(ug_ref, sh_ref)
        for rb in range(tm // sub):
            acc = jnp.zeros((sub, CW), F32) + b_ref[...]
            for k in range(KW):
                off = rb * sub + HALO - (KW - 1) + k
                acc = acc + w_ref[k:k + 1, :] * _window(ug_ref, sh_ref, off, sub)
            yc_ref[rb * sub:(rb + 1) * sub, :] = acc
            mu = jnp.mean(acc, axis=-1, keepdims=True)
            cen = acc - mu
            rstd = lax.rsqrt(jnp.mean(cen * cen, axis=-1, keepdims=True) + EPS)
            ln = cen * rstd * lg_ref[...] + lb_ref[...]
            s_ref[rb * sub:(rb + 1) * sub, :] = (ln * _sig(ln)).astype(BF16)

    prev = lambda i: (jnp.maximum(i * hb - 1, 0), 0)
    return pl.pallas_call(
        body, name="conv_fwd",
        out_shape=(jax.ShapeDtypeStruct((S, CW), F32), jax.ShapeDtypeStruct((S, CW), BF16)),
        grid=(S // tm,),
        in_specs=[pl.BlockSpec((tm, CW), lambda i: (i, 0)), pl.BlockSpec((tm, CW), lambda i: (i, 1)),
                  pl.BlockSpec((HALO, CW), prev), pl.BlockSpec((HALO, CW), lambda i: (jnp.maximum(i * hb - 1, 0), 1)),
                  _full((KW, CW)), _full((1, CW)), _full((1, CW)), _full((1, CW))],
        out_specs=(pl.BlockSpec((tm, CW), lambda i: (i, 0)), pl.BlockSpec((tm, CW), lambda i: (i, 0))),
        scratch_shapes=[pltpu.VMEM((tm + HALO, CW), F32), pltpu.VMEM((8, tm + HALO, CW), F32)],
        compiler_params=_cp(("parallel",)))(z, z, z, z, conv_w, conv_b, ln_g, ln_b)


def _conv_bwd_ln(dyconv, w_cp, yc, ln_g, ln_b):
    S = yc.shape[0]
    tm = _row_tile(S)

    def body(dy_ref, w_ref, yc_ref, lg_ref, lb_ref, dyc_ref, dlg_ref, dlb_ref, dcb_ref):
        i = pl.program_id(0)
        dsc = lax.dot_general(dy_ref[...], w_ref[...], (((1,), (1,)), ((), ())), preferred_element_type=F32)
        yc_v = yc_ref[...]
        mu = jnp.mean(yc_v, axis=-1, keepdims=True)
        cen = yc_v - mu
        rstd = lax.rsqrt(jnp.mean(cen * cen, axis=-1, keepdims=True) + EPS)
        yn = cen * rstd
        ln = yn * lg_ref[...] + lb_ref[...]
        sl = _sig(ln)
        dln = dsc * (sl * (1.0 + ln * (1.0 - sl)))
        dyn = dln * lg_ref[...]
        dyc = rstd * (dyn - jnp.mean(dyn, axis=-1, keepdims=True)
                      - yn * jnp.mean(dyn * yn, axis=-1, keepdims=True))
        dyc_ref[...] = dyc

        @pl.when(i == 0)
        def _():
            dlg_ref[...] = jnp.zeros_like(dlg_ref)
            dlb_ref[...] = jnp.zeros_like(dlb_ref)
            dcb_ref[...] = jnp.zeros_like(dcb_ref)

        dlg_ref[...] += _colsum8(dln * yn)
        dlb_ref[...] += _colsum8(dln)
        dcb_ref[...] += _colsum8(dyc)

    row = pl.BlockSpec((tm, CW), lambda i: (i, 0))
    acc = jax.ShapeDtypeStruct((8, CW), F32)
    return pl.pallas_call(
        body, name="conv_bwd_ln",
        out_shape=(jax.ShapeDtypeStruct((S, CW), F32), acc, acc, acc), grid=(S // tm,),
        in_specs=[pl.BlockSpec((tm, D_MODEL), lambda i: (i, 0)), _full((CW, D_MODEL)), row, _full((1, CW)),
                  _full((1, CW))],
        out_specs=(row, _full((8, CW)), _full((8, CW)), _full((8, CW))),
        compiler_params=_cp(("arbitrary",)))(dyconv, w_cp, yc, ln_g, ln_b)


def _conv_bwd(dyc, z, conv_w):
    S = z.shape[0]
    tm = min(128, S)
    sub = 32
    hb = tm // HALO
    nt = S // tm

    def body(d_ref, dn_ref, a_ref, g_ref, ha_ref, hg_ref, w_ref, dz_ref, dw_ref, ug_ref, dy_ref, ugs_ref, dys_ref):
        i = pl.program_id(0)
        halo = ha_ref[...].astype(F32) * _sig(hg_ref[...].astype(F32))
        ug_ref[0:HALO, :] = jnp.where(i == 0, 0.0, halo)
        a = a_ref[...].astype(F32)
        sg = _sig(g_ref[...].astype(F32))
        ug_ref[HALO:, :] = a * sg
        dy_ref[0:tm, :] = d_ref[...]
        dy_ref[tm:, :] = jnp.where(i == nt - 1, 0.0, dn_ref[...])
        _fill_shifted(ug_ref, ugs_ref)
        _fill_shifted(dy_ref, dys_ref)

        @pl.when(i == 0)
        def _():
            dw_ref[...] = jnp.zeros_like(dw_ref)

        for rb in range(tm // sub):
            r0 = rb * sub
            acc = jnp.zeros((sub, CW), F32)
            dyc_b = dy_ref[r0:r0 + sub, :]
            for k in range(KW):
                up = r0 + (KW - 1) - k
                acc = acc + w_ref[k:k + 1, :] * _window(dy_ref, dys_ref, up, sub)
                off = r0 + HALO - (KW - 1) + k
                dw_ref[k * 8:(k + 1) * 8, :] += _colsum8(dyc_b * _window(ug_ref, ugs_ref, off, sub))
            a_b = a[r0:r0 + sub, :]
            sg_b = sg[r0:r0 + sub, :]
            dz_ref[r0:r0 + sub, 0:CW] = (acc * sg_b).astype(BF16)
            dz_ref[r0:r0 + sub, CW:2 * CW] = (acc * a_b * sg_b * (1.0 - sg_b)).astype(BF16)

    return pl.pallas_call(
        body, name="conv_bwd",
        out_shape=(jax.ShapeDtypeStruct((S, 2 * CW), BF16), jax.ShapeDtypeStruct((KW * 8, CW), F32)),
        grid=(nt,),
        in_specs=[pl.BlockSpec((tm, CW), lambda i: (i, 0)),
                  pl.BlockSpec((HALO, CW), lambda i: (jnp.minimum((i + 1) * hb, nt * hb - 1), 0)),
                  pl.BlockSpec((tm, CW), lambda i: (i, 0)), pl.BlockSpec((tm, CW), lambda i: (i, 1)),
                  pl.BlockSpec((HALO, CW), lambda i: (jnp.maximum(i * hb - 1, 0), 0)),
                  pl.BlockSpec((HALO, CW), lambda i: (jnp.maximum(i * hb - 1, 0), 1)),
                  _full((KW, CW))],
        out_specs=(pl.BlockSpec((tm, 2 * CW), lambda i: (i, 0)), _full((KW * 8, CW))),
        scratch_shapes=[pltpu.VMEM((tm + HALO, CW), F32), pltpu.VMEM((tm + HALO, CW), F32),
                        pltpu.VMEM((8, tm + HALO, CW), F32), pltpu.VMEM((8, tm + HALO, CW), F32)],
        compiler_params=_cp(("arbitrary",)))(dyc, dyc, z, z, z, z, conv_w)


_GELU_C = math.sqrt(2.0 / math.pi)


def _gelu(x):
    return 0.5 * x * (1.0 + jnp.tanh(_GELU_C * (x + 0.044715 * x * x * x)))


def _gelu_grad(x):
    t = jnp.tanh(_GELU_C * (x + 0.044715 * x * x * x))
    return 0.5 * (1.0 + t) + 0.5 * x * (1.0 - t * t) * (_GELU_C * (1.0 + 3 * 0.044715 * x * x))


_NCL = 4
_UC = CW // _NCL
_LW = NST // _NCL
_CS = 2 * _LW


def _ssm_fwd(z, bb, cm, d, tab):
    S = z.shape[0]
    tm = min(512, S)

    def body(u_ref, bb_ref, cm_ref, d_ref, t_ref, x_ref, ys_ref, yg_ref, car_ref):
        i = pl.program_id(0)

        @pl.when(i == 0)
        def _():
            car_ref[...] = jnp.zeros_like(car_ref)

        u16 = u_ref[...]
        u = u16.astype(F32)
        for c in range(_NCL):
            lre = pl.ds(c * _CS, _LW)
            lim = pl.ds(c * _CS + _LW, _LW)
            tl = pl.ds(c * _LW, _LW)
            x_ref[:, c * _CS:(c + 1) * _CS] = jnp.dot(u16[:, c * _UC:(c + 1) * _UC], bb_ref[c],
                                                      preferred_element_type=F32)

            def blk(j, car):
                cr, ci = car
                rows = pl.ds(pl.multiple_of(j * 8, 8), 8)
                r = x_ref[rows, lre]
                im = x_ref[rows, lim]
                for lvl, s in enumerate((1, 2, 4)):
                    mr = t_ref[16 * lvl:16 * lvl + 8, tl]
                    mi = t_ref[16 * lvl + 8:16 * lvl + 16, tl]
                    sr = pltpu.roll(r, s, 0)
                    si = pltpu.roll(im, s, 0)
                    r, im = r + (mr * sr - mi * si), im + (mr * si + mi * sr)
                pr = t_ref[48:56, tl]
                pi_ = t_ref[56:64, tl]
                r, im = r + (pr * cr - pi_ * ci), im + (pr * ci + pi_ * cr)
                x_ref[rows, lre] = r
                x_ref[rows, lim] = im
                return (jnp.broadcast_to(r[7:8, :], (8, _LW)), jnp.broadcast_to(im[7:8, :], (8, _LW)))

            cr, ci = lax.fori_loop(0, tm // 8, blk, (car_ref[:, lre], car_ref[:, lim]))
            car_ref[:, lre] = cr
            car_ref[:, lim] = ci
            cols = slice(c * _UC, (c + 1) * _UC)
            ys = jnp.dot(x_ref[:, c * _CS:(c + 1) * _CS].astype(BF16), cm_ref[c], preferred_element_type=F32)
            ys = ys + d_ref[:, cols] * u[:, cols]
            ys_ref[:, cols] = ys
            yg_ref[:, cols] = _gelu(ys).astype(BF16)

    return pl.pallas_call(
        body, name="ssm_fwd",
        out_shape=(jax.ShapeDtypeStruct((S, 2 * NST), F32), jax.ShapeDtypeStruct((S, CW), F32),
                   jax.ShapeDtypeStruct((S, CW), BF16)),
        grid=(S // tm,),
        in_specs=[pl.BlockSpec((tm, CW), lambda i: (i, 2)), _full((_NCL, _UC, _CS)), _full((_NCL, _CS, _UC)),
                  _full((1, CW)), _full((64, NST))],
        out_specs=(pl.BlockSpec((tm, 2 * NST), lambda i: (i, 0)), pl.BlockSpec((tm, CW), lambda i: (i, 0)),
                   pl.BlockSpec((tm, CW), lambda i: (i, 0))),
        scratch_shapes=[pltpu.VMEM((8, 2 * NST), F32)],
        compiler_params=_cp(("arbitrary",)))(z, bb, cm, d, tab)


def _ssm_bwd(dzz, w_glu, ys, z, xs, cmt, bbt, d, tab, after):
    S = z.shape[0]
    tm = min(512, S)
    nt = S // tm
    tdims = (((0,), (0,)), ((), ()))

    def body(dzz_ref, wglu_ref, ys_ref, u_ref, x_ref, cmt_ref, bbt_ref, d_ref, t_ref, after_ref,
             du_ref, de_ref, dd_ref, dc_hbm, dbb_hbm, car_ref, lam_ref, dc_ref, dbb_ref):
        i = pl.program_id(0)

        @pl.when(i == 0)
        def _():
            car_ref[...] = jnp.zeros_like(car_ref)
            de_ref[...] = jnp.zeros_like(de_ref)
            dd_ref[...] = jnp.zeros_like(dd_ref)
            dc_ref[...] = jnp.zeros_like(dc_ref)
            dbb_ref[...] = jnp.zeros_like(dbb_ref)

        u16 = u_ref[...]
        u = u16.astype(F32)
        dyg = lax.dot_general(dzz_ref[...], wglu_ref[...], (((1,), (1,)), ((), ())), preferred_element_type=F32)
        dys = dyg * _gelu_grad(ys_ref[...])
        dys16 = dys.astype(BF16)
        dd_ref[...] += _colsum8(dys * u)
        row = lax.broadcasted_iota(jnp.int32, (8, _LW), 0)
        for c in range(_NCL):
            lre = pl.ds(c * _CS, _LW)
            lim = pl.ds(c * _CS + _LW, _LW)
            tl = pl.ds(c * _LW, _LW)
            cols = slice(c * _UC, (c + 1) * _UC)
            span = slice(c * _CS, (c + 1) * _CS)
            dc_ref[cols, :] += lax.dot_general(dys16[:, cols], x_ref[:, span].astype(BF16), tdims,
                                               preferred_element_type=F32)
            lam_ref[...] = jnp.dot(dys16[:, cols], cmt_ref[c], preferred_element_type=F32)

            def blk(jj, car):
                cr, ci, ar, ai = car
                j = tm // 8 - 1 - jj
                rows = pl.ds(pl.multiple_of(j * 8, 8), 8)
                r = lam_ref[rows, 0:_LW]
                im = lam_ref[rows, _LW:_CS]
                for lvl, s in enumerate((1, 2, 4)):
                    mr = t_ref[16 * lvl:16 * lvl + 8, tl]
                    mi = t_ref[16 * lvl + 8:16 * lvl + 16, tl]
                    sr = pltpu.roll(r, 8 - s, 0)
                    si = pltpu.roll(im, 8 - s, 0)
                    r, im = r + (mr * sr - mi * si), im + (mr * si + mi * sr)
                pr = t_ref[48:56, tl]
                pi_ = t_ref[56:64, tl]
                r, im = r + (pr * cr - pi_ * ci), im + (pr * ci + pi_ * cr)
                lam_ref[rows, 0:_LW] = r
                lam_ref[rows, _LW:_CS] = im
                nr = jnp.where(row == 7, cr, pltpu.roll(r, 7, 0))
                ni = jnp.where(row == 7, ci, pltpu.roll(im, 7, 0))
                xr = x_ref[rows, lre]
                xi = x_ref[rows, lim]
                ar = ar + (nr * xr + ni * xi)
                ai = ai + (ni * xr - nr * xi)
                return (jnp.broadcast_to(r[0:1, :], (8, _LW)), jnp.broadcast_to(im[0:1, :], (8, _LW)), ar, ai)

            zero = jnp.zeros((8, _LW), F32)
            cr, ci, ar, ai = lax.fori_loop(0, tm // 8, blk, (car_ref[:, lre], car_ref[:, lim], zero, zero))
            car_ref[:, lre] = cr
            car_ref[:, lim] = ci
            de_ref[0:8, tl] += ar
            de_ref[8:16, tl] += ai
            lam16 = lam_ref[...].astype(BF16)
            dbb_ref[cols, :] += lax.dot_general(u16[:, cols], lam16, tdims, preferred_element_type=F32)
            du = jnp.dot(lam16, bbt_ref[c], preferred_element_type=F32) + dys[:, cols] * d_ref[:, cols]
            du_ref[:, cols] = du.astype(BF16)

        @pl.when(i == nt - 1)
        def _():
            pltpu.sync_copy(dc_ref, dc_hbm)
            pltpu.sync_copy(dbb_ref, dbb_hbm)

    rev = lambda i: (nt - 1 - i, 0)
    once = lambda shape: pl.BlockSpec(shape, lambda *_: (0,) * len(shape), pipeline_mode=pl.Buffered(1))
    cross = jax.ShapeDtypeStruct((CW, _CS), F32)
    return pl.pallas_call(
        body, name="ssm_bwd",
        out_shape=(jax.ShapeDtypeStruct((S, CW), BF16), jax.ShapeDtypeStruct((16, NST), F32),
                   jax.ShapeDtypeStruct((8, CW), F32), cross, cross),
        grid=(nt,),
        in_specs=[pl.BlockSpec((tm, 2 * D_MODEL), rev), once((CW, 2 * D_MODEL)), pl.BlockSpec((tm, CW), rev),
                  pl.BlockSpec((tm, CW), lambda i: (nt - 1 - i, 2)), pl.BlockSpec((tm, 2 * NST), rev),
                  once((_NCL, _UC, _CS)), once((_NCL, _CS, _UC)), _full((1, CW)), once((64, NST)),
                  pl.BlockSpec(memory_space=pl.ANY)],
        out_specs=(pl.BlockSpec((tm, CW), rev), _full((16, NST)), _full((8, CW)),
                   pl.BlockSpec(memory_space=pl.ANY), pl.BlockSpec(memory_space=pl.ANY)),
        scratch_shapes=[pltpu.VMEM((8, 2 * NST), F32), pltpu.VMEM((tm, _CS), F32),
                        pltpu.VMEM((CW, _CS), F32), pltpu.VMEM((CW, _CS), F32)],
        compiler_params=_cp(("arbitrary",)))(dzz, w_glu, ys, z, xs, cmt, bbt, d, tab, after)


def _ssm_prep(a_re, a_im, b_re, b_im, log_dt):
    dt = jnp.exp(log_dt.reshape(G))[:, None]
    mag = jnp.exp(dt * a_re)
    e_re, e_im = mag * jnp.cos(dt * a_im), mag * jnp.sin(dt * a_im)
    n_re, n_im = e_re - 1.0, e_im
    den = a_re * a_re + a_im * a_im
    q_re = (n_re * a_re + n_im * a_im) / den
    q_im = (n_im * a_re - n_re * a_im) / den
    bb_re = q_re[..., None] * b_re - q_im[..., None] * b_im
    bb_im = q_re[..., None] * b_im + q_im[..., None] * b_re
    return e_re, e_im, bb_re, bb_im


def _scan_tables(e_re, e_im, reverse):
    er = e_re.reshape(1, NST)
    ei = e_im.reshape(1, NST)
    if reverse:
        ei = -ei
    pows = [(er, ei)]
    for _ in range(7):
        pr, pi_ = pows[-1]
        pows.append((pr * er - pi_ * ei, pr * ei + pi_ * er))
    row = jnp.arange(8)[:, None]
    out = []
    for s in (1, 2, 4):
        pr, pi_ = pows[s - 1]
        keep = (row + s <= 7) if reverse else (row >= s)
        out += [jnp.where(keep, pr, 0.0), jnp.where(keep, pi_, 0.0)]
    allr = jnp.concatenate([p[0] for p in pows], 0)
    alli = jnp.concatenate([p[1] for p in pows], 0)
    if reverse:
        allr, alli = allr[::-1], alli[::-1]
    out += [allr, alli]
    return jnp.concatenate(out, 0).astype(F32)


def _block_diag_mats(bb_re, bb_im, c_re, c_im):
    gc = G // _NCL
    eye = jnp.eye(gc, dtype=F32)
    bre = jnp.einsum("cjph,jk->cjhkp", bb_re.reshape(_NCL, gc, P, H), eye).reshape(_NCL, _UC, _LW)
    bim = jnp.einsum("cjph,jk->cjhkp", bb_im.reshape(_NCL, gc, P, H), eye).reshape(_NCL, _UC, _LW)
    bb = jnp.concatenate([bre, bim], 2)
    cre = jnp.einsum("cjhp,jk->cjpkh", c_re.reshape(_NCL, gc, H, P), eye).reshape(_NCL, _LW, _UC)
    cim = jnp.einsum("cjhp,jk->cjpkh", c_im.reshape(_NCL, gc, H, P), eye).reshape(_NCL, _LW, _UC)
    cm = jnp.concatenate([cre, -cim], 1)
    return bb, cm


def _diag_blocks(cross):
    gc = G // _NCL
    six = cross.reshape(_NCL, gc, H, 2, gc, P)
    same = jnp.eye(gc, dtype=bool)[None, :, None, None, :, None]
    diag = jnp.sum(jnp.where(same, six, 0.0), axis=4)
    diag = jnp.moveaxis(diag, 3, 0).reshape(2, G, H, P)
    return diag[0], diag[1]


def _mix_fwd(scv, yg, z, x, w_cp, w_glu, w_out, g1, n2g, sc2, sh2):
    S = z.shape[0]
    tm = min(512, S)
    D = D_MODEL

    def body(s_ref, yg_ref, glc0_ref, glc1_ref, gls0_ref, gls1_ref, x_ref, wcp_ref, wglu_ref, wout_ref,
             g1_ref, n2_ref, sc_ref, sh_ref, yc_ref, zz_ref, m_ref, o_ref, x2_ref, h2_ref):
        y_conv = jnp.dot(s_ref[...], wcp_ref[...], preferred_element_type=F32)
        zz = jnp.dot(yg_ref[...], wglu_ref[...], preferred_element_type=F32)
        yc_ref[...] = y_conv.astype(BF16)
        zz_ref[...] = zz.astype(BF16)
        for half, (glc_ref, gls_ref) in enumerate(((glc0_ref, gls0_ref), (glc1_ref, gls1_ref))):
            lo, hi = half * CW, (half + 1) * CW
            y_ssm = zz[:, lo:hi] * _sig(zz[:, D + lo:D + hi])
            m_ref[:, lo:hi] = (_sig(glc_ref[...].astype(F32)) * y_conv[:, lo:hi]
                               + _sig(gls_ref[...].astype(F32)) * y_ssm).astype(BF16)
        o = jnp.dot(m_ref[...], wout_ref[...], preferred_element_type=F32)
        o_ref[...] = o.astype(BF16)
        xv = x_ref[...] + g1_ref[...] * o
        x2_ref[...] = xv
        r = lax.rsqrt(jnp.mean(xv * xv, axis=-1, keepdims=True) + EPS)
        h2_ref[...] = (xv * r * (n2_ref[...] * (1.0 + sc_ref[...])) + sh_ref[...]).astype(BF16)

    zb_ = lambda j: pl.BlockSpec((tm, CW), lambda i: (i, j))
    row = lambda w: pl.BlockSpec((tm, w), lambda i: (i, 0))
    par = _full((1, D))
    bf = lambda w: jax.ShapeDtypeStruct((S, w), BF16)
    return pl.pallas_call(
        body, name="mix_fwd",
        out_shape=(bf(D), bf(2 * D), bf(D), bf(D), jax.ShapeDtypeStruct((S, D), F32), bf(D)),
        grid=(S // tm,),
        in_specs=[row(CW), row(CW), zb_(3), zb_(4), zb_(5), zb_(6), row(D), _resident((CW, D)),
                  _resident((CW, 2 * D)), _resident((D, D)), par, par, par, par],
        out_specs=(row(D), row(2 * D), row(D), row(D), row(D), row(D)),
        compiler_params=_cp(("parallel",)))(scv, yg, z, z, z, z, x, w_cp, w_glu, w_out, g1, n2g, sc2, sh2)


def _mix_bwd(do, w_out, z, zz, y_conv, after):
    S = z.shape[0]
    tm = min(512, S)
    D = D_MODEL

    def body(do_ref, w_ref, glc0_ref, glc1_ref, gls0_ref, gls1_ref, za_ref, zb_ref, yc_ref, after_ref,
             dyc_ref, dgl_ref, dzz_ref):
        dm = lax.dot_general(do_ref[...], w_ref[...], (((1,), (1,)), ((), ())), preferred_element_type=F32)
        for half, (glc_ref, gls_ref) in enumerate(((glc0_ref, gls0_ref), (glc1_ref, gls1_ref))):
            lo, hi = half * CW, (half + 1) * CW
            dm_v = dm[:, lo:hi]
            sgc = _sig(glc_ref[...].astype(F32))
            sgs = _sig(gls_ref[...].astype(F32))
            szb = _sig(zb_ref[:, lo:hi].astype(F32))
            za = za_ref[:, lo:hi].astype(F32)
            dyc_ref[:, lo:hi] = (dm_v * sgc).astype(BF16)
            dgl_ref[:, lo:hi] = (dm_v * yc_ref[:, lo:hi].astype(F32) * sgc * (1.0 - sgc)).astype(BF16)
            dys = dm_v * sgs
            dgl_ref[:, D + lo:D + hi] = (dys * (za * szb) * (1.0 - sgs)).astype(BF16)
            dzz_ref[:, lo:hi] = (dys * szb).astype(BF16)
            dzz_ref[:, D + lo:D + hi] = (dys * za * szb * (1.0 - szb)).astype(BF16)

    zb_ = lambda j: pl.BlockSpec((tm, CW), lambda i: (i, j))
    wide = lambda j: pl.BlockSpec((tm, D), lambda i: (i, j))
    return pl.pallas_call(
        body, name="mix_bwd",
        out_shape=(jax.ShapeDtypeStruct((S, D), BF16), jax.ShapeDtypeStruct((S, 2 * D), BF16),
                   jax.ShapeDtypeStruct((S, 2 * D), BF16)),
        grid=(S // tm,),
        in_specs=[wide(0), _resident((D, D)), zb_(3), zb_(4), zb_(5), zb_(6), wide(0), wide(1), wide(0),
                  pl.BlockSpec(memory_space=pl.ANY)],
        out_specs=(wide(0), pl.BlockSpec((tm, 2 * D), lambda i: (i, 0)), pl.BlockSpec((tm, 2 * D), lambda i: (i, 0))),
        compiler_params=_cp(("parallel",)))(do, w_out, z, z, z, z, zz, zz, y_conv, after)


_FC = 1408


def _ffn_in_act(h2, w_fi):
    S, D = h2.shape
    tm = min(512, S)

    def body(h_ref, w_ref, f_ref, a_ref):
        hv = h_ref[...]
        for c in range(FH // _FC):
            lo, hi = c * _FC, (c + 1) * _FC
            g = jnp.dot(hv, w_ref[:, lo:hi], preferred_element_type=F32)
            u = jnp.dot(hv, w_ref[:, FH + lo:FH + hi], preferred_element_type=F32)
            f_ref[:, lo:hi] = g.astype(BF16)
            f_ref[:, FH + lo:FH + hi] = u.astype(BF16)
            a_ref[:, lo:hi] = (g * _sig(g) * u).astype(BF16)

    return pl.pallas_call(
        body, name="ffn_in_act",
        out_shape=(jax.ShapeDtypeStruct((S, 2 * FH), BF16), jax.ShapeDtypeStruct((S, FH), BF16)),
        grid=(S // tm,),
        in_specs=[pl.BlockSpec((tm, D), lambda i: (i, 0)), _resident((D, 2 * FH))],
        out_specs=(pl.BlockSpec((tm, 2 * FH), lambda i: (i, 0)), pl.BlockSpec((tm, FH), lambda i: (i, 0))),
        compiler_params=_cp(("parallel",)))(h2, w_fi)


def _ffn_bwd(do2, w_fo, f, after):
    S, D = do2.shape
    tm = min(512, S)

    def body(d_ref, w_ref, f_ref, after_ref, df_ref):
        dv = d_ref[...]
        for c in range(FH // _FC):
            lo, hi = c * _FC, (c + 1) * _FC
            dact = lax.dot_general(dv, w_ref[lo:hi, :], (((1,), (1,)), ((), ())), preferred_element_type=F32)
            g = f_ref[:, lo:hi].astype(F32)
            u = f_ref[:, FH + lo:FH + hi].astype(F32)
            sg = _sig(g)
            df_ref[:, lo:hi] = (dact * u * (sg * (1.0 + g * (1.0 - sg)))).astype(BF16)
            df_ref[:, FH + lo:FH + hi] = (dact * g * sg).astype(BF16)

    return pl.pallas_call(
        body, name="ffn_bwd", out_shape=jax.ShapeDtypeStruct((S, 2 * FH), BF16), grid=(S // tm,),
        in_specs=[pl.BlockSpec((tm, D), lambda i: (i, 0)), _resident((FH, D)),
                  pl.BlockSpec((tm, 2 * FH), lambda i: (i, 0)), pl.BlockSpec(memory_space=pl.ANY)],
        out_specs=pl.BlockSpec((tm, 2 * FH), lambda i: (i, 0)),
        compiler_params=_cp(("parallel",)))(do2, w_fo, f, after)


def _ffn_out_final(x2, act, w_fo, g2, fg, tgt):
    S, D = x2.shape
    tm = min(512, S)

    def body(x2_ref, a_ref, w_ref, g2_ref, fg_ref, t_ref, dx3_ref, do2_ref, ls_ref, dfg_ref, dg2_ref):
        i = pl.program_id(0)

        @pl.when(i == 0)
        def _():
            ls_ref[...] = jnp.zeros_like(ls_ref)
            dfg_ref[...] = jnp.zeros_like(dfg_ref)
            dg2_ref[...] = jnp.zeros_like(dg2_ref)

        o2 = jnp.dot(a_ref[...], w_ref[...], preferred_element_type=F32)
        x3 = x2_ref[...] + g2_ref[...] * o2
        r = lax.rsqrt(jnp.mean(x3 * x3, axis=-1, keepdims=True) + EPS)
        xn = x3 * r
        err = xn * fg_ref[...] - t_ref[...]
        dy = err * (1.0 / D)
        dxn = dy * fg_ref[...]
        dx3 = r * (dxn - xn * jnp.mean(dxn * xn, axis=-1, keepdims=True))
        dx3_ref[...] = dx3
        do2_ref[...] = (dx3 * g2_ref[...]).astype(BF16)
        e2 = _colsum8(err * err)
        lanes = e2[:, 0:128]
        for q in range(1, D // 128):
            lanes = lanes + e2[:, q * 128:(q + 1) * 128]
        ls_ref[...] += lanes * (0.5 / D)
        dfg_ref[...] += _colsum8(dy * xn)
        dg2_ref[...] += _colsum8(dx3 * o2)

    row = pl.BlockSpec((tm, D), lambda i: (i, 0))
    par = _full((1, D))
    return pl.pallas_call(
        body, name="final_loss",
        out_shape=(jax.ShapeDtypeStruct((S, D), F32), jax.ShapeDtypeStruct((S, D), BF16),
                   jax.ShapeDtypeStruct((8, 128), F32), jax.ShapeDtypeStruct((8, D), F32),
                   jax.ShapeDtypeStruct((8, D), F32)),
        grid=(S // tm,), in_specs=[row, pl.BlockSpec((tm, FH), lambda i: (i, 0)), _resident((FH, D)), par, par, row],
        out_specs=(row, row, _full((8, 128)), _full((8, D)), _full((8, D))),
        compiler_params=_cp(("arbitrary",)))(x2, act, w_fo, g2, fg, tgt)


def _normmod_bwd(dsrc, w, xin, dres, g, sc, gate, o, after, name):
    S, D = xin.shape
    parts = list(dsrc) if isinstance(dsrc, (list, tuple)) else [dsrc]
    widths = [p.shape[1] for p in parts]
    K = sum(widths)
    tm = min(512, S)
    npart = len(parts)

    def body(*refs):
        ds_refs = refs[:npart]
        w_ref, x_ref, dr_ref, g_ref, sc_ref, gate_ref, o_ref, after_ref = refs[npart:npart + 8]
        dx_ref, do_ref, dsh_ref, dsc_ref, dg_ref, dgate_ref = refs[npart + 8:]
        i = pl.program_id(0)

        @pl.when(i == 0)
        def _():
            dsh_ref[...] = jnp.zeros_like(dsh_ref)
            dsc_ref[...] = jnp.zeros_like(dsc_ref)
            dg_ref[...] = jnp.zeros_like(dg_ref)
            dgate_ref[...] = jnp.zeros_like(dgate_ref)

        gv = g_ref[...]
        scale = 1.0 + sc_ref[...]
        xv = x_ref[...]
        r = lax.rsqrt(jnp.mean(xv * xv, axis=-1, keepdims=True) + EPS)
        xn = xv * r
        dh_v, col = None, 0
        for ds_ref, wd in zip(ds_refs, widths):
            t = lax.dot_general(ds_ref[...], w_ref[:, col:col + wd], (((1,), (1,)), ((), ())),
                                preferred_element_type=F32)
            dh_v = t if dh_v is None else dh_v + t
            col += wd
        dxn = dh_v * (gv * scale)
        dx = dr_ref[...] + r * (dxn - xn * jnp.mean(dxn * xn, axis=-1, keepdims=True))
        dx_ref[...] = dx
        do_ref[...] = (dx * gate_ref[...]).astype(BF16)
        hx = dh_v * xn
        dsh_ref[...] += _colsum8(dh_v)
        dsc_ref[...] += _colsum8(hx) * gv
        dg_ref[...] += _colsum8(hx) * scale
        dgate_ref[...] += _colsum8(dx * o_ref[...])

    row = pl.BlockSpec((tm, D), lambda i: (i, 0))
    par = _full((1, D))
    acc = jax.ShapeDtypeStruct((8, D), F32)
    return pl.pallas_call(
        body, name=name,
        out_shape=(jax.ShapeDtypeStruct((S, D), F32), jax.ShapeDtypeStruct((S, D), BF16), acc, acc, acc, acc),
        grid=(S // tm,),
        in_specs=[pl.BlockSpec((tm, wd), lambda i: (i, 0)) for wd in widths]
        + [_resident((D, K)), row, row, par, par, par, row, pl.BlockSpec(memory_space=pl.ANY)],
        out_specs=(row, row, _full((8, D)), _full((8, D)), _full((8, D)), _full((8, D))),
        compiler_params=_cp(("arbitrary",)))(*parts, w, xin, dres, g, sc, gate, o, after)


def _me():
    return lax.axis_index("x"), lax.axis_index("y"), lax.axis_index("c")


def _allgather8(v, name, after=()):
    R, C = v.shape
    after = list(after)

    def body(v_ref, *rest):
        out_ref, send_sems, recv_sems, local_sem = rest[len(after):]
        x, y, c = _me()
        mine = pltpu.make_async_copy(v_ref, out_ref.at[4 * x + 2 * y + c], local_sem)
        mine.start()
        copies = []
        for k in range(1, N_DEV):
            fx, fy, fc = (k >> 2) & 1, (k >> 1) & 1, k & 1
            peer = (x ^ fx, y ^ fy, c ^ fc)
            copies.append(pltpu.make_async_remote_copy(
                src_ref=v_ref, dst_ref=out_ref.at[4 * x + 2 * y + c],
                send_sem=send_sems.at[k - 1], recv_sem=recv_sems.at[k - 1],
                device_id=peer, device_id_type=MESH))
        for cp in copies:
            cp.start()
        for k in range(1, N_DEV):
            fx, fy, fc = (k >> 2) & 1, (k >> 1) & 1, k & 1
            src_slot = 4 * (x ^ fx) + 2 * (y ^ fy) + (c ^ fc)
            pltpu.make_async_remote_copy(
                src_ref=v_ref, dst_ref=out_ref.at[src_slot],
                send_sem=send_sems.at[k - 1], recv_sem=recv_sems.at[k - 1],
                device_id=(x ^ fx, y ^ fy, c ^ fc), device_id_type=MESH).wait_recv()
        for cp in copies:
            cp.wait_send()
        mine.wait()

    return pl.pallas_call(
        body, name=name, out_shape=jax.ShapeDtypeStruct((N_DEV, R, C), v.dtype),
        in_specs=[pl.BlockSpec(memory_space=pltpu.VMEM)] + [pl.BlockSpec(memory_space=pl.ANY)] * len(after),
        out_specs=pl.BlockSpec(memory_space=pltpu.VMEM),
        scratch_shapes=[pltpu.SemaphoreType.DMA((N_DEV - 1,)), pltpu.SemaphoreType.DMA((N_DEV - 1,)),
                        pltpu.SemaphoreType.DMA],
        compiler_params=pltpu.CompilerParams(vmem_limit_bytes=VMEM_LIMIT))(v, *after)


_HBM = pl.BlockSpec(memory_space=pltpu.HBM)
_SEM = pl.BlockSpec(memory_space=pltpu.SEMAPHORE)
_EFFECT = pltpu.SideEffectType.DATAFLOW_SIDE_EFFECTING
_N_PEER = N_CHIP - 1


def _chip_part(ref, axis, n, chip):
    start = pl.multiple_of(chip * n, 8)
    return ref.at[pl.ds(start, n), :] if axis == 0 else ref.at[:, pl.ds(start, n)]


def _gather_copy(k, src_ref, land_ref, send_sems, recv_sems, axis, arriving):
    x, y, c = _me()
    px, py = x ^ ((k >> 1) & 1), y ^ (k & 1)
    chip = 2 * px + py if arriving else 2 * x + y
    return pltpu.make_async_remote_copy(
        src_ref=src_ref, dst_ref=_chip_part(land_ref, axis, src_ref.shape[axis], chip),
        send_sem=send_sems.at[k - 1], recv_sem=recv_sems.at[k - 1], device_id=(px, py, c), device_id_type=MESH)


def _scatter_copy(k, grad_ref, land_ref, send_sems, recv_sems, axis):
    x, y, c = _me()
    px, py = x ^ ((k >> 1) & 1), y ^ (k & 1)
    return pltpu.make_async_remote_copy(
        src_ref=_chip_part(grad_ref, axis, grad_ref.shape[axis] // N_CHIP, 2 * px + py), dst_ref=land_ref.at[k],
        send_sem=send_sems.at[k - 1], recv_sem=recv_sems.at[k - 1], device_id=(px, py, c), device_id_type=MESH)


def _scatter_own(grad_ref, land_ref, send_sems, axis):
    x, y, _ = _me()
    return pltpu.make_async_copy(_chip_part(grad_ref, axis, grad_ref.shape[axis] // N_CHIP, 2 * x + y),
                                 land_ref.at[0], send_sems.at[_N_PEER])


def _own_copy(src_ref, land_ref, sends, axis):
    x, y, _ = _me()
    return pltpu.make_async_copy(src_ref, _chip_part(land_ref, axis, src_ref.shape[axis], 2 * x + y),
                                 sends.at[_N_PEER])


def _gather_start(shards, axes, after, name):
    nw = len(shards)
    lands = []
    for s, ax in zip(shards, axes):
        shp = list(s.shape)
        shp[ax] *= N_CHIP
        lands.append(lax.empty(tuple(shp), s.dtype))

    def body(*refs):
        srcs, zones = refs[:nw], refs[nw:2 * nw]
        sends, recvs = refs[2 * nw + 1:3 * nw + 1], refs[3 * nw + 1:4 * nw + 1]
        token = refs[-1]
        for w in range(nw):
            for k in range(1, N_CHIP):
                _gather_copy(k, srcs[w], zones[w], sends[w], recvs[w], axes[w], False).start()
        for w in range(nw):
            _own_copy(srcs[w], zones[w], sends[w], axes[w]).start()
        token[...] = jnp.zeros_like(token)

    outs = pl.pallas_call(
        body, name=name,
        out_shape=tuple([pltpu.SemaphoreType.DMA((_N_PEER + 1,))] * nw + [pltpu.SemaphoreType.DMA((_N_PEER,))] * nw
                        + [pltpu.HBM(a.shape, a.dtype) for a in list(shards) + list(lands)]
                        + [jax.ShapeDtypeStruct((8, 128), F32)]),
        in_specs=[_HBM] * (2 * nw) + [pl.BlockSpec(memory_space=pl.ANY)],
        out_specs=tuple([_SEM] * (2 * nw) + [_HBM] * (2 * nw) + [pl.BlockSpec(memory_space=pltpu.VMEM)]),
        input_output_aliases={i: 2 * nw + i for i in range(2 * nw)},
        compiler_params=pltpu.CompilerParams(has_side_effects=_EFFECT),
    )(*([pltpu.with_memory_space_constraint(a, pltpu.HBM) for a in list(shards) + list(lands)] + [after]))
    per_weight = [(outs[w], outs[nw + w], outs[2 * nw + w], outs[3 * nw + w]) for w in range(nw)]
    return per_weight, outs[-1]


def _gather_wait(state, axis, after, name):
    send_sems, recv_sems, shard, land = state

    after = list(after) if isinstance(after, (list, tuple)) else [after]

    def body(src_ref, land_ref, sends, recvs, *rest):
        for k in range(1, N_CHIP):
            _gather_copy(k, src_ref, land_ref, sends, recvs, axis, False).wait_send()
            _gather_copy(k, src_ref, land_ref, sends, recvs, axis, True).wait_recv()
        _own_copy(src_ref, land_ref, sends, axis).wait()

    return pl.pallas_call(
        body, name=name, out_shape=(pltpu.HBM(shard.shape, shard.dtype), pltpu.HBM(land.shape, land.dtype)),
        in_specs=[_HBM, _HBM, _SEM, _SEM] + [pl.BlockSpec(memory_space=pl.ANY)] * len(after), out_specs=(_HBM, _HBM),
        input_output_aliases={0: 0, 1: 1},
        compiler_params=pltpu.CompilerParams(has_side_effects=_EFFECT),
    )(shard, land, send_sems, recv_sems, *after)[1]


def _half_rows(ref, c):
    k2 = ref.shape[0] // 2
    return pl.ds(pl.multiple_of(c * k2, 8), k2)


def _half_copy(k, shard_ref, land_ref, send_sems, recv_sems, arriving):
    x, y, c = _me()
    px, py = x ^ ((k >> 1) & 1), y ^ (k & 1)
    n = shard_ref.shape[1]
    chip = 2 * px + py if arriving else 2 * x + y
    return pltpu.make_async_remote_copy(
        src_ref=shard_ref.at[_half_rows(shard_ref, c), :],
        dst_ref=land_ref.at[_half_rows(land_ref, c), pl.ds(pl.multiple_of(chip * n, 128), n)],
        send_sem=send_sems.at[k - 1], recv_sem=recv_sems.at[k - 1], device_id=(px, py, c), device_id_type=MESH)


def _half_own(shard_ref, land_ref, send_sems):
    x, y, c = _me()
    n = shard_ref.shape[1]
    return pltpu.make_async_copy(
        shard_ref.at[_half_rows(shard_ref, c), :],
        land_ref.at[_half_rows(land_ref, c), pl.ds(pl.multiple_of((2 * x + y) * n, 128), n)], send_sems.at[_N_PEER])


def _half_gather_start(shard, after, name):
    K, n = shard.shape
    land = lax.empty((K, N_CHIP * n), shard.dtype)

    def body(shard_ref, land_ref, after_ref, sends, recvs, shard_thru, land_thru, token):
        for k in range(1, N_CHIP):
            _half_copy(k, shard_ref, land_ref, sends, recvs, False).start()
        _half_own(shard_ref, land_ref, sends).start()
        token[...] = jnp.zeros_like(token)

    outs = pl.pallas_call(
        body, name=name,
        out_shape=(pltpu.SemaphoreType.DMA((_N_PEER + 1,)), pltpu.SemaphoreType.DMA((_N_PEER,)),
                   pltpu.HBM(shard.shape, shard.dtype), pltpu.HBM(land.shape, land.dtype),
                   jax.ShapeDtypeStruct((8, 128), F32)),
        in_specs=[_HBM, _HBM, pl.BlockSpec(memory_space=pl.ANY)],
        out_specs=(_SEM, _SEM, _HBM, _HBM, pl.BlockSpec(memory_space=pltpu.VMEM)),
        input_output_aliases={0: 2, 1: 3},
        compiler_params=pltpu.CompilerParams(has_side_effects=_EFFECT),
    )(pltpu.with_memory_space_constraint(shard, pltpu.HBM), pltpu.with_memory_space_constraint(land, pltpu.HBM), after)
    return outs[:4], outs[4]


def _half_gather_wait(state, after, name):
    send_sems, recv_sems, shard, land = state
    after = list(after)

    def body(shard_ref, land_ref, sends, recvs, *rest):
        for k in range(1, N_CHIP):
            _half_copy(k, shard_ref, land_ref, sends, recvs, False).wait_send()
            _half_copy(k, shard_ref, land_ref, sends, recvs, True).wait_recv()
        _half_own(shard_ref, land_ref, sends).wait()

    return pl.pallas_call(
        body, name=name, out_shape=(pltpu.HBM(shard.shape, shard.dtype), pltpu.HBM(land.shape, land.dtype)),
        in_specs=[_HBM, _HBM, _SEM, _SEM] + [pl.BlockSpec(memory_space=pl.ANY)] * len(after), out_specs=(_HBM, _HBM),
        input_output_aliases={0: 0, 1: 1},
        compiler_params=pltpu.CompilerParams(has_side_effects=_EFFECT),
    )(shard, land, send_sems, recv_sems, *after)[1]


def _half_swap_copy(land_ref, send_sem, recv_sem, arriving):
    x, y, c = _me()
    rows = _half_rows(land_ref, 1 - c if arriving else c)
    return pltpu.make_async_remote_copy(src_ref=land_ref.at[rows, :], dst_ref=land_ref.at[rows, :], send_sem=send_sem,
                                        recv_sem=recv_sem, device_id=(x, y, 1 - c), device_id_type=MESH)


def _half_swap_start(land, name):
    def body(land_ref, send, recv, land_thru, token):
        _half_swap_copy(land_ref, send.at[0], recv.at[0], False).start()
        token[...] = jnp.zeros_like(token)

    sem = pltpu.SemaphoreType.DMA((1,))
    outs = pl.pallas_call(
        body, name=name,
        out_shape=(sem, sem, pltpu.HBM(land.shape, land.dtype), jax.ShapeDtypeStruct((8, 128), F32)),
        in_specs=[_HBM], out_specs=(_SEM, _SEM, _HBM, pl.BlockSpec(memory_space=pltpu.VMEM)),
        input_output_aliases={0: 2},
        compiler_params=pltpu.CompilerParams(has_side_effects=_EFFECT),
    )(pltpu.with_memory_space_constraint(land, pltpu.HBM))
    return outs[:3], outs[3]


def _half_swap_wait(state, after, name):
    send, recv, land = state

    def body(land_ref, send_ref, recv_ref, after_ref, got_ref):
        _half_swap_copy(land_ref, send_ref.at[0], recv_ref.at[0], False).wait_send()
        _half_swap_copy(land_ref, send_ref.at[0], recv_ref.at[0], True).wait_recv()

    return pl.pallas_call(
        body, name=name, out_shape=pltpu.HBM(land.shape, land.dtype),
        in_specs=[_HBM, _SEM, _SEM, pl.BlockSpec(memory_space=pl.ANY)], out_specs=_HBM,
        input_output_aliases={0: 0},
        compiler_params=pltpu.CompilerParams(has_side_effects=_EFFECT),
    )(land, send, recv, after)


def _all8_copy(k, v_ref, land_ref, send_sems, recv_sems, arriving):
    x, y, c = _me()
    px, py, pc = x ^ ((k >> 2) & 1), y ^ ((k >> 1) & 1), c ^ (k & 1)
    slot = 4 * px + 2 * py + pc if arriving else 4 * x + 2 * y + c
    return pltpu.make_async_remote_copy(
        src_ref=v_ref, dst_ref=land_ref.at[slot], send_sem=send_sems.at[k - 1], recv_sem=recv_sems.at[k - 1],
        device_id=(px, py, pc), device_id_type=MESH)


def _all8_own(v_ref, land_ref, send_sems):
    x, y, c = _me()
    return pltpu.make_async_copy(v_ref, land_ref.at[4 * x + 2 * y + c], send_sems.at[N_DEV - 1])


def _all8_start(v, name):
    land = lax.empty((N_DEV,) + v.shape, v.dtype)

    def body(v_ref, land_ref, sends, recvs, v_thru, land_thru, token):
        for k in range(1, N_DEV):
            _all8_copy(k, v_ref, land_ref, sends, recvs, False).start()
        _all8_own(v_ref, land_ref, sends).start()
        token[...] = jnp.zeros_like(token)

    outs = pl.pallas_call(
        body, name=name,
        out_shape=(pltpu.SemaphoreType.DMA((N_DEV,)), pltpu.SemaphoreType.DMA((N_DEV - 1,)),
                   pltpu.HBM(v.shape, v.dtype), pltpu.HBM(land.shape, land.dtype),
                   jax.ShapeDtypeStruct((8, 128), F32)),
        in_specs=[_HBM, _HBM], out_specs=(_SEM, _SEM, _HBM, _HBM, pl.BlockSpec(memory_space=pltpu.VMEM)),
        input_output_aliases={0: 2, 1: 3},
        compiler_params=pltpu.CompilerParams(has_side_effects=_EFFECT),
    )(pltpu.with_memory_space_constraint(v, pltpu.HBM), pltpu.with_memory_space_constraint(land, pltpu.HBM))
    return outs[:4], outs[4]


def _all8_wait(state, after, name):
    send_sems, recv_sems, v, land = state

    def body(v_ref, land_ref, sends, recvs, after_ref, v_dead, got_ref):
        for k in range(1, N_DEV):
            _all8_copy(k, v_ref, land_ref, sends, recvs, False).wait_send()
            _all8_copy(k, v_ref, land_ref, sends, recvs, True).wait_recv()
        _all8_own(v_ref, land_ref, sends).wait()

    return pl.pallas_call(
        body, name=name, out_shape=(pltpu.HBM(v.shape, v.dtype), pltpu.HBM(land.shape, land.dtype)),
        in_specs=[_HBM, _HBM, _SEM, _SEM, pl.BlockSpec(memory_space=pl.ANY)], out_specs=(_HBM, _HBM),
        input_output_aliases={0: 0, 1: 1},
        compiler_params=pltpu.CompilerParams(has_side_effects=_EFFECT),
    )(v, land, send_sems, recv_sems, after)[1]


def _swap_copy(w, src_ref, land_ref, send_sems, recv_sems):
    x, y, c = _me()
    return pltpu.make_async_remote_copy(src_ref=src_ref, dst_ref=land_ref, send_sem=send_sems.at[w],
                                        recv_sem=recv_sems.at[w], device_id=(x, y, 1 - c), device_id_type=MESH)


def _swap_start(arrs, after, name):
    nw = len(arrs)
    lands = [lax.empty(a.shape, a.dtype) for a in arrs]

    def body(*refs):
        srcs, zones = refs[:nw], refs[nw:2 * nw]
        sends, recvs = refs[2 * nw + 1], refs[2 * nw + 2]
        for w in range(nw):
            _swap_copy(w, srcs[w], zones[w], sends, recvs).start()
        refs[-1][...] = jnp.zeros_like(refs[-1])

    sem = pltpu.SemaphoreType.DMA((nw,))
    outs = pl.pallas_call(
        body, name=name,
        out_shape=tuple([sem, sem] + [pltpu.HBM(a.shape, a.dtype) for a in list(arrs) + lands]
                        + [jax.ShapeDtypeStruct((8, 128), F32)]),
        in_specs=[_HBM] * (2 * nw) + [pl.BlockSpec(memory_space=pl.ANY)],
        out_specs=tuple([_SEM, _SEM] + [_HBM] * (2 * nw) + [pl.BlockSpec(memory_space=pltpu.VMEM)]),
        input_output_aliases={i: 2 + i for i in range(2 * nw)},
        compiler_params=pltpu.CompilerParams(has_side_effects=_EFFECT),
    )(*([pltpu.with_memory_space_constraint(a, pltpu.HBM) for a in list(arrs) + lands] + [after]))
    return (outs[0], outs[1], outs[2:2 + nw], outs[2 + nw:2 + 2 * nw]), outs[-1]


def _swap_wait(state, after, name):
    send_sems, recv_sems, arrs, lands = state
    nw = len(arrs)

    def body(*refs):
        srcs, zones = refs[:nw], refs[nw:2 * nw]
        sends, recvs = refs[2 * nw], refs[2 * nw + 1]
        for w in range(nw):
            cp = _swap_copy(w, srcs[w], zones[w], sends, recvs)
            cp.wait_send()
            cp.wait_recv()

    outs = pl.pallas_call(
        body, name=name, out_shape=tuple(pltpu.HBM(a.shape, a.dtype) for a in list(arrs) + list(lands)),
        in_specs=[_HBM] * (2 * nw) + [_SEM, _SEM, pl.BlockSpec(memory_space=pl.ANY)],
        out_specs=tuple([_HBM] * (2 * nw)),
        input_output_aliases={i: i for i in range(2 * nw)},
        compiler_params=pltpu.CompilerParams(has_side_effects=_EFFECT),
    )(*arrs, *lands, send_sems, recv_sems, after)
    return list(outs[:nw]), list(outs[nw:])


def _scatter_start(grad, axis, name):
    shp = list(grad.shape)
    shp[axis] //= N_CHIP
    land = lax.empty((N_CHIP,) + tuple(shp), grad.dtype)

    def body(grad_ref, land_ref, sends, recvs, grad_thru, land_thru, token):
        for k in range(1, N_CHIP):
            _scatter_copy(k, grad_ref, land_ref, sends, recvs, axis).start()
        _scatter_own(grad_ref, land_ref, sends, axis).start()
        token[...] = jnp.zeros_like(token)

    outs = pl.pallas_call(
        body, name=name,
        out_shape=(pltpu.SemaphoreType.DMA((_N_PEER + 1,)), pltpu.SemaphoreType.DMA((_N_PEER,)),
                   pltpu.HBM(grad.shape, grad.dtype), pltpu.HBM(land.shape, land.dtype),
                   jax.ShapeDtypeStruct((8, 128), F32)),
        in_specs=[_HBM, _HBM], out_specs=(_SEM, _SEM, _HBM, _HBM, pl.BlockSpec(memory_space=pltpu.VMEM)),
        input_output_aliases={0: 2, 1: 3},
        compiler_params=pltpu.CompilerParams(has_side_effects=_EFFECT),
    )(pltpu.with_memory_space_constraint(grad, pltpu.HBM), pltpu.with_memory_space_constraint(land, pltpu.HBM))
    return outs[:4], outs[4]


def _scatter_wait(state, axis, after, name):
    send_sems, recv_sems, grad, land = state

    def body(grad_ref, land_ref, sends, recvs, after_ref, grad_dead, got_ref):
        for k in range(1, N_CHIP):
            cp = _scatter_copy(k, grad_ref, land_ref, sends, recvs, axis)
            cp.wait_send()
            cp.wait_recv()
        _scatter_own(grad_ref, land_ref, sends, axis).wait()

    return pl.pallas_call(
        body, name=name, out_shape=(pltpu.HBM(grad.shape, grad.dtype), pltpu.HBM(land.shape, land.dtype)),
        in_specs=[_HBM, _HBM, _SEM, _SEM, pl.BlockSpec(memory_space=pl.ANY)], out_specs=(_HBM, _HBM),
        input_output_aliases={0: 0, 1: 1},
        compiler_params=pltpu.CompilerParams(has_side_effects=_EFFECT),
    )(grad, land, send_sems, recv_sems, after)[1]


_C1 = 1.0 - B1 ** STEP
_C2 = 1.0 - B2 ** STEP


def _adam_math(w, g, m, v):
    m = B1 * m + (1.0 - B1) * g
    v = B2 * v + (1.0 - B2) * (g * g)
    delta = -LR * ((m / _C1) / (jnp.sqrt(v / _C2) + AEPS) + WD * w)
    return delta, m, v


def _adamw(w, m, v, groups, name):
    R, C = w.shape
    tr = R if R <= 256 else (128 if R % 128 == 0 else 176)
    assert R % tr == 0, (name, R)
    gparts = [p for grp in groups for p in grp]
    sizes = [len(grp) for grp in groups]
    ng = len(gparts)

    def body(*refs):
        w_ref, m_ref, v_ref = refs[:3]
        g_refs = list(refs[3:3 + ng])
        g_out, d_out, m_out, v_out = refs[3 + ng:]
        g = None
        for size in sizes:
            s = None
            for r in [g_refs.pop(0) for _ in range(size)]:
                terms = [r[q] for q in range(r.shape[0])] if len(r.shape) == 3 else [r[...]]
                for t in terms:
                    s = t.astype(F32) if s is None else s + t.astype(F32)
            g = s if g is None else g + s
        delta, mn, vn = _adam_math(w_ref[...], g, m_ref[...], v_ref[...])
        g_out[...] = g
        d_out[...] = delta
        m_out[...] = mn
        v_out[...] = vn

    blk = pl.BlockSpec((tr, C), lambda i: (i, 0))
    g_specs = [blk if p.ndim == 2 else pl.BlockSpec((p.shape[0], tr, C), lambda i: (0, i, 0)) for p in gparts]
    sds = jax.ShapeDtypeStruct((R, C), F32)
    return pl.pallas_call(
        body, name=name, out_shape=(sds, sds, sds, sds), grid=(R // tr,),
        in_specs=[blk, blk, blk] + g_specs, out_specs=(blk, blk, blk, blk),
        compiler_params=_cp(("parallel",)))(w, m, v, *gparts)


def _adamw_small(stack, names, wts, mom, var, sum_only, name):
    items, row = [], 0
    for n in names:
        shape = (KW, CW) if n == "conv_w" else wts[n].shape
        size = int(np.prod(shape))
        vec = len(shape) == 2 and shape[0] == 1 and n not in sum_only
        view = shape if vec else (-(-size // _PACK_COLS), _PACK_COLS)
        items.append((n, row, size, vec, view))
        row += _pack_rows(shape)
    upd = [it for it in items if it[0] not in sum_only]
    operands = [stack]
    for n, _, _, _, view in upd:
        operands += [d[n].reshape(view) for d in (wts, mom, var)]

    def grad(stack_ref, r0, nrows, ncols):
        g = stack_ref[0, r0:r0 + nrows, 0:ncols]
        for q in range(1, N_DEV):
            g = g + stack_ref[q, r0:r0 + nrows, 0:ncols]
        return g

    def body(*refs):
        stack_ref, ins, outs = refs[0], refs[1:1 + 3 * len(upd)], refs[1 + 3 * len(upd):]
        o = 0
        for idx, (n, r0, size, vec, view) in enumerate(upd):
            w_ref, m_ref, v_ref = ins[3 * idx:3 * idx + 3]
            g_out, d_out, m_out, v_out = outs[o:o + 4]
            o += 4
            if vec:
                pieces = [(j, j * _PACK_COLS, min((j + 1) * _PACK_COLS, size)) for j in range(-(-size // _PACK_COLS))]
            else:
                pieces = [(None, 0, _PACK_COLS)]
            for j, lo, hi in pieces:
                if vec:
                    g = grad(stack_ref, r0 + j, 1, hi - lo)
                    sl = (slice(None), slice(lo, hi))
                else:
                    g = grad(stack_ref, r0, view[0], _PACK_COLS)
                    sl = (slice(None), slice(None))
                delta, mn, vn = _adam_math(w_ref[sl], g, m_ref[sl], v_ref[sl])
                g_out[sl] = g
                d_out[sl] = delta
                m_out[sl] = mn
                v_out[sl] = vn
        for n, r0, size, vec, view in items:
            if n in sum_only:
                outs[o][...] = grad(stack_ref, r0, view[0], _PACK_COLS)
                o += 1

    out_shape = []
    for n, _, _, _, view in upd:
        out_shape += [jax.ShapeDtypeStruct(view, F32)] * 4
    out_shape += [jax.ShapeDtypeStruct(view, F32) for n, _, _, _, view in items if n in sum_only]
    vm = pl.BlockSpec(memory_space=pltpu.VMEM)
    res = pl.pallas_call(
        body, name=name, out_shape=tuple(out_shape), in_specs=[vm] * len(operands),
        out_specs=tuple([vm] * len(out_shape)),
        compiler_params=pltpu.CompilerParams(vmem_limit_bytes=VMEM_LIMIT))(*operands)
    updated = {n: tuple(r.reshape(wts[n].shape) for r in res[4 * i:4 * i + 4]) for i, (n, *_) in enumerate(upd)}
    sums = dict(zip([it[0] for it in items if it[0] in sum_only], res[4 * len(upd):]))
    return updated, sums


def _adamw_native(tensors, name):
    nt = len(tensors)

    def body(*refs):
        ins, outs = refs[:4 * nt], refs[4 * nt:]
        for t in range(nt):
            w_ref, m_ref, v_ref, g_ref = ins[4 * t:4 * t + 4]
            g = g_ref[...]
            delta, mn, vn = _adam_math(w_ref[...], g, m_ref[...], v_ref[...])
            outs[4 * t][...] = g
            outs[4 * t + 1][...] = delta
            outs[4 * t + 2][...] = mn
            outs[4 * t + 3][...] = vn

    vm = pl.BlockSpec(memory_space=pltpu.VMEM)
    flat = [a for tup in tensors for a in tup]
    res = pl.pallas_call(
        body, name=name, out_shape=tuple(jax.ShapeDtypeStruct(tup[0].shape, F32) for tup in tensors for _ in range(4)),
        in_specs=[vm] * len(flat), out_specs=tuple([vm] * (4 * nt)),
        compiler_params=pltpu.CompilerParams(vmem_limit_bytes=VMEM_LIMIT))(*flat)
    return [tuple(res[4 * t:4 * t + 4]) for t in range(nt)]


def _mod_shard(c_all, w_ada, b_ada_cols):
    n = w_ada.shape[1]
    tn = 512

    def body(c_ref, w_ref, b_ref, o_ref):
        cv = c_ref[...]
        ca = (cv * _sig(cv)).astype(BF16)
        o_ref[...] = jnp.dot(ca, w_ref[...].astype(BF16), preferred_element_type=F32) + b_ref[...]

    return pl.pallas_call(
        body, name="mod_shard", out_shape=jax.ShapeDtypeStruct((N_DEV, n), F32), grid=(n // tn,),
        in_specs=[_full((N_DEV, D_MODEL)), pl.BlockSpec((D_MODEL, tn), lambda j: (0, j)),
                  pl.BlockSpec((1, tn), lambda j: (0, j))],
        out_specs=pl.BlockSpec((N_DEV, tn), lambda j: (0, j)),
        compiler_params=_cp(("parallel",)))(c_all, w_ada, b_ada_cols)


def _ada_grad(c_all, dmod_cols, after):
    n = dmod_cols.shape[1]
    tn = 512

    def body(c_ref, d_ref, after_ref, o_ref):
        cv = c_ref[...]
        ca = cv * _sig(cv)
        o_ref[...] = lax.dot_general(ca, d_ref[...], (((0,), (0,)), ((), ())),
                                     preferred_element_type=F32, precision=lax.Precision.HIGHEST)

    return pl.pallas_call(
        body, name="ada_grad", out_shape=jax.ShapeDtypeStruct((D_MODEL, n), F32), grid=(n // tn,),
        in_specs=[_full((N_DEV, D_MODEL)), pl.BlockSpec((N_DEV, tn), lambda j: (0, j)),
                  pl.BlockSpec(memory_space=pl.ANY)],
        out_specs=pl.BlockSpec((D_MODEL, tn), lambda j: (0, j)),
        compiler_params=_cp(("parallel",)))(c_all, dmod_cols, after)


def _ssm_tables(W):
    e_re, e_im, bb_re, bb_im = _ssm_prep(W["ssm_a_re"], W["ssm_a_im"], W["ssm_b_re"], W["ssm_b_im"], W["ssm_log_dt"])
    bb, cm = _block_diag_mats(bb_re, bb_im, W["ssm_c_re"], W["ssm_c_im"])
    bb16, cm16 = bb.astype(BF16), cm.astype(BF16)
    return (bb16, cm16, jnp.swapaxes(bb16, 1, 2), jnp.swapaxes(cm16, 1, 2),
            _scan_tables(e_re, e_im, False), _scan_tables(e_re, e_im, True))


def _device_step(x, mod, W, tables, tgt, getw, put, early):
    sh1, sc1, g1, sh2, sc2, g2 = [mod[:, i * D_MODEL:(i + 1) * D_MODEL] for i in range(6)]
    bb16, cm16, bbt16, cmt16, tab_f, tab_b = tables

    w_in = getw("w_in", [mod, *tables])
    h1, z = _in_proj(x, W["norm1_g"], sc1, sh1, w_in)
    yc, scv = _conv_fwd(z, W["conv_w"], W["conv_b"], W["conv_ln_g"], W["conv_ln_b"])
    xs, ys, yg = _ssm_fwd(z, bb16, cm16, W["ssm_d"], tab_f)
    w_cp, w_glu, w_out = getw("conv_proj", scv), getw("ssm_glu", yg), getw("w_out", yg)
    y_conv, zz, merged, o, x2, h2 = _mix_fwd(scv, yg, z, x, w_cp, w_glu, w_out, g1, W["norm2_g"], sc2, sh2)
    w_fi = getw("w_ffn_in", h2)
    f, act = _ffn_in_act(h2, w_fi)
    w_fo = getw("w_ffn_out", act)
    dx3, do2, loss8, dfg8, dg2_8 = _ffn_out_final(x2, act, w_fo, g2, W["final_g"], tgt)

    sm = {}
    tok = put("w_ffn_out", _matmul(act, do2, "tn", 1408, 1024, 2048, BF16, "mm_g_ffn_out"))
    df = _ffn_bwd(do2, w_fo, f, tok)
    tok = put("w_ffn_in", _matmul(h2, df, "tn", 1024, 1408, 2048, BF16, "mm_g_ffn_in"))
    dx2, do, dsh2, dsc2, dn2, dg1_8 = _normmod_bwd(df, w_fi, x2, dx3, W["norm2_g"], sc2, g1, o, tok, "d_h2_normmod2_bwd")
    tok = put("w_out", _matmul(merged, do, "tn", 1024, 1024, 4096, BF16, "mm_g_w_out"))
    dyconv, dgl, dzz = _mix_bwd(do, w_out, z, zz, y_conv, tok)
    tok = put("ssm_glu", _matmul(yg, dzz, "tn", 512, 1024, 4096, BF16, "mm_g_ssm_glu"))
    tok = put("conv_proj", _matmul(scv, dyconv, "tn", 512, 1024, 4096, BF16, "mm_g_conv_proj", after=tok))
    du, de16, dd8, dc_full, dbb_full = _ssm_bwd(dzz, w_glu, ys, z, xs, cmt16, bbt16, W["ssm_d"], tab_b, tok)
    dyc, dlg8, dlb8, dcb8 = _conv_bwd_ln(dyconv, w_cp, yc, W["conv_ln_g"], W["conv_ln_b"])
    dz_conv, dcw = _conv_bwd(dyc, z, W["conv_w"])

    s8 = lambda a: jnp.sum(a, axis=0, keepdims=True)
    de = de16.reshape(2, 8, NST).sum(1)
    de_re, de_im = de[0].reshape(G, P), de[1].reshape(G, P)
    dc_re, dc_im = _diag_blocks(dc_full)
    dc_im = -dc_im
    dbb_re, dbb_im = [jnp.swapaxes(t, 1, 2) for t in _diag_blocks(dbb_full)]
    _, vjp = jax.vjp(_ssm_prep, W["ssm_a_re"], W["ssm_a_im"], W["ssm_b_re"], W["ssm_b_im"], W["ssm_log_dt"])
    sm["ssm_a_re"], sm["ssm_a_im"], sm["ssm_b_re"], sm["ssm_b_im"], sm["ssm_log_dt"] = vjp((de_re, de_im, dbb_re, dbb_im))
    sm["ssm_c_re"], sm["ssm_c_im"] = dc_re, dc_im
    sm["ssm_d"] = s8(dd8)
    sm["norm2_g"] = s8(dn2)
    sm["conv_b"], sm["conv_ln_g"], sm["conv_ln_b"] = s8(dcb8), s8(dlg8), s8(dlb8)
    sm["conv_w"] = dcw.reshape(KW, 8, CW).sum(1)
    sm["final_g"] = s8(dfg8)
    tok = early(sm)

    dz = [dz_conv, du, dgl]
    tok = put("w_in", _matmul(h1, dz, "tn", 1024, 512, 4096, BF16, "mm_g_w_in", after=tok))
    dx, _, dsh1, dsc1, dn1, _ = _normmod_bwd(dz, w_in, x, dx2, W["norm1_g"], sc1, g1, o, tok, "d_h1_normmod1_bwd")
    dmod = jnp.concatenate([s8(dsh1), s8(dsc1), s8(dg1_8), s8(dsh2), s8(dsc2), s8(dg2_8)], axis=1)
    return loss8, dx, s8(dn1), dmod


_BIG = ("w_in", "conv_proj", "ssm_glu", "w_out", "w_ffn_in", "w_ffn_out")
_BIG_AXIS = {"w_in": 1, "conv_proj": 1, "ssm_glu": 1, "w_out": 0, "w_ffn_in": 1, "w_ffn_out": 0}
_EARLY = ("conv_w", "conv_b", "conv_ln_g", "conv_ln_b", "ssm_a_re", "ssm_a_im", "ssm_b_re", "ssm_b_im", "ssm_c_re",
          "ssm_c_im", "ssm_d", "ssm_log_dt", "norm2_g", "final_g")
_LATE = ("norm1_g", "b_ada")
_S5_MATS = ("ssm_a_re", "ssm_a_im", "ssm_b_re", "ssm_b_im", "ssm_c_re", "ssm_c_im")
_ORDER = ("w_ada", "b_ada", "norm1_g", "w_in", "conv_w", "conv_b", "conv_ln_g", "conv_ln_b", "conv_proj",
          "ssm_a_re", "ssm_a_im", "ssm_b_re", "ssm_b_im", "ssm_c_re", "ssm_c_im", "ssm_d", "ssm_log_dt", "ssm_glu",
          "w_out", "norm2_g", "w_ffn_in", "w_ffn_out", "final_g")
_PACK_COLS = 1024


def _pack_rows(shape):
    return -(-int(np.prod(shape)) // (8 * _PACK_COLS)) * 8


def _pack(arrs):
    parts = []
    for a in arrs:
        flat = a.reshape(-1)
        n = _pack_rows(a.shape)
        parts.append(jnp.pad(flat, (0, n * _PACK_COLS - flat.shape[0])).reshape(n, _PACK_COLS))
    return jnp.concatenate(parts, 0)


def kernel(x, c, w_ada, b_ada, norm1_g, w_in, conv_w, conv_b, conv_ln_g, conv_ln_b, conv_proj, ssm_a_re, ssm_a_im, ssm_b_re, ssm_b_im, ssm_c_re, ssm_c_im, ssm_d, ssm_log_dt, ssm_glu, w_out, norm2_g, w_ffn_in, w_ffn_out, final_g, loss_target, m_w_ada, m_b_ada, m_norm1_g, m_w_in, m_conv_w, m_conv_b, m_conv_ln_g, m_conv_ln_b, m_conv_proj, m_ssm_a_re, m_ssm_a_im, m_ssm_b_re, m_ssm_b_im, m_ssm_c_re, m_ssm_c_im, m_ssm_d, m_ssm_log_dt, m_ssm_glu, m_w_out, m_norm2_g, m_w_ffn_in, m_w_ffn_out, m_final_g, v_w_ada, v_b_ada, v_norm1_g, v_w_in, v_conv_w, v_conv_b, v_conv_ln_g, v_conv_ln_b, v_conv_proj, v_ssm_a_re, v_ssm_a_im, v_ssm_b_re, v_ssm_b_im, v_ssm_c_re, v_ssm_c_im, v_ssm_d, v_ssm_log_dt, v_ssm_glu, v_w_out, v_norm2_g, v_w_ffn_in, v_w_ffn_out, v_final_g):
    given = dict(locals())
    mx, my, mc = _me()
    chip = 2 * mx + my
    dev = 4 * mx + 2 * my + mc
    def canon(a):
        return a.reshape(1, -1) if a.ndim <= 2 else a[0]

    wts = {n: canon(given[n]) for n in _ORDER}
    mom = {n: canon(given["m_" + n]) for n in _ORDER}
    var = {n: canon(given["v_" + n]) for n in _ORDER}

    W = {n: wts[n] for n in _ORDER if n not in _BIG}
    rest = [n for n in _BIG if n != "w_in"]
    rest_shards = [wts[n].astype(BF16) for n in rest]
    state_in, token = _half_gather_start(wts["w_in"].astype(BF16), c, "gather_start_w_in")
    W["ssm_log_dt"] = wts["ssm_log_dt"] + token[0:1, 0:1]
    W["ssm_c_re"] = wts["ssm_c_re"] + token[0, 0]
    tables = _ssm_tables(W)

    c_all = _allgather8(jnp.broadcast_to(c, (8, D_MODEL)), "gather_c", after=[*tables, *rest_shards])[:, 0, :]
    n_ada = wts["w_ada"].shape[1]
    b_cols = lax.dynamic_slice(wts["b_ada"], (0, chip * n_ada), (1, n_ada))
    mod_cols = _mod_shard(c_all, wts["w_ada"], b_cols)
    halves = _half_gather_wait(state_in, [mod_cols], "gather_wait_w_in")
    state_in, token = _half_swap_start(halves, "gather_swap_start_w_in")
    mods = _allgather8(mod_cols, "gather_mod", after=[token])
    mod = jnp.concatenate([lax.dynamic_index_in_dim(mods[2 * q], dev, 0, keepdims=True) for q in range(N_CHIP)], axis=1)
    conv_w_full = _allgather8(jnp.pad(wts["conv_w"], ((0, 1), (0, 0))), "gather_conv_w", after=[token])
    W["conv_w"] = jnp.concatenate([conv_w_full[2 * q, :KW] for q in range(N_CHIP)], axis=1)
    w_in_full = _half_swap_wait(state_in, mod + W["conv_w"][0:1, 0:1], "gather_swap_wait_w_in")
    gstate, token = _gather_start(rest_shards, [_BIG_AXIS[n] for n in rest], w_in_full, "gather_start_rest")
    gstate = dict(zip(rest, gstate))
    mod = mod + token[0:1, 0:1]

    def getw(n, after):
        if n == "w_in":
            return w_in_full
        return _gather_wait(gstate[n], _BIG_AXIS[n], after, "gather_wait_" + n)

    sstate, estate = {}, []

    def put(n, g):
        sstate[n], tok = _scatter_start(g, _BIG_AXIS[n], "scatter_start_" + n)
        return tok

    first5 = [n for n in _BIG if n != "w_in"]

    def early(sm):
        state, tok = _all8_start(_pack([sm[n] for n in _EARLY]), "small_start")
        estate.append(state)
        held = [_scatter_wait(sstate[n], _BIG_AXIS[n], tok, "scatter_wait_" + n) for n in first5]
        state, tok = _swap_start(held, tok, "swap_start")
        estate.append(state)
        return tok

    loss8, dx, dn1, dmod = _device_step(x[0], mod, W, tables, loss_target[0], getw, put, early)

    held5, sib5 = _swap_wait(estate[1], dx, "swap_wait")
    outs = {}
    for i, n in enumerate(first5):
        outs[n] = _adamw(wts[n], mom[n], var[n], [[held5[i]], [sib5[i]]], "adamw_" + n)
    allp = _all8_wait(estate[0], dx, "small_wait")
    upd, sums = _adamw_small(allp, _EARLY, wts, mom, var, ("conv_w",) + _S5_MATS, "adamw_small")
    outs.update(upd)

    def swapped(n, a):
        return jnp.swapaxes(a, 1, 2) if n in ("ssm_b_re", "ssm_b_im") else a

    def summed(n):
        return swapped(n, sums[n].reshape(-1)[:wts[n].size].reshape(wts[n].shape))

    res = _adamw_native([(swapped(n, wts[n]), swapped(n, mom[n]), swapped(n, var[n]), summed(n)) for n in _S5_MATS],
                        "adamw_s5")
    for n, r in zip(_S5_MATS, res):
        outs[n] = tuple(swapped(n, a) for a in r)

    late = _allgather8(_pack([dn1, dmod, loss8]), "gather_late", after=[outs[n][1] for n in first5])
    n_late = _pack_rows((D_MODEL,)) + _pack_rows((6 * D_MODEL,))
    loss = jnp.sum(late[:, n_late:, :])
    late = late[:, :n_late, :]
    held_in = _scatter_wait(sstate["w_in"], _BIG_AXIS["w_in"], late, "scatter_wait_w_in")
    state_in, tok = _swap_start([held_in], late, "swap_start_w_in")

    r1 = _pack_rows((D_MODEL,))
    dmod_all = late[:, r1:, :].reshape(N_DEV, -1)[:, :6 * D_MODEL]
    dmod_cols = lax.dynamic_slice(dmod_all, (0, chip * n_ada), (N_DEV, n_ada))
    g_ada = _ada_grad(c_all, dmod_cols, tok)
    outs["w_ada"] = _adamw(wts["w_ada"], mom["w_ada"], var["w_ada"], [[g_ada]], "adamw_w_ada")
    upd, _ = _adamw_small(late, _LATE, wts, mom, var, (), "adamw_late")
    outs.update(upd)
    held_in, sib_in = _swap_wait(state_in, outs["w_ada"][1], "swap_wait_w_in")
    outs["w_in"] = _adamw(wts["w_in"], mom["w_in"], var["w_in"], [held_in, sib_in], "adamw_w_in")
    g_cw_full = sums["conv_w"].reshape(-1)[:KW * CW].reshape(KW, CW)
    g_cw = lax.dynamic_slice(g_cw_full, (0, chip * (CW // N_CHIP)), (KW, CW // N_CHIP))
    pad = lambda a: jnp.pad(a, ((0, 1), (0, 0)))
    r_cw = _adamw(pad(wts["conv_w"]), pad(mom["conv_w"]), pad(var["conv_w"]), [[pad(g_cw)]], "adamw_conv_w")
    outs["conv_w"] = tuple(r[:KW] for r in r_cw)

    def shaped(n, a):
        return a.reshape(given[n].shape)

    result = [loss, dx[None]]
    for q in range(4):
        result += [shaped(n, outs[n][q]) for n in _ORDER]
    return tuple(result)
```

```python
import math

import jax
import jax.numpy as jnp
import numpy as np
from jax import lax
from jax.experimental import pallas as pl
from jax.experimental.pallas import tpu as pltpu

F32 = jnp.float32
BF16 = jnp.bfloat16
EPS = 1e-6
D_MODEL = 1024
CW = 512
KW = 31
HALO = 32
G, P, H = 32, 64, 16
NST = G * P
FH = 2816
N_DEV = 8
N_CHIP = 4
VMEM_LIMIT = 56 * 1024 * 1024
LR, B1, B2, AEPS, WD, STEP = 0.001, 0.9, 0.999, 1e-08, 0.01, 10
MESH = pl.DeviceIdType.MESH


def _cp(sem=None):
    return pltpu.CompilerParams(dimension_semantics=sem, vmem_limit_bytes=VMEM_LIMIT)


def _sig(x):
    return jax.nn.sigmoid(x)


def _full(shape):
    return pl.BlockSpec(shape, lambda *_: (0,) * len(shape))


def _resident(shape):
    return pl.BlockSpec(shape, lambda *_: (0,) * len(shape), pipeline_mode=pl.Buffered(1))


def _colsum8(v):
    t, c = v.shape
    return jnp.sum(v.reshape(t // 8, 8, c), axis=0)


def _matmul(a, b, mode, tm, tn, tk, out_dtype, name, after=None, n_outer=False, m_cols=None):
    m0 = 0
    b_parts = list(b) if isinstance(b, (list, tuple)) else [b]
    if mode == "nn":
        (M, K), N = a.shape, sum(p.shape[1] for p in b_parts)
    elif mode == "nt":
        (M, K), N = a.shape, b.shape[0]
    else:
        (K, M), N = a.shape, sum(p.shape[1] for p in b_parts)
        if m_cols is not None:
            m0, M = m_cols
    tm, tn, tk = min(tm, M), min(tn, N), min(tk, K)
    assert M % tm == 0 and N % tn == 0 and K % tk == 0 and m0 % tm == 0, (name, M, N, K, tm, tn, tk)
    assert len(b_parts) == 1 or (mode != "nt" and all(p.shape[1] % tn == 0 for p in b_parts)), name
    nk = K // tk
    mb = m0 // tm
    counts = [p.shape[1] // tn for p in b_parts] if mode != "nt" else [N // tn]
    starts = [sum(counts[:p]) for p in range(len(counts))]

    def ij(fn):
        return (lambda j, i, k: fn(i, j, k)) if n_outer else fn

    if mode == "nn":
        a_spec = pl.BlockSpec((tm, tk), ij(lambda i, j, k: (i, k)))
        b_spec = pl.BlockSpec((tk, tn), ij(lambda i, j, k: (k, j)))
        dims = (((1,), (0,)), ((), ()))
    elif mode == "nt":
        a_spec = pl.BlockSpec((tm, tk), ij(lambda i, j, k: (i, k)))
        b_spec = pl.BlockSpec((tn, tk), ij(lambda i, j, k: (j, k)))
        dims = (((1,), (1,)), ((), ()))
    else:
        a_spec = pl.BlockSpec((tk, tm), ij(lambda i, j, k: (k, i + mb)))
        dims = (((0,), (0,)), ((), ()))
    if mode != "nt":
        b_specs = [pl.BlockSpec((tk, tn), ij(lambda i, j, k, s=s, n=n: (k, jnp.clip(j - s, 0, n - 1))))
                   for s, n in zip(starts, counts)]
    else:
        b_specs = [b_spec]
    nb = len(b_parts)

    def body(a_ref, *rest):
        b_refs = rest[:nb]
        o_ref, acc_ref = rest[-2:]
        j = pl.program_id(0 if n_outer else 1)
        k = pl.program_id(2)

        def compute(b_ref):
            part = lax.dot_general(a_ref[...].astype(BF16), b_ref[...].astype(BF16), dims,
                                   preferred_element_type=F32)
            if nk == 1:
                o_ref[...] = part.astype(out_dtype)
            else:
                @pl.when(k == 0)
                def _():
                    acc_ref[...] = part

                @pl.when(k > 0)
                def _():
                    acc_ref[...] += part

                @pl.when(k == nk - 1)
                def _():
                    o_ref[...] = acc_ref[...].astype(out_dtype)

        if nb == 1:
            compute(b_refs[0])
        else:
            for p in range(nb):
                pl.when(jnp.logical_and(j >= starts[p], j < starts[p] + counts[p]))(
                    lambda b_ref=b_refs[p]: compute(b_ref))

    return pl.pallas_call(
        body, name=name,
        out_shape=jax.ShapeDtypeStruct((M, N), out_dtype),
        grid=(N // tn, M // tm, nk) if n_outer else (M // tm, N // tn, nk),
        in_specs=[a_spec] + b_specs + ([] if after is None else [pl.BlockSpec(memory_space=pl.ANY)]),
        out_specs=pl.BlockSpec((tm, tn), ij(lambda i, j, k: (i, j))),
        scratch_shapes=[pltpu.VMEM((tm, tn) if nk > 1 else (8, 128), F32)],
        compiler_params=_cp(("parallel", "parallel", "arbitrary")),
    )(*([a] + b_parts + ([] if after is None else [after])))


def _row_tile(S):
    return min(512, S)


def _in_proj(x, g, sc, sh, w_in):
    S, D = x.shape
    N = w_in.shape[1]
    tm = min(512, S)

    def body(x_ref, g_ref, sc_ref, sh_ref, w_ref, ht_ref, z_ref):
        xv = x_ref[...]
        r = lax.rsqrt(jnp.mean(xv * xv, axis=-1, keepdims=True) + EPS)
        h = xv * r * (g_ref[...] * (1.0 + sc_ref[...])) + sh_ref[...]
        ht_ref[...] = h.T.astype(BF16)
        z_ref[...] = jnp.dot(h.astype(BF16), w_ref[...], preferred_element_type=F32).astype(BF16)

    row = pl.BlockSpec((tm, D), lambda i: (i, 0))
    par = _full((1, D))
    return pl.pallas_call(
        body, name="in_proj",
        out_shape=(jax.ShapeDtypeStruct((D, S), BF16), jax.ShapeDtypeStruct((S, N), BF16)), grid=(S // tm,),
        in_specs=[row, par, par, par, _resident((D, N))],
        out_specs=(pl.BlockSpec((D, tm), lambda i: (0, i)), pl.BlockSpec((tm, N), lambda i: (i, 0))),
        compiler_params=_cp(("parallel",)))(x, g, sc, sh, w_in)


def _fill_shifted(buf_ref, sh_ref):
    n = buf_ref.shape[0] - 8
    for s in range(1, 8):
        sh_ref[s, 0:n, :] = buf_ref[s:s + n, :]


def _window(buf_ref, sh_ref, off, n):
    s = off % 8
    return buf_ref[off:off + n, :] if s == 0 else sh_ref[s, off - s:off - s + n, :]


def _conv_fwd(z, conv_w, conv_b, ln_g, ln_b):
    S = z.shape[0]
    tm = min(128, S)
    sub = 32
    hb = tm // HALO

    def body(a_ref, g_ref, ha_ref, hg_ref, w_ref, b_ref, lg_ref, lb_ref, yc_ref, s_ref, ug_ref, sh_ref):
        i = pl.program_id(0)
        halo = ha_ref[...].astype(F32) * _sig(hg_ref[...].astype(F32))
        ug_ref[0:HALO, :] = jnp.where(i == 0, 0.0, halo)
        ug_ref[HALO:, :] = a_ref[...].astype(F32) * _sig(g_ref[...].astype(F32))
        _fill_shifted(ug_ref, sh_ref)
        for rb in range(tm // sub):
            acc = jnp.zeros((sub, CW), F32) + b_ref[...]
            for k in range(KW):
                off = rb * sub + HALO - (KW - 1) + k
                acc = acc + w_ref[k:k + 1, :] * _window(ug_ref, sh_ref, off, sub)
            yc_ref[rb * sub:(rb + 1) * sub, :] = acc
            mu = jnp.mean(acc, axis=-1, keepdims=True)
            cen = acc - mu
            rstd = lax.rsqrt(jnp.mean(cen * cen, axis=-1, keepdims=True) + EPS)
            ln = cen * rstd * lg_ref[...] + lb_ref[...]
            s_ref[rb * sub:(rb + 1) * sub, :] = (ln * _sig(ln)).astype(BF16)

    prev = lambda i: (jnp.maximum(i * hb - 1, 0), 0)
    return pl.pallas_call(
        body, name="conv_fwd",
        out_shape=(jax.ShapeDtypeStruct((S, CW), F32), jax.ShapeDtypeStruct((S, CW), BF16)),
        grid=(S // tm,),
        in_specs=[pl.BlockSpec((tm, CW), lambda i: (i, 0)), pl.BlockSpec((tm, CW), lambda i: (i, 1)),
                  pl.BlockSpec((HALO, CW), prev), pl.BlockSpec((HALO, CW), lambda i: (jnp.maximum(i * hb - 1, 0), 1)),
                  _full((KW, CW)), _full((1, CW)), _full((1, CW)), _full((1, CW))],
        out_specs=(pl.BlockSpec((tm, CW), lambda i: (i, 0)), pl.BlockSpec((tm, CW), lambda i: (i, 0))),
        scratch_shapes=[pltpu.VMEM((tm + HALO, CW), F32), pltpu.VMEM((8, tm + HALO, CW), F32)],
        compiler_params=_cp(("parallel",)))(z, z, z, z, conv_w, conv_b, ln_g, ln_b)


def _conv_bwd_ln(dyconv, w_cp, yc, ln_g, ln_b):
    S = yc.shape[0]
    tm = _row_tile(S)

    def body(dy_ref, w_ref, yc_ref, lg_ref, lb_ref, dyc_ref, dlg_ref, dlb_ref, dcb_ref):
        i = pl.program_id(0)
        dsc = lax.dot_general(dy_ref[...], w_ref[...], (((1,), (1,)), ((), ())), preferred_element_type=F32)
        yc_v = yc_ref[...]
        mu = jnp.mean(yc_v, axis=-1, keepdims=True)
        cen = yc_v - mu
        rstd = lax.rsqrt(jnp.mean(cen * cen, axis=-1, keepdims=True) + EPS)
        yn = cen * rstd
        ln = yn * lg_ref[...] + lb_ref[...]
        sl = _sig(ln)
        dln = dsc * (sl * (1.0 + ln * (1.0 - sl)))
        dyn = dln * lg_ref[...]
        dyc = rstd * (dyn - jnp.mean(dyn, axis=-1, keepdims=True)
                      - yn * jnp.mean(dyn * yn, axis=-1, keepdims=True))
        dyc_ref[...] = dyc

        @pl.when(i == 0)
        def _():
            dlg_ref[...] = jnp.zeros_like(dlg_ref)
            dlb_ref[...] = jnp.zeros_like(dlb_ref)
            dcb_ref[...] = jnp.zeros_like(dcb_ref)

        dlg_ref[...] += _colsum8(dln * yn)
        dlb_ref[...] += _colsum8(dln)
        dcb_ref[...] += _colsum8(dyc)

    row = pl.BlockSpec((tm, CW), lambda i: (i, 0))
    acc = jax.ShapeDtypeStruct((8, CW), F32)
    return pl.pallas_call(
        body, name="conv_bwd_ln",
        out_shape=(jax.ShapeDtypeStruct((S, CW), F32), acc, acc, acc), grid=(S // tm,),
        in_specs=[pl.BlockSpec((tm, D_MODEL), lambda i: (i, 0)), _full((CW, D_MODEL)), row, _full((1, CW)),
                  _full((1, CW))],
        out_specs=(row, _full((8, CW)), _full((8, CW)), _full((8, CW))),
        compiler_params=_cp(("arbitrary",)))(dyconv, w_cp, yc, ln_g, ln_b)


def _conv_bwd(dyc, z, conv_w):
    S = z.shape[0]
    tm = min(128, S)
    sub = 32
    hb = tm // HALO
    nt = S // tm

    def body(d_ref, dn_ref, a_ref, g_ref, ha_ref, hg_ref, w_ref, dz_ref, dw_ref, ug_ref, dy_ref, ugs_ref, dys_ref):
        i = pl.program_id(0)
        halo = ha_ref[...].astype(F32) * _sig(hg_ref[...].astype(F32))
        ug_ref[0:HALO, :] = jnp.where(i == 0, 0.0, halo)
        a = a_ref[...].astype(F32)
        sg = _sig(g_ref[...].astype(F32))
        ug_ref[HALO:, :] = a * sg
        dy_ref[0:tm, :] = d_ref[...]
        dy_ref[tm:, :] = jnp.where(i == nt - 1, 0.0, dn_ref[...])
        _fill_shifted(ug_ref, ugs_ref)
        _fill_shifted(dy_ref, dys_ref)

        @pl.when(i == 0)
        def _():
            dw_ref[...] = jnp.zeros_like(dw_ref)

        for rb in range(tm // sub):
            r0 = rb * sub
            acc = jnp.zeros((sub, CW), F32)
            dyc_b = dy_ref[r0:r0 + sub, :]
            for k in range(KW):
                up = r0 + (KW - 1) - k
                acc = acc + w_ref[k:k + 1, :] * _window(dy_ref, dys_ref, up, sub)
                off = r0 + HALO - (KW - 1) + k
                dw_ref[k * 8:(k + 1) * 8, :] += _colsum8(dyc_b * _window(ug_ref, ugs_ref, off, sub))
            a_b = a[r0:r0 + sub, :]
            sg_b = sg[r0:r0 + sub, :]
            dz_ref[r0:r0 + sub, 0:CW] = (acc * sg_b).astype(BF16)
            dz_ref[r0:r0 + sub, CW:2 * CW] = (acc * a_b * sg_b * (1.0 - sg_b)).astype(BF16)

    return pl.pallas_call(
        body, name="conv_bwd",
        out_shape=(jax.ShapeDtypeStruct((S, 2 * CW), BF16), jax.ShapeDtypeStruct((KW * 8, CW), F32)),
        grid=(nt,),
        in_specs=[pl.BlockSpec((tm, CW), lambda i: (i, 0)),
                  pl.BlockSpec((HALO, CW), lambda i: (jnp.minimum((i + 1) * hb, nt * hb - 1), 0)),
                  pl.BlockSpec((tm, CW), lambda i: (i, 0)), pl.BlockSpec((tm, CW), lambda i: (i, 1)),
                  pl.BlockSpec((HALO, CW), lambda i: (jnp.maximum(i * hb - 1, 0), 0)),
                  pl.BlockSpec((HALO, CW), lambda i: (jnp.maximum(i * hb - 1, 0), 1)),
                  _full((KW, CW))],
        out_specs=(pl.BlockSpec((tm, 2 * CW), lambda i: (i, 0)), _full((KW * 8, CW))),
        scratch_shapes=[pltpu.VMEM((tm + HALO, CW), F32), pltpu.VMEM((tm + HALO, CW), F32),
                        pltpu.VMEM((8, tm + HALO, CW), F32), pltpu.VMEM((8, tm + HALO, CW), F32)],
        compiler_params=_cp(("arbitrary",)))(dyc, dyc, z, z, z, z, conv_w)


_GELU_C = math.sqrt(2.0 / math.pi)


def _gelu(x):
    return 0.5 * x * (1.0 + jnp.tanh(_GELU_C * (x + 0.044715 * x * x * x)))


def _gelu_grad(x):
    t = jnp.tanh(_GELU_C * (x + 0.044715 * x * x * x))
    return 0.5 * (1.0 + t) + 0.5 * x * (1.0 - t * t) * (_GELU_C * (1.0 + 3 * 0.044715 * x * x))


_NCL = 4
_UC = CW // _NCL
_LW = NST // _NCL
_CS = 2 * _LW


def _ssm_fwd(z, bb, cm, d, tab):
    S = z.shape[0]
    tm = min(512, S)

    def body(u_ref, bb_ref, cm_ref, d_ref, t_ref, x_ref, ys_ref, yg_ref, car_ref):
        i = pl.program_id(0)

        @pl.when(i == 0)
        def _():
            car_ref[...] = jnp.zeros_like(car_ref)

        u16 = u_ref[...]
        u = u16.astype(F32)
        for c in range(_NCL):
            lre = pl.ds(c * _CS, _LW)
            lim = pl.ds(c * _CS + _LW, _LW)
            tl = pl.ds(c * _LW, _LW)
            x_ref[:, c * _CS:(c + 1) * _CS] = jnp.dot(u16[:, c * _UC:(c + 1) * _UC], bb_ref[c],
                                                      preferred_element_type=F32)

            def blk(j, car):
                cr, ci = car
                rows = pl.ds(pl.multiple_of(j * 8, 8), 8)
                r = x_ref[rows, lre]
                im = x_ref[rows, lim]
                for lvl, s in enumerate((1, 2, 4)):
                    mr = t_ref[16 * lvl:16 * lvl + 8, tl]
                    mi = t_ref[16 * lvl + 8:16 * lvl + 16, tl]
                    sr = pltpu.roll(r, s, 0)
                    si = pltpu.roll(im, s, 0)
                    r, im = r + (mr * sr - mi * si), im + (mr * si + mi * sr)
                pr = t_ref[48:56, tl]
                pi_ = t_ref[56:64, tl]
                r, im = r + (pr * cr - pi_ * ci), im + (pr * ci + pi_ * cr)
                x_ref[rows, lre] = r
                x_ref[rows, lim] = im
                return (jnp.broadcast_to(r[7:8, :], (8, _LW)), jnp.broadcast_to(im[7:8, :], (8, _LW)))

            cr, ci = lax.fori_loop(0, tm // 8, blk, (car_ref[:, lre], car_ref[:, lim]))
            car_ref[:, lre] = cr
            car_ref[:, lim] = ci
            cols = slice(c * _UC, (c + 1) * _UC)
            ys = jnp.dot(x_ref[:, c * _CS:(c + 1) * _CS].astype(BF16), cm_ref[c], preferred_element_type=F32)
            ys = ys + d_ref[:, cols] * u[:, cols]
            ys_ref[:, cols] = ys
            yg_ref[:, cols] = _gelu(ys).astype(BF16)

    return pl.pallas_call(
        body, name="ssm_fwd",
        out_shape=(jax.ShapeDtypeStruct((S, 2 * NST), F32), jax.ShapeDtypeStruct((S, CW), F32),
                   jax.ShapeDtypeStruct((S, CW), BF16)),
        grid=(S // tm,),
        in_specs=[pl.BlockSpec((tm, CW), lambda i: (i, 2)), _full((_NCL, _UC, _CS)), _full((_NCL, _CS, _UC)),
                  _full((1, CW)), _full((64, NST))],
        out_specs=(pl.BlockSpec((tm, 2 * NST), lambda i: (i, 0)), pl.BlockSpec((tm, CW), lambda i: (i, 0)),
                   pl.BlockSpec((tm, CW), lambda i: (i, 0))),
        scratch_shapes=[pltpu.VMEM((8, 2 * NST), F32)],
        compiler_params=_cp(("arbitrary",)))(z, bb, cm, d, tab)


def _ssm_bwd(dzz, w_glu, ys, z, xs, cmt, bbt, d, tab, after):
    S = z.shape[0]
    tm = min(512, S)
    nt = S // tm
    tdims = (((0,), (0,)), ((), ()))

    def body(dzz_ref, wglu_ref, ys_ref, u_ref, x_ref, cmt_ref, bbt_ref, d_ref, t_ref, after_ref,
             du_ref, de_ref, dd_ref, dc_hbm, dbb_hbm, car_ref, lam_ref, dc_ref, dbb_ref):
        i = pl.program_id(0)

        @pl.when(i == 0)
        def _():
            car_ref[...] = jnp.zeros_like(car_ref)
            de_ref[...] = jnp.zeros_like(de_ref)
            dd_ref[...] = jnp.zeros_like(dd_ref)
            dc_ref[...] = jnp.zeros_like(dc_ref)
            dbb_ref[...] = jnp.zeros_like(dbb_ref)

        u16 = u_ref[...]
        u = u16.astype(F32)
        dyg = lax.dot_general(dzz_ref[...], wglu_ref[...], (((1,), (1,)), ((), ())), preferred_element_type=F32)
        dys = dyg * _gelu_grad(ys_ref[...])
        dys16 = dys.astype(BF16)
        dd_ref[...] += _colsum8(dys * u)
        row = lax.broadcasted_iota(jnp.int32, (8, _LW), 0)
        for c in range(_NCL):
            lre = pl.ds(c * _CS, _LW)
            lim = pl.ds(c * _CS + _LW, _LW)
            tl = pl.ds(c * _LW, _LW)
            cols = slice(c * _UC, (c + 1) * _UC)
            span = slice(c * _CS, (c + 1) * _CS)
            dc_ref[cols, :] += lax.dot_general(dys16[:, cols], x_ref[:, span].astype(BF16), tdims,
                                               preferred_element_type=F32)
            lam_ref[...] = jnp.dot(dys16[:, cols], cmt_ref[c], preferred_element_type=F32)

            def blk(jj, car):
                cr, ci, ar, ai = car
                j = tm // 8 - 1 - jj
                rows = pl.ds(pl.multiple_of(j * 8, 8), 8)
                r = lam_ref[rows, 0:_LW]
                im = lam_ref[rows, _LW:_CS]
                for lvl, s in enumerate((1, 2, 4)):
                    mr = t_ref[16 * lvl:16 * lvl + 8, tl]
                    mi = t_ref[16 * lvl + 8:16 * lvl + 16, tl]
                    sr = pltpu.roll(r, 8 - s, 0)
                    si = pltpu.roll(im, 8 - s, 0)
                    r, im = r + (mr * sr - mi * si), im + (mr * si + mi * sr)
                pr = t_ref[48:56, tl]
                pi_ = t_ref[56:64, tl]
                r, im = r + (pr * cr - pi_ * ci), im + (pr * ci + pi_ * cr)
                lam_ref[rows, 0:_LW] = r
                lam_ref[rows, _LW:_CS] = im
                nr = jnp.where(row == 7, cr, pltpu.roll(r, 7, 0))
                ni = jnp.where(row == 7, ci, pltpu.roll(im, 7, 0))
                xr = x_ref[rows, lre]
                xi = x_ref[rows, lim]
                ar = ar + (nr * xr + ni * xi)
                ai = ai + (ni * xr - nr * xi)
                return (jnp.broadcast_to(r[0:1, :], (8, _LW)), jnp.broadcast_to(im[0:1, :], (8, _LW)), ar, ai)

            zero = jnp.zeros((8, _LW), F32)
            cr, ci, ar, ai = lax.fori_loop(0, tm // 8, blk, (car_ref[:, lre], car_ref[:, lim], zero, zero))
            car_ref[:, lre] = cr
            car_ref[:, lim] = ci
            de_ref[0:8, tl] += ar
            de_ref[8:16, tl] += ai
            lam16 = lam_ref[...].astype(BF16)
            dbb_ref[cols, :] += lax.dot_general(u16[:, cols], lam16, tdims, preferred_element_type=F32)
            du = jnp.dot(lam16, bbt_ref[c], preferred_element_type=F32) + dys[:, cols] * d_ref[:, cols]
            du_ref[:, cols] = du.astype(BF16)

        @pl.when(i == nt - 1)
        def _():
            pltpu.sync_copy(dc_ref, dc_hbm)
            pltpu.sync_copy(dbb_ref, dbb_hbm)

    rev = lambda i: (nt - 1 - i, 0)
    once = lambda shape: pl.BlockSpec(shape, lambda *_: (0,) * len(shape), pipeline_mode=pl.Buffered(1))
    cross = jax.ShapeDtypeStruct((CW, _CS), F32)
    return pl.pallas_call(
        body, name="ssm_bwd",
        out_shape=(jax.ShapeDtypeStruct((S, CW), BF16), jax.ShapeDtypeStruct((16, NST), F32),
                   jax.ShapeDtypeStruct((8, CW), F32), cross, cross),
        grid=(nt,),
        in_specs=[pl.BlockSpec((tm, 2 * D_MODEL), rev), once((CW, 2 * D_MODEL)), pl.BlockSpec((tm, CW), rev),
                  pl.BlockSpec((tm, CW), lambda i: (nt - 1 - i, 2)), pl.BlockSpec((tm, 2 * NST), rev),
                  once((_NCL, _UC, _CS)), once((_NCL, _CS, _UC)), _full((1, CW)), once((64, NST)),
                  pl.BlockSpec(memory_space=pl.ANY)],
        out_specs=(pl.BlockSpec((tm, CW), rev), _full((16, NST)), _full((8, CW)),
                   pl.BlockSpec(memory_space=pl.ANY), pl.BlockSpec(memory_space=pl.ANY)),
        scratch_shapes=[pltpu.VMEM((8, 2 * NST), F32), pltpu.VMEM((tm, _CS), F32),
                        pltpu.VMEM((CW, _CS), F32), pltpu.VMEM((CW, _CS), F32)],
        compiler_params=_cp(("arbitrary",)))(dzz, w_glu, ys, z, xs, cmt, bbt, d, tab, after)


def _ssm_prep(a_re, a_im, b_re, b_im, log_dt):
    dt = jnp.exp(log_dt.reshape(G))[:, None]
    mag = jnp.exp(dt * a_re)
    e_re, e_im = mag * jnp.cos(dt * a_im), mag * jnp.sin(dt * a_im)
    n_re, n_im = e_re - 1.0, e_im
    den = a_re * a_re + a_im * a_im
    q_re = (n_re * a_re + n_im * a_im) / den
    q_im = (n_im * a_re - n_re * a_im) / den
    bb_re = q_re[..., None] * b_re - q_im[..., None] * b_im
    bb_im = q_re[..., None] * b_im + q_im[..., None] * b_re
    return e_re, e_im, bb_re, bb_im


def _scan_tables(e_re, e_im, reverse):
    er = e_re.reshape(1, NST)
    ei = e_im.reshape(1, NST)
    if reverse:
        ei = -ei
    pows = [(er, ei)]
    for _ in range(7):
        pr, pi_ = pows[-1]
        pows.append((pr * er - pi_ * ei, pr * ei + pi_ * er))
    row = jnp.arange(8)[:, None]
    out = []
    for s in (1, 2, 4):
        pr, pi_ = pows[s - 1]
        keep = (row + s <= 7) if reverse else (row >= s)
        out += [jnp.where(keep, pr, 0.0), jnp.where(keep, pi_, 0.0)]
    allr = jnp.concatenate([p[0] for p in pows], 0)
    alli = jnp.concatenate([p[1] for p in pows], 0)
    if reverse:
        allr, alli = allr[::-1], alli[::-1]
    out += [allr, alli]
    return jnp.concatenate(out, 0).astype(F32)


def _block_diag_mats(bb_re, bb_im, c_re, c_im):
    gc = G // _NCL
    eye = jnp.eye(gc, dtype=F32)
    bre = jnp.einsum("cjph,jk->cjhkp", bb_re.reshape(_NCL, gc, P, H), eye).reshape(_NCL, _UC, _LW)
    bim = jnp.einsum("cjph,jk->cjhkp", bb_im.reshape(_NCL, gc, P, H), eye).reshape(_NCL, _UC, _LW)
    bb = jnp.concatenate([bre, bim], 2)
    cre = jnp.einsum("cjhp,jk->cjpkh", c_re.reshape(_NCL, gc, H, P), eye).reshape(_NCL, _LW, _UC)
    cim = jnp.einsum("cjhp,jk->cjpkh", c_im.reshape(_NCL, gc, H, P), eye).reshape(_NCL, _LW, _UC)
    cm = jnp.concatenate([cre, -cim], 1)
    return bb, cm


def _diag_blocks(cross):
    gc = G // _NCL
    six = cross.reshape(_NCL, gc, H, 2, gc, P)
    same = jnp.eye(gc, dtype=bool)[None, :, None, None, :, None]
    diag = jnp.sum(jnp.where(same, six, 0.0), axis=4)
    diag = jnp.moveaxis(diag, 3, 0).reshape(2, G, H, P)
    return diag[0], diag[1]


def _mix_fwd(scv, yg, z, x, w_cp, w_glu, w_out, g1, n2g, sc2, sh2):
    S = z.shape[0]
    tm = min(512, S)
    D = D_MODEL

    def body(s_ref, yg_ref, glc0_ref, glc1_ref, gls0_ref, gls1_ref, x_ref, wcp_ref, wglu_ref, wout_ref,
             g1_ref, n2_ref, sc_ref, sh_ref, yc_ref, zz_ref, m_ref, o_ref, x2_ref, h2_ref):
        y_conv = jnp.dot(s_ref[...], wcp_ref[...], preferred_element_type=F32)
        zz = jnp.dot(yg_ref[...], wglu_ref[...], preferred_element_type=F32)
        yc_ref[...] = y_conv.astype(BF16)
        zz_ref[...] = zz.astype(BF16)
        for half, (glc_ref, gls_ref) in enumerate(((glc0_ref, gls0_ref), (glc1_ref, gls1_ref))):
            lo, hi = half * CW, (half + 1) * CW
            y_ssm = zz[:, lo:hi] * _sig(zz[:, D + lo:D + hi])
            m_ref[:, lo:hi] = (_sig(glc_ref[...].astype(F32)) * y_conv[:, lo:hi]
                               + _sig(gls_ref[...].astype(F32)) * y_ssm).astype(BF16)
        o = jnp.dot(m_ref[...], wout_ref[...], preferred_element_type=F32)
        o_ref[...] = o.astype(BF16)
        xv = x_ref[...] + g1_ref[...] * o
        x2_ref[...] = xv
        r = lax.rsqrt(jnp.mean(xv * xv, axis=-1, keepdims=True) + EPS)
        h2_ref[...] = (xv * r * (n2_ref[...] * (1.0 + sc_ref[...])) + sh_ref[...]).astype(BF16)

    zb_ = lambda j: pl.BlockSpec((tm, CW), lambda i: (i, j))
    row = lambda w: pl.BlockSpec((tm, w), lambda i: (i, 0))
    par = _full((1, D))
    bf = lambda w: jax.ShapeDtypeStruct((S, w), BF16)
    return pl.pallas_call(
        body, name="mix_fwd",
        out_shape=(bf(D), bf(2 * D), bf(D), bf(D), jax.ShapeDtypeStruct((S, D), F32), bf(D)),
        grid=(S // tm,),
        in_specs=[row(CW), row(CW), zb_(3), zb_(4), zb_(5), zb_(6), row(D), _resident((CW, D)),
                  _resident((CW, 2 * D)), _resident((D, D)), par, par, par, par],
        out_specs=(row(D), row(2 * D), row(D), row(D), row(D), row(D)),
        compiler_params=_cp(("parallel",)))(scv, yg, z, z, z, z, x, w_cp, w_glu, w_out, g1, n2g, sc2, sh2)


def _mix_bwd(do, w_out, z, zz, y_conv, after):
    S = z.shape[0]
    tm = min(512, S)
    D = D_MODEL

    def body(do_ref, w_ref, glc0_ref, glc1_ref, gls0_ref, gls1_ref, za_ref, zb_ref, yc_ref, after_ref,
             dyc_ref, dgl_ref, dzz_ref):
        dm = lax.dot_general(do_ref[...], w_ref[...], (((1,), (1,)), ((), ())), preferred_element_type=F32)
        for half, (glc_ref, gls_ref) in enumerate(((glc0_ref, gls0_ref), (glc1_ref, gls1_ref))):
            lo, hi = half * CW, (half + 1) * CW
            dm_v = dm[:, lo:hi]
            sgc = _sig(glc_ref[...].astype(F32))
            sgs = _sig(gls_ref[...].astype(F32))
            szb = _sig(zb_ref[:, lo:hi].astype(F32))
            za = za_ref[:, lo:hi].astype(F32)
            dyc_ref[:, lo:hi] = (dm_v * sgc).astype(BF16)
            dgl_ref[:, lo:hi] = (dm_v * yc_ref[:, lo:hi].astype(F32) * sgc * (1.0 - sgc)).astype(BF16)
            dys = dm_v * sgs
            dgl_ref[:, D + lo:D + hi] = (dys * (za * szb) * (1.0 - sgs)).astype(BF16)
            dzz_ref[:, lo:hi] = (dys * szb).astype(BF16)
            dzz_ref[:, D + lo:D + hi] = (dys * za * szb * (1.0 - szb)).astype(BF16)

    zb_ = lambda j: pl.BlockSpec((tm, CW), lambda i: (i, j))
    wide = lambda j: pl.BlockSpec((tm, D), lambda i: (i, j))
    return pl.pallas_call(
        body, name="mix_bwd",
        out_shape=(jax.ShapeDtypeStruct((S, D), BF16), jax.ShapeDtypeStruct((S, 2 * D), BF16),
                   jax.ShapeDtypeStruct((S, 2 * D), BF16)),
        grid=(S // tm,),
        in_specs=[wide(0), _resident((D, D)), zb_(3), zb_(4), zb_(5), zb_(6), wide(0), wide(1), wide(0),
                  pl.BlockSpec(memory_space=pl.ANY)],
        out_specs=(wide(0), pl.BlockSpec((tm, 2 * D), lambda i: (i, 0)), pl.BlockSpec((tm, 2 * D), lambda i: (i, 0))),
        compiler_params=_cp(("parallel",)))(do, w_out, z, z, z, z, zz, zz, y_conv, after)


_FC = 1408


def _ffn_in_act(h2, w_fi):
    S, D = h2.shape
    tm = min(512, S)

    def body(h_ref, w_ref, f_ref, a_ref):
        hv = h_ref[...]
        for c in range(FH // _FC):
            lo, hi = c * _FC, (c + 1) * _FC
            g = jnp.dot(hv, w_ref[:, lo:hi], preferred_element_type=F32)
            u = jnp.dot(hv, w_ref[:, FH + lo:FH + hi], preferred_element_type=F32)
            f_ref[:, lo:hi] = g.astype(BF16)
            f_ref[:, FH + lo:FH + hi] = u.astype(BF16)
            a_ref[:, lo:hi] = (g * _sig(g) * u).astype(BF16)

    return pl.pallas_call(
        body, name="ffn_in_act",
        out_shape=(jax.ShapeDtypeStruct((S, 2 * FH), BF16), jax.ShapeDtypeStruct((S, FH), BF16)),
        grid=(S // tm,),
        in_specs=[pl.BlockSpec((tm, D), lambda i: (i, 0)), _resident((D, 2 * FH))],
        out_specs=(pl.BlockSpec((tm, 2 * FH), lambda i: (i, 0)), pl.BlockSpec((tm, FH), lambda i: (i, 0))),
        compiler_params=_cp(("parallel",)))(h2, w_fi)


def _ffn_bwd(do2, w_fo, f, after):
    S, D = do2.shape
    tm = min(512, S)

    def body(d_ref, w_ref, f_ref, after_ref, df_ref):
        dv = d_ref[...]
        for c in range(FH // _FC):
            lo, hi = c * _FC, (c + 1) * _FC
            dact = lax.dot_general(dv, w_ref[lo:hi, :], (((1,), (1,)), ((), ())), preferred_element_type=F32)
            g = f_ref[:, lo:hi].astype(F32)
            u = f_ref[:, FH + lo:FH + hi].astype(F32)
            sg = _sig(g)
            df_ref[:, lo:hi] = (dact * u * (sg * (1.0 + g * (1.0 - sg)))).astype(BF16)
            df_ref[:, FH + lo:FH + hi] = (dact * g * sg).astype(BF16)

    return pl.pallas_call(
        body, name="ffn_bwd", out_shape=jax.ShapeDtypeStruct((S, 2 * FH), BF16), grid=(S // tm,),
        in_specs=[pl.BlockSpec((tm, D), lambda i: (i, 0)), _resident((FH, D)),
                  pl.BlockSpec((tm, 2 * FH), lambda i: (i, 0)), pl.BlockSpec(memory_space=pl.ANY)],
        out_specs=pl.BlockSpec((tm, 2 * FH), lambda i: (i, 0)),
        compiler_params=_cp(("parallel",)))(do2, w_fo, f, after)


def _ffn_out_final(x2, act, w_fo, g2, fg, tgt):
    S, D = x2.shape
    tm = min(512, S)

    def body(x2_ref, a_ref, w_ref, g2_ref, fg_ref, t_ref, dx3_ref, do2_ref, ls_ref, dfg_ref, dg2_ref):
        i = pl.program_id(0)

        @pl.when(i == 0)
        def _():
            ls_ref[...] = jnp.zeros_like(ls_ref)
            dfg_ref[...] = jnp.zeros_like(dfg_ref)
            dg2_ref[...] = jnp.zeros_like(dg2_ref)

        o2 = jnp.dot(a_ref[...], w_ref[...], preferred_element_type=F32)
        x3 = x2_ref[...] + g2_ref[...] * o2
        r = lax.rsqrt(jnp.mean(x3 * x3, axis=-1, keepdims=True) + EPS)
        xn = x3 * r
        err = xn * fg_ref[...] - t_ref[...]
        dy = err * (1.0 / D)
        dxn = dy * fg_ref[...]
        dx3 = r * (dxn - xn * jnp.mean(dxn * xn, axis=-1, keepdims=True))
        dx3_ref[...] = dx3
        do2_ref[...] = (dx3 * g2_ref[...]).astype(BF16)
        e2 = _colsum8(err * err)
        lanes = e2[:, 0:128]
        for q in range(1, D // 128):
            lanes = lanes + e2[:, q * 128:(q + 1) * 128]
        ls_ref[...] += lanes * (0.5 / D)
        dfg_ref[...] += _colsum8(dy * xn)
        dg2_ref[...] += _colsum8(dx3 * o2)

    row = pl.BlockSpec((tm, D), lambda i: (i, 0))
    par = _full((1, D))
    return pl.pallas_call(
        body, name="final_loss",
        out_shape=(jax.ShapeDtypeStruct((S, D), F32), jax.ShapeDtypeStruct((S, D), BF16),
                   jax.ShapeDtypeStruct((8, 128), F32), jax.ShapeDtypeStruct((8, D), F32),
                   jax.ShapeDtypeStruct((8, D), F32)),
        grid=(S // tm,), in_specs=[row, pl.BlockSpec((tm, FH), lambda i: (i, 0)), _resident((FH, D)), par, par, row],
        out_specs=(row, row, _full((8, 128)), _full((8, D)), _full((8, D))),
        compiler_params=_cp(("arbitrary",)))(x2, act, w_fo, g2, fg, tgt)


def _normmod_bwd(dsrc, w, xin, dres, g, sc, gate, o, after, name):
    S, D = xin.shape
    parts = list(dsrc) if isinstance(dsrc, (list, tuple)) else [dsrc]
    widths = [p.shape[1] for p in parts]
    K = sum(widths)
    tm = min(512, S)
    npart = len(parts)

    def body(*refs):
        ds_refs = refs[:npart]
        w_ref, x_ref, dr_ref, g_ref, sc_ref, gate_ref, o_ref, after_ref = refs[npart:npart + 8]
        dx_ref, do_ref, dsh_ref, dsc_ref, dg_ref, dgate_ref = refs[npart + 8:]
        i = pl.program_id(0)

        @pl.when(i == 0)
        def _():
            dsh_ref[...] = jnp.zeros_like(dsh_ref)
            dsc_ref[...] = jnp.zeros_like(dsc_ref)
            dg_ref[...] = jnp.zeros_like(dg_ref)
            dgate_ref[...] = jnp.zeros_like(dgate_ref)

        gv = g_ref[...]
        scale = 1.0 + sc_ref[...]
        xv = x_ref[...]
        r = lax.rsqrt(jnp.mean(xv * xv, axis=-1, keepdims=True) + EPS)
        xn = xv * r
        dh_v, col = None, 0
        for ds_ref, wd in zip(ds_refs, widths):
            t = lax.dot_general(ds_ref[...], w_ref[:, col:col + wd], (((1,), (1,)), ((), ())),
                                preferred_element_type=F32)
            dh_v = t if dh_v is None else dh_v + t
            col += wd
        dxn = dh_v * (gv * scale)
        dx = dr_ref[...] + r * (dxn - xn * jnp.mean(dxn * xn, axis=-1, keepdims=True))
        dx_ref[...] = dx
        do_ref[...] = (dx * gate_ref[...]).astype(BF16)
        hx = dh_v * xn
        dsh_ref[...] += _colsum8(dh_v)
        dsc_ref[...] += _colsum8(hx) * gv
        dg_ref[...] += _colsum8(hx) * scale
        dgate_ref[...] += _colsum8(dx * o_ref[...])

    row = pl.BlockSpec((tm, D), lambda i: (i, 0))
    par = _full((1, D))
    acc = jax.ShapeDtypeStruct((8, D), F32)
    return pl.pallas_call(
        body, name=name,
        out_shape=(jax.ShapeDtypeStruct((S, D), F32), jax.ShapeDtypeStruct((S, D), BF16), acc, acc, acc, acc),
        grid=(S // tm,),
        in_specs=[pl.BlockSpec((tm, wd), lambda i: (i, 0)) for wd in widths]
        + [_resident((D, K)), row, row, par, par, par, row, pl.BlockSpec(memory_space=pl.ANY)],
        out_specs=(row, row, _full((8, D)), _full((8, D)), _full((8, D)), _full((8, D))),
        compiler_params=_cp(("arbitrary",)))(*parts, w, xin, dres, g, sc, gate, o, after)


def _me():
    return lax.axis_index("x"), lax.axis_index("y"), lax.axis_index("c")


def _allgather8(v, name, after=()):
    R, C = v.shape
    after = list(after)

    def body(v_ref, *rest):
        out_ref, send_sems, recv_sems, local_sem = rest[len(after):]
        x, y, c = _me()
        mine = pltpu.make_async_copy(v_ref, out_ref.at[4 * x + 2 * y + c], local_sem)
        mine.start()
        copies = []
        for k in range(1, N_DEV):
            fx, fy, fc = (k >> 2) & 1, (k >> 1) & 1, k & 1
            peer = (x ^ fx, y ^ fy, c ^ fc)
            copies.append(pltpu.make_async_remote_copy(
                src_ref=v_ref, dst_ref=out_ref.at[4 * x + 2 * y + c],
                send_sem=send_sems.at[k - 1], recv_sem=recv_sems.at[k - 1],
                device_id=peer, device_id_type=MESH))
        for cp in copies:
            cp.start()
        for k in range(1, N_DEV):
            fx, fy, fc = (k >> 2) & 1, (k >> 1) & 1, k & 1
            src_slot = 4 * (x ^ fx) + 2 * (y ^ fy) + (c ^ fc)
            pltpu.make_async_remote_copy(
                src_ref=v_ref, dst_ref=out_ref.at[src_slot],
                send_sem=send_sems.at[k - 1], recv_sem=recv_sems.at[k - 1],
                device_id=(x ^ fx, y ^ fy, c ^ fc), device_id_type=MESH).wait_recv()
        for cp in copies:
            cp.wait_send()
        mine.wait()

    return pl.pallas_call(
        body, name=name, out_shape=jax.ShapeDtypeStruct((N_DEV, R, C), v.dtype),
        in_specs=[pl.BlockSpec(memory_space=pltpu.VMEM)] + [pl.BlockSpec(memory_space=pl.ANY)] * len(after),
        out_specs=pl.BlockSpec(memory_space=pltpu.VMEM),
        scratch_shapes=[pltpu.SemaphoreType.DMA((N_DEV - 1,)), pltpu.SemaphoreType.DMA((N_DEV - 1,)),
                        pltpu.SemaphoreType.DMA],
        compiler_params=pltpu.CompilerParams(vmem_limit_bytes=VMEM_LIMIT))(v, *after)


_HBM = pl.BlockSpec(memory_space=pltpu.HBM)
_SEM = pl.BlockSpec(memory_space=pltpu.SEMAPHORE)
_EFFECT = pltpu.SideEffectType.DATAFLOW_SIDE_EFFECTING
_N_PEER = N_CHIP - 1


def _chip_part(ref, axis, n, chip):
    start = pl.multiple_of(chip * n, 8)
    return ref.at[pl.ds(start, n), :] if axis == 0 else ref.at[:, pl.ds(start, n)]


def _gather_copy(k, src_ref, land_ref, send_sems, recv_sems, axis, arriving):
    x, y, c = _me()
    px, py = x ^ ((k >> 1) & 1), y ^ (k & 1)
    chip = 2 * px + py if arriving else 2 * x + y
    return pltpu.make_async_remote_copy(
        src_ref=src_ref, dst_ref=_chip_part(land_ref, axis, src_ref.shape[axis], chip),
        send_sem=send_sems.at[k - 1], recv_sem=recv_sems.at[k - 1], device_id=(px, py, c), device_id_type=MESH)


def _scatter_copy(k, grad_ref, land_ref, send_sems, recv_sems, axis):
    x, y, c = _me()
    px, py = x ^ ((k >> 1) & 1), y ^ (k & 1)
    return pltpu.make_async_remote_copy(
        src_ref=_chip_part(grad_ref, axis, grad_ref.shape[axis] // N_CHIP, 2 * px + py), dst_ref=land_ref.at[k],
        send_sem=send_sems.at[k - 1], recv_sem=recv_sems.at[k - 1], device_id=(px, py, c), device_id_type=MESH)


def _scatter_own(grad_ref, land_ref, send_sems, axis):
    x, y, _ = _me()
    return pltpu.make_async_copy(_chip_part(grad_ref, axis, grad_ref.shape[axis] // N_CHIP, 2 * x + y),
                                 land_ref.at[0], send_sems.at[_N_PEER])


def _own_copy(src_ref, land_ref, sends, axis):
    x, y, _ = _me()
    return pltpu.make_async_copy(src_ref, _chip_part(land_ref, axis, src_ref.shape[axis], 2 * x + y),
                                 sends.at[_N_PEER])


def _gather_start(shards, axes, after, name):
    nw = len(shards)
    lands = []
    for s, ax in zip(shards, axes):
        shp = list(s.shape)
        shp[ax] *= N_CHIP
        lands.append(lax.empty(tuple(shp), s.dtype))

    def body(*refs):
        srcs, zones = refs[:nw], refs[nw:2 * nw]
        sends, recvs = refs[2 * nw + 1:3 * nw + 1], refs[3 * nw + 1:4 * nw + 1]
        token = refs[-1]
        for w in range(nw):
            for k in range(1, N_CHIP):
                _gather_copy(k, srcs[w], zones[w], sends[w], recvs[w], axes[w], False).start()
        for w in range(nw):
            _own_copy(srcs[w], zones[w], sends[w], axes[w]).start()
        token[...] = jnp.zeros_like(token)

    outs = pl.pallas_call(
        body, name=name,
        out_shape=tuple([pltpu.SemaphoreType.DMA((_N_PEER + 1,))] * nw + [pltpu.SemaphoreType.DMA((_N_PEER,))] * nw
                        + [pltpu.HBM(a.shape, a.dtype) for a in list(shards) + list(lands)]
                        + [jax.ShapeDtypeStruct((8, 128), F32)]),
        in_specs=[_HBM] * (2 * nw) + [pl.BlockSpec(memory_space=pl.ANY)],
        out_specs=tuple([_SEM] * (2 * nw) + [_HBM] * (2 * nw) + [pl.BlockSpec(memory_space=pltpu.VMEM)]),
        input_output_aliases={i: 2 * nw + i for i in range(2 * nw)},
        compiler_params=pltpu.CompilerParams(has_side_effects=_EFFECT),
    )(*([pltpu.with_memory_space_constraint(a, pltpu.HBM) for a in list(shards) + list(lands)] + [after]))
    per_weight = [(outs[w], outs[nw + w], outs[2 * nw + w], outs[3 * nw + w]) for w in range(nw)]
    return per_weight, outs[-1]


def _gather_wait(state, axis, after, name):
    send_sems, recv_sems, shard, land = state

    after = list(after) if isinstance(after, (list, tuple)) else [after]

    def body(src_ref, land_ref, sends, recvs, *rest):
        for k in range(1, N_CHIP):
            _gather_copy(k, src_ref, land_ref, sends, recvs, axis, False).wait_send()
            _gather_copy(k, src_ref, land_ref, sends, recvs, axis, True).wait_recv()
        _own_copy(src_ref, land_ref, sends, axis).wait()

    return pl.pallas_call(
        body, name=name, out_shape=(pltpu.HBM(shard.shape, shard.dtype), pltpu.HBM(land.shape, land.dtype)),
        in_specs=[_HBM, _HBM, _SEM, _SEM] + [pl.BlockSpec(memory_space=pl.ANY)] * len(after), out_specs=(_HBM, _HBM),
        input_output_aliases={0: 0, 1: 1},
        compiler_params=pltpu.CompilerParams(has_side_effects=_EFFECT),
    )(shard, land, send_sems, recv_sems, *after)[1]


def _half_rows(ref, c):
    k2 = ref.shape[0] // 2
    return pl.ds(pl.multiple_of(c * k2, 8), k2)


def _half_copy(k, shard_ref, land_ref, send_sems, recv_sems, arriving):
    x, y, c = _me()
    px, py = x ^ ((k >> 1) & 1), y ^ (k & 1)
    n = shard_ref.shape[1]
    chip = 2 * px + py if arriving else 2 * x + y
    return pltpu.make_async_remote_copy(
        src_ref=shard_ref.at[_half_rows(shard_ref, c), :],
        dst_ref=land_ref.at[_half_rows(land_ref, c), pl.ds(pl.multiple_of(chip * n, 128), n)],
        send_sem=send_sems.at[k - 1], recv_sem=recv_sems.at[k - 1], device_id=(px, py, c), device_id_type=MESH)


def _half_own(shard_ref, land_ref, send_sems):
    x, y, c = _me()
    n = shard_ref.shape[1]
    return pltpu.make_async_copy(
        shard_ref.at[_half_rows(shard_ref, c), :],
        land_ref.at[_half_rows(land_ref, c), pl.ds(pl.multiple_of((2 * x + y) * n, 128), n)], send_sems.at[_N_PEER])


def _half_gather_start(shard, after, name):
    K, n = shard.shape
    land = lax.empty((K, N_CHIP * n), shard.dtype)

    def body(shard_ref, land_ref, after_ref, sends, recvs, shard_thru, land_thru, token):
        for k in range(1, N_CHIP):
            _half_copy(k, shard_ref, land_ref, sends, recvs, False).start()
        _half_own(shard_ref, land_ref, sends).start()
        token[...] = jnp.zeros_like(token)

    outs = pl.pallas_call(
        body, name=name,
        out_shape=(pltpu.SemaphoreType.DMA((_N_PEER + 1,)), pltpu.SemaphoreType.DMA((_N_PEER,)),
                   pltpu.HBM(shard.shape, shard.dtype), pltpu.HBM(land.shape, land.dtype),
                   jax.ShapeDtypeStruct((8, 128), F32)),
        in_specs=[_HBM, _HBM, pl.BlockSpec(memory_space=pl.ANY)],
        out_specs=(_SEM, _SEM, _HBM, _HBM, pl.BlockSpec(memory_space=pltpu.VMEM)),
        input_output_aliases={0: 2, 1: 3},
        compiler_params=pltpu.CompilerParams(has_side_effects=_EFFECT),
    )(pltpu.with_memory_space_constraint(shard, pltpu.HBM), pltpu.with_memory_space_constraint(land, pltpu.HBM), after)
    return outs[:4], outs[4]


def _half_gather_wait(state, after, name):
    send_sems, recv_sems, shard, land = state
    after = list(after)

    def body(shard_ref, land_ref, sends, recvs, *rest):
        for k in range(1, N_CHIP):
            _half_copy(k, shard_ref, land_ref, sends, recvs, False).wait_send()
            _half_copy(k, shard_ref, land_ref, sends, recvs, True).wait_recv()
        _half_own(shard_ref, land_ref, sends).wait()

    return pl.pallas_call(
        body, name=name, out_shape=(pltpu.HBM(shard.shape, shard.dtype), pltpu.HBM(land.shape, land.dtype)),
        in_specs=[_HBM, _HBM, _SEM, _SEM] + [pl.BlockSpec(memory_space=pl.ANY)] * len(after), out_specs=(_HBM, _HBM),
        input_output_aliases={0: 0, 1: 1},
        compiler_params=pltpu.CompilerParams(has_side_effects=_EFFECT),
    )(shard, land, send_sems, recv_sems, *after)[1]


def _half_swap_copy(land_ref, send_sem, recv_sem, arriving):
    x, y, c = _me()
    rows = _half_rows(land_ref, 1 - c if arriving else c)
    return pltpu.make_async_remote_copy(src_ref=land_ref.at[rows, :], dst_ref=land_ref.at[rows, :], send_sem=send_sem,
                                        recv_sem=recv_sem, device_id=(x, y, 1 - c), device_id_type=MESH)


def _half_swap_start(land, name):
    def body(land_ref, send, recv, land_thru, token):
        _half_swap_copy(land_ref, send.at[0], recv.at[0], False).start()
        token[...] = jnp.zeros_like(token)

    sem = pltpu.SemaphoreType.DMA((1,))
    outs = pl.pallas_call(
        body, name=name,
        out_shape=(sem, sem, pltpu.HBM(land.shape, land.dtype), jax.ShapeDtypeStruct((8, 128), F32)),
        in_specs=[_HBM], out_specs=(_SEM, _SEM, _HBM, pl.BlockSpec(memory_space=pltpu.VMEM)),
        input_output_aliases={0: 2},
        compiler_params=pltpu.CompilerParams(has_side_effects=_EFFECT),
    )(pltpu.with_memory_space_constraint(land, pltpu.HBM))
    return outs[:3], outs[3]


def _half_swap_wait(state, after, name):
    send, recv, land = state

    def body(land_ref, send_ref, recv_ref, after_ref, got_ref):
        _half_swap_copy(land_ref, send_ref.at[0], recv_ref.at[0], False).wait_send()
        _half_swap_copy(land_ref, send_ref.at[0], recv_ref.at[0], True).wait_recv()

    return pl.pallas_call(
        body, name=name, out_shape=pltpu.HBM(land.shape, land.dtype),
        in_specs=[_HBM, _SEM, _SEM, pl.BlockSpec(memory_space=pl.ANY)], out_specs=_HBM,
        input_output_aliases={0: 0},
        compiler_params=pltpu.CompilerParams(has_side_effects=_EFFECT),
    )(land, send, recv, after)


def _all8_copy(k, v_ref, land_ref, send_sems, recv_sems, arriving):
    x, y, c = _me()
    px, py, pc = x ^ ((k >> 2) & 1), y ^ ((k >> 1) & 1), c ^ (k & 1)
    slot = 4 * px + 2 * py + pc if arriving else 4 * x + 2 * y + c
    return pltpu.make_async_remote_copy(
        src_ref=v_ref, dst_ref=land_ref.at[slot], send_sem=send_sems.at[k - 1], recv_sem=recv_sems.at[k - 1],
        device_id=(px, py, pc), device_id_type=MESH)


def _all8_own(v_ref, land_ref, send_sems):
    x, y, c = _me()
    return pltpu.make_async_copy(v_ref, land_ref.at[4 * x + 2 * y + c], send_sems.at[N_DEV - 1])


def _all8_start(v, name):
    land = lax.empty((N_DEV,) + v.shape, v.dtype)

    def body(v_ref, land_ref, sends, recvs, v_thru, land_thru, token):
        for k in range(1, N_DEV):
            _all8_copy(k, v_ref, land_ref, sends, recvs, False).start()
        _all8_own(v_ref, land_ref, sends).start()
        token[...] = jnp.zeros_like(token)

    outs = pl.pallas_call(
        body, name=name,
        out_shape=(pltpu.SemaphoreType.DMA((N_DEV,)), pltpu.SemaphoreType.DMA((N_DEV - 1,)),
                   pltpu.HBM(v.shape, v.dtype), pltpu.HBM(land.shape, land.dtype),
                   jax.ShapeDtypeStruct((8, 128), F32)),
        in_specs=[_HBM, _HBM], out_specs=(_SEM, _SEM, _HBM, _HBM, pl.BlockSpec(memory_space=pltpu.VMEM)),
        input_output_aliases={0: 2, 1: 3},
        compiler_params=pltpu.CompilerParams(has_side_effects=_EFFECT),
    )(pltpu.with_memory_space_constraint(v, pltpu.HBM), pltpu.with_memory_space_constraint(land, pltpu.HBM))
    return outs[:4], outs[4]


def _all8_wait(state, after, name):
    send_sems, recv_sems, v, land = state

    def body(v_ref, land_ref, sends, recvs, after_ref, v_dead, got_ref):
        for k in range(1, N_DEV):
            _all8_copy(k, v_ref, land_ref, sends, recvs, False).wait_send()
            _all8_copy(k, v_ref, land_ref, sends, recvs, True).wait_recv()
        _all8_own(v_ref, land_ref, sends).wait()

    return pl.pallas_call(
        body, name=name, out_shape=(pltpu.HBM(v.shape, v.dtype), pltpu.HBM(land.shape, land.dtype)),
        in_specs=[_HBM, _HBM, _SEM, _SEM, pl.BlockSpec(memory_space=pl.ANY)], out_specs=(_HBM, _HBM),
        input_output_aliases={0: 0, 1: 1},
        compiler_params=pltpu.CompilerParams(has_side_effects=_EFFECT),
    )(v, land, send_sems, recv_sems, after)[1]


def _swap_copy(w, src_ref, land_ref, send_sems, recv_sems):
    x, y, c = _me()
    return pltpu.make_async_remote_copy(src_ref=src_ref, dst_ref=land_ref, send_sem=send_sems.at[w],
                                        recv_sem=recv_sems.at[w], device_id=(x, y, 1 - c), device_id_type=MESH)


def _swap_start(arrs, after, name):
    nw = len(arrs)
    lands = [lax.empty(a.shape, a.dtype) for a in arrs]

    def body(*refs):
        srcs, zones = refs[:nw], refs[nw:2 * nw]
        sends, recvs = refs[2 * nw + 1], refs[2 * nw + 2]
        for w in range(nw):
            _swap_copy(w, srcs[w], zones[w], sends, recvs).start()
        refs[-1][...] = jnp.zeros_like(refs[-1])

    sem = pltpu.SemaphoreType.DMA((nw,))
    outs = pl.pallas_call(
        body, name=name,
        out_shape=tuple([sem, sem] + [pltpu.HBM(a.shape, a.dtype) for a in list(arrs) + lands]
                        + [jax.ShapeDtypeStruct((8, 128), F32)]),
        in_specs=[_HBM] * (2 * nw) + [pl.BlockSpec(memory_space=pl.ANY)],
        out_specs=tuple([_SEM, _SEM] + [_HBM] * (2 * nw) + [pl.BlockSpec(memory_space=pltpu.VMEM)]),
        input_output_aliases={i: 2 + i for i in range(2 * nw)},
        compiler_params=pltpu.CompilerParams(has_side_effects=_EFFECT),
    )(*([pltpu.with_memory_space_constraint(a, pltpu.HBM) for a in list(arrs) + lands] + [after]))
    return (outs[0], outs[1], outs[2:2 + nw], outs[2 + nw:2 + 2 * nw]), outs[-1]


def _swap_wait(state, after, name):
    send_sems, recv_sems, arrs, lands = state
    nw = len(arrs)

    def body(*refs):
        srcs, zones = refs[:nw], refs[nw:2 * nw]
        sends, recvs = refs[2 * nw], refs[2 * nw + 1]
        for w in range(nw):
            cp = _swap_copy(w, srcs[w], zones[w], sends, recvs)
            cp.wait_send()
            cp.wait_recv()

    outs = pl.pallas_call(
        body, name=name, out_shape=tuple(pltpu.HBM(a.shape, a.dtype) for a in list(arrs) + list(lands)),
        in_specs=[_HBM] * (2 * nw) + [_SEM, _SEM, pl.BlockSpec(memory_space=pl.ANY)],
        out_specs=tuple([_HBM] * (2 * nw)),
        input_output_aliases={i: i for i in range(2 * nw)},
        compiler_params=pltpu.CompilerParams(has_side_effects=_EFFECT),
    )(*arrs, *lands, send_sems, recv_sems, after)
    return list(outs[:nw]), list(outs[nw:])


def _scatter_start(grad, axis, name):
    shp = list(grad.shape)
    shp[axis] //= N_CHIP
    land = lax.empty((N_CHIP,) + tuple(shp), grad.dtype)

    def body(grad_ref, land_ref, sends, recvs, grad_thru, land_thru, token):
        for k in range(1, N_CHIP):
            _scatter_copy(k, grad_ref, land_ref, sends, recvs, axis).start()
        _scatter_own(grad_ref, land_ref, sends, axis).start()
        token[...] = jnp.zeros_like(token)

    outs = pl.pallas_call(
        body, name=name,
        out_shape=(pltpu.SemaphoreType.DMA((_N_PEER + 1,)), pltpu.SemaphoreType.DMA((_N_PEER,)),
                   pltpu.HBM(grad.shape, grad.dtype), pltpu.HBM(land.shape, land.dtype),
                   jax.ShapeDtypeStruct((8, 128), F32)),
        in_specs=[_HBM, _HBM], out_specs=(_SEM, _SEM, _HBM, _HBM, pl.BlockSpec(memory_space=pltpu.VMEM)),
        input_output_aliases={0: 2, 1: 3},
        compiler_params=pltpu.CompilerParams(has_side_effects=_EFFECT),
    )(pltpu.with_memory_space_constraint(grad, pltpu.HBM), pltpu.with_memory_space_constraint(land, pltpu.HBM))
    return outs[:4], outs[4]


def _scatter_wait(state, axis, after, name):
    send_sems, recv_sems, grad, land = state

    def body(grad_ref, land_ref, sends, recvs, after_ref, grad_dead, got_ref):
        for k in range(1, N_CHIP):
            cp = _scatter_copy(k, grad_ref, land_ref, sends, recvs, axis)
            cp.wait_send()
            cp.wait_recv()
        _scatter_own(grad_ref, land_ref, sends, axis).wait()

    return pl.pallas_call(
        body, name=name, out_shape=(pltpu.HBM(grad.shape, grad.dtype), pltpu.HBM(land.shape, land.dtype)),
        in_specs=[_HBM, _HBM, _SEM, _SEM, pl.BlockSpec(memory_space=pl.ANY)], out_specs=(_HBM, _HBM),
        input_output_aliases={0: 0, 1: 1},
        compiler_params=pltpu.CompilerParams(has_side_effects=_EFFECT),
    )(grad, land, send_sems, recv_sems, after)[1]


_C1 = 1.0 - B1 ** STEP
_C2 = 1.0 - B2 ** STEP


def _adam_math(w, g, m, v):
    m = B1 * m + (1.0 - B1) * g
    v = B2 * v + (1.0 - B2) * (g * g)
    delta = -LR * ((m / _C1) / (jnp.sqrt(v / _C2) + AEPS) + WD * w)
    return delta, m, v


def _adamw(w, m, v, groups, name):
    R, C = w.shape
    tr = R if R <= 256 else (128 if R % 128 == 0 else 176)
    assert R % tr == 0, (name, R)
    gparts = [p for grp in groups for p in grp]
    sizes = [len(grp) for grp in groups]
    ng = len(gparts)

    def body(*refs):
        w_ref, m_ref, v_ref = refs[:3]
        g_refs = list(refs[3:3 + ng])
        g_out, d_out, m_out, v_out = refs[3 + ng:]
        g = None
        for size in sizes:
            s = None
            for r in [g_refs.pop(0) for _ in range(size)]:
                terms = [r[q] for q in range(r.shape[0])] if len(r.shape) == 3 else [r[...]]
                for t in terms:
                    s = t.astype(F32) if s is None else s + t.astype(F32)
            g = s if g is None else g + s
        delta, mn, vn = _adam_math(w_ref[...], g, m_ref[...], v_ref[...])
        g_out[...] = g
        d_out[...] = delta
        m_out[...] = mn
        v_out[...] = vn

    blk = pl.BlockSpec((tr, C), lambda i: (i, 0))
    g_specs = [blk if p.ndim == 2 else pl.BlockSpec((p.shape[0], tr, C), lambda i: (0, i, 0)) for p in gparts]
    sds = jax.ShapeDtypeStruct((R, C), F32)
    return pl.pallas_call(
        body, name=name, out_shape=(sds, sds, sds, sds), grid=(R // tr,),
        in_specs=[blk, blk, blk] + g_specs, out_specs=(blk, blk, blk, blk),
        compiler_params=_cp(("parallel",)))(w, m, v, *gparts)


def _adamw_small(stack, names, wts, mom, var, sum_only, name):
    items, row = [], 0
    for n in names:
        shape = (KW, CW) if n == "conv_w" else wts[n].shape
        size = int(np.prod(shape))
        vec = len(shape) == 2 and shape[0] == 1 and n not in sum_only
        view = shape if vec else (-(-size // _PACK_COLS), _PACK_COLS)
        items.append((n, row, size, vec, view))
        row += _pack_rows(shape)
    upd = [it for it in items if it[0] not in sum_only]
    operands = [stack]
    for n, _, _, _, view in upd:
        operands += [d[n].reshape(view) for d in (wts, mom, var)]

    def grad(stack_ref, r0, nrows, ncols):
        g = stack_ref[0, r0:r0 + nrows, 0:ncols]
        for q in range(1, N_DEV):
            g = g + stack_ref[q, r0:r0 + nrows, 0:ncols]
        return g

    def body(*refs):
        stack_ref, ins, outs = refs[0], refs[1:1 + 3 * len(upd)], refs[1 + 3 * len(upd):]
        o = 0
        for idx, (n, r0, size, vec, view) in enumerate(upd):
            w_ref, m_ref, v_ref = ins[3 * idx:3 * idx + 3]
            g_out, d_out, m_out, v_out = outs[o:o + 4]
            o += 4
            if vec:
                pieces = [(j, j * _PACK_COLS, min((j + 1) * _PACK_COLS, size)) for j in range(-(-size // _PACK_COLS))]
            else:
                pieces = [(None, 0, _PACK_COLS)]
            for j, lo, hi in pieces:
                if vec:
                    g = grad(stack_ref, r0 + j, 1, hi - lo)
                    sl = (slice(None), slice(lo, hi))
                else:
                    g = grad(stack_ref, r0, view[0], _PACK_COLS)
                    sl = (slice(None), slice(None))
                delta, mn, vn = _adam_math(w_ref[sl], g, m_ref[sl], v_ref[sl])
                g_out[sl] = g
                d_out[sl] = delta
                m_out[sl] = mn
                v_out[sl] = vn
        for n, r0, size, vec, view in items:
            if n in sum_only:
                outs[o][...] = grad(stack_ref, r0, view[0], _PACK_COLS)
                o += 1

    out_shape = []
    for n, _, _, _, view in upd:
        out_shape += [jax.ShapeDtypeStruct(view, F32)] * 4
    out_shape += [jax.ShapeDtypeStruct(view, F32) for n, _, _, _, view in items if n in sum_only]
    vm = pl.BlockSpec(memory_space=pltpu.VMEM)
    res = pl.pallas_call(
        body, name=name, out_shape=tuple(out_shape), in_specs=[vm] * len(operands),
        out_specs=tuple([vm] * len(out_shape)),
        compiler_params=pltpu.CompilerParams(vmem_limit_bytes=VMEM_LIMIT))(*operands)
    updated = {n: tuple(r.reshape(wts[n].shape) for r in res[4 * i:4 * i + 4]) for i, (n, *_) in enumerate(upd)}
    sums = dict(zip([it[0] for it in items if it[0] in sum_only], res[4 * len(upd):]))
    return updated, sums


def _adamw_native(tensors, name):
    nt = len(tensors)

    def body(*refs):
        ins, outs = refs[:4 * nt], refs[4 * nt:]
        for t in range(nt):
            w_ref, m_ref, v_ref, g_ref = ins[4 * t:4 * t + 4]
            g = g_ref[...]
            delta, mn, vn = _adam_math(w_ref[...], g, m_ref[...], v_ref[...])
            outs[4 * t][...] = g
            outs[4 * t + 1][...] = delta
            outs[4 * t + 2][...] = mn
            outs[4 * t + 3][...] = vn

    vm = pl.BlockSpec(memory_space=pltpu.VMEM)
    flat = [a for tup in tensors for a in tup]
    res = pl.pallas_call(
        body, name=name, out_shape=tuple(jax.ShapeDtypeStruct(tup[0].shape, F32) for tup in tensors for _ in range(4)),
        in_specs=[vm] * len(flat), out_specs=tuple([vm] * (4 * nt)),
        compiler_params=pltpu.CompilerParams(vmem_limit_bytes=VMEM_LIMIT))(*flat)
    return [tuple(res[4 * t:4 * t + 4]) for t in range(nt)]


def _mod_shard(c_all, w_ada, b_ada_cols):
    n = w_ada.shape[1]
    tn = 512

    def body(c_ref, w_ref, b_ref, o_ref):
        cv = c_ref[...]
        ca = (cv * _sig(cv)).astype(BF16)
        o_ref[...] = jnp.dot(ca, w_ref[...].astype(BF16), preferred_element_type=F32) + b_ref[...]

    return pl.pallas_call(
        body, name="mod_shard", out_shape=jax.ShapeDtypeStruct((N_DEV, n), F32), grid=(n // tn,),
        in_specs=[_full((N_DEV, D_MODEL)), pl.BlockSpec((D_MODEL, tn), lambda j: (0, j)),
                  pl.BlockSpec((1, tn), lambda j: (0, j))],
        out_specs=pl.BlockSpec((N_DEV, tn), lambda j: (0, j)),
        compiler_params=_cp(("parallel",)))(c_all, w_ada, b_ada_cols)


def _ada_grad(c_all, dmod_cols, after):
    n = dmod_cols.shape[1]
    tn = 512

    def body(c_ref, d_ref, after_ref, o_ref):
        cv = c_ref[...]
        ca = cv * _sig(cv)
        o_ref[...] = lax.dot_general(ca, d_ref[...], (((0,), (0,)), ((), ())),
                                     preferred_element_type=F32, precision=lax.Precision.HIGHEST)

    return pl.pallas_call(
        body, name="ada_grad", out_shape=jax.ShapeDtypeStruct((D_MODEL, n), F32), grid=(n // tn,),
        in_specs=[_full((N_DEV, D_MODEL)), pl.BlockSpec((N_DEV, tn), lambda j: (0, j)),
                  pl.BlockSpec(memory_space=pl.ANY)],
        out_specs=pl.BlockSpec((D_MODEL, tn), lambda j: (0, j)),
        compiler_params=_cp(("parallel",)))(c_all, dmod_cols, after)


def _ssm_tables(W):
    e_re, e_im, bb_re, bb_im = _ssm_prep(W["ssm_a_re"], W["ssm_a_im"], W["ssm_b_re"], W["ssm_b_im"], W["ssm_log_dt"])
    bb, cm = _block_diag_mats(bb_re, bb_im, W["ssm_c_re"], W["ssm_c_im"])
    bb16, cm16 = bb.astype(BF16), cm.astype(BF16)
    return (bb16, cm16, jnp.swapaxes(bb16, 1, 2), jnp.swapaxes(cm16, 1, 2),
            _scan_tables(e_re, e_im, False), _scan_tables(e_re, e_im, True))


def _device_step(x, mod, W, tables, tgt, getw, put, early):
    sh1, sc1, g1, sh2, sc2, g2 = [mod[:, i * D_MODEL:(i + 1) * D_MODEL] for i in range(6)]
    bb16, cm16, bbt16, cmt16, tab_f, tab_b = tables

    w_in = getw("w_in", [mod, *tables])
    h1t, z = _in_proj(x, W["norm1_g"], sc1, sh1, w_in)
    yc, scv = _conv_fwd(z, W["conv_w"], W["conv_b"], W["conv_ln_g"], W["conv_ln_b"])
    xs, ys, yg = _ssm_fwd(z, bb16, cm16, W["ssm_d"], tab_f)
    w_cp, w_glu, w_out = getw("conv_proj", scv), getw("ssm_glu", yg), getw("w_out", yg)
    y_conv, zz, merged, o, x2, h2 = _mix_fwd(scv, yg, z, x, w_cp, w_glu, w_out, g1, W["norm2_g"], sc2, sh2)
    w_fi = getw("w_ffn_in", h2)
    f, act = _ffn_in_act(h2, w_fi)
    w_fo = getw("w_ffn_out", act)
    dx3, do2, loss8, dfg8, dg2_8 = _ffn_out_final(x2, act, w_fo, g2, W["final_g"], tgt)

    sm = {}
    tok = put("w_ffn_out", _matmul(act, do2, "tn", 1408, 1024, 2048, BF16, "mm_g_ffn_out"))
    df = _ffn_bwd(do2, w_fo, f, tok)
    tok = put("w_ffn_in", _matmul(h2, df, "tn", 1024, 1408, 2048, BF16, "mm_g_ffn_in"))
    dx2, do, dsh2, dsc2, dn2, dg1_8 = _normmod_bwd(df, w_fi, x2, dx3, W["norm2_g"], sc2, g1, o, tok, "d_h2_normmod2_bwd")
    tok = put("w_out", _matmul(merged, do, "tn", 1024, 1024, 4096, BF16, "mm_g_w_out"))
    dyconv, dgl, dzz = _mix_bwd(do, w_out, z, zz, y_conv, tok)
    tok = put("ssm_glu", _matmul(yg, dzz, "tn", 512, 1024, 4096, BF16, "mm_g_ssm_glu"))
    tok = put("conv_proj", _matmul(scv, dyconv, "tn", 512, 1024, 4096, BF16, "mm_g_conv_proj", after=tok))
    du, de16, dd8, dc_full, dbb_full = _ssm_bwd(dzz, w_glu, ys, z, xs, cmt16, bbt16, W["ssm_d"], tab_b, tok)
    dyc, dlg8, dlb8, dcb8 = _conv_bwd_ln(dyconv, w_cp, yc, W["conv_ln_g"], W["conv_ln_b"])
    dz_conv, dcw = _conv_bwd(dyc, z, W["conv_w"])

    s8 = lambda a: jnp.sum(a, axis=0, keepdims=True)
    de = de16.reshape(2, 8, NST).sum(1)
    de_re, de_im = de[0].reshape(G, P), de[1].reshape(G, P)
    dc_re, dc_im = _diag_blocks(dc_full)
    dc_im = -dc_im
    dbb_re, dbb_im = [jnp.swapaxes(t, 1, 2) for t in _diag_blocks(dbb_full)]
    _, vjp = jax.vjp(_ssm_prep, W["ssm_a_re"], W["ssm_a_im"], W["ssm_b_re"], W["ssm_b_im"], W["ssm_log_dt"])
    sm["ssm_a_re"], sm["ssm_a_im"], sm["ssm_b_re"], sm["ssm_b_im"], sm["ssm_log_dt"] = vjp((de_re, de_im, dbb_re, dbb_im))
    sm["ssm_c_re"], sm["ssm_c_im"] = dc_re, dc_im
    sm["ssm_d"] = s8(dd8)
    sm["norm2_g"] = s8(dn2)
    sm["conv_b"], sm["conv_ln_g"], sm["conv_ln_b"] = s8(dcb8), s8(dlg8), s8(dlb8)
    sm["conv_w"] = dcw.reshape(KW, 8, CW).sum(1)
    sm["final_g"] = s8(dfg8)
    tok = early(sm)

    dz = [dz_conv, du, dgl]
    tok = put("w_in", _matmul(h1t, dz, "nn", 1024, 512, 4096, BF16, "mm_g_w_in", after=tok))
    dx, _, dsh1, dsc1, dn1, _ = _normmod_bwd(dz, w_in, x, dx2, W["norm1_g"], sc1, g1, o, tok, "d_h1_normmod1_bwd")
    dmod = jnp.concatenate([s8(dsh1), s8(dsc1), s8(dg1_8), s8(dsh2), s8(dsc2), s8(dg2_8)], axis=1)
    return loss8, dx, s8(dn1), dmod


_BIG = ("w_in", "conv_proj", "ssm_glu", "w_out", "w_ffn_in", "w_ffn_out")
_BIG_AXIS = {"w_in": 1, "conv_proj": 1, "ssm_glu": 1, "w_out": 0, "w_ffn_in": 1, "w_ffn_out": 0}
_EARLY = ("conv_w", "conv_b", "conv_ln_g", "conv_ln_b", "ssm_a_re", "ssm_a_im", "ssm_b_re", "ssm_b_im", "ssm_c_re",
          "ssm_c_im", "ssm_d", "ssm_log_dt", "norm2_g", "final_g")
_LATE = ("norm1_g", "b_ada")
_S5_MATS = ("ssm_a_re", "ssm_a_im", "ssm_b_re", "ssm_b_im", "ssm_c_re", "ssm_c_im")
_ORDER = ("w_ada", "b_ada", "norm1_g", "w_in", "conv_w", "conv_b", "conv_ln_g", "conv_ln_b", "conv_proj",
          "ssm_a_re", "ssm_a_im", "ssm_b_re", "ssm_b_im", "ssm_c_re", "ssm_c_im", "ssm_d", "ssm_log_dt", "ssm_glu",
          "w_out", "norm2_g", "w_ffn_in", "w_ffn_out", "final_g")
_PACK_COLS = 1024


def _pack_rows(shape):
    return -(-int(np.prod(shape)) // (8 * _PACK_COLS)) * 8


def _pack(arrs):
    parts = []
    for a in arrs:
        flat = a.reshape(-1)
        n = _pack_rows(a.shape)
        parts.append(jnp.pad(flat, (0, n * _PACK_COLS - flat.shape[0])).reshape(n, _PACK_COLS))
    return jnp.concatenate(parts, 0)


def kernel(x, c, w_ada, b_ada, norm1_g, w_in, conv_w, conv_b, conv_ln_g, conv_ln_b, conv_proj, ssm_a_re, ssm_a_im, ssm_b_re, ssm_b_im, ssm_c_re, ssm_c_im, ssm_d, ssm_log_dt, ssm_glu, w_out, norm2_g, w_ffn_in, w_ffn_out, final_g, loss_target, m_w_ada, m_b_ada, m_norm1_g, m_w_in, m_conv_w, m_conv_b, m_conv_ln_g, m_conv_ln_b, m_conv_proj, m_ssm_a_re, m_ssm_a_im, m_ssm_b_re, m_ssm_b_im, m_ssm_c_re, m_ssm_c_im, m_ssm_d, m_ssm_log_dt, m_ssm_glu, m_w_out, m_norm2_g, m_w_ffn_in, m_w_ffn_out, m_final_g, v_w_ada, v_b_ada, v_norm1_g, v_w_in, v_conv_w, v_conv_b, v_conv_ln_g, v_conv_ln_b, v_conv_proj, v_ssm_a_re, v_ssm_a_im, v_ssm_b_re, v_ssm_b_im, v_ssm_c_re, v_ssm_c_im, v_ssm_d, v_ssm_log_dt, v_ssm_glu, v_w_out, v_norm2_g, v_w_ffn_in, v_w_ffn_out, v_final_g):
    given = dict(locals())
    mx, my, mc = _me()
    chip = 2 * mx + my
    dev = 4 * mx + 2 * my + mc
    def canon(a):
        return a.reshape(1, -1) if a.ndim <= 2 else a[0]

    wts = {n: canon(given[n]) for n in _ORDER}
    mom = {n: canon(given["m_" + n]) for n in _ORDER}
    var = {n: canon(given["v_" + n]) for n in _ORDER}

    W = {n: wts[n] for n in _ORDER if n not in _BIG}
    rest = [n for n in _BIG if n != "w_in"]
    rest_shards = [wts[n].astype(BF16) for n in rest]
    state_in, token = _half_gather_start(wts["w_in"].astype(BF16), c, "gather_start_w_in")
    W["ssm_log_dt"] = wts["ssm_log_dt"] + token[0:1, 0:1]
    W["ssm_c_re"] = wts["ssm_c_re"] + token[0, 0]
    tables = _ssm_tables(W)

    c_all = _allgather8(jnp.broadcast_to(c, (8, D_MODEL)), "gather_c", after=[*tables, *rest_shards])[:, 0, :]
    n_ada = wts["w_ada"].shape[1]
    b_cols = lax.dynamic_slice(wts["b_ada"], (0, chip * n_ada), (1, n_ada))
    mod_cols = _mod_shard(c_all, wts["w_ada"], b_cols)
    halves = _half_gather_wait(state_in, [mod_cols], "gather_wait_w_in")
    state_in, token = _half_swap_start(halves, "gather_swap_start_w_in")
    mods = _allgather8(mod_cols, "gather_mod", after=[token])
    mod = jnp.concatenate([lax.dynamic_index_in_dim(mods[2 * q], dev, 0, keepdims=True) for q in range(N_CHIP)], axis=1)
    conv_w_full = _allgather8(jnp.pad(wts["conv_w"], ((0, 1), (0, 0))), "gather_conv_w", after=[token])
    W["conv_w"] = jnp.concatenate([conv_w_full[2 * q, :KW] for q in range(N_CHIP)], axis=1)
    w_in_full = _half_swap_wait(state_in, mod + W["conv_w"][0:1, 0:1], "gather_swap_wait_w_in")
    gstate, token = _gather_start(rest_shards, [_BIG_AXIS[n] for n in rest], w_in_full, "gather_start_rest")
    gstate = dict(zip(rest, gstate))
    mod = mod + token[0:1, 0:1]

    def getw(n, after):
        if n == "w_in":
            return w_in_full
        return _gather_wait(gstate[n], _BIG_AXIS[n], after, "gather_wait_" + n)

    sstate, estate = {}, []

    def put(n, g):
        sstate[n], tok = _scatter_start(g, _BIG_AXIS[n], "scatter_start_" + n)
        return tok

    first5 = [n for n in _BIG if n != "w_in"]

    def early(sm):
        state, tok = _all8_start(_pack([sm[n] for n in _EARLY]), "small_start")
        estate.append(state)
        held = [_scatter_wait(sstate[n], _BIG_AXIS[n], tok, "scatter_wait_" + n) for n in first5]
        state, tok = _swap_start(held, tok, "swap_start")
        estate.append(state)
        return tok

    loss8, dx, dn1, dmod = _device_step(x[0], mod, W, tables, loss_target[0], getw, put, early)

    held5, sib5 = _swap_wait(estate[1], dx, "swap_wait")
    outs = {}
    for i, n in enumerate(first5):
        outs[n] = _adamw(wts[n], mom[n], var[n], [[held5[i]], [sib5[i]]], "adamw_" + n)
    allp = _all8_wait(estate[0], dx, "small_wait")
    upd, sums = _adamw_small(allp, _EARLY, wts, mom, var, ("conv_w",) + _S5_MATS, "adamw_small")
    outs.update(upd)

    def swapped(n, a):
        return jnp.swapaxes(a, 1, 2) if n in ("ssm_b_re", "ssm_b_im") else a

    def summed(n):
        return swapped(n, sums[n].reshape(-1)[:wts[n].size].reshape(wts[n].shape))

    res = _adamw_native([(swapped(n, wts[n]), swapped(n, mom[n]), swapped(n, var[n]), summed(n)) for n in _S5_MATS],
                        "adamw_s5")
    for n, r in zip(_S5_MATS, res):
        outs[n] = tuple(swapped(n, a) for a in r)

    late = _allgather8(_pack([dn1, dmod, loss8]), "gather_late", after=[outs[n][1] for n in first5])
    n_late = _pack_rows((D_MODEL,)) + _pack_rows((6 * D_MODEL,))
    loss = jnp.sum(late[:, n_late:, :])
    late = late[:, :n_late, :]
    held_in = _scatter_wait(sstate["w_in"], _BIG_AXIS["w_in"], late, "scatter_wait_w_in")
    state_in, tok = _swap_start([held_in], late, "swap_start_w_in")

    r1 = _pack_rows((D_MODEL,))
    dmod_all = late[:, r1:, :].reshape(N_DEV, -1)[:, :6 * D_MODEL]
    dmod_cols = lax.dynamic_slice(dmod_all, (0, chip * n_ada), (N_DEV, n_ada))
    g_ada = _ada_grad(c_all, dmod_cols, tok)
    outs["w_ada"] = _adamw(wts["w_ada"], mom["w_ada"], var["w_ada"], [[g_ada]], "adamw_w_ada")
    upd, _ = _adamw_small(late, _LATE, wts, mom, var, (), "adamw_late")
    outs.update(upd)
    held_in, sib_in = _swap_wait(state_in, outs["w_ada"][1], "swap_wait_w_in")
    outs["w_in"] = _adamw(wts["w_in"], mom["w_in"], var["w_in"], [held_in, sib_in], "adamw_w_in")
    g_cw_full = sums["conv_w"].reshape(-1)[:KW * CW].reshape(KW, CW)
    g_cw = lax.dynamic_slice(g_cw_full, (0, chip * (CW // N_CHIP)), (KW, CW // N_CHIP))
    pad = lambda a: jnp.pad(a, ((0, 1), (0, 0)))
    r_cw = _adamw(pad(wts["conv_w"]), pad(mom["conv_w"]), pad(var["conv_w"]), [[pad(g_cw)]], "adamw_conv_w")
    outs["conv_w"] = tuple(r[:KW] for r in r_cw)

    def shaped(n, a):
        return a.reshape(given[n].shape)

    result = [loss, dx[None]]
    for q in range(4):
        result += [shaped(n, outs[n][q]) for n in _ORDER]
    return tuple(result)
```

```python
import math

import jax
import jax.numpy as jnp
import numpy as np
from jax import lax
from jax.experimental import pallas as pl
from jax.experimental.pallas import tpu as pltpu

F32 = jnp.float32
BF16 = jnp.bfloat16
EPS = 1e-6
D_MODEL = 1024
CW = 512
KW = 31
HALO = 32
G, P, H = 32, 64, 16
NST = G * P
FH = 2816
N_DEV = 8
N_CHIP = 4
VMEM_LIMIT = 56 * 1024 * 1024
LR, B1, B2, AEPS, WD, STEP = 0.001, 0.9, 0.999, 1e-08, 0.01, 10
MESH = pl.DeviceIdType.MESH


def _cp(sem=None):
    return pltpu.CompilerParams(dimension_semantics=sem, vmem_limit_bytes=VMEM_LIMIT)


def _sig(x):
    return jax.nn.sigmoid(x)


def _full(shape):
    return pl.BlockSpec(shape, lambda *_: (0,) * len(shape))


def _resident(shape):
    return pl.BlockSpec(shape, lambda *_: (0,) * len(shape), pipeline_mode=pl.Buffered(1))


def _colsum8(v):
    t, c = v.shape
    return jnp.sum(v.reshape(t // 8, 8, c), axis=0)


def _matmul(a, b, mode, tm, tn, tk, out_dtype, name, after=None, n_outer=False, m_cols=None):
    m0 = 0
    b_parts = list(b) if isinstance(b, (list, tuple)) else [b]
    if mode == "nn":
        (M, K), N = a.shape, b.shape[1]
    elif mode == "nt":
        (M, K), N = a.shape, b.shape[0]
    else:
        (K, M), N = a.shape, sum(p.shape[1] for p in b_parts)
        if m_cols is not None:
            m0, M = m_cols
    tm, tn, tk = min(tm, M), min(tn, N), min(tk, K)
    assert M % tm == 0 and N % tn == 0 and K % tk == 0 and m0 % tm == 0, (name, M, N, K, tm, tn, tk)
    assert len(b_parts) == 1 or (mode == "tn" and all(p.shape[1] % tn == 0 for p in b_parts)), name
    nk = K // tk
    mb = m0 // tm
    counts = [p.shape[1] // tn for p in b_parts] if mode == "tn" else [N // tn]
    starts = [sum(counts[:p]) for p in range(len(counts))]

    def ij(fn):
        return (lambda j, i, k: fn(i, j, k)) if n_outer else fn

    if mode == "nn":
        a_spec = pl.BlockSpec((tm, tk), ij(lambda i, j, k: (i, k)))
        b_spec = pl.BlockSpec((tk, tn), ij(lambda i, j, k: (k, j)))
        dims = (((1,), (0,)), ((), ()))
    elif mode == "nt":
        a_spec = pl.BlockSpec((tm, tk), ij(lambda i, j, k: (i, k)))
        b_spec = pl.BlockSpec((tn, tk), ij(lambda i, j, k: (j, k)))
        dims = (((1,), (1,)), ((), ()))
    else:
        a_spec = pl.BlockSpec((tk, tm), ij(lambda i, j, k: (k, i + mb)))
        dims = (((0,), (0,)), ((), ()))
    if mode == "tn":
        b_specs = [pl.BlockSpec((tk, tn), ij(lambda i, j, k, s=s, n=n: (k, jnp.clip(j - s, 0, n - 1))))
                   for s, n in zip(starts, counts)]
    else:
        b_specs = [b_spec]
    nb = len(b_parts)

    def body(a_ref, *rest):
        b_refs = rest[:nb]
        o_ref, acc_ref = rest[-2:]
        j = pl.program_id(0 if n_outer else 1)
        k = pl.program_id(2)

        def compute(b_ref):
            part = lax.dot_general(a_ref[...].astype(BF16), b_ref[...].astype(BF16), dims,
                                   preferred_element_type=F32)
            if nk == 1:
                o_ref[...] = part.astype(out_dtype)
            else:
                @pl.when(k == 0)
                def _():
                    acc_ref[...] = part

                @pl.when(k > 0)
                def _():
                    acc_ref[...] += part

                @pl.when(k == nk - 1)
                def _():
                    o_ref[...] = acc_ref[...].astype(out_dtype)

        if nb == 1:
            compute(b_refs[0])
        else:
            for p in range(nb):
                pl.when(jnp.logical_and(j >= starts[p], j < starts[p] + counts[p]))(
                    lambda b_ref=b_refs[p]: compute(b_ref))

    return pl.pallas_call(
        body, name=name,
        out_shape=jax.ShapeDtypeStruct((M, N), out_dtype),
        grid=(N // tn, M // tm, nk) if n_outer else (M // tm, N // tn, nk),
        in_specs=[a_spec] + b_specs + ([] if after is None else [pl.BlockSpec(memory_space=pl.ANY)]),
        out_specs=pl.BlockSpec((tm, tn), ij(lambda i, j, k: (i, j))),
        scratch_shapes=[pltpu.VMEM((tm, tn) if nk > 1 else (8, 128), F32)],
        compiler_params=_cp(("parallel", "parallel", "arbitrary")),
    )(*([a] + b_parts + ([] if after is None else [after])))


def _row_tile(S):
    return min(512, S)


def _in_proj(x, g, sc, sh, w_in):
    S, D = x.shape
    N = w_in.shape[1]
    tm = min(512, S)

    def body(x_ref, g_ref, sc_ref, sh_ref, w_ref, h_ref, z_ref):
        xv = x_ref[...]
        r = lax.rsqrt(jnp.mean(xv * xv, axis=-1, keepdims=True) + EPS)
        h = (xv * r * (g_ref[...] * (1.0 + sc_ref[...])) + sh_ref[...]).astype(BF16)
        h_ref[...] = h
        z_ref[...] = jnp.dot(h, w_ref[...], preferred_element_type=F32).astype(BF16)

    row = pl.BlockSpec((tm, D), lambda i: (i, 0))
    par = _full((1, D))
    return pl.pallas_call(
        body, name="in_proj",
        out_shape=(jax.ShapeDtypeStruct((S, D), BF16), jax.ShapeDtypeStruct((S, N), BF16)), grid=(S // tm,),
        in_specs=[row, par, par, par, _resident((D, N))], out_specs=(row, pl.BlockSpec((tm, N), lambda i: (i, 0))),
        compiler_params=_cp(("parallel",)))(x, g, sc, sh, w_in)


def _fill_shifted(buf_ref, sh_ref):
    n = buf_ref.shape[0] - 8
    for s in range(1, 8):
        sh_ref[s, 0:n, :] = buf_ref[s:s + n, :]


def _window(buf_ref, sh_ref, off, n):
    s = off % 8
    return buf_ref[off:off + n, :] if s == 0 else sh_ref[s, off - s:off - s + n, :]


def _conv_fwd(z, conv_w, conv_b, ln_g, ln_b):
    S = z.shape[0]
    tm = min(256, S)
    sub = 32
    hb = tm // HALO

    def body(a_ref, g_ref, ha_ref, hg_ref, w_ref, b_ref, lg_ref, lb_ref, yc_ref, s_ref, ug_ref, sh_ref):
        i = pl.program_id(0)
        halo = ha_ref[...].astype(F32) * _sig(hg_ref[...].astype(F32))
        ug_ref[0:HALO, :] = jnp.where(i == 0, 0.0, halo)
        ug_ref[HALO:, :] = a_ref[...].astype(F32) * _sig(g_ref[...].astype(F32))
        _fill_shifted(ug_ref, sh_ref)
        for rb in range(tm // sub):
            acc = jnp.zeros((sub, CW), F32) + b_ref[...]
            for k in range(KW):
                off = rb * sub + HALO - (KW - 1) + k
                acc = acc + w_ref[k:k + 1, :] * _window(ug_ref, sh_ref, off, sub)
            yc_ref[rb * sub:(rb + 1) * sub, :] = acc
            mu = jnp.mean(acc, axis=-1, keepdims=True)
            cen = acc - mu
            rstd = lax.rsqrt(jnp.mean(cen * cen, axis=-1, keepdims=True) + EPS)
            ln = cen * rstd * lg_ref[...] + lb_ref[...]
            s_ref[rb * sub:(rb + 1) * sub, :] = (ln * _sig(ln)).astype(BF16)

    prev = lambda i: (jnp.maximum(i * hb - 1, 0), 0)
    return pl.pallas_call(
        body, name="conv_fwd",
        out_shape=(jax.ShapeDtypeStruct((S, CW), F32), jax.ShapeDtypeStruct((S, CW), BF16)),
        grid=(S // tm,),
        in_specs=[pl.BlockSpec((tm, CW), lambda i: (i, 0)), pl.BlockSpec((tm, CW), lambda i: (i, 1)),
                  pl.BlockSpec((HALO, CW), prev), pl.BlockSpec((HALO, CW), lambda i: (jnp.maximum(i * hb - 1, 0), 1)),
                  _full((KW, CW)), _full((1, CW)), _full((1, CW)), _full((1, CW))],
        out_specs=(pl.BlockSpec((tm, CW), lambda i: (i, 0)), pl.BlockSpec((tm, CW), lambda i: (i, 0))),
        scratch_shapes=[pltpu.VMEM((tm + HALO, CW), F32), pltpu.VMEM((8, tm + HALO, CW), F32)],
        compiler_params=_cp(("parallel",)))(z, z, z, z, conv_w, conv_b, ln_g, ln_b)


def _conv_bwd_ln(dyconv, w_cp, yc, ln_g, ln_b):
    S = yc.shape[0]
    tm = _row_tile(S)

    def body(dy_ref, w_ref, yc_ref, lg_ref, lb_ref, dyc_ref, dlg_ref, dlb_ref, dcb_ref):
        i = pl.program_id(0)
        dsc = lax.dot_general(dy_ref[...], w_ref[...], (((1,), (1,)), ((), ())), preferred_element_type=F32)
        yc_v = yc_ref[...]
        mu = jnp.mean(yc_v, axis=-1, keepdims=True)
        cen = yc_v - mu
        rstd = lax.rsqrt(jnp.mean(cen * cen, axis=-1, keepdims=True) + EPS)
        yn = cen * rstd
        ln = yn * lg_ref[...] + lb_ref[...]
        sl = _sig(ln)
        dln = dsc * (sl * (1.0 + ln * (1.0 - sl)))
        dyn = dln * lg_ref[...]
        dyc = rstd * (dyn - jnp.mean(dyn, axis=-1, keepdims=True)
                      - yn * jnp.mean(dyn * yn, axis=-1, keepdims=True))
        dyc_ref[...] = dyc

        @pl.when(i == 0)
        def _():
            dlg_ref[...] = jnp.zeros_like(dlg_ref)
            dlb_ref[...] = jnp.zeros_like(dlb_ref)
            dcb_ref[...] = jnp.zeros_like(dcb_ref)

        dlg_ref[...] += _colsum8(dln * yn)
        dlb_ref[...] += _colsum8(dln)
        dcb_ref[...] += _colsum8(dyc)

    row = pl.BlockSpec((tm, CW), lambda i: (i, 0))
    acc = jax.ShapeDtypeStruct((8, CW), F32)
    return pl.pallas_call(
        body, name="conv_bwd_ln",
        out_shape=(jax.ShapeDtypeStruct((S, CW), F32), acc, acc, acc), grid=(S // tm,),
        in_specs=[pl.BlockSpec((tm, D_MODEL), lambda i: (i, 0)), _full((CW, D_MODEL)), row, _full((1, CW)),
                  _full((1, CW))],
        out_specs=(row, _full((8, CW)), _full((8, CW)), _full((8, CW))),
        compiler_params=_cp(("arbitrary",)))(dyconv, w_cp, yc, ln_g, ln_b)


def _conv_bwd(dyc, z, conv_w):
    S = z.shape[0]
    tm = min(256, S)
    sub = 32
    hb = tm // HALO
    nt = S // tm

    def body(d_ref, dn_ref, a_ref, g_ref, ha_ref, hg_ref, w_ref, dz_ref, dw_ref, ug_ref, dy_ref, ugs_ref, dys_ref):
        i = pl.program_id(0)
        halo = ha_ref[...].astype(F32) * _sig(hg_ref[...].astype(F32))
        ug_ref[0:HALO, :] = jnp.where(i == 0, 0.0, halo)
        a = a_ref[...].astype(F32)
        sg = _sig(g_ref[...].astype(F32))
        ug_ref[HALO:, :] = a * sg
        dy_ref[0:tm, :] = d_ref[...]
        dy_ref[tm:, :] = jnp.where(i == nt - 1, 0.0, dn_ref[...])
        _fill_shifted(ug_ref, ugs_ref)
        _fill_shifted(dy_ref, dys_ref)

        @pl.when(i == 0)
        def _():
            dw_ref[...] = jnp.zeros_like(dw_ref)

        for rb in range(tm // sub):
            r0 = rb * sub
            acc = jnp.zeros((sub, CW), F32)
            dyc_b = dy_ref[r0:r0 + sub, :]
            for k in range(KW):
                up = r0 + (KW - 1) - k
                acc = acc + w_ref[k:k + 1, :] * _window(dy_ref, dys_ref, up, sub)
                off = r0 + HALO - (KW - 1) + k
                dw_ref[k * 8:(k + 1) * 8, :] += _colsum8(dyc_b * _window(ug_ref, ugs_ref, off, sub))
            a_b = a[r0:r0 + sub, :]
            sg_b = sg[r0:r0 + sub, :]
            dz_ref[r0:r0 + sub, 0:CW] = (acc * sg_b).astype(BF16)
            dz_ref[r0:r0 + sub, CW:2 * CW] = (acc * a_b * sg_b * (1.0 - sg_b)).astype(BF16)

    return pl.pallas_call(
        body, name="conv_bwd",
        out_shape=(jax.ShapeDtypeStruct((S, 2 * CW), BF16), jax.ShapeDtypeStruct((KW * 8, CW), F32)),
        grid=(nt,),
        in_specs=[pl.BlockSpec((tm, CW), lambda i: (i, 0)),
                  pl.BlockSpec((HALO, CW), lambda i: (jnp.minimum((i + 1) * hb, nt * hb - 1), 0)),
                  pl.BlockSpec((tm, CW), lambda i: (i, 0)), pl.BlockSpec((tm, CW), lambda i: (i, 1)),
                  pl.BlockSpec((HALO, CW), lambda i: (jnp.maximum(i * hb - 1, 0), 0)),
                  pl.BlockSpec((HALO, CW), lambda i: (jnp.maximum(i * hb - 1, 0), 1)),
                  _full((KW, CW))],
        out_specs=(pl.BlockSpec((tm, 2 * CW), lambda i: (i, 0)), _full((KW * 8, CW))),
        scratch_shapes=[pltpu.VMEM((tm + HALO, CW), F32), pltpu.VMEM((tm + HALO, CW), F32),
                        pltpu.VMEM((8, tm + HALO, CW), F32), pltpu.VMEM((8, tm + HALO, CW), F32)],
        compiler_params=_cp(("arbitrary",)))(dyc, dyc, z, z, z, z, conv_w)


_GELU_C = math.sqrt(2.0 / math.pi)


def _gelu(x):
    return 0.5 * x * (1.0 + jnp.tanh(_GELU_C * (x + 0.044715 * x * x * x)))


def _gelu_grad(x):
    t = jnp.tanh(_GELU_C * (x + 0.044715 * x * x * x))
    return 0.5 * (1.0 + t) + 0.5 * x * (1.0 - t * t) * (_GELU_C * (1.0 + 3 * 0.044715 * x * x))


_NCL = 4
_UC = CW // _NCL
_LW = NST // _NCL
_CS = 2 * _LW


def _ssm_fwd(z, bb, cm, d, tab):
    S = z.shape[0]
    tm = min(512, S)

    def body(u_ref, bb_ref, cm_ref, d_ref, t_ref, x_ref, ys_ref, yg_ref, car_ref):
        i = pl.program_id(0)

        @pl.when(i == 0)
        def _():
            car_ref[...] = jnp.zeros_like(car_ref)

        u16 = u_ref[...]
        u = u16.astype(F32)
        for c in range(_NCL):
            lre = pl.ds(c * _CS, _LW)
            lim = pl.ds(c * _CS + _LW, _LW)
            tl = pl.ds(c * _LW, _LW)
            x_ref[:, c * _CS:(c + 1) * _CS] = jnp.dot(u16[:, c * _UC:(c + 1) * _UC], bb_ref[c],
                                                      preferred_element_type=F32)

            def blk(j, car):
                cr, ci = car
                rows = pl.ds(pl.multiple_of(j * 8, 8), 8)
                r = x_ref[rows, lre]
                im = x_ref[rows, lim]
                for lvl, s in enumerate((1, 2, 4)):
                    mr = t_ref[16 * lvl:16 * lvl + 8, tl]
                    mi = t_ref[16 * lvl + 8:16 * lvl + 16, tl]
                    sr = pltpu.roll(r, s, 0)
                    si = pltpu.roll(im, s, 0)
                    r, im = r + (mr * sr - mi * si), im + (mr * si + mi * sr)
                pr = t_ref[48:56, tl]
                pi_ = t_ref[56:64, tl]
                r, im = r + (pr * cr - pi_ * ci), im + (pr * ci + pi_ * cr)
                x_ref[rows, lre] = r
                x_ref[rows, lim] = im
                return (jnp.broadcast_to(r[7:8, :], (8, _LW)), jnp.broadcast_to(im[7:8, :], (8, _LW)))

            cr, ci = lax.fori_loop(0, tm // 8, blk, (car_ref[:, lre], car_ref[:, lim]))
            car_ref[:, lre] = cr
            car_ref[:, lim] = ci
            cols = slice(c * _UC, (c + 1) * _UC)
            ys = jnp.dot(x_ref[:, c * _CS:(c + 1) * _CS].astype(BF16), cm_ref[c], preferred_element_type=F32)
            ys = ys + d_ref[:, cols] * u[:, cols]
            ys_ref[:, cols] = ys
            yg_ref[:, cols] = _gelu(ys).astype(BF16)

    return pl.pallas_call(
        body, name="ssm_fwd",
        out_shape=(jax.ShapeDtypeStruct((S, 2 * NST), F32), jax.ShapeDtypeStruct((S, CW), F32),
                   jax.ShapeDtypeStruct((S, CW), BF16)),
        grid=(S // tm,),
        in_specs=[pl.BlockSpec((tm, CW), lambda i: (i, 2)), _full((_NCL, _UC, _CS)), _full((_NCL, _CS, _UC)),
                  _full((1, CW)), _full((64, NST))],
        out_specs=(pl.BlockSpec((tm, 2 * NST), lambda i: (i, 0)), pl.BlockSpec((tm, CW), lambda i: (i, 0)),
                   pl.BlockSpec((tm, CW), lambda i: (i, 0))),
        scratch_shapes=[pltpu.VMEM((8, 2 * NST), F32)],
        compiler_params=_cp(("arbitrary",)))(z, bb, cm, d, tab)


def _ssm_bwd(dzz, w_glu, ys, z, xs, cmt, bbt, d, tab, after):
    S = z.shape[0]
    tm = min(512, S)
    nt = S // tm
    tdims = (((0,), (0,)), ((), ()))

    def body(dzz_ref, wglu_ref, ys_ref, u_ref, x_ref, cmt_ref, bbt_ref, d_ref, t_ref, after_ref,
             du_ref, de_ref, dd_ref, dc_hbm, dbb_hbm, car_ref, lam_ref, dc_ref, dbb_ref):
        i = pl.program_id(0)

        @pl.when(i == 0)
        def _():
            car_ref[...] = jnp.zeros_like(car_ref)
            de_ref[...] = jnp.zeros_like(de_ref)
            dd_ref[...] = jnp.zeros_like(dd_ref)
            dc_ref[...] = jnp.zeros_like(dc_ref)
            dbb_ref[...] = jnp.zeros_like(dbb_ref)

        u16 = u_ref[...]
        u = u16.astype(F32)
        dyg = lax.dot_general(dzz_ref[...], wglu_ref[...], (((1,), (1,)), ((), ())), preferred_element_type=F32)
        dys = dyg * _gelu_grad(ys_ref[...])
        dys16 = dys.astype(BF16)
        dd_ref[...] += _colsum8(dys * u)
        row = lax.broadcasted_iota(jnp.int32, (8, _LW), 0)
        for c in range(_NCL):
            lre = pl.ds(c * _CS, _LW)
            lim = pl.ds(c * _CS + _LW, _LW)
            tl = pl.ds(c * _LW, _LW)
            cols = slice(c * _UC, (c + 1) * _UC)
            span = slice(c * _CS, (c + 1) * _CS)
            dc_ref[cols, :] += lax.dot_general(dys16[:, cols], x_ref[:, span].astype(BF16), tdims,
                                               preferred_element_type=F32)
            lam_ref[...] = jnp.dot(dys16[:, cols], cmt_ref[c], preferred_element_type=F32)

            def blk(jj, car):
                cr, ci, ar, ai = car
                j = tm // 8 - 1 - jj
                rows = pl.ds(pl.multiple_of(j * 8, 8), 8)
                r = lam_ref[rows, 0:_LW]
                im = lam_ref[rows, _LW:_CS]
                for lvl, s in enumerate((1, 2, 4)):
                    mr = t_ref[16 * lvl:16 * lvl + 8, tl]
                    mi = t_ref[16 * lvl + 8:16 * lvl + 16, tl]
                    sr = pltpu.roll(r, 8 - s, 0)
                    si = pltpu.roll(im, 8 - s, 0)
                    r, im = r + (mr * sr - mi * si), im + (mr * si + mi * sr)
                pr = t_ref[48:56, tl]
                pi_ = t_ref[56:64, tl]
                r, im = r + (pr * cr - pi_ * ci), im + (pr * ci + pi_ * cr)
                lam_ref[rows, 0:_LW] = r
                lam_ref[rows, _LW:_CS] = im
                nr = jnp.where(row == 7, cr, pltpu.roll(r, 7, 0))
                ni = jnp.where(row == 7, ci, pltpu.roll(im, 7, 0))
                xr = x_ref[rows, lre]
                xi = x_ref[rows, lim]
                ar = ar + (nr * xr + ni * xi)
                ai = ai + (ni * xr - nr * xi)
                return (jnp.broadcast_to(r[0:1, :], (8, _LW)), jnp.broadcast_to(im[0:1, :], (8, _LW)), ar, ai)

            zero = jnp.zeros((8, _LW), F32)
            cr, ci, ar, ai = lax.fori_loop(0, tm // 8, blk, (car_ref[:, lre], car_ref[:, lim], zero, zero))
            car_ref[:, lre] = cr
            car_ref[:, lim] = ci
            de_ref[0:8, tl] += ar
            de_ref[8:16, tl] += ai
            lam16 = lam_ref[...].astype(BF16)
            dbb_ref[cols, :] += lax.dot_general(u16[:, cols], lam16, tdims, preferred_element_type=F32)
            du = jnp.dot(lam16, bbt_ref[c], preferred_element_type=F32) + dys[:, cols] * d_ref[:, cols]
            du_ref[:, cols] = du.astype(BF16)

        @pl.when(i == nt - 1)
        def _():
            pltpu.sync_copy(dc_ref, dc_hbm)
            pltpu.sync_copy(dbb_ref, dbb_hbm)

    rev = lambda i: (nt - 1 - i, 0)
    once = lambda shape: pl.BlockSpec(shape, lambda *_: (0,) * len(shape), pipeline_mode=pl.Buffered(1))
    cross = jax.ShapeDtypeStruct((CW, _CS), F32)
    return pl.pallas_call(
        body, name="ssm_bwd",
        out_shape=(jax.ShapeDtypeStruct((S, CW), BF16), jax.ShapeDtypeStruct((16, NST), F32),
                   jax.ShapeDtypeStruct((8, CW), F32), cross, cross),
        grid=(nt,),
        in_specs=[pl.BlockSpec((tm, 2 * D_MODEL), rev), once((CW, 2 * D_MODEL)), pl.BlockSpec((tm, CW), rev),
                  pl.BlockSpec((tm, CW), lambda i: (nt - 1 - i, 2)), pl.BlockSpec((tm, 2 * NST), rev),
                  once((_NCL, _UC, _CS)), once((_NCL, _CS, _UC)), _full((1, CW)), once((64, NST)),
                  pl.BlockSpec(memory_space=pl.ANY)],
        out_specs=(pl.BlockSpec((tm, CW), rev), _full((16, NST)), _full((8, CW)),
                   pl.BlockSpec(memory_space=pl.ANY), pl.BlockSpec(memory_space=pl.ANY)),
        scratch_shapes=[pltpu.VMEM((8, 2 * NST), F32), pltpu.VMEM((tm, _CS), F32),
                        pltpu.VMEM((CW, _CS), F32), pltpu.VMEM((CW, _CS), F32)],
        compiler_params=_cp(("arbitrary",)))(dzz, w_glu, ys, z, xs, cmt, bbt, d, tab, after)


def _ssm_prep(a_re, a_im, b_re, b_im, log_dt):
    dt = jnp.exp(log_dt.reshape(G))[:, None]
    mag = jnp.exp(dt * a_re)
    e_re, e_im = mag * jnp.cos(dt * a_im), mag * jnp.sin(dt * a_im)
    n_re, n_im = e_re - 1.0, e_im
    den = a_re * a_re + a_im * a_im
    q_re = (n_re * a_re + n_im * a_im) / den
    q_im = (n_im * a_re - n_re * a_im) / den
    bb_re = q_re[..., None] * b_re - q_im[..., None] * b_im
    bb_im = q_re[..., None] * b_im + q_im[..., None] * b_re
    return e_re, e_im, bb_re, bb_im


def _scan_tables(e_re, e_im, reverse):
    er = e_re.reshape(1, NST)
    ei = e_im.reshape(1, NST)
    if reverse:
        ei = -ei
    pows = [(er, ei)]
    for _ in range(7):
        pr, pi_ = pows[-1]
        pows.append((pr * er - pi_ * ei, pr * ei + pi_ * er))
    row = jnp.arange(8)[:, None]
    out = []
    for s in (1, 2, 4):
        pr, pi_ = pows[s - 1]
        keep = (row + s <= 7) if reverse else (row >= s)
        out += [jnp.where(keep, pr, 0.0), jnp.where(keep, pi_, 0.0)]
    allr = jnp.concatenate([p[0] for p in pows], 0)
    alli = jnp.concatenate([p[1] for p in pows], 0)
    if reverse:
        allr, alli = allr[::-1], alli[::-1]
    out += [allr, alli]
    return jnp.concatenate(out, 0).astype(F32)


def _block_diag_mats(bb_re, bb_im, c_re, c_im):
    gc = G // _NCL
    eye = jnp.eye(gc, dtype=F32)
    bre = jnp.einsum("cjph,jk->cjhkp", bb_re.reshape(_NCL, gc, P, H), eye).reshape(_NCL, _UC, _LW)
    bim = jnp.einsum("cjph,jk->cjhkp", bb_im.reshape(_NCL, gc, P, H), eye).reshape(_NCL, _UC, _LW)
    bb = jnp.concatenate([bre, bim], 2)
    cre = jnp.einsum("cjhp,jk->cjpkh", c_re.reshape(_NCL, gc, H, P), eye).reshape(_NCL, _LW, _UC)
    cim = jnp.einsum("cjhp,jk->cjpkh", c_im.reshape(_NCL, gc, H, P), eye).reshape(_NCL, _LW, _UC)
    cm = jnp.concatenate([cre, -cim], 1)
    return bb, cm


def _diag_blocks(cross):
    gc = G // _NCL
    six = cross.reshape(_NCL, gc, H, 2, gc, P)
    same = jnp.eye(gc, dtype=bool)[None, :, None, None, :, None]
    diag = jnp.sum(jnp.where(same, six, 0.0), axis=4)
    diag = jnp.moveaxis(diag, 3, 0).reshape(2, G, H, P)
    return diag[0], diag[1]


def _mix_fwd(scv, yg, z, x, w_cp, w_glu, w_out, g1, n2g, sc2, sh2):
    S = z.shape[0]
    tm = min(512, S)
    D = D_MODEL

    def body(s_ref, yg_ref, glc0_ref, glc1_ref, gls0_ref, gls1_ref, x_ref, wcp_ref, wglu_ref, wout_ref,
             g1_ref, n2_ref, sc_ref, sh_ref, yc_ref, zz_ref, m_ref, o_ref, x2_ref, h2_ref):
        y_conv = jnp.dot(s_ref[...], wcp_ref[...], preferred_element_type=F32)
        zz = jnp.dot(yg_ref[...], wglu_ref[...], preferred_element_type=F32)
        yc_ref[...] = y_conv.astype(BF16)
        zz_ref[...] = zz.astype(BF16)
        for half, (glc_ref, gls_ref) in enumerate(((glc0_ref, gls0_ref), (glc1_ref, gls1_ref))):
            lo, hi = half * CW, (half + 1) * CW
            y_ssm = zz[:, lo:hi] * _sig(zz[:, D + lo:D + hi])
            m_ref[:, lo:hi] = (_sig(glc_ref[...].astype(F32)) * y_conv[:, lo:hi]
                               + _sig(gls_ref[...].astype(F32)) * y_ssm).astype(BF16)
        o = jnp.dot(m_ref[...], wout_ref[...], preferred_element_type=F32)
        o_ref[...] = o.astype(BF16)
        xv = x_ref[...] + g1_ref[...] * o
        x2_ref[...] = xv
        r = lax.rsqrt(jnp.mean(xv * xv, axis=-1, keepdims=True) + EPS)
        h2_ref[...] = (xv * r * (n2_ref[...] * (1.0 + sc_ref[...])) + sh_ref[...]).astype(BF16)

    zb_ = lambda j: pl.BlockSpec((tm, CW), lambda i: (i, j))
    row = lambda w: pl.BlockSpec((tm, w), lambda i: (i, 0))
    par = _full((1, D))
    bf = lambda w: jax.ShapeDtypeStruct((S, w), BF16)
    return pl.pallas_call(
        body, name="mix_fwd",
        out_shape=(bf(D), bf(2 * D), bf(D), bf(D), jax.ShapeDtypeStruct((S, D), F32), bf(D)),
        grid=(S // tm,),
        in_specs=[row(CW), row(CW), zb_(3), zb_(4), zb_(5), zb_(6), row(D), _resident((CW, D)),
                  _resident((CW, 2 * D)), _resident((D, D)), par, par, par, par],
        out_specs=(row(D), row(2 * D), row(D), row(D), row(D), row(D)),
        compiler_params=_cp(("parallel",)))(scv, yg, z, z, z, z, x, w_cp, w_glu, w_out, g1, n2g, sc2, sh2)


def _mix_bwd(do, w_out, z, zz, y_conv, after):
    S = z.shape[0]
    tm = min(512, S)
    D = D_MODEL

    def body(do_ref, w_ref, glc0_ref, glc1_ref, gls0_ref, gls1_ref, za_ref, zb_ref, yc_ref, after_ref,
             dyc_ref, dgl_ref, dzz_ref):
        dm = lax.dot_general(do_ref[...], w_ref[...], (((1,), (1,)), ((), ())), preferred_element_type=F32)
        for half, (glc_ref, gls_ref) in enumerate(((glc0_ref, gls0_ref), (glc1_ref, gls1_ref))):
            lo, hi = half * CW, (half + 1) * CW
            dm_v = dm[:, lo:hi]
            sgc = _sig(glc_ref[...].astype(F32))
            sgs = _sig(gls_ref[...].astype(F32))
            szb = _sig(zb_ref[:, lo:hi].astype(F32))
            za = za_ref[:, lo:hi].astype(F32)
            dyc_ref[:, lo:hi] = (dm_v * sgc).astype(BF16)
            dgl_ref[:, lo:hi] = (dm_v * yc_ref[:, lo:hi].astype(F32) * sgc * (1.0 - sgc)).astype(BF16)
            dys = dm_v * sgs
            dgl_ref[:, D + lo:D + hi] = (dys * (za * szb) * (1.0 - sgs)).astype(BF16)
            dzz_ref[:, lo:hi] = (dys * szb).astype(BF16)
            dzz_ref[:, D + lo:D + hi] = (dys * za * szb * (1.0 - szb)).astype(BF16)

    zb_ = lambda j: pl.BlockSpec((tm, CW), lambda i: (i, j))
    wide = lambda j: pl.BlockSpec((tm, D), lambda i: (i, j))
    return pl.pallas_call(
        body, name="mix_bwd",
        out_shape=(jax.ShapeDtypeStruct((S, D), BF16), jax.ShapeDtypeStruct((S, 2 * D), BF16),
                   jax.ShapeDtypeStruct((S, 2 * D), BF16)),
        grid=(S // tm,),
        in_specs=[wide(0), _resident((D, D)), zb_(3), zb_(4), zb_(5), zb_(6), wide(0), wide(1), wide(0),
                  pl.BlockSpec(memory_space=pl.ANY)],
        out_specs=(wide(0), pl.BlockSpec((tm, 2 * D), lambda i: (i, 0)), pl.BlockSpec((tm, 2 * D), lambda i: (i, 0))),
        compiler_params=_cp(("parallel",)))(do, w_out, z, z, z, z, zz, zz, y_conv, after)


_FC = 1408


def _ffn_in_act(h2, w_fi):
    S, D = h2.shape
    tm = min(512, S)

    def body(h_ref, w_ref, f_ref, a_ref):
        hv = h_ref[...]
        for c in range(FH // _FC):
            lo, hi = c * _FC, (c + 1) * _FC
            g = jnp.dot(hv, w_ref[:, lo:hi], preferred_element_type=F32)
            u = jnp.dot(hv, w_ref[:, FH + lo:FH + hi], preferred_element_type=F32)
            f_ref[:, lo:hi] = g.astype(BF16)
            f_ref[:, FH + lo:FH + hi] = u.astype(BF16)
            a_ref[:, lo:hi] = (g * _sig(g) * u).astype(BF16)

    return pl.pallas_call(
        body, name="ffn_in_act",
        out_shape=(jax.ShapeDtypeStruct((S, 2 * FH), BF16), jax.ShapeDtypeStruct((S, FH), BF16)),
        grid=(S // tm,),
        in_specs=[pl.BlockSpec((tm, D), lambda i: (i, 0)), _resident((D, 2 * FH))],
        out_specs=(pl.BlockSpec((tm, 2 * FH), lambda i: (i, 0)), pl.BlockSpec((tm, FH), lambda i: (i, 0))),
        compiler_params=_cp(("parallel",)))(h2, w_fi)


def _ffn_bwd(do2, w_fo, f, after):
    S, D = do2.shape
    tm = min(512, S)

    def body(d_ref, w_ref, f_ref, after_ref, df_ref):
        dv = d_ref[...]
        for c in range(FH // _FC):
            lo, hi = c * _FC, (c + 1) * _FC
            dact = lax.dot_general(dv, w_ref[lo:hi, :], (((1,), (1,)), ((), ())), preferred_element_type=F32)
            g = f_ref[:, lo:hi].astype(F32)
            u = f_ref[:, FH + lo:FH + hi].astype(F32)
            sg = _sig(g)
            df_ref[:, lo:hi] = (dact * u * (sg * (1.0 + g * (1.0 - sg)))).astype(BF16)
            df_ref[:, FH + lo:FH + hi] = (dact * g * sg).astype(BF16)

    return pl.pallas_call(
        body, name="ffn_bwd", out_shape=jax.ShapeDtypeStruct((S, 2 * FH), BF16), grid=(S // tm,),
        in_specs=[pl.BlockSpec((tm, D), lambda i: (i, 0)), _resident((FH, D)),
                  pl.BlockSpec((tm, 2 * FH), lambda i: (i, 0)), pl.BlockSpec(memory_space=pl.ANY)],
        out_specs=pl.BlockSpec((tm, 2 * FH), lambda i: (i, 0)),
        compiler_params=_cp(("parallel",)))(do2, w_fo, f, after)


def _ffn_out_final(x2, act, w_fo, g2, fg, tgt):
    S, D = x2.shape
    tm = min(512, S)

    def body(x2_ref, a_ref, w_ref, g2_ref, fg_ref, t_ref, dx3_ref, do2_ref, ls_ref, dfg_ref, dg2_ref):
        i = pl.program_id(0)

        @pl.when(i == 0)
        def _():
            ls_ref[...] = jnp.zeros_like(ls_ref)
            dfg_ref[...] = jnp.zeros_like(dfg_ref)
            dg2_ref[...] = jnp.zeros_like(dg2_ref)

        o2 = jnp.dot(a_ref[...], w_ref[...], preferred_element_type=F32)
        x3 = x2_ref[...] + g2_ref[...] * o2
        r = lax.rsqrt(jnp.mean(x3 * x3, axis=-1, keepdims=True) + EPS)
        xn = x3 * r
        err = xn * fg_ref[...] - t_ref[...]
        dy = err * (1.0 / D)
        dxn = dy * fg_ref[...]
        dx3 = r * (dxn - xn * jnp.mean(dxn * xn, axis=-1, keepdims=True))
        dx3_ref[...] = dx3
        do2_ref[...] = (dx3 * g2_ref[...]).astype(BF16)
        e2 = _colsum8(err * err)
        lanes = e2[:, 0:128]
        for q in range(1, D // 128):
            lanes = lanes + e2[:, q * 128:(q + 1) * 128]
        ls_ref[...] += lanes * (0.5 / D)
        dfg_ref[...] += _colsum8(dy * xn)
        dg2_ref[...] += _colsum8(dx3 * o2)

    row = pl.BlockSpec((tm, D), lambda i: (i, 0))
    par = _full((1, D))
    return pl.pallas_call(
        body, name="final_loss",
        out_shape=(jax.ShapeDtypeStruct((S, D), F32), jax.ShapeDtypeStruct((S, D), BF16),
                   jax.ShapeDtypeStruct((8, 128), F32), jax.ShapeDtypeStruct((8, D), F32),
                   jax.ShapeDtypeStruct((8, D), F32)),
        grid=(S // tm,), in_specs=[row, pl.BlockSpec((tm, FH), lambda i: (i, 0)), _resident((FH, D)), par, par, row],
        out_specs=(row, row, _full((8, 128)), _full((8, D)), _full((8, D))),
        compiler_params=_cp(("arbitrary",)))(x2, act, w_fo, g2, fg, tgt)


def _normmod_bwd(dsrc, w, xin, dres, g, sc, gate, o, after, name):
    S, D = xin.shape
    parts = list(dsrc) if isinstance(dsrc, (list, tuple)) else [dsrc]
    widths = [p.shape[1] for p in parts]
    K = sum(widths)
    tm = min(512, S)
    npart = len(parts)

    def body(*refs):
        ds_refs = refs[:npart]
        w_ref, x_ref, dr_ref, g_ref, sc_ref, gate_ref, o_ref, after_ref = refs[npart:npart + 8]
        dx_ref, do_ref, dsh_ref, dsc_ref, dg_ref, dgate_ref = refs[npart + 8:]
        i = pl.program_id(0)

        @pl.when(i == 0)
        def _():
            dsh_ref[...] = jnp.zeros_like(dsh_ref)
            dsc_ref[...] = jnp.zeros_like(dsc_ref)
            dg_ref[...] = jnp.zeros_like(dg_ref)
            dgate_ref[...] = jnp.zeros_like(dgate_ref)

        gv = g_ref[...]
        scale = 1.0 + sc_ref[...]
        xv = x_ref[...]
        r = lax.rsqrt(jnp.mean(xv * xv, axis=-1, keepdims=True) + EPS)
        xn = xv * r
        dh_v, col = None, 0
        for ds_ref, wd in zip(ds_refs, widths):
            t = lax.dot_general(ds_ref[...], w_ref[:, col:col + wd], (((1,), (1,)), ((), ())),
                                preferred_element_type=F32)
            dh_v = t if dh_v is None else dh_v + t
            col += wd
        dxn = dh_v * (gv * scale)
        dx = dr_ref[...] + r * (dxn - xn * jnp.mean(dxn * xn, axis=-1, keepdims=True))
        dx_ref[...] = dx
        do_ref[...] = (dx * gate_ref[...]).astype(BF16)
        hx = dh_v * xn
        dsh_ref[...] += _colsum8(dh_v)
        dsc_ref[...] += _colsum8(hx) * gv
        dg_ref[...] += _colsum8(hx) * scale
        dgate_ref[...] += _colsum8(dx * o_ref[...])

    row = pl.BlockSpec((tm, D), lambda i: (i, 0))
    par = _full((1, D))
    acc = jax.ShapeDtypeStruct((8, D), F32)
    return pl.pallas_call(
        body, name=name,
        out_shape=(jax.ShapeDtypeStruct((S, D), F32), jax.ShapeDtypeStruct((S, D), BF16), acc, acc, acc, acc),
        grid=(S // tm,),
        in_specs=[pl.BlockSpec((tm, wd), lambda i: (i, 0)) for wd in widths]
        + [_resident((D, K)), row, row, par, par, par, row, pl.BlockSpec(memory_space=pl.ANY)],
        out_specs=(row, row, _full((8, D)), _full((8, D)), _full((8, D)), _full((8, D))),
        compiler_params=_cp(("arbitrary",)))(*parts, w, xin, dres, g, sc, gate, o, after)


def _me():
    return lax.axis_index("x"), lax.axis_index("y"), lax.axis_index("c")


def _allgather8(v, name, after=()):
    R, C = v.shape
    after = list(after)

    def body(v_ref, *rest):
        out_ref, send_sems, recv_sems, local_sem = rest[len(after):]
        x, y, c = _me()
        mine = pltpu.make_async_copy(v_ref, out_ref.at[4 * x + 2 * y + c], local_sem)
        mine.start()
        copies = []
        for k in range(1, N_DEV):
            fx, fy, fc = (k >> 2) & 1, (k >> 1) & 1, k & 1
            peer = (x ^ fx, y ^ fy, c ^ fc)
            copies.append(pltpu.make_async_remote_copy(
                src_ref=v_ref, dst_ref=out_ref.at[4 * x + 2 * y + c],
                send_sem=send_sems.at[k - 1], recv_sem=recv_sems.at[k - 1],
                device_id=peer, device_id_type=MESH))
        for cp in copies:
            cp.start()
        for k in range(1, N_DEV):
            fx, fy, fc = (k >> 2) & 1, (k >> 1) & 1, k & 1
            src_slot = 4 * (x ^ fx) + 2 * (y ^ fy) + (c ^ fc)
            pltpu.make_async_remote_copy(
                src_ref=v_ref, dst_ref=out_ref.at[src_slot],
                send_sem=send_sems.at[k - 1], recv_sem=recv_sems.at[k - 1],
                device_id=(x ^ fx, y ^ fy, c ^ fc), device_id_type=MESH).wait_recv()
        for cp in copies:
            cp.wait_send()
        mine.wait()

    return pl.pallas_call(
        body, name=name, out_shape=jax.ShapeDtypeStruct((N_DEV, R, C), v.dtype),
        in_specs=[pl.BlockSpec(memory_space=pltpu.VMEM)] + [pl.BlockSpec(memory_space=pl.ANY)] * len(after),
        out_specs=pl.BlockSpec(memory_space=pltpu.VMEM),
        scratch_shapes=[pltpu.SemaphoreType.DMA((N_DEV - 1,)), pltpu.SemaphoreType.DMA((N_DEV - 1,)),
                        pltpu.SemaphoreType.DMA],
        compiler_params=pltpu.CompilerParams(vmem_limit_bytes=VMEM_LIMIT))(v, *after)


_HBM = pl.BlockSpec(memory_space=pltpu.HBM)
_SEM = pl.BlockSpec(memory_space=pltpu.SEMAPHORE)
_EFFECT = pltpu.SideEffectType.DATAFLOW_SIDE_EFFECTING
_N_PEER = N_CHIP - 1


def _chip_part(ref, axis, n, chip):
    start = pl.multiple_of(chip * n, 8)
    return ref.at[pl.ds(start, n), :] if axis == 0 else ref.at[:, pl.ds(start, n)]


def _gather_copy(k, src_ref, land_ref, send_sems, recv_sems, axis, arriving):
    x, y, c = _me()
    px, py = x ^ ((k >> 1) & 1), y ^ (k & 1)
    chip = 2 * px + py if arriving else 2 * x + y
    return pltpu.make_async_remote_copy(
        src_ref=src_ref, dst_ref=_chip_part(land_ref, axis, src_ref.shape[axis], chip),
        send_sem=send_sems.at[k - 1], recv_sem=recv_sems.at[k - 1], device_id=(px, py, c), device_id_type=MESH)


def _scatter_copy(k, grad_ref, land_ref, send_sems, recv_sems, axis):
    x, y, c = _me()
    px, py = x ^ ((k >> 1) & 1), y ^ (k & 1)
    return pltpu.make_async_remote_copy(
        src_ref=_chip_part(grad_ref, axis, grad_ref.shape[axis] // N_CHIP, 2 * px + py), dst_ref=land_ref.at[k],
        send_sem=send_sems.at[k - 1], recv_sem=recv_sems.at[k - 1], device_id=(px, py, c), device_id_type=MESH)


def _scatter_own(grad_ref, land_ref, send_sems, axis):
    x, y, _ = _me()
    return pltpu.make_async_copy(_chip_part(grad_ref, axis, grad_ref.shape[axis] // N_CHIP, 2 * x + y),
                                 land_ref.at[0], send_sems.at[_N_PEER])


def _own_copy(src_ref, land_ref, sends, axis):
    x, y, _ = _me()
    return pltpu.make_async_copy(src_ref, _chip_part(land_ref, axis, src_ref.shape[axis], 2 * x + y),
                                 sends.at[_N_PEER])


def _gather_start(shards, axes, after, name):
    nw = len(shards)
    lands = []
    for s, ax in zip(shards, axes):
        shp = list(s.shape)
        shp[ax] *= N_CHIP
        lands.append(lax.empty(tuple(shp), s.dtype))

    def body(*refs):
        srcs, zones = refs[:nw], refs[nw:2 * nw]
        sends, recvs = refs[2 * nw + 1:3 * nw + 1], refs[3 * nw + 1:4 * nw + 1]
        token = refs[-1]
        for w in range(nw):
            for k in range(1, N_CHIP):
                _gather_copy(k, srcs[w], zones[w], sends[w], recvs[w], axes[w], False).start()
        for w in range(nw):
            _own_copy(srcs[w], zones[w], sends[w], axes[w]).start()
        token[...] = jnp.zeros_like(token)

    outs = pl.pallas_call(
        body, name=name,
        out_shape=tuple([pltpu.SemaphoreType.DMA((_N_PEER + 1,))] * nw + [pltpu.SemaphoreType.DMA((_N_PEER,))] * nw
                        + [pltpu.HBM(a.shape, a.dtype) for a in list(shards) + list(lands)]
                        + [jax.ShapeDtypeStruct((8, 128), F32)]),
        in_specs=[_HBM] * (2 * nw) + [pl.BlockSpec(memory_space=pl.ANY)],
        out_specs=tuple([_SEM] * (2 * nw) + [_HBM] * (2 * nw) + [pl.BlockSpec(memory_space=pltpu.VMEM)]),
        input_output_aliases={i: 2 * nw + i for i in range(2 * nw)},
        compiler_params=pltpu.CompilerParams(has_side_effects=_EFFECT),
    )(*([pltpu.with_memory_space_constraint(a, pltpu.HBM) for a in list(shards) + list(lands)] + [after]))
    per_weight = [(outs[w], outs[nw + w], outs[2 * nw + w], outs[3 * nw + w]) for w in range(nw)]
    return per_weight, outs[-1]


def _gather_wait(state, axis, after, name):
    send_sems, recv_sems, shard, land = state

    after = list(after) if isinstance(after, (list, tuple)) else [after]

    def body(src_ref, land_ref, sends, recvs, *rest):
        for k in range(1, N_CHIP):
            _gather_copy(k, src_ref, land_ref, sends, recvs, axis, False).wait_send()
            _gather_copy(k, src_ref, land_ref, sends, recvs, axis, True).wait_recv()
        _own_copy(src_ref, land_ref, sends, axis).wait()

    return pl.pallas_call(
        body, name=name, out_shape=(pltpu.HBM(shard.shape, shard.dtype), pltpu.HBM(land.shape, land.dtype)),
        in_specs=[_HBM, _HBM, _SEM, _SEM] + [pl.BlockSpec(memory_space=pl.ANY)] * len(after), out_specs=(_HBM, _HBM),
        input_output_aliases={0: 0, 1: 1},
        compiler_params=pltpu.CompilerParams(has_side_effects=_EFFECT),
    )(shard, land, send_sems, recv_sems, *after)[1]


def _half_rows(ref, c):
    k2 = ref.shape[0] // 2
    return pl.ds(pl.multiple_of(c * k2, 8), k2)


def _half_copy(k, shard_ref, land_ref, send_sems, recv_sems, arriving):
    x, y, c = _me()
    px, py = x ^ ((k >> 1) & 1), y ^ (k & 1)
    n = shard_ref.shape[1]
    chip = 2 * px + py if arriving else 2 * x + y
    return pltpu.make_async_remote_copy(
        src_ref=shard_ref.at[_half_rows(shard_ref, c), :],
        dst_ref=land_ref.at[_half_rows(land_ref, c), pl.ds(pl.multiple_of(chip * n, 128), n)],
        send_sem=send_sems.at[k - 1], recv_sem=recv_sems.at[k - 1], device_id=(px, py, c), device_id_type=MESH)


def _half_own(shard_ref, land_ref, send_sems):
    x, y, c = _me()
    n = shard_ref.shape[1]
    return pltpu.make_async_copy(
        shard_ref.at[_half_rows(shard_ref, c), :],
        land_ref.at[_half_rows(land_ref, c), pl.ds(pl.multiple_of((2 * x + y) * n, 128), n)], send_sems.at[_N_PEER])


def _half_gather_start(shard, after, name):
    K, n = shard.shape
    land = lax.empty((K, N_CHIP * n), shard.dtype)

    def body(shard_ref, land_ref, after_ref, sends, recvs, shard_thru, land_thru, token):
        for k in range(1, N_CHIP):
            _half_copy(k, shard_ref, land_ref, sends, recvs, False).start()
        _half_own(shard_ref, land_ref, sends).start()
        token[...] = jnp.zeros_like(token)

    outs = pl.pallas_call(
        body, name=name,
        out_shape=(pltpu.SemaphoreType.DMA((_N_PEER + 1,)), pltpu.SemaphoreType.DMA((_N_PEER,)),
                   pltpu.HBM(shard.shape, shard.dtype), pltpu.HBM(land.shape, land.dtype),
                   jax.ShapeDtypeStruct((8, 128), F32)),
        in_specs=[_HBM, _HBM, pl.BlockSpec(memory_space=pl.ANY)],
        out_specs=(_SEM, _SEM, _HBM, _HBM, pl.BlockSpec(memory_space=pltpu.VMEM)),
        input_output_aliases={0: 2, 1: 3},
        compiler_params=pltpu.CompilerParams(has_side_effects=_EFFECT),
    )(pltpu.with_memory_space_constraint(shard, pltpu.HBM), pltpu.with_memory_space_constraint(land, pltpu.HBM), after)
    return outs[:4], outs[4]


def _half_gather_wait(state, after, name):
    send_sems, recv_sems, shard, land = state
    after = list(after)

    def body(shard_ref, land_ref, sends, recvs, *rest):
        for k in range(1, N_CHIP):
            _half_copy(k, shard_ref, land_ref, sends, recvs, False).wait_send()
            _half_copy(k, shard_ref, land_ref, sends, recvs, True).wait_recv()
        _half_own(shard_ref, land_ref, sends).wait()

    return pl.pallas_call(
        body, name=name, out_shape=(pltpu.HBM(shard.shape, shard.dtype), pltpu.HBM(land.shape, land.dtype)),
        in_specs=[_HBM, _HBM, _SEM, _SEM] + [pl.BlockSpec(memory_space=pl.ANY)] * len(after), out_specs=(_HBM, _HBM),
        input_output_aliases={0: 0, 1: 1},
        compiler_params=pltpu.CompilerParams(has_side_effects=_EFFECT),
    )(shard, land, send_sems, recv_sems, *after)[1]


def _half_swap_copy(land_ref, send_sem, recv_sem, arriving):
    x, y, c = _me()
    rows = _half_rows(land_ref, 1 - c if arriving else c)
    return pltpu.make_async_remote_copy(src_ref=land_ref.at[rows, :], dst_ref=land_ref.at[rows, :], send_sem=send_sem,
                                        recv_sem=recv_sem, device_id=(x, y, 1 - c), device_id_type=MESH)


def _half_swap_start(land, name):
    def body(land_ref, send, recv, land_thru, token):
        _half_swap_copy(land_ref, send.at[0], recv.at[0], False).start()
        token[...] = jnp.zeros_like(token)

    sem = pltpu.SemaphoreType.DMA((1,))
    outs = pl.pallas_call(
        body, name=name,
        out_shape=(sem, sem, pltpu.HBM(land.shape, land.dtype), jax.ShapeDtypeStruct((8, 128), F32)),
        in_specs=[_HBM], out_specs=(_SEM, _SEM, _HBM, pl.BlockSpec(memory_space=pltpu.VMEM)),
        input_output_aliases={0: 2},
        compiler_params=pltpu.CompilerParams(has_side_effects=_EFFECT),
    )(pltpu.with_memory_space_constraint(land, pltpu.HBM))
    return outs[:3], outs[3]


def _half_swap_wait(state, after, name):
    send, recv, land = state

    def body(land_ref, send_ref, recv_ref, after_ref, got_ref):
        _half_swap_copy(land_ref, send_ref.at[0], recv_ref.at[0], False).wait_send()
        _half_swap_copy(land_ref, send_ref.at[0], recv_ref.at[0], True).wait_recv()

    return pl.pallas_call(
        body, name=name, out_shape=pltpu.HBM(land.shape, land.dtype),
        in_specs=[_HBM, _SEM, _SEM, pl.BlockSpec(memory_space=pl.ANY)], out_specs=_HBM,
        input_output_aliases={0: 0},
        compiler_params=pltpu.CompilerParams(has_side_effects=_EFFECT),
    )(land, send, recv, after)


def _all8_copy(k, v_ref, land_ref, send_sems, recv_sems, arriving):
    x, y, c = _me()
    px, py, pc = x ^ ((k >> 2) & 1), y ^ ((k >> 1) & 1), c ^ (k & 1)
    slot = 4 * px + 2 * py + pc if arriving else 4 * x + 2 * y + c
    return pltpu.make_async_remote_copy(
        src_ref=v_ref, dst_ref=land_ref.at[slot], send_sem=send_sems.at[k - 1], recv_sem=recv_sems.at[k - 1],
        device_id=(px, py, pc), device_id_type=MESH)


def _all8_own(v_ref, land_ref, send_sems):
    x, y, c = _me()
    return pltpu.make_async_copy(v_ref, land_ref.at[4 * x + 2 * y + c], send_sems.at[N_DEV - 1])


def _all8_start(v, name):
    land = lax.empty((N_DEV,) + v.shape, v.dtype)

    def body(v_ref, land_ref, sends, recvs, v_thru, land_thru, token):
        for k in range(1, N_DEV):
            _all8_copy(k, v_ref, land_ref, sends, recvs, False).start()
        _all8_own(v_ref, land_ref, sends).start()
        token[...] = jnp.zeros_like(token)

    outs = pl.pallas_call(
        body, name=name,
        out_shape=(pltpu.SemaphoreType.DMA((N_DEV,)), pltpu.SemaphoreType.DMA((N_DEV - 1,)),
                   pltpu.HBM(v.shape, v.dtype), pltpu.HBM(land.shape, land.dtype),
                   jax.ShapeDtypeStruct((8, 128), F32)),
        in_specs=[_HBM, _HBM], out_specs=(_SEM, _SEM, _HBM, _HBM, pl.BlockSpec(memory_space=pltpu.VMEM)),
        input_output_aliases={0: 2, 1: 3},
        compiler_params=pltpu.CompilerParams(has_side_effects=_EFFECT),
    )(pltpu.with_memory_space_constraint(v, pltpu.HBM), pltpu.with_memory_space_constraint(land, pltpu.HBM))
    return outs[:4], outs[4]


def _all8_wait(state, after, name):
    send_sems, recv_sems, v, land = state

    def body(v_ref, land_ref, sends, recvs, after_ref, v_dead, got_ref):
        for k in range(1, N_DEV):
            _all8_copy(k, v_ref, land_ref, sends, recvs, False).wait_send()
            _all8_copy(k, v_ref, land_ref, sends, recvs, True).wait_recv()
        _all8_own(v_ref, land_ref, sends).wait()

    return pl.pallas_call(
        body, name=name, out_shape=(pltpu.HBM(v.shape, v.dtype), pltpu.HBM(land.shape, land.dtype)),
        in_specs=[_HBM, _HBM, _SEM, _SEM, pl.BlockSpec(memory_space=pl.ANY)], out_specs=(_HBM, _HBM),
        input_output_aliases={0: 0, 1: 1},
        compiler_params=pltpu.CompilerParams(has_side_effects=_EFFECT),
    )(v, land, send_sems, recv_sems, after)[1]


def _swap_copy(w, src_ref, land_ref, send_sems, recv_sems):
    x, y, c = _me()
    return pltpu.make_async_remote_copy(src_ref=src_ref, dst_ref=land_ref, send_sem=send_sems.at[w],
                                        recv_sem=recv_sems.at[w], device_id=(x, y, 1 - c), device_id_type=MESH)


def _swap_start(arrs, after, name):
    nw = len(arrs)
    lands = [lax.empty(a.shape, a.dtype) for a in arrs]

    def body(*refs):
        srcs, zones = refs[:nw], refs[nw:2 * nw]
        sends, recvs = refs[2 * nw + 1], refs[2 * nw + 2]
        for w in range(nw):
            _swap_copy(w, srcs[w], zones[w], sends, recvs).start()
        refs[-1][...] = jnp.zeros_like(refs[-1])

    sem = pltpu.SemaphoreType.DMA((nw,))
    outs = pl.pallas_call(
        body, name=name,
        out_shape=tuple([sem, sem] + [pltpu.HBM(a.shape, a.dtype) for a in list(arrs) + lands]
                        + [jax.ShapeDtypeStruct((8, 128), F32)]),
        in_specs=[_HBM] * (2 * nw) + [pl.BlockSpec(memory_space=pl.ANY)],
        out_specs=tuple([_SEM, _SEM] + [_HBM] * (2 * nw) + [pl.BlockSpec(memory_space=pltpu.VMEM)]),
        input_output_aliases={i: 2 + i for i in range(2 * nw)},
        compiler_params=pltpu.CompilerParams(has_side_effects=_EFFECT),
    )(*([pltpu.with_memory_space_constraint(a, pltpu.HBM) for a in list(arrs) + lands] + [after]))
    return (outs[0], outs[1], outs[2:2 + nw], outs[2 + nw:2 + 2 * nw]), outs[-1]


def _swap_wait(state, after, name):
    send_sems, recv_sems, arrs, lands = state
    nw = len(arrs)

    def body(*refs):
        srcs, zones = refs[:nw], refs[nw:2 * nw]
        sends, recvs = refs[2 * nw], refs[2 * nw + 1]
        for w in range(nw):
            cp = _swap_copy(w, srcs[w], zones[w], sends, recvs)
            cp.wait_send()
            cp.wait_recv()

    outs = pl.pallas_call(
        body, name=name, out_shape=tuple(pltpu.HBM(a.shape, a.dtype) for a in list(arrs) + list(lands)),
        in_specs=[_HBM] * (2 * nw) + [_SEM, _SEM, pl.BlockSpec(memory_space=pl.ANY)],
        out_specs=tuple([_HBM] * (2 * nw)),
        input_output_aliases={i: i for i in range(2 * nw)},
        compiler_params=pltpu.CompilerParams(has_side_effects=_EFFECT),
    )(*arrs, *lands, send_sems, recv_sems, after)
    return list(outs[:nw]), list(outs[nw:])


def _scatter_start(grad, axis, name):
    shp = list(grad.shape)
    shp[axis] //= N_CHIP
    land = lax.empty((N_CHIP,) + tuple(shp), grad.dtype)

    def body(grad_ref, land_ref, sends, recvs, grad_thru, land_thru, token):
        for k in range(1, N_CHIP):
            _scatter_copy(k, grad_ref, land_ref, sends, recvs, axis).start()
        _scatter_own(grad_ref, land_ref, sends, axis).start()
        token[...] = jnp.zeros_like(token)

    outs = pl.pallas_call(
        body, name=name,
        out_shape=(pltpu.SemaphoreType.DMA((_N_PEER + 1,)), pltpu.SemaphoreType.DMA((_N_PEER,)),
                   pltpu.HBM(grad.shape, grad.dtype), pltpu.HBM(land.shape, land.dtype),
                   jax.ShapeDtypeStruct((8, 128), F32)),
        in_specs=[_HBM, _HBM], out_specs=(_SEM, _SEM, _HBM, _HBM, pl.BlockSpec(memory_space=pltpu.VMEM)),
        input_output_aliases={0: 2, 1: 3},
        compiler_params=pltpu.CompilerParams(has_side_effects=_EFFECT),
    )(pltpu.with_memory_space_constraint(grad, pltpu.HBM), pltpu.with_memory_space_constraint(land, pltpu.HBM))
    return outs[:4], outs[4]


def _scatter_wait(state, axis, after, name):
    send_sems, recv_sems, grad, land = state

    def body(grad_ref, land_ref, sends, recvs, after_ref, grad_dead, got_ref):
        for k in range(1, N_CHIP):
            cp = _scatter_copy(k, grad_ref, land_ref, sends, recvs, axis)
            cp.wait_send()
            cp.wait_recv()
        _scatter_own(grad_ref, land_ref, sends, axis).wait()

    return pl.pallas_call(
        body, name=name, out_shape=(pltpu.HBM(grad.shape, grad.dtype), pltpu.HBM(land.shape, land.dtype)),
        in_specs=[_HBM, _HBM, _SEM, _SEM, pl.BlockSpec(memory_space=pl.ANY)], out_specs=(_HBM, _HBM),
        input_output_aliases={0: 0, 1: 1},
        compiler_params=pltpu.CompilerParams(has_side_effects=_EFFECT),
    )(grad, land, send_sems, recv_sems, after)[1]


_C1 = 1.0 - B1 ** STEP
_C2 = 1.0 - B2 ** STEP


def _adam_math(w, g, m, v):
    m = B1 * m + (1.0 - B1) * g
    v = B2 * v + (1.0 - B2) * (g * g)
    delta = -LR * ((m / _C1) / (jnp.sqrt(v / _C2) + AEPS) + WD * w)
    return delta, m, v


def _adamw(w, m, v, groups, name):
    R, C = w.shape
    tr = R if R <= 256 else (128 if R % 128 == 0 else 176)
    assert R % tr == 0, (name, R)
    gparts = [p for grp in groups for p in grp]
    sizes = [len(grp) for grp in groups]
    ng = len(gparts)

    def body(*refs):
        w_ref, m_ref, v_ref = refs[:3]
        g_refs = list(refs[3:3 + ng])
        g_out, d_out, m_out, v_out = refs[3 + ng:]
        g = None
        for size in sizes:
            s = None
            for r in [g_refs.pop(0) for _ in range(size)]:
                terms = [r[q] for q in range(r.shape[0])] if len(r.shape) == 3 else [r[...]]
                for t in terms:
                    s = t.astype(F32) if s is None else s + t.astype(F32)
            g = s if g is None else g + s
        delta, mn, vn = _adam_math(w_ref[...], g, m_ref[...], v_ref[...])
        g_out[...] = g
        d_out[...] = delta
        m_out[...] = mn
        v_out[...] = vn

    blk = pl.BlockSpec((tr, C), lambda i: (i, 0))
    g_specs = [blk if p.ndim == 2 else pl.BlockSpec((p.shape[0], tr, C), lambda i: (0, i, 0)) for p in gparts]
    sds = jax.ShapeDtypeStruct((R, C), F32)
    return pl.pallas_call(
        body, name=name, out_shape=(sds, sds, sds, sds), grid=(R // tr,),
        in_specs=[blk, blk, blk] + g_specs, out_specs=(blk, blk, blk, blk),
        compiler_params=_cp(("parallel",)))(w, m, v, *gparts)


def _adamw_small(stack, names, wts, mom, var, sum_only, name):
    items, row = [], 0
    for n in names:
        shape = (KW, CW) if n == "conv_w" else wts[n].shape
        size = int(np.prod(shape))
        vec = len(shape) == 2 and shape[0] == 1 and n not in sum_only
        view = shape if vec else (-(-size // _PACK_COLS), _PACK_COLS)
        items.append((n, row, size, vec, view))
        row += _pack_rows(shape)
    upd = [it for it in items if it[0] not in sum_only]
    operands = [stack]
    for n, _, _, _, view in upd:
        operands += [d[n].reshape(view) for d in (wts, mom, var)]

    def grad(stack_ref, r0, nrows, ncols):
        g = stack_ref[0, r0:r0 + nrows, 0:ncols]
        for q in range(1, N_DEV):
            g = g + stack_ref[q, r0:r0 + nrows, 0:ncols]
        return g

    def body(*refs):
        stack_ref, ins, outs = refs[0], refs[1:1 + 3 * len(upd)], refs[1 + 3 * len(upd):]
        o = 0
        for idx, (n, r0, size, vec, view) in enumerate(upd):
            w_ref, m_ref, v_ref = ins[3 * idx:3 * idx + 3]
            g_out, d_out, m_out, v_out = outs[o:o + 4]
            o += 4
            if vec:
                pieces = [(j, j * _PACK_COLS, min((j + 1) * _PACK_COLS, size)) for j in range(-(-size // _PACK_COLS))]
            else:
                pieces = [(None, 0, _PACK_COLS)]
            for j, lo, hi in pieces:
                if vec:
                    g = grad(stack_ref, r0 + j, 1, hi - lo)
                    sl = (slice(None), slice(lo, hi))
                else:
                    g = grad(stack_ref, r0, view[0], _PACK_COLS)
                    sl = (slice(None), slice(None))
                delta, mn, vn = _adam_math(w_ref[sl], g, m_ref[sl], v_ref[sl])
                g_out[sl] = g
                d_out[sl] = delta
                m_out[sl] = mn
                v_out[sl] = vn
        for n, r0, size, vec, view in items:
            if n in sum_only:
                outs[o][...] = grad(stack_ref, r0, view[0], _PACK_COLS)
                o += 1

    out_shape = []
    for n, _, _, _, view in upd:
        out_shape += [jax.ShapeDtypeStruct(view, F32)] * 4
    out_shape += [jax.ShapeDtypeStruct(view, F32) for n, _, _, _, view in items if n in sum_only]
    vm = pl.BlockSpec(memory_space=pltpu.VMEM)
    res = pl.pallas_call(
        body, name=name, out_shape=tuple(out_shape), in_specs=[vm] * len(operands),
        out_specs=tuple([vm] * len(out_shape)),
        compiler_params=pltpu.CompilerParams(vmem_limit_bytes=VMEM_LIMIT))(*operands)
    updated = {n: tuple(r.reshape(wts[n].shape) for r in res[4 * i:4 * i + 4]) for i, (n, *_) in enumerate(upd)}
    sums = dict(zip([it[0] for it in items if it[0] in sum_only], res[4 * len(upd):]))
    return updated, sums


def _adamw_native(tensors, name):
    nt = len(tensors)

    def body(*refs):
        ins, outs = refs[:4 * nt], refs[4 * nt:]
        for t in range(nt):
            w_ref, m_ref, v_ref, g_ref = ins[4 * t:4 * t + 4]
            g = g_ref[...]
            delta, mn, vn = _adam_math(w_ref[...], g, m_ref[...], v_ref[...])
            outs[4 * t][...] = g
            outs[4 * t + 1][...] = delta
            outs[4 * t + 2][...] = mn
            outs[4 * t + 3][...] = vn

    vm = pl.BlockSpec(memory_space=pltpu.VMEM)
    flat = [a for tup in tensors for a in tup]
    res = pl.pallas_call(
        body, name=name, out_shape=tuple(jax.ShapeDtypeStruct(tup[0].shape, F32) for tup in tensors for _ in range(4)),
        in_specs=[vm] * len(flat), out_specs=tuple([vm] * (4 * nt)),
        compiler_params=pltpu.CompilerParams(vmem_limit_bytes=VMEM_LIMIT))(*flat)
    return [tuple(res[4 * t:4 * t + 4]) for t in range(nt)]


def _mod_shard(c_all, w_ada, b_ada_cols):
    n = w_ada.shape[1]
    tn = 512

    def body(c_ref, w_ref, b_ref, o_ref):
        cv = c_ref[...]
        ca = (cv * _sig(cv)).astype(BF16)
        o_ref[...] = jnp.dot(ca, w_ref[...].astype(BF16), preferred_element_type=F32) + b_ref[...]

    return pl.pallas_call(
        body, name="mod_shard", out_shape=jax.ShapeDtypeStruct((N_DEV, n), F32), grid=(n // tn,),
        in_specs=[_full((N_DEV, D_MODEL)), pl.BlockSpec((D_MODEL, tn), lambda j: (0, j)),
                  pl.BlockSpec((1, tn), lambda j: (0, j))],
        out_specs=pl.BlockSpec((N_DEV, tn), lambda j: (0, j)),
        compiler_params=_cp(("parallel",)))(c_all, w_ada, b_ada_cols)


def _ada_grad(c_all, dmod_cols, after):
    n = dmod_cols.shape[1]
    tn = 512

    def body(c_ref, d_ref, after_ref, o_ref):
        cv = c_ref[...]
        ca = cv * _sig(cv)
        o_ref[...] = lax.dot_general(ca, d_ref[...], (((0,), (0,)), ((), ())),
                                     preferred_element_type=F32, precision=lax.Precision.HIGHEST)

    return pl.pallas_call(
        body, name="ada_grad", out_shape=jax.ShapeDtypeStruct((D_MODEL, n), F32), grid=(n // tn,),
        in_specs=[_full((N_DEV, D_MODEL)), pl.BlockSpec((N_DEV, tn), lambda j: (0, j)),
                  pl.BlockSpec(memory_space=pl.ANY)],
        out_specs=pl.BlockSpec((D_MODEL, tn), lambda j: (0, j)),
        compiler_params=_cp(("parallel",)))(c_all, dmod_cols, after)


def _ssm_tables(W):
    e_re, e_im, bb_re, bb_im = _ssm_prep(W["ssm_a_re"], W["ssm_a_im"], W["ssm_b_re"], W["ssm_b_im"], W["ssm_log_dt"])
    bb, cm = _block_diag_mats(bb_re, bb_im, W["ssm_c_re"], W["ssm_c_im"])
    bb16, cm16 = bb.astype(BF16), cm.astype(BF16)
    return (bb16, cm16, jnp.swapaxes(bb16, 1, 2), jnp.swapaxes(cm16, 1, 2),
            _scan_tables(e_re, e_im, False), _scan_tables(e_re, e_im, True))


def _device_step(x, mod, W, tables, tgt, getw, put, early):
    sh1, sc1, g1, sh2, sc2, g2 = [mod[:, i * D_MODEL:(i + 1) * D_MODEL] for i in range(6)]
    bb16, cm16, bbt16, cmt16, tab_f, tab_b = tables

    w_in = getw("w_in", [mod, *tables])
    h1, z = _in_proj(x, W["norm1_g"], sc1, sh1, w_in)
    yc, scv = _conv_fwd(z, W["conv_w"], W["conv_b"], W["conv_ln_g"], W["conv_ln_b"])
    xs, ys, yg = _ssm_fwd(z, bb16, cm16, W["ssm_d"], tab_f)
    w_cp, w_glu, w_out = getw("conv_proj", scv), getw("ssm_glu", yg), getw("w_out", yg)
    y_conv, zz, merged, o, x2, h2 = _mix_fwd(scv, yg, z, x, w_cp, w_glu, w_out, g1, W["norm2_g"], sc2, sh2)
    w_fi = getw("w_ffn_in", h2)
    f, act = _ffn_in_act(h2, w_fi)
    w_fo = getw("w_ffn_out", act)
    dx3, do2, loss8, dfg8, dg2_8 = _ffn_out_final(x2, act, w_fo, g2, W["final_g"], tgt)

    sm = {}
    tok = put("w_ffn_out", _matmul(act, do2, "tn", 1408, 1024, 2048, BF16, "mm_g_ffn_out"))
    df = _ffn_bwd(do2, w_fo, f, tok)
    tok = put("w_ffn_in", _matmul(h2, df, "tn", 1024, 1408, 2048, BF16, "mm_g_ffn_in"))
    dx2, do, dsh2, dsc2, dn2, dg1_8 = _normmod_bwd(df, w_fi, x2, dx3, W["norm2_g"], sc2, g1, o, tok, "d_h2_normmod2_bwd")
    tok = put("w_out", _matmul(merged, do, "tn", 1024, 1024, 4096, BF16, "mm_g_w_out"))
    dyconv, dgl, dzz = _mix_bwd(do, w_out, z, zz, y_conv, tok)
    tok = put("ssm_glu", _matmul(yg, dzz, "tn", 512, 1024, 4096, BF16, "mm_g_ssm_glu"))
    tok = put("conv_proj", _matmul(scv, dyconv, "tn", 512, 1024, 4096, BF16, "mm_g_conv_proj", after=tok))
    du, de16, dd8, dc_full, dbb_full = _ssm_bwd(dzz, w_glu, ys, z, xs, cmt16, bbt16, W["ssm_d"], tab_b, tok)
    dyc, dlg8, dlb8, dcb8 = _conv_bwd_ln(dyconv, w_cp, yc, W["conv_ln_g"], W["conv_ln_b"])
    dz_conv, dcw = _conv_bwd(dyc, z, W["conv_w"])

    s8 = lambda a: jnp.sum(a, axis=0, keepdims=True)
    de = de16.reshape(2, 8, NST).sum(1)
    de_re, de_im = de[0].reshape(G, P), de[1].reshape(G, P)
    dc_re, dc_im = _diag_blocks(dc_full)
    dc_im = -dc_im
    dbb_re, dbb_im = [jnp.swapaxes(t, 1, 2) for t in _diag_blocks(dbb_full)]
    _, vjp = jax.vjp(_ssm_prep, W["ssm_a_re"], W["ssm_a_im"], W["ssm_b_re"], W["ssm_b_im"], W["ssm_log_dt"])
    sm["ssm_a_re"], sm["ssm_a_im"], sm["ssm_b_re"], sm["ssm_b_im"], sm["ssm_log_dt"] = vjp((de_re, de_im, dbb_re, dbb_im))
    sm["ssm_c_re"], sm["ssm_c_im"] = dc_re, dc_im
    sm["ssm_d"] = s8(dd8)
    sm["norm2_g"] = s8(dn2)
    sm["conv_b"], sm["conv_ln_g"], sm["conv_ln_b"] = s8(dcb8), s8(dlg8), s8(dlb8)
    sm["conv_w"] = dcw.reshape(KW, 8, CW).sum(1)
    sm["final_g"] = s8(dfg8)
    tok = early(sm)

    dz = [dz_conv, du, dgl]
    tok = put("w_in", _matmul(h1, dz, "tn", 1024, 512, 4096, BF16, "mm_g_w_in", after=tok))
    dx, _, dsh1, dsc1, dn1, _ = _normmod_bwd(dz, w_in, x, dx2, W["norm1_g"], sc1, g1, o, tok, "d_h1_normmod1_bwd")
    dmod = jnp.concatenate([s8(dsh1), s8(dsc1), s8(dg1_8), s8(dsh2), s8(dsc2), s8(dg2_8)], axis=1)
    return loss8, dx, s8(dn1), dmod


_BIG = ("w_in", "conv_proj", "ssm_glu", "w_out", "w_ffn_in", "w_ffn_out")
_BIG_AXIS = {"w_in": 1, "conv_proj": 1, "ssm_glu": 1, "w_out": 0, "w_ffn_in": 1, "w_ffn_out": 0}
_EARLY = ("conv_w", "conv_b", "conv_ln_g", "conv_ln_b", "ssm_a_re", "ssm_a_im", "ssm_b_re", "ssm_b_im", "ssm_c_re",
          "ssm_c_im", "ssm_d", "ssm_log_dt", "norm2_g", "final_g")
_LATE = ("norm1_g", "b_ada")
_S5_MATS = ("ssm_a_re", "ssm_a_im", "ssm_b_re", "ssm_b_im", "ssm_c_re", "ssm_c_im")
_ORDER = ("w_ada", "b_ada", "norm1_g", "w_in", "conv_w", "conv_b", "conv_ln_g", "conv_ln_b", "conv_proj",
          "ssm_a_re", "ssm_a_im", "ssm_b_re", "ssm_b_im", "ssm_c_re", "ssm_c_im", "ssm_d", "ssm_log_dt", "ssm_glu",
          "w_out", "norm2_g", "w_ffn_in", "w_ffn_out", "final_g")
_PACK_COLS = 1024


def _pack_rows(shape):
    return -(-int(np.prod(shape)) // (8 * _PACK_COLS)) * 8


def _pack(arrs):
    parts = []
    for a in arrs:
        flat = a.reshape(-1)
        n = _pack_rows(a.shape)
        parts.append(jnp.pad(flat, (0, n * _PACK_COLS - flat.shape[0])).reshape(n, _PACK_COLS))
    return jnp.concatenate(parts, 0)


def kernel(x, c, w_ada, b_ada, norm1_g, w_in, conv_w, conv_b, conv_ln_g, conv_ln_b, conv_proj, ssm_a_re, ssm_a_im, ssm_b_re, ssm_b_im, ssm_c_re, ssm_c_im, ssm_d, ssm_log_dt, ssm_glu, w_out, norm2_g, w_ffn_in, w_ffn_out, final_g, loss_target, m_w_ada, m_b_ada, m_norm1_g, m_w_in, m_conv_w, m_conv_b, m_conv_ln_g, m_conv_ln_b, m_conv_proj, m_ssm_a_re, m_ssm_a_im, m_ssm_b_re, m_ssm_b_im, m_ssm_c_re, m_ssm_c_im, m_ssm_d, m_ssm_log_dt, m_ssm_glu, m_w_out, m_norm2_g, m_w_ffn_in, m_w_ffn_out, m_final_g, v_w_ada, v_b_ada, v_norm1_g, v_w_in, v_conv_w, v_conv_b, v_conv_ln_g, v_conv_ln_b, v_conv_proj, v_ssm_a_re, v_ssm_a_im, v_ssm_b_re, v_ssm_b_im, v_ssm_c_re, v_ssm_c_im, v_ssm_d, v_ssm_log_dt, v_ssm_glu, v_w_out, v_norm2_g, v_w_ffn_in, v_w_ffn_out, v_final_g):
    given = dict(locals())
    mx, my, mc = _me()
    chip = 2 * mx + my
    dev = 4 * mx + 2 * my + mc
    def canon(a):
        return a.reshape(1, -1) if a.ndim <= 2 else a[0]

    wts = {n: canon(given[n]) for n in _ORDER}
    mom = {n: canon(given["m_" + n]) for n in _ORDER}
    var = {n: canon(given["v_" + n]) for n in _ORDER}

    W = {n: wts[n] for n in _ORDER if n not in _BIG}
    rest = [n for n in _BIG if n != "w_in"]
    rest_shards = [wts[n].astype(BF16) for n in rest]
    state_in, token = _half_gather_start(wts["w_in"].astype(BF16), c, "gather_start_w_in")
    W["ssm_log_dt"] = wts["ssm_log_dt"] + token[0:1, 0:1]
    W["ssm_c_re"] = wts["ssm_c_re"] + token[0, 0]
    tables = _ssm_tables(W)

    c_all = _allgather8(jnp.broadcast_to(c, (8, D_MODEL)), "gather_c", after=[*tables, *rest_shards])[:, 0, :]
    n_ada = wts["w_ada"].shape[1]
    b_cols = lax.dynamic_slice(wts["b_ada"], (0, chip * n_ada), (1, n_ada))
    mod_cols = _mod_shard(c_all, wts["w_ada"], b_cols)
    halves = _half_gather_wait(state_in, [mod_cols], "gather_wait_w_in")
    state_in, token = _half_swap_start(halves, "gather_swap_start_w_in")
    mods = _allgather8(mod_cols, "gather_mod", after=[token])
    mod = jnp.concatenate([lax.dynamic_index_in_dim(mods[2 * q], dev, 0, keepdims=True) for q in range(N_CHIP)], axis=1)
    conv_w_full = _allgather8(jnp.pad(wts["conv_w"], ((0, 1), (0, 0))), "gather_conv_w", after=[token])
    W["conv_w"] = jnp.concatenate([conv_w_full[2 * q, :KW] for q in range(N_CHIP)], axis=1)
    w_in_full = _half_swap_wait(state_in, mod + W["conv_w"][0:1, 0:1], "gather_swap_wait_w_in")
    gstate, token = _gather_start(rest_shards, [_BIG_AXIS[n] for n in rest], w_in_full, "gather_start_rest")
    gstate = dict(zip(rest, gstate))
    mod = mod + token[0:1, 0:1]

    def getw(n, after):
        if n == "w_in":
            return w_in_full
        return _gather_wait(gstate[n], _BIG_AXIS[n], after, "gather_wait_" + n)

    sstate, estate = {}, []

    def put(n, g):
        sstate[n], tok = _scatter_start(g, _BIG_AXIS[n], "scatter_start_" + n)
        return tok

    first5 = [n for n in _BIG if n != "w_in"]

    def early(sm):
        state, tok = _all8_start(_pack([sm[n] for n in _EARLY]), "small_start")
        estate.append(state)
        held = [_scatter_wait(sstate[n], _BIG_AXIS[n], tok, "scatter_wait_" + n) for n in first5]
        state, tok = _swap_start(held, tok, "swap_start")
        estate.append(state)
        return tok

    loss8, dx, dn1, dmod = _device_step(x[0], mod, W, tables, loss_target[0], getw, put, early)

    held5, sib5 = _swap_wait(estate[1], dx, "swap_wait")
    outs = {}
    for i, n in enumerate(first5):
        outs[n] = _adamw(wts[n], mom[n], var[n], [[held5[i]], [sib5[i]]], "adamw_" + n)
    allp = _all8_wait(estate[0], dx, "small_wait")
    upd, sums = _adamw_small(allp, _EARLY, wts, mom, var, ("conv_w",) + _S5_MATS, "adamw_small")
    outs.update(upd)

    def swapped(n, a):
        return jnp.swapaxes(a, 1, 2) if n in ("ssm_b_re", "ssm_b_im") else a

    def summed(n):
        return swapped(n, sums[n].reshape(-1)[:wts[n].size].reshape(wts[n].shape))

    res = _adamw_native([(swapped(n, wts[n]), swapped(n, mom[n]), swapped(n, var[n]), summed(n)) for n in _S5_MATS],
                        "adamw_s5")
    for n, r in zip(_S5_MATS, res):
        outs[n] = tuple(swapped(n, a) for a in r)

    late = _allgather8(_pack([dn1, dmod, loss8]), "gather_late", after=[outs[n][1] for n in first5])
    n_late = _pack_rows((D_MODEL,)) + _pack_rows((6 * D_MODEL,))
    loss = jnp.sum(late[:, n_late:, :])
    late = late[:, :n_late, :]
    held_in = _scatter_wait(sstate["w_in"], _BIG_AXIS["w_in"], late, "scatter_wait_w_in")
    state_in, tok = _swap_start([held_in], late, "swap_start_w_in")

    r1 = _pack_rows((D_MODEL,))
    dmod_all = late[:, r1:, :].reshape(N_DEV, -1)[:, :6 * D_MODEL]
    dmod_cols = lax.dynamic_slice(dmod_all, (0, chip * n_ada), (N_DEV, n_ada))
    g_ada = _ada_grad(c_all, dmod_cols, tok)
    outs["w_ada"] = _adamw(wts["w_ada"], mom["w_ada"], var["w_ada"], [[g_ada]], "adamw_w_ada")
    upd, _ = _adamw_small(late, _LATE, wts, mom, var, (), "adamw_late")
    outs.update(upd)
    held_in, sib_in = _swap_wait(state_in, outs["w_ada"][1], "swap_wait_w_in")
    outs["w_in"] = _adamw(wts["w_in"], mom["w_in"], var["w_in"], [held_in, sib_in], "adamw_w_in")
    g_cw_full = sums["conv_w"].reshape(-1)[:KW * CW].reshape(KW, CW)
    g_cw = lax.dynamic_slice(g_cw_full, (0, chip * (CW // N_CHIP)), (KW, CW // N_CHIP))
    pad = lambda a: jnp.pad(a, ((0, 1), (0, 0)))
    r_cw = _adamw(pad(wts["conv_w"]), pad(mom["conv_w"]), pad(var["conv_w"]), [[pad(g_cw)]], "adamw_conv_w")
    outs["conv_w"] = tuple(r[:KW] for r in r_cw)

    def shaped(n, a):
        return a.reshape(given[n].shape)

    result = [loss, dx[None]]
    for q in range(4):
        result += [shaped(n, outs[n][q]) for n in _ORDER]
    return tuple(result)
```

```python
import math

import jax
import jax.numpy as jnp
import numpy as np
from jax import lax
from jax.experimental import pallas as pl
from jax.experimental.pallas import tpu as pltpu

F32 = jnp.float32
BF16 = jnp.bfloat16
EPS = 1e-6
D_MODEL = 1024
CW = 512
KW = 31
HALO = 32
G, P, H = 32, 64, 16
NST = G * P
FH = 2816
N_DEV = 8
N_CHIP = 4
VMEM_LIMIT = 56 * 1024 * 1024
LR, B1, B2, AEPS, WD, STEP = 0.001, 0.9, 0.999, 1e-08, 0.01, 10
MESH = pl.DeviceIdType.MESH


def _cp(sem=None):
    return pltpu.CompilerParams(dimension_semantics=sem, vmem_limit_bytes=VMEM_LIMIT)


def _sig(x):
    return jax.nn.sigmoid(x)


def _full(shape):
    return pl.BlockSpec(shape, lambda *_: (0,) * len(shape))


def _resident(shape):
    return pl.BlockSpec(shape, lambda *_: (0,) * len(shape), pipeline_mode=pl.Buffered(1))


def _colsum8(v):
    t, c = v.shape
    return jnp.sum(v.reshape(t // 8, 8, c), axis=0)


def _matmul(a, b, mode, tm, tn, tk, out_dtype, name, after=None, n_outer=False, m_cols=None):
    m0 = 0
    b_parts = list(b) if isinstance(b, (list, tuple)) else [b]
    if mode == "nn":
        (M, K), N = a.shape, b.shape[1]
    elif mode == "nt":
        (M, K), N = a.shape, b.shape[0]
    else:
        (K, M), N = a.shape, sum(p.shape[1] for p in b_parts)
        if m_cols is not None:
            m0, M = m_cols
    tm, tn, tk = min(tm, M), min(tn, N), min(tk, K)
    assert M % tm == 0 and N % tn == 0 and K % tk == 0 and m0 % tm == 0, (name, M, N, K, tm, tn, tk)
    assert len(b_parts) == 1 or (mode == "tn" and all(p.shape[1] % tn == 0 for p in b_parts)), name
    nk = K // tk
    mb = m0 // tm
    counts = [p.shape[1] // tn for p in b_parts] if mode == "tn" else [N // tn]
    starts = [sum(counts[:p]) for p in range(len(counts))]

    def ij(fn):
        return (lambda j, i, k: fn(i, j, k)) if n_outer else fn

    if mode == "nn":
        a_spec = pl.BlockSpec((tm, tk), ij(lambda i, j, k: (i, k)))
        b_spec = pl.BlockSpec((tk, tn), ij(lambda i, j, k: (k, j)))
        dims = (((1,), (0,)), ((), ()))
    elif mode == "nt":
        a_spec = pl.BlockSpec((tm, tk), ij(lambda i, j, k: (i, k)))
        b_spec = pl.BlockSpec((tn, tk), ij(lambda i, j, k: (j, k)))
        dims = (((1,), (1,)), ((), ()))
    else:
        a_spec = pl.BlockSpec((tk, tm), ij(lambda i, j, k: (k, i + mb)))
        dims = (((0,), (0,)), ((), ()))
    if mode == "tn":
        b_specs = [pl.BlockSpec((tk, tn), ij(lambda i, j, k, s=s, n=n: (k, jnp.clip(j - s, 0, n - 1))))
                   for s, n in zip(starts, counts)]
    else:
        b_specs = [b_spec]
    nb = len(b_parts)

    def body(a_ref, *rest):
        b_refs = rest[:nb]
        o_ref, acc_ref = rest[-2:]
        j = pl.program_id(0 if n_outer else 1)
        k = pl.program_id(2)

        def compute(b_ref):
            part = lax.dot_general(a_ref[...].astype(BF16), b_ref[...].astype(BF16), dims,
                                   preferred_element_type=F32)
            if nk == 1:
                o_ref[...] = part.astype(out_dtype)
            else:
                @pl.when(k == 0)
                def _():
                    acc_ref[...] = part

                @pl.when(k > 0)
                def _():
                    acc_ref[...] += part

                @pl.when(k == nk - 1)
                def _():
                    o_ref[...] = acc_ref[...].astype(out_dtype)

        if nb == 1:
            compute(b_refs[0])
        else:
            for p in range(nb):
                pl.when(jnp.logical_and(j >= starts[p], j < starts[p] + counts[p]))(
                    lambda b_ref=b_refs[p]: compute(b_ref))

    return pl.pallas_call(
        body, name=name,
        out_shape=jax.ShapeDtypeStruct((M, N), out_dtype),
        grid=(N // tn, M // tm, nk) if n_outer else (M // tm, N // tn, nk),
        in_specs=[a_spec] + b_specs + ([] if after is None else [pl.BlockSpec(memory_space=pl.ANY)]),
        out_specs=pl.BlockSpec((tm, tn), ij(lambda i, j, k: (i, j))),
        scratch_shapes=[pltpu.VMEM((tm, tn) if nk > 1 else (8, 128), F32)],
        compiler_params=_cp(("parallel", "parallel", "arbitrary")),
    )(*([a] + b_parts + ([] if after is None else [after])))


def _row_tile(S):
    return min(512, S)


def _in_proj(x, g, sc, sh, w_in):
    S, D = x.shape
    N = w_in.shape[1]
    tm = min(512, S)

    def body(x_ref, g_ref, sc_ref, sh_ref, w_ref, h_ref, z_ref):
        xv = x_ref[...]
        r = lax.rsqrt(jnp.mean(xv * xv, axis=-1, keepdims=True) + EPS)
        h = (xv * r * (g_ref[...] * (1.0 + sc_ref[...])) + sh_ref[...]).astype(BF16)
        h_ref[...] = h
        z_ref[...] = jnp.dot(h, w_ref[...], preferred_element_type=F32).astype(BF16)

    row = pl.BlockSpec((tm, D), lambda i: (i, 0))
    par = _full((1, D))
    return pl.pallas_call(
        body, name="in_proj",
        out_shape=(jax.ShapeDtypeStruct((S, D), BF16), jax.ShapeDtypeStruct((S, N), BF16)), grid=(S // tm,),
        in_specs=[row, par, par, par, _resident((D, N))], out_specs=(row, pl.BlockSpec((tm, N), lambda i: (i, 0))),
        compiler_params=_cp(("parallel",)))(x, g, sc, sh, w_in)


def _fill_shifted(buf_ref, sh_ref):
    n = buf_ref.shape[0] - 8
    for s in range(1, 8):
        sh_ref[s, 0:n, :] = buf_ref[s:s + n, :]


def _window(buf_ref, sh_ref, off, n):
    s = off % 8
    return buf_ref[off:off + n, :] if s == 0 else sh_ref[s, off - s:off - s + n, :]


def _conv_fwd(z, conv_w, conv_b, ln_g, ln_b):
    S = z.shape[0]
    tm = min(128, S)
    sub = 32
    hb = tm // HALO

    def body(a_ref, g_ref, ha_ref, hg_ref, w_ref, b_ref, lg_ref, lb_ref, yc_ref, s_ref, ug_ref, sh_ref):
        i = pl.program_id(0)
        halo = ha_ref[...].astype(F32) * _sig(hg_ref[...].astype(F32))
        ug_ref[0:HALO, :] = jnp.where(i == 0, 0.0, halo)
        ug_ref[HALO:, :] = a_ref[...].astype(F32) * _sig(g_ref[...].astype(F32))
        _fill_shifted(ug_ref, sh_ref)
        for rb in range(tm // sub):
            acc = jnp.zeros((sub, CW), F32) + b_ref[...]
            for k in range(KW):
                off = rb * sub + HALO - (KW - 1) + k
                acc = acc + w_ref[k:k + 1, :] * _window(ug_ref, sh_ref, off, sub)
            yc_ref[rb * sub:(rb + 1) * sub, :] = acc
            mu = jnp.mean(acc, axis=-1, keepdims=True)
            cen = acc - mu
            rstd = lax.rsqrt(jnp.mean(cen * cen, axis=-1, keepdims=True) + EPS)
            ln = cen * rstd * lg_ref[...] + lb_ref[...]
            s_ref[rb * sub:(rb + 1) * sub, :] = (ln * _sig(ln)).astype(BF16)

    prev = lambda i: (jnp.maximum(i * hb - 1, 0), 0)
    return pl.pallas_call(
        body, name="conv_fwd",
        out_shape=(jax.ShapeDtypeStruct((S, CW), F32), jax.ShapeDtypeStruct((S, CW), BF16)),
        grid=(S // tm,),
        in_specs=[pl.BlockSpec((tm, CW), lambda i: (i, 0)), pl.BlockSpec((tm, CW), lambda i: (i, 1)),
                  pl.BlockSpec((HALO, CW), prev), pl.BlockSpec((HALO, CW), lambda i: (jnp.maximum(i * hb - 1, 0), 1)),
                  _full((KW, CW)), _full((1, CW)), _full((1, CW)), _full((1, CW))],
        out_specs=(pl.BlockSpec((tm, CW), lambda i: (i, 0)), pl.BlockSpec((tm, CW), lambda i: (i, 0))),
        scratch_shapes=[pltpu.VMEM((tm + HALO, CW), F32), pltpu.VMEM((8, tm + HALO, CW), F32)],
        compiler_params=_cp(("parallel",)))(z, z, z, z, conv_w, conv_b, ln_g, ln_b)


def _conv_bwd_ln(dyconv, w_cp, yc, ln_g, ln_b):
    S = yc.shape[0]
    tm = _row_tile(S)

    def body(dy_ref, w_ref, yc_ref, lg_ref, lb_ref, dyc_ref, dlg_ref, dlb_ref, dcb_ref):
        i = pl.program_id(0)
        dsc = lax.dot_general(dy_ref[...], w_ref[...], (((1,), (1,)), ((), ())), preferred_element_type=F32)
        yc_v = yc_ref[...]
        mu = jnp.mean(yc_v, axis=-1, keepdims=True)
        cen = yc_v - mu
        rstd = lax.rsqrt(jnp.mean(cen * cen, axis=-1, keepdims=True) + EPS)
        yn = cen * rstd
        ln = yn * lg_ref[...] + lb_ref[...]
        sl = _sig(ln)
        dln = dsc * (sl * (1.0 + ln * (1.0 - sl)))
        dyn = dln * lg_ref[...]
        dyc = rstd * (dyn - jnp.mean(dyn, axis=-1, keepdims=True)
                      - yn * jnp.mean(dyn * yn, axis=-1, keepdims=True))
        dyc_ref[...] = dyc

        @pl.when(i == 0)
        def _():
            dlg_ref[...] = jnp.zeros_like(dlg_ref)
            dlb_ref[...] = jnp.zeros_like(dlb_ref)
            dcb_ref[...] = jnp.zeros_like(dcb_ref)

        dlg_ref[...] += _colsum8(dln * yn)
        dlb_ref[...] += _colsum8(dln)
        dcb_ref[...] += _colsum8(dyc)

    row = pl.BlockSpec((tm, CW), lambda i: (i, 0))
    acc = jax.ShapeDtypeStruct((8, CW), F32)
    return pl.pallas_call(
        body, name="conv_bwd_ln",
        out_shape=(jax.ShapeDtypeStruct((S, CW), F32), acc, acc, acc), grid=(S // tm,),
        in_specs=[pl.BlockSpec((tm, D_MODEL), lambda i: (i, 0)), _full((CW, D_MODEL)), row, _full((1, CW)),
                  _full((1, CW))],
        out_specs=(row, _full((8, CW)), _full((8, CW)), _full((8, CW))),
        compiler_params=_cp(("arbitrary",)))(dyconv, w_cp, yc, ln_g, ln_b)


def _conv_bwd(dyc, z, conv_w):
    S = z.shape[0]
    tm = min(128, S)
    sub = 32
    hb = tm // HALO
    nt = S // tm

    def body(d_ref, dn_ref, a_ref, g_ref, ha_ref, hg_ref, w_ref, dz_ref, dw_ref, ug_ref, dy_ref, ugs_ref, dys_ref):
        i = pl.program_id(0)
        halo = ha_ref[...].astype(F32) * _sig(hg_ref[...].astype(F32))
        ug_ref[0:HALO, :] = jnp.where(i == 0, 0.0, halo)
        a = a_ref[...].astype(F32)
        sg = _sig(g_ref[...].astype(F32))
        ug_ref[HALO:, :] = a * sg
        dy_ref[0:tm, :] = d_ref[...]
        dy_ref[tm:, :] = jnp.where(i == nt - 1, 0.0, dn_ref[...])
        _fill_shifted(ug_ref, ugs_ref)
        _fill_shifted(dy_ref, dys_ref)

        @pl.when(i == 0)
        def _():
            dw_ref[...] = jnp.zeros_like(dw_ref)

        for rb in range(tm // sub):
            r0 = rb * sub
            acc = jnp.zeros((sub, CW), F32)
            dyc_b = dy_ref[r0:r0 + sub, :]
            for k in range(KW):
                up = r0 + (KW - 1) - k
                acc = acc + w_ref[k:k + 1, :] * _window(dy_ref, dys_ref, up, sub)
                off = r0 + HALO - (KW - 1) + k
                dw_ref[k * 8:(k + 1) * 8, :] += _colsum8(dyc_b * _window(ug_ref, ugs_ref, off, sub))
            a_b = a[r0:r0 + sub, :]
            sg_b = sg[r0:r0 + sub, :]
            dz_ref[r0:r0 + sub, 0:CW] = (acc * sg_b).astype(BF16)
            dz_ref[r0:r0 + sub, CW:2 * CW] = (acc * a_b * sg_b * (1.0 - sg_b)).astype(BF16)

    return pl.pallas_call(
        body, name="conv_bwd",
        out_shape=(jax.ShapeDtypeStruct((S, 2 * CW), BF16), jax.ShapeDtypeStruct((KW * 8, CW), F32)),
        grid=(nt,),
        in_specs=[pl.BlockSpec((tm, CW), lambda i: (i, 0)),
                  pl.BlockSpec((HALO, CW), lambda i: (jnp.minimum((i + 1) * hb, nt * hb - 1), 0)),
                  pl.BlockSpec((tm, CW), lambda i: (i, 0)), pl.BlockSpec((tm, CW), lambda i: (i, 1)),
                  pl.BlockSpec((HALO, CW), lambda i: (jnp.maximum(i * hb - 1, 0), 0)),
                  pl.BlockSpec((HALO, CW), lambda i: (jnp.maximum(i * hb - 1, 0), 1)),
                  _full((KW, CW))],
        out_specs=(pl.BlockSpec((tm, 2 * CW), lambda i: (i, 0)), _full((KW * 8, CW))),
        scratch_shapes=[pltpu.VMEM((tm + HALO, CW), F32), pltpu.VMEM((tm + HALO, CW), F32),
                        pltpu.VMEM((8, tm + HALO, CW), F32), pltpu.VMEM((8, tm + HALO, CW), F32)],
        compiler_params=_cp(("arbitrary",)))(dyc, dyc, z, z, z, z, conv_w)


_GELU_C = math.sqrt(2.0 / math.pi)


def _gelu(x):
    return 0.5 * x * (1.0 + jnp.tanh(_GELU_C * (x + 0.044715 * x * x * x)))


def _gelu_grad(x):
    t = jnp.tanh(_GELU_C * (x + 0.044715 * x * x * x))
    return 0.5 * (1.0 + t) + 0.5 * x * (1.0 - t * t) * (_GELU_C * (1.0 + 3 * 0.044715 * x * x))


_NCL = 4
_UC = CW // _NCL
_LW = NST // _NCL
_CS = 2 * _LW


def _ssm_fwd(z, bb, cm, d, tab):
    S = z.shape[0]
    tm = min(512, S)

    def body(u_ref, bb_ref, cm_ref, d_ref, t_ref, x_ref, ys_ref, yg_ref, car_ref):
        i = pl.program_id(0)

        @pl.when(i == 0)
        def _():
            car_ref[...] = jnp.zeros_like(car_ref)

        u16 = u_ref[...]
        u = u16.astype(F32)
        for c in range(_NCL):
            lre = pl.ds(c * _CS, _LW)
            lim = pl.ds(c * _CS + _LW, _LW)
            tl = pl.ds(c * _LW, _LW)
            x_ref[:, c * _CS:(c + 1) * _CS] = jnp.dot(u16[:, c * _UC:(c + 1) * _UC], bb_ref[c],
                                                      preferred_element_type=F32)

            def blk(j, car):
                cr, ci = car
                rows = pl.ds(pl.multiple_of(j * 8, 8), 8)
                r = x_ref[rows, lre]
                im = x_ref[rows, lim]
                for lvl, s in enumerate((1, 2, 4)):
                    mr = t_ref[16 * lvl:16 * lvl + 8, tl]
                    mi = t_ref[16 * lvl + 8:16 * lvl + 16, tl]
                    sr = pltpu.roll(r, s, 0)
                    si = pltpu.roll(im, s, 0)
                    r, im = r + (mr * sr - mi * si), im + (mr * si + mi * sr)
                pr = t_ref[48:56, tl]
                pi_ = t_ref[56:64, tl]
                r, im = r + (pr * cr - pi_ * ci), im + (pr * ci + pi_ * cr)
                x_ref[rows, lre] = r
                x_ref[rows, lim] = im
                return (jnp.broadcast_to(r[7:8, :], (8, _LW)), jnp.broadcast_to(im[7:8, :], (8, _LW)))

            cr, ci = lax.fori_loop(0, tm // 8, blk, (car_ref[:, lre], car_ref[:, lim]))
            car_ref[:, lre] = cr
            car_ref[:, lim] = ci
            cols = slice(c * _UC, (c + 1) * _UC)
            ys = jnp.dot(x_ref[:, c * _CS:(c + 1) * _CS].astype(BF16), cm_ref[c], preferred_element_type=F32)
            ys = ys + d_ref[:, cols] * u[:, cols]
            ys_ref[:, cols] = ys
            yg_ref[:, cols] = _gelu(ys).astype(BF16)

    return pl.pallas_call(
        body, name="ssm_fwd",
        out_shape=(jax.ShapeDtypeStruct((S, 2 * NST), F32), jax.ShapeDtypeStruct((S, CW), F32),
                   jax.ShapeDtypeStruct((S, CW), BF16)),
        grid=(S // tm,),
        in_specs=[pl.BlockSpec((tm, CW), lambda i: (i, 2)), _full((_NCL, _UC, _CS)), _full((_NCL, _CS, _UC)),
                  _full((1, CW)), _full((64, NST))],
        out_specs=(pl.BlockSpec((tm, 2 * NST), lambda i: (i, 0)), pl.BlockSpec((tm, CW), lambda i: (i, 0)),
                   pl.BlockSpec((tm, CW), lambda i: (i, 0))),
        scratch_shapes=[pltpu.VMEM((8, 2 * NST), F32)],
        compiler_params=_cp(("arbitrary",)))(z, bb, cm, d, tab)


def _ssm_bwd(dzz, w_glu, ys, z, xs, cmt, bbt, d, tab, after):
    S = z.shape[0]
    tm = min(512, S)
    nt = S // tm
    tdims = (((0,), (0,)), ((), ()))

    def body(dzz_ref, wglu_ref, ys_ref, u_ref, x_ref, cmt_ref, bbt_ref, d_ref, t_ref, after_ref,
             du_ref, de_ref, dd_ref, dc_hbm, dbb_hbm, car_ref, lam_ref, dc_ref, dbb_ref):
        i = pl.program_id(0)

        @pl.when(i == 0)
        def _():
            car_ref[...] = jnp.zeros_like(car_ref)
            de_ref[...] = jnp.zeros_like(de_ref)
            dd_ref[...] = jnp.zeros_like(dd_ref)
            dc_ref[...] = jnp.zeros_like(dc_ref)
            dbb_ref[...] = jnp.zeros_like(dbb_ref)

        u16 = u_ref[...]
        u = u16.astype(F32)
        dyg = lax.dot_general(dzz_ref[...], wglu_ref[...], (((1,), (1,)), ((), ())), preferred_element_type=F32)
        dys = dyg * _gelu_grad(ys_ref[...])
        dys16 = dys.astype(BF16)
        dd_ref[...] += _colsum8(dys * u)
        row = lax.broadcasted_iota(jnp.int32, (8, _LW), 0)
        for c in range(_NCL):
            lre = pl.ds(c * _CS, _LW)
            lim = pl.ds(c * _CS + _LW, _LW)
            tl = pl.ds(c * _LW, _LW)
            cols = slice(c * _UC, (c + 1) * _UC)
            span = slice(c * _CS, (c + 1) * _CS)
            dc_ref[cols, :] += lax.dot_general(dys16[:, cols], x_ref[:, span].astype(BF16), tdims,
                                               preferred_element_type=F32)
            lam_ref[...] = jnp.dot(dys16[:, cols], cmt_ref[c], preferred_element_type=F32)

            def blk(jj, car):
                cr, ci, ar, ai = car
                j = tm // 8 - 1 - jj
                rows = pl.ds(pl.multiple_of(j * 8, 8), 8)
                r = lam_ref[rows, 0:_LW]
                im = lam_ref[rows, _LW:_CS]
                for lvl, s in enumerate((1, 2, 4)):
                    mr = t_ref[16 * lvl:16 * lvl + 8, tl]
                    mi = t_ref[16 * lvl + 8:16 * lvl + 16, tl]
                    sr = pltpu.roll(r, 8 - s, 0)
                    si = pltpu.roll(im, 8 - s, 0)
                    r, im = r + (mr * sr - mi * si), im + (mr * si + mi * sr)
                pr = t_ref[48:56, tl]
                pi_ = t_ref[56:64, tl]
                r, im = r + (pr * cr - pi_ * ci), im + (pr * ci + pi_ * cr)
                lam_ref[rows, 0:_LW] = r
                lam_ref[rows, _LW:_CS] = im
                nr = jnp.where(row == 7, cr, pltpu.roll(r, 7, 0))
                ni = jnp.where(row == 7, ci, pltpu.roll(im, 7, 0))
                xr = x_ref[rows, lre]
                xi = x_ref[rows, lim]
                ar = ar + (nr * xr + ni * xi)
                ai = ai + (ni * xr - nr * xi)
                return (jnp.broadcast_to(r[0:1, :], (8, _LW)), jnp.broadcast_to(im[0:1, :], (8, _LW)), ar, ai)

            zero = jnp.zeros((8, _LW), F32)
            cr, ci, ar, ai = lax.fori_loop(0, tm // 8, blk, (car_ref[:, lre], car_ref[:, lim], zero, zero))
            car_ref[:, lre] = cr
            car_ref[:, lim] = ci
            de_ref[0:8, tl] += ar
            de_ref[8:16, tl] += ai
            lam16 = lam_ref[...].astype(BF16)
            dbb_ref[cols, :] += lax.dot_general(u16[:, cols], lam16, tdims, preferred_element_type=F32)
            du = jnp.dot(lam16, bbt_ref[c], preferred_element_type=F32) + dys[:, cols] * d_ref[:, cols]
            du_ref[:, cols] = du.astype(BF16)

        @pl.when(i == nt - 1)
        def _():
            pltpu.sync_copy(dc_ref, dc_hbm)
            pltpu.sync_copy(dbb_ref, dbb_hbm)

    rev = lambda i: (nt - 1 - i, 0)
    once = lambda shape: pl.BlockSpec(shape, lambda *_: (0,) * len(shape), pipeline_mode=pl.Buffered(1))
    cross = jax.ShapeDtypeStruct((CW, _CS), F32)
    return pl.pallas_call(
        body, name="ssm_bwd",
        out_shape=(jax.ShapeDtypeStruct((S, CW), BF16), jax.ShapeDtypeStruct((16, NST), F32),
                   jax.ShapeDtypeStruct((8, CW), F32), cross, cross),
        grid=(nt,),
        in_specs=[pl.BlockSpec((tm, 2 * D_MODEL), rev), once((CW, 2 * D_MODEL)), pl.BlockSpec((tm, CW), rev),
                  pl.BlockSpec((tm, CW), lambda i: (nt - 1 - i, 2)), pl.BlockSpec((tm, 2 * NST), rev),
                  once((_NCL, _UC, _CS)), once((_NCL, _CS, _UC)), _full((1, CW)), once((64, NST)),
                  pl.BlockSpec(memory_space=pl.ANY)],
        out_specs=(pl.BlockSpec((tm, CW), rev), _full((16, NST)), _full((8, CW)),
                   pl.BlockSpec(memory_space=pl.ANY), pl.BlockSpec(memory_space=pl.ANY)),
        scratch_shapes=[pltpu.VMEM((8, 2 * NST), F32), pltpu.VMEM((tm, _CS), F32),
                        pltpu.VMEM((CW, _CS), F32), pltpu.VMEM((CW, _CS), F32)],
        compiler_params=_cp(("arbitrary",)))(dzz, w_glu, ys, z, xs, cmt, bbt, d, tab, after)


def _ssm_prep(a_re, a_im, b_re, b_im, log_dt):
    dt = jnp.exp(log_dt.reshape(G))[:, None]
    mag = jnp.exp(dt * a_re)
    e_re, e_im = mag * jnp.cos(dt * a_im), mag * jnp.sin(dt * a_im)
    n_re, n_im = e_re - 1.0, e_im
    den = a_re * a_re + a_im * a_im
    q_re = (n_re * a_re + n_im * a_im) / den
    q_im = (n_im * a_re - n_re * a_im) / den
    bb_re = q_re[..., None] * b_re - q_im[..., None] * b_im
    bb_im = q_re[..., None] * b_im + q_im[..., None] * b_re
    return e_re, e_im, bb_re, bb_im


def _scan_tables(e_re, e_im, reverse):
    er = e_re.reshape(1, NST)
    ei = e_im.reshape(1, NST)
    if reverse:
        ei = -ei
    pows = [(er, ei)]
    for _ in range(7):
        pr, pi_ = pows[-1]
        pows.append((pr * er - pi_ * ei, pr * ei + pi_ * er))
    row = jnp.arange(8)[:, None]
    out = []
    for s in (1, 2, 4):
        pr, pi_ = pows[s - 1]
        keep = (row + s <= 7) if reverse else (row >= s)
        out += [jnp.where(keep, pr, 0.0), jnp.where(keep, pi_, 0.0)]
    allr = jnp.concatenate([p[0] for p in pows], 0)
    alli = jnp.concatenate([p[1] for p in pows], 0)
    if reverse:
        allr, alli = allr[::-1], alli[::-1]
    out += [allr, alli]
    return jnp.concatenate(out, 0).astype(F32)


def _block_diag_mats(bb_re, bb_im, c_re, c_im):
    gc = G // _NCL
    eye = jnp.eye(gc, dtype=F32)
    bre = jnp.einsum("cjph,jk->cjhkp", bb_re.reshape(_NCL, gc, P, H), eye).reshape(_NCL, _UC, _LW)
    bim = jnp.einsum("cjph,jk->cjhkp", bb_im.reshape(_NCL, gc, P, H), eye).reshape(_NCL, _UC, _LW)
    bb = jnp.concatenate([bre, bim], 2)
    cre = jnp.einsum("cjhp,jk->cjpkh", c_re.reshape(_NCL, gc, H, P), eye).reshape(_NCL, _LW, _UC)
    cim = jnp.einsum("cjhp,jk->cjpkh", c_im.reshape(_NCL, gc, H, P), eye).reshape(_NCL, _LW, _UC)
    cm = jnp.concatenate([cre, -cim], 1)
    return bb, cm


def _diag_blocks(cross):
    gc = G // _NCL
    six = cross.reshape(_NCL, gc, H, 2, gc, P)
    same = jnp.eye(gc, dtype=bool)[None, :, None, None, :, None]
    diag = jnp.sum(jnp.where(same, six, 0.0), axis=4)
    diag = jnp.moveaxis(diag, 3, 0).reshape(2, G, H, P)
    return diag[0], diag[1]


def _mix_fwd(scv, yg, z, x, w_cp, w_glu, w_out, g1, n2g, sc2, sh2):
    S = z.shape[0]
    tm = min(512, S)
    D = D_MODEL

    def body(s_ref, yg_ref, glc0_ref, glc1_ref, gls0_ref, gls1_ref, x_ref, wcp_ref, wglu_ref, wout_ref,
             g1_ref, n2_ref, sc_ref, sh_ref, yc_ref, zz_ref, m_ref, o_ref, x2_ref, h2_ref):
        y_conv = jnp.dot(s_ref[...], wcp_ref[...], preferred_element_type=F32)
        zz = jnp.dot(yg_ref[...], wglu_ref[...], preferred_element_type=F32)
        yc_ref[...] = y_conv.astype(BF16)
        zz_ref[...] = zz.astype(BF16)
        for half, (glc_ref, gls_ref) in enumerate(((glc0_ref, gls0_ref), (glc1_ref, gls1_ref))):
            lo, hi = half * CW, (half + 1) * CW
            y_ssm = zz[:, lo:hi] * _sig(zz[:, D + lo:D + hi])
            m_ref[:, lo:hi] = (_sig(glc_ref[...].astype(F32)) * y_conv[:, lo:hi]
                               + _sig(gls_ref[...].astype(F32)) * y_ssm).astype(BF16)
        o = jnp.dot(m_ref[...], wout_ref[...], preferred_element_type=F32)
        o_ref[...] = o.astype(BF16)
        xv = x_ref[...] + g1_ref[...] * o
        x2_ref[...] = xv
        r = lax.rsqrt(jnp.mean(xv * xv, axis=-1, keepdims=True) + EPS)
        h2_ref[...] = (xv * r * (n2_ref[...] * (1.0 + sc_ref[...])) + sh_ref[...]).astype(BF16)

    zb_ = lambda j: pl.BlockSpec((tm, CW), lambda i: (i, j))
    row = lambda w: pl.BlockSpec((tm, w), lambda i: (i, 0))
    par = _full((1, D))
    bf = lambda w: jax.ShapeDtypeStruct((S, w), BF16)
    return pl.pallas_call(
        body, name="mix_fwd",
        out_shape=(bf(D), bf(2 * D), bf(D), bf(D), jax.ShapeDtypeStruct((S, D), F32), bf(D)),
        grid=(S // tm,),
        in_specs=[row(CW), row(CW), zb_(3), zb_(4), zb_(5), zb_(6), row(D), _resident((CW, D)),
                  _resident((CW, 2 * D)), _resident((D, D)), par, par, par, par],
        out_specs=(row(D), row(2 * D), row(D), row(D), row(D), row(D)),
        compiler_params=_cp(("parallel",)))(scv, yg, z, z, z, z, x, w_cp, w_glu, w_out, g1, n2g, sc2, sh2)


def _mix_bwd(do, w_out, z, zz, y_conv, after):
    S = z.shape[0]
    tm = min(512, S)
    D = D_MODEL

    def body(do_ref, w_ref, glc0_ref, glc1_ref, gls0_ref, gls1_ref, za_ref, zb_ref, yc_ref, after_ref,
             dyc_ref, dgl_ref, dzz_ref):
        dm = lax.dot_general(do_ref[...], w_ref[...], (((1,), (1,)), ((), ())), preferred_element_type=F32)
        for half, (glc_ref, gls_ref) in enumerate(((glc0_ref, gls0_ref), (glc1_ref, gls1_ref))):
            lo, hi = half * CW, (half + 1) * CW
            dm_v = dm[:, lo:hi]
            sgc = _sig(glc_ref[...].astype(F32))
            sgs = _sig(gls_ref[...].astype(F32))
            szb = _sig(zb_ref[:, lo:hi].astype(F32))
            za = za_ref[:, lo:hi].astype(F32)
            dyc_ref[:, lo:hi] = (dm_v * sgc).astype(BF16)
            dgl_ref[:, lo:hi] = (dm_v * yc_ref[:, lo:hi].astype(F32) * sgc * (1.0 - sgc)).astype(BF16)
            dys = dm_v * sgs
            dgl_ref[:, D + lo:D + hi] = (dys * (za * szb) * (1.0 - sgs)).astype(BF16)
            dzz_ref[:, lo:hi] = (dys * szb).astype(BF16)
            dzz_ref[:, D + lo:D + hi] = (dys * za * szb * (1.0 - szb)).astype(BF16)

    zb_ = lambda j: pl.BlockSpec((tm, CW), lambda i: (i, j))
    wide = lambda j: pl.BlockSpec((tm, D), lambda i: (i, j))
    return pl.pallas_call(
        body, name="mix_bwd",
        out_shape=(jax.ShapeDtypeStruct((S, D), BF16), jax.ShapeDtypeStruct((S, 2 * D), BF16),
                   jax.ShapeDtypeStruct((S, 2 * D), BF16)),
        grid=(S // tm,),
        in_specs=[wide(0), _resident((D, D)), zb_(3), zb_(4), zb_(5), zb_(6), wide(0), wide(1), wide(0),
                  pl.BlockSpec(memory_space=pl.ANY)],
        out_specs=(wide(0), pl.BlockSpec((tm, 2 * D), lambda i: (i, 0)), pl.BlockSpec((tm, 2 * D), lambda i: (i, 0))),
        compiler_params=_cp(("parallel",)))(do, w_out, z, z, z, z, zz, zz, y_conv, after)


_FC = 1408


def _ffn_in_act(h2, w_fi):
    S, D = h2.shape
    tm = min(512, S)

    def body(h_ref, w_ref, f_ref, a_ref):
        hv = h_ref[...]
        for c in range(FH // _FC):
            lo, hi = c * _FC, (c + 1) * _FC
            g = jnp.dot(hv, w_ref[:, lo:hi], preferred_element_type=F32)
            u = jnp.dot(hv, w_ref[:, FH + lo:FH + hi], preferred_element_type=F32)
            f_ref[:, lo:hi] = g.astype(BF16)
            f_ref[:, FH + lo:FH + hi] = u.astype(BF16)
            a_ref[:, lo:hi] = (g * _sig(g) * u).astype(BF16)

    return pl.pallas_call(
        body, name="ffn_in_act",
        out_shape=(jax.ShapeDtypeStruct((S, 2 * FH), BF16), jax.ShapeDtypeStruct((S, FH), BF16)),
        grid=(S // tm,),
        in_specs=[pl.BlockSpec((tm, D), lambda i: (i, 0)), _resident((D, 2 * FH))],
        out_specs=(pl.BlockSpec((tm, 2 * FH), lambda i: (i, 0)), pl.BlockSpec((tm, FH), lambda i: (i, 0))),
        compiler_params=_cp(("parallel",)))(h2, w_fi)


def _ffn_bwd(do2, w_fo, f, after):
    S, D = do2.shape
    tm = min(512, S)

    def body(d_ref, w_ref, f_ref, after_ref, df_ref):
        dv = d_ref[...]
        for c in range(FH // _FC):
            lo, hi = c * _FC, (c + 1) * _FC
            dact = lax.dot_general(dv, w_ref[lo:hi, :], (((1,), (1,)), ((), ())), preferred_element_type=F32)
            g = f_ref[:, lo:hi].astype(F32)
            u = f_ref[:, FH + lo:FH + hi].astype(F32)
            sg = _sig(g)
            df_ref[:, lo:hi] = (dact * u * (sg * (1.0 + g * (1.0 - sg)))).astype(BF16)
            df_ref[:, FH + lo:FH + hi] = (dact * g * sg).astype(BF16)

    return pl.pallas_call(
        body, name="ffn_bwd", out_shape=jax.ShapeDtypeStruct((S, 2 * FH), BF16), grid=(S // tm,),
        in_specs=[pl.BlockSpec((tm, D), lambda i: (i, 0)), _resident((FH, D)),
                  pl.BlockSpec((tm, 2 * FH), lambda i: (i, 0)), pl.BlockSpec(memory_space=pl.ANY)],
        out_specs=pl.BlockSpec((tm, 2 * FH), lambda i: (i, 0)),
        compiler_params=_cp(("parallel",)))(do2, w_fo, f, after)


def _ffn_out_final(x2, act, w_fo, g2, fg, tgt):
    S, D = x2.shape
    tm = min(512, S)

    def body(x2_ref, a_ref, w_ref, g2_ref, fg_ref, t_ref, dx3_ref, do2_ref, ls_ref, dfg_ref, dg2_ref):
        i = pl.program_id(0)

        @pl.when(i == 0)
        def _():
            ls_ref[...] = jnp.zeros_like(ls_ref)
            dfg_ref[...] = jnp.zeros_like(dfg_ref)
            dg2_ref[...] = jnp.zeros_like(dg2_ref)

        o2 = jnp.dot(a_ref[...], w_ref[...], preferred_element_type=F32)
        x3 = x2_ref[...] + g2_ref[...] * o2
        r = lax.rsqrt(jnp.mean(x3 * x3, axis=-1, keepdims=True) + EPS)
        xn = x3 * r
        err = xn * fg_ref[...] - t_ref[...]
        dy = err * (1.0 / D)
        dxn = dy * fg_ref[...]
        dx3 = r * (dxn - xn * jnp.mean(dxn * xn, axis=-1, keepdims=True))
        dx3_ref[...] = dx3
        do2_ref[...] = (dx3 * g2_ref[...]).astype(BF16)
        e2 = _colsum8(err * err)
        lanes = e2[:, 0:128]
        for q in range(1, D // 128):
            lanes = lanes + e2[:, q * 128:(q + 1) * 128]
        ls_ref[...] += lanes * (0.5 / D)
        dfg_ref[...] += _colsum8(dy * xn)
        dg2_ref[...] += _colsum8(dx3 * o2)

    row = pl.BlockSpec((tm, D), lambda i: (i, 0))
    par = _full((1, D))
    return pl.pallas_call(
        body, name="final_loss",
        out_shape=(jax.ShapeDtypeStruct((S, D), F32), jax.ShapeDtypeStruct((S, D), BF16),
                   jax.ShapeDtypeStruct((8, 128), F32), jax.ShapeDtypeStruct((8, D), F32),
                   jax.ShapeDtypeStruct((8, D), F32)),
        grid=(S // tm,), in_specs=[row, pl.BlockSpec((tm, FH), lambda i: (i, 0)), _resident((FH, D)), par, par, row],
        out_specs=(row, row, _full((8, 128)), _full((8, D)), _full((8, D))),
        compiler_params=_cp(("arbitrary",)))(x2, act, w_fo, g2, fg, tgt)


def _normmod_bwd(dsrc, w, xin, dres, g, sc, gate, o, after, name):
    S, D = xin.shape
    parts = list(dsrc) if isinstance(dsrc, (list, tuple)) else [dsrc]
    widths = [p.shape[1] for p in parts]
    K = sum(widths)
    tm = min(512, S)
    npart = len(parts)

    def body(*refs):
        ds_refs = refs[:npart]
        w_ref, x_ref, dr_ref, g_ref, sc_ref, gate_ref, o_ref, after_ref = refs[npart:npart + 8]
        dx_ref, do_ref, dsh_ref, dsc_ref, dg_ref, dgate_ref = refs[npart + 8:]
        i = pl.program_id(0)

        @pl.when(i == 0)
        def _():
            dsh_ref[...] = jnp.zeros_like(dsh_ref)
            dsc_ref[...] = jnp.zeros_like(dsc_ref)
            dg_ref[...] = jnp.zeros_like(dg_ref)
            dgate_ref[...] = jnp.zeros_like(dgate_ref)

        gv = g_ref[...]
        scale = 1.0 + sc_ref[...]
        xv = x_ref[...]
        r = lax.rsqrt(jnp.mean(xv * xv, axis=-1, keepdims=True) + EPS)
        xn = xv * r
        dh_v, col = None, 0
        for ds_ref, wd in zip(ds_refs, widths):
            t = lax.dot_general(ds_ref[...], w_ref[:, col:col + wd], (((1,), (1,)), ((), ())),
                                preferred_element_type=F32)
            dh_v = t if dh_v is None else dh_v + t
            col += wd
        dxn = dh_v * (gv * scale)
        dx = dr_ref[...] + r * (dxn - xn * jnp.mean(dxn * xn, axis=-1, keepdims=True))
        dx_ref[...] = dx
        do_ref[...] = (dx * gate_ref[...]).astype(BF16)
        hx = dh_v * xn
        dsh_ref[...] += _colsum8(dh_v)
        dsc_ref[...] += _colsum8(hx) * gv
        dg_ref[...] += _colsum8(hx) * scale
        dgate_ref[...] += _colsum8(dx * o_ref[...])

    row = pl.BlockSpec((tm, D), lambda i: (i, 0))
    par = _full((1, D))
    acc = jax.ShapeDtypeStruct((8, D), F32)
    return pl.pallas_call(
        body, name=name,
        out_shape=(jax.ShapeDtypeStruct((S, D), F32), jax.ShapeDtypeStruct((S, D), BF16), acc, acc, acc, acc),
        grid=(S // tm,),
        in_specs=[pl.BlockSpec((tm, wd), lambda i: (i, 0)) for wd in widths]
        + [_resident((D, K)), row, row, par, par, par, row, pl.BlockSpec(memory_space=pl.ANY)],
        out_specs=(row, row, _full((8, D)), _full((8, D)), _full((8, D)), _full((8, D))),
        compiler_params=_cp(("arbitrary",)))(*parts, w, xin, dres, g, sc, gate, o, after)


def _me():
    return lax.axis_index("x"), lax.axis_index("y"), lax.axis_index("c")


def _allgather8(v, name, after=()):
    R, C = v.shape
    after = list(after)

    def body(v_ref, *rest):
        out_ref, send_sems, recv_sems, local_sem = rest[len(after):]
        x, y, c = _me()
        mine = pltpu.make_async_copy(v_ref, out_ref.at[4 * x + 2 * y + c], local_sem)
        mine.start()
        copies = []
        for k in range(1, N_DEV):
            fx, fy, fc = (k >> 2) & 1, (k >> 1) & 1, k & 1
            peer = (x ^ fx, y ^ fy, c ^ fc)
            copies.append(pltpu.make_async_remote_copy(
                src_ref=v_ref, dst_ref=out_ref.at[4 * x + 2 * y + c],
                send_sem=send_sems.at[k - 1], recv_sem=recv_sems.at[k - 1],
                device_id=peer, device_id_type=MESH))
        for cp in copies:
            cp.start()
        for k in range(1, N_DEV):
            fx, fy, fc = (k >> 2) & 1, (k >> 1) & 1, k & 1
            src_slot = 4 * (x ^ fx) + 2 * (y ^ fy) + (c ^ fc)
            pltpu.make_async_remote_copy(
                src_ref=v_ref, dst_ref=out_ref.at[src_slot],
                send_sem=send_sems.at[k - 1], recv_sem=recv_sems.at[k - 1],
                device_id=(x ^ fx, y ^ fy, c ^ fc), device_id_type=MESH).wait_recv()
        for cp in copies:
            cp.wait_send()
        mine.wait()

    return pl.pallas_call(
        body, name=name, out_shape=jax.ShapeDtypeStruct((N_DEV, R, C), v.dtype),
        in_specs=[pl.BlockSpec(memory_space=pltpu.VMEM)] + [pl.BlockSpec(memory_space=pl.ANY)] * len(after),
        out_specs=pl.BlockSpec(memory_space=pltpu.VMEM),
        scratch_shapes=[pltpu.SemaphoreType.DMA((N_DEV - 1,)), pltpu.SemaphoreType.DMA((N_DEV - 1,)),
                        pltpu.SemaphoreType.DMA],
        compiler_params=pltpu.CompilerParams(vmem_limit_bytes=VMEM_LIMIT))(v, *after)


_HBM = pl.BlockSpec(memory_space=pltpu.HBM)
_SEM = pl.BlockSpec(memory_space=pltpu.SEMAPHORE)
_EFFECT = pltpu.SideEffectType.DATAFLOW_SIDE_EFFECTING
_N_PEER = N_CHIP - 1


def _chip_part(ref, axis, n, chip):
    start = pl.multiple_of(chip * n, 8)
    return ref.at[pl.ds(start, n), :] if axis == 0 else ref.at[:, pl.ds(start, n)]


def _gather_copy(k, src_ref, land_ref, send_sems, recv_sems, axis, arriving):
    x, y, c = _me()
    px, py = x ^ ((k >> 1) & 1), y ^ (k & 1)
    chip = 2 * px + py if arriving else 2 * x + y
    return pltpu.make_async_remote_copy(
        src_ref=src_ref, dst_ref=_chip_part(land_ref, axis, src_ref.shape[axis], chip),
        send_sem=send_sems.at[k - 1], recv_sem=recv_sems.at[k - 1], device_id=(px, py, c), device_id_type=MESH)


def _scatter_copy(k, grad_ref, land_ref, send_sems, recv_sems, axis):
    x, y, c = _me()
    px, py = x ^ ((k >> 1) & 1), y ^ (k & 1)
    return pltpu.make_async_remote_copy(
        src_ref=_chip_part(grad_ref, axis, grad_ref.shape[axis] // N_CHIP, 2 * px + py), dst_ref=land_ref.at[k],
        send_sem=send_sems.at[k - 1], recv_sem=recv_sems.at[k - 1], device_id=(px, py, c), device_id_type=MESH)


def _scatter_own(grad_ref, land_ref, send_sems, axis):
    x, y, _ = _me()
    return pltpu.make_async_copy(_chip_part(grad_ref, axis, grad_ref.shape[axis] // N_CHIP, 2 * x + y),
                                 land_ref.at[0], send_sems.at[_N_PEER])


def _own_copy(src_ref, land_ref, sends, axis):
    x, y, _ = _me()
    return pltpu.make_async_copy(src_ref, _chip_part(land_ref, axis, src_ref.shape[axis], 2 * x + y),
                                 sends.at[_N_PEER])


def _gather_start(shards, axes, after, name):
    nw = len(shards)
    lands = []
    for s, ax in zip(shards, axes):
        shp = list(s.shape)
        shp[ax] *= N_CHIP
        lands.append(lax.empty(tuple(shp), s.dtype))

    def body(*refs):
        srcs, zones = refs[:nw], refs[nw:2 * nw]
        sends, recvs = refs[2 * nw + 1:3 * nw + 1], refs[3 * nw + 1:4 * nw + 1]
        token = refs[-1]
        for w in range(nw):
            for k in range(1, N_CHIP):
                _gather_copy(k, srcs[w], zones[w], sends[w], recvs[w], axes[w], False).start()
        for w in range(nw):
            _own_copy(srcs[w], zones[w], sends[w], axes[w]).start()
        token[...] = jnp.zeros_like(token)

    outs = pl.pallas_call(
        body, name=name,
        out_shape=tuple([pltpu.SemaphoreType.DMA((_N_PEER + 1,))] * nw + [pltpu.SemaphoreType.DMA((_N_PEER,))] * nw
                        + [pltpu.HBM(a.shape, a.dtype) for a in list(shards) + list(lands)]
                        + [jax.ShapeDtypeStruct((8, 128), F32)]),
        in_specs=[_HBM] * (2 * nw) + [pl.BlockSpec(memory_space=pl.ANY)],
        out_specs=tuple([_SEM] * (2 * nw) + [_HBM] * (2 * nw) + [pl.BlockSpec(memory_space=pltpu.VMEM)]),
        input_output_aliases={i: 2 * nw + i for i in range(2 * nw)},
        compiler_params=pltpu.CompilerParams(has_side_effects=_EFFECT),
    )(*([pltpu.with_memory_space_constraint(a, pltpu.HBM) for a in list(shards) + list(lands)] + [after]))
    per_weight = [(outs[w], outs[nw + w], outs[2 * nw + w], outs[3 * nw + w]) for w in range(nw)]
    return per_weight, outs[-1]


def _gather_wait(state, axis, after, name):
    send_sems, recv_sems, shard, land = state

    after = list(after) if isinstance(after, (list, tuple)) else [after]

    def body(src_ref, land_ref, sends, recvs, *rest):
        for k in range(1, N_CHIP):
            _gather_copy(k, src_ref, land_ref, sends, recvs, axis, False).wait_send()
            _gather_copy(k, src_ref, land_ref, sends, recvs, axis, True).wait_recv()
        _own_copy(src_ref, land_ref, sends, axis).wait()

    return pl.pallas_call(
        body, name=name, out_shape=(pltpu.HBM(shard.shape, shard.dtype), pltpu.HBM(land.shape, land.dtype)),
        in_specs=[_HBM, _HBM, _SEM, _SEM] + [pl.BlockSpec(memory_space=pl.ANY)] * len(after), out_specs=(_HBM, _HBM),
        input_output_aliases={0: 0, 1: 1},
        compiler_params=pltpu.CompilerParams(has_side_effects=_EFFECT),
    )(shard, land, send_sems, recv_sems, *after)[1]


def _half_rows(ref, c):
    k2 = ref.shape[0] // 2
    return pl.ds(pl.multiple_of(c * k2, 8), k2)


def _half_copy(k, shard_ref, land_ref, send_sems, recv_sems, arriving):
    x, y, c = _me()
    px, py = x ^ ((k >> 1) & 1), y ^ (k & 1)
    n = shard_ref.shape[1]
    chip = 2 * px + py if arriving else 2 * x + y
    return pltpu.make_async_remote_copy(
        src_ref=shard_ref.at[_half_rows(shard_ref, c), :],
        dst_ref=land_ref.at[_half_rows(land_ref, c), pl.ds(pl.multiple_of(chip * n, 128), n)],
        send_sem=send_sems.at[k - 1], recv_sem=recv_sems.at[k - 1], device_id=(px, py, c), device_id_type=MESH)


def _half_own(shard_ref, land_ref, send_sems):
    x, y, c = _me()
    n = shard_ref.shape[1]
    return pltpu.make_async_copy(
        shard_ref.at[_half_rows(shard_ref, c), :],
        land_ref.at[_half_rows(land_ref, c), pl.ds(pl.multiple_of((2 * x + y) * n, 128), n)], send_sems.at[_N_PEER])


def _half_gather_start(shard, after, name):
    K, n = shard.shape
    land = lax.empty((K, N_CHIP * n), shard.dtype)

    def body(shard_ref, land_ref, after_ref, sends, recvs, shard_thru, land_thru, token):
        for k in range(1, N_CHIP):
            _half_copy(k, shard_ref, land_ref, sends, recvs, False).start()
        _half_own(shard_ref, land_ref, sends).start()
        token[...] = jnp.zeros_like(token)

    outs = pl.pallas_call(
        body, name=name,
        out_shape=(pltpu.SemaphoreType.DMA((_N_PEER + 1,)), pltpu.SemaphoreType.DMA((_N_PEER,)),
                   pltpu.HBM(shard.shape, shard.dtype), pltpu.HBM(land.shape, land.dtype),
                   jax.ShapeDtypeStruct((8, 128), F32)),
        in_specs=[_HBM, _HBM, pl.BlockSpec(memory_space=pl.ANY)],
        out_specs=(_SEM, _SEM, _HBM, _HBM, pl.BlockSpec(memory_space=pltpu.VMEM)),
        input_output_aliases={0: 2, 1: 3},
        compiler_params=pltpu.CompilerParams(has_side_effects=_EFFECT),
    )(pltpu.with_memory_space_constraint(shard, pltpu.HBM), pltpu.with_memory_space_constraint(land, pltpu.HBM), after)
    return outs[:4], outs[4]


def _half_gather_wait(state, after, name):
    send_sems, recv_sems, shard, land = state
    after = list(after)

    def body(shard_ref, land_ref, sends, recvs, *rest):
        for k in range(1, N_CHIP):
            _half_copy(k, shard_ref, land_ref, sends, recvs, False).wait_send()
            _half_copy(k, shard_ref, land_ref, sends, recvs, True).wait_recv()
        _half_own(shard_ref, land_ref, sends).wait()

    return pl.pallas_call(
        body, name=name, out_shape=(pltpu.HBM(shard.shape, shard.dtype), pltpu.HBM(land.shape, land.dtype)),
        in_specs=[_HBM, _HBM, _SEM, _SEM] + [pl.BlockSpec(memory_space=pl.ANY)] * len(after), out_specs=(_HBM, _HBM),
        input_output_aliases={0: 0, 1: 1},
        compiler_params=pltpu.CompilerParams(has_side_effects=_EFFECT),
    )(shard, land, send_sems, recv_sems, *after)[1]


def _half_swap_copy(land_ref, send_sem, recv_sem, arriving):
    x, y, c = _me()
    rows = _half_rows(land_ref, 1 - c if arriving else c)
    return pltpu.make_async_remote_copy(src_ref=land_ref.at[rows, :], dst_ref=land_ref.at[rows, :], send_sem=send_sem,
                                        recv_sem=recv_sem, device_id=(x, y, 1 - c), device_id_type=MESH)


def _half_swap_start(land, name):
    def body(land_ref, send, recv, land_thru, token):
        _half_swap_copy(land_ref, send.at[0], recv.at[0], False).start()
        token[...] = jnp.zeros_like(token)

    sem = pltpu.SemaphoreType.DMA((1,))
    outs = pl.pallas_call(
        body, name=name,
        out_shape=(sem, sem, pltpu.HBM(land.shape, land.dtype), jax.ShapeDtypeStruct((8, 128), F32)),
        in_specs=[_HBM], out_specs=(_SEM, _SEM, _HBM, pl.BlockSpec(memory_space=pltpu.VMEM)),
        input_output_aliases={0: 2},
        compiler_params=pltpu.CompilerParams(has_side_effects=_EFFECT),
    )(pltpu.with_memory_space_constraint(land, pltpu.HBM))
    return outs[:3], outs[3]


def _half_swap_wait(state, after, name):
    send, recv, land = state

    def body(land_ref, send_ref, recv_ref, after_ref, got_ref):
        _half_swap_copy(land_ref, send_ref.at[0], recv_ref.at[0], False).wait_send()
        _half_swap_copy(land_ref, send_ref.at[0], recv_ref.at[0], True).wait_recv()

    return pl.pallas_call(
        body, name=name, out_shape=pltpu.HBM(land.shape, land.dtype),
        in_specs=[_HBM, _SEM, _SEM, pl.BlockSpec(memory_space=pl.ANY)], out_specs=_HBM,
        input_output_aliases={0: 0},
        compiler_params=pltpu.CompilerParams(has_side_effects=_EFFECT),
    )(land, send, recv, after)


def _all8_copy(k, v_ref, land_ref, send_sems, recv_sems, arriving):
    x, y, c = _me()
    px, py, pc = x ^ ((k >> 2) & 1), y ^ ((k >> 1) & 1), c ^ (k & 1)
    slot = 4 * px + 2 * py + pc if arriving else 4 * x + 2 * y + c
    return pltpu.make_async_remote_copy(
        src_ref=v_ref, dst_ref=land_ref.at[slot], send_sem=send_sems.at[k - 1], recv_sem=recv_sems.at[k - 1],
        device_id=(px, py, pc), device_id_type=MESH)


def _all8_own(v_ref, land_ref, send_sems):
    x, y, c = _me()
    return pltpu.make_async_copy(v_ref, land_ref.at[4 * x + 2 * y + c], send_sems.at[N_DEV - 1])


def _all8_start(v, name):
    land = lax.empty((N_DEV,) + v.shape, v.dtype)

    def body(v_ref, land_ref, sends, recvs, v_thru, land_thru, token):
        for k in range(1, N_DEV):
            _all8_copy(k, v_ref, land_ref, sends, recvs, False).start()
        _all8_own(v_ref, land_ref, sends).start()
        token[...] = jnp.zeros_like(token)

    outs = pl.pallas_call(
        body, name=name,
        out_shape=(pltpu.SemaphoreType.DMA((N_DEV,)), pltpu.SemaphoreType.DMA((N_DEV - 1,)),
                   pltpu.HBM(v.shape, v.dtype), pltpu.HBM(land.shape, land.dtype),
                   jax.ShapeDtypeStruct((8, 128), F32)),
        in_specs=[_HBM, _HBM], out_specs=(_SEM, _SEM, _HBM, _HBM, pl.BlockSpec(memory_space=pltpu.VMEM)),
        input_output_aliases={0: 2, 1: 3},
        compiler_params=pltpu.CompilerParams(has_side_effects=_EFFECT),
    )(pltpu.with_memory_space_constraint(v, pltpu.HBM), pltpu.with_memory_space_constraint(land, pltpu.HBM))
    return outs[:4], outs[4]


def _all8_wait(state, after, name):
    send_sems, recv_sems, v, land = state

    def body(v_ref, land_ref, sends, recvs, after_ref, v_dead, got_ref):
        for k in range(1, N_DEV):
            _all8_copy(k, v_ref, land_ref, sends, recvs, False).wait_send()
            _all8_copy(k, v_ref, land_ref, sends, recvs, True).wait_recv()
        _all8_own(v_ref, land_ref, sends).wait()

    return pl.pallas_call(
        body, name=name, out_shape=(pltpu.HBM(v.shape, v.dtype), pltpu.HBM(land.shape, land.dtype)),
        in_specs=[_HBM, _HBM, _SEM, _SEM, pl.BlockSpec(memory_space=pl.ANY)], out_specs=(_HBM, _HBM),
        input_output_aliases={0: 0, 1: 1},
        compiler_params=pltpu.CompilerParams(has_side_effects=_EFFECT),
    )(v, land, send_sems, recv_sems, after)[1]


def _swap_copy(w, src_ref, land_ref, send_sems, recv_sems):
    x, y, c = _me()
    return pltpu.make_async_remote_copy(src_ref=src_ref, dst_ref=land_ref, send_sem=send_sems.at[w],
                                        recv_sem=recv_sems.at[w], device_id=(x, y, 1 - c), device_id_type=MESH)


def _swap_start(arrs, after, name):
    nw = len(arrs)
    lands = [lax.empty(a.shape, a.dtype) for a in arrs]

    def body(*refs):
        srcs, zones = refs[:nw], refs[nw:2 * nw]
        sends, recvs = refs[2 * nw + 1], refs[2 * nw + 2]
        for w in range(nw):
            _swap_copy(w, srcs[w], zones[w], sends, recvs).start()
        refs[-1][...] = jnp.zeros_like(refs[-1])

    sem = pltpu.SemaphoreType.DMA((nw,))
    outs = pl.pallas_call(
        body, name=name,
        out_shape=tuple([sem, sem] + [pltpu.HBM(a.shape, a.dtype) for a in list(arrs) + lands]
                        + [jax.ShapeDtypeStruct((8, 128), F32)]),
        in_specs=[_HBM] * (2 * nw) + [pl.BlockSpec(memory_space=pl.ANY)],
        out_specs=tuple([_SEM, _SEM] + [_HBM] * (2 * nw) + [pl.BlockSpec(memory_space=pltpu.VMEM)]),
        input_output_aliases={i: 2 + i for i in range(2 * nw)},
        compiler_params=pltpu.CompilerParams(has_side_effects=_EFFECT),
    )(*([pltpu.with_memory_space_constraint(a, pltpu.HBM) for a in list(arrs) + lands] + [after]))
    return (outs[0], outs[1], outs[2:2 + nw], outs[2 + nw:2 + 2 * nw]), outs[-1]


def _swap_wait(state, after, name):
    send_sems, recv_sems, arrs, lands = state
    nw = len(arrs)

    def body(*refs):
        srcs, zones = refs[:nw], refs[nw:2 * nw]
        sends, recvs = refs[2 * nw], refs[2 * nw + 1]
        for w in range(nw):
            cp = _swap_copy(w, srcs[w], zones[w], sends, recvs)
            cp.wait_send()
            cp.wait_recv()

    outs = pl.pallas_call(
        body, name=name, out_shape=tuple(pltpu.HBM(a.shape, a.dtype) for a in list(arrs) + list(lands)),
        in_specs=[_HBM] * (2 * nw) + [_SEM, _SEM, pl.BlockSpec(memory_space=pl.ANY)],
        out_specs=tuple([_HBM] * (2 * nw)),
        input_output_aliases={i: i for i in range(2 * nw)},
        compiler_params=pltpu.CompilerParams(has_side_effects=_EFFECT),
    )(*arrs, *lands, send_sems, recv_sems, after)
    return list(outs[:nw]), list(outs[nw:])


def _scatter_start(grad, axis, name):
    shp = list(grad.shape)
    shp[axis] //= N_CHIP
    land = lax.empty((N_CHIP,) + tuple(shp), grad.dtype)

    def body(grad_ref, land_ref, sends, recvs, grad_thru, land_thru, token):
        for k in range(1, N_CHIP):
            _scatter_copy(k, grad_ref, land_ref, sends, recvs, axis).start()
        _scatter_own(grad_ref, land_ref, sends, axis).start()
        token[...] = jnp.zeros_like(token)

    outs = pl.pallas_call(
        body, name=name,
        out_shape=(pltpu.SemaphoreType.DMA((_N_PEER + 1,)), pltpu.SemaphoreType.DMA((_N_PEER,)),
                   pltpu.HBM(grad.shape, grad.dtype), pltpu.HBM(land.shape, land.dtype),
                   jax.ShapeDtypeStruct((8, 128), F32)),
        in_specs=[_HBM, _HBM], out_specs=(_SEM, _SEM, _HBM, _HBM, pl.BlockSpec(memory_space=pltpu.VMEM)),
        input_output_aliases={0: 2, 1: 3},
        compiler_params=pltpu.CompilerParams(has_side_effects=_EFFECT),
    )(pltpu.with_memory_space_constraint(grad, pltpu.HBM), pltpu.with_memory_space_constraint(land, pltpu.HBM))
    return outs[:4], outs[4]


def _scatter_wait(state, axis, after, name):
    send_sems, recv_sems, grad, land = state

    def body(grad_ref, land_ref, sends, recvs, after_ref, grad_dead, got_ref):
        for k in range(1, N_CHIP):
            cp = _scatter_copy(k, grad_ref, land_ref, sends, recvs, axis)
            cp.wait_send()
            cp.wait_recv()
        _scatter_own(grad_ref, land_ref, sends, axis).wait()

    return pl.pallas_call(
        body, name=name, out_shape=(pltpu.HBM(grad.shape, grad.dtype), pltpu.HBM(land.shape, land.dtype)),
        in_specs=[_HBM, _HBM, _SEM, _SEM, pl.BlockSpec(memory_space=pl.ANY)], out_specs=(_HBM, _HBM),
        input_output_aliases={0: 0, 1: 1},
        compiler_params=pltpu.CompilerParams(has_side_effects=_EFFECT),
    )(grad, land, send_sems, recv_sems, after)[1]


_C1 = 1.0 - B1 ** STEP
_C2 = 1.0 - B2 ** STEP


def _adam_math(w, g, m, v):
    m = B1 * m + (1.0 - B1) * g
    v = B2 * v + (1.0 - B2) * (g * g)
    delta = -LR * ((m / _C1) / (jnp.sqrt(v / _C2) + AEPS) + WD * w)
    return delta, m, v


def _adamw(w, m, v, groups, name):
    R, C = w.shape
    tr = R if R <= 256 else (128 if R % 128 == 0 else 176)
    assert R % tr == 0, (name, R)
    gparts = [p for grp in groups for p in grp]
    sizes = [len(grp) for grp in groups]
    ng = len(gparts)

    def body(*refs):
        w_ref, m_ref, v_ref = refs[:3]
        g_refs = list(refs[3:3 + ng])
        g_out, d_out, m_out, v_out = refs[3 + ng:]
        g = None
        for size in sizes:
            s = None
            for r in [g_refs.pop(0) for _ in range(size)]:
                terms = [r[q] for q in range(r.shape[0])] if len(r.shape) == 3 else [r[...]]
                for t in terms:
                    s = t.astype(F32) if s is None else s + t.astype(F32)
            g = s if g is None else g + s
        delta, mn, vn = _adam_math(w_ref[...], g, m_ref[...], v_ref[...])
        g_out[...] = g
        d_out[...] = delta
        m_out[...] = mn
        v_out[...] = vn

    blk = pl.BlockSpec((tr, C), lambda i: (i, 0))
    g_specs = [blk if p.ndim == 2 else pl.BlockSpec((p.shape[0], tr, C), lambda i: (0, i, 0)) for p in gparts]
    sds = jax.ShapeDtypeStruct((R, C), F32)
    return pl.pallas_call(
        body, name=name, out_shape=(sds, sds, sds, sds), grid=(R // tr,),
        in_specs=[blk, blk, blk] + g_specs, out_specs=(blk, blk, blk, blk),
        compiler_params=_cp(("parallel",)))(w, m, v, *gparts)


def _adamw_small(stack, names, wts, mom, var, sum_only, name):
    items, row = [], 0
    for n in names:
        shape = (KW, CW) if n == "conv_w" else wts[n].shape
        size = int(np.prod(shape))
        vec = len(shape) == 2 and shape[0] == 1 and n not in sum_only
        view = shape if vec else (-(-size // _PACK_COLS), _PACK_COLS)
        items.append((n, row, size, vec, view))
        row += _pack_rows(shape)
    upd = [it for it in items if it[0] not in sum_only]
    operands = [stack]
    for n, _, _, _, view in upd:
        operands += [d[n].reshape(view) for d in (wts, mom, var)]

    def grad(stack_ref, r0, nrows, ncols):
        g = stack_ref[0, r0:r0 + nrows, 0:ncols]
        for q in range(1, N_DEV):
            g = g + stack_ref[q, r0:r0 + nrows, 0:ncols]
        return g

    def body(*refs):
        stack_ref, ins, outs = refs[0], refs[1:1 + 3 * len(upd)], refs[1 + 3 * len(upd):]
        o = 0
        for idx, (n, r0, size, vec, view) in enumerate(upd):
            w_ref, m_ref, v_ref = ins[3 * idx:3 * idx + 3]
            g_out, d_out, m_out, v_out = outs[o:o + 4]
            o += 4
            if vec:
                pieces = [(j, j * _PACK_COLS, min((j + 1) * _PACK_COLS, size)) for j in range(-(-size // _PACK_COLS))]
            else:
                pieces = [(None, 0, _PACK_COLS)]
            for j, lo, hi in pieces:
                if vec:
                    g = grad(stack_ref, r0 + j, 1, hi - lo)
                    sl = (slice(None), slice(lo, hi))
                else:
                    g = grad(stack_ref, r0, view[0], _PACK_COLS)
                    sl = (slice(None), slice(None))
                delta, mn, vn = _adam_math(w_ref[sl], g, m_ref[sl], v_ref[sl])
                g_out[sl] = g
                d_out[sl] = delta
                m_out[sl] = mn
                v_out[sl] = vn
        for n, r0, size, vec, view in items:
            if n in sum_only:
                outs[o][...] = grad(stack_ref, r0, view[0], _PACK_COLS)
                o += 1

    out_shape = []
    for n, _, _, _, view in upd:
        out_shape += [jax.ShapeDtypeStruct(view, F32)] * 4
    out_shape += [jax.ShapeDtypeStruct(view, F32) for n, _, _, _, view in items if n in sum_only]
    vm = pl.BlockSpec(memory_space=pltpu.VMEM)
    res = pl.pallas_call(
        body, name=name, out_shape=tuple(out_shape), in_specs=[vm] * len(operands),
        out_specs=tuple([vm] * len(out_shape)),
        compiler_params=pltpu.CompilerParams(vmem_limit_bytes=VMEM_LIMIT))(*operands)
    updated = {n: tuple(r.reshape(wts[n].shape) for r in res[4 * i:4 * i + 4]) for i, (n, *_) in enumerate(upd)}
    sums = dict(zip([it[0] for it in items if it[0] in sum_only], res[4 * len(upd):]))
    return updated, sums


def _adamw_native(tensors, name):
    nt = len(tensors)

    def body(*refs):
        ins, outs = refs[:4 * nt], refs[4 * nt:]
        for t in range(nt):
            w_ref, m_ref, v_ref, g_ref = ins[4 * t:4 * t + 4]
            g = g_ref[...]
            delta, mn, vn = _adam_math(w_ref[...], g, m_ref[...], v_ref[...])
            outs[4 * t][...] = g
            outs[4 * t + 1][...] = delta
            outs[4 * t + 2][...] = mn
            outs[4 * t + 3][...] = vn

    vm = pl.BlockSpec(memory_space=pltpu.VMEM)
    flat = [a for tup in tensors for a in tup]
    res = pl.pallas_call(
        body, name=name, out_shape=tuple(jax.ShapeDtypeStruct(tup[0].shape, F32) for tup in tensors for _ in range(4)),
        in_specs=[vm] * len(flat), out_specs=tuple([vm] * (4 * nt)),
        compiler_params=pltpu.CompilerParams(vmem_limit_bytes=VMEM_LIMIT))(*flat)
    return [tuple(res[4 * t:4 * t + 4]) for t in range(nt)]


def _mod_shard(c_all, w_ada, b_ada_cols):
    n = w_ada.shape[1]
    tn = 512

    def body(c_ref, w_ref, b_ref, o_ref):
        cv = c_ref[...]
        ca = (cv * _sig(cv)).astype(BF16)
        o_ref[...] = jnp.dot(ca, w_ref[...].astype(BF16), preferred_element_type=F32) + b_ref[...]

    return pl.pallas_call(
        body, name="mod_shard", out_shape=jax.ShapeDtypeStruct((N_DEV, n), F32), grid=(n // tn,),
        in_specs=[_full((N_DEV, D_MODEL)), pl.BlockSpec((D_MODEL, tn), lambda j: (0, j)),
                  pl.BlockSpec((1, tn), lambda j: (0, j))],
        out_specs=pl.BlockSpec((N_DEV, tn), lambda j: (0, j)),
        compiler_params=_cp(("parallel",)))(c_all, w_ada, b_ada_cols)


def _ada_grad(c_all, dmod_cols, after):
    n = dmod_cols.shape[1]
    tn = 512

    def body(c_ref, d_ref, after_ref, o_ref):
        cv = c_ref[...]
        ca = cv * _sig(cv)
        o_ref[...] = lax.dot_general(ca, d_ref[...], (((0,), (0,)), ((), ())),
                                     preferred_element_type=F32, precision=lax.Precision.HIGHEST)

    return pl.pallas_call(
        body, name="ada_grad", out_shape=jax.ShapeDtypeStruct((D_MODEL, n), F32), grid=(n // tn,),
        in_specs=[_full((N_DEV, D_MODEL)), pl.BlockSpec((N_DEV, tn), lambda j: (0, j)),
                  pl.BlockSpec(memory_space=pl.ANY)],
        out_specs=pl.BlockSpec((D_MODEL, tn), lambda j: (0, j)),
        compiler_params=_cp(("parallel",)))(c_all, dmod_cols, after)


def _ssm_tables(W):
    e_re, e_im, bb_re, bb_im = _ssm_prep(W["ssm_a_re"], W["ssm_a_im"], W["ssm_b_re"], W["ssm_b_im"], W["ssm_log_dt"])
    bb, cm = _block_diag_mats(bb_re, bb_im, W["ssm_c_re"], W["ssm_c_im"])
    bb16, cm16 = bb.astype(BF16), cm.astype(BF16)
    return (bb16, cm16, jnp.swapaxes(bb16, 1, 2), jnp.swapaxes(cm16, 1, 2),
            _scan_tables(e_re, e_im, False), _scan_tables(e_re, e_im, True))


def _device_step(x, mod, W, tables, tgt, getw, put, early):
    sh1, sc1, g1, sh2, sc2, g2 = [mod[:, i * D_MODEL:(i + 1) * D_MODEL] for i in range(6)]
    bb16, cm16, bbt16, cmt16, tab_f, tab_b = tables

    w_in = getw("w_in", [mod, *tables])
    h1, z = _in_proj(x, W["norm1_g"], sc1, sh1, w_in)
    yc, scv = _conv_fwd(z, W["conv_w"], W["conv_b"], W["conv_ln_g"], W["conv_ln_b"])
    xs, ys, yg = _ssm_fwd(z, bb16, cm16, W["ssm_d"], tab_f)
    w_cp, w_glu, w_out = getw("conv_proj", scv), getw("ssm_glu", yg), getw("w_out", yg)
    y_conv, zz, merged, o, x2, h2 = _mix_fwd(scv, yg, z, x, w_cp, w_glu, w_out, g1, W["norm2_g"], sc2, sh2)
    w_fi = getw("w_ffn_in", h2)
    f, act = _ffn_in_act(h2, w_fi)
    w_fo = getw("w_ffn_out", act)
    dx3, do2, loss8, dfg8, dg2_8 = _ffn_out_final(x2, act, w_fo, g2, W["final_g"], tgt)

    sm = {}
    tok = put("w_ffn_out", _matmul(act, do2, "tn", 1408, 1024, 2048, BF16, "mm_g_ffn_out"))
    df = _ffn_bwd(do2, w_fo, f, tok)
    tok = put("w_ffn_in", _matmul(h2, df, "tn", 1024, 1408, 2048, BF16, "mm_g_ffn_in"))
    dx2, do, dsh2, dsc2, dn2, dg1_8 = _normmod_bwd(df, w_fi, x2, dx3, W["norm2_g"], sc2, g1, o, tok, "d_h2_normmod2_bwd")
    tok = put("w_out", _matmul(merged, do, "tn", 1024, 1024, 4096, BF16, "mm_g_w_out"))
    dyconv, dgl, dzz = _mix_bwd(do, w_out, z, zz, y_conv, tok)
    tok = put("ssm_glu", _matmul(yg, dzz, "tn", 512, 1024, 4096, BF16, "mm_g_ssm_glu"))
    tok = put("conv_proj", _matmul(scv, dyconv, "tn", 512, 1024, 4096, BF16, "mm_g_conv_proj", after=tok))
    du, de16, dd8, dc_full, dbb_full = _ssm_bwd(dzz, w_glu, ys, z, xs, cmt16, bbt16, W["ssm_d"], tab_b, tok)
    dyc, dlg8, dlb8, dcb8 = _conv_bwd_ln(dyconv, w_cp, yc, W["conv_ln_g"], W["conv_ln_b"])
    dz_conv, dcw = _conv_bwd(dyc, z, W["conv_w"])

    s8 = lambda a: jnp.sum(a, axis=0, keepdims=True)
    de = de16.reshape(2, 8, NST).sum(1)
    de_re, de_im = de[0].reshape(G, P), de[1].reshape(G, P)
    dc_re, dc_im = _diag_blocks(dc_full)
    dc_im = -dc_im
    dbb_re, dbb_im = [jnp.swapaxes(t, 1, 2) for t in _diag_blocks(dbb_full)]
    _, vjp = jax.vjp(_ssm_prep, W["ssm_a_re"], W["ssm_a_im"], W["ssm_b_re"], W["ssm_b_im"], W["ssm_log_dt"])
    sm["ssm_a_re"], sm["ssm_a_im"], sm["ssm_b_re"], sm["ssm_b_im"], sm["ssm_log_dt"] = vjp((de_re, de_im, dbb_re, dbb_im))
    sm["ssm_c_re"], sm["ssm_c_im"] = dc_re, dc_im
    sm["ssm_d"] = s8(dd8)
    sm["norm2_g"] = s8(dn2)
    sm["conv_b"], sm["conv_ln_g"], sm["conv_ln_b"] = s8(dcb8), s8(dlg8), s8(dlb8)
    sm["conv_w"] = dcw.reshape(KW, 8, CW).sum(1)
    sm["final_g"] = s8(dfg8)
    tok = early(sm)

    dz = [dz_conv, du, dgl]
    tok = put("w_in", _matmul(h1, dz, "tn", 1024, 512, 4096, BF16, "mm_g_w_in", after=tok))
    dx, _, dsh1, dsc1, dn1, _ = _normmod_bwd(dz, w_in, x, dx2, W["norm1_g"], sc1, g1, o, tok, "d_h1_normmod1_bwd")
    dmod = jnp.concatenate([s8(dsh1), s8(dsc1), s8(dg1_8), s8(dsh2), s8(dsc2), s8(dg2_8)], axis=1)
    return loss8, dx, s8(dn1), dmod


_BIG = ("w_in", "conv_proj", "ssm_glu", "w_out", "w_ffn_in", "w_ffn_out")
_BIG_AXIS = {"w_in": 1, "conv_proj": 1, "ssm_glu": 1, "w_out": 0, "w_ffn_in": 1, "w_ffn_out": 0}
_EARLY = ("conv_w", "conv_b", "conv_ln_g", "conv_ln_b", "ssm_a_re", "ssm_a_im", "ssm_b_re", "ssm_b_im", "ssm_c_re",
          "ssm_c_im", "ssm_d", "ssm_log_dt", "norm2_g", "final_g")
_LATE = ("norm1_g", "b_ada")
_S5_MATS = ("ssm_a_re", "ssm_a_im", "ssm_b_re", "ssm_b_im", "ssm_c_re", "ssm_c_im")
_ORDER = ("w_ada", "b_ada", "norm1_g", "w_in", "conv_w", "conv_b", "conv_ln_g", "conv_ln_b", "conv_proj",
          "ssm_a_re", "ssm_a_im", "ssm_b_re", "ssm_b_im", "ssm_c_re", "ssm_c_im", "ssm_d", "ssm_log_dt", "ssm_glu",
          "w_out", "norm2_g", "w_ffn_in", "w_ffn_out", "final_g")
_PACK_COLS = 1024


def _pack_rows(shape):
    return -(-int(np.prod(shape)) // (8 * _PACK_COLS)) * 8


def _pack(arrs):
    parts = []
    for a in arrs:
        flat = a.reshape(-1)
        n = _pack_rows(a.shape)
        parts.append(jnp.pad(flat, (0, n * _PACK_COLS - flat.shape[0])).reshape(n, _PACK_COLS))
    return jnp.concatenate(parts, 0)


def kernel(x, c, w_ada, b_ada, norm1_g, w_in, conv_w, conv_b, conv_ln_g, conv_ln_b, conv_proj, ssm_a_re, ssm_a_im, ssm_b_re, ssm_b_im, ssm_c_re, ssm_c_im, ssm_d, ssm_log_dt, ssm_glu, w_out, norm2_g, w_ffn_in, w_ffn_out, final_g, loss_target, m_w_ada, m_b_ada, m_norm1_g, m_w_in, m_conv_w, m_conv_b, m_conv_ln_g, m_conv_ln_b, m_conv_proj, m_ssm_a_re, m_ssm_a_im, m_ssm_b_re, m_ssm_b_im, m_ssm_c_re, m_ssm_c_im, m_ssm_d, m_ssm_log_dt, m_ssm_glu, m_w_out, m_norm2_g, m_w_ffn_in, m_w_ffn_out, m_final_g, v_w_ada, v_b_ada, v_norm1_g, v_w_in, v_conv_w, v_conv_b, v_conv_ln_g, v_conv_ln_b, v_conv_proj, v_ssm_a_re, v_ssm_a_im, v_ssm_b_re, v_ssm_b_im, v_ssm_c_re, v_ssm_c_im, v_ssm_d, v_ssm_log_dt, v_ssm_glu, v_w_out, v_norm2_g, v_w_ffn_in, v_w_ffn_out, v_final_g):
    given = dict(locals())
    mx, my, mc = _me()
    chip = 2 * mx + my
    dev = 4 * mx + 2 * my + mc
    def canon(a):
        return a.reshape(1, -1) if a.ndim <= 2 else a[0]

    wts = {n: canon(given[n]) for n in _ORDER}
    mom = {n: canon(given["m_" + n]) for n in _ORDER}
    var = {n: canon(given["v_" + n]) for n in _ORDER}

    W = {n: wts[n] for n in _ORDER if n not in _BIG}
    rest = [n for n in _BIG if n != "w_in"]
    rest_shards = [wts[n].astype(BF16) for n in rest]
    state_in, token = _half_gather_start(wts["w_in"].astype(BF16), c, "gather_start_w_in")
    W["ssm_log_dt"] = wts["ssm_log_dt"] + token[0:1, 0:1]
    W["ssm_c_re"] = wts["ssm_c_re"] + token[0, 0]
    tables = _ssm_tables(W)

    c_all = _allgather8(jnp.broadcast_to(c, (8, D_MODEL)), "gather_c", after=[*tables, *rest_shards])[:, 0, :]
    n_ada = wts["w_ada"].shape[1]
    b_cols = lax.dynamic_slice(wts["b_ada"], (0, chip * n_ada), (1, n_ada))
    mod_cols = _mod_shard(c_all, wts["w_ada"], b_cols)
    halves = _half_gather_wait(state_in, [mod_cols], "gather_wait_w_in")
    state_in, token = _half_swap_start(halves, "gather_swap_start_w_in")
    mods = _allgather8(mod_cols, "gather_mod", after=[token])
    mod = jnp.concatenate([lax.dynamic_index_in_dim(mods[2 * q], dev, 0, keepdims=True) for q in range(N_CHIP)], axis=1)
    conv_w_full = _allgather8(jnp.pad(wts["conv_w"], ((0, 1), (0, 0))), "gather_conv_w", after=[token])
    W["conv_w"] = jnp.concatenate([conv_w_full[2 * q, :KW] for q in range(N_CHIP)], axis=1)
    w_in_full = _half_swap_wait(state_in, mod + W["conv_w"][0:1, 0:1], "gather_swap_wait_w_in")
    gstate, token = _gather_start(rest_shards, [_BIG_AXIS[n] for n in rest], w_in_full, "gather_start_rest")
    gstate = dict(zip(rest, gstate))
    mod = mod + token[0:1, 0:1]

    def getw(n, after):
        if n == "w_in":
            return w_in_full
        return _gather_wait(gstate[n], _BIG_AXIS[n], after, "gather_wait_" + n)

    sstate, estate = {}, []

    def put(n, g):
        sstate[n], tok = _scatter_start(g, _BIG_AXIS[n], "scatter_start_" + n)
        return tok

    first5 = [n for n in _BIG if n != "w_in"]

    def early(sm):
        state, tok = _all8_start(_pack([sm[n] for n in _EARLY]), "small_start")
        estate.append(state)
        held = [_scatter_wait(sstate[n], _BIG_AXIS[n], tok, "scatter_wait_" + n) for n in first5]
        state, tok = _swap_start(held, tok, "swap_start")
        estate.append(state)
        return tok

    loss8, dx, dn1, dmod = _device_step(x[0], mod, W, tables, loss_target[0], getw, put, early)

    late_state, tok = _all8_start(_pack([dn1, dmod, loss8]), "late_start")
    held5, sib5 = _swap_wait(estate[1], tok, "swap_wait")
    outs = {}
    for i, n in enumerate(first5):
        outs[n] = _adamw(wts[n], mom[n], var[n], [[held5[i]], [sib5[i]]], "adamw_" + n)
    allp = _all8_wait(estate[0], dx, "small_wait")
    upd, sums = _adamw_small(allp, _EARLY, wts, mom, var, ("conv_w",) + _S5_MATS, "adamw_small")
    outs.update(upd)

    def swapped(n, a):
        return jnp.swapaxes(a, 1, 2) if n in ("ssm_b_re", "ssm_b_im") else a

    def summed(n):
        return swapped(n, sums[n].reshape(-1)[:wts[n].size].reshape(wts[n].shape))

    res = _adamw_native([(swapped(n, wts[n]), swapped(n, mom[n]), swapped(n, var[n]), summed(n)) for n in _S5_MATS],
                        "adamw_s5")
    for n, r in zip(_S5_MATS, res):
        outs[n] = tuple(swapped(n, a) for a in r)

    late = _all8_wait(late_state, outs[first5[-1]][1], "late_wait")
    n_late = _pack_rows((D_MODEL,)) + _pack_rows((6 * D_MODEL,))
    loss = jnp.sum(late[:, n_late:, :])
    late = late[:, :n_late, :]
    held_in = _scatter_wait(sstate["w_in"], _BIG_AXIS["w_in"], late, "scatter_wait_w_in")
    state_in, tok = _swap_start([held_in], late, "swap_start_w_in")

    r1 = _pack_rows((D_MODEL,))
    dmod_all = late[:, r1:, :].reshape(N_DEV, -1)[:, :6 * D_MODEL]
    dmod_cols = lax.dynamic_slice(dmod_all, (0, chip * n_ada), (N_DEV, n_ada))
    g_ada = _ada_grad(c_all, dmod_cols, tok)
    outs["w_ada"] = _adamw(wts["w_ada"], mom["w_ada"], var["w_ada"], [[g_ada]], "adamw_w_ada")
    upd, _ = _adamw_small(late, _LATE, wts, mom, var, (), "adamw_late")
    outs.update(upd)
    held_in, sib_in = _swap_wait(state_in, outs["w_ada"][1], "swap_wait_w_in")
    outs["w_in"] = _adamw(wts["w_in"], mom["w_in"], var["w_in"], [held_in, sib_in], "adamw_w_in")
    g_cw_full = sums["conv_w"].reshape(-1)[:KW * CW].reshape(KW, CW)
    g_cw = lax.dynamic_slice(g_cw_full, (0, chip * (CW // N_CHIP)), (KW, CW // N_CHIP))
    pad = lambda a: jnp.pad(a, ((0, 1), (0, 0)))
    r_cw = _adamw(pad(wts["conv_w"]), pad(mom["conv_w"]), pad(var["conv_w"]), [[pad(g_cw)]], "adamw_conv_w")
    outs["conv_w"] = tuple(r[:KW] for r in r_cw)

    def shaped(n, a):
        return a.reshape(given[n].shape)

    result = [loss, dx[None]]
    for q in range(4):
        result += [shaped(n, outs[n][q]) for n in _ORDER]
    return tuple(result)
```

```python
import math

import jax
import jax.numpy as jnp
import numpy as np
from jax import lax
from jax.experimental import pallas as pl
from jax.experimental.pallas import tpu as pltpu

F32 = jnp.float32
BF16 = jnp.bfloat16
EPS = 1e-6
D_MODEL = 1024
CW = 512
KW = 31
HALO = 32
G, P, H = 32, 64, 16
NST = G * P
FH = 2816
N_DEV = 8
N_CHIP = 4
VMEM_LIMIT = 56 * 1024 * 1024
LR, B1, B2, AEPS, WD, STEP = 0.001, 0.9, 0.999, 1e-08, 0.01, 10
MESH = pl.DeviceIdType.MESH


def _cp(sem=None):
    return pltpu.CompilerParams(dimension_semantics=sem, vmem_limit_bytes=VMEM_LIMIT)


def _sig(x):
    return jax.nn.sigmoid(x)


def _full(shape):
    return pl.BlockSpec(shape, lambda *_: (0,) * len(shape))


def _resident(shape):
    return pl.BlockSpec(shape, lambda *_: (0,) * len(shape), pipeline_mode=pl.Buffered(1))


def _colsum8(v):
    t, c = v.shape
    return jnp.sum(v.reshape(t // 8, 8, c), axis=0)


def _matmul(a, b, mode, tm, tn, tk, out_dtype, name, after=None, n_outer=False, m_cols=None):
    m0 = 0
    b_parts = list(b) if isinstance(b, (list, tuple)) else [b]
    if mode == "nn":
        (M, K), N = a.shape, b.shape[1]
    elif mode == "nt":
        (M, K), N = a.shape, b.shape[0]
    else:
        (K, M), N = a.shape, sum(p.shape[1] for p in b_parts)
        if m_cols is not None:
            m0, M = m_cols
    tm, tn, tk = min(tm, M), min(tn, N), min(tk, K)
    assert M % tm == 0 and N % tn == 0 and K % tk == 0 and m0 % tm == 0, (name, M, N, K, tm, tn, tk)
    assert len(b_parts) == 1 or (mode == "tn" and all(p.shape[1] % tn == 0 for p in b_parts)), name
    nk = K // tk
    mb = m0 // tm
    counts = [p.shape[1] // tn for p in b_parts] if mode == "tn" else [N // tn]
    starts = [sum(counts[:p]) for p in range(len(counts))]

    def ij(fn):
        return (lambda j, i, k: fn(i, j, k)) if n_outer else fn

    if mode == "nn":
        a_spec = pl.BlockSpec((tm, tk), ij(lambda i, j, k: (i, k)))
        b_spec = pl.BlockSpec((tk, tn), ij(lambda i, j, k: (k, j)))
        dims = (((1,), (0,)), ((), ()))
    elif mode == "nt":
        a_spec = pl.BlockSpec((tm, tk), ij(lambda i, j, k: (i, k)))
        b_spec = pl.BlockSpec((tn, tk), ij(lambda i, j, k: (j, k)))
        dims = (((1,), (1,)), ((), ()))
    else:
        a_spec = pl.BlockSpec((tk, tm), ij(lambda i, j, k: (k, i + mb)))
        dims = (((0,), (0,)), ((), ()))
    if mode == "tn":
        b_specs = [pl.BlockSpec((tk, tn), ij(lambda i, j, k, s=s, n=n: (k, jnp.clip(j - s, 0, n - 1))))
                   for s, n in zip(starts, counts)]
    else:
        b_specs = [b_spec]
    nb = len(b_parts)

    def body(a_ref, *rest):
        b_refs = rest[:nb]
        o_ref, acc_ref = rest[-2:]
        j = pl.program_id(0 if n_outer else 1)
        k = pl.program_id(2)

        def compute(b_ref):
            part = lax.dot_general(a_ref[...].astype(BF16), b_ref[...].astype(BF16), dims,
                                   preferred_element_type=F32)
            if nk == 1:
                o_ref[...] = part.astype(out_dtype)
            else:
                @pl.when(k == 0)
                def _():
                    acc_ref[...] = part

                @pl.when(k > 0)
                def _():
                    acc_ref[...] += part

                @pl.when(k == nk - 1)
                def _():
                    o_ref[...] = acc_ref[...].astype(out_dtype)

        if nb == 1:
            compute(b_refs[0])
        else:
            for p in range(nb):
                pl.when(jnp.logical_and(j >= starts[p], j < starts[p] + counts[p]))(
                    lambda b_ref=b_refs[p]: compute(b_ref))

    return pl.pallas_call(
        body, name=name,
        out_shape=jax.ShapeDtypeStruct((M, N), out_dtype),
        grid=(N // tn, M // tm, nk) if n_outer else (M // tm, N // tn, nk),
        in_specs=[a_spec] + b_specs + ([] if after is None else [pl.BlockSpec(memory_space=pl.ANY)]),
        out_specs=pl.BlockSpec((tm, tn), ij(lambda i, j, k: (i, j))),
        scratch_shapes=[pltpu.VMEM((tm, tn) if nk > 1 else (8, 128), F32)],
        compiler_params=_cp(("parallel", "parallel", "arbitrary")),
    )(*([a] + b_parts + ([] if after is None else [after])))


def _row_tile(S):
    return min(512, S)


def _in_proj(x, g, sc, sh, w_in):
    S, D = x.shape
    N = w_in.shape[1]
    tm = min(512, S)

    def body(x_ref, g_ref, sc_ref, sh_ref, w_ref, h_ref, z_ref):
        xv = x_ref[...]
        r = lax.rsqrt(jnp.mean(xv * xv, axis=-1, keepdims=True) + EPS)
        h = (xv * r * (g_ref[...] * (1.0 + sc_ref[...])) + sh_ref[...]).astype(BF16)
        h_ref[...] = h
        z_ref[...] = jnp.dot(h, w_ref[...], preferred_element_type=F32).astype(BF16)

    row = pl.BlockSpec((tm, D), lambda i: (i, 0))
    par = _full((1, D))
    return pl.pallas_call(
        body, name="in_proj",
        out_shape=(jax.ShapeDtypeStruct((S, D), BF16), jax.ShapeDtypeStruct((S, N), BF16)), grid=(S // tm,),
        in_specs=[row, par, par, par, _resident((D, N))], out_specs=(row, pl.BlockSpec((tm, N), lambda i: (i, 0))),
        compiler_params=_cp(("parallel",)))(x, g, sc, sh, w_in)


def _fill_shifted(buf_ref, sh_ref):
    n = buf_ref.shape[0] - 8
    for s in range(1, 8):
        sh_ref[s, 0:n, :] = buf_ref[s:s + n, :]


def _window(buf_ref, sh_ref, off, n):
    s = off % 8
    return buf_ref[off:off + n, :] if s == 0 else sh_ref[s, off - s:off - s + n, :]


def _conv_fwd(z, conv_w, conv_b, ln_g, ln_b):
    S = z.shape[0]
    tm = min(128, S)
    sub = 32
    hb = tm // HALO

    def body(a_ref, g_ref, ha_ref, hg_ref, w_ref, b_ref, lg_ref, lb_ref, yc_ref, s_ref, ug_ref, sh_ref):
        i = pl.program_id(0)
        halo = ha_ref[...].astype(F32) * _sig(hg_ref[...].astype(F32))
        ug_ref[0:HALO, :] = jnp.where(i == 0, 0.0, halo)
        ug_ref[HALO:, :] = a_ref[...].astype(F32) * _sig(g_ref[...].astype(F32))
        _fill_shifted(ug_ref, sh_ref)
        for rb in range(tm // sub):
            acc = jnp.zeros((sub, CW), F32) + b_ref[...]
            for k in range(KW):
                off = rb * sub + HALO - (KW - 1) + k
                acc = acc + w_ref[k:k + 1, :] * _window(ug_ref, sh_ref, off, sub)
            yc_ref[rb * sub:(rb + 1) * sub, :] = acc
            mu = jnp.mean(acc, axis=-1, keepdims=True)
            cen = acc - mu
            rstd = lax.rsqrt(jnp.mean(cen * cen, axis=-1, keepdims=True) + EPS)
            ln = cen * rstd * lg_ref[...] + lb_ref[...]
            s_ref[rb * sub:(rb + 1) * sub, :] = (ln * _sig(ln)).astype(BF16)

    prev = lambda i: (jnp.maximum(i * hb - 1, 0), 0)
    return pl.pallas_call(
        body, name="conv_fwd",
        out_shape=(jax.ShapeDtypeStruct((S, CW), F32), jax.ShapeDtypeStruct((S, CW), BF16)),
        grid=(S // tm,),
        in_specs=[pl.BlockSpec((tm, CW), lambda i: (i, 0)), pl.BlockSpec((tm, CW), lambda i: (i, 1)),
                  pl.BlockSpec((HALO, CW), prev), pl.BlockSpec((HALO, CW), lambda i: (jnp.maximum(i * hb - 1, 0), 1)),
                  _full((KW, CW)), _full((1, CW)), _full((1, CW)), _full((1, CW))],
        out_specs=(pl.BlockSpec((tm, CW), lambda i: (i, 0)), pl.BlockSpec((tm, CW), lambda i: (i, 0))),
        scratch_shapes=[pltpu.VMEM((tm + HALO, CW), F32), pltpu.VMEM((8, tm + HALO, CW), F32)],
        compiler_params=_cp(("parallel",)))(z, z, z, z, conv_w, conv_b, ln_g, ln_b)


def _conv_bwd_ln(dyconv, w_cp, yc, ln_g, ln_b):
    S = yc.shape[0]
    tm = _row_tile(S)

    def body(dy_ref, w_ref, yc_ref, lg_ref, lb_ref, dyc_ref, dlg_ref, dlb_ref, dcb_ref):
        i = pl.program_id(0)
        dsc = lax.dot_general(dy_ref[...], w_ref[...], (((1,), (1,)), ((), ())), preferred_element_type=F32)
        yc_v = yc_ref[...]
        mu = jnp.mean(yc_v, axis=-1, keepdims=True)
        cen = yc_v - mu
        rstd = lax.rsqrt(jnp.mean(cen * cen, axis=-1, keepdims=True) + EPS)
        yn = cen * rstd
        ln = yn * lg_ref[...] + lb_ref[...]
        sl = _sig(ln)
        dln = dsc * (sl * (1.0 + ln * (1.0 - sl)))
        dyn = dln * lg_ref[...]
        dyc = rstd * (dyn - jnp.mean(dyn, axis=-1, keepdims=True)
                      - yn * jnp.mean(dyn * yn, axis=-1, keepdims=True))
        dyc_ref[...] = dyc

        @pl.when(i == 0)
        def _():
            dlg_ref[...] = jnp.zeros_like(dlg_ref)
            dlb_ref[...] = jnp.zeros_like(dlb_ref)
            dcb_ref[...] = jnp.zeros_like(dcb_ref)

        dlg_ref[...] += _colsum8(dln * yn)
        dlb_ref[...] += _colsum8(dln)
        dcb_ref[...] += _colsum8(dyc)

    row = pl.BlockSpec((tm, CW), lambda i: (i, 0))
    acc = jax.ShapeDtypeStruct((8, CW), F32)
    return pl.pallas_call(
        body, name="conv_bwd_ln",
        out_shape=(jax.ShapeDtypeStruct((S, CW), F32), acc, acc, acc), grid=(S // tm,),
        in_specs=[pl.BlockSpec((tm, D_MODEL), lambda i: (i, 0)), _full((CW, D_MODEL)), row, _full((1, CW)),
                  _full((1, CW))],
        out_specs=(row, _full((8, CW)), _full((8, CW)), _full((8, CW))),
        compiler_params=_cp(("arbitrary",)))(dyconv, w_cp, yc, ln_g, ln_b)


def _conv_bwd(dyc, z, conv_w):
    S = z.shape[0]
    tm = min(128, S)
    sub = 32
    hb = tm // HALO
    nt = S // tm

    def body(d_ref, dn_ref, a_ref, g_ref, ha_ref, hg_ref, w_ref, dz_ref, dw_ref, ug_ref, dy_ref, ugs_ref, dys_ref):
        i = pl.program_id(0)
        halo = ha_ref[...].astype(F32) * _sig(hg_ref[...].astype(F32))
        ug_ref[0:HALO, :] = jnp.where(i == 0, 0.0, halo)
        a = a_ref[...].astype(F32)
        sg = _sig(g_ref[...].astype(F32))
        ug_ref[HALO:, :] = a * sg
        dy_ref[0:tm, :] = d_ref[...]
        dy_ref[tm:, :] = jnp.where(i == nt - 1, 0.0, dn_ref[...])
        _fill_shifted(ug_ref, ugs_ref)
        _fill_shifted(dy_ref, dys_ref)

        @pl.when(i == 0)
        def _():
            dw_ref[...] = jnp.zeros_like(dw_ref)

        for rb in range(tm // sub):
            r0 = rb * sub
            acc = jnp.zeros((sub, CW), F32)
            dyc_b = dy_ref[r0:r0 + sub, :]
            for k in range(KW):
                up = r0 + (KW - 1) - k
                acc = acc + w_ref[k:k + 1, :] * _window(dy_ref, dys_ref, up, sub)
                off = r0 + HALO - (KW - 1) + k
                dw_ref[k * 8:(k + 1) * 8, :] += _colsum8(dyc_b * _window(ug_ref, ugs_ref, off, sub))
            a_b = a[r0:r0 + sub, :]
            sg_b = sg[r0:r0 + sub, :]
            dz_ref[r0:r0 + sub, 0:CW] = (acc * sg_b).astype(BF16)
            dz_ref[r0:r0 + sub, CW:2 * CW] = (acc * a_b * sg_b * (1.0 - sg_b)).astype(BF16)

    return pl.pallas_call(
        body, name="conv_bwd",
        out_shape=(jax.ShapeDtypeStruct((S, 2 * CW), BF16), jax.ShapeDtypeStruct((KW * 8, CW), F32)),
        grid=(nt,),
        in_specs=[pl.BlockSpec((tm, CW), lambda i: (i, 0)),
                  pl.BlockSpec((HALO, CW), lambda i: (jnp.minimum((i + 1) * hb, nt * hb - 1), 0)),
                  pl.BlockSpec((tm, CW), lambda i: (i, 0)), pl.BlockSpec((tm, CW), lambda i: (i, 1)),
                  pl.BlockSpec((HALO, CW), lambda i: (jnp.maximum(i * hb - 1, 0), 0)),
                  pl.BlockSpec((HALO, CW), lambda i: (jnp.maximum(i * hb - 1, 0), 1)),
                  _full((KW, CW))],
        out_specs=(pl.BlockSpec((tm, 2 * CW), lambda i: (i, 0)), _full((KW * 8, CW))),
        scratch_shapes=[pltpu.VMEM((tm + HALO, CW), F32), pltpu.VMEM((tm + HALO, CW), F32),
                        pltpu.VMEM((8, tm + HALO, CW), F32), pltpu.VMEM((8, tm + HALO, CW), F32)],
        compiler_params=_cp(("arbitrary",)))(dyc, dyc, z, z, z, z, conv_w)


_GELU_C = math.sqrt(2.0 / math.pi)


def _gelu(x):
    return 0.5 * x * (1.0 + jnp.tanh(_GELU_C * (x + 0.044715 * x * x * x)))


def _gelu_grad(x):
    t = jnp.tanh(_GELU_C * (x + 0.044715 * x * x * x))
    return 0.5 * (1.0 + t) + 0.5 * x * (1.0 - t * t) * (_GELU_C * (1.0 + 3 * 0.044715 * x * x))


_NCL = 4
_UC = CW // _NCL
_LW = NST // _NCL
_CS = 2 * _LW


def _ssm_fwd(z, bb, cm, d, tab):
    S = z.shape[0]
    tm = min(512, S)

    def body(u_ref, bb_ref, cm_ref, d_ref, t_ref, x_ref, ys_ref, yg_ref, car_ref):
        i = pl.program_id(0)

        @pl.when(i == 0)
        def _():
            car_ref[...] = jnp.zeros_like(car_ref)

        u16 = u_ref[...]
        u = u16.astype(F32)
        for c in range(_NCL):
            lre = pl.ds(c * _CS, _LW)
            lim = pl.ds(c * _CS + _LW, _LW)
            tl = pl.ds(c * _LW, _LW)
            x_ref[:, c * _CS:(c + 1) * _CS] = jnp.dot(u16[:, c * _UC:(c + 1) * _UC], bb_ref[c],
                                                      preferred_element_type=F32)

            def blk(j, car):
                cr, ci = car
                rows = pl.ds(pl.multiple_of(j * 8, 8), 8)
                r = x_ref[rows, lre]
                im = x_ref[rows, lim]
                for lvl, s in enumerate((1, 2, 4)):
                    mr = t_ref[16 * lvl:16 * lvl + 8, tl]
                    mi = t_ref[16 * lvl + 8:16 * lvl + 16, tl]
                    sr = pltpu.roll(r, s, 0)
                    si = pltpu.roll(im, s, 0)
                    r, im = r + (mr * sr - mi * si), im + (mr * si + mi * sr)
                pr = t_ref[48:56, tl]
                pi_ = t_ref[56:64, tl]
                r, im = r + (pr * cr - pi_ * ci), im + (pr * ci + pi_ * cr)
                x_ref[rows, lre] = r
                x_ref[rows, lim] = im
                return (jnp.broadcast_to(r[7:8, :], (8, _LW)), jnp.broadcast_to(im[7:8, :], (8, _LW)))

            cr, ci = lax.fori_loop(0, tm // 8, blk, (car_ref[:, lre], car_ref[:, lim]))
            car_ref[:, lre] = cr
            car_ref[:, lim] = ci
            cols = slice(c * _UC, (c + 1) * _UC)
            ys = jnp.dot(x_ref[:, c * _CS:(c + 1) * _CS].astype(BF16), cm_ref[c], preferred_element_type=F32)
            ys = ys + d_ref[:, cols] * u[:, cols]
            ys_ref[:, cols] = ys
            yg_ref[:, cols] = _gelu(ys).astype(BF16)

    return pl.pallas_call(
        body, name="ssm_fwd",
        out_shape=(jax.ShapeDtypeStruct((S, 2 * NST), F32), jax.ShapeDtypeStruct((S, CW), F32),
                   jax.ShapeDtypeStruct((S, CW), BF16)),
        grid=(S // tm,),
        in_specs=[pl.BlockSpec((tm, CW), lambda i: (i, 2)), _full((_NCL, _UC, _CS)), _full((_NCL, _CS, _UC)),
                  _full((1, CW)), _full((64, NST))],
        out_specs=(pl.BlockSpec((tm, 2 * NST), lambda i: (i, 0)), pl.BlockSpec((tm, CW), lambda i: (i, 0)),
                   pl.BlockSpec((tm, CW), lambda i: (i, 0))),
        scratch_shapes=[pltpu.VMEM((8, 2 * NST), F32)],
        compiler_params=_cp(("arbitrary",)))(z, bb, cm, d, tab)


def _ssm_bwd(dzz, w_glu, ys, z, xs, cmt, bbt, d, tab, after):
    S = z.shape[0]
    tm = min(512, S)
    nt = S // tm
    tdims = (((0,), (0,)), ((), ()))

    def body(dzz_ref, wglu_ref, ys_ref, u_ref, x_ref, cmt_ref, bbt_ref, d_ref, t_ref, after_ref,
             du_ref, de_ref, dd_ref, dc_hbm, dbb_hbm, car_ref, lam_ref, dc_ref, dbb_ref):
        i = pl.program_id(0)

        @pl.when(i == 0)
        def _():
            car_ref[...] = jnp.zeros_like(car_ref)
            de_ref[...] = jnp.zeros_like(de_ref)
            dd_ref[...] = jnp.zeros_like(dd_ref)
            dc_ref[...] = jnp.zeros_like(dc_ref)
            dbb_ref[...] = jnp.zeros_like(dbb_ref)

        u16 = u_ref[...]
        u = u16.astype(F32)
        dyg = lax.dot_general(dzz_ref[...], wglu_ref[...], (((1,), (1,)), ((), ())), preferred_element_type=F32)
        dys = dyg * _gelu_grad(ys_ref[...])
        dys16 = dys.astype(BF16)
        dd_ref[...] += _colsum8(dys * u)
        row = lax.broadcasted_iota(jnp.int32, (8, _LW), 0)
        for c in range(_NCL):
            lre = pl.ds(c * _CS, _LW)
            lim = pl.ds(c * _CS + _LW, _LW)
            tl = pl.ds(c * _LW, _LW)
            cols = slice(c * _UC, (c + 1) * _UC)
            span = slice(c * _CS, (c + 1) * _CS)
            dc_ref[cols, :] += lax.dot_general(dys16[:, cols], x_ref[:, span].astype(BF16), tdims,
                                               preferred_element_type=F32)
            lam_ref[...] = jnp.dot(dys16[:, cols], cmt_ref[c], preferred_element_type=F32)

            def blk(jj, car):
                cr, ci, ar, ai = car
                j = tm // 8 - 1 - jj
                rows = pl.ds(pl.multiple_of(j * 8, 8), 8)
                r = lam_ref[rows, 0:_LW]
                im = lam_ref[rows, _LW:_CS]
                for lvl, s in enumerate((1, 2, 4)):
                    mr = t_ref[16 * lvl:16 * lvl + 8, tl]
                    mi = t_ref[16 * lvl + 8:16 * lvl + 16, tl]
                    sr = pltpu.roll(r, 8 - s, 0)
                    si = pltpu.roll(im, 8 - s, 0)
                    r, im = r + (mr * sr - mi * si), im + (mr * si + mi * sr)
                pr = t_ref[48:56, tl]
                pi_ = t_ref[56:64, tl]
                r, im = r + (pr * cr - pi_ * ci), im + (pr * ci + pi_ * cr)
                lam_ref[rows, 0:_LW] = r
                lam_ref[rows, _LW:_CS] = im
                nr = jnp.where(row == 7, cr, pltpu.roll(r, 7, 0))
                ni = jnp.where(row == 7, ci, pltpu.roll(im, 7, 0))
                xr = x_ref[rows, lre]
                xi = x_ref[rows, lim]
                ar = ar + (nr * xr + ni * xi)
                ai = ai + (ni * xr - nr * xi)
                return (jnp.broadcast_to(r[0:1, :], (8, _LW)), jnp.broadcast_to(im[0:1, :], (8, _LW)), ar, ai)

            zero = jnp.zeros((8, _LW), F32)
            cr, ci, ar, ai = lax.fori_loop(0, tm // 8, blk, (car_ref[:, lre], car_ref[:, lim], zero, zero))
            car_ref[:, lre] = cr
            car_ref[:, lim] = ci
            de_ref[0:8, tl] += ar
            de_ref[8:16, tl] += ai
            lam16 = lam_ref[...].astype(BF16)
            dbb_ref[cols, :] += lax.dot_general(u16[:, cols], lam16, tdims, preferred_element_type=F32)
            du = jnp.dot(lam16, bbt_ref[c], preferred_element_type=F32) + dys[:, cols] * d_ref[:, cols]
            du_ref[:, cols] = du.astype(BF16)

        @pl.when(i == nt - 1)
        def _():
            pltpu.sync_copy(dc_ref, dc_hbm)
            pltpu.sync_copy(dbb_ref, dbb_hbm)

    rev = lambda i: (nt - 1 - i, 0)
    once = lambda shape: pl.BlockSpec(shape, lambda *_: (0,) * len(shape), pipeline_mode=pl.Buffered(1))
    cross = jax.ShapeDtypeStruct((CW, _CS), F32)
    return pl.pallas_call(
        body, name="ssm_bwd",
        out_shape=(jax.ShapeDtypeStruct((S, CW), BF16), jax.ShapeDtypeStruct((16, NST), F32),
                   jax.ShapeDtypeStruct((8, CW), F32), cross, cross),
        grid=(nt,),
        in_specs=[pl.BlockSpec((tm, 2 * D_MODEL), rev), once((CW, 2 * D_MODEL)), pl.BlockSpec((tm, CW), rev),
                  pl.BlockSpec((tm, CW), lambda i: (nt - 1 - i, 2)), pl.BlockSpec((tm, 2 * NST), rev),
                  once((_NCL, _UC, _CS)), once((_NCL, _CS, _UC)), _full((1, CW)), once((64, NST)),
                  pl.BlockSpec(memory_space=pl.ANY)],
        out_specs=(pl.BlockSpec((tm, CW), rev), _full((16, NST)), _full((8, CW)),
                   pl.BlockSpec(memory_space=pl.ANY), pl.BlockSpec(memory_space=pl.ANY)),
        scratch_shapes=[pltpu.VMEM((8, 2 * NST), F32), pltpu.VMEM((tm, _CS), F32),
                        pltpu.VMEM((CW, _CS), F32), pltpu.VMEM((CW, _CS), F32)],
        compiler_params=_cp(("arbitrary",)))(dzz, w_glu, ys, z, xs, cmt, bbt, d, tab, after)


def _ssm_prep(a_re, a_im, b_re, b_im, log_dt):
    dt = jnp.exp(log_dt.reshape(G))[:, None]
    mag = jnp.exp(dt * a_re)
    e_re, e_im = mag * jnp.cos(dt * a_im), mag * jnp.sin(dt * a_im)
    n_re, n_im = e_re - 1.0, e_im
    den = a_re * a_re + a_im * a_im
    q_re = (n_re * a_re + n_im * a_im) / den
    q_im = (n_im * a_re - n_re * a_im) / den
    bb_re = q_re[..., None] * b_re - q_im[..., None] * b_im
    bb_im = q_re[..., None] * b_im + q_im[..., None] * b_re
    return e_re, e_im, bb_re, bb_im


def _scan_tables(e_re, e_im, reverse):
    er = e_re.reshape(1, NST)
    ei = e_im.reshape(1, NST)
    if reverse:
        ei = -ei
    pows = [(er, ei)]
    for _ in range(7):
        pr, pi_ = pows[-1]
        pows.append((pr * er - pi_ * ei, pr * ei + pi_ * er))
    row = jnp.arange(8)[:, None]
    out = []
    for s in (1, 2, 4):
        pr, pi_ = pows[s - 1]
        keep = (row + s <= 7) if reverse else (row >= s)
        out += [jnp.where(keep, pr, 0.0), jnp.where(keep, pi_, 0.0)]
    allr = jnp.concatenate([p[0] for p in pows], 0)
    alli = jnp.concatenate([p[1] for p in pows], 0)
    if reverse:
        allr, alli = allr[::-1], alli[::-1]
    out += [allr, alli]
    return jnp.concatenate(out, 0).astype(F32)


def _block_diag_mats(bb_re, bb_im, c_re, c_im):
    gc = G // _NCL
    eye = jnp.eye(gc, dtype=F32)
    bre = jnp.einsum("cjph,jk->cjhkp", bb_re.reshape(_NCL, gc, P, H), eye).reshape(_NCL, _UC, _LW)
    bim = jnp.einsum("cjph,jk->cjhkp", bb_im.reshape(_NCL, gc, P, H), eye).reshape(_NCL, _UC, _LW)
    bb = jnp.concatenate([bre, bim], 2)
    cre = jnp.einsum("cjhp,jk->cjpkh", c_re.reshape(_NCL, gc, H, P), eye).reshape(_NCL, _LW, _UC)
    cim = jnp.einsum("cjhp,jk->cjpkh", c_im.reshape(_NCL, gc, H, P), eye).reshape(_NCL, _LW, _UC)
    cm = jnp.concatenate([cre, -cim], 1)
    return bb, cm


def _diag_blocks(cross):
    gc = G // _NCL
    six = cross.reshape(_NCL, gc, H, 2, gc, P)
    same = jnp.eye(gc, dtype=bool)[None, :, None, None, :, None]
    diag = jnp.sum(jnp.where(same, six, 0.0), axis=4)
    diag = jnp.moveaxis(diag, 3, 0).reshape(2, G, H, P)
    return diag[0], diag[1]


def _mix_fwd(scv, yg, z, x, w_cp, w_glu, w_out, g1, n2g, sc2, sh2):
    S = z.shape[0]
    tm = min(512, S)
    D = D_MODEL

    def body(s_ref, yg_ref, glc0_ref, glc1_ref, gls0_ref, gls1_ref, x_ref, wcp_ref, wglu_ref, wout_ref,
             g1_ref, n2_ref, sc_ref, sh_ref, yc_ref, zz_ref, m_ref, o_ref, x2_ref, h2_ref):
        y_conv = jnp.dot(s_ref[...], wcp_ref[...], preferred_element_type=F32)
        zz = jnp.dot(yg_ref[...], wglu_ref[...], preferred_element_type=F32)
        yc_ref[...] = y_conv.astype(BF16)
        zz_ref[...] = zz.astype(BF16)
        for half, (glc_ref, gls_ref) in enumerate(((glc0_ref, gls0_ref), (glc1_ref, gls1_ref))):
            lo, hi = half * CW, (half + 1) * CW
            y_ssm = zz[:, lo:hi] * _sig(zz[:, D + lo:D + hi])
            m_ref[:, lo:hi] = (_sig(glc_ref[...].astype(F32)) * y_conv[:, lo:hi]
                               + _sig(gls_ref[...].astype(F32)) * y_ssm).astype(BF16)
        o = jnp.dot(m_ref[...], wout_ref[...], preferred_element_type=F32)
        o_ref[...] = o.astype(BF16)
        xv = x_ref[...] + g1_ref[...] * o
        x2_ref[...] = xv
        r = lax.rsqrt(jnp.mean(xv * xv, axis=-1, keepdims=True) + EPS)
        h2_ref[...] = (xv * r * (n2_ref[...] * (1.0 + sc_ref[...])) + sh_ref[...]).astype(BF16)

    zb_ = lambda j: pl.BlockSpec((tm, CW), lambda i: (i, j))
    row = lambda w: pl.BlockSpec((tm, w), lambda i: (i, 0))
    par = _full((1, D))
    bf = lambda w: jax.ShapeDtypeStruct((S, w), BF16)
    return pl.pallas_call(
        body, name="mix_fwd",
        out_shape=(bf(D), bf(2 * D), bf(D), bf(D), jax.ShapeDtypeStruct((S, D), F32), bf(D)),
        grid=(S // tm,),
        in_specs=[row(CW), row(CW), zb_(3), zb_(4), zb_(5), zb_(6), row(D), _resident((CW, D)),
                  _resident((CW, 2 * D)), _resident((D, D)), par, par, par, par],
        out_specs=(row(D), row(2 * D), row(D), row(D), row(D), row(D)),
        compiler_params=_cp(("parallel",)))(scv, yg, z, z, z, z, x, w_cp, w_glu, w_out, g1, n2g, sc2, sh2)


def _mix_bwd(do, w_out, z, zz, y_conv, after):
    S = z.shape[0]
    tm = min(512, S)
    D = D_MODEL

    def body(do_ref, w_ref, glc0_ref, glc1_ref, gls0_ref, gls1_ref, za_ref, zb_ref, yc_ref, after_ref,
             dyc_ref, dgl_ref, dzz_ref):
        dm = lax.dot_general(do_ref[...], w_ref[...], (((1,), (1,)), ((), ())), preferred_element_type=F32)
        for half, (glc_ref, gls_ref) in enumerate(((glc0_ref, gls0_ref), (glc1_ref, gls1_ref))):
            lo, hi = half * CW, (half + 1) * CW
            dm_v = dm[:, lo:hi]
            sgc = _sig(glc_ref[...].astype(F32))
            sgs = _sig(gls_ref[...].astype(F32))
            szb = _sig(zb_ref[:, lo:hi].astype(F32))
            za = za_ref[:, lo:hi].astype(F32)
            dyc_ref[:, lo:hi] = (dm_v * sgc).astype(BF16)
            dgl_ref[:, lo:hi] = (dm_v * yc_ref[:, lo:hi].astype(F32) * sgc * (1.0 - sgc)).astype(BF16)
            dys = dm_v * sgs
            dgl_ref[:, D + lo:D + hi] = (dys * (za * szb) * (1.0 - sgs)).astype(BF16)
            dzz_ref[:, lo:hi] = (dys * szb).astype(BF16)
            dzz_ref[:, D + lo:D + hi] = (dys * za * szb * (1.0 - szb)).astype(BF16)

    zb_ = lambda j: pl.BlockSpec((tm, CW), lambda i: (i, j))
    wide = lambda j: pl.BlockSpec((tm, D), lambda i: (i, j))
    return pl.pallas_call(
        body, name="mix_bwd",
        out_shape=(jax.ShapeDtypeStruct((S, D), BF16), jax.ShapeDtypeStruct((S, 2 * D), BF16),
                   jax.ShapeDtypeStruct((S, 2 * D), BF16)),
        grid=(S // tm,),
        in_specs=[wide(0), _resident((D, D)), zb_(3), zb_(4), zb_(5), zb_(6), wide(0), wide(1), wide(0),
                  pl.BlockSpec(memory_space=pl.ANY)],
        out_specs=(wide(0), pl.BlockSpec((tm, 2 * D), lambda i: (i, 0)), pl.BlockSpec((tm, 2 * D), lambda i: (i, 0))),
        compiler_params=_cp(("parallel",)))(do, w_out, z, z, z, z, zz, zz, y_conv, after)


_FC = 1408


def _ffn_in_act(h2, w_fi):
    S, D = h2.shape
    tm = min(512, S)

    def body(h_ref, w_ref, f_ref, a_ref):
        hv = h_ref[...]
        for c in range(FH // _FC):
            lo, hi = c * _FC, (c + 1) * _FC
            g = jnp.dot(hv, w_ref[:, lo:hi], preferred_element_type=F32)
            u = jnp.dot(hv, w_ref[:, FH + lo:FH + hi], preferred_element_type=F32)
            f_ref[:, lo:hi] = g.astype(BF16)
            f_ref[:, FH + lo:FH + hi] = u.astype(BF16)
            a_ref[:, lo:hi] = (g * _sig(g) * u).astype(BF16)

    return pl.pallas_call(
        body, name="ffn_in_act",
        out_shape=(jax.ShapeDtypeStruct((S, 2 * FH), BF16), jax.ShapeDtypeStruct((S, FH), BF16)),
        grid=(S // tm,),
        in_specs=[pl.BlockSpec((tm, D), lambda i: (i, 0)), _resident((D, 2 * FH))],
        out_specs=(pl.BlockSpec((tm, 2 * FH), lambda i: (i, 0)), pl.BlockSpec((tm, FH), lambda i: (i, 0))),
        compiler_params=_cp(("parallel",)))(h2, w_fi)


def _ffn_bwd(do2, w_fo, f, after):
    S, D = do2.shape
    tm = min(512, S)

    def body(d_ref, w_ref, f_ref, after_ref, df_ref):
        dv = d_ref[...]
        for c in range(FH // _FC):
            lo, hi = c * _FC, (c + 1) * _FC
            dact = lax.dot_general(dv, w_ref[lo:hi, :], (((1,), (1,)), ((), ())), preferred_element_type=F32)
            g = f_ref[:, lo:hi].astype(F32)
            u = f_ref[:, FH + lo:FH + hi].astype(F32)
            sg = _sig(g)
            df_ref[:, lo:hi] = (dact * u * (sg * (1.0 + g * (1.0 - sg)))).astype(BF16)
            df_ref[:, FH + lo:FH + hi] = (dact * g * sg).astype(BF16)

    return pl.pallas_call(
        body, name="ffn_bwd", out_shape=jax.ShapeDtypeStruct((S, 2 * FH), BF16), grid=(S // tm,),
        in_specs=[pl.BlockSpec((tm, D), lambda i: (i, 0)), _resident((FH, D)),
                  pl.BlockSpec((tm, 2 * FH), lambda i: (i, 0)), pl.BlockSpec(memory_space=pl.ANY)],
        out_specs=pl.BlockSpec((tm, 2 * FH), lambda i: (i, 0)),
        compiler_params=_cp(("parallel",)))(do2, w_fo, f, after)


def _ffn_out_final(x2, act, w_fo, g2, fg, tgt):
    S, D = x2.shape
    tm = min(512, S)

    def body(x2_ref, a_ref, w_ref, g2_ref, fg_ref, t_ref, dx3_ref, do2_ref, ls_ref, dfg_ref, dg2_ref):
        i = pl.program_id(0)

        @pl.when(i == 0)
        def _():
            ls_ref[...] = jnp.zeros_like(ls_ref)
            dfg_ref[...] = jnp.zeros_like(dfg_ref)
            dg2_ref[...] = jnp.zeros_like(dg2_ref)

        o2 = jnp.dot(a_ref[...], w_ref[...], preferred_element_type=F32)
        x3 = x2_ref[...] + g2_ref[...] * o2
        r = lax.rsqrt(jnp.mean(x3 * x3, axis=-1, keepdims=True) + EPS)
        xn = x3 * r
        err = xn * fg_ref[...] - t_ref[...]
        dy = err * (1.0 / D)
        dxn = dy * fg_ref[...]
        dx3 = r * (dxn - xn * jnp.mean(dxn * xn, axis=-1, keepdims=True))
        dx3_ref[...] = dx3
        do2_ref[...] = (dx3 * g2_ref[...]).astype(BF16)
        e2 = _colsum8(err * err)
        lanes = e2[:, 0:128]
        for q in range(1, D // 128):
            lanes = lanes + e2[:, q * 128:(q + 1) * 128]
        ls_ref[...] += lanes * (0.5 / D)
        dfg_ref[...] += _colsum8(dy * xn)
        dg2_ref[...] += _colsum8(dx3 * o2)

    row = pl.BlockSpec((tm, D), lambda i: (i, 0))
    par = _full((1, D))
    return pl.pallas_call(
        body, name="final_loss",
        out_shape=(jax.ShapeDtypeStruct((S, D), F32), jax.ShapeDtypeStruct((S, D), BF16),
                   jax.ShapeDtypeStruct((8, 128), F32), jax.ShapeDtypeStruct((8, D), F32),
                   jax.ShapeDtypeStruct((8, D), F32)),
        grid=(S // tm,), in_specs=[row, pl.BlockSpec((tm, FH), lambda i: (i, 0)), _resident((FH, D)), par, par, row],
        out_specs=(row, row, _full((8, 128)), _full((8, D)), _full((8, D))),
        compiler_params=_cp(("arbitrary",)))(x2, act, w_fo, g2, fg, tgt)


def _normmod_bwd(dsrc, w, xin, dres, g, sc, gate, o, after, name):
    S, D = xin.shape
    parts = list(dsrc) if isinstance(dsrc, (list, tuple)) else [dsrc]
    widths = [p.shape[1] for p in parts]
    K = sum(widths)
    tm = min(512, S)
    npart = len(parts)

    def body(*refs):
        ds_refs = refs[:npart]
        w_ref, x_ref, dr_ref, g_ref, sc_ref, gate_ref, o_ref, after_ref = refs[npart:npart + 8]
        dx_ref, do_ref, dsh_ref, dsc_ref, dg_ref, dgate_ref = refs[npart + 8:]
        i = pl.program_id(0)

        @pl.when(i == 0)
        def _():
            dsh_ref[...] = jnp.zeros_like(dsh_ref)
            dsc_ref[...] = jnp.zeros_like(dsc_ref)
            dg_ref[...] = jnp.zeros_like(dg_ref)
            dgate_ref[...] = jnp.zeros_like(dgate_ref)

        gv = g_ref[...]
        scale = 1.0 + sc_ref[...]
        xv = x_ref[...]
        r = lax.rsqrt(jnp.mean(xv * xv, axis=-1, keepdims=True) + EPS)
        xn = xv * r
        dh_v, col = None, 0
        for ds_ref, wd in zip(ds_refs, widths):
            t = lax.dot_general(ds_ref[...], w_ref[:, col:col + wd], (((1,), (1,)), ((), ())),
                                preferred_element_type=F32)
            dh_v = t if dh_v is None else dh_v + t
            col += wd
        dxn = dh_v * (gv * scale)
        dx = dr_ref[...] + r * (dxn - xn * jnp.mean(dxn * xn, axis=-1, keepdims=True))
        dx_ref[...] = dx
        do_ref[...] = (dx * gate_ref[...]).astype(BF16)
        hx = dh_v * xn
        dsh_ref[...] += _colsum8(dh_v)
        dsc_ref[...] += _colsum8(hx) * gv
        dg_ref[...] += _colsum8(hx) * scale
        dgate_ref[...] += _colsum8(dx * o_ref[...])

    row = pl.BlockSpec((tm, D), lambda i: (i, 0))
    par = _full((1, D))
    acc = jax.ShapeDtypeStruct((8, D), F32)
    return pl.pallas_call(
        body, name=name,
        out_shape=(jax.ShapeDtypeStruct((S, D), F32), jax.ShapeDtypeStruct((S, D), BF16), acc, acc, acc, acc),
        grid=(S // tm,),
        in_specs=[pl.BlockSpec((tm, wd), lambda i: (i, 0)) for wd in widths]
        + [_resident((D, K)), row, row, par, par, par, row, pl.BlockSpec(memory_space=pl.ANY)],
        out_specs=(row, row, _full((8, D)), _full((8, D)), _full((8, D)), _full((8, D))),
        compiler_params=_cp(("arbitrary",)))(*parts, w, xin, dres, g, sc, gate, o, after)


def _me():
    return lax.axis_index("x"), lax.axis_index("y"), lax.axis_index("c")


def _allgather8(v, name, after=()):
    several = isinstance(v, (list, tuple))
    vs = list(v) if several else [v]
    nv = len(vs)
    after = list(after)

    def body(*refs):
        v_refs = refs[:nv]
        out_refs = refs[nv + len(after):2 * nv + len(after)]
        send_sems, recv_sems, local_sems = refs[2 * nv + len(after):]
        x, y, c = _me()
        local = [pltpu.make_async_copy(v_refs[a], out_refs[a].at[4 * x + 2 * y + c], local_sems.at[a])
                 for a in range(nv)]
        for cp in local:
            cp.start()
        copies = []
        for a in range(nv):
            for k in range(1, N_DEV):
                fx, fy, fc = (k >> 2) & 1, (k >> 1) & 1, k & 1
                peer = (x ^ fx, y ^ fy, c ^ fc)
                copies.append(pltpu.make_async_remote_copy(
                    src_ref=v_refs[a], dst_ref=out_refs[a].at[4 * x + 2 * y + c],
                    send_sem=send_sems.at[a, k - 1], recv_sem=recv_sems.at[a, k - 1],
                    device_id=peer, device_id_type=MESH))
        for cp in copies:
            cp.start()
        for a in range(nv):
            for k in range(1, N_DEV):
                fx, fy, fc = (k >> 2) & 1, (k >> 1) & 1, k & 1
                src_slot = 4 * (x ^ fx) + 2 * (y ^ fy) + (c ^ fc)
                pltpu.make_async_remote_copy(
                    src_ref=v_refs[a], dst_ref=out_refs[a].at[src_slot],
                    send_sem=send_sems.at[a, k - 1], recv_sem=recv_sems.at[a, k - 1],
                    device_id=(x ^ fx, y ^ fy, c ^ fc), device_id_type=MESH).wait_recv()
        for cp in copies:
            cp.wait_send()
        for cp in local:
            cp.wait()

    vm = pl.BlockSpec(memory_space=pltpu.VMEM)
    outs = pl.pallas_call(
        body, name=name, out_shape=tuple(jax.ShapeDtypeStruct((N_DEV,) + a.shape, a.dtype) for a in vs),
        in_specs=[vm] * nv + [pl.BlockSpec(memory_space=pl.ANY)] * len(after),
        out_specs=tuple([vm] * nv),
        scratch_shapes=[pltpu.SemaphoreType.DMA((nv, N_DEV - 1)), pltpu.SemaphoreType.DMA((nv, N_DEV - 1)),
                        pltpu.SemaphoreType.DMA((nv,))],
        compiler_params=pltpu.CompilerParams(vmem_limit_bytes=VMEM_LIMIT))(*vs, *after)
    return list(outs) if several else outs[0]


_HBM = pl.BlockSpec(memory_space=pltpu.HBM)
_SEM = pl.BlockSpec(memory_space=pltpu.SEMAPHORE)
_EFFECT = pltpu.SideEffectType.DATAFLOW_SIDE_EFFECTING
_N_PEER = N_CHIP - 1


def _chip_part(ref, axis, n, chip):
    start = pl.multiple_of(chip * n, 8)
    return ref.at[pl.ds(start, n), :] if axis == 0 else ref.at[:, pl.ds(start, n)]


def _gather_copy(k, src_ref, land_ref, send_sems, recv_sems, axis, arriving):
    x, y, c = _me()
    px, py = x ^ ((k >> 1) & 1), y ^ (k & 1)
    chip = 2 * px + py if arriving else 2 * x + y
    return pltpu.make_async_remote_copy(
        src_ref=src_ref, dst_ref=_chip_part(land_ref, axis, src_ref.shape[axis], chip),
        send_sem=send_sems.at[k - 1], recv_sem=recv_sems.at[k - 1], device_id=(px, py, c), device_id_type=MESH)


def _scatter_copy(k, grad_ref, land_ref, send_sems, recv_sems, axis):
    x, y, c = _me()
    px, py = x ^ ((k >> 1) & 1), y ^ (k & 1)
    return pltpu.make_async_remote_copy(
        src_ref=_chip_part(grad_ref, axis, grad_ref.shape[axis] // N_CHIP, 2 * px + py), dst_ref=land_ref.at[k],
        send_sem=send_sems.at[k - 1], recv_sem=recv_sems.at[k - 1], device_id=(px, py, c), device_id_type=MESH)


def _scatter_own(grad_ref, land_ref, send_sems, axis):
    x, y, _ = _me()
    return pltpu.make_async_copy(_chip_part(grad_ref, axis, grad_ref.shape[axis] // N_CHIP, 2 * x + y),
                                 land_ref.at[0], send_sems.at[_N_PEER])


def _own_copy(src_ref, land_ref, sends, axis):
    x, y, _ = _me()
    return pltpu.make_async_copy(src_ref, _chip_part(land_ref, axis, src_ref.shape[axis], 2 * x + y),
                                 sends.at[_N_PEER])


def _gather_start(shards, axes, after, name):
    nw = len(shards)
    lands = []
    for s, ax in zip(shards, axes):
        shp = list(s.shape)
        shp[ax] *= N_CHIP
        lands.append(lax.empty(tuple(shp), s.dtype))

    def body(*refs):
        srcs, zones = refs[:nw], refs[nw:2 * nw]
        sends, recvs = refs[2 * nw + 1:3 * nw + 1], refs[3 * nw + 1:4 * nw + 1]
        token = refs[-1]
        for w in range(nw):
            for k in range(1, N_CHIP):
                _gather_copy(k, srcs[w], zones[w], sends[w], recvs[w], axes[w], False).start()
        for w in range(nw):
            _own_copy(srcs[w], zones[w], sends[w], axes[w]).start()
        token[...] = jnp.zeros_like(token)

    outs = pl.pallas_call(
        body, name=name,
        out_shape=tuple([pltpu.SemaphoreType.DMA((_N_PEER + 1,))] * nw + [pltpu.SemaphoreType.DMA((_N_PEER,))] * nw
                        + [pltpu.HBM(a.shape, a.dtype) for a in list(shards) + list(lands)]
                        + [jax.ShapeDtypeStruct((8, 128), F32)]),
        in_specs=[_HBM] * (2 * nw) + [pl.BlockSpec(memory_space=pl.ANY)],
        out_specs=tuple([_SEM] * (2 * nw) + [_HBM] * (2 * nw) + [pl.BlockSpec(memory_space=pltpu.VMEM)]),
        input_output_aliases={i: 2 * nw + i for i in range(2 * nw)},
        compiler_params=pltpu.CompilerParams(has_side_effects=_EFFECT),
    )(*([pltpu.with_memory_space_constraint(a, pltpu.HBM) for a in list(shards) + list(lands)] + [after]))
    per_weight = [(outs[w], outs[nw + w], outs[2 * nw + w], outs[3 * nw + w]) for w in range(nw)]
    return per_weight, outs[-1]


def _gather_wait(state, axis, after, name):
    send_sems, recv_sems, shard, land = state

    after = list(after) if isinstance(after, (list, tuple)) else [after]

    def body(src_ref, land_ref, sends, recvs, *rest):
        for k in range(1, N_CHIP):
            _gather_copy(k, src_ref, land_ref, sends, recvs, axis, False).wait_send()
            _gather_copy(k, src_ref, land_ref, sends, recvs, axis, True).wait_recv()
        _own_copy(src_ref, land_ref, sends, axis).wait()

    return pl.pallas_call(
        body, name=name, out_shape=(pltpu.HBM(shard.shape, shard.dtype), pltpu.HBM(land.shape, land.dtype)),
        in_specs=[_HBM, _HBM, _SEM, _SEM] + [pl.BlockSpec(memory_space=pl.ANY)] * len(after), out_specs=(_HBM, _HBM),
        input_output_aliases={0: 0, 1: 1},
        compiler_params=pltpu.CompilerParams(has_side_effects=_EFFECT),
    )(shard, land, send_sems, recv_sems, *after)[1]


def _half_rows(ref, c):
    k2 = ref.shape[0] // 2
    return pl.ds(pl.multiple_of(c * k2, 8), k2)


def _half_copy(k, shard_ref, land_ref, send_sems, recv_sems, arriving):
    x, y, c = _me()
    px, py = x ^ ((k >> 1) & 1), y ^ (k & 1)
    n = shard_ref.shape[1]
    chip = 2 * px + py if arriving else 2 * x + y
    return pltpu.make_async_remote_copy(
        src_ref=shard_ref.at[_half_rows(shard_ref, c), :],
        dst_ref=land_ref.at[_half_rows(land_ref, c), pl.ds(pl.multiple_of(chip * n, 128), n)],
        send_sem=send_sems.at[k - 1], recv_sem=recv_sems.at[k - 1], device_id=(px, py, c), device_id_type=MESH)


def _half_own(shard_ref, land_ref, send_sems):
    x, y, c = _me()
    n = shard_ref.shape[1]
    return pltpu.make_async_copy(
        shard_ref.at[_half_rows(shard_ref, c), :],
        land_ref.at[_half_rows(land_ref, c), pl.ds(pl.multiple_of((2 * x + y) * n, 128), n)], send_sems.at[_N_PEER])


def _half_gather_start(shard, after, name):
    K, n = shard.shape
    land = lax.empty((K, N_CHIP * n), shard.dtype)

    def body(shard_ref, land_ref, after_ref, sends, recvs, shard_thru, land_thru, token):
        for k in range(1, N_CHIP):
            _half_copy(k, shard_ref, land_ref, sends, recvs, False).start()
        _half_own(shard_ref, land_ref, sends).start()
        token[...] = jnp.zeros_like(token)

    outs = pl.pallas_call(
        body, name=name,
        out_shape=(pltpu.SemaphoreType.DMA((_N_PEER + 1,)), pltpu.SemaphoreType.DMA((_N_PEER,)),
                   pltpu.HBM(shard.shape, shard.dtype), pltpu.HBM(land.shape, land.dtype),
                   jax.ShapeDtypeStruct((8, 128), F32)),
        in_specs=[_HBM, _HBM, pl.BlockSpec(memory_space=pl.ANY)],
        out_specs=(_SEM, _SEM, _HBM, _HBM, pl.BlockSpec(memory_space=pltpu.VMEM)),
        input_output_aliases={0: 2, 1: 3},
        compiler_params=pltpu.CompilerParams(has_side_effects=_EFFECT),
    )(pltpu.with_memory_space_constraint(shard, pltpu.HBM), pltpu.with_memory_space_constraint(land, pltpu.HBM), after)
    return outs[:4], outs[4]


def _half_gather_wait(state, after, name):
    send_sems, recv_sems, shard, land = state
    after = list(after)

    def body(shard_ref, land_ref, sends, recvs, *rest):
        for k in range(1, N_CHIP):
            _half_copy(k, shard_ref, land_ref, sends, recvs, False).wait_send()
            _half_copy(k, shard_ref, land_ref, sends, recvs, True).wait_recv()
        _half_own(shard_ref, land_ref, sends).wait()

    return pl.pallas_call(
        body, name=name, out_shape=(pltpu.HBM(shard.shape, shard.dtype), pltpu.HBM(land.shape, land.dtype)),
        in_specs=[_HBM, _HBM, _SEM, _SEM] + [pl.BlockSpec(memory_space=pl.ANY)] * len(after), out_specs=(_HBM, _HBM),
        input_output_aliases={0: 0, 1: 1},
        compiler_params=pltpu.CompilerParams(has_side_effects=_EFFECT),
    )(shard, land, send_sems, recv_sems, *after)[1]


def _half_swap_copy(land_ref, send_sem, recv_sem, arriving):
    x, y, c = _me()
    rows = _half_rows(land_ref, 1 - c if arriving else c)
    return pltpu.make_async_remote_copy(src_ref=land_ref.at[rows, :], dst_ref=land_ref.at[rows, :], send_sem=send_sem,
                                        recv_sem=recv_sem, device_id=(x, y, 1 - c), device_id_type=MESH)


def _half_swap_start(land, name):
    def body(land_ref, send, recv, land_thru, token):
        _half_swap_copy(land_ref, send.at[0], recv.at[0], False).start()
        token[...] = jnp.zeros_like(token)

    sem = pltpu.SemaphoreType.DMA((1,))
    outs = pl.pallas_call(
        body, name=name,
        out_shape=(sem, sem, pltpu.HBM(land.shape, land.dtype), jax.ShapeDtypeStruct((8, 128), F32)),
        in_specs=[_HBM], out_specs=(_SEM, _SEM, _HBM, pl.BlockSpec(memory_space=pltpu.VMEM)),
        input_output_aliases={0: 2},
        compiler_params=pltpu.CompilerParams(has_side_effects=_EFFECT),
    )(pltpu.with_memory_space_constraint(land, pltpu.HBM))
    return outs[:3], outs[3]


def _half_swap_wait(state, after, name):
    send, recv, land = state

    def body(land_ref, send_ref, recv_ref, after_ref, got_ref):
        _half_swap_copy(land_ref, send_ref.at[0], recv_ref.at[0], False).wait_send()
        _half_swap_copy(land_ref, send_ref.at[0], recv_ref.at[0], True).wait_recv()

    return pl.pallas_call(
        body, name=name, out_shape=pltpu.HBM(land.shape, land.dtype),
        in_specs=[_HBM, _SEM, _SEM, pl.BlockSpec(memory_space=pl.ANY)], out_specs=_HBM,
        input_output_aliases={0: 0},
        compiler_params=pltpu.CompilerParams(has_side_effects=_EFFECT),
    )(land, send, recv, after)


def _all8_copy(k, v_ref, land_ref, send_sems, recv_sems, arriving):
    x, y, c = _me()
    px, py, pc = x ^ ((k >> 2) & 1), y ^ ((k >> 1) & 1), c ^ (k & 1)
    slot = 4 * px + 2 * py + pc if arriving else 4 * x + 2 * y + c
    return pltpu.make_async_remote_copy(
        src_ref=v_ref, dst_ref=land_ref.at[slot], send_sem=send_sems.at[k - 1], recv_sem=recv_sems.at[k - 1],
        device_id=(px, py, pc), device_id_type=MESH)


def _all8_own(v_ref, land_ref, send_sems):
    x, y, c = _me()
    return pltpu.make_async_copy(v_ref, land_ref.at[4 * x + 2 * y + c], send_sems.at[N_DEV - 1])


def _all8_start(v, name):
    land = lax.empty((N_DEV,) + v.shape, v.dtype)

    def body(v_ref, land_ref, sends, recvs, v_thru, land_thru, token):
        for k in range(1, N_DEV):
            _all8_copy(k, v_ref, land_ref, sends, recvs, False).start()
        _all8_own(v_ref, land_ref, sends).start()
        token[...] = jnp.zeros_like(token)

    outs = pl.pallas_call(
        body, name=name,
        out_shape=(pltpu.SemaphoreType.DMA((N_DEV,)), pltpu.SemaphoreType.DMA((N_DEV - 1,)),
                   pltpu.HBM(v.shape, v.dtype), pltpu.HBM(land.shape, land.dtype),
                   jax.ShapeDtypeStruct((8, 128), F32)),
        in_specs=[_HBM, _HBM], out_specs=(_SEM, _SEM, _HBM, _HBM, pl.BlockSpec(memory_space=pltpu.VMEM)),
        input_output_aliases={0: 2, 1: 3},
        compiler_params=pltpu.CompilerParams(has_side_effects=_EFFECT),
    )(pltpu.with_memory_space_constraint(v, pltpu.HBM), pltpu.with_memory_space_constraint(land, pltpu.HBM))
    return outs[:4], outs[4]


def _all8_wait(state, after, name):
    send_sems, recv_sems, v, land = state

    def body(v_ref, land_ref, sends, recvs, after_ref, v_dead, got_ref):
        for k in range(1, N_DEV):
            _all8_copy(k, v_ref, land_ref, sends, recvs, False).wait_send()
            _all8_copy(k, v_ref, land_ref, sends, recvs, True).wait_recv()
        _all8_own(v_ref, land_ref, sends).wait()

    return pl.pallas_call(
        body, name=name, out_shape=(pltpu.HBM(v.shape, v.dtype), pltpu.HBM(land.shape, land.dtype)),
        in_specs=[_HBM, _HBM, _SEM, _SEM, pl.BlockSpec(memory_space=pl.ANY)], out_specs=(_HBM, _HBM),
        input_output_aliases={0: 0, 1: 1},
        compiler_params=pltpu.CompilerParams(has_side_effects=_EFFECT),
    )(v, land, send_sems, recv_sems, after)[1]


def _swap_copy(w, src_ref, land_ref, send_sems, recv_sems):
    x, y, c = _me()
    return pltpu.make_async_remote_copy(src_ref=src_ref, dst_ref=land_ref, send_sem=send_sems.at[w],
                                        recv_sem=recv_sems.at[w], device_id=(x, y, 1 - c), device_id_type=MESH)


def _swap_start(arrs, after, name):
    nw = len(arrs)
    lands = [lax.empty(a.shape, a.dtype) for a in arrs]

    def body(*refs):
        srcs, zones = refs[:nw], refs[nw:2 * nw]
        sends, recvs = refs[2 * nw + 1], refs[2 * nw + 2]
        for w in range(nw):
            _swap_copy(w, srcs[w], zones[w], sends, recvs).start()
        refs[-1][...] = jnp.zeros_like(refs[-1])

    sem = pltpu.SemaphoreType.DMA((nw,))
    outs = pl.pallas_call(
        body, name=name,
        out_shape=tuple([sem, sem] + [pltpu.HBM(a.shape, a.dtype) for a in list(arrs) + lands]
                        + [jax.ShapeDtypeStruct((8, 128), F32)]),
        in_specs=[_HBM] * (2 * nw) + [pl.BlockSpec(memory_space=pl.ANY)],
        out_specs=tuple([_SEM, _SEM] + [_HBM] * (2 * nw) + [pl.BlockSpec(memory_space=pltpu.VMEM)]),
        input_output_aliases={i: 2 + i for i in range(2 * nw)},
        compiler_params=pltpu.CompilerParams(has_side_effects=_EFFECT),
    )(*([pltpu.with_memory_space_constraint(a, pltpu.HBM) for a in list(arrs) + lands] + [after]))
    return (outs[0], outs[1], outs[2:2 + nw], outs[2 + nw:2 + 2 * nw]), outs[-1]


def _swap_wait(state, after, name):
    send_sems, recv_sems, arrs, lands = state
    nw = len(arrs)

    def body(*refs):
        srcs, zones = refs[:nw], refs[nw:2 * nw]
        sends, recvs = refs[2 * nw], refs[2 * nw + 1]
        for w in range(nw):
            cp = _swap_copy(w, srcs[w], zones[w], sends, recvs)
            cp.wait_send()
            cp.wait_recv()

    outs = pl.pallas_call(
        body, name=name, out_shape=tuple(pltpu.HBM(a.shape, a.dtype) for a in list(arrs) + list(lands)),
        in_specs=[_HBM] * (2 * nw) + [_SEM, _SEM, pl.BlockSpec(memory_space=pl.ANY)],
        out_specs=tuple([_HBM] * (2 * nw)),
        input_output_aliases={i: i for i in range(2 * nw)},
        compiler_params=pltpu.CompilerParams(has_side_effects=_EFFECT),
    )(*arrs, *lands, send_sems, recv_sems, after)
    return list(outs[:nw]), list(outs[nw:])


def _scatter_start(grad, axis, name):
    shp = list(grad.shape)
    shp[axis] //= N_CHIP
    land = lax.empty((N_CHIP,) + tuple(shp), grad.dtype)

    def body(grad_ref, land_ref, sends, recvs, grad_thru, land_thru, token):
        for k in range(1, N_CHIP):
            _scatter_copy(k, grad_ref, land_ref, sends, recvs, axis).start()
        _scatter_own(grad_ref, land_ref, sends, axis).start()
        token[...] = jnp.zeros_like(token)

    outs = pl.pallas_call(
        body, name=name,
        out_shape=(pltpu.SemaphoreType.DMA((_N_PEER + 1,)), pltpu.SemaphoreType.DMA((_N_PEER,)),
                   pltpu.HBM(grad.shape, grad.dtype), pltpu.HBM(land.shape, land.dtype),
                   jax.ShapeDtypeStruct((8, 128), F32)),
        in_specs=[_HBM, _HBM], out_specs=(_SEM, _SEM, _HBM, _HBM, pl.BlockSpec(memory_space=pltpu.VMEM)),
        input_output_aliases={0: 2, 1: 3},
        compiler_params=pltpu.CompilerParams(has_side_effects=_EFFECT),
    )(pltpu.with_memory_space_constraint(grad, pltpu.HBM), pltpu.with_memory_space_constraint(land, pltpu.HBM))
    return outs[:4], outs[4]


def _scatter_wait(state, axis, after, name):
    send_sems, recv_sems, grad, land = state

    def body(grad_ref, land_ref, sends, recvs, after_ref, grad_dead, got_ref):
        for k in range(1, N_CHIP):
            cp = _scatter_copy(k, grad_ref, land_ref, sends, recvs, axis)
            cp.wait_send()
            cp.wait_recv()
        _scatter_own(grad_ref, land_ref, sends, axis).wait()

    return pl.pallas_call(
        body, name=name, out_shape=(pltpu.HBM(grad.shape, grad.dtype), pltpu.HBM(land.shape, land.dtype)),
        in_specs=[_HBM, _HBM, _SEM, _SEM, pl.BlockSpec(memory_space=pl.ANY)], out_specs=(_HBM, _HBM),
        input_output_aliases={0: 0, 1: 1},
        compiler_params=pltpu.CompilerParams(has_side_effects=_EFFECT),
    )(grad, land, send_sems, recv_sems, after)[1]


_C1 = 1.0 - B1 ** STEP
_C2 = 1.0 - B2 ** STEP


def _adam_math(w, g, m, v):
    m = B1 * m + (1.0 - B1) * g
    v = B2 * v + (1.0 - B2) * (g * g)
    delta = -LR * ((m / _C1) / (jnp.sqrt(v / _C2) + AEPS) + WD * w)
    return delta, m, v


def _adamw(w, m, v, groups, name):
    R, C = w.shape
    tr = R if R <= 256 else (128 if R % 128 == 0 else 176)
    assert R % tr == 0, (name, R)
    gparts = [p for grp in groups for p in grp]
    sizes = [len(grp) for grp in groups]
    ng = len(gparts)

    def body(*refs):
        w_ref, m_ref, v_ref = refs[:3]
        g_refs = list(refs[3:3 + ng])
        g_out, d_out, m_out, v_out = refs[3 + ng:]
        g = None
        for size in sizes:
            s = None
            for r in [g_refs.pop(0) for _ in range(size)]:
                terms = [r[q] for q in range(r.shape[0])] if len(r.shape) == 3 else [r[...]]
                for t in terms:
                    s = t.astype(F32) if s is None else s + t.astype(F32)
            g = s if g is None else g + s
        delta, mn, vn = _adam_math(w_ref[...], g, m_ref[...], v_ref[...])
        g_out[...] = g
        d_out[...] = delta
        m_out[...] = mn
        v_out[...] = vn

    blk = pl.BlockSpec((tr, C), lambda i: (i, 0))
    g_specs = [blk if p.ndim == 2 else pl.BlockSpec((p.shape[0], tr, C), lambda i: (0, i, 0)) for p in gparts]
    sds = jax.ShapeDtypeStruct((R, C), F32)
    return pl.pallas_call(
        body, name=name, out_shape=(sds, sds, sds, sds), grid=(R // tr,),
        in_specs=[blk, blk, blk] + g_specs, out_specs=(blk, blk, blk, blk),
        compiler_params=_cp(("parallel",)))(w, m, v, *gparts)


def _adamw_small(stack, names, wts, mom, var, sum_only, name):
    items, row = [], 0
    for n in names:
        shape = (KW, CW) if n == "conv_w" else wts[n].shape
        size = int(np.prod(shape))
        vec = len(shape) == 2 and shape[0] == 1 and n not in sum_only
        view = shape if vec else (-(-size // _PACK_COLS), _PACK_COLS)
        items.append((n, row, size, vec, view))
        row += _pack_rows(shape)
    upd = [it for it in items if it[0] not in sum_only]
    operands = [stack]
    for n, _, _, _, view in upd:
        operands += [d[n].reshape(view) for d in (wts, mom, var)]

    def grad(stack_ref, r0, nrows, ncols):
        g = stack_ref[0, r0:r0 + nrows, 0:ncols]
        for q in range(1, N_DEV):
            g = g + stack_ref[q, r0:r0 + nrows, 0:ncols]
        return g

    def body(*refs):
        stack_ref, ins, outs = refs[0], refs[1:1 + 3 * len(upd)], refs[1 + 3 * len(upd):]
        o = 0
        for idx, (n, r0, size, vec, view) in enumerate(upd):
            w_ref, m_ref, v_ref = ins[3 * idx:3 * idx + 3]
            g_out, d_out, m_out, v_out = outs[o:o + 4]
            o += 4
            if vec:
                pieces = [(j, j * _PACK_COLS, min((j + 1) * _PACK_COLS, size)) for j in range(-(-size // _PACK_COLS))]
            else:
                pieces = [(None, 0, _PACK_COLS)]
            for j, lo, hi in pieces:
                if vec:
                    g = grad(stack_ref, r0 + j, 1, hi - lo)
                    sl = (slice(None), slice(lo, hi))
                else:
                    g = grad(stack_ref, r0, view[0], _PACK_COLS)
                    sl = (slice(None), slice(None))
                delta, mn, vn = _adam_math(w_ref[sl], g, m_ref[sl], v_ref[sl])
                g_out[sl] = g
                d_out[sl] = delta
                m_out[sl] = mn
                v_out[sl] = vn
        for n, r0, size, vec, view in items:
            if n in sum_only:
                outs[o][...] = grad(stack_ref, r0, view[0], _PACK_COLS)
                o += 1

    out_shape = []
    for n, _, _, _, view in upd:
        out_shape += [jax.ShapeDtypeStruct(view, F32)] * 4
    out_shape += [jax.ShapeDtypeStruct(view, F32) for n, _, _, _, view in items if n in sum_only]
    vm = pl.BlockSpec(memory_space=pltpu.VMEM)
    res = pl.pallas_call(
        body, name=name, out_shape=tuple(out_shape), in_specs=[vm] * len(operands),
        out_specs=tuple([vm] * len(out_shape)),
        compiler_params=pltpu.CompilerParams(vmem_limit_bytes=VMEM_LIMIT))(*operands)
    updated = {n: tuple(r.reshape(wts[n].shape) for r in res[4 * i:4 * i + 4]) for i, (n, *_) in enumerate(upd)}
    sums = dict(zip([it[0] for it in items if it[0] in sum_only], res[4 * len(upd):]))
    return updated, sums


def _adamw_native(tensors, name):
    nt = len(tensors)

    def body(*refs):
        ins, outs = refs[:4 * nt], refs[4 * nt:]
        for t in range(nt):
            w_ref, m_ref, v_ref, g_ref = ins[4 * t:4 * t + 4]
            g = g_ref[...]
            delta, mn, vn = _adam_math(w_ref[...], g, m_ref[...], v_ref[...])
            outs[4 * t][...] = g
            outs[4 * t + 1][...] = delta
            outs[4 * t + 2][...] = mn
            outs[4 * t + 3][...] = vn

    vm = pl.BlockSpec(memory_space=pltpu.VMEM)
    flat = [a for tup in tensors for a in tup]
    res = pl.pallas_call(
        body, name=name, out_shape=tuple(jax.ShapeDtypeStruct(tup[0].shape, F32) for tup in tensors for _ in range(4)),
        in_specs=[vm] * len(flat), out_specs=tuple([vm] * (4 * nt)),
        compiler_params=pltpu.CompilerParams(vmem_limit_bytes=VMEM_LIMIT))(*flat)
    return [tuple(res[4 * t:4 * t + 4]) for t in range(nt)]


def _mod_shard(c_all, w_ada, b_ada_cols):
    n = w_ada.shape[1]
    tn = 512

    def body(c_ref, w_ref, b_ref, o_ref):
        cv = c_ref[...]
        ca = (cv * _sig(cv)).astype(BF16)
        o_ref[...] = jnp.dot(ca, w_ref[...].astype(BF16), preferred_element_type=F32) + b_ref[...]

    return pl.pallas_call(
        body, name="mod_shard", out_shape=jax.ShapeDtypeStruct((N_DEV, n), F32), grid=(n // tn,),
        in_specs=[_full((N_DEV, D_MODEL)), pl.BlockSpec((D_MODEL, tn), lambda j: (0, j)),
                  pl.BlockSpec((1, tn), lambda j: (0, j))],
        out_specs=pl.BlockSpec((N_DEV, tn), lambda j: (0, j)),
        compiler_params=_cp(("parallel",)))(c_all, w_ada, b_ada_cols)


def _ada_grad(c_all, dmod_cols, after):
    n = dmod_cols.shape[1]
    tn = 512

    def body(c_ref, d_ref, after_ref, o_ref):
        cv = c_ref[...]
        ca = cv * _sig(cv)
        o_ref[...] = lax.dot_general(ca, d_ref[...], (((0,), (0,)), ((), ())),
                                     preferred_element_type=F32, precision=lax.Precision.HIGHEST)

    return pl.pallas_call(
        body, name="ada_grad", out_shape=jax.ShapeDtypeStruct((D_MODEL, n), F32), grid=(n // tn,),
        in_specs=[_full((N_DEV, D_MODEL)), pl.BlockSpec((N_DEV, tn), lambda j: (0, j)),
                  pl.BlockSpec(memory_space=pl.ANY)],
        out_specs=pl.BlockSpec((D_MODEL, tn), lambda j: (0, j)),
        compiler_params=_cp(("parallel",)))(c_all, dmod_cols, after)


def _ssm_tables(W):
    e_re, e_im, bb_re, bb_im = _ssm_prep(W["ssm_a_re"], W["ssm_a_im"], W["ssm_b_re"], W["ssm_b_im"], W["ssm_log_dt"])
    bb, cm = _block_diag_mats(bb_re, bb_im, W["ssm_c_re"], W["ssm_c_im"])
    bb16, cm16 = bb.astype(BF16), cm.astype(BF16)
    return (bb16, cm16, jnp.swapaxes(bb16, 1, 2), jnp.swapaxes(cm16, 1, 2),
            _scan_tables(e_re, e_im, False), _scan_tables(e_re, e_im, True))


def _device_step(x, mod, W, tables, tgt, getw, put, early):
    sh1, sc1, g1, sh2, sc2, g2 = [mod[:, i * D_MODEL:(i + 1) * D_MODEL] for i in range(6)]
    bb16, cm16, bbt16, cmt16, tab_f, tab_b = tables

    w_in = getw("w_in", [mod, *tables])
    h1, z = _in_proj(x, W["norm1_g"], sc1, sh1, w_in)
    yc, scv = _conv_fwd(z, W["conv_w"], W["conv_b"], W["conv_ln_g"], W["conv_ln_b"])
    xs, ys, yg = _ssm_fwd(z, bb16, cm16, W["ssm_d"], tab_f)
    w_cp, w_glu, w_out = getw("conv_proj", scv), getw("ssm_glu", yg), getw("w_out", yg)
    y_conv, zz, merged, o, x2, h2 = _mix_fwd(scv, yg, z, x, w_cp, w_glu, w_out, g1, W["norm2_g"], sc2, sh2)
    w_fi = getw("w_ffn_in", h2)
    f, act = _ffn_in_act(h2, w_fi)
    w_fo = getw("w_ffn_out", act)
    dx3, do2, loss8, dfg8, dg2_8 = _ffn_out_final(x2, act, w_fo, g2, W["final_g"], tgt)

    sm = {}
    tok = put("w_ffn_out", _matmul(act, do2, "tn", 1408, 1024, 2048, BF16, "mm_g_ffn_out"))
    df = _ffn_bwd(do2, w_fo, f, tok)
    tok = put("w_ffn_in", _matmul(h2, df, "tn", 1024, 1408, 2048, BF16, "mm_g_ffn_in"))
    dx2, do, dsh2, dsc2, dn2, dg1_8 = _normmod_bwd(df, w_fi, x2, dx3, W["norm2_g"], sc2, g1, o, tok, "d_h2_normmod2_bwd")
    tok = put("w_out", _matmul(merged, do, "tn", 1024, 1024, 4096, BF16, "mm_g_w_out"))
    dyconv, dgl, dzz = _mix_bwd(do, w_out, z, zz, y_conv, tok)
    tok = put("ssm_glu", _matmul(yg, dzz, "tn", 512, 1024, 4096, BF16, "mm_g_ssm_glu"))
    tok = put("conv_proj", _matmul(scv, dyconv, "tn", 512, 1024, 4096, BF16, "mm_g_conv_proj", after=tok))
    du, de16, dd8, dc_full, dbb_full = _ssm_bwd(dzz, w_glu, ys, z, xs, cmt16, bbt16, W["ssm_d"], tab_b, tok)
    dyc, dlg8, dlb8, dcb8 = _conv_bwd_ln(dyconv, w_cp, yc, W["conv_ln_g"], W["conv_ln_b"])
    dz_conv, dcw = _conv_bwd(dyc, z, W["conv_w"])

    s8 = lambda a: jnp.sum(a, axis=0, keepdims=True)
    de = de16.reshape(2, 8, NST).sum(1)
    de_re, de_im = de[0].reshape(G, P), de[1].reshape(G, P)
    dc_re, dc_im = _diag_blocks(dc_full)
    dc_im = -dc_im
    dbb_re, dbb_im = [jnp.swapaxes(t, 1, 2) for t in _diag_blocks(dbb_full)]
    _, vjp = jax.vjp(_ssm_prep, W["ssm_a_re"], W["ssm_a_im"], W["ssm_b_re"], W["ssm_b_im"], W["ssm_log_dt"])
    sm["ssm_a_re"], sm["ssm_a_im"], sm["ssm_b_re"], sm["ssm_b_im"], sm["ssm_log_dt"] = vjp((de_re, de_im, dbb_re, dbb_im))
    sm["ssm_c_re"], sm["ssm_c_im"] = dc_re, dc_im
    sm["ssm_d"] = s8(dd8)
    sm["norm2_g"] = s8(dn2)
    sm["conv_b"], sm["conv_ln_g"], sm["conv_ln_b"] = s8(dcb8), s8(dlg8), s8(dlb8)
    sm["conv_w"] = dcw.reshape(KW, 8, CW).sum(1)
    sm["final_g"] = s8(dfg8)
    tok = early(sm)

    dz = [dz_conv, du, dgl]
    tok = put("w_in", _matmul(h1, dz, "tn", 1024, 512, 4096, BF16, "mm_g_w_in", after=tok))
    dx, _, dsh1, dsc1, dn1, _ = _normmod_bwd(dz, w_in, x, dx2, W["norm1_g"], sc1, g1, o, tok, "d_h1_normmod1_bwd")
    dmod = jnp.concatenate([s8(dsh1), s8(dsc1), s8(dg1_8), s8(dsh2), s8(dsc2), s8(dg2_8)], axis=1)
    return loss8, dx, s8(dn1), dmod


_BIG = ("w_in", "conv_proj", "ssm_glu", "w_out", "w_ffn_in", "w_ffn_out")
_BIG_AXIS = {"w_in": 1, "conv_proj": 1, "ssm_glu": 1, "w_out": 0, "w_ffn_in": 1, "w_ffn_out": 0}
_EARLY = ("conv_w", "conv_b", "conv_ln_g", "conv_ln_b", "ssm_a_re", "ssm_a_im", "ssm_b_re", "ssm_b_im", "ssm_c_re",
          "ssm_c_im", "ssm_d", "ssm_log_dt", "norm2_g", "final_g")
_LATE = ("norm1_g", "b_ada")
_S5_MATS = ("ssm_a_re", "ssm_a_im", "ssm_b_re", "ssm_b_im", "ssm_c_re", "ssm_c_im")
_ORDER = ("w_ada", "b_ada", "norm1_g", "w_in", "conv_w", "conv_b", "conv_ln_g", "conv_ln_b", "conv_proj",
          "ssm_a_re", "ssm_a_im", "ssm_b_re", "ssm_b_im", "ssm_c_re", "ssm_c_im", "ssm_d", "ssm_log_dt", "ssm_glu",
          "w_out", "norm2_g", "w_ffn_in", "w_ffn_out", "final_g")
_PACK_COLS = 1024


def _pack_rows(shape):
    return -(-int(np.prod(shape)) // (8 * _PACK_COLS)) * 8


def _pack(arrs):
    parts = []
    for a in arrs:
        flat = a.reshape(-1)
        n = _pack_rows(a.shape)
        parts.append(jnp.pad(flat, (0, n * _PACK_COLS - flat.shape[0])).reshape(n, _PACK_COLS))
    return jnp.concatenate(parts, 0)


def kernel(x, c, w_ada, b_ada, norm1_g, w_in, conv_w, conv_b, conv_ln_g, conv_ln_b, conv_proj, ssm_a_re, ssm_a_im, ssm_b_re, ssm_b_im, ssm_c_re, ssm_c_im, ssm_d, ssm_log_dt, ssm_glu, w_out, norm2_g, w_ffn_in, w_ffn_out, final_g, loss_target, m_w_ada, m_b_ada, m_norm1_g, m_w_in, m_conv_w, m_conv_b, m_conv_ln_g, m_conv_ln_b, m_conv_proj, m_ssm_a_re, m_ssm_a_im, m_ssm_b_re, m_ssm_b_im, m_ssm_c_re, m_ssm_c_im, m_ssm_d, m_ssm_log_dt, m_ssm_glu, m_w_out, m_norm2_g, m_w_ffn_in, m_w_ffn_out, m_final_g, v_w_ada, v_b_ada, v_norm1_g, v_w_in, v_conv_w, v_conv_b, v_conv_ln_g, v_conv_ln_b, v_conv_proj, v_ssm_a_re, v_ssm_a_im, v_ssm_b_re, v_ssm_b_im, v_ssm_c_re, v_ssm_c_im, v_ssm_d, v_ssm_log_dt, v_ssm_glu, v_w_out, v_norm2_g, v_w_ffn_in, v_w_ffn_out, v_final_g):
    given = dict(locals())
    mx, my, mc = _me()
    chip = 2 * mx + my
    dev = 4 * mx + 2 * my + mc
    def canon(a):
        return a.reshape(1, -1) if a.ndim <= 2 else a[0]

    wts = {n: canon(given[n]) for n in _ORDER}
    mom = {n: canon(given["m_" + n]) for n in _ORDER}
    var = {n: canon(given["v_" + n]) for n in _ORDER}

    W = {n: wts[n] for n in _ORDER if n not in _BIG}
    rest = [n for n in _BIG if n != "w_in"]
    rest_shards = [wts[n].astype(BF16) for n in rest]
    state_in, token = _half_gather_start(wts["w_in"].astype(BF16), c, "gather_start_w_in")
    W["ssm_log_dt"] = wts["ssm_log_dt"] + token[0:1, 0:1]
    W["ssm_c_re"] = wts["ssm_c_re"] + token[0, 0]
    tables = _ssm_tables(W)

    c_all, conv_w_full = _allgather8([jnp.broadcast_to(c, (8, D_MODEL)), jnp.pad(wts["conv_w"], ((0, 1), (0, 0)))],
                                     "gather_c_conv_w", after=[*tables, *rest_shards])
    c_all = c_all[:, 0, :]
    n_ada = wts["w_ada"].shape[1]
    b_cols = lax.dynamic_slice(wts["b_ada"], (0, chip * n_ada), (1, n_ada))
    mod_cols = _mod_shard(c_all, wts["w_ada"], b_cols)
    halves = _half_gather_wait(state_in, [mod_cols], "gather_wait_w_in")
    state_in, token = _half_swap_start(halves, "gather_swap_start_w_in")
    mods = _allgather8(mod_cols, "gather_mod", after=[token])
    mod = jnp.concatenate([lax.dynamic_index_in_dim(mods[2 * q], dev, 0, keepdims=True) for q in range(N_CHIP)], axis=1)
    W["conv_w"] = jnp.concatenate([conv_w_full[2 * q, :KW] for q in range(N_CHIP)], axis=1)
    w_in_full = _half_swap_wait(state_in, mod + W["conv_w"][0:1, 0:1], "gather_swap_wait_w_in")
    gstate, token = _gather_start(rest_shards, [_BIG_AXIS[n] for n in rest], w_in_full, "gather_start_rest")
    gstate = dict(zip(rest, gstate))
    mod = mod + token[0:1, 0:1]

    def getw(n, after):
        if n == "w_in":
            return w_in_full
        return _gather_wait(gstate[n], _BIG_AXIS[n], after, "gather_wait_" + n)

    sstate, estate = {}, []

    def put(n, g):
        sstate[n], tok = _scatter_start(g, _BIG_AXIS[n], "scatter_start_" + n)
        return tok

    first5 = [n for n in _BIG if n != "w_in"]

    def early(sm):
        state, tok = _all8_start(_pack([sm[n] for n in _EARLY]), "small_start")
        estate.append(state)
        held = [_scatter_wait(sstate[n], _BIG_AXIS[n], tok, "scatter_wait_" + n) for n in first5]
        state, tok = _swap_start(held, tok, "swap_start")
        estate.append(state)
        return tok

    loss8, dx, dn1, dmod = _device_step(x[0], mod, W, tables, loss_target[0], getw, put, early)

    late_state, tok = _all8_start(_pack([dn1, dmod, loss8]), "late_start")
    held5, sib5 = _swap_wait(estate[1], tok, "swap_wait")
    outs = {}
    for i, n in enumerate(first5):
        outs[n] = _adamw(wts[n], mom[n], var[n], [[held5[i]], [sib5[i]]], "adamw_" + n)
    allp = _all8_wait(estate[0], dx, "small_wait")
    upd, sums = _adamw_small(allp, _EARLY, wts, mom, var, ("conv_w",) + _S5_MATS, "adamw_small")
    outs.update(upd)

    def swapped(n, a):
        return jnp.swapaxes(a, 1, 2) if n in ("ssm_b_re", "ssm_b_im") else a

    def summed(n):
        return swapped(n, sums[n].reshape(-1)[:wts[n].size].reshape(wts[n].shape))

    res = _adamw_native([(swapped(n, wts[n]), swapped(n, mom[n]), swapped(n, var[n]), summed(n)) for n in _S5_MATS],
                        "adamw_s5")
    for n, r in zip(_S5_MATS, res):
        outs[n] = tuple(swapped(n, a) for a in r)

    late = _all8_wait(late_state, outs[first5[-1]][1], "late_wait")
    n_late = _pack_rows((D_MODEL,)) + _pack_rows((6 * D_MODEL,))
    loss = jnp.sum(late[:, n_late:, :])
    late = late[:, :n_late, :]
    held_in = _scatter_wait(sstate["w_in"], _BIG_AXIS["w_in"], late, "scatter_wait_w_in")
    state_in, tok = _swap_start([held_in], late, "swap_start_w_in")

    r1 = _pack_rows((D_MODEL,))
    dmod_all = late[:, r1:, :].reshape(N_DEV, -1)[:, :6 * D_MODEL]
    dmod_cols = lax.dynamic_slice(dmod_all, (0, chip * n_ada), (N_DEV, n_ada))
    g_ada = _ada_grad(c_all, dmod_cols, tok)
    outs["w_ada"] = _adamw(wts["w_ada"], mom["w_ada"], var["w_ada"], [[g_ada]], "adamw_w_ada")
    upd, _ = _adamw_small(late, _LATE, wts, mom, var, (), "adamw_late")
    outs.update(upd)
    held_in, sib_in = _swap_wait(state_in, outs["w_ada"][1], "swap_wait_w_in")
    outs["w_in"] = _adamw(wts["w_in"], mom["w_in"], var["w_in"], [held_in, sib_in], "adamw_w_in")
    g_cw_full = sums["conv_w"].reshape(-1)[:KW * CW].reshape(KW, CW)
    g_cw = lax.dynamic_slice(g_cw_full, (0, chip * (CW // N_CHIP)), (KW, CW // N_CHIP))
    pad = lambda a: jnp.pad(a, ((0, 1), (0, 0)))
    r_cw = _adamw(pad(wts["conv_w"]), pad(mom["conv_w"]), pad(var["conv_w"]), [[pad(g_cw)]], "adamw_conv_w")
    outs["conv_w"] = tuple(r[:KW] for r in r_cw)

    def shaped(n, a):
        return a.reshape(given[n].shape)

    result = [loss, dx[None]]
    for q in range(4):
        result += [shaped(n, outs[n][q]) for n in _ORDER]
    return tuple(result)
```

```python
import math

import jax
import jax.numpy as jnp
import numpy as np
from jax import lax
from jax.experimental import pallas as pl
from jax.experimental.pallas import tpu as pltpu

F32 = jnp.float32
BF16 = jnp.bfloat16
EPS = 1e-6
D_MODEL = 1024
CW = 512
KW = 31
HALO = 32
G, P, H = 32, 64, 16
NST = G * P
FH = 2816
N_DEV = 8
N_CHIP = 4
VMEM_LIMIT = 56 * 1024 * 1024
LR, B1, B2, AEPS, WD, STEP = 0.001, 0.9, 0.999, 1e-08, 0.01, 10
MESH = pl.DeviceIdType.MESH


def _cp(sem=None):
    return pltpu.CompilerParams(dimension_semantics=sem, vmem_limit_bytes=VMEM_LIMIT)


def _sig(x):
    return jax.nn.sigmoid(x)


def _full(shape):
    return pl.BlockSpec(shape, lambda *_: (0,) * len(shape))


def _resident(shape):
    return pl.BlockSpec(shape, lambda *_: (0,) * len(shape), pipeline_mode=pl.Buffered(1))


def _colsum8(v):
    t, c = v.shape
    return jnp.sum(v.reshape(t // 8, 8, c), axis=0)


def _matmul(a, b, mode, tm, tn, tk, out_dtype, name, after=None, n_outer=False, m_cols=None):
    m0 = 0
    b_parts = list(b) if isinstance(b, (list, tuple)) else [b]
    if mode == "nn":
        (M, K), N = a.shape, b.shape[1]
    elif mode == "nt":
        (M, K), N = a.shape, b.shape[0]
    else:
        (K, M), N = a.shape, sum(p.shape[1] for p in b_parts)
        if m_cols is not None:
            m0, M = m_cols
    tm, tn, tk = min(tm, M), min(tn, N), min(tk, K)
    assert M % tm == 0 and N % tn == 0 and K % tk == 0 and m0 % tm == 0, (name, M, N, K, tm, tn, tk)
    assert len(b_parts) == 1 or (mode == "tn" and all(p.shape[1] % tn == 0 for p in b_parts)), name
    nk = K // tk
    mb = m0 // tm
    counts = [p.shape[1] // tn for p in b_parts] if mode == "tn" else [N // tn]
    starts = [sum(counts[:p]) for p in range(len(counts))]

    def ij(fn):
        return (lambda j, i, k: fn(i, j, k)) if n_outer else fn

    if mode == "nn":
        a_spec = pl.BlockSpec((tm, tk), ij(lambda i, j, k: (i, k)))
        b_spec = pl.BlockSpec((tk, tn), ij(lambda i, j, k: (k, j)))
        dims = (((1,), (0,)), ((), ()))
    elif mode == "nt":
        a_spec = pl.BlockSpec((tm, tk), ij(lambda i, j, k: (i, k)))
        b_spec = pl.BlockSpec((tn, tk), ij(lambda i, j, k: (j, k)))
        dims = (((1,), (1,)), ((), ()))
    else:
        a_spec = pl.BlockSpec((tk, tm), ij(lambda i, j, k: (k, i + mb)))
        dims = (((0,), (0,)), ((), ()))
    if mode == "tn":
        b_specs = [pl.BlockSpec((tk, tn), ij(lambda i, j, k, s=s, n=n: (k, jnp.clip(j - s, 0, n - 1))))
                   for s, n in zip(starts, counts)]
    else:
        b_specs = [b_spec]
    nb = len(b_parts)

    def body(a_ref, *rest):
        b_refs = rest[:nb]
        o_ref, acc_ref = rest[-2:]
        j = pl.program_id(0 if n_outer else 1)
        k = pl.program_id(2)

        def compute(b_ref):
            part = lax.dot_general(a_ref[...].astype(BF16), b_ref[...].astype(BF16), dims,
                                   preferred_element_type=F32)
            if nk == 1:
                o_ref[...] = part.astype(out_dtype)
            else:
                @pl.when(k == 0)
                def _():
                    acc_ref[...] = part

                @pl.when(k > 0)
                def _():
                    acc_ref[...] += part

                @pl.when(k == nk - 1)
                def _():
                    o_ref[...] = acc_ref[...].astype(out_dtype)

        if nb == 1:
            compute(b_refs[0])
        else:
            for p in range(nb):
                pl.when(jnp.logical_and(j >= starts[p], j < starts[p] + counts[p]))(
                    lambda b_ref=b_refs[p]: compute(b_ref))

    return pl.pallas_call(
        body, name=name,
        out_shape=jax.ShapeDtypeStruct((M, N), out_dtype),
        grid=(N // tn, M // tm, nk) if n_outer else (M // tm, N // tn, nk),
        in_specs=[a_spec] + b_specs + ([] if after is None else [pl.BlockSpec(memory_space=pl.ANY)]),
        out_specs=pl.BlockSpec((tm, tn), ij(lambda i, j, k: (i, j))),
        scratch_shapes=[pltpu.VMEM((tm, tn) if nk > 1 else (8, 128), F32)],
        compiler_params=_cp(("parallel", "parallel", "arbitrary")),
    )(*([a] + b_parts + ([] if after is None else [after])))


def _row_tile(S):
    return min(512, S)


def _in_proj(x, g, sc, sh, w_in):
    S, D = x.shape
    N = w_in.shape[1]
    tm = min(512, S)

    def body(x_ref, g_ref, sc_ref, sh_ref, w_ref, h_ref, z_ref):
        xv = x_ref[...]
        r = lax.rsqrt(jnp.mean(xv * xv, axis=-1, keepdims=True) + EPS)
        h = (xv * r * (g_ref[...] * (1.0 + sc_ref[...])) + sh_ref[...]).astype(BF16)
        h_ref[...] = h
        z_ref[...] = jnp.dot(h, w_ref[...], preferred_element_type=F32).astype(BF16)

    row = pl.BlockSpec((tm, D), lambda i: (i, 0))
    par = _full((1, D))
    return pl.pallas_call(
        body, name="in_proj",
        out_shape=(jax.ShapeDtypeStruct((S, D), BF16), jax.ShapeDtypeStruct((S, N), BF16)), grid=(S // tm,),
        in_specs=[row, par, par, par, _resident((D, N))], out_specs=(row, pl.BlockSpec((tm, N), lambda i: (i, 0))),
        compiler_params=_cp(("parallel",)))(x, g, sc, sh, w_in)


def _fill_shifted(buf_ref, sh_ref):
    n = buf_ref.shape[0] - 8
    for s in range(1, 8):
        sh_ref[s, 0:n, :] = buf_ref[s:s + n, :]


def _window(buf_ref, sh_ref, off, n):
    s = off % 8
    return buf_ref[off:off + n, :] if s == 0 else sh_ref[s, off - s:off - s + n, :]


def _conv_fwd(z, conv_w, conv_b, ln_g, ln_b):
    S = z.shape[0]
    tm = min(128, S)
    sub = 32
    hb = tm // HALO

    def body(a_ref, g_ref, ha_ref, hg_ref, w_ref, b_ref, lg_ref, lb_ref, yc_ref, s_ref, ug_ref, sh_ref):
        i = pl.program_id(0)
        halo = ha_ref[...].astype(F32) * _sig(hg_ref[...].astype(F32))
        ug_ref[0:HALO, :] = jnp.where(i == 0, 0.0, halo)
        ug_ref[HALO:, :] = a_ref[...].astype(F32) * _sig(g_ref[...].astype(F32))
        _fill_shifted(ug_ref, sh_ref)
        for rb in range(tm // sub):
            acc = jnp.zeros((sub, CW), F32) + b_ref[...]
            for k in range(KW):
                off = rb * sub + HALO - (KW - 1) + k
                acc = acc + w_ref[k:k + 1, :] * _window(ug_ref, sh_ref, off, sub)
            yc_ref[rb * sub:(rb + 1) * sub, :] = acc
            mu = jnp.mean(acc, axis=-1, keepdims=True)
            cen = acc - mu
            rstd = lax.rsqrt(jnp.mean(cen * cen, axis=-1, keepdims=True) + EPS)
            ln = cen * rstd * lg_ref[...] + lb_ref[...]
            s_ref[rb * sub:(rb + 1) * sub, :] = (ln * _sig(ln)).astype(BF16)

    prev = lambda i: (jnp.maximum(i * hb - 1, 0), 0)
    return pl.pallas_call(
        body, name="conv_fwd",
        out_shape=(jax.ShapeDtypeStruct((S, CW), F32), jax.ShapeDtypeStruct((S, CW), BF16)),
        grid=(S // tm,),
        in_specs=[pl.BlockSpec((tm, CW), lambda i: (i, 0)), pl.BlockSpec((tm, CW), lambda i: (i, 1)),
                  pl.BlockSpec((HALO, CW), prev), pl.BlockSpec((HALO, CW), lambda i: (jnp.maximum(i * hb - 1, 0), 1)),
                  _full((KW, CW)), _full((1, CW)), _full((1, CW)), _full((1, CW))],
        out_specs=(pl.BlockSpec((tm, CW), lambda i: (i, 0)), pl.BlockSpec((tm, CW), lambda i: (i, 0))),
        scratch_shapes=[pltpu.VMEM((tm + HALO, CW), F32), pltpu.VMEM((8, tm + HALO, CW), F32)],
        compiler_params=_cp(("parallel",)))(z, z, z, z, conv_w, conv_b, ln_g, ln_b)


def _conv_bwd_ln(dyconv, w_cp, yc, ln_g, ln_b):
    S = yc.shape[0]
    tm = _row_tile(S)

    def body(dy_ref, w_ref, yc_ref, lg_ref, lb_ref, dyc_ref, dlg_ref, dlb_ref, dcb_ref):
        i = pl.program_id(0)
        dsc = lax.dot_general(dy_ref[...], w_ref[...], (((1,), (1,)), ((), ())), preferred_element_type=F32)
        yc_v = yc_ref[...]
        mu = jnp.mean(yc_v, axis=-1, keepdims=True)
        cen = yc_v - mu
        rstd = lax.rsqrt(jnp.mean(cen * cen, axis=-1, keepdims=True) + EPS)
        yn = cen * rstd
        ln = yn * lg_ref[...] + lb_ref[...]
        sl = _sig(ln)
        dln = dsc * (sl * (1.0 + ln * (1.0 - sl)))
        dyn = dln * lg_ref[...]
        dyc = rstd * (dyn - jnp.mean(dyn, axis=-1, keepdims=True)
                      - yn * jnp.mean(dyn * yn, axis=-1, keepdims=True))
        dyc_ref[...] = dyc

        @pl.when(i == 0)
        def _():
            dlg_ref[...] = jnp.zeros_like(dlg_ref)
            dlb_ref[...] = jnp.zeros_like(dlb_ref)
            dcb_ref[...] = jnp.zeros_like(dcb_ref)

        dlg_ref[...] += _colsum8(dln * yn)
        dlb_ref[...] += _colsum8(dln)
        dcb_ref[...] += _colsum8(dyc)

    row = pl.BlockSpec((tm, CW), lambda i: (i, 0))
    acc = jax.ShapeDtypeStruct((8, CW), F32)
    return pl.pallas_call(
        body, name="conv_bwd_ln",
        out_shape=(jax.ShapeDtypeStruct((S, CW), F32), acc, acc, acc), grid=(S // tm,),
        in_specs=[pl.BlockSpec((tm, D_MODEL), lambda i: (i, 0)), _full((CW, D_MODEL)), row, _full((1, CW)),
                  _full((1, CW))],
        out_specs=(row, _full((8, CW)), _full((8, CW)), _full((8, CW))),
        compiler_params=_cp(("arbitrary",)))(dyconv, w_cp, yc, ln_g, ln_b)


def _conv_bwd(dyc, z, conv_w):
    S = z.shape[0]
    tm = min(128, S)
    sub = 32
    hb = tm // HALO
    nt = S // tm

    def body(d_ref, dn_ref, a_ref, g_ref, ha_ref, hg_ref, w_ref, dz_ref, dw_ref, ug_ref, dy_ref, ugs_ref, dys_ref):
        i = pl.program_id(0)
        halo = ha_ref[...].astype(F32) * _sig(hg_ref[...].astype(F32))
        ug_ref[0:HALO, :] = jnp.where(i == 0, 0.0, halo)
        a = a_ref[...].astype(F32)
        sg = _sig(g_ref[...].astype(F32))
        ug_ref[HALO:, :] = a * sg
        dy_ref[0:tm, :] = d_ref[...]
        dy_ref[tm:, :] = jnp.where(i == nt - 1, 0.0, dn_ref[...])
        _fill_shifted(ug_ref, ugs_ref)
        _fill_shifted(dy_ref, dys_ref)

        @pl.when(i == 0)
        def _():
            dw_ref[...] = jnp.zeros_like(dw_ref)

        for rb in range(tm // sub):
            r0 = rb * sub
            acc = jnp.zeros((sub, CW), F32)
            dyc_b = dy_ref[r0:r0 + sub, :]
            for k in range(KW):
                up = r0 + (KW - 1) - k
                acc = acc + w_ref[k:k + 1, :] * _window(dy_ref, dys_ref, up, sub)
                off = r0 + HALO - (KW - 1) + k
                dw_ref[k * 8:(k + 1) * 8, :] += _colsum8(dyc_b * _window(ug_ref, ugs_ref, off, sub))
            a_b = a[r0:r0 + sub, :]
            sg_b = sg[r0:r0 + sub, :]
            dz_ref[r0:r0 + sub, 0:CW] = (acc * sg_b).astype(BF16)
            dz_ref[r0:r0 + sub, CW:2 * CW] = (acc * a_b * sg_b * (1.0 - sg_b)).astype(BF16)

    return pl.pallas_call(
        body, name="conv_bwd",
        out_shape=(jax.ShapeDtypeStruct((S, 2 * CW), BF16), jax.ShapeDtypeStruct((KW * 8, CW), F32)),
        grid=(nt,),
        in_specs=[pl.BlockSpec((tm, CW), lambda i: (i, 0)),
                  pl.BlockSpec((HALO, CW), lambda i: (jnp.minimum((i + 1) * hb, nt * hb - 1), 0)),
                  pl.BlockSpec((tm, CW), lambda i: (i, 0)), pl.BlockSpec((tm, CW), lambda i: (i, 1)),
                  pl.BlockSpec((HALO, CW), lambda i: (jnp.maximum(i * hb - 1, 0), 0)),
                  pl.BlockSpec((HALO, CW), lambda i: (jnp.maximum(i * hb - 1, 0), 1)),
                  _full((KW, CW))],
        out_specs=(pl.BlockSpec((tm, 2 * CW), lambda i: (i, 0)), _full((KW * 8, CW))),
        scratch_shapes=[pltpu.VMEM((tm + HALO, CW), F32), pltpu.VMEM((tm + HALO, CW), F32),
                        pltpu.VMEM((8, tm + HALO, CW), F32), pltpu.VMEM((8, tm + HALO, CW), F32)],
        compiler_params=_cp(("arbitrary",)))(dyc, dyc, z, z, z, z, conv_w)


_GELU_C = math.sqrt(2.0 / math.pi)


def _gelu(x):
    return 0.5 * x * (1.0 + jnp.tanh(_GELU_C * (x + 0.044715 * x * x * x)))


def _gelu_grad(x):
    t = jnp.tanh(_GELU_C * (x + 0.044715 * x * x * x))
    return 0.5 * (1.0 + t) + 0.5 * x * (1.0 - t * t) * (_GELU_C * (1.0 + 3 * 0.044715 * x * x))


_NCL = 4
_UC = CW // _NCL
_LW = NST // _NCL
_CS = 2 * _LW


def _ssm_fwd(z, bb, cm, d, tab):
    S = z.shape[0]
    tm = min(512, S)

    def body(u_ref, bb_ref, cm_ref, d_ref, t_ref, x_ref, ys_ref, yg_ref, car_ref):
        i = pl.program_id(0)

        @pl.when(i == 0)
        def _():
            car_ref[...] = jnp.zeros_like(car_ref)

        u16 = u_ref[...]
        u = u16.astype(F32)
        for c in range(_NCL):
            lre = pl.ds(c * _CS, _LW)
            lim = pl.ds(c * _CS + _LW, _LW)
            tl = pl.ds(c * _LW, _LW)
            x_ref[:, c * _CS:(c + 1) * _CS] = jnp.dot(u16[:, c * _UC:(c + 1) * _UC], bb_ref[c],
                                                      preferred_element_type=F32)

            def blk(j, car):
                cr, ci = car
                rows = pl.ds(pl.multiple_of(j * 8, 8), 8)
                r = x_ref[rows, lre]
                im = x_ref[rows, lim]
                for lvl, s in enumerate((1, 2, 4)):
                    mr = t_ref[16 * lvl:16 * lvl + 8, tl]
                    mi = t_ref[16 * lvl + 8:16 * lvl + 16, tl]
                    sr = pltpu.roll(r, s, 0)
                    si = pltpu.roll(im, s, 0)
                    r, im = r + (mr * sr - mi * si), im + (mr * si + mi * sr)
                pr = t_ref[48:56, tl]
                pi_ = t_ref[56:64, tl]
                r, im = r + (pr * cr - pi_ * ci), im + (pr * ci + pi_ * cr)
                x_ref[rows, lre] = r
                x_ref[rows, lim] = im
                return (jnp.broadcast_to(r[7:8, :], (8, _LW)), jnp.broadcast_to(im[7:8, :], (8, _LW)))

            cr, ci = lax.fori_loop(0, tm // 8, blk, (car_ref[:, lre], car_ref[:, lim]))
            car_ref[:, lre] = cr
            car_ref[:, lim] = ci
            cols = slice(c * _UC, (c + 1) * _UC)
            ys = jnp.dot(x_ref[:, c * _CS:(c + 1) * _CS].astype(BF16), cm_ref[c], preferred_element_type=F32)
            ys = ys + d_ref[:, cols] * u[:, cols]
            ys_ref[:, cols] = ys
            yg_ref[:, cols] = _gelu(ys).astype(BF16)

    return pl.pallas_call(
        body, name="ssm_fwd",
        out_shape=(jax.ShapeDtypeStruct((S, 2 * NST), F32), jax.ShapeDtypeStruct((S, CW), F32),
                   jax.ShapeDtypeStruct((S, CW), BF16)),
        grid=(S // tm,),
        in_specs=[pl.BlockSpec((tm, CW), lambda i: (i, 2)), _full((_NCL, _UC, _CS)), _full((_NCL, _CS, _UC)),
                  _full((1, CW)), _full((64, NST))],
        out_specs=(pl.BlockSpec((tm, 2 * NST), lambda i: (i, 0)), pl.BlockSpec((tm, CW), lambda i: (i, 0)),
                   pl.BlockSpec((tm, CW), lambda i: (i, 0))),
        scratch_shapes=[pltpu.VMEM((8, 2 * NST), F32)],
        compiler_params=_cp(("arbitrary",)))(z, bb, cm, d, tab)


def _ssm_bwd(dzz, w_glu, ys, z, xs, cmt, bbt, d, tab, after):
    S = z.shape[0]
    tm = min(512, S)
    nt = S // tm
    tdims = (((0,), (0,)), ((), ()))

    def body(dzz_ref, wglu_ref, ys_ref, u_ref, x_ref, cmt_ref, bbt_ref, d_ref, t_ref, after_ref,
             du_ref, de_ref, dd_ref, dc_hbm, dbb_hbm, car_ref, lam_ref, dc_ref, dbb_ref):
        i = pl.program_id(0)

        @pl.when(i == 0)
        def _():
            car_ref[...] = jnp.zeros_like(car_ref)
            de_ref[...] = jnp.zeros_like(de_ref)
            dd_ref[...] = jnp.zeros_like(dd_ref)
            dc_ref[...] = jnp.zeros_like(dc_ref)
            dbb_ref[...] = jnp.zeros_like(dbb_ref)

        u16 = u_ref[...]
        u = u16.astype(F32)
        dyg = lax.dot_general(dzz_ref[...], wglu_ref[...], (((1,), (1,)), ((), ())), preferred_element_type=F32)
        dys = dyg * _gelu_grad(ys_ref[...])
        dys16 = dys.astype(BF16)
        dd_ref[...] += _colsum8(dys * u)
        row = lax.broadcasted_iota(jnp.int32, (8, _LW), 0)
        for c in range(_NCL):
            lre = pl.ds(c * _CS, _LW)
            lim = pl.ds(c * _CS + _LW, _LW)
            tl = pl.ds(c * _LW, _LW)
            cols = slice(c * _UC, (c + 1) * _UC)
            span = slice(c * _CS, (c + 1) * _CS)
            dc_ref[cols, :] += lax.dot_general(dys16[:, cols], x_ref[:, span].astype(BF16), tdims,
                                               preferred_element_type=F32)
            lam_ref[...] = jnp.dot(dys16[:, cols], cmt_ref[c], preferred_element_type=F32)

            def blk(jj, car):
                cr, ci, ar, ai = car
                j = tm // 8 - 1 - jj
                rows = pl.ds(pl.multiple_of(j * 8, 8), 8)
                r = lam_ref[rows, 0:_LW]
                im = lam_ref[rows, _LW:_CS]
                for lvl, s in enumerate((1, 2, 4)):
                    mr = t_ref[16 * lvl:16 * lvl + 8, tl]
                    mi = t_ref[16 * lvl + 8:16 * lvl + 16, tl]
                    sr = pltpu.roll(r, 8 - s, 0)
                    si = pltpu.roll(im, 8 - s, 0)
                    r, im = r + (mr * sr - mi * si), im + (mr * si + mi * sr)
                pr = t_ref[48:56, tl]
                pi_ = t_ref[56:64, tl]
                r, im = r + (pr * cr - pi_ * ci), im + (pr * ci + pi_ * cr)
                lam_ref[rows, 0:_LW] = r
                lam_ref[rows, _LW:_CS] = im
                nr = jnp.where(row == 7, cr, pltpu.roll(r, 7, 0))
                ni = jnp.where(row == 7, ci, pltpu.roll(im, 7, 0))
                xr = x_ref[rows, lre]
                xi = x_ref[rows, lim]
                ar = ar + (nr * xr + ni * xi)
                ai = ai + (ni * xr - nr * xi)
                return (jnp.broadcast_to(r[0:1, :], (8, _LW)), jnp.broadcast_to(im[0:1, :], (8, _LW)), ar, ai)

            zero = jnp.zeros((8, _LW), F32)
            cr, ci, ar, ai = lax.fori_loop(0, tm // 8, blk, (car_ref[:, lre], car_ref[:, lim], zero, zero))
            car_ref[:, lre] = cr
            car_ref[:, lim] = ci
            de_ref[0:8, tl] += ar
            de_ref[8:16, tl] += ai
            lam16 = lam_ref[...].astype(BF16)
            dbb_ref[cols, :] += lax.dot_general(u16[:, cols], lam16, tdims, preferred_element_type=F32)
            du = jnp.dot(lam16, bbt_ref[c], preferred_element_type=F32) + dys[:, cols] * d_ref[:, cols]
            du_ref[:, cols] = du.astype(BF16)

        @pl.when(i == nt - 1)
        def _():
            pltpu.sync_copy(dc_ref, dc_hbm)
            pltpu.sync_copy(dbb_ref, dbb_hbm)

    rev = lambda i: (nt - 1 - i, 0)
    once = lambda shape: pl.BlockSpec(shape, lambda *_: (0,) * len(shape), pipeline_mode=pl.Buffered(1))
    cross = jax.ShapeDtypeStruct((CW, _CS), F32)
    return pl.pallas_call(
        body, name="ssm_bwd",
        out_shape=(jax.ShapeDtypeStruct((S, CW), BF16), jax.ShapeDtypeStruct((16, NST), F32),
                   jax.ShapeDtypeStruct((8, CW), F32), cross, cross),
        grid=(nt,),
        in_specs=[pl.BlockSpec((tm, 2 * D_MODEL), rev), once((CW, 2 * D_MODEL)), pl.BlockSpec((tm, CW), rev),
                  pl.BlockSpec((tm, CW), lambda i: (nt - 1 - i, 2)), pl.BlockSpec((tm, 2 * NST), rev),
                  once((_NCL, _UC, _CS)), once((_NCL, _CS, _UC)), _full((1, CW)), once((64, NST)),
                  pl.BlockSpec(memory_space=pl.ANY)],
        out_specs=(pl.BlockSpec((tm, CW), rev), _full((16, NST)), _full((8, CW)),
                   pl.BlockSpec(memory_space=pl.ANY), pl.BlockSpec(memory_space=pl.ANY)),
        scratch_shapes=[pltpu.VMEM((8, 2 * NST), F32), pltpu.VMEM((tm, _CS), F32),
                        pltpu.VMEM((CW, _CS), F32), pltpu.VMEM((CW, _CS), F32)],
        compiler_params=_cp(("arbitrary",)))(dzz, w_glu, ys, z, xs, cmt, bbt, d, tab, after)


def _ssm_prep(a_re, a_im, b_re, b_im, log_dt):
    dt = jnp.exp(log_dt.reshape(G))[:, None]
    mag = jnp.exp(dt * a_re)
    e_re, e_im = mag * jnp.cos(dt * a_im), mag * jnp.sin(dt * a_im)
    n_re, n_im = e_re - 1.0, e_im
    den = a_re * a_re + a_im * a_im
    q_re = (n_re * a_re + n_im * a_im) / den
    q_im = (n_im * a_re - n_re * a_im) / den
    bb_re = q_re[..., None] * b_re - q_im[..., None] * b_im
    bb_im = q_re[..., None] * b_im + q_im[..., None] * b_re
    return e_re, e_im, bb_re, bb_im


def _scan_tables(e_re, e_im, reverse):
    er = e_re.reshape(1, NST)
    ei = e_im.reshape(1, NST)
    if reverse:
        ei = -ei
    pows = [(er, ei)]
    for _ in range(7):
        pr, pi_ = pows[-1]
        pows.append((pr * er - pi_ * ei, pr * ei + pi_ * er))
    row = jnp.arange(8)[:, None]
    out = []
    for s in (1, 2, 4):
        pr, pi_ = pows[s - 1]
        keep = (row + s <= 7) if reverse else (row >= s)
        out += [jnp.where(keep, pr, 0.0), jnp.where(keep, pi_, 0.0)]
    allr = jnp.concatenate([p[0] for p in pows], 0)
    alli = jnp.concatenate([p[1] for p in pows], 0)
    if reverse:
        allr, alli = allr[::-1], alli[::-1]
    out += [allr, alli]
    return jnp.concatenate(out, 0).astype(F32)


def _block_diag_mats(bb_re, bb_im, c_re, c_im):
    gc = G // _NCL
    eye = jnp.eye(gc, dtype=F32)
    bre = jnp.einsum("cjph,jk->cjhkp", bb_re.reshape(_NCL, gc, P, H), eye).reshape(_NCL, _UC, _LW)
    bim = jnp.einsum("cjph,jk->cjhkp", bb_im.reshape(_NCL, gc, P, H), eye).reshape(_NCL, _UC, _LW)
    bb = jnp.concatenate([bre, bim], 2)
    cre = jnp.einsum("cjhp,jk->cjpkh", c_re.reshape(_NCL, gc, H, P), eye).reshape(_NCL, _LW, _UC)
    cim = jnp.einsum("cjhp,jk->cjpkh", c_im.reshape(_NCL, gc, H, P), eye).reshape(_NCL, _LW, _UC)
    cm = jnp.concatenate([cre, -cim], 1)
    return bb, cm


def _diag_blocks(cross):
    gc = G // _NCL
    six = cross.reshape(_NCL, gc, H, 2, gc, P)
    same = jnp.eye(gc, dtype=bool)[None, :, None, None, :, None]
    diag = jnp.sum(jnp.where(same, six, 0.0), axis=4)
    diag = jnp.moveaxis(diag, 3, 0).reshape(2, G, H, P)
    return diag[0], diag[1]


def _mix_fwd(scv, yg, z, x, w_cp, w_glu, w_out, g1, n2g, sc2, sh2):
    S = z.shape[0]
    tm = min(512, S)
    D = D_MODEL

    def body(s_ref, yg_ref, glc0_ref, glc1_ref, gls0_ref, gls1_ref, x_ref, wcp_ref, wglu_ref, wout_ref,
             g1_ref, n2_ref, sc_ref, sh_ref, yc_ref, zz_ref, m_ref, o_ref, x2_ref, h2_ref):
        y_conv = jnp.dot(s_ref[...], wcp_ref[...], preferred_element_type=F32)
        zz = jnp.dot(yg_ref[...], wglu_ref[...], preferred_element_type=F32)
        yc_ref[...] = y_conv.astype(BF16)
        zz_ref[...] = zz.astype(BF16)
        for half, (glc_ref, gls_ref) in enumerate(((glc0_ref, gls0_ref), (glc1_ref, gls1_ref))):
            lo, hi = half * CW, (half + 1) * CW
            y_ssm = zz[:, lo:hi] * _sig(zz[:, D + lo:D + hi])
            m_ref[:, lo:hi] = (_sig(glc_ref[...].astype(F32)) * y_conv[:, lo:hi]
                               + _sig(gls_ref[...].astype(F32)) * y_ssm).astype(BF16)
        o = jnp.dot(m_ref[...], wout_ref[...], preferred_element_type=F32)
        o_ref[...] = o.astype(BF16)
        xv = x_ref[...] + g1_ref[...] * o
        x2_ref[...] = xv
        r = lax.rsqrt(jnp.mean(xv * xv, axis=-1, keepdims=True) + EPS)
        h2_ref[...] = (xv * r * (n2_ref[...] * (1.0 + sc_ref[...])) + sh_ref[...]).astype(BF16)

    zb_ = lambda j: pl.BlockSpec((tm, CW), lambda i: (i, j))
    row = lambda w: pl.BlockSpec((tm, w), lambda i: (i, 0))
    par = _full((1, D))
    bf = lambda w: jax.ShapeDtypeStruct((S, w), BF16)
    return pl.pallas_call(
        body, name="mix_fwd",
        out_shape=(bf(D), bf(2 * D), bf(D), bf(D), jax.ShapeDtypeStruct((S, D), F32), bf(D)),
        grid=(S // tm,),
        in_specs=[row(CW), row(CW), zb_(3), zb_(4), zb_(5), zb_(6), row(D), _resident((CW, D)),
                  _resident((CW, 2 * D)), _resident((D, D)), par, par, par, par],
        out_specs=(row(D), row(2 * D), row(D), row(D), row(D), row(D)),
        compiler_params=_cp(("parallel",)))(scv, yg, z, z, z, z, x, w_cp, w_glu, w_out, g1, n2g, sc2, sh2)


def _mix_bwd(do, w_out, z, zz, y_conv, after):
    S = z.shape[0]
    tm = min(512, S)
    D = D_MODEL

    def body(do_ref, w_ref, glc0_ref, glc1_ref, gls0_ref, gls1_ref, za_ref, zb_ref, yc_ref, after_ref,
             dyc_ref, dgl_ref, dzz_ref):
        dm = lax.dot_general(do_ref[...], w_ref[...], (((1,), (1,)), ((), ())), preferred_element_type=F32)
        for half, (glc_ref, gls_ref) in enumerate(((glc0_ref, gls0_ref), (glc1_ref, gls1_ref))):
            lo, hi = half * CW, (half + 1) * CW
            dm_v = dm[:, lo:hi]
            sgc = _sig(glc_ref[...].astype(F32))
            sgs = _sig(gls_ref[...].astype(F32))
            szb = _sig(zb_ref[:, lo:hi].astype(F32))
            za = za_ref[:, lo:hi].astype(F32)
            dyc_ref[:, lo:hi] = (dm_v * sgc).astype(BF16)
            dgl_ref[:, lo:hi] = (dm_v * yc_ref[:, lo:hi].astype(F32) * sgc * (1.0 - sgc)).astype(BF16)
            dys = dm_v * sgs
            dgl_ref[:, D + lo:D + hi] = (dys * (za * szb) * (1.0 - sgs)).astype(BF16)
            dzz_ref[:, lo:hi] = (dys * szb).astype(BF16)
            dzz_ref[:, D + lo:D + hi] = (dys * za * szb * (1.0 - szb)).astype(BF16)

    zb_ = lambda j: pl.BlockSpec((tm, CW), lambda i: (i, j))
    wide = lambda j: pl.BlockSpec((tm, D), lambda i: (i, j))
    return pl.pallas_call(
        body, name="mix_bwd",
        out_shape=(jax.ShapeDtypeStruct((S, D), BF16), jax.ShapeDtypeStruct((S, 2 * D), BF16),
                   jax.ShapeDtypeStruct((S, 2 * D), BF16)),
        grid=(S // tm,),
        in_specs=[wide(0), _resident((D, D)), zb_(3), zb_(4), zb_(5), zb_(6), wide(0), wide(1), wide(0),
                  pl.BlockSpec(memory_space=pl.ANY)],
        out_specs=(wide(0), pl.BlockSpec((tm, 2 * D), lambda i: (i, 0)), pl.BlockSpec((tm, 2 * D), lambda i: (i, 0))),
        compiler_params=_cp(("parallel",)))(do, w_out, z, z, z, z, zz, zz, y_conv, after)


_FC = 1408


def _ffn_in_act(h2, w_fi):
    S, D = h2.shape
    tm = min(512, S)

    def body(h_ref, w_ref, f_ref, a_ref):
        hv = h_ref[...]
        for c in range(FH // _FC):
            lo, hi = c * _FC, (c + 1) * _FC
            g = jnp.dot(hv, w_ref[:, lo:hi], preferred_element_type=F32)
            u = jnp.dot(hv, w_ref[:, FH + lo:FH + hi], preferred_element_type=F32)
            f_ref[:, lo:hi] = g.astype(BF16)
            f_ref[:, FH + lo:FH + hi] = u.astype(BF16)
            a_ref[:, lo:hi] = (g * _sig(g) * u).astype(BF16)

    return pl.pallas_call(
        body, name="ffn_in_act",
        out_shape=(jax.ShapeDtypeStruct((S, 2 * FH), BF16), jax.ShapeDtypeStruct((S, FH), BF16)),
        grid=(S // tm,),
        in_specs=[pl.BlockSpec((tm, D), lambda i: (i, 0)), _resident((D, 2 * FH))],
        out_specs=(pl.BlockSpec((tm, 2 * FH), lambda i: (i, 0)), pl.BlockSpec((tm, FH), lambda i: (i, 0))),
        compiler_params=_cp(("parallel",)))(h2, w_fi)


def _ffn_bwd(do2, w_fo, f, after):
    S, D = do2.shape
    tm = min(512, S)

    def body(d_ref, w_ref, f_ref, after_ref, df_ref):
        dv = d_ref[...]
        for c in range(FH // _FC):
            lo, hi = c * _FC, (c + 1) * _FC
            dact = lax.dot_general(dv, w_ref[lo:hi, :], (((1,), (1,)), ((), ())), preferred_element_type=F32)
            g = f_ref[:, lo:hi].astype(F32)
            u = f_ref[:, FH + lo:FH + hi].astype(F32)
            sg = _sig(g)
            df_ref[:, lo:hi] = (dact * u * (sg * (1.0 + g * (1.0 - sg)))).astype(BF16)
            df_ref[:, FH + lo:FH + hi] = (dact * g * sg).astype(BF16)

    return pl.pallas_call(
        body, name="ffn_bwd", out_shape=jax.ShapeDtypeStruct((S, 2 * FH), BF16), grid=(S // tm,),
        in_specs=[pl.BlockSpec((tm, D), lambda i: (i, 0)), _resident((FH, D)),
                  pl.BlockSpec((tm, 2 * FH), lambda i: (i, 0)), pl.BlockSpec(memory_space=pl.ANY)],
        out_specs=pl.BlockSpec((tm, 2 * FH), lambda i: (i, 0)),
        compiler_params=_cp(("parallel",)))(do2, w_fo, f, after)


def _ffn_out_final(x2, act, w_fo, g2, fg, tgt):
    S, D = x2.shape
    tm = min(512, S)

    def body(x2_ref, a_ref, w_ref, g2_ref, fg_ref, t_ref, dx3_ref, do2_ref, ls_ref, dfg_ref, dg2_ref):
        i = pl.program_id(0)

        @pl.when(i == 0)
        def _():
            ls_ref[...] = jnp.zeros_like(ls_ref)
            dfg_ref[...] = jnp.zeros_like(dfg_ref)
            dg2_ref[...] = jnp.zeros_like(dg2_ref)

        o2 = jnp.dot(a_ref[...], w_ref[...], preferred_element_type=F32)
        x3 = x2_ref[...] + g2_ref[...] * o2
        r = lax.rsqrt(jnp.mean(x3 * x3, axis=-1, keepdims=True) + EPS)
        xn = x3 * r
        err = xn * fg_ref[...] - t_ref[...]
        dy = err * (1.0 / D)
        dxn = dy * fg_ref[...]
        dx3 = r * (dxn - xn * jnp.mean(dxn * xn, axis=-1, keepdims=True))
        dx3_ref[...] = dx3
        do2_ref[...] = (dx3 * g2_ref[...]).astype(BF16)
        e2 = _colsum8(err * err)
        lanes = e2[:, 0:128]
        for q in range(1, D // 128):
            lanes = lanes + e2[:, q * 128:(q + 1) * 128]
        ls_ref[...] += lanes * (0.5 / D)
        dfg_ref[...] += _colsum8(dy * xn)
        dg2_ref[...] += _colsum8(dx3 * o2)

    row = pl.BlockSpec((tm, D), lambda i: (i, 0))
    par = _full((1, D))
    return pl.pallas_call(
        body, name="final_loss",
        out_shape=(jax.ShapeDtypeStruct((S, D), F32), jax.ShapeDtypeStruct((S, D), BF16),
                   jax.ShapeDtypeStruct((8, 128), F32), jax.ShapeDtypeStruct((8, D), F32),
                   jax.ShapeDtypeStruct((8, D), F32)),
        grid=(S // tm,), in_specs=[row, pl.BlockSpec((tm, FH), lambda i: (i, 0)), _resident((FH, D)), par, par, row],
        out_specs=(row, row, _full((8, 128)), _full((8, D)), _full((8, D))),
        compiler_params=_cp(("arbitrary",)))(x2, act, w_fo, g2, fg, tgt)


def _normmod_bwd(dsrc, w, xin, dres, g, sc, gate, o, after, name):
    S, D = xin.shape
    parts = list(dsrc) if isinstance(dsrc, (list, tuple)) else [dsrc]
    widths = [p.shape[1] for p in parts]
    K = sum(widths)
    tm = min(512, S)
    npart = len(parts)

    def body(*refs):
        ds_refs = refs[:npart]
        w_ref, x_ref, dr_ref, g_ref, sc_ref, gate_ref, o_ref, after_ref = refs[npart:npart + 8]
        dx_ref, do_ref, dsh_ref, dsc_ref, dg_ref, dgate_ref = refs[npart + 8:]
        i = pl.program_id(0)

        @pl.when(i == 0)
        def _():
            dsh_ref[...] = jnp.zeros_like(dsh_ref)
            dsc_ref[...] = jnp.zeros_like(dsc_ref)
            dg_ref[...] = jnp.zeros_like(dg_ref)
            dgate_ref[...] = jnp.zeros_like(dgate_ref)

        gv = g_ref[...]
        scale = 1.0 + sc_ref[...]
        xv = x_ref[...]
        r = lax.rsqrt(jnp.mean(xv * xv, axis=-1, keepdims=True) + EPS)
        xn = xv * r
        dh_v, col = None, 0
        for ds_ref, wd in zip(ds_refs, widths):
            t = lax.dot_general(ds_ref[...], w_ref[:, col:col + wd], (((1,), (1,)), ((), ())),
                                preferred_element_type=F32)
            dh_v = t if dh_v is None else dh_v + t
            col += wd
        dxn = dh_v * (gv * scale)
        dx = dr_ref[...] + r * (dxn - xn * jnp.mean(dxn * xn, axis=-1, keepdims=True))
        dx_ref[...] = dx
        do_ref[...] = (dx * gate_ref[...]).astype(BF16)
        hx = dh_v * xn
        dsh_ref[...] += _colsum8(dh_v)
        dsc_ref[...] += _colsum8(hx) * gv
        dg_ref[...] += _colsum8(hx) * scale
        dgate_ref[...] += _colsum8(dx * o_ref[...])

    row = pl.BlockSpec((tm, D), lambda i: (i, 0))
    par = _full((1, D))
    acc = jax.ShapeDtypeStruct((8, D), F32)
    return pl.pallas_call(
        body, name=name,
        out_shape=(jax.ShapeDtypeStruct((S, D), F32), jax.ShapeDtypeStruct((S, D), BF16), acc, acc, acc, acc),
        grid=(S // tm,),
        in_specs=[pl.BlockSpec((tm, wd), lambda i: (i, 0)) for wd in widths]
        + [_resident((D, K)), row, row, par, par, par, row, pl.BlockSpec(memory_space=pl.ANY)],
        out_specs=(row, row, _full((8, D)), _full((8, D)), _full((8, D)), _full((8, D))),
        compiler_params=_cp(("arbitrary",)))(*parts, w, xin, dres, g, sc, gate, o, after)


def _me():
    return lax.axis_index("x"), lax.axis_index("y"), lax.axis_index("c")


def _allgather8(v, name, after=()):
    several = isinstance(v, (list, tuple))
    vs = list(v) if several else [v]
    nv = len(vs)
    after = list(after)

    def body(*refs):
        v_refs = refs[:nv]
        out_refs = refs[nv + len(after):2 * nv + len(after)]
        send_sems, recv_sems, local_sems = refs[2 * nv + len(after):]
        x, y, c = _me()
        local = [pltpu.make_async_copy(v_refs[a], out_refs[a].at[4 * x + 2 * y + c], local_sems.at[a])
                 for a in range(nv)]
        for cp in local:
            cp.start()
        copies = []
        for a in range(nv):
            for k in range(1, N_DEV):
                fx, fy, fc = (k >> 2) & 1, (k >> 1) & 1, k & 1
                peer = (x ^ fx, y ^ fy, c ^ fc)
                copies.append(pltpu.make_async_remote_copy(
                    src_ref=v_refs[a], dst_ref=out_refs[a].at[4 * x + 2 * y + c],
                    send_sem=send_sems.at[a, k - 1], recv_sem=recv_sems.at[a, k - 1],
                    device_id=peer, device_id_type=MESH))
        for cp in copies:
            cp.start()
        for a in range(nv):
            for k in range(1, N_DEV):
                fx, fy, fc = (k >> 2) & 1, (k >> 1) & 1, k & 1
                src_slot = 4 * (x ^ fx) + 2 * (y ^ fy) + (c ^ fc)
                pltpu.make_async_remote_copy(
                    src_ref=v_refs[a], dst_ref=out_refs[a].at[src_slot],
                    send_sem=send_sems.at[a, k - 1], recv_sem=recv_sems.at[a, k - 1],
                    device_id=(x ^ fx, y ^ fy, c ^ fc), device_id_type=MESH).wait_recv()
        for cp in copies:
            cp.wait_send()
        for cp in local:
            cp.wait()

    vm = pl.BlockSpec(memory_space=pltpu.VMEM)
    outs = pl.pallas_call(
        body, name=name, out_shape=tuple(jax.ShapeDtypeStruct((N_DEV,) + a.shape, a.dtype) for a in vs),
        in_specs=[vm] * nv + [pl.BlockSpec(memory_space=pl.ANY)] * len(after),
        out_specs=tuple([vm] * nv),
        scratch_shapes=[pltpu.SemaphoreType.DMA((nv, N_DEV - 1)), pltpu.SemaphoreType.DMA((nv, N_DEV - 1)),
                        pltpu.SemaphoreType.DMA((nv,))],
        compiler_params=pltpu.CompilerParams(vmem_limit_bytes=VMEM_LIMIT))(*vs, *after)
    return list(outs) if several else outs[0]


def _mod_gather(c_all, w_ada, b_ada_cols, after):
    n = w_ada.shape[1]
    after = list(after)

    def body(c_ref, w_ref, b_ref, *rest):
        out_ref, v_ref, send_sems, recv_sems, local_sem = rest[len(after):]
        cv = c_ref[...]
        ca = (cv * _sig(cv)).astype(BF16)
        v_ref[...] = jnp.dot(ca, w_ref[...].astype(BF16), preferred_element_type=F32) + b_ref[...]
        x, y, c = _me()
        mine = pltpu.make_async_copy(v_ref, out_ref.at[4 * x + 2 * y + c], local_sem)
        mine.start()
        copies = []
        for k in range(1, N_DEV):
            fx, fy, fc = (k >> 2) & 1, (k >> 1) & 1, k & 1
            copies.append(pltpu.make_async_remote_copy(
                src_ref=v_ref, dst_ref=out_ref.at[4 * x + 2 * y + c],
                send_sem=send_sems.at[k - 1], recv_sem=recv_sems.at[k - 1],
                device_id=(x ^ fx, y ^ fy, c ^ fc), device_id_type=MESH))
        for cp in copies:
            cp.start()
        for k in range(1, N_DEV):
            fx, fy, fc = (k >> 2) & 1, (k >> 1) & 1, k & 1
            pltpu.make_async_remote_copy(
                src_ref=v_ref, dst_ref=out_ref.at[4 * (x ^ fx) + 2 * (y ^ fy) + (c ^ fc)],
                send_sem=send_sems.at[k - 1], recv_sem=recv_sems.at[k - 1],
                device_id=(x ^ fx, y ^ fy, c ^ fc), device_id_type=MESH).wait_recv()
        for cp in copies:
            cp.wait_send()
        mine.wait()

    vm = pl.BlockSpec(memory_space=pltpu.VMEM)
    return pl.pallas_call(
        body, name="mod_gather", out_shape=jax.ShapeDtypeStruct((N_DEV, N_DEV, n), F32),
        in_specs=[vm, vm, vm] + [pl.BlockSpec(memory_space=pl.ANY)] * len(after), out_specs=vm,
        scratch_shapes=[pltpu.VMEM((N_DEV, n), F32), pltpu.SemaphoreType.DMA((N_DEV - 1,)),
                        pltpu.SemaphoreType.DMA((N_DEV - 1,)), pltpu.SemaphoreType.DMA],
        compiler_params=pltpu.CompilerParams(vmem_limit_bytes=VMEM_LIMIT))(c_all, w_ada, b_ada_cols, *after)


_HBM = pl.BlockSpec(memory_space=pltpu.HBM)
_SEM = pl.BlockSpec(memory_space=pltpu.SEMAPHORE)
_EFFECT = pltpu.SideEffectType.DATAFLOW_SIDE_EFFECTING
_N_PEER = N_CHIP - 1


def _chip_part(ref, axis, n, chip):
    start = pl.multiple_of(chip * n, 8)
    return ref.at[pl.ds(start, n), :] if axis == 0 else ref.at[:, pl.ds(start, n)]


def _gather_copy(k, src_ref, land_ref, send_sems, recv_sems, axis, arriving):
    x, y, c = _me()
    px, py = x ^ ((k >> 1) & 1), y ^ (k & 1)
    chip = 2 * px + py if arriving else 2 * x + y
    return pltpu.make_async_remote_copy(
        src_ref=src_ref, dst_ref=_chip_part(land_ref, axis, src_ref.shape[axis], chip),
        send_sem=send_sems.at[k - 1], recv_sem=recv_sems.at[k - 1], device_id=(px, py, c), device_id_type=MESH)


def _scatter_copy(k, grad_ref, land_ref, send_sems, recv_sems, axis):
    x, y, c = _me()
    px, py = x ^ ((k >> 1) & 1), y ^ (k & 1)
    return pltpu.make_async_remote_copy(
        src_ref=_chip_part(grad_ref, axis, grad_ref.shape[axis] // N_CHIP, 2 * px + py), dst_ref=land_ref.at[k],
        send_sem=send_sems.at[k - 1], recv_sem=recv_sems.at[k - 1], device_id=(px, py, c), device_id_type=MESH)


def _scatter_own(grad_ref, land_ref, send_sems, axis):
    x, y, _ = _me()
    return pltpu.make_async_copy(_chip_part(grad_ref, axis, grad_ref.shape[axis] // N_CHIP, 2 * x + y),
                                 land_ref.at[0], send_sems.at[_N_PEER])


def _own_copy(src_ref, land_ref, sends, axis):
    x, y, _ = _me()
    return pltpu.make_async_copy(src_ref, _chip_part(land_ref, axis, src_ref.shape[axis], 2 * x + y),
                                 sends.at[_N_PEER])


def _gather_start(shards, axes, after, name):
    nw = len(shards)
    lands = []
    for s, ax in zip(shards, axes):
        shp = list(s.shape)
        shp[ax] *= N_CHIP
        lands.append(lax.empty(tuple(shp), s.dtype))

    def body(*refs):
        srcs, zones = refs[:nw], refs[nw:2 * nw]
        sends, recvs = refs[2 * nw + 1:3 * nw + 1], refs[3 * nw + 1:4 * nw + 1]
        token = refs[-1]
        for w in range(nw):
            for k in range(1, N_CHIP):
                _gather_copy(k, srcs[w], zones[w], sends[w], recvs[w], axes[w], False).start()
        for w in range(nw):
            _own_copy(srcs[w], zones[w], sends[w], axes[w]).start()
        token[...] = jnp.zeros_like(token)

    outs = pl.pallas_call(
        body, name=name,
        out_shape=tuple([pltpu.SemaphoreType.DMA((_N_PEER + 1,))] * nw + [pltpu.SemaphoreType.DMA((_N_PEER,))] * nw
                        + [pltpu.HBM(a.shape, a.dtype) for a in list(shards) + list(lands)]
                        + [jax.ShapeDtypeStruct((8, 128), F32)]),
        in_specs=[_HBM] * (2 * nw) + [pl.BlockSpec(memory_space=pl.ANY)],
        out_specs=tuple([_SEM] * (2 * nw) + [_HBM] * (2 * nw) + [pl.BlockSpec(memory_space=pltpu.VMEM)]),
        input_output_aliases={i: 2 * nw + i for i in range(2 * nw)},
        compiler_params=pltpu.CompilerParams(has_side_effects=_EFFECT),
    )(*([pltpu.with_memory_space_constraint(a, pltpu.HBM) for a in list(shards) + list(lands)] + [after]))
    per_weight = [(outs[w], outs[nw + w], outs[2 * nw + w], outs[3 * nw + w]) for w in range(nw)]
    return per_weight, outs[-1]


def _gather_wait(state, axis, after, name):
    send_sems, recv_sems, shard, land = state

    after = list(after) if isinstance(after, (list, tuple)) else [after]

    def body(src_ref, land_ref, sends, recvs, *rest):
        for k in range(1, N_CHIP):
            _gather_copy(k, src_ref, land_ref, sends, recvs, axis, False).wait_send()
            _gather_copy(k, src_ref, land_ref, sends, recvs, axis, True).wait_recv()
        _own_copy(src_ref, land_ref, sends, axis).wait()

    return pl.pallas_call(
        body, name=name, out_shape=(pltpu.HBM(shard.shape, shard.dtype), pltpu.HBM(land.shape, land.dtype)),
        in_specs=[_HBM, _HBM, _SEM, _SEM] + [pl.BlockSpec(memory_space=pl.ANY)] * len(after), out_specs=(_HBM, _HBM),
        input_output_aliases={0: 0, 1: 1},
        compiler_params=pltpu.CompilerParams(has_side_effects=_EFFECT),
    )(shard, land, send_sems, recv_sems, *after)[1]


def _half_rows(ref, c):
    k2 = ref.shape[0] // 2
    return pl.ds(pl.multiple_of(c * k2, 8), k2)


def _half_copy(k, shard_ref, land_ref, send_sems, recv_sems, arriving):
    x, y, c = _me()
    px, py = x ^ ((k >> 1) & 1), y ^ (k & 1)
    n = shard_ref.shape[1]
    chip = 2 * px + py if arriving else 2 * x + y
    return pltpu.make_async_remote_copy(
        src_ref=shard_ref.at[_half_rows(shard_ref, c), :],
        dst_ref=land_ref.at[_half_rows(land_ref, c), pl.ds(pl.multiple_of(chip * n, 128), n)],
        send_sem=send_sems.at[k - 1], recv_sem=recv_sems.at[k - 1], device_id=(px, py, c), device_id_type=MESH)


def _half_own(shard_ref, land_ref, send_sems):
    x, y, c = _me()
    n = shard_ref.shape[1]
    return pltpu.make_async_copy(
        shard_ref.at[_half_rows(shard_ref, c), :],
        land_ref.at[_half_rows(land_ref, c), pl.ds(pl.multiple_of((2 * x + y) * n, 128), n)], send_sems.at[_N_PEER])


def _half_gather_start(shard, after, name):
    K, n = shard.shape
    land = lax.empty((K, N_CHIP * n), shard.dtype)

    def body(shard_ref, land_ref, after_ref, sends, recvs, shard_thru, land_thru, token):
        for k in range(1, N_CHIP):
            _half_copy(k, shard_ref, land_ref, sends, recvs, False).start()
        _half_own(shard_ref, land_ref, sends).start()
        token[...] = jnp.zeros_like(token)

    outs = pl.pallas_call(
        body, name=name,
        out_shape=(pltpu.SemaphoreType.DMA((_N_PEER + 1,)), pltpu.SemaphoreType.DMA((_N_PEER,)),
                   pltpu.HBM(shard.shape, shard.dtype), pltpu.HBM(land.shape, land.dtype),
                   jax.ShapeDtypeStruct((8, 128), F32)),
        in_specs=[_HBM, _HBM, pl.BlockSpec(memory_space=pl.ANY)],
        out_specs=(_SEM, _SEM, _HBM, _HBM, pl.BlockSpec(memory_space=pltpu.VMEM)),
        input_output_aliases={0: 2, 1: 3},
        compiler_params=pltpu.CompilerParams(has_side_effects=_EFFECT),
    )(pltpu.with_memory_space_constraint(shard, pltpu.HBM), pltpu.with_memory_space_constraint(land, pltpu.HBM), after)
    return outs[:4], outs[4]


def _half_gather_wait(state, after, name):
    send_sems, recv_sems, shard, land = state
    after = list(after)

    def body(shard_ref, land_ref, sends, recvs, *rest):
        for k in range(1, N_CHIP):
            _half_copy(k, shard_ref, land_ref, sends, recvs, False).wait_send()
            _half_copy(k, shard_ref, land_ref, sends, recvs, True).wait_recv()
        _half_own(shard_ref, land_ref, sends).wait()

    return pl.pallas_call(
        body, name=name, out_shape=(pltpu.HBM(shard.shape, shard.dtype), pltpu.HBM(land.shape, land.dtype)),
        in_specs=[_HBM, _HBM, _SEM, _SEM] + [pl.BlockSpec(memory_space=pl.ANY)] * len(after), out_specs=(_HBM, _HBM),
        input_output_aliases={0: 0, 1: 1},
        compiler_params=pltpu.CompilerParams(has_side_effects=_EFFECT),
    )(shard, land, send_sems, recv_sems, *after)[1]


def _half_swap_copy(land_ref, send_sem, recv_sem, arriving):
    x, y, c = _me()
    rows = _half_rows(land_ref, 1 - c if arriving else c)
    return pltpu.make_async_remote_copy(src_ref=land_ref.at[rows, :], dst_ref=land_ref.at[rows, :], send_sem=send_sem,
                                        recv_sem=recv_sem, device_id=(x, y, 1 - c), device_id_type=MESH)


def _half_swap_start(land, name):
    def body(land_ref, send, recv, land_thru, token):
        _half_swap_copy(land_ref, send.at[0], recv.at[0], False).start()
        token[...] = jnp.zeros_like(token)

    sem = pltpu.SemaphoreType.DMA((1,))
    outs = pl.pallas_call(
        body, name=name,
        out_shape=(sem, sem, pltpu.HBM(land.shape, land.dtype), jax.ShapeDtypeStruct((8, 128), F32)),
        in_specs=[_HBM], out_specs=(_SEM, _SEM, _HBM, pl.BlockSpec(memory_space=pltpu.VMEM)),
        input_output_aliases={0: 2},
        compiler_params=pltpu.CompilerParams(has_side_effects=_EFFECT),
    )(pltpu.with_memory_space_constraint(land, pltpu.HBM))
    return outs[:3], outs[3]


def _half_swap_wait(state, after, name):
    send, recv, land = state

    def body(land_ref, send_ref, recv_ref, after_ref, got_ref):
        _half_swap_copy(land_ref, send_ref.at[0], recv_ref.at[0], False).wait_send()
        _half_swap_copy(land_ref, send_ref.at[0], recv_ref.at[0], True).wait_recv()

    return pl.pallas_call(
        body, name=name, out_shape=pltpu.HBM(land.shape, land.dtype),
        in_specs=[_HBM, _SEM, _SEM, pl.BlockSpec(memory_space=pl.ANY)], out_specs=_HBM,
        input_output_aliases={0: 0},
        compiler_params=pltpu.CompilerParams(has_side_effects=_EFFECT),
    )(land, send, recv, after)


def _all8_copy(k, v_ref, land_ref, send_sems, recv_sems, arriving):
    x, y, c = _me()
    px, py, pc = x ^ ((k >> 2) & 1), y ^ ((k >> 1) & 1), c ^ (k & 1)
    slot = 4 * px + 2 * py + pc if arriving else 4 * x + 2 * y + c
    return pltpu.make_async_remote_copy(
        src_ref=v_ref, dst_ref=land_ref.at[slot], send_sem=send_sems.at[k - 1], recv_sem=recv_sems.at[k - 1],
        device_id=(px, py, pc), device_id_type=MESH)


def _all8_own(v_ref, land_ref, send_sems):
    x, y, c = _me()
    return pltpu.make_async_copy(v_ref, land_ref.at[4 * x + 2 * y + c], send_sems.at[N_DEV - 1])


def _all8_start(v, name):
    land = lax.empty((N_DEV,) + v.shape, v.dtype)

    def body(v_ref, land_ref, sends, recvs, v_thru, land_thru, token):
        for k in range(1, N_DEV):
            _all8_copy(k, v_ref, land_ref, sends, recvs, False).start()
        _all8_own(v_ref, land_ref, sends).start()
        token[...] = jnp.zeros_like(token)

    outs = pl.pallas_call(
        body, name=name,
        out_shape=(pltpu.SemaphoreType.DMA((N_DEV,)), pltpu.SemaphoreType.DMA((N_DEV - 1,)),
                   pltpu.HBM(v.shape, v.dtype), pltpu.HBM(land.shape, land.dtype),
                   jax.ShapeDtypeStruct((8, 128), F32)),
        in_specs=[_HBM, _HBM], out_specs=(_SEM, _SEM, _HBM, _HBM, pl.BlockSpec(memory_space=pltpu.VMEM)),
        input_output_aliases={0: 2, 1: 3},
        compiler_params=pltpu.CompilerParams(has_side_effects=_EFFECT),
    )(pltpu.with_memory_space_constraint(v, pltpu.HBM), pltpu.with_memory_space_constraint(land, pltpu.HBM))
    return outs[:4], outs[4]


def _all8_wait(state, after, name):
    send_sems, recv_sems, v, land = state

    def body(v_ref, land_ref, sends, recvs, after_ref, v_dead, got_ref):
        for k in range(1, N_DEV):
            _all8_copy(k, v_ref, land_ref, sends, recvs, False).wait_send()
            _all8_copy(k, v_ref, land_ref, sends, recvs, True).wait_recv()
        _all8_own(v_ref, land_ref, sends).wait()

    return pl.pallas_call(
        body, name=name, out_shape=(pltpu.HBM(v.shape, v.dtype), pltpu.HBM(land.shape, land.dtype)),
        in_specs=[_HBM, _HBM, _SEM, _SEM, pl.BlockSpec(memory_space=pl.ANY)], out_specs=(_HBM, _HBM),
        input_output_aliases={0: 0, 1: 1},
        compiler_params=pltpu.CompilerParams(has_side_effects=_EFFECT),
    )(v, land, send_sems, recv_sems, after)[1]


def _swap_copy(w, src_ref, land_ref, send_sems, recv_sems):
    x, y, c = _me()
    return pltpu.make_async_remote_copy(src_ref=src_ref, dst_ref=land_ref, send_sem=send_sems.at[w],
                                        recv_sem=recv_sems.at[w], device_id=(x, y, 1 - c), device_id_type=MESH)


def _swap_start(arrs, after, name):
    nw = len(arrs)
    lands = [lax.empty(a.shape, a.dtype) for a in arrs]

    def body(*refs):
        srcs, zones = refs[:nw], refs[nw:2 * nw]
        sends, recvs = refs[2 * nw + 1], refs[2 * nw + 2]
        for w in range(nw):
            _swap_copy(w, srcs[w], zones[w], sends, recvs).start()
        refs[-1][...] = jnp.zeros_like(refs[-1])

    sem = pltpu.SemaphoreType.DMA((nw,))
    outs = pl.pallas_call(
        body, name=name,
        out_shape=tuple([sem, sem] + [pltpu.HBM(a.shape, a.dtype) for a in list(arrs) + lands]
                        + [jax.ShapeDtypeStruct((8, 128), F32)]),
        in_specs=[_HBM] * (2 * nw) + [pl.BlockSpec(memory_space=pl.ANY)],
        out_specs=tuple([_SEM, _SEM] + [_HBM] * (2 * nw) + [pl.BlockSpec(memory_space=pltpu.VMEM)]),
        input_output_aliases={i: 2 + i for i in range(2 * nw)},
        compiler_params=pltpu.CompilerParams(has_side_effects=_EFFECT),
    )(*([pltpu.with_memory_space_constraint(a, pltpu.HBM) for a in list(arrs) + lands] + [after]))
    return (outs[0], outs[1], outs[2:2 + nw], outs[2 + nw:2 + 2 * nw]), outs[-1]


def _swap_wait(state, after, name):
    send_sems, recv_sems, arrs, lands = state
    nw = len(arrs)

    def body(*refs):
        srcs, zones = refs[:nw], refs[nw:2 * nw]
        sends, recvs = refs[2 * nw], refs[2 * nw + 1]
        for w in range(nw):
            cp = _swap_copy(w, srcs[w], zones[w], sends, recvs)
            cp.wait_send()
            cp.wait_recv()

    outs = pl.pallas_call(
        body, name=name, out_shape=tuple(pltpu.HBM(a.shape, a.dtype) for a in list(arrs) + list(lands)),
        in_specs=[_HBM] * (2 * nw) + [_SEM, _SEM, pl.BlockSpec(memory_space=pl.ANY)],
        out_specs=tuple([_HBM] * (2 * nw)),
        input_output_aliases={i: i for i in range(2 * nw)},
        compiler_params=pltpu.CompilerParams(has_side_effects=_EFFECT),
    )(*arrs, *lands, send_sems, recv_sems, after)
    return list(outs[:nw]), list(outs[nw:])


def _scatter_start(grad, axis, name):
    shp = list(grad.shape)
    shp[axis] //= N_CHIP
    land = lax.empty((N_CHIP,) + tuple(shp), grad.dtype)

    def body(grad_ref, land_ref, sends, recvs, grad_thru, land_thru, token):
        for k in range(1, N_CHIP):
            _scatter_copy(k, grad_ref, land_ref, sends, recvs, axis).start()
        _scatter_own(grad_ref, land_ref, sends, axis).start()
        token[...] = jnp.zeros_like(token)

    outs = pl.pallas_call(
        body, name=name,
        out_shape=(pltpu.SemaphoreType.DMA((_N_PEER + 1,)), pltpu.SemaphoreType.DMA((_N_PEER,)),
                   pltpu.HBM(grad.shape, grad.dtype), pltpu.HBM(land.shape, land.dtype),
                   jax.ShapeDtypeStruct((8, 128), F32)),
        in_specs=[_HBM, _HBM], out_specs=(_SEM, _SEM, _HBM, _HBM, pl.BlockSpec(memory_space=pltpu.VMEM)),
        input_output_aliases={0: 2, 1: 3},
        compiler_params=pltpu.CompilerParams(has_side_effects=_EFFECT),
    )(pltpu.with_memory_space_constraint(grad, pltpu.HBM), pltpu.with_memory_space_constraint(land, pltpu.HBM))
    return outs[:4], outs[4]


def _scatter_wait(state, axis, after, name):
    send_sems, recv_sems, grad, land = state

    def body(grad_ref, land_ref, sends, recvs, after_ref, grad_dead, got_ref):
        for k in range(1, N_CHIP):
            cp = _scatter_copy(k, grad_ref, land_ref, sends, recvs, axis)
            cp.wait_send()
            cp.wait_recv()
        _scatter_own(grad_ref, land_ref, sends, axis).wait()

    return pl.pallas_call(
        body, name=name, out_shape=(pltpu.HBM(grad.shape, grad.dtype), pltpu.HBM(land.shape, land.dtype)),
        in_specs=[_HBM, _HBM, _SEM, _SEM, pl.BlockSpec(memory_space=pl.ANY)], out_specs=(_HBM, _HBM),
        input_output_aliases={0: 0, 1: 1},
        compiler_params=pltpu.CompilerParams(has_side_effects=_EFFECT),
    )(grad, land, send_sems, recv_sems, after)[1]


_C1 = 1.0 - B1 ** STEP
_C2 = 1.0 - B2 ** STEP


def _adam_math(w, g, m, v):
    m = B1 * m + (1.0 - B1) * g
    v = B2 * v + (1.0 - B2) * (g * g)
    delta = -LR * ((m / _C1) / (jnp.sqrt(v / _C2) + AEPS) + WD * w)
    return delta, m, v


def _adamw(w, m, v, groups, name):
    R, C = w.shape
    tr = R if R <= 256 else (128 if R % 128 == 0 else 176)
    assert R % tr == 0, (name, R)
    gparts = [p for grp in groups for p in grp]
    sizes = [len(grp) for grp in groups]
    ng = len(gparts)

    def body(*refs):
        w_ref, m_ref, v_ref = refs[:3]
        g_refs = list(refs[3:3 + ng])
        g_out, d_out, m_out, v_out = refs[3 + ng:]
        g = None
        for size in sizes:
            s = None
            for r in [g_refs.pop(0) for _ in range(size)]:
                terms = [r[q] for q in range(r.shape[0])] if len(r.shape) == 3 else [r[...]]
                for t in terms:
                    s = t.astype(F32) if s is None else s + t.astype(F32)
            g = s if g is None else g + s
        delta, mn, vn = _adam_math(w_ref[...], g, m_ref[...], v_ref[...])
        g_out[...] = g
        d_out[...] = delta
        m_out[...] = mn
        v_out[...] = vn

    blk = pl.BlockSpec((tr, C), lambda i: (i, 0))
    g_specs = [blk if p.ndim == 2 else pl.BlockSpec((p.shape[0], tr, C), lambda i: (0, i, 0)) for p in gparts]
    sds = jax.ShapeDtypeStruct((R, C), F32)
    return pl.pallas_call(
        body, name=name, out_shape=(sds, sds, sds, sds), grid=(R // tr,),
        in_specs=[blk, blk, blk] + g_specs, out_specs=(blk, blk, blk, blk),
        compiler_params=_cp(("parallel",)))(w, m, v, *gparts)


def _adamw_small(stack, names, wts, mom, var, sum_only, name):
    items, row = [], 0
    for n in names:
        shape = (KW, CW) if n == "conv_w" else wts[n].shape
        size = int(np.prod(shape))
        vec = len(shape) == 2 and shape[0] == 1 and n not in sum_only
        view = shape if vec else (-(-size // _PACK_COLS), _PACK_COLS)
        items.append((n, row, size, vec, view))
        row += _pack_rows(shape)
    upd = [it for it in items if it[0] not in sum_only]
    operands = [stack]
    for n, _, _, _, view in upd:
        operands += [d[n].reshape(view) for d in (wts, mom, var)]

    def grad(stack_ref, r0, nrows, ncols):
        g = stack_ref[0, r0:r0 + nrows, 0:ncols]
        for q in range(1, N_DEV):
            g = g + stack_ref[q, r0:r0 + nrows, 0:ncols]
        return g

    def body(*refs):
        stack_ref, ins, outs = refs[0], refs[1:1 + 3 * len(upd)], refs[1 + 3 * len(upd):]
        o = 0
        for idx, (n, r0, size, vec, view) in enumerate(upd):
            w_ref, m_ref, v_ref = ins[3 * idx:3 * idx + 3]
            g_out, d_out, m_out, v_out = outs[o:o + 4]
            o += 4
            if vec:
                pieces = [(j, j * _PACK_COLS, min((j + 1) * _PACK_COLS, size)) for j in range(-(-size // _PACK_COLS))]
            else:
                pieces = [(None, 0, _PACK_COLS)]
            for j, lo, hi in pieces:
                if vec:
                    g = grad(stack_ref, r0 + j, 1, hi - lo)
                    sl = (slice(None), slice(lo, hi))
                else:
                    g = grad(stack_ref, r0, view[0], _PACK_COLS)
                    sl = (slice(None), slice(None))
                delta, mn, vn = _adam_math(w_ref[sl], g, m_ref[sl], v_ref[sl])
                g_out[sl] = g
                d_out[sl] = delta
                m_out[sl] = mn
                v_out[sl] = vn
        for n, r0, size, vec, view in items:
            if n in sum_only:
                outs[o][...] = grad(stack_ref, r0, view[0], _PACK_COLS)
                o += 1

    out_shape = []
    for n, _, _, _, view in upd:
        out_shape += [jax.ShapeDtypeStruct(view, F32)] * 4
    out_shape += [jax.ShapeDtypeStruct(view, F32) for n, _, _, _, view in items if n in sum_only]
    vm = pl.BlockSpec(memory_space=pltpu.VMEM)
    res = pl.pallas_call(
        body, name=name, out_shape=tuple(out_shape), in_specs=[vm] * len(operands),
        out_specs=tuple([vm] * len(out_shape)),
        compiler_params=pltpu.CompilerParams(vmem_limit_bytes=VMEM_LIMIT))(*operands)
    updated = {n: tuple(r.reshape(wts[n].shape) for r in res[4 * i:4 * i + 4]) for i, (n, *_) in enumerate(upd)}
    sums = dict(zip([it[0] for it in items if it[0] in sum_only], res[4 * len(upd):]))
    return updated, sums


def _adamw_native(tensors, name):
    nt = len(tensors)

    def body(*refs):
        ins, outs = refs[:4 * nt], refs[4 * nt:]
        for t in range(nt):
            w_ref, m_ref, v_ref, g_ref = ins[4 * t:4 * t + 4]
            g = g_ref[...]
            delta, mn, vn = _adam_math(w_ref[...], g, m_ref[...], v_ref[...])
            outs[4 * t][...] = g
            outs[4 * t + 1][...] = delta
            outs[4 * t + 2][...] = mn
            outs[4 * t + 3][...] = vn

    vm = pl.BlockSpec(memory_space=pltpu.VMEM)
    flat = [a for tup in tensors for a in tup]
    res = pl.pallas_call(
        body, name=name, out_shape=tuple(jax.ShapeDtypeStruct(tup[0].shape, F32) for tup in tensors for _ in range(4)),
        in_specs=[vm] * len(flat), out_specs=tuple([vm] * (4 * nt)),
        compiler_params=pltpu.CompilerParams(vmem_limit_bytes=VMEM_LIMIT))(*flat)
    return [tuple(res[4 * t:4 * t + 4]) for t in range(nt)]


def _ada_grad(c_all, dmod_cols, after):
    n = dmod_cols.shape[1]
    tn = 512

    def body(c_ref, d_ref, after_ref, o_ref):
        cv = c_ref[...]
        ca = cv * _sig(cv)
        o_ref[...] = lax.dot_general(ca, d_ref[...], (((0,), (0,)), ((), ())),
                                     preferred_element_type=F32, precision=lax.Precision.HIGHEST)

    return pl.pallas_call(
        body, name="ada_grad", out_shape=jax.ShapeDtypeStruct((D_MODEL, n), F32), grid=(n // tn,),
        in_specs=[_full((N_DEV, D_MODEL)), pl.BlockSpec((N_DEV, tn), lambda j: (0, j)),
                  pl.BlockSpec(memory_space=pl.ANY)],
        out_specs=pl.BlockSpec((D_MODEL, tn), lambda j: (0, j)),
        compiler_params=_cp(("parallel",)))(c_all, dmod_cols, after)


def _ssm_tables(W):
    e_re, e_im, bb_re, bb_im = _ssm_prep(W["ssm_a_re"], W["ssm_a_im"], W["ssm_b_re"], W["ssm_b_im"], W["ssm_log_dt"])
    bb, cm = _block_diag_mats(bb_re, bb_im, W["ssm_c_re"], W["ssm_c_im"])
    bb16, cm16 = bb.astype(BF16), cm.astype(BF16)
    return (bb16, cm16, jnp.swapaxes(bb16, 1, 2), jnp.swapaxes(cm16, 1, 2),
            _scan_tables(e_re, e_im, False), _scan_tables(e_re, e_im, True))


def _device_step(x, mod, W, tables, tgt, getw, put, early):
    sh1, sc1, g1, sh2, sc2, g2 = [mod[:, i * D_MODEL:(i + 1) * D_MODEL] for i in range(6)]
    bb16, cm16, bbt16, cmt16, tab_f, tab_b = tables

    w_in = getw("w_in", [mod, *tables])
    h1, z = _in_proj(x, W["norm1_g"], sc1, sh1, w_in)
    yc, scv = _conv_fwd(z, W["conv_w"], W["conv_b"], W["conv_ln_g"], W["conv_ln_b"])
    xs, ys, yg = _ssm_fwd(z, bb16, cm16, W["ssm_d"], tab_f)
    w_cp, w_glu, w_out = getw("conv_proj", scv), getw("ssm_glu", yg), getw("w_out", yg)
    y_conv, zz, merged, o, x2, h2 = _mix_fwd(scv, yg, z, x, w_cp, w_glu, w_out, g1, W["norm2_g"], sc2, sh2)
    w_fi = getw("w_ffn_in", h2)
    f, act = _ffn_in_act(h2, w_fi)
    w_fo = getw("w_ffn_out", act)
    dx3, do2, loss8, dfg8, dg2_8 = _ffn_out_final(x2, act, w_fo, g2, W["final_g"], tgt)

    sm = {}
    tok = put("w_ffn_out", _matmul(act, do2, "tn", 1408, 1024, 2048, BF16, "mm_g_ffn_out"))
    df = _ffn_bwd(do2, w_fo, f, tok)
    tok = put("w_ffn_in", _matmul(h2, df, "tn", 1024, 1408, 2048, BF16, "mm_g_ffn_in"))
    dx2, do, dsh2, dsc2, dn2, dg1_8 = _normmod_bwd(df, w_fi, x2, dx3, W["norm2_g"], sc2, g1, o, tok, "d_h2_normmod2_bwd")
    tok = put("w_out", _matmul(merged, do, "tn", 1024, 1024, 4096, BF16, "mm_g_w_out"))
    dyconv, dgl, dzz = _mix_bwd(do, w_out, z, zz, y_conv, tok)
    tok = put("ssm_glu", _matmul(yg, dzz, "tn", 512, 1024, 4096, BF16, "mm_g_ssm_glu"))
    tok = put("conv_proj", _matmul(scv, dyconv, "tn", 512, 1024, 4096, BF16, "mm_g_conv_proj", after=tok))
    du, de16, dd8, dc_full, dbb_full = _ssm_bwd(dzz, w_glu, ys, z, xs, cmt16, bbt16, W["ssm_d"], tab_b, tok)
    dyc, dlg8, dlb8, dcb8 = _conv_bwd_ln(dyconv, w_cp, yc, W["conv_ln_g"], W["conv_ln_b"])
    dz_conv, dcw = _conv_bwd(dyc, z, W["conv_w"])

    s8 = lambda a: jnp.sum(a, axis=0, keepdims=True)
    de = de16.reshape(2, 8, NST).sum(1)
    de_re, de_im = de[0].reshape(G, P), de[1].reshape(G, P)
    dc_re, dc_im = _diag_blocks(dc_full)
    dc_im = -dc_im
    dbb_re, dbb_im = [jnp.swapaxes(t, 1, 2) for t in _diag_blocks(dbb_full)]
    _, vjp = jax.vjp(_ssm_prep, W["ssm_a_re"], W["ssm_a_im"], W["ssm_b_re"], W["ssm_b_im"], W["ssm_log_dt"])
    sm["ssm_a_re"], sm["ssm_a_im"], sm["ssm_b_re"], sm["ssm_b_im"], sm["ssm_log_dt"] = vjp((de_re, de_im, dbb_re, dbb_im))
    sm["ssm_c_re"], sm["ssm_c_im"] = dc_re, dc_im
    sm["ssm_d"] = s8(dd8)
    sm["norm2_g"] = s8(dn2)
    sm["conv_b"], sm["conv_ln_g"], sm["conv_ln_b"] = s8(dcb8), s8(dlg8), s8(dlb8)
    sm["conv_w"] = dcw.reshape(KW, 8, CW).sum(1)
    sm["final_g"] = s8(dfg8)
    tok = early(sm)

    dz = [dz_conv, du, dgl]
    tok = put("w_in", _matmul(h1, dz, "tn", 1024, 512, 4096, BF16, "mm_g_w_in", after=tok))
    dx, _, dsh1, dsc1, dn1, _ = _normmod_bwd(dz, w_in, x, dx2, W["norm1_g"], sc1, g1, o, tok, "d_h1_normmod1_bwd")
    dmod = jnp.concatenate([s8(dsh1), s8(dsc1), s8(dg1_8), s8(dsh2), s8(dsc2), s8(dg2_8)], axis=1)
    return loss8, dx, s8(dn1), dmod


_BIG = ("w_in", "conv_proj", "ssm_glu", "w_out", "w_ffn_in", "w_ffn_out")
_BIG_AXIS = {"w_in": 1, "conv_proj": 1, "ssm_glu": 1, "w_out": 0, "w_ffn_in": 1, "w_ffn_out": 0}
_EARLY = ("conv_w", "conv_b", "conv_ln_g", "conv_ln_b", "ssm_a_re", "ssm_a_im", "ssm_b_re", "ssm_b_im", "ssm_c_re",
          "ssm_c_im", "ssm_d", "ssm_log_dt", "norm2_g", "final_g")
_LATE = ("norm1_g", "b_ada")
_S5_MATS = ("ssm_a_re", "ssm_a_im", "ssm_b_re", "ssm_b_im", "ssm_c_re", "ssm_c_im")
_ORDER = ("w_ada", "b_ada", "norm1_g", "w_in", "conv_w", "conv_b", "conv_ln_g", "conv_ln_b", "conv_proj",
          "ssm_a_re", "ssm_a_im", "ssm_b_re", "ssm_b_im", "ssm_c_re", "ssm_c_im", "ssm_d", "ssm_log_dt", "ssm_glu",
          "w_out", "norm2_g", "w_ffn_in", "w_ffn_out", "final_g")
_PACK_COLS = 1024


def _pack_rows(shape):
    return -(-int(np.prod(shape)) // (8 * _PACK_COLS)) * 8


def _pack(arrs):
    parts = []
    for a in arrs:
        flat = a.reshape(-1)
        n = _pack_rows(a.shape)
        parts.append(jnp.pad(flat, (0, n * _PACK_COLS - flat.shape[0])).reshape(n, _PACK_COLS))
    return jnp.concatenate(parts, 0)


def kernel(x, c, w_ada, b_ada, norm1_g, w_in, conv_w, conv_b, conv_ln_g, conv_ln_b, conv_proj, ssm_a_re, ssm_a_im, ssm_b_re, ssm_b_im, ssm_c_re, ssm_c_im, ssm_d, ssm_log_dt, ssm_glu, w_out, norm2_g, w_ffn_in, w_ffn_out, final_g, loss_target, m_w_ada, m_b_ada, m_norm1_g, m_w_in, m_conv_w, m_conv_b, m_conv_ln_g, m_conv_ln_b, m_conv_proj, m_ssm_a_re, m_ssm_a_im, m_ssm_b_re, m_ssm_b_im, m_ssm_c_re, m_ssm_c_im, m_ssm_d, m_ssm_log_dt, m_ssm_glu, m_w_out, m_norm2_g, m_w_ffn_in, m_w_ffn_out, m_final_g, v_w_ada, v_b_ada, v_norm1_g, v_w_in, v_conv_w, v_conv_b, v_conv_ln_g, v_conv_ln_b, v_conv_proj, v_ssm_a_re, v_ssm_a_im, v_ssm_b_re, v_ssm_b_im, v_ssm_c_re, v_ssm_c_im, v_ssm_d, v_ssm_log_dt, v_ssm_glu, v_w_out, v_norm2_g, v_w_ffn_in, v_w_ffn_out, v_final_g):
    given = dict(locals())
    mx, my, mc = _me()
    chip = 2 * mx + my
    dev = 4 * mx + 2 * my + mc
    def canon(a):
        return a.reshape(1, -1) if a.ndim <= 2 else a[0]

    wts = {n: canon(given[n]) for n in _ORDER}
    mom = {n: canon(given["m_" + n]) for n in _ORDER}
    var = {n: canon(given["v_" + n]) for n in _ORDER}

    W = {n: wts[n] for n in _ORDER if n not in _BIG}
    rest = [n for n in _BIG if n != "w_in"]
    rest_shards = [wts[n].astype(BF16) for n in rest]
    state_in, token = _half_gather_start(wts["w_in"].astype(BF16), c, "gather_start_w_in")
    W["ssm_log_dt"] = wts["ssm_log_dt"] + token[0:1, 0:1]
    W["ssm_c_re"] = wts["ssm_c_re"] + token[0, 0]
    tables = _ssm_tables(W)

    c_all, conv_w_full = _allgather8([jnp.broadcast_to(c, (8, D_MODEL)), jnp.pad(wts["conv_w"], ((0, 1), (0, 0)))],
                                     "gather_c_conv_w", after=[*tables, *rest_shards])
    c_all = c_all[:, 0, :]
    n_ada = wts["w_ada"].shape[1]
    b_cols = lax.dynamic_slice(wts["b_ada"], (0, chip * n_ada), (1, n_ada))
    halves = _half_gather_wait(state_in, [c_all], "gather_wait_w_in")
    state_in, token = _half_swap_start(halves, "gather_swap_start_w_in")
    mods = _mod_gather(c_all, wts["w_ada"], b_cols, [token])
    mod = jnp.concatenate([lax.dynamic_index_in_dim(mods[2 * q], dev, 0, keepdims=True) for q in range(N_CHIP)], axis=1)
    W["conv_w"] = jnp.concatenate([conv_w_full[2 * q, :KW] for q in range(N_CHIP)], axis=1)
    w_in_full = _half_swap_wait(state_in, mod + W["conv_w"][0:1, 0:1], "gather_swap_wait_w_in")
    gstate, token = _gather_start(rest_shards, [_BIG_AXIS[n] for n in rest], w_in_full, "gather_start_rest")
    gstate = dict(zip(rest, gstate))
    mod = mod + token[0:1, 0:1]

    def getw(n, after):
        if n == "w_in":
            return w_in_full
        return _gather_wait(gstate[n], _BIG_AXIS[n], after, "gather_wait_" + n)

    sstate, estate = {}, []

    def put(n, g):
        sstate[n], tok = _scatter_start(g, _BIG_AXIS[n], "scatter_start_" + n)
        return tok

    first5 = [n for n in _BIG if n != "w_in"]

    def early(sm):
        state, tok = _all8_start(_pack([sm[n] for n in _EARLY]), "small_start")
        estate.append(state)
        held = [_scatter_wait(sstate[n], _BIG_AXIS[n], tok, "scatter_wait_" + n) for n in first5]
        state, tok = _swap_start(held, tok, "swap_start")
        estate.append(state)
        return tok

    loss8, dx, dn1, dmod = _device_step(x[0], mod, W, tables, loss_target[0], getw, put, early)

    late_state, tok = _all8_start(_pack([dn1, dmod, loss8]), "late_start")
    held5, sib5 = _swap_wait(estate[1], tok, "swap_wait")
    outs = {}
    for i, n in enumerate(first5):
        outs[n] = _adamw(wts[n], mom[n], var[n], [[held5[i]], [sib5[i]]], "adamw_" + n)
    allp = _all8_wait(estate[0], dx, "small_wait")
    upd, sums = _adamw_small(allp, _EARLY, wts, mom, var, ("conv_w",) + _S5_MATS, "adamw_small")
    outs.update(upd)

    def swapped(n, a):
        return jnp.swapaxes(a, 1, 2) if n in ("ssm_b_re", "ssm_b_im") else a

    def summed(n):
        return swapped(n, sums[n].reshape(-1)[:wts[n].size].reshape(wts[n].shape))

    res = _adamw_native([(swapped(n, wts[n]), swapped(n, mom[n]), swapped(n, var[n]), summed(n)) for n in _S5_MATS],
                        "adamw_s5")
    for n, r in zip(_S5_MATS, res):
        outs[n] = tuple(swapped(n, a) for a in r)

    late = _all8_wait(late_state, outs[first5[-1]][1], "late_wait")
    n_late = _pack_rows((D_MODEL,)) + _pack_rows((6 * D_MODEL,))
    loss = jnp.sum(late[:, n_late:, :])
    late = late[:, :n_late, :]
    held_in = _scatter_wait(sstate["w_in"], _BIG_AXIS["w_in"], late, "scatter_wait_w_in")
    state_in, tok = _swap_start([held_in], late, "swap_start_w_in")

    r1 = _pack_rows((D_MODEL,))
    dmod_all = late[:, r1:, :].reshape(N_DEV, -1)[:, :6 * D_MODEL]
    dmod_cols = lax.dynamic_slice(dmod_all, (0, chip * n_ada), (N_DEV, n_ada))
    g_ada = _ada_grad(c_all, dmod_cols, tok)
    outs["w_ada"] = _adamw(wts["w_ada"], mom["w_ada"], var["w_ada"], [[g_ada]], "adamw_w_ada")
    upd, _ = _adamw_small(late, _LATE, wts, mom, var, (), "adamw_late")
    outs.update(upd)
    held_in, sib_in = _swap_wait(state_in, outs["w_ada"][1], "swap_wait_w_in")
    outs["w_in"] = _adamw(wts["w_in"], mom["w_in"], var["w_in"], [held_in, sib_in], "adamw_w_in")
    g_cw_full = sums["conv_w"].reshape(-1)[:KW * CW].reshape(KW, CW)
    g_cw = lax.dynamic_slice(g_cw_full, (0, chip * (CW // N_CHIP)), (KW, CW // N_CHIP))
    pad = lambda a: jnp.pad(a, ((0, 1), (0, 0)))
    r_cw = _adamw(pad(wts["conv_w"]), pad(mom["conv_w"]), pad(var["conv_w"]), [[pad(g_cw)]], "adamw_conv_w")
    outs["conv_w"] = tuple(r[:KW] for r in r_cw)

    def shaped(n, a):
        return a.reshape(given[n].shape)

    result = [loss, dx[None]]
    for q in range(4):
        result += [shaped(n, outs[n][q]) for n in _ORDER]
    return tuple(result)
```
